```python
import math
import jax, jax.numpy as jnp
from jax import lax
import numpy as np

D_MODEL = 1024
BATCH = 8
SEQ = 4096
DEPTH = 1

N_MEM = 256
SWA_HEADS = 8
SWA_KV_HEADS = 2
SWA_HEAD_DIM = 64
WINDOW = 128
FOX_HEADS = 8
FOX_HEAD_DIM = 64
MEM_HEADS = 4
MEM_HEAD_DIM = 128
N_BRANCHES = 3
D_FF = 4 * D_MODEL
REL_BUCKETS = 32
REL_MAX_DIST = 128
BLOCK = 128
EPS = 1e-6
NEG = -1e30

SWA_Q = SWA_HEADS * SWA_HEAD_DIM
SWA_KV = SWA_KV_HEADS * SWA_HEAD_DIM
FOX_W = FOX_HEADS * FOX_HEAD_DIM
MEM_W = MEM_HEADS * MEM_HEAD_DIM
GATE_W = N_BRANCHES * D_MODEL
SPLIT_POINTS = (
    SWA_Q,
    SWA_Q + SWA_KV,
    SWA_Q + 2 * SWA_KV,
    SWA_Q + 2 * SWA_KV + FOX_W,
    SWA_Q + 2 * SWA_KV + 2 * FOX_W,
    SWA_Q + 2 * SWA_KV + 3 * FOX_W,
    SWA_Q + 2 * SWA_KV + 3 * FOX_W + FOX_HEADS,
    SWA_Q + 2 * SWA_KV + 3 * FOX_W + FOX_HEADS + MEM_W,
)
IN_WIDTH = SWA_Q + 2 * SWA_KV + 3 * FOX_W + FOX_HEADS + MEM_W + GATE_W

kernel_name = "hybrid_swa_fox_memory_gated_block"


def rmsnorm(x, g):
    xf = x.astype(jnp.float32)
    y = xf * lax.rsqrt(jnp.mean(xf * xf, axis=-1, keepdims=True) + EPS)
    return (y * g.astype(jnp.float32)).astype(x.dtype)


def t5_causal_bucket(dist):
    max_exact = REL_BUCKETS // 2
    d = jnp.maximum(dist, 0)
    df = jnp.maximum(d, 1).astype(jnp.float32)
    large = max_exact + (jnp.log(df / max_exact) / math.log(REL_MAX_DIST / max_exact)
                         * (REL_BUCKETS - max_exact)).astype(jnp.int32)
    large = jnp.minimum(large, REL_BUCKETS - 1)
    return jnp.where(d < max_exact, d, large)


def swa_attention(q, k, v, sinks, rel_bias):
    B, S = q.shape[0], q.shape[1]
    nb = S // BLOCK
    G = SWA_HEADS // SWA_KV_HEADS
    qb = q.reshape(B, nb, BLOCK, SWA_KV_HEADS, G, SWA_HEAD_DIM)
    pad = ((0, 0), (BLOCK, 0), (0, 0), (0, 0))
    kp = jnp.pad(k, pad).reshape(B, nb + 1, BLOCK, SWA_KV_HEADS, SWA_HEAD_DIM)
    vp = jnp.pad(v, pad).reshape(B, nb + 1, BLOCK, SWA_KV_HEADS, SWA_HEAD_DIM)
    kb = jnp.concatenate([kp[:, :-1], kp[:, 1:]], axis=2)
    vb = jnp.concatenate([vp[:, :-1], vp[:, 1:]], axis=2)
    scale = SWA_HEAD_DIM ** -0.5
    s = jnp.einsum('bnqhgd,bnkhd->bnhgqk', qb, kb).astype(jnp.float32) * scale
    t_loc = jnp.arange(BLOCK)[:, None] + BLOCK
    s_loc = jnp.arange(2 * BLOCK)[None, :]
    dist = t_loc - s_loc
    bias = rel_bias[t5_causal_bucket(dist)]
    bias = bias.reshape(BLOCK, 2 * BLOCK, SWA_KV_HEADS, G).transpose(2, 3, 0, 1)
    s = s + bias.astype(jnp.float32)
    band = (dist >= 0) & (dist < WINDOW)
    key_pos = jnp.arange(nb)[:, None] * BLOCK + s_loc - BLOCK
    valid = band[None] & (key_pos >= 0)[:, None, :]
    s = jnp.where(valid[None, :, None, None], s, NEG)
    sink = sinks.astype(jnp.float32).reshape(1, 1, SWA_KV_HEADS, G, 1, 1)
    m = jnp.maximum(jnp.max(s, axis=-1, keepdims=True), sink)
    p = jnp.exp(s - m)
    denom = jnp.sum(p, axis=-1, keepdims=True) + jnp.exp(sink - m)
    p = (p / denom).astype(v.dtype)
    o = jnp.einsum('bnhgqk,bnkhd->bnqhgd', p, vb)
    return o.reshape(B, S, SWA_Q)


def forgetting_attention(q, k, v, log_f):
    B, S = q.shape[0], q.shape[1]
    nb = S // BLOCK
    scale = FOX_HEAD_DIM ** -0.5
    c = jnp.cumsum(log_f, axis=1)
    c_k = c.transpose(0, 2, 1)
    qb = q.reshape(B, nb, BLOCK, FOX_HEADS, FOX_HEAD_DIM).transpose(1, 0, 2, 3, 4)
    cqb = c.reshape(B, nb, BLOCK, FOX_HEADS).transpose(1, 0, 3, 2)
    key_pos = jnp.arange(S)

    def one_block(args):
        qi, cqi, i = args
        s = jnp.einsum('bqhd,bkhd->bhqk', qi, k).astype(jnp.float32) * scale
        s = s + cqi[..., None] - c_k[:, :, None, :]
        q_pos = i * BLOCK + jnp.arange(BLOCK)
        causal = key_pos[None, :] <= q_pos[:, None]
        s = jnp.where(causal, s, NEG)
        p = jax.nn.softmax(s, axis=-1).astype(v.dtype)
        return jnp.einsum('bhqk,bkhd->bqhd', p, v)

    o = lax.map(one_block, (qb, cqb, jnp.arange(nb)))
    return o.transpose(1, 0, 2, 3, 4).reshape(B, S, FOX_W)


def memory_attention(q, mk, mv):
    B, S = q.shape[0], q.shape[1]
    scale = MEM_HEAD_DIM ** -0.5
    s = jnp.einsum('bshd,bmhd->bhsm', q, mk).astype(jnp.float32) * scale
    p = jax.nn.softmax(s, axis=-1).astype(mv.dtype)
    o = jnp.einsum('bhsm,bmhd->bshd', p, mv)
    return o.reshape(B, S, MEM_W)


def _fwd_setup_inputs(seed: int = 0) -> dict:
    key = jax.random.key(seed)
    ks = jax.random.split(key, 24)
    f32 = jnp.float32

    def w(k, shape, fan_in):
        return jax.random.normal(k, shape, f32) * fan_in ** -0.5

    def gain(k, shape):
        return 1.0 + 0.05 * jax.random.normal(k, shape, f32)

    return {
        "x": jax.random.normal(ks[0], (BATCH, SEQ, D_MODEL), f32),
        "mem": jax.random.normal(ks[1], (BATCH, N_MEM, D_MODEL), f32),
        "g_mix": gain(ks[2], (DEPTH, D_MODEL)),
        "w_in": w(ks[3], (DEPTH, D_MODEL, IN_WIDTH), D_MODEL),
        "b_gate": 0.02 * jax.random.normal(ks[4], (DEPTH, GATE_W), f32),
        "b_forget": 3.0 + 0.1 * jax.random.normal(ks[5], (DEPTH, FOX_HEADS), f32),
        "qn_swa": gain(ks[6], (DEPTH, SWA_HEAD_DIM)),
        "kn_swa": gain(ks[7], (DEPTH, SWA_HEAD_DIM)),
        "sink_swa": 0.5 * jax.random.normal(ks[8], (DEPTH, SWA_HEADS), f32),
        "rel_bias": 0.5 * jax.random.normal(ks[9], (REL_BUCKETS, SWA_HEADS), f32),
        "qn_fox": gain(ks[10], (DEPTH, FOX_HEAD_DIM)),
        "kn_fox": gain(ks[11], (DEPTH, FOX_HEAD_DIM)),
        "g_mem": gain(ks[12], (DEPTH, D_MODEL)),
        "w_mem_kv": w(ks[13], (DEPTH, D_MODEL, 2 * MEM_W), D_MODEL),
        "qn_mem": gain(ks[14], (DEPTH, MEM_HEAD_DIM)),
        "kn_mem": gain(ks[15], (DEPTH, MEM_HEAD_DIM)),
        "w_o_swa": w(ks[16], (DEPTH, SWA_Q, D_MODEL), SWA_Q),
        "w_o_fox": w(ks[17], (DEPTH, FOX_W, D_MODEL), FOX_W),
        "w_o_mem": w(ks[18], (DEPTH, MEM_W, D_MODEL), MEM_W),
        "w_out": w(ks[19], (DEPTH, D_MODEL, D_MODEL), D_MODEL),
        "g_mlp": gain(ks[20], (DEPTH, D_MODEL)),
        "w_mlp_up": w(ks[21], (DEPTH, D_MODEL, D_FF), D_MODEL),
        "w_mlp_down": w(ks[22], (DEPTH, D_FF, D_MODEL), D_FF),
    }


def _fwd_reference(x, mem, g_mix, w_in, b_gate, b_forget, qn_swa, kn_swa, sink_swa, rel_bias,
              qn_fox, kn_fox, g_mem, w_mem_kv, qn_mem, kn_mem, w_o_swa, w_o_fox, w_o_mem,
              w_out, g_mlp, w_mlp_up, w_mlp_down):
    B, S = x.shape[0], x.shape[1]
    M = mem.shape[1]
    for layer in range(DEPTH):
        h = rmsnorm(x, g_mix[layer])
        proj = h @ w_in[layer]
        qa, ka, va, qf, kf, vf, fl, qm, gl = jnp.split(proj, SPLIT_POINTS, axis=-1)

        qa = rmsnorm(qa.reshape(B, S, SWA_HEADS, SWA_HEAD_DIM), qn_swa[layer])
        ka = rmsnorm(ka.reshape(B, S, SWA_KV_HEADS, SWA_HEAD_DIM), kn_swa[layer])
        va = va.reshape(B, S, SWA_KV_HEADS, SWA_HEAD_DIM)
        ya = swa_attention(qa, ka, va, sink_swa[layer], rel_bias) @ w_o_swa[layer]

        qf = rmsnorm(qf.reshape(B, S, FOX_HEADS, FOX_HEAD_DIM), qn_fox[layer])
        kf = rmsnorm(kf.reshape(B, S, FOX_HEADS, FOX_HEAD_DIM), kn_fox[layer])
        vf = vf.reshape(B, S, FOX_HEADS, FOX_HEAD_DIM)
        log_f = jax.nn.log_sigmoid(fl.astype(jnp.float32) + b_forget[layer].astype(jnp.float32))
        yf = forgetting_attention(qf, kf, vf, log_f) @ w_o_fox[layer]

        mem_n = rmsnorm(mem, g_mem[layer])
        mk, mv = jnp.split(mem_n @ w_mem_kv[layer], 2, axis=-1)
        mk = rmsnorm(mk.reshape(B, M, MEM_HEADS, MEM_HEAD_DIM), kn_mem[layer])
        mv = mv.reshape(B, M, MEM_HEADS, MEM_HEAD_DIM)
        qm = rmsnorm(qm.reshape(B, S, MEM_HEADS, MEM_HEAD_DIM), qn_mem[layer])
        ym = memory_attention(qm, mk, mv) @ w_o_mem[layer]

        gates = jax.nn.sigmoid((gl + b_gate[layer]).astype(jnp.float32)).astype(x.dtype)
        gates = gates.reshape(B, S, N_BRANCHES, D_MODEL)
        merged = gates[:, :, 0] * ya + gates[:, :, 1] * yf + gates[:, :, 2] * ym
        x = x + merged @ w_out[layer]

        hm = rmsnorm(x, g_mlp[layer])
        u = jnp.square(jax.nn.relu(hm @ w_mlp_up[layer]))
        x = x + u @ w_mlp_down[layer]
    return x


import jax as _jax
import jax.numpy as _jnp

TWIN_FORMAT = 'train_step'
FWD_PARAMS = ['x', 'mem', 'g_mix', 'w_in', 'b_gate', 'b_forget', 'qn_swa', 'kn_swa', 'sink_swa', 'rel_bias', 'qn_fox', 'kn_fox', 'g_mem', 'w_mem_kv', 'qn_mem', 'kn_mem', 'w_o_swa', 'w_o_fox', 'w_o_mem', 'w_out', 'g_mlp', 'w_mlp_up', 'w_mlp_down']
TWIN_WEIGHTS = ['g_mix', 'w_in', 'b_gate', 'b_forget', 'qn_swa', 'kn_swa', 'sink_swa', 'rel_bias', 'qn_fox', 'kn_fox', 'g_mem', 'w_mem_kv', 'qn_mem', 'kn_mem', 'w_o_swa', 'w_o_fox', 'w_o_mem', 'w_out', 'g_mlp', 'w_mlp_up', 'w_mlp_down']
TWIN_DIFF_INPUT = 'x'
TWIN_INPUTS = ['x', 'mem', 'g_mix', 'w_in', 'b_gate', 'b_forget', 'qn_swa', 'kn_swa', 'sink_swa', 'rel_bias', 'qn_fox', 'kn_fox', 'g_mem', 'w_mem_kv', 'qn_mem', 'kn_mem', 'w_o_swa', 'w_o_fox', 'w_o_mem', 'w_out', 'g_mlp', 'w_mlp_up', 'w_mlp_down', 'loss_target', 'm_g_mix', 'm_w_in', 'm_b_gate', 'm_b_forget', 'm_qn_swa', 'm_kn_swa', 'm_sink_swa', 'm_rel_bias', 'm_qn_fox', 'm_kn_fox', 'm_g_mem', 'm_w_mem_kv', 'm_qn_mem', 'm_kn_mem', 'm_w_o_swa', 'm_w_o_fox', 'm_w_o_mem', 'm_w_out', 'm_g_mlp', 'm_w_mlp_up', 'm_w_mlp_down', 'v_g_mix', 'v_w_in', 'v_b_gate', 'v_b_forget', 'v_qn_swa', 'v_kn_swa', 'v_sink_swa', 'v_rel_bias', 'v_qn_fox', 'v_kn_fox', 'v_g_mem', 'v_w_mem_kv', 'v_qn_mem', 'v_kn_mem', 'v_w_o_swa', 'v_w_o_fox', 'v_w_o_mem', 'v_w_out', 'v_g_mlp', 'v_w_mlp_up', 'v_w_mlp_down']
TWIN_OUTPUTS = ['loss', 'grad_x', 'grad_g_mix', 'grad_w_in', 'grad_b_gate', 'grad_b_forget', 'grad_qn_swa', 'grad_kn_swa', 'grad_sink_swa', 'grad_rel_bias', 'grad_qn_fox', 'grad_kn_fox', 'grad_g_mem', 'grad_w_mem_kv', 'grad_qn_mem', 'grad_kn_mem', 'grad_w_o_swa', 'grad_w_o_fox', 'grad_w_o_mem', 'grad_w_out', 'grad_g_mlp', 'grad_w_mlp_up', 'grad_w_mlp_down', 'delta_g_mix', 'delta_w_in', 'delta_b_gate', 'delta_b_forget', 'delta_qn_swa', 'delta_kn_swa', 'delta_sink_swa', 'delta_rel_bias', 'delta_qn_fox', 'delta_kn_fox', 'delta_g_mem', 'delta_w_mem_kv', 'delta_qn_mem', 'delta_kn_mem', 'delta_w_o_swa', 'delta_w_o_fox', 'delta_w_o_mem', 'delta_w_out', 'delta_g_mlp', 'delta_w_mlp_up', 'delta_w_mlp_down', 'new_m_g_mix', 'new_m_w_in', 'new_m_b_gate', 'new_m_b_forget', 'new_m_qn_swa', 'new_m_kn_swa', 'new_m_sink_swa', 'new_m_rel_bias', 'new_m_qn_fox', 'new_m_kn_fox', 'new_m_g_mem', 'new_m_w_mem_kv', 'new_m_qn_mem', 'new_m_kn_mem', 'new_m_w_o_swa', 'new_m_w_o_fox', 'new_m_w_o_mem', 'new_m_w_out', 'new_m_g_mlp', 'new_m_w_mlp_up', 'new_m_w_mlp_down', 'new_v_g_mix', 'new_v_w_in', 'new_v_b_gate', 'new_v_b_forget', 'new_v_qn_swa', 'new_v_kn_swa', 'new_v_sink_swa', 'new_v_rel_bias', 'new_v_qn_fox', 'new_v_kn_fox', 'new_v_g_mem', 'new_v_w_mem_kv', 'new_v_qn_mem', 'new_v_kn_mem', 'new_v_w_o_swa', 'new_v_w_o_fox', 'new_v_w_o_mem', 'new_v_w_out', 'new_v_g_mlp', 'new_v_w_mlp_up', 'new_v_w_mlp_down']
TWIN_LEAF_KINDS = {'loss': 'loss', 'grad_x': 'grad_x', 'grad_g_mix': 'grad_w', 'grad_w_in': 'grad_w', 'grad_b_gate': 'grad_w', 'grad_b_forget': 'grad_w', 'grad_qn_swa': 'grad_w', 'grad_kn_swa': 'grad_w', 'grad_sink_swa': 'grad_w', 'grad_rel_bias': 'grad_w', 'grad_qn_fox': 'grad_w', 'grad_kn_fox': 'grad_w', 'grad_g_mem': 'grad_w', 'grad_w_mem_kv': 'grad_w', 'grad_qn_mem': 'grad_w', 'grad_kn_mem': 'grad_w', 'grad_w_o_swa': 'grad_w', 'grad_w_o_fox': 'grad_w', 'grad_w_o_mem': 'grad_w', 'grad_w_out': 'grad_w', 'grad_g_mlp': 'grad_w', 'grad_w_mlp_up': 'grad_w', 'grad_w_mlp_down': 'grad_w', 'delta_g_mix': 'delta_w', 'delta_w_in': 'delta_w', 'delta_b_gate': 'delta_w', 'delta_b_forget': 'delta_w', 'delta_qn_swa': 'delta_w', 'delta_kn_swa': 'delta_w', 'delta_sink_swa': 'delta_w', 'delta_rel_bias': 'delta_w', 'delta_qn_fox': 'delta_w', 'delta_kn_fox': 'delta_w', 'delta_g_mem': 'delta_w', 'delta_w_mem_kv': 'delta_w', 'delta_qn_mem': 'delta_w', 'delta_kn_mem': 'delta_w', 'delta_w_o_swa': 'delta_w', 'delta_w_o_fox': 'delta_w', 'delta_w_o_mem': 'delta_w', 'delta_w_out': 'delta_w', 'delta_g_mlp': 'delta_w', 'delta_w_mlp_up': 'delta_w', 'delta_w_mlp_down': 'delta_w', 'new_m_g_mix': 'new_m', 'new_m_w_in': 'new_m', 'new_m_b_gate': 'new_m', 'new_m_b_forget': 'new_m', 'new_m_qn_swa': 'new_m', 'new_m_kn_swa': 'new_m', 'new_m_sink_swa': 'new_m', 'new_m_rel_bias': 'new_m', 'new_m_qn_fox': 'new_m', 'new_m_kn_fox': 'new_m', 'new_m_g_mem': 'new_m', 'new_m_w_mem_kv': 'new_m', 'new_m_qn_mem': 'new_m', 'new_m_kn_mem': 'new_m', 'new_m_w_o_swa': 'new_m', 'new_m_w_o_fox': 'new_m', 'new_m_w_o_mem': 'new_m', 'new_m_w_out': 'new_m', 'new_m_g_mlp': 'new_m', 'new_m_w_mlp_up': 'new_m', 'new_m_w_mlp_down': 'new_m', 'new_v_g_mix': 'new_v', 'new_v_w_in': 'new_v', 'new_v_b_gate': 'new_v', 'new_v_b_forget': 'new_v', 'new_v_qn_swa': 'new_v', 'new_v_kn_swa': 'new_v', 'new_v_sink_swa': 'new_v', 'new_v_rel_bias': 'new_v', 'new_v_qn_fox': 'new_v', 'new_v_kn_fox': 'new_v', 'new_v_g_mem': 'new_v', 'new_v_w_mem_kv': 'new_v', 'new_v_qn_mem': 'new_v', 'new_v_kn_mem': 'new_v', 'new_v_w_o_swa': 'new_v', 'new_v_w_o_fox': 'new_v', 'new_v_w_o_mem': 'new_v', 'new_v_w_out': 'new_v', 'new_v_g_mlp': 'new_v', 'new_v_w_mlp_up': 'new_v', 'new_v_w_mlp_down': 'new_v'}


def _forward(args):
    return _fwd_reference(*[args[k] for k in FWD_PARAMS])


def _output_shape():
    out = _jax.eval_shape(lambda: _forward(_fwd_setup_inputs(0)))
    return out.shape, out.dtype

N_MICROBATCH = 1
ADAM_LR = 0.001
ADAM_B1 = 0.9
ADAM_B2 = 0.999
ADAM_EPS = 1e-08
ADAM_WD = 0.01
ADAM_STEP = 10
PER_EXAMPLE_BATCH_AXIS = {'x': 0, 'mem': 0, 'loss_target': 0}
SHARED_INPUTS = []
_WEIGHT_DTYPES = {'g_mix': _jnp.float32, 'w_in': _jnp.float32, 'b_gate': _jnp.float32, 'b_forget': _jnp.float32, 'qn_swa': _jnp.float32, 'kn_swa': _jnp.float32, 'sink_swa': _jnp.float32, 'rel_bias': _jnp.float32, 'qn_fox': _jnp.float32, 'kn_fox': _jnp.float32, 'g_mem': _jnp.float32, 'w_mem_kv': _jnp.float32, 'qn_mem': _jnp.float32, 'kn_mem': _jnp.float32, 'w_o_swa': _jnp.float32, 'w_o_fox': _jnp.float32, 'w_o_mem': _jnp.float32, 'w_out': _jnp.float32, 'g_mlp': _jnp.float32, 'w_mlp_up': _jnp.float32, 'w_mlp_down': _jnp.float32}
MOMENT_SCALE = {'g_mix': 9.081573e-01, 'w_in': 1.201185e-01, 'b_gate': 1.309871e-01, 'b_forget': 2.993419e+01, 'qn_swa': 2.110880e+00, 'kn_swa': 2.141834e+00, 'sink_swa': 4.134512e-01, 'rel_bias': 2.844018e-01, 'qn_fox': 6.268462e+00, 'kn_fox': 6.327390e+00, 'g_mem': 4.456839e-01, 'w_mem_kv': 3.896624e-01, 'qn_mem': 7.200743e-01, 'kn_mem': 7.427269e-01, 'w_o_swa': 1.307464e-01, 'w_o_fox': 1.565293e-01, 'w_o_mem': 4.062371e-01, 'w_out': 3.233891e-01, 'g_mlp': 9.631489e+01, 'w_mlp_up': 7.022000e-01, 'w_mlp_down': 7.961900e+00}


def _to_microbatches(a, axis):
    t = _jnp.moveaxis(a, axis, 0)
    t = t.reshape((N_MICROBATCH, t.shape[0] // N_MICROBATCH) + t.shape[1:])
    return _jnp.moveaxis(t, 1, axis + 1)


def setup_inputs(seed: int = 0) -> dict:
    inp = _fwd_setup_inputs(seed)
    key = _jax.random.fold_in(_jax.random.key(seed), 7919)
    shape, _ = _output_shape()
    out = dict(inp)
    out["loss_target"] = _jax.random.normal(_jax.random.fold_in(key, 0), shape, _jnp.float32)
    for i, name in enumerate(TWIN_WEIGHTS):
        w = inp[name].astype(_jnp.float32)
        if MOMENT_SCALE is None:
            s = _jnp.sqrt(_jnp.mean(_jnp.square(w)) + 1e-30)
        else:
            s = MOMENT_SCALE[name]
        km, kv = _jax.random.split(_jax.random.fold_in(key, i + 1))
        out[name] = w
        out["m_" + name] = s * _jax.random.normal(km, w.shape, _jnp.float32)
        out["v_" + name] = (s * s) * _jax.random.uniform(kv, w.shape, _jnp.float32, 0.5, 1.5)
    if N_MICROBATCH > 1:
        for name, axis in PER_EXAMPLE_BATCH_AXIS.items():
            out[name] = _to_microbatches(out[name], axis)
    return {'x': out['x'], 'mem': out['mem'], 'g_mix': out['g_mix'], 'w_in': out['w_in'], 'b_gate': out['b_gate'], 'b_forget': out['b_forget'], 'qn_swa': out['qn_swa'], 'kn_swa': out['kn_swa'], 'sink_swa': out['sink_swa'], 'rel_bias': out['rel_bias'], 'qn_fox': out['qn_fox'], 'kn_fox': out['kn_fox'], 'g_mem': out['g_mem'], 'w_mem_kv': out['w_mem_kv'], 'qn_mem': out['qn_mem'], 'kn_mem': out['kn_mem'], 'w_o_swa': out['w_o_swa'], 'w_o_fox': out['w_o_fox'], 'w_o_mem': out['w_o_mem'], 'w_out': out['w_out'], 'g_mlp': out['g_mlp'], 'w_mlp_up': out['w_mlp_up'], 'w_mlp_down': out['w_mlp_down'], 'loss_target': out['loss_target'], 'm_g_mix': out['m_g_mix'], 'm_w_in': out['m_w_in'], 'm_b_gate': out['m_b_gate'], 'm_b_forget': out['m_b_forget'], 'm_qn_swa': out['m_qn_swa'], 'm_kn_swa': out['m_kn_swa'], 'm_sink_swa': out['m_sink_swa'], 'm_rel_bias': out['m_rel_bias'], 'm_qn_fox': out['m_qn_fox'], 'm_kn_fox': out['m_kn_fox'], 'm_g_mem': out['m_g_mem'], 'm_w_mem_kv': out['m_w_mem_kv'], 'm_qn_mem': out['m_qn_mem'], 'm_kn_mem': out['m_kn_mem'], 'm_w_o_swa': out['m_w_o_swa'], 'm_w_o_fox': out['m_w_o_fox'], 'm_w_o_mem': out['m_w_o_mem'], 'm_w_out': out['m_w_out'], 'm_g_mlp': out['m_g_mlp'], 'm_w_mlp_up': out['m_w_mlp_up'], 'm_w_mlp_down': out['m_w_mlp_down'], 'v_g_mix': out['v_g_mix'], 'v_w_in': out['v_w_in'], 'v_b_gate': out['v_b_gate'], 'v_b_forget': out['v_b_forget'], 'v_qn_swa': out['v_qn_swa'], 'v_kn_swa': out['v_kn_swa'], 'v_sink_swa': out['v_sink_swa'], 'v_rel_bias': out['v_rel_bias'], 'v_qn_fox': out['v_qn_fox'], 'v_kn_fox': out['v_kn_fox'], 'v_g_mem': out['v_g_mem'], 'v_w_mem_kv': out['v_w_mem_kv'], 'v_qn_mem': out['v_qn_mem'], 'v_kn_mem': out['v_kn_mem'], 'v_w_o_swa': out['v_w_o_swa'], 'v_w_o_fox': out['v_w_o_fox'], 'v_w_o_mem': out['v_w_o_mem'], 'v_w_out': out['v_w_out'], 'v_g_mlp': out['v_g_mlp'], 'v_w_mlp_up': out['v_w_mlp_up'], 'v_w_mlp_down': out['v_w_mlp_down']}


def _loss(weights, diff, rest, loss_target):
    with _jax.named_scope("forward"):
        args = {**rest, TWIN_DIFF_INPUT: diff, **{k: w.astype(_WEIGHT_DTYPES[k]) for k, w in weights.items()}}
        y = _forward(args)
    with _jax.named_scope("loss_head"):
        err = _jnp.square(y.astype(_jnp.float32) - loss_target)
        return 0.5 * _jnp.sum(_jnp.mean(err, axis=-1)) if err.ndim else 0.5 * err


def _adamw(w, g, m, v):
    m = ADAM_B1 * m + (1.0 - ADAM_B1) * g
    v = ADAM_B2 * v + (1.0 - ADAM_B2) * _jnp.square(g)
    m_hat = m / (1.0 - ADAM_B1 ** ADAM_STEP)
    v_hat = v / (1.0 - ADAM_B2 ** ADAM_STEP)
    delta = -ADAM_LR * (m_hat / (_jnp.sqrt(v_hat) + ADAM_EPS) + ADAM_WD * w)
    return delta, m, v


def reference(x, mem, g_mix, w_in, b_gate, b_forget, qn_swa, kn_swa, sink_swa, rel_bias, qn_fox, kn_fox, g_mem, w_mem_kv, qn_mem, kn_mem, w_o_swa, w_o_fox, w_o_mem, w_out, g_mlp, w_mlp_up, w_mlp_down, loss_target, m_g_mix, m_w_in, m_b_gate, m_b_forget, m_qn_swa, m_kn_swa, m_sink_swa, m_rel_bias, m_qn_fox, m_kn_fox, m_g_mem, m_w_mem_kv, m_qn_mem, m_kn_mem, m_w_o_swa, m_w_o_fox, m_w_o_mem, m_w_out, m_g_mlp, m_w_mlp_up, m_w_mlp_down, v_g_mix, v_w_in, v_b_gate, v_b_forget, v_qn_swa, v_kn_swa, v_sink_swa, v_rel_bias, v_qn_fox, v_kn_fox, v_g_mem, v_w_mem_kv, v_qn_mem, v_kn_mem, v_w_o_swa, v_w_o_fox, v_w_o_mem, v_w_out, v_g_mlp, v_w_mlp_up, v_w_mlp_down):
    given = dict(x=x, mem=mem, g_mix=g_mix, w_in=w_in, b_gate=b_gate, b_forget=b_forget, qn_swa=qn_swa, kn_swa=kn_swa, sink_swa=sink_swa, rel_bias=rel_bias, qn_fox=qn_fox, kn_fox=kn_fox, g_mem=g_mem, w_mem_kv=w_mem_kv, qn_mem=qn_mem, kn_mem=kn_mem, w_o_swa=w_o_swa, w_o_fox=w_o_fox, w_o_mem=w_o_mem, w_out=w_out, g_mlp=g_mlp, w_mlp_up=w_mlp_up, w_mlp_down=w_mlp_down, loss_target=loss_target, m_g_mix=m_g_mix, m_w_in=m_w_in, m_b_gate=m_b_gate, m_b_forget=m_b_forget, m_qn_swa=m_qn_swa, m_kn_swa=m_kn_swa, m_sink_swa=m_sink_swa, m_rel_bias=m_rel_bias, m_qn_fox=m_qn_fox, m_kn_fox=m_kn_fox, m_g_mem=m_g_mem, m_w_mem_kv=m_w_mem_kv, m_qn_mem=m_qn_mem, m_kn_mem=m_kn_mem, m_w_o_swa=m_w_o_swa, m_w_o_fox=m_w_o_fox, m_w_o_mem=m_w_o_mem, m_w_out=m_w_out, m_g_mlp=m_g_mlp, m_w_mlp_up=m_w_mlp_up, m_w_mlp_down=m_w_mlp_down, v_g_mix=v_g_mix, v_w_in=v_w_in, v_b_gate=v_b_gate, v_b_forget=v_b_forget, v_qn_swa=v_qn_swa, v_kn_swa=v_kn_swa, v_sink_swa=v_sink_swa, v_rel_bias=v_rel_bias, v_qn_fox=v_qn_fox, v_kn_fox=v_kn_fox, v_g_mem=v_g_mem, v_w_mem_kv=v_w_mem_kv, v_qn_mem=v_qn_mem, v_kn_mem=v_kn_mem, v_w_o_swa=v_w_o_swa, v_w_o_fox=v_w_o_fox, v_w_o_mem=v_w_o_mem, v_w_out=v_w_out, v_g_mlp=v_g_mlp, v_w_mlp_up=v_w_mlp_up, v_w_mlp_down=v_w_mlp_down)
    weights = {n: given[n] for n in TWIN_WEIGHTS}
    shared = {n: given[n] for n in SHARED_INPUTS}
    per_example = {n: given[n] for n in ['x', 'mem']}
    grad_fn = _jax.value_and_grad(_loss, argnums=(0, 1))

    def one_microbatch(ex, loss_target):
        ex = dict(ex)
        diff = ex.pop(TWIN_DIFF_INPUT)
        return grad_fn(weights, diff, {**shared, **ex}, loss_target)

    if N_MICROBATCH == 1:
        loss, (grad_w, grad_x) = one_microbatch(per_example, given["loss_target"])
    else:
        def body(carry, xs):
            loss_sum, grad_sum = carry
            l_k, (gw_k, gx_k) = one_microbatch(xs[0], xs[1])
            with _jax.named_scope("update"):
                return (loss_sum + l_k, _jax.tree.map(_jnp.add, grad_sum, gw_k)), gx_k

        init = (_jnp.zeros((), _jnp.float32), _jax.tree.map(_jnp.zeros_like, weights))
        (loss, grad_w), grad_x = _jax.lax.scan(body, init, (per_example, given["loss_target"]))
    with _jax.named_scope("update"):
        delta_w, new_m, new_v = {}, {}, {}
        for n in TWIN_WEIGHTS:
            delta_w[n], new_m[n], new_v[n] = _adamw(weights[n], grad_w[n], given["m_" + n], given["v_" + n])
    return (loss, grad_x, *[grad_w[n] for n in TWIN_WEIGHTS], *[delta_w[n] for n in TWIN_WEIGHTS],
            *[new_m[n] for n in TWIN_WEIGHTS], *[new_v[n] for n in TWIN_WEIGHTS])
```

```python
import functools
import math

import jax
import jax.numpy as jnp
from jax import lax
from jax.experimental import pallas as pl
from jax.experimental.pallas import tpu as pltpu

F32 = jnp.float32
BF16 = jnp.bfloat16

D_MODEL = 1024
N_MEM = 256
SWA_HEADS = 8
SWA_KV_HEADS = 2
SWA_HEAD_DIM = 64
WINDOW = 128
FOX_HEADS = 8
FOX_HEAD_DIM = 64
MEM_HEADS = 4
MEM_HEAD_DIM = 128
D_FF = 4 * D_MODEL
REL_BUCKETS = 32
REL_MAX_DIST = 128
EPS = 1e-6
NEG = -1e30
GATE_W = 3 * D_MODEL
IN_WIDTH = 5896
N_SHARD = 4
IN_SHARD = IN_WIDTH // N_SHARD
IN_SHARD_PAD = 1536

ADAM_LR = 0.001
ADAM_B1 = 0.9
ADAM_B2 = 0.999
ADAM_EPS = 1e-08
ADAM_WD = 0.01
ADAM_STEP = 10

LANES = 128
V7X_VMEM_BYTES = 64 * 1024 * 1024
VMEM_LIMIT = V7X_VMEM_BYTES * 3 // 4

C_QA, C_QF, C_KF, C_VF, C_QM, C_KA, C_VA, C_FL, C_GL = 0, 512, 1024, 1536, 2048, 2560, 2688, 2816, 3072
LO_W = 3072
PROJ_W = 6144

NN = (((1,), (0,)), ((), ()))
NT = (((1,), (1,)), ((), ()))
TN = (((0,), (0,)), ((), ()))


def _dot(a, b, dims=NN):
    return lax.dot_general(a, b, dims, preferred_element_type=F32)


def _cparams(*sem):
    return pltpu.CompilerParams(dimension_semantics=sem, vmem_limit_bytes=VMEM_LIMIT)


def _split3(a):
    hi = a.astype(BF16)
    r1 = a - hi.astype(F32)
    mid = r1.astype(BF16)
    lo = (r1 - mid.astype(F32)).astype(BF16)
    return hi, mid, lo


def _dot3_right(a, g):
    hi, mid, lo = _split3(a)
    return _dot(hi, g) + _dot(mid, g) + _dot(lo, g)


def _dot3_left(g, a):
    hi, mid, lo = _split3(a)
    return _dot(g, hi) + _dot(g, mid) + _dot(g, lo)


def _group_mean_matrix(d):
    r = jnp.arange(LANES)
    return jnp.where((r[:, None] // d) == (r[None, :] // d), 1.0 / d, 0.0).astype(BF16)


def _lane(shape):
    return lax.broadcasted_iota(jnp.int32, shape, len(shape) - 1)


def _matmul(name, a, b, *, dims, grid, a_spec, b_spec, acc_shape, outs, epilogue, extra=()):
    nk = grid[2]
    n_extra = len(extra)

    def body(a_ref, b_ref, *rest):
        extra_refs = rest[:n_extra]
        out_refs = rest[n_extra:-1]
        acc_ref = rest[-1]
        i, j, k = pl.program_id(0), pl.program_id(1), pl.program_id(2)
        part = _dot(a_ref[...].astype(BF16), b_ref[...].astype(BF16), dims)

        @pl.when(k == 0)
        def _():
            acc_ref[...] = part

        @pl.when(k > 0)
        def _():
            acc_ref[...] += part

        @pl.when(k == nk - 1)
        def _():
            epilogue(acc_ref, extra_refs, out_refs, (i, j))

    res = pl.pallas_call(
        body,
        name=name,
        grid=grid,
        in_specs=[a_spec, b_spec] + [s for _, s in extra],
        out_specs=[s for _, s in outs],
        out_shape=[s for s, _ in outs],
        scratch_shapes=[pltpu.VMEM(acc_shape, F32)],
        compiler_params=_cparams("arbitrary", "arbitrary", "arbitrary"),
    )(a, b, *[x for x, _ in extra])
    return res


def _epi_store(acc_ref, extra_refs, out_refs, ij):
    out_refs[0][...] = acc_ref[...].astype(out_refs[0].dtype)


def _rms_rows(x, g):
    r = lax.rsqrt(jnp.mean(x * x, axis=-1, keepdims=True) + EPS)
    return x * r, r


def _rmsnorm_bwd_rows(dh, x, g):
    xhat, r = _rms_rows(x, g)
    dxh = dh * g
    dx = r * (dxh - xhat * jnp.mean(dxh * xhat, axis=-1, keepdims=True))
    return dx, jnp.sum(dh * xhat, axis=0, keepdims=True)


def _rmsnorm(name, x, g, tb):
    T, Dm = x.shape

    def body(x_ref, g_ref, o_ref):
        xhat, _ = _rms_rows(x_ref[...], None)
        o_ref[...] = (xhat * g_ref[...]).astype(o_ref.dtype)

    return pl.pallas_call(
        body, name=name, grid=(T // tb,),
        in_specs=[pl.BlockSpec((tb, Dm), lambda i: (i, 0)), pl.BlockSpec((1, Dm), lambda i: (0, 0))],
        out_specs=pl.BlockSpec((tb, Dm), lambda i: (i, 0)),
        out_shape=jax.ShapeDtypeStruct((T, Dm), BF16),
        compiler_params=_cparams("parallel"),
    )(x, g)


def _head_norm(x, gm, gain):
    ms = _dot3_right(x * x, gm)
    r = lax.rsqrt(ms + EPS)
    return x * r * gain, x * r


def _head_norm_bwd(dy, x, gm, gain):
    ms = _dot3_right(x * x, gm)
    r = lax.rsqrt(ms + EPS)
    xhat = x * r
    dxh = dy * gain
    dx = r * (dxh - xhat * _dot3_right(dxh * xhat, gm))
    return dx, jnp.sum(dy * xhat, axis=0, keepdims=True)


def _log_sigmoid(z):
    return jnp.minimum(z, 0.0) - jnp.log(1.0 + jnp.exp(-jnp.abs(z)))


def _prep_fwd(proj, gains, bfor, tril, gm64, gm128, T, tb):
    nb = T // tb

    def body(qa_ref, qf_ref, kf_ref, vf_ref, qm_ref, ka_ref, va_ref, fl_ref, gains_ref, bfor_ref, tril_ref,
             gm64_ref, gm128_ref,
             qa_o, qf_o, kf_o, vf_o, qm_o, kad_o, vad_o, cb_o, crow_o, carry):
        i = pl.program_id(0)
        gm64v = gm64_ref[...]
        gm128v = gm128_ref[...]
        lane = _lane((tb, LANES))

        def norm512(src, dst, row, gm):
            gain = gains_ref[row:row + 1, :]
            for c in range(4):
                sl = slice(c * LANES, (c + 1) * LANES)
                y, _ = _head_norm(src[:, sl], gm, gain)
                dst[:, sl] = y.astype(dst.dtype)

        norm512(qa_ref, qa_o, 0, gm64v)
        norm512(qf_ref, qf_o, 2, gm64v)
        norm512(kf_ref, kf_o, 3, gm64v)
        norm512(qm_ref, qm_o, 4, gm128v)
        vf_o[...] = vf_ref[...].astype(vf_o.dtype)

        ka_n, _ = _head_norm(ka_ref[...], gm64v, gains_ref[1:2, :])
        ka_r = pltpu.roll(ka_n, 64, 1)
        va = va_ref[...]
        va_r = pltpu.roll(va, 64, 1)
        lo = lane < 64
        kad_o[0] = jnp.where(lo, ka_n, ka_r).astype(kad_o.dtype)
        kad_o[1] = jnp.where(lo, ka_r, ka_n).astype(kad_o.dtype)
        vad_o[0] = jnp.where(lo, va, va_r).astype(vad_o.dtype)
        vad_o[1] = jnp.where(lo, va_r, va).astype(vad_o.dtype)

        @pl.when(i == 0)
        def _():
            carry[...] = jnp.zeros_like(carry)

        logf = jnp.where(lane < FOX_HEADS, _log_sigmoid(fl_ref[...] + bfor_ref[...]), 0.0)
        c = _dot3_left(tril_ref[...], logf) + carry[0:1, :]
        carry[...] = jnp.broadcast_to(c[tb - 1:tb, :], carry.shape)
        for h in range(FOX_HEADS):
            col = jnp.sum(jnp.where(lane == h, c, 0.0), axis=1, keepdims=True)
            cb_o[h] = jnp.broadcast_to(col, (tb, LANES))
        crow_o[...] = c.T[0:8, :]

    def seg(width, start):
        return pl.BlockSpec((tb, width), lambda i, s=start // width: (i, s))

    const = lambda shape: pl.BlockSpec(shape, lambda i: tuple(0 for _ in shape))
    rows512 = pl.BlockSpec((tb, 512), lambda i: (i, 0))
    outs = pl.pallas_call(
        body, name="prep_fwd", grid=(nb,),
        in_specs=[seg(512, C_QA), seg(512, C_QF), seg(512, C_KF), seg(512, C_VF), seg(512, C_QM),
                  seg(128, C_KA), seg(128, C_VA), seg(128, C_FL),
                  const((8, LANES)), const((1, LANES)), const((tb, tb)), const((LANES, LANES)), const((LANES, LANES))],
        out_specs=[rows512, rows512, rows512, rows512, rows512,
                   pl.BlockSpec((2, tb, LANES), lambda i: (0, i, 0)), pl.BlockSpec((2, tb, LANES), lambda i: (0, i, 0)),
                   pl.BlockSpec((FOX_HEADS, tb, LANES), lambda i: (0, i, 0)), pl.BlockSpec((8, tb), lambda i: (0, i))],
        out_shape=[jax.ShapeDtypeStruct((T, 512), BF16)] * 5
        + [jax.ShapeDtypeStruct((2, T, LANES), BF16)] * 2
        + [jax.ShapeDtypeStruct((FOX_HEADS, T, LANES), F32), jax.ShapeDtypeStruct((8, T), F32)],
        scratch_shapes=[pltpu.VMEM((8, LANES), F32)],
        compiler_params=_cparams("arbitrary"),
    )(proj, proj, proj, proj, proj, proj, proj, proj, gains, bfor, tril, gm64, gm128)
    return outs


def _prep_bwd(proj, dqa, dkad, dvad, dqf, dkf, dvf, dqm, dccol, gains, bfor, triu, gm64, gm128, T, tb):
    nb = T // tb

    def body(qa_ref, qf_ref, kf_ref, qm_ref, ka_ref, fl_ref,
             dqa_ref, dkad_ref, dvad_ref, dqf_ref, dkf_ref, dvf_ref, dqm_ref, dc_ref,
             gains_ref, bfor_ref, triu_ref, gm64_ref, gm128_ref,
             dlo_o, gacc_o, carry):
        i = pl.program_id(0)
        gm64v = gm64_ref[...]
        gm128v = gm128_ref[...]
        lane = _lane((tb, LANES))

        @pl.when(i == 0)
        def _():
            carry[...] = jnp.zeros_like(carry)
            gacc_o[...] = jnp.zeros_like(gacc_o)

        def norm512_bwd(dsrc, xsrc, col0, row, gm):
            gain = gains_ref[row:row + 1, :]
            gsum = jnp.zeros((1, LANES), F32)
            for c in range(4):
                sl = slice(c * LANES, (c + 1) * LANES)
                dx, dg = _head_norm_bwd(dsrc[:, sl], xsrc[:, sl], gm, gain)
                dlo_o[:, col0 + c * LANES:col0 + (c + 1) * LANES] = dx.astype(dlo_o.dtype)
                gsum = gsum + dg
            gacc_o[row:row + 1, :] += gsum

        norm512_bwd(dqa_ref, qa_ref, C_QA, 0, gm64v)
        norm512_bwd(dqf_ref, qf_ref, C_QF, 2, gm64v)
        norm512_bwd(dkf_ref, kf_ref, C_KF, 3, gm64v)
        norm512_bwd(dqm_ref, qm_ref, C_QM, 4, gm128v)
        dlo_o[:, C_VF:C_VF + 512] = dvf_ref[...].astype(dlo_o.dtype)

        lo = lane < 64

        def fold(ref):
            f0 = ref[0] + pltpu.roll(ref[0], 64, 1)
            f1 = ref[1] + pltpu.roll(ref[1], 64, 1)
            return jnp.where(lo, f0, f1)

        dka, dg = _head_norm_bwd(fold(dkad_ref), ka_ref[...], gm64v, gains_ref[1:2, :])
        gacc_o[1:2, :] += dg
        dlo_o[:, C_KA:C_KA + LANES] = dka.astype(dlo_o.dtype)
        dlo_o[:, C_VA:C_VA + LANES] = fold(dvad_ref).astype(dlo_o.dtype)

        dc = dc_ref[...]
        dlogf = _dot3_left(triu_ref[...], dc) + carry[0:1, :]
        carry[...] = jnp.broadcast_to(dlogf[0:1, :], carry.shape)
        z = fl_ref[...] + bfor_ref[...]
        dfl = jnp.where(lane < FOX_HEADS, dlogf / (1.0 + jnp.exp(z)), 0.0)
        gacc_o[5:6, :] += jnp.sum(dfl, axis=0, keepdims=True)
        dlo_o[:, C_FL:C_FL + LANES] = dfl.astype(dlo_o.dtype)
        dlo_o[:, C_FL + LANES:C_FL + 2 * LANES] = jnp.zeros((tb, LANES), dlo_o.dtype)

    rev = lambda i: nb - 1 - i

    def seg(width, start):
        return pl.BlockSpec((tb, width), lambda i, s=start // width: (rev(i), s))

    const = lambda shape: pl.BlockSpec(shape, lambda i: tuple(0 for _ in shape))
    rows512 = pl.BlockSpec((tb, 512), lambda i: (rev(i), 0))
    dup = pl.BlockSpec((2, tb, LANES), lambda i: (0, rev(i), 0))
    return pl.pallas_call(
        body, name="prep_bwd", grid=(nb,),
        in_specs=[seg(512, C_QA), seg(512, C_QF), seg(512, C_KF), seg(512, C_QM), seg(128, C_KA), seg(128, C_FL),
                  rows512, dup, dup, rows512, rows512, rows512, rows512,
                  pl.BlockSpec((tb, LANES), lambda i: (rev(i), 0)),
                  const((8, LANES)), const((1, LANES)), const((tb, tb)), const((LANES, LANES)), const((LANES, LANES))],
        out_specs=[pl.BlockSpec((tb, LO_W), lambda i: (rev(i), 0)), const((8, LANES))],
        out_shape=[jax.ShapeDtypeStruct((T, LO_W), BF16), jax.ShapeDtypeStruct((8, LANES), F32)],
        scratch_shapes=[pltpu.VMEM((8, LANES), F32)],
        compiler_params=_cparams("arbitrary"),
    )(proj, proj, proj, proj, proj, proj, dqa, dkad, dvad, dqf, dkf, dvf, dqm, dccol, gains, bfor, triu, gm64, gm128)


def _fox_scores(qh, kv, cb, crow_row, rows, cols, scale, tk):
    s = _dot(qh, kv, NT) * scale
    s = s + jnp.tile(cb, (1, tk // LANES)) - crow_row
    return jnp.where(cols <= rows, s, NEG)


def _fox_fwd(q, k, v, cb, crow, T, tq):
    nq = T // tq
    tk = tq
    scale = FOX_HEAD_DIM ** -0.5

    def body(q_ref, k_ref, v_ref, cb_ref, crow_ref, o_ref, lse_ref, m_s, l_s, acc_s):
        p_, i, j = pl.program_id(0), pl.program_id(1), pl.program_id(2)

        @pl.when(j == 0)
        def _():
            m_s[...] = jnp.full(m_s.shape, NEG, F32)
            l_s[...] = jnp.zeros_like(l_s)
            acc_s[...] = jnp.zeros_like(acc_s)

        @pl.when(j <= i)
        def _():
            lane = _lane((tq, LANES))
            qv, kv, vv = q_ref[...], k_ref[...], v_ref[...]
            rows = lax.broadcasted_iota(jnp.int32, (tq, tk), 0) + i * tq
            cols = lax.broadcasted_iota(jnp.int32, (tq, tk), 1) + j * tk
            for sub in range(2):
                hm = (lane >= 64) if sub else (lane < 64)
                qh = jnp.where(hm, qv, jnp.zeros_like(qv))
                s = _fox_scores(qh, kv, cb_ref[sub], crow_ref[pl.ds(2 * p_ + sub, 1), :], rows, cols, scale, tk)
                m_prev, l_prev = m_s[sub], l_s[sub]
                m_next = jnp.maximum(m_prev, jnp.max(s, axis=1, keepdims=True))
                p = jnp.exp(s - jnp.tile(m_next, (1, tk // LANES)))
                alpha = jnp.exp(m_prev - m_next)
                l_s[sub] = alpha * l_prev + jnp.sum(p, axis=1, keepdims=True)
                m_s[sub] = m_next
                acc_s[sub] = acc_s[sub] * alpha + _dot(p.astype(BF16), vv)

        @pl.when(j == nq - 1)
        def _():
            lane = _lane((tq, LANES))
            o_ref[...] = jnp.where(lane < 64, acc_s[0] / l_s[0], acc_s[1] / l_s[1]).astype(o_ref.dtype)
            lse_ref[0] = m_s[0] + jnp.log(l_s[0])
            lse_ref[1] = m_s[1] + jnp.log(l_s[1])

    qspec = pl.BlockSpec((tq, LANES), lambda p, i, j: (i, p))
    kspec = pl.BlockSpec((tk, LANES), lambda p, i, j: (jnp.minimum(j, i), p))
    stat = pl.BlockSpec((2, tq, LANES), lambda p, i, j: (p, i, 0))
    return pl.pallas_call(
        body, name="fox_fwd", grid=(4, nq, nq),
        in_specs=[qspec, kspec, kspec, stat, pl.BlockSpec((8, tk), lambda p, i, j: (0, jnp.minimum(j, i)))],
        out_specs=[qspec, stat],
        out_shape=[jax.ShapeDtypeStruct((T, 512), BF16), jax.ShapeDtypeStruct((FOX_HEADS, T, LANES), F32)],
        scratch_shapes=[pltpu.VMEM((2, tq, LANES), F32)] * 3,
        compiler_params=_cparams("parallel", "parallel", "arbitrary"),
    )(q, k, v, cb, crow)


def _fox_bwd(q, k, v, do, lse, delta, cb, crow, T, tq):
    nq = T // tq
    tk = tq
    scale = FOX_HEAD_DIM ** -0.5

    def body(q_ref, k_ref, v_ref, do_ref, lse_ref, dl_ref, cb_ref, crow_ref,
             dq_ref, dk_ref, dv_ref, dc_ref, dcq_ref, dk_s, dv_s, dc_s):
        p_, j, i = pl.program_id(0), pl.program_id(1), pl.program_id(2)

        @pl.when((j == 0) & (i == 0))
        def _():
            dq_ref[...] = jnp.zeros_like(dq_ref)
            dcq_ref[...] = jnp.zeros_like(dcq_ref)

        @pl.when(i == 0)
        def _():
            dk_s[...] = jnp.zeros_like(dk_s)
            dv_s[...] = jnp.zeros_like(dv_s)
            dc_s[...] = jnp.zeros_like(dc_s)

        @pl.when(i >= j)
        def _():
            lane = _lane((tq, LANES))
            qv, kv, vv, dov = q_ref[...], k_ref[...], v_ref[...], do_ref[...]
            rows = lax.broadcasted_iota(jnp.int32, (tq, tk), 0) + i * tq
            cols = lax.broadcasted_iota(jnp.int32, (tq, tk), 1) + j * tk
            dqs, rsums = [], []
            for sub in range(2):
                hm = (lane >= 64) if sub else (lane < 64)
                qh = jnp.where(hm, qv, jnp.zeros_like(qv))
                doh = jnp.where(hm, dov, jnp.zeros_like(dov))
                s = _fox_scores(qh, kv, cb_ref[sub], crow_ref[pl.ds(2 * p_ + sub, 1), :], rows, cols, scale, tk)
                p = jnp.exp(s - jnp.tile(lse_ref[sub], (1, tk // LANES)))
                dp = _dot(doh, vv, NT)
                ds = p * (dp - jnp.tile(dl_ref[sub], (1, tk // LANES)))
                dsb = ds.astype(BF16)
                dv_s[...] += _dot(p.astype(BF16), doh, TN)
                dk_s[...] += _dot(dsb, qh, TN) * scale
                dqs.append(_dot(dsb, kv) * scale)
                dc_s[sub:sub + 1, :] -= jnp.sum(ds, axis=0, keepdims=True)
                rsums.append(jnp.sum(ds, axis=1, keepdims=True))
            qrows = pl.ds(pl.multiple_of(i * tq, tq), tq)
            dq_ref[qrows, :] += jnp.where(lane < 64, dqs[0], dqs[1])
            dcq_ref[qrows, :] += jnp.where(lane < 64, rsums[0], rsums[1])

        @pl.when(i == nq - 1)
        def _():
            dk_ref[...] = dk_s[...]
            dv_ref[...] = dv_s[...]
            dc_ref[...] = dc_s[...]

    qspec = pl.BlockSpec((tq, LANES), lambda p, j, i: (jnp.maximum(i, j), p))
    kspec = pl.BlockSpec((tk, LANES), lambda p, j, i: (j, p))
    stat = pl.BlockSpec((2, tq, LANES), lambda p, j, i: (p, jnp.maximum(i, j), 0))
    return pl.pallas_call(
        body, name="fox_bwd", grid=(4, nq, nq),
        in_specs=[qspec, kspec, kspec, qspec, stat, stat, stat, pl.BlockSpec((8, tk), lambda p, j, i: (0, j))],
        out_specs=[pl.BlockSpec((T, LANES), lambda p, j, i: (0, p)), kspec, kspec,
                   pl.BlockSpec((None, 8, tk), lambda p, j, i: (p, 0, j)), pl.BlockSpec((T, LANES), lambda p, j, i: (0, p))],
        out_shape=[jax.ShapeDtypeStruct((T, 512), F32)] * 3 + [jax.ShapeDtypeStruct((4, 8, T), F32),
                                                                jax.ShapeDtypeStruct((T, 512), F32)],
        scratch_shapes=[pltpu.VMEM((tk, LANES), F32), pltpu.VMEM((tk, LANES), F32), pltpu.VMEM((8, tk), F32)],
        compiler_params=_cparams("arbitrary", "arbitrary", "arbitrary"),
    )(q, k, v, do, lse, delta, cb, crow)


SWA_SUB = 4
SWA_TB = SWA_SUB * WINDOW


def _t5_bucket_matrix():
    t = jnp.arange(WINDOW)[:, None] + WINDOW
    s = jnp.arange(2 * WINDOW)[None, :]
    max_exact = REL_BUCKETS // 2
    d = jnp.maximum(t - s, 0)
    df = jnp.maximum(d, 1).astype(F32)
    large = max_exact + (jnp.log(df / max_exact) / math.log(REL_MAX_DIST / max_exact)
                         * (REL_BUCKETS - max_exact)).astype(jnp.int32)
    large = jnp.minimum(large, REL_BUCKETS - 1)
    return jnp.where(d < max_exact, d, large).astype(jnp.int32)


def _swa_bias(rel_bias, bucket):
    def body(rel_ref, bucket_ref, o_ref):
        b = bucket_ref[...]
        for h in range(SWA_HEADS):
            acc = jnp.zeros(b.shape, F32)
            for r in range(REL_BUCKETS):
                acc = jnp.where(b == r, rel_ref[r, h], acc)
            o_ref[h] = acc

    return pl.pallas_call(
        body, name="swa_bias",
        in_specs=[pl.BlockSpec(memory_space=pltpu.SMEM), pl.BlockSpec(memory_space=pltpu.VMEM)],
        out_specs=pl.BlockSpec(memory_space=pltpu.VMEM),
        out_shape=jax.ShapeDtypeStruct((SWA_HEADS, WINDOW, 2 * WINDOW), F32),
    )(rel_bias, bucket)


def _swa_bias_bwd(dbias, bucket):
    def body(db_ref, bucket_ref, o_ref):
        b = bucket_ref[...]
        lane = _lane((1, LANES))
        for r in range(REL_BUCKETS):
            row = jnp.zeros((1, LANES), F32)
            for h in range(SWA_HEADS):
                part = jnp.sum(jnp.where(b == r, db_ref[h], 0.0), axis=0, keepdims=True)
                tot = jnp.sum(part, axis=1, keepdims=True)
                row = jnp.where(lane == h, tot, row)
            o_ref[r:r + 1, :] = row

    return pl.pallas_call(
        body, name="swa_bias_bwd",
        in_specs=[pl.BlockSpec(memory_space=pltpu.VMEM), pl.BlockSpec(memory_space=pltpu.VMEM)],
        out_specs=pl.BlockSpec(memory_space=pltpu.VMEM),
        out_shape=jax.ShapeDtypeStruct((REL_BUCKETS, LANES), F32),
    )(dbias, bucket)


def _swa_valid(r, i):
    t = lax.broadcasted_iota(jnp.int32, (WINDOW, 2 * WINDOW), 0) + WINDOW
    s = lax.broadcasted_iota(jnp.int32, (WINDOW, 2 * WINDOW), 1)
    dist = t - s
    band = (dist >= 0) & (dist < WINDOW)
    if r == 0:
        band = band & ((s >= WINDOW) | (i > 0))
    return band


def _swa_fwd(sinks, q, kad, vad, bias, T):
    nb = T // SWA_TB
    scale = SWA_HEAD_DIM ** -0.5
    W = WINDOW

    def body(sink_ref, q_ref, k_ref, kp_ref, v_ref, vp_ref, bias_ref, o_ref, lse_ref):
        p_, i = pl.program_id(0), pl.program_id(1)
        lane = _lane((W, LANES))
        for r in range(SWA_SUB):
            rs = slice(r * W, (r + 1) * W)
            ps = slice((r - 1) * W, r * W)
            qr = q_ref[rs, :]
            k_own, v_own = k_ref[rs, :], v_ref[rs, :]
            k_prev = kp_ref[...] if r == 0 else k_ref[ps, :]
            v_prev = vp_ref[...] if r == 0 else v_ref[ps, :]
            valid = _swa_valid(r, i)
            outs = []
            for sub in range(2):
                hm = (lane >= 64) if sub else (lane < 64)
                qh = jnp.where(hm, qr, jnp.zeros_like(qr))
                s = jnp.concatenate([_dot(qh, k_prev, NT), _dot(qh, k_own, NT)], axis=1) * scale + bias_ref[sub]
                s = jnp.where(valid, s, NEG)
                sink = sink_ref[2 * p_ + sub]
                m = jnp.maximum(jnp.max(s, axis=1, keepdims=True), sink)
                p = jnp.exp(s - m)
                denom = jnp.sum(p, axis=1, keepdims=True) + jnp.exp(sink - m)
                pn = (p / denom).astype(BF16)
                outs.append(_dot(pn[:, :W], v_prev) + _dot(pn[:, W:], v_own))
                lse_ref[sub, rs, :] = jnp.broadcast_to(m + jnp.log(denom), (W, LANES))
            o_ref[rs, :] = jnp.where(lane < 64, outs[0], outs[1]).astype(o_ref.dtype)

    qspec = pl.BlockSpec((SWA_TB, LANES), lambda p, i: (i, p))
    own = pl.BlockSpec((None, SWA_TB, LANES), lambda p, i: (p // 2, i, 0))
    prev = pl.BlockSpec((None, W, LANES), lambda p, i: (p // 2, jnp.maximum(SWA_SUB * i - 1, 0), 0))
    stat = pl.BlockSpec((2, SWA_TB, LANES), lambda p, i: (p, i, 0))
    return pl.pallas_call(
        body, name="swa_fwd", grid=(4, nb),
        in_specs=[pl.BlockSpec(memory_space=pltpu.SMEM), qspec, own, prev, own, prev,
                  pl.BlockSpec((2, W, 2 * W), lambda p, i: (p, 0, 0))],
        out_specs=[qspec, stat],
        out_shape=[jax.ShapeDtypeStruct((T, 512), BF16), jax.ShapeDtypeStruct((SWA_HEADS, T, LANES), F32)],
        compiler_params=_cparams("parallel", "parallel"),
    )(sinks, q, kad, kad, vad, vad, bias)


def _swa_bwd(sinks, q, kad, vad, bias, do, lse, delta, T):
    nb = T // SWA_TB
    scale = SWA_HEAD_DIM ** -0.5
    W = WINDOW

    def body(sink_ref, q_ref, k_ref, kp_ref, v_ref, vp_ref, bias_ref, do_ref, lse_ref, dl_ref,
             dq_ref, dkad_ref, dvad_ref, dbias_ref, dsk_ref):
        p_, i = pl.program_id(0), pl.program_id(1)
        kvh = p_ // 2
        lane = _lane((W, LANES))

        @pl.when((p_ == 0) & (i == 0))
        def _():
            dkad_ref[...] = jnp.zeros_like(dkad_ref)
            dvad_ref[...] = jnp.zeros_like(dvad_ref)

        @pl.when(i == 0)
        def _():
            dbias_ref[...] = jnp.zeros_like(dbias_ref)
            dsk_ref[...] = jnp.zeros_like(dsk_ref)

        for r in range(SWA_SUB):
            rs = slice(r * W, (r + 1) * W)
            ps = slice((r - 1) * W, r * W)
            qr, dor = q_ref[rs, :], do_ref[rs, :]
            k_own, v_own = k_ref[rs, :], v_ref[rs, :]
            k_prev = kp_ref[...] if r == 0 else k_ref[ps, :]
            v_prev = vp_ref[...] if r == 0 else v_ref[ps, :]
            valid = _swa_valid(r, i)
            own_row = pl.multiple_of(i * SWA_TB + r * W, W)
            dqs = []
            dk_own = jnp.zeros((W, LANES), F32)
            dk_prev = jnp.zeros((W, LANES), F32)
            dv_own = jnp.zeros((W, LANES), F32)
            dv_prev = jnp.zeros((W, LANES), F32)
            for sub in range(2):
                hm = (lane >= 64) if sub else (lane < 64)
                qh = jnp.where(hm, qr, jnp.zeros_like(qr))
                doh = jnp.where(hm, dor, jnp.zeros_like(dor))
                s = jnp.concatenate([_dot(qh, k_prev, NT), _dot(qh, k_own, NT)], axis=1) * scale + bias_ref[sub]
                s = jnp.where(valid, s, NEG)
                lse_b = lse_ref[sub, rs, :]
                dl_b = dl_ref[sub, rs, :]
                p = jnp.exp(s - jnp.tile(lse_b, (1, 2)))
                dp = jnp.concatenate([_dot(doh, v_prev, NT), _dot(doh, v_own, NT)], axis=1)
                ds = p * (dp - jnp.tile(dl_b, (1, 2)))
                dbias_ref[sub] += ds
                sink = sink_ref[2 * p_ + sub]
                dsk_ref[sub:sub + 1, :] += jnp.sum(jnp.exp(sink - lse_b) * dl_b, axis=0, keepdims=True)
                dsb = ds.astype(BF16)
                pb = p.astype(BF16)
                dqs.append((_dot(dsb[:, :W], k_prev) + _dot(dsb[:, W:], k_own)) * scale)
                dk_prev += _dot(dsb[:, :W], qh, TN) * scale
                dk_own += _dot(dsb[:, W:], qh, TN) * scale
                dv_prev += _dot(pb[:, :W], doh, TN)
                dv_own += _dot(pb[:, W:], doh, TN)
            dq_ref[rs, :] = jnp.where(lane < 64, dqs[0], dqs[1])
            dkad_ref[kvh, pl.ds(own_row, W), :] += dk_own
            dvad_ref[kvh, pl.ds(own_row, W), :] += dv_own
            if r == 0:
                @pl.when(i > 0)
                def _():
                    prev_row = pl.multiple_of(i * SWA_TB - W, W)
                    dkad_ref[kvh, pl.ds(prev_row, W), :] += dk_prev
                    dvad_ref[kvh, pl.ds(prev_row, W), :] += dv_prev
            else:
                prev_row = pl.multiple_of(i * SWA_TB + (r - 1) * W, W)
                dkad_ref[kvh, pl.ds(prev_row, W), :] += dk_prev
                dvad_ref[kvh, pl.ds(prev_row, W), :] += dv_prev

    qspec = pl.BlockSpec((SWA_TB, LANES), lambda p, i: (i, p))
    own = pl.BlockSpec((None, SWA_TB, LANES), lambda p, i: (p // 2, i, 0))
    prev = pl.BlockSpec((None, W, LANES), lambda p, i: (p // 2, jnp.maximum(SWA_SUB * i - 1, 0), 0))
    stat = pl.BlockSpec((2, SWA_TB, LANES), lambda p, i: (p, i, 0))
    full = pl.BlockSpec((2, T, LANES), lambda p, i: (0, 0, 0))
    return pl.pallas_call(
        body, name="swa_bwd", grid=(4, nb),
        in_specs=[pl.BlockSpec(memory_space=pltpu.SMEM), qspec, own, prev, own, prev,
                  pl.BlockSpec((2, W, 2 * W), lambda p, i: (p, 0, 0)), qspec, stat, stat],
        out_specs=[qspec, full, full, pl.BlockSpec((2, W, 2 * W), lambda p, i: (p, 0, 0)),
                   pl.BlockSpec((None, 8, LANES), lambda p, i: (p, 0, 0))],
        out_shape=[jax.ShapeDtypeStruct((T, 512), F32), jax.ShapeDtypeStruct((2, T, LANES), F32),
                   jax.ShapeDtypeStruct((2, T, LANES), F32), jax.ShapeDtypeStruct((SWA_HEADS, W, 2 * W), F32),
                   jax.ShapeDtypeStruct((4, 8, LANES), F32)],
        compiler_params=_cparams("arbitrary", "arbitrary"),
    )(sinks, q, kad, kad, vad, vad, bias, do, lse, delta)


def _mem_fwd(q, mk, mv, T, tq):
    scale = MEM_HEAD_DIM ** -0.5

    def body(q_ref, k_ref, v_ref, o_ref, lse_ref):
        s = _dot(q_ref[...], k_ref[...], NT) * scale
        m = jnp.max(s, axis=1, keepdims=True)
        p = jnp.exp(s - m)
        l = jnp.sum(p, axis=1, keepdims=True)
        o_ref[...] = _dot((p / l).astype(BF16), v_ref[...]).astype(o_ref.dtype)
        lse_ref[...] = jnp.broadcast_to(m + jnp.log(l), (tq, LANES))

    qspec = pl.BlockSpec((tq, LANES), lambda h, i: (i, h))
    kspec = pl.BlockSpec((N_MEM, LANES), lambda h, i: (0, h))
    return pl.pallas_call(
        body, name="mem_fwd", grid=(MEM_HEADS, T // tq),
        in_specs=[qspec, kspec, kspec],
        out_specs=[qspec, pl.BlockSpec((None, tq, LANES), lambda h, i: (h, i, 0))],
        out_shape=[jax.ShapeDtypeStruct((T, 512), BF16), jax.ShapeDtypeStruct((MEM_HEADS, T, LANES), F32)],
        compiler_params=_cparams("parallel", "parallel"),
    )(q, mk, mv)


def _mem_bwd(q, mk, mv, do, lse, delta, T, tq):
    scale = MEM_HEAD_DIM ** -0.5
    rep = N_MEM // LANES

    def body(q_ref, k_ref, v_ref, do_ref, lse_ref, dl_ref, dq_ref, dk_ref, dv_ref):
        i = pl.program_id(1)

        @pl.when(i == 0)
        def _():
            dk_ref[...] = jnp.zeros_like(dk_ref)
            dv_ref[...] = jnp.zeros_like(dv_ref)

        qv, dov = q_ref[...], do_ref[...]
        s = _dot(qv, k_ref[...], NT) * scale
        p = jnp.exp(s - jnp.tile(lse_ref[...], (1, rep)))
        dp = _dot(dov, v_ref[...], NT)
        ds = p * (dp - jnp.tile(dl_ref[...], (1, rep)))
        dsb = ds.astype(BF16)
        dq_ref[...] = _dot(dsb, k_ref[...]) * scale
        dk_ref[...] += _dot(dsb, qv, TN) * scale
        dv_ref[...] += _dot(p.astype(BF16), dov, TN)

    qspec = pl.BlockSpec((tq, LANES), lambda h, i: (i, h))
    kspec = pl.BlockSpec((N_MEM, LANES), lambda h, i: (0, h))
    stat = pl.BlockSpec((None, tq, LANES), lambda h, i: (h, i, 0))
    return pl.pallas_call(
        body, name="mem_bwd", grid=(MEM_HEADS, T // tq),
        in_specs=[qspec, kspec, kspec, qspec, stat, stat],
        out_specs=[qspec, kspec, kspec],
        out_shape=[jax.ShapeDtypeStruct((T, 512), F32), jax.ShapeDtypeStruct((N_MEM, 512), F32),
                   jax.ShapeDtypeStruct((N_MEM, 512), F32)],
        compiler_params=_cparams("arbitrary", "arbitrary"),
    )(q, mk, mv, do, lse, delta)


def _mem_prep_fwd(mem, g_mem, w_kv, kn_gain, gm128):
    def body(mem_ref, g_ref, w_ref, kn_ref, gm_ref, memn_o, kv_o, mk_o, mv_o):
        xhat, _ = _rms_rows(mem_ref[...], None)
        memn = (xhat * g_ref[...]).astype(BF16)
        memn_o[...] = memn
        kv = _dot(memn, w_ref[...])
        kv_o[...] = kv
        gm = gm_ref[...]
        for c in range(4):
            sl = slice(c * LANES, (c + 1) * LANES)
            y, _ = _head_norm(kv[:, sl], gm, kn_ref[...])
            mk_o[:, sl] = y.astype(BF16)
        mv_o[...] = kv[:, 512:].astype(BF16)

    vm = pl.BlockSpec(memory_space=pltpu.VMEM)
    return pl.pallas_call(
        body, name="mem_prep_fwd", in_specs=[vm] * 5, out_specs=[vm] * 4,
        out_shape=[jax.ShapeDtypeStruct((N_MEM, D_MODEL), BF16), jax.ShapeDtypeStruct((N_MEM, D_MODEL), F32),
                   jax.ShapeDtypeStruct((N_MEM, 512), BF16), jax.ShapeDtypeStruct((N_MEM, 512), BF16)],
        compiler_params=pltpu.CompilerParams(vmem_limit_bytes=VMEM_LIMIT),
    )(mem, g_mem, w_kv, kn_gain, gm128)


def _mem_prep_bwd(mem, g_mem, memn, kv, w_kv, kn_gain, gm128, dmk, dmv):
    def body(mem_ref, g_ref, memn_ref, kv_ref, w_ref, kn_ref, gm_ref, dmk_ref, dmv_ref, dw_o, dg_o, dkn_o, dkv_s):
        gm = gm_ref[...]
        dkn = jnp.zeros((1, LANES), F32)
        for c in range(4):
            sl = slice(c * LANES, (c + 1) * LANES)
            dx, dg = _head_norm_bwd(dmk_ref[:, sl], kv_ref[:, sl], gm, kn_ref[...])
            dkv_s[:, sl] = dx.astype(BF16)
            dkn = dkn + dg
        dkn_o[...] = dkn
        dkv_s[:, 512:] = dmv_ref[...].astype(BF16)
        dkv = dkv_s[...]
        dw_o[...] = _dot(memn_ref[...], dkv, TN)
        dmemn = _dot(dkv, w_ref[...], NT)
        xhat, _ = _rms_rows(mem_ref[...], None)
        dg_o[...] = jnp.sum(dmemn * xhat, axis=0, keepdims=True)

    vm = pl.BlockSpec(memory_space=pltpu.VMEM)
    return pl.pallas_call(
        body, name="mem_prep_bwd", in_specs=[vm] * 9, out_specs=[vm] * 3,
        out_shape=[jax.ShapeDtypeStruct((D_MODEL, D_MODEL), F32), jax.ShapeDtypeStruct((1, D_MODEL), F32),
                   jax.ShapeDtypeStruct((1, LANES), F32)],
        scratch_shapes=[pltpu.VMEM((N_MEM, D_MODEL), BF16)],
        compiler_params=pltpu.CompilerParams(vmem_limit_bytes=VMEM_LIMIT),
    )(mem, g_mem, memn, kv, w_kv, kn_gain, gm128, dmk, dmv)


SLOT_O = D_MODEL // N_SHARD


def _merge_fwd(proj, b_gate, o3, w3, T, tb):
    def body(gl_ref, bg_ref, oa_ref, of_ref, om_ref, wa_ref, wf_ref, wm_ref, out_ref):
        o_refs = (oa_ref, of_ref, om_ref)
        w_refs = (wa_ref, wf_ref, wm_ref)
        for n in range(N_SHARD):
            acc = jnp.zeros((tb, SLOT_O), F32)
            for b in range(3):
                c0 = b * D_MODEL + n * SLOT_O
                g = jax.nn.sigmoid(gl_ref[:, c0:c0 + SLOT_O] + bg_ref[:, c0:c0 + SLOT_O])
                acc = acc + g * _dot(o_refs[b][...], w_refs[b][n])
            out_ref[:, n * SLOT_O:(n + 1) * SLOT_O] = acc.astype(out_ref.dtype)

    rows = pl.BlockSpec((tb, 512), lambda i: (i, 0))
    wspec = pl.BlockSpec((N_SHARD, 512, SLOT_O), lambda i: (0, 0, 0))
    return pl.pallas_call(
        body, name="merge_fwd", grid=(T // tb,),
        in_specs=[pl.BlockSpec((tb, GATE_W), lambda i: (i, 1)), pl.BlockSpec((1, GATE_W), lambda i: (0, 0)),
                  rows, rows, rows, wspec, wspec, wspec],
        out_specs=pl.BlockSpec((tb, D_MODEL), lambda i: (i, 0)),
        out_shape=jax.ShapeDtypeStruct((T, D_MODEL), BF16),
        compiler_params=_cparams("parallel"),
    )(proj, b_gate, *o3, *w3)


def _merge_bwd(proj, b_gate, o3, w3, dmerged, T, tb):
    heads = (SWA_HEADS, FOX_HEADS, MEM_HEADS)

    def body(gl_ref, bg_ref, oa_ref, of_ref, om_ref, wa_ref, wf_ref, wm_ref, dm_ref,
             dgl_o, doa_o, dof_o, dom_o, dla_o, dlf_o, dlm_o, dwa_o, dwf_o, dwm_o, dbg_o):
        i = pl.program_id(0)
        o_refs = (oa_ref, of_ref, om_ref)
        w_refs = (wa_ref, wf_ref, wm_ref)
        do_refs = (doa_o, dof_o, dom_o)
        dl_refs = (dla_o, dlf_o, dlm_o)
        dw_refs = (dwa_o, dwf_o, dwm_o)

        @pl.when(i == 0)
        def _():
            for r in dw_refs:
                r[...] = jnp.zeros_like(r)
            dbg_o[...] = jnp.zeros_like(dbg_o)

        lane = _lane((tb, LANES))
        for b in range(3):
            ob = o_refs[b][...]
            do = jnp.zeros((tb, 512), F32)
            for n in range(N_SHARD):
                c0 = b * D_MODEL + n * SLOT_O
                g = jax.nn.sigmoid(gl_ref[:, c0:c0 + SLOT_O] + bg_ref[:, c0:c0 + SLOT_O])
                dm = dm_ref[:, n * SLOT_O:(n + 1) * SLOT_O]
                y = _dot(ob, w_refs[b][n])
                dgl = dm * y * g * (1.0 - g)
                dgl_o[:, c0:c0 + SLOT_O] = dgl.astype(dgl_o.dtype)
                dbg_o[:, c0:c0 + SLOT_O] += jnp.sum(dgl, axis=0, keepdims=True)
                dy = (dm * g).astype(BF16)
                do = do + _dot(dy, w_refs[b][n], NT)
                dw_refs[b][n] += _dot(ob, dy, TN)
            do_refs[b][...] = do.astype(BF16)
            prod = do * ob.astype(F32)
            for c in range(4):
                blk = prod[:, c * LANES:(c + 1) * LANES]
                if heads[b] == 8:
                    lo = jnp.sum(jnp.where(lane < 64, blk, 0.0), axis=1, keepdims=True)
                    hi = jnp.sum(jnp.where(lane >= 64, blk, 0.0), axis=1, keepdims=True)
                    dl_refs[b][2 * c] = jnp.broadcast_to(lo, (tb, LANES))
                    dl_refs[b][2 * c + 1] = jnp.broadcast_to(hi, (tb, LANES))
                else:
                    dl_refs[b][c] = jnp.broadcast_to(jnp.sum(blk, axis=1, keepdims=True), (tb, LANES))

    rows = pl.BlockSpec((tb, 512), lambda i: (i, 0))
    wspec = pl.BlockSpec((N_SHARD, 512, SLOT_O), lambda i: (0, 0, 0))
    stat = lambda h: pl.BlockSpec((h, tb, LANES), lambda i: (0, i, 0))
    return pl.pallas_call(
        body, name="merge_bwd", grid=(T // tb,),
        in_specs=[pl.BlockSpec((tb, GATE_W), lambda i: (i, 1)), pl.BlockSpec((1, GATE_W), lambda i: (0, 0)),
                  rows, rows, rows, wspec, wspec, wspec, pl.BlockSpec((tb, D_MODEL), lambda i: (i, 0))],
        out_specs=[pl.BlockSpec((tb, GATE_W), lambda i: (i, 0)), rows, rows, rows,
                   stat(8), stat(8), stat(4), wspec, wspec, wspec, pl.BlockSpec((1, GATE_W), lambda i: (0, 0))],
        out_shape=[jax.ShapeDtypeStruct((T, GATE_W), BF16)] + [jax.ShapeDtypeStruct((T, 512), BF16)] * 3
        + [jax.ShapeDtypeStruct((8, T, LANES), F32)] * 2 + [jax.ShapeDtypeStruct((4, T, LANES), F32)]
        + [jax.ShapeDtypeStruct((N_SHARD, 512, SLOT_O), F32)] * 3 + [jax.ShapeDtypeStruct((1, GATE_W), F32)],
        compiler_params=_cparams("arbitrary"),
    )(proj, b_gate, *o3, *w3, dmerged)


def _local_step(x, mem, tgt, small, wc, w_kv, w_o3, w_out, w_up, w_down):
    T = x.shape[0]
    tm = min(512, T)
    tile2 = lambda v: jnp.tile(v.reshape(1, -1), (1, LANES // v.size))
    gains = jnp.concatenate([tile2(small["qn_swa"]), tile2(small["kn_swa"]), tile2(small["qn_fox"]),
                             tile2(small["kn_fox"]), tile2(small["qn_mem"]), jnp.zeros((3, LANES), F32)], axis=0)
    kn_mem = small["kn_mem"].reshape(1, LANES)
    bfor = jnp.pad(small["b_forget"].reshape(1, -1), ((0, 0), (0, LANES - FOX_HEADS)))
    gm64 = _group_mean_matrix(64)
    gm128 = _group_mean_matrix(128)
    tb_prep = min(256, T)
    ones = jnp.ones((tb_prep, tb_prep), F32)
    tril = jnp.tril(ones).astype(BF16)
    triu = jnp.triu(ones).astype(BF16)
    bucket = _t5_bucket_matrix()
    g_mix, g_mlp, g_mem = small["g_mix"], small["g_mlp"], small["g_mem"]
    b_gate = small["b_gate"]
    sinks = small["sink_swa"].reshape(-1)

    h = _rmsnorm("rms_mix", x, g_mix, tm)
    (proj,) = _matmul(
        "mm_proj", h, wc, dims=NN, grid=(T // tm, PROJ_W // 512, 1),
        a_spec=pl.BlockSpec((tm, D_MODEL), lambda i, j, k: (i, 0)),
        b_spec=pl.BlockSpec((D_MODEL, 512), lambda i, j, k: (0, j)),
        acc_shape=(tm, 512),
        outs=[(jax.ShapeDtypeStruct((T, PROJ_W), F32), pl.BlockSpec((tm, 512), lambda i, j, k: (i, j)))],
        epilogue=_epi_store)
    qa, qf, kf, vf, qm, kad, vad, cb, crow = _prep_fwd(proj, gains, bfor, tril, gm64, gm128, T, tb_prep)
    bias = _swa_bias(small["rel_bias"], bucket)
    o_swa, lse_swa = _swa_fwd(sinks, qa, kad, vad, bias, T)
    o_fox, lse_fox = _fox_fwd(qf, kf, vf, cb, crow, T, tm)
    memn, kv, mk, mv = _mem_prep_fwd(mem, g_mem, w_kv, kn_mem, gm128)
    o_mem, lse_mem = _mem_fwd(qm, mk, mv, T, tm)
    o3 = (o_swa, o_fox, o_mem)
    merged = _merge_fwd(proj, b_gate, o3, w_o3, T, min(256, T))

    def epi_residual(acc_ref, extra_refs, out_refs, ij):
        out_refs[0][...] = extra_refs[0][...] + acc_ref[...]

    row_full = pl.BlockSpec((tm, D_MODEL), lambda i, j, k: (i, 0))
    (x2,) = _matmul(
        "mm_out", merged, w_out, dims=NN, grid=(T // tm, 1, 1),
        a_spec=row_full, b_spec=pl.BlockSpec((D_MODEL, D_MODEL), lambda i, j, k: (0, 0)),
        acc_shape=(tm, D_MODEL), extra=[(x, row_full)],
        outs=[(jax.ShapeDtypeStruct((T, D_MODEL), F32), row_full)], epilogue=epi_residual)
    hm = _rmsnorm("rms_mlp", x2, g_mlp, tm)

    def epi_relu2(acc_ref, extra_refs, out_refs, ij):
        up = acc_ref[...]
        out_refs[0][...] = up
        r = jnp.maximum(up, 0.0)
        out_refs[1][...] = (r * r).astype(BF16)

    up_blk = pl.BlockSpec((tm, 512), lambda i, j, k: (i, j))
    up, u = _matmul(
        "mm_up", hm, w_up, dims=NN, grid=(T // tm, D_FF // 512, 1),
        a_spec=row_full, b_spec=pl.BlockSpec((None, D_MODEL, 512), lambda i, j, k: (j // 2, 0, j % 2)),
        acc_shape=(tm, 512),
        outs=[(jax.ShapeDtypeStruct((T, D_FF), F32), up_blk), (jax.ShapeDtypeStruct((T, D_FF), BF16), up_blk)],
        epilogue=epi_relu2)

    def epi_loss(acc_ref, extra_refs, out_refs, ij):
        y = extra_refs[0][...] + acc_ref[...]
        err = y - extra_refs[1][...]
        out_refs[0][...] = err * (1.0 / D_MODEL)
        sq = jnp.sum(jnp.sum(err * err, axis=1, keepdims=True), axis=0, keepdims=True)

        @pl.when(ij[0] == 0)
        def _():
            out_refs[1][...] = jnp.zeros_like(out_refs[1])

        out_refs[1][...] += jnp.broadcast_to(sq, out_refs[1].shape)

    dy, loss_acc = _matmul(
        "mm_down", u, w_down, dims=NN, grid=(T // tm, 1, D_FF // 512),
        a_spec=pl.BlockSpec((tm, 512), lambda i, j, k: (i, k)),
        b_spec=pl.BlockSpec((512, D_MODEL), lambda i, j, k: (k, 0)),
        acc_shape=(tm, D_MODEL), extra=[(x2, row_full), (tgt, row_full)],
        outs=[(jax.ShapeDtypeStruct((T, D_MODEL), F32), row_full),
              (jax.ShapeDtypeStruct((8, LANES), F32), pl.BlockSpec((8, LANES), lambda i, j, k: (0, 0)))],
        epilogue=epi_loss)
    loss = loss_acc[0, 0] * (0.5 / D_MODEL)

    def epi_dup(acc_ref, extra_refs, out_refs, ij):
        out_refs[0][...] = (acc_ref[...] * (2.0 * jnp.maximum(extra_refs[0][...], 0.0))).astype(BF16)

    (dup,) = _matmul(
        "mm_dup", dy, w_down, dims=NT, grid=(T // tm, D_FF // 512, 1),
        a_spec=row_full, b_spec=pl.BlockSpec((512, D_MODEL), lambda i, j, k: (j, 0)),
        acc_shape=(tm, 512), extra=[(up, up_blk)],
        outs=[(jax.ShapeDtypeStruct((T, D_FF), BF16), up_blk)], epilogue=epi_dup)

    nkt = T // tm
    (d_w_down,) = _matmul(
        "mm_dw_down", u, dy, dims=TN, grid=(D_FF // 512, 1, nkt),
        a_spec=pl.BlockSpec((tm, 512), lambda i, j, k: (k, i)),
        b_spec=pl.BlockSpec((tm, D_MODEL), lambda i, j, k: (k, 0)),
        acc_shape=(512, D_MODEL),
        outs=[(jax.ShapeDtypeStruct((D_FF, D_MODEL), F32), pl.BlockSpec((512, D_MODEL), lambda i, j, k: (i, 0)))],
        epilogue=_epi_store)
    (d_w_up,) = _matmul(
        "mm_dw_up", hm, dup, dims=TN, grid=(D_MODEL // 512, D_FF // 512, nkt),
        a_spec=pl.BlockSpec((tm, 512), lambda i, j, k: (k, i)),
        b_spec=pl.BlockSpec((tm, 512), lambda i, j, k: (k, j)),
        acc_shape=(512, 512),
        outs=[(jax.ShapeDtypeStruct((N_SHARD, D_MODEL, D_MODEL), F32),
               pl.BlockSpec((None, 512, 512), lambda i, j, k: (j // 2, i, j % 2)))],
        epilogue=_epi_store)

    def make_epi_rms_bwd(with_residual):
        def epi(acc_ref, extra_refs, out_refs, ij):
            dx, dg = _rmsnorm_bwd_rows(acc_ref[...], extra_refs[0][...], extra_refs[1][...])
            if with_residual:
                dx = dx + extra_refs[2][...]
            if len(extra_refs) > 3:
                dx = dx + extra_refs[3][...]
            out_refs[0][...] = dx

            @pl.when(ij[0] == 0)
            def _():
                out_refs[1][...] = jnp.zeros_like(out_refs[1])

            out_refs[1][...] += dg
        return epi

    gain_spec = pl.BlockSpec((1, D_MODEL), lambda i, j, k: (0, 0))
    dx2, d_g_mlp = _matmul(
        "mm_dhm", dup, w_up, dims=NT, grid=(T // tm, 1, D_FF // 512),
        a_spec=pl.BlockSpec((tm, 512), lambda i, j, k: (i, k)),
        b_spec=pl.BlockSpec((None, D_MODEL, 512), lambda i, j, k: (k // 2, 0, k % 2)),
        acc_shape=(tm, D_MODEL), extra=[(x2, row_full), (g_mlp, gain_spec), (dy, row_full)],
        outs=[(jax.ShapeDtypeStruct((T, D_MODEL), F32), row_full), (jax.ShapeDtypeStruct((1, D_MODEL), F32), gain_spec)],
        epilogue=make_epi_rms_bwd(True))

    (dmerged,) = _matmul(
        "mm_dmerged", dx2, w_out, dims=NT, grid=(T // tm, 1, 1),
        a_spec=row_full, b_spec=pl.BlockSpec((D_MODEL, D_MODEL), lambda i, j, k: (0, 0)),
        acc_shape=(tm, D_MODEL), outs=[(jax.ShapeDtypeStruct((T, D_MODEL), F32), row_full)], epilogue=_epi_store)
    (d_w_out,) = _matmul(
        "mm_dw_out", merged, dx2, dims=TN, grid=(D_MODEL // 512, 1, nkt),
        a_spec=pl.BlockSpec((tm, 512), lambda i, j, k: (k, i)),
        b_spec=pl.BlockSpec((tm, D_MODEL), lambda i, j, k: (k, 0)),
        acc_shape=(512, D_MODEL),
        outs=[(jax.ShapeDtypeStruct((D_MODEL, D_MODEL), F32), pl.BlockSpec((512, D_MODEL), lambda i, j, k: (i, 0)))],
        epilogue=_epi_store)
    (dgl, do_swa, do_fox, do_mem, dl_swa, dl_fox, dl_mem, d_wo_swa, d_wo_fox, d_wo_mem, d_b_gate) = _merge_bwd(
        proj, b_gate, o3, w_o3, dmerged, T, min(256, T))

    dqa, dkad, dvad, dbias, dsk = _swa_bwd(sinks, qa, kad, vad, bias, do_swa, lse_swa, dl_swa, T)
    dqf, dkf, dvf, dcrow, dcq = _fox_bwd(qf, kf, vf, do_fox, lse_fox, dl_fox, cb, crow, T, tm)
    dqm, dmk, dmv = _mem_bwd(qm, mk, mv, do_mem, lse_mem, dl_mem, T, tm)
    d_w_kv, d_g_mem, d_kn_mem = _mem_prep_bwd(mem, g_mem, memn, kv, w_kv, kn_mem, gm128, dmk, dmv)
    d_rel = _swa_bias_bwd(dbias, bucket)
    dc_keys = dcrow[:, :2, :].reshape(FOX_HEADS, T).T
    dc_queries = dcq.reshape(T, FOX_HEADS, FOX_HEAD_DIM)[:, :, 0]
    dccol = jnp.pad(dc_keys + dc_queries, ((0, 0), (0, LANES - FOX_HEADS)))
    dlo, gacc = _prep_bwd(proj, dqa, dkad, dvad, dqf, dkf, dvf, dqm, dccol, gains, bfor, triu, gm64, gm128, T, tb_prep)

    def dwc_half(name, dpart):
        (res,) = _matmul(
            name, h, dpart, dims=TN, grid=(D_MODEL // 512, LO_W // 512, nkt),
            a_spec=pl.BlockSpec((tm, 512), lambda i, j, k: (k, i)),
            b_spec=pl.BlockSpec((tm, 512), lambda i, j, k: (k, j)),
            acc_shape=(512, 512),
            outs=[(jax.ShapeDtypeStruct((D_MODEL, LO_W), F32), pl.BlockSpec((512, 512), lambda i, j, k: (i, j)))],
            epilogue=_epi_store)
        return res

    d_wc_lo = dwc_half("mm_dwc_lo", dlo)
    d_wc_gl = dwc_half("mm_dwc_gl", dgl)
    (dh_lo,) = _matmul(
        "mm_dh_lo", dlo, wc, dims=NT, grid=(T // tm, 1, LO_W // 512),
        a_spec=pl.BlockSpec((tm, 512), lambda i, j, k: (i, k)),
        b_spec=pl.BlockSpec((D_MODEL, 512), lambda i, j, k: (0, k)),
        acc_shape=(tm, D_MODEL), outs=[(jax.ShapeDtypeStruct((T, D_MODEL), F32), row_full)], epilogue=_epi_store)

    def epi_dx(acc_ref, extra_refs, out_refs, ij):
        dhh = acc_ref[...] + extra_refs[3][...]
        dx, dg = _rmsnorm_bwd_rows(dhh, extra_refs[0][...], extra_refs[1][...])
        out_refs[0][...] = dx + extra_refs[2][...]

        @pl.when(ij[0] == 0)
        def _():
            out_refs[1][...] = jnp.zeros_like(out_refs[1])

        out_refs[1][...] += dg

    grad_x, d_g_mix = _matmul(
        "mm_dh_gl", dgl, wc, dims=NT, grid=(T // tm, 1, GATE_W // 512),
        a_spec=pl.BlockSpec((tm, 512), lambda i, j, k: (i, k)),
        b_spec=pl.BlockSpec((D_MODEL, 512), lambda i, j, k: (0, k + LO_W // 512)),
        acc_shape=(tm, D_MODEL), extra=[(x, row_full), (g_mix, gain_spec), (dx2, row_full), (dh_lo, row_full)],
        outs=[(jax.ShapeDtypeStruct((T, D_MODEL), F32), row_full), (jax.ShapeDtypeStruct((1, D_MODEL), F32), gain_spec)],
        epilogue=epi_dx)

    fold64 = lambda row: (row[:64] + row[64:]).reshape(1, 64)
    grads = {
        "g_mix": d_g_mix, "b_gate": d_b_gate, "b_forget": gacc[5, :FOX_HEADS].reshape(1, FOX_HEADS),
        "qn_swa": fold64(gacc[0]), "kn_swa": fold64(gacc[1]),
        "sink_swa": -dsk[:, :2, 0].reshape(1, SWA_HEADS), "rel_bias": d_rel[:, :SWA_HEADS],
        "qn_fox": fold64(gacc[2]), "kn_fox": fold64(gacc[3]),
        "g_mem": d_g_mem, "qn_mem": gacc[4].reshape(1, LANES), "kn_mem": d_kn_mem, "g_mlp": d_g_mlp,
        "wc_lo": d_wc_lo, "wc_gl": d_wc_gl, "w_mem_kv": d_w_kv,
        "w_o_swa": d_wo_swa, "w_o_fox": d_wo_fox, "w_o_mem": d_wo_mem,
        "w_out": d_w_out, "w_mlp_up": d_w_up, "w_mlp_down": d_w_down,
    }
    return loss, grad_x, grads


MESH = pl.DeviceIdType.MESH
ANY = pl.BlockSpec(memory_space=pl.ANY)


def _place():
    x, y, c = lax.axis_index("x"), lax.axis_index("y"), lax.axis_index("c")
    chips = [(1 - x, y), (x, 1 - y), (1 - x, 1 - y)]
    return x, y, c, chips


def _all_gather_shards(shards):
    n = len(shards)

    def body(*refs):
        src, out = refs[:n], refs[n:2 * n]
        ici_send, ici_recv, d2d_send, d2d_recv, local_sem = refs[2 * n:]
        x, y, c, chips = _place()
        sibling = (x, y, 1 - c)
        me = 2 * x + y

        def half(a, who):
            hr = shards[a].shape[0] // 2
            return pl.ds(pl.multiple_of(who * hr, hr), hr)

        local = [pltpu.make_async_copy(src[a], out[a].at[me], local_sem.at[a]) for a in range(n)]
        for cp in local:
            cp.start()

        def ici(a, j, slot, to):
            return pltpu.make_async_remote_copy(
                src_ref=src[a].at[half(a, c)], dst_ref=out[a].at[slot, half(a, c)],
                send_sem=ici_send.at[3 * a + j], recv_sem=ici_recv.at[3 * a + j], device_id=to, device_id_type=MESH)

        def d2d(a, j, slot, which):
            part = out[a].at[slot, half(a, which)]
            return pltpu.make_async_remote_copy(
                src_ref=part, dst_ref=part, send_sem=d2d_send.at[3 * a + j], recv_sem=d2d_recv.at[3 * a + j],
                device_id=sibling, device_id_type=MESH)

        sends = [ici(a, j, me, (*chip, c)) for a in range(n) for j, chip in enumerate(chips)]
        for cp in sends:
            cp.start()
        passed = []
        for a in range(n):
            for j, (px, py) in enumerate(chips):
                ici(a, j, 2 * px + py, (px, py, c)).wait_recv()
                cp = d2d(a, j, 2 * px + py, c)
                cp.start()
                passed.append(cp)
        for a in range(n):
            for j, (px, py) in enumerate(chips):
                d2d(a, j, 2 * px + py, 1 - c).wait_recv()
        for cp in sends + passed:
            cp.wait_send()
        for cp in local:
            cp.wait()

    return pl.pallas_call(
        body, name="all_gather_weights",
        in_specs=[ANY] * n, out_specs=[ANY] * n,
        out_shape=[jax.ShapeDtypeStruct((N_SHARD,) + s.shape, s.dtype) for s in shards],
        scratch_shapes=[pltpu.SemaphoreType.DMA((3 * n,))] * 4 + [pltpu.SemaphoreType.DMA((n,))],
    )(*shards)


def _pair_exchange(gs):
    n = len(gs)

    def body(*refs):
        src, stage = refs[:n], refs[n:2 * n]
        send_sem, recv_sem = refs[2 * n:]
        x, y, c, _ = _place()
        copies = []
        for a in range(n):
            hr = gs[a].shape[1] // 2
            theirs = pl.ds(pl.multiple_of((1 - c) * hr, hr), hr)
            copies.append(pltpu.make_async_remote_copy(
                src_ref=src[a].at[:, theirs, :], dst_ref=stage[a], send_sem=send_sem.at[a], recv_sem=recv_sem.at[a],
                device_id=(x, y, 1 - c), device_id_type=MESH))
        for cp in copies:
            cp.start()
        for cp in copies:
            cp.wait()

    return pl.pallas_call(
        body, name="pair_exchange", in_specs=[ANY] * n, out_specs=[ANY] * n,
        out_shape=[jax.ShapeDtypeStruct((N_SHARD, g.shape[1] // 2, g.shape[2]), g.dtype) for g in gs],
        scratch_shapes=[pltpu.SemaphoreType.DMA((n,))] * 2,
    )(*gs)


def _chip_exchange(sums):
    n = len(sums)

    def body(*refs):
        src, got = refs[:n], refs[n:2 * n]
        send_sem, recv_sem = refs[2 * n:]
        x, y, c, chips = _place()
        copies = []
        for a in range(n):
            for j, (px, py) in enumerate(chips):
                copies.append(pltpu.make_async_remote_copy(
                    src_ref=src[a].at[2 * px + py], dst_ref=got[a].at[j],
                    send_sem=send_sem.at[3 * a + j], recv_sem=recv_sem.at[3 * a + j],
                    device_id=(px, py, c), device_id_type=MESH))
        for cp in copies:
            cp.start()
        for cp in copies:
            cp.wait()

    return pl.pallas_call(
        body, name="chip_exchange", in_specs=[ANY] * n, out_specs=[ANY] * n,
        out_shape=[jax.ShapeDtypeStruct((3,) + s.shape[1:], s.dtype) for s in sums],
        scratch_shapes=[pltpu.SemaphoreType.DMA((3 * n,))] * 2,
    )(*sums)


def _pair_gather(halves):
    n = len(halves)

    def body(*refs):
        src, full = refs[:n], refs[n:2 * n]
        send_sem, recv_sem, local_sem = refs[2 * n:]
        x, y, c, _ = _place()
        copies, local = [], []
        for a in range(n):
            hr = halves[a].shape[0]
            mine = full[a].at[pl.ds(pl.multiple_of(c * hr, hr), hr)]
            local.append(pltpu.make_async_copy(src[a], mine, local_sem.at[a]))
            copies.append(pltpu.make_async_remote_copy(
                src_ref=src[a], dst_ref=mine, send_sem=send_sem.at[a], recv_sem=recv_sem.at[a],
                device_id=(x, y, 1 - c), device_id_type=MESH))
        for cp in local + copies:
            cp.start()
        for cp in copies + local:
            cp.wait()

    return pl.pallas_call(
        body, name="pair_gather", in_specs=[ANY] * n, out_specs=[ANY] * n,
        out_shape=[jax.ShapeDtypeStruct((2 * h.shape[0], h.shape[1]), h.dtype) for h in halves],
        scratch_shapes=[pltpu.SemaphoreType.DMA((n,))] * 3,
    )(*halves)


def _row_block(rows):
    return min(rows, 128)


def _pair_sum(name, place, g, stage):
    _, R, C = g.shape
    hr = R // 2
    rb = _row_block(hr)
    nb = hr // rb

    def body(place_ref, g_ref, st_ref, sum_bf, own_f32):
        s = pl.program_id(1)
        tot = g_ref[...] + st_ref[...]
        sum_bf[...] = tot.astype(BF16)

        @pl.when(s == place_ref[0])
        def _():
            own_f32[...] = tot

    return pl.pallas_call(
        body, name=name,
        grid_spec=pltpu.PrefetchScalarGridSpec(
            num_scalar_prefetch=1, grid=(nb, N_SHARD),
            in_specs=[pl.BlockSpec((None, rb, C), lambda i, s, pr: (s, pr[1] * nb + i, 0)),
                      pl.BlockSpec((None, rb, C), lambda i, s, pr: (s, i, 0))],
            out_specs=[pl.BlockSpec((None, rb, C), lambda i, s, pr: (s, i, 0)),
                       pl.BlockSpec((rb, C), lambda i, s, pr: (i, 0))]),
        out_shape=[jax.ShapeDtypeStruct((N_SHARD, hr, C), BF16), jax.ShapeDtypeStruct((hr, C), F32)],
        compiler_params=_cparams("arbitrary", "arbitrary"),
    )(place, g, stage)


def _final_sum(name, own, got):
    hr, C = own.shape
    rb = _row_block(hr)

    def body(own_ref, got_ref, o_ref):
        o_ref[...] = ((own_ref[...] + got_ref[0].astype(F32)) + got_ref[1].astype(F32)) + got_ref[2].astype(F32)

    return pl.pallas_call(
        body, name=name, grid=(hr // rb,),
        in_specs=[pl.BlockSpec((rb, C), lambda i: (i, 0)), pl.BlockSpec((3, rb, C), lambda i: (0, i, 0))],
        out_specs=pl.BlockSpec((rb, C), lambda i: (i, 0)),
        out_shape=jax.ShapeDtypeStruct((hr, C), F32),
        compiler_params=_cparams("parallel"),
    )(own, got)


def _adamw_math(w, g, m, v):
    m = ADAM_B1 * m + (1.0 - ADAM_B1) * g
    v = ADAM_B2 * v + (1.0 - ADAM_B2) * (g * g)
    m_hat = m / (1.0 - ADAM_B1 ** ADAM_STEP)
    v_hat = v / (1.0 - ADAM_B2 ** ADAM_STEP)
    delta = -ADAM_LR * (m_hat / (jnp.sqrt(v_hat) + ADAM_EPS) + ADAM_WD * w)
    return delta, m, v


def _adamw(name, w, g, m, v):
    R, Cw = w.shape
    Cg = g.shape[1]
    rb = _row_block(R)

    def body(w_ref, g_ref, m_ref, v_ref, g_o, d_o, m_o, v_o):
        gv = g_ref[...]
        delta, mn, vn = _adamw_math(w_ref[...], gv, m_ref[...], v_ref[...])
        g_o[...] = gv
        d_o[...] = delta
        m_o[...] = mn
        v_o[...] = vn

    blk = pl.BlockSpec((rb, Cg), lambda i: (i, 0))
    return pl.pallas_call(
        body, name=name, grid=(R // rb,),
        in_specs=[blk] * 4, out_specs=[blk] * 4,
        out_shape=[jax.ShapeDtypeStruct((R, Cw), F32)] * 4,
        compiler_params=_cparams("parallel"),
    )(w, g, m, v)


N_DEV = 8
SMALL_ROWS = 64


def _small_allreduce_adamw(g, w, m, v):
    def body(g_ref, w_ref, m_ref, v_ref, all_ref, gs_o, d_o, m_o, v_o, send_sems, recv_sems, local_sem):
        x, y, c, chips = _place()
        me, sibling = (x, y, c), (x, y, 1 - c)

        def rows(px, py, pc):
            return all_ref.at[pl.ds(pl.multiple_of((4 * px + 2 * py + pc) * SMALL_ROWS, SMALL_ROWS), SMALL_ROWS), :]

        def copy(k, block, to, src=None):
            return pltpu.make_async_remote_copy(
                src_ref=rows(*block) if src is None else src, dst_ref=rows(*block),
                send_sem=send_sems.at[k], recv_sem=recv_sems.at[k], device_id=to, device_id_type=MESH)

        mine = pltpu.make_async_copy(g_ref, rows(*me), local_sem)
        mine.start()
        first = [copy(0, me, sibling, src=g_ref)]
        first += [copy(1 + j, me, (*chip, c), src=g_ref) for j, chip in enumerate(chips)]
        for cp in first:
            cp.start()
        passed = [copy(4 + j, (*chip, c), sibling) for j, chip in enumerate(chips)]
        for j, chip in enumerate(chips):
            copy(1 + j, (*chip, c), me).wait_recv()
            passed[j].start()
        copy(0, sibling, me).wait_recv()
        for j, chip in enumerate(chips):
            copy(4 + j, (*chip, 1 - c), me).wait_recv()
        for cp in first + passed:
            cp.wait_send()
        mine.wait()

        tot = all_ref[0:SMALL_ROWS, :]
        for d in range(1, N_DEV):
            tot = tot + all_ref[d * SMALL_ROWS:(d + 1) * SMALL_ROWS, :]
        delta, mn, vn = _adamw_math(w_ref[...], tot, m_ref[...], v_ref[...])
        gs_o[...] = tot
        d_o[...] = delta
        m_o[...] = mn
        v_o[...] = vn

    vm = pl.BlockSpec(memory_space=pltpu.VMEM)
    shp = jax.ShapeDtypeStruct((SMALL_ROWS, LANES), F32)
    res = pl.pallas_call(
        body, name="small_allreduce_adamw", in_specs=[vm] * 4, out_specs=[vm] * 5,
        out_shape=[jax.ShapeDtypeStruct((N_DEV * SMALL_ROWS, LANES), F32), shp, shp, shp, shp],
        scratch_shapes=[pltpu.SemaphoreType.DMA((7,)), pltpu.SemaphoreType.DMA((7,)), pltpu.SemaphoreType.DMA],
    )(g, w, m, v)
    return res[1:]


SMALL_NAMES = ("g_mix", "b_gate", "b_forget", "qn_swa", "kn_swa", "sink_swa", "rel_bias", "qn_fox", "kn_fox",
               "g_mem", "qn_mem", "kn_mem", "g_mlp")
BIG_NAMES = ("w_in", "w_mem_kv", "w_o_swa", "w_o_fox", "w_o_mem", "w_out", "w_mlp_up", "w_mlp_down")
WEIGHT_NAMES = ("g_mix", "w_in", "b_gate", "b_forget", "qn_swa", "kn_swa", "sink_swa", "rel_bias", "qn_fox", "kn_fox",
                "g_mem", "w_mem_kv", "qn_mem", "kn_mem", "w_o_swa", "w_o_fox", "w_o_mem", "w_out", "g_mlp",
                "w_mlp_up", "w_mlp_down")


def _pack_small(parts, extra=None):
    rows = []
    for n in SMALL_NAMES:
        flat = parts[n].reshape(-1).astype(F32)
        flat = jnp.pad(flat, (0, (-flat.size) % LANES))
        rows.append(flat.reshape(-1, LANES))
    if extra is not None:
        rows.append(jnp.pad(extra.reshape(1, 1), ((0, 0), (0, LANES - 1))))
    packed = jnp.concatenate(rows, axis=0)
    return jnp.pad(packed, ((0, SMALL_ROWS - packed.shape[0]), (0, 0)))


def _unpack_small(packed, shapes):
    out, r = {}, 0
    for n in SMALL_NAMES:
        size = math.prod(shapes[n])
        nr = -(-size // LANES)
        out[n] = packed[r:r + nr].reshape(-1)[:size].reshape(shapes[n])
        r += nr
    return out, packed[r, 0]


def _reorder_w_in(w_full):
    seg = lambda a, b: w_full[:, a:b]
    pad = jnp.zeros((w_full.shape[0], C_GL - C_FL - FOX_HEADS), w_full.dtype)
    return jnp.concatenate([seg(0, 512), seg(768, 1280), seg(1280, 1792), seg(1792, 2304), seg(2312, 2824),
                            seg(512, 640), seg(640, 768), seg(2304, 2312), pad, seg(2824, IN_WIDTH)], axis=1)


def _restore_w_in(lo, gl):
    s = lambda a, b: lo[:, a:b]
    return jnp.concatenate([s(C_QA, C_QA + 512), s(C_KA, C_KA + 128), s(C_VA, C_VA + 128), s(C_QF, C_QF + 512),
                            s(C_KF, C_KF + 512), s(C_VF, C_VF + 512), s(C_FL, C_FL + FOX_HEADS), s(C_QM, C_QM + 512),
                            gl], axis=1)


def kernel(x, mem, g_mix, w_in, b_gate, b_forget, qn_swa, kn_swa, sink_swa, rel_bias, qn_fox, kn_fox, g_mem, w_mem_kv, qn_mem, kn_mem, w_o_swa, w_o_fox, w_o_mem, w_out, g_mlp, w_mlp_up, w_mlp_down, loss_target, m_g_mix, m_w_in, m_b_gate, m_b_forget, m_qn_swa, m_kn_swa, m_sink_swa, m_rel_bias, m_qn_fox, m_kn_fox, m_g_mem, m_w_mem_kv, m_qn_mem, m_kn_mem, m_w_o_swa, m_w_o_fox, m_w_o_mem, m_w_out, m_g_mlp, m_w_mlp_up, m_w_mlp_down, v_g_mix, v_w_in, v_b_gate, v_b_forget, v_qn_swa, v_kn_swa, v_sink_swa, v_rel_bias, v_qn_fox, v_kn_fox, v_g_mem, v_w_mem_kv, v_qn_mem, v_kn_mem, v_w_o_swa, v_w_o_fox, v_w_o_mem, v_w_out, v_g_mlp, v_w_mlp_up, v_w_mlp_down):
    given = dict(locals())
    W = {n: given[n] for n in WEIGHT_NAMES}
    M = {n: given["m_" + n] for n in WEIGHT_NAMES}
    V = {n: given["v_" + n] for n in WEIGHT_NAMES}
    pad_in = ((0, 0), (0, IN_SHARD_PAD - IN_SHARD))

    shards = [jnp.pad(w_in[0].astype(BF16), pad_in)] + [W[n][0].astype(BF16) for n in BIG_NAMES[1:]]
    g_in, g_kv, g_oa, g_of, g_om, g_out, g_up, g_down = _all_gather_shards(shards)
    w_full = jnp.concatenate([g_in[s, :, :IN_SHARD] for s in range(N_SHARD)], axis=1)
    wc = _reorder_w_in(w_full)
    small = {n: (W[n] if n == "rel_bias" else W[n].reshape(1, -1)) for n in SMALL_NAMES}

    loss, grad_x, grads = _local_step(
        x[0], mem[0], loss_target[0], small, wc, g_kv.reshape(D_MODEL, D_MODEL), (g_oa, g_of, g_om),
        g_out.reshape(D_MODEL, D_MODEL), g_up, g_down.reshape(D_FF, D_MODEL))

    d_full = _restore_w_in(grads["wc_lo"], grads["wc_gl"])
    d_in = jnp.stack([jnp.pad(d_full[:, s * IN_SHARD:(s + 1) * IN_SHARD], pad_in) for s in range(N_SHARD)])
    slot_rows = lambda a: a.reshape(N_SHARD, a.shape[0] // N_SHARD, a.shape[1])
    local = [d_in, slot_rows(grads["w_mem_kv"]), grads["w_o_swa"], grads["w_o_fox"], grads["w_o_mem"],
             slot_rows(grads["w_out"]), grads["w_mlp_up"], slot_rows(grads["w_mlp_down"])]
    place = jnp.stack([2 * lax.axis_index("x") + lax.axis_index("y"), lax.axis_index("c")]).astype(jnp.int32)
    staged = _pair_exchange(local)
    sums = [_pair_sum("pair_sum_" + n, place, g, st) for n, g, st in zip(BIG_NAMES, local, staged)]
    got = _chip_exchange([s[0] for s in sums])
    halves = [_final_sum("final_sum_" + n, s[1], r) for n, s, r in zip(BIG_NAMES, sums, got)]
    summed = _pair_gather(halves)

    out = {}
    for n, g in zip(BIG_NAMES, summed):
        res = _adamw("adamw_" + n, W[n][0], g, M[n][0], V[n][0])
        out[n] = [r.reshape(W[n].shape) for r in res]
    shapes = {n: W[n].shape for n in SMALL_NAMES}
    packed = _small_allreduce_adamw(_pack_small(grads, loss), _pack_small(W), _pack_small(M), _pack_small(V))
    unpacked = [_unpack_small(p, shapes) for p in packed]
    for n in SMALL_NAMES:
        out[n] = [u[0][n] for u in unpacked]
    loss_total = unpacked[0][1]

    return (loss_total, grad_x.reshape(x.shape),
            *[out[n][0] for n in WEIGHT_NAMES], *[out[n][1] for n in WEIGHT_NAMES],
            *[out[n][2] for n in WEIGHT_NAMES], *[out[n][3] for n in WEIGHT_NAMES])
```

```python
import functools
import math

import jax
import jax.numpy as jnp
from jax import lax
from jax.experimental import pallas as pl
from jax.experimental.pallas import tpu as pltpu

F32 = jnp.float32
BF16 = jnp.bfloat16

D_MODEL = 1024
N_MEM = 256
SWA_HEADS = 8
SWA_KV_HEADS = 2
SWA_HEAD_DIM = 64
WINDOW = 128
FOX_HEADS = 8
FOX_HEAD_DIM = 64
MEM_HEADS = 4
MEM_HEAD_DIM = 128
D_FF = 4 * D_MODEL
REL_BUCKETS = 32
REL_MAX_DIST = 128
EPS = 1e-6
NEG = -1e30
GATE_W = 3 * D_MODEL
IN_WIDTH = 5896
N_SHARD = 4
IN_SHARD = IN_WIDTH // N_SHARD
IN_SHARD_PAD = 1536

ADAM_LR = 0.001
ADAM_B1 = 0.9
ADAM_B2 = 0.999
ADAM_EPS = 1e-08
ADAM_WD = 0.01
ADAM_STEP = 10

LANES = 128
V7X_VMEM_BYTES = 64 * 1024 * 1024
VMEM_LIMIT = V7X_VMEM_BYTES * 3 // 4

C_QA, C_QF, C_KF, C_VF, C_QM, C_KA, C_VA, C_FL, C_GL = 0, 512, 1024, 1536, 2048, 2560, 2688, 2816, 3072
LO_W = 3072
PROJ_W = 6144

NN = (((1,), (0,)), ((), ()))
NT = (((1,), (1,)), ((), ()))
TN = (((0,), (0,)), ((), ()))


def _dot(a, b, dims=NN):
    return lax.dot_general(a, b, dims, preferred_element_type=F32)


def _cparams(*sem):
    return pltpu.CompilerParams(dimension_semantics=sem, vmem_limit_bytes=VMEM_LIMIT)


def _split3(a):
    hi = a.astype(BF16)
    r1 = a - hi.astype(F32)
    mid = r1.astype(BF16)
    lo = (r1 - mid.astype(F32)).astype(BF16)
    return hi, mid, lo


def _dot3_right(a, g):
    hi, mid, lo = _split3(a)
    return _dot(hi, g) + _dot(mid, g) + _dot(lo, g)


def _dot3_left(g, a):
    hi, mid, lo = _split3(a)
    return _dot(g, hi) + _dot(g, mid) + _dot(g, lo)


def _group_mean_matrix(d):
    r = jnp.arange(LANES)
    return jnp.where((r[:, None] // d) == (r[None, :] // d), 1.0 / d, 0.0).astype(BF16)


def _lane(shape):
    return lax.broadcasted_iota(jnp.int32, shape, len(shape) - 1)


def _matmul(name, a, b, *, dims, grid, a_spec, b_spec, acc_shape, outs, epilogue, extra=()):
    nk = grid[2]
    n_extra = len(extra)

    def body(a_ref, b_ref, *rest):
        extra_refs = rest[:n_extra]
        out_refs = rest[n_extra:n_extra + len(outs)]
        i, j, k = pl.program_id(0), pl.program_id(1), pl.program_id(2)
        part = _dot(a_ref[...].astype(BF16), b_ref[...].astype(BF16), dims)
        if nk == 1:
            epilogue(part, extra_refs, out_refs, (i, j))
            return
        acc_ref = rest[-1]

        @pl.when(k == 0)
        def _():
            acc_ref[...] = part

        @pl.when((k > 0) & (k < nk - 1))
        def _():
            acc_ref[...] += part

        @pl.when(k == nk - 1)
        def _():
            epilogue(acc_ref[...] + part, extra_refs, out_refs, (i, j))

    res = pl.pallas_call(
        body,
        name=name,
        grid=grid,
        in_specs=[a_spec, b_spec] + [s for _, s in extra],
        out_specs=[s for _, s in outs],
        out_shape=[s for s, _ in outs],
        scratch_shapes=[pltpu.VMEM(acc_shape, F32)] if nk > 1 else [],
        compiler_params=_cparams("arbitrary", "arbitrary", "arbitrary"),
    )(a, b, *[x for x, _ in extra])
    return res


def _epi_store(acc, extra_refs, out_refs, ij):
    out_refs[0][...] = acc.astype(out_refs[0].dtype)


def _rms_rows(x, g):
    r = lax.rsqrt(jnp.mean(x * x, axis=-1, keepdims=True) + EPS)
    return x * r, r


def _rmsnorm_bwd_rows(dh, x, g):
    xhat, r = _rms_rows(x, g)
    dxh = dh * g
    dx = r * (dxh - xhat * jnp.mean(dxh * xhat, axis=-1, keepdims=True))
    return dx, jnp.sum(dh * xhat, axis=0, keepdims=True)


def _rmsnorm(name, x, g, tb):
    T, Dm = x.shape

    def body(x_ref, g_ref, o_ref):
        xhat, _ = _rms_rows(x_ref[...], None)
        o_ref[...] = (xhat * g_ref[...]).astype(o_ref.dtype)

    return pl.pallas_call(
        body, name=name, grid=(T // tb,),
        in_specs=[pl.BlockSpec((tb, Dm), lambda i: (i, 0)), pl.BlockSpec((1, Dm), lambda i: (0, 0))],
        out_specs=pl.BlockSpec((tb, Dm), lambda i: (i, 0)),
        out_shape=jax.ShapeDtypeStruct((T, Dm), BF16),
        compiler_params=_cparams("parallel"),
    )(x, g)


def _head_norm(x, gm, gain):
    ms = _dot3_right(x * x, gm)
    r = lax.rsqrt(ms + EPS)
    return x * r * gain, x * r


def _head_norm_bwd(dy, x, gm, gain):
    ms = _dot3_right(x * x, gm)
    r = lax.rsqrt(ms + EPS)
    xhat = x * r
    dxh = dy * gain
    dx = r * (dxh - xhat * _dot3_right(dxh * xhat, gm))
    return dx, jnp.sum(dy * xhat, axis=0, keepdims=True)


def _log_sigmoid(z):
    return jnp.minimum(z, 0.0) - jnp.log(1.0 + jnp.exp(-jnp.abs(z)))


def _prep_fwd(proj, gains, bfor, tril, gm64, gm128, T, tb):
    nb = T // tb

    def body(qa_ref, qf_ref, kf_ref, vf_ref, qm_ref, ka_ref, va_ref, fl_ref, gains_ref, bfor_ref, tril_ref,
             gm64_ref, gm128_ref,
             qa_o, qf_o, kf_o, vf_o, qm_o, kad_o, vad_o, cb_o, crow_o, carry):
        i = pl.program_id(0)
        gm64v = gm64_ref[...]
        gm128v = gm128_ref[...]
        lane = _lane((tb, LANES))

        def norm512(src, dst, row, gm):
            gain = gains_ref[row:row + 1, :]
            for c in range(4):
                sl = slice(c * LANES, (c + 1) * LANES)
                y, _ = _head_norm(src[:, sl], gm, gain)
                dst[:, sl] = y.astype(dst.dtype)

        norm512(qa_ref, qa_o, 0, gm64v)
        norm512(qf_ref, qf_o, 2, gm64v)
        norm512(kf_ref, kf_o, 3, gm64v)
        norm512(qm_ref, qm_o, 4, gm128v)
        vf_o[...] = vf_ref[...].astype(vf_o.dtype)

        ka_n, _ = _head_norm(ka_ref[...], gm64v, gains_ref[1:2, :])
        ka_r = pltpu.roll(ka_n, 64, 1)
        va = va_ref[...]
        va_r = pltpu.roll(va, 64, 1)
        lo = lane < 64
        kad_o[0] = jnp.where(lo, ka_n, ka_r).astype(kad_o.dtype)
        kad_o[1] = jnp.where(lo, ka_r, ka_n).astype(kad_o.dtype)
        vad_o[0] = jnp.where(lo, va, va_r).astype(vad_o.dtype)
        vad_o[1] = jnp.where(lo, va_r, va).astype(vad_o.dtype)

        @pl.when(i == 0)
        def _():
            carry[...] = jnp.zeros_like(carry)

        logf = jnp.where(lane < FOX_HEADS, _log_sigmoid(fl_ref[...] + bfor_ref[...]), 0.0)
        c = _dot3_left(tril_ref[...], logf) + carry[0:1, :]
        carry[...] = jnp.broadcast_to(c[tb - 1:tb, :], carry.shape)
        for h in range(FOX_HEADS):
            col = jnp.sum(jnp.where(lane == h, c, 0.0), axis=1, keepdims=True)
            cb_o[h] = jnp.broadcast_to(col, (tb, LANES))
        crow_o[...] = c.T[0:8, :]

    def seg(width, start):
        return pl.BlockSpec((tb, width), lambda i, s=start // width: (i, s))

    const = lambda shape: pl.BlockSpec(shape, lambda i: tuple(0 for _ in shape))
    rows512 = pl.BlockSpec((tb, 512), lambda i: (i, 0))
    outs = pl.pallas_call(
        body, name="prep_fwd", grid=(nb,),
        in_specs=[seg(512, C_QA), seg(512, C_QF), seg(512, C_KF), seg(512, C_VF), seg(512, C_QM),
                  seg(128, C_KA), seg(128, C_VA), seg(128, C_FL),
                  const((8, LANES)), const((1, LANES)), const((tb, tb)), const((LANES, LANES)), const((LANES, LANES))],
        out_specs=[rows512, rows512, rows512, rows512, rows512,
                   pl.BlockSpec((2, tb, LANES), lambda i: (0, i, 0)), pl.BlockSpec((2, tb, LANES), lambda i: (0, i, 0)),
                   pl.BlockSpec((FOX_HEADS, tb, LANES), lambda i: (0, i, 0)), pl.BlockSpec((8, tb), lambda i: (0, i))],
        out_shape=[jax.ShapeDtypeStruct((T, 512), BF16)] * 5
        + [jax.ShapeDtypeStruct((2, T, LANES), BF16)] * 2
        + [jax.ShapeDtypeStruct((FOX_HEADS, T, LANES), F32), jax.ShapeDtypeStruct((8, T), F32)],
        scratch_shapes=[pltpu.VMEM((8, LANES), F32)],
        compiler_params=_cparams("arbitrary"),
    )(proj, proj, proj, proj, proj, proj, proj, proj, gains, bfor, tril, gm64, gm128)
    return outs


def _prep_bwd(proj, dqa, dkad, dvad, dqf, dkf, dvf, dqm, dccol, gains, bfor, triu, gm64, gm128, T, tb):
    nb = T // tb

    def body(qa_ref, qf_ref, kf_ref, qm_ref, ka_ref, fl_ref,
             dqa_ref, dkad_ref, dvad_ref, dqf_ref, dkf_ref, dvf_ref, dqm_ref, dc_ref,
             gains_ref, bfor_ref, triu_ref, gm64_ref, gm128_ref,
             dlo_o, gacc_o, carry):
        i = pl.program_id(0)
        gm64v = gm64_ref[...]
        gm128v = gm128_ref[...]
        lane = _lane((tb, LANES))

        @pl.when(i == 0)
        def _():
            carry[...] = jnp.zeros_like(carry)
            gacc_o[...] = jnp.zeros_like(gacc_o)

        def norm512_bwd(dsrc, xsrc, col0, row, gm):
            gain = gains_ref[row:row + 1, :]
            gsum = jnp.zeros((1, LANES), F32)
            for c in range(4):
                sl = slice(c * LANES, (c + 1) * LANES)
                dx, dg = _head_norm_bwd(dsrc[:, sl], xsrc[:, sl], gm, gain)
                dlo_o[:, col0 + c * LANES:col0 + (c + 1) * LANES] = dx.astype(dlo_o.dtype)
                gsum = gsum + dg
            gacc_o[row:row + 1, :] += gsum

        norm512_bwd(dqa_ref, qa_ref, C_QA, 0, gm64v)
        norm512_bwd(dqf_ref, qf_ref, C_QF, 2, gm64v)
        norm512_bwd(dkf_ref, kf_ref, C_KF, 3, gm64v)
        norm512_bwd(dqm_ref, qm_ref, C_QM, 4, gm128v)
        dlo_o[:, C_VF:C_VF + 512] = dvf_ref[...].astype(dlo_o.dtype)

        lo = lane < 64

        def fold(ref):
            f0 = ref[0] + pltpu.roll(ref[0], 64, 1)
            f1 = ref[1] + pltpu.roll(ref[1], 64, 1)
            return jnp.where(lo, f0, f1)

        dka, dg = _head_norm_bwd(fold(dkad_ref), ka_ref[...], gm64v, gains_ref[1:2, :])
        gacc_o[1:2, :] += dg
        dlo_o[:, C_KA:C_KA + LANES] = dka.astype(dlo_o.dtype)
        dlo_o[:, C_VA:C_VA + LANES] = fold(dvad_ref).astype(dlo_o.dtype)

        dc = dc_ref[...]
        dlogf = _dot3_left(triu_ref[...], dc) + carry[0:1, :]
        carry[...] = jnp.broadcast_to(dlogf[0:1, :], carry.shape)
        z = fl_ref[...] + bfor_ref[...]
        dfl = jnp.where(lane < FOX_HEADS, dlogf / (1.0 + jnp.exp(z)), 0.0)
        gacc_o[5:6, :] += jnp.sum(dfl, axis=0, keepdims=True)
        dlo_o[:, C_FL:C_FL + LANES] = dfl.astype(dlo_o.dtype)
        dlo_o[:, C_FL + LANES:C_FL + 2 * LANES] = jnp.zeros((tb, LANES), dlo_o.dtype)

    rev = lambda i: nb - 1 - i

    def seg(width, start):
        return pl.BlockSpec((tb, width), lambda i, s=start // width: (rev(i), s))

    const = lambda shape: pl.BlockSpec(shape, lambda i: tuple(0 for _ in shape))
    rows512 = pl.BlockSpec((tb, 512), lambda i: (rev(i), 0))
    dup = pl.BlockSpec((2, tb, LANES), lambda i: (0, rev(i), 0))
    return pl.pallas_call(
        body, name="prep_bwd", grid=(nb,),
        in_specs=[seg(512, C_QA), seg(512, C_QF), seg(512, C_KF), seg(512, C_QM), seg(128, C_KA), seg(128, C_FL),
                  rows512, dup, dup, rows512, rows512, rows512, rows512,
                  pl.BlockSpec((tb, LANES), lambda i: (rev(i), 0)),
                  const((8, LANES)), const((1, LANES)), const((tb, tb)), const((LANES, LANES)), const((LANES, LANES))],
        out_specs=[pl.BlockSpec((tb, LO_W), lambda i: (rev(i), 0)), const((8, LANES))],
        out_shape=[jax.ShapeDtypeStruct((T, LO_W), BF16), jax.ShapeDtypeStruct((8, LANES), F32)],
        scratch_shapes=[pltpu.VMEM((8, LANES), F32)],
        compiler_params=_cparams("arbitrary"),
    )(proj, proj, proj, proj, proj, proj, dqa, dkad, dvad, dqf, dkf, dvf, dqm, dccol, gains, bfor, triu, gm64, gm128)


def _fox_scores(qh, kv, cb, crow_row, rows, cols, scale, tk):
    s = _dot(qh, kv, NT) * scale
    s = s + jnp.tile(cb, (1, tk // LANES)) - crow_row
    return jnp.where(cols <= rows, s, NEG)


def _fox_fwd(q, k, v, cb, crow, T, tq):
    nq = T // tq
    tk = tq
    scale = FOX_HEAD_DIM ** -0.5

    def body(q_ref, k_ref, v_ref, cb_ref, crow_ref, o_ref, lse_ref, m_s, l_s, acc_s):
        p_, i, j = pl.program_id(0), pl.program_id(1), pl.program_id(2)

        @pl.when(j == 0)
        def _():
            m_s[...] = jnp.full(m_s.shape, NEG, F32)
            l_s[...] = jnp.zeros_like(l_s)
            acc_s[...] = jnp.zeros_like(acc_s)

        @pl.when(j <= i)
        def _():
            lane = _lane((tq, LANES))
            qv, kv, vv = q_ref[...], k_ref[...], v_ref[...]
            rows = lax.broadcasted_iota(jnp.int32, (tq, tk), 0) + i * tq
            cols = lax.broadcasted_iota(jnp.int32, (tq, tk), 1) + j * tk
            for sub in range(2):
                hm = (lane >= 64) if sub else (lane < 64)
                qh = jnp.where(hm, qv, jnp.zeros_like(qv))
                s = _fox_scores(qh, kv, cb_ref[sub], crow_ref[pl.ds(2 * p_ + sub, 1), :], rows, cols, scale, tk)
                m_prev, l_prev = m_s[sub], l_s[sub]
                m_next = jnp.maximum(m_prev, jnp.max(s, axis=1, keepdims=True))
                p = jnp.exp(s - jnp.tile(m_next, (1, tk // LANES)))
                alpha = jnp.exp(m_prev - m_next)
                l_s[sub] = alpha * l_prev + jnp.sum(p, axis=1, keepdims=True)
                m_s[sub] = m_next
                acc_s[sub] = acc_s[sub] * alpha + _dot(p.astype(BF16), vv)

        @pl.when(j == nq - 1)
        def _():
            lane = _lane((tq, LANES))
            o_ref[...] = jnp.where(lane < 64, acc_s[0] / l_s[0], acc_s[1] / l_s[1]).astype(o_ref.dtype)
            lse_ref[0] = m_s[0] + jnp.log(l_s[0])
            lse_ref[1] = m_s[1] + jnp.log(l_s[1])

    qspec = pl.BlockSpec((tq, LANES), lambda p, i, j: (i, p))
    kspec = pl.BlockSpec((tk, LANES), lambda p, i, j: (jnp.minimum(j, i), p))
    stat = pl.BlockSpec((2, tq, LANES), lambda p, i, j: (p, i, 0))
    return pl.pallas_call(
        body, name="fox_fwd", grid=(4, nq, nq),
        in_specs=[qspec, kspec, kspec, stat, pl.BlockSpec((8, tk), lambda p, i, j: (0, jnp.minimum(j, i)))],
        out_specs=[qspec, stat],
        out_shape=[jax.ShapeDtypeStruct((T, 512), BF16), jax.ShapeDtypeStruct((FOX_HEADS, T, LANES), F32)],
        scratch_shapes=[pltpu.VMEM((2, tq, LANES), F32)] * 3,
        compiler_params=_cparams("parallel", "parallel", "arbitrary"),
    )(q, k, v, cb, crow)


def _fox_bwd(q, k, v, do, lse, delta, cb, crow, T, tq):
    nq = T // tq
    tk = tq
    scale = FOX_HEAD_DIM ** -0.5

    def body(q_ref, k_ref, v_ref, do_ref, lse_ref, dl_ref, cb_ref, crow_ref,
             dq_ref, dk_ref, dv_ref, dc_ref, dcq_ref, dk_s, dv_s, dc_s):
        p_, j, i = pl.program_id(0), pl.program_id(1), pl.program_id(2)

        @pl.when((j == 0) & (i == 0))
        def _():
            dq_ref[...] = jnp.zeros_like(dq_ref)
            dcq_ref[...] = jnp.zeros_like(dcq_ref)

        @pl.when(i == 0)
        def _():
            dk_s[...] = jnp.zeros_like(dk_s)
            dv_s[...] = jnp.zeros_like(dv_s)
            dc_s[...] = jnp.zeros_like(dc_s)

        @pl.when(i >= j)
        def _():
            lane = _lane((tq, LANES))
            qv, kv, vv, dov = q_ref[...], k_ref[...], v_ref[...], do_ref[...]
            rows = lax.broadcasted_iota(jnp.int32, (tq, tk), 0) + i * tq
            cols = lax.broadcasted_iota(jnp.int32, (tq, tk), 1) + j * tk
            dqs, rsums = [], []
            for sub in range(2):
                hm = (lane >= 64) if sub else (lane < 64)
                qh = jnp.where(hm, qv, jnp.zeros_like(qv))
                doh = jnp.where(hm, dov, jnp.zeros_like(dov))
                s = _fox_scores(qh, kv, cb_ref[sub], crow_ref[pl.ds(2 * p_ + sub, 1), :], rows, cols, scale, tk)
                p = jnp.exp(s - jnp.tile(lse_ref[sub], (1, tk // LANES)))
                dp = _dot(doh, vv, NT)
                ds = p * (dp - jnp.tile(dl_ref[sub], (1, tk // LANES)))
                dsb = ds.astype(BF16)
                dv_s[...] += _dot(p.astype(BF16), doh, TN)
                dk_s[...] += _dot(dsb, qh, TN) * scale
                dqs.append(_dot(dsb, kv) * scale)
                dc_s[sub:sub + 1, :] -= jnp.sum(ds, axis=0, keepdims=True)
                rsums.append(jnp.sum(ds, axis=1, keepdims=True))
            qrows = pl.ds(pl.multiple_of(i * tq, tq), tq)
            dq_ref[qrows, :] += jnp.where(lane < 64, dqs[0], dqs[1])
            dcq_ref[qrows, :] += jnp.where(lane < 64, rsums[0], rsums[1])

        @pl.when(i == nq - 1)
        def _():
            dk_ref[...] = dk_s[...]
            dv_ref[...] = dv_s[...]
            dc_ref[...] = dc_s[...]

    qspec = pl.BlockSpec((tq, LANES), lambda p, j, i: (jnp.maximum(i, j), p))
    kspec = pl.BlockSpec((tk, LANES), lambda p, j, i: (j, p))
    stat = pl.BlockSpec((2, tq, LANES), lambda p, j, i: (p, jnp.maximum(i, j), 0))
    return pl.pallas_call(
        body, name="fox_bwd", grid=(4, nq, nq),
        in_specs=[qspec, kspec, kspec, qspec, stat, stat, stat, pl.BlockSpec((8, tk), lambda p, j, i: (0, j))],
        out_specs=[pl.BlockSpec((T, LANES), lambda p, j, i: (0, p)), kspec, kspec,
                   pl.BlockSpec((None, 8, tk), lambda p, j, i: (p, 0, j)), pl.BlockSpec((T, LANES), lambda p, j, i: (0, p))],
        out_shape=[jax.ShapeDtypeStruct((T, 512), F32)] * 3 + [jax.ShapeDtypeStruct((4, 8, T), F32),
                                                                jax.ShapeDtypeStruct((T, 512), F32)],
        scratch_shapes=[pltpu.VMEM((tk, LANES), F32), pltpu.VMEM((tk, LANES), F32), pltpu.VMEM((8, tk), F32)],
        compiler_params=_cparams("arbitrary", "arbitrary", "arbitrary"),
    )(q, k, v, do, lse, delta, cb, crow)


SWA_SUB = 4
SWA_TB = SWA_SUB * WINDOW


def _t5_bucket_matrix():
    t = jnp.arange(WINDOW)[:, None] + WINDOW
    s = jnp.arange(2 * WINDOW)[None, :]
    max_exact = REL_BUCKETS // 2
    d = jnp.maximum(t - s, 0)
    df = jnp.maximum(d, 1).astype(F32)
    large = max_exact + (jnp.log(df / max_exact) / math.log(REL_MAX_DIST / max_exact)
                         * (REL_BUCKETS - max_exact)).astype(jnp.int32)
    large = jnp.minimum(large, REL_BUCKETS - 1)
    return jnp.where(d < max_exact, d, large).astype(jnp.int32)


def _swa_bias(rel_bias, bucket):
    def body(rel_ref, bucket_ref, o_ref):
        b = bucket_ref[...]
        for h in range(SWA_HEADS):
            acc = jnp.zeros(b.shape, F32)
            for r in range(REL_BUCKETS):
                acc = jnp.where(b == r, rel_ref[r, h], acc)
            o_ref[h] = acc

    return pl.pallas_call(
        body, name="swa_bias",
        in_specs=[pl.BlockSpec(memory_space=pltpu.SMEM), pl.BlockSpec(memory_space=pltpu.VMEM)],
        out_specs=pl.BlockSpec(memory_space=pltpu.VMEM),
        out_shape=jax.ShapeDtypeStruct((SWA_HEADS, WINDOW, 2 * WINDOW), F32),
    )(rel_bias, bucket)


def _swa_bias_bwd(dbias, bucket):
    def body(db_ref, bucket_ref, o_ref):
        b = bucket_ref[...]
        lane = _lane((1, LANES))
        for r in range(REL_BUCKETS):
            row = jnp.zeros((1, LANES), F32)
            for h in range(SWA_HEADS):
                part = jnp.sum(jnp.where(b == r, db_ref[h], 0.0), axis=0, keepdims=True)
                tot = jnp.sum(part, axis=1, keepdims=True)
                row = jnp.where(lane == h, tot, row)
            o_ref[r:r + 1, :] = row

    return pl.pallas_call(
        body, name="swa_bias_bwd",
        in_specs=[pl.BlockSpec(memory_space=pltpu.VMEM), pl.BlockSpec(memory_space=pltpu.VMEM)],
        out_specs=pl.BlockSpec(memory_space=pltpu.VMEM),
        out_shape=jax.ShapeDtypeStruct((REL_BUCKETS, LANES), F32),
    )(dbias, bucket)


def _swa_valid(r, i):
    t = lax.broadcasted_iota(jnp.int32, (WINDOW, 2 * WINDOW), 0) + WINDOW
    s = lax.broadcasted_iota(jnp.int32, (WINDOW, 2 * WINDOW), 1)
    dist = t - s
    band = (dist >= 0) & (dist < WINDOW)
    if r == 0:
        band = band & ((s >= WINDOW) | (i > 0))
    return band


def _swa_fwd(sinks, q, kad, vad, bias, T):
    nb = T // SWA_TB
    scale = SWA_HEAD_DIM ** -0.5
    W = WINDOW

    def body(sink_ref, q_ref, k_ref, kp_ref, v_ref, vp_ref, bias_ref, o_ref, lse_ref):
        p_, i = pl.program_id(0), pl.program_id(1)
        lane = _lane((W, LANES))
        for r in range(SWA_SUB):
            rs = slice(r * W, (r + 1) * W)
            ps = slice((r - 1) * W, r * W)
            qr = q_ref[rs, :]
            k_own, v_own = k_ref[rs, :], v_ref[rs, :]
            k_prev = kp_ref[...] if r == 0 else k_ref[ps, :]
            v_prev = vp_ref[...] if r == 0 else v_ref[ps, :]
            valid = _swa_valid(r, i)
            outs = []
            for sub in range(2):
                hm = (lane >= 64) if sub else (lane < 64)
                qh = jnp.where(hm, qr, jnp.zeros_like(qr))
                s = jnp.concatenate([_dot(qh, k_prev, NT), _dot(qh, k_own, NT)], axis=1) * scale + bias_ref[sub]
                s = jnp.where(valid, s, NEG)
                sink = sink_ref[2 * p_ + sub]
                m = jnp.maximum(jnp.max(s, axis=1, keepdims=True), sink)
                p = jnp.exp(s - m)
                denom = jnp.sum(p, axis=1, keepdims=True) + jnp.exp(sink - m)
                pn = (p / denom).astype(BF16)
                outs.append(_dot(pn[:, :W], v_prev) + _dot(pn[:, W:], v_own))
                lse_ref[sub, rs, :] = jnp.broadcast_to(m + jnp.log(denom), (W, LANES))
            o_ref[rs, :] = jnp.where(lane < 64, outs[0], outs[1]).astype(o_ref.dtype)

    qspec = pl.BlockSpec((SWA_TB, LANES), lambda p, i: (i, p))
    own = pl.BlockSpec((None, SWA_TB, LANES), lambda p, i: (p // 2, i, 0))
    prev = pl.BlockSpec((None, W, LANES), lambda p, i: (p // 2, jnp.maximum(SWA_SUB * i - 1, 0), 0))
    stat = pl.BlockSpec((2, SWA_TB, LANES), lambda p, i: (p, i, 0))
    return pl.pallas_call(
        body, name="swa_fwd", grid=(4, nb),
        in_specs=[pl.BlockSpec(memory_space=pltpu.SMEM), qspec, own, prev, own, prev,
                  pl.BlockSpec((2, W, 2 * W), lambda p, i: (p, 0, 0))],
        out_specs=[qspec, stat],
        out_shape=[jax.ShapeDtypeStruct((T, 512), BF16), jax.ShapeDtypeStruct((SWA_HEADS, T, LANES), F32)],
        compiler_params=_cparams("parallel", "parallel"),
    )(sinks, q, kad, kad, vad, vad, bias)


def _swa_bwd(sinks, q, kad, vad, bias, do, lse, delta, T):
    nb = T // SWA_TB
    scale = SWA_HEAD_DIM ** -0.5
    W = WINDOW

    def body(sink_ref, q_ref, k_ref, kp_ref, v_ref, vp_ref, bias_ref, do_ref, lse_ref, dl_ref,
             dq_ref, dkad_ref, dvad_ref, dbias_ref, dsk_ref):
        p_, i = pl.program_id(0), pl.program_id(1)
        kvh = p_ // 2
        lane = _lane((W, LANES))

        @pl.when((p_ == 0) & (i == 0))
        def _():
            dkad_ref[...] = jnp.zeros_like(dkad_ref)
            dvad_ref[...] = jnp.zeros_like(dvad_ref)

        @pl.when(i == 0)
        def _():
            dbias_ref[...] = jnp.zeros_like(dbias_ref)
            dsk_ref[...] = jnp.zeros_like(dsk_ref)

        for r in range(SWA_SUB):
            rs = slice(r * W, (r + 1) * W)
            ps = slice((r - 1) * W, r * W)
            qr, dor = q_ref[rs, :], do_ref[rs, :]
            k_own, v_own = k_ref[rs, :], v_ref[rs, :]
            k_prev = kp_ref[...] if r == 0 else k_ref[ps, :]
            v_prev = vp_ref[...] if r == 0 else v_ref[ps, :]
            valid = _swa_valid(r, i)
            own_row = pl.multiple_of(i * SWA_TB + r * W, W)
            dqs = []
            dk_own = jnp.zeros((W, LANES), F32)
            dk_prev = jnp.zeros((W, LANES), F32)
            dv_own = jnp.zeros((W, LANES), F32)
            dv_prev = jnp.zeros((W, LANES), F32)
            for sub in range(2):
                hm = (lane >= 64) if sub else (lane < 64)
                qh = jnp.where(hm, qr, jnp.zeros_like(qr))
                doh = jnp.where(hm, dor, jnp.zeros_like(dor))
                s = jnp.concatenate([_dot(qh, k_prev, NT), _dot(qh, k_own, NT)], axis=1) * scale + bias_ref[sub]
                s = jnp.where(valid, s, NEG)
                lse_b = lse_ref[sub, rs, :]
                dl_b = dl_ref[sub, rs, :]
                p = jnp.exp(s - jnp.tile(lse_b, (1, 2)))
                dp = jnp.concatenate([_dot(doh, v_prev, NT), _dot(doh, v_own, NT)], axis=1)
                ds = p * (dp - jnp.tile(dl_b, (1, 2)))
                dbias_ref[sub] += ds
                sink = sink_ref[2 * p_ + sub]
                dsk_ref[sub:sub + 1, :] += jnp.sum(jnp.exp(sink - lse_b) * dl_b, axis=0, keepdims=True)
                dsb = ds.astype(BF16)
                pb = p.astype(BF16)
                dqs.append((_dot(dsb[:, :W], k_prev) + _dot(dsb[:, W:], k_own)) * scale)
                dk_prev += _dot(dsb[:, :W], qh, TN) * scale
                dk_own += _dot(dsb[:, W:], qh, TN) * scale
                dv_prev += _dot(pb[:, :W], doh, TN)
                dv_own += _dot(pb[:, W:], doh, TN)
            dq_ref[rs, :] = jnp.where(lane < 64, dqs[0], dqs[1])
            dkad_ref[kvh, pl.ds(own_row, W), :] += dk_own
            dvad_ref[kvh, pl.ds(own_row, W), :] += dv_own
            if r == 0:
                @pl.when(i > 0)
                def _():
                    prev_row = pl.multiple_of(i * SWA_TB - W, W)
                    dkad_ref[kvh, pl.ds(prev_row, W), :] += dk_prev
                    dvad_ref[kvh, pl.ds(prev_row, W), :] += dv_prev
            else:
                prev_row = pl.multiple_of(i * SWA_TB + (r - 1) * W, W)
                dkad_ref[kvh, pl.ds(prev_row, W), :] += dk_prev
                dvad_ref[kvh, pl.ds(prev_row, W), :] += dv_prev

    qspec = pl.BlockSpec((SWA_TB, LANES), lambda p, i: (i, p))
    own = pl.BlockSpec((None, SWA_TB, LANES), lambda p, i: (p // 2, i, 0))
    prev = pl.BlockSpec((None, W, LANES), lambda p, i: (p // 2, jnp.maximum(SWA_SUB * i - 1, 0), 0))
    stat = pl.BlockSpec((2, SWA_TB, LANES), lambda p, i: (p, i, 0))
    full = pl.BlockSpec((2, T, LANES), lambda p, i: (0, 0, 0))
    return pl.pallas_call(
        body, name="swa_bwd", grid=(4, nb),
        in_specs=[pl.BlockSpec(memory_space=pltpu.SMEM), qspec, own, prev, own, prev,
                  pl.BlockSpec((2, W, 2 * W), lambda p, i: (p, 0, 0)), qspec, stat, stat],
        out_specs=[qspec, full, full, pl.BlockSpec((2, W, 2 * W), lambda p, i: (p, 0, 0)),
                   pl.BlockSpec((None, 8, LANES), lambda p, i: (p, 0, 0))],
        out_shape=[jax.ShapeDtypeStruct((T, 512), F32), jax.ShapeDtypeStruct((2, T, LANES), F32),
                   jax.ShapeDtypeStruct((2, T, LANES), F32), jax.ShapeDtypeStruct((SWA_HEADS, W, 2 * W), F32),
                   jax.ShapeDtypeStruct((4, 8, LANES), F32)],
        compiler_params=_cparams("arbitrary", "arbitrary"),
    )(sinks, q, kad, kad, vad, vad, bias, do, lse, delta)


def _mem_fwd(q, mk, mv, T, tq):
    scale = MEM_HEAD_DIM ** -0.5

    def body(q_ref, k_ref, v_ref, o_ref, lse_ref):
        s = _dot(q_ref[...], k_ref[...], NT) * scale
        m = jnp.max(s, axis=1, keepdims=True)
        p = jnp.exp(s - m)
        l = jnp.sum(p, axis=1, keepdims=True)
        o_ref[...] = _dot((p / l).astype(BF16), v_ref[...]).astype(o_ref.dtype)
        lse_ref[...] = jnp.broadcast_to(m + jnp.log(l), (tq, LANES))

    qspec = pl.BlockSpec((tq, LANES), lambda h, i: (i, h))
    kspec = pl.BlockSpec((N_MEM, LANES), lambda h, i: (0, h))
    return pl.pallas_call(
        body, name="mem_fwd", grid=(MEM_HEADS, T // tq),
        in_specs=[qspec, kspec, kspec],
        out_specs=[qspec, pl.BlockSpec((None, tq, LANES), lambda h, i: (h, i, 0))],
        out_shape=[jax.ShapeDtypeStruct((T, 512), BF16), jax.ShapeDtypeStruct((MEM_HEADS, T, LANES), F32)],
        compiler_params=_cparams("parallel", "parallel"),
    )(q, mk, mv)


def _mem_bwd(q, mk, mv, do, lse, delta, T, tq):
    scale = MEM_HEAD_DIM ** -0.5
    rep = N_MEM // LANES

    def body(q_ref, k_ref, v_ref, do_ref, lse_ref, dl_ref, dq_ref, dk_ref, dv_ref):
        i = pl.program_id(1)

        @pl.when(i == 0)
        def _():
            dk_ref[...] = jnp.zeros_like(dk_ref)
            dv_ref[...] = jnp.zeros_like(dv_ref)

        qv, dov = q_ref[...], do_ref[...]
        s = _dot(qv, k_ref[...], NT) * scale
        p = jnp.exp(s - jnp.tile(lse_ref[...], (1, rep)))
        dp = _dot(dov, v_ref[...], NT)
        ds = p * (dp - jnp.tile(dl_ref[...], (1, rep)))
        dsb = ds.astype(BF16)
        dq_ref[...] = _dot(dsb, k_ref[...]) * scale
        dk_ref[...] += _dot(dsb, qv, TN) * scale
        dv_ref[...] += _dot(p.astype(BF16), dov, TN)

    qspec = pl.BlockSpec((tq, LANES), lambda h, i: (i, h))
    kspec = pl.BlockSpec((N_MEM, LANES), lambda h, i: (0, h))
    stat = pl.BlockSpec((None, tq, LANES), lambda h, i: (h, i, 0))
    return pl.pallas_call(
        body, name="mem_bwd", grid=(MEM_HEADS, T // tq),
        in_specs=[qspec, kspec, kspec, qspec, stat, stat],
        out_specs=[qspec, kspec, kspec],
        out_shape=[jax.ShapeDtypeStruct((T, 512), F32), jax.ShapeDtypeStruct((N_MEM, 512), F32),
                   jax.ShapeDtypeStruct((N_MEM, 512), F32)],
        compiler_params=_cparams("arbitrary", "arbitrary"),
    )(q, mk, mv, do, lse, delta)


def _mem_prep_fwd(mem, g_mem, w_kv, kn_gain, gm128):
    def body(mem_ref, g_ref, w_ref, kn_ref, gm_ref, memn_o, kv_o, mk_o, mv_o):
        xhat, _ = _rms_rows(mem_ref[...], None)
        memn = (xhat * g_ref[...]).astype(BF16)
        memn_o[...] = memn
        kv = _dot(memn, w_ref[...])
        kv_o[...] = kv
        gm = gm_ref[...]
        for c in range(4):
            sl = slice(c * LANES, (c + 1) * LANES)
            y, _ = _head_norm(kv[:, sl], gm, kn_ref[...])
            mk_o[:, sl] = y.astype(BF16)
        mv_o[...] = kv[:, 512:].astype(BF16)

    vm = pl.BlockSpec(memory_space=pltpu.VMEM)
    return pl.pallas_call(
        body, name="mem_prep_fwd", in_specs=[vm] * 5, out_specs=[vm] * 4,
        out_shape=[jax.ShapeDtypeStruct((N_MEM, D_MODEL), BF16), jax.ShapeDtypeStruct((N_MEM, D_MODEL), F32),
                   jax.ShapeDtypeStruct((N_MEM, 512), BF16), jax.ShapeDtypeStruct((N_MEM, 512), BF16)],
        compiler_params=pltpu.CompilerParams(vmem_limit_bytes=VMEM_LIMIT),
    )(mem, g_mem, w_kv, kn_gain, gm128)


def _mem_prep_bwd(mem, g_mem, memn, kv, w_kv, kn_gain, gm128, dmk, dmv):
    def body(mem_ref, g_ref, memn_ref, kv_ref, w_ref, kn_ref, gm_ref, dmk_ref, dmv_ref, dw_o, dg_o, dkn_o, dkv_s):
        gm = gm_ref[...]
        dkn = jnp.zeros((1, LANES), F32)
        for c in range(4):
            sl = slice(c * LANES, (c + 1) * LANES)
            dx, dg = _head_norm_bwd(dmk_ref[:, sl], kv_ref[:, sl], gm, kn_ref[...])
            dkv_s[:, sl] = dx.astype(BF16)
            dkn = dkn + dg
        dkn_o[...] = dkn
        dkv_s[:, 512:] = dmv_ref[...].astype(BF16)
        dkv = dkv_s[...]
        dw_o[...] = _dot(memn_ref[...], dkv, TN)
        dmemn = _dot(dkv, w_ref[...], NT)
        xhat, _ = _rms_rows(mem_ref[...], None)
        dg_o[...] = jnp.sum(dmemn * xhat, axis=0, keepdims=True)

    vm = pl.BlockSpec(memory_space=pltpu.VMEM)
    return pl.pallas_call(
        body, name="mem_prep_bwd", in_specs=[vm] * 9, out_specs=[vm] * 3,
        out_shape=[jax.ShapeDtypeStruct((D_MODEL, D_MODEL), F32), jax.ShapeDtypeStruct((1, D_MODEL), F32),
                   jax.ShapeDtypeStruct((1, LANES), F32)],
        scratch_shapes=[pltpu.VMEM((N_MEM, D_MODEL), BF16)],
        compiler_params=pltpu.CompilerParams(vmem_limit_bytes=VMEM_LIMIT),
    )(mem, g_mem, memn, kv, w_kv, kn_gain, gm128, dmk, dmv)


SLOT_O = D_MODEL // N_SHARD


def _merge_fwd(proj, b_gate, o3, w3, T, tb):
    def body(gl_ref, bg_ref, oa_ref, of_ref, om_ref, wa_ref, wf_ref, wm_ref, out_ref):
        o_refs = (oa_ref, of_ref, om_ref)
        w_refs = (wa_ref, wf_ref, wm_ref)
        for n in range(N_SHARD):
            acc = jnp.zeros((tb, SLOT_O), F32)
            for b in range(3):
                c0 = b * D_MODEL + n * SLOT_O
                g = jax.nn.sigmoid(gl_ref[:, c0:c0 + SLOT_O] + bg_ref[:, c0:c0 + SLOT_O])
                acc = acc + g * _dot(o_refs[b][...], w_refs[b][n])
            out_ref[:, n * SLOT_O:(n + 1) * SLOT_O] = acc.astype(out_ref.dtype)

    rows = pl.BlockSpec((tb, 512), lambda i: (i, 0))
    wspec = pl.BlockSpec((N_SHARD, 512, SLOT_O), lambda i: (0, 0, 0))
    return pl.pallas_call(
        body, name="merge_fwd", grid=(T // tb,),
        in_specs=[pl.BlockSpec((tb, GATE_W), lambda i: (i, 1)), pl.BlockSpec((1, GATE_W), lambda i: (0, 0)),
                  rows, rows, rows, wspec, wspec, wspec],
        out_specs=pl.BlockSpec((tb, D_MODEL), lambda i: (i, 0)),
        out_shape=jax.ShapeDtypeStruct((T, D_MODEL), BF16),
        compiler_params=_cparams("parallel"),
    )(proj, b_gate, *o3, *w3)


def _merge_bwd(proj, b_gate, o3, w3, dmerged, T, tb):
    heads = (SWA_HEADS, FOX_HEADS, MEM_HEADS)

    def body(gl_ref, bg_ref, oa_ref, of_ref, om_ref, wa_ref, wf_ref, wm_ref, dm_ref,
             dgl_o, doa_o, dof_o, dom_o, dla_o, dlf_o, dlm_o, dwa_o, dwf_o, dwm_o, dbg_o):
        i = pl.program_id(0)
        o_refs = (oa_ref, of_ref, om_ref)
        w_refs = (wa_ref, wf_ref, wm_ref)
        do_refs = (doa_o, dof_o, dom_o)
        dl_refs = (dla_o, dlf_o, dlm_o)
        dw_refs = (dwa_o, dwf_o, dwm_o)

        @pl.when(i == 0)
        def _():
            for r in dw_refs:
                r[...] = jnp.zeros_like(r)
            dbg_o[...] = jnp.zeros_like(dbg_o)

        lane = _lane((tb, LANES))
        for b in range(3):
            ob = o_refs[b][...]
            do = jnp.zeros((tb, 512), F32)
            for n in range(N_SHARD):
                c0 = b * D_MODEL + n * SLOT_O
                g = jax.nn.sigmoid(gl_ref[:, c0:c0 + SLOT_O] + bg_ref[:, c0:c0 + SLOT_O])
                dm = dm_ref[:, n * SLOT_O:(n + 1) * SLOT_O]
                y = _dot(ob, w_refs[b][n])
                dgl = dm * y * g * (1.0 - g)
                dgl_o[:, c0:c0 + SLOT_O] = dgl.astype(dgl_o.dtype)
                dbg_o[:, c0:c0 + SLOT_O] += jnp.sum(dgl, axis=0, keepdims=True)
                dy = (dm * g).astype(BF16)
                do = do + _dot(dy, w_refs[b][n], NT)
                dw_refs[b][n] += _dot(ob, dy, TN)
            do_refs[b][...] = do.astype(BF16)
            prod = do * ob.astype(F32)
            for c in range(4):
                blk = prod[:, c * LANES:(c + 1) * LANES]
                if heads[b] == 8:
                    lo = jnp.sum(jnp.where(lane < 64, blk, 0.0), axis=1, keepdims=True)
                    hi = jnp.sum(jnp.where(lane >= 64, blk, 0.0), axis=1, keepdims=True)
                    dl_refs[b][2 * c] = jnp.broadcast_to(lo, (tb, LANES))
                    dl_refs[b][2 * c + 1] = jnp.broadcast_to(hi, (tb, LANES))
                else:
                    dl_refs[b][c] = jnp.broadcast_to(jnp.sum(blk, axis=1, keepdims=True), (tb, LANES))

    rows = pl.BlockSpec((tb, 512), lambda i: (i, 0))
    wspec = pl.BlockSpec((N_SHARD, 512, SLOT_O), lambda i: (0, 0, 0))
    stat = lambda h: pl.BlockSpec((h, tb, LANES), lambda i: (0, i, 0))
    return pl.pallas_call(
        body, name="merge_bwd", grid=(T // tb,),
        in_specs=[pl.BlockSpec((tb, GATE_W), lambda i: (i, 1)), pl.BlockSpec((1, GATE_W), lambda i: (0, 0)),
                  rows, rows, rows, wspec, wspec, wspec, pl.BlockSpec((tb, D_MODEL), lambda i: (i, 0))],
        out_specs=[pl.BlockSpec((tb, GATE_W), lambda i: (i, 0)), rows, rows, rows,
                   stat(8), stat(8), stat(4), wspec, wspec, wspec, pl.BlockSpec((1, GATE_W), lambda i: (0, 0))],
        out_shape=[jax.ShapeDtypeStruct((T, GATE_W), BF16)] + [jax.ShapeDtypeStruct((T, 512), BF16)] * 3
        + [jax.ShapeDtypeStruct((8, T, LANES), F32)] * 2 + [jax.ShapeDtypeStruct((4, T, LANES), F32)]
        + [jax.ShapeDtypeStruct((N_SHARD, 512, SLOT_O), F32)] * 3 + [jax.ShapeDtypeStruct((1, GATE_W), F32)],
        compiler_params=_cparams("arbitrary"),
    )(proj, b_gate, *o3, *w3, dmerged)


def _local_step(x, mem, tgt, small, wc, w_kv, w_o3, w_out, w_up, w_down):
    T = x.shape[0]
    tm = min(512, T)
    tile2 = lambda v: jnp.tile(v.reshape(1, -1), (1, LANES // v.size))
    gains = jnp.concatenate([tile2(small["qn_swa"]), tile2(small["kn_swa"]), tile2(small["qn_fox"]),
                             tile2(small["kn_fox"]), tile2(small["qn_mem"]), jnp.zeros((3, LANES), F32)], axis=0)
    kn_mem = small["kn_mem"].reshape(1, LANES)
    bfor = jnp.pad(small["b_forget"].reshape(1, -1), ((0, 0), (0, LANES - FOX_HEADS)))
    gm64 = _group_mean_matrix(64)
    gm128 = _group_mean_matrix(128)
    tb_prep = min(256, T)
    ones = jnp.ones((tb_prep, tb_prep), F32)
    tril = jnp.tril(ones).astype(BF16)
    triu = jnp.triu(ones).astype(BF16)
    bucket = _t5_bucket_matrix()
    g_mix, g_mlp, g_mem = small["g_mix"], small["g_mlp"], small["g_mem"]
    b_gate = small["b_gate"]
    sinks = small["sink_swa"].reshape(-1)

    tl = min(1024, T)
    sq = pl.BlockSpec((tl, D_MODEL), lambda i, j, k: (i, j))
    h = _rmsnorm("rms_mix", x, g_mix, tm)
    (proj,) = _matmul(
        "mm_proj", h, wc, dims=NN, grid=(T // tl, PROJ_W // D_MODEL, 1),
        a_spec=pl.BlockSpec((tl, D_MODEL), lambda i, j, k: (i, 0)),
        b_spec=pl.BlockSpec((D_MODEL, D_MODEL), lambda i, j, k: (0, j)),
        acc_shape=(tl, D_MODEL),
        outs=[(jax.ShapeDtypeStruct((T, PROJ_W), F32), sq)],
        epilogue=_epi_store)
    qa, qf, kf, vf, qm, kad, vad, cb, crow = _prep_fwd(proj, gains, bfor, tril, gm64, gm128, T, tb_prep)
    bias = _swa_bias(small["rel_bias"], bucket)
    o_swa, lse_swa = _swa_fwd(sinks, qa, kad, vad, bias, T)
    o_fox, lse_fox = _fox_fwd(qf, kf, vf, cb, crow, T, tm)
    memn, kv, mk, mv = _mem_prep_fwd(mem, g_mem, w_kv, kn_mem, gm128)
    o_mem, lse_mem = _mem_fwd(qm, mk, mv, T, tm)
    o3 = (o_swa, o_fox, o_mem)
    merged = _merge_fwd(proj, b_gate, o3, w_o3, T, min(256, T))

    def epi_residual(acc, extra_refs, out_refs, ij):
        out_refs[0][...] = extra_refs[0][...] + acc

    row_full = pl.BlockSpec((tm, D_MODEL), lambda i, j, k: (i, 0))
    row_big = pl.BlockSpec((tl, D_MODEL), lambda i, j, k: (i, 0))
    whole = pl.BlockSpec((D_MODEL, D_MODEL), lambda i, j, k: (0, 0))
    (x2,) = _matmul(
        "mm_out", merged, w_out, dims=NN, grid=(T // tl, 1, 1),
        a_spec=row_big, b_spec=whole,
        acc_shape=(tl, D_MODEL), extra=[(x, row_big)],
        outs=[(jax.ShapeDtypeStruct((T, D_MODEL), F32), row_big)], epilogue=epi_residual)
    hm = _rmsnorm("rms_mlp", x2, g_mlp, tm)

    def epi_relu2(acc, extra_refs, out_refs, ij):
        out_refs[0][...] = acc
        r = jnp.maximum(acc, 0.0)
        out_refs[1][...] = (r * r).astype(BF16)

    up, u = _matmul(
        "mm_up", hm, w_up, dims=NN, grid=(T // tl, N_SHARD, 1),
        a_spec=row_big, b_spec=pl.BlockSpec((None, D_MODEL, D_MODEL), lambda i, j, k: (j, 0, 0)),
        acc_shape=(tl, D_MODEL),
        outs=[(jax.ShapeDtypeStruct((T, D_FF), F32), sq), (jax.ShapeDtypeStruct((T, D_FF), BF16), sq)],
        epilogue=epi_relu2)

    def epi_loss(acc, extra_refs, out_refs, ij):
        y = extra_refs[0][...] + acc
        err = y - extra_refs[1][...]
        out_refs[0][...] = err * (1.0 / D_MODEL)
        sq = jnp.sum(jnp.sum(err * err, axis=1, keepdims=True), axis=0, keepdims=True)

        @pl.when(ij[0] == 0)
        def _():
            out_refs[1][...] = jnp.zeros_like(out_refs[1])

        out_refs[1][...] += jnp.broadcast_to(sq, out_refs[1].shape)

    kblk = pl.BlockSpec((tl, D_MODEL), lambda i, j, k: (i, k))
    dy, loss_acc = _matmul(
        "mm_down", u, w_down, dims=NN, grid=(T // tl, 1, N_SHARD),
        a_spec=kblk, b_spec=pl.BlockSpec((D_MODEL, D_MODEL), lambda i, j, k: (k, 0)),
        acc_shape=(tl, D_MODEL), extra=[(x2, row_big), (tgt, row_big)],
        outs=[(jax.ShapeDtypeStruct((T, D_MODEL), F32), row_big),
              (jax.ShapeDtypeStruct((8, LANES), F32), pl.BlockSpec((8, LANES), lambda i, j, k: (0, 0)))],
        epilogue=epi_loss)
    loss = loss_acc[0, 0] * (0.5 / D_MODEL)

    def epi_dup(acc, extra_refs, out_refs, ij):
        out_refs[0][...] = (acc * (2.0 * jnp.maximum(extra_refs[0][...], 0.0))).astype(BF16)

    (dup,) = _matmul(
        "mm_dup", dy, w_down, dims=NT, grid=(T // tl, N_SHARD, 1),
        a_spec=row_big, b_spec=pl.BlockSpec((D_MODEL, D_MODEL), lambda i, j, k: (j, 0)),
        acc_shape=(tl, D_MODEL), extra=[(up, sq)],
        outs=[(jax.ShapeDtypeStruct((T, D_FF), BF16), sq)], epilogue=epi_dup)

    nkt = T // tl
    t_rows = pl.BlockSpec((tl, D_MODEL), lambda i, j, k: (k, i))
    t_cols = pl.BlockSpec((tl, D_MODEL), lambda i, j, k: (k, j))
    (d_w_down,) = _matmul(
        "mm_dw_down", u, dy, dims=TN, grid=(N_SHARD, 1, nkt),
        a_spec=t_rows, b_spec=t_cols, acc_shape=(D_MODEL, D_MODEL),
        outs=[(jax.ShapeDtypeStruct((D_FF, D_MODEL), F32), pl.BlockSpec((D_MODEL, D_MODEL), lambda i, j, k: (i, 0)))],
        epilogue=_epi_store)
    (d_w_up,) = _matmul(
        "mm_dw_up", hm, dup, dims=TN, grid=(1, N_SHARD, nkt),
        a_spec=t_rows, b_spec=t_cols, acc_shape=(D_MODEL, D_MODEL),
        outs=[(jax.ShapeDtypeStruct((N_SHARD, D_MODEL, D_MODEL), F32),
               pl.BlockSpec((None, D_MODEL, D_MODEL), lambda i, j, k: (j, 0, 0)))],
        epilogue=_epi_store)

    def epi_rms_bwd(acc, extra_refs, out_refs, ij):
        dx, dg = _rmsnorm_bwd_rows(acc, extra_refs[0][...], extra_refs[1][...])
        out_refs[0][...] = dx + extra_refs[2][...]

        @pl.when(ij[0] == 0)
        def _():
            out_refs[1][...] = jnp.zeros_like(out_refs[1])

        out_refs[1][...] += dg

    gain_spec = pl.BlockSpec((1, D_MODEL), lambda i, j, k: (0, 0))
    dx2, d_g_mlp = _matmul(
        "mm_dhm", dup, w_up, dims=NT, grid=(T // tl, 1, N_SHARD),
        a_spec=kblk, b_spec=pl.BlockSpec((None, D_MODEL, D_MODEL), lambda i, j, k: (k, 0, 0)),
        acc_shape=(tl, D_MODEL), extra=[(x2, row_big), (g_mlp, gain_spec), (dy, row_big)],
        outs=[(jax.ShapeDtypeStruct((T, D_MODEL), F32), row_big), (jax.ShapeDtypeStruct((1, D_MODEL), F32), gain_spec)],
        epilogue=epi_rms_bwd)

    (dmerged,) = _matmul(
        "mm_dmerged", dx2, w_out, dims=NT, grid=(T // tl, 1, 1),
        a_spec=row_big, b_spec=whole,
        acc_shape=(tl, D_MODEL), outs=[(jax.ShapeDtypeStruct((T, D_MODEL), F32), row_big)], epilogue=_epi_store)
    (d_w_out,) = _matmul(
        "mm_dw_out", merged, dx2, dims=TN, grid=(1, 1, nkt),
        a_spec=t_rows, b_spec=t_cols, acc_shape=(D_MODEL, D_MODEL),
        outs=[(jax.ShapeDtypeStruct((D_MODEL, D_MODEL), F32), whole)],
        epilogue=_epi_store)
    (dgl, do_swa, do_fox, do_mem, dl_swa, dl_fox, dl_mem, d_wo_swa, d_wo_fox, d_wo_mem, d_b_gate) = _merge_bwd(
        proj, b_gate, o3, w_o3, dmerged, T, min(256, T))

    dqa, dkad, dvad, dbias, dsk = _swa_bwd(sinks, qa, kad, vad, bias, do_swa, lse_swa, dl_swa, T)
    dqf, dkf, dvf, dcrow, dcq = _fox_bwd(qf, kf, vf, do_fox, lse_fox, dl_fox, cb, crow, T, tm)
    dqm, dmk, dmv = _mem_bwd(qm, mk, mv, do_mem, lse_mem, dl_mem, T, tm)
    d_w_kv, d_g_mem, d_kn_mem = _mem_prep_bwd(mem, g_mem, memn, kv, w_kv, kn_mem, gm128, dmk, dmv)
    d_rel = _swa_bias_bwd(dbias, bucket)
    dc_keys = dcrow[:, :2, :].reshape(FOX_HEADS, T).T
    dc_queries = dcq.reshape(T, FOX_HEADS, FOX_HEAD_DIM)[:, :, 0]
    dccol = jnp.pad(dc_keys + dc_queries, ((0, 0), (0, LANES - FOX_HEADS)))
    dlo, gacc = _prep_bwd(proj, dqa, dkad, dvad, dqf, dkf, dvf, dqm, dccol, gains, bfor, triu, gm64, gm128, T, tb_prep)

    def dwc_half(name, dpart):
        (res,) = _matmul(
            name, h, dpart, dims=TN, grid=(1, LO_W // D_MODEL, nkt),
            a_spec=t_rows, b_spec=t_cols, acc_shape=(D_MODEL, D_MODEL),
            outs=[(jax.ShapeDtypeStruct((D_MODEL, LO_W), F32), pl.BlockSpec((D_MODEL, D_MODEL), lambda i, j, k: (0, j)))],
            epilogue=_epi_store)
        return res

    d_wc_lo = dwc_half("mm_dwc_lo", dlo)
    d_wc_gl = dwc_half("mm_dwc_gl", dgl)
    (dh_lo,) = _matmul(
        "mm_dh_lo", dlo, wc, dims=NT, grid=(T // tl, 1, LO_W // D_MODEL),
        a_spec=kblk, b_spec=pl.BlockSpec((D_MODEL, D_MODEL), lambda i, j, k: (0, k)),
        acc_shape=(tl, D_MODEL), outs=[(jax.ShapeDtypeStruct((T, D_MODEL), F32), row_big)], epilogue=_epi_store)

    def epi_dx(acc, extra_refs, out_refs, ij):
        dhh = acc + extra_refs[3][...]
        dx, dg = _rmsnorm_bwd_rows(dhh, extra_refs[0][...], extra_refs[1][...])
        out_refs[0][...] = dx + extra_refs[2][...]

        @pl.when(ij[0] == 0)
        def _():
            out_refs[1][...] = jnp.zeros_like(out_refs[1])

        out_refs[1][...] += dg

    grad_x, d_g_mix = _matmul(
        "mm_dh_gl", dgl, wc, dims=NT, grid=(T // tm, 1, GATE_W // D_MODEL),
        a_spec=pl.BlockSpec((tm, D_MODEL), lambda i, j, k: (i, k)),
        b_spec=pl.BlockSpec((D_MODEL, D_MODEL), lambda i, j, k: (0, k + LO_W // D_MODEL)),
        acc_shape=(tm, D_MODEL), extra=[(x, row_full), (g_mix, gain_spec), (dx2, row_full), (dh_lo, row_full)],
        outs=[(jax.ShapeDtypeStruct((T, D_MODEL), F32), row_full), (jax.ShapeDtypeStruct((1, D_MODEL), F32), gain_spec)],
        epilogue=epi_dx)

    fold64 = lambda row: (row[:64] + row[64:]).reshape(1, 64)
    grads = {
        "g_mix": d_g_mix, "b_gate": d_b_gate, "b_forget": gacc[5, :FOX_HEADS].reshape(1, FOX_HEADS),
        "qn_swa": fold64(gacc[0]), "kn_swa": fold64(gacc[1]),
        "sink_swa": -dsk[:, :2, 0].reshape(1, SWA_HEADS), "rel_bias": d_rel[:, :SWA_HEADS],
        "qn_fox": fold64(gacc[2]), "kn_fox": fold64(gacc[3]),
        "g_mem": d_g_mem, "qn_mem": gacc[4].reshape(1, LANES), "kn_mem": d_kn_mem, "g_mlp": d_g_mlp,
        "wc_lo": d_wc_lo, "wc_gl": d_wc_gl, "w_mem_kv": d_w_kv,
        "w_o_swa": d_wo_swa, "w_o_fox": d_wo_fox, "w_o_mem": d_wo_mem,
        "w_out": d_w_out, "w_mlp_up": d_w_up, "w_mlp_down": d_w_down,
    }
    return loss, grad_x, grads


MESH = pl.DeviceIdType.MESH
ANY = pl.BlockSpec(memory_space=pl.ANY)


def _place():
    x, y, c = lax.axis_index("x"), lax.axis_index("y"), lax.axis_index("c")
    chips = [(1 - x, y), (x, 1 - y), (1 - x, 1 - y)]
    return x, y, c, chips


def _all_gather_shards(slots):
    n = len(slots)

    def body(*refs):
        out = refs[n:2 * n]
        ici_send, ici_recv, d2d_send, d2d_recv = refs[2 * n:]
        x, y, c, chips = _place()
        sibling = (x, y, 1 - c)
        me = 2 * x + y

        def half(a, who):
            hr = slots[a].shape[1] // 2
            return pl.ds(pl.multiple_of(who * hr, hr), hr)

        def ici(a, j, slot, to):
            return pltpu.make_async_remote_copy(
                src_ref=out[a].at[me, half(a, c)], dst_ref=out[a].at[slot, half(a, c)],
                send_sem=ici_send.at[3 * a + j], recv_sem=ici_recv.at[3 * a + j], device_id=to, device_id_type=MESH)

        def d2d(a, j, slot, which):
            part = out[a].at[slot, half(a, which)]
            return pltpu.make_async_remote_copy(
                src_ref=part, dst_ref=part, send_sem=d2d_send.at[3 * a + j], recv_sem=d2d_recv.at[3 * a + j],
                device_id=sibling, device_id_type=MESH)

        sends = [ici(a, j, me, (*chip, c)) for a in range(n) for j, chip in enumerate(chips)]
        for cp in sends:
            cp.start()
        passed = []
        for a in range(n):
            for j, (px, py) in enumerate(chips):
                ici(a, j, 2 * px + py, (px, py, c)).wait_recv()
                cp = d2d(a, j, 2 * px + py, c)
                cp.start()
                passed.append(cp)
        for a in range(n):
            for j, (px, py) in enumerate(chips):
                d2d(a, j, 2 * px + py, 1 - c).wait_recv()
        for cp in sends + passed:
            cp.wait_send()

    return pl.pallas_call(
        body, name="all_gather_weights",
        in_specs=[ANY] * n, out_specs=[ANY] * n,
        out_shape=[jax.ShapeDtypeStruct(s.shape, s.dtype) for s in slots],
        input_output_aliases={a: a for a in range(n)},
        scratch_shapes=[pltpu.SemaphoreType.DMA((3 * n,))] * 4,
    )(*slots)


def _pair_exchange(gs):
    n = len(gs)

    def body(*refs):
        src, stage = refs[:n], refs[n:2 * n]
        send_sem, recv_sem = refs[2 * n:]
        x, y, c, _ = _place()
        copies = []
        for a in range(n):
            hr = gs[a].shape[1] // 2
            theirs = pl.ds(pl.multiple_of((1 - c) * hr, hr), hr)
            copies.append(pltpu.make_async_remote_copy(
                src_ref=src[a].at[:, theirs, :], dst_ref=stage[a], send_sem=send_sem.at[a], recv_sem=recv_sem.at[a],
                device_id=(x, y, 1 - c), device_id_type=MESH))
        for cp in copies:
            cp.start()
        for cp in copies:
            cp.wait()

    return pl.pallas_call(
        body, name="pair_exchange", in_specs=[ANY] * n, out_specs=[ANY] * n,
        out_shape=[jax.ShapeDtypeStruct((N_SHARD, g.shape[1] // 2, g.shape[2]), g.dtype) for g in gs],
        scratch_shapes=[pltpu.SemaphoreType.DMA((n,))] * 2,
    )(*gs)


def _chip_exchange(sums):
    n = len(sums)

    def body(*refs):
        src, got = refs[:n], refs[n:2 * n]
        send_sem, recv_sem = refs[2 * n:]
        x, y, c, chips = _place()
        copies = []
        for a in range(n):
            for j, (px, py) in enumerate(chips):
                copies.append(pltpu.make_async_remote_copy(
                    src_ref=src[a].at[2 * px + py], dst_ref=got[a].at[j],
                    send_sem=send_sem.at[3 * a + j], recv_sem=recv_sem.at[3 * a + j],
                    device_id=(px, py, c), device_id_type=MESH))
        for cp in copies:
            cp.start()
        for cp in copies:
            cp.wait()

    return pl.pallas_call(
        body, name="chip_exchange", in_specs=[ANY] * n, out_specs=[ANY] * n,
        out_shape=[jax.ShapeDtypeStruct((3,) + s.shape[1:], s.dtype) for s in sums],
        scratch_shapes=[pltpu.SemaphoreType.DMA((3 * n,))] * 2,
    )(*sums)


def _pair_gather(fulls):
    n = len(fulls)

    def body(*refs):
        full = refs[n:2 * n]
        send_sem, recv_sem = refs[2 * n:]
        x, y, c, _ = _place()
        copies = []
        for a in range(n):
            hr = fulls[a].shape[0] // 2
            mine = full[a].at[pl.ds(pl.multiple_of(c * hr, hr), hr)]
            copies.append(pltpu.make_async_remote_copy(
                src_ref=mine, dst_ref=mine, send_sem=send_sem.at[a], recv_sem=recv_sem.at[a],
                device_id=(x, y, 1 - c), device_id_type=MESH))
        for cp in copies:
            cp.start()
        for cp in copies:
            cp.wait()

    return pl.pallas_call(
        body, name="pair_gather", in_specs=[ANY] * n, out_specs=[ANY] * n,
        out_shape=[jax.ShapeDtypeStruct(f.shape, f.dtype) for f in fulls],
        input_output_aliases={a: a for a in range(n)},
        scratch_shapes=[pltpu.SemaphoreType.DMA((n,))] * 2,
    )(*fulls)


def _row_block(rows):
    return min(rows, 128)


def _pair_sum(name, place, g, stage):
    _, R, C = g.shape
    hr = R // 2
    rb = _row_block(hr)
    nb = hr // rb

    def body(place_ref, g_ref, st_ref, sum_bf, own_f32):
        s = pl.program_id(1)
        tot = g_ref[...] + st_ref[...]
        sum_bf[...] = tot.astype(BF16)

        @pl.when(s == place_ref[0])
        def _():
            own_f32[...] = tot

    return pl.pallas_call(
        body, name=name,
        grid_spec=pltpu.PrefetchScalarGridSpec(
            num_scalar_prefetch=1, grid=(nb, N_SHARD),
            in_specs=[pl.BlockSpec((None, rb, C), lambda i, s, pr: (s, pr[1] * nb + i, 0)),
                      pl.BlockSpec((None, rb, C), lambda i, s, pr: (s, i, 0))],
            out_specs=[pl.BlockSpec((None, rb, C), lambda i, s, pr: (s, i, 0)),
                       pl.BlockSpec((rb, C), lambda i, s, pr: (i, 0))]),
        out_shape=[jax.ShapeDtypeStruct((N_SHARD, hr, C), BF16), jax.ShapeDtypeStruct((hr, C), F32)],
        compiler_params=_cparams("arbitrary", "arbitrary"),
    )(place, g, stage)


def _final_sum(name, place, own, got):
    hr, C = own.shape
    rb = _row_block(hr)
    nb = hr // rb

    def body(place_ref, own_ref, got_ref, o_ref):
        o_ref[...] = ((own_ref[...] + got_ref[0].astype(F32)) + got_ref[1].astype(F32)) + got_ref[2].astype(F32)

    return pl.pallas_call(
        body, name=name,
        grid_spec=pltpu.PrefetchScalarGridSpec(
            num_scalar_prefetch=1, grid=(nb,),
            in_specs=[pl.BlockSpec((rb, C), lambda i, pr: (i, 0)), pl.BlockSpec((3, rb, C), lambda i, pr: (0, i, 0))],
            out_specs=pl.BlockSpec((rb, C), lambda i, pr: (pr[1] * nb + i, 0))),
        out_shape=jax.ShapeDtypeStruct((2 * hr, C), F32),
        compiler_params=_cparams("arbitrary"),
    )(place, own, got)


def _adamw_math(w, g, m, v):
    m = ADAM_B1 * m + (1.0 - ADAM_B1) * g
    v = ADAM_B2 * v + (1.0 - ADAM_B2) * (g * g)
    m_hat = m / (1.0 - ADAM_B1 ** ADAM_STEP)
    v_hat = v / (1.0 - ADAM_B2 ** ADAM_STEP)
    delta = -ADAM_LR * (m_hat / (jnp.sqrt(v_hat) + ADAM_EPS) + ADAM_WD * w)
    return delta, m, v


def _adamw(name, w, g, m, v):
    R, Cw = w.shape
    Cg = g.shape[1]
    rb = _row_block(R)

    def body(w_ref, g_ref, m_ref, v_ref, g_o, d_o, m_o, v_o):
        gv = g_ref[...]
        delta, mn, vn = _adamw_math(w_ref[...], gv, m_ref[...], v_ref[...])
        g_o[...] = gv
        d_o[...] = delta
        m_o[...] = mn
        v_o[...] = vn

    blk = pl.BlockSpec((rb, Cg), lambda i: (i, 0))
    return pl.pallas_call(
        body, name=name, grid=(R // rb,),
        in_specs=[blk] * 4, out_specs=[blk] * 4,
        out_shape=[jax.ShapeDtypeStruct((R, Cw), F32)] * 4,
        compiler_params=_cparams("parallel"),
    )(w, g, m, v)


N_DEV = 8
SMALL_ROWS = 64


def _small_allreduce_adamw(g, w, m, v):
    def body(g_ref, w_ref, m_ref, v_ref, all_ref, gs_o, d_o, m_o, v_o, send_sems, recv_sems, local_sem):
        x, y, c, chips = _place()
        me, sibling = (x, y, c), (x, y, 1 - c)

        def rows(px, py, pc):
            return all_ref.at[pl.ds(pl.multiple_of((4 * px + 2 * py + pc) * SMALL_ROWS, SMALL_ROWS), SMALL_ROWS), :]

        def copy(k, block, to, src=None):
            return pltpu.make_async_remote_copy(
                src_ref=rows(*block) if src is None else src, dst_ref=rows(*block),
                send_sem=send_sems.at[k], recv_sem=recv_sems.at[k], device_id=to, device_id_type=MESH)

        mine = pltpu.make_async_copy(g_ref, rows(*me), local_sem)
        mine.start()
        first = [copy(0, me, sibling, src=g_ref)]
        first += [copy(1 + j, me, (*chip, c), src=g_ref) for j, chip in enumerate(chips)]
        for cp in first:
            cp.start()
        passed = [copy(4 + j, (*chip, c), sibling) for j, chip in enumerate(chips)]
        for j, chip in enumerate(chips):
            copy(1 + j, (*chip, c), me).wait_recv()
            passed[j].start()
        copy(0, sibling, me).wait_recv()
        for j, chip in enumerate(chips):
            copy(4 + j, (*chip, 1 - c), me).wait_recv()
        for cp in first + passed:
            cp.wait_send()
        mine.wait()

        tot = all_ref[0:SMALL_ROWS, :]
        for d in range(1, N_DEV):
            tot = tot + all_ref[d * SMALL_ROWS:(d + 1) * SMALL_ROWS, :]
        delta, mn, vn = _adamw_math(w_ref[...], tot, m_ref[...], v_ref[...])
        gs_o[...] = tot
        d_o[...] = delta
        m_o[...] = mn
        v_o[...] = vn

    vm = pl.BlockSpec(memory_space=pltpu.VMEM)
    shp = jax.ShapeDtypeStruct((SMALL_ROWS, LANES), F32)
    res = pl.pallas_call(
        body, name="small_allreduce_adamw", in_specs=[vm] * 4, out_specs=[vm] * 5,
        out_shape=[jax.ShapeDtypeStruct((N_DEV * SMALL_ROWS, LANES), F32), shp, shp, shp, shp],
        scratch_shapes=[pltpu.SemaphoreType.DMA((7,)), pltpu.SemaphoreType.DMA((7,)), pltpu.SemaphoreType.DMA],
    )(g, w, m, v)
    return res[1:]


SMALL_NAMES = ("g_mix", "b_gate", "b_forget", "qn_swa", "kn_swa", "sink_swa", "rel_bias", "qn_fox", "kn_fox",
               "g_mem", "qn_mem", "kn_mem", "g_mlp")
BIG_NAMES = ("w_in", "w_mem_kv", "w_o_swa", "w_o_fox", "w_o_mem", "w_out", "w_mlp_up", "w_mlp_down")
WEIGHT_NAMES = ("g_mix", "w_in", "b_gate", "b_forget", "qn_swa", "kn_swa", "sink_swa", "rel_bias", "qn_fox", "kn_fox",
                "g_mem", "w_mem_kv", "qn_mem", "kn_mem", "w_o_swa", "w_o_fox", "w_o_mem", "w_out", "g_mlp",
                "w_mlp_up", "w_mlp_down")


def _pack_small(parts, extra=None):
    rows = []
    for n in SMALL_NAMES:
        flat = parts[n].reshape(-1).astype(F32)
        flat = jnp.pad(flat, (0, (-flat.size) % LANES))
        rows.append(flat.reshape(-1, LANES))
    if extra is not None:
        rows.append(jnp.pad(extra.reshape(1, 1), ((0, 0), (0, LANES - 1))))
    packed = jnp.concatenate(rows, axis=0)
    return jnp.pad(packed, ((0, SMALL_ROWS - packed.shape[0]), (0, 0)))


def _unpack_small(packed, shapes):
    out, r = {}, 0
    for n in SMALL_NAMES:
        size = math.prod(shapes[n])
        nr = -(-size // LANES)
        out[n] = packed[r:r + nr].reshape(-1)[:size].reshape(shapes[n])
        r += nr
    return out, packed[r, 0]


def _reorder_w_in(w_full):
    seg = lambda a, b: w_full[:, a:b]
    pad = jnp.zeros((w_full.shape[0], C_GL - C_FL - FOX_HEADS), w_full.dtype)
    return jnp.concatenate([seg(0, 512), seg(768, 1280), seg(1280, 1792), seg(1792, 2304), seg(2312, 2824),
                            seg(512, 640), seg(640, 768), seg(2304, 2312), pad, seg(2824, IN_WIDTH)], axis=1)


def _restore_w_in(lo, gl):
    s = lambda a, b: lo[:, a:b]
    return jnp.concatenate([s(C_QA, C_QA + 512), s(C_KA, C_KA + 128), s(C_VA, C_VA + 128), s(C_QF, C_QF + 512),
                            s(C_KF, C_KF + 512), s(C_VF, C_VF + 512), s(C_FL, C_FL + FOX_HEADS), s(C_QM, C_QM + 512),
                            gl], axis=1)


def kernel(x, mem, g_mix, w_in, b_gate, b_forget, qn_swa, kn_swa, sink_swa, rel_bias, qn_fox, kn_fox, g_mem, w_mem_kv, qn_mem, kn_mem, w_o_swa, w_o_fox, w_o_mem, w_out, g_mlp, w_mlp_up, w_mlp_down, loss_target, m_g_mix, m_w_in, m_b_gate, m_b_forget, m_qn_swa, m_kn_swa, m_sink_swa, m_rel_bias, m_qn_fox, m_kn_fox, m_g_mem, m_w_mem_kv, m_qn_mem, m_kn_mem, m_w_o_swa, m_w_o_fox, m_w_o_mem, m_w_out, m_g_mlp, m_w_mlp_up, m_w_mlp_down, v_g_mix, v_w_in, v_b_gate, v_b_forget, v_qn_swa, v_kn_swa, v_sink_swa, v_rel_bias, v_qn_fox, v_kn_fox, v_g_mem, v_w_mem_kv, v_qn_mem, v_kn_mem, v_w_o_swa, v_w_o_fox, v_w_o_mem, v_w_out, v_g_mlp, v_w_mlp_up, v_w_mlp_down):
    given = dict(locals())
    W = {n: given[n] for n in WEIGHT_NAMES}
    M = {n: given["m_" + n] for n in WEIGHT_NAMES}
    V = {n: given["v_" + n] for n in WEIGHT_NAMES}
    pad_in = ((0, 0), (0, IN_SHARD_PAD - IN_SHARD))

    shards = [jnp.pad(w_in[0].astype(BF16), pad_in)] + [W[n][0].astype(BF16) for n in BIG_NAMES[1:]]
    slots = [jnp.broadcast_to(s[None], (N_SHARD,) + s.shape) for s in shards]
    g_in, g_kv, g_oa, g_of, g_om, g_out, g_up, g_down = _all_gather_shards(slots)
    w_full = jnp.concatenate([g_in[s, :, :IN_SHARD] for s in range(N_SHARD)], axis=1)
    wc = _reorder_w_in(w_full)
    small = {n: (W[n] if n == "rel_bias" else W[n].reshape(1, -1)) for n in SMALL_NAMES}

    loss, grad_x, grads = _local_step(
        x[0], mem[0], loss_target[0], small, wc, g_kv.reshape(D_MODEL, D_MODEL), (g_oa, g_of, g_om),
        g_out.reshape(D_MODEL, D_MODEL), g_up, g_down.reshape(D_FF, D_MODEL))

    d_full = _restore_w_in(grads["wc_lo"], grads["wc_gl"])
    d_in = jnp.stack([jnp.pad(d_full[:, s * IN_SHARD:(s + 1) * IN_SHARD], pad_in) for s in range(N_SHARD)])
    slot_rows = lambda a: a.reshape(N_SHARD, a.shape[0] // N_SHARD, a.shape[1])
    local = [d_in, slot_rows(grads["w_mem_kv"]), grads["w_o_swa"], grads["w_o_fox"], grads["w_o_mem"],
             slot_rows(grads["w_out"]), grads["w_mlp_up"], slot_rows(grads["w_mlp_down"])]
    place = jnp.stack([2 * lax.axis_index("x") + lax.axis_index("y"), lax.axis_index("c")]).astype(jnp.int32)
    staged = _pair_exchange(local)
    sums = [_pair_sum("pair_sum_" + n, place, g, st) for n, g, st in zip(BIG_NAMES, local, staged)]
    got = _chip_exchange([s[0] for s in sums])
    halves = [_final_sum("final_sum_" + n, place, s[1], r) for n, s, r in zip(BIG_NAMES, sums, got)]
    summed = _pair_gather(halves)

    out = {}
    for n, g in zip(BIG_NAMES, summed):
        res = _adamw("adamw_" + n, W[n][0], g, M[n][0], V[n][0])
        out[n] = [r.reshape(W[n].shape) for r in res]
    shapes = {n: W[n].shape for n in SMALL_NAMES}
    packed = _small_allreduce_adamw(_pack_small(grads, loss), _pack_small(W), _pack_small(M), _pack_small(V))
    unpacked = [_unpack_small(p, shapes) for p in packed]
    for n in SMALL_NAMES:
        out[n] = [u[0][n] for u in unpacked]
    loss_total = unpacked[0][1]

    return (loss_total, grad_x.reshape(x.shape),
            *[out[n][0] for n in WEIGHT_NAMES], *[out[n][1] for n in WEIGHT_NAMES],
            *[out[n][2] for n in WEIGHT_NAMES], *[out[n][3] for n in WEIGHT_NAMES])
```

```python
import functools
import math

import jax
import jax.numpy as jnp
from jax import lax
from jax.experimental import pallas as pl
from jax.experimental.pallas import tpu as pltpu

F32 = jnp.float32
BF16 = jnp.bfloat16

D_MODEL = 1024
N_MEM = 256
SWA_HEADS = 8
SWA_KV_HEADS = 2
SWA_HEAD_DIM = 64
WINDOW = 128
FOX_HEADS = 8
FOX_HEAD_DIM = 64
MEM_HEADS = 4
MEM_HEAD_DIM = 128
D_FF = 4 * D_MODEL
REL_BUCKETS = 32
REL_MAX_DIST = 128
EPS = 1e-6
NEG = -1e30
GATE_W = 3 * D_MODEL
IN_WIDTH = 5896
N_SHARD = 4
IN_SHARD = IN_WIDTH // N_SHARD
IN_SHARD_PAD = 1536

ADAM_LR = 0.001
ADAM_B1 = 0.9
ADAM_B2 = 0.999
ADAM_EPS = 1e-08
ADAM_WD = 0.01
ADAM_STEP = 10

LANES = 128
V7X_VMEM_BYTES = 64 * 1024 * 1024
VMEM_LIMIT = V7X_VMEM_BYTES * 3 // 4

C_QA, C_QF, C_KF, C_VF, C_QM, C_KA, C_VA, C_FL, C_GL = 0, 512, 1024, 1536, 2048, 2560, 2688, 2816, 3072
LO_W = 3072
PROJ_W = 6144

NN = (((1,), (0,)), ((), ()))
NT = (((1,), (1,)), ((), ()))
TN = (((0,), (0,)), ((), ()))


def _dot(a, b, dims=NN):
    return lax.dot_general(a, b, dims, preferred_element_type=F32)


def _cparams(*sem):
    return pltpu.CompilerParams(dimension_semantics=sem, vmem_limit_bytes=VMEM_LIMIT)


def _split3(a):
    hi = a.astype(BF16)
    r1 = a - hi.astype(F32)
    mid = r1.astype(BF16)
    lo = (r1 - mid.astype(F32)).astype(BF16)
    return hi, mid, lo


def _dot3_right(a, g):
    hi, mid, lo = _split3(a)
    return _dot(hi, g) + _dot(mid, g) + _dot(lo, g)


def _dot3_left(g, a):
    hi, mid, lo = _split3(a)
    return _dot(g, hi) + _dot(g, mid) + _dot(g, lo)


def _group_mean_matrix(d):
    r = jnp.arange(LANES)
    return jnp.where((r[:, None] // d) == (r[None, :] // d), 1.0 / d, 0.0).astype(BF16)


def _lane(shape):
    return lax.broadcasted_iota(jnp.int32, shape, len(shape) - 1)


def _matmul(name, a, b, *, dims, grid, a_spec, b_spec, acc_shape, outs, epilogue, extra=()):
    nk = grid[2]
    n_extra = len(extra)

    def body(a_ref, b_ref, *rest):
        extra_refs = rest[:n_extra]
        out_refs = rest[n_extra:n_extra + len(outs)]
        i, j, k = pl.program_id(0), pl.program_id(1), pl.program_id(2)
        part = _dot(a_ref[...].astype(BF16), b_ref[...].astype(BF16), dims)
        if nk == 1:
            epilogue(part, extra_refs, out_refs, (i, j))
            return
        acc_ref = rest[-1]

        @pl.when(k == 0)
        def _():
            acc_ref[...] = part

        @pl.when((k > 0) & (k < nk - 1))
        def _():
            acc_ref[...] += part

        @pl.when(k == nk - 1)
        def _():
            epilogue(acc_ref[...] + part, extra_refs, out_refs, (i, j))

    res = pl.pallas_call(
        body,
        name=name,
        grid=grid,
        in_specs=[a_spec, b_spec] + [s for _, s in extra],
        out_specs=[s for _, s in outs],
        out_shape=[s for s, _ in outs],
        scratch_shapes=[pltpu.VMEM(acc_shape, F32)] if nk > 1 else [],
        compiler_params=_cparams("arbitrary", "arbitrary", "arbitrary"),
    )(a, b, *[x for x, _ in extra])
    return res


def _epi_store(acc, extra_refs, out_refs, ij):
    out_refs[0][...] = acc.astype(out_refs[0].dtype)


def _rms_rows(x, g):
    r = lax.rsqrt(jnp.mean(x * x, axis=-1, keepdims=True) + EPS)
    return x * r, r


def _rmsnorm_bwd_rows(dh, x, g):
    xhat, r = _rms_rows(x, g)
    dxh = dh * g
    dx = r * (dxh - xhat * jnp.mean(dxh * xhat, axis=-1, keepdims=True))
    return dx, jnp.sum(dh * xhat, axis=0, keepdims=True)


def _rmsnorm(name, x, g, tb):
    T, Dm = x.shape

    def body(x_ref, g_ref, o_ref):
        xhat, _ = _rms_rows(x_ref[...], None)
        o_ref[...] = (xhat * g_ref[...]).astype(o_ref.dtype)

    return pl.pallas_call(
        body, name=name, grid=(T // tb,),
        in_specs=[pl.BlockSpec((tb, Dm), lambda i: (i, 0)), pl.BlockSpec((1, Dm), lambda i: (0, 0))],
        out_specs=pl.BlockSpec((tb, Dm), lambda i: (i, 0)),
        out_shape=jax.ShapeDtypeStruct((T, Dm), BF16),
        compiler_params=_cparams("parallel"),
    )(x, g)


def _head_norm(x, gm, gain):
    ms = _dot3_right(x * x, gm)
    r = lax.rsqrt(ms + EPS)
    return x * r * gain, x * r


def _head_norm_bwd(dy, x, gm, gain):
    ms = _dot3_right(x * x, gm)
    r = lax.rsqrt(ms + EPS)
    xhat = x * r
    dxh = dy * gain
    dx = r * (dxh - xhat * _dot3_right(dxh * xhat, gm))
    return dx, jnp.sum(dy * xhat, axis=0, keepdims=True)


def _log_sigmoid(z):
    return jnp.minimum(z, 0.0) - jnp.log(1.0 + jnp.exp(-jnp.abs(z)))


def _prep_fwd(proj, gains, bfor, tril, gm64, gm128, T, tb):
    nb = T // tb

    def body(qa_ref, qf_ref, kf_ref, vf_ref, qm_ref, ka_ref, va_ref, fl_ref, gains_ref, bfor_ref, tril_ref,
             gm64_ref, gm128_ref,
             qa_o, qf_o, kf_o, vf_o, qm_o, kad_o, vad_o, qaug_o, kaug_o, carry):
        i = pl.program_id(0)
        gm64v = gm64_ref[...]
        gm128v = gm128_ref[...]
        lane = _lane((tb, LANES))

        def norm512(src, dst, row, gm, scale=1.0):
            gain = gains_ref[row:row + 1, :]
            for c in range(4):
                sl = slice(c * LANES, (c + 1) * LANES)
                y, _ = _head_norm(src[:, sl], gm, gain)
                dst[:, sl] = (y * scale).astype(dst.dtype)

        norm512(qa_ref, qa_o, 0, gm64v)
        norm512(qf_ref, qf_o, 2, gm64v, FOX_SCALE)
        norm512(kf_ref, kf_o, 3, gm64v)
        norm512(qm_ref, qm_o, 4, gm128v)
        vf_o[...] = vf_ref[...].astype(vf_o.dtype)

        ka_n, _ = _head_norm(ka_ref[...], gm64v, gains_ref[1:2, :])
        ka_r = pltpu.roll(ka_n, 64, 1)
        va = va_ref[...]
        va_r = pltpu.roll(va, 64, 1)
        lo = lane < 64
        kad_o[0] = jnp.where(lo, ka_n, ka_r).astype(kad_o.dtype)
        kad_o[1] = jnp.where(lo, ka_r, ka_n).astype(kad_o.dtype)
        vad_o[0] = jnp.where(lo, va, va_r).astype(vad_o.dtype)
        vad_o[1] = jnp.where(lo, va_r, va).astype(vad_o.dtype)

        @pl.when(i == 0)
        def _():
            carry[...] = jnp.zeros_like(carry)

        logf = jnp.where(lane < FOX_HEADS, _log_sigmoid(fl_ref[...] + bfor_ref[...]), 0.0)
        c = _dot3_left(tril_ref[...], logf) + carry[0:1, :]
        carry[...] = jnp.broadcast_to(c[tb - 1:tb, :], carry.shape)
        for pair in range(FOX_HEADS // 2):
            qaug = jnp.zeros((tb, LANES), F32)
            kaug = jnp.zeros((tb, LANES), F32)
            for sub in range(2):
                col = jnp.sum(jnp.where(lane == 2 * pair + sub, c, 0.0), axis=1, keepdims=True)
                pieces = [p.astype(F32) for p in _split3(col)]
                base = AUG_STRIDE * sub
                for e in range(3):
                    qaug = jnp.where(lane == base + AUG_C + e, pieces[e], qaug)
                    kaug = jnp.where(lane == base + AUG_NEG_C + e, -pieces[e], kaug)
                qaug = jnp.where((lane >= base + AUG_NEG_C) & (lane < base + AUG_NEG_C + 3), 1.0, qaug)
                ones_k = ((lane >= base + AUG_C) & (lane < base + AUG_C + 3)) | (
                    (lane >= base + AUG_STAT) & (lane < base + AUG_STAT + 3))
                kaug = jnp.where(ones_k, 1.0, kaug)
            sl = slice(pair * LANES, (pair + 1) * LANES)
            qaug_o[:, sl] = qaug.astype(BF16)
            kaug_o[:, sl] = kaug.astype(BF16)

    def seg(width, start):
        return pl.BlockSpec((tb, width), lambda i, s=start // width: (i, s))

    const = lambda shape: pl.BlockSpec(shape, lambda i: tuple(0 for _ in shape))
    rows512 = pl.BlockSpec((tb, 512), lambda i: (i, 0))
    outs = pl.pallas_call(
        body, name="prep_fwd", grid=(nb,),
        in_specs=[seg(512, C_QA), seg(512, C_QF), seg(512, C_KF), seg(512, C_VF), seg(512, C_QM),
                  seg(128, C_KA), seg(128, C_VA), seg(128, C_FL),
                  const((8, LANES)), const((1, LANES)), const((tb, tb)), const((LANES, LANES)), const((LANES, LANES))],
        out_specs=[rows512, rows512, rows512, rows512, rows512,
                   pl.BlockSpec((2, tb, LANES), lambda i: (0, i, 0)), pl.BlockSpec((2, tb, LANES), lambda i: (0, i, 0)),
                   rows512, rows512],
        out_shape=[jax.ShapeDtypeStruct((T, 512), BF16)] * 5
        + [jax.ShapeDtypeStruct((2, T, LANES), BF16)] * 2
        + [jax.ShapeDtypeStruct((T, 512), BF16)] * 2,
        scratch_shapes=[pltpu.VMEM((8, LANES), F32)],
        compiler_params=_cparams("arbitrary"),
    )(proj, proj, proj, proj, proj, proj, proj, proj, gains, bfor, tril, gm64, gm128)
    return outs


def _prep_bwd(proj, dqa, dkad, dvad, dqf, dkf, dvf, dqm, dccol, gains, bfor, triu, gm64, gm128, T, tb):
    nb = T // tb

    def body(qa_ref, qf_ref, kf_ref, qm_ref, ka_ref, fl_ref,
             dqa_ref, dkad_ref, dvad_ref, dqf_ref, dkf_ref, dvf_ref, dqm_ref, dc_ref,
             gains_ref, bfor_ref, triu_ref, gm64_ref, gm128_ref,
             dlo_o, gacc_o, carry):
        i = pl.program_id(0)
        gm64v = gm64_ref[...]
        gm128v = gm128_ref[...]
        lane = _lane((tb, LANES))

        @pl.when(i == 0)
        def _():
            carry[...] = jnp.zeros_like(carry)
            gacc_o[...] = jnp.zeros_like(gacc_o)

        def norm512_bwd(dsrc, xsrc, col0, row, gm):
            gain = gains_ref[row:row + 1, :]
            gsum = jnp.zeros((1, LANES), F32)
            for c in range(4):
                sl = slice(c * LANES, (c + 1) * LANES)
                dx, dg = _head_norm_bwd(dsrc[:, sl], xsrc[:, sl], gm, gain)
                dlo_o[:, col0 + c * LANES:col0 + (c + 1) * LANES] = dx.astype(dlo_o.dtype)
                gsum = gsum + dg
            gacc_o[row:row + 1, :] += gsum

        norm512_bwd(dqa_ref, qa_ref, C_QA, 0, gm64v)
        norm512_bwd(dqf_ref, qf_ref, C_QF, 2, gm64v)
        norm512_bwd(dkf_ref, kf_ref, C_KF, 3, gm64v)
        norm512_bwd(dqm_ref, qm_ref, C_QM, 4, gm128v)
        dlo_o[:, C_VF:C_VF + 512] = dvf_ref[...].astype(dlo_o.dtype)

        lo = lane < 64

        def fold(ref):
            f0 = ref[0] + pltpu.roll(ref[0], 64, 1)
            f1 = ref[1] + pltpu.roll(ref[1], 64, 1)
            return jnp.where(lo, f0, f1)

        dka, dg = _head_norm_bwd(fold(dkad_ref), ka_ref[...], gm64v, gains_ref[1:2, :])
        gacc_o[1:2, :] += dg
        dlo_o[:, C_KA:C_KA + LANES] = dka.astype(dlo_o.dtype)
        dlo_o[:, C_VA:C_VA + LANES] = fold(dvad_ref).astype(dlo_o.dtype)

        dc = dc_ref[...]
        dlogf = _dot3_left(triu_ref[...], dc) + carry[0:1, :]
        carry[...] = jnp.broadcast_to(dlogf[0:1, :], carry.shape)
        z = fl_ref[...] + bfor_ref[...]
        dfl = jnp.where(lane < FOX_HEADS, dlogf / (1.0 + jnp.exp(z)), 0.0)
        gacc_o[5:6, :] += jnp.sum(dfl, axis=0, keepdims=True)
        dlo_o[:, C_FL:C_FL + LANES] = dfl.astype(dlo_o.dtype)
        dlo_o[:, C_FL + LANES:C_FL + 2 * LANES] = jnp.zeros((tb, LANES), dlo_o.dtype)

    rev = lambda i: nb - 1 - i

    def seg(width, start):
        return pl.BlockSpec((tb, width), lambda i, s=start // width: (rev(i), s))

    const = lambda shape: pl.BlockSpec(shape, lambda i: tuple(0 for _ in shape))
    rows512 = pl.BlockSpec((tb, 512), lambda i: (rev(i), 0))
    dup = pl.BlockSpec((2, tb, LANES), lambda i: (0, rev(i), 0))
    return pl.pallas_call(
        body, name="prep_bwd", grid=(nb,),
        in_specs=[seg(512, C_QA), seg(512, C_QF), seg(512, C_KF), seg(512, C_QM), seg(128, C_KA), seg(128, C_FL),
                  rows512, dup, dup, rows512, rows512, rows512, rows512,
                  pl.BlockSpec((tb, LANES), lambda i: (rev(i), 0)),
                  const((8, LANES)), const((1, LANES)), const((tb, tb)), const((LANES, LANES)), const((LANES, LANES))],
        out_specs=[pl.BlockSpec((tb, LO_W), lambda i: (rev(i), 0)), const((8, LANES))],
        out_shape=[jax.ShapeDtypeStruct((T, LO_W), BF16), jax.ShapeDtypeStruct((8, LANES), F32)],
        scratch_shapes=[pltpu.VMEM((8, LANES), F32)],
        compiler_params=_cparams("arbitrary"),
    )(proj, proj, proj, proj, proj, proj, dqa, dkad, dvad, dqf, dkf, dvf, dqm, dccol, gains, bfor, triu, gm64, gm128)


FOX_SCALE = FOX_HEAD_DIM ** -0.5
AUG_STRIDE = 16
AUG_C = 0
AUG_NEG_C = 3
AUG_STAT = 6


def _fox_head_mask(sub, rows):
    lane = _lane((rows, 2 * LANES))
    main = (lane >= 64 * sub) & (lane < 64 * sub + 64)
    aug = (lane >= LANES + AUG_STRIDE * sub) & (lane < LANES + AUG_STRIDE * (sub + 1))
    return main | aug


def _fox_fwd(q, qaug, k, kaug, v, T, tq):
    nq = T // tq
    tk = tq
    rep = tk // LANES

    def body(q_ref, qa_ref, k_ref, ka_ref, v_ref, o_ref, qab_ref, m_s, acc_s):
        p_, i, j = pl.program_id(0), pl.program_id(1), pl.program_id(2)

        @pl.when(j == 0)
        def _():
            m_s[...] = jnp.full(m_s.shape, NEG, F32)
            acc_s[...] = jnp.zeros_like(acc_s)

        def step(diagonal):
            q2 = jnp.concatenate([q_ref[...], qa_ref[...]], axis=1)
            k2 = jnp.concatenate([k_ref[...], ka_ref[...]], axis=1)
            v2 = jnp.concatenate([v_ref[...], ka_ref[...]], axis=1)
            if diagonal:
                causal = (lax.broadcasted_iota(jnp.int32, (tq, tk), 1) <= lax.broadcasted_iota(jnp.int32, (tq, tk), 0))
            for sub in range(2):
                qh = jnp.where(_fox_head_mask(sub, tq), q2, jnp.zeros_like(q2))
                s = _dot(qh, k2, NT)
                if diagonal:
                    s = jnp.where(causal, s, NEG)
                m_prev = m_s[sub]
                m_next = jnp.maximum(m_prev, jnp.max(s, axis=1, keepdims=True))
                p = jnp.exp(s - jnp.tile(m_next, (1, rep)))
                alpha = jnp.exp(m_prev - m_next)
                m_s[sub] = m_next
                acc_s[sub] = acc_s[sub] * jnp.tile(alpha, (1, 2)) + _dot(p.astype(BF16), v2)

        @pl.when(j == i)
        def _():
            step(True)

        @pl.when(j < i)
        def _():
            step(False)

        @pl.when(j == nq - 1)
        def _():
            lane = _lane((tq, LANES))
            outs = []
            qab = qa_ref[...].astype(F32)
            for sub in range(2):
                acc = acc_s[sub]
                base = AUG_STRIDE * sub
                l = jnp.sum(jnp.where(lane == base + AUG_C, acc[:, LANES:], 0.0), axis=1, keepdims=True)
                outs.append(acc[:, :LANES] / l)
                lse = jnp.max(m_s[sub], axis=1, keepdims=True) + jnp.log(l)
                pieces = _split3(-lse)
                for e in range(3):
                    qab = jnp.where(lane == base + AUG_STAT + e, pieces[e].astype(F32), qab)
            o_ref[...] = jnp.where(lane < 64, outs[0], outs[1]).astype(o_ref.dtype)
            qab_ref[...] = qab.astype(BF16)

    qspec = pl.BlockSpec((tq, LANES), lambda p, i, j: (i, p))
    kspec = pl.BlockSpec((tk, LANES), lambda p, i, j: (jnp.minimum(j, i), p))
    return pl.pallas_call(
        body, name="fox_fwd", grid=(4, nq, nq),
        in_specs=[qspec, qspec, kspec, kspec, kspec],
        out_specs=[qspec, qspec],
        out_shape=[jax.ShapeDtypeStruct((T, 512), BF16), jax.ShapeDtypeStruct((T, 512), BF16)],
        scratch_shapes=[pltpu.VMEM((2, tq, LANES), F32), pltpu.VMEM((2, tq, 2 * LANES), F32)],
        compiler_params=_cparams("parallel", "parallel", "arbitrary"),
    )(q, qaug, k, kaug, v)


def _fox_bwd(q, qaug, k, kaug, v, do, doaug, T, tq):
    nq = T // tq
    tk = tq

    def body(q_ref, qa_ref, k_ref, ka_ref, v_ref, do_ref, doa_ref,
             dq_ref, dqa_ref, dk_ref, dka_ref, dv_ref, dk_s, dv_s):
        p_, j, i = pl.program_id(0), pl.program_id(1), pl.program_id(2)

        @pl.when((j == 0) & (i == 0))
        def _():
            dq_ref[...] = jnp.zeros_like(dq_ref)
            dqa_ref[...] = jnp.zeros_like(dqa_ref)

        @pl.when(i == 0)
        def _():
            dk_s[...] = jnp.zeros_like(dk_s)
            dv_s[...] = jnp.zeros_like(dv_s)

        def step(diagonal):
            q2 = jnp.concatenate([q_ref[...], qa_ref[...]], axis=1)
            k2 = jnp.concatenate([k_ref[...], ka_ref[...]], axis=1)
            v2 = jnp.concatenate([v_ref[...], ka_ref[...]], axis=1)
            do2 = jnp.concatenate([do_ref[...], doa_ref[...]], axis=1)
            if diagonal:
                causal = (lax.broadcasted_iota(jnp.int32, (tq, tk), 1) <= lax.broadcasted_iota(jnp.int32, (tq, tk), 0))
            dqs = []
            for sub in range(2):
                hm = _fox_head_mask(sub, tq)
                qh = jnp.where(hm, q2, jnp.zeros_like(q2))
                doh = jnp.where(hm, do2, jnp.zeros_like(do2))
                s = _dot(qh, k2, NT)
                if diagonal:
                    s = jnp.where(causal, s, NEG)
                p = jnp.exp(s)
                ds = p * _dot(doh, v2, NT)
                dsb = ds.astype(BF16)
                dv_s[...] += _dot(p.astype(BF16), doh[:, :LANES], TN)
                dk_s[...] += _dot(dsb, qh, TN)
                dqs.append(_dot(dsb, k2))
            dq2 = jnp.where(_fox_head_mask(0, tq), dqs[0], dqs[1])
            qrows = pl.ds(pl.multiple_of(i * tq, tq), tq)
            dq_ref[qrows, :] += dq2[:, :LANES] * FOX_SCALE
            dqa_ref[qrows, :] += dq2[:, LANES:]

        @pl.when(i == j)
        def _():
            step(True)

        @pl.when(i > j)
        def _():
            step(False)

        @pl.when(i == nq - 1)
        def _():
            dk_ref[...] = dk_s[:, :LANES]
            dka_ref[...] = dk_s[:, LANES:]
            dv_ref[...] = dv_s[...]

    qspec = pl.BlockSpec((tq, LANES), lambda p, j, i: (jnp.maximum(i, j), p))
    kspec = pl.BlockSpec((tk, LANES), lambda p, j, i: (j, p))
    resident = pl.BlockSpec((T, LANES), lambda p, j, i: (0, p))
    return pl.pallas_call(
        body, name="fox_bwd", grid=(4, nq, nq),
        in_specs=[qspec, qspec, kspec, kspec, kspec, qspec, qspec],
        out_specs=[resident, resident, kspec, kspec, kspec],
        out_shape=[jax.ShapeDtypeStruct((T, 512), F32)] * 5,
        scratch_shapes=[pltpu.VMEM((tk, 2 * LANES), F32), pltpu.VMEM((tk, LANES), F32)],
        compiler_params=_cparams("arbitrary", "arbitrary", "arbitrary"),
    )(q, qaug, k, kaug, v, do, doaug)


SWA_SUB = 4
SWA_TB = SWA_SUB * WINDOW


def _t5_bucket_matrix():
    t = jnp.arange(WINDOW)[:, None] + WINDOW
    s = jnp.arange(2 * WINDOW)[None, :]
    max_exact = REL_BUCKETS // 2
    d = jnp.maximum(t - s, 0)
    df = jnp.maximum(d, 1).astype(F32)
    large = max_exact + (jnp.log(df / max_exact) / math.log(REL_MAX_DIST / max_exact)
                         * (REL_BUCKETS - max_exact)).astype(jnp.int32)
    large = jnp.minimum(large, REL_BUCKETS - 1)
    return jnp.where(d < max_exact, d, large).astype(jnp.int32)


def _swa_bias(rel_bias, bucket):
    def body(rel_ref, bucket_ref, o_ref):
        b = bucket_ref[...]
        for h in range(SWA_HEADS):
            acc = jnp.zeros(b.shape, F32)
            for r in range(REL_BUCKETS):
                acc = jnp.where(b == r, rel_ref[r, h], acc)
            o_ref[h] = acc

    return pl.pallas_call(
        body, name="swa_bias",
        in_specs=[pl.BlockSpec(memory_space=pltpu.SMEM), pl.BlockSpec(memory_space=pltpu.VMEM)],
        out_specs=pl.BlockSpec(memory_space=pltpu.VMEM),
        out_shape=jax.ShapeDtypeStruct((SWA_HEADS, WINDOW, 2 * WINDOW), F32),
    )(rel_bias, bucket)


def _swa_bias_bwd(dbias, bucket):
    def body(db_ref, bucket_ref, o_ref):
        b = bucket_ref[...]
        lane = _lane((1, LANES))
        for r in range(REL_BUCKETS):
            row = jnp.zeros((1, LANES), F32)
            for h in range(SWA_HEADS):
                part = jnp.sum(jnp.where(b == r, db_ref[h], 0.0), axis=0, keepdims=True)
                tot = jnp.sum(part, axis=1, keepdims=True)
                row = jnp.where(lane == h, tot, row)
            o_ref[r:r + 1, :] = row

    return pl.pallas_call(
        body, name="swa_bias_bwd",
        in_specs=[pl.BlockSpec(memory_space=pltpu.VMEM), pl.BlockSpec(memory_space=pltpu.VMEM)],
        out_specs=pl.BlockSpec(memory_space=pltpu.VMEM),
        out_shape=jax.ShapeDtypeStruct((REL_BUCKETS, LANES), F32),
    )(dbias, bucket)


def _swa_valid(r, i):
    t = lax.broadcasted_iota(jnp.int32, (WINDOW, 2 * WINDOW), 0) + WINDOW
    s = lax.broadcasted_iota(jnp.int32, (WINDOW, 2 * WINDOW), 1)
    dist = t - s
    band = (dist >= 0) & (dist < WINDOW)
    if r == 0:
        band = band & ((s >= WINDOW) | (i > 0))
    return band


def _swa_fwd(sinks, q, kad, vad, bias, T):
    nb = T // SWA_TB
    scale = SWA_HEAD_DIM ** -0.5
    W = WINDOW

    def body(sink_ref, q_ref, k_ref, kp_ref, v_ref, vp_ref, bias_ref, o_ref, lse_ref):
        p_, i = pl.program_id(0), pl.program_id(1)
        lane = _lane((W, LANES))
        for r in range(SWA_SUB):
            rs = slice(r * W, (r + 1) * W)
            ps = slice((r - 1) * W, r * W)
            qr = q_ref[rs, :]
            k_own, v_own = k_ref[rs, :], v_ref[rs, :]
            k_prev = kp_ref[...] if r == 0 else k_ref[ps, :]
            v_prev = vp_ref[...] if r == 0 else v_ref[ps, :]
            valid = _swa_valid(r, i)
            outs = []
            for sub in range(2):
                hm = (lane >= 64) if sub else (lane < 64)
                qh = jnp.where(hm, qr, jnp.zeros_like(qr))
                s = jnp.concatenate([_dot(qh, k_prev, NT), _dot(qh, k_own, NT)], axis=1) * scale + bias_ref[sub]
                s = jnp.where(valid, s, NEG)
                sink = sink_ref[2 * p_ + sub]
                m = jnp.maximum(jnp.max(s, axis=1, keepdims=True), sink)
                p = jnp.exp(s - m)
                denom = jnp.sum(p, axis=1, keepdims=True) + jnp.exp(sink - m)
                pn = (p / denom).astype(BF16)
                outs.append(_dot(pn[:, :W], v_prev) + _dot(pn[:, W:], v_own))
                lse_ref[sub, rs, :] = jnp.broadcast_to(m + jnp.log(denom), (W, LANES))
            o_ref[rs, :] = jnp.where(lane < 64, outs[0], outs[1]).astype(o_ref.dtype)

    qspec = pl.BlockSpec((SWA_TB, LANES), lambda p, i: (i, p))
    own = pl.BlockSpec((None, SWA_TB, LANES), lambda p, i: (p // 2, i, 0))
    prev = pl.BlockSpec((None, W, LANES), lambda p, i: (p // 2, jnp.maximum(SWA_SUB * i - 1, 0), 0))
    stat = pl.BlockSpec((2, SWA_TB, LANES), lambda p, i: (p, i, 0))
    return pl.pallas_call(
        body, name="swa_fwd", grid=(4, nb),
        in_specs=[pl.BlockSpec(memory_space=pltpu.SMEM), qspec, own, prev, own, prev,
                  pl.BlockSpec((2, W, 2 * W), lambda p, i: (p, 0, 0))],
        out_specs=[qspec, stat],
        out_shape=[jax.ShapeDtypeStruct((T, 512), BF16), jax.ShapeDtypeStruct((SWA_HEADS, T, LANES), F32)],
        compiler_params=_cparams("parallel", "parallel"),
    )(sinks, q, kad, kad, vad, vad, bias)


def _swa_bwd(sinks, q, kad, vad, bias, do, lse, delta, T):
    nb = T // SWA_TB
    scale = SWA_HEAD_DIM ** -0.5
    W = WINDOW

    def body(sink_ref, q_ref, k_ref, kp_ref, v_ref, vp_ref, bias_ref, do_ref, lse_ref, dl_ref,
             dq_ref, dkad_ref, dvad_ref, dbias_ref, dsk_ref):
        p_, i = pl.program_id(0), pl.program_id(1)
        kvh = p_ // 2
        lane = _lane((W, LANES))

        @pl.when((p_ == 0) & (i == 0))
        def _():
            dkad_ref[...] = jnp.zeros_like(dkad_ref)
            dvad_ref[...] = jnp.zeros_like(dvad_ref)

        @pl.when(i == 0)
        def _():
            dbias_ref[...] = jnp.zeros_like(dbias_ref)
            dsk_ref[...] = jnp.zeros_like(dsk_ref)

        for r in range(SWA_SUB):
            rs = slice(r * W, (r + 1) * W)
            ps = slice((r - 1) * W, r * W)
            qr, dor = q_ref[rs, :], do_ref[rs, :]
            k_own, v_own = k_ref[rs, :], v_ref[rs, :]
            k_prev = kp_ref[...] if r == 0 else k_ref[ps, :]
            v_prev = vp_ref[...] if r == 0 else v_ref[ps, :]
            valid = _swa_valid(r, i)
            own_row = pl.multiple_of(i * SWA_TB + r * W, W)
            dqs = []
            dk_own = jnp.zeros((W, LANES), F32)
            dk_prev = jnp.zeros((W, LANES), F32)
            dv_own = jnp.zeros((W, LANES), F32)
            dv_prev = jnp.zeros((W, LANES), F32)
            for sub in range(2):
                hm = (lane >= 64) if sub else (lane < 64)
                qh = jnp.where(hm, qr, jnp.zeros_like(qr))
                doh = jnp.where(hm, dor, jnp.zeros_like(dor))
                s = jnp.concatenate([_dot(qh, k_prev, NT), _dot(qh, k_own, NT)], axis=1) * scale + bias_ref[sub]
                s = jnp.where(valid, s, NEG)
                lse_b = lse_ref[sub, rs, :]
                dl_b = dl_ref[sub, rs, :]
                p = jnp.exp(s - jnp.tile(lse_b, (1, 2)))
                dp = jnp.concatenate([_dot(doh, v_prev, NT), _dot(doh, v_own, NT)], axis=1)
                ds = p * (dp - jnp.tile(dl_b, (1, 2)))
                dbias_ref[sub] += ds
                sink = sink_ref[2 * p_ + sub]
                dsk_ref[sub:sub + 1, :] += jnp.sum(jnp.exp(sink - lse_b) * dl_b, axis=0, keepdims=True)
                dsb = ds.astype(BF16)
                pb = p.astype(BF16)
                dqs.append((_dot(dsb[:, :W], k_prev) + _dot(dsb[:, W:], k_own)) * scale)
                dk_prev += _dot(dsb[:, :W], qh, TN) * scale
                dk_own += _dot(dsb[:, W:], qh, TN) * scale
                dv_prev += _dot(pb[:, :W], doh, TN)
                dv_own += _dot(pb[:, W:], doh, TN)
            dq_ref[rs, :] = jnp.where(lane < 64, dqs[0], dqs[1])
            dkad_ref[kvh, pl.ds(own_row, W), :] += dk_own
            dvad_ref[kvh, pl.ds(own_row, W), :] += dv_own
            if r == 0:
                @pl.when(i > 0)
                def _():
                    prev_row = pl.multiple_of(i * SWA_TB - W, W)
                    dkad_ref[kvh, pl.ds(prev_row, W), :] += dk_prev
                    dvad_ref[kvh, pl.ds(prev_row, W), :] += dv_prev
            else:
                prev_row = pl.multiple_of(i * SWA_TB + (r - 1) * W, W)
                dkad_ref[kvh, pl.ds(prev_row, W), :] += dk_prev
                dvad_ref[kvh, pl.ds(prev_row, W), :] += dv_prev

    qspec = pl.BlockSpec((SWA_TB, LANES), lambda p, i: (i, p))
    own = pl.BlockSpec((None, SWA_TB, LANES), lambda p, i: (p // 2, i, 0))
    prev = pl.BlockSpec((None, W, LANES), lambda p, i: (p // 2, jnp.maximum(SWA_SUB * i - 1, 0), 0))
    stat = pl.BlockSpec((2, SWA_TB, LANES), lambda p, i: (p, i, 0))
    full = pl.BlockSpec((2, T, LANES), lambda p, i: (0, 0, 0))
    return pl.pallas_call(
        body, name="swa_bwd", grid=(4, nb),
        in_specs=[pl.BlockSpec(memory_space=pltpu.SMEM), qspec, own, prev, own, prev,
                  pl.BlockSpec((2, W, 2 * W), lambda p, i: (p, 0, 0)), qspec, stat, stat],
        out_specs=[qspec, full, full, pl.BlockSpec((2, W, 2 * W), lambda p, i: (p, 0, 0)),
                   pl.BlockSpec((None, 8, LANES), lambda p, i: (p, 0, 0))],
        out_shape=[jax.ShapeDtypeStruct((T, 512), F32), jax.ShapeDtypeStruct((2, T, LANES), F32),
                   jax.ShapeDtypeStruct((2, T, LANES), F32), jax.ShapeDtypeStruct((SWA_HEADS, W, 2 * W), F32),
                   jax.ShapeDtypeStruct((4, 8, LANES), F32)],
        compiler_params=_cparams("arbitrary", "arbitrary"),
    )(sinks, q, kad, kad, vad, vad, bias, do, lse, delta)


def _mem_fwd(q, mk, mv, T, tq):
    scale = MEM_HEAD_DIM ** -0.5

    def body(q_ref, k_ref, v_ref, o_ref, lse_ref):
        s = _dot(q_ref[...], k_ref[...], NT) * scale
        m = jnp.max(s, axis=1, keepdims=True)
        p = jnp.exp(s - m)
        l = jnp.sum(p, axis=1, keepdims=True)
        o_ref[...] = _dot((p / l).astype(BF16), v_ref[...]).astype(o_ref.dtype)
        lse_ref[...] = jnp.broadcast_to(m + jnp.log(l), (tq, LANES))

    qspec = pl.BlockSpec((tq, LANES), lambda h, i: (i, h))
    kspec = pl.BlockSpec((N_MEM, LANES), lambda h, i: (0, h))
    return pl.pallas_call(
        body, name="mem_fwd", grid=(MEM_HEADS, T // tq),
        in_specs=[qspec, kspec, kspec],
        out_specs=[qspec, pl.BlockSpec((None, tq, LANES), lambda h, i: (h, i, 0))],
        out_shape=[jax.ShapeDtypeStruct((T, 512), BF16), jax.ShapeDtypeStruct((MEM_HEADS, T, LANES), F32)],
        compiler_params=_cparams("parallel", "parallel"),
    )(q, mk, mv)


def _mem_bwd(q, mk, mv, do, lse, delta, T, tq):
    scale = MEM_HEAD_DIM ** -0.5
    rep = N_MEM // LANES

    def body(q_ref, k_ref, v_ref, do_ref, lse_ref, dl_ref, dq_ref, dk_ref, dv_ref):
        i = pl.program_id(1)

        @pl.when(i == 0)
        def _():
            dk_ref[...] = jnp.zeros_like(dk_ref)
            dv_ref[...] = jnp.zeros_like(dv_ref)

        qv, dov = q_ref[...], do_ref[...]
        s = _dot(qv, k_ref[...], NT) * scale
        p = jnp.exp(s - jnp.tile(lse_ref[...], (1, rep)))
        dp = _dot(dov, v_ref[...], NT)
        ds = p * (dp - jnp.tile(dl_ref[...], (1, rep)))
        dsb = ds.astype(BF16)
        dq_ref[...] = _dot(dsb, k_ref[...]) * scale
        dk_ref[...] += _dot(dsb, qv, TN) * scale
        dv_ref[...] += _dot(p.astype(BF16), dov, TN)

    qspec = pl.BlockSpec((tq, LANES), lambda h, i: (i, h))
    kspec = pl.BlockSpec((N_MEM, LANES), lambda h, i: (0, h))
    stat = pl.BlockSpec((None, tq, LANES), lambda h, i: (h, i, 0))
    return pl.pallas_call(
        body, name="mem_bwd", grid=(MEM_HEADS, T // tq),
        in_specs=[qspec, kspec, kspec, qspec, stat, stat],
        out_specs=[qspec, kspec, kspec],
        out_shape=[jax.ShapeDtypeStruct((T, 512), F32), jax.ShapeDtypeStruct((N_MEM, 512), F32),
                   jax.ShapeDtypeStruct((N_MEM, 512), F32)],
        compiler_params=_cparams("arbitrary", "arbitrary"),
    )(q, mk, mv, do, lse, delta)


def _mem_prep_fwd(mem, g_mem, w_kv, kn_gain, gm128):
    def body(mem_ref, g_ref, w_ref, kn_ref, gm_ref, memn_o, kv_o, mk_o, mv_o):
        xhat, _ = _rms_rows(mem_ref[...], None)
        memn = (xhat * g_ref[...]).astype(BF16)
        memn_o[...] = memn
        kv = _dot(memn, w_ref[...])
        kv_o[...] = kv
        gm = gm_ref[...]
        for c in range(4):
            sl = slice(c * LANES, (c + 1) * LANES)
            y, _ = _head_norm(kv[:, sl], gm, kn_ref[...])
            mk_o[:, sl] = y.astype(BF16)
        mv_o[...] = kv[:, 512:].astype(BF16)

    vm = pl.BlockSpec(memory_space=pltpu.VMEM)
    return pl.pallas_call(
        body, name="mem_prep_fwd", in_specs=[vm] * 5, out_specs=[vm] * 4,
        out_shape=[jax.ShapeDtypeStruct((N_MEM, D_MODEL), BF16), jax.ShapeDtypeStruct((N_MEM, D_MODEL), F32),
                   jax.ShapeDtypeStruct((N_MEM, 512), BF16), jax.ShapeDtypeStruct((N_MEM, 512), BF16)],
        compiler_params=pltpu.CompilerParams(vmem_limit_bytes=VMEM_LIMIT),
    )(mem, g_mem, w_kv, kn_gain, gm128)


def _mem_prep_bwd(mem, g_mem, memn, kv, w_kv, kn_gain, gm128, dmk, dmv):
    def body(mem_ref, g_ref, memn_ref, kv_ref, w_ref, kn_ref, gm_ref, dmk_ref, dmv_ref, dw_o, dg_o, dkn_o, dkv_s):
        gm = gm_ref[...]
        dkn = jnp.zeros((1, LANES), F32)
        for c in range(4):
            sl = slice(c * LANES, (c + 1) * LANES)
            dx, dg = _head_norm_bwd(dmk_ref[:, sl], kv_ref[:, sl], gm, kn_ref[...])
            dkv_s[:, sl] = dx.astype(BF16)
            dkn = dkn + dg
        dkn_o[...] = dkn
        dkv_s[:, 512:] = dmv_ref[...].astype(BF16)
        dkv = dkv_s[...]
        dw_o[...] = _dot(memn_ref[...], dkv, TN)
        dmemn = _dot(dkv, w_ref[...], NT)
        xhat, _ = _rms_rows(mem_ref[...], None)
        dg_o[...] = jnp.sum(dmemn * xhat, axis=0, keepdims=True)

    vm = pl.BlockSpec(memory_space=pltpu.VMEM)
    return pl.pallas_call(
        body, name="mem_prep_bwd", in_specs=[vm] * 9, out_specs=[vm] * 3,
        out_shape=[jax.ShapeDtypeStruct((D_MODEL, D_MODEL), F32), jax.ShapeDtypeStruct((1, D_MODEL), F32),
                   jax.ShapeDtypeStruct((1, LANES), F32)],
        scratch_shapes=[pltpu.VMEM((N_MEM, D_MODEL), BF16)],
        compiler_params=pltpu.CompilerParams(vmem_limit_bytes=VMEM_LIMIT),
    )(mem, g_mem, memn, kv, w_kv, kn_gain, gm128, dmk, dmv)


SLOT_O = D_MODEL // N_SHARD


def _merge_fwd(proj, b_gate, o3, w3, T, tb):
    def body(gl_ref, bg_ref, oa_ref, of_ref, om_ref, wa_ref, wf_ref, wm_ref, out_ref):
        o_refs = (oa_ref, of_ref, om_ref)
        w_refs = (wa_ref, wf_ref, wm_ref)
        for n in range(N_SHARD):
            acc = jnp.zeros((tb, SLOT_O), F32)
            for b in range(3):
                c0 = b * D_MODEL + n * SLOT_O
                g = jax.nn.sigmoid(gl_ref[:, c0:c0 + SLOT_O] + bg_ref[:, c0:c0 + SLOT_O])
                acc = acc + g * _dot(o_refs[b][...], w_refs[b][n])
            out_ref[:, n * SLOT_O:(n + 1) * SLOT_O] = acc.astype(out_ref.dtype)

    rows = pl.BlockSpec((tb, 512), lambda i: (i, 0))
    wspec = pl.BlockSpec((N_SHARD, 512, SLOT_O), lambda i: (0, 0, 0))
    return pl.pallas_call(
        body, name="merge_fwd", grid=(T // tb,),
        in_specs=[pl.BlockSpec((tb, GATE_W), lambda i: (i, 1)), pl.BlockSpec((1, GATE_W), lambda i: (0, 0)),
                  rows, rows, rows, wspec, wspec, wspec],
        out_specs=pl.BlockSpec((tb, D_MODEL), lambda i: (i, 0)),
        out_shape=jax.ShapeDtypeStruct((T, D_MODEL), BF16),
        compiler_params=_cparams("parallel"),
    )(proj, b_gate, *o3, *w3)


def _merge_bwd(proj, b_gate, o3, w3, dmerged, T, tb):
    heads = (SWA_HEADS, FOX_HEADS, MEM_HEADS)

    def body(gl_ref, bg_ref, oa_ref, of_ref, om_ref, wa_ref, wf_ref, wm_ref, dm_ref,
             dgl_o, doa_o, dof_o, dom_o, dla_o, dlf_o, dlm_o, dwa_o, dwf_o, dwm_o, dbg_o):
        i = pl.program_id(0)
        o_refs = (oa_ref, of_ref, om_ref)
        w_refs = (wa_ref, wf_ref, wm_ref)
        do_refs = (doa_o, dof_o, dom_o)
        dl_refs = (dla_o, dlf_o, dlm_o)
        dw_refs = (dwa_o, dwf_o, dwm_o)

        @pl.when(i == 0)
        def _():
            for r in dw_refs:
                r[...] = jnp.zeros_like(r)
            dbg_o[...] = jnp.zeros_like(dbg_o)

        lane = _lane((tb, LANES))
        for b in range(3):
            ob = o_refs[b][...]
            do = jnp.zeros((tb, 512), F32)
            for n in range(N_SHARD):
                c0 = b * D_MODEL + n * SLOT_O
                g = jax.nn.sigmoid(gl_ref[:, c0:c0 + SLOT_O] + bg_ref[:, c0:c0 + SLOT_O])
                dm = dm_ref[:, n * SLOT_O:(n + 1) * SLOT_O]
                y = _dot(ob, w_refs[b][n])
                dgl = dm * y * g * (1.0 - g)
                dgl_o[:, c0:c0 + SLOT_O] = dgl.astype(dgl_o.dtype)
                dbg_o[:, c0:c0 + SLOT_O] += jnp.sum(dgl, axis=0, keepdims=True)
                dy = (dm * g).astype(BF16)
                do = do + _dot(dy, w_refs[b][n], NT)
                dw_refs[b][n] += _dot(ob, dy, TN)
            do_refs[b][...] = do.astype(BF16)
            prod = do * ob.astype(F32)
            for c in range(4):
                blk = prod[:, c * LANES:(c + 1) * LANES]
                if heads[b] == 8:
                    lo = jnp.sum(jnp.where(lane < 64, blk, 0.0), axis=1, keepdims=True)
                    hi = jnp.sum(jnp.where(lane >= 64, blk, 0.0), axis=1, keepdims=True)
                    if b == 1:
                        aug = jnp.zeros((tb, LANES), F32)
                        for sub, dl in enumerate((lo, hi)):
                            for e, piece in enumerate(_split3(-dl)):
                                aug = jnp.where(lane == AUG_STRIDE * sub + AUG_C + e, piece.astype(F32), aug)
                        dl_refs[b][:, c * LANES:(c + 1) * LANES] = aug.astype(BF16)
                    else:
                        dl_refs[b][2 * c] = jnp.broadcast_to(lo, (tb, LANES))
                        dl_refs[b][2 * c + 1] = jnp.broadcast_to(hi, (tb, LANES))
                else:
                    dl_refs[b][c] = jnp.broadcast_to(jnp.sum(blk, axis=1, keepdims=True), (tb, LANES))

    rows = pl.BlockSpec((tb, 512), lambda i: (i, 0))
    wspec = pl.BlockSpec((N_SHARD, 512, SLOT_O), lambda i: (0, 0, 0))
    stat = lambda h: pl.BlockSpec((h, tb, LANES), lambda i: (0, i, 0))
    return pl.pallas_call(
        body, name="merge_bwd", grid=(T // tb,),
        in_specs=[pl.BlockSpec((tb, GATE_W), lambda i: (i, 1)), pl.BlockSpec((1, GATE_W), lambda i: (0, 0)),
                  rows, rows, rows, wspec, wspec, wspec, pl.BlockSpec((tb, D_MODEL), lambda i: (i, 0))],
        out_specs=[pl.BlockSpec((tb, GATE_W), lambda i: (i, 0)), rows, rows, rows,
                   stat(8), rows, stat(4), wspec, wspec, wspec, pl.BlockSpec((1, GATE_W), lambda i: (0, 0))],
        out_shape=[jax.ShapeDtypeStruct((T, GATE_W), BF16)] + [jax.ShapeDtypeStruct((T, 512), BF16)] * 3
        + [jax.ShapeDtypeStruct((8, T, LANES), F32), jax.ShapeDtypeStruct((T, 512), BF16),
           jax.ShapeDtypeStruct((4, T, LANES), F32)]
        + [jax.ShapeDtypeStruct((N_SHARD, 512, SLOT_O), F32)] * 3 + [jax.ShapeDtypeStruct((1, GATE_W), F32)],
        compiler_params=_cparams("arbitrary"),
    )(proj, b_gate, *o3, *w3, dmerged)


def _local_step(x, mem, tgt, small, wc, w_kv, w_o3, w_out, w_up, w_down):
    T = x.shape[0]
    tm = min(512, T)
    tile2 = lambda v: jnp.tile(v.reshape(1, -1), (1, LANES // v.size))
    gains = jnp.concatenate([tile2(small["qn_swa"]), tile2(small["kn_swa"]), tile2(small["qn_fox"]),
                             tile2(small["kn_fox"]), tile2(small["qn_mem"]), jnp.zeros((3, LANES), F32)], axis=0)
    kn_mem = small["kn_mem"].reshape(1, LANES)
    bfor = jnp.pad(small["b_forget"].reshape(1, -1), ((0, 0), (0, LANES - FOX_HEADS)))
    gm64 = _group_mean_matrix(64)
    gm128 = _group_mean_matrix(128)
    tb_prep = min(256, T)
    ones = jnp.ones((tb_prep, tb_prep), F32)
    tril = jnp.tril(ones).astype(BF16)
    triu = jnp.triu(ones).astype(BF16)
    bucket = _t5_bucket_matrix()
    g_mix, g_mlp, g_mem = small["g_mix"], small["g_mlp"], small["g_mem"]
    b_gate = small["b_gate"]
    sinks = small["sink_swa"].reshape(-1)

    tl = min(1024, T)
    sq = pl.BlockSpec((tl, D_MODEL), lambda i, j, k: (i, j))
    h = _rmsnorm("rms_mix", x, g_mix, tm)
    (proj,) = _matmul(
        "mm_proj", h, wc, dims=NN, grid=(T // tl, PROJ_W // D_MODEL, 1),
        a_spec=pl.BlockSpec((tl, D_MODEL), lambda i, j, k: (i, 0)),
        b_spec=pl.BlockSpec((D_MODEL, D_MODEL), lambda i, j, k: (0, j)),
        acc_shape=(tl, D_MODEL),
        outs=[(jax.ShapeDtypeStruct((T, PROJ_W), F32), sq)],
        epilogue=_epi_store)
    qa, qf, kf, vf, qm, kad, vad, qf_aug, kf_aug = _prep_fwd(proj, gains, bfor, tril, gm64, gm128, T, tb_prep)
    bias = _swa_bias(small["rel_bias"], bucket)
    o_swa, lse_swa = _swa_fwd(sinks, qa, kad, vad, bias, T)
    o_fox, qf_aug_bwd = _fox_fwd(qf, qf_aug, kf, kf_aug, vf, T, tm)
    memn, kv, mk, mv = _mem_prep_fwd(mem, g_mem, w_kv, kn_mem, gm128)
    o_mem, lse_mem = _mem_fwd(qm, mk, mv, T, tm)
    o3 = (o_swa, o_fox, o_mem)
    merged = _merge_fwd(proj, b_gate, o3, w_o3, T, min(256, T))

    def epi_residual(acc, extra_refs, out_refs, ij):
        out_refs[0][...] = extra_refs[0][...] + acc

    row_full = pl.BlockSpec((tm, D_MODEL), lambda i, j, k: (i, 0))
    row_big = pl.BlockSpec((tl, D_MODEL), lambda i, j, k: (i, 0))
    whole = pl.BlockSpec((D_MODEL, D_MODEL), lambda i, j, k: (0, 0))
    (x2,) = _matmul(
        "mm_out", merged, w_out, dims=NN, grid=(T // tl, 1, 1),
        a_spec=row_big, b_spec=whole,
        acc_shape=(tl, D_MODEL), extra=[(x, row_big)],
        outs=[(jax.ShapeDtypeStruct((T, D_MODEL), F32), row_big)], epilogue=epi_residual)
    hm = _rmsnorm("rms_mlp", x2, g_mlp, tm)

    def epi_relu2(acc, extra_refs, out_refs, ij):
        out_refs[0][...] = acc
        r = jnp.maximum(acc, 0.0)
        out_refs[1][...] = (r * r).astype(BF16)

    up, u = _matmul(
        "mm_up", hm, w_up, dims=NN, grid=(T // tl, N_SHARD, 1),
        a_spec=row_big, b_spec=pl.BlockSpec((None, D_MODEL, D_MODEL), lambda i, j, k: (j, 0, 0)),
        acc_shape=(tl, D_MODEL),
        outs=[(jax.ShapeDtypeStruct((T, D_FF), F32), sq), (jax.ShapeDtypeStruct((T, D_FF), BF16), sq)],
        epilogue=epi_relu2)

    def epi_loss(acc, extra_refs, out_refs, ij):
        y = extra_refs[0][...] + acc
        err = y - extra_refs[1][...]
        out_refs[0][...] = err * (1.0 / D_MODEL)
        sq = jnp.sum(jnp.sum(err * err, axis=1, keepdims=True), axis=0, keepdims=True)

        @pl.when(ij[0] == 0)
        def _():
            out_refs[1][...] = jnp.zeros_like(out_refs[1])

        out_refs[1][...] += jnp.broadcast_to(sq, out_refs[1].shape)

    kblk = pl.BlockSpec((tl, D_MODEL), lambda i, j, k: (i, k))
    dy, loss_acc = _matmul(
        "mm_down", u, w_down, dims=NN, grid=(T // tl, 1, N_SHARD),
        a_spec=kblk, b_spec=pl.BlockSpec((D_MODEL, D_MODEL), lambda i, j, k: (k, 0)),
        acc_shape=(tl, D_MODEL), extra=[(x2, row_big), (tgt, row_big)],
        outs=[(jax.ShapeDtypeStruct((T, D_MODEL), F32), row_big),
              (jax.ShapeDtypeStruct((8, LANES), F32), pl.BlockSpec((8, LANES), lambda i, j, k: (0, 0)))],
        epilogue=epi_loss)
    loss = loss_acc[0, 0] * (0.5 / D_MODEL)

    def epi_dup(acc, extra_refs, out_refs, ij):
        out_refs[0][...] = (acc * (2.0 * jnp.maximum(extra_refs[0][...], 0.0))).astype(BF16)

    (dup,) = _matmul(
        "mm_dup", dy, w_down, dims=NT, grid=(T // tl, N_SHARD, 1),
        a_spec=row_big, b_spec=pl.BlockSpec((D_MODEL, D_MODEL), lambda i, j, k: (j, 0)),
        acc_shape=(tl, D_MODEL), extra=[(up, sq)],
        outs=[(jax.ShapeDtypeStruct((T, D_FF), BF16), sq)], epilogue=epi_dup)

    nkt = T // tl
    t_rows = pl.BlockSpec((tl, D_MODEL), lambda i, j, k: (k, i))
    t_cols = pl.BlockSpec((tl, D_MODEL), lambda i, j, k: (k, j))
    (d_w_down,) = _matmul(
        "mm_dw_down", u, dy, dims=TN, grid=(N_SHARD, 1, nkt),
        a_spec=t_rows, b_spec=t_cols, acc_shape=(D_MODEL, D_MODEL),
        outs=[(jax.ShapeDtypeStruct((D_FF, D_MODEL), F32), pl.BlockSpec((D_MODEL, D_MODEL), lambda i, j, k: (i, 0)))],
        epilogue=_epi_store)
    (d_w_up,) = _matmul(
        "mm_dw_up", hm, dup, dims=TN, grid=(1, N_SHARD, nkt),
        a_spec=t_rows, b_spec=t_cols, acc_shape=(D_MODEL, D_MODEL),
        outs=[(jax.ShapeDtypeStruct((N_SHARD, D_MODEL, D_MODEL), F32),
               pl.BlockSpec((None, D_MODEL, D_MODEL), lambda i, j, k: (j, 0, 0)))],
        epilogue=_epi_store)

    def epi_rms_bwd(acc, extra_refs, out_refs, ij):
        dx, dg = _rmsnorm_bwd_rows(acc, extra_refs[0][...], extra_refs[1][...])
        out_refs[0][...] = dx + extra_refs[2][...]

        @pl.when(ij[0] == 0)
        def _():
            out_refs[1][...] = jnp.zeros_like(out_refs[1])

        out_refs[1][...] += dg

    gain_spec = pl.BlockSpec((1, D_MODEL), lambda i, j, k: (0, 0))
    dx2, d_g_mlp = _matmul(
        "mm_dhm", dup, w_up, dims=NT, grid=(T // tl, 1, N_SHARD),
        a_spec=kblk, b_spec=pl.BlockSpec((None, D_MODEL, D_MODEL), lambda i, j, k: (k, 0, 0)),
        acc_shape=(tl, D_MODEL), extra=[(x2, row_big), (g_mlp, gain_spec), (dy, row_big)],
        outs=[(jax.ShapeDtypeStruct((T, D_MODEL), F32), row_big), (jax.ShapeDtypeStruct((1, D_MODEL), F32), gain_spec)],
        epilogue=epi_rms_bwd)

    (dmerged,) = _matmul(
        "mm_dmerged", dx2, w_out, dims=NT, grid=(T // tl, 1, 1),
        a_spec=row_big, b_spec=whole,
        acc_shape=(tl, D_MODEL), outs=[(jax.ShapeDtypeStruct((T, D_MODEL), F32), row_big)], epilogue=_epi_store)
    (d_w_out,) = _matmul(
        "mm_dw_out", merged, dx2, dims=TN, grid=(1, 1, nkt),
        a_spec=t_rows, b_spec=t_cols, acc_shape=(D_MODEL, D_MODEL),
        outs=[(jax.ShapeDtypeStruct((D_MODEL, D_MODEL), F32), whole)],
        epilogue=_epi_store)
    (dgl, do_swa, do_fox, do_mem, dl_swa, do_fox_aug, dl_mem, d_wo_swa, d_wo_fox, d_wo_mem, d_b_gate) = _merge_bwd(
        proj, b_gate, o3, w_o3, dmerged, T, min(256, T))

    dqa, dkad, dvad, dbias, dsk = _swa_bwd(sinks, qa, kad, vad, bias, do_swa, lse_swa, dl_swa, T)
    dqf, dqf_aug, dkf, dkf_aug, dvf = _fox_bwd(qf, qf_aug_bwd, kf, kf_aug, vf, do_fox, do_fox_aug, T, tm)
    dqm, dmk, dmv = _mem_bwd(qm, mk, mv, do_mem, lse_mem, dl_mem, T, tm)
    d_w_kv, d_g_mem, d_kn_mem = _mem_prep_bwd(mem, g_mem, memn, kv, w_kv, kn_mem, gm128, dmk, dmv)
    d_rel = _swa_bias_bwd(dbias, bucket)
    aug_lane = lambda a, lane: a.reshape(T, FOX_HEADS // 2, LANES)[:, :, lane:lane + AUG_STRIDE + 1:AUG_STRIDE]
    dc_queries = aug_lane(dqf_aug, AUG_C).reshape(T, FOX_HEADS)
    dc_keys = aug_lane(dkf_aug, AUG_NEG_C).reshape(T, FOX_HEADS)
    dccol = jnp.pad(dc_queries - dc_keys, ((0, 0), (0, LANES - FOX_HEADS)))
    dlo, gacc = _prep_bwd(proj, dqa, dkad, dvad, dqf, dkf, dvf, dqm, dccol, gains, bfor, triu, gm64, gm128, T, tb_prep)

    def dwc_half(name, dpart):
        (res,) = _matmul(
            name, h, dpart, dims=TN, grid=(1, LO_W // D_MODEL, nkt),
            a_spec=t_rows, b_spec=t_cols, acc_shape=(D_MODEL, D_MODEL),
            outs=[(jax.ShapeDtypeStruct((D_MODEL, LO_W), F32), pl.BlockSpec((D_MODEL, D_MODEL), lambda i, j, k: (0, j)))],
            epilogue=_epi_store)
        return res

    d_wc_lo = dwc_half("mm_dwc_lo", dlo)
    d_wc_gl = dwc_half("mm_dwc_gl", dgl)
    (dh_lo,) = _matmul(
        "mm_dh_lo", dlo, wc, dims=NT, grid=(T // tl, 1, LO_W // D_MODEL),
        a_spec=kblk, b_spec=pl.BlockSpec((D_MODEL, D_MODEL), lambda i, j, k: (0, k)),
        acc_shape=(tl, D_MODEL), outs=[(jax.ShapeDtypeStruct((T, D_MODEL), F32), row_big)], epilogue=_epi_store)

    def epi_dx(acc, extra_refs, out_refs, ij):
        dhh = acc + extra_refs[3][...]
        dx, dg = _rmsnorm_bwd_rows(dhh, extra_refs[0][...], extra_refs[1][...])
        out_refs[0][...] = dx + extra_refs[2][...]

        @pl.when(ij[0] == 0)
        def _():
            out_refs[1][...] = jnp.zeros_like(out_refs[1])

        out_refs[1][...] += dg

    grad_x, d_g_mix = _matmul(
        "mm_dh_gl", dgl, wc, dims=NT, grid=(T // tm, 1, GATE_W // D_MODEL),
        a_spec=pl.BlockSpec((tm, D_MODEL), lambda i, j, k: (i, k)),
        b_spec=pl.BlockSpec((D_MODEL, D_MODEL), lambda i, j, k: (0, k + LO_W // D_MODEL)),
        acc_shape=(tm, D_MODEL), extra=[(x, row_full), (g_mix, gain_spec), (dx2, row_full), (dh_lo, row_full)],
        outs=[(jax.ShapeDtypeStruct((T, D_MODEL), F32), row_full), (jax.ShapeDtypeStruct((1, D_MODEL), F32), gain_spec)],
        epilogue=epi_dx)

    fold64 = lambda row: (row[:64] + row[64:]).reshape(1, 64)
    grads = {
        "g_mix": d_g_mix, "b_gate": d_b_gate, "b_forget": gacc[5, :FOX_HEADS].reshape(1, FOX_HEADS),
        "qn_swa": fold64(gacc[0]), "kn_swa": fold64(gacc[1]),
        "sink_swa": -dsk[:, :2, 0].reshape(1, SWA_HEADS), "rel_bias": d_rel[:, :SWA_HEADS],
        "qn_fox": fold64(gacc[2]), "kn_fox": fold64(gacc[3]),
        "g_mem": d_g_mem, "qn_mem": gacc[4].reshape(1, LANES), "kn_mem": d_kn_mem, "g_mlp": d_g_mlp,
        "wc_lo": d_wc_lo, "wc_gl": d_wc_gl, "w_mem_kv": d_w_kv,
        "w_o_swa": d_wo_swa, "w_o_fox": d_wo_fox, "w_o_mem": d_wo_mem,
        "w_out": d_w_out, "w_mlp_up": d_w_up, "w_mlp_down": d_w_down,
    }
    return loss, grad_x, grads


MESH = pl.DeviceIdType.MESH
ANY = pl.BlockSpec(memory_space=pl.ANY)


def _place():
    x, y, c = lax.axis_index("x"), lax.axis_index("y"), lax.axis_index("c")
    chips = [(1 - x, y), (x, 1 - y), (1 - x, 1 - y)]
    return x, y, c, chips


def _all_gather_shards(slots):
    n = len(slots)

    def body(*refs):
        out = refs[n:2 * n]
        ici_send, ici_recv, d2d_send, d2d_recv = refs[2 * n:]
        x, y, c, chips = _place()
        sibling = (x, y, 1 - c)
        me = 2 * x + y

        def half(a, who):
            hr = slots[a].shape[1] // 2
            return pl.ds(pl.multiple_of(who * hr, hr), hr)

        def ici(a, j, slot, to):
            return pltpu.make_async_remote_copy(
                src_ref=out[a].at[me, half(a, c)], dst_ref=out[a].at[slot, half(a, c)],
                send_sem=ici_send.at[3 * a + j], recv_sem=ici_recv.at[3 * a + j], device_id=to, device_id_type=MESH)

        def d2d(a, j, slot, which):
            part = out[a].at[slot, half(a, which)]
            return pltpu.make_async_remote_copy(
                src_ref=part, dst_ref=part, send_sem=d2d_send.at[3 * a + j], recv_sem=d2d_recv.at[3 * a + j],
                device_id=sibling, device_id_type=MESH)

        sends = [ici(a, j, me, (*chip, c)) for a in range(n) for j, chip in enumerate(chips)]
        for cp in sends:
            cp.start()
        passed = []
        for a in range(n):
            for j, (px, py) in enumerate(chips):
                ici(a, j, 2 * px + py, (px, py, c)).wait_recv()
                cp = d2d(a, j, 2 * px + py, c)
                cp.start()
                passed.append(cp)
        for a in range(n):
            for j, (px, py) in enumerate(chips):
                d2d(a, j, 2 * px + py, 1 - c).wait_recv()
        for cp in sends + passed:
            cp.wait_send()

    return pl.pallas_call(
        body, name="all_gather_weights",
        in_specs=[ANY] * n, out_specs=[ANY] * n,
        out_shape=[jax.ShapeDtypeStruct(s.shape, s.dtype) for s in slots],
        input_output_aliases={a: a for a in range(n)},
        scratch_shapes=[pltpu.SemaphoreType.DMA((3 * n,))] * 4,
    )(*slots)


def _pair_exchange(gs):
    n = len(gs)

    def body(*refs):
        src, stage = refs[:n], refs[n:2 * n]
        send_sem, recv_sem = refs[2 * n:]
        x, y, c, _ = _place()
        copies = []
        for a in range(n):
            hr = gs[a].shape[1] // 2
            theirs = pl.ds(pl.multiple_of((1 - c) * hr, hr), hr)
            copies.append(pltpu.make_async_remote_copy(
                src_ref=src[a].at[:, theirs, :], dst_ref=stage[a], send_sem=send_sem.at[a], recv_sem=recv_sem.at[a],
                device_id=(x, y, 1 - c), device_id_type=MESH))
        for cp in copies:
            cp.start()
        for cp in copies:
            cp.wait()

    return pl.pallas_call(
        body, name="pair_exchange", in_specs=[ANY] * n, out_specs=[ANY] * n,
        out_shape=[jax.ShapeDtypeStruct((N_SHARD, g.shape[1] // 2, g.shape[2]), g.dtype) for g in gs],
        scratch_shapes=[pltpu.SemaphoreType.DMA((n,))] * 2,
    )(*gs)


def _chip_exchange(sums):
    n = len(sums)

    def body(*refs):
        src, got = refs[:n], refs[n:2 * n]
        send_sem, recv_sem = refs[2 * n:]
        x, y, c, chips = _place()
        copies = []
        for a in range(n):
            for j, (px, py) in enumerate(chips):
                copies.append(pltpu.make_async_remote_copy(
                    src_ref=src[a].at[2 * px + py], dst_ref=got[a].at[j],
                    send_sem=send_sem.at[3 * a + j], recv_sem=recv_sem.at[3 * a + j],
                    device_id=(px, py, c), device_id_type=MESH))
        for cp in copies:
            cp.start()
        for cp in copies:
            cp.wait()

    return pl.pallas_call(
        body, name="chip_exchange", in_specs=[ANY] * n, out_specs=[ANY] * n,
        out_shape=[jax.ShapeDtypeStruct((3,) + s.shape[1:], s.dtype) for s in sums],
        scratch_shapes=[pltpu.SemaphoreType.DMA((3 * n,))] * 2,
    )(*sums)


def _pair_gather(fulls):
    n = len(fulls)

    def body(*refs):
        full = refs[n:2 * n]
        send_sem, recv_sem = refs[2 * n:]
        x, y, c, _ = _place()
        copies = []
        for a in range(n):
            hr = fulls[a].shape[0] // 2
            mine = full[a].at[pl.ds(pl.multiple_of(c * hr, hr), hr)]
            copies.append(pltpu.make_async_remote_copy(
                src_ref=mine, dst_ref=mine, send_sem=send_sem.at[a], recv_sem=recv_sem.at[a],
                device_id=(x, y, 1 - c), device_id_type=MESH))
        for cp in copies:
            cp.start()
        for cp in copies:
            cp.wait()

    return pl.pallas_call(
        body, name="pair_gather", in_specs=[ANY] * n, out_specs=[ANY] * n,
        out_shape=[jax.ShapeDtypeStruct(f.shape, f.dtype) for f in fulls],
        input_output_aliases={a: a for a in range(n)},
        scratch_shapes=[pltpu.SemaphoreType.DMA((n,))] * 2,
    )(*fulls)


ELEMENTWISE_BLOCK_ELEMS = 256 * 1024


def _row_block(rows, cols):
    rb = 8
    while rb * 2 * cols <= ELEMENTWISE_BLOCK_ELEMS and rb * 2 <= rows:
        rb *= 2
    return rb


def _pair_sum(name, place, g, stage):
    _, R, C = g.shape
    hr = R // 2
    rb = _row_block(hr, C)
    nb = hr // rb

    def body(place_ref, g_ref, st_ref, sum_bf, own_f32):
        s = pl.program_id(1)
        tot = g_ref[...] + st_ref[...]
        sum_bf[...] = tot.astype(BF16)

        @pl.when(s == place_ref[0])
        def _():
            own_f32[...] = tot

    return pl.pallas_call(
        body, name=name,
        grid_spec=pltpu.PrefetchScalarGridSpec(
            num_scalar_prefetch=1, grid=(nb, N_SHARD),
            in_specs=[pl.BlockSpec((None, rb, C), lambda i, s, pr: (s, pr[1] * nb + i, 0)),
                      pl.BlockSpec((None, rb, C), lambda i, s, pr: (s, i, 0))],
            out_specs=[pl.BlockSpec((None, rb, C), lambda i, s, pr: (s, i, 0)),
                       pl.BlockSpec((rb, C), lambda i, s, pr: (i, 0))]),
        out_shape=[jax.ShapeDtypeStruct((N_SHARD, hr, C), BF16), jax.ShapeDtypeStruct((hr, C), F32)],
        compiler_params=_cparams("arbitrary", "arbitrary"),
    )(place, g, stage)


def _final_sum(name, place, own, got):
    hr, C = own.shape
    rb = _row_block(hr, C)
    nb = hr // rb

    def body(place_ref, own_ref, got_ref, o_ref):
        o_ref[...] = ((own_ref[...] + got_ref[0].astype(F32)) + got_ref[1].astype(F32)) + got_ref[2].astype(F32)

    return pl.pallas_call(
        body, name=name,
        grid_spec=pltpu.PrefetchScalarGridSpec(
            num_scalar_prefetch=1, grid=(nb,),
            in_specs=[pl.BlockSpec((rb, C), lambda i, pr: (i, 0)), pl.BlockSpec((3, rb, C), lambda i, pr: (0, i, 0))],
            out_specs=pl.BlockSpec((rb, C), lambda i, pr: (pr[1] * nb + i, 0))),
        out_shape=jax.ShapeDtypeStruct((2 * hr, C), F32),
        compiler_params=_cparams("arbitrary"),
    )(place, own, got)


def _adamw_math(w, g, m, v):
    m = ADAM_B1 * m + (1.0 - ADAM_B1) * g
    v = ADAM_B2 * v + (1.0 - ADAM_B2) * (g * g)
    m_hat = m / (1.0 - ADAM_B1 ** ADAM_STEP)
    v_hat = v / (1.0 - ADAM_B2 ** ADAM_STEP)
    delta = -ADAM_LR * (m_hat / (jnp.sqrt(v_hat) + ADAM_EPS) + ADAM_WD * w)
    return delta, m, v


def _adamw(name, w, g, m, v):
    R, Cw = w.shape
    Cg = g.shape[1]
    rb = _row_block(R, Cg)

    def body(w_ref, g_ref, m_ref, v_ref, g_o, d_o, m_o, v_o):
        gv = g_ref[...]
        delta, mn, vn = _adamw_math(w_ref[...], gv, m_ref[...], v_ref[...])
        g_o[...] = gv
        d_o[...] = delta
        m_o[...] = mn
        v_o[...] = vn

    blk = pl.BlockSpec((rb, Cg), lambda i: (i, 0))
    return pl.pallas_call(
        body, name=name, grid=(R // rb,),
        in_specs=[blk] * 4, out_specs=[blk] * 4,
        out_shape=[jax.ShapeDtypeStruct((R, Cw), F32)] * 4,
        compiler_params=_cparams("parallel"),
    )(w, g, m, v)


N_DEV = 8
SMALL_ROWS = 64


def _small_allreduce_adamw(g, w, m, v):
    def body(g_ref, w_ref, m_ref, v_ref, all_ref, gs_o, d_o, m_o, v_o, send_sems, recv_sems, local_sem):
        x, y, c, chips = _place()
        me, sibling = (x, y, c), (x, y, 1 - c)

        def rows(px, py, pc):
            return all_ref.at[pl.ds(pl.multiple_of((4 * px + 2 * py + pc) * SMALL_ROWS, SMALL_ROWS), SMALL_ROWS), :]

        def copy(k, block, to, src=None):
            return pltpu.make_async_remote_copy(
                src_ref=rows(*block) if src is None else src, dst_ref=rows(*block),
                send_sem=send_sems.at[k], recv_sem=recv_sems.at[k], device_id=to, device_id_type=MESH)

        mine = pltpu.make_async_copy(g_ref, rows(*me), local_sem)
        mine.start()
        first = [copy(0, me, sibling, src=g_ref)]
        first += [copy(1 + j, me, (*chip, c), src=g_ref) for j, chip in enumerate(chips)]
        for cp in first:
            cp.start()
        passed = [copy(4 + j, (*chip, c), sibling) for j, chip in enumerate(chips)]
        for j, chip in enumerate(chips):
            copy(1 + j, (*chip, c), me).wait_recv()
            passed[j].start()
        copy(0, sibling, me).wait_recv()
        for j, chip in enumerate(chips):
            copy(4 + j, (*chip, 1 - c), me).wait_recv()
        for cp in first + passed:
            cp.wait_send()
        mine.wait()

        tot = all_ref[0:SMALL_ROWS, :]
        for d in range(1, N_DEV):
            tot = tot + all_ref[d * SMALL_ROWS:(d + 1) * SMALL_ROWS, :]
        delta, mn, vn = _adamw_math(w_ref[...], tot, m_ref[...], v_ref[...])
        gs_o[...] = tot
        d_o[...] = delta
        m_o[...] = mn
        v_o[...] = vn

    vm = pl.BlockSpec(memory_space=pltpu.VMEM)
    shp = jax.ShapeDtypeStruct((SMALL_ROWS, LANES), F32)
    res = pl.pallas_call(
        body, name="small_allreduce_adamw", in_specs=[vm] * 4, out_specs=[vm] * 5,
        out_shape=[jax.ShapeDtypeStruct((N_DEV * SMALL_ROWS, LANES), F32), shp, shp, shp, shp],
        scratch_shapes=[pltpu.SemaphoreType.DMA((7,)), pltpu.SemaphoreType.DMA((7,)), pltpu.SemaphoreType.DMA],
    )(g, w, m, v)
    return res[1:]


SMALL_NAMES = ("g_mix", "b_gate", "b_forget", "qn_swa", "kn_swa", "sink_swa", "rel_bias", "qn_fox", "kn_fox",
               "g_mem", "qn_mem", "kn_mem", "g_mlp")
BIG_NAMES = ("w_in", "w_mem_kv", "w_o_swa", "w_o_fox", "w_o_mem", "w_out", "w_mlp_up", "w_mlp_down")
WEIGHT_NAMES = ("g_mix", "w_in", "b_gate", "b_forget", "qn_swa", "kn_swa", "sink_swa", "rel_bias", "qn_fox", "kn_fox",
                "g_mem", "w_mem_kv", "qn_mem", "kn_mem", "w_o_swa", "w_o_fox", "w_o_mem", "w_out", "g_mlp",
                "w_mlp_up", "w_mlp_down")


def _pack_small(parts, extra=None):
    rows = []
    for n in SMALL_NAMES:
        flat = parts[n].reshape(-1).astype(F32)
        flat = jnp.pad(flat, (0, (-flat.size) % LANES))
        rows.append(flat.reshape(-1, LANES))
    if extra is not None:
        rows.append(jnp.pad(extra.reshape(1, 1), ((0, 0), (0, LANES - 1))))
    packed = jnp.concatenate(rows, axis=0)
    return jnp.pad(packed, ((0, SMALL_ROWS - packed.shape[0]), (0, 0)))


def _unpack_small(packed, shapes):
    out, r = {}, 0
    for n in SMALL_NAMES:
        size = math.prod(shapes[n])
        nr = -(-size // LANES)
        out[n] = packed[r:r + nr].reshape(-1)[:size].reshape(shapes[n])
        r += nr
    return out, packed[r, 0]


def _reorder_w_in(w_full):
    seg = lambda a, b: w_full[:, a:b]
    pad = jnp.zeros((w_full.shape[0], C_GL - C_FL - FOX_HEADS), w_full.dtype)
    return jnp.concatenate([seg(0, 512), seg(768, 1280), seg(1280, 1792), seg(1792, 2304), seg(2312, 2824),
                            seg(512, 640), seg(640, 768), seg(2304, 2312), pad, seg(2824, IN_WIDTH)], axis=1)


def _restore_w_in(lo, gl):
    s = lambda a, b: lo[:, a:b]
    return jnp.concatenate([s(C_QA, C_QA + 512), s(C_KA, C_KA + 128), s(C_VA, C_VA + 128), s(C_QF, C_QF + 512),
                            s(C_KF, C_KF + 512), s(C_VF, C_VF + 512), s(C_FL, C_FL + FOX_HEADS), s(C_QM, C_QM + 512),
                            gl], axis=1)


def kernel(x, mem, g_mix, w_in, b_gate, b_forget, qn_swa, kn_swa, sink_swa, rel_bias, qn_fox, kn_fox, g_mem, w_mem_kv, qn_mem, kn_mem, w_o_swa, w_o_fox, w_o_mem, w_out, g_mlp, w_mlp_up, w_mlp_down, loss_target, m_g_mix, m_w_in, m_b_gate, m_b_forget, m_qn_swa, m_kn_swa, m_sink_swa, m_rel_bias, m_qn_fox, m_kn_fox, m_g_mem, m_w_mem_kv, m_qn_mem, m_kn_mem, m_w_o_swa, m_w_o_fox, m_w_o_mem, m_w_out, m_g_mlp, m_w_mlp_up, m_w_mlp_down, v_g_mix, v_w_in, v_b_gate, v_b_forget, v_qn_swa, v_kn_swa, v_sink_swa, v_rel_bias, v_qn_fox, v_kn_fox, v_g_mem, v_w_mem_kv, v_qn_mem, v_kn_mem, v_w_o_swa, v_w_o_fox, v_w_o_mem, v_w_out, v_g_mlp, v_w_mlp_up, v_w_mlp_down):
    given = dict(locals())
    W = {n: given[n] for n in WEIGHT_NAMES}
    M = {n: given["m_" + n] for n in WEIGHT_NAMES}
    V = {n: given["v_" + n] for n in WEIGHT_NAMES}
    pad_in = ((0, 0), (0, IN_SHARD_PAD - IN_SHARD))

    shards = [jnp.pad(w_in[0].astype(BF16), pad_in)] + [W[n][0].astype(BF16) for n in BIG_NAMES[1:]]
    slots = [jnp.broadcast_to(s[None], (N_SHARD,) + s.shape) for s in shards]
    g_in, g_kv, g_oa, g_of, g_om, g_out, g_up, g_down = _all_gather_shards(slots)
    w_full = jnp.concatenate([g_in[s, :, :IN_SHARD] for s in range(N_SHARD)], axis=1)
    wc = _reorder_w_in(w_full)
    small = {n: (W[n] if n == "rel_bias" else W[n].reshape(1, -1)) for n in SMALL_NAMES}

    loss, grad_x, grads = _local_step(
        x[0], mem[0], loss_target[0], small, wc, g_kv.reshape(D_MODEL, D_MODEL), (g_oa, g_of, g_om),
        g_out.reshape(D_MODEL, D_MODEL), g_up, g_down.reshape(D_FF, D_MODEL))

    d_full = _restore_w_in(grads["wc_lo"], grads["wc_gl"])
    d_in = jnp.stack([jnp.pad(d_full[:, s * IN_SHARD:(s + 1) * IN_SHARD], pad_in) for s in range(N_SHARD)])
    slot_rows = lambda a: a.reshape(N_SHARD, a.shape[0] // N_SHARD, a.shape[1])
    local = [d_in, slot_rows(grads["w_mem_kv"]), grads["w_o_swa"], grads["w_o_fox"], grads["w_o_mem"],
             slot_rows(grads["w_out"]), grads["w_mlp_up"], slot_rows(grads["w_mlp_down"])]
    place = jnp.stack([2 * lax.axis_index("x") + lax.axis_index("y"), lax.axis_index("c")]).astype(jnp.int32)
    staged = _pair_exchange(local)
    sums = [_pair_sum("pair_sum_" + n, place, g, st) for n, g, st in zip(BIG_NAMES, local, staged)]
    got = _chip_exchange([s[0] for s in sums])
    halves = [_final_sum("final_sum_" + n, place, s[1], r) for n, s, r in zip(BIG_NAMES, sums, got)]
    summed = _pair_gather(halves)

    out = {}
    for n, g in zip(BIG_NAMES, summed):
        res = _adamw("adamw_" + n, W[n][0], g, M[n][0], V[n][0])
        out[n] = [r.reshape(W[n].shape) for r in res]
    shapes = {n: W[n].shape for n in SMALL_NAMES}
    packed = _small_allreduce_adamw(_pack_small(grads, loss), _pack_small(W), _pack_small(M), _pack_small(V))
    unpacked = [_unpack_small(p, shapes) for p in packed]
    for n in SMALL_NAMES:
        out[n] = [u[0][n] for u in unpacked]
    loss_total = unpacked[0][1]

    return (loss_total, grad_x.reshape(x.shape),
            *[out[n][0] for n in WEIGHT_NAMES], *[out[n][1] for n in WEIGHT_NAMES],
            *[out[n][2] for n in WEIGHT_NAMES], *[out[n][3] for n in WEIGHT_NAMES])
```

```python
import functools
import math

import jax
import jax.numpy as jnp
from jax import lax
from jax.experimental import pallas as pl
from jax.experimental.pallas import tpu as pltpu
from jax.experimental.pallas import tpu_sc as plsc

F32 = jnp.float32
BF16 = jnp.bfloat16

D_MODEL = 1024
N_MEM = 256
SWA_HEADS = 8
SWA_KV_HEADS = 2
SWA_HEAD_DIM = 64
WINDOW = 128
FOX_HEADS = 8
FOX_HEAD_DIM = 64
MEM_HEADS = 4
MEM_HEAD_DIM = 128
D_FF = 4 * D_MODEL
REL_BUCKETS = 32
REL_MAX_DIST = 128
EPS = 1e-6
NEG = -1e30
GATE_W = 3 * D_MODEL
IN_WIDTH = 5896
N_SHARD = 4
IN_SHARD = IN_WIDTH // N_SHARD
IN_SHARD_PAD = 1536

ADAM_LR = 0.001
ADAM_B1 = 0.9
ADAM_B2 = 0.999
ADAM_EPS = 1e-08
ADAM_WD = 0.01
ADAM_STEP = 10

LANES = 128
V7X_VMEM_BYTES = 64 * 1024 * 1024
VMEM_LIMIT = V7X_VMEM_BYTES * 3 // 4

C_QA, C_QF, C_KF, C_VF, C_QM, C_KA, C_VA, C_FL, C_GL = 0, 512, 1024, 1536, 2048, 2560, 2688, 2816, 3072
LO_W = 3072
PROJ_W = 6144

NN = (((1,), (0,)), ((), ()))
NT = (((1,), (1,)), ((), ()))
TN = (((0,), (0,)), ((), ()))


def _dot(a, b, dims=NN):
    return lax.dot_general(a, b, dims, preferred_element_type=F32)


def _cparams(*sem):
    return pltpu.CompilerParams(dimension_semantics=sem, vmem_limit_bytes=VMEM_LIMIT)


def _split3(a):
    hi = a.astype(BF16)
    r1 = a - hi.astype(F32)
    mid = r1.astype(BF16)
    lo = (r1 - mid.astype(F32)).astype(BF16)
    return hi, mid, lo


def _dot3_right(a, g):
    hi, mid, lo = _split3(a)
    return _dot(hi, g) + _dot(mid, g) + _dot(lo, g)


def _dot3_left(g, a):
    hi, mid, lo = _split3(a)
    return _dot(g, hi) + _dot(g, mid) + _dot(g, lo)


def _group_mean_matrix(d):
    r = jnp.arange(LANES)
    return jnp.where((r[:, None] // d) == (r[None, :] // d), 1.0 / d, 0.0).astype(BF16)


def _lane(shape):
    return lax.broadcasted_iota(jnp.int32, shape, len(shape) - 1)


def _matmul(name, a, b, *, dims, grid, a_spec, b_spec, acc_shape, outs, epilogue, extra=()):
    nk = grid[2]
    n_extra = len(extra)

    def body(a_ref, b_ref, *rest):
        extra_refs = rest[:n_extra]
        out_refs = rest[n_extra:n_extra + len(outs)]
        i, j, k = pl.program_id(0), pl.program_id(1), pl.program_id(2)
        part = _dot(a_ref[...].astype(BF16), b_ref[...].astype(BF16), dims)
        if nk == 1:
            epilogue(part, extra_refs, out_refs, (i, j))
            return
        acc_ref = rest[-1]

        @pl.when(k == 0)
        def _():
            acc_ref[...] = part

        @pl.when((k > 0) & (k < nk - 1))
        def _():
            acc_ref[...] += part

        @pl.when(k == nk - 1)
        def _():
            epilogue(acc_ref[...] + part, extra_refs, out_refs, (i, j))

    res = pl.pallas_call(
        body,
        name=name,
        grid=grid,
        in_specs=[a_spec, b_spec] + [s for _, s in extra],
        out_specs=[s for _, s in outs],
        out_shape=[s for s, _ in outs],
        scratch_shapes=[pltpu.VMEM(acc_shape, F32)] if nk > 1 else [],
        compiler_params=_cparams("arbitrary", "arbitrary", "arbitrary"),
    )(a, b, *[x for x, _ in extra])
    return res


def _epi_store(acc, extra_refs, out_refs, ij):
    out_refs[0][...] = acc.astype(out_refs[0].dtype)


def _rms_rows(x, g):
    r = lax.rsqrt(jnp.mean(x * x, axis=-1, keepdims=True) + EPS)
    return x * r, r


def _rmsnorm_bwd_rows(dh, x, g):
    xhat, r = _rms_rows(x, g)
    dxh = dh * g
    dx = r * (dxh - xhat * jnp.mean(dxh * xhat, axis=-1, keepdims=True))
    return dx, jnp.sum(dh * xhat, axis=0, keepdims=True)


def _rmsnorm(name, x, g, tb):
    T, Dm = x.shape

    def body(x_ref, g_ref, o_ref):
        xhat, _ = _rms_rows(x_ref[...], None)
        o_ref[...] = (xhat * g_ref[...]).astype(o_ref.dtype)

    return pl.pallas_call(
        body, name=name, grid=(T // tb,),
        in_specs=[pl.BlockSpec((tb, Dm), lambda i: (i, 0)), pl.BlockSpec((1, Dm), lambda i: (0, 0))],
        out_specs=pl.BlockSpec((tb, Dm), lambda i: (i, 0)),
        out_shape=jax.ShapeDtypeStruct((T, Dm), BF16),
        compiler_params=_cparams("parallel"),
    )(x, g)


def _head_norm(x, gm, gain):
    ms = _dot3_right(x * x, gm)
    r = lax.rsqrt(ms + EPS)
    return x * r * gain, x * r


def _head_norm_bwd(dy, x, gm, gain):
    ms = _dot3_right(x * x, gm)
    r = lax.rsqrt(ms + EPS)
    xhat = x * r
    dxh = dy * gain
    dx = r * (dxh - xhat * _dot3_right(dxh * xhat, gm))
    return dx, jnp.sum(dy * xhat, axis=0, keepdims=True)


def _log_sigmoid(z):
    return jnp.minimum(z, 0.0) - jnp.log(1.0 + jnp.exp(-jnp.abs(z)))


def _prep_fwd(proj, gains, bfor, tril, gm64, gm128, T, tb):
    nb = T // tb

    def body(qa_ref, qf_ref, kf_ref, vf_ref, qm_ref, ka_ref, va_ref, fl_ref, gains_ref, bfor_ref, tril_ref,
             gm64_ref, gm128_ref,
             qa_o, qf_o, kf_o, vf_o, qm_o, kad_o, vad_o, qaug_o, kaug_o, carry):
        i = pl.program_id(0)
        gm64v = gm64_ref[...]
        gm128v = gm128_ref[...]
        lane = _lane((tb, LANES))

        def norm512(src, dst, row, gm, scale=1.0):
            gain = gains_ref[row:row + 1, :]
            for c in range(4):
                sl = slice(c * LANES, (c + 1) * LANES)
                y, _ = _head_norm(src[:, sl], gm, gain)
                dst[:, sl] = (y * scale).astype(dst.dtype)

        norm512(qa_ref, qa_o, 0, gm64v)
        norm512(qf_ref, qf_o, 2, gm64v, FOX_SCALE)
        norm512(kf_ref, kf_o, 3, gm64v)
        norm512(qm_ref, qm_o, 4, gm128v)
        vf_o[...] = vf_ref[...].astype(vf_o.dtype)

        ka_n, _ = _head_norm(ka_ref[...], gm64v, gains_ref[1:2, :])
        ka_r = pltpu.roll(ka_n, 64, 1)
        va = va_ref[...]
        va_r = pltpu.roll(va, 64, 1)
        lo = lane < 64
        kad_o[0] = jnp.where(lo, ka_n, ka_r).astype(kad_o.dtype)
        kad_o[1] = jnp.where(lo, ka_r, ka_n).astype(kad_o.dtype)
        vad_o[0] = jnp.where(lo, va, va_r).astype(vad_o.dtype)
        vad_o[1] = jnp.where(lo, va_r, va).astype(vad_o.dtype)

        @pl.when(i == 0)
        def _():
            carry[...] = jnp.zeros_like(carry)

        logf = jnp.where(lane < FOX_HEADS, _log_sigmoid(fl_ref[...] + bfor_ref[...]), 0.0)
        c = _dot3_left(tril_ref[...], logf) + carry[0:1, :]
        carry[...] = jnp.broadcast_to(c[tb - 1:tb, :], carry.shape)
        for pair in range(FOX_HEADS // 2):
            qaug = jnp.zeros((tb, LANES), F32)
            kaug = jnp.zeros((tb, LANES), F32)
            for sub in range(2):
                col = jnp.sum(jnp.where(lane == 2 * pair + sub, c, 0.0), axis=1, keepdims=True)
                pieces = [p.astype(F32) for p in _split3(col)]
                base = AUG_STRIDE * sub
                for e in range(3):
                    qaug = jnp.where(lane == base + AUG_C + e, pieces[e], qaug)
                    kaug = jnp.where(lane == base + AUG_NEG_C + e, -pieces[e], kaug)
                qaug = jnp.where((lane >= base + AUG_NEG_C) & (lane < base + AUG_NEG_C + 3), 1.0, qaug)
                ones_k = ((lane >= base + AUG_C) & (lane < base + AUG_C + 3)) | (
                    (lane >= base + AUG_STAT) & (lane < base + AUG_STAT + 3))
                kaug = jnp.where(ones_k, 1.0, kaug)
            sl = slice(pair * LANES, (pair + 1) * LANES)
            qaug_o[:, sl] = qaug.astype(BF16)
            kaug_o[:, sl] = kaug.astype(BF16)

    def seg(width, start):
        return pl.BlockSpec((tb, width), lambda i, s=start // width: (i, s))

    const = lambda shape: pl.BlockSpec(shape, lambda i: tuple(0 for _ in shape))
    rows512 = pl.BlockSpec((tb, 512), lambda i: (i, 0))
    outs = pl.pallas_call(
        body, name="prep_fwd", grid=(nb,),
        in_specs=[seg(512, C_QA), seg(512, C_QF), seg(512, C_KF), seg(512, C_VF), seg(512, C_QM),
                  seg(128, C_KA), seg(128, C_VA), seg(128, C_FL),
                  const((8, LANES)), const((1, LANES)), const((tb, tb)), const((LANES, LANES)), const((LANES, LANES))],
        out_specs=[rows512, rows512, rows512, rows512, rows512,
                   pl.BlockSpec((2, tb, LANES), lambda i: (0, i, 0)), pl.BlockSpec((2, tb, LANES), lambda i: (0, i, 0)),
                   rows512, rows512],
        out_shape=[jax.ShapeDtypeStruct((T, 512), BF16)] * 5
        + [jax.ShapeDtypeStruct((2, T, LANES), BF16)] * 2
        + [jax.ShapeDtypeStruct((T, 512), BF16)] * 2,
        scratch_shapes=[pltpu.VMEM((8, LANES), F32)],
        compiler_params=_cparams("arbitrary"),
    )(proj, proj, proj, proj, proj, proj, proj, proj, gains, bfor, tril, gm64, gm128)
    return outs


def _prep_bwd(proj, dqa, dkad, dvad, dqf, dkf, dvf, dqm, dccol, gains, bfor, triu, gm64, gm128, T, tb):
    nb = T // tb

    def body(qa_ref, qf_ref, kf_ref, qm_ref, ka_ref, fl_ref,
             dqa_ref, dkad_ref, dvad_ref, dqf_ref, dkf_ref, dvf_ref, dqm_ref, dc_ref,
             gains_ref, bfor_ref, triu_ref, gm64_ref, gm128_ref,
             dlo_o, gacc_o, carry):
        i = pl.program_id(0)
        gm64v = gm64_ref[...]
        gm128v = gm128_ref[...]
        lane = _lane((tb, LANES))

        @pl.when(i == 0)
        def _():
            carry[...] = jnp.zeros_like(carry)
            gacc_o[...] = jnp.zeros_like(gacc_o)

        def norm512_bwd(dsrc, xsrc, col0, row, gm):
            gain = gains_ref[row:row + 1, :]
            gsum = jnp.zeros((1, LANES), F32)
            for c in range(4):
                sl = slice(c * LANES, (c + 1) * LANES)
                dx, dg = _head_norm_bwd(dsrc[:, sl], xsrc[:, sl], gm, gain)
                dlo_o[:, col0 + c * LANES:col0 + (c + 1) * LANES] = dx.astype(dlo_o.dtype)
                gsum = gsum + dg
            gacc_o[row:row + 1, :] += gsum

        norm512_bwd(dqa_ref, qa_ref, C_QA, 0, gm64v)
        norm512_bwd(dqf_ref, qf_ref, C_QF, 2, gm64v)
        norm512_bwd(dkf_ref, kf_ref, C_KF, 3, gm64v)
        norm512_bwd(dqm_ref, qm_ref, C_QM, 4, gm128v)
        dlo_o[:, C_VF:C_VF + 512] = dvf_ref[...].astype(dlo_o.dtype)

        lo = lane < 64

        def fold(ref):
            f0 = ref[0] + pltpu.roll(ref[0], 64, 1)
            f1 = ref[1] + pltpu.roll(ref[1], 64, 1)
            return jnp.where(lo, f0, f1)

        dka, dg = _head_norm_bwd(fold(dkad_ref), ka_ref[...], gm64v, gains_ref[1:2, :])
        gacc_o[1:2, :] += dg
        dlo_o[:, C_KA:C_KA + LANES] = dka.astype(dlo_o.dtype)
        dlo_o[:, C_VA:C_VA + LANES] = fold(dvad_ref).astype(dlo_o.dtype)

        dc = dc_ref[...]
        dlogf = _dot3_left(triu_ref[...], dc) + carry[0:1, :]
        carry[...] = jnp.broadcast_to(dlogf[0:1, :], carry.shape)
        z = fl_ref[...] + bfor_ref[...]
        dfl = jnp.where(lane < FOX_HEADS, dlogf / (1.0 + jnp.exp(z)), 0.0)
        gacc_o[5:6, :] += jnp.sum(dfl, axis=0, keepdims=True)
        dlo_o[:, C_FL:C_FL + LANES] = dfl.astype(dlo_o.dtype)
        dlo_o[:, C_FL + LANES:C_FL + 2 * LANES] = jnp.zeros((tb, LANES), dlo_o.dtype)

    rev = lambda i: nb - 1 - i

    def seg(width, start):
        return pl.BlockSpec((tb, width), lambda i, s=start // width: (rev(i), s))

    const = lambda shape: pl.BlockSpec(shape, lambda i: tuple(0 for _ in shape))
    rows512 = pl.BlockSpec((tb, 512), lambda i: (rev(i), 0))
    dup = pl.BlockSpec((2, tb, LANES), lambda i: (0, rev(i), 0))
    return pl.pallas_call(
        body, name="prep_bwd", grid=(nb,),
        in_specs=[seg(512, C_QA), seg(512, C_QF), seg(512, C_KF), seg(512, C_QM), seg(128, C_KA), seg(128, C_FL),
                  rows512, dup, dup, rows512, rows512, rows512, rows512,
                  pl.BlockSpec((tb, LANES), lambda i: (rev(i), 0)),
                  const((8, LANES)), const((1, LANES)), const((tb, tb)), const((LANES, LANES)), const((LANES, LANES))],
        out_specs=[pl.BlockSpec((tb, LO_W), lambda i: (rev(i), 0)), const((8, LANES))],
        out_shape=[jax.ShapeDtypeStruct((T, LO_W), BF16), jax.ShapeDtypeStruct((8, LANES), F32)],
        scratch_shapes=[pltpu.VMEM((8, LANES), F32)],
        compiler_params=_cparams("arbitrary"),
    )(proj, proj, proj, proj, proj, proj, dqa, dkad, dvad, dqf, dkf, dvf, dqm, dccol, gains, bfor, triu, gm64, gm128)


FOX_SCALE = FOX_HEAD_DIM ** -0.5
AUG_STRIDE = 16
AUG_C = 0
AUG_NEG_C = 3
AUG_STAT = 6


def _fox_head_mask(sub, rows):
    lane = _lane((rows, 2 * LANES))
    main = (lane >= 64 * sub) & (lane < 64 * sub + 64)
    aug = (lane >= LANES + AUG_STRIDE * sub) & (lane < LANES + AUG_STRIDE * (sub + 1))
    return main | aug


def _fox_fwd(q, qaug, k, kaug, v, T, tq):
    nq = T // tq
    tk = tq
    rep = tk // LANES

    def body(q_ref, qa_ref, k_ref, ka_ref, v_ref, o_ref, qab_ref, m_s, acc_s):
        p_, i, j = pl.program_id(0), pl.program_id(1), pl.program_id(2)

        @pl.when(j == 0)
        def _():
            m_s[...] = jnp.full(m_s.shape, NEG, F32)
            acc_s[...] = jnp.zeros_like(acc_s)

        def step(diagonal):
            q2 = jnp.concatenate([q_ref[...], qa_ref[...]], axis=1)
            k2 = jnp.concatenate([k_ref[...], ka_ref[...]], axis=1)
            v2 = jnp.concatenate([v_ref[...], ka_ref[...]], axis=1)
            if diagonal:
                causal = (lax.broadcasted_iota(jnp.int32, (tq, tk), 1) <= lax.broadcasted_iota(jnp.int32, (tq, tk), 0))
            for sub in range(2):
                qh = jnp.where(_fox_head_mask(sub, tq), q2, jnp.zeros_like(q2))
                s = _dot(qh, k2, NT)
                if diagonal:
                    s = jnp.where(causal, s, NEG)
                m_prev = m_s[sub]
                m_next = jnp.maximum(m_prev, jnp.max(s, axis=1, keepdims=True))
                p = jnp.exp(s - jnp.tile(m_next, (1, rep)))
                alpha = jnp.exp(m_prev - m_next)
                m_s[sub] = m_next
                acc_s[sub] = acc_s[sub] * jnp.tile(alpha, (1, 2)) + _dot(p.astype(BF16), v2)

        @pl.when(j == i)
        def _():
            step(True)

        @pl.when(j < i)
        def _():
            step(False)

        @pl.when(j == nq - 1)
        def _():
            lane = _lane((tq, LANES))
            outs = []
            qab = qa_ref[...].astype(F32)
            for sub in range(2):
                acc = acc_s[sub]
                base = AUG_STRIDE * sub
                l = jnp.sum(jnp.where(lane == base + AUG_C, acc[:, LANES:], 0.0), axis=1, keepdims=True)
                outs.append(acc[:, :LANES] / l)
                lse = jnp.max(m_s[sub], axis=1, keepdims=True) + jnp.log(l)
                pieces = _split3(-lse)
                for e in range(3):
                    qab = jnp.where(lane == base + AUG_STAT + e, pieces[e].astype(F32), qab)
            o_ref[...] = jnp.where(lane < 64, outs[0], outs[1]).astype(o_ref.dtype)
            qab_ref[...] = qab.astype(BF16)

    qspec = pl.BlockSpec((tq, LANES), lambda p, i, j: (i, p))
    kspec = pl.BlockSpec((tk, LANES), lambda p, i, j: (jnp.minimum(j, i), p))
    return pl.pallas_call(
        body, name="fox_fwd", grid=(4, nq, nq),
        in_specs=[qspec, qspec, kspec, kspec, kspec],
        out_specs=[qspec, qspec],
        out_shape=[jax.ShapeDtypeStruct((T, 512), BF16), jax.ShapeDtypeStruct((T, 512), BF16)],
        scratch_shapes=[pltpu.VMEM((2, tq, LANES), F32), pltpu.VMEM((2, tq, 2 * LANES), F32)],
        compiler_params=_cparams("parallel", "parallel", "arbitrary"),
    )(q, qaug, k, kaug, v)


def _fox_bwd(q, qaug, k, kaug, v, do, doaug, T, tq):
    nq = T // tq
    tk = tq

    def body(q_ref, qa_ref, k_ref, ka_ref, v_ref, do_ref, doa_ref,
             dq_ref, dqa_ref, dk_ref, dka_ref, dv_ref, dk_s, dv_s):
        p_, j, i = pl.program_id(0), pl.program_id(1), pl.program_id(2)

        @pl.when((j == 0) & (i == 0))
        def _():
            dq_ref[...] = jnp.zeros_like(dq_ref)
            dqa_ref[...] = jnp.zeros_like(dqa_ref)

        @pl.when(i == 0)
        def _():
            dk_s[...] = jnp.zeros_like(dk_s)
            dv_s[...] = jnp.zeros_like(dv_s)

        def step(diagonal):
            q2 = jnp.concatenate([q_ref[...], qa_ref[...]], axis=1)
            k2 = jnp.concatenate([k_ref[...], ka_ref[...]], axis=1)
            v2 = jnp.concatenate([v_ref[...], ka_ref[...]], axis=1)
            do2 = jnp.concatenate([do_ref[...], doa_ref[...]], axis=1)
            if diagonal:
                causal = (lax.broadcasted_iota(jnp.int32, (tq, tk), 1) <= lax.broadcasted_iota(jnp.int32, (tq, tk), 0))
            dqs = []
            for sub in range(2):
                hm = _fox_head_mask(sub, tq)
                qh = jnp.where(hm, q2, jnp.zeros_like(q2))
                doh = jnp.where(hm, do2, jnp.zeros_like(do2))
                s = _dot(qh, k2, NT)
                if diagonal:
                    s = jnp.where(causal, s, NEG)
                p = jnp.exp(s)
                ds = p * _dot(doh, v2, NT)
                dsb = ds.astype(BF16)
                dv_s[...] += _dot(p.astype(BF16), doh[:, :LANES], TN)
                dk_s[...] += _dot(dsb, qh, TN)
                dqs.append(_dot(dsb, k2))
            dq2 = jnp.where(_fox_head_mask(0, tq), dqs[0], dqs[1])
            qrows = pl.ds(pl.multiple_of(i * tq, tq), tq)
            dq_ref[qrows, :] += dq2[:, :LANES] * FOX_SCALE
            dqa_ref[qrows, :] += dq2[:, LANES:]

        @pl.when(i == j)
        def _():
            step(True)

        @pl.when(i > j)
        def _():
            step(False)

        @pl.when(i == nq - 1)
        def _():
            dk_ref[...] = dk_s[:, :LANES]
            dka_ref[...] = dk_s[:, LANES:]
            dv_ref[...] = dv_s[...]

    qspec = pl.BlockSpec((tq, LANES), lambda p, j, i: (jnp.maximum(i, j), p))
    kspec = pl.BlockSpec((tk, LANES), lambda p, j, i: (j, p))
    resident = pl.BlockSpec((T, LANES), lambda p, j, i: (0, p))
    return pl.pallas_call(
        body, name="fox_bwd", grid=(4, nq, nq),
        in_specs=[qspec, qspec, kspec, kspec, kspec, qspec, qspec],
        out_specs=[resident, resident, kspec, kspec, kspec],
        out_shape=[jax.ShapeDtypeStruct((T, 512), F32)] * 5,
        scratch_shapes=[pltpu.VMEM((tk, 2 * LANES), F32), pltpu.VMEM((tk, LANES), F32)],
        compiler_params=_cparams("arbitrary", "arbitrary", "arbitrary"),
    )(q, qaug, k, kaug, v, do, doaug)


SWA_SUB = 4
SWA_TB = SWA_SUB * WINDOW


def _t5_bucket_matrix():
    t = jnp.arange(WINDOW)[:, None] + WINDOW
    s = jnp.arange(2 * WINDOW)[None, :]
    max_exact = REL_BUCKETS // 2
    d = jnp.maximum(t - s, 0)
    df = jnp.maximum(d, 1).astype(F32)
    large = max_exact + (jnp.log(df / max_exact) / math.log(REL_MAX_DIST / max_exact)
                         * (REL_BUCKETS - max_exact)).astype(jnp.int32)
    large = jnp.minimum(large, REL_BUCKETS - 1)
    return jnp.where(d < max_exact, d, large).astype(jnp.int32)


def _swa_bias(rel_bias, bucket):
    def body(rel_ref, bucket_ref, o_ref):
        b = bucket_ref[...]
        for h in range(SWA_HEADS):
            acc = jnp.zeros(b.shape, F32)
            for r in range(REL_BUCKETS):
                acc = jnp.where(b == r, rel_ref[r, h], acc)
            o_ref[h] = acc

    return pl.pallas_call(
        body, name="swa_bias",
        in_specs=[pl.BlockSpec(memory_space=pltpu.SMEM), pl.BlockSpec(memory_space=pltpu.VMEM)],
        out_specs=pl.BlockSpec(memory_space=pltpu.VMEM),
        out_shape=jax.ShapeDtypeStruct((SWA_HEADS, WINDOW, 2 * WINDOW), F32),
    )(rel_bias, bucket)


def _swa_bias_bwd(dbias, bucket):
    def body(db_ref, bucket_ref, o_ref):
        b = bucket_ref[...]
        lane = _lane((1, LANES))
        for r in range(REL_BUCKETS):
            row = jnp.zeros((1, LANES), F32)
            for h in range(SWA_HEADS):
                part = jnp.sum(jnp.where(b == r, db_ref[h], 0.0), axis=0, keepdims=True)
                tot = jnp.sum(part, axis=1, keepdims=True)
                row = jnp.where(lane == h, tot, row)
            o_ref[r:r + 1, :] = row

    return pl.pallas_call(
        body, name="swa_bias_bwd",
        in_specs=[pl.BlockSpec(memory_space=pltpu.VMEM), pl.BlockSpec(memory_space=pltpu.VMEM)],
        out_specs=pl.BlockSpec(memory_space=pltpu.VMEM),
        out_shape=jax.ShapeDtypeStruct((REL_BUCKETS, LANES), F32),
    )(dbias, bucket)


def _swa_valid(r, i):
    t = lax.broadcasted_iota(jnp.int32, (WINDOW, 2 * WINDOW), 0) + WINDOW
    s = lax.broadcasted_iota(jnp.int32, (WINDOW, 2 * WINDOW), 1)
    dist = t - s
    band = (dist >= 0) & (dist < WINDOW)
    if r == 0:
        band = band & ((s >= WINDOW) | (i > 0))
    return band


def _swa_fwd(sinks, q, kad, vad, bias, T):
    nb = T // SWA_TB
    scale = SWA_HEAD_DIM ** -0.5
    W = WINDOW

    def body(sink_ref, q_ref, k_ref, kp_ref, v_ref, vp_ref, bias_ref, o_ref, lse_ref):
        p_, i = pl.program_id(0), pl.program_id(1)
        lane = _lane((W, LANES))
        for r in range(SWA_SUB):
            rs = slice(r * W, (r + 1) * W)
            ps = slice((r - 1) * W, r * W)
            qr = q_ref[rs, :]
            k_own, v_own = k_ref[rs, :], v_ref[rs, :]
            k_prev = kp_ref[...] if r == 0 else k_ref[ps, :]
            v_prev = vp_ref[...] if r == 0 else v_ref[ps, :]
            valid = _swa_valid(r, i)
            outs = []
            for sub in range(2):
                hm = (lane >= 64) if sub else (lane < 64)
                qh = jnp.where(hm, qr, jnp.zeros_like(qr))
                s = jnp.concatenate([_dot(qh, k_prev, NT), _dot(qh, k_own, NT)], axis=1) * scale + bias_ref[sub]
                s = jnp.where(valid, s, NEG)
                sink = sink_ref[2 * p_ + sub]
                m = jnp.maximum(jnp.max(s, axis=1, keepdims=True), sink)
                p = jnp.exp(s - m)
                denom = jnp.sum(p, axis=1, keepdims=True) + jnp.exp(sink - m)
                pn = (p / denom).astype(BF16)
                outs.append(_dot(pn[:, :W], v_prev) + _dot(pn[:, W:], v_own))
                lse_ref[sub, rs, :] = jnp.broadcast_to(m + jnp.log(denom), (W, LANES))
            o_ref[rs, :] = jnp.where(lane < 64, outs[0], outs[1]).astype(o_ref.dtype)

    qspec = pl.BlockSpec((SWA_TB, LANES), lambda p, i: (i, p))
    own = pl.BlockSpec((None, SWA_TB, LANES), lambda p, i: (p // 2, i, 0))
    prev = pl.BlockSpec((None, W, LANES), lambda p, i: (p // 2, jnp.maximum(SWA_SUB * i - 1, 0), 0))
    stat = pl.BlockSpec((2, SWA_TB, LANES), lambda p, i: (p, i, 0))
    return pl.pallas_call(
        body, name="swa_fwd", grid=(4, nb),
        in_specs=[pl.BlockSpec(memory_space=pltpu.SMEM), qspec, own, prev, own, prev,
                  pl.BlockSpec((2, W, 2 * W), lambda p, i: (p, 0, 0))],
        out_specs=[qspec, stat],
        out_shape=[jax.ShapeDtypeStruct((T, 512), BF16), jax.ShapeDtypeStruct((SWA_HEADS, T, LANES), F32)],
        compiler_params=_cparams("parallel", "parallel"),
    )(sinks, q, kad, kad, vad, vad, bias)


def _swa_bwd(sinks, q, kad, vad, bias, do, lse, delta, T):
    nb = T // SWA_TB
    scale = SWA_HEAD_DIM ** -0.5
    W = WINDOW

    def body(sink_ref, q_ref, k_ref, kp_ref, v_ref, vp_ref, bias_ref, do_ref, lse_ref, dl_ref,
             dq_ref, dkad_ref, dvad_ref, dbias_ref, dsk_ref):
        p_, i = pl.program_id(0), pl.program_id(1)
        kvh = p_ // 2
        lane = _lane((W, LANES))

        @pl.when((p_ == 0) & (i == 0))
        def _():
            dkad_ref[...] = jnp.zeros_like(dkad_ref)
            dvad_ref[...] = jnp.zeros_like(dvad_ref)

        @pl.when(i == 0)
        def _():
            dbias_ref[...] = jnp.zeros_like(dbias_ref)
            dsk_ref[...] = jnp.zeros_like(dsk_ref)

        for r in range(SWA_SUB):
            rs = slice(r * W, (r + 1) * W)
            ps = slice((r - 1) * W, r * W)
            qr, dor = q_ref[rs, :], do_ref[rs, :]
            k_own, v_own = k_ref[rs, :], v_ref[rs, :]
            k_prev = kp_ref[...] if r == 0 else k_ref[ps, :]
            v_prev = vp_ref[...] if r == 0 else v_ref[ps, :]
            valid = _swa_valid(r, i)
            own_row = pl.multiple_of(i * SWA_TB + r * W, W)
            dqs = []
            dk_own = jnp.zeros((W, LANES), F32)
            dk_prev = jnp.zeros((W, LANES), F32)
            dv_own = jnp.zeros((W, LANES), F32)
            dv_prev = jnp.zeros((W, LANES), F32)
            for sub in range(2):
                hm = (lane >= 64) if sub else (lane < 64)
                qh = jnp.where(hm, qr, jnp.zeros_like(qr))
                doh = jnp.where(hm, dor, jnp.zeros_like(dor))
                s = jnp.concatenate([_dot(qh, k_prev, NT), _dot(qh, k_own, NT)], axis=1) * scale + bias_ref[sub]
                s = jnp.where(valid, s, NEG)
                lse_b = lse_ref[sub, rs, :]
                dl_b = dl_ref[sub, rs, :]
                p = jnp.exp(s - jnp.tile(lse_b, (1, 2)))
                dp = jnp.concatenate([_dot(doh, v_prev, NT), _dot(doh, v_own, NT)], axis=1)
                ds = p * (dp - jnp.tile(dl_b, (1, 2)))
                dbias_ref[sub] += ds
                sink = sink_ref[2 * p_ + sub]
                dsk_ref[sub:sub + 1, :] += jnp.sum(jnp.exp(sink - lse_b) * dl_b, axis=0, keepdims=True)
                dsb = ds.astype(BF16)
                pb = p.astype(BF16)
                dqs.append((_dot(dsb[:, :W], k_prev) + _dot(dsb[:, W:], k_own)) * scale)
                dk_prev += _dot(dsb[:, :W], qh, TN) * scale
                dk_own += _dot(dsb[:, W:], qh, TN) * scale
                dv_prev += _dot(pb[:, :W], doh, TN)
                dv_own += _dot(pb[:, W:], doh, TN)
            dq_ref[rs, :] = jnp.where(lane < 64, dqs[0], dqs[1])
            dkad_ref[kvh, pl.ds(own_row, W), :] += dk_own
            dvad_ref[kvh, pl.ds(own_row, W), :] += dv_own
            if r == 0:
                @pl.when(i > 0)
                def _():
                    prev_row = pl.multiple_of(i * SWA_TB - W, W)
                    dkad_ref[kvh, pl.ds(prev_row, W), :] += dk_prev
                    dvad_ref[kvh, pl.ds(prev_row, W), :] += dv_prev
            else:
                prev_row = pl.multiple_of(i * SWA_TB + (r - 1) * W, W)
                dkad_ref[kvh, pl.ds(prev_row, W), :] += dk_prev
                dvad_ref[kvh, pl.ds(prev_row, W), :] += dv_prev

    qspec = pl.BlockSpec((SWA_TB, LANES), lambda p, i: (i, p))
    own = pl.BlockSpec((None, SWA_TB, LANES), lambda p, i: (p // 2, i, 0))
    prev = pl.BlockSpec((None, W, LANES), lambda p, i: (p // 2, jnp.maximum(SWA_SUB * i - 1, 0), 0))
    stat = pl.BlockSpec((2, SWA_TB, LANES), lambda p, i: (p, i, 0))
    full = pl.BlockSpec((2, T, LANES), lambda p, i: (0, 0, 0))
    return pl.pallas_call(
        body, name="swa_bwd", grid=(4, nb),
        in_specs=[pl.BlockSpec(memory_space=pltpu.SMEM), qspec, own, prev, own, prev,
                  pl.BlockSpec((2, W, 2 * W), lambda p, i: (p, 0, 0)), qspec, stat, stat],
        out_specs=[qspec, full, full, pl.BlockSpec((2, W, 2 * W), lambda p, i: (p, 0, 0)),
                   pl.BlockSpec((None, 8, LANES), lambda p, i: (p, 0, 0))],
        out_shape=[jax.ShapeDtypeStruct((T, 512), F32), jax.ShapeDtypeStruct((2, T, LANES), F32),
                   jax.ShapeDtypeStruct((2, T, LANES), F32), jax.ShapeDtypeStruct((SWA_HEADS, W, 2 * W), F32),
                   jax.ShapeDtypeStruct((4, 8, LANES), F32)],
        compiler_params=_cparams("arbitrary", "arbitrary"),
    )(sinks, q, kad, kad, vad, vad, bias, do, lse, delta)


def _mem_fwd(q, mk, mv, T, tq):
    scale = MEM_HEAD_DIM ** -0.5

    def body(q_ref, k_ref, v_ref, o_ref, lse_ref):
        s = _dot(q_ref[...], k_ref[...], NT) * scale
        m = jnp.max(s, axis=1, keepdims=True)
        p = jnp.exp(s - m)
        l = jnp.sum(p, axis=1, keepdims=True)
        o_ref[...] = _dot((p / l).astype(BF16), v_ref[...]).astype(o_ref.dtype)
        lse_ref[...] = jnp.broadcast_to(m + jnp.log(l), (tq, LANES))

    qspec = pl.BlockSpec((tq, LANES), lambda h, i: (i, h))
    kspec = pl.BlockSpec((N_MEM, LANES), lambda h, i: (0, h))
    return pl.pallas_call(
        body, name="mem_fwd", grid=(MEM_HEADS, T // tq),
        in_specs=[qspec, kspec, kspec],
        out_specs=[qspec, pl.BlockSpec((None, tq, LANES), lambda h, i: (h, i, 0))],
        out_shape=[jax.ShapeDtypeStruct((T, 512), BF16), jax.ShapeDtypeStruct((MEM_HEADS, T, LANES), F32)],
        compiler_params=_cparams("parallel", "parallel"),
    )(q, mk, mv)


def _mem_bwd(q, mk, mv, do, lse, delta, T, tq):
    scale = MEM_HEAD_DIM ** -0.5
    rep = N_MEM // LANES

    def body(q_ref, k_ref, v_ref, do_ref, lse_ref, dl_ref, dq_ref, dk_ref, dv_ref):
        i = pl.program_id(1)

        @pl.when(i == 0)
        def _():
            dk_ref[...] = jnp.zeros_like(dk_ref)
            dv_ref[...] = jnp.zeros_like(dv_ref)

        qv, dov = q_ref[...], do_ref[...]
        s = _dot(qv, k_ref[...], NT) * scale
        p = jnp.exp(s - jnp.tile(lse_ref[...], (1, rep)))
        dp = _dot(dov, v_ref[...], NT)
        ds = p * (dp - jnp.tile(dl_ref[...], (1, rep)))
        dsb = ds.astype(BF16)
        dq_ref[...] = _dot(dsb, k_ref[...]) * scale
        dk_ref[...] += _dot(dsb, qv, TN) * scale
        dv_ref[...] += _dot(p.astype(BF16), dov, TN)

    qspec = pl.BlockSpec((tq, LANES), lambda h, i: (i, h))
    kspec = pl.BlockSpec((N_MEM, LANES), lambda h, i: (0, h))
    stat = pl.BlockSpec((None, tq, LANES), lambda h, i: (h, i, 0))
    return pl.pallas_call(
        body, name="mem_bwd", grid=(MEM_HEADS, T // tq),
        in_specs=[qspec, kspec, kspec, qspec, stat, stat],
        out_specs=[qspec, kspec, kspec],
        out_shape=[jax.ShapeDtypeStruct((T, 512), F32), jax.ShapeDtypeStruct((N_MEM, 512), F32),
                   jax.ShapeDtypeStruct((N_MEM, 512), F32)],
        compiler_params=_cparams("arbitrary", "arbitrary"),
    )(q, mk, mv, do, lse, delta)


def _mem_prep_fwd(mem, g_mem, w_kv, kn_gain, gm128):
    def body(mem_ref, g_ref, w_ref, kn_ref, gm_ref, memn_o, kv_o, mk_o, mv_o):
        xhat, _ = _rms_rows(mem_ref[...], None)
        memn = (xhat * g_ref[...]).astype(BF16)
        memn_o[...] = memn
        kv = _dot(memn, w_ref[...])
        kv_o[...] = kv
        gm = gm_ref[...]
        for c in range(4):
            sl = slice(c * LANES, (c + 1) * LANES)
            y, _ = _head_norm(kv[:, sl], gm, kn_ref[...])
            mk_o[:, sl] = y.astype(BF16)
        mv_o[...] = kv[:, 512:].astype(BF16)

    vm = pl.BlockSpec(memory_space=pltpu.VMEM)
    return pl.pallas_call(
        body, name="mem_prep_fwd", in_specs=[vm] * 5, out_specs=[vm] * 4,
        out_shape=[jax.ShapeDtypeStruct((N_MEM, D_MODEL), BF16), jax.ShapeDtypeStruct((N_MEM, D_MODEL), F32),
                   jax.ShapeDtypeStruct((N_MEM, 512), BF16), jax.ShapeDtypeStruct((N_MEM, 512), BF16)],
        compiler_params=pltpu.CompilerParams(vmem_limit_bytes=VMEM_LIMIT),
    )(mem, g_mem, w_kv, kn_gain, gm128)


def _mem_prep_bwd(mem, g_mem, memn, kv, w_kv, kn_gain, gm128, dmk, dmv):
    def body(mem_ref, g_ref, memn_ref, kv_ref, w_ref, kn_ref, gm_ref, dmk_ref, dmv_ref, dw_o, dg_o, dkn_o, dkv_s):
        gm = gm_ref[...]
        dkn = jnp.zeros((1, LANES), F32)
        for c in range(4):
            sl = slice(c * LANES, (c + 1) * LANES)
            dx, dg = _head_norm_bwd(dmk_ref[:, sl], kv_ref[:, sl], gm, kn_ref[...])
            dkv_s[:, sl] = dx.astype(BF16)
            dkn = dkn + dg
        dkn_o[...] = dkn
        dkv_s[:, 512:] = dmv_ref[...].astype(BF16)
        dkv = dkv_s[...]
        dw_o[...] = _dot(memn_ref[...], dkv, TN)
        dmemn = _dot(dkv, w_ref[...], NT)
        xhat, _ = _rms_rows(mem_ref[...], None)
        dg_o[...] = jnp.sum(dmemn * xhat, axis=0, keepdims=True)

    vm = pl.BlockSpec(memory_space=pltpu.VMEM)
    return pl.pallas_call(
        body, name="mem_prep_bwd", in_specs=[vm] * 9, out_specs=[vm] * 3,
        out_shape=[jax.ShapeDtypeStruct((D_MODEL, D_MODEL), F32), jax.ShapeDtypeStruct((1, D_MODEL), F32),
                   jax.ShapeDtypeStruct((1, LANES), F32)],
        scratch_shapes=[pltpu.VMEM((N_MEM, D_MODEL), BF16)],
        compiler_params=pltpu.CompilerParams(vmem_limit_bytes=VMEM_LIMIT),
    )(mem, g_mem, memn, kv, w_kv, kn_gain, gm128, dmk, dmv)


SLOT_O = D_MODEL // N_SHARD


def _merge_fwd(proj, b_gate, o3, w3, T, tb):
    def body(gl_ref, bg_ref, oa_ref, of_ref, om_ref, wa_ref, wf_ref, wm_ref, out_ref):
        o_refs = (oa_ref, of_ref, om_ref)
        w_refs = (wa_ref, wf_ref, wm_ref)
        for n in range(N_SHARD):
            acc = jnp.zeros((tb, SLOT_O), F32)
            for b in range(3):
                c0 = b * D_MODEL + n * SLOT_O
                g = jax.nn.sigmoid(gl_ref[:, c0:c0 + SLOT_O] + bg_ref[:, c0:c0 + SLOT_O])
                acc = acc + g * _dot(o_refs[b][...], w_refs[b][n])
            out_ref[:, n * SLOT_O:(n + 1) * SLOT_O] = acc.astype(out_ref.dtype)

    rows = pl.BlockSpec((tb, 512), lambda i: (i, 0))
    wspec = pl.BlockSpec((N_SHARD, 512, SLOT_O), lambda i: (0, 0, 0))
    return pl.pallas_call(
        body, name="merge_fwd", grid=(T // tb,),
        in_specs=[pl.BlockSpec((tb, GATE_W), lambda i: (i, 1)), pl.BlockSpec((1, GATE_W), lambda i: (0, 0)),
                  rows, rows, rows, wspec, wspec, wspec],
        out_specs=pl.BlockSpec((tb, D_MODEL), lambda i: (i, 0)),
        out_shape=jax.ShapeDtypeStruct((T, D_MODEL), BF16),
        compiler_params=_cparams("parallel"),
    )(proj, b_gate, *o3, *w3)


def _merge_bwd(proj, b_gate, o3, w3, dmerged, T, tb):
    heads = (SWA_HEADS, FOX_HEADS, MEM_HEADS)

    def body(gl_ref, bg_ref, oa_ref, of_ref, om_ref, wa_ref, wf_ref, wm_ref, dm_ref,
             dgl_o, doa_o, dof_o, dom_o, dla_o, dlf_o, dlm_o, dwa_o, dwf_o, dwm_o, dbg_o):
        i = pl.program_id(0)
        o_refs = (oa_ref, of_ref, om_ref)
        w_refs = (wa_ref, wf_ref, wm_ref)
        do_refs = (doa_o, dof_o, dom_o)
        dl_refs = (dla_o, dlf_o, dlm_o)
        dw_refs = (dwa_o, dwf_o, dwm_o)

        @pl.when(i == 0)
        def _():
            for r in dw_refs:
                r[...] = jnp.zeros_like(r)
            dbg_o[...] = jnp.zeros_like(dbg_o)

        lane = _lane((tb, LANES))
        for b in range(3):
            ob = o_refs[b][...]
            do = jnp.zeros((tb, 512), F32)
            for n in range(N_SHARD):
                c0 = b * D_MODEL + n * SLOT_O
                g = jax.nn.sigmoid(gl_ref[:, c0:c0 + SLOT_O] + bg_ref[:, c0:c0 + SLOT_O])
                dm = dm_ref[:, n * SLOT_O:(n + 1) * SLOT_O]
                y = _dot(ob, w_refs[b][n])
                dgl = dm * y * g * (1.0 - g)
                dgl_o[:, c0:c0 + SLOT_O] = dgl.astype(dgl_o.dtype)
                dbg_o[:, c0:c0 + SLOT_O] += jnp.sum(dgl, axis=0, keepdims=True)
                dy = (dm * g).astype(BF16)
                do = do + _dot(dy, w_refs[b][n], NT)
                dw_refs[b][n] += _dot(ob, dy, TN)
            do_refs[b][...] = do.astype(BF16)
            prod = do * ob.astype(F32)
            for c in range(4):
                blk = prod[:, c * LANES:(c + 1) * LANES]
                if heads[b] == 8:
                    lo = jnp.sum(jnp.where(lane < 64, blk, 0.0), axis=1, keepdims=True)
                    hi = jnp.sum(jnp.where(lane >= 64, blk, 0.0), axis=1, keepdims=True)
                    if b == 1:
                        aug = jnp.zeros((tb, LANES), F32)
                        for sub, dl in enumerate((lo, hi)):
                            for e, piece in enumerate(_split3(-dl)):
                                aug = jnp.where(lane == AUG_STRIDE * sub + AUG_C + e, piece.astype(F32), aug)
                        dl_refs[b][:, c * LANES:(c + 1) * LANES] = aug.astype(BF16)
                    else:
                        dl_refs[b][2 * c] = jnp.broadcast_to(lo, (tb, LANES))
                        dl_refs[b][2 * c + 1] = jnp.broadcast_to(hi, (tb, LANES))
                else:
                    dl_refs[b][c] = jnp.broadcast_to(jnp.sum(blk, axis=1, keepdims=True), (tb, LANES))

    rows = pl.BlockSpec((tb, 512), lambda i: (i, 0))
    wspec = pl.BlockSpec((N_SHARD, 512, SLOT_O), lambda i: (0, 0, 0))
    stat = lambda h: pl.BlockSpec((h, tb, LANES), lambda i: (0, i, 0))
    return pl.pallas_call(
        body, name="merge_bwd", grid=(T // tb,),
        in_specs=[pl.BlockSpec((tb, GATE_W), lambda i: (i, 1)), pl.BlockSpec((1, GATE_W), lambda i: (0, 0)),
                  rows, rows, rows, wspec, wspec, wspec, pl.BlockSpec((tb, D_MODEL), lambda i: (i, 0))],
        out_specs=[pl.BlockSpec((tb, GATE_W), lambda i: (i, 0)), rows, rows, rows,
                   stat(8), rows, stat(4), wspec, wspec, wspec, pl.BlockSpec((1, GATE_W), lambda i: (0, 0))],
        out_shape=[jax.ShapeDtypeStruct((T, GATE_W), BF16)] + [jax.ShapeDtypeStruct((T, 512), BF16)] * 3
        + [jax.ShapeDtypeStruct((8, T, LANES), F32), jax.ShapeDtypeStruct((T, 512), BF16),
           jax.ShapeDtypeStruct((4, T, LANES), F32)]
        + [jax.ShapeDtypeStruct((N_SHARD, 512, SLOT_O), F32)] * 3 + [jax.ShapeDtypeStruct((1, GATE_W), F32)],
        compiler_params=_cparams("arbitrary"),
    )(proj, b_gate, *o3, *w3, dmerged)


def _local_step(x, mem, tgt, small, wc, w_kv, w_o3, w_out, w_up, w_down):
    T = x.shape[0]
    tm = min(512, T)
    tile2 = lambda v: jnp.tile(v.reshape(1, -1), (1, LANES // v.size))
    gains = jnp.concatenate([tile2(small["qn_swa"]), tile2(small["kn_swa"]), tile2(small["qn_fox"]),
                             tile2(small["kn_fox"]), tile2(small["qn_mem"]), jnp.zeros((3, LANES), F32)], axis=0)
    kn_mem = small["kn_mem"].reshape(1, LANES)
    bfor = jnp.pad(small["b_forget"].reshape(1, -1), ((0, 0), (0, LANES - FOX_HEADS)))
    gm64 = _group_mean_matrix(64)
    gm128 = _group_mean_matrix(128)
    tb_prep = min(256, T)
    ones = jnp.ones((tb_prep, tb_prep), F32)
    tril = jnp.tril(ones).astype(BF16)
    triu = jnp.triu(ones).astype(BF16)
    bucket = _t5_bucket_matrix()
    g_mix, g_mlp, g_mem = small["g_mix"], small["g_mlp"], small["g_mem"]
    b_gate = small["b_gate"]
    sinks = small["sink_swa"].reshape(-1)

    tl = min(1024, T)
    sq = pl.BlockSpec((tl, D_MODEL), lambda i, j, k: (i, j))
    h = _rmsnorm("rms_mix", x, g_mix, tm)
    (proj,) = _matmul(
        "mm_proj", h, wc, dims=NN, grid=(T // tl, PROJ_W // D_MODEL, 1),
        a_spec=pl.BlockSpec((tl, D_MODEL), lambda i, j, k: (i, 0)),
        b_spec=pl.BlockSpec((D_MODEL, D_MODEL), lambda i, j, k: (0, j)),
        acc_shape=(tl, D_MODEL),
        outs=[(jax.ShapeDtypeStruct((T, PROJ_W), F32), sq)],
        epilogue=_epi_store)
    qa, qf, kf, vf, qm, kad, vad, qf_aug, kf_aug = _prep_fwd(proj, gains, bfor, tril, gm64, gm128, T, tb_prep)
    bias = _swa_bias(small["rel_bias"], bucket)
    o_swa, lse_swa = _swa_fwd(sinks, qa, kad, vad, bias, T)
    o_fox, qf_aug_bwd = _fox_fwd(qf, qf_aug, kf, kf_aug, vf, T, tm)
    memn, kv, mk, mv = _mem_prep_fwd(mem, g_mem, w_kv, kn_mem, gm128)
    o_mem, lse_mem = _mem_fwd(qm, mk, mv, T, tm)
    o3 = (o_swa, o_fox, o_mem)
    merged = _merge_fwd(proj, b_gate, o3, w_o3, T, min(256, T))

    def epi_residual(acc, extra_refs, out_refs, ij):
        out_refs[0][...] = extra_refs[0][...] + acc

    row_full = pl.BlockSpec((tm, D_MODEL), lambda i, j, k: (i, 0))
    row_big = pl.BlockSpec((tl, D_MODEL), lambda i, j, k: (i, 0))
    whole = pl.BlockSpec((D_MODEL, D_MODEL), lambda i, j, k: (0, 0))
    (x2,) = _matmul(
        "mm_out", merged, w_out, dims=NN, grid=(T // tl, 1, 1),
        a_spec=row_big, b_spec=whole,
        acc_shape=(tl, D_MODEL), extra=[(x, row_big)],
        outs=[(jax.ShapeDtypeStruct((T, D_MODEL), F32), row_big)], epilogue=epi_residual)
    hm = _rmsnorm("rms_mlp", x2, g_mlp, tm)

    def epi_relu2(acc, extra_refs, out_refs, ij):
        out_refs[0][...] = acc
        r = jnp.maximum(acc, 0.0)
        out_refs[1][...] = (r * r).astype(BF16)

    up, u = _matmul(
        "mm_up", hm, w_up, dims=NN, grid=(T // tl, N_SHARD, 1),
        a_spec=row_big, b_spec=pl.BlockSpec((None, D_MODEL, D_MODEL), lambda i, j, k: (j, 0, 0)),
        acc_shape=(tl, D_MODEL),
        outs=[(jax.ShapeDtypeStruct((T, D_FF), F32), sq), (jax.ShapeDtypeStruct((T, D_FF), BF16), sq)],
        epilogue=epi_relu2)

    def epi_loss(acc, extra_refs, out_refs, ij):
        y = extra_refs[0][...] + acc
        err = y - extra_refs[1][...]
        out_refs[0][...] = err * (1.0 / D_MODEL)
        sq = jnp.sum(jnp.sum(err * err, axis=1, keepdims=True), axis=0, keepdims=True)

        @pl.when(ij[0] == 0)
        def _():
            out_refs[1][...] = jnp.zeros_like(out_refs[1])

        out_refs[1][...] += jnp.broadcast_to(sq, out_refs[1].shape)

    kblk = pl.BlockSpec((tl, D_MODEL), lambda i, j, k: (i, k))
    dy, loss_acc = _matmul(
        "mm_down", u, w_down, dims=NN, grid=(T // tl, 1, N_SHARD),
        a_spec=kblk, b_spec=pl.BlockSpec((D_MODEL, D_MODEL), lambda i, j, k: (k, 0)),
        acc_shape=(tl, D_MODEL), extra=[(x2, row_big), (tgt, row_big)],
        outs=[(jax.ShapeDtypeStruct((T, D_MODEL), F32), row_big),
              (jax.ShapeDtypeStruct((8, LANES), F32), pl.BlockSpec((8, LANES), lambda i, j, k: (0, 0)))],
        epilogue=epi_loss)
    loss = loss_acc[0, 0] * (0.5 / D_MODEL)

    def epi_dup(acc, extra_refs, out_refs, ij):
        out_refs[0][...] = (acc * (2.0 * jnp.maximum(extra_refs[0][...], 0.0))).astype(BF16)

    (dup,) = _matmul(
        "mm_dup", dy, w_down, dims=NT, grid=(T // tl, N_SHARD, 1),
        a_spec=row_big, b_spec=pl.BlockSpec((D_MODEL, D_MODEL), lambda i, j, k: (j, 0)),
        acc_shape=(tl, D_MODEL), extra=[(up, sq)],
        outs=[(jax.ShapeDtypeStruct((T, D_FF), BF16), sq)], epilogue=epi_dup)

    nkt = T // tl
    t_rows = pl.BlockSpec((tl, D_MODEL), lambda i, j, k: (k, i))
    t_cols = pl.BlockSpec((tl, D_MODEL), lambda i, j, k: (k, j))
    (d_w_down,) = _matmul(
        "mm_dw_down", u, dy, dims=TN, grid=(N_SHARD, 1, nkt),
        a_spec=t_rows, b_spec=t_cols, acc_shape=(D_MODEL, D_MODEL),
        outs=[(jax.ShapeDtypeStruct((D_FF, D_MODEL), F32), pl.BlockSpec((D_MODEL, D_MODEL), lambda i, j, k: (i, 0)))],
        epilogue=_epi_store)
    (d_w_up,) = _matmul(
        "mm_dw_up", hm, dup, dims=TN, grid=(1, N_SHARD, nkt),
        a_spec=t_rows, b_spec=t_cols, acc_shape=(D_MODEL, D_MODEL),
        outs=[(jax.ShapeDtypeStruct((N_SHARD, D_MODEL, D_MODEL), F32),
               pl.BlockSpec((None, D_MODEL, D_MODEL), lambda i, j, k: (j, 0, 0)))],
        epilogue=_epi_store)

    def epi_rms_bwd(acc, extra_refs, out_refs, ij):
        dx, dg = _rmsnorm_bwd_rows(acc, extra_refs[0][...], extra_refs[1][...])
        out_refs[0][...] = dx + extra_refs[2][...]

        @pl.when(ij[0] == 0)
        def _():
            out_refs[1][...] = jnp.zeros_like(out_refs[1])

        out_refs[1][...] += dg

    gain_spec = pl.BlockSpec((1, D_MODEL), lambda i, j, k: (0, 0))
    dx2, d_g_mlp = _matmul(
        "mm_dhm", dup, w_up, dims=NT, grid=(T // tl, 1, N_SHARD),
        a_spec=kblk, b_spec=pl.BlockSpec((None, D_MODEL, D_MODEL), lambda i, j, k: (k, 0, 0)),
        acc_shape=(tl, D_MODEL), extra=[(x2, row_big), (g_mlp, gain_spec), (dy, row_big)],
        outs=[(jax.ShapeDtypeStruct((T, D_MODEL), F32), row_big), (jax.ShapeDtypeStruct((1, D_MODEL), F32), gain_spec)],
        epilogue=epi_rms_bwd)

    (dmerged,) = _matmul(
        "mm_dmerged", dx2, w_out, dims=NT, grid=(T // tl, 1, 1),
        a_spec=row_big, b_spec=whole,
        acc_shape=(tl, D_MODEL), outs=[(jax.ShapeDtypeStruct((T, D_MODEL), F32), row_big)], epilogue=_epi_store)
    (d_w_out,) = _matmul(
        "mm_dw_out", merged, dx2, dims=TN, grid=(1, 1, nkt),
        a_spec=t_rows, b_spec=t_cols, acc_shape=(D_MODEL, D_MODEL),
        outs=[(jax.ShapeDtypeStruct((D_MODEL, D_MODEL), F32), whole)],
        epilogue=_epi_store)
    (dgl, do_swa, do_fox, do_mem, dl_swa, do_fox_aug, dl_mem, d_wo_swa, d_wo_fox, d_wo_mem, d_b_gate) = _merge_bwd(
        proj, b_gate, o3, w_o3, dmerged, T, min(256, T))

    dqa, dkad, dvad, dbias, dsk = _swa_bwd(sinks, qa, kad, vad, bias, do_swa, lse_swa, dl_swa, T)
    dqf, dqf_aug, dkf, dkf_aug, dvf = _fox_bwd(qf, qf_aug_bwd, kf, kf_aug, vf, do_fox, do_fox_aug, T, tm)
    dqm, dmk, dmv = _mem_bwd(qm, mk, mv, do_mem, lse_mem, dl_mem, T, tm)
    d_w_kv, d_g_mem, d_kn_mem = _mem_prep_bwd(mem, g_mem, memn, kv, w_kv, kn_mem, gm128, dmk, dmv)
    d_rel = _swa_bias_bwd(dbias, bucket)
    aug_lane = lambda a, lane: a.reshape(T, FOX_HEADS // 2, LANES)[:, :, lane:lane + AUG_STRIDE + 1:AUG_STRIDE]
    dc_queries = aug_lane(dqf_aug, AUG_C).reshape(T, FOX_HEADS)
    dc_keys = aug_lane(dkf_aug, AUG_NEG_C).reshape(T, FOX_HEADS)
    dccol = jnp.pad(dc_queries - dc_keys, ((0, 0), (0, LANES - FOX_HEADS)))
    dlo, gacc = _prep_bwd(proj, dqa, dkad, dvad, dqf, dkf, dvf, dqm, dccol, gains, bfor, triu, gm64, gm128, T, tb_prep)

    def dwc_half(name, dpart):
        (res,) = _matmul(
            name, h, dpart, dims=TN, grid=(1, LO_W // D_MODEL, nkt),
            a_spec=t_rows, b_spec=t_cols, acc_shape=(D_MODEL, D_MODEL),
            outs=[(jax.ShapeDtypeStruct((D_MODEL, LO_W), F32), pl.BlockSpec((D_MODEL, D_MODEL), lambda i, j, k: (0, j)))],
            epilogue=_epi_store)
        return res

    d_wc_lo = dwc_half("mm_dwc_lo", dlo)
    d_wc_gl = dwc_half("mm_dwc_gl", dgl)
    (dh_lo,) = _matmul(
        "mm_dh_lo", dlo, wc, dims=NT, grid=(T // tl, 1, LO_W // D_MODEL),
        a_spec=kblk, b_spec=pl.BlockSpec((D_MODEL, D_MODEL), lambda i, j, k: (0, k)),
        acc_shape=(tl, D_MODEL), outs=[(jax.ShapeDtypeStruct((T, D_MODEL), F32), row_big)], epilogue=_epi_store)

    def epi_dx(acc, extra_refs, out_refs, ij):
        dhh = acc + extra_refs[3][...]
        dx, dg = _rmsnorm_bwd_rows(dhh, extra_refs[0][...], extra_refs[1][...])
        out_refs[0][...] = dx + extra_refs[2][...]

        @pl.when(ij[0] == 0)
        def _():
            out_refs[1][...] = jnp.zeros_like(out_refs[1])

        out_refs[1][...] += dg

    grad_x, d_g_mix = _matmul(
        "mm_dh_gl", dgl, wc, dims=NT, grid=(T // tm, 1, GATE_W // D_MODEL),
        a_spec=pl.BlockSpec((tm, D_MODEL), lambda i, j, k: (i, k)),
        b_spec=pl.BlockSpec((D_MODEL, D_MODEL), lambda i, j, k: (0, k + LO_W // D_MODEL)),
        acc_shape=(tm, D_MODEL), extra=[(x, row_full), (g_mix, gain_spec), (dx2, row_full), (dh_lo, row_full)],
        outs=[(jax.ShapeDtypeStruct((T, D_MODEL), F32), row_full), (jax.ShapeDtypeStruct((1, D_MODEL), F32), gain_spec)],
        epilogue=epi_dx)

    fold64 = lambda row: (row[:64] + row[64:]).reshape(1, 64)
    grads = {
        "g_mix": d_g_mix, "b_gate": d_b_gate, "b_forget": gacc[5, :FOX_HEADS].reshape(1, FOX_HEADS),
        "qn_swa": fold64(gacc[0]), "kn_swa": fold64(gacc[1]),
        "sink_swa": -dsk[:, :2, 0].reshape(1, SWA_HEADS), "rel_bias": d_rel[:, :SWA_HEADS],
        "qn_fox": fold64(gacc[2]), "kn_fox": fold64(gacc[3]),
        "g_mem": d_g_mem, "qn_mem": gacc[4].reshape(1, LANES), "kn_mem": d_kn_mem, "g_mlp": d_g_mlp,
        "wc_lo": d_wc_lo, "wc_gl": d_wc_gl, "w_mem_kv": d_w_kv,
        "w_o_swa": d_wo_swa, "w_o_fox": d_wo_fox, "w_o_mem": d_wo_mem,
        "w_out": d_w_out, "w_mlp_up": d_w_up, "w_mlp_down": d_w_down,
    }
    return loss, grad_x, grads


MESH = pl.DeviceIdType.MESH
ANY = pl.BlockSpec(memory_space=pl.ANY)


def _place():
    x, y, c = lax.axis_index("x"), lax.axis_index("y"), lax.axis_index("c")
    chips = [(1 - x, y), (x, 1 - y), (1 - x, 1 - y)]
    return x, y, c, chips


def _all_gather_shards(slots):
    n = len(slots)

    def body(*refs):
        out = refs[n:2 * n]
        ici_send, ici_recv, d2d_send, d2d_recv = refs[2 * n:]
        x, y, c, chips = _place()
        sibling = (x, y, 1 - c)
        me = 2 * x + y

        def half(a, who):
            hr = slots[a].shape[1] // 2
            return pl.ds(pl.multiple_of(who * hr, hr), hr)

        def ici(a, j, slot, to):
            return pltpu.make_async_remote_copy(
                src_ref=out[a].at[me, half(a, c)], dst_ref=out[a].at[slot, half(a, c)],
                send_sem=ici_send.at[3 * a + j], recv_sem=ici_recv.at[3 * a + j], device_id=to, device_id_type=MESH)

        def d2d(a, j, slot, which):
            part = out[a].at[slot, half(a, which)]
            return pltpu.make_async_remote_copy(
                src_ref=part, dst_ref=part, send_sem=d2d_send.at[3 * a + j], recv_sem=d2d_recv.at[3 * a + j],
                device_id=sibling, device_id_type=MESH)

        sends = [ici(a, j, me, (*chip, c)) for a in range(n) for j, chip in enumerate(chips)]
        for cp in sends:
            cp.start()
        passed = []
        for a in range(n):
            for j, (px, py) in enumerate(chips):
                ici(a, j, 2 * px + py, (px, py, c)).wait_recv()
                cp = d2d(a, j, 2 * px + py, c)
                cp.start()
                passed.append(cp)
        for a in range(n):
            for j, (px, py) in enumerate(chips):
                d2d(a, j, 2 * px + py, 1 - c).wait_recv()
        for cp in sends + passed:
            cp.wait_send()

    return pl.pallas_call(
        body, name="all_gather_weights",
        in_specs=[ANY] * n, out_specs=[ANY] * n,
        out_shape=[jax.ShapeDtypeStruct(s.shape, s.dtype) for s in slots],
        input_output_aliases={a: a for a in range(n)},
        scratch_shapes=[pltpu.SemaphoreType.DMA((3 * n,))] * 4,
    )(*slots)


def _handshake(peers):
    barrier = pltpu.get_barrier_semaphore()
    for peer in peers:
        pl.semaphore_signal(barrier, inc=1, device_id=peer, device_id_type=MESH)
    pl.semaphore_wait(barrier, len(peers))


def _all_gather_shards_async(slots):
    n = len(slots)
    bufs = [jax.new_ref(s, memory_space=pltpu.MemorySpace.HBM) for s in slots]

    def body(ici_send, ici_recv, d2d_send, d2d_recv):
        x, y, c, chips = _place()
        sibling = (x, y, 1 - c)
        me = 2 * x + y
        _handshake([(px, py, c) for px, py in chips] + [sibling])

        def half(a, who):
            hr = slots[a].shape[1] // 2
            return pl.ds(pl.multiple_of(who * hr, hr), hr)

        def ici(a, j, slot, to):
            return pltpu.make_async_remote_copy(
                src_ref=bufs[a].at[me, half(a, c)], dst_ref=bufs[a].at[slot, half(a, c)],
                send_sem=ici_send.at[3 * a + j], recv_sem=ici_recv.at[3 * a + j], device_id=to, device_id_type=MESH)

        def d2d(a, j, slot, which):
            part = bufs[a].at[slot, half(a, which)]
            return pltpu.make_async_remote_copy(
                src_ref=part, dst_ref=part, send_sem=d2d_send.at[3 * a + j], recv_sem=d2d_recv.at[3 * a + j],
                device_id=sibling, device_id_type=MESH)

        sends = [ici(a, j, me, (*chip, c)) for a in range(n) for j, chip in enumerate(chips)]
        for cp in sends:
            cp.start()
        passed = []
        for a in range(n):
            for j, (px, py) in enumerate(chips):
                ici(a, j, 2 * px + py, (px, py, c)).wait_recv()
                cp = d2d(a, j, 2 * px + py, c)
                cp.start()
                passed.append(cp)
        for a in range(n):
            for j, (px, py) in enumerate(chips):
                d2d(a, j, 2 * px + py, 1 - c).wait_recv()
        for cp in sends + passed:
            cp.wait_send()

    pl.kernel(
        body, mesh=plsc.ScalarSubcoreMesh(axis_name="seq", num_cores=1), name="all_gather_weights_async",
        scratch_types=[pltpu.SemaphoreType.DMA((3 * n,))] * 4,
        compiler_params=pltpu.CompilerParams(collective_id=1),
    )()
    return [b[...] for b in bufs]


def _pair_exchange(gs):
    n = len(gs)

    def body(*refs):
        src, stage = refs[:n], refs[n:2 * n]
        send_sem, recv_sem = refs[2 * n:]
        x, y, c, _ = _place()
        copies = []
        for a in range(n):
            hr = gs[a].shape[1] // 2
            theirs = pl.ds(pl.multiple_of((1 - c) * hr, hr), hr)
            copies.append(pltpu.make_async_remote_copy(
                src_ref=src[a].at[:, theirs, :], dst_ref=stage[a], send_sem=send_sem.at[a], recv_sem=recv_sem.at[a],
                device_id=(x, y, 1 - c), device_id_type=MESH))
        for cp in copies:
            cp.start()
        for cp in copies:
            cp.wait()

    return pl.pallas_call(
        body, name="pair_exchange", in_specs=[ANY] * n, out_specs=[ANY] * n,
        out_shape=[jax.ShapeDtypeStruct((N_SHARD, g.shape[1] // 2, g.shape[2]), g.dtype) for g in gs],
        scratch_shapes=[pltpu.SemaphoreType.DMA((n,))] * 2,
    )(*gs)


def _chip_exchange(sums):
    n = len(sums)

    def body(*refs):
        src, got = refs[:n], refs[n:2 * n]
        send_sem, recv_sem = refs[2 * n:]
        x, y, c, chips = _place()
        copies = []
        for a in range(n):
            for j, (px, py) in enumerate(chips):
                copies.append(pltpu.make_async_remote_copy(
                    src_ref=src[a].at[2 * px + py], dst_ref=got[a].at[j],
                    send_sem=send_sem.at[3 * a + j], recv_sem=recv_sem.at[3 * a + j],
                    device_id=(px, py, c), device_id_type=MESH))
        for cp in copies:
            cp.start()
        for cp in copies:
            cp.wait()

    return pl.pallas_call(
        body, name="chip_exchange", in_specs=[ANY] * n, out_specs=[ANY] * n,
        out_shape=[jax.ShapeDtypeStruct((3,) + s.shape[1:], s.dtype) for s in sums],
        scratch_shapes=[pltpu.SemaphoreType.DMA((3 * n,))] * 2,
    )(*sums)


def _pair_gather(fulls):
    n = len(fulls)

    def body(*refs):
        full = refs[n:2 * n]
        send_sem, recv_sem = refs[2 * n:]
        x, y, c, _ = _place()
        copies = []
        for a in range(n):
            hr = fulls[a].shape[0] // 2
            mine = full[a].at[pl.ds(pl.multiple_of(c * hr, hr), hr)]
            copies.append(pltpu.make_async_remote_copy(
                src_ref=mine, dst_ref=mine, send_sem=send_sem.at[a], recv_sem=recv_sem.at[a],
                device_id=(x, y, 1 - c), device_id_type=MESH))
        for cp in copies:
            cp.start()
        for cp in copies:
            cp.wait()

    return pl.pallas_call(
        body, name="pair_gather", in_specs=[ANY] * n, out_specs=[ANY] * n,
        out_shape=[jax.ShapeDtypeStruct(f.shape, f.dtype) for f in fulls],
        input_output_aliases={a: a for a in range(n)},
        scratch_shapes=[pltpu.SemaphoreType.DMA((n,))] * 2,
    )(*fulls)


ELEMENTWISE_BLOCK_ELEMS = 256 * 1024


def _row_block(rows, cols):
    rb = 8
    while rb * 2 * cols <= ELEMENTWISE_BLOCK_ELEMS and rb * 2 <= rows:
        rb *= 2
    return rb


def _pair_sum(name, place, g, stage):
    _, R, C = g.shape
    hr = R // 2
    rb = _row_block(hr, C)
    nb = hr // rb

    def body(place_ref, g_ref, st_ref, sum_bf, own_f32):
        s = pl.program_id(1)
        tot = g_ref[...] + st_ref[...]
        sum_bf[...] = tot.astype(BF16)

        @pl.when(s == place_ref[0])
        def _():
            own_f32[...] = tot

    return pl.pallas_call(
        body, name=name,
        grid_spec=pltpu.PrefetchScalarGridSpec(
            num_scalar_prefetch=1, grid=(nb, N_SHARD),
            in_specs=[pl.BlockSpec((None, rb, C), lambda i, s, pr: (s, pr[1] * nb + i, 0)),
                      pl.BlockSpec((None, rb, C), lambda i, s, pr: (s, i, 0))],
            out_specs=[pl.BlockSpec((None, rb, C), lambda i, s, pr: (s, i, 0)),
                       pl.BlockSpec((rb, C), lambda i, s, pr: (i, 0))]),
        out_shape=[jax.ShapeDtypeStruct((N_SHARD, hr, C), BF16), jax.ShapeDtypeStruct((hr, C), F32)],
        compiler_params=_cparams("arbitrary", "arbitrary"),
    )(place, g, stage)


def _final_sum(name, place, own, got):
    hr, C = own.shape
    rb = _row_block(hr, C)
    nb = hr // rb

    def body(place_ref, own_ref, got_ref, o_ref):
        o_ref[...] = ((own_ref[...] + got_ref[0].astype(F32)) + got_ref[1].astype(F32)) + got_ref[2].astype(F32)

    return pl.pallas_call(
        body, name=name,
        grid_spec=pltpu.PrefetchScalarGridSpec(
            num_scalar_prefetch=1, grid=(nb,),
            in_specs=[pl.BlockSpec((rb, C), lambda i, pr: (i, 0)), pl.BlockSpec((3, rb, C), lambda i, pr: (0, i, 0))],
            out_specs=pl.BlockSpec((rb, C), lambda i, pr: (pr[1] * nb + i, 0))),
        out_shape=jax.ShapeDtypeStruct((2 * hr, C), F32),
        compiler_params=_cparams("arbitrary"),
    )(place, own, got)


def _adamw_math(w, g, m, v):
    m = ADAM_B1 * m + (1.0 - ADAM_B1) * g
    v = ADAM_B2 * v + (1.0 - ADAM_B2) * (g * g)
    m_hat = m / (1.0 - ADAM_B1 ** ADAM_STEP)
    v_hat = v / (1.0 - ADAM_B2 ** ADAM_STEP)
    delta = -ADAM_LR * (m_hat / (jnp.sqrt(v_hat) + ADAM_EPS) + ADAM_WD * w)
    return delta, m, v


def _adamw(name, w, g, m, v):
    R, Cw = w.shape
    Cg = g.shape[1]
    rb = _row_block(R, Cg)

    def body(w_ref, g_ref, m_ref, v_ref, g_o, d_o, m_o, v_o):
        gv = g_ref[...]
        delta, mn, vn = _adamw_math(w_ref[...], gv, m_ref[...], v_ref[...])
        g_o[...] = gv
        d_o[...] = delta
        m_o[...] = mn
        v_o[...] = vn

    blk = pl.BlockSpec((rb, Cg), lambda i: (i, 0))
    return pl.pallas_call(
        body, name=name, grid=(R // rb,),
        in_specs=[blk] * 4, out_specs=[blk] * 4,
        out_shape=[jax.ShapeDtypeStruct((R, Cw), F32)] * 4,
        compiler_params=_cparams("parallel"),
    )(w, g, m, v)


N_DEV = 8
SMALL_ROWS = 64


def _small_allreduce_adamw(g, w, m, v):
    def body(g_ref, w_ref, m_ref, v_ref, all_ref, gs_o, d_o, m_o, v_o, send_sems, recv_sems, local_sem):
        x, y, c, chips = _place()
        me, sibling = (x, y, c), (x, y, 1 - c)

        def rows(px, py, pc):
            return all_ref.at[pl.ds(pl.multiple_of((4 * px + 2 * py + pc) * SMALL_ROWS, SMALL_ROWS), SMALL_ROWS), :]

        def copy(k, block, to, src=None):
            return pltpu.make_async_remote_copy(
                src_ref=rows(*block) if src is None else src, dst_ref=rows(*block),
                send_sem=send_sems.at[k], recv_sem=recv_sems.at[k], device_id=to, device_id_type=MESH)

        mine = pltpu.make_async_copy(g_ref, rows(*me), local_sem)
        mine.start()
        first = [copy(0, me, sibling, src=g_ref)]
        first += [copy(1 + j, me, (*chip, c), src=g_ref) for j, chip in enumerate(chips)]
        for cp in first:
            cp.start()
        passed = [copy(4 + j, (*chip, c), sibling) for j, chip in enumerate(chips)]
        for j, chip in enumerate(chips):
            copy(1 + j, (*chip, c), me).wait_recv()
            passed[j].start()
        copy(0, sibling, me).wait_recv()
        for j, chip in enumerate(chips):
            copy(4 + j, (*chip, 1 - c), me).wait_recv()
        for cp in first + passed:
            cp.wait_send()
        mine.wait()

        tot = all_ref[0:SMALL_ROWS, :]
        for d in range(1, N_DEV):
            tot = tot + all_ref[d * SMALL_ROWS:(d + 1) * SMALL_ROWS, :]
        delta, mn, vn = _adamw_math(w_ref[...], tot, m_ref[...], v_ref[...])
        gs_o[...] = tot
        d_o[...] = delta
        m_o[...] = mn
        v_o[...] = vn

    vm = pl.BlockSpec(memory_space=pltpu.VMEM)
    shp = jax.ShapeDtypeStruct((SMALL_ROWS, LANES), F32)
    res = pl.pallas_call(
        body, name="small_allreduce_adamw", in_specs=[vm] * 4, out_specs=[vm] * 5,
        out_shape=[jax.ShapeDtypeStruct((N_DEV * SMALL_ROWS, LANES), F32), shp, shp, shp, shp],
        scratch_shapes=[pltpu.SemaphoreType.DMA((7,)), pltpu.SemaphoreType.DMA((7,)), pltpu.SemaphoreType.DMA],
    )(g, w, m, v)
    return res[1:]


SMALL_NAMES = ("g_mix", "b_gate", "b_forget", "qn_swa", "kn_swa", "sink_swa", "rel_bias", "qn_fox", "kn_fox",
               "g_mem", "qn_mem", "kn_mem", "g_mlp")
BIG_NAMES = ("w_in", "w_mem_kv", "w_o_swa", "w_o_fox", "w_o_mem", "w_out", "w_mlp_up", "w_mlp_down")
WEIGHT_NAMES = ("g_mix", "w_in", "b_gate", "b_forget", "qn_swa", "kn_swa", "sink_swa", "rel_bias", "qn_fox", "kn_fox",
                "g_mem", "w_mem_kv", "qn_mem", "kn_mem", "w_o_swa", "w_o_fox", "w_o_mem", "w_out", "g_mlp",
                "w_mlp_up", "w_mlp_down")


def _pack_small(parts, extra=None):
    rows = []
    for n in SMALL_NAMES:
        flat = parts[n].reshape(-1).astype(F32)
        flat = jnp.pad(flat, (0, (-flat.size) % LANES))
        rows.append(flat.reshape(-1, LANES))
    if extra is not None:
        rows.append(jnp.pad(extra.reshape(1, 1), ((0, 0), (0, LANES - 1))))
    packed = jnp.concatenate(rows, axis=0)
    return jnp.pad(packed, ((0, SMALL_ROWS - packed.shape[0]), (0, 0)))


def _unpack_small(packed, shapes):
    out, r = {}, 0
    for n in SMALL_NAMES:
        size = math.prod(shapes[n])
        nr = -(-size // LANES)
        out[n] = packed[r:r + nr].reshape(-1)[:size].reshape(shapes[n])
        r += nr
    return out, packed[r, 0]


def _reorder_w_in(w_full):
    seg = lambda a, b: w_full[:, a:b]
    pad = jnp.zeros((w_full.shape[0], C_GL - C_FL - FOX_HEADS), w_full.dtype)
    return jnp.concatenate([seg(0, 512), seg(768, 1280), seg(1280, 1792), seg(1792, 2304), seg(2312, 2824),
                            seg(512, 640), seg(640, 768), seg(2304, 2312), pad, seg(2824, IN_WIDTH)], axis=1)


def _restore_w_in(lo, gl):
    s = lambda a, b: lo[:, a:b]
    return jnp.concatenate([s(C_QA, C_QA + 512), s(C_KA, C_KA + 128), s(C_VA, C_VA + 128), s(C_QF, C_QF + 512),
                            s(C_KF, C_KF + 512), s(C_VF, C_VF + 512), s(C_FL, C_FL + FOX_HEADS), s(C_QM, C_QM + 512),
                            gl], axis=1)


def kernel(x, mem, g_mix, w_in, b_gate, b_forget, qn_swa, kn_swa, sink_swa, rel_bias, qn_fox, kn_fox, g_mem, w_mem_kv, qn_mem, kn_mem, w_o_swa, w_o_fox, w_o_mem, w_out, g_mlp, w_mlp_up, w_mlp_down, loss_target, m_g_mix, m_w_in, m_b_gate, m_b_forget, m_qn_swa, m_kn_swa, m_sink_swa, m_rel_bias, m_qn_fox, m_kn_fox, m_g_mem, m_w_mem_kv, m_qn_mem, m_kn_mem, m_w_o_swa, m_w_o_fox, m_w_o_mem, m_w_out, m_g_mlp, m_w_mlp_up, m_w_mlp_down, v_g_mix, v_w_in, v_b_gate, v_b_forget, v_qn_swa, v_kn_swa, v_sink_swa, v_rel_bias, v_qn_fox, v_kn_fox, v_g_mem, v_w_mem_kv, v_qn_mem, v_kn_mem, v_w_o_swa, v_w_o_fox, v_w_o_mem, v_w_out, v_g_mlp, v_w_mlp_up, v_w_mlp_down):
    given = dict(locals())
    W = {n: given[n] for n in WEIGHT_NAMES}
    M = {n: given["m_" + n] for n in WEIGHT_NAMES}
    V = {n: given["v_" + n] for n in WEIGHT_NAMES}
    pad_in = ((0, 0), (0, IN_SHARD_PAD - IN_SHARD))

    shards = [jnp.pad(w_in[0].astype(BF16), pad_in)] + [W[n][0].astype(BF16) for n in BIG_NAMES[1:]]
    slots = [jnp.broadcast_to(s[None], (N_SHARD,) + s.shape) for s in shards]
    (g_in,) = _all_gather_shards(slots[:1])
    g_kv, g_oa, g_of, g_om, g_out, g_up, g_down = _all_gather_shards_async(slots[1:])
    w_full = jnp.concatenate([g_in[s, :, :IN_SHARD] for s in range(N_SHARD)], axis=1)
    wc = _reorder_w_in(w_full)
    small = {n: (W[n] if n == "rel_bias" else W[n].reshape(1, -1)) for n in SMALL_NAMES}

    loss, grad_x, grads = _local_step(
        x[0], mem[0], loss_target[0], small, wc, g_kv.reshape(D_MODEL, D_MODEL), (g_oa, g_of, g_om),
        g_out.reshape(D_MODEL, D_MODEL), g_up, g_down.reshape(D_FF, D_MODEL))

    d_full = _restore_w_in(grads["wc_lo"], grads["wc_gl"])
    d_in = jnp.stack([jnp.pad(d_full[:, s * IN_SHARD:(s + 1) * IN_SHARD], pad_in) for s in range(N_SHARD)])
    slot_rows = lambda a: a.reshape(N_SHARD, a.shape[0] // N_SHARD, a.shape[1])
    local = [d_in, slot_rows(grads["w_mem_kv"]), grads["w_o_swa"], grads["w_o_fox"], grads["w_o_mem"],
             slot_rows(grads["w_out"]), grads["w_mlp_up"], slot_rows(grads["w_mlp_down"])]
    place = jnp.stack([2 * lax.axis_index("x") + lax.axis_index("y"), lax.axis_index("c")]).astype(jnp.int32)
    staged = _pair_exchange(local)
    sums = [_pair_sum("pair_sum_" + n, place, g, st) for n, g, st in zip(BIG_NAMES, local, staged)]
    got = _chip_exchange([s[0] for s in sums])
    halves = [_final_sum("final_sum_" + n, place, s[1], r) for n, s, r in zip(BIG_NAMES, sums, got)]
    summed = _pair_gather(halves)

    out = {}
    for n, g in zip(BIG_NAMES, summed):
        res = _adamw("adamw_" + n, W[n][0], g, M[n][0], V[n][0])
        out[n] = [r.reshape(W[n].shape) for r in res]
    shapes = {n: W[n].shape for n in SMALL_NAMES}
    packed = _small_allreduce_adamw(_pack_small(grads, loss), _pack_small(W), _pack_small(M), _pack_small(V))
    unpacked = [_unpack_small(p, shapes) for p in packed]
    for n in SMALL_NAMES:
        out[n] = [u[0][n] for u in unpacked]
    loss_total = unpacked[0][1]

    return (loss_total, grad_x.reshape(x.shape),
            *[out[n][0] for n in WEIGHT_NAMES], *[out[n][1] for n in WEIGHT_NAMES],
            *[out[n][2] for n in WEIGHT_NAMES], *[out[n][3] for n in WEIGHT_NAMES])
```

```python
import functools
import math

import jax
import jax.numpy as jnp
from jax import lax
from jax.experimental import pallas as pl
from jax.experimental.pallas import tpu as pltpu
from jax.experimental.pallas import tpu_sc as plsc

F32 = jnp.float32
BF16 = jnp.bfloat16

D_MODEL = 1024
N_MEM = 256
SWA_HEADS = 8
SWA_KV_HEADS = 2
SWA_HEAD_DIM = 64
WINDOW = 128
FOX_HEADS = 8
FOX_HEAD_DIM = 64
MEM_HEADS = 4
MEM_HEAD_DIM = 128
D_FF = 4 * D_MODEL
REL_BUCKETS = 32
REL_MAX_DIST = 128
EPS = 1e-6
NEG = -1e30
GATE_W = 3 * D_MODEL
IN_WIDTH = 5896
N_SHARD = 4
IN_SHARD = IN_WIDTH // N_SHARD
IN_SHARD_PAD = 1536

ADAM_LR = 0.001
ADAM_B1 = 0.9
ADAM_B2 = 0.999
ADAM_EPS = 1e-08
ADAM_WD = 0.01
ADAM_STEP = 10

LANES = 128
V7X_VMEM_BYTES = 64 * 1024 * 1024
VMEM_LIMIT = V7X_VMEM_BYTES * 3 // 4

C_QA, C_QF, C_KF, C_VF, C_QM, C_KA, C_VA, C_FL, C_GL = 0, 512, 1024, 1536, 2048, 2560, 2688, 2816, 3072
LO_W = 3072
PROJ_W = 6144

NN = (((1,), (0,)), ((), ()))
NT = (((1,), (1,)), ((), ()))
TN = (((0,), (0,)), ((), ()))


def _dot(a, b, dims=NN):
    return lax.dot_general(a, b, dims, preferred_element_type=F32)


def _cparams(*sem):
    return pltpu.CompilerParams(dimension_semantics=sem, vmem_limit_bytes=VMEM_LIMIT)


def _split3(a):
    hi = a.astype(BF16)
    r1 = a - hi.astype(F32)
    mid = r1.astype(BF16)
    lo = (r1 - mid.astype(F32)).astype(BF16)
    return hi, mid, lo


def _dot3_right(a, g):
    hi, mid, lo = _split3(a)
    return _dot(hi, g) + _dot(mid, g) + _dot(lo, g)


def _dot3_left(g, a):
    hi, mid, lo = _split3(a)
    return _dot(g, hi) + _dot(g, mid) + _dot(g, lo)


def _group_mean_matrix(d):
    r = jnp.arange(LANES)
    return jnp.where((r[:, None] // d) == (r[None, :] // d), 1.0 / d, 0.0).astype(BF16)


def _lane(shape):
    return lax.broadcasted_iota(jnp.int32, shape, len(shape) - 1)


def _matmul(name, a, b, *, dims, grid, a_spec, b_spec, acc_shape, outs, epilogue, extra=()):
    nk = grid[2]
    n_extra = len(extra)

    def body(a_ref, b_ref, *rest):
        extra_refs = rest[:n_extra]
        out_refs = rest[n_extra:n_extra + len(outs)]
        i, j, k = pl.program_id(0), pl.program_id(1), pl.program_id(2)
        part = _dot(a_ref[...].astype(BF16), b_ref[...].astype(BF16), dims)
        if nk == 1:
            epilogue(part, extra_refs, out_refs, (i, j))
            return
        acc_ref = rest[-1]

        @pl.when(k == 0)
        def _():
            acc_ref[...] = part

        @pl.when((k > 0) & (k < nk - 1))
        def _():
            acc_ref[...] += part

        @pl.when(k == nk - 1)
        def _():
            epilogue(acc_ref[...] + part, extra_refs, out_refs, (i, j))

    res = pl.pallas_call(
        body,
        name=name,
        grid=grid,
        in_specs=[a_spec, b_spec] + [s for _, s in extra],
        out_specs=[s for _, s in outs],
        out_shape=[s for s, _ in outs],
        scratch_shapes=[pltpu.VMEM(acc_shape, F32)] if nk > 1 else [],
        compiler_params=_cparams("arbitrary", "arbitrary", "arbitrary"),
    )(a, b, *[x for x, _ in extra])
    return res


def _epi_store(acc, extra_refs, out_refs, ij):
    out_refs[0][...] = acc.astype(out_refs[0].dtype)


def _rms_rows(x, g):
    r = lax.rsqrt(jnp.mean(x * x, axis=-1, keepdims=True) + EPS)
    return x * r, r


def _rmsnorm_bwd_rows(dh, x, g):
    xhat, r = _rms_rows(x, g)
    dxh = dh * g
    dx = r * (dxh - xhat * jnp.mean(dxh * xhat, axis=-1, keepdims=True))
    return dx, jnp.sum(dh * xhat, axis=0, keepdims=True)


def _rmsnorm(name, x, g, tb):
    T, Dm = x.shape

    def body(x_ref, g_ref, o_ref):
        xhat, _ = _rms_rows(x_ref[...], None)
        o_ref[...] = (xhat * g_ref[...]).astype(o_ref.dtype)

    return pl.pallas_call(
        body, name=name, grid=(T // tb,),
        in_specs=[pl.BlockSpec((tb, Dm), lambda i: (i, 0)), pl.BlockSpec((1, Dm), lambda i: (0, 0))],
        out_specs=pl.BlockSpec((tb, Dm), lambda i: (i, 0)),
        out_shape=jax.ShapeDtypeStruct((T, Dm), BF16),
        compiler_params=_cparams("parallel"),
    )(x, g)


def _head_norm(x, gm, gain):
    ms = _dot3_right(x * x, gm)
    r = lax.rsqrt(ms + EPS)
    return x * r * gain, x * r


def _head_norm_bwd(dy, x, gm, gain):
    ms = _dot3_right(x * x, gm)
    r = lax.rsqrt(ms + EPS)
    xhat = x * r
    dxh = dy * gain
    dx = r * (dxh - xhat * _dot3_right(dxh * xhat, gm))
    return dx, jnp.sum(dy * xhat, axis=0, keepdims=True)


def _log_sigmoid(z):
    return jnp.minimum(z, 0.0) - jnp.log(1.0 + jnp.exp(-jnp.abs(z)))


def _prep_fwd(proj, gains, bfor, tril, gm64, gm128, T, tb):
    nb = T // tb

    def body(qa_ref, qf_ref, kf_ref, vf_ref, qm_ref, ka_ref, va_ref, fl_ref, gains_ref, bfor_ref, tril_ref,
             gm64_ref, gm128_ref,
             qa_o, qf_o, kf_o, vf_o, qm_o, kad_o, vad_o, qaug_o, kaug_o, carry):
        i = pl.program_id(0)
        gm64v = gm64_ref[...]
        gm128v = gm128_ref[...]
        lane = _lane((tb, LANES))

        def norm512(src, dst, row, gm, scale=1.0):
            gain = gains_ref[row:row + 1, :]
            for c in range(4):
                sl = slice(c * LANES, (c + 1) * LANES)
                y, _ = _head_norm(src[:, sl], gm, gain)
                dst[:, sl] = (y * scale).astype(dst.dtype)

        norm512(qa_ref, qa_o, 0, gm64v)
        norm512(qf_ref, qf_o, 2, gm64v, FOX_SCALE)
        norm512(kf_ref, kf_o, 3, gm64v)
        norm512(qm_ref, qm_o, 4, gm128v)
        vf_o[...] = vf_ref[...].astype(vf_o.dtype)

        ka_n, _ = _head_norm(ka_ref[...], gm64v, gains_ref[1:2, :])
        ka_r = pltpu.roll(ka_n, 64, 1)
        va = va_ref[...]
        va_r = pltpu.roll(va, 64, 1)
        lo = lane < 64
        kad_o[0] = jnp.where(lo, ka_n, ka_r).astype(kad_o.dtype)
        kad_o[1] = jnp.where(lo, ka_r, ka_n).astype(kad_o.dtype)
        vad_o[0] = jnp.where(lo, va, va_r).astype(vad_o.dtype)
        vad_o[1] = jnp.where(lo, va_r, va).astype(vad_o.dtype)

        @pl.when(i == 0)
        def _():
            carry[...] = jnp.zeros_like(carry)

        logf = jnp.where(lane < FOX_HEADS, _log_sigmoid(fl_ref[...] + bfor_ref[...]), 0.0)
        c = _dot3_left(tril_ref[...], logf) + carry[0:1, :]
        carry[...] = jnp.broadcast_to(c[tb - 1:tb, :], carry.shape)
        for pair in range(FOX_HEADS // 2):
            qaug = jnp.zeros((tb, LANES), F32)
            kaug = jnp.zeros((tb, LANES), F32)
            for sub in range(2):
                col = jnp.sum(jnp.where(lane == 2 * pair + sub, c, 0.0), axis=1, keepdims=True)
                pieces = [p.astype(F32) for p in _split3(col)]
                base = AUG_STRIDE * sub
                for e in range(3):
                    qaug = jnp.where(lane == base + AUG_C + e, pieces[e], qaug)
                    kaug = jnp.where(lane == base + AUG_NEG_C + e, -pieces[e], kaug)
                qaug = jnp.where((lane >= base + AUG_NEG_C) & (lane < base + AUG_NEG_C + 3), 1.0, qaug)
                ones_k = ((lane >= base + AUG_C) & (lane < base + AUG_C + 3)) | (
                    (lane >= base + AUG_STAT) & (lane < base + AUG_STAT + 3))
                kaug = jnp.where(ones_k, 1.0, kaug)
            sl = slice(pair * LANES, (pair + 1) * LANES)
            qaug_o[:, sl] = qaug.astype(BF16)
            kaug_o[:, sl] = kaug.astype(BF16)

    def seg(width, start):
        return pl.BlockSpec((tb, width), lambda i, s=start // width: (i, s))

    const = lambda shape: pl.BlockSpec(shape, lambda i: tuple(0 for _ in shape))
    rows512 = pl.BlockSpec((tb, 512), lambda i: (i, 0))
    outs = pl.pallas_call(
        body, name="prep_fwd", grid=(nb,),
        in_specs=[seg(512, C_QA), seg(512, C_QF), seg(512, C_KF), seg(512, C_VF), seg(512, C_QM),
                  seg(128, C_KA), seg(128, C_VA), seg(128, C_FL),
                  const((8, LANES)), const((1, LANES)), const((tb, tb)), const((LANES, LANES)), const((LANES, LANES))],
        out_specs=[rows512, rows512, rows512, rows512, rows512,
                   pl.BlockSpec((2, tb, LANES), lambda i: (0, i, 0)), pl.BlockSpec((2, tb, LANES), lambda i: (0, i, 0)),
                   rows512, rows512],
        out_shape=[jax.ShapeDtypeStruct((T, 512), BF16)] * 5
        + [jax.ShapeDtypeStruct((2, T, LANES), BF16)] * 2
        + [jax.ShapeDtypeStruct((T, 512), BF16)] * 2,
        scratch_shapes=[pltpu.VMEM((8, LANES), F32)],
        compiler_params=_cparams("arbitrary"),
    )(proj, proj, proj, proj, proj, proj, proj, proj, gains, bfor, tril, gm64, gm128)
    return outs


def _prep_bwd(proj, dqa, dkad, dvad, dqf, dkf, dvf, dqm, dccol, gains, bfor, triu, gm64, gm128, T, tb):
    nb = T // tb

    def body(qa_ref, qf_ref, kf_ref, qm_ref, ka_ref, fl_ref,
             dqa_ref, dkad_ref, dvad_ref, dqf_ref, dkf_ref, dvf_ref, dqm_ref, dc_ref,
             gains_ref, bfor_ref, triu_ref, gm64_ref, gm128_ref,
             dlo_o, gacc_o, carry):
        i = pl.program_id(0)
        gm64v = gm64_ref[...]
        gm128v = gm128_ref[...]
        lane = _lane((tb, LANES))

        @pl.when(i == 0)
        def _():
            carry[...] = jnp.zeros_like(carry)
            gacc_o[...] = jnp.zeros_like(gacc_o)

        def norm512_bwd(dsrc, xsrc, col0, row, gm):
            gain = gains_ref[row:row + 1, :]
            gsum = jnp.zeros((1, LANES), F32)
            for c in range(4):
                sl = slice(c * LANES, (c + 1) * LANES)
                dx, dg = _head_norm_bwd(dsrc[:, sl], xsrc[:, sl], gm, gain)
                dlo_o[:, col0 + c * LANES:col0 + (c + 1) * LANES] = dx.astype(dlo_o.dtype)
                gsum = gsum + dg
            gacc_o[row:row + 1, :] += gsum

        norm512_bwd(dqa_ref, qa_ref, C_QA, 0, gm64v)
        norm512_bwd(dqf_ref, qf_ref, C_QF, 2, gm64v)
        norm512_bwd(dkf_ref, kf_ref, C_KF, 3, gm64v)
        norm512_bwd(dqm_ref, qm_ref, C_QM, 4, gm128v)
        dlo_o[:, C_VF:C_VF + 512] = dvf_ref[...].astype(dlo_o.dtype)

        lo = lane < 64

        def fold(ref):
            f0 = ref[0] + pltpu.roll(ref[0], 64, 1)
            f1 = ref[1] + pltpu.roll(ref[1], 64, 1)
            return jnp.where(lo, f0, f1)

        dka, dg = _head_norm_bwd(fold(dkad_ref), ka_ref[...], gm64v, gains_ref[1:2, :])
        gacc_o[1:2, :] += dg
        dlo_o[:, C_KA:C_KA + LANES] = dka.astype(dlo_o.dtype)
        dlo_o[:, C_VA:C_VA + LANES] = fold(dvad_ref).astype(dlo_o.dtype)

        dc = dc_ref[...]
        dlogf = _dot3_left(triu_ref[...], dc) + carry[0:1, :]
        carry[...] = jnp.broadcast_to(dlogf[0:1, :], carry.shape)
        z = fl_ref[...] + bfor_ref[...]
        dfl = jnp.where(lane < FOX_HEADS, dlogf / (1.0 + jnp.exp(z)), 0.0)
        gacc_o[5:6, :] += jnp.sum(dfl, axis=0, keepdims=True)
        dlo_o[:, C_FL:C_FL + LANES] = dfl.astype(dlo_o.dtype)
        dlo_o[:, C_FL + LANES:C_FL + 2 * LANES] = jnp.zeros((tb, LANES), dlo_o.dtype)

    rev = lambda i: nb - 1 - i

    def seg(width, start):
        return pl.BlockSpec((tb, width), lambda i, s=start // width: (rev(i), s))

    const = lambda shape: pl.BlockSpec(shape, lambda i: tuple(0 for _ in shape))
    rows512 = pl.BlockSpec((tb, 512), lambda i: (rev(i), 0))
    dup = pl.BlockSpec((2, tb, LANES), lambda i: (0, rev(i), 0))
    return pl.pallas_call(
        body, name="prep_bwd", grid=(nb,),
        in_specs=[seg(512, C_QA), seg(512, C_QF), seg(512, C_KF), seg(512, C_QM), seg(128, C_KA), seg(128, C_FL),
                  rows512, dup, dup, rows512, rows512, rows512, rows512,
                  pl.BlockSpec((tb, LANES), lambda i: (rev(i), 0)),
                  const((8, LANES)), const((1, LANES)), const((tb, tb)), const((LANES, LANES)), const((LANES, LANES))],
        out_specs=[pl.BlockSpec((tb, LO_W), lambda i: (rev(i), 0)), const((8, LANES))],
        out_shape=[jax.ShapeDtypeStruct((T, LO_W), BF16), jax.ShapeDtypeStruct((8, LANES), F32)],
        scratch_shapes=[pltpu.VMEM((8, LANES), F32)],
        compiler_params=_cparams("arbitrary"),
    )(proj, proj, proj, proj, proj, proj, dqa, dkad, dvad, dqf, dkf, dvf, dqm, dccol, gains, bfor, triu, gm64, gm128)


FOX_SCALE = FOX_HEAD_DIM ** -0.5
AUG_STRIDE = 16
AUG_C = 0
AUG_NEG_C = 3
AUG_STAT = 6


def _fox_head_mask(sub, rows):
    lane = _lane((rows, 2 * LANES))
    main = (lane >= 64 * sub) & (lane < 64 * sub + 64)
    aug = (lane >= LANES + AUG_STRIDE * sub) & (lane < LANES + AUG_STRIDE * (sub + 1))
    return main | aug


def _fox_fwd(q, qaug, k, kaug, v, T, tq):
    nq = T // tq
    tk = tq
    rep = tk // LANES

    def body(q_ref, qa_ref, k_ref, ka_ref, v_ref, o_ref, qab_ref, m_s, acc_s):
        p_, i, j = pl.program_id(0), pl.program_id(1), pl.program_id(2)

        @pl.when(j == 0)
        def _():
            m_s[...] = jnp.full(m_s.shape, NEG, F32)
            acc_s[...] = jnp.zeros_like(acc_s)

        def step(diagonal):
            q2 = jnp.concatenate([q_ref[...], qa_ref[...]], axis=1)
            k2 = jnp.concatenate([k_ref[...], ka_ref[...]], axis=1)
            v2 = jnp.concatenate([v_ref[...], ka_ref[...]], axis=1)
            if diagonal:
                causal = (lax.broadcasted_iota(jnp.int32, (tq, tk), 1) <= lax.broadcasted_iota(jnp.int32, (tq, tk), 0))
            for sub in range(2):
                qh = jnp.where(_fox_head_mask(sub, tq), q2, jnp.zeros_like(q2))
                s = _dot(qh, k2, NT)
                if diagonal:
                    s = jnp.where(causal, s, NEG)
                m_prev = m_s[sub]
                m_next = jnp.maximum(m_prev, jnp.max(s, axis=1, keepdims=True))
                p = jnp.exp(s - jnp.tile(m_next, (1, rep)))
                alpha = jnp.exp(m_prev - m_next)
                m_s[sub] = m_next
                acc_s[sub] = acc_s[sub] * jnp.tile(alpha, (1, 2)) + _dot(p.astype(BF16), v2)

        @pl.when(j == i)
        def _():
            step(True)

        @pl.when(j < i)
        def _():
            step(False)

        @pl.when(j == nq - 1)
        def _():
            lane = _lane((tq, LANES))
            outs = []
            qab = qa_ref[...].astype(F32)
            for sub in range(2):
                acc = acc_s[sub]
                base = AUG_STRIDE * sub
                l = jnp.sum(jnp.where(lane == base + AUG_C, acc[:, LANES:], 0.0), axis=1, keepdims=True)
                outs.append(acc[:, :LANES] / l)
                lse = jnp.max(m_s[sub], axis=1, keepdims=True) + jnp.log(l)
                pieces = _split3(-lse)
                for e in range(3):
                    qab = jnp.where(lane == base + AUG_STAT + e, pieces[e].astype(F32), qab)
            o_ref[...] = jnp.where(lane < 64, outs[0], outs[1]).astype(o_ref.dtype)
            qab_ref[...] = qab.astype(BF16)

    qspec = pl.BlockSpec((tq, LANES), lambda p, i, j: (i, p))
    kspec = pl.BlockSpec((tk, LANES), lambda p, i, j: (jnp.minimum(j, i), p))
    return pl.pallas_call(
        body, name="fox_fwd", grid=(4, nq, nq),
        in_specs=[qspec, qspec, kspec, kspec, kspec],
        out_specs=[qspec, qspec],
        out_shape=[jax.ShapeDtypeStruct((T, 512), BF16), jax.ShapeDtypeStruct((T, 512), BF16)],
        scratch_shapes=[pltpu.VMEM((2, tq, LANES), F32), pltpu.VMEM((2, tq, 2 * LANES), F32)],
        compiler_params=_cparams("parallel", "parallel", "arbitrary"),
    )(q, qaug, k, kaug, v)


def _fox_bwd(q, qaug, k, kaug, v, do, doaug, T, tq):
    nq = T // tq
    tk = tq

    def body(q_ref, qa_ref, k_ref, ka_ref, v_ref, do_ref, doa_ref,
             dq_ref, dqa_ref, dk_ref, dka_ref, dv_ref, dk_s, dv_s):
        p_, j, i = pl.program_id(0), pl.program_id(1), pl.program_id(2)

        @pl.when((j == 0) & (i == 0))
        def _():
            dq_ref[...] = jnp.zeros_like(dq_ref)
            dqa_ref[...] = jnp.zeros_like(dqa_ref)

        @pl.when(i == 0)
        def _():
            dk_s[...] = jnp.zeros_like(dk_s)
            dv_s[...] = jnp.zeros_like(dv_s)

        def step(diagonal):
            q2 = jnp.concatenate([q_ref[...], qa_ref[...]], axis=1)
            k2 = jnp.concatenate([k_ref[...], ka_ref[...]], axis=1)
            v2 = jnp.concatenate([v_ref[...], ka_ref[...]], axis=1)
            do2 = jnp.concatenate([do_ref[...], doa_ref[...]], axis=1)
            if diagonal:
                causal = (lax.broadcasted_iota(jnp.int32, (tq, tk), 1) <= lax.broadcasted_iota(jnp.int32, (tq, tk), 0))
            dqs = []
            for sub in range(2):
                hm = _fox_head_mask(sub, tq)
                qh = jnp.where(hm, q2, jnp.zeros_like(q2))
                doh = jnp.where(hm, do2, jnp.zeros_like(do2))
                s = _dot(qh, k2, NT)
                if diagonal:
                    s = jnp.where(causal, s, NEG)
                p = jnp.exp(s)
                ds = p * _dot(doh, v2, NT)
                dsb = ds.astype(BF16)
                dv_s[...] += _dot(p.astype(BF16), doh[:, :LANES], TN)
                dk_s[...] += _dot(dsb, qh, TN)
                dqs.append(_dot(dsb, k2))
            dq2 = jnp.where(_fox_head_mask(0, tq), dqs[0], dqs[1])
            qrows = pl.ds(pl.multiple_of(i * tq, tq), tq)
            dq_ref[qrows, :] += dq2[:, :LANES] * FOX_SCALE
            dqa_ref[qrows, :] += dq2[:, LANES:]

        @pl.when(i == j)
        def _():
            step(True)

        @pl.when(i > j)
        def _():
            step(False)

        @pl.when(i == nq - 1)
        def _():
            dk_ref[...] = dk_s[:, :LANES]
            dka_ref[...] = dk_s[:, LANES:]
            dv_ref[...] = dv_s[...]

    qspec = pl.BlockSpec((tq, LANES), lambda p, j, i: (jnp.maximum(i, j), p))
    kspec = pl.BlockSpec((tk, LANES), lambda p, j, i: (j, p))
    resident = pl.BlockSpec((T, LANES), lambda p, j, i: (0, p))
    return pl.pallas_call(
        body, name="fox_bwd", grid=(4, nq, nq),
        in_specs=[qspec, qspec, kspec, kspec, kspec, qspec, qspec],
        out_specs=[resident, resident, kspec, kspec, kspec],
        out_shape=[jax.ShapeDtypeStruct((T, 512), F32)] * 5,
        scratch_shapes=[pltpu.VMEM((tk, 2 * LANES), F32), pltpu.VMEM((tk, LANES), F32)],
        compiler_params=_cparams("arbitrary", "arbitrary", "arbitrary"),
    )(q, qaug, k, kaug, v, do, doaug)


SWA_SUB = 4
SWA_TB = SWA_SUB * WINDOW


def _t5_bucket_matrix():
    t = jnp.arange(WINDOW)[:, None] + WINDOW
    s = jnp.arange(2 * WINDOW)[None, :]
    max_exact = REL_BUCKETS // 2
    d = jnp.maximum(t - s, 0)
    df = jnp.maximum(d, 1).astype(F32)
    large = max_exact + (jnp.log(df / max_exact) / math.log(REL_MAX_DIST / max_exact)
                         * (REL_BUCKETS - max_exact)).astype(jnp.int32)
    large = jnp.minimum(large, REL_BUCKETS - 1)
    return jnp.where(d < max_exact, d, large).astype(jnp.int32)


def _swa_bias(rel_bias, bucket):
    def body(rel_ref, bucket_ref, o_ref):
        b = bucket_ref[...]
        for h in range(SWA_HEADS):
            acc = jnp.zeros(b.shape, F32)
            for r in range(REL_BUCKETS):
                acc = jnp.where(b == r, rel_ref[r, h], acc)
            o_ref[h] = acc

    return pl.pallas_call(
        body, name="swa_bias",
        in_specs=[pl.BlockSpec(memory_space=pltpu.SMEM), pl.BlockSpec(memory_space=pltpu.VMEM)],
        out_specs=pl.BlockSpec(memory_space=pltpu.VMEM),
        out_shape=jax.ShapeDtypeStruct((SWA_HEADS, WINDOW, 2 * WINDOW), F32),
    )(rel_bias, bucket)


def _swa_bias_bwd(dbias, bucket):
    def body(db_ref, bucket_ref, o_ref):
        b = bucket_ref[...]
        lane = _lane((1, LANES))
        for r in range(REL_BUCKETS):
            row = jnp.zeros((1, LANES), F32)
            for h in range(SWA_HEADS):
                part = jnp.sum(jnp.where(b == r, db_ref[h], 0.0), axis=0, keepdims=True)
                tot = jnp.sum(part, axis=1, keepdims=True)
                row = jnp.where(lane == h, tot, row)
            o_ref[r:r + 1, :] = row

    return pl.pallas_call(
        body, name="swa_bias_bwd",
        in_specs=[pl.BlockSpec(memory_space=pltpu.VMEM), pl.BlockSpec(memory_space=pltpu.VMEM)],
        out_specs=pl.BlockSpec(memory_space=pltpu.VMEM),
        out_shape=jax.ShapeDtypeStruct((REL_BUCKETS, LANES), F32),
    )(dbias, bucket)


def _swa_valid(r, i):
    t = lax.broadcasted_iota(jnp.int32, (WINDOW, 2 * WINDOW), 0) + WINDOW
    s = lax.broadcasted_iota(jnp.int32, (WINDOW, 2 * WINDOW), 1)
    dist = t - s
    band = (dist >= 0) & (dist < WINDOW)
    if r == 0:
        band = band & ((s >= WINDOW) | (i > 0))
    return band


def _swa_fwd(sinks, q, kad, vad, bias, T):
    nb = T // SWA_TB
    scale = SWA_HEAD_DIM ** -0.5
    W = WINDOW

    def body(sink_ref, q_ref, k_ref, kp_ref, v_ref, vp_ref, bias_ref, o_ref, lse_ref):
        p_, i = pl.program_id(0), pl.program_id(1)
        lane = _lane((W, LANES))
        for r in range(SWA_SUB):
            rs = slice(r * W, (r + 1) * W)
            ps = slice((r - 1) * W, r * W)
            qr = q_ref[rs, :]
            k_own, v_own = k_ref[rs, :], v_ref[rs, :]
            k_prev = kp_ref[...] if r == 0 else k_ref[ps, :]
            v_prev = vp_ref[...] if r == 0 else v_ref[ps, :]
            valid = _swa_valid(r, i)
            outs = []
            for sub in range(2):
                hm = (lane >= 64) if sub else (lane < 64)
                qh = jnp.where(hm, qr, jnp.zeros_like(qr))
                s = jnp.concatenate([_dot(qh, k_prev, NT), _dot(qh, k_own, NT)], axis=1) * scale + bias_ref[sub]
                s = jnp.where(valid, s, NEG)
                sink = sink_ref[2 * p_ + sub]
                m = jnp.maximum(jnp.max(s, axis=1, keepdims=True), sink)
                p = jnp.exp(s - m)
                denom = jnp.sum(p, axis=1, keepdims=True) + jnp.exp(sink - m)
                pn = (p / denom).astype(BF16)
                outs.append(_dot(pn[:, :W], v_prev) + _dot(pn[:, W:], v_own))
                lse_ref[sub, rs, :] = jnp.broadcast_to(m + jnp.log(denom), (W, LANES))
            o_ref[rs, :] = jnp.where(lane < 64, outs[0], outs[1]).astype(o_ref.dtype)

    qspec = pl.BlockSpec((SWA_TB, LANES), lambda p, i: (i, p))
    own = pl.BlockSpec((None, SWA_TB, LANES), lambda p, i: (p // 2, i, 0))
    prev = pl.BlockSpec((None, W, LANES), lambda p, i: (p // 2, jnp.maximum(SWA_SUB * i - 1, 0), 0))
    stat = pl.BlockSpec((2, SWA_TB, LANES), lambda p, i: (p, i, 0))
    return pl.pallas_call(
        body, name="swa_fwd", grid=(4, nb),
        in_specs=[pl.BlockSpec(memory_space=pltpu.SMEM), qspec, own, prev, own, prev,
                  pl.BlockSpec((2, W, 2 * W), lambda p, i: (p, 0, 0))],
        out_specs=[qspec, stat],
        out_shape=[jax.ShapeDtypeStruct((T, 512), BF16), jax.ShapeDtypeStruct((SWA_HEADS, T, LANES), F32)],
        compiler_params=_cparams("parallel", "parallel"),
    )(sinks, q, kad, kad, vad, vad, bias)


def _swa_bwd(sinks, q, kad, vad, bias, do, lse, delta, T):
    nb = T // SWA_TB
    scale = SWA_HEAD_DIM ** -0.5
    W = WINDOW

    def body(sink_ref, q_ref, k_ref, kp_ref, v_ref, vp_ref, bias_ref, do_ref, lse_ref, dl_ref,
             dq_ref, dkad_ref, dvad_ref, dbias_ref, dsk_ref):
        p_, i = pl.program_id(0), pl.program_id(1)
        kvh = p_ // 2
        lane = _lane((W, LANES))

        @pl.when((p_ == 0) & (i == 0))
        def _():
            dkad_ref[...] = jnp.zeros_like(dkad_ref)
            dvad_ref[...] = jnp.zeros_like(dvad_ref)

        @pl.when(i == 0)
        def _():
            dbias_ref[...] = jnp.zeros_like(dbias_ref)
            dsk_ref[...] = jnp.zeros_like(dsk_ref)

        for r in range(SWA_SUB):
            rs = slice(r * W, (r + 1) * W)
            ps = slice((r - 1) * W, r * W)
            qr, dor = q_ref[rs, :], do_ref[rs, :]
            k_own, v_own = k_ref[rs, :], v_ref[rs, :]
            k_prev = kp_ref[...] if r == 0 else k_ref[ps, :]
            v_prev = vp_ref[...] if r == 0 else v_ref[ps, :]
            valid = _swa_valid(r, i)
            own_row = pl.multiple_of(i * SWA_TB + r * W, W)
            dqs = []
            dk_own = jnp.zeros((W, LANES), F32)
            dk_prev = jnp.zeros((W, LANES), F32)
            dv_own = jnp.zeros((W, LANES), F32)
            dv_prev = jnp.zeros((W, LANES), F32)
            for sub in range(2):
                hm = (lane >= 64) if sub else (lane < 64)
                qh = jnp.where(hm, qr, jnp.zeros_like(qr))
                doh = jnp.where(hm, dor, jnp.zeros_like(dor))
                s = jnp.concatenate([_dot(qh, k_prev, NT), _dot(qh, k_own, NT)], axis=1) * scale + bias_ref[sub]
                s = jnp.where(valid, s, NEG)
                lse_b = lse_ref[sub, rs, :]
                dl_b = dl_ref[sub, rs, :]
                p = jnp.exp(s - jnp.tile(lse_b, (1, 2)))
                dp = jnp.concatenate([_dot(doh, v_prev, NT), _dot(doh, v_own, NT)], axis=1)
                ds = p * (dp - jnp.tile(dl_b, (1, 2)))
                dbias_ref[sub] += ds
                sink = sink_ref[2 * p_ + sub]
                dsk_ref[sub:sub + 1, :] += jnp.sum(jnp.exp(sink - lse_b) * dl_b, axis=0, keepdims=True)
                dsb = ds.astype(BF16)
                pb = p.astype(BF16)
                dqs.append((_dot(dsb[:, :W], k_prev) + _dot(dsb[:, W:], k_own)) * scale)
                dk_prev += _dot(dsb[:, :W], qh, TN) * scale
                dk_own += _dot(dsb[:, W:], qh, TN) * scale
                dv_prev += _dot(pb[:, :W], doh, TN)
                dv_own += _dot(pb[:, W:], doh, TN)
            dq_ref[rs, :] = jnp.where(lane < 64, dqs[0], dqs[1])
            dkad_ref[kvh, pl.ds(own_row, W), :] += dk_own
            dvad_ref[kvh, pl.ds(own_row, W), :] += dv_own
            if r == 0:
                @pl.when(i > 0)
                def _():
                    prev_row = pl.multiple_of(i * SWA_TB - W, W)
                    dkad_ref[kvh, pl.ds(prev_row, W), :] += dk_prev
                    dvad_ref[kvh, pl.ds(prev_row, W), :] += dv_prev
            else:
                prev_row = pl.multiple_of(i * SWA_TB + (r - 1) * W, W)
                dkad_ref[kvh, pl.ds(prev_row, W), :] += dk_prev
                dvad_ref[kvh, pl.ds(prev_row, W), :] += dv_prev

    qspec = pl.BlockSpec((SWA_TB, LANES), lambda p, i: (i, p))
    own = pl.BlockSpec((None, SWA_TB, LANES), lambda p, i: (p // 2, i, 0))
    prev = pl.BlockSpec((None, W, LANES), lambda p, i: (p // 2, jnp.maximum(SWA_SUB * i - 1, 0), 0))
    stat = pl.BlockSpec((2, SWA_TB, LANES), lambda p, i: (p, i, 0))
    full = pl.BlockSpec((2, T, LANES), lambda p, i: (0, 0, 0))
    return pl.pallas_call(
        body, name="swa_bwd", grid=(4, nb),
        in_specs=[pl.BlockSpec(memory_space=pltpu.SMEM), qspec, own, prev, own, prev,
                  pl.BlockSpec((2, W, 2 * W), lambda p, i: (p, 0, 0)), qspec, stat, stat],
        out_specs=[qspec, full, full, pl.BlockSpec((2, W, 2 * W), lambda p, i: (p, 0, 0)),
                   pl.BlockSpec((None, 8, LANES), lambda p, i: (p, 0, 0))],
        out_shape=[jax.ShapeDtypeStruct((T, 512), F32), jax.ShapeDtypeStruct((2, T, LANES), F32),
                   jax.ShapeDtypeStruct((2, T, LANES), F32), jax.ShapeDtypeStruct((SWA_HEADS, W, 2 * W), F32),
                   jax.ShapeDtypeStruct((4, 8, LANES), F32)],
        compiler_params=_cparams("arbitrary", "arbitrary"),
    )(sinks, q, kad, kad, vad, vad, bias, do, lse, delta)


def _mem_fwd(q, mk, mv, T, tq):
    scale = MEM_HEAD_DIM ** -0.5

    def body(q_ref, k_ref, v_ref, o_ref, lse_ref):
        s = _dot(q_ref[...], k_ref[...], NT) * scale
        m = jnp.max(s, axis=1, keepdims=True)
        p = jnp.exp(s - m)
        l = jnp.sum(p, axis=1, keepdims=True)
        o_ref[...] = _dot((p / l).astype(BF16), v_ref[...]).astype(o_ref.dtype)
        lse_ref[...] = jnp.broadcast_to(m + jnp.log(l), (tq, LANES))

    qspec = pl.BlockSpec((tq, LANES), lambda h, i: (i, h))
    kspec = pl.BlockSpec((N_MEM, LANES), lambda h, i: (0, h))
    return pl.pallas_call(
        body, name="mem_fwd", grid=(MEM_HEADS, T // tq),
        in_specs=[qspec, kspec, kspec],
        out_specs=[qspec, pl.BlockSpec((None, tq, LANES), lambda h, i: (h, i, 0))],
        out_shape=[jax.ShapeDtypeStruct((T, 512), BF16), jax.ShapeDtypeStruct((MEM_HEADS, T, LANES), F32)],
        compiler_params=_cparams("parallel", "parallel"),
    )(q, mk, mv)


def _mem_bwd(q, mk, mv, do, lse, delta, T, tq):
    scale = MEM_HEAD_DIM ** -0.5
    rep = N_MEM // LANES

    def body(q_ref, k_ref, v_ref, do_ref, lse_ref, dl_ref, dq_ref, dk_ref, dv_ref):
        i = pl.program_id(1)

        @pl.when(i == 0)
        def _():
            dk_ref[...] = jnp.zeros_like(dk_ref)
            dv_ref[...] = jnp.zeros_like(dv_ref)

        qv, dov = q_ref[...], do_ref[...]
        s = _dot(qv, k_ref[...], NT) * scale
        p = jnp.exp(s - jnp.tile(lse_ref[...], (1, rep)))
        dp = _dot(dov, v_ref[...], NT)
        ds = p * (dp - jnp.tile(dl_ref[...], (1, rep)))
        dsb = ds.astype(BF16)
        dq_ref[...] = _dot(dsb, k_ref[...]) * scale
        dk_ref[...] += _dot(dsb, qv, TN) * scale
        dv_ref[...] += _dot(p.astype(BF16), dov, TN)

    qspec = pl.BlockSpec((tq, LANES), lambda h, i: (i, h))
    kspec = pl.BlockSpec((N_MEM, LANES), lambda h, i: (0, h))
    stat = pl.BlockSpec((None, tq, LANES), lambda h, i: (h, i, 0))
    return pl.pallas_call(
        body, name="mem_bwd", grid=(MEM_HEADS, T // tq),
        in_specs=[qspec, kspec, kspec, qspec, stat, stat],
        out_specs=[qspec, kspec, kspec],
        out_shape=[jax.ShapeDtypeStruct((T, 512), F32), jax.ShapeDtypeStruct((N_MEM, 512), F32),
                   jax.ShapeDtypeStruct((N_MEM, 512), F32)],
        compiler_params=_cparams("arbitrary", "arbitrary"),
    )(q, mk, mv, do, lse, delta)


def _mem_prep_fwd(mem, g_mem, w_kv, kn_gain, gm128):
    def body(mem_ref, g_ref, w_ref, kn_ref, gm_ref, memn_o, kv_o, mk_o, mv_o):
        xhat, _ = _rms_rows(mem_ref[...], None)
        memn = (xhat * g_ref[...]).astype(BF16)
        memn_o[...] = memn
        kv = _dot(memn, w_ref[...])
        kv_o[...] = kv
        gm = gm_ref[...]
        for c in range(4):
            sl = slice(c * LANES, (c + 1) * LANES)
            y, _ = _head_norm(kv[:, sl], gm, kn_ref[...])
            mk_o[:, sl] = y.astype(BF16)
        mv_o[...] = kv[:, 512:].astype(BF16)

    vm = pl.BlockSpec(memory_space=pltpu.VMEM)
    return pl.pallas_call(
        body, name="mem_prep_fwd", in_specs=[vm] * 5, out_specs=[vm] * 4,
        out_shape=[jax.ShapeDtypeStruct((N_MEM, D_MODEL), BF16), jax.ShapeDtypeStruct((N_MEM, D_MODEL), F32),
                   jax.ShapeDtypeStruct((N_MEM, 512), BF16), jax.ShapeDtypeStruct((N_MEM, 512), BF16)],
        compiler_params=pltpu.CompilerParams(vmem_limit_bytes=VMEM_LIMIT),
    )(mem, g_mem, w_kv, kn_gain, gm128)


def _mem_prep_bwd(mem, g_mem, memn, kv, w_kv, kn_gain, gm128, dmk, dmv):
    def body(mem_ref, g_ref, memn_ref, kv_ref, w_ref, kn_ref, gm_ref, dmk_ref, dmv_ref, dw_o, dg_o, dkn_o, dkv_s):
        gm = gm_ref[...]
        dkn = jnp.zeros((1, LANES), F32)
        for c in range(4):
            sl = slice(c * LANES, (c + 1) * LANES)
            dx, dg = _head_norm_bwd(dmk_ref[:, sl], kv_ref[:, sl], gm, kn_ref[...])
            dkv_s[:, sl] = dx.astype(BF16)
            dkn = dkn + dg
        dkn_o[...] = dkn
        dkv_s[:, 512:] = dmv_ref[...].astype(BF16)
        dkv = dkv_s[...]
        dw_o[...] = _dot(memn_ref[...], dkv, TN)
        dmemn = _dot(dkv, w_ref[...], NT)
        xhat, _ = _rms_rows(mem_ref[...], None)
        dg_o[...] = jnp.sum(dmemn * xhat, axis=0, keepdims=True)

    vm = pl.BlockSpec(memory_space=pltpu.VMEM)
    return pl.pallas_call(
        body, name="mem_prep_bwd", in_specs=[vm] * 9, out_specs=[vm] * 3,
        out_shape=[jax.ShapeDtypeStruct((D_MODEL, D_MODEL), F32), jax.ShapeDtypeStruct((1, D_MODEL), F32),
                   jax.ShapeDtypeStruct((1, LANES), F32)],
        scratch_shapes=[pltpu.VMEM((N_MEM, D_MODEL), BF16)],
        compiler_params=pltpu.CompilerParams(vmem_limit_bytes=VMEM_LIMIT),
    )(mem, g_mem, memn, kv, w_kv, kn_gain, gm128, dmk, dmv)


SLOT_O = D_MODEL // N_SHARD


def _merge_fwd(proj, b_gate, o3, w3, T, tb):
    def body(gl_ref, bg_ref, oa_ref, of_ref, om_ref, wa_ref, wf_ref, wm_ref, out_ref):
        o_refs = (oa_ref, of_ref, om_ref)
        w_refs = (wa_ref, wf_ref, wm_ref)
        for n in range(N_SHARD):
            acc = jnp.zeros((tb, SLOT_O), F32)
            for b in range(3):
                c0 = b * D_MODEL + n * SLOT_O
                g = jax.nn.sigmoid(gl_ref[:, c0:c0 + SLOT_O] + bg_ref[:, c0:c0 + SLOT_O])
                acc = acc + g * _dot(o_refs[b][...], w_refs[b][n])
            out_ref[:, n * SLOT_O:(n + 1) * SLOT_O] = acc.astype(out_ref.dtype)

    rows = pl.BlockSpec((tb, 512), lambda i: (i, 0))
    wspec = pl.BlockSpec((N_SHARD, 512, SLOT_O), lambda i: (0, 0, 0))
    return pl.pallas_call(
        body, name="merge_fwd", grid=(T // tb,),
        in_specs=[pl.BlockSpec((tb, GATE_W), lambda i: (i, 1)), pl.BlockSpec((1, GATE_W), lambda i: (0, 0)),
                  rows, rows, rows, wspec, wspec, wspec],
        out_specs=pl.BlockSpec((tb, D_MODEL), lambda i: (i, 0)),
        out_shape=jax.ShapeDtypeStruct((T, D_MODEL), BF16),
        compiler_params=_cparams("parallel"),
    )(proj, b_gate, *o3, *w3)


def _merge_bwd(proj, b_gate, o3, w3, dmerged, T, tb):
    heads = (SWA_HEADS, FOX_HEADS, MEM_HEADS)

    def body(gl_ref, bg_ref, oa_ref, of_ref, om_ref, wa_ref, wf_ref, wm_ref, dm_ref,
             dgl_o, doa_o, dof_o, dom_o, dla_o, dlf_o, dlm_o, dwa_o, dwf_o, dwm_o, dbg_o):
        i = pl.program_id(0)
        o_refs = (oa_ref, of_ref, om_ref)
        w_refs = (wa_ref, wf_ref, wm_ref)
        do_refs = (doa_o, dof_o, dom_o)
        dl_refs = (dla_o, dlf_o, dlm_o)
        dw_refs = (dwa_o, dwf_o, dwm_o)

        @pl.when(i == 0)
        def _():
            for r in dw_refs:
                r[...] = jnp.zeros_like(r)
            dbg_o[...] = jnp.zeros_like(dbg_o)

        lane = _lane((tb, LANES))
        for b in range(3):
            ob = o_refs[b][...]
            do = jnp.zeros((tb, 512), F32)
            for n in range(N_SHARD):
                c0 = b * D_MODEL + n * SLOT_O
                g = jax.nn.sigmoid(gl_ref[:, c0:c0 + SLOT_O] + bg_ref[:, c0:c0 + SLOT_O])
                dm = dm_ref[:, n * SLOT_O:(n + 1) * SLOT_O]
                y = _dot(ob, w_refs[b][n])
                dgl = dm * y * g * (1.0 - g)
                dgl_o[:, c0:c0 + SLOT_O] = dgl.astype(dgl_o.dtype)
                dbg_o[:, c0:c0 + SLOT_O] += jnp.sum(dgl, axis=0, keepdims=True)
                dy = (dm * g).astype(BF16)
                do = do + _dot(dy, w_refs[b][n], NT)
                dw_refs[b][n] += _dot(ob, dy, TN)
            do_refs[b][...] = do.astype(BF16)
            prod = do * ob.astype(F32)
            for c in range(4):
                blk = prod[:, c * LANES:(c + 1) * LANES]
                if heads[b] == 8:
                    lo = jnp.sum(jnp.where(lane < 64, blk, 0.0), axis=1, keepdims=True)
                    hi = jnp.sum(jnp.where(lane >= 64, blk, 0.0), axis=1, keepdims=True)
                    if b == 1:
                        aug = jnp.zeros((tb, LANES), F32)
                        for sub, dl in enumerate((lo, hi)):
                            for e, piece in enumerate(_split3(-dl)):
                                aug = jnp.where(lane == AUG_STRIDE * sub + AUG_C + e, piece.astype(F32), aug)
                        dl_refs[b][:, c * LANES:(c + 1) * LANES] = aug.astype(BF16)
                    else:
                        dl_refs[b][2 * c] = jnp.broadcast_to(lo, (tb, LANES))
                        dl_refs[b][2 * c + 1] = jnp.broadcast_to(hi, (tb, LANES))
                else:
                    dl_refs[b][c] = jnp.broadcast_to(jnp.sum(blk, axis=1, keepdims=True), (tb, LANES))

    rows = pl.BlockSpec((tb, 512), lambda i: (i, 0))
    wspec = pl.BlockSpec((N_SHARD, 512, SLOT_O), lambda i: (0, 0, 0))
    stat = lambda h: pl.BlockSpec((h, tb, LANES), lambda i: (0, i, 0))
    return pl.pallas_call(
        body, name="merge_bwd", grid=(T // tb,),
        in_specs=[pl.BlockSpec((tb, GATE_W), lambda i: (i, 1)), pl.BlockSpec((1, GATE_W), lambda i: (0, 0)),
                  rows, rows, rows, wspec, wspec, wspec, pl.BlockSpec((tb, D_MODEL), lambda i: (i, 0))],
        out_specs=[pl.BlockSpec((tb, GATE_W), lambda i: (i, 0)), rows, rows, rows,
                   stat(8), rows, stat(4), wspec, wspec, wspec, pl.BlockSpec((1, GATE_W), lambda i: (0, 0))],
        out_shape=[jax.ShapeDtypeStruct((T, GATE_W), BF16)] + [jax.ShapeDtypeStruct((T, 512), BF16)] * 3
        + [jax.ShapeDtypeStruct((8, T, LANES), F32), jax.ShapeDtypeStruct((T, 512), BF16),
           jax.ShapeDtypeStruct((4, T, LANES), F32)]
        + [jax.ShapeDtypeStruct((N_SHARD, 512, SLOT_O), F32)] * 3 + [jax.ShapeDtypeStruct((1, GATE_W), F32)],
        compiler_params=_cparams("arbitrary"),
    )(proj, b_gate, *o3, *w3, dmerged)


def _local_step(x, mem, tgt, small, wc, w_kv, w_o3, w_out, w_up, w_down):
    T = x.shape[0]
    tm = min(512, T)
    tile2 = lambda v: jnp.tile(v.reshape(1, -1), (1, LANES // v.size))
    gains = jnp.concatenate([tile2(small["qn_swa"]), tile2(small["kn_swa"]), tile2(small["qn_fox"]),
                             tile2(small["kn_fox"]), tile2(small["qn_mem"]), jnp.zeros((3, LANES), F32)], axis=0)
    kn_mem = small["kn_mem"].reshape(1, LANES)
    bfor = jnp.pad(small["b_forget"].reshape(1, -1), ((0, 0), (0, LANES - FOX_HEADS)))
    gm64 = _group_mean_matrix(64)
    gm128 = _group_mean_matrix(128)
    tb_prep = min(256, T)
    ones = jnp.ones((tb_prep, tb_prep), F32)
    tril = jnp.tril(ones).astype(BF16)
    triu = jnp.triu(ones).astype(BF16)
    bucket = _t5_bucket_matrix()
    g_mix, g_mlp, g_mem = small["g_mix"], small["g_mlp"], small["g_mem"]
    b_gate = small["b_gate"]
    sinks = small["sink_swa"].reshape(-1)

    tl = min(1024, T)
    sq = pl.BlockSpec((tl, D_MODEL), lambda i, j, k: (i, j))
    h = _rmsnorm("rms_mix", x, g_mix, tm)
    (proj,) = _matmul(
        "mm_proj", h, wc, dims=NN, grid=(T // tl, PROJ_W // D_MODEL, 1),
        a_spec=pl.BlockSpec((tl, D_MODEL), lambda i, j, k: (i, 0)),
        b_spec=pl.BlockSpec((D_MODEL, D_MODEL), lambda i, j, k: (0, j)),
        acc_shape=(tl, D_MODEL),
        outs=[(jax.ShapeDtypeStruct((T, PROJ_W), F32), sq)],
        epilogue=_epi_store)
    qa, qf, kf, vf, qm, kad, vad, qf_aug, kf_aug = _prep_fwd(proj, gains, bfor, tril, gm64, gm128, T, tb_prep)
    bias = _swa_bias(small["rel_bias"], bucket)
    o_swa, lse_swa = _swa_fwd(sinks, qa, kad, vad, bias, T)
    o_fox, qf_aug_bwd = _fox_fwd(qf, qf_aug, kf, kf_aug, vf, T, tm)
    memn, kv, mk, mv = _mem_prep_fwd(mem, g_mem, w_kv, kn_mem, gm128)
    o_mem, lse_mem = _mem_fwd(qm, mk, mv, T, tm)
    o3 = (o_swa, o_fox, o_mem)
    merged = _merge_fwd(proj, b_gate, o3, w_o3, T, min(256, T))

    def epi_residual(acc, extra_refs, out_refs, ij):
        out_refs[0][...] = extra_refs[0][...] + acc

    row_full = pl.BlockSpec((tm, D_MODEL), lambda i, j, k: (i, 0))
    row_big = pl.BlockSpec((tl, D_MODEL), lambda i, j, k: (i, 0))
    whole = pl.BlockSpec((D_MODEL, D_MODEL), lambda i, j, k: (0, 0))
    (x2,) = _matmul(
        "mm_out", merged, w_out, dims=NN, grid=(T // tl, 1, 1),
        a_spec=row_big, b_spec=whole,
        acc_shape=(tl, D_MODEL), extra=[(x, row_big)],
        outs=[(jax.ShapeDtypeStruct((T, D_MODEL), F32), row_big)], epilogue=epi_residual)
    hm = _rmsnorm("rms_mlp", x2, g_mlp, tm)

    def epi_relu2(acc, extra_refs, out_refs, ij):
        out_refs[0][...] = acc
        r = jnp.maximum(acc, 0.0)
        out_refs[1][...] = (r * r).astype(BF16)

    up, u = _matmul(
        "mm_up", hm, w_up, dims=NN, grid=(T // tl, N_SHARD, 1),
        a_spec=row_big, b_spec=pl.BlockSpec((None, D_MODEL, D_MODEL), lambda i, j, k: (j, 0, 0)),
        acc_shape=(tl, D_MODEL),
        outs=[(jax.ShapeDtypeStruct((T, D_FF), F32), sq), (jax.ShapeDtypeStruct((T, D_FF), BF16), sq)],
        epilogue=epi_relu2)

    def epi_loss(acc, extra_refs, out_refs, ij):
        y = extra_refs[0][...] + acc
        err = y - extra_refs[1][...]
        out_refs[0][...] = err * (1.0 / D_MODEL)
        sq = jnp.sum(jnp.sum(err * err, axis=1, keepdims=True), axis=0, keepdims=True)

        @pl.when(ij[0] == 0)
        def _():
            out_refs[1][...] = jnp.zeros_like(out_refs[1])

        out_refs[1][...] += jnp.broadcast_to(sq, out_refs[1].shape)

    kblk = pl.BlockSpec((tl, D_MODEL), lambda i, j, k: (i, k))
    dy, loss_acc = _matmul(
        "mm_down", u, w_down, dims=NN, grid=(T // tl, 1, N_SHARD),
        a_spec=kblk, b_spec=pl.BlockSpec((D_MODEL, D_MODEL), lambda i, j, k: (k, 0)),
        acc_shape=(tl, D_MODEL), extra=[(x2, row_big), (tgt, row_big)],
        outs=[(jax.ShapeDtypeStruct((T, D_MODEL), F32), row_big),
              (jax.ShapeDtypeStruct((8, LANES), F32), pl.BlockSpec((8, LANES), lambda i, j, k: (0, 0)))],
        epilogue=epi_loss)
    loss = loss_acc[0, 0] * (0.5 / D_MODEL)

    def epi_dup(acc, extra_refs, out_refs, ij):
        out_refs[0][...] = (acc * (2.0 * jnp.maximum(extra_refs[0][...], 0.0))).astype(BF16)

    (dup,) = _matmul(
        "mm_dup", dy, w_down, dims=NT, grid=(T // tl, N_SHARD, 1),
        a_spec=row_big, b_spec=pl.BlockSpec((D_MODEL, D_MODEL), lambda i, j, k: (j, 0)),
        acc_shape=(tl, D_MODEL), extra=[(up, sq)],
        outs=[(jax.ShapeDtypeStruct((T, D_FF), BF16), sq)], epilogue=epi_dup)

    nkt = T // tl
    t_rows = pl.BlockSpec((tl, D_MODEL), lambda i, j, k: (k, i))
    t_cols = pl.BlockSpec((tl, D_MODEL), lambda i, j, k: (k, j))
    (d_w_down,) = _matmul(
        "mm_dw_down", u, dy, dims=TN, grid=(N_SHARD, 1, nkt),
        a_spec=t_rows, b_spec=t_cols, acc_shape=(D_MODEL, D_MODEL),
        outs=[(jax.ShapeDtypeStruct((D_FF, D_MODEL), F32), pl.BlockSpec((D_MODEL, D_MODEL), lambda i, j, k: (i, 0)))],
        epilogue=_epi_store)
    (d_w_up,) = _matmul(
        "mm_dw_up", hm, dup, dims=TN, grid=(1, N_SHARD, nkt),
        a_spec=t_rows, b_spec=t_cols, acc_shape=(D_MODEL, D_MODEL),
        outs=[(jax.ShapeDtypeStruct((N_SHARD, D_MODEL, D_MODEL), F32),
               pl.BlockSpec((None, D_MODEL, D_MODEL), lambda i, j, k: (j, 0, 0)))],
        epilogue=_epi_store)

    def epi_rms_bwd(acc, extra_refs, out_refs, ij):
        dx, dg = _rmsnorm_bwd_rows(acc, extra_refs[0][...], extra_refs[1][...])
        out_refs[0][...] = dx + extra_refs[2][...]

        @pl.when(ij[0] == 0)
        def _():
            out_refs[1][...] = jnp.zeros_like(out_refs[1])

        out_refs[1][...] += dg

    gain_spec = pl.BlockSpec((1, D_MODEL), lambda i, j, k: (0, 0))
    dx2, d_g_mlp = _matmul(
        "mm_dhm", dup, w_up, dims=NT, grid=(T // tl, 1, N_SHARD),
        a_spec=kblk, b_spec=pl.BlockSpec((None, D_MODEL, D_MODEL), lambda i, j, k: (k, 0, 0)),
        acc_shape=(tl, D_MODEL), extra=[(x2, row_big), (g_mlp, gain_spec), (dy, row_big)],
        outs=[(jax.ShapeDtypeStruct((T, D_MODEL), F32), row_big), (jax.ShapeDtypeStruct((1, D_MODEL), F32), gain_spec)],
        epilogue=epi_rms_bwd)

    (dmerged,) = _matmul(
        "mm_dmerged", dx2, w_out, dims=NT, grid=(T // tl, 1, 1),
        a_spec=row_big, b_spec=whole,
        acc_shape=(tl, D_MODEL), outs=[(jax.ShapeDtypeStruct((T, D_MODEL), F32), row_big)], epilogue=_epi_store)
    (d_w_out,) = _matmul(
        "mm_dw_out", merged, dx2, dims=TN, grid=(1, 1, nkt),
        a_spec=t_rows, b_spec=t_cols, acc_shape=(D_MODEL, D_MODEL),
        outs=[(jax.ShapeDtypeStruct((D_MODEL, D_MODEL), F32), whole)],
        epilogue=_epi_store)
    (dgl, do_swa, do_fox, do_mem, dl_swa, do_fox_aug, dl_mem, d_wo_swa, d_wo_fox, d_wo_mem, d_b_gate) = _merge_bwd(
        proj, b_gate, o3, w_o3, dmerged, T, min(256, T))

    dqa, dkad, dvad, dbias, dsk = _swa_bwd(sinks, qa, kad, vad, bias, do_swa, lse_swa, dl_swa, T)
    dqf, dqf_aug, dkf, dkf_aug, dvf = _fox_bwd(qf, qf_aug_bwd, kf, kf_aug, vf, do_fox, do_fox_aug, T, tm)
    dqm, dmk, dmv = _mem_bwd(qm, mk, mv, do_mem, lse_mem, dl_mem, T, tm)
    d_w_kv, d_g_mem, d_kn_mem = _mem_prep_bwd(mem, g_mem, memn, kv, w_kv, kn_mem, gm128, dmk, dmv)
    d_rel = _swa_bias_bwd(dbias, bucket)
    aug_lane = lambda a, lane: a.reshape(T, FOX_HEADS // 2, LANES)[:, :, lane:lane + AUG_STRIDE + 1:AUG_STRIDE]
    dc_queries = aug_lane(dqf_aug, AUG_C).reshape(T, FOX_HEADS)
    dc_keys = aug_lane(dkf_aug, AUG_NEG_C).reshape(T, FOX_HEADS)
    dccol = jnp.pad(dc_queries - dc_keys, ((0, 0), (0, LANES - FOX_HEADS)))
    dlo, gacc = _prep_bwd(proj, dqa, dkad, dvad, dqf, dkf, dvf, dqm, dccol, gains, bfor, triu, gm64, gm128, T, tb_prep)

    def dwc_half(name, dpart):
        (res,) = _matmul(
            name, h, dpart, dims=TN, grid=(1, LO_W // D_MODEL, nkt),
            a_spec=t_rows, b_spec=t_cols, acc_shape=(D_MODEL, D_MODEL),
            outs=[(jax.ShapeDtypeStruct((D_MODEL, LO_W), F32), pl.BlockSpec((D_MODEL, D_MODEL), lambda i, j, k: (0, j)))],
            epilogue=_epi_store)
        return res

    d_wc_lo = dwc_half("mm_dwc_lo", dlo)
    d_wc_gl = dwc_half("mm_dwc_gl", dgl)
    (dh_lo,) = _matmul(
        "mm_dh_lo", dlo, wc, dims=NT, grid=(T // tl, 1, LO_W // D_MODEL),
        a_spec=kblk, b_spec=pl.BlockSpec((D_MODEL, D_MODEL), lambda i, j, k: (0, k)),
        acc_shape=(tl, D_MODEL), outs=[(jax.ShapeDtypeStruct((T, D_MODEL), F32), row_big)], epilogue=_epi_store)

    def epi_dx(acc, extra_refs, out_refs, ij):
        dhh = acc + extra_refs[3][...]
        dx, dg = _rmsnorm_bwd_rows(dhh, extra_refs[0][...], extra_refs[1][...])
        out_refs[0][...] = dx + extra_refs[2][...]

        @pl.when(ij[0] == 0)
        def _():
            out_refs[1][...] = jnp.zeros_like(out_refs[1])

        out_refs[1][...] += dg

    grad_x, d_g_mix = _matmul(
        "mm_dh_gl", dgl, wc, dims=NT, grid=(T // tm, 1, GATE_W // D_MODEL),
        a_spec=pl.BlockSpec((tm, D_MODEL), lambda i, j, k: (i, k)),
        b_spec=pl.BlockSpec((D_MODEL, D_MODEL), lambda i, j, k: (0, k + LO_W // D_MODEL)),
        acc_shape=(tm, D_MODEL), extra=[(x, row_full), (g_mix, gain_spec), (dx2, row_full), (dh_lo, row_full)],
        outs=[(jax.ShapeDtypeStruct((T, D_MODEL), F32), row_full), (jax.ShapeDtypeStruct((1, D_MODEL), F32), gain_spec)],
        epilogue=epi_dx)

    fold64 = lambda row: (row[:64] + row[64:]).reshape(1, 64)
    grads = {
        "g_mix": d_g_mix, "b_gate": d_b_gate, "b_forget": gacc[5, :FOX_HEADS].reshape(1, FOX_HEADS),
        "qn_swa": fold64(gacc[0]), "kn_swa": fold64(gacc[1]),
        "sink_swa": -dsk[:, :2, 0].reshape(1, SWA_HEADS), "rel_bias": d_rel[:, :SWA_HEADS],
        "qn_fox": fold64(gacc[2]), "kn_fox": fold64(gacc[3]),
        "g_mem": d_g_mem, "qn_mem": gacc[4].reshape(1, LANES), "kn_mem": d_kn_mem, "g_mlp": d_g_mlp,
        "wc_lo": d_wc_lo, "wc_gl": d_wc_gl, "w_mem_kv": d_w_kv,
        "w_o_swa": d_wo_swa, "w_o_fox": d_wo_fox, "w_o_mem": d_wo_mem,
        "w_out": d_w_out, "w_mlp_up": d_w_up, "w_mlp_down": d_w_down,
    }
    return loss, grad_x, grads


MESH = pl.DeviceIdType.MESH
ANY = pl.BlockSpec(memory_space=pl.ANY)


def _place():
    x, y, c = lax.axis_index("x"), lax.axis_index("y"), lax.axis_index("c")
    chips = [(1 - x, y), (x, 1 - y), (1 - x, 1 - y)]
    return x, y, c, chips


def _all_gather_shards(slots):
    n = len(slots)

    def body(*refs):
        out = refs[n:2 * n]
        ici_send, ici_recv, d2d_send, d2d_recv = refs[2 * n:]
        x, y, c, chips = _place()
        sibling = (x, y, 1 - c)
        me = 2 * x + y

        def half(a, who):
            hr = slots[a].shape[1] // 2
            return pl.ds(pl.multiple_of(who * hr, hr), hr)

        def ici(a, j, slot, to):
            return pltpu.make_async_remote_copy(
                src_ref=out[a].at[me, half(a, c)], dst_ref=out[a].at[slot, half(a, c)],
                send_sem=ici_send.at[3 * a + j], recv_sem=ici_recv.at[3 * a + j], device_id=to, device_id_type=MESH)

        def d2d(a, j, slot, which):
            part = out[a].at[slot, half(a, which)]
            return pltpu.make_async_remote_copy(
                src_ref=part, dst_ref=part, send_sem=d2d_send.at[3 * a + j], recv_sem=d2d_recv.at[3 * a + j],
                device_id=sibling, device_id_type=MESH)

        sends = [ici(a, j, me, (*chip, c)) for a in range(n) for j, chip in enumerate(chips)]
        for cp in sends:
            cp.start()
        passed = []
        for a in range(n):
            for j, (px, py) in enumerate(chips):
                ici(a, j, 2 * px + py, (px, py, c)).wait_recv()
                cp = d2d(a, j, 2 * px + py, c)
                cp.start()
                passed.append(cp)
        for a in range(n):
            for j, (px, py) in enumerate(chips):
                d2d(a, j, 2 * px + py, 1 - c).wait_recv()
        for cp in sends + passed:
            cp.wait_send()

    return pl.pallas_call(
        body, name="all_gather_weights",
        in_specs=[ANY] * n, out_specs=[ANY] * n,
        out_shape=[jax.ShapeDtypeStruct(s.shape, s.dtype) for s in slots],
        input_output_aliases={a: a for a in range(n)},
        scratch_shapes=[pltpu.SemaphoreType.DMA((3 * n,))] * 4,
    )(*slots)


def _handshake(peers):
    barrier = pltpu.get_barrier_semaphore()
    for peer in peers:
        pl.semaphore_signal(barrier, inc=1, device_id=peer, device_id_type=MESH)
    pl.semaphore_wait(barrier, len(peers))


def _all_gather_shards_async(slots):
    n = len(slots)
    bufs = [jax.new_ref(s, memory_space=pltpu.MemorySpace.HBM) for s in slots]

    def body(ici_send, ici_recv, d2d_send, d2d_recv):
        x, y, c, chips = _place()
        sibling = (x, y, 1 - c)
        me = 2 * x + y
        _handshake([(px, py, c) for px, py in chips] + [sibling])

        def half(a, who):
            hr = slots[a].shape[1] // 2
            return pl.ds(pl.multiple_of(who * hr, hr), hr)

        def ici(a, j, slot, to):
            return pltpu.make_async_remote_copy(
                src_ref=bufs[a].at[me, half(a, c)], dst_ref=bufs[a].at[slot, half(a, c)],
                send_sem=ici_send.at[3 * a + j], recv_sem=ici_recv.at[3 * a + j], device_id=to, device_id_type=MESH)

        def d2d(a, j, slot, which):
            part = bufs[a].at[slot, half(a, which)]
            return pltpu.make_async_remote_copy(
                src_ref=part, dst_ref=part, send_sem=d2d_send.at[3 * a + j], recv_sem=d2d_recv.at[3 * a + j],
                device_id=sibling, device_id_type=MESH)

        sends = [ici(a, j, me, (*chip, c)) for a in range(n) for j, chip in enumerate(chips)]
        for cp in sends:
            cp.start()
        passed = []
        for a in range(n):
            for j, (px, py) in enumerate(chips):
                ici(a, j, 2 * px + py, (px, py, c)).wait_recv()
                cp = d2d(a, j, 2 * px + py, c)
                cp.start()
                passed.append(cp)
        for a in range(n):
            for j, (px, py) in enumerate(chips):
                d2d(a, j, 2 * px + py, 1 - c).wait_recv()
        for cp in sends + passed:
            cp.wait_send()

    pl.kernel(
        body, mesh=plsc.ScalarSubcoreMesh(axis_name="seq", num_cores=1), name="all_gather_weights_async",
        scratch_types=[pltpu.SemaphoreType.DMA((3 * n,))] * 4,
        compiler_params=pltpu.CompilerParams(collective_id=1),
    )()
    return [b[...] for b in bufs]


def _sequencer_call(name, collective_id, n_sems, body):
    pl.kernel(
        body, mesh=plsc.ScalarSubcoreMesh(axis_name="seq", num_cores=1), name=name,
        scratch_types=[pltpu.SemaphoreType.DMA((n_sems,))] * 2,
        compiler_params=pltpu.CompilerParams(collective_id=collective_id),
    )()


def _hbm_ref(value):
    return jax.new_ref(value, memory_space=pltpu.MemorySpace.HBM)


def _pair_exchange(name, collective_id, gs):
    n = len(gs)
    src = [_hbm_ref(g) for g in gs]
    stage = [jax.empty_ref(jax.ShapeDtypeStruct((N_SHARD, g.shape[1] // 2, g.shape[2]), g.dtype),
                           memory_space=pltpu.MemorySpace.HBM) for g in gs]

    def body(send_sem, recv_sem):
        x, y, c, _ = _place()
        sibling = (x, y, 1 - c)
        _handshake([sibling])
        copies = []
        for a in range(n):
            hr = gs[a].shape[1] // 2
            theirs = pl.ds(pl.multiple_of((1 - c) * hr, hr), hr)
            copies.append(pltpu.make_async_remote_copy(
                src_ref=src[a].at[:, theirs, :], dst_ref=stage[a], send_sem=send_sem.at[a], recv_sem=recv_sem.at[a],
                device_id=sibling, device_id_type=MESH))
        for cp in copies:
            cp.start()
        for cp in copies:
            cp.wait()

    _sequencer_call(name, collective_id, n, body)
    return [s[...] for s in stage]


def _chip_exchange(name, collective_id, sums):
    n = len(sums)
    src = [_hbm_ref(s) for s in sums]
    got = [jax.empty_ref(jax.ShapeDtypeStruct((3,) + s.shape[1:], s.dtype), memory_space=pltpu.MemorySpace.HBM)
           for s in sums]

    def body(send_sem, recv_sem):
        x, y, c, chips = _place()
        _handshake([(px, py, c) for px, py in chips])
        copies = []
        for a in range(n):
            for j, (px, py) in enumerate(chips):
                copies.append(pltpu.make_async_remote_copy(
                    src_ref=src[a].at[2 * px + py], dst_ref=got[a].at[j],
                    send_sem=send_sem.at[3 * a + j], recv_sem=recv_sem.at[3 * a + j],
                    device_id=(px, py, c), device_id_type=MESH))
        for cp in copies:
            cp.start()
        for cp in copies:
            cp.wait()

    _sequencer_call(name, collective_id, 3 * n, body)
    return [g[...] for g in got]


def _pair_gather(name, collective_id, fulls):
    n = len(fulls)
    full = [_hbm_ref(f) for f in fulls]

    def body(send_sem, recv_sem):
        x, y, c, _ = _place()
        sibling = (x, y, 1 - c)
        _handshake([sibling])
        copies = []
        for a in range(n):
            hr = fulls[a].shape[0] // 2
            mine = full[a].at[pl.ds(pl.multiple_of(c * hr, hr), hr)]
            copies.append(pltpu.make_async_remote_copy(
                src_ref=mine, dst_ref=mine, send_sem=send_sem.at[a], recv_sem=recv_sem.at[a],
                device_id=sibling, device_id_type=MESH))
        for cp in copies:
            cp.start()
        for cp in copies:
            cp.wait()

    _sequencer_call(name, collective_id, n, body)
    return [f[...] for f in full]


ELEMENTWISE_BLOCK_ELEMS = 256 * 1024


def _row_block(rows, cols):
    rb = 8
    while rb * 2 * cols <= ELEMENTWISE_BLOCK_ELEMS and rb * 2 <= rows:
        rb *= 2
    return rb


def _pair_sum(name, place, g, stage):
    _, R, C = g.shape
    hr = R // 2
    rb = _row_block(hr, C)
    nb = hr // rb

    def body(place_ref, g_ref, st_ref, sum_bf, own_f32):
        s = pl.program_id(1)
        tot = g_ref[...] + st_ref[...]
        sum_bf[...] = tot.astype(BF16)

        @pl.when(s == place_ref[0])
        def _():
            own_f32[...] = tot

    return pl.pallas_call(
        body, name=name,
        grid_spec=pltpu.PrefetchScalarGridSpec(
            num_scalar_prefetch=1, grid=(nb, N_SHARD),
            in_specs=[pl.BlockSpec((None, rb, C), lambda i, s, pr: (s, pr[1] * nb + i, 0)),
                      pl.BlockSpec((None, rb, C), lambda i, s, pr: (s, i, 0))],
            out_specs=[pl.BlockSpec((None, rb, C), lambda i, s, pr: (s, i, 0)),
                       pl.BlockSpec((rb, C), lambda i, s, pr: (i, 0))]),
        out_shape=[jax.ShapeDtypeStruct((N_SHARD, hr, C), BF16), jax.ShapeDtypeStruct((hr, C), F32)],
        compiler_params=_cparams("arbitrary", "arbitrary"),
    )(place, g, stage)


def _final_sum(name, place, own, got):
    hr, C = own.shape
    rb = _row_block(hr, C)
    nb = hr // rb

    def body(place_ref, own_ref, got_ref, o_ref):
        o_ref[...] = ((own_ref[...] + got_ref[0].astype(F32)) + got_ref[1].astype(F32)) + got_ref[2].astype(F32)

    return pl.pallas_call(
        body, name=name,
        grid_spec=pltpu.PrefetchScalarGridSpec(
            num_scalar_prefetch=1, grid=(nb,),
            in_specs=[pl.BlockSpec((rb, C), lambda i, pr: (i, 0)), pl.BlockSpec((3, rb, C), lambda i, pr: (0, i, 0))],
            out_specs=pl.BlockSpec((rb, C), lambda i, pr: (pr[1] * nb + i, 0))),
        out_shape=jax.ShapeDtypeStruct((2 * hr, C), F32),
        compiler_params=_cparams("arbitrary"),
    )(place, own, got)


def _adamw_math(w, g, m, v):
    m = ADAM_B1 * m + (1.0 - ADAM_B1) * g
    v = ADAM_B2 * v + (1.0 - ADAM_B2) * (g * g)
    m_hat = m / (1.0 - ADAM_B1 ** ADAM_STEP)
    v_hat = v / (1.0 - ADAM_B2 ** ADAM_STEP)
    delta = -ADAM_LR * (m_hat / (jnp.sqrt(v_hat) + ADAM_EPS) + ADAM_WD * w)
    return delta, m, v


def _adamw(name, w, g, m, v):
    R, Cw = w.shape
    Cg = g.shape[1]
    rb = _row_block(R, Cg)

    def body(w_ref, g_ref, m_ref, v_ref, g_o, d_o, m_o, v_o):
        gv = g_ref[...]
        delta, mn, vn = _adamw_math(w_ref[...], gv, m_ref[...], v_ref[...])
        g_o[...] = gv
        d_o[...] = delta
        m_o[...] = mn
        v_o[...] = vn

    blk = pl.BlockSpec((rb, Cg), lambda i: (i, 0))
    return pl.pallas_call(
        body, name=name, grid=(R // rb,),
        in_specs=[blk] * 4, out_specs=[blk] * 4,
        out_shape=[jax.ShapeDtypeStruct((R, Cw), F32)] * 4,
        compiler_params=_cparams("parallel"),
    )(w, g, m, v)


N_DEV = 8
SMALL_ROWS = 64


def _small_allreduce_adamw(g, w, m, v):
    def body(g_ref, w_ref, m_ref, v_ref, all_ref, gs_o, d_o, m_o, v_o, send_sems, recv_sems, local_sem):
        x, y, c, chips = _place()
        me, sibling = (x, y, c), (x, y, 1 - c)

        def rows(px, py, pc):
            return all_ref.at[pl.ds(pl.multiple_of((4 * px + 2 * py + pc) * SMALL_ROWS, SMALL_ROWS), SMALL_ROWS), :]

        def copy(k, block, to, src=None):
            return pltpu.make_async_remote_copy(
                src_ref=rows(*block) if src is None else src, dst_ref=rows(*block),
                send_sem=send_sems.at[k], recv_sem=recv_sems.at[k], device_id=to, device_id_type=MESH)

        mine = pltpu.make_async_copy(g_ref, rows(*me), local_sem)
        mine.start()
        first = [copy(0, me, sibling, src=g_ref)]
        first += [copy(1 + j, me, (*chip, c), src=g_ref) for j, chip in enumerate(chips)]
        for cp in first:
            cp.start()
        passed = [copy(4 + j, (*chip, c), sibling) for j, chip in enumerate(chips)]
        for j, chip in enumerate(chips):
            copy(1 + j, (*chip, c), me).wait_recv()
            passed[j].start()
        copy(0, sibling, me).wait_recv()
        for j, chip in enumerate(chips):
            copy(4 + j, (*chip, 1 - c), me).wait_recv()
        for cp in first + passed:
            cp.wait_send()
        mine.wait()

        tot = all_ref[0:SMALL_ROWS, :]
        for d in range(1, N_DEV):
            tot = tot + all_ref[d * SMALL_ROWS:(d + 1) * SMALL_ROWS, :]
        delta, mn, vn = _adamw_math(w_ref[...], tot, m_ref[...], v_ref[...])
        gs_o[...] = tot
        d_o[...] = delta
        m_o[...] = mn
        v_o[...] = vn

    vm = pl.BlockSpec(memory_space=pltpu.VMEM)
    shp = jax.ShapeDtypeStruct((SMALL_ROWS, LANES), F32)
    res = pl.pallas_call(
        body, name="small_allreduce_adamw", in_specs=[vm] * 4, out_specs=[vm] * 5,
        out_shape=[jax.ShapeDtypeStruct((N_DEV * SMALL_ROWS, LANES), F32), shp, shp, shp, shp],
        scratch_shapes=[pltpu.SemaphoreType.DMA((7,)), pltpu.SemaphoreType.DMA((7,)), pltpu.SemaphoreType.DMA],
    )(g, w, m, v)
    return res[1:]


SMALL_NAMES = ("g_mix", "b_gate", "b_forget", "qn_swa", "kn_swa", "sink_swa", "rel_bias", "qn_fox", "kn_fox",
               "g_mem", "qn_mem", "kn_mem", "g_mlp")
BIG_NAMES = ("w_in", "w_mem_kv", "w_o_swa", "w_o_fox", "w_o_mem", "w_out", "w_mlp_up", "w_mlp_down")
WEIGHT_NAMES = ("g_mix", "w_in", "b_gate", "b_forget", "qn_swa", "kn_swa", "sink_swa", "rel_bias", "qn_fox", "kn_fox",
                "g_mem", "w_mem_kv", "qn_mem", "kn_mem", "w_o_swa", "w_o_fox", "w_o_mem", "w_out", "g_mlp",
                "w_mlp_up", "w_mlp_down")


def _pack_small(parts, extra=None):
    rows = []
    for n in SMALL_NAMES:
        flat = parts[n].reshape(-1).astype(F32)
        flat = jnp.pad(flat, (0, (-flat.size) % LANES))
        rows.append(flat.reshape(-1, LANES))
    if extra is not None:
        rows.append(jnp.pad(extra.reshape(1, 1), ((0, 0), (0, LANES - 1))))
    packed = jnp.concatenate(rows, axis=0)
    return jnp.pad(packed, ((0, SMALL_ROWS - packed.shape[0]), (0, 0)))


def _unpack_small(packed, shapes):
    out, r = {}, 0
    for n in SMALL_NAMES:
        size = math.prod(shapes[n])
        nr = -(-size // LANES)
        out[n] = packed[r:r + nr].reshape(-1)[:size].reshape(shapes[n])
        r += nr
    return out, packed[r, 0]


def _reorder_w_in(w_full):
    seg = lambda a, b: w_full[:, a:b]
    pad = jnp.zeros((w_full.shape[0], C_GL - C_FL - FOX_HEADS), w_full.dtype)
    return jnp.concatenate([seg(0, 512), seg(768, 1280), seg(1280, 1792), seg(1792, 2304), seg(2312, 2824),
                            seg(512, 640), seg(640, 768), seg(2304, 2312), pad, seg(2824, IN_WIDTH)], axis=1)


def _restore_w_in(lo, gl):
    s = lambda a, b: lo[:, a:b]
    return jnp.concatenate([s(C_QA, C_QA + 512), s(C_KA, C_KA + 128), s(C_VA, C_VA + 128), s(C_QF, C_QF + 512),
                            s(C_KF, C_KF + 512), s(C_VF, C_VF + 512), s(C_FL, C_FL + FOX_HEADS), s(C_QM, C_QM + 512),
                            gl], axis=1)


def kernel(x, mem, g_mix, w_in, b_gate, b_forget, qn_swa, kn_swa, sink_swa, rel_bias, qn_fox, kn_fox, g_mem, w_mem_kv, qn_mem, kn_mem, w_o_swa, w_o_fox, w_o_mem, w_out, g_mlp, w_mlp_up, w_mlp_down, loss_target, m_g_mix, m_w_in, m_b_gate, m_b_forget, m_qn_swa, m_kn_swa, m_sink_swa, m_rel_bias, m_qn_fox, m_kn_fox, m_g_mem, m_w_mem_kv, m_qn_mem, m_kn_mem, m_w_o_swa, m_w_o_fox, m_w_o_mem, m_w_out, m_g_mlp, m_w_mlp_up, m_w_mlp_down, v_g_mix, v_w_in, v_b_gate, v_b_forget, v_qn_swa, v_kn_swa, v_sink_swa, v_rel_bias, v_qn_fox, v_kn_fox, v_g_mem, v_w_mem_kv, v_qn_mem, v_kn_mem, v_w_o_swa, v_w_o_fox, v_w_o_mem, v_w_out, v_g_mlp, v_w_mlp_up, v_w_mlp_down):
    given = dict(locals())
    W = {n: given[n] for n in WEIGHT_NAMES}
    M = {n: given["m_" + n] for n in WEIGHT_NAMES}
    V = {n: given["v_" + n] for n in WEIGHT_NAMES}
    pad_in = ((0, 0), (0, IN_SHARD_PAD - IN_SHARD))

    shards = [jnp.pad(w_in[0].astype(BF16), pad_in)] + [W[n][0].astype(BF16) for n in BIG_NAMES[1:]]
    slots = [jnp.broadcast_to(s[None], (N_SHARD,) + s.shape) for s in shards]
    (g_in,) = _all_gather_shards(slots[:1])
    g_in, late = lax.optimization_barrier((g_in, slots[1:]))
    g_kv, g_oa, g_of, g_om, g_out, g_up, g_down = _all_gather_shards_async(late)
    w_full = jnp.concatenate([g_in[s, :, :IN_SHARD] for s in range(N_SHARD)], axis=1)
    wc = _reorder_w_in(w_full)
    small = {n: (W[n] if n == "rel_bias" else W[n].reshape(1, -1)) for n in SMALL_NAMES}

    loss, grad_x, grads = _local_step(
        x[0], mem[0], loss_target[0], small, wc, g_kv.reshape(D_MODEL, D_MODEL), (g_oa, g_of, g_om),
        g_out.reshape(D_MODEL, D_MODEL), g_up, g_down.reshape(D_FF, D_MODEL))

    d_full = _restore_w_in(grads["wc_lo"], grads["wc_gl"])
    d_in = jnp.stack([jnp.pad(d_full[:, s * IN_SHARD:(s + 1) * IN_SHARD], pad_in) for s in range(N_SHARD)])
    slot_rows = lambda a: a.reshape(N_SHARD, a.shape[0] // N_SHARD, a.shape[1])
    local = {"w_in": d_in, "w_mem_kv": slot_rows(grads["w_mem_kv"]), "w_o_swa": grads["w_o_swa"],
             "w_o_fox": grads["w_o_fox"], "w_o_mem": grads["w_o_mem"], "w_out": slot_rows(grads["w_out"]),
             "w_mlp_up": grads["w_mlp_up"], "w_mlp_down": slot_rows(grads["w_mlp_down"])}
    place = jnp.stack([2 * lax.axis_index("x") + lax.axis_index("y"), lax.axis_index("c")]).astype(jnp.int32)

    out = {}
    for tag, first_id, names in (("early", 2, ("w_mlp_down", "w_mlp_up", "w_out")),
                                 ("late", 5, ("w_in", "w_mem_kv", "w_o_swa", "w_o_fox", "w_o_mem"))):
        mine = [local[n] for n in names]
        staged = _pair_exchange("pair_exchange_" + tag, first_id, mine)
        sums = [_pair_sum("pair_sum_" + n, place, g, st) for n, g, st in zip(names, mine, staged)]
        got = _chip_exchange("chip_exchange_" + tag, first_id + 1, [s[0] for s in sums])
        halves = [_final_sum("final_sum_" + n, place, s[1], r) for n, s, r in zip(names, sums, got)]
        summed = _pair_gather("pair_gather_" + tag, first_id + 2, halves)
        for n, g in zip(names, summed):
            res = _adamw("adamw_" + n, W[n][0], g, M[n][0], V[n][0])
            out[n] = [r.reshape(W[n].shape) for r in res]
    shapes = {n: W[n].shape for n in SMALL_NAMES}
    packed = _small_allreduce_adamw(_pack_small(grads, loss), _pack_small(W), _pack_small(M), _pack_small(V))
    unpacked = [_unpack_small(p, shapes) for p in packed]
    for n in SMALL_NAMES:
        out[n] = [u[0][n] for u in unpacked]
    loss_total = unpacked[0][1]

    return (loss_total, grad_x.reshape(x.shape),
            *[out[n][0] for n in WEIGHT_NAMES], *[out[n][1] for n in WEIGHT_NAMES],
            *[out[n][2] for n in WEIGHT_NAMES], *[out[n][3] for n in WEIGHT_NAMES])
```

```python
import functools
import math

import jax
import jax.numpy as jnp
from jax import lax
from jax.experimental import pallas as pl
from jax.experimental.pallas import tpu as pltpu
from jax.experimental.pallas import tpu_sc as plsc

F32 = jnp.float32
BF16 = jnp.bfloat16

D_MODEL = 1024
N_MEM = 256
SWA_HEADS = 8
SWA_KV_HEADS = 2
SWA_HEAD_DIM = 64
WINDOW = 128
FOX_HEADS = 8
FOX_HEAD_DIM = 64
MEM_HEADS = 4
MEM_HEAD_DIM = 128
D_FF = 4 * D_MODEL
REL_BUCKETS = 32
REL_MAX_DIST = 128
EPS = 1e-6
NEG = -1e30
GATE_W = 3 * D_MODEL
IN_WIDTH = 5896
N_SHARD = 4
IN_SHARD = IN_WIDTH // N_SHARD
IN_SHARD_PAD = 1536

ADAM_LR = 0.001
ADAM_B1 = 0.9
ADAM_B2 = 0.999
ADAM_EPS = 1e-08
ADAM_WD = 0.01
ADAM_STEP = 10

LANES = 128
V7X_VMEM_BYTES = 64 * 1024 * 1024
VMEM_LIMIT = V7X_VMEM_BYTES * 3 // 4

C_QA, C_QF, C_KF, C_VF, C_QM, C_KA, C_VA, C_FL, C_GL = 0, 512, 1024, 1536, 2048, 2560, 2688, 2816, 3072
LO_W = 3072
PROJ_W = 6144

NN = (((1,), (0,)), ((), ()))
NT = (((1,), (1,)), ((), ()))
TN = (((0,), (0,)), ((), ()))


def _dot(a, b, dims=NN):
    return lax.dot_general(a, b, dims, preferred_element_type=F32)


def _cparams(*sem):
    return pltpu.CompilerParams(dimension_semantics=sem, vmem_limit_bytes=VMEM_LIMIT)


def _split3(a):
    hi = a.astype(BF16)
    r1 = a - hi.astype(F32)
    mid = r1.astype(BF16)
    lo = (r1 - mid.astype(F32)).astype(BF16)
    return hi, mid, lo


def _dot3_right(a, g):
    hi, mid, lo = _split3(a)
    return _dot(hi, g) + _dot(mid, g) + _dot(lo, g)


def _dot3_left(g, a):
    hi, mid, lo = _split3(a)
    return _dot(g, hi) + _dot(g, mid) + _dot(g, lo)


def _group_mean_matrix(d):
    r = jnp.arange(LANES)
    return jnp.where((r[:, None] // d) == (r[None, :] // d), 1.0 / d, 0.0).astype(BF16)


def _lane(shape):
    return lax.broadcasted_iota(jnp.int32, shape, len(shape) - 1)


def _matmul(name, a, b, *, dims, grid, a_spec, b_spec, acc_shape, outs, epilogue, extra=()):
    nk = grid[2]
    n_extra = len(extra)

    def body(a_ref, b_ref, *rest):
        extra_refs = rest[:n_extra]
        out_refs = rest[n_extra:n_extra + len(outs)]
        i, j, k = pl.program_id(0), pl.program_id(1), pl.program_id(2)
        part = _dot(a_ref[...].astype(BF16), b_ref[...].astype(BF16), dims)
        if nk == 1:
            epilogue(part, extra_refs, out_refs, (i, j))
            return
        acc_ref = rest[-1]

        @pl.when(k == 0)
        def _():
            acc_ref[...] = part

        @pl.when((k > 0) & (k < nk - 1))
        def _():
            acc_ref[...] += part

        @pl.when(k == nk - 1)
        def _():
            epilogue(acc_ref[...] + part, extra_refs, out_refs, (i, j))

    res = pl.pallas_call(
        body,
        name=name,
        grid=grid,
        in_specs=[a_spec, b_spec] + [s for _, s in extra],
        out_specs=[s for _, s in outs],
        out_shape=[s for s, _ in outs],
        scratch_shapes=[pltpu.VMEM(acc_shape, F32)] if nk > 1 else [],
        compiler_params=_cparams("arbitrary", "arbitrary", "arbitrary"),
    )(a, b, *[x for x, _ in extra])
    return res


def _epi_store(acc, extra_refs, out_refs, ij):
    out_refs[0][...] = acc.astype(out_refs[0].dtype)


def _rms_rows(x, g):
    r = lax.rsqrt(jnp.mean(x * x, axis=-1, keepdims=True) + EPS)
    return x * r, r


def _rmsnorm_bwd_rows(dh, x, g):
    xhat, r = _rms_rows(x, g)
    dxh = dh * g
    dx = r * (dxh - xhat * jnp.mean(dxh * xhat, axis=-1, keepdims=True))
    return dx, jnp.sum(dh * xhat, axis=0, keepdims=True)


def _rmsnorm(name, x, g, tb):
    T, Dm = x.shape

    def body(x_ref, g_ref, o_ref):
        xhat, _ = _rms_rows(x_ref[...], None)
        o_ref[...] = (xhat * g_ref[...]).astype(o_ref.dtype)

    return pl.pallas_call(
        body, name=name, grid=(T // tb,),
        in_specs=[pl.BlockSpec((tb, Dm), lambda i: (i, 0)), pl.BlockSpec((1, Dm), lambda i: (0, 0))],
        out_specs=pl.BlockSpec((tb, Dm), lambda i: (i, 0)),
        out_shape=jax.ShapeDtypeStruct((T, Dm), BF16),
        compiler_params=_cparams("parallel"),
    )(x, g)


def _head_norm(x, gm, gain):
    ms = _dot3_right(x * x, gm)
    r = lax.rsqrt(ms + EPS)
    return x * r * gain, x * r


def _head_norm_bwd(dy, x, gm, gain):
    ms = _dot3_right(x * x, gm)
    r = lax.rsqrt(ms + EPS)
    xhat = x * r
    dxh = dy * gain
    dx = r * (dxh - xhat * _dot3_right(dxh * xhat, gm))
    return dx, jnp.sum(dy * xhat, axis=0, keepdims=True)


def _log_sigmoid(z):
    return jnp.minimum(z, 0.0) - jnp.log(1.0 + jnp.exp(-jnp.abs(z)))


def _prep_fwd(proj, gains, bfor, tril, gm64, gm128, T, tb):
    nb = T // tb

    def body(qa_ref, qf_ref, kf_ref, vf_ref, qm_ref, ka_ref, va_ref, fl_ref, gains_ref, bfor_ref, tril_ref,
             gm64_ref, gm128_ref,
             qa_o, qf_o, kf_o, vf_o, qm_o, kad_o, vad_o, qaug_o, kaug_o, carry):
        i = pl.program_id(0)
        gm64v = gm64_ref[...]
        gm128v = gm128_ref[...]
        lane = _lane((tb, LANES))

        def norm512(src, dst, row, gm, scale=1.0):
            gain = gains_ref[row:row + 1, :]
            for c in range(4):
                sl = slice(c * LANES, (c + 1) * LANES)
                y, _ = _head_norm(src[:, sl], gm, gain)
                dst[:, sl] = (y * scale).astype(dst.dtype)

        norm512(qa_ref, qa_o, 0, gm64v)
        norm512(qf_ref, qf_o, 2, gm64v, FOX_SCALE)
        norm512(kf_ref, kf_o, 3, gm64v)
        norm512(qm_ref, qm_o, 4, gm128v)
        vf_o[...] = vf_ref[...].astype(vf_o.dtype)

        ka_n, _ = _head_norm(ka_ref[...], gm64v, gains_ref[1:2, :])
        ka_r = pltpu.roll(ka_n, 64, 1)
        va = va_ref[...]
        va_r = pltpu.roll(va, 64, 1)
        lo = lane < 64
        kad_o[0] = jnp.where(lo, ka_n, ka_r).astype(kad_o.dtype)
        kad_o[1] = jnp.where(lo, ka_r, ka_n).astype(kad_o.dtype)
        vad_o[0] = jnp.where(lo, va, va_r).astype(vad_o.dtype)
        vad_o[1] = jnp.where(lo, va_r, va).astype(vad_o.dtype)

        @pl.when(i == 0)
        def _():
            carry[...] = jnp.zeros_like(carry)

        logf = jnp.where(lane < FOX_HEADS, _log_sigmoid(fl_ref[...] + bfor_ref[...]), 0.0)
        c = _dot3_left(tril_ref[...], logf) + carry[0:1, :]
        carry[...] = jnp.broadcast_to(c[tb - 1:tb, :], carry.shape)
        for pair in range(FOX_HEADS // 2):
            qaug = jnp.zeros((tb, LANES), F32)
            kaug = jnp.zeros((tb, LANES), F32)
            for sub in range(2):
                col = jnp.sum(jnp.where(lane == 2 * pair + sub, c, 0.0), axis=1, keepdims=True)
                pieces = [p.astype(F32) for p in _split3(col)]
                base = AUG_STRIDE * sub
                for e in range(3):
                    qaug = jnp.where(lane == base + AUG_C + e, pieces[e], qaug)
                    kaug = jnp.where(lane == base + AUG_NEG_C + e, -pieces[e], kaug)
                qaug = jnp.where((lane >= base + AUG_NEG_C) & (lane < base + AUG_NEG_C + 3), 1.0, qaug)
                ones_k = ((lane >= base + AUG_C) & (lane < base + AUG_C + 3)) | (
                    (lane >= base + AUG_STAT) & (lane < base + AUG_STAT + 3))
                kaug = jnp.where(ones_k, 1.0, kaug)
            sl = slice(pair * LANES, (pair + 1) * LANES)
            qaug_o[:, sl] = qaug.astype(BF16)
            kaug_o[:, sl] = kaug.astype(BF16)

    def seg(width, start):
        return pl.BlockSpec((tb, width), lambda i, s=start // width: (i, s))

    const = lambda shape: pl.BlockSpec(shape, lambda i: tuple(0 for _ in shape))
    rows512 = pl.BlockSpec((tb, 512), lambda i: (i, 0))
    outs = pl.pallas_call(
        body, name="prep_fwd", grid=(nb,),
        in_specs=[seg(512, C_QA), seg(512, C_QF), seg(512, C_KF), seg(512, C_VF), seg(512, C_QM),
                  seg(128, C_KA), seg(128, C_VA), seg(128, C_FL),
                  const((8, LANES)), const((1, LANES)), const((tb, tb)), const((LANES, LANES)), const((LANES, LANES))],
        out_specs=[rows512, rows512, rows512, rows512, rows512,
                   pl.BlockSpec((2, tb, LANES), lambda i: (0, i, 0)), pl.BlockSpec((2, tb, LANES), lambda i: (0, i, 0)),
                   rows512, rows512],
        out_shape=[jax.ShapeDtypeStruct((T, 512), BF16)] * 5
        + [jax.ShapeDtypeStruct((2, T, LANES), BF16)] * 2
        + [jax.ShapeDtypeStruct((T, 512), BF16)] * 2,
        scratch_shapes=[pltpu.VMEM((8, LANES), F32)],
        compiler_params=_cparams("arbitrary"),
    )(proj, proj, proj, proj, proj, proj, proj, proj, gains, bfor, tril, gm64, gm128)
    return outs


def _prep_bwd(proj, dqa, dkad, dvad, dqf, dkf, dvf, dqm, dccol, gains, bfor, triu, gm64, gm128, T, tb):
    nb = T // tb

    def body(qa_ref, qf_ref, kf_ref, qm_ref, ka_ref, fl_ref,
             dqa_ref, dkad_ref, dvad_ref, dqf_ref, dkf_ref, dvf_ref, dqm_ref, dc_ref,
             gains_ref, bfor_ref, triu_ref, gm64_ref, gm128_ref,
             dlo_o, gacc_o, carry):
        i = pl.program_id(0)
        gm64v = gm64_ref[...]
        gm128v = gm128_ref[...]
        lane = _lane((tb, LANES))

        @pl.when(i == 0)
        def _():
            carry[...] = jnp.zeros_like(carry)
            gacc_o[...] = jnp.zeros_like(gacc_o)

        def norm512_bwd(dsrc, xsrc, col0, row, gm):
            gain = gains_ref[row:row + 1, :]
            gsum = jnp.zeros((1, LANES), F32)
            for c in range(4):
                sl = slice(c * LANES, (c + 1) * LANES)
                dx, dg = _head_norm_bwd(dsrc[:, sl], xsrc[:, sl], gm, gain)
                dlo_o[:, col0 + c * LANES:col0 + (c + 1) * LANES] = dx.astype(dlo_o.dtype)
                gsum = gsum + dg
            gacc_o[row:row + 1, :] += gsum

        norm512_bwd(dqa_ref, qa_ref, C_QA, 0, gm64v)
        norm512_bwd(dqf_ref, qf_ref, C_QF, 2, gm64v)
        norm512_bwd(dkf_ref, kf_ref, C_KF, 3, gm64v)
        norm512_bwd(dqm_ref, qm_ref, C_QM, 4, gm128v)
        dlo_o[:, C_VF:C_VF + 512] = dvf_ref[...].astype(dlo_o.dtype)

        lo = lane < 64

        def fold(ref):
            f0 = ref[0] + pltpu.roll(ref[0], 64, 1)
            f1 = ref[1] + pltpu.roll(ref[1], 64, 1)
            return jnp.where(lo, f0, f1)

        dka, dg = _head_norm_bwd(fold(dkad_ref), ka_ref[...], gm64v, gains_ref[1:2, :])
        gacc_o[1:2, :] += dg
        dlo_o[:, C_KA:C_KA + LANES] = dka.astype(dlo_o.dtype)
        dlo_o[:, C_VA:C_VA + LANES] = fold(dvad_ref).astype(dlo_o.dtype)

        dc = dc_ref[...]
        dlogf = _dot3_left(triu_ref[...], dc) + carry[0:1, :]
        carry[...] = jnp.broadcast_to(dlogf[0:1, :], carry.shape)
        z = fl_ref[...] + bfor_ref[...]
        dfl = jnp.where(lane < FOX_HEADS, dlogf / (1.0 + jnp.exp(z)), 0.0)
        gacc_o[5:6, :] += jnp.sum(dfl, axis=0, keepdims=True)
        dlo_o[:, C_FL:C_FL + LANES] = dfl.astype(dlo_o.dtype)
        dlo_o[:, C_FL + LANES:C_FL + 2 * LANES] = jnp.zeros((tb, LANES), dlo_o.dtype)

    rev = lambda i: nb - 1 - i

    def seg(width, start):
        return pl.BlockSpec((tb, width), lambda i, s=start // width: (rev(i), s))

    const = lambda shape: pl.BlockSpec(shape, lambda i: tuple(0 for _ in shape))
    rows512 = pl.BlockSpec((tb, 512), lambda i: (rev(i), 0))
    dup = pl.BlockSpec((2, tb, LANES), lambda i: (0, rev(i), 0))
    return pl.pallas_call(
        body, name="prep_bwd", grid=(nb,),
        in_specs=[seg(512, C_QA), seg(512, C_QF), seg(512, C_KF), seg(512, C_QM), seg(128, C_KA), seg(128, C_FL),
                  rows512, dup, dup, rows512, rows512, rows512, rows512,
                  pl.BlockSpec((tb, LANES), lambda i: (rev(i), 0)),
                  const((8, LANES)), const((1, LANES)), const((tb, tb)), const((LANES, LANES)), const((LANES, LANES))],
        out_specs=[pl.BlockSpec((tb, LO_W), lambda i: (rev(i), 0)), const((8, LANES))],
        out_shape=[jax.ShapeDtypeStruct((T, LO_W), BF16), jax.ShapeDtypeStruct((8, LANES), F32)],
        scratch_shapes=[pltpu.VMEM((8, LANES), F32)],
        compiler_params=_cparams("arbitrary"),
    )(proj, proj, proj, proj, proj, proj, dqa, dkad, dvad, dqf, dkf, dvf, dqm, dccol, gains, bfor, triu, gm64, gm128)


FOX_SCALE = FOX_HEAD_DIM ** -0.5
AUG_STRIDE = 16
AUG_C = 0
AUG_NEG_C = 3
AUG_STAT = 6


def _fox_head_mask(sub, rows):
    lane = _lane((rows, 2 * LANES))
    main = (lane >= 64 * sub) & (lane < 64 * sub + 64)
    aug = (lane >= LANES + AUG_STRIDE * sub) & (lane < LANES + AUG_STRIDE * (sub + 1))
    return main | aug


def _fox_fwd(q, qaug, k, kaug, v, T, tq):
    nq = T // tq
    tk = tq
    rep = tk // LANES

    def body(q_ref, qa_ref, k_ref, ka_ref, v_ref, o_ref, qab_ref, m_s, acc_s):
        p_, i, j = pl.program_id(0), pl.program_id(1), pl.program_id(2)

        @pl.when(j == 0)
        def _():
            m_s[...] = jnp.full(m_s.shape, NEG, F32)
            acc_s[...] = jnp.zeros_like(acc_s)

        def step(diagonal):
            q2 = jnp.concatenate([q_ref[...], qa_ref[...]], axis=1)
            k2 = jnp.concatenate([k_ref[...], ka_ref[...]], axis=1)
            v2 = jnp.concatenate([v_ref[...], ka_ref[...]], axis=1)
            if diagonal:
                causal = (lax.broadcasted_iota(jnp.int32, (tq, tk), 1) <= lax.broadcasted_iota(jnp.int32, (tq, tk), 0))
            for sub in range(2):
                qh = jnp.where(_fox_head_mask(sub, tq), q2, jnp.zeros_like(q2))
                s = _dot(qh, k2, NT)
                if diagonal:
                    s = jnp.where(causal, s, NEG)
                m_prev = m_s[sub]
                m_next = jnp.maximum(m_prev, jnp.max(s, axis=1, keepdims=True))
                p = jnp.exp(s - jnp.tile(m_next, (1, rep)))
                alpha = jnp.exp(m_prev - m_next)
                m_s[sub] = m_next
                acc_s[sub] = acc_s[sub] * jnp.tile(alpha, (1, 2)) + _dot(p.astype(BF16), v2)

        @pl.when(j == i)
        def _():
            step(True)

        @pl.when(j < i)
        def _():
            step(False)

        @pl.when(j == nq - 1)
        def _():
            lane = _lane((tq, LANES))
            outs = []
            qab = qa_ref[...].astype(F32)
            for sub in range(2):
                acc = acc_s[sub]
                base = AUG_STRIDE * sub
                l = jnp.sum(jnp.where(lane == base + AUG_C, acc[:, LANES:], 0.0), axis=1, keepdims=True)
                outs.append(acc[:, :LANES] / l)
                lse = jnp.max(m_s[sub], axis=1, keepdims=True) + jnp.log(l)
                pieces = _split3(-lse)
                for e in range(3):
                    qab = jnp.where(lane == base + AUG_STAT + e, pieces[e].astype(F32), qab)
            o_ref[...] = jnp.where(lane < 64, outs[0], outs[1]).astype(o_ref.dtype)
            qab_ref[...] = qab.astype(BF16)

    qspec = pl.BlockSpec((tq, LANES), lambda p, i, j: (i, p))
    kspec = pl.BlockSpec((tk, LANES), lambda p, i, j: (jnp.minimum(j, i), p))
    return pl.pallas_call(
        body, name="fox_fwd", grid=(4, nq, nq),
        in_specs=[qspec, qspec, kspec, kspec, kspec],
        out_specs=[qspec, qspec],
        out_shape=[jax.ShapeDtypeStruct((T, 512), BF16), jax.ShapeDtypeStruct((T, 512), BF16)],
        scratch_shapes=[pltpu.VMEM((2, tq, LANES), F32), pltpu.VMEM((2, tq, 2 * LANES), F32)],
        compiler_params=_cparams("parallel", "parallel", "arbitrary"),
    )(q, qaug, k, kaug, v)


def _fox_bwd(q, qaug, k, kaug, v, do, doaug, T, tq):
    nq = T // tq
    tk = tq

    def body(q_ref, qa_ref, k_ref, ka_ref, v_ref, do_ref, doa_ref,
             dq_ref, dqa_ref, dk_ref, dka_ref, dv_ref, dk_s, dv_s):
        p_, j, i = pl.program_id(0), pl.program_id(1), pl.program_id(2)

        @pl.when((j == 0) & (i == 0))
        def _():
            dq_ref[...] = jnp.zeros_like(dq_ref)
            dqa_ref[...] = jnp.zeros_like(dqa_ref)

        @pl.when(i == 0)
        def _():
            dk_s[...] = jnp.zeros_like(dk_s)
            dv_s[...] = jnp.zeros_like(dv_s)

        def step(diagonal):
            q2 = jnp.concatenate([q_ref[...], qa_ref[...]], axis=1)
            k2 = jnp.concatenate([k_ref[...], ka_ref[...]], axis=1)
            v2 = jnp.concatenate([v_ref[...], ka_ref[...]], axis=1)
            do2 = jnp.concatenate([do_ref[...], doa_ref[...]], axis=1)
            if diagonal:
                causal = (lax.broadcasted_iota(jnp.int32, (tq, tk), 1) <= lax.broadcasted_iota(jnp.int32, (tq, tk), 0))
            dqs = []
            for sub in range(2):
                hm = _fox_head_mask(sub, tq)
                qh = jnp.where(hm, q2, jnp.zeros_like(q2))
                doh = jnp.where(hm, do2, jnp.zeros_like(do2))
                s = _dot(qh, k2, NT)
                if diagonal:
                    s = jnp.where(causal, s, NEG)
                p = jnp.exp(s)
                ds = p * _dot(doh, v2, NT)
                dsb = ds.astype(BF16)
                dv_s[...] += _dot(p.astype(BF16), doh[:, :LANES], TN)
                dk_s[...] += _dot(dsb, qh, TN)
                dqs.append(_dot(dsb, k2))
            dq2 = jnp.where(_fox_head_mask(0, tq), dqs[0], dqs[1])
            qrows = pl.ds(pl.multiple_of(i * tq, tq), tq)
            dq_ref[qrows, :] += dq2[:, :LANES] * FOX_SCALE
            dqa_ref[qrows, :] += dq2[:, LANES:]

        @pl.when(i == j)
        def _():
            step(True)

        @pl.when(i > j)
        def _():
            step(False)

        @pl.when(i == nq - 1)
        def _():
            dk_ref[...] = dk_s[:, :LANES]
            dka_ref[...] = dk_s[:, LANES:]
            dv_ref[...] = dv_s[...]

    qspec = pl.BlockSpec((tq, LANES), lambda p, j, i: (jnp.maximum(i, j), p))
    kspec = pl.BlockSpec((tk, LANES), lambda p, j, i: (j, p))
    resident = pl.BlockSpec((T, LANES), lambda p, j, i: (0, p))
    return pl.pallas_call(
        body, name="fox_bwd", grid=(4, nq, nq),
        in_specs=[qspec, qspec, kspec, kspec, kspec, qspec, qspec],
        out_specs=[resident, resident, kspec, kspec, kspec],
        out_shape=[jax.ShapeDtypeStruct((T, 512), F32)] * 5,
        scratch_shapes=[pltpu.VMEM((tk, 2 * LANES), F32), pltpu.VMEM((tk, LANES), F32)],
        compiler_params=_cparams("arbitrary", "arbitrary", "arbitrary"),
    )(q, qaug, k, kaug, v, do, doaug)


SWA_SUB = 4
SWA_TB = SWA_SUB * WINDOW


def _t5_bucket_matrix():
    t = jnp.arange(WINDOW)[:, None] + WINDOW
    s = jnp.arange(2 * WINDOW)[None, :]
    max_exact = REL_BUCKETS // 2
    d = jnp.maximum(t - s, 0)
    df = jnp.maximum(d, 1).astype(F32)
    large = max_exact + (jnp.log(df / max_exact) / math.log(REL_MAX_DIST / max_exact)
                         * (REL_BUCKETS - max_exact)).astype(jnp.int32)
    large = jnp.minimum(large, REL_BUCKETS - 1)
    return jnp.where(d < max_exact, d, large).astype(jnp.int32)


def _swa_bias(rel_bias, bucket):
    def body(rel_ref, bucket_ref, o_ref):
        b = bucket_ref[...]
        for h in range(SWA_HEADS):
            acc = jnp.zeros(b.shape, F32)
            for r in range(REL_BUCKETS):
                acc = jnp.where(b == r, rel_ref[r, h], acc)
            o_ref[h] = acc

    return pl.pallas_call(
        body, name="swa_bias",
        in_specs=[pl.BlockSpec(memory_space=pltpu.SMEM), pl.BlockSpec(memory_space=pltpu.VMEM)],
        out_specs=pl.BlockSpec(memory_space=pltpu.VMEM),
        out_shape=jax.ShapeDtypeStruct((SWA_HEADS, WINDOW, 2 * WINDOW), F32),
    )(rel_bias, bucket)


def _swa_bias_bwd(dbias, bucket):
    def body(db_ref, bucket_ref, o_ref):
        b = bucket_ref[...]
        lane = _lane((1, LANES))
        for r in range(REL_BUCKETS):
            row = jnp.zeros((1, LANES), F32)
            for h in range(SWA_HEADS):
                part = jnp.sum(jnp.where(b == r, db_ref[h], 0.0), axis=0, keepdims=True)
                tot = jnp.sum(part, axis=1, keepdims=True)
                row = jnp.where(lane == h, tot, row)
            o_ref[r:r + 1, :] = row

    return pl.pallas_call(
        body, name="swa_bias_bwd",
        in_specs=[pl.BlockSpec(memory_space=pltpu.VMEM), pl.BlockSpec(memory_space=pltpu.VMEM)],
        out_specs=pl.BlockSpec(memory_space=pltpu.VMEM),
        out_shape=jax.ShapeDtypeStruct((REL_BUCKETS, LANES), F32),
    )(dbias, bucket)


def _swa_valid(r, i):
    t = lax.broadcasted_iota(jnp.int32, (WINDOW, 2 * WINDOW), 0) + WINDOW
    s = lax.broadcasted_iota(jnp.int32, (WINDOW, 2 * WINDOW), 1)
    dist = t - s
    band = (dist >= 0) & (dist < WINDOW)
    if r == 0:
        band = band & ((s >= WINDOW) | (i > 0))
    return band


def _swa_fwd(sinks, q, kad, vad, bias, T):
    nb = T // SWA_TB
    scale = SWA_HEAD_DIM ** -0.5
    W = WINDOW

    def body(sink_ref, q_ref, k_ref, kp_ref, v_ref, vp_ref, bias_ref, o_ref, lse_ref):
        p_, i = pl.program_id(0), pl.program_id(1)
        lane = _lane((W, LANES))
        for r in range(SWA_SUB):
            rs = slice(r * W, (r + 1) * W)
            ps = slice((r - 1) * W, r * W)
            qr = q_ref[rs, :]
            k_own, v_own = k_ref[rs, :], v_ref[rs, :]
            k_prev = kp_ref[...] if r == 0 else k_ref[ps, :]
            v_prev = vp_ref[...] if r == 0 else v_ref[ps, :]
            valid = _swa_valid(r, i)
            outs = []
            for sub in range(2):
                hm = (lane >= 64) if sub else (lane < 64)
                qh = jnp.where(hm, qr, jnp.zeros_like(qr))
                s = jnp.concatenate([_dot(qh, k_prev, NT), _dot(qh, k_own, NT)], axis=1) * scale + bias_ref[sub]
                s = jnp.where(valid, s, NEG)
                sink = sink_ref[2 * p_ + sub]
                m = jnp.maximum(jnp.max(s, axis=1, keepdims=True), sink)
                p = jnp.exp(s - m)
                denom = jnp.sum(p, axis=1, keepdims=True) + jnp.exp(sink - m)
                pn = (p / denom).astype(BF16)
                outs.append(_dot(pn[:, :W], v_prev) + _dot(pn[:, W:], v_own))
                lse_ref[sub, rs, :] = jnp.broadcast_to(m + jnp.log(denom), (W, LANES))
            o_ref[rs, :] = jnp.where(lane < 64, outs[0], outs[1]).astype(o_ref.dtype)

    qspec = pl.BlockSpec((SWA_TB, LANES), lambda p, i: (i, p))
    own = pl.BlockSpec((None, SWA_TB, LANES), lambda p, i: (p // 2, i, 0))
    prev = pl.BlockSpec((None, W, LANES), lambda p, i: (p // 2, jnp.maximum(SWA_SUB * i - 1, 0), 0))
    stat = pl.BlockSpec((2, SWA_TB, LANES), lambda p, i: (p, i, 0))
    return pl.pallas_call(
        body, name="swa_fwd", grid=(4, nb),
        in_specs=[pl.BlockSpec(memory_space=pltpu.SMEM), qspec, own, prev, own, prev,
                  pl.BlockSpec((2, W, 2 * W), lambda p, i: (p, 0, 0))],
        out_specs=[qspec, stat],
        out_shape=[jax.ShapeDtypeStruct((T, 512), BF16), jax.ShapeDtypeStruct((SWA_HEADS, T, LANES), F32)],
        compiler_params=_cparams("parallel", "parallel"),
    )(sinks, q, kad, kad, vad, vad, bias)


def _swa_bwd(sinks, q, kad, vad, bias, do, lse, delta, T):
    nb = T // SWA_TB
    scale = SWA_HEAD_DIM ** -0.5
    W = WINDOW

    def body(sink_ref, q_ref, k_ref, kp_ref, v_ref, vp_ref, bias_ref, do_ref, lse_ref, dl_ref,
             dq_ref, dkad_ref, dvad_ref, dbias_ref, dsk_ref):
        p_, i = pl.program_id(0), pl.program_id(1)
        kvh = p_ // 2
        lane = _lane((W, LANES))

        @pl.when((p_ == 0) & (i == 0))
        def _():
            dkad_ref[...] = jnp.zeros_like(dkad_ref)
            dvad_ref[...] = jnp.zeros_like(dvad_ref)

        @pl.when(i == 0)
        def _():
            dbias_ref[...] = jnp.zeros_like(dbias_ref)
            dsk_ref[...] = jnp.zeros_like(dsk_ref)

        for r in range(SWA_SUB):
            rs = slice(r * W, (r + 1) * W)
            ps = slice((r - 1) * W, r * W)
            qr, dor = q_ref[rs, :], do_ref[rs, :]
            k_own, v_own = k_ref[rs, :], v_ref[rs, :]
            k_prev = kp_ref[...] if r == 0 else k_ref[ps, :]
            v_prev = vp_ref[...] if r == 0 else v_ref[ps, :]
            valid = _swa_valid(r, i)
            own_row = pl.multiple_of(i * SWA_TB + r * W, W)
            dqs = []
            dk_own = jnp.zeros((W, LANES), F32)
            dk_prev = jnp.zeros((W, LANES), F32)
            dv_own = jnp.zeros((W, LANES), F32)
            dv_prev = jnp.zeros((W, LANES), F32)
            for sub in range(2):
                hm = (lane >= 64) if sub else (lane < 64)
                qh = jnp.where(hm, qr, jnp.zeros_like(qr))
                doh = jnp.where(hm, dor, jnp.zeros_like(dor))
                s = jnp.concatenate([_dot(qh, k_prev, NT), _dot(qh, k_own, NT)], axis=1) * scale + bias_ref[sub]
                s = jnp.where(valid, s, NEG)
                lse_b = lse_ref[sub, rs, :]
                dl_b = dl_ref[sub, rs, :]
                p = jnp.exp(s - jnp.tile(lse_b, (1, 2)))
                dp = jnp.concatenate([_dot(doh, v_prev, NT), _dot(doh, v_own, NT)], axis=1)
                ds = p * (dp - jnp.tile(dl_b, (1, 2)))
                dbias_ref[sub] += ds
                sink = sink_ref[2 * p_ + sub]
                dsk_ref[sub:sub + 1, :] += jnp.sum(jnp.exp(sink - lse_b) * dl_b, axis=0, keepdims=True)
                dsb = ds.astype(BF16)
                pb = p.astype(BF16)
                dqs.append((_dot(dsb[:, :W], k_prev) + _dot(dsb[:, W:], k_own)) * scale)
                dk_prev += _dot(dsb[:, :W], qh, TN) * scale
                dk_own += _dot(dsb[:, W:], qh, TN) * scale
                dv_prev += _dot(pb[:, :W], doh, TN)
                dv_own += _dot(pb[:, W:], doh, TN)
            dq_ref[rs, :] = jnp.where(lane < 64, dqs[0], dqs[1])
            dkad_ref[kvh, pl.ds(own_row, W), :] += dk_own
            dvad_ref[kvh, pl.ds(own_row, W), :] += dv_own
            if r == 0:
                @pl.when(i > 0)
                def _():
                    prev_row = pl.multiple_of(i * SWA_TB - W, W)
                    dkad_ref[kvh, pl.ds(prev_row, W), :] += dk_prev
                    dvad_ref[kvh, pl.ds(prev_row, W), :] += dv_prev
            else:
                prev_row = pl.multiple_of(i * SWA_TB + (r - 1) * W, W)
                dkad_ref[kvh, pl.ds(prev_row, W), :] += dk_prev
                dvad_ref[kvh, pl.ds(prev_row, W), :] += dv_prev

    qspec = pl.BlockSpec((SWA_TB, LANES), lambda p, i: (i, p))
    own = pl.BlockSpec((None, SWA_TB, LANES), lambda p, i: (p // 2, i, 0))
    prev = pl.BlockSpec((None, W, LANES), lambda p, i: (p // 2, jnp.maximum(SWA_SUB * i - 1, 0), 0))
    stat = pl.BlockSpec((2, SWA_TB, LANES), lambda p, i: (p, i, 0))
    full = pl.BlockSpec((2, T, LANES), lambda p, i: (0, 0, 0))
    return pl.pallas_call(
        body, name="swa_bwd", grid=(4, nb),
        in_specs=[pl.BlockSpec(memory_space=pltpu.SMEM), qspec, own, prev, own, prev,
                  pl.BlockSpec((2, W, 2 * W), lambda p, i: (p, 0, 0)), qspec, stat, stat],
        out_specs=[qspec, full, full, pl.BlockSpec((2, W, 2 * W), lambda p, i: (p, 0, 0)),
                   pl.BlockSpec((None, 8, LANES), lambda p, i: (p, 0, 0))],
        out_shape=[jax.ShapeDtypeStruct((T, 512), F32), jax.ShapeDtypeStruct((2, T, LANES), F32),
                   jax.ShapeDtypeStruct((2, T, LANES), F32), jax.ShapeDtypeStruct((SWA_HEADS, W, 2 * W), F32),
                   jax.ShapeDtypeStruct((4, 8, LANES), F32)],
        compiler_params=_cparams("arbitrary", "arbitrary"),
    )(sinks, q, kad, kad, vad, vad, bias, do, lse, delta)


def _mem_fwd(q, mk, mv, T, tq):
    scale = MEM_HEAD_DIM ** -0.5

    def body(q_ref, k_ref, v_ref, o_ref, lse_ref):
        s = _dot(q_ref[...], k_ref[...], NT) * scale
        m = jnp.max(s, axis=1, keepdims=True)
        p = jnp.exp(s - m)
        l = jnp.sum(p, axis=1, keepdims=True)
        o_ref[...] = _dot((p / l).astype(BF16), v_ref[...]).astype(o_ref.dtype)
        lse_ref[...] = jnp.broadcast_to(m + jnp.log(l), (tq, LANES))

    qspec = pl.BlockSpec((tq, LANES), lambda h, i: (i, h))
    kspec = pl.BlockSpec((N_MEM, LANES), lambda h, i: (0, h))
    return pl.pallas_call(
        body, name="mem_fwd", grid=(MEM_HEADS, T // tq),
        in_specs=[qspec, kspec, kspec],
        out_specs=[qspec, pl.BlockSpec((None, tq, LANES), lambda h, i: (h, i, 0))],
        out_shape=[jax.ShapeDtypeStruct((T, 512), BF16), jax.ShapeDtypeStruct((MEM_HEADS, T, LANES), F32)],
        compiler_params=_cparams("parallel", "parallel"),
    )(q, mk, mv)


def _mem_bwd(q, mk, mv, do, lse, delta, T, tq):
    scale = MEM_HEAD_DIM ** -0.5
    rep = N_MEM // LANES

    def body(q_ref, k_ref, v_ref, do_ref, lse_ref, dl_ref, dq_ref, dk_ref, dv_ref):
        i = pl.program_id(1)

        @pl.when(i == 0)
        def _():
            dk_ref[...] = jnp.zeros_like(dk_ref)
            dv_ref[...] = jnp.zeros_like(dv_ref)

        qv, dov = q_ref[...], do_ref[...]
        s = _dot(qv, k_ref[...], NT) * scale
        p = jnp.exp(s - jnp.tile(lse_ref[...], (1, rep)))
        dp = _dot(dov, v_ref[...], NT)
        ds = p * (dp - jnp.tile(dl_ref[...], (1, rep)))
        dsb = ds.astype(BF16)
        dq_ref[...] = _dot(dsb, k_ref[...]) * scale
        dk_ref[...] += _dot(dsb, qv, TN) * scale
        dv_ref[...] += _dot(p.astype(BF16), dov, TN)

    qspec = pl.BlockSpec((tq, LANES), lambda h, i: (i, h))
    kspec = pl.BlockSpec((N_MEM, LANES), lambda h, i: (0, h))
    stat = pl.BlockSpec((None, tq, LANES), lambda h, i: (h, i, 0))
    return pl.pallas_call(
        body, name="mem_bwd", grid=(MEM_HEADS, T // tq),
        in_specs=[qspec, kspec, kspec, qspec, stat, stat],
        out_specs=[qspec, kspec, kspec],
        out_shape=[jax.ShapeDtypeStruct((T, 512), F32), jax.ShapeDtypeStruct((N_MEM, 512), F32),
                   jax.ShapeDtypeStruct((N_MEM, 512), F32)],
        compiler_params=_cparams("arbitrary", "arbitrary"),
    )(q, mk, mv, do, lse, delta)


def _mem_prep_fwd(mem, g_mem, w_kv, kn_gain, gm128):
    def body(mem_ref, g_ref, w_ref, kn_ref, gm_ref, memn_o, kv_o, mk_o, mv_o):
        xhat, _ = _rms_rows(mem_ref[...], None)
        memn = (xhat * g_ref[...]).astype(BF16)
        memn_o[...] = memn
        kv = _dot(memn, w_ref[...])
        kv_o[...] = kv
        gm = gm_ref[...]
        for c in range(4):
            sl = slice(c * LANES, (c + 1) * LANES)
            y, _ = _head_norm(kv[:, sl], gm, kn_ref[...])
            mk_o[:, sl] = y.astype(BF16)
        mv_o[...] = kv[:, 512:].astype(BF16)

    vm = pl.BlockSpec(memory_space=pltpu.VMEM)
    return pl.pallas_call(
        body, name="mem_prep_fwd", in_specs=[vm] * 5, out_specs=[vm] * 4,
        out_shape=[jax.ShapeDtypeStruct((N_MEM, D_MODEL), BF16), jax.ShapeDtypeStruct((N_MEM, D_MODEL), F32),
                   jax.ShapeDtypeStruct((N_MEM, 512), BF16), jax.ShapeDtypeStruct((N_MEM, 512), BF16)],
        compiler_params=pltpu.CompilerParams(vmem_limit_bytes=VMEM_LIMIT),
    )(mem, g_mem, w_kv, kn_gain, gm128)


def _mem_prep_bwd(mem, g_mem, memn, kv, w_kv, kn_gain, gm128, dmk, dmv):
    def body(mem_ref, g_ref, memn_ref, kv_ref, w_ref, kn_ref, gm_ref, dmk_ref, dmv_ref, dw_o, dg_o, dkn_o, dkv_s):
        gm = gm_ref[...]
        dkn = jnp.zeros((1, LANES), F32)
        for c in range(4):
            sl = slice(c * LANES, (c + 1) * LANES)
            dx, dg = _head_norm_bwd(dmk_ref[:, sl], kv_ref[:, sl], gm, kn_ref[...])
            dkv_s[:, sl] = dx.astype(BF16)
            dkn = dkn + dg
        dkn_o[...] = dkn
        dkv_s[:, 512:] = dmv_ref[...].astype(BF16)
        dkv = dkv_s[...]
        dw_o[...] = _dot(memn_ref[...], dkv, TN)
        dmemn = _dot(dkv, w_ref[...], NT)
        xhat, _ = _rms_rows(mem_ref[...], None)
        dg_o[...] = jnp.sum(dmemn * xhat, axis=0, keepdims=True)

    vm = pl.BlockSpec(memory_space=pltpu.VMEM)
    return pl.pallas_call(
        body, name="mem_prep_bwd", in_specs=[vm] * 9, out_specs=[vm] * 3,
        out_shape=[jax.ShapeDtypeStruct((D_MODEL, D_MODEL), F32), jax.ShapeDtypeStruct((1, D_MODEL), F32),
                   jax.ShapeDtypeStruct((1, LANES), F32)],
        scratch_shapes=[pltpu.VMEM((N_MEM, D_MODEL), BF16)],
        compiler_params=pltpu.CompilerParams(vmem_limit_bytes=VMEM_LIMIT),
    )(mem, g_mem, memn, kv, w_kv, kn_gain, gm128, dmk, dmv)


SLOT_O = D_MODEL // N_SHARD


def _merge_fwd(proj, b_gate, o3, w3, T, tb):
    def body(gl_ref, bg_ref, oa_ref, of_ref, om_ref, wa_ref, wf_ref, wm_ref, out_ref):
        o_refs = (oa_ref, of_ref, om_ref)
        w_refs = (wa_ref, wf_ref, wm_ref)
        for n in range(N_SHARD):
            acc = jnp.zeros((tb, SLOT_O), F32)
            for b in range(3):
                c0 = b * D_MODEL + n * SLOT_O
                g = jax.nn.sigmoid(gl_ref[:, c0:c0 + SLOT_O] + bg_ref[:, c0:c0 + SLOT_O])
                acc = acc + g * _dot(o_refs[b][...], w_refs[b][n])
            out_ref[:, n * SLOT_O:(n + 1) * SLOT_O] = acc.astype(out_ref.dtype)

    rows = pl.BlockSpec((tb, 512), lambda i: (i, 0))
    wspec = pl.BlockSpec((N_SHARD, 512, SLOT_O), lambda i: (0, 0, 0))
    return pl.pallas_call(
        body, name="merge_fwd", grid=(T // tb,),
        in_specs=[pl.BlockSpec((tb, GATE_W), lambda i: (i, 1)), pl.BlockSpec((1, GATE_W), lambda i: (0, 0)),
                  rows, rows, rows, wspec, wspec, wspec],
        out_specs=pl.BlockSpec((tb, D_MODEL), lambda i: (i, 0)),
        out_shape=jax.ShapeDtypeStruct((T, D_MODEL), BF16),
        compiler_params=_cparams("parallel"),
    )(proj, b_gate, *o3, *w3)


def _merge_bwd(proj, b_gate, o3, w3, dmerged, T, tb):
    heads = (SWA_HEADS, FOX_HEADS, MEM_HEADS)

    def body(gl_ref, bg_ref, oa_ref, of_ref, om_ref, wa_ref, wf_ref, wm_ref, dm_ref,
             dgl_o, doa_o, dof_o, dom_o, dla_o, dlf_o, dlm_o, dwa_o, dwf_o, dwm_o, dbg_o):
        i = pl.program_id(0)
        o_refs = (oa_ref, of_ref, om_ref)
        w_refs = (wa_ref, wf_ref, wm_ref)
        do_refs = (doa_o, dof_o, dom_o)
        dl_refs = (dla_o, dlf_o, dlm_o)
        dw_refs = (dwa_o, dwf_o, dwm_o)

        @pl.when(i == 0)
        def _():
            for r in dw_refs:
                r[...] = jnp.zeros_like(r)
            dbg_o[...] = jnp.zeros_like(dbg_o)

        lane = _lane((tb, LANES))
        for b in range(3):
            ob = o_refs[b][...]
            do = jnp.zeros((tb, 512), F32)
            for n in range(N_SHARD):
                c0 = b * D_MODEL + n * SLOT_O
                g = jax.nn.sigmoid(gl_ref[:, c0:c0 + SLOT_O] + bg_ref[:, c0:c0 + SLOT_O])
                dm = dm_ref[:, n * SLOT_O:(n + 1) * SLOT_O]
                y = _dot(ob, w_refs[b][n])
                dgl = dm * y * g * (1.0 - g)
                dgl_o[:, c0:c0 + SLOT_O] = dgl.astype(dgl_o.dtype)
                dbg_o[:, c0:c0 + SLOT_O] += jnp.sum(dgl, axis=0, keepdims=True)
                dy = (dm * g).astype(BF16)
                do = do + _dot(dy, w_refs[b][n], NT)
                dw_refs[b][n] += _dot(ob, dy, TN)
            do_refs[b][...] = do.astype(BF16)
            prod = do * ob.astype(F32)
            for c in range(4):
                blk = prod[:, c * LANES:(c + 1) * LANES]
                if heads[b] == 8:
                    lo = jnp.sum(jnp.where(lane < 64, blk, 0.0), axis=1, keepdims=True)
                    hi = jnp.sum(jnp.where(lane >= 64, blk, 0.0), axis=1, keepdims=True)
                    if b == 1:
                        aug = jnp.zeros((tb, LANES), F32)
                        for sub, dl in enumerate((lo, hi)):
                            for e, piece in enumerate(_split3(-dl)):
                                aug = jnp.where(lane == AUG_STRIDE * sub + AUG_C + e, piece.astype(F32), aug)
                        dl_refs[b][:, c * LANES:(c + 1) * LANES] = aug.astype(BF16)
                    else:
                        dl_refs[b][2 * c] = jnp.broadcast_to(lo, (tb, LANES))
                        dl_refs[b][2 * c + 1] = jnp.broadcast_to(hi, (tb, LANES))
                else:
                    dl_refs[b][c] = jnp.broadcast_to(jnp.sum(blk, axis=1, keepdims=True), (tb, LANES))

    rows = pl.BlockSpec((tb, 512), lambda i: (i, 0))
    wspec = pl.BlockSpec((N_SHARD, 512, SLOT_O), lambda i: (0, 0, 0))
    stat = lambda h: pl.BlockSpec((h, tb, LANES), lambda i: (0, i, 0))
    return pl.pallas_call(
        body, name="merge_bwd", grid=(T // tb,),
        in_specs=[pl.BlockSpec((tb, GATE_W), lambda i: (i, 1)), pl.BlockSpec((1, GATE_W), lambda i: (0, 0)),
                  rows, rows, rows, wspec, wspec, wspec, pl.BlockSpec((tb, D_MODEL), lambda i: (i, 0))],
        out_specs=[pl.BlockSpec((tb, GATE_W), lambda i: (i, 0)), rows, rows, rows,
                   stat(8), rows, stat(4), wspec, wspec, wspec, pl.BlockSpec((1, GATE_W), lambda i: (0, 0))],
        out_shape=[jax.ShapeDtypeStruct((T, GATE_W), BF16)] + [jax.ShapeDtypeStruct((T, 512), BF16)] * 3
        + [jax.ShapeDtypeStruct((8, T, LANES), F32), jax.ShapeDtypeStruct((T, 512), BF16),
           jax.ShapeDtypeStruct((4, T, LANES), F32)]
        + [jax.ShapeDtypeStruct((N_SHARD, 512, SLOT_O), F32)] * 3 + [jax.ShapeDtypeStruct((1, GATE_W), F32)],
        compiler_params=_cparams("arbitrary"),
    )(proj, b_gate, *o3, *w3, dmerged)


def _local_step(x, mem, tgt, small, wc, w_kv, w_o3, w_out, w_up, w_down, reducer):
    T = x.shape[0]
    tm = min(512, T)
    tile2 = lambda v: jnp.tile(v.reshape(1, -1), (1, LANES // v.size))
    gains = jnp.concatenate([tile2(small["qn_swa"]), tile2(small["kn_swa"]), tile2(small["qn_fox"]),
                             tile2(small["kn_fox"]), tile2(small["qn_mem"]), jnp.zeros((3, LANES), F32)], axis=0)
    kn_mem = small["kn_mem"].reshape(1, LANES)
    bfor = jnp.pad(small["b_forget"].reshape(1, -1), ((0, 0), (0, LANES - FOX_HEADS)))
    gm64 = _group_mean_matrix(64)
    gm128 = _group_mean_matrix(128)
    tb_prep = min(256, T)
    ones = jnp.ones((tb_prep, tb_prep), F32)
    tril = jnp.tril(ones).astype(BF16)
    triu = jnp.triu(ones).astype(BF16)
    bucket = _t5_bucket_matrix()
    g_mix, g_mlp, g_mem = small["g_mix"], small["g_mlp"], small["g_mem"]
    b_gate = small["b_gate"]
    sinks = small["sink_swa"].reshape(-1)

    tl = min(1024, T)
    sq = pl.BlockSpec((tl, D_MODEL), lambda i, j, k: (i, j))
    h = _rmsnorm("rms_mix", x, g_mix, tm)
    (proj,) = _matmul(
        "mm_proj", h, wc, dims=NN, grid=(T // tl, PROJ_W // D_MODEL, 1),
        a_spec=pl.BlockSpec((tl, D_MODEL), lambda i, j, k: (i, 0)),
        b_spec=pl.BlockSpec((D_MODEL, D_MODEL), lambda i, j, k: (0, j)),
        acc_shape=(tl, D_MODEL),
        outs=[(jax.ShapeDtypeStruct((T, PROJ_W), F32), sq)],
        epilogue=_epi_store)
    qa, qf, kf, vf, qm, kad, vad, qf_aug, kf_aug = _prep_fwd(proj, gains, bfor, tril, gm64, gm128, T, tb_prep)
    bias = _swa_bias(small["rel_bias"], bucket)
    o_swa, lse_swa = _swa_fwd(sinks, qa, kad, vad, bias, T)
    o_fox, qf_aug_bwd = _fox_fwd(qf, qf_aug, kf, kf_aug, vf, T, tm)
    memn, kv, mk, mv = _mem_prep_fwd(mem, g_mem, w_kv, kn_mem, gm128)
    o_mem, lse_mem = _mem_fwd(qm, mk, mv, T, tm)
    o3 = (o_swa, o_fox, o_mem)
    merged = _merge_fwd(proj, b_gate, o3, w_o3, T, min(256, T))

    def epi_residual(acc, extra_refs, out_refs, ij):
        out_refs[0][...] = extra_refs[0][...] + acc

    row_full = pl.BlockSpec((tm, D_MODEL), lambda i, j, k: (i, 0))
    row_big = pl.BlockSpec((tl, D_MODEL), lambda i, j, k: (i, 0))
    whole = pl.BlockSpec((D_MODEL, D_MODEL), lambda i, j, k: (0, 0))
    (x2,) = _matmul(
        "mm_out", merged, w_out, dims=NN, grid=(T // tl, 1, 1),
        a_spec=row_big, b_spec=whole,
        acc_shape=(tl, D_MODEL), extra=[(x, row_big)],
        outs=[(jax.ShapeDtypeStruct((T, D_MODEL), F32), row_big)], epilogue=epi_residual)
    hm = _rmsnorm("rms_mlp", x2, g_mlp, tm)

    def epi_relu2(acc, extra_refs, out_refs, ij):
        out_refs[0][...] = acc
        r = jnp.maximum(acc, 0.0)
        out_refs[1][...] = (r * r).astype(BF16)

    up, u = _matmul(
        "mm_up", hm, w_up, dims=NN, grid=(T // tl, N_SHARD, 1),
        a_spec=row_big, b_spec=pl.BlockSpec((None, D_MODEL, D_MODEL), lambda i, j, k: (j, 0, 0)),
        acc_shape=(tl, D_MODEL),
        outs=[(jax.ShapeDtypeStruct((T, D_FF), F32), sq), (jax.ShapeDtypeStruct((T, D_FF), BF16), sq)],
        epilogue=epi_relu2)

    def epi_loss(acc, extra_refs, out_refs, ij):
        y = extra_refs[0][...] + acc
        err = y - extra_refs[1][...]
        out_refs[0][...] = err * (1.0 / D_MODEL)
        sq = jnp.sum(jnp.sum(err * err, axis=1, keepdims=True), axis=0, keepdims=True)

        @pl.when(ij[0] == 0)
        def _():
            out_refs[1][...] = jnp.zeros_like(out_refs[1])

        out_refs[1][...] += jnp.broadcast_to(sq, out_refs[1].shape)

    kblk = pl.BlockSpec((tl, D_MODEL), lambda i, j, k: (i, k))
    dy, loss_acc = _matmul(
        "mm_down", u, w_down, dims=NN, grid=(T // tl, 1, N_SHARD),
        a_spec=kblk, b_spec=pl.BlockSpec((D_MODEL, D_MODEL), lambda i, j, k: (k, 0)),
        acc_shape=(tl, D_MODEL), extra=[(x2, row_big), (tgt, row_big)],
        outs=[(jax.ShapeDtypeStruct((T, D_MODEL), F32), row_big),
              (jax.ShapeDtypeStruct((8, LANES), F32), pl.BlockSpec((8, LANES), lambda i, j, k: (0, 0)))],
        epilogue=epi_loss)
    loss = loss_acc[0, 0] * (0.5 / D_MODEL)

    def epi_dup(acc, extra_refs, out_refs, ij):
        out_refs[0][...] = (acc * (2.0 * jnp.maximum(extra_refs[0][...], 0.0))).astype(BF16)

    (dup,) = _matmul(
        "mm_dup", dy, w_down, dims=NT, grid=(T // tl, N_SHARD, 1),
        a_spec=row_big, b_spec=pl.BlockSpec((D_MODEL, D_MODEL), lambda i, j, k: (j, 0)),
        acc_shape=(tl, D_MODEL), extra=[(up, sq)],
        outs=[(jax.ShapeDtypeStruct((T, D_FF), BF16), sq)], epilogue=epi_dup)

    nkt = T // tl
    t_rows = pl.BlockSpec((tl, D_MODEL), lambda i, j, k: (k, i))
    t_cols = pl.BlockSpec((tl, D_MODEL), lambda i, j, k: (k, j))
    (d_w_down,) = _matmul(
        "mm_dw_down", u, dy, dims=TN, grid=(N_SHARD, 1, nkt),
        a_spec=t_rows, b_spec=t_cols, acc_shape=(D_MODEL, D_MODEL),
        outs=[(jax.ShapeDtypeStruct((D_FF, D_MODEL), F32), pl.BlockSpec((D_MODEL, D_MODEL), lambda i, j, k: (i, 0)))],
        epilogue=_epi_store)
    (d_w_up,) = _matmul(
        "mm_dw_up", hm, dup, dims=TN, grid=(1, N_SHARD, nkt),
        a_spec=t_rows, b_spec=t_cols, acc_shape=(D_MODEL, D_MODEL),
        outs=[(jax.ShapeDtypeStruct((N_SHARD, D_MODEL, D_MODEL), F32),
               pl.BlockSpec((None, D_MODEL, D_MODEL), lambda i, j, k: (j, 0, 0)))],
        epilogue=_epi_store)

    def epi_rms_bwd(acc, extra_refs, out_refs, ij):
        dx, dg = _rmsnorm_bwd_rows(acc, extra_refs[0][...], extra_refs[1][...])
        out_refs[0][...] = dx + extra_refs[2][...]

        @pl.when(ij[0] == 0)
        def _():
            out_refs[1][...] = jnp.zeros_like(out_refs[1])

        out_refs[1][...] += dg

    gain_spec = pl.BlockSpec((1, D_MODEL), lambda i, j, k: (0, 0))
    dx2, d_g_mlp = _matmul(
        "mm_dhm", dup, w_up, dims=NT, grid=(T // tl, 1, N_SHARD),
        a_spec=kblk, b_spec=pl.BlockSpec((None, D_MODEL, D_MODEL), lambda i, j, k: (k, 0, 0)),
        acc_shape=(tl, D_MODEL), extra=[(x2, row_big), (g_mlp, gain_spec), (dy, row_big)],
        outs=[(jax.ShapeDtypeStruct((T, D_MODEL), F32), row_big), (jax.ShapeDtypeStruct((1, D_MODEL), F32), gain_spec)],
        epilogue=epi_rms_bwd)

    (dmerged,) = _matmul(
        "mm_dmerged", dx2, w_out, dims=NT, grid=(T // tl, 1, 1),
        a_spec=row_big, b_spec=whole,
        acc_shape=(tl, D_MODEL), outs=[(jax.ShapeDtypeStruct((T, D_MODEL), F32), row_big)], epilogue=_epi_store)
    (d_w_out,) = _matmul(
        "mm_dw_out", merged, dx2, dims=TN, grid=(1, 1, nkt),
        a_spec=t_rows, b_spec=t_cols, acc_shape=(D_MODEL, D_MODEL),
        outs=[(jax.ShapeDtypeStruct((D_MODEL, D_MODEL), F32), whole)],
        epilogue=_epi_store)
    dmerged = reducer.early_start({"w_mlp_down": d_w_down, "w_mlp_up": d_w_up, "w_out": d_w_out}, dmerged)
    (dgl, do_swa, do_fox, do_mem, dl_swa, do_fox_aug, dl_mem, d_wo_swa, d_wo_fox, d_wo_mem, d_b_gate) = _merge_bwd(
        proj, b_gate, o3, w_o3, dmerged, T, min(256, T))
    do_fox = reducer.early_send(do_fox)

    dqa, dkad, dvad, dbias, dsk = _swa_bwd(sinks, qa, kad, vad, bias, do_swa, lse_swa, dl_swa, T)
    dqf, dqf_aug, dkf, dkf_aug, dvf = _fox_bwd(qf, qf_aug_bwd, kf, kf_aug, vf, do_fox, do_fox_aug, T, tm)
    dvf = reducer.early_finish(dvf)
    dqm, dmk, dmv = _mem_bwd(qm, mk, mv, do_mem, lse_mem, dl_mem, T, tm)
    d_w_kv, d_g_mem, d_kn_mem = _mem_prep_bwd(mem, g_mem, memn, kv, w_kv, kn_mem, gm128, dmk, dmv)
    d_rel = _swa_bias_bwd(dbias, bucket)
    aug_lane = lambda a, lane: a.reshape(T, FOX_HEADS // 2, LANES)[:, :, lane:lane + AUG_STRIDE + 1:AUG_STRIDE]
    dc_queries = aug_lane(dqf_aug, AUG_C).reshape(T, FOX_HEADS)
    dc_keys = aug_lane(dkf_aug, AUG_NEG_C).reshape(T, FOX_HEADS)
    dccol = jnp.pad(dc_queries - dc_keys, ((0, 0), (0, LANES - FOX_HEADS)))
    dlo, gacc = _prep_bwd(proj, dqa, dkad, dvad, dqf, dkf, dvf, dqm, dccol, gains, bfor, triu, gm64, gm128, T, tb_prep)

    def dwc_half(name, dpart):
        (res,) = _matmul(
            name, h, dpart, dims=TN, grid=(1, LO_W // D_MODEL, nkt),
            a_spec=t_rows, b_spec=t_cols, acc_shape=(D_MODEL, D_MODEL),
            outs=[(jax.ShapeDtypeStruct((D_MODEL, LO_W), F32), pl.BlockSpec((D_MODEL, D_MODEL), lambda i, j, k: (0, j)))],
            epilogue=_epi_store)
        return res

    d_wc_lo = dwc_half("mm_dwc_lo", dlo)
    d_wc_gl = dwc_half("mm_dwc_gl", dgl)
    dlo = reducer.late_start({"wc_lo": d_wc_lo, "wc_gl": d_wc_gl, "w_mem_kv": d_w_kv, "w_o_swa": d_wo_swa,
                              "w_o_fox": d_wo_fox, "w_o_mem": d_wo_mem}, dlo)
    (dh_lo,) = _matmul(
        "mm_dh_lo", dlo, wc, dims=NT, grid=(T // tl, 1, LO_W // D_MODEL),
        a_spec=kblk, b_spec=pl.BlockSpec((D_MODEL, D_MODEL), lambda i, j, k: (0, k)),
        acc_shape=(tl, D_MODEL), outs=[(jax.ShapeDtypeStruct((T, D_MODEL), F32), row_big)], epilogue=_epi_store)
    dh_lo = reducer.late_send(dh_lo)

    def epi_dx(acc, extra_refs, out_refs, ij):
        dhh = acc + extra_refs[3][...]
        dx, dg = _rmsnorm_bwd_rows(dhh, extra_refs[0][...], extra_refs[1][...])
        out_refs[0][...] = dx + extra_refs[2][...]

        @pl.when(ij[0] == 0)
        def _():
            out_refs[1][...] = jnp.zeros_like(out_refs[1])

        out_refs[1][...] += dg

    grad_x, d_g_mix = _matmul(
        "mm_dh_gl", dgl, wc, dims=NT, grid=(T // tm, 1, GATE_W // D_MODEL),
        a_spec=pl.BlockSpec((tm, D_MODEL), lambda i, j, k: (i, k)),
        b_spec=pl.BlockSpec((D_MODEL, D_MODEL), lambda i, j, k: (0, k + LO_W // D_MODEL)),
        acc_shape=(tm, D_MODEL), extra=[(x, row_full), (g_mix, gain_spec), (dx2, row_full), (dh_lo, row_full)],
        outs=[(jax.ShapeDtypeStruct((T, D_MODEL), F32), row_full), (jax.ShapeDtypeStruct((1, D_MODEL), F32), gain_spec)],
        epilogue=epi_dx)
    grad_x = reducer.late_finish(grad_x)

    fold64 = lambda row: (row[:64] + row[64:]).reshape(1, 64)
    grads = {
        "g_mix": d_g_mix, "b_gate": d_b_gate, "b_forget": gacc[5, :FOX_HEADS].reshape(1, FOX_HEADS),
        "qn_swa": fold64(gacc[0]), "kn_swa": fold64(gacc[1]),
        "sink_swa": -dsk[:, :2, 0].reshape(1, SWA_HEADS), "rel_bias": d_rel[:, :SWA_HEADS],
        "qn_fox": fold64(gacc[2]), "kn_fox": fold64(gacc[3]),
        "g_mem": d_g_mem, "qn_mem": gacc[4].reshape(1, LANES), "kn_mem": d_kn_mem, "g_mlp": d_g_mlp,
    }
    return loss, grad_x, grads


MESH = pl.DeviceIdType.MESH
ANY = pl.BlockSpec(memory_space=pl.ANY)


def _place():
    x, y, c = lax.axis_index("x"), lax.axis_index("y"), lax.axis_index("c")
    chips = [(1 - x, y), (x, 1 - y), (1 - x, 1 - y)]
    return x, y, c, chips


def _all_gather_shards(slots):
    n = len(slots)

    def body(*refs):
        out = refs[n:2 * n]
        ici_send, ici_recv, d2d_send, d2d_recv = refs[2 * n:]
        x, y, c, chips = _place()
        sibling = (x, y, 1 - c)
        me = 2 * x + y

        def half(a, who):
            hr = slots[a].shape[1] // 2
            return pl.ds(pl.multiple_of(who * hr, hr), hr)

        def ici(a, j, slot, to):
            return pltpu.make_async_remote_copy(
                src_ref=out[a].at[me, half(a, c)], dst_ref=out[a].at[slot, half(a, c)],
                send_sem=ici_send.at[3 * a + j], recv_sem=ici_recv.at[3 * a + j], device_id=to, device_id_type=MESH)

        def d2d(a, j, slot, which):
            part = out[a].at[slot, half(a, which)]
            return pltpu.make_async_remote_copy(
                src_ref=part, dst_ref=part, send_sem=d2d_send.at[3 * a + j], recv_sem=d2d_recv.at[3 * a + j],
                device_id=sibling, device_id_type=MESH)

        sends = [ici(a, j, me, (*chip, c)) for a in range(n) for j, chip in enumerate(chips)]
        for cp in sends:
            cp.start()
        passed = []
        for a in range(n):
            for j, (px, py) in enumerate(chips):
                ici(a, j, 2 * px + py, (px, py, c)).wait_recv()
                cp = d2d(a, j, 2 * px + py, c)
                cp.start()
                passed.append(cp)
        for a in range(n):
            for j, (px, py) in enumerate(chips):
                d2d(a, j, 2 * px + py, 1 - c).wait_recv()
        for cp in sends + passed:
            cp.wait_send()

    return pl.pallas_call(
        body, name="all_gather_weights",
        in_specs=[ANY] * n, out_specs=[ANY] * n,
        out_shape=[jax.ShapeDtypeStruct(s.shape, s.dtype) for s in slots],
        input_output_aliases={a: a for a in range(n)},
        scratch_shapes=[pltpu.SemaphoreType.DMA((3 * n,))] * 4,
    )(*slots)


def _handshake(peers):
    barrier = pltpu.get_barrier_semaphore()
    for peer in peers:
        pl.semaphore_signal(barrier, inc=1, device_id=peer, device_id_type=MESH)
    pl.semaphore_wait(barrier, len(peers))


def _all_gather_shards_async(slots):
    n = len(slots)
    bufs = [jax.new_ref(s, memory_space=pltpu.MemorySpace.HBM) for s in slots]

    def body(ici_send, ici_recv, d2d_send, d2d_recv):
        x, y, c, chips = _place()
        sibling = (x, y, 1 - c)
        me = 2 * x + y
        _handshake([(px, py, c) for px, py in chips] + [sibling])

        def half(a, who):
            hr = slots[a].shape[1] // 2
            return pl.ds(pl.multiple_of(who * hr, hr), hr)

        def ici(a, j, slot, to):
            return pltpu.make_async_remote_copy(
                src_ref=bufs[a].at[me, half(a, c)], dst_ref=bufs[a].at[slot, half(a, c)],
                send_sem=ici_send.at[3 * a + j], recv_sem=ici_recv.at[3 * a + j], device_id=to, device_id_type=MESH)

        def d2d(a, j, slot, which):
            part = bufs[a].at[slot, half(a, which)]
            return pltpu.make_async_remote_copy(
                src_ref=part, dst_ref=part, send_sem=d2d_send.at[3 * a + j], recv_sem=d2d_recv.at[3 * a + j],
                device_id=sibling, device_id_type=MESH)

        sends = [ici(a, j, me, (*chip, c)) for a in range(n) for j, chip in enumerate(chips)]
        for cp in sends:
            cp.start()
        passed = []
        for a in range(n):
            for j, (px, py) in enumerate(chips):
                ici(a, j, 2 * px + py, (px, py, c)).wait_recv()
                cp = d2d(a, j, 2 * px + py, c)
                cp.start()
                passed.append(cp)
        for a in range(n):
            for j, (px, py) in enumerate(chips):
                d2d(a, j, 2 * px + py, 1 - c).wait_recv()
        for cp in sends + passed:
            cp.wait_send()

    pl.kernel(
        body, mesh=plsc.ScalarSubcoreMesh(axis_name="seq", num_cores=1), name="all_gather_weights_async",
        scratch_types=[pltpu.SemaphoreType.DMA((3 * n,))] * 4,
        compiler_params=pltpu.CompilerParams(collective_id=1),
    )()
    return [b[...] for b in bufs]


def _sequencer_call(name, collective_id, n_sems, body):
    pl.kernel(
        body, mesh=plsc.ScalarSubcoreMesh(axis_name="seq", num_cores=1), name=name,
        scratch_types=[pltpu.SemaphoreType.DMA((n_sems,))] * 2,
        compiler_params=pltpu.CompilerParams(collective_id=collective_id),
    )()


def _hbm_ref(value):
    return jax.new_ref(value, memory_space=pltpu.MemorySpace.HBM)


def _pair_exchange(name, collective_id, gs):
    n = len(gs)
    src = [_hbm_ref(g) for g in gs]
    stage = [jax.empty_ref(jax.ShapeDtypeStruct((N_SHARD, g.shape[1] // 2, g.shape[2]), g.dtype),
                           memory_space=pltpu.MemorySpace.HBM) for g in gs]

    def body(send_sem, recv_sem):
        x, y, c, _ = _place()
        sibling = (x, y, 1 - c)
        _handshake([sibling])
        copies = []
        for a in range(n):
            hr = gs[a].shape[1] // 2
            theirs = pl.ds(pl.multiple_of((1 - c) * hr, hr), hr)
            copies.append(pltpu.make_async_remote_copy(
                src_ref=src[a].at[:, theirs, :], dst_ref=stage[a], send_sem=send_sem.at[a], recv_sem=recv_sem.at[a],
                device_id=sibling, device_id_type=MESH))
        for cp in copies:
            cp.start()
        for cp in copies:
            cp.wait()

    _sequencer_call(name, collective_id, n, body)
    return [s[...] for s in stage]


def _chip_exchange(name, collective_id, sums):
    n = len(sums)
    src = [_hbm_ref(s) for s in sums]
    got = [jax.empty_ref(jax.ShapeDtypeStruct((3,) + s.shape[1:], s.dtype), memory_space=pltpu.MemorySpace.HBM)
           for s in sums]

    def body(send_sem, recv_sem):
        x, y, c, chips = _place()
        _handshake([(px, py, c) for px, py in chips])
        copies = []
        for a in range(n):
            for j, (px, py) in enumerate(chips):
                copies.append(pltpu.make_async_remote_copy(
                    src_ref=src[a].at[2 * px + py], dst_ref=got[a].at[j],
                    send_sem=send_sem.at[3 * a + j], recv_sem=recv_sem.at[3 * a + j],
                    device_id=(px, py, c), device_id_type=MESH))
        for cp in copies:
            cp.start()
        for cp in copies:
            cp.wait()

    _sequencer_call(name, collective_id, 3 * n, body)
    return [g[...] for g in got]


def _pair_gather(name, collective_id, fulls):
    n = len(fulls)
    full = [_hbm_ref(f) for f in fulls]

    def body(send_sem, recv_sem):
        x, y, c, _ = _place()
        sibling = (x, y, 1 - c)
        _handshake([sibling])
        copies = []
        for a in range(n):
            hr = fulls[a].shape[0] // 2
            mine = full[a].at[pl.ds(pl.multiple_of(c * hr, hr), hr)]
            copies.append(pltpu.make_async_remote_copy(
                src_ref=mine, dst_ref=mine, send_sem=send_sem.at[a], recv_sem=recv_sem.at[a],
                device_id=sibling, device_id_type=MESH))
        for cp in copies:
            cp.start()
        for cp in copies:
            cp.wait()

    _sequencer_call(name, collective_id, n, body)
    return [f[...] for f in full]


ELEMENTWISE_BLOCK_ELEMS = 256 * 1024


def _row_block(rows, cols):
    rb = 8
    while rb * 2 * cols <= ELEMENTWISE_BLOCK_ELEMS and rb * 2 <= rows:
        rb *= 2
    return rb


def _pair_sum(name, place, g, stage):
    _, R, C = g.shape
    hr = R // 2
    rb = _row_block(hr, C)
    nb = hr // rb

    def body(place_ref, g_ref, st_ref, sum_bf, own_f32):
        s = pl.program_id(1)
        tot = g_ref[...] + st_ref[...]
        sum_bf[...] = tot.astype(BF16)

        @pl.when(s == place_ref[0])
        def _():
            own_f32[...] = tot

    return pl.pallas_call(
        body, name=name,
        grid_spec=pltpu.PrefetchScalarGridSpec(
            num_scalar_prefetch=1, grid=(nb, N_SHARD),
            in_specs=[pl.BlockSpec((None, rb, C), lambda i, s, pr: (s, pr[1] * nb + i, 0)),
                      pl.BlockSpec((None, rb, C), lambda i, s, pr: (s, i, 0))],
            out_specs=[pl.BlockSpec((None, rb, C), lambda i, s, pr: (s, i, 0)),
                       pl.BlockSpec((rb, C), lambda i, s, pr: (i, 0))]),
        out_shape=[jax.ShapeDtypeStruct((N_SHARD, hr, C), BF16), jax.ShapeDtypeStruct((hr, C), F32)],
        compiler_params=_cparams("arbitrary", "arbitrary"),
    )(place, g, stage)


def _final_sum(name, place, own, got):
    hr, C = own.shape
    rb = _row_block(hr, C)
    nb = hr // rb

    def body(place_ref, own_ref, got_ref, o_ref):
        o_ref[...] = ((own_ref[...] + got_ref[0].astype(F32)) + got_ref[1].astype(F32)) + got_ref[2].astype(F32)

    return pl.pallas_call(
        body, name=name,
        grid_spec=pltpu.PrefetchScalarGridSpec(
            num_scalar_prefetch=1, grid=(nb,),
            in_specs=[pl.BlockSpec((rb, C), lambda i, pr: (i, 0)), pl.BlockSpec((3, rb, C), lambda i, pr: (0, i, 0))],
            out_specs=pl.BlockSpec((rb, C), lambda i, pr: (pr[1] * nb + i, 0))),
        out_shape=jax.ShapeDtypeStruct((2 * hr, C), F32),
        compiler_params=_cparams("arbitrary"),
    )(place, own, got)


def _adamw_math(w, g, m, v):
    m = ADAM_B1 * m + (1.0 - ADAM_B1) * g
    v = ADAM_B2 * v + (1.0 - ADAM_B2) * (g * g)
    m_hat = m / (1.0 - ADAM_B1 ** ADAM_STEP)
    v_hat = v / (1.0 - ADAM_B2 ** ADAM_STEP)
    delta = -ADAM_LR * (m_hat / (jnp.sqrt(v_hat) + ADAM_EPS) + ADAM_WD * w)
    return delta, m, v


def _adamw(name, w, g, m, v):
    R, Cw = w.shape
    Cg = g.shape[1]
    rb = _row_block(R, Cg)

    def body(w_ref, g_ref, m_ref, v_ref, g_o, d_o, m_o, v_o):
        gv = g_ref[...]
        delta, mn, vn = _adamw_math(w_ref[...], gv, m_ref[...], v_ref[...])
        g_o[...] = gv
        d_o[...] = delta
        m_o[...] = mn
        v_o[...] = vn

    blk = pl.BlockSpec((rb, Cg), lambda i: (i, 0))
    return pl.pallas_call(
        body, name=name, grid=(R // rb,),
        in_specs=[blk] * 4, out_specs=[blk] * 4,
        out_shape=[jax.ShapeDtypeStruct((R, Cw), F32)] * 4,
        compiler_params=_cparams("parallel"),
    )(w, g, m, v)


N_DEV = 8
SMALL_ROWS = 64


def _small_allreduce_adamw(g, w, m, v):
    def body(g_ref, w_ref, m_ref, v_ref, all_ref, gs_o, d_o, m_o, v_o, send_sems, recv_sems, local_sem):
        x, y, c, chips = _place()
        me, sibling = (x, y, c), (x, y, 1 - c)

        def rows(px, py, pc):
            return all_ref.at[pl.ds(pl.multiple_of((4 * px + 2 * py + pc) * SMALL_ROWS, SMALL_ROWS), SMALL_ROWS), :]

        def copy(k, block, to, src=None):
            return pltpu.make_async_remote_copy(
                src_ref=rows(*block) if src is None else src, dst_ref=rows(*block),
                send_sem=send_sems.at[k], recv_sem=recv_sems.at[k], device_id=to, device_id_type=MESH)

        mine = pltpu.make_async_copy(g_ref, rows(*me), local_sem)
        mine.start()
        first = [copy(0, me, sibling, src=g_ref)]
        first += [copy(1 + j, me, (*chip, c), src=g_ref) for j, chip in enumerate(chips)]
        for cp in first:
            cp.start()
        passed = [copy(4 + j, (*chip, c), sibling) for j, chip in enumerate(chips)]
        for j, chip in enumerate(chips):
            copy(1 + j, (*chip, c), me).wait_recv()
            passed[j].start()
        copy(0, sibling, me).wait_recv()
        for j, chip in enumerate(chips):
            copy(4 + j, (*chip, 1 - c), me).wait_recv()
        for cp in first + passed:
            cp.wait_send()
        mine.wait()

        tot = all_ref[0:SMALL_ROWS, :]
        for d in range(1, N_DEV):
            tot = tot + all_ref[d * SMALL_ROWS:(d + 1) * SMALL_ROWS, :]
        delta, mn, vn = _adamw_math(w_ref[...], tot, m_ref[...], v_ref[...])
        gs_o[...] = tot
        d_o[...] = delta
        m_o[...] = mn
        v_o[...] = vn

    vm = pl.BlockSpec(memory_space=pltpu.VMEM)
    shp = jax.ShapeDtypeStruct((SMALL_ROWS, LANES), F32)
    res = pl.pallas_call(
        body, name="small_allreduce_adamw", in_specs=[vm] * 4, out_specs=[vm] * 5,
        out_shape=[jax.ShapeDtypeStruct((N_DEV * SMALL_ROWS, LANES), F32), shp, shp, shp, shp],
        scratch_shapes=[pltpu.SemaphoreType.DMA((7,)), pltpu.SemaphoreType.DMA((7,)), pltpu.SemaphoreType.DMA],
    )(g, w, m, v)
    return res[1:]


SMALL_NAMES = ("g_mix", "b_gate", "b_forget", "qn_swa", "kn_swa", "sink_swa", "rel_bias", "qn_fox", "kn_fox",
               "g_mem", "qn_mem", "kn_mem", "g_mlp")
BIG_NAMES = ("w_in", "w_mem_kv", "w_o_swa", "w_o_fox", "w_o_mem", "w_out", "w_mlp_up", "w_mlp_down")
WEIGHT_NAMES = ("g_mix", "w_in", "b_gate", "b_forget", "qn_swa", "kn_swa", "sink_swa", "rel_bias", "qn_fox", "kn_fox",
                "g_mem", "w_mem_kv", "qn_mem", "kn_mem", "w_o_swa", "w_o_fox", "w_o_mem", "w_out", "g_mlp",
                "w_mlp_up", "w_mlp_down")


def _pack_small(parts, extra=None):
    rows = []
    for n in SMALL_NAMES:
        flat = parts[n].reshape(-1).astype(F32)
        flat = jnp.pad(flat, (0, (-flat.size) % LANES))
        rows.append(flat.reshape(-1, LANES))
    if extra is not None:
        rows.append(jnp.pad(extra.reshape(1, 1), ((0, 0), (0, LANES - 1))))
    packed = jnp.concatenate(rows, axis=0)
    return jnp.pad(packed, ((0, SMALL_ROWS - packed.shape[0]), (0, 0)))


def _unpack_small(packed, shapes):
    out, r = {}, 0
    for n in SMALL_NAMES:
        size = math.prod(shapes[n])
        nr = -(-size // LANES)
        out[n] = packed[r:r + nr].reshape(-1)[:size].reshape(shapes[n])
        r += nr
    return out, packed[r, 0]


def _reorder_w_in(w_full):
    seg = lambda a, b: w_full[:, a:b]
    pad = jnp.zeros((w_full.shape[0], C_GL - C_FL - FOX_HEADS), w_full.dtype)
    return jnp.concatenate([seg(0, 512), seg(768, 1280), seg(1280, 1792), seg(1792, 2304), seg(2312, 2824),
                            seg(512, 640), seg(640, 768), seg(2304, 2312), pad, seg(2824, IN_WIDTH)], axis=1)


def _restore_w_in(lo, gl):
    s = lambda a, b: lo[:, a:b]
    return jnp.concatenate([s(C_QA, C_QA + 512), s(C_KA, C_KA + 128), s(C_VA, C_VA + 128), s(C_QF, C_QF + 512),
                            s(C_KF, C_KF + 512), s(C_VF, C_VF + 512), s(C_FL, C_FL + FOX_HEADS), s(C_QM, C_QM + 512),
                            gl], axis=1)


def _after(first, then):
    return lax.optimization_barrier((first, then))


class _ReduceGroup:
    def __init__(self, tag, first_collective_id, place):
        self.tag, self.first_id, self.place = tag, first_collective_id, place

    def start(self, local, tie):
        self.names = tuple(local)
        mine, tie = _after([local[n] for n in self.names], tie)
        self.mine = mine
        self.staged = _pair_exchange("pair_exchange_" + self.tag, self.first_id, mine)
        return tie

    def send(self, tie):
        staged, tie = _after(self.staged, tie)
        sums = [_pair_sum("pair_sum_" + n, self.place, g, st) for n, g, st in zip(self.names, self.mine, staged)]
        travel, tie = _after([s[0] for s in sums], tie)
        self.own = [s[1] for s in sums]
        self.got = _chip_exchange("chip_exchange_" + self.tag, self.first_id + 1, travel)
        return tie

    def finish(self, tie):
        got, tie = _after(self.got, tie)
        halves = [_final_sum("final_sum_" + n, self.place, o, r) for n, o, r in zip(self.names, self.own, got)]
        halves, tie = _after(halves, tie)
        summed = _pair_gather("pair_gather_" + self.tag, self.first_id + 2, halves)
        self.summed = dict(zip(self.names, summed))
        return tie


class _GradReducer:
    def __init__(self, place):
        self.early = _ReduceGroup("early", 2, place)
        self.late = _ReduceGroup("late", 5, place)

    @staticmethod
    def _slot_rows(a):
        return a.reshape(N_SHARD, a.shape[0] // N_SHARD, a.shape[1])

    def early_start(self, g, tie):
        return self.early.start({"w_mlp_down": self._slot_rows(g["w_mlp_down"]), "w_mlp_up": g["w_mlp_up"],
                                 "w_out": self._slot_rows(g["w_out"])}, tie)

    def early_send(self, tie):
        return self.early.send(tie)

    def early_finish(self, tie):
        return self.early.finish(tie)

    def late_start(self, g, tie):
        pad_in = ((0, 0), (0, IN_SHARD_PAD - IN_SHARD))
        d_full = _restore_w_in(g["wc_lo"], g["wc_gl"])
        d_in = jnp.stack([jnp.pad(d_full[:, s * IN_SHARD:(s + 1) * IN_SHARD], pad_in) for s in range(N_SHARD)])
        return self.late.start({"w_in": d_in, "w_mem_kv": self._slot_rows(g["w_mem_kv"]), "w_o_swa": g["w_o_swa"],
                                "w_o_fox": g["w_o_fox"], "w_o_mem": g["w_o_mem"]}, tie)

    def late_send(self, tie):
        return self.late.send(tie)

    def late_finish(self, tie):
        return self.late.finish(tie)

    @property
    def summed(self):
        return {**self.early.summed, **self.late.summed}


def kernel(x, mem, g_mix, w_in, b_gate, b_forget, qn_swa, kn_swa, sink_swa, rel_bias, qn_fox, kn_fox, g_mem, w_mem_kv, qn_mem, kn_mem, w_o_swa, w_o_fox, w_o_mem, w_out, g_mlp, w_mlp_up, w_mlp_down, loss_target, m_g_mix, m_w_in, m_b_gate, m_b_forget, m_qn_swa, m_kn_swa, m_sink_swa, m_rel_bias, m_qn_fox, m_kn_fox, m_g_mem, m_w_mem_kv, m_qn_mem, m_kn_mem, m_w_o_swa, m_w_o_fox, m_w_o_mem, m_w_out, m_g_mlp, m_w_mlp_up, m_w_mlp_down, v_g_mix, v_w_in, v_b_gate, v_b_forget, v_qn_swa, v_kn_swa, v_sink_swa, v_rel_bias, v_qn_fox, v_kn_fox, v_g_mem, v_w_mem_kv, v_qn_mem, v_kn_mem, v_w_o_swa, v_w_o_fox, v_w_o_mem, v_w_out, v_g_mlp, v_w_mlp_up, v_w_mlp_down):
    given = dict(locals())
    W = {n: given[n] for n in WEIGHT_NAMES}
    M = {n: given["m_" + n] for n in WEIGHT_NAMES}
    V = {n: given["v_" + n] for n in WEIGHT_NAMES}
    pad_in = ((0, 0), (0, IN_SHARD_PAD - IN_SHARD))

    shards = [jnp.pad(w_in[0].astype(BF16), pad_in)] + [W[n][0].astype(BF16) for n in BIG_NAMES[1:]]
    slots = [jnp.broadcast_to(s[None], (N_SHARD,) + s.shape) for s in shards]
    (g_in,) = _all_gather_shards(slots[:1])
    g_in, late = lax.optimization_barrier((g_in, slots[1:]))
    g_kv, g_oa, g_of, g_om, g_out, g_up, g_down = _all_gather_shards_async(late)
    w_full = jnp.concatenate([g_in[s, :, :IN_SHARD] for s in range(N_SHARD)], axis=1)
    wc = _reorder_w_in(w_full)
    small = {n: (W[n] if n == "rel_bias" else W[n].reshape(1, -1)) for n in SMALL_NAMES}

    place = jnp.stack([2 * lax.axis_index("x") + lax.axis_index("y"), lax.axis_index("c")]).astype(jnp.int32)
    reducer = _GradReducer(place)
    loss, grad_x, grads = _local_step(
        x[0], mem[0], loss_target[0], small, wc, g_kv.reshape(D_MODEL, D_MODEL), (g_oa, g_of, g_om),
        g_out.reshape(D_MODEL, D_MODEL), g_up, g_down.reshape(D_FF, D_MODEL), reducer)

    out = {}
    for n in BIG_NAMES:
        res = _adamw("adamw_" + n, W[n][0], reducer.summed[n], M[n][0], V[n][0])
        out[n] = [r.reshape(W[n].shape) for r in res]
    shapes = {n: W[n].shape for n in SMALL_NAMES}
    packed = _small_allreduce_adamw(_pack_small(grads, loss), _pack_small(W), _pack_small(M), _pack_small(V))
    unpacked = [_unpack_small(p, shapes) for p in packed]
    for n in SMALL_NAMES:
        out[n] = [u[0][n] for u in unpacked]
    loss_total = unpacked[0][1]

    return (loss_total, grad_x.reshape(x.shape),
            *[out[n][0] for n in WEIGHT_NAMES], *[out[n][1] for n in WEIGHT_NAMES],
            *[out[n][2] for n in WEIGHT_NAMES], *[out[n][3] for n in WEIGHT_NAMES])
```

```python
import functools
import math

import jax
import jax.numpy as jnp
from jax import lax
from jax.experimental import pallas as pl
from jax.experimental.pallas import tpu as pltpu
from jax.experimental.pallas import tpu_sc as plsc

F32 = jnp.float32
BF16 = jnp.bfloat16

D_MODEL = 1024
N_MEM = 256
SWA_HEADS = 8
SWA_KV_HEADS = 2
SWA_HEAD_DIM = 64
WINDOW = 128
FOX_HEADS = 8
FOX_HEAD_DIM = 64
MEM_HEADS = 4
MEM_HEAD_DIM = 128
D_FF = 4 * D_MODEL
REL_BUCKETS = 32
REL_MAX_DIST = 128
EPS = 1e-6
NEG = -1e30
GATE_W = 3 * D_MODEL
IN_WIDTH = 5896
N_SHARD = 4
IN_SHARD = IN_WIDTH // N_SHARD
IN_SHARD_PAD = 1536

ADAM_LR = 0.001
ADAM_B1 = 0.9
ADAM_B2 = 0.999
ADAM_EPS = 1e-08
ADAM_WD = 0.01
ADAM_STEP = 10

LANES = 128
V7X_VMEM_BYTES = 64 * 1024 * 1024
VMEM_LIMIT = V7X_VMEM_BYTES * 3 // 4

C_QA, C_QF, C_KF, C_VF, C_QM, C_KA, C_VA, C_FL, C_GL = 0, 512, 1024, 1536, 2048, 2560, 2688, 2816, 3072
LO_W = 3072
PROJ_W = 6144

NN = (((1,), (0,)), ((), ()))
NT = (((1,), (1,)), ((), ()))
TN = (((0,), (0,)), ((), ()))


def _dot(a, b, dims=NN):
    return lax.dot_general(a, b, dims, preferred_element_type=F32)


def _cparams(*sem):
    return pltpu.CompilerParams(dimension_semantics=sem, vmem_limit_bytes=VMEM_LIMIT)


def _split3(a):
    hi = a.astype(BF16)
    r1 = a - hi.astype(F32)
    mid = r1.astype(BF16)
    lo = (r1 - mid.astype(F32)).astype(BF16)
    return hi, mid, lo


def _dot3_right(a, g):
    hi, mid, lo = _split3(a)
    return _dot(hi, g) + _dot(mid, g) + _dot(lo, g)


def _dot3_left(g, a):
    hi, mid, lo = _split3(a)
    return _dot(g, hi) + _dot(g, mid) + _dot(g, lo)


def _group_mean_matrix(d):
    r = jnp.arange(LANES)
    return jnp.where((r[:, None] // d) == (r[None, :] // d), 1.0 / d, 0.0).astype(BF16)


def _lane(shape):
    return lax.broadcasted_iota(jnp.int32, shape, len(shape) - 1)


def _matmul(name, a, b, *, dims, grid, a_spec, b_spec, acc_shape, outs, epilogue, extra=()):
    nk = grid[2]
    n_extra = len(extra)

    def body(a_ref, b_ref, *rest):
        extra_refs = rest[:n_extra]
        out_refs = rest[n_extra:n_extra + len(outs)]
        i, j, k = pl.program_id(0), pl.program_id(1), pl.program_id(2)
        part = _dot(a_ref[...].astype(BF16), b_ref[...].astype(BF16), dims)
        if nk == 1:
            epilogue(part, extra_refs, out_refs, (i, j))
            return
        acc_ref = rest[-1]

        @pl.when(k == 0)
        def _():
            acc_ref[...] = part

        @pl.when((k > 0) & (k < nk - 1))
        def _():
            acc_ref[...] += part

        @pl.when(k == nk - 1)
        def _():
            epilogue(acc_ref[...] + part, extra_refs, out_refs, (i, j))

    res = pl.pallas_call(
        body,
        name=name,
        grid=grid,
        in_specs=[a_spec, b_spec] + [s for _, s in extra],
        out_specs=[s for _, s in outs],
        out_shape=[s for s, _ in outs],
        scratch_shapes=[pltpu.VMEM(acc_shape, F32)] if nk > 1 else [],
        compiler_params=_cparams("arbitrary", "arbitrary", "arbitrary"),
    )(a, b, *[x for x, _ in extra])
    return res


def _epi_store(acc, extra_refs, out_refs, ij):
    out_refs[0][...] = acc.astype(out_refs[0].dtype)


def _rms_rows(x, g):
    r = lax.rsqrt(jnp.mean(x * x, axis=-1, keepdims=True) + EPS)
    return x * r, r


def _rmsnorm_bwd_rows(dh, x, g):
    xhat, r = _rms_rows(x, g)
    dxh = dh * g
    dx = r * (dxh - xhat * jnp.mean(dxh * xhat, axis=-1, keepdims=True))
    return dx, jnp.sum(dh * xhat, axis=0, keepdims=True)


def _rmsnorm(name, x, g, tb):
    T, Dm = x.shape

    def body(x_ref, g_ref, o_ref):
        xhat, _ = _rms_rows(x_ref[...], None)
        o_ref[...] = (xhat * g_ref[...]).astype(o_ref.dtype)

    return pl.pallas_call(
        body, name=name, grid=(T // tb,),
        in_specs=[pl.BlockSpec((tb, Dm), lambda i: (i, 0)), pl.BlockSpec((1, Dm), lambda i: (0, 0))],
        out_specs=pl.BlockSpec((tb, Dm), lambda i: (i, 0)),
        out_shape=jax.ShapeDtypeStruct((T, Dm), BF16),
        compiler_params=_cparams("parallel"),
    )(x, g)


def _head_norm(x, gm, gain):
    ms = _dot3_right(x * x, gm)
    r = lax.rsqrt(ms + EPS)
    return x * r * gain, x * r


def _head_norm_bwd(dy, x, gm, gain):
    ms = _dot3_right(x * x, gm)
    r = lax.rsqrt(ms + EPS)
    xhat = x * r
    dxh = dy * gain
    dx = r * (dxh - xhat * _dot3_right(dxh * xhat, gm))
    return dx, jnp.sum(dy * xhat, axis=0, keepdims=True)


def _log_sigmoid(z):
    return jnp.minimum(z, 0.0) - jnp.log(1.0 + jnp.exp(-jnp.abs(z)))


def _prep_fwd(proj, gains, bfor, tril, gm64, gm128, T, tb):
    nb = T // tb

    def body(qa_ref, qf_ref, kf_ref, vf_ref, qm_ref, ka_ref, va_ref, fl_ref, gains_ref, bfor_ref, tril_ref,
             gm64_ref, gm128_ref,
             qa_o, qf_o, kf_o, vf_o, qm_o, kad_o, vad_o, qaug_o, kaug_o, carry):
        i = pl.program_id(0)
        gm64v = gm64_ref[...]
        gm128v = gm128_ref[...]
        lane = _lane((tb, LANES))

        def norm512(src, dst, row, gm, scale=1.0):
            gain = gains_ref[row:row + 1, :]
            for c in range(4):
                sl = slice(c * LANES, (c + 1) * LANES)
                y, _ = _head_norm(src[:, sl], gm, gain)
                dst[:, sl] = (y * scale).astype(dst.dtype)

        norm512(qa_ref, qa_o, 0, gm64v)
        norm512(qf_ref, qf_o, 2, gm64v, FOX_SCALE)
        norm512(kf_ref, kf_o, 3, gm64v)
        norm512(qm_ref, qm_o, 4, gm128v)
        vf_o[...] = vf_ref[...].astype(vf_o.dtype)

        ka_n, _ = _head_norm(ka_ref[...], gm64v, gains_ref[1:2, :])
        ka_r = pltpu.roll(ka_n, 64, 1)
        va = va_ref[...]
        va_r = pltpu.roll(va, 64, 1)
        lo = lane < 64
        kad_o[0] = jnp.where(lo, ka_n, ka_r).astype(kad_o.dtype)
        kad_o[1] = jnp.where(lo, ka_r, ka_n).astype(kad_o.dtype)
        vad_o[0] = jnp.where(lo, va, va_r).astype(vad_o.dtype)
        vad_o[1] = jnp.where(lo, va_r, va).astype(vad_o.dtype)

        @pl.when(i == 0)
        def _():
            carry[...] = jnp.zeros_like(carry)

        logf = jnp.where(lane < FOX_HEADS, _log_sigmoid(fl_ref[...] + bfor_ref[...]), 0.0)
        c = _dot3_left(tril_ref[...], logf) + carry[0:1, :]
        carry[...] = jnp.broadcast_to(c[tb - 1:tb, :], carry.shape)
        for pair in range(FOX_HEADS // 2):
            qaug = jnp.zeros((tb, LANES), F32)
            kaug = jnp.zeros((tb, LANES), F32)
            for sub in range(2):
                col = jnp.sum(jnp.where(lane == 2 * pair + sub, c, 0.0), axis=1, keepdims=True)
                pieces = [p.astype(F32) for p in _split3(col)]
                base = AUG_STRIDE * sub
                for e in range(3):
                    qaug = jnp.where(lane == base + AUG_C + e, pieces[e], qaug)
                    kaug = jnp.where(lane == base + AUG_NEG_C + e, -pieces[e], kaug)
                qaug = jnp.where((lane >= base + AUG_NEG_C) & (lane < base + AUG_NEG_C + 3), 1.0, qaug)
                ones_k = ((lane >= base + AUG_C) & (lane < base + AUG_C + 3)) | (
                    (lane >= base + AUG_STAT) & (lane < base + AUG_STAT + 3))
                kaug = jnp.where(ones_k, 1.0, kaug)
            sl = slice(pair * LANES, (pair + 1) * LANES)
            qaug_o[:, sl] = qaug.astype(BF16)
            kaug_o[:, sl] = kaug.astype(BF16)

    def seg(width, start):
        return pl.BlockSpec((tb, width), lambda i, s=start // width: (i, s))

    const = lambda shape: pl.BlockSpec(shape, lambda i: tuple(0 for _ in shape))
    rows512 = pl.BlockSpec((tb, 512), lambda i: (i, 0))
    outs = pl.pallas_call(
        body, name="prep_fwd", grid=(nb,),
        in_specs=[seg(512, C_QA), seg(512, C_QF), seg(512, C_KF), seg(512, C_VF), seg(512, C_QM),
                  seg(128, C_KA), seg(128, C_VA), seg(128, C_FL),
                  const((8, LANES)), const((1, LANES)), const((tb, tb)), const((LANES, LANES)), const((LANES, LANES))],
        out_specs=[rows512, rows512, rows512, rows512, rows512,
                   pl.BlockSpec((2, tb, LANES), lambda i: (0, i, 0)), pl.BlockSpec((2, tb, LANES), lambda i: (0, i, 0)),
                   rows512, rows512],
        out_shape=[jax.ShapeDtypeStruct((T, 512), BF16)] * 5
        + [jax.ShapeDtypeStruct((2, T, LANES), BF16)] * 2
        + [jax.ShapeDtypeStruct((T, 512), BF16)] * 2,
        scratch_shapes=[pltpu.VMEM((8, LANES), F32)],
        compiler_params=_cparams("arbitrary"),
    )(proj, proj, proj, proj, proj, proj, proj, proj, gains, bfor, tril, gm64, gm128)
    return outs


def _prep_bwd(proj, dqa, dkad, dvad, dqf, dkf, dvf, dqm, dqf_aug, dkf_aug, gains, bfor, triu, gm64, gm128, T, tb):
    nb = T // tb

    def body(qa_ref, qf_ref, kf_ref, qm_ref, ka_ref, fl_ref,
             dqa_ref, dkad_ref, dvad_ref, dqf_ref, dkf_ref, dvf_ref, dqm_ref, dqfa_ref, dkfa_ref,
             gains_ref, bfor_ref, triu_ref, gm64_ref, gm128_ref,
             dlo_o, gacc_o, carry):
        i = pl.program_id(0)
        gm64v = gm64_ref[...]
        gm128v = gm128_ref[...]
        lane = _lane((tb, LANES))

        @pl.when(i == 0)
        def _():
            carry[...] = jnp.zeros_like(carry)
            gacc_o[...] = jnp.zeros_like(gacc_o)

        def norm512_bwd(dsrc, xsrc, col0, row, gm):
            gain = gains_ref[row:row + 1, :]
            gsum = jnp.zeros((1, LANES), F32)
            for c in range(4):
                sl = slice(c * LANES, (c + 1) * LANES)
                dx, dg = _head_norm_bwd(dsrc[:, sl], xsrc[:, sl], gm, gain)
                dlo_o[:, col0 + c * LANES:col0 + (c + 1) * LANES] = dx.astype(dlo_o.dtype)
                gsum = gsum + dg
            gacc_o[row:row + 1, :] += gsum

        norm512_bwd(dqa_ref, qa_ref, C_QA, 0, gm64v)
        norm512_bwd(dqf_ref, qf_ref, C_QF, 2, gm64v)
        norm512_bwd(dkf_ref, kf_ref, C_KF, 3, gm64v)
        norm512_bwd(dqm_ref, qm_ref, C_QM, 4, gm128v)
        dlo_o[:, C_VF:C_VF + 512] = dvf_ref[...].astype(dlo_o.dtype)

        lo = lane < 64

        def fold(ref):
            f0 = ref[0] + pltpu.roll(ref[0], 64, 1)
            f1 = ref[1] + pltpu.roll(ref[1], 64, 1)
            return jnp.where(lo, f0, f1)

        dka, dg = _head_norm_bwd(fold(dkad_ref), ka_ref[...], gm64v, gains_ref[1:2, :])
        gacc_o[1:2, :] += dg
        dlo_o[:, C_KA:C_KA + LANES] = dka.astype(dlo_o.dtype)
        dlo_o[:, C_VA:C_VA + LANES] = fold(dvad_ref).astype(dlo_o.dtype)

        dc = jnp.zeros((tb, LANES), F32)
        for pair in range(FOX_HEADS // 2):
            sl = slice(pair * LANES, (pair + 1) * LANES)
            rows_sum, cols_sum = dqfa_ref[:, sl], dkfa_ref[:, sl]
            for sub in range(2):
                diff = (jnp.where(lane == AUG_STRIDE * sub + AUG_C, rows_sum, 0.0)
                        - jnp.where(lane == AUG_STRIDE * sub + AUG_NEG_C, cols_sum, 0.0))
                dc = jnp.where(lane == 2 * pair + sub, jnp.sum(diff, axis=1, keepdims=True), dc)
        dlogf = _dot3_left(triu_ref[...], dc) + carry[0:1, :]
        carry[...] = jnp.broadcast_to(dlogf[0:1, :], carry.shape)
        z = fl_ref[...] + bfor_ref[...]
        dfl = jnp.where(lane < FOX_HEADS, dlogf / (1.0 + jnp.exp(z)), 0.0)
        gacc_o[5:6, :] += jnp.sum(dfl, axis=0, keepdims=True)
        dlo_o[:, C_FL:C_FL + LANES] = dfl.astype(dlo_o.dtype)
        dlo_o[:, C_FL + LANES:C_FL + 2 * LANES] = jnp.zeros((tb, LANES), dlo_o.dtype)

    rev = lambda i: nb - 1 - i

    def seg(width, start):
        return pl.BlockSpec((tb, width), lambda i, s=start // width: (rev(i), s))

    const = lambda shape: pl.BlockSpec(shape, lambda i: tuple(0 for _ in shape))
    rows512 = pl.BlockSpec((tb, 512), lambda i: (rev(i), 0))
    dup = pl.BlockSpec((2, tb, LANES), lambda i: (0, rev(i), 0))
    return pl.pallas_call(
        body, name="prep_bwd", grid=(nb,),
        in_specs=[seg(512, C_QA), seg(512, C_QF), seg(512, C_KF), seg(512, C_QM), seg(128, C_KA), seg(128, C_FL),
                  rows512, dup, dup, rows512, rows512, rows512, rows512, rows512, rows512,
                  const((8, LANES)), const((1, LANES)), const((tb, tb)), const((LANES, LANES)), const((LANES, LANES))],
        out_specs=[pl.BlockSpec((tb, LO_W), lambda i: (rev(i), 0)), const((8, LANES))],
        out_shape=[jax.ShapeDtypeStruct((T, LO_W), BF16), jax.ShapeDtypeStruct((8, LANES), F32)],
        scratch_shapes=[pltpu.VMEM((8, LANES), F32)],
        compiler_params=_cparams("arbitrary"),
    )(proj, proj, proj, proj, proj, proj, dqa, dkad, dvad, dqf, dkf, dvf, dqm, dqf_aug, dkf_aug,
      gains, bfor, triu, gm64, gm128)


FOX_SCALE = FOX_HEAD_DIM ** -0.5
AUG_STRIDE = 16
AUG_C = 0
AUG_NEG_C = 3
AUG_STAT = 6


def _fox_head_mask(sub, rows):
    lane = _lane((rows, 2 * LANES))
    main = (lane >= 64 * sub) & (lane < 64 * sub + 64)
    aug = (lane >= LANES + AUG_STRIDE * sub) & (lane < LANES + AUG_STRIDE * (sub + 1))
    return main | aug


def _fox_fwd(q, qaug, k, kaug, v, T, tq):
    nq = T // tq
    tk = tq
    rep = tk // LANES

    def body(q_ref, qa_ref, k_ref, ka_ref, v_ref, o_ref, qab_ref, m_s, acc_s):
        p_, i, j = pl.program_id(0), pl.program_id(1), pl.program_id(2)

        @pl.when(j == 0)
        def _():
            m_s[...] = jnp.full(m_s.shape, NEG, F32)
            acc_s[...] = jnp.zeros_like(acc_s)

        def step(diagonal):
            q2 = jnp.concatenate([q_ref[...], qa_ref[...]], axis=1)
            k2 = jnp.concatenate([k_ref[...], ka_ref[...]], axis=1)
            v2 = jnp.concatenate([v_ref[...], ka_ref[...]], axis=1)
            if diagonal:
                causal = (lax.broadcasted_iota(jnp.int32, (tq, tk), 1) <= lax.broadcasted_iota(jnp.int32, (tq, tk), 0))
            for sub in range(2):
                qh = jnp.where(_fox_head_mask(sub, tq), q2, jnp.zeros_like(q2))
                s = _dot(qh, k2, NT)
                if diagonal:
                    s = jnp.where(causal, s, NEG)
                m_prev = m_s[sub]
                m_next = jnp.maximum(m_prev, jnp.max(s, axis=1, keepdims=True))
                p = jnp.exp(s - jnp.tile(m_next, (1, rep)))
                alpha = jnp.exp(m_prev - m_next)
                m_s[sub] = m_next
                acc_s[sub] = acc_s[sub] * jnp.tile(alpha, (1, 2)) + _dot(p.astype(BF16), v2)

        @pl.when(j == i)
        def _():
            step(True)

        @pl.when(j < i)
        def _():
            step(False)

        @pl.when(j == nq - 1)
        def _():
            lane = _lane((tq, LANES))
            outs = []
            qab = qa_ref[...].astype(F32)
            for sub in range(2):
                acc = acc_s[sub]
                base = AUG_STRIDE * sub
                l = jnp.sum(jnp.where(lane == base + AUG_C, acc[:, LANES:], 0.0), axis=1, keepdims=True)
                outs.append(acc[:, :LANES] / l)
                lse = jnp.max(m_s[sub], axis=1, keepdims=True) + jnp.log(l)
                pieces = _split3(-lse)
                for e in range(3):
                    qab = jnp.where(lane == base + AUG_STAT + e, pieces[e].astype(F32), qab)
            o_ref[...] = jnp.where(lane < 64, outs[0], outs[1]).astype(o_ref.dtype)
            qab_ref[...] = qab.astype(BF16)

    qspec = pl.BlockSpec((tq, LANES), lambda p, i, j: (i, p))
    kspec = pl.BlockSpec((tk, LANES), lambda p, i, j: (jnp.minimum(j, i), p))
    return pl.pallas_call(
        body, name="fox_fwd", grid=(4, nq, nq),
        in_specs=[qspec, qspec, kspec, kspec, kspec],
        out_specs=[qspec, qspec],
        out_shape=[jax.ShapeDtypeStruct((T, 512), BF16), jax.ShapeDtypeStruct((T, 512), BF16)],
        scratch_shapes=[pltpu.VMEM((2, tq, LANES), F32), pltpu.VMEM((2, tq, 2 * LANES), F32)],
        compiler_params=_cparams("parallel", "parallel", "arbitrary"),
    )(q, qaug, k, kaug, v)


def _fox_bwd(q, qaug, k, kaug, v, do, doaug, T, tq):
    nq = T // tq
    tk = tq

    def body(q_ref, qa_ref, k_ref, ka_ref, v_ref, do_ref, doa_ref,
             dq_ref, dqa_ref, dk_ref, dka_ref, dv_ref, dk_s, dv_s):
        p_, j, i = pl.program_id(0), pl.program_id(1), pl.program_id(2)

        @pl.when((j == 0) & (i == 0))
        def _():
            dq_ref[...] = jnp.zeros_like(dq_ref)
            dqa_ref[...] = jnp.zeros_like(dqa_ref)

        @pl.when(i == 0)
        def _():
            dk_s[...] = jnp.zeros_like(dk_s)
            dv_s[...] = jnp.zeros_like(dv_s)

        def step(diagonal):
            q2 = jnp.concatenate([q_ref[...], qa_ref[...]], axis=1)
            k2 = jnp.concatenate([k_ref[...], ka_ref[...]], axis=1)
            v2 = jnp.concatenate([v_ref[...], ka_ref[...]], axis=1)
            do2 = jnp.concatenate([do_ref[...], doa_ref[...]], axis=1)
            if diagonal:
                causal = (lax.broadcasted_iota(jnp.int32, (tq, tk), 1) <= lax.broadcasted_iota(jnp.int32, (tq, tk), 0))
            dqs = []
            for sub in range(2):
                hm = _fox_head_mask(sub, tq)
                qh = jnp.where(hm, q2, jnp.zeros_like(q2))
                doh = jnp.where(hm, do2, jnp.zeros_like(do2))
                s = _dot(qh, k2, NT)
                if diagonal:
                    s = jnp.where(causal, s, NEG)
                p = jnp.exp(s)
                ds = p * _dot(doh, v2, NT)
                dsb = ds.astype(BF16)
                dv_s[...] += _dot(p.astype(BF16), doh[:, :LANES], TN)
                dk_s[...] += _dot(dsb, qh, TN)
                dqs.append(_dot(dsb, k2))
            dq2 = jnp.where(_fox_head_mask(0, tq), dqs[0], dqs[1])
            qrows = pl.ds(pl.multiple_of(i * tq, tq), tq)
            dq_ref[qrows, :] += dq2[:, :LANES] * FOX_SCALE
            dqa_ref[qrows, :] += dq2[:, LANES:]

        @pl.when(i == j)
        def _():
            step(True)

        @pl.when(i > j)
        def _():
            step(False)

        @pl.when(i == nq - 1)
        def _():
            dk_ref[...] = dk_s[:, :LANES]
            dka_ref[...] = dk_s[:, LANES:]
            dv_ref[...] = dv_s[...]

    qspec = pl.BlockSpec((tq, LANES), lambda p, j, i: (jnp.maximum(i, j), p))
    kspec = pl.BlockSpec((tk, LANES), lambda p, j, i: (j, p))
    resident = pl.BlockSpec((T, LANES), lambda p, j, i: (0, p))
    return pl.pallas_call(
        body, name="fox_bwd", grid=(4, nq, nq),
        in_specs=[qspec, qspec, kspec, kspec, kspec, qspec, qspec],
        out_specs=[resident, resident, kspec, kspec, kspec],
        out_shape=[jax.ShapeDtypeStruct((T, 512), F32)] * 5,
        scratch_shapes=[pltpu.VMEM((tk, 2 * LANES), F32), pltpu.VMEM((tk, LANES), F32)],
        compiler_params=_cparams("arbitrary", "arbitrary", "arbitrary"),
    )(q, qaug, k, kaug, v, do, doaug)


SWA_SUB = 4
SWA_TB = SWA_SUB * WINDOW


def _t5_bucket_matrix():
    t = jnp.arange(WINDOW)[:, None] + WINDOW
    s = jnp.arange(2 * WINDOW)[None, :]
    max_exact = REL_BUCKETS // 2
    d = jnp.maximum(t - s, 0)
    df = jnp.maximum(d, 1).astype(F32)
    large = max_exact + (jnp.log(df / max_exact) / math.log(REL_MAX_DIST / max_exact)
                         * (REL_BUCKETS - max_exact)).astype(jnp.int32)
    large = jnp.minimum(large, REL_BUCKETS - 1)
    return jnp.where(d < max_exact, d, large).astype(jnp.int32)


def _swa_bias(rel_bias, bucket):
    def body(rel_ref, bucket_ref, o_ref):
        b = bucket_ref[...]
        for h in range(SWA_HEADS):
            acc = jnp.zeros(b.shape, F32)
            for r in range(REL_BUCKETS):
                acc = jnp.where(b == r, rel_ref[r, h], acc)
            o_ref[h] = acc

    return pl.pallas_call(
        body, name="swa_bias",
        in_specs=[pl.BlockSpec(memory_space=pltpu.SMEM), pl.BlockSpec(memory_space=pltpu.VMEM)],
        out_specs=pl.BlockSpec(memory_space=pltpu.VMEM),
        out_shape=jax.ShapeDtypeStruct((SWA_HEADS, WINDOW, 2 * WINDOW), F32),
    )(rel_bias, bucket)


def _swa_bias_bwd(dbias, bucket):
    def body(db_ref, bucket_ref, o_ref):
        b = bucket_ref[...]
        lane = _lane((1, LANES))
        for r in range(REL_BUCKETS):
            row = jnp.zeros((1, LANES), F32)
            for h in range(SWA_HEADS):
                part = jnp.sum(jnp.where(b == r, db_ref[h], 0.0), axis=0, keepdims=True)
                tot = jnp.sum(part, axis=1, keepdims=True)
                row = jnp.where(lane == h, tot, row)
            o_ref[r:r + 1, :] = row

    return pl.pallas_call(
        body, name="swa_bias_bwd",
        in_specs=[pl.BlockSpec(memory_space=pltpu.VMEM), pl.BlockSpec(memory_space=pltpu.VMEM)],
        out_specs=pl.BlockSpec(memory_space=pltpu.VMEM),
        out_shape=jax.ShapeDtypeStruct((REL_BUCKETS, LANES), F32),
    )(dbias, bucket)


SWA_GROUP = SWA_HEADS // SWA_KV_HEADS


def _swa_valid(r, i):
    t = (lax.broadcasted_iota(jnp.int32, (SWA_GROUP * WINDOW, 2 * WINDOW), 0) & (WINDOW - 1)) + WINDOW
    s = lax.broadcasted_iota(jnp.int32, (SWA_GROUP * WINDOW, 2 * WINDOW), 1)
    dist = t - s
    band = (dist >= 0) & (dist < WINDOW)
    if r == 0:
        band = band & ((s >= WINDOW) | (i > 0))
    return band


def _swa_stack(blk):
    lane = _lane((WINDOW, LANES))
    parts = []
    for g in range(SWA_GROUP):
        b = blk[:, LANES * (g // 2):LANES * (g // 2 + 1)]
        parts.append(jnp.where((lane >= 64) if g % 2 else (lane < 64), b, jnp.zeros_like(b)))
    return jnp.concatenate(parts, axis=0)


def _swa_unstack(st):
    lane = _lane((WINDOW, LANES))
    W = WINDOW
    return jnp.concatenate([jnp.where(lane < 64, st[2 * b * W:(2 * b + 1) * W], st[(2 * b + 1) * W:(2 * b + 2) * W])
                            for b in range(2)], axis=1)


def _swa_sink_column(sink_ref, kvh):
    row = lax.broadcasted_iota(jnp.int32, (SWA_GROUP * WINDOW, 1), 0)
    col = jnp.full((SWA_GROUP * WINDOW, 1), sink_ref[SWA_GROUP * kvh + SWA_GROUP - 1], F32)
    for g in range(SWA_GROUP - 2, -1, -1):
        col = jnp.where(row < (g + 1) * WINDOW, sink_ref[SWA_GROUP * kvh + g], col)
    return col


def _swa_specs(T):
    W = WINDOW
    qspec = pl.BlockSpec((SWA_TB, 2 * LANES), lambda h, i: (i, h))
    own = pl.BlockSpec((None, SWA_TB, LANES), lambda h, i: (h, i, 0))
    prev = pl.BlockSpec((None, W, LANES), lambda h, i: (h, jnp.maximum(SWA_SUB * i - 1, 0), 0))
    stat = pl.BlockSpec((SWA_GROUP, SWA_TB, LANES), lambda h, i: (h, i, 0))
    bias = pl.BlockSpec((None, SWA_GROUP * W, 2 * W), lambda h, i: (h, 0, 0))
    return qspec, own, prev, stat, bias


def _swa_fwd(sinks, q, kad, vad, bias, T):
    nb = T // SWA_TB
    scale = SWA_HEAD_DIM ** -0.5
    W = WINDOW

    def body(sink_ref, q_ref, k_ref, kp_ref, v_ref, vp_ref, bias_ref, o_ref, lse_ref):
        kvh, i = pl.program_id(0), pl.program_id(1)
        sink = _swa_sink_column(sink_ref, kvh)
        for r in range(SWA_SUB):
            rs = slice(r * W, (r + 1) * W)
            ps = slice((r - 1) * W, r * W)
            k_own, v_own = k_ref[rs, :], v_ref[rs, :]
            k_prev = kp_ref[...] if r == 0 else k_ref[ps, :]
            v_prev = vp_ref[...] if r == 0 else v_ref[ps, :]
            qs = _swa_stack(q_ref[rs, :])
            s = jnp.concatenate([_dot(qs, k_prev, NT), _dot(qs, k_own, NT)], axis=1) * scale + bias_ref[...]
            s = jnp.where(_swa_valid(r, i), s, NEG)
            m = jnp.maximum(jnp.max(s, axis=1, keepdims=True), sink)
            p = jnp.exp(s - m)
            denom = jnp.sum(p, axis=1, keepdims=True) + jnp.exp(sink - m)
            pn = (p / denom).astype(BF16)
            o_ref[rs, :] = _swa_unstack(_dot(pn[:, :W], v_prev) + _dot(pn[:, W:], v_own)).astype(o_ref.dtype)
            lse = m + jnp.log(denom)
            for g in range(SWA_GROUP):
                lse_ref[g, rs, :] = jnp.broadcast_to(lse[g * W:(g + 1) * W], (W, LANES))

    qspec, own, prev, stat, bspec = _swa_specs(T)
    return pl.pallas_call(
        body, name="swa_fwd", grid=(SWA_KV_HEADS, nb),
        in_specs=[pl.BlockSpec(memory_space=pltpu.SMEM), qspec, own, prev, own, prev, bspec],
        out_specs=[qspec, stat],
        out_shape=[jax.ShapeDtypeStruct((T, 512), BF16), jax.ShapeDtypeStruct((SWA_HEADS, T, LANES), F32)],
        compiler_params=_cparams("parallel", "parallel"),
    )(sinks, q, kad, kad, vad, vad, bias.reshape(SWA_KV_HEADS, SWA_GROUP * W, 2 * W))


def _swa_bwd(sinks, q, kad, vad, bias, do, lse, delta, T):
    nb = T // SWA_TB
    scale = SWA_HEAD_DIM ** -0.5
    W = WINDOW

    def body(sink_ref, q_ref, k_ref, kp_ref, v_ref, vp_ref, bias_ref, do_ref, lse_ref, dl_ref,
             dq_ref, dkad_ref, dvad_ref, dbias_ref, dsk_ref):
        kvh, i = pl.program_id(0), pl.program_id(1)
        sink = _swa_sink_column(sink_ref, kvh)

        @pl.when((kvh == 0) & (i == 0))
        def _():
            dkad_ref[...] = jnp.zeros_like(dkad_ref)
            dvad_ref[...] = jnp.zeros_like(dvad_ref)

        @pl.when(i == 0)
        def _():
            dbias_ref[...] = jnp.zeros_like(dbias_ref)
            dsk_ref[...] = jnp.zeros_like(dsk_ref)

        for r in range(SWA_SUB):
            rs = slice(r * W, (r + 1) * W)
            ps = slice((r - 1) * W, r * W)
            k_own, v_own = k_ref[rs, :], v_ref[rs, :]
            k_prev = kp_ref[...] if r == 0 else k_ref[ps, :]
            v_prev = vp_ref[...] if r == 0 else v_ref[ps, :]
            qs = _swa_stack(q_ref[rs, :])
            dos = _swa_stack(do_ref[rs, :])
            lse_b = jnp.concatenate([lse_ref[g, rs, :] for g in range(SWA_GROUP)], axis=0)
            dl_b = jnp.concatenate([dl_ref[g, rs, :] for g in range(SWA_GROUP)], axis=0)
            s = jnp.concatenate([_dot(qs, k_prev, NT), _dot(qs, k_own, NT)], axis=1) * scale + bias_ref[...]
            s = jnp.where(_swa_valid(r, i), s, NEG)
            p = jnp.exp(s - jnp.tile(lse_b, (1, 2)))
            dp = jnp.concatenate([_dot(dos, v_prev, NT), _dot(dos, v_own, NT)], axis=1)
            ds = p * (dp - jnp.tile(dl_b, (1, 2)))
            sink_term = jnp.exp(sink - lse_b) * dl_b
            for g in range(SWA_GROUP):
                dbias_ref[g] += ds[g * W:(g + 1) * W]
                dsk_ref[g:g + 1, :] += jnp.sum(sink_term[g * W:(g + 1) * W], axis=0, keepdims=True)
            dsb = ds.astype(BF16)
            pb = p.astype(BF16)
            dq_ref[rs, :] = _swa_unstack((_dot(dsb[:, :W], k_prev) + _dot(dsb[:, W:], k_own)) * scale)
            own_row = pl.multiple_of(i * SWA_TB + r * W, W)
            dkad_ref[kvh, pl.ds(own_row, W), :] += _dot(dsb[:, W:], qs, TN) * scale
            dvad_ref[kvh, pl.ds(own_row, W), :] += _dot(pb[:, W:], dos, TN)
            dk_prev = _dot(dsb[:, :W], qs, TN) * scale
            dv_prev = _dot(pb[:, :W], dos, TN)
            if r == 0:
                @pl.when(i > 0)
                def _():
                    prev_row = pl.multiple_of(i * SWA_TB - W, W)
                    dkad_ref[kvh, pl.ds(prev_row, W), :] += dk_prev
                    dvad_ref[kvh, pl.ds(prev_row, W), :] += dv_prev
            else:
                prev_row = pl.multiple_of(i * SWA_TB + (r - 1) * W, W)
                dkad_ref[kvh, pl.ds(prev_row, W), :] += dk_prev
                dvad_ref[kvh, pl.ds(prev_row, W), :] += dv_prev

    qspec, own, prev, stat, bspec = _swa_specs(T)
    full = pl.BlockSpec((SWA_KV_HEADS, T, LANES), lambda h, i: (0, 0, 0))
    return pl.pallas_call(
        body, name="swa_bwd", grid=(SWA_KV_HEADS, nb),
        in_specs=[pl.BlockSpec(memory_space=pltpu.SMEM), qspec, own, prev, own, prev, bspec, qspec, stat, stat],
        out_specs=[qspec, full, full, pl.BlockSpec((SWA_GROUP, W, 2 * W), lambda h, i: (h, 0, 0)),
                   pl.BlockSpec((None, 8, LANES), lambda h, i: (h, 0, 0))],
        out_shape=[jax.ShapeDtypeStruct((T, 512), F32), jax.ShapeDtypeStruct((SWA_KV_HEADS, T, LANES), F32),
                   jax.ShapeDtypeStruct((SWA_KV_HEADS, T, LANES), F32), jax.ShapeDtypeStruct((SWA_HEADS, W, 2 * W), F32),
                   jax.ShapeDtypeStruct((SWA_KV_HEADS, 8, LANES), F32)],
        compiler_params=_cparams("arbitrary", "arbitrary"),
    )(sinks, q, kad, kad, vad, vad, bias.reshape(SWA_KV_HEADS, SWA_GROUP * W, 2 * W), do, lse, delta)


def _mem_fwd(q, mk, mv, T, tq):
    scale = MEM_HEAD_DIM ** -0.5

    def body(q_ref, k_ref, v_ref, o_ref, lse_ref):
        s = _dot(q_ref[...], k_ref[...], NT) * scale
        m = jnp.max(s, axis=1, keepdims=True)
        p = jnp.exp(s - m)
        l = jnp.sum(p, axis=1, keepdims=True)
        o_ref[...] = _dot((p / l).astype(BF16), v_ref[...]).astype(o_ref.dtype)
        lse_ref[...] = jnp.broadcast_to(m + jnp.log(l), (tq, LANES))

    qspec = pl.BlockSpec((tq, LANES), lambda h, i: (i, h))
    kspec = pl.BlockSpec((N_MEM, LANES), lambda h, i: (0, h))
    return pl.pallas_call(
        body, name="mem_fwd", grid=(MEM_HEADS, T // tq),
        in_specs=[qspec, kspec, kspec],
        out_specs=[qspec, pl.BlockSpec((None, tq, LANES), lambda h, i: (h, i, 0))],
        out_shape=[jax.ShapeDtypeStruct((T, 512), BF16), jax.ShapeDtypeStruct((MEM_HEADS, T, LANES), F32)],
        compiler_params=_cparams("parallel", "parallel"),
    )(q, mk, mv)


def _mem_bwd(q, mk, mv, do, lse, delta, T, tq):
    scale = MEM_HEAD_DIM ** -0.5
    rep = N_MEM // LANES

    def body(q_ref, k_ref, v_ref, do_ref, lse_ref, dl_ref, dq_ref, dk_ref, dv_ref):
        i = pl.program_id(1)

        @pl.when(i == 0)
        def _():
            dk_ref[...] = jnp.zeros_like(dk_ref)
            dv_ref[...] = jnp.zeros_like(dv_ref)

        qv, dov = q_ref[...], do_ref[...]
        s = _dot(qv, k_ref[...], NT) * scale
        p = jnp.exp(s - jnp.tile(lse_ref[...], (1, rep)))
        dp = _dot(dov, v_ref[...], NT)
        ds = p * (dp - jnp.tile(dl_ref[...], (1, rep)))
        dsb = ds.astype(BF16)
        dq_ref[...] = _dot(dsb, k_ref[...]) * scale
        dk_ref[...] += _dot(dsb, qv, TN) * scale
        dv_ref[...] += _dot(p.astype(BF16), dov, TN)

    qspec = pl.BlockSpec((tq, LANES), lambda h, i: (i, h))
    kspec = pl.BlockSpec((N_MEM, LANES), lambda h, i: (0, h))
    stat = pl.BlockSpec((None, tq, LANES), lambda h, i: (h, i, 0))
    return pl.pallas_call(
        body, name="mem_bwd", grid=(MEM_HEADS, T // tq),
        in_specs=[qspec, kspec, kspec, qspec, stat, stat],
        out_specs=[qspec, kspec, kspec],
        out_shape=[jax.ShapeDtypeStruct((T, 512), F32), jax.ShapeDtypeStruct((N_MEM, 512), F32),
                   jax.ShapeDtypeStruct((N_MEM, 512), F32)],
        compiler_params=_cparams("arbitrary", "arbitrary"),
    )(q, mk, mv, do, lse, delta)


def _mem_prep_fwd(mem, g_mem, w_kv, kn_gain, gm128):
    def body(mem_ref, g_ref, w_ref, kn_ref, gm_ref, memn_o, kv_o, mk_o, mv_o):
        xhat, _ = _rms_rows(mem_ref[...], None)
        memn = (xhat * g_ref[...]).astype(BF16)
        memn_o[...] = memn
        kv = _dot(memn, w_ref[...])
        kv_o[...] = kv
        gm = gm_ref[...]
        for c in range(4):
            sl = slice(c * LANES, (c + 1) * LANES)
            y, _ = _head_norm(kv[:, sl], gm, kn_ref[...])
            mk_o[:, sl] = y.astype(BF16)
        mv_o[...] = kv[:, 512:].astype(BF16)

    vm = pl.BlockSpec(memory_space=pltpu.VMEM)
    return pl.pallas_call(
        body, name="mem_prep_fwd", in_specs=[vm] * 5, out_specs=[vm] * 4,
        out_shape=[jax.ShapeDtypeStruct((N_MEM, D_MODEL), BF16), jax.ShapeDtypeStruct((N_MEM, D_MODEL), F32),
                   jax.ShapeDtypeStruct((N_MEM, 512), BF16), jax.ShapeDtypeStruct((N_MEM, 512), BF16)],
        compiler_params=pltpu.CompilerParams(vmem_limit_bytes=VMEM_LIMIT),
    )(mem, g_mem, w_kv, kn_gain, gm128)


def _mem_prep_bwd(mem, g_mem, memn, kv, w_kv, kn_gain, gm128, dmk, dmv):
    def body(mem_ref, g_ref, memn_ref, kv_ref, w_ref, kn_ref, gm_ref, dmk_ref, dmv_ref, dw_o, dg_o, dkn_o, dkv_s):
        gm = gm_ref[...]
        dkn = jnp.zeros((1, LANES), F32)
        for c in range(4):
            sl = slice(c * LANES, (c + 1) * LANES)
            dx, dg = _head_norm_bwd(dmk_ref[:, sl], kv_ref[:, sl], gm, kn_ref[...])
            dkv_s[:, sl] = dx.astype(BF16)
            dkn = dkn + dg
        dkn_o[...] = dkn
        dkv_s[:, 512:] = dmv_ref[...].astype(BF16)
        dkv = dkv_s[...]
        dw_o[...] = _dot(memn_ref[...], dkv, TN)
        dmemn = _dot(dkv, w_ref[...], NT)
        xhat, _ = _rms_rows(mem_ref[...], None)
        dg_o[...] = jnp.sum(dmemn * xhat, axis=0, keepdims=True)

    vm = pl.BlockSpec(memory_space=pltpu.VMEM)
    return pl.pallas_call(
        body, name="mem_prep_bwd", in_specs=[vm] * 9, out_specs=[vm] * 3,
        out_shape=[jax.ShapeDtypeStruct((D_MODEL, D_MODEL), F32), jax.ShapeDtypeStruct((1, D_MODEL), F32),
                   jax.ShapeDtypeStruct((1, LANES), F32)],
        scratch_shapes=[pltpu.VMEM((N_MEM, D_MODEL), BF16)],
        compiler_params=pltpu.CompilerParams(vmem_limit_bytes=VMEM_LIMIT),
    )(mem, g_mem, memn, kv, w_kv, kn_gain, gm128, dmk, dmv)


SLOT_O = D_MODEL // N_SHARD


def _merge_fwd(proj, b_gate, o3, w3, T, tb):
    def body(gl_ref, bg_ref, oa_ref, of_ref, om_ref, wa_ref, wf_ref, wm_ref, out_ref):
        o_refs = (oa_ref, of_ref, om_ref)
        w_refs = (wa_ref, wf_ref, wm_ref)
        for n in range(N_SHARD):
            acc = jnp.zeros((tb, SLOT_O), F32)
            for b in range(3):
                c0 = b * D_MODEL + n * SLOT_O
                g = jax.nn.sigmoid(gl_ref[:, c0:c0 + SLOT_O] + bg_ref[:, c0:c0 + SLOT_O])
                acc = acc + g * _dot(o_refs[b][...], w_refs[b][n])
            out_ref[:, n * SLOT_O:(n + 1) * SLOT_O] = acc.astype(out_ref.dtype)

    rows = pl.BlockSpec((tb, 512), lambda i: (i, 0))
    wspec = pl.BlockSpec((N_SHARD, 512, SLOT_O), lambda i: (0, 0, 0))
    return pl.pallas_call(
        body, name="merge_fwd", grid=(T // tb,),
        in_specs=[pl.BlockSpec((tb, GATE_W), lambda i: (i, 1)), pl.BlockSpec((1, GATE_W), lambda i: (0, 0)),
                  rows, rows, rows, wspec, wspec, wspec],
        out_specs=pl.BlockSpec((tb, D_MODEL), lambda i: (i, 0)),
        out_shape=jax.ShapeDtypeStruct((T, D_MODEL), BF16),
        compiler_params=_cparams("parallel"),
    )(proj, b_gate, *o3, *w3)


def _merge_bwd(proj, b_gate, o3, w3, dmerged, T, tb):
    heads = (SWA_HEADS, FOX_HEADS, MEM_HEADS)

    def body(gl_ref, bg_ref, oa_ref, of_ref, om_ref, wa_ref, wf_ref, wm_ref, dm_ref,
             dgl_o, doa_o, dof_o, dom_o, dla_o, dlf_o, dlm_o, dwa_o, dwf_o, dwm_o, dbg_o):
        i = pl.program_id(0)
        o_refs = (oa_ref, of_ref, om_ref)
        w_refs = (wa_ref, wf_ref, wm_ref)
        do_refs = (doa_o, dof_o, dom_o)
        dl_refs = (dla_o, dlf_o, dlm_o)
        dw_refs = (dwa_o, dwf_o, dwm_o)

        @pl.when(i == 0)
        def _():
            for r in dw_refs:
                r[...] = jnp.zeros_like(r)
            dbg_o[...] = jnp.zeros_like(dbg_o)

        lane = _lane((tb, LANES))
        for b in range(3):
            ob = o_refs[b][...]
            do = jnp.zeros((tb, 512), F32)
            for n in range(N_SHARD):
                c0 = b * D_MODEL + n * SLOT_O
                g = jax.nn.sigmoid(gl_ref[:, c0:c0 + SLOT_O] + bg_ref[:, c0:c0 + SLOT_O])
                dm = dm_ref[:, n * SLOT_O:(n + 1) * SLOT_O]
                y = _dot(ob, w_refs[b][n])
                dgl = dm * y * g * (1.0 - g)
                dgl_o[:, c0:c0 + SLOT_O] = dgl.astype(dgl_o.dtype)
                dbg_o[:, c0:c0 + SLOT_O] += jnp.sum(dgl, axis=0, keepdims=True)
                dy = (dm * g).astype(BF16)
                do = do + _dot(dy, w_refs[b][n], NT)
                dw_refs[b][n] += _dot(ob, dy, TN)
            do_refs[b][...] = do.astype(BF16)
            prod = do * ob.astype(F32)
            for c in range(4):
                blk = prod[:, c * LANES:(c + 1) * LANES]
                if heads[b] == 8:
                    lo = jnp.sum(jnp.where(lane < 64, blk, 0.0), axis=1, keepdims=True)
                    hi = jnp.sum(jnp.where(lane >= 64, blk, 0.0), axis=1, keepdims=True)
                    if b == 1:
                        aug = jnp.zeros((tb, LANES), F32)
                        for sub, dl in enumerate((lo, hi)):
                            for e, piece in enumerate(_split3(-dl)):
                                aug = jnp.where(lane == AUG_STRIDE * sub + AUG_C + e, piece.astype(F32), aug)
                        dl_refs[b][:, c * LANES:(c + 1) * LANES] = aug.astype(BF16)
                    else:
                        dl_refs[b][2 * c] = jnp.broadcast_to(lo, (tb, LANES))
                        dl_refs[b][2 * c + 1] = jnp.broadcast_to(hi, (tb, LANES))
                else:
                    dl_refs[b][c] = jnp.broadcast_to(jnp.sum(blk, axis=1, keepdims=True), (tb, LANES))

    rows = pl.BlockSpec((tb, 512), lambda i: (i, 0))
    wspec = pl.BlockSpec((N_SHARD, 512, SLOT_O), lambda i: (0, 0, 0))
    stat = lambda h: pl.BlockSpec((h, tb, LANES), lambda i: (0, i, 0))
    return pl.pallas_call(
        body, name="merge_bwd", grid=(T // tb,),
        in_specs=[pl.BlockSpec((tb, GATE_W), lambda i: (i, 1)), pl.BlockSpec((1, GATE_W), lambda i: (0, 0)),
                  rows, rows, rows, wspec, wspec, wspec, pl.BlockSpec((tb, D_MODEL), lambda i: (i, 0))],
        out_specs=[pl.BlockSpec((tb, GATE_W), lambda i: (i, 0)), rows, rows, rows,
                   stat(8), rows, stat(4), wspec, wspec, wspec, pl.BlockSpec((1, GATE_W), lambda i: (0, 0))],
        out_shape=[jax.ShapeDtypeStruct((T, GATE_W), BF16)] + [jax.ShapeDtypeStruct((T, 512), BF16)] * 3
        + [jax.ShapeDtypeStruct((8, T, LANES), F32), jax.ShapeDtypeStruct((T, 512), BF16),
           jax.ShapeDtypeStruct((4, T, LANES), F32)]
        + [jax.ShapeDtypeStruct((N_SHARD, 512, SLOT_O), F32)] * 3 + [jax.ShapeDtypeStruct((1, GATE_W), F32)],
        compiler_params=_cparams("arbitrary"),
    )(proj, b_gate, *o3, *w3, dmerged)


def _local_step(x, mem, tgt, small, wc, w_kv, w_o3, w_out, w_up, w_down, reducer):
    T = x.shape[0]
    tm = min(512, T)
    tile2 = lambda v: jnp.tile(v.reshape(1, -1), (1, LANES // v.size))
    gains = jnp.concatenate([tile2(small["qn_swa"]), tile2(small["kn_swa"]), tile2(small["qn_fox"]),
                             tile2(small["kn_fox"]), tile2(small["qn_mem"]), jnp.zeros((3, LANES), F32)], axis=0)
    kn_mem = small["kn_mem"].reshape(1, LANES)
    bfor = jnp.pad(small["b_forget"].reshape(1, -1), ((0, 0), (0, LANES - FOX_HEADS)))
    gm64 = _group_mean_matrix(64)
    gm128 = _group_mean_matrix(128)
    tb_prep = min(256, T)
    ones = jnp.ones((tb_prep, tb_prep), F32)
    tril = jnp.tril(ones).astype(BF16)
    triu = jnp.triu(ones).astype(BF16)
    bucket = _t5_bucket_matrix()
    g_mix, g_mlp, g_mem = small["g_mix"], small["g_mlp"], small["g_mem"]
    b_gate = small["b_gate"]
    sinks = small["sink_swa"].reshape(-1)

    tl = min(1024, T)
    sq = pl.BlockSpec((tl, D_MODEL), lambda i, j, k: (i, j))
    h = _rmsnorm("rms_mix", x, g_mix, tm)
    (proj,) = _matmul(
        "mm_proj", h, wc, dims=NN, grid=(T // tl, PROJ_W // D_MODEL, 1),
        a_spec=pl.BlockSpec((tl, D_MODEL), lambda i, j, k: (i, 0)),
        b_spec=pl.BlockSpec((D_MODEL, D_MODEL), lambda i, j, k: (0, j)),
        acc_shape=(tl, D_MODEL),
        outs=[(jax.ShapeDtypeStruct((T, PROJ_W), F32), sq)],
        epilogue=_epi_store)
    qa, qf, kf, vf, qm, kad, vad, qf_aug, kf_aug = _prep_fwd(proj, gains, bfor, tril, gm64, gm128, T, tb_prep)
    bias = _swa_bias(small["rel_bias"], bucket)
    o_swa, lse_swa = _swa_fwd(sinks, qa, kad, vad, bias, T)
    o_fox, qf_aug_bwd = _fox_fwd(qf, qf_aug, kf, kf_aug, vf, T, tm)
    memn, kv, mk, mv = _mem_prep_fwd(mem, g_mem, w_kv, kn_mem, gm128)
    o_mem, lse_mem = _mem_fwd(qm, mk, mv, T, tm)
    o3 = (o_swa, o_fox, o_mem)
    merged = _merge_fwd(proj, b_gate, o3, w_o3, T, min(256, T))

    def epi_residual(acc, extra_refs, out_refs, ij):
        out_refs[0][...] = extra_refs[0][...] + acc

    row_full = pl.BlockSpec((tm, D_MODEL), lambda i, j, k: (i, 0))
    row_big = pl.BlockSpec((tl, D_MODEL), lambda i, j, k: (i, 0))
    whole = pl.BlockSpec((D_MODEL, D_MODEL), lambda i, j, k: (0, 0))
    (x2,) = _matmul(
        "mm_out", merged, w_out, dims=NN, grid=(T // tl, 1, 1),
        a_spec=row_big, b_spec=whole,
        acc_shape=(tl, D_MODEL), extra=[(x, row_big)],
        outs=[(jax.ShapeDtypeStruct((T, D_MODEL), F32), row_big)], epilogue=epi_residual)
    hm = _rmsnorm("rms_mlp", x2, g_mlp, tm)

    def epi_relu2(acc, extra_refs, out_refs, ij):
        out_refs[0][...] = acc
        r = jnp.maximum(acc, 0.0)
        out_refs[1][...] = (r * r).astype(BF16)

    up, u = _matmul(
        "mm_up", hm, w_up, dims=NN, grid=(T // tl, N_SHARD, 1),
        a_spec=row_big, b_spec=pl.BlockSpec((None, D_MODEL, D_MODEL), lambda i, j, k: (j, 0, 0)),
        acc_shape=(tl, D_MODEL),
        outs=[(jax.ShapeDtypeStruct((T, D_FF), F32), sq), (jax.ShapeDtypeStruct((T, D_FF), BF16), sq)],
        epilogue=epi_relu2)

    def epi_loss(acc, extra_refs, out_refs, ij):
        y = extra_refs[0][...] + acc
        err = y - extra_refs[1][...]
        out_refs[0][...] = err * (1.0 / D_MODEL)
        sq = jnp.sum(jnp.sum(err * err, axis=1, keepdims=True), axis=0, keepdims=True)

        @pl.when(ij[0] == 0)
        def _():
            out_refs[1][...] = jnp.zeros_like(out_refs[1])

        out_refs[1][...] += jnp.broadcast_to(sq, out_refs[1].shape)

    kblk = pl.BlockSpec((tl, D_MODEL), lambda i, j, k: (i, k))
    dy, loss_acc = _matmul(
        "mm_down", u, w_down, dims=NN, grid=(T // tl, 1, N_SHARD),
        a_spec=kblk, b_spec=pl.BlockSpec((D_MODEL, D_MODEL), lambda i, j, k: (k, 0)),
        acc_shape=(tl, D_MODEL), extra=[(x2, row_big), (tgt, row_big)],
        outs=[(jax.ShapeDtypeStruct((T, D_MODEL), F32), row_big),
              (jax.ShapeDtypeStruct((8, LANES), F32), pl.BlockSpec((8, LANES), lambda i, j, k: (0, 0)))],
        epilogue=epi_loss)
    loss = loss_acc[0, 0] * (0.5 / D_MODEL)

    def epi_dup(acc, extra_refs, out_refs, ij):
        out_refs[0][...] = (acc * (2.0 * jnp.maximum(extra_refs[0][...], 0.0))).astype(BF16)

    (dup,) = _matmul(
        "mm_dup", dy, w_down, dims=NT, grid=(T // tl, N_SHARD, 1),
        a_spec=row_big, b_spec=pl.BlockSpec((D_MODEL, D_MODEL), lambda i, j, k: (j, 0)),
        acc_shape=(tl, D_MODEL), extra=[(up, sq)],
        outs=[(jax.ShapeDtypeStruct((T, D_FF), BF16), sq)], epilogue=epi_dup)

    nkt = T // tl
    t_rows = pl.BlockSpec((tl, D_MODEL), lambda i, j, k: (k, i))
    t_cols = pl.BlockSpec((tl, D_MODEL), lambda i, j, k: (k, j))
    (d_w_down,) = _matmul(
        "mm_dw_down", u, dy, dims=TN, grid=(N_SHARD, 1, nkt),
        a_spec=t_rows, b_spec=t_cols, acc_shape=(D_MODEL, D_MODEL),
        outs=[(jax.ShapeDtypeStruct((D_FF, D_MODEL), F32), pl.BlockSpec((D_MODEL, D_MODEL), lambda i, j, k: (i, 0)))],
        epilogue=_epi_store)
    (d_w_up,) = _matmul(
        "mm_dw_up", hm, dup, dims=TN, grid=(1, N_SHARD, nkt),
        a_spec=t_rows, b_spec=t_cols, acc_shape=(D_MODEL, D_MODEL),
        outs=[(jax.ShapeDtypeStruct((N_SHARD, D_MODEL, D_MODEL), F32),
               pl.BlockSpec((None, D_MODEL, D_MODEL), lambda i, j, k: (j, 0, 0)))],
        epilogue=_epi_store)

    def epi_rms_bwd(acc, extra_refs, out_refs, ij):
        dx, dg = _rmsnorm_bwd_rows(acc, extra_refs[0][...], extra_refs[1][...])
        out_refs[0][...] = dx + extra_refs[2][...]

        @pl.when(ij[0] == 0)
        def _():
            out_refs[1][...] = jnp.zeros_like(out_refs[1])

        out_refs[1][...] += dg

    gain_spec = pl.BlockSpec((1, D_MODEL), lambda i, j, k: (0, 0))
    dx2, d_g_mlp = _matmul(
        "mm_dhm", dup, w_up, dims=NT, grid=(T // tl, 1, N_SHARD),
        a_spec=kblk, b_spec=pl.BlockSpec((None, D_MODEL, D_MODEL), lambda i, j, k: (k, 0, 0)),
        acc_shape=(tl, D_MODEL), extra=[(x2, row_big), (g_mlp, gain_spec), (dy, row_big)],
        outs=[(jax.ShapeDtypeStruct((T, D_MODEL), F32), row_big), (jax.ShapeDtypeStruct((1, D_MODEL), F32), gain_spec)],
        epilogue=epi_rms_bwd)

    (dmerged,) = _matmul(
        "mm_dmerged", dx2, w_out, dims=NT, grid=(T // tl, 1, 1),
        a_spec=row_big, b_spec=whole,
        acc_shape=(tl, D_MODEL), outs=[(jax.ShapeDtypeStruct((T, D_MODEL), F32), row_big)], epilogue=_epi_store)
    (d_w_out,) = _matmul(
        "mm_dw_out", merged, dx2, dims=TN, grid=(1, 1, nkt),
        a_spec=t_rows, b_spec=t_cols, acc_shape=(D_MODEL, D_MODEL),
        outs=[(jax.ShapeDtypeStruct((D_MODEL, D_MODEL), F32), whole)],
        epilogue=_epi_store)
    dmerged = reducer.early_start({"w_mlp_down": d_w_down, "w_mlp_up": d_w_up, "w_out": d_w_out}, dmerged)
    (dgl, do_swa, do_fox, do_mem, dl_swa, do_fox_aug, dl_mem, d_wo_swa, d_wo_fox, d_wo_mem, d_b_gate) = _merge_bwd(
        proj, b_gate, o3, w_o3, dmerged, T, min(256, T))
    do_fox = reducer.early_send(do_fox)

    dqa, dkad, dvad, dbias, dsk = _swa_bwd(sinks, qa, kad, vad, bias, do_swa, lse_swa, dl_swa, T)
    dqf, dqf_aug, dkf, dkf_aug, dvf = _fox_bwd(qf, qf_aug_bwd, kf, kf_aug, vf, do_fox, do_fox_aug, T, tm)
    dvf = reducer.early_finish(dvf)
    dqm, dmk, dmv = _mem_bwd(qm, mk, mv, do_mem, lse_mem, dl_mem, T, tm)
    d_w_kv, d_g_mem, d_kn_mem = _mem_prep_bwd(mem, g_mem, memn, kv, w_kv, kn_mem, gm128, dmk, dmv)
    d_rel = _swa_bias_bwd(dbias, bucket)
    dlo, gacc = _prep_bwd(proj, dqa, dkad, dvad, dqf, dkf, dvf, dqm, dqf_aug, dkf_aug, gains, bfor, triu, gm64, gm128,
                          T, tb_prep)

    def dwc_half(name, dpart):
        (res,) = _matmul(
            name, h, dpart, dims=TN, grid=(1, LO_W // D_MODEL, nkt),
            a_spec=t_rows, b_spec=t_cols, acc_shape=(D_MODEL, D_MODEL),
            outs=[(jax.ShapeDtypeStruct((D_MODEL, LO_W), F32), pl.BlockSpec((D_MODEL, D_MODEL), lambda i, j, k: (0, j)))],
            epilogue=_epi_store)
        return res

    d_wc_lo = dwc_half("mm_dwc_lo", dlo)
    d_wc_gl = dwc_half("mm_dwc_gl", dgl)
    dlo = reducer.late_start({"wc_lo": d_wc_lo, "wc_gl": d_wc_gl, "w_mem_kv": d_w_kv, "w_o_swa": d_wo_swa,
                              "w_o_fox": d_wo_fox, "w_o_mem": d_wo_mem}, dlo)
    (dh_lo,) = _matmul(
        "mm_dh_lo", dlo, wc, dims=NT, grid=(T // tl, 1, LO_W // D_MODEL),
        a_spec=kblk, b_spec=pl.BlockSpec((D_MODEL, D_MODEL), lambda i, j, k: (0, k)),
        acc_shape=(tl, D_MODEL), outs=[(jax.ShapeDtypeStruct((T, D_MODEL), F32), row_big)], epilogue=_epi_store)
    dh_lo = reducer.late_send(dh_lo)

    def epi_dx(acc, extra_refs, out_refs, ij):
        dhh = acc + extra_refs[3][...]
        dx, dg = _rmsnorm_bwd_rows(dhh, extra_refs[0][...], extra_refs[1][...])
        out_refs[0][...] = dx + extra_refs[2][...]

        @pl.when(ij[0] == 0)
        def _():
            out_refs[1][...] = jnp.zeros_like(out_refs[1])

        out_refs[1][...] += dg

    grad_x, d_g_mix = _matmul(
        "mm_dh_gl", dgl, wc, dims=NT, grid=(T // tm, 1, GATE_W // D_MODEL),
        a_spec=pl.BlockSpec((tm, D_MODEL), lambda i, j, k: (i, k)),
        b_spec=pl.BlockSpec((D_MODEL, D_MODEL), lambda i, j, k: (0, k + LO_W // D_MODEL)),
        acc_shape=(tm, D_MODEL), extra=[(x, row_full), (g_mix, gain_spec), (dx2, row_full), (dh_lo, row_full)],
        outs=[(jax.ShapeDtypeStruct((T, D_MODEL), F32), row_full), (jax.ShapeDtypeStruct((1, D_MODEL), F32), gain_spec)],
        epilogue=epi_dx)
    grad_x = reducer.late_finish(grad_x)

    fold64 = lambda row: (row[:64] + row[64:]).reshape(1, 64)
    grads = {
        "g_mix": d_g_mix, "b_gate": d_b_gate, "b_forget": gacc[5, :FOX_HEADS].reshape(1, FOX_HEADS),
        "qn_swa": fold64(gacc[0]), "kn_swa": fold64(gacc[1]),
        "sink_swa": -dsk[:, :SWA_GROUP, 0].reshape(1, SWA_HEADS), "rel_bias": d_rel[:, :SWA_HEADS],
        "qn_fox": fold64(gacc[2]), "kn_fox": fold64(gacc[3]),
        "g_mem": d_g_mem, "qn_mem": gacc[4].reshape(1, LANES), "kn_mem": d_kn_mem, "g_mlp": d_g_mlp,
    }
    return loss, grad_x, grads


MESH = pl.DeviceIdType.MESH
ANY = pl.BlockSpec(memory_space=pl.ANY)


def _place():
    x, y, c = lax.axis_index("x"), lax.axis_index("y"), lax.axis_index("c")
    chips = [(1 - x, y), (x, 1 - y), (1 - x, 1 - y)]
    return x, y, c, chips


def _all_gather_shards(slots):
    n = len(slots)

    def body(*refs):
        out = refs[n:2 * n]
        ici_send, ici_recv, d2d_send, d2d_recv = refs[2 * n:]
        x, y, c, chips = _place()
        sibling = (x, y, 1 - c)
        me = 2 * x + y

        def half(a, who):
            hr = slots[a].shape[1] // 2
            return pl.ds(pl.multiple_of(who * hr, hr), hr)

        def ici(a, j, slot, to):
            return pltpu.make_async_remote_copy(
                src_ref=out[a].at[me, half(a, c)], dst_ref=out[a].at[slot, half(a, c)],
                send_sem=ici_send.at[3 * a + j], recv_sem=ici_recv.at[3 * a + j], device_id=to, device_id_type=MESH)

        def d2d(a, j, slot, which):
            part = out[a].at[slot, half(a, which)]
            return pltpu.make_async_remote_copy(
                src_ref=part, dst_ref=part, send_sem=d2d_send.at[3 * a + j], recv_sem=d2d_recv.at[3 * a + j],
                device_id=sibling, device_id_type=MESH)

        sends = [ici(a, j, me, (*chip, c)) for a in range(n) for j, chip in enumerate(chips)]
        for cp in sends:
            cp.start()
        passed = []
        for a in range(n):
            for j, (px, py) in enumerate(chips):
                ici(a, j, 2 * px + py, (px, py, c)).wait_recv()
                cp = d2d(a, j, 2 * px + py, c)
                cp.start()
                passed.append(cp)
        for a in range(n):
            for j, (px, py) in enumerate(chips):
                d2d(a, j, 2 * px + py, 1 - c).wait_recv()
        for cp in sends + passed:
            cp.wait_send()

    return pl.pallas_call(
        body, name="all_gather_weights",
        in_specs=[ANY] * n, out_specs=[ANY] * n,
        out_shape=[jax.ShapeDtypeStruct(s.shape, s.dtype) for s in slots],
        input_output_aliases={a: a for a in range(n)},
        scratch_shapes=[pltpu.SemaphoreType.DMA((3 * n,))] * 4,
    )(*slots)


def _handshake(peers):
    barrier = pltpu.get_barrier_semaphore()
    for peer in peers:
        pl.semaphore_signal(barrier, inc=1, device_id=peer, device_id_type=MESH)
    pl.semaphore_wait(barrier, len(peers))


def _all_gather_shards_async(slots):
    n = len(slots)
    bufs = [jax.new_ref(s, memory_space=pltpu.MemorySpace.HBM) for s in slots]

    def body(ici_send, ici_recv, d2d_send, d2d_recv):
        x, y, c, chips = _place()
        sibling = (x, y, 1 - c)
        me = 2 * x + y
        _handshake([(px, py, c) for px, py in chips] + [sibling])

        def half(a, who):
            hr = slots[a].shape[1] // 2
            return pl.ds(pl.multiple_of(who * hr, hr), hr)

        def ici(a, j, slot, to):
            return pltpu.make_async_remote_copy(
                src_ref=bufs[a].at[me, half(a, c)], dst_ref=bufs[a].at[slot, half(a, c)],
                send_sem=ici_send.at[3 * a + j], recv_sem=ici_recv.at[3 * a + j], device_id=to, device_id_type=MESH)

        def d2d(a, j, slot, which):
            part = bufs[a].at[slot, half(a, which)]
            return pltpu.make_async_remote_copy(
                src_ref=part, dst_ref=part, send_sem=d2d_send.at[3 * a + j], recv_sem=d2d_recv.at[3 * a + j],
                device_id=sibling, device_id_type=MESH)

        sends = [ici(a, j, me, (*chip, c)) for a in range(n) for j, chip in enumerate(chips)]
        for cp in sends:
            cp.start()
        passed = []
        for a in range(n):
            for j, (px, py) in enumerate(chips):
                ici(a, j, 2 * px + py, (px, py, c)).wait_recv()
                cp = d2d(a, j, 2 * px + py, c)
                cp.start()
                passed.append(cp)
        for a in range(n):
            for j, (px, py) in enumerate(chips):
                d2d(a, j, 2 * px + py, 1 - c).wait_recv()
        for cp in sends + passed:
            cp.wait_send()

    pl.kernel(
        body, mesh=plsc.ScalarSubcoreMesh(axis_name="seq", num_cores=1), name="all_gather_weights_async",
        scratch_types=[pltpu.SemaphoreType.DMA((3 * n,))] * 4,
        compiler_params=pltpu.CompilerParams(collective_id=1),
    )()
    return [b[...] for b in bufs]


def _sequencer_call(name, collective_id, n_sems, body):
    pl.kernel(
        body, mesh=plsc.ScalarSubcoreMesh(axis_name="seq", num_cores=1), name=name,
        scratch_types=[pltpu.SemaphoreType.DMA((n_sems,))] * 2,
        compiler_params=pltpu.CompilerParams(collective_id=collective_id),
    )()


def _hbm_ref(value):
    return jax.new_ref(value, memory_space=pltpu.MemorySpace.HBM)


def _pair_exchange(name, collective_id, gs):
    n = len(gs)
    src = [_hbm_ref(g) for g in gs]
    stage = [jax.empty_ref(jax.ShapeDtypeStruct((N_SHARD, g.shape[1] // 2, g.shape[2]), g.dtype),
                           memory_space=pltpu.MemorySpace.HBM) for g in gs]

    def body(send_sem, recv_sem):
        x, y, c, _ = _place()
        sibling = (x, y, 1 - c)
        _handshake([sibling])
        copies = []
        for a in range(n):
            hr = gs[a].shape[1] // 2
            theirs = pl.ds(pl.multiple_of((1 - c) * hr, hr), hr)
            copies.append(pltpu.make_async_remote_copy(
                src_ref=src[a].at[:, theirs, :], dst_ref=stage[a], send_sem=send_sem.at[a], recv_sem=recv_sem.at[a],
                device_id=sibling, device_id_type=MESH))
        for cp in copies:
            cp.start()
        for cp in copies:
            cp.wait()

    _sequencer_call(name, collective_id, n, body)
    return [s[...] for s in stage]


def _chip_exchange(name, collective_id, sums):
    n = len(sums)
    src = [_hbm_ref(s) for s in sums]
    got = [jax.empty_ref(jax.ShapeDtypeStruct((3,) + s.shape[1:], s.dtype), memory_space=pltpu.MemorySpace.HBM)
           for s in sums]

    def body(send_sem, recv_sem):
        x, y, c, chips = _place()
        _handshake([(px, py, c) for px, py in chips])
        copies = []
        for a in range(n):
            for j, (px, py) in enumerate(chips):
                copies.append(pltpu.make_async_remote_copy(
                    src_ref=src[a].at[2 * px + py], dst_ref=got[a].at[j],
                    send_sem=send_sem.at[3 * a + j], recv_sem=recv_sem.at[3 * a + j],
                    device_id=(px, py, c), device_id_type=MESH))
        for cp in copies:
            cp.start()
        for cp in copies:
            cp.wait()

    _sequencer_call(name, collective_id, 3 * n, body)
    return [g[...] for g in got]


def _pair_gather(name, collective_id, fulls):
    n = len(fulls)
    full = [_hbm_ref(f) for f in fulls]

    def body(send_sem, recv_sem):
        x, y, c, _ = _place()
        sibling = (x, y, 1 - c)
        _handshake([sibling])
        copies = []
        for a in range(n):
            hr = fulls[a].shape[0] // 2
            mine = full[a].at[pl.ds(pl.multiple_of(c * hr, hr), hr)]
            copies.append(pltpu.make_async_remote_copy(
                src_ref=mine, dst_ref=mine, send_sem=send_sem.at[a], recv_sem=recv_sem.at[a],
                device_id=sibling, device_id_type=MESH))
        for cp in copies:
            cp.start()
        for cp in copies:
            cp.wait()

    _sequencer_call(name, collective_id, n, body)
    return [f[...] for f in full]


ELEMENTWISE_BLOCK_ELEMS = 256 * 1024


def _row_block(rows, cols):
    rb = 8
    while rb * 2 * cols <= ELEMENTWISE_BLOCK_ELEMS and rb * 2 <= rows:
        rb *= 2
    return rb


def _pair_sum(name, place, g, stage):
    _, R, C = g.shape
    hr = R // 2
    rb = _row_block(hr, C)
    nb = hr // rb

    def body(place_ref, g_ref, st_ref, sum_bf, own_f32):
        s = pl.program_id(1)
        tot = g_ref[...] + st_ref[...]
        sum_bf[...] = tot.astype(BF16)

        @pl.when(s == place_ref[0])
        def _():
            own_f32[...] = tot

    return pl.pallas_call(
        body, name=name,
        grid_spec=pltpu.PrefetchScalarGridSpec(
            num_scalar_prefetch=1, grid=(nb, N_SHARD),
            in_specs=[pl.BlockSpec((None, rb, C), lambda i, s, pr: (s, pr[1] * nb + i, 0)),
                      pl.BlockSpec((None, rb, C), lambda i, s, pr: (s, i, 0))],
            out_specs=[pl.BlockSpec((None, rb, C), lambda i, s, pr: (s, i, 0)),
                       pl.BlockSpec((rb, C), lambda i, s, pr: (i, 0))]),
        out_shape=[jax.ShapeDtypeStruct((N_SHARD, hr, C), BF16), jax.ShapeDtypeStruct((hr, C), F32)],
        compiler_params=_cparams("arbitrary", "arbitrary"),
    )(place, g, stage)


def _final_sum(name, place, own, got):
    hr, C = own.shape
    rb = _row_block(hr, C)
    nb = hr // rb

    def body(place_ref, own_ref, got_ref, o_ref):
        o_ref[...] = ((own_ref[...] + got_ref[0].astype(F32)) + got_ref[1].astype(F32)) + got_ref[2].astype(F32)

    return pl.pallas_call(
        body, name=name,
        grid_spec=pltpu.PrefetchScalarGridSpec(
            num_scalar_prefetch=1, grid=(nb,),
            in_specs=[pl.BlockSpec((rb, C), lambda i, pr: (i, 0)), pl.BlockSpec((3, rb, C), lambda i, pr: (0, i, 0))],
            out_specs=pl.BlockSpec((rb, C), lambda i, pr: (pr[1] * nb + i, 0))),
        out_shape=jax.ShapeDtypeStruct((2 * hr, C), F32),
        compiler_params=_cparams("arbitrary"),
    )(place, own, got)


def _adamw_math(w, g, m, v):
    m = ADAM_B1 * m + (1.0 - ADAM_B1) * g
    v = ADAM_B2 * v + (1.0 - ADAM_B2) * (g * g)
    m_hat = m / (1.0 - ADAM_B1 ** ADAM_STEP)
    v_hat = v / (1.0 - ADAM_B2 ** ADAM_STEP)
    delta = -ADAM_LR * (m_hat / (jnp.sqrt(v_hat) + ADAM_EPS) + ADAM_WD * w)
    return delta, m, v


def _adamw(name, w, g, m, v):
    R, Cw = w.shape
    Cg = g.shape[1]
    rb = _row_block(R, Cg)

    def body(w_ref, g_ref, m_ref, v_ref, g_o, d_o, m_o, v_o):
        gv = g_ref[...]
        delta, mn, vn = _adamw_math(w_ref[...], gv, m_ref[...], v_ref[...])
        g_o[...] = gv
        d_o[...] = delta
        m_o[...] = mn
        v_o[...] = vn

    blk = pl.BlockSpec((rb, Cg), lambda i: (i, 0))
    return pl.pallas_call(
        body, name=name, grid=(R // rb,),
        in_specs=[blk] * 4, out_specs=[blk] * 4,
        out_shape=[jax.ShapeDtypeStruct((R, Cw), F32)] * 4,
        compiler_params=_cparams("parallel"),
    )(w, g, m, v)


N_DEV = 8
SMALL_ROWS = 64


def _small_allreduce_adamw(g, w, m, v):
    def body(g_ref, w_ref, m_ref, v_ref, all_ref, gs_o, d_o, m_o, v_o, send_sems, recv_sems, local_sem):
        x, y, c, chips = _place()
        me, sibling = (x, y, c), (x, y, 1 - c)

        def rows(px, py, pc):
            return all_ref.at[pl.ds(pl.multiple_of((4 * px + 2 * py + pc) * SMALL_ROWS, SMALL_ROWS), SMALL_ROWS), :]

        def copy(k, block, to, src=None):
            return pltpu.make_async_remote_copy(
                src_ref=rows(*block) if src is None else src, dst_ref=rows(*block),
                send_sem=send_sems.at[k], recv_sem=recv_sems.at[k], device_id=to, device_id_type=MESH)

        mine = pltpu.make_async_copy(g_ref, rows(*me), local_sem)
        mine.start()
        first = [copy(0, me, sibling, src=g_ref)]
        first += [copy(1 + j, me, (*chip, c), src=g_ref) for j, chip in enumerate(chips)]
        for cp in first:
            cp.start()
        passed = [copy(4 + j, (*chip, c), sibling) for j, chip in enumerate(chips)]
        for j, chip in enumerate(chips):
            copy(1 + j, (*chip, c), me).wait_recv()
            passed[j].start()
        copy(0, sibling, me).wait_recv()
        for j, chip in enumerate(chips):
            copy(4 + j, (*chip, 1 - c), me).wait_recv()
        for cp in first + passed:
            cp.wait_send()
        mine.wait()

        tot = all_ref[0:SMALL_ROWS, :]
        for d in range(1, N_DEV):
            tot = tot + all_ref[d * SMALL_ROWS:(d + 1) * SMALL_ROWS, :]
        delta, mn, vn = _adamw_math(w_ref[...], tot, m_ref[...], v_ref[...])
        gs_o[...] = tot
        d_o[...] = delta
        m_o[...] = mn
        v_o[...] = vn

    vm = pl.BlockSpec(memory_space=pltpu.VMEM)
    shp = jax.ShapeDtypeStruct((SMALL_ROWS, LANES), F32)
    res = pl.pallas_call(
        body, name="small_allreduce_adamw", in_specs=[vm] * 4, out_specs=[vm] * 5,
        out_shape=[jax.ShapeDtypeStruct((N_DEV * SMALL_ROWS, LANES), F32), shp, shp, shp, shp],
        scratch_shapes=[pltpu.SemaphoreType.DMA((7,)), pltpu.SemaphoreType.DMA((7,)), pltpu.SemaphoreType.DMA],
    )(g, w, m, v)
    return res[1:]


SMALL_NAMES = ("g_mix", "b_gate", "b_forget", "qn_swa", "kn_swa", "sink_swa", "rel_bias", "qn_fox", "kn_fox",
               "g_mem", "qn_mem", "kn_mem", "g_mlp")
BIG_NAMES = ("w_in", "w_mem_kv", "w_o_swa", "w_o_fox", "w_o_mem", "w_out", "w_mlp_up", "w_mlp_down")
WEIGHT_NAMES = ("g_mix", "w_in", "b_gate", "b_forget", "qn_swa", "kn_swa", "sink_swa", "rel_bias", "qn_fox", "kn_fox",
                "g_mem", "w_mem_kv", "qn_mem", "kn_mem", "w_o_swa", "w_o_fox", "w_o_mem", "w_out", "g_mlp",
                "w_mlp_up", "w_mlp_down")


def _pack_small(parts, extra=None):
    rows = []
    for n in SMALL_NAMES:
        flat = parts[n].reshape(-1).astype(F32)
        flat = jnp.pad(flat, (0, (-flat.size) % LANES))
        rows.append(flat.reshape(-1, LANES))
    if extra is not None:
        rows.append(jnp.pad(extra.reshape(1, 1), ((0, 0), (0, LANES - 1))))
    packed = jnp.concatenate(rows, axis=0)
    return jnp.pad(packed, ((0, SMALL_ROWS - packed.shape[0]), (0, 0)))


def _unpack_small(packed, shapes):
    out, r = {}, 0
    for n in SMALL_NAMES:
        size = math.prod(shapes[n])
        nr = -(-size // LANES)
        out[n] = packed[r:r + nr].reshape(-1)[:size].reshape(shapes[n])
        r += nr
    return out, packed[r, 0]


W_IN_SEGMENTS = ((C_QA, 0, 512), (C_QF, 768, 512), (C_KF, 1280, 512), (C_VF, 1792, 512), (C_QM, 2312, 512),
                 (C_KA, 512, 128), (C_VA, 640, 128), (C_FL, 2304, FOX_HEADS), (C_GL, 2824, GATE_W))
RELAYOUT_ROWS = 256


def _permute_pieces(src_of_dst):
    blocks = []
    for b in range(len(src_of_dst) // LANES):
        runs, lane = [], 0
        while lane < LANES:
            src = src_of_dst[b * LANES + lane]
            if src is None:
                lane += 1
                continue
            plane, col = src
            end = lane + 1
            while (end < LANES and src_of_dst[b * LANES + end] == (plane, col + end - lane)
                   and (col + end - lane) // LANES == col // LANES):
                end += 1
            runs.append((plane, col // LANES, (lane - col) % LANES, lane, end))
            lane = end
        blocks.append(runs)
    return blocks


def _permuted_block(runs, load, rows):
    lane = _lane((rows, LANES))
    acc = jnp.zeros((rows, LANES), F32)
    for plane, blk, shift, lo, hi in runs:
        x = load(plane, blk).astype(F32)
        if shift:
            x = pltpu.roll(x, shift, 1)
        acc = x if (lo, hi) == (0, LANES) else jnp.where((lane >= lo) & (lane < hi), x, acc)
    return acc


def _w_in_to_segments(g_in):
    src_of_dst = [None] * PROJ_W
    for mine, theirs, width in W_IN_SEGMENTS:
        for k in range(width):
            src_of_dst[mine + k] = ((theirs + k) // IN_SHARD, (theirs + k) % IN_SHARD)
    blocks = _permute_pieces(src_of_dst)
    rb = RELAYOUT_ROWS

    def body(src_ref, out_ref):
        for b, runs in enumerate(blocks):
            blk = _permuted_block(runs, lambda p, c: src_ref[p, :, c * LANES:(c + 1) * LANES], rb)
            out_ref[:, b * LANES:(b + 1) * LANES] = blk.astype(out_ref.dtype)

    return pl.pallas_call(
        body, name="w_in_to_segments", grid=(D_MODEL // rb,),
        in_specs=[pl.BlockSpec((N_SHARD, rb, IN_SHARD_PAD), lambda i: (0, i, 0))],
        out_specs=pl.BlockSpec((rb, PROJ_W), lambda i: (i, 0)),
        out_shape=jax.ShapeDtypeStruct((D_MODEL, PROJ_W), g_in.dtype),
        compiler_params=_cparams("parallel"),
    )(g_in)


def _w_in_from_segments(lo, gl):
    mine_of_theirs = {}
    for mine, theirs, width in W_IN_SEGMENTS:
        for k in range(width):
            mine_of_theirs[theirs + k] = mine + k
    src_of_dst = [None] * (N_SHARD * IN_SHARD_PAD)
    for s in range(N_SHARD):
        for l in range(IN_SHARD):
            j = mine_of_theirs[s * IN_SHARD + l]
            src_of_dst[s * IN_SHARD_PAD + l] = (j // LO_W, j % LO_W)
    blocks = _permute_pieces(src_of_dst)
    per_slot = IN_SHARD_PAD // LANES
    rb = RELAYOUT_ROWS

    def body(lo_ref, gl_ref, out_ref):
        planes = (lo_ref, gl_ref)
        for b, runs in enumerate(blocks):
            blk = _permuted_block(runs, lambda p, c: planes[p][:, c * LANES:(c + 1) * LANES], rb)
            c0 = (b % per_slot) * LANES
            out_ref[b // per_slot, :, c0:c0 + LANES] = blk

    half = pl.BlockSpec((rb, LO_W), lambda i: (i, 0))
    return pl.pallas_call(
        body, name="w_in_from_segments", grid=(D_MODEL // rb,),
        in_specs=[half, half],
        out_specs=pl.BlockSpec((N_SHARD, rb, IN_SHARD_PAD), lambda i: (0, i, 0)),
        out_shape=jax.ShapeDtypeStruct((N_SHARD, D_MODEL, IN_SHARD_PAD), F32),
        compiler_params=_cparams("parallel"),
    )(lo, gl)


def _after(first, then):
    return lax.optimization_barrier((first, then))


class _ReduceGroup:
    def __init__(self, tag, first_collective_id, place):
        self.tag, self.first_id, self.place = tag, first_collective_id, place

    def start(self, local, tie):
        self.names = tuple(local)
        mine, tie = _after([local[n] for n in self.names], tie)
        self.mine = mine
        self.staged = _pair_exchange("pair_exchange_" + self.tag, self.first_id, mine)
        return tie

    def send(self, tie):
        staged, tie = _after(self.staged, tie)
        sums = [_pair_sum("pair_sum_" + n, self.place, g, st) for n, g, st in zip(self.names, self.mine, staged)]
        travel, tie = _after([s[0] for s in sums], tie)
        self.own = [s[1] for s in sums]
        self.got = _chip_exchange("chip_exchange_" + self.tag, self.first_id + 1, travel)
        return tie

    def finish(self, tie):
        got, tie = _after(self.got, tie)
        halves = [_final_sum("final_sum_" + n, self.place, o, r) for n, o, r in zip(self.names, self.own, got)]
        halves, tie = _after(halves, tie)
        summed = _pair_gather("pair_gather_" + self.tag, self.first_id + 2, halves)
        self.summed = dict(zip(self.names, summed))
        return tie


class _GradReducer:
    def __init__(self, place):
        self.early = _ReduceGroup("early", 2, place)
        self.late = _ReduceGroup("late", 5, place)

    @staticmethod
    def _slot_rows(a):
        return a.reshape(N_SHARD, a.shape[0] // N_SHARD, a.shape[1])

    def early_start(self, g, tie):
        return self.early.start({"w_mlp_down": self._slot_rows(g["w_mlp_down"]), "w_mlp_up": g["w_mlp_up"],
                                 "w_out": self._slot_rows(g["w_out"])}, tie)

    def early_send(self, tie):
        return self.early.send(tie)

    def early_finish(self, tie):
        return self.early.finish(tie)

    def late_start(self, g, tie):
        d_in = _w_in_from_segments(g["wc_lo"], g["wc_gl"])
        return self.late.start({"w_in": d_in, "w_mem_kv": self._slot_rows(g["w_mem_kv"]), "w_o_swa": g["w_o_swa"],
                                "w_o_fox": g["w_o_fox"], "w_o_mem": g["w_o_mem"]}, tie)

    def late_send(self, tie):
        return self.late.send(tie)

    def late_finish(self, tie):
        return self.late.finish(tie)

    @property
    def summed(self):
        return {**self.early.summed, **self.late.summed}


def kernel(x, mem, g_mix, w_in, b_gate, b_forget, qn_swa, kn_swa, sink_swa, rel_bias, qn_fox, kn_fox, g_mem, w_mem_kv, qn_mem, kn_mem, w_o_swa, w_o_fox, w_o_mem, w_out, g_mlp, w_mlp_up, w_mlp_down, loss_target, m_g_mix, m_w_in, m_b_gate, m_b_forget, m_qn_swa, m_kn_swa, m_sink_swa, m_rel_bias, m_qn_fox, m_kn_fox, m_g_mem, m_w_mem_kv, m_qn_mem, m_kn_mem, m_w_o_swa, m_w_o_fox, m_w_o_mem, m_w_out, m_g_mlp, m_w_mlp_up, m_w_mlp_down, v_g_mix, v_w_in, v_b_gate, v_b_forget, v_qn_swa, v_kn_swa, v_sink_swa, v_rel_bias, v_qn_fox, v_kn_fox, v_g_mem, v_w_mem_kv, v_qn_mem, v_kn_mem, v_w_o_swa, v_w_o_fox, v_w_o_mem, v_w_out, v_g_mlp, v_w_mlp_up, v_w_mlp_down):
    given = dict(locals())
    W = {n: given[n] for n in WEIGHT_NAMES}
    M = {n: given["m_" + n] for n in WEIGHT_NAMES}
    V = {n: given["v_" + n] for n in WEIGHT_NAMES}
    pad_in = ((0, 0), (0, IN_SHARD_PAD - IN_SHARD))

    shards = [jnp.pad(w_in[0].astype(BF16), pad_in)] + [W[n][0].astype(BF16) for n in BIG_NAMES[1:]]
    slots = [jnp.broadcast_to(s[None], (N_SHARD,) + s.shape) for s in shards]
    (g_in,) = _all_gather_shards(slots[:1])
    g_in, late = lax.optimization_barrier((g_in, slots[1:]))
    g_kv, g_oa, g_of, g_om, g_out, g_up, g_down = _all_gather_shards_async(late)
    wc = _w_in_to_segments(g_in)
    small = {n: (W[n] if n == "rel_bias" else W[n].reshape(1, -1)) for n in SMALL_NAMES}

    place = jnp.stack([2 * lax.axis_index("x") + lax.axis_index("y"), lax.axis_index("c")]).astype(jnp.int32)
    reducer = _GradReducer(place)
    loss, grad_x, grads = _local_step(
        x[0], mem[0], loss_target[0], small, wc, g_kv.reshape(D_MODEL, D_MODEL), (g_oa, g_of, g_om),
        g_out.reshape(D_MODEL, D_MODEL), g_up, g_down.reshape(D_FF, D_MODEL), reducer)

    out = {}
    for n in BIG_NAMES:
        res = _adamw("adamw_" + n, W[n][0], reducer.summed[n], M[n][0], V[n][0])
        out[n] = [r.reshape(W[n].shape) for r in res]
    shapes = {n: W[n].shape for n in SMALL_NAMES}
    packed = _small_allreduce_adamw(_pack_small(grads, loss), _pack_small(W), _pack_small(M), _pack_small(V))
    unpacked = [_unpack_small(p, shapes) for p in packed]
    for n in SMALL_NAMES:
        out[n] = [u[0][n] for u in unpacked]
    loss_total = unpacked[0][1]

    return (loss_total, grad_x.reshape(x.shape),
            *[out[n][0] for n in WEIGHT_NAMES], *[out[n][1] for n in WEIGHT_NAMES],
            *[out[n][2] for n in WEIGHT_NAMES], *[out[n][3] for n in WEIGHT_NAMES])
```

```python
import functools
import math

import jax
import jax.numpy as jnp
from jax import lax
from jax.experimental import pallas as pl
from jax.experimental.pallas import tpu as pltpu
from jax.experimental.pallas import tpu_sc as plsc

F32 = jnp.float32
BF16 = jnp.bfloat16

D_MODEL = 1024
N_MEM = 256
SWA_HEADS = 8
SWA_KV_HEADS = 2
SWA_HEAD_DIM = 64
WINDOW = 128
FOX_HEADS = 8
FOX_HEAD_DIM = 64
MEM_HEADS = 4
MEM_HEAD_DIM = 128
D_FF = 4 * D_MODEL
REL_BUCKETS = 32
REL_MAX_DIST = 128
EPS = 1e-6
NEG = -1e30
GATE_W = 3 * D_MODEL
IN_WIDTH = 5896
N_SHARD = 4
IN_SHARD = IN_WIDTH // N_SHARD
IN_SHARD_PAD = 1536

ADAM_LR = 0.001
ADAM_B1 = 0.9
ADAM_B2 = 0.999
ADAM_EPS = 1e-08
ADAM_WD = 0.01
ADAM_STEP = 10

LANES = 128
V7X_VMEM_BYTES = 64 * 1024 * 1024
VMEM_LIMIT = V7X_VMEM_BYTES * 3 // 4

C_QA, C_QF, C_KF, C_VF, C_QM, C_KA, C_VA, C_FL, C_GL = 0, 512, 1024, 1536, 2048, 2560, 2688, 2816, 3072
LO_W = 3072
PROJ_W = 6144

NN = (((1,), (0,)), ((), ()))
NT = (((1,), (1,)), ((), ()))
TN = (((0,), (0,)), ((), ()))


def _dot(a, b, dims=NN):
    return lax.dot_general(a, b, dims, preferred_element_type=F32)


def _cparams(*sem):
    return pltpu.CompilerParams(dimension_semantics=sem, vmem_limit_bytes=VMEM_LIMIT)


def _split3(a):
    hi = a.astype(BF16)
    r1 = a - hi.astype(F32)
    mid = r1.astype(BF16)
    lo = (r1 - mid.astype(F32)).astype(BF16)
    return hi, mid, lo


def _dot3_right(a, g):
    hi, mid, lo = _split3(a)
    return _dot(hi, g) + _dot(mid, g) + _dot(lo, g)


def _dot3_left(g, a):
    hi, mid, lo = _split3(a)
    return _dot(g, hi) + _dot(g, mid) + _dot(g, lo)


def _group_mean_matrix(d):
    r = jnp.arange(LANES)
    return jnp.where((r[:, None] // d) == (r[None, :] // d), 1.0 / d, 0.0).astype(BF16)


def _lane(shape):
    return lax.broadcasted_iota(jnp.int32, shape, len(shape) - 1)


def _matmul(name, a, b, *, dims, grid, a_spec, b_spec, acc_shape, outs, epilogue, extra=()):
    nk = grid[2]
    n_extra = len(extra)

    def body(a_ref, b_ref, *rest):
        extra_refs = rest[:n_extra]
        out_refs = rest[n_extra:n_extra + len(outs)]
        i, j, k = pl.program_id(0), pl.program_id(1), pl.program_id(2)
        part = _dot(a_ref[...].astype(BF16), b_ref[...].astype(BF16), dims)
        if nk == 1:
            epilogue(part, extra_refs, out_refs, (i, j))
            return
        acc_ref = rest[-1]

        @pl.when(k == 0)
        def _():
            acc_ref[...] = part

        @pl.when((k > 0) & (k < nk - 1))
        def _():
            acc_ref[...] += part

        @pl.when(k == nk - 1)
        def _():
            epilogue(acc_ref[...] + part, extra_refs, out_refs, (i, j))

    res = pl.pallas_call(
        body,
        name=name,
        grid=grid,
        in_specs=[a_spec, b_spec] + [s for _, s in extra],
        out_specs=[s for _, s in outs],
        out_shape=[s for s, _ in outs],
        scratch_shapes=[pltpu.VMEM(acc_shape, F32)] if nk > 1 else [],
        compiler_params=_cparams("arbitrary", "arbitrary", "arbitrary"),
    )(a, b, *[x for x, _ in extra])
    return res


def _epi_store(acc, extra_refs, out_refs, ij):
    out_refs[0][...] = acc.astype(out_refs[0].dtype)


def _rms_rows(x, g):
    r = lax.rsqrt(jnp.mean(x * x, axis=-1, keepdims=True) + EPS)
    return x * r, r


def _rmsnorm_bwd_rows(dh, x, g):
    xhat, r = _rms_rows(x, g)
    dxh = dh * g
    dx = r * (dxh - xhat * jnp.mean(dxh * xhat, axis=-1, keepdims=True))
    return dx, jnp.sum(dh * xhat, axis=0, keepdims=True)


def _rmsnorm(name, x, g, tb):
    T, Dm = x.shape

    def body(x_ref, g_ref, o_ref):
        xhat, _ = _rms_rows(x_ref[...], None)
        o_ref[...] = (xhat * g_ref[...]).astype(o_ref.dtype)

    return pl.pallas_call(
        body, name=name, grid=(T // tb,),
        in_specs=[pl.BlockSpec((tb, Dm), lambda i: (i, 0)), pl.BlockSpec((1, Dm), lambda i: (0, 0))],
        out_specs=pl.BlockSpec((tb, Dm), lambda i: (i, 0)),
        out_shape=jax.ShapeDtypeStruct((T, Dm), BF16),
        compiler_params=_cparams("parallel"),
    )(x, g)


def _head_norm(x, gm, gain):
    ms = _dot3_right(x * x, gm)
    r = lax.rsqrt(ms + EPS)
    return x * r * gain, x * r


def _head_norm_bwd(dy, x, gm, gain):
    ms = _dot3_right(x * x, gm)
    r = lax.rsqrt(ms + EPS)
    xhat = x * r
    dxh = dy * gain
    dx = r * (dxh - xhat * _dot3_right(dxh * xhat, gm))
    return dx, jnp.sum(dy * xhat, axis=0, keepdims=True)


def _log_sigmoid(z):
    return jnp.minimum(z, 0.0) - jnp.log(1.0 + jnp.exp(-jnp.abs(z)))


def _prep_fwd(proj, gains, bfor, tril, gm64, gm128, T, tb):
    nb = T // tb

    def body(qa_ref, qf_ref, kf_ref, vf_ref, qm_ref, ka_ref, va_ref, fl_ref, gains_ref, bfor_ref, tril_ref,
             gm64_ref, gm128_ref,
             qa_o, qf_o, kf_o, vf_o, qm_o, kad_o, vad_o, qaug_o, kaug_o, carry):
        i = pl.program_id(0)
        gm64v = gm64_ref[...]
        gm128v = gm128_ref[...]
        lane = _lane((tb, LANES))

        def norm512(src, dst, row, gm, scale=1.0):
            gain = gains_ref[row:row + 1, :]
            for c in range(4):
                sl = slice(c * LANES, (c + 1) * LANES)
                y, _ = _head_norm(src[:, sl], gm, gain)
                dst[:, sl] = (y * scale).astype(dst.dtype)

        norm512(qa_ref, qa_o, 0, gm64v)
        norm512(qf_ref, qf_o, 2, gm64v, FOX_SCALE)
        norm512(kf_ref, kf_o, 3, gm64v)
        norm512(qm_ref, qm_o, 4, gm128v)
        vf_o[...] = vf_ref[...].astype(vf_o.dtype)

        ka_n, _ = _head_norm(ka_ref[...], gm64v, gains_ref[1:2, :])
        ka_r = pltpu.roll(ka_n, 64, 1)
        va = va_ref[...]
        va_r = pltpu.roll(va, 64, 1)
        lo = lane < 64
        kad_o[0] = jnp.where(lo, ka_n, ka_r).astype(kad_o.dtype)
        kad_o[1] = jnp.where(lo, ka_r, ka_n).astype(kad_o.dtype)
        vad_o[0] = jnp.where(lo, va, va_r).astype(vad_o.dtype)
        vad_o[1] = jnp.where(lo, va_r, va).astype(vad_o.dtype)

        @pl.when(i == 0)
        def _():
            carry[...] = jnp.zeros_like(carry)

        logf = jnp.where(lane < FOX_HEADS, _log_sigmoid(fl_ref[...] + bfor_ref[...]), 0.0)
        c = _dot3_left(tril_ref[...], logf) + carry[0:1, :]
        carry[...] = jnp.broadcast_to(c[tb - 1:tb, :], carry.shape)
        for pair in range(FOX_HEADS // 2):
            qaug = jnp.zeros((tb, LANES), F32)
            kaug = jnp.zeros((tb, LANES), F32)
            for sub in range(2):
                col = jnp.sum(jnp.where(lane == 2 * pair + sub, c, 0.0), axis=1, keepdims=True)
                pieces = [p.astype(F32) for p in _split3(col)]
                base = AUG_STRIDE * sub
                for e in range(3):
                    qaug = jnp.where(lane == base + AUG_C + e, pieces[e], qaug)
                    kaug = jnp.where(lane == base + AUG_NEG_C + e, -pieces[e], kaug)
                qaug = jnp.where((lane >= base + AUG_NEG_C) & (lane < base + AUG_NEG_C + 3), 1.0, qaug)
                ones_k = ((lane >= base + AUG_C) & (lane < base + AUG_C + 3)) | (
                    (lane >= base + AUG_STAT) & (lane < base + AUG_STAT + 3))
                kaug = jnp.where(ones_k, 1.0, kaug)
            sl = slice(pair * LANES, (pair + 1) * LANES)
            qaug_o[:, sl] = qaug.astype(BF16)
            kaug_o[:, sl] = kaug.astype(BF16)

    def seg(width, start):
        return pl.BlockSpec((tb, width), lambda i, s=start // width: (i, s))

    const = lambda shape: pl.BlockSpec(shape, lambda i: tuple(0 for _ in shape))
    rows512 = pl.BlockSpec((tb, 512), lambda i: (i, 0))
    outs = pl.pallas_call(
        body, name="prep_fwd", grid=(nb,),
        in_specs=[seg(512, C_QA), seg(512, C_QF), seg(512, C_KF), seg(512, C_VF), seg(512, C_QM),
                  seg(128, C_KA), seg(128, C_VA), seg(128, C_FL),
                  const((8, LANES)), const((1, LANES)), const((tb, tb)), const((LANES, LANES)), const((LANES, LANES))],
        out_specs=[rows512, rows512, rows512, rows512, rows512,
                   pl.BlockSpec((2, tb, LANES), lambda i: (0, i, 0)), pl.BlockSpec((2, tb, LANES), lambda i: (0, i, 0)),
                   rows512, rows512],
        out_shape=[jax.ShapeDtypeStruct((T, 512), BF16)] * 5
        + [jax.ShapeDtypeStruct((2, T, LANES), BF16)] * 2
        + [jax.ShapeDtypeStruct((T, 512), BF16)] * 2,
        scratch_shapes=[pltpu.VMEM((8, LANES), F32)],
        compiler_params=_cparams("arbitrary"),
    )(proj, proj, proj, proj, proj, proj, proj, proj, gains, bfor, tril, gm64, gm128)
    return outs


def _prep_bwd(proj, dqa, dkad, dvad, dqf, dkf, dvf, dqm, dqf_aug, dkf_aug, gains, bfor, triu, gm64, gm128, T, tb):
    nb = T // tb

    def body(qa_ref, qf_ref, kf_ref, qm_ref, ka_ref, fl_ref,
             dqa_ref, dkad_ref, dvad_ref, dqf_ref, dkf_ref, dvf_ref, dqm_ref, dqfa_ref, dkfa_ref,
             gains_ref, bfor_ref, triu_ref, gm64_ref, gm128_ref,
             dlo_o, gacc_o, carry):
        i = pl.program_id(0)
        gm64v = gm64_ref[...]
        gm128v = gm128_ref[...]
        lane = _lane((tb, LANES))

        @pl.when(i == 0)
        def _():
            carry[...] = jnp.zeros_like(carry)
            gacc_o[...] = jnp.zeros_like(gacc_o)

        def norm512_bwd(dsrc, xsrc, col0, row, gm):
            gain = gains_ref[row:row + 1, :]
            gsum = jnp.zeros((1, LANES), F32)
            for c in range(4):
                sl = slice(c * LANES, (c + 1) * LANES)
                dx, dg = _head_norm_bwd(dsrc[:, sl], xsrc[:, sl], gm, gain)
                dlo_o[:, col0 + c * LANES:col0 + (c + 1) * LANES] = dx.astype(dlo_o.dtype)
                gsum = gsum + dg
            gacc_o[row:row + 1, :] += gsum

        norm512_bwd(dqa_ref, qa_ref, C_QA, 0, gm64v)
        norm512_bwd(dqf_ref, qf_ref, C_QF, 2, gm64v)
        norm512_bwd(dkf_ref, kf_ref, C_KF, 3, gm64v)
        norm512_bwd(dqm_ref, qm_ref, C_QM, 4, gm128v)
        dlo_o[:, C_VF:C_VF + 512] = dvf_ref[...].astype(dlo_o.dtype)

        lo = lane < 64

        def fold(ref):
            f0 = ref[0] + pltpu.roll(ref[0], 64, 1)
            f1 = ref[1] + pltpu.roll(ref[1], 64, 1)
            return jnp.where(lo, f0, f1)

        dka, dg = _head_norm_bwd(fold(dkad_ref), ka_ref[...], gm64v, gains_ref[1:2, :])
        gacc_o[1:2, :] += dg
        dlo_o[:, C_KA:C_KA + LANES] = dka.astype(dlo_o.dtype)
        dlo_o[:, C_VA:C_VA + LANES] = fold(dvad_ref).astype(dlo_o.dtype)

        dc = jnp.zeros((tb, LANES), F32)
        for pair in range(FOX_HEADS // 2):
            sl = slice(pair * LANES, (pair + 1) * LANES)
            rows_sum, cols_sum = dqfa_ref[:, sl], dkfa_ref[:, sl]
            for sub in range(2):
                diff = (jnp.where(lane == AUG_STRIDE * sub + AUG_C, rows_sum, 0.0)
                        - jnp.where(lane == AUG_STRIDE * sub + AUG_NEG_C, cols_sum, 0.0))
                dc = jnp.where(lane == 2 * pair + sub, jnp.sum(diff, axis=1, keepdims=True), dc)
        dlogf = _dot3_left(triu_ref[...], dc) + carry[0:1, :]
        carry[...] = jnp.broadcast_to(dlogf[0:1, :], carry.shape)
        z = fl_ref[...] + bfor_ref[...]
        dfl = jnp.where(lane < FOX_HEADS, dlogf / (1.0 + jnp.exp(z)), 0.0)
        gacc_o[5:6, :] += jnp.sum(dfl, axis=0, keepdims=True)
        dlo_o[:, C_FL:C_FL + LANES] = dfl.astype(dlo_o.dtype)
        dlo_o[:, C_FL + LANES:C_FL + 2 * LANES] = jnp.zeros((tb, LANES), dlo_o.dtype)

    rev = lambda i: nb - 1 - i

    def seg(width, start):
        return pl.BlockSpec((tb, width), lambda i, s=start // width: (rev(i), s))

    const = lambda shape: pl.BlockSpec(shape, lambda i: tuple(0 for _ in shape))
    rows512 = pl.BlockSpec((tb, 512), lambda i: (rev(i), 0))
    dup = pl.BlockSpec((2, tb, LANES), lambda i: (0, rev(i), 0))
    return pl.pallas_call(
        body, name="prep_bwd", grid=(nb,),
        in_specs=[seg(512, C_QA), seg(512, C_QF), seg(512, C_KF), seg(512, C_QM), seg(128, C_KA), seg(128, C_FL),
                  rows512, dup, dup, rows512, rows512, rows512, rows512, rows512, rows512,
                  const((8, LANES)), const((1, LANES)), const((tb, tb)), const((LANES, LANES)), const((LANES, LANES))],
        out_specs=[pl.BlockSpec((tb, LO_W), lambda i: (rev(i), 0)), const((8, LANES))],
        out_shape=[jax.ShapeDtypeStruct((T, LO_W), BF16), jax.ShapeDtypeStruct((8, LANES), F32)],
        scratch_shapes=[pltpu.VMEM((8, LANES), F32)],
        compiler_params=_cparams("arbitrary"),
    )(proj, proj, proj, proj, proj, proj, dqa, dkad, dvad, dqf, dkf, dvf, dqm, dqf_aug, dkf_aug,
      gains, bfor, triu, gm64, gm128)


FOX_SCALE = FOX_HEAD_DIM ** -0.5
AUG_STRIDE = 16
AUG_C = 0
AUG_NEG_C = 3
AUG_STAT = 6
FOX_TQ, FOX_TK = 1024, 1024
FOX_BWD_TQ, FOX_BWD_TK = 1024, 1024


def _fox_head_mask(sub, rows):
    lane = _lane((rows, 2 * LANES))
    main = (lane >= 64 * sub) & (lane < 64 * sub + 64)
    aug = (lane >= LANES + AUG_STRIDE * sub) & (lane < LANES + AUG_STRIDE * (sub + 1))
    return main | aug


def _fox_fwd(q, qaug, k, kaug, v, T, tq, tk):
    nq, nk = T // tq, T // tk
    rep = tk // LANES
    last_of = lambda i: (i * tq + tq - 1) // tk

    def body(q_ref, qa_ref, k_ref, ka_ref, v_ref, o_ref, qab_ref, m_s, acc_s):
        p_, i, j = pl.program_id(0), pl.program_id(1), pl.program_id(2)
        last = last_of(i)

        @pl.when(j == 0)
        def _():
            m_s[...] = jnp.full(m_s.shape, NEG, F32)
            acc_s[...] = jnp.zeros_like(acc_s)

        def step(diagonal):
            q2 = jnp.concatenate([q_ref[...], qa_ref[...]], axis=1)
            k2 = jnp.concatenate([k_ref[...], ka_ref[...]], axis=1)
            v2 = jnp.concatenate([v_ref[...], ka_ref[...]], axis=1)
            if diagonal:
                causal = (lax.broadcasted_iota(jnp.int32, (tq, tk), 1) + j * tk
                          <= lax.broadcasted_iota(jnp.int32, (tq, tk), 0) + i * tq)
            scores = [_dot(jnp.where(_fox_head_mask(sub, tq), q2, jnp.zeros_like(q2)), k2, NT) for sub in range(2)]
            for sub in range(2):
                s = scores[sub]
                if diagonal:
                    s = jnp.where(causal, s, NEG)
                m_prev = m_s[sub]
                m_next = jnp.maximum(m_prev, jnp.max(s, axis=1, keepdims=True))
                p = jnp.exp(s - jnp.tile(m_next, (1, rep)))
                alpha = jnp.exp(m_prev - m_next)
                m_s[sub] = m_next
                acc_s[sub] = acc_s[sub] * jnp.tile(alpha, (1, 2)) + _dot(p.astype(BF16), v2)

        @pl.when(j == last)
        def _():
            step(True)

        @pl.when(j < last)
        def _():
            step(False)

        @pl.when(j == nk - 1)
        def _():
            lane = _lane((tq, LANES))
            outs = []
            qab = qa_ref[...].astype(F32)
            for sub in range(2):
                acc = acc_s[sub]
                base = AUG_STRIDE * sub
                l = jnp.sum(jnp.where(lane == base + AUG_C, acc[:, LANES:], 0.0), axis=1, keepdims=True)
                outs.append(acc[:, :LANES] / l)
                lse = jnp.max(m_s[sub], axis=1, keepdims=True) + jnp.log(l)
                pieces = _split3(-lse)
                for e in range(3):
                    qab = jnp.where(lane == base + AUG_STAT + e, pieces[e].astype(F32), qab)
            o_ref[...] = jnp.where(lane < 64, outs[0], outs[1]).astype(o_ref.dtype)
            qab_ref[...] = qab.astype(BF16)

    qspec = pl.BlockSpec((tq, LANES), lambda p, i, j: (i, p))
    kspec = pl.BlockSpec((tk, LANES), lambda p, i, j: (jnp.minimum(j, last_of(i)), p))
    return pl.pallas_call(
        body, name="fox_fwd", grid=(4, nq, nk),
        in_specs=[qspec, qspec, kspec, kspec, kspec],
        out_specs=[qspec, qspec],
        out_shape=[jax.ShapeDtypeStruct((T, 512), BF16), jax.ShapeDtypeStruct((T, 512), BF16)],
        scratch_shapes=[pltpu.VMEM((2, tq, LANES), F32), pltpu.VMEM((2, tq, 2 * LANES), F32)],
        compiler_params=_cparams("parallel", "parallel", "arbitrary"),
    )(q, qaug, k, kaug, v)


def _fox_bwd(q, qaug, k, kaug, v, do, doaug, T, tq, tk):
    nq, nk = T // tq, T // tk
    first_of = lambda j: (j * tk) // tq

    def body(q_ref, qa_ref, k_ref, ka_ref, v_ref, do_ref, doa_ref,
             dq_ref, dqa_ref, dk_ref, dka_ref, dv_ref, dk_s, dv_s):
        p_, j, i = pl.program_id(0), pl.program_id(1), pl.program_id(2)
        masked = i * tq < (j + 1) * tk - 1

        @pl.when((j == 0) & (i == 0))
        def _():
            dq_ref[...] = jnp.zeros_like(dq_ref)
            dqa_ref[...] = jnp.zeros_like(dqa_ref)

        @pl.when(i == 0)
        def _():
            dk_s[...] = jnp.zeros_like(dk_s)
            dv_s[...] = jnp.zeros_like(dv_s)

        def step(diagonal):
            q2 = jnp.concatenate([q_ref[...], qa_ref[...]], axis=1)
            k2 = jnp.concatenate([k_ref[...], ka_ref[...]], axis=1)
            v2 = jnp.concatenate([v_ref[...], ka_ref[...]], axis=1)
            do2 = jnp.concatenate([do_ref[...], doa_ref[...]], axis=1)
            if diagonal:
                causal = (lax.broadcasted_iota(jnp.int32, (tq, tk), 1) + j * tk
                          <= lax.broadcasted_iota(jnp.int32, (tq, tk), 0) + i * tq)
            qh = [jnp.where(_fox_head_mask(sub, tq), q2, jnp.zeros_like(q2)) for sub in range(2)]
            doh = [jnp.where(_fox_head_mask(sub, tq), do2, jnp.zeros_like(do2)) for sub in range(2)]
            scores = [_dot(qh[sub], k2, NT) for sub in range(2)]
            dps = [_dot(doh[sub], v2, NT) for sub in range(2)]
            dqs = []
            for sub in range(2):
                s = scores[sub]
                if diagonal:
                    s = jnp.where(causal, s, NEG)
                p = jnp.exp(s)
                dsb = (p * dps[sub]).astype(BF16)
                dv_s[...] += _dot(p.astype(BF16), doh[sub][:, :LANES], TN)
                dk_s[...] += _dot(dsb, qh[sub], TN)
                dqs.append(_dot(dsb, k2))
            dq2 = jnp.where(_fox_head_mask(0, tq), dqs[0], dqs[1])
            qrows = pl.ds(pl.multiple_of(i * tq, tq), tq)
            dq_ref[qrows, :] += dq2[:, :LANES] * FOX_SCALE
            dqa_ref[qrows, :] += dq2[:, LANES:]

        @pl.when((i >= first_of(j)) & masked)
        def _():
            step(True)

        @pl.when((i >= first_of(j)) & jnp.logical_not(masked))
        def _():
            step(False)

        @pl.when(i == nq - 1)
        def _():
            dk_ref[...] = dk_s[:, :LANES]
            dka_ref[...] = dk_s[:, LANES:]
            dv_ref[...] = dv_s[...]

    qspec = pl.BlockSpec((tq, LANES), lambda p, j, i: (jnp.maximum(i, first_of(j)), p))
    kspec = pl.BlockSpec((tk, LANES), lambda p, j, i: (j, p))
    resident = pl.BlockSpec((T, LANES), lambda p, j, i: (0, p))
    return pl.pallas_call(
        body, name="fox_bwd", grid=(4, nk, nq),
        in_specs=[qspec, qspec, kspec, kspec, kspec, qspec, qspec],
        out_specs=[resident, resident, kspec, kspec, kspec],
        out_shape=[jax.ShapeDtypeStruct((T, 512), F32)] * 5,
        scratch_shapes=[pltpu.VMEM((tk, 2 * LANES), F32), pltpu.VMEM((tk, LANES), F32)],
        compiler_params=_cparams("arbitrary", "arbitrary", "arbitrary"),
    )(q, qaug, k, kaug, v, do, doaug)


SWA_SUB = 4
SWA_TB = SWA_SUB * WINDOW


def _t5_bucket_matrix():
    t = jnp.arange(WINDOW)[:, None] + WINDOW
    s = jnp.arange(2 * WINDOW)[None, :]
    max_exact = REL_BUCKETS // 2
    d = jnp.maximum(t - s, 0)
    df = jnp.maximum(d, 1).astype(F32)
    large = max_exact + (jnp.log(df / max_exact) / math.log(REL_MAX_DIST / max_exact)
                         * (REL_BUCKETS - max_exact)).astype(jnp.int32)
    large = jnp.minimum(large, REL_BUCKETS - 1)
    return jnp.where(d < max_exact, d, large).astype(jnp.int32)


def _swa_bias(rel_bias, bucket):
    def body(rel_ref, bucket_ref, o_ref):
        b = bucket_ref[...]
        for h in range(SWA_HEADS):
            acc = jnp.zeros(b.shape, F32)
            for r in range(REL_BUCKETS):
                acc = jnp.where(b == r, rel_ref[r, h], acc)
            o_ref[h] = acc

    return pl.pallas_call(
        body, name="swa_bias",
        in_specs=[pl.BlockSpec(memory_space=pltpu.SMEM), pl.BlockSpec(memory_space=pltpu.VMEM)],
        out_specs=pl.BlockSpec(memory_space=pltpu.VMEM),
        out_shape=jax.ShapeDtypeStruct((SWA_HEADS, WINDOW, 2 * WINDOW), F32),
    )(rel_bias, bucket)


def _swa_bias_bwd(dbias, bucket):
    def body(db_ref, bucket_ref, o_ref):
        b = bucket_ref[...]
        lane = _lane((1, LANES))
        for r in range(REL_BUCKETS):
            row = jnp.zeros((1, LANES), F32)
            for h in range(SWA_HEADS):
                part = jnp.sum(jnp.where(b == r, db_ref[h], 0.0), axis=0, keepdims=True)
                tot = jnp.sum(part, axis=1, keepdims=True)
                row = jnp.where(lane == h, tot, row)
            o_ref[r:r + 1, :] = row

    return pl.pallas_call(
        body, name="swa_bias_bwd",
        in_specs=[pl.BlockSpec(memory_space=pltpu.VMEM), pl.BlockSpec(memory_space=pltpu.VMEM)],
        out_specs=pl.BlockSpec(memory_space=pltpu.VMEM),
        out_shape=jax.ShapeDtypeStruct((REL_BUCKETS, LANES), F32),
    )(dbias, bucket)


SWA_GROUP = SWA_HEADS // SWA_KV_HEADS


def _swa_valid(r, i):
    t = (lax.broadcasted_iota(jnp.int32, (SWA_GROUP * WINDOW, 2 * WINDOW), 0) & (WINDOW - 1)) + WINDOW
    s = lax.broadcasted_iota(jnp.int32, (SWA_GROUP * WINDOW, 2 * WINDOW), 1)
    dist = t - s
    band = (dist >= 0) & (dist < WINDOW)
    if r == 0:
        band = band & ((s >= WINDOW) | (i > 0))
    return band


def _swa_stack(blk):
    lane = _lane((WINDOW, LANES))
    parts = []
    for g in range(SWA_GROUP):
        b = blk[:, LANES * (g // 2):LANES * (g // 2 + 1)]
        parts.append(jnp.where((lane >= 64) if g % 2 else (lane < 64), b, jnp.zeros_like(b)))
    return jnp.concatenate(parts, axis=0)


def _swa_unstack(st):
    lane = _lane((WINDOW, LANES))
    W = WINDOW
    return jnp.concatenate([jnp.where(lane < 64, st[2 * b * W:(2 * b + 1) * W], st[(2 * b + 1) * W:(2 * b + 2) * W])
                            for b in range(2)], axis=1)


def _swa_sink_column(sink_ref, kvh):
    row = lax.broadcasted_iota(jnp.int32, (SWA_GROUP * WINDOW, 1), 0)
    col = jnp.full((SWA_GROUP * WINDOW, 1), sink_ref[SWA_GROUP * kvh + SWA_GROUP - 1], F32)
    for g in range(SWA_GROUP - 2, -1, -1):
        col = jnp.where(row < (g + 1) * WINDOW, sink_ref[SWA_GROUP * kvh + g], col)
    return col


def _swa_specs(T):
    W = WINDOW
    qspec = pl.BlockSpec((SWA_TB, 2 * LANES), lambda h, i: (i, h))
    own = pl.BlockSpec((None, SWA_TB, LANES), lambda h, i: (h, i, 0))
    prev = pl.BlockSpec((None, W, LANES), lambda h, i: (h, jnp.maximum(SWA_SUB * i - 1, 0), 0))
    stat = pl.BlockSpec((SWA_GROUP, SWA_TB, LANES), lambda h, i: (h, i, 0))
    bias = pl.BlockSpec((None, SWA_GROUP * W, 2 * W), lambda h, i: (h, 0, 0))
    return qspec, own, prev, stat, bias


def _swa_fwd(sinks, q, kad, vad, bias, T):
    nb = T // SWA_TB
    scale = SWA_HEAD_DIM ** -0.5
    W = WINDOW

    def body(sink_ref, q_ref, k_ref, kp_ref, v_ref, vp_ref, bias_ref, o_ref, lse_ref):
        kvh, i = pl.program_id(0), pl.program_id(1)
        sink = _swa_sink_column(sink_ref, kvh)
        for r in range(SWA_SUB):
            rs = slice(r * W, (r + 1) * W)
            ps = slice((r - 1) * W, r * W)
            k_own, v_own = k_ref[rs, :], v_ref[rs, :]
            k_prev = kp_ref[...] if r == 0 else k_ref[ps, :]
            v_prev = vp_ref[...] if r == 0 else v_ref[ps, :]
            qs = _swa_stack(q_ref[rs, :])
            s = jnp.concatenate([_dot(qs, k_prev, NT), _dot(qs, k_own, NT)], axis=1) * scale + bias_ref[...]
            s = jnp.where(_swa_valid(r, i), s, NEG)
            m = jnp.maximum(jnp.max(s, axis=1, keepdims=True), sink)
            p = jnp.exp(s - m)
            denom = jnp.sum(p, axis=1, keepdims=True) + jnp.exp(sink - m)
            pn = (p / denom).astype(BF16)
            o_ref[rs, :] = _swa_unstack(_dot(pn[:, :W], v_prev) + _dot(pn[:, W:], v_own)).astype(o_ref.dtype)
            lse = m + jnp.log(denom)
            for g in range(SWA_GROUP):
                lse_ref[g, rs, :] = jnp.broadcast_to(lse[g * W:(g + 1) * W], (W, LANES))

    qspec, own, prev, stat, bspec = _swa_specs(T)
    return pl.pallas_call(
        body, name="swa_fwd", grid=(SWA_KV_HEADS, nb),
        in_specs=[pl.BlockSpec(memory_space=pltpu.SMEM), qspec, own, prev, own, prev, bspec],
        out_specs=[qspec, stat],
        out_shape=[jax.ShapeDtypeStruct((T, 512), BF16), jax.ShapeDtypeStruct((SWA_HEADS, T, LANES), F32)],
        compiler_params=_cparams("parallel", "parallel"),
    )(sinks, q, kad, kad, vad, vad, bias.reshape(SWA_KV_HEADS, SWA_GROUP * W, 2 * W))


def _swa_bwd(sinks, q, kad, vad, bias, do, lse, delta, T):
    nb = T // SWA_TB
    scale = SWA_HEAD_DIM ** -0.5
    W = WINDOW

    def body(sink_ref, q_ref, k_ref, kp_ref, v_ref, vp_ref, bias_ref, do_ref, lse_ref, dl_ref,
             dq_ref, dkad_ref, dvad_ref, dbias_ref, dsk_ref):
        kvh, i = pl.program_id(0), pl.program_id(1)
        sink = _swa_sink_column(sink_ref, kvh)

        @pl.when((kvh == 0) & (i == 0))
        def _():
            dkad_ref[...] = jnp.zeros_like(dkad_ref)
            dvad_ref[...] = jnp.zeros_like(dvad_ref)

        @pl.when(i == 0)
        def _():
            dbias_ref[...] = jnp.zeros_like(dbias_ref)
            dsk_ref[...] = jnp.zeros_like(dsk_ref)

        for r in range(SWA_SUB):
            rs = slice(r * W, (r + 1) * W)
            ps = slice((r - 1) * W, r * W)
            k_own, v_own = k_ref[rs, :], v_ref[rs, :]
            k_prev = kp_ref[...] if r == 0 else k_ref[ps, :]
            v_prev = vp_ref[...] if r == 0 else v_ref[ps, :]
            qs = _swa_stack(q_ref[rs, :])
            dos = _swa_stack(do_ref[rs, :])
            lse_b = jnp.concatenate([lse_ref[g, rs, :] for g in range(SWA_GROUP)], axis=0)
            dl_b = jnp.concatenate([dl_ref[g, rs, :] for g in range(SWA_GROUP)], axis=0)
            s = jnp.concatenate([_dot(qs, k_prev, NT), _dot(qs, k_own, NT)], axis=1) * scale + bias_ref[...]
            s = jnp.where(_swa_valid(r, i), s, NEG)
            p = jnp.exp(s - jnp.tile(lse_b, (1, 2)))
            dp = jnp.concatenate([_dot(dos, v_prev, NT), _dot(dos, v_own, NT)], axis=1)
            ds = p * (dp - jnp.tile(dl_b, (1, 2)))
            sink_term = jnp.exp(sink - lse_b) * dl_b
            for g in range(SWA_GROUP):
                dbias_ref[g] += ds[g * W:(g + 1) * W]
                dsk_ref[g:g + 1, :] += jnp.sum(sink_term[g * W:(g + 1) * W], axis=0, keepdims=True)
            dsb = ds.astype(BF16)
            pb = p.astype(BF16)
            dq_ref[rs, :] = _swa_unstack((_dot(dsb[:, :W], k_prev) + _dot(dsb[:, W:], k_own)) * scale)
            own_row = pl.multiple_of(i * SWA_TB + r * W, W)
            dkad_ref[kvh, pl.ds(own_row, W), :] += _dot(dsb[:, W:], qs, TN) * scale
            dvad_ref[kvh, pl.ds(own_row, W), :] += _dot(pb[:, W:], dos, TN)
            dk_prev = _dot(dsb[:, :W], qs, TN) * scale
            dv_prev = _dot(pb[:, :W], dos, TN)
            if r == 0:
                @pl.when(i > 0)
                def _():
                    prev_row = pl.multiple_of(i * SWA_TB - W, W)
                    dkad_ref[kvh, pl.ds(prev_row, W), :] += dk_prev
                    dvad_ref[kvh, pl.ds(prev_row, W), :] += dv_prev
            else:
                prev_row = pl.multiple_of(i * SWA_TB + (r - 1) * W, W)
                dkad_ref[kvh, pl.ds(prev_row, W), :] += dk_prev
                dvad_ref[kvh, pl.ds(prev_row, W), :] += dv_prev

    qspec, own, prev, stat, bspec = _swa_specs(T)
    full = pl.BlockSpec((SWA_KV_HEADS, T, LANES), lambda h, i: (0, 0, 0))
    return pl.pallas_call(
        body, name="swa_bwd", grid=(SWA_KV_HEADS, nb),
        in_specs=[pl.BlockSpec(memory_space=pltpu.SMEM), qspec, own, prev, own, prev, bspec, qspec, stat, stat],
        out_specs=[qspec, full, full, pl.BlockSpec((SWA_GROUP, W, 2 * W), lambda h, i: (h, 0, 0)),
                   pl.BlockSpec((None, 8, LANES), lambda h, i: (h, 0, 0))],
        out_shape=[jax.ShapeDtypeStruct((T, 512), F32), jax.ShapeDtypeStruct((SWA_KV_HEADS, T, LANES), F32),
                   jax.ShapeDtypeStruct((SWA_KV_HEADS, T, LANES), F32), jax.ShapeDtypeStruct((SWA_HEADS, W, 2 * W), F32),
                   jax.ShapeDtypeStruct((SWA_KV_HEADS, 8, LANES), F32)],
        compiler_params=_cparams("arbitrary", "arbitrary"),
    )(sinks, q, kad, kad, vad, vad, bias.reshape(SWA_KV_HEADS, SWA_GROUP * W, 2 * W), do, lse, delta)


def _mem_fwd(q, mk, mv, T, tq):
    scale = MEM_HEAD_DIM ** -0.5

    def body(q_ref, k_ref, v_ref, o_ref, lse_ref):
        s = _dot(q_ref[...], k_ref[...], NT) * scale
        m = jnp.max(s, axis=1, keepdims=True)
        p = jnp.exp(s - m)
        l = jnp.sum(p, axis=1, keepdims=True)
        o_ref[...] = _dot((p / l).astype(BF16), v_ref[...]).astype(o_ref.dtype)
        lse_ref[...] = jnp.broadcast_to(m + jnp.log(l), (tq, LANES))

    qspec = pl.BlockSpec((tq, LANES), lambda h, i: (i, h))
    kspec = pl.BlockSpec((N_MEM, LANES), lambda h, i: (0, h))
    return pl.pallas_call(
        body, name="mem_fwd", grid=(MEM_HEADS, T // tq),
        in_specs=[qspec, kspec, kspec],
        out_specs=[qspec, pl.BlockSpec((None, tq, LANES), lambda h, i: (h, i, 0))],
        out_shape=[jax.ShapeDtypeStruct((T, 512), BF16), jax.ShapeDtypeStruct((MEM_HEADS, T, LANES), F32)],
        compiler_params=_cparams("parallel", "parallel"),
    )(q, mk, mv)


def _mem_bwd(q, mk, mv, do, lse, delta, T, tq):
    scale = MEM_HEAD_DIM ** -0.5
    rep = N_MEM // LANES

    def body(q_ref, k_ref, v_ref, do_ref, lse_ref, dl_ref, dq_ref, dk_ref, dv_ref):
        i = pl.program_id(1)

        @pl.when(i == 0)
        def _():
            dk_ref[...] = jnp.zeros_like(dk_ref)
            dv_ref[...] = jnp.zeros_like(dv_ref)

        qv, dov = q_ref[...], do_ref[...]
        s = _dot(qv, k_ref[...], NT) * scale
        p = jnp.exp(s - jnp.tile(lse_ref[...], (1, rep)))
        dp = _dot(dov, v_ref[...], NT)
        ds = p * (dp - jnp.tile(dl_ref[...], (1, rep)))
        dsb = ds.astype(BF16)
        dq_ref[...] = _dot(dsb, k_ref[...]) * scale
        dk_ref[...] += _dot(dsb, qv, TN) * scale
        dv_ref[...] += _dot(p.astype(BF16), dov, TN)

    qspec = pl.BlockSpec((tq, LANES), lambda h, i: (i, h))
    kspec = pl.BlockSpec((N_MEM, LANES), lambda h, i: (0, h))
    stat = pl.BlockSpec((None, tq, LANES), lambda h, i: (h, i, 0))
    return pl.pallas_call(
        body, name="mem_bwd", grid=(MEM_HEADS, T // tq),
        in_specs=[qspec, kspec, kspec, qspec, stat, stat],
        out_specs=[qspec, kspec, kspec],
        out_shape=[jax.ShapeDtypeStruct((T, 512), F32), jax.ShapeDtypeStruct((N_MEM, 512), F32),
                   jax.ShapeDtypeStruct((N_MEM, 512), F32)],
        compiler_params=_cparams("arbitrary", "arbitrary"),
    )(q, mk, mv, do, lse, delta)


def _mem_prep_fwd(mem, g_mem, w_kv, kn_gain, gm128):
    def body(mem_ref, g_ref, w_ref, kn_ref, gm_ref, memn_o, kv_o, mk_o, mv_o):
        xhat, _ = _rms_rows(mem_ref[...], None)
        memn = (xhat * g_ref[...]).astype(BF16)
        memn_o[...] = memn
        kv = _dot(memn, w_ref[...])
        kv_o[...] = kv
        gm = gm_ref[...]
        for c in range(4):
            sl = slice(c * LANES, (c + 1) * LANES)
            y, _ = _head_norm(kv[:, sl], gm, kn_ref[...])
            mk_o[:, sl] = y.astype(BF16)
        mv_o[...] = kv[:, 512:].astype(BF16)

    vm = pl.BlockSpec(memory_space=pltpu.VMEM)
    return pl.pallas_call(
        body, name="mem_prep_fwd", in_specs=[vm] * 5, out_specs=[vm] * 4,
        out_shape=[jax.ShapeDtypeStruct((N_MEM, D_MODEL), BF16), jax.ShapeDtypeStruct((N_MEM, D_MODEL), F32),
                   jax.ShapeDtypeStruct((N_MEM, 512), BF16), jax.ShapeDtypeStruct((N_MEM, 512), BF16)],
        compiler_params=pltpu.CompilerParams(vmem_limit_bytes=VMEM_LIMIT),
    )(mem, g_mem, w_kv, kn_gain, gm128)


def _mem_prep_bwd(mem, g_mem, memn, kv, w_kv, kn_gain, gm128, dmk, dmv):
    def body(mem_ref, g_ref, memn_ref, kv_ref, w_ref, kn_ref, gm_ref, dmk_ref, dmv_ref, dw_o, dg_o, dkn_o, dkv_s):
        gm = gm_ref[...]
        dkn = jnp.zeros((1, LANES), F32)
        for c in range(4):
            sl = slice(c * LANES, (c + 1) * LANES)
            dx, dg = _head_norm_bwd(dmk_ref[:, sl], kv_ref[:, sl], gm, kn_ref[...])
            dkv_s[:, sl] = dx.astype(BF16)
            dkn = dkn + dg
        dkn_o[...] = dkn
        dkv_s[:, 512:] = dmv_ref[...].astype(BF16)
        dkv = dkv_s[...]
        dw_o[...] = _dot(memn_ref[...], dkv, TN)
        dmemn = _dot(dkv, w_ref[...], NT)
        xhat, _ = _rms_rows(mem_ref[...], None)
        dg_o[...] = jnp.sum(dmemn * xhat, axis=0, keepdims=True)

    vm = pl.BlockSpec(memory_space=pltpu.VMEM)
    return pl.pallas_call(
        body, name="mem_prep_bwd", in_specs=[vm] * 9, out_specs=[vm] * 3,
        out_shape=[jax.ShapeDtypeStruct((D_MODEL, D_MODEL), F32), jax.ShapeDtypeStruct((1, D_MODEL), F32),
                   jax.ShapeDtypeStruct((1, LANES), F32)],
        scratch_shapes=[pltpu.VMEM((N_MEM, D_MODEL), BF16)],
        compiler_params=pltpu.CompilerParams(vmem_limit_bytes=VMEM_LIMIT),
    )(mem, g_mem, memn, kv, w_kv, kn_gain, gm128, dmk, dmv)


SLOT_O = D_MODEL // N_SHARD


def _merge_fwd(proj, b_gate, o3, w3, T, tb):
    def body(gl_ref, bg_ref, oa_ref, of_ref, om_ref, wa_ref, wf_ref, wm_ref, out_ref):
        o_refs = (oa_ref, of_ref, om_ref)
        w_refs = (wa_ref, wf_ref, wm_ref)
        for n in range(N_SHARD):
            acc = jnp.zeros((tb, SLOT_O), F32)
            for b in range(3):
                c0 = b * D_MODEL + n * SLOT_O
                g = jax.nn.sigmoid(gl_ref[:, c0:c0 + SLOT_O] + bg_ref[:, c0:c0 + SLOT_O])
                acc = acc + g * _dot(o_refs[b][...], w_refs[b][n])
            out_ref[:, n * SLOT_O:(n + 1) * SLOT_O] = acc.astype(out_ref.dtype)

    rows = pl.BlockSpec((tb, 512), lambda i: (i, 0))
    wspec = pl.BlockSpec((N_SHARD, 512, SLOT_O), lambda i: (0, 0, 0))
    return pl.pallas_call(
        body, name="merge_fwd", grid=(T // tb,),
        in_specs=[pl.BlockSpec((tb, GATE_W), lambda i: (i, 1)), pl.BlockSpec((1, GATE_W), lambda i: (0, 0)),
                  rows, rows, rows, wspec, wspec, wspec],
        out_specs=pl.BlockSpec((tb, D_MODEL), lambda i: (i, 0)),
        out_shape=jax.ShapeDtypeStruct((T, D_MODEL), BF16),
        compiler_params=_cparams("parallel"),
    )(proj, b_gate, *o3, *w3)


def _merge_bwd(proj, b_gate, o3, w3, dmerged, T, tb):
    heads = (SWA_HEADS, FOX_HEADS, MEM_HEADS)

    def body(gl_ref, bg_ref, oa_ref, of_ref, om_ref, wa_ref, wf_ref, wm_ref, dm_ref,
             dgl_o, doa_o, dof_o, dom_o, dla_o, dlf_o, dlm_o, dwa_o, dwf_o, dwm_o, dbg_o):
        i = pl.program_id(0)
        o_refs = (oa_ref, of_ref, om_ref)
        w_refs = (wa_ref, wf_ref, wm_ref)
        do_refs = (doa_o, dof_o, dom_o)
        dl_refs = (dla_o, dlf_o, dlm_o)
        dw_refs = (dwa_o, dwf_o, dwm_o)

        @pl.when(i == 0)
        def _():
            for r in dw_refs:
                r[...] = jnp.zeros_like(r)
            dbg_o[...] = jnp.zeros_like(dbg_o)

        lane = _lane((tb, LANES))
        for b in range(3):
            ob = o_refs[b][...]
            do = jnp.zeros((tb, 512), F32)
            for n in range(N_SHARD):
                c0 = b * D_MODEL + n * SLOT_O
                g = jax.nn.sigmoid(gl_ref[:, c0:c0 + SLOT_O] + bg_ref[:, c0:c0 + SLOT_O])
                dm = dm_ref[:, n * SLOT_O:(n + 1) * SLOT_O]
                y = _dot(ob, w_refs[b][n])
                dgl = dm * y * g * (1.0 - g)
                dgl_o[:, c0:c0 + SLOT_O] = dgl.astype(dgl_o.dtype)
                dbg_o[:, c0:c0 + SLOT_O] += jnp.sum(dgl, axis=0, keepdims=True)
                dy = (dm * g).astype(BF16)
                do = do + _dot(dy, w_refs[b][n], NT)
                dw_refs[b][n] += _dot(ob, dy, TN)
            do_refs[b][...] = do.astype(BF16)
            prod = do * ob.astype(F32)
            for c in range(4):
                blk = prod[:, c * LANES:(c + 1) * LANES]
                if heads[b] == 8:
                    lo = jnp.sum(jnp.where(lane < 64, blk, 0.0), axis=1, keepdims=True)
                    hi = jnp.sum(jnp.where(lane >= 64, blk, 0.0), axis=1, keepdims=True)
                    if b == 1:
                        aug = jnp.zeros((tb, LANES), F32)
                        for sub, dl in enumerate((lo, hi)):
                            for e, piece in enumerate(_split3(-dl)):
                                aug = jnp.where(lane == AUG_STRIDE * sub + AUG_C + e, piece.astype(F32), aug)
                        dl_refs[b][:, c * LANES:(c + 1) * LANES] = aug.astype(BF16)
                    else:
                        dl_refs[b][2 * c] = jnp.broadcast_to(lo, (tb, LANES))
                        dl_refs[b][2 * c + 1] = jnp.broadcast_to(hi, (tb, LANES))
                else:
                    dl_refs[b][c] = jnp.broadcast_to(jnp.sum(blk, axis=1, keepdims=True), (tb, LANES))

    rows = pl.BlockSpec((tb, 512), lambda i: (i, 0))
    wspec = pl.BlockSpec((N_SHARD, 512, SLOT_O), lambda i: (0, 0, 0))
    stat = lambda h: pl.BlockSpec((h, tb, LANES), lambda i: (0, i, 0))
    return pl.pallas_call(
        body, name="merge_bwd", grid=(T // tb,),
        in_specs=[pl.BlockSpec((tb, GATE_W), lambda i: (i, 1)), pl.BlockSpec((1, GATE_W), lambda i: (0, 0)),
                  rows, rows, rows, wspec, wspec, wspec, pl.BlockSpec((tb, D_MODEL), lambda i: (i, 0))],
        out_specs=[pl.BlockSpec((tb, GATE_W), lambda i: (i, 0)), rows, rows, rows,
                   stat(8), rows, stat(4), wspec, wspec, wspec, pl.BlockSpec((1, GATE_W), lambda i: (0, 0))],
        out_shape=[jax.ShapeDtypeStruct((T, GATE_W), BF16)] + [jax.ShapeDtypeStruct((T, 512), BF16)] * 3
        + [jax.ShapeDtypeStruct((8, T, LANES), F32), jax.ShapeDtypeStruct((T, 512), BF16),
           jax.ShapeDtypeStruct((4, T, LANES), F32)]
        + [jax.ShapeDtypeStruct((N_SHARD, 512, SLOT_O), F32)] * 3 + [jax.ShapeDtypeStruct((1, GATE_W), F32)],
        compiler_params=_cparams("arbitrary"),
    )(proj, b_gate, *o3, *w3, dmerged)


def _local_step(x, mem, tgt, small, wc, w_kv, w_o3, w_out, w_up, w_down, reducer):
    T = x.shape[0]
    tm = min(512, T)
    tile2 = lambda v: jnp.tile(v.reshape(1, -1), (1, LANES // v.size))
    gains = jnp.concatenate([tile2(small["qn_swa"]), tile2(small["kn_swa"]), tile2(small["qn_fox"]),
                             tile2(small["kn_fox"]), tile2(small["qn_mem"]), jnp.zeros((3, LANES), F32)], axis=0)
    kn_mem = small["kn_mem"].reshape(1, LANES)
    bfor = jnp.pad(small["b_forget"].reshape(1, -1), ((0, 0), (0, LANES - FOX_HEADS)))
    gm64 = _group_mean_matrix(64)
    gm128 = _group_mean_matrix(128)
    tb_prep = min(256, T)
    ones = jnp.ones((tb_prep, tb_prep), F32)
    tril = jnp.tril(ones).astype(BF16)
    triu = jnp.triu(ones).astype(BF16)
    bucket = _t5_bucket_matrix()
    g_mix, g_mlp, g_mem = small["g_mix"], small["g_mlp"], small["g_mem"]
    b_gate = small["b_gate"]
    sinks = small["sink_swa"].reshape(-1)

    tl = min(1024, T)
    sq = pl.BlockSpec((tl, D_MODEL), lambda i, j, k: (i, j))
    h = _rmsnorm("rms_mix", x, g_mix, tm)
    (proj,) = _matmul(
        "mm_proj", h, wc, dims=NN, grid=(T // tl, PROJ_W // D_MODEL, 1),
        a_spec=pl.BlockSpec((tl, D_MODEL), lambda i, j, k: (i, 0)),
        b_spec=pl.BlockSpec((D_MODEL, D_MODEL), lambda i, j, k: (0, j)),
        acc_shape=(tl, D_MODEL),
        outs=[(jax.ShapeDtypeStruct((T, PROJ_W), F32), sq)],
        epilogue=_epi_store)
    qa, qf, kf, vf, qm, kad, vad, qf_aug, kf_aug = _prep_fwd(proj, gains, bfor, tril, gm64, gm128, T, tb_prep)
    bias = _swa_bias(small["rel_bias"], bucket)
    o_swa, lse_swa = _swa_fwd(sinks, qa, kad, vad, bias, T)
    o_fox, qf_aug_bwd = _fox_fwd(qf, qf_aug, kf, kf_aug, vf, T, min(FOX_TQ, T), min(FOX_TK, T))
    memn, kv, mk, mv = _mem_prep_fwd(mem, g_mem, w_kv, kn_mem, gm128)
    o_mem, lse_mem = _mem_fwd(qm, mk, mv, T, tm)
    o3 = (o_swa, o_fox, o_mem)
    merged = _merge_fwd(proj, b_gate, o3, w_o3, T, min(256, T))

    def epi_residual(acc, extra_refs, out_refs, ij):
        out_refs[0][...] = extra_refs[0][...] + acc

    row_full = pl.BlockSpec((tm, D_MODEL), lambda i, j, k: (i, 0))
    row_big = pl.BlockSpec((tl, D_MODEL), lambda i, j, k: (i, 0))
    whole = pl.BlockSpec((D_MODEL, D_MODEL), lambda i, j, k: (0, 0))
    (x2,) = _matmul(
        "mm_out", merged, w_out, dims=NN, grid=(T // tl, 1, 1),
        a_spec=row_big, b_spec=whole,
        acc_shape=(tl, D_MODEL), extra=[(x, row_big)],
        outs=[(jax.ShapeDtypeStruct((T, D_MODEL), F32), row_big)], epilogue=epi_residual)
    hm = _rmsnorm("rms_mlp", x2, g_mlp, tm)

    def epi_relu2(acc, extra_refs, out_refs, ij):
        out_refs[0][...] = acc
        r = jnp.maximum(acc, 0.0)
        out_refs[1][...] = (r * r).astype(BF16)

    up, u = _matmul(
        "mm_up", hm, w_up, dims=NN, grid=(T // tl, N_SHARD, 1),
        a_spec=row_big, b_spec=pl.BlockSpec((None, D_MODEL, D_MODEL), lambda i, j, k: (j, 0, 0)),
        acc_shape=(tl, D_MODEL),
        outs=[(jax.ShapeDtypeStruct((T, D_FF), F32), sq), (jax.ShapeDtypeStruct((T, D_FF), BF16), sq)],
        epilogue=epi_relu2)

    def epi_loss(acc, extra_refs, out_refs, ij):
        y = extra_refs[0][...] + acc
        err = y - extra_refs[1][...]
        out_refs[0][...] = err * (1.0 / D_MODEL)
        sq = jnp.sum(jnp.sum(err * err, axis=1, keepdims=True), axis=0, keepdims=True)

        @pl.when(ij[0] == 0)
        def _():
            out_refs[1][...] = jnp.zeros_like(out_refs[1])

        out_refs[1][...] += jnp.broadcast_to(sq, out_refs[1].shape)

    kblk = pl.BlockSpec((tl, D_MODEL), lambda i, j, k: (i, k))
    dy, loss_acc = _matmul(
        "mm_down", u, w_down, dims=NN, grid=(T // tl, 1, N_SHARD),
        a_spec=kblk, b_spec=pl.BlockSpec((D_MODEL, D_MODEL), lambda i, j, k: (k, 0)),
        acc_shape=(tl, D_MODEL), extra=[(x2, row_big), (tgt, row_big)],
        outs=[(jax.ShapeDtypeStruct((T, D_MODEL), F32), row_big),
              (jax.ShapeDtypeStruct((8, LANES), F32), pl.BlockSpec((8, LANES), lambda i, j, k: (0, 0)))],
        epilogue=epi_loss)
    loss = loss_acc[0, 0] * (0.5 / D_MODEL)

    def epi_dup(acc, extra_refs, out_refs, ij):
        out_refs[0][...] = (acc * (2.0 * jnp.maximum(extra_refs[0][...], 0.0))).astype(BF16)

    (dup,) = _matmul(
        "mm_dup", dy, w_down, dims=NT, grid=(T // tl, N_SHARD, 1),
        a_spec=row_big, b_spec=pl.BlockSpec((D_MODEL, D_MODEL), lambda i, j, k: (j, 0)),
        acc_shape=(tl, D_MODEL), extra=[(up, sq)],
        outs=[(jax.ShapeDtypeStruct((T, D_FF), BF16), sq)], epilogue=epi_dup)

    nkt = T // tl
    t_rows = pl.BlockSpec((tl, D_MODEL), lambda i, j, k: (k, i))
    t_cols = pl.BlockSpec((tl, D_MODEL), lambda i, j, k: (k, j))
    (d_w_down,) = _matmul(
        "mm_dw_down", u, dy, dims=TN, grid=(N_SHARD, 1, nkt),
        a_spec=t_rows, b_spec=t_cols, acc_shape=(D_MODEL, D_MODEL),
        outs=[(jax.ShapeDtypeStruct((D_FF, D_MODEL), F32), pl.BlockSpec((D_MODEL, D_MODEL), lambda i, j, k: (i, 0)))],
        epilogue=_epi_store)
    (d_w_up,) = _matmul(
        "mm_dw_up", hm, dup, dims=TN, grid=(1, N_SHARD, nkt),
        a_spec=t_rows, b_spec=t_cols, acc_shape=(D_MODEL, D_MODEL),
        outs=[(jax.ShapeDtypeStruct((N_SHARD, D_MODEL, D_MODEL), F32),
               pl.BlockSpec((None, D_MODEL, D_MODEL), lambda i, j, k: (j, 0, 0)))],
        epilogue=_epi_store)

    def epi_rms_bwd(acc, extra_refs, out_refs, ij):
        dx, dg = _rmsnorm_bwd_rows(acc, extra_refs[0][...], extra_refs[1][...])
        out_refs[0][...] = dx + extra_refs[2][...]

        @pl.when(ij[0] == 0)
        def _():
            out_refs[1][...] = jnp.zeros_like(out_refs[1])

        out_refs[1][...] += dg

    gain_spec = pl.BlockSpec((1, D_MODEL), lambda i, j, k: (0, 0))
    dx2, d_g_mlp = _matmul(
        "mm_dhm", dup, w_up, dims=NT, grid=(T // tl, 1, N_SHARD),
        a_spec=kblk, b_spec=pl.BlockSpec((None, D_MODEL, D_MODEL), lambda i, j, k: (k, 0, 0)),
        acc_shape=(tl, D_MODEL), extra=[(x2, row_big), (g_mlp, gain_spec), (dy, row_big)],
        outs=[(jax.ShapeDtypeStruct((T, D_MODEL), F32), row_big), (jax.ShapeDtypeStruct((1, D_MODEL), F32), gain_spec)],
        epilogue=epi_rms_bwd)

    (dmerged,) = _matmul(
        "mm_dmerged", dx2, w_out, dims=NT, grid=(T // tl, 1, 1),
        a_spec=row_big, b_spec=whole,
        acc_shape=(tl, D_MODEL), outs=[(jax.ShapeDtypeStruct((T, D_MODEL), F32), row_big)], epilogue=_epi_store)
    (d_w_out,) = _matmul(
        "mm_dw_out", merged, dx2, dims=TN, grid=(1, 1, nkt),
        a_spec=t_rows, b_spec=t_cols, acc_shape=(D_MODEL, D_MODEL),
        outs=[(jax.ShapeDtypeStruct((D_MODEL, D_MODEL), F32), whole)],
        epilogue=_epi_store)
    dmerged = reducer.early_start({"w_mlp_down": d_w_down, "w_mlp_up": d_w_up, "w_out": d_w_out}, dmerged)
    (dgl, do_swa, do_fox, do_mem, dl_swa, do_fox_aug, dl_mem, d_wo_swa, d_wo_fox, d_wo_mem, d_b_gate) = _merge_bwd(
        proj, b_gate, o3, w_o3, dmerged, T, min(256, T))
    do_fox = reducer.early_send(do_fox)

    dqa, dkad, dvad, dbias, dsk = _swa_bwd(sinks, qa, kad, vad, bias, do_swa, lse_swa, dl_swa, T)
    dqf, dqf_aug, dkf, dkf_aug, dvf = _fox_bwd(qf, qf_aug_bwd, kf, kf_aug, vf, do_fox, do_fox_aug, T,
                                               min(FOX_BWD_TQ, T), min(FOX_BWD_TK, T))
    dvf = reducer.early_finish(dvf)
    dqm, dmk, dmv = _mem_bwd(qm, mk, mv, do_mem, lse_mem, dl_mem, T, tm)
    d_w_kv, d_g_mem, d_kn_mem = _mem_prep_bwd(mem, g_mem, memn, kv, w_kv, kn_mem, gm128, dmk, dmv)
    d_rel = _swa_bias_bwd(dbias, bucket)
    dlo, gacc = _prep_bwd(proj, dqa, dkad, dvad, dqf, dkf, dvf, dqm, dqf_aug, dkf_aug, gains, bfor, triu, gm64, gm128,
                          T, tb_prep)

    def dwc_half(name, dpart):
        (res,) = _matmul(
            name, h, dpart, dims=TN, grid=(1, LO_W // D_MODEL, nkt),
            a_spec=t_rows, b_spec=t_cols, acc_shape=(D_MODEL, D_MODEL),
            outs=[(jax.ShapeDtypeStruct((D_MODEL, LO_W), F32), pl.BlockSpec((D_MODEL, D_MODEL), lambda i, j, k: (0, j)))],
            epilogue=_epi_store)
        return res

    d_wc_lo = dwc_half("mm_dwc_lo", dlo)
    d_wc_gl = dwc_half("mm_dwc_gl", dgl)
    dlo = reducer.late_start({"wc_lo": d_wc_lo, "wc_gl": d_wc_gl, "w_mem_kv": d_w_kv, "w_o_swa": d_wo_swa,
                              "w_o_fox": d_wo_fox, "w_o_mem": d_wo_mem}, dlo)
    (dh_lo,) = _matmul(
        "mm_dh_lo", dlo, wc, dims=NT, grid=(T // tl, 1, LO_W // D_MODEL),
        a_spec=kblk, b_spec=pl.BlockSpec((D_MODEL, D_MODEL), lambda i, j, k: (0, k)),
        acc_shape=(tl, D_MODEL), outs=[(jax.ShapeDtypeStruct((T, D_MODEL), F32), row_big)], epilogue=_epi_store)
    dh_lo = reducer.late_send(dh_lo)

    def epi_dx(acc, extra_refs, out_refs, ij):
        dhh = acc + extra_refs[3][...]
        dx, dg = _rmsnorm_bwd_rows(dhh, extra_refs[0][...], extra_refs[1][...])
        out_refs[0][...] = dx + extra_refs[2][...]

        @pl.when(ij[0] == 0)
        def _():
            out_refs[1][...] = jnp.zeros_like(out_refs[1])

        out_refs[1][...] += dg

    grad_x, d_g_mix = _matmul(
        "mm_dh_gl", dgl, wc, dims=NT, grid=(T // tm, 1, GATE_W // D_MODEL),
        a_spec=pl.BlockSpec((tm, D_MODEL), lambda i, j, k: (i, k)),
        b_spec=pl.BlockSpec((D_MODEL, D_MODEL), lambda i, j, k: (0, k + LO_W // D_MODEL)),
        acc_shape=(tm, D_MODEL), extra=[(x, row_full), (g_mix, gain_spec), (dx2, row_full), (dh_lo, row_full)],
        outs=[(jax.ShapeDtypeStruct((T, D_MODEL), F32), row_full), (jax.ShapeDtypeStruct((1, D_MODEL), F32), gain_spec)],
        epilogue=epi_dx)
    grad_x = reducer.late_finish(grad_x)

    fold64 = lambda row: (row[:64] + row[64:]).reshape(1, 64)
    grads = {
        "g_mix": d_g_mix, "b_gate": d_b_gate, "b_forget": gacc[5, :FOX_HEADS].reshape(1, FOX_HEADS),
        "qn_swa": fold64(gacc[0]), "kn_swa": fold64(gacc[1]),
        "sink_swa": -dsk[:, :SWA_GROUP, 0].reshape(1, SWA_HEADS), "rel_bias": d_rel[:, :SWA_HEADS],
        "qn_fox": fold64(gacc[2]), "kn_fox": fold64(gacc[3]),
        "g_mem": d_g_mem, "qn_mem": gacc[4].reshape(1, LANES), "kn_mem": d_kn_mem, "g_mlp": d_g_mlp,
    }
    return loss, grad_x, grads


MESH = pl.DeviceIdType.MESH
ANY = pl.BlockSpec(memory_space=pl.ANY)


def _place():
    x, y, c = lax.axis_index("x"), lax.axis_index("y"), lax.axis_index("c")
    chips = [(1 - x, y), (x, 1 - y), (1 - x, 1 - y)]
    return x, y, c, chips


def _all_gather_shards(slots):
    n = len(slots)

    def body(*refs):
        out = refs[n:2 * n]
        ici_send, ici_recv, d2d_send, d2d_recv = refs[2 * n:]
        x, y, c, chips = _place()
        sibling = (x, y, 1 - c)
        me = 2 * x + y

        def half(a, who):
            hr = slots[a].shape[1] // 2
            return pl.ds(pl.multiple_of(who * hr, hr), hr)

        def ici(a, j, slot, to):
            return pltpu.make_async_remote_copy(
                src_ref=out[a].at[me, half(a, c)], dst_ref=out[a].at[slot, half(a, c)],
                send_sem=ici_send.at[3 * a + j], recv_sem=ici_recv.at[3 * a + j], device_id=to, device_id_type=MESH)

        def d2d(a, j, slot, which):
            part = out[a].at[slot, half(a, which)]
            return pltpu.make_async_remote_copy(
                src_ref=part, dst_ref=part, send_sem=d2d_send.at[3 * a + j], recv_sem=d2d_recv.at[3 * a + j],
                device_id=sibling, device_id_type=MESH)

        sends = [ici(a, j, me, (*chip, c)) for a in range(n) for j, chip in enumerate(chips)]
        for cp in sends:
            cp.start()
        passed = []
        for a in range(n):
            for j, (px, py) in enumerate(chips):
                ici(a, j, 2 * px + py, (px, py, c)).wait_recv()
                cp = d2d(a, j, 2 * px + py, c)
                cp.start()
                passed.append(cp)
        for a in range(n):
            for j, (px, py) in enumerate(chips):
                d2d(a, j, 2 * px + py, 1 - c).wait_recv()
        for cp in sends + passed:
            cp.wait_send()

    return pl.pallas_call(
        body, name="all_gather_weights",
        in_specs=[ANY] * n, out_specs=[ANY] * n,
        out_shape=[jax.ShapeDtypeStruct(s.shape, s.dtype) for s in slots],
        input_output_aliases={a: a for a in range(n)},
        scratch_shapes=[pltpu.SemaphoreType.DMA((3 * n,))] * 4,
    )(*slots)


def _handshake(peers):
    barrier = pltpu.get_barrier_semaphore()
    for peer in peers:
        pl.semaphore_signal(barrier, inc=1, device_id=peer, device_id_type=MESH)
    pl.semaphore_wait(barrier, len(peers))


def _all_gather_shards_async(slots):
    n = len(slots)
    bufs = [jax.new_ref(s, memory_space=pltpu.MemorySpace.HBM) for s in slots]

    def body(ici_send, ici_recv, d2d_send, d2d_recv):
        x, y, c, chips = _place()
        sibling = (x, y, 1 - c)
        me = 2 * x + y
        _handshake([(px, py, c) for px, py in chips] + [sibling])

        def half(a, who):
            hr = slots[a].shape[1] // 2
            return pl.ds(pl.multiple_of(who * hr, hr), hr)

        def ici(a, j, slot, to):
            return pltpu.make_async_remote_copy(
                src_ref=bufs[a].at[me, half(a, c)], dst_ref=bufs[a].at[slot, half(a, c)],
                send_sem=ici_send.at[3 * a + j], recv_sem=ici_recv.at[3 * a + j], device_id=to, device_id_type=MESH)

        def d2d(a, j, slot, which):
            part = bufs[a].at[slot, half(a, which)]
            return pltpu.make_async_remote_copy(
                src_ref=part, dst_ref=part, send_sem=d2d_send.at[3 * a + j], recv_sem=d2d_recv.at[3 * a + j],
                device_id=sibling, device_id_type=MESH)

        sends = [ici(a, j, me, (*chip, c)) for a in range(n) for j, chip in enumerate(chips)]
        for cp in sends:
            cp.start()
        passed = []
        for a in range(n):
            for j, (px, py) in enumerate(chips):
                ici(a, j, 2 * px + py, (px, py, c)).wait_recv()
                cp = d2d(a, j, 2 * px + py, c)
                cp.start()
                passed.append(cp)
        for a in range(n):
            for j, (px, py) in enumerate(chips):
                d2d(a, j, 2 * px + py, 1 - c).wait_recv()
        for cp in sends + passed:
            cp.wait_send()

    pl.kernel(
        body, mesh=plsc.ScalarSubcoreMesh(axis_name="seq", num_cores=1), name="all_gather_weights_async",
        scratch_types=[pltpu.SemaphoreType.DMA((3 * n,))] * 4,
        compiler_params=pltpu.CompilerParams(collective_id=1),
    )()
    return [b[...] for b in bufs]


def _sequencer_call(name, collective_id, n_sems, body):
    pl.kernel(
        body, mesh=plsc.ScalarSubcoreMesh(axis_name="seq", num_cores=1), name=name,
        scratch_types=[pltpu.SemaphoreType.DMA((n_sems,))] * 2,
        compiler_params=pltpu.CompilerParams(collective_id=collective_id),
    )()


def _hbm_ref(value):
    return jax.new_ref(value, memory_space=pltpu.MemorySpace.HBM)


def _pair_exchange(name, collective_id, gs):
    n = len(gs)
    src = [_hbm_ref(g) for g in gs]
    stage = [jax.empty_ref(jax.ShapeDtypeStruct((N_SHARD, g.shape[1] // 2, g.shape[2]), g.dtype),
                           memory_space=pltpu.MemorySpace.HBM) for g in gs]

    def body(send_sem, recv_sem):
        x, y, c, _ = _place()
        sibling = (x, y, 1 - c)
        _handshake([sibling])
        copies = []
        for a in range(n):
            hr = gs[a].shape[1] // 2
            theirs = pl.ds(pl.multiple_of((1 - c) * hr, hr), hr)
            copies.append(pltpu.make_async_remote_copy(
                src_ref=src[a].at[:, theirs, :], dst_ref=stage[a], send_sem=send_sem.at[a], recv_sem=recv_sem.at[a],
                device_id=sibling, device_id_type=MESH))
        for cp in copies:
            cp.start()
        for cp in copies:
            cp.wait()

    _sequencer_call(name, collective_id, n, body)
    return [s[...] for s in stage]


def _chip_exchange(name, collective_id, sums):
    n = len(sums)
    src = [_hbm_ref(s) for s in sums]
    got = [jax.empty_ref(jax.ShapeDtypeStruct((3,) + s.shape[1:], s.dtype), memory_space=pltpu.MemorySpace.HBM)
           for s in sums]

    def body(send_sem, recv_sem):
        x, y, c, chips = _place()
        _handshake([(px, py, c) for px, py in chips])
        copies = []
        for a in range(n):
            for j, (px, py) in enumerate(chips):
                copies.append(pltpu.make_async_remote_copy(
                    src_ref=src[a].at[2 * px + py], dst_ref=got[a].at[j],
                    send_sem=send_sem.at[3 * a + j], recv_sem=recv_sem.at[3 * a + j],
                    device_id=(px, py, c), device_id_type=MESH))
        for cp in copies:
            cp.start()
        for cp in copies:
            cp.wait()

    _sequencer_call(name, collective_id, 3 * n, body)
    return [g[...] for g in got]


def _pair_gather(name, collective_id, fulls):
    n = len(fulls)
    full = [_hbm_ref(f) for f in fulls]

    def body(send_sem, recv_sem):
        x, y, c, _ = _place()
        sibling = (x, y, 1 - c)
        _handshake([sibling])
        copies = []
        for a in range(n):
            hr = fulls[a].shape[0] // 2
            mine = full[a].at[pl.ds(pl.multiple_of(c * hr, hr), hr)]
            copies.append(pltpu.make_async_remote_copy(
                src_ref=mine, dst_ref=mine, send_sem=send_sem.at[a], recv_sem=recv_sem.at[a],
                device_id=sibling, device_id_type=MESH))
        for cp in copies:
            cp.start()
        for cp in copies:
            cp.wait()

    _sequencer_call(name, collective_id, n, body)
    return [f[...] for f in full]


ELEMENTWISE_BLOCK_ELEMS = 256 * 1024


def _row_block(rows, cols):
    rb = 8
    while rb * 2 * cols <= ELEMENTWISE_BLOCK_ELEMS and rb * 2 <= rows:
        rb *= 2
    return rb


def _pair_sum(name, place, g, stage):
    _, R, C = g.shape
    hr = R // 2
    rb = _row_block(hr, C)
    nb = hr // rb

    def body(place_ref, g_ref, st_ref, sum_bf, own_f32):
        s = pl.program_id(1)
        tot = g_ref[...] + st_ref[...]
        sum_bf[...] = tot.astype(BF16)

        @pl.when(s == place_ref[0])
        def _():
            own_f32[...] = tot

    return pl.pallas_call(
        body, name=name,
        grid_spec=pltpu.PrefetchScalarGridSpec(
            num_scalar_prefetch=1, grid=(nb, N_SHARD),
            in_specs=[pl.BlockSpec((None, rb, C), lambda i, s, pr: (s, pr[1] * nb + i, 0)),
                      pl.BlockSpec((None, rb, C), lambda i, s, pr: (s, i, 0))],
            out_specs=[pl.BlockSpec((None, rb, C), lambda i, s, pr: (s, i, 0)),
                       pl.BlockSpec((rb, C), lambda i, s, pr: (i, 0))]),
        out_shape=[jax.ShapeDtypeStruct((N_SHARD, hr, C), BF16), jax.ShapeDtypeStruct((hr, C), F32)],
        compiler_params=_cparams("arbitrary", "arbitrary"),
    )(place, g, stage)


def _final_sum(name, place, own, got):
    hr, C = own.shape
    rb = _row_block(hr, C)
    nb = hr // rb

    def body(place_ref, own_ref, got_ref, o_ref):
        o_ref[...] = ((own_ref[...] + got_ref[0].astype(F32)) + got_ref[1].astype(F32)) + got_ref[2].astype(F32)

    return pl.pallas_call(
        body, name=name,
        grid_spec=pltpu.PrefetchScalarGridSpec(
            num_scalar_prefetch=1, grid=(nb,),
            in_specs=[pl.BlockSpec((rb, C), lambda i, pr: (i, 0)), pl.BlockSpec((3, rb, C), lambda i, pr: (0, i, 0))],
            out_specs=pl.BlockSpec((rb, C), lambda i, pr: (pr[1] * nb + i, 0))),
        out_shape=jax.ShapeDtypeStruct((2 * hr, C), F32),
        compiler_params=_cparams("arbitrary"),
    )(place, own, got)


def _adamw_math(w, g, m, v):
    m = ADAM_B1 * m + (1.0 - ADAM_B1) * g
    v = ADAM_B2 * v + (1.0 - ADAM_B2) * (g * g)
    m_hat = m / (1.0 - ADAM_B1 ** ADAM_STEP)
    v_hat = v / (1.0 - ADAM_B2 ** ADAM_STEP)
    delta = -ADAM_LR * (m_hat / (jnp.sqrt(v_hat) + ADAM_EPS) + ADAM_WD * w)
    return delta, m, v


def _adamw(name, w, g, m, v):
    R, Cw = w.shape
    Cg = g.shape[1]
    rb = _row_block(R, Cg)

    def body(w_ref, g_ref, m_ref, v_ref, g_o, d_o, m_o, v_o):
        gv = g_ref[...]
        delta, mn, vn = _adamw_math(w_ref[...], gv, m_ref[...], v_ref[...])
        g_o[...] = gv
        d_o[...] = delta
        m_o[...] = mn
        v_o[...] = vn

    blk = pl.BlockSpec((rb, Cg), lambda i: (i, 0))
    return pl.pallas_call(
        body, name=name, grid=(R // rb,),
        in_specs=[blk] * 4, out_specs=[blk] * 4,
        out_shape=[jax.ShapeDtypeStruct((R, Cw), F32)] * 4,
        compiler_params=_cparams("parallel"),
    )(w, g, m, v)


N_DEV = 8
SMALL_ROWS = 64


def _small_allreduce_adamw(g, w, m, v):
    def body(g_ref, w_ref, m_ref, v_ref, all_ref, gs_o, d_o, m_o, v_o, send_sems, recv_sems, local_sem):
        x, y, c, chips = _place()
        me, sibling = (x, y, c), (x, y, 1 - c)

        def rows(px, py, pc):
            return all_ref.at[pl.ds(pl.multiple_of((4 * px + 2 * py + pc) * SMALL_ROWS, SMALL_ROWS), SMALL_ROWS), :]

        def copy(k, block, to, src=None):
            return pltpu.make_async_remote_copy(
                src_ref=rows(*block) if src is None else src, dst_ref=rows(*block),
                send_sem=send_sems.at[k], recv_sem=recv_sems.at[k], device_id=to, device_id_type=MESH)

        mine = pltpu.make_async_copy(g_ref, rows(*me), local_sem)
        mine.start()
        first = [copy(0, me, sibling, src=g_ref)]
        first += [copy(1 + j, me, (*chip, c), src=g_ref) for j, chip in enumerate(chips)]
        for cp in first:
            cp.start()
        passed = [copy(4 + j, (*chip, c), sibling) for j, chip in enumerate(chips)]
        for j, chip in enumerate(chips):
            copy(1 + j, (*chip, c), me).wait_recv()
            passed[j].start()
        copy(0, sibling, me).wait_recv()
        for j, chip in enumerate(chips):
            copy(4 + j, (*chip, 1 - c), me).wait_recv()
        for cp in first + passed:
            cp.wait_send()
        mine.wait()

        tot = all_ref[0:SMALL_ROWS, :]
        for d in range(1, N_DEV):
            tot = tot + all_ref[d * SMALL_ROWS:(d + 1) * SMALL_ROWS, :]
        delta, mn, vn = _adamw_math(w_ref[...], tot, m_ref[...], v_ref[...])
        gs_o[...] = tot
        d_o[...] = delta
        m_o[...] = mn
        v_o[...] = vn

    vm = pl.BlockSpec(memory_space=pltpu.VMEM)
    shp = jax.ShapeDtypeStruct((SMALL_ROWS, LANES), F32)
    res = pl.pallas_call(
        body, name="small_allreduce_adamw", in_specs=[vm] * 4, out_specs=[vm] * 5,
        out_shape=[jax.ShapeDtypeStruct((N_DEV * SMALL_ROWS, LANES), F32), shp, shp, shp, shp],
        scratch_shapes=[pltpu.SemaphoreType.DMA((7,)), pltpu.SemaphoreType.DMA((7,)), pltpu.SemaphoreType.DMA],
    )(g, w, m, v)
    return res[1:]


SMALL_NAMES = ("g_mix", "b_gate", "b_forget", "qn_swa", "kn_swa", "sink_swa", "rel_bias", "qn_fox", "kn_fox",
               "g_mem", "qn_mem", "kn_mem", "g_mlp")
BIG_NAMES = ("w_in", "w_mem_kv", "w_o_swa", "w_o_fox", "w_o_mem", "w_out", "w_mlp_up", "w_mlp_down")
WEIGHT_NAMES = ("g_mix", "w_in", "b_gate", "b_forget", "qn_swa", "kn_swa", "sink_swa", "rel_bias", "qn_fox", "kn_fox",
                "g_mem", "w_mem_kv", "qn_mem", "kn_mem", "w_o_swa", "w_o_fox", "w_o_mem", "w_out", "g_mlp",
                "w_mlp_up", "w_mlp_down")


def _pack_small(parts, extra=None):
    rows = []
    for n in SMALL_NAMES:
        flat = parts[n].reshape(-1).astype(F32)
        flat = jnp.pad(flat, (0, (-flat.size) % LANES))
        rows.append(flat.reshape(-1, LANES))
    if extra is not None:
        rows.append(jnp.pad(extra.reshape(1, 1), ((0, 0), (0, LANES - 1))))
    packed = jnp.concatenate(rows, axis=0)
    return jnp.pad(packed, ((0, SMALL_ROWS - packed.shape[0]), (0, 0)))


def _unpack_small(packed, shapes):
    out, r = {}, 0
    for n in SMALL_NAMES:
        size = math.prod(shapes[n])
        nr = -(-size // LANES)
        out[n] = packed[r:r + nr].reshape(-1)[:size].reshape(shapes[n])
        r += nr
    return out, packed[r, 0]


W_IN_SEGMENTS = ((C_QA, 0, 512), (C_QF, 768, 512), (C_KF, 1280, 512), (C_VF, 1792, 512), (C_QM, 2312, 512),
                 (C_KA, 512, 128), (C_VA, 640, 128), (C_FL, 2304, FOX_HEADS), (C_GL, 2824, GATE_W))
RELAYOUT_ROWS = 256


def _permute_pieces(src_of_dst):
    blocks = []
    for b in range(len(src_of_dst) // LANES):
        runs, lane = [], 0
        while lane < LANES:
            src = src_of_dst[b * LANES + lane]
            if src is None:
                lane += 1
                continue
            plane, col = src
            end = lane + 1
            while (end < LANES and src_of_dst[b * LANES + end] == (plane, col + end - lane)
                   and (col + end - lane) // LANES == col // LANES):
                end += 1
            runs.append((plane, col // LANES, (lane - col) % LANES, lane, end))
            lane = end
        blocks.append(runs)
    return blocks


def _permuted_block(runs, load, rows):
    lane = _lane((rows, LANES))
    acc = jnp.zeros((rows, LANES), F32)
    for plane, blk, shift, lo, hi in runs:
        x = load(plane, blk).astype(F32)
        if shift:
            x = pltpu.roll(x, shift, 1)
        acc = x if (lo, hi) == (0, LANES) else jnp.where((lane >= lo) & (lane < hi), x, acc)
    return acc


def _w_in_to_segments(g_in):
    src_of_dst = [None] * PROJ_W
    for mine, theirs, width in W_IN_SEGMENTS:
        for k in range(width):
            src_of_dst[mine + k] = ((theirs + k) // IN_SHARD, (theirs + k) % IN_SHARD)
    blocks = _permute_pieces(src_of_dst)
    rb = RELAYOUT_ROWS

    def body(src_ref, out_ref):
        for b, runs in enumerate(blocks):
            blk = _permuted_block(runs, lambda p, c: src_ref[p, :, c * LANES:(c + 1) * LANES], rb)
            out_ref[:, b * LANES:(b + 1) * LANES] = blk.astype(out_ref.dtype)

    return pl.pallas_call(
        body, name="w_in_to_segments", grid=(D_MODEL // rb,),
        in_specs=[pl.BlockSpec((N_SHARD, rb, IN_SHARD_PAD), lambda i: (0, i, 0))],
        out_specs=pl.BlockSpec((rb, PROJ_W), lambda i: (i, 0)),
        out_shape=jax.ShapeDtypeStruct((D_MODEL, PROJ_W), g_in.dtype),
        compiler_params=_cparams("parallel"),
    )(g_in)


def _w_in_from_segments(lo, gl):
    mine_of_theirs = {}
    for mine, theirs, width in W_IN_SEGMENTS:
        for k in range(width):
            mine_of_theirs[theirs + k] = mine + k
    src_of_dst = [None] * (N_SHARD * IN_SHARD_PAD)
    for s in range(N_SHARD):
        for l in range(IN_SHARD):
            j = mine_of_theirs[s * IN_SHARD + l]
            src_of_dst[s * IN_SHARD_PAD + l] = (j // LO_W, j % LO_W)
    blocks = _permute_pieces(src_of_dst)
    per_slot = IN_SHARD_PAD // LANES
    rb = RELAYOUT_ROWS

    def body(lo_ref, gl_ref, out_ref):
        planes = (lo_ref, gl_ref)
        for b, runs in enumerate(blocks):
            blk = _permuted_block(runs, lambda p, c: planes[p][:, c * LANES:(c + 1) * LANES], rb)
            c0 = (b % per_slot) * LANES
            out_ref[b // per_slot, :, c0:c0 + LANES] = blk

    half = pl.BlockSpec((rb, LO_W), lambda i: (i, 0))
    return pl.pallas_call(
        body, name="w_in_from_segments", grid=(D_MODEL // rb,),
        in_specs=[half, half],
        out_specs=pl.BlockSpec((N_SHARD, rb, IN_SHARD_PAD), lambda i: (0, i, 0)),
        out_shape=jax.ShapeDtypeStruct((N_SHARD, D_MODEL, IN_SHARD_PAD), F32),
        compiler_params=_cparams("parallel"),
    )(lo, gl)


def _after(first, then):
    return lax.optimization_barrier((first, then))


class _ReduceGroup:
    def __init__(self, tag, first_collective_id, place):
        self.tag, self.first_id, self.place = tag, first_collective_id, place

    def start(self, local, tie):
        self.names = tuple(local)
        mine, tie = _after([local[n] for n in self.names], tie)
        self.mine = mine
        self.staged = _pair_exchange("pair_exchange_" + self.tag, self.first_id, mine)
        return tie

    def send(self, tie):
        staged, tie = _after(self.staged, tie)
        sums = [_pair_sum("pair_sum_" + n, self.place, g, st) for n, g, st in zip(self.names, self.mine, staged)]
        travel, tie = _after([s[0] for s in sums], tie)
        self.own = [s[1] for s in sums]
        self.got = _chip_exchange("chip_exchange_" + self.tag, self.first_id + 1, travel)
        return tie

    def finish(self, tie):
        got, tie = _after(self.got, tie)
        halves = [_final_sum("final_sum_" + n, self.place, o, r) for n, o, r in zip(self.names, self.own, got)]
        halves, tie = _after(halves, tie)
        summed = _pair_gather("pair_gather_" + self.tag, self.first_id + 2, halves)
        self.summed = dict(zip(self.names, summed))
        return tie


class _GradReducer:
    def __init__(self, place):
        self.early = _ReduceGroup("early", 2, place)
        self.late = _ReduceGroup("late", 5, place)

    @staticmethod
    def _slot_rows(a):
        return a.reshape(N_SHARD, a.shape[0] // N_SHARD, a.shape[1])

    def early_start(self, g, tie):
        return self.early.start({"w_mlp_down": self._slot_rows(g["w_mlp_down"]), "w_mlp_up": g["w_mlp_up"],
                                 "w_out": self._slot_rows(g["w_out"])}, tie)

    def early_send(self, tie):
        return self.early.send(tie)

    def early_finish(self, tie):
        return self.early.finish(tie)

    def late_start(self, g, tie):
        d_in = _w_in_from_segments(g["wc_lo"], g["wc_gl"])
        return self.late.start({"w_in": d_in, "w_mem_kv": self._slot_rows(g["w_mem_kv"]), "w_o_swa": g["w_o_swa"],
                                "w_o_fox": g["w_o_fox"], "w_o_mem": g["w_o_mem"]}, tie)

    def late_send(self, tie):
        return self.late.send(tie)

    def late_finish(self, tie):
        return self.late.finish(tie)

    @property
    def summed(self):
        return {**self.early.summed, **self.late.summed}


def kernel(x, mem, g_mix, w_in, b_gate, b_forget, qn_swa, kn_swa, sink_swa, rel_bias, qn_fox, kn_fox, g_mem, w_mem_kv, qn_mem, kn_mem, w_o_swa, w_o_fox, w_o_mem, w_out, g_mlp, w_mlp_up, w_mlp_down, loss_target, m_g_mix, m_w_in, m_b_gate, m_b_forget, m_qn_swa, m_kn_swa, m_sink_swa, m_rel_bias, m_qn_fox, m_kn_fox, m_g_mem, m_w_mem_kv, m_qn_mem, m_kn_mem, m_w_o_swa, m_w_o_fox, m_w_o_mem, m_w_out, m_g_mlp, m_w_mlp_up, m_w_mlp_down, v_g_mix, v_w_in, v_b_gate, v_b_forget, v_qn_swa, v_kn_swa, v_sink_swa, v_rel_bias, v_qn_fox, v_kn_fox, v_g_mem, v_w_mem_kv, v_qn_mem, v_kn_mem, v_w_o_swa, v_w_o_fox, v_w_o_mem, v_w_out, v_g_mlp, v_w_mlp_up, v_w_mlp_down):
    given = dict(locals())
    W = {n: given[n] for n in WEIGHT_NAMES}
    M = {n: given["m_" + n] for n in WEIGHT_NAMES}
    V = {n: given["v_" + n] for n in WEIGHT_NAMES}
    pad_in = ((0, 0), (0, IN_SHARD_PAD - IN_SHARD))

    shards = [jnp.pad(w_in[0].astype(BF16), pad_in)] + [W[n][0].astype(BF16) for n in BIG_NAMES[1:]]
    slots = [jnp.broadcast_to(s[None], (N_SHARD,) + s.shape) for s in shards]
    (g_in,) = _all_gather_shards(slots[:1])
    g_in, late = lax.optimization_barrier((g_in, slots[1:]))
    g_kv, g_oa, g_of, g_om, g_out, g_up, g_down = _all_gather_shards_async(late)
    wc = _w_in_to_segments(g_in)
    small = {n: (W[n] if n == "rel_bias" else W[n].reshape(1, -1)) for n in SMALL_NAMES}

    place = jnp.stack([2 * lax.axis_index("x") + lax.axis_index("y"), lax.axis_index("c")]).astype(jnp.int32)
    reducer = _GradReducer(place)
    loss, grad_x, grads = _local_step(
        x[0], mem[0], loss_target[0], small, wc, g_kv.reshape(D_MODEL, D_MODEL), (g_oa, g_of, g_om),
        g_out.reshape(D_MODEL, D_MODEL), g_up, g_down.reshape(D_FF, D_MODEL), reducer)

    out = {}
    for n in BIG_NAMES:
        res = _adamw("adamw_" + n, W[n][0], reducer.summed[n], M[n][0], V[n][0])
        out[n] = [r.reshape(W[n].shape) for r in res]
    shapes = {n: W[n].shape for n in SMALL_NAMES}
    packed = _small_allreduce_adamw(_pack_small(grads, loss), _pack_small(W), _pack_small(M), _pack_small(V))
    unpacked = [_unpack_small(p, shapes) for p in packed]
    for n in SMALL_NAMES:
        out[n] = [u[0][n] for u in unpacked]
    loss_total = unpacked[0][1]

    return (loss_total, grad_x.reshape(x.shape),
            *[out[n][0] for n in WEIGHT_NAMES], *[out[n][1] for n in WEIGHT_NAMES],
            *[out[n][2] for n in WEIGHT_NAMES], *[out[n][3] for n in WEIGHT_NAMES])
```

```python
import functools
import math

import jax
import jax.numpy as jnp
from jax import lax
from jax.experimental import pallas as pl
from jax.experimental.pallas import tpu as pltpu
from jax.experimental.pallas import tpu_sc as plsc

F32 = jnp.float32
BF16 = jnp.bfloat16

D_MODEL = 1024
N_MEM = 256
SWA_HEADS = 8
SWA_KV_HEADS = 2
SWA_HEAD_DIM = 64
WINDOW = 128
FOX_HEADS = 8
FOX_HEAD_DIM = 64
MEM_HEADS = 4
MEM_HEAD_DIM = 128
D_FF = 4 * D_MODEL
REL_BUCKETS = 32
REL_MAX_DIST = 128
EPS = 1e-6
NEG = -1e30
GATE_W = 3 * D_MODEL
IN_WIDTH = 5896
N_SHARD = 4
IN_SHARD = IN_WIDTH // N_SHARD
IN_SHARD_PAD = 1536

ADAM_LR = 0.001
ADAM_B1 = 0.9
ADAM_B2 = 0.999
ADAM_EPS = 1e-08
ADAM_WD = 0.01
ADAM_STEP = 10

LANES = 128
V7X_VMEM_BYTES = 64 * 1024 * 1024
VMEM_LIMIT = V7X_VMEM_BYTES * 3 // 4

C_QA, C_QF, C_KF, C_VF, C_QM, C_KA, C_VA, C_FL, C_GL = 0, 512, 1024, 1536, 2048, 2560, 2688, 2816, 3072
LO_W = 3072
PROJ_W = 6144

NN = (((1,), (0,)), ((), ()))
NT = (((1,), (1,)), ((), ()))
TN = (((0,), (0,)), ((), ()))


def _dot(a, b, dims=NN):
    return lax.dot_general(a, b, dims, preferred_element_type=F32)


def _cparams(*sem):
    return pltpu.CompilerParams(dimension_semantics=sem, vmem_limit_bytes=VMEM_LIMIT)


def _split3(a):
    hi = a.astype(BF16)
    r1 = a - hi.astype(F32)
    mid = r1.astype(BF16)
    lo = (r1 - mid.astype(F32)).astype(BF16)
    return hi, mid, lo


def _dot3_right(a, g):
    hi, mid, lo = _split3(a)
    return _dot(hi, g) + _dot(mid, g) + _dot(lo, g)


def _dot3_left(g, a):
    hi, mid, lo = _split3(a)
    return _dot(g, hi) + _dot(g, mid) + _dot(g, lo)


def _group_mean_matrix(d):
    r = jnp.arange(LANES)
    return jnp.where((r[:, None] // d) == (r[None, :] // d), 1.0 / d, 0.0).astype(BF16)


def _lane(shape):
    return lax.broadcasted_iota(jnp.int32, shape, len(shape) - 1)


def _matmul(name, a, b, *, dims, grid, a_spec, b_spec, acc_shape, outs, epilogue, extra=()):
    nk = grid[2]
    n_extra = len(extra)

    def body(a_ref, b_ref, *rest):
        extra_refs = rest[:n_extra]
        out_refs = rest[n_extra:n_extra + len(outs)]
        i, j, k = pl.program_id(0), pl.program_id(1), pl.program_id(2)
        part = _dot(a_ref[...].astype(BF16), b_ref[...].astype(BF16), dims)
        if nk == 1:
            epilogue(part, extra_refs, out_refs, (i, j))
            return
        acc_ref = rest[-1]

        @pl.when(k == 0)
        def _():
            acc_ref[...] = part

        @pl.when((k > 0) & (k < nk - 1))
        def _():
            acc_ref[...] += part

        @pl.when(k == nk - 1)
        def _():
            epilogue(acc_ref[...] + part, extra_refs, out_refs, (i, j))

    res = pl.pallas_call(
        body,
        name=name,
        grid=grid,
        in_specs=[a_spec, b_spec] + [s for _, s in extra],
        out_specs=[s for _, s in outs],
        out_shape=[s for s, _ in outs],
        scratch_shapes=[pltpu.VMEM(acc_shape, F32)] if nk > 1 else [],
        compiler_params=_cparams("arbitrary", "arbitrary", "arbitrary"),
    )(a, b, *[x for x, _ in extra])
    return res


def _epi_store(acc, extra_refs, out_refs, ij):
    out_refs[0][...] = acc.astype(out_refs[0].dtype)


def _rms_rows(x, g):
    r = lax.rsqrt(jnp.mean(x * x, axis=-1, keepdims=True) + EPS)
    return x * r, r


def _rmsnorm_bwd_rows(dh, x, g):
    xhat, r = _rms_rows(x, g)
    dxh = dh * g
    dx = r * (dxh - xhat * jnp.mean(dxh * xhat, axis=-1, keepdims=True))
    return dx, jnp.sum(dh * xhat, axis=0, keepdims=True)


def _rmsnorm(name, x, g, tb):
    T, Dm = x.shape

    def body(x_ref, g_ref, o_ref):
        xhat, _ = _rms_rows(x_ref[...], None)
        o_ref[...] = (xhat * g_ref[...]).astype(o_ref.dtype)

    return pl.pallas_call(
        body, name=name, grid=(T // tb,),
        in_specs=[pl.BlockSpec((tb, Dm), lambda i: (i, 0)), pl.BlockSpec((1, Dm), lambda i: (0, 0))],
        out_specs=pl.BlockSpec((tb, Dm), lambda i: (i, 0)),
        out_shape=jax.ShapeDtypeStruct((T, Dm), BF16),
        compiler_params=_cparams("parallel"),
    )(x, g)


def _head_norm(x, gm, gain):
    ms = _dot3_right(x * x, gm)
    r = lax.rsqrt(ms + EPS)
    return x * r * gain, x * r


def _head_norm_bwd(dy, x, gm, gain):
    ms = _dot3_right(x * x, gm)
    r = lax.rsqrt(ms + EPS)
    xhat = x * r
    dxh = dy * gain
    dx = r * (dxh - xhat * _dot3_right(dxh * xhat, gm))
    return dx, jnp.sum(dy * xhat, axis=0, keepdims=True)


def _log_sigmoid(z):
    return jnp.minimum(z, 0.0) - jnp.log(1.0 + jnp.exp(-jnp.abs(z)))


def _prep_fwd(proj, gains, bfor, tril, gm64, gm128, T, tb):
    nb = T // tb

    def body(qa_ref, qf_ref, kf_ref, vf_ref, qm_ref, ka_ref, va_ref, fl_ref, gains_ref, bfor_ref, tril_ref,
             gm64_ref, gm128_ref,
             qa_o, qf_o, kf_o, vf_o, qm_o, kad_o, vad_o, qaug_o, kaug_o, carry):
        i = pl.program_id(0)
        gm64v = gm64_ref[...]
        gm128v = gm128_ref[...]
        lane = _lane((tb, LANES))

        def norm512(src, dst, row, gm, scale=1.0):
            gain = gains_ref[row:row + 1, :]
            for c in range(4):
                sl = slice(c * LANES, (c + 1) * LANES)
                y, _ = _head_norm(src[:, sl], gm, gain)
                dst[:, sl] = (y * scale).astype(dst.dtype)

        norm512(qa_ref, qa_o, 0, gm64v)
        norm512(qf_ref, qf_o, 2, gm64v, FOX_SCALE)
        norm512(kf_ref, kf_o, 3, gm64v)
        norm512(qm_ref, qm_o, 4, gm128v)
        vf_o[...] = vf_ref[...].astype(vf_o.dtype)

        ka_n, _ = _head_norm(ka_ref[...], gm64v, gains_ref[1:2, :])
        ka_r = pltpu.roll(ka_n, 64, 1)
        va = va_ref[...]
        va_r = pltpu.roll(va, 64, 1)
        lo = lane < 64
        kad_o[0] = jnp.where(lo, ka_n, ka_r).astype(kad_o.dtype)
        kad_o[1] = jnp.where(lo, ka_r, ka_n).astype(kad_o.dtype)
        vad_o[0] = jnp.where(lo, va, va_r).astype(vad_o.dtype)
        vad_o[1] = jnp.where(lo, va_r, va).astype(vad_o.dtype)

        @pl.when(i == 0)
        def _():
            carry[...] = jnp.zeros_like(carry)

        logf = jnp.where(lane < FOX_HEADS, _log_sigmoid(fl_ref[...] + bfor_ref[...]), 0.0)
        c = _dot3_left(tril_ref[...], logf) + carry[0:1, :]
        carry[...] = jnp.broadcast_to(c[tb - 1:tb, :], carry.shape)
        for pair in range(FOX_HEADS // 2):
            qaug = jnp.zeros((tb, LANES), F32)
            kaug = jnp.zeros((tb, LANES), F32)
            for sub in range(2):
                col = jnp.sum(jnp.where(lane == 2 * pair + sub, c, 0.0), axis=1, keepdims=True)
                pieces = [p.astype(F32) for p in _split3(col)]
                base = AUG_STRIDE * sub
                for e in range(3):
                    qaug = jnp.where(lane == base + AUG_C + e, pieces[e], qaug)
                    kaug = jnp.where(lane == base + AUG_NEG_C + e, -pieces[e], kaug)
                qaug = jnp.where((lane >= base + AUG_NEG_C) & (lane < base + AUG_NEG_C + 3), 1.0, qaug)
                ones_k = ((lane >= base + AUG_C) & (lane < base + AUG_C + 3)) | (
                    (lane >= base + AUG_STAT) & (lane < base + AUG_STAT + 3))
                kaug = jnp.where(ones_k, 1.0, kaug)
            sl = slice(pair * LANES, (pair + 1) * LANES)
            qaug_o[:, sl] = qaug.astype(BF16)
            kaug_o[:, sl] = kaug.astype(BF16)

    def seg(width, start):
        return pl.BlockSpec((tb, width), lambda i, s=start // width: (i, s))

    const = lambda shape: pl.BlockSpec(shape, lambda i: tuple(0 for _ in shape))
    rows512 = pl.BlockSpec((tb, 512), lambda i: (i, 0))
    outs = pl.pallas_call(
        body, name="prep_fwd", grid=(nb,),
        in_specs=[seg(512, C_QA), seg(512, C_QF), seg(512, C_KF), seg(512, C_VF), seg(512, C_QM),
                  seg(128, C_KA), seg(128, C_VA), seg(128, C_FL),
                  const((8, LANES)), const((1, LANES)), const((tb, tb)), const((LANES, LANES)), const((LANES, LANES))],
        out_specs=[rows512, rows512, rows512, rows512, rows512,
                   pl.BlockSpec((2, tb, LANES), lambda i: (0, i, 0)), pl.BlockSpec((2, tb, LANES), lambda i: (0, i, 0)),
                   rows512, rows512],
        out_shape=[jax.ShapeDtypeStruct((T, 512), BF16)] * 5
        + [jax.ShapeDtypeStruct((2, T, LANES), BF16)] * 2
        + [jax.ShapeDtypeStruct((T, 512), BF16)] * 2,
        scratch_shapes=[pltpu.VMEM((8, LANES), F32)],
        compiler_params=_cparams("arbitrary"),
    )(proj, proj, proj, proj, proj, proj, proj, proj, gains, bfor, tril, gm64, gm128)
    return outs


def _prep_bwd(proj, dqa, dkad, dvad, dqf, dkf, dvf, dqm, dqf_aug, dkf_aug, gains, bfor, triu, gm64, gm128, T, tb):
    nb = T // tb

    def body(qa_ref, qf_ref, kf_ref, qm_ref, ka_ref, fl_ref,
             dqa_ref, dkad_ref, dvad_ref, dqf_ref, dkf_ref, dvf_ref, dqm_ref, dqfa_ref, dkfa_ref,
             gains_ref, bfor_ref, triu_ref, gm64_ref, gm128_ref,
             dlo_o, gacc_o, carry):
        i = pl.program_id(0)
        gm64v = gm64_ref[...]
        gm128v = gm128_ref[...]
        lane = _lane((tb, LANES))

        @pl.when(i == 0)
        def _():
            carry[...] = jnp.zeros_like(carry)
            gacc_o[...] = jnp.zeros_like(gacc_o)

        def norm512_bwd(dsrc, xsrc, col0, row, gm):
            gain = gains_ref[row:row + 1, :]
            gsum = jnp.zeros((1, LANES), F32)
            for c in range(4):
                sl = slice(c * LANES, (c + 1) * LANES)
                dx, dg = _head_norm_bwd(dsrc[:, sl], xsrc[:, sl], gm, gain)
                dlo_o[:, col0 + c * LANES:col0 + (c + 1) * LANES] = dx.astype(dlo_o.dtype)
                gsum = gsum + dg
            gacc_o[row:row + 1, :] += gsum

        norm512_bwd(dqa_ref, qa_ref, C_QA, 0, gm64v)
        norm512_bwd(dqf_ref, qf_ref, C_QF, 2, gm64v)
        norm512_bwd(dkf_ref, kf_ref, C_KF, 3, gm64v)
        norm512_bwd(dqm_ref, qm_ref, C_QM, 4, gm128v)
        dlo_o[:, C_VF:C_VF + 512] = dvf_ref[...].astype(dlo_o.dtype)

        lo = lane < 64

        def fold(ref):
            f0 = ref[0] + pltpu.roll(ref[0], 64, 1)
            f1 = ref[1] + pltpu.roll(ref[1], 64, 1)
            return jnp.where(lo, f0, f1)

        dka, dg = _head_norm_bwd(fold(dkad_ref), ka_ref[...], gm64v, gains_ref[1:2, :])
        gacc_o[1:2, :] += dg
        dlo_o[:, C_KA:C_KA + LANES] = dka.astype(dlo_o.dtype)
        dlo_o[:, C_VA:C_VA + LANES] = fold(dvad_ref).astype(dlo_o.dtype)

        dc = jnp.zeros((tb, LANES), F32)
        for pair in range(FOX_HEADS // 2):
            sl = slice(pair * LANES, (pair + 1) * LANES)
            rows_sum, cols_sum = dqfa_ref[:, sl], dkfa_ref[:, sl]
            for sub in range(2):
                diff = (jnp.where(lane == AUG_STRIDE * sub + AUG_C, rows_sum, 0.0)
                        - jnp.where(lane == AUG_STRIDE * sub + AUG_NEG_C, cols_sum, 0.0))
                dc = jnp.where(lane == 2 * pair + sub, jnp.sum(diff, axis=1, keepdims=True), dc)
        dlogf = _dot3_left(triu_ref[...], dc) + carry[0:1, :]
        carry[...] = jnp.broadcast_to(dlogf[0:1, :], carry.shape)
        z = fl_ref[...] + bfor_ref[...]
        dfl = jnp.where(lane < FOX_HEADS, dlogf / (1.0 + jnp.exp(z)), 0.0)
        gacc_o[5:6, :] += jnp.sum(dfl, axis=0, keepdims=True)
        dlo_o[:, C_FL:C_FL + LANES] = dfl.astype(dlo_o.dtype)
        dlo_o[:, C_FL + LANES:C_FL + 2 * LANES] = jnp.zeros((tb, LANES), dlo_o.dtype)

    rev = lambda i: nb - 1 - i

    def seg(width, start):
        return pl.BlockSpec((tb, width), lambda i, s=start // width: (rev(i), s))

    const = lambda shape: pl.BlockSpec(shape, lambda i: tuple(0 for _ in shape))
    rows512 = pl.BlockSpec((tb, 512), lambda i: (rev(i), 0))
    dup = pl.BlockSpec((2, tb, LANES), lambda i: (0, rev(i), 0))
    return pl.pallas_call(
        body, name="prep_bwd", grid=(nb,),
        in_specs=[seg(512, C_QA), seg(512, C_QF), seg(512, C_KF), seg(512, C_QM), seg(128, C_KA), seg(128, C_FL),
                  rows512, dup, dup, rows512, rows512, rows512, rows512, rows512, rows512,
                  const((8, LANES)), const((1, LANES)), const((tb, tb)), const((LANES, LANES)), const((LANES, LANES))],
        out_specs=[pl.BlockSpec((tb, LO_W), lambda i: (rev(i), 0)), const((8, LANES))],
        out_shape=[jax.ShapeDtypeStruct((T, LO_W), BF16), jax.ShapeDtypeStruct((8, LANES), F32)],
        scratch_shapes=[pltpu.VMEM((8, LANES), F32)],
        compiler_params=_cparams("arbitrary"),
    )(proj, proj, proj, proj, proj, proj, dqa, dkad, dvad, dqf, dkf, dvf, dqm, dqf_aug, dkf_aug,
      gains, bfor, triu, gm64, gm128)


FOX_SCALE = FOX_HEAD_DIM ** -0.5
AUG_STRIDE = 16
AUG_C = 0
AUG_NEG_C = 3
AUG_STAT = 6
FOX_TQ, FOX_TK = 1024, 1024
FOX_BWD_TQ, FOX_BWD_TK = 1024, 1024


def _fox_head_mask(sub, rows):
    lane = _lane((rows, 2 * LANES))
    main = (lane >= 64 * sub) & (lane < 64 * sub + 64)
    aug = (lane >= LANES + AUG_STRIDE * sub) & (lane < LANES + AUG_STRIDE * (sub + 1))
    return main | aug


def _fox_fwd(q, qaug, k, kaug, v, T, tq, tk):
    nq, nk = T // tq, T // tk
    rep = tk // LANES
    last_of = lambda i: (i * tq + tq - 1) // tk

    def body(q_ref, qa_ref, k_ref, ka_ref, v_ref, o_ref, qab_ref, m_s, acc_s):
        p_, i, j = pl.program_id(0), pl.program_id(1), pl.program_id(2)
        last = last_of(i)

        @pl.when(j == 0)
        def _():
            m_s[...] = jnp.full(m_s.shape, NEG, F32)
            acc_s[...] = jnp.zeros_like(acc_s)

        def step(diagonal):
            q2 = jnp.concatenate([q_ref[...], qa_ref[...]], axis=1)
            k2 = jnp.concatenate([k_ref[...], ka_ref[...]], axis=1)
            v2 = jnp.concatenate([v_ref[...], ka_ref[...]], axis=1)
            if diagonal:
                causal = (lax.broadcasted_iota(jnp.int32, (tq, tk), 1) + j * tk
                          <= lax.broadcasted_iota(jnp.int32, (tq, tk), 0) + i * tq)
            scores = [_dot(jnp.where(_fox_head_mask(sub, tq), q2, jnp.zeros_like(q2)), k2, NT) for sub in range(2)]
            for sub in range(2):
                s = scores[sub]
                if diagonal:
                    s = jnp.where(causal, s, NEG)
                m_prev = m_s[sub]
                m_next = jnp.maximum(m_prev, jnp.max(s, axis=1, keepdims=True))
                p = jnp.exp(s - jnp.tile(m_next, (1, rep)))
                alpha = jnp.exp(m_prev - m_next)
                m_s[sub] = m_next
                acc_s[sub] = acc_s[sub] * jnp.tile(alpha, (1, 2)) + _dot(p.astype(BF16), v2)

        @pl.when(j == last)
        def _():
            step(True)

        @pl.when(j < last)
        def _():
            step(False)

        @pl.when(j == nk - 1)
        def _():
            lane = _lane((tq, LANES))
            outs = []
            qab = qa_ref[...].astype(F32)
            for sub in range(2):
                acc = acc_s[sub]
                base = AUG_STRIDE * sub
                l = jnp.sum(jnp.where(lane == base + AUG_C, acc[:, LANES:], 0.0), axis=1, keepdims=True)
                outs.append(acc[:, :LANES] / l)
                lse = jnp.max(m_s[sub], axis=1, keepdims=True) + jnp.log(l)
                pieces = _split3(-lse)
                for e in range(3):
                    qab = jnp.where(lane == base + AUG_STAT + e, pieces[e].astype(F32), qab)
            o_ref[...] = jnp.where(lane < 64, outs[0], outs[1]).astype(o_ref.dtype)
            qab_ref[...] = qab.astype(BF16)

    qspec = pl.BlockSpec((tq, LANES), lambda p, i, j: (i, p))
    kspec = pl.BlockSpec((tk, LANES), lambda p, i, j: (jnp.minimum(j, last_of(i)), p))
    return pl.pallas_call(
        body, name="fox_fwd", grid=(4, nq, nk),
        in_specs=[qspec, qspec, kspec, kspec, kspec],
        out_specs=[qspec, qspec],
        out_shape=[jax.ShapeDtypeStruct((T, 512), BF16), jax.ShapeDtypeStruct((T, 512), BF16)],
        scratch_shapes=[pltpu.VMEM((2, tq, LANES), F32), pltpu.VMEM((2, tq, 2 * LANES), F32)],
        compiler_params=_cparams("parallel", "parallel", "arbitrary"),
    )(q, qaug, k, kaug, v)


def _fox_bwd(q, qaug, k, kaug, v, do, doaug, T, tq, tk):
    nq, nk = T // tq, T // tk
    first_of = lambda j: (j * tk) // tq

    def body(q_ref, qa_ref, k_ref, ka_ref, v_ref, do_ref, doa_ref,
             dq_ref, dqa_ref, dk_ref, dka_ref, dv_ref, dk_s, dv_s):
        p_, j, i = pl.program_id(0), pl.program_id(1), pl.program_id(2)
        masked = i * tq < (j + 1) * tk - 1

        @pl.when((j == 0) & (i == 0))
        def _():
            dq_ref[...] = jnp.zeros_like(dq_ref)
            dqa_ref[...] = jnp.zeros_like(dqa_ref)

        @pl.when(i == 0)
        def _():
            dk_s[...] = jnp.zeros_like(dk_s)
            dv_s[...] = jnp.zeros_like(dv_s)

        def step(diagonal):
            q2 = jnp.concatenate([q_ref[...], qa_ref[...]], axis=1)
            k2 = jnp.concatenate([k_ref[...], ka_ref[...]], axis=1)
            v2 = jnp.concatenate([v_ref[...], ka_ref[...]], axis=1)
            do2 = jnp.concatenate([do_ref[...], doa_ref[...]], axis=1)
            if diagonal:
                causal = (lax.broadcasted_iota(jnp.int32, (tq, tk), 1) + j * tk
                          <= lax.broadcasted_iota(jnp.int32, (tq, tk), 0) + i * tq)
            qh = [jnp.where(_fox_head_mask(sub, tq), q2, jnp.zeros_like(q2)) for sub in range(2)]
            doh = [jnp.where(_fox_head_mask(sub, tq), do2, jnp.zeros_like(do2)) for sub in range(2)]
            scores = [_dot(qh[sub], k2, NT) for sub in range(2)]
            dps = [_dot(doh[sub], v2, NT) for sub in range(2)]
            dqs = []
            for sub in range(2):
                s = scores[sub]
                if diagonal:
                    s = jnp.where(causal, s, NEG)
                p = jnp.exp(s)
                dsb = (p * dps[sub]).astype(BF16)
                dv_s[...] += _dot(p.astype(BF16), doh[sub][:, :LANES], TN)
                dk_s[...] += _dot(dsb, qh[sub], TN)
                dqs.append(_dot(dsb, k2))
            dq2 = jnp.where(_fox_head_mask(0, tq), dqs[0], dqs[1])
            qrows = pl.ds(pl.multiple_of(i * tq, tq), tq)
            dq_ref[qrows, :] += dq2[:, :LANES] * FOX_SCALE
            dqa_ref[qrows, :] += dq2[:, LANES:]

        @pl.when((i >= first_of(j)) & masked)
        def _():
            step(True)

        @pl.when((i >= first_of(j)) & jnp.logical_not(masked))
        def _():
            step(False)

        @pl.when(i == nq - 1)
        def _():
            dk_ref[...] = dk_s[:, :LANES]
            dka_ref[...] = dk_s[:, LANES:]
            dv_ref[...] = dv_s[...]

    qspec = pl.BlockSpec((tq, LANES), lambda p, j, i: (jnp.maximum(i, first_of(j)), p))
    kspec = pl.BlockSpec((tk, LANES), lambda p, j, i: (j, p))
    resident = pl.BlockSpec((T, LANES), lambda p, j, i: (0, p))
    return pl.pallas_call(
        body, name="fox_bwd", grid=(4, nk, nq),
        in_specs=[qspec, qspec, kspec, kspec, kspec, qspec, qspec],
        out_specs=[resident, resident, kspec, kspec, kspec],
        out_shape=[jax.ShapeDtypeStruct((T, 512), F32)] * 5,
        scratch_shapes=[pltpu.VMEM((tk, 2 * LANES), F32), pltpu.VMEM((tk, LANES), F32)],
        compiler_params=_cparams("arbitrary", "arbitrary", "arbitrary"),
    )(q, qaug, k, kaug, v, do, doaug)


SWA_SUB = 4
SWA_TB = SWA_SUB * WINDOW


def _t5_bucket_matrix():
    t = jnp.arange(WINDOW)[:, None] + WINDOW
    s = jnp.arange(2 * WINDOW)[None, :]
    max_exact = REL_BUCKETS // 2
    d = jnp.maximum(t - s, 0)
    df = jnp.maximum(d, 1).astype(F32)
    large = max_exact + (jnp.log(df / max_exact) / math.log(REL_MAX_DIST / max_exact)
                         * (REL_BUCKETS - max_exact)).astype(jnp.int32)
    large = jnp.minimum(large, REL_BUCKETS - 1)
    return jnp.where(d < max_exact, d, large).astype(jnp.int32)


def _swa_bias(rel_bias, bucket):
    def body(rel_ref, bucket_ref, o_ref):
        b = bucket_ref[...]
        for h in range(SWA_HEADS):
            acc = jnp.zeros(b.shape, F32)
            for r in range(REL_BUCKETS):
                acc = jnp.where(b == r, rel_ref[r, h], acc)
            o_ref[h] = acc

    return pl.pallas_call(
        body, name="swa_bias",
        in_specs=[pl.BlockSpec(memory_space=pltpu.SMEM), pl.BlockSpec(memory_space=pltpu.VMEM)],
        out_specs=pl.BlockSpec(memory_space=pltpu.VMEM),
        out_shape=jax.ShapeDtypeStruct((SWA_HEADS, WINDOW, 2 * WINDOW), F32),
    )(rel_bias, bucket)


def _swa_bias_bwd(dbias, bucket):
    def body(db_ref, bucket_ref, o_ref):
        b = bucket_ref[...]
        lane = _lane((1, LANES))
        for r in range(REL_BUCKETS):
            row = jnp.zeros((1, LANES), F32)
            for h in range(SWA_HEADS):
                part = jnp.sum(jnp.where(b == r, db_ref[h], 0.0), axis=0, keepdims=True)
                tot = jnp.sum(part, axis=1, keepdims=True)
                row = jnp.where(lane == h, tot, row)
            o_ref[r:r + 1, :] = row

    return pl.pallas_call(
        body, name="swa_bias_bwd",
        in_specs=[pl.BlockSpec(memory_space=pltpu.VMEM), pl.BlockSpec(memory_space=pltpu.VMEM)],
        out_specs=pl.BlockSpec(memory_space=pltpu.VMEM),
        out_shape=jax.ShapeDtypeStruct((REL_BUCKETS, LANES), F32),
    )(dbias, bucket)


SWA_GROUP = SWA_HEADS // SWA_KV_HEADS


def _swa_valid(r, i):
    t = (lax.broadcasted_iota(jnp.int32, (SWA_GROUP * WINDOW, 2 * WINDOW), 0) & (WINDOW - 1)) + WINDOW
    s = lax.broadcasted_iota(jnp.int32, (SWA_GROUP * WINDOW, 2 * WINDOW), 1)
    dist = t - s
    band = (dist >= 0) & (dist < WINDOW)
    if r == 0:
        band = band & ((s >= WINDOW) | (i > 0))
    return band


def _swa_stack(blk):
    lane = _lane((WINDOW, LANES))
    parts = []
    for g in range(SWA_GROUP):
        b = blk[:, LANES * (g // 2):LANES * (g // 2 + 1)]
        parts.append(jnp.where((lane >= 64) if g % 2 else (lane < 64), b, jnp.zeros_like(b)))
    return jnp.concatenate(parts, axis=0)


def _swa_unstack(st):
    lane = _lane((WINDOW, LANES))
    W = WINDOW
    return jnp.concatenate([jnp.where(lane < 64, st[2 * b * W:(2 * b + 1) * W], st[(2 * b + 1) * W:(2 * b + 2) * W])
                            for b in range(2)], axis=1)


def _swa_sink_column(sink_ref, kvh):
    row = lax.broadcasted_iota(jnp.int32, (SWA_GROUP * WINDOW, 1), 0)
    col = jnp.full((SWA_GROUP * WINDOW, 1), sink_ref[SWA_GROUP * kvh + SWA_GROUP - 1], F32)
    for g in range(SWA_GROUP - 2, -1, -1):
        col = jnp.where(row < (g + 1) * WINDOW, sink_ref[SWA_GROUP * kvh + g], col)
    return col


def _swa_specs(T):
    W = WINDOW
    qspec = pl.BlockSpec((SWA_TB, 2 * LANES), lambda h, i: (i, h))
    own = pl.BlockSpec((None, SWA_TB, LANES), lambda h, i: (h, i, 0))
    prev = pl.BlockSpec((None, W, LANES), lambda h, i: (h, jnp.maximum(SWA_SUB * i - 1, 0), 0))
    stat = pl.BlockSpec((SWA_GROUP, SWA_TB, LANES), lambda h, i: (h, i, 0))
    bias = pl.BlockSpec((None, SWA_GROUP * W, 2 * W), lambda h, i: (h, 0, 0))
    return qspec, own, prev, stat, bias


def _swa_fwd(sinks, q, kad, vad, bias, T):
    nb = T // SWA_TB
    scale = SWA_HEAD_DIM ** -0.5
    W = WINDOW

    def body(sink_ref, q_ref, k_ref, kp_ref, v_ref, vp_ref, bias_ref, o_ref, lse_ref):
        kvh, i = pl.program_id(0), pl.program_id(1)
        sink = _swa_sink_column(sink_ref, kvh)
        for r in range(SWA_SUB):
            rs = slice(r * W, (r + 1) * W)
            ps = slice((r - 1) * W, r * W)
            k_own, v_own = k_ref[rs, :], v_ref[rs, :]
            k_prev = kp_ref[...] if r == 0 else k_ref[ps, :]
            v_prev = vp_ref[...] if r == 0 else v_ref[ps, :]
            qs = _swa_stack(q_ref[rs, :])
            s = jnp.concatenate([_dot(qs, k_prev, NT), _dot(qs, k_own, NT)], axis=1) * scale + bias_ref[...]
            s = jnp.where(_swa_valid(r, i), s, NEG)
            m = jnp.maximum(jnp.max(s, axis=1, keepdims=True), sink)
            p = jnp.exp(s - m)
            denom = jnp.sum(p, axis=1, keepdims=True) + jnp.exp(sink - m)
            pn = (p / denom).astype(BF16)
            o_ref[rs, :] = _swa_unstack(_dot(pn[:, :W], v_prev) + _dot(pn[:, W:], v_own)).astype(o_ref.dtype)
            lse = m + jnp.log(denom)
            for g in range(SWA_GROUP):
                lse_ref[g, rs, :] = jnp.broadcast_to(lse[g * W:(g + 1) * W], (W, LANES))

    qspec, own, prev, stat, bspec = _swa_specs(T)
    return pl.pallas_call(
        body, name="swa_fwd", grid=(SWA_KV_HEADS, nb),
        in_specs=[pl.BlockSpec(memory_space=pltpu.SMEM), qspec, own, prev, own, prev, bspec],
        out_specs=[qspec, stat],
        out_shape=[jax.ShapeDtypeStruct((T, 512), BF16), jax.ShapeDtypeStruct((SWA_HEADS, T, LANES), F32)],
        compiler_params=_cparams("parallel", "parallel"),
    )(sinks, q, kad, kad, vad, vad, bias.reshape(SWA_KV_HEADS, SWA_GROUP * W, 2 * W))


def _swa_bwd(sinks, q, kad, vad, bias, do, lse, delta, T):
    nb = T // SWA_TB
    scale = SWA_HEAD_DIM ** -0.5
    W = WINDOW

    def body(sink_ref, q_ref, k_ref, kp_ref, v_ref, vp_ref, bias_ref, do_ref, lse_ref, dl_ref,
             dq_ref, dkad_ref, dvad_ref, dbias_ref, dsk_ref):
        kvh, i = pl.program_id(0), pl.program_id(1)
        sink = _swa_sink_column(sink_ref, kvh)

        @pl.when((kvh == 0) & (i == 0))
        def _():
            dkad_ref[...] = jnp.zeros_like(dkad_ref)
            dvad_ref[...] = jnp.zeros_like(dvad_ref)

        @pl.when(i == 0)
        def _():
            dbias_ref[...] = jnp.zeros_like(dbias_ref)
            dsk_ref[...] = jnp.zeros_like(dsk_ref)

        for r in range(SWA_SUB):
            rs = slice(r * W, (r + 1) * W)
            ps = slice((r - 1) * W, r * W)
            k_own, v_own = k_ref[rs, :], v_ref[rs, :]
            k_prev = kp_ref[...] if r == 0 else k_ref[ps, :]
            v_prev = vp_ref[...] if r == 0 else v_ref[ps, :]
            qs = _swa_stack(q_ref[rs, :])
            dos = _swa_stack(do_ref[rs, :])
            lse_b = jnp.concatenate([lse_ref[g, rs, :] for g in range(SWA_GROUP)], axis=0)
            dl_b = jnp.concatenate([dl_ref[g, rs, :] for g in range(SWA_GROUP)], axis=0)
            s = jnp.concatenate([_dot(qs, k_prev, NT), _dot(qs, k_own, NT)], axis=1) * scale + bias_ref[...]
            s = jnp.where(_swa_valid(r, i), s, NEG)
            p = jnp.exp(s - jnp.tile(lse_b, (1, 2)))
            dp = jnp.concatenate([_dot(dos, v_prev, NT), _dot(dos, v_own, NT)], axis=1)
            ds = p * (dp - jnp.tile(dl_b, (1, 2)))
            sink_term = jnp.exp(sink - lse_b) * dl_b
            for g in range(SWA_GROUP):
                dbias_ref[g] += ds[g * W:(g + 1) * W]
                dsk_ref[g:g + 1, :] += jnp.sum(sink_term[g * W:(g + 1) * W], axis=0, keepdims=True)
            dsb = ds.astype(BF16)
            pb = p.astype(BF16)
            dq_ref[rs, :] = _swa_unstack((_dot(dsb[:, :W], k_prev) + _dot(dsb[:, W:], k_own)) * scale)
            own_row = pl.multiple_of(i * SWA_TB + r * W, W)
            dkad_ref[kvh, pl.ds(own_row, W), :] += _dot(dsb[:, W:], qs, TN) * scale
            dvad_ref[kvh, pl.ds(own_row, W), :] += _dot(pb[:, W:], dos, TN)
            dk_prev = _dot(dsb[:, :W], qs, TN) * scale
            dv_prev = _dot(pb[:, :W], dos, TN)
            if r == 0:
                @pl.when(i > 0)
                def _():
                    prev_row = pl.multiple_of(i * SWA_TB - W, W)
                    dkad_ref[kvh, pl.ds(prev_row, W), :] += dk_prev
                    dvad_ref[kvh, pl.ds(prev_row, W), :] += dv_prev
            else:
                prev_row = pl.multiple_of(i * SWA_TB + (r - 1) * W, W)
                dkad_ref[kvh, pl.ds(prev_row, W), :] += dk_prev
                dvad_ref[kvh, pl.ds(prev_row, W), :] += dv_prev

    qspec, own, prev, stat, bspec = _swa_specs(T)
    full = pl.BlockSpec((SWA_KV_HEADS, T, LANES), lambda h, i: (0, 0, 0))
    return pl.pallas_call(
        body, name="swa_bwd", grid=(SWA_KV_HEADS, nb),
        in_specs=[pl.BlockSpec(memory_space=pltpu.SMEM), qspec, own, prev, own, prev, bspec, qspec, stat, stat],
        out_specs=[qspec, full, full, pl.BlockSpec((SWA_GROUP, W, 2 * W), lambda h, i: (h, 0, 0)),
                   pl.BlockSpec((None, 8, LANES), lambda h, i: (h, 0, 0))],
        out_shape=[jax.ShapeDtypeStruct((T, 512), F32), jax.ShapeDtypeStruct((SWA_KV_HEADS, T, LANES), F32),
                   jax.ShapeDtypeStruct((SWA_KV_HEADS, T, LANES), F32), jax.ShapeDtypeStruct((SWA_HEADS, W, 2 * W), F32),
                   jax.ShapeDtypeStruct((SWA_KV_HEADS, 8, LANES), F32)],
        compiler_params=_cparams("arbitrary", "arbitrary"),
    )(sinks, q, kad, kad, vad, vad, bias.reshape(SWA_KV_HEADS, SWA_GROUP * W, 2 * W), do, lse, delta)


def _mem_fwd(q, mk, mv, T, tq):
    scale = MEM_HEAD_DIM ** -0.5

    def body(q_ref, k_ref, v_ref, o_ref, lse_ref):
        s = _dot(q_ref[...], k_ref[...], NT) * scale
        m = jnp.max(s, axis=1, keepdims=True)
        p = jnp.exp(s - m)
        l = jnp.sum(p, axis=1, keepdims=True)
        o_ref[...] = _dot((p / l).astype(BF16), v_ref[...]).astype(o_ref.dtype)
        lse_ref[...] = jnp.broadcast_to(m + jnp.log(l), (tq, LANES))

    qspec = pl.BlockSpec((tq, LANES), lambda h, i: (i, h))
    kspec = pl.BlockSpec((N_MEM, LANES), lambda h, i: (0, h))
    return pl.pallas_call(
        body, name="mem_fwd", grid=(MEM_HEADS, T // tq),
        in_specs=[qspec, kspec, kspec],
        out_specs=[qspec, pl.BlockSpec((None, tq, LANES), lambda h, i: (h, i, 0))],
        out_shape=[jax.ShapeDtypeStruct((T, 512), BF16), jax.ShapeDtypeStruct((MEM_HEADS, T, LANES), F32)],
        compiler_params=_cparams("parallel", "parallel"),
    )(q, mk, mv)


def _mem_bwd(q, mk, mv, do, lse, delta, T, tq):
    scale = MEM_HEAD_DIM ** -0.5
    rep = N_MEM // LANES

    def body(q_ref, k_ref, v_ref, do_ref, lse_ref, dl_ref, dq_ref, dk_ref, dv_ref):
        i = pl.program_id(1)

        @pl.when(i == 0)
        def _():
            dk_ref[...] = jnp.zeros_like(dk_ref)
            dv_ref[...] = jnp.zeros_like(dv_ref)

        qv, dov = q_ref[...], do_ref[...]
        s = _dot(qv, k_ref[...], NT) * scale
        p = jnp.exp(s - jnp.tile(lse_ref[...], (1, rep)))
        dp = _dot(dov, v_ref[...], NT)
        ds = p * (dp - jnp.tile(dl_ref[...], (1, rep)))
        dsb = ds.astype(BF16)
        dq_ref[...] = _dot(dsb, k_ref[...]) * scale
        dk_ref[...] += _dot(dsb, qv, TN) * scale
        dv_ref[...] += _dot(p.astype(BF16), dov, TN)

    qspec = pl.BlockSpec((tq, LANES), lambda h, i: (i, h))
    kspec = pl.BlockSpec((N_MEM, LANES), lambda h, i: (0, h))
    stat = pl.BlockSpec((None, tq, LANES), lambda h, i: (h, i, 0))
    return pl.pallas_call(
        body, name="mem_bwd", grid=(MEM_HEADS, T // tq),
        in_specs=[qspec, kspec, kspec, qspec, stat, stat],
        out_specs=[qspec, kspec, kspec],
        out_shape=[jax.ShapeDtypeStruct((T, 512), F32), jax.ShapeDtypeStruct((N_MEM, 512), F32),
                   jax.ShapeDtypeStruct((N_MEM, 512), F32)],
        compiler_params=_cparams("arbitrary", "arbitrary"),
    )(q, mk, mv, do, lse, delta)


def _mem_prep_fwd(mem, g_mem, w_kv, kn_gain, gm128):
    def body(mem_ref, g_ref, w_ref, kn_ref, gm_ref, memn_o, kv_o, mk_o, mv_o):
        xhat, _ = _rms_rows(mem_ref[...], None)
        memn = (xhat * g_ref[...]).astype(BF16)
        memn_o[...] = memn
        kv = _dot(memn, w_ref[...])
        kv_o[...] = kv
        gm = gm_ref[...]
        for c in range(4):
            sl = slice(c * LANES, (c + 1) * LANES)
            y, _ = _head_norm(kv[:, sl], gm, kn_ref[...])
            mk_o[:, sl] = y.astype(BF16)
        mv_o[...] = kv[:, 512:].astype(BF16)

    vm = pl.BlockSpec(memory_space=pltpu.VMEM)
    return pl.pallas_call(
        body, name="mem_prep_fwd", in_specs=[vm] * 5, out_specs=[vm] * 4,
        out_shape=[jax.ShapeDtypeStruct((N_MEM, D_MODEL), BF16), jax.ShapeDtypeStruct((N_MEM, D_MODEL), F32),
                   jax.ShapeDtypeStruct((N_MEM, 512), BF16), jax.ShapeDtypeStruct((N_MEM, 512), BF16)],
        compiler_params=pltpu.CompilerParams(vmem_limit_bytes=VMEM_LIMIT),
    )(mem, g_mem, w_kv, kn_gain, gm128)


def _mem_prep_bwd(mem, g_mem, memn, kv, w_kv, kn_gain, gm128, dmk, dmv):
    def body(mem_ref, g_ref, memn_ref, kv_ref, w_ref, kn_ref, gm_ref, dmk_ref, dmv_ref, dw_o, dg_o, dkn_o, dkv_s):
        gm = gm_ref[...]
        dkn = jnp.zeros((1, LANES), F32)
        for c in range(4):
            sl = slice(c * LANES, (c + 1) * LANES)
            dx, dg = _head_norm_bwd(dmk_ref[:, sl], kv_ref[:, sl], gm, kn_ref[...])
            dkv_s[:, sl] = dx.astype(BF16)
            dkn = dkn + dg
        dkn_o[...] = dkn
        dkv_s[:, 512:] = dmv_ref[...].astype(BF16)
        dkv = dkv_s[...]
        dw_o[...] = _dot(memn_ref[...], dkv, TN)
        dmemn = _dot(dkv, w_ref[...], NT)
        xhat, _ = _rms_rows(mem_ref[...], None)
        dg_o[...] = jnp.sum(dmemn * xhat, axis=0, keepdims=True)

    vm = pl.BlockSpec(memory_space=pltpu.VMEM)
    return pl.pallas_call(
        body, name="mem_prep_bwd", in_specs=[vm] * 9, out_specs=[vm] * 3,
        out_shape=[jax.ShapeDtypeStruct((D_MODEL, D_MODEL), F32), jax.ShapeDtypeStruct((1, D_MODEL), F32),
                   jax.ShapeDtypeStruct((1, LANES), F32)],
        scratch_shapes=[pltpu.VMEM((N_MEM, D_MODEL), BF16)],
        compiler_params=pltpu.CompilerParams(vmem_limit_bytes=VMEM_LIMIT),
    )(mem, g_mem, memn, kv, w_kv, kn_gain, gm128, dmk, dmv)


SLOT_O = D_MODEL // N_SHARD


def _merge_fwd(proj, b_gate, o3, w3, T, tb):
    def body(gl_ref, bg_ref, oa_ref, of_ref, om_ref, wa_ref, wf_ref, wm_ref, out_ref):
        o_refs = (oa_ref, of_ref, om_ref)
        w_refs = (wa_ref, wf_ref, wm_ref)
        for n in range(N_SHARD):
            acc = jnp.zeros((tb, SLOT_O), F32)
            for b in range(3):
                c0 = b * D_MODEL + n * SLOT_O
                g = jax.nn.sigmoid(gl_ref[:, c0:c0 + SLOT_O] + bg_ref[:, c0:c0 + SLOT_O])
                acc = acc + g * _dot(o_refs[b][...], w_refs[b][n])
            out_ref[:, n * SLOT_O:(n + 1) * SLOT_O] = acc.astype(out_ref.dtype)

    rows = pl.BlockSpec((tb, 512), lambda i: (i, 0))
    wspec = pl.BlockSpec((N_SHARD, 512, SLOT_O), lambda i: (0, 0, 0))
    return pl.pallas_call(
        body, name="merge_fwd", grid=(T // tb,),
        in_specs=[pl.BlockSpec((tb, GATE_W), lambda i: (i, 1)), pl.BlockSpec((1, GATE_W), lambda i: (0, 0)),
                  rows, rows, rows, wspec, wspec, wspec],
        out_specs=pl.BlockSpec((tb, D_MODEL), lambda i: (i, 0)),
        out_shape=jax.ShapeDtypeStruct((T, D_MODEL), BF16),
        compiler_params=_cparams("parallel"),
    )(proj, b_gate, *o3, *w3)


def _merge_bwd(proj, b_gate, o3, w3, dmerged, T, tb):
    heads = (SWA_HEADS, FOX_HEADS, MEM_HEADS)

    def body(gl_ref, bg_ref, oa_ref, of_ref, om_ref, wa_ref, wf_ref, wm_ref, dm_ref,
             dgl_o, doa_o, dof_o, dom_o, dla_o, dlf_o, dlm_o, dwa_o, dwf_o, dwm_o, dbg_o):
        i = pl.program_id(0)
        o_refs = (oa_ref, of_ref, om_ref)
        w_refs = (wa_ref, wf_ref, wm_ref)
        do_refs = (doa_o, dof_o, dom_o)
        dl_refs = (dla_o, dlf_o, dlm_o)
        dw_refs = (dwa_o, dwf_o, dwm_o)

        @pl.when(i == 0)
        def _():
            for r in dw_refs:
                r[...] = jnp.zeros_like(r)
            dbg_o[...] = jnp.zeros_like(dbg_o)

        lane = _lane((tb, LANES))
        for b in range(3):
            ob = o_refs[b][...]
            do = jnp.zeros((tb, 512), F32)
            for n in range(N_SHARD):
                c0 = b * D_MODEL + n * SLOT_O
                g = jax.nn.sigmoid(gl_ref[:, c0:c0 + SLOT_O] + bg_ref[:, c0:c0 + SLOT_O])
                dm = dm_ref[:, n * SLOT_O:(n + 1) * SLOT_O]
                y = _dot(ob, w_refs[b][n])
                dgl = dm * y * g * (1.0 - g)
                dgl_o[:, c0:c0 + SLOT_O] = dgl.astype(dgl_o.dtype)
                dbg_o[:, c0:c0 + SLOT_O] += jnp.sum(dgl, axis=0, keepdims=True)
                dy = (dm * g).astype(BF16)
                do = do + _dot(dy, w_refs[b][n], NT)
                dw_refs[b][n] += _dot(ob, dy, TN)
            do_refs[b][...] = do.astype(BF16)
            prod = do * ob.astype(F32)
            for c in range(4):
                blk = prod[:, c * LANES:(c + 1) * LANES]
                if heads[b] == 8:
                    lo = jnp.sum(jnp.where(lane < 64, blk, 0.0), axis=1, keepdims=True)
                    hi = jnp.sum(jnp.where(lane >= 64, blk, 0.0), axis=1, keepdims=True)
                    if b == 1:
                        aug = jnp.zeros((tb, LANES), F32)
                        for sub, dl in enumerate((lo, hi)):
                            for e, piece in enumerate(_split3(-dl)):
                                aug = jnp.where(lane == AUG_STRIDE * sub + AUG_C + e, piece.astype(F32), aug)
                        dl_refs[b][:, c * LANES:(c + 1) * LANES] = aug.astype(BF16)
                    else:
                        dl_refs[b][2 * c] = jnp.broadcast_to(lo, (tb, LANES))
                        dl_refs[b][2 * c + 1] = jnp.broadcast_to(hi, (tb, LANES))
                else:
                    dl_refs[b][c] = jnp.broadcast_to(jnp.sum(blk, axis=1, keepdims=True), (tb, LANES))

    rows = pl.BlockSpec((tb, 512), lambda i: (i, 0))
    wspec = pl.BlockSpec((N_SHARD, 512, SLOT_O), lambda i: (0, 0, 0))
    stat = lambda h: pl.BlockSpec((h, tb, LANES), lambda i: (0, i, 0))
    return pl.pallas_call(
        body, name="merge_bwd", grid=(T // tb,),
        in_specs=[pl.BlockSpec((tb, GATE_W), lambda i: (i, 1)), pl.BlockSpec((1, GATE_W), lambda i: (0, 0)),
                  rows, rows, rows, wspec, wspec, wspec, pl.BlockSpec((tb, D_MODEL), lambda i: (i, 0))],
        out_specs=[pl.BlockSpec((tb, GATE_W), lambda i: (i, 0)), rows, rows, rows,
                   stat(8), rows, stat(4), wspec, wspec, wspec, pl.BlockSpec((1, GATE_W), lambda i: (0, 0))],
        out_shape=[jax.ShapeDtypeStruct((T, GATE_W), BF16)] + [jax.ShapeDtypeStruct((T, 512), BF16)] * 3
        + [jax.ShapeDtypeStruct((8, T, LANES), F32), jax.ShapeDtypeStruct((T, 512), BF16),
           jax.ShapeDtypeStruct((4, T, LANES), F32)]
        + [jax.ShapeDtypeStruct((N_SHARD, 512, SLOT_O), F32)] * 3 + [jax.ShapeDtypeStruct((1, GATE_W), F32)],
        compiler_params=_cparams("arbitrary"),
    )(proj, b_gate, *o3, *w3, dmerged)


def _local_step(x, mem, tgt, small, wc, w_kv, w_o3, w_out, w_up, w_down, reducer):
    T = x.shape[0]
    tm = min(512, T)
    tile2 = lambda v: jnp.tile(v.reshape(1, -1), (1, LANES // v.size))
    gains = jnp.concatenate([tile2(small["qn_swa"]), tile2(small["kn_swa"]), tile2(small["qn_fox"]),
                             tile2(small["kn_fox"]), tile2(small["qn_mem"]), jnp.zeros((3, LANES), F32)], axis=0)
    kn_mem = small["kn_mem"].reshape(1, LANES)
    bfor = jnp.pad(small["b_forget"].reshape(1, -1), ((0, 0), (0, LANES - FOX_HEADS)))
    gm64 = _group_mean_matrix(64)
    gm128 = _group_mean_matrix(128)
    tb_prep = min(256, T)
    ones = jnp.ones((tb_prep, tb_prep), F32)
    tril = jnp.tril(ones).astype(BF16)
    triu = jnp.triu(ones).astype(BF16)
    bucket = _t5_bucket_matrix()
    g_mix, g_mlp, g_mem = small["g_mix"], small["g_mlp"], small["g_mem"]
    b_gate = small["b_gate"]
    sinks = small["sink_swa"].reshape(-1)

    tl = min(1024, T)
    sq = pl.BlockSpec((tl, D_MODEL), lambda i, j, k: (i, j))
    h = _rmsnorm("rms_mix", x, g_mix, tm)
    (proj,) = _matmul(
        "mm_proj", h, wc, dims=NN, grid=(T // tl, PROJ_W // D_MODEL, 1),
        a_spec=pl.BlockSpec((tl, D_MODEL), lambda i, j, k: (i, 0)),
        b_spec=pl.BlockSpec((D_MODEL, D_MODEL), lambda i, j, k: (0, j)),
        acc_shape=(tl, D_MODEL),
        outs=[(jax.ShapeDtypeStruct((T, PROJ_W), F32), sq)],
        epilogue=_epi_store)
    qa, qf, kf, vf, qm, kad, vad, qf_aug, kf_aug = _prep_fwd(proj, gains, bfor, tril, gm64, gm128, T, tb_prep)
    bias = _swa_bias(small["rel_bias"], bucket)
    o_swa, lse_swa = _swa_fwd(sinks, qa, kad, vad, bias, T)
    o_fox, qf_aug_bwd = _fox_fwd(qf, qf_aug, kf, kf_aug, vf, T, min(FOX_TQ, T), min(FOX_TK, T))
    memn, kv, mk, mv = _mem_prep_fwd(mem, g_mem, w_kv, kn_mem, gm128)
    o_mem, lse_mem = _mem_fwd(qm, mk, mv, T, tm)
    o3 = (o_swa, o_fox, o_mem)
    merged = _merge_fwd(proj, b_gate, o3, w_o3, T, min(256, T))

    def epi_residual(acc, extra_refs, out_refs, ij):
        out_refs[0][...] = extra_refs[0][...] + acc

    row_full = pl.BlockSpec((tm, D_MODEL), lambda i, j, k: (i, 0))
    row_big = pl.BlockSpec((tl, D_MODEL), lambda i, j, k: (i, 0))
    whole = pl.BlockSpec((D_MODEL, D_MODEL), lambda i, j, k: (0, 0))
    (x2,) = _matmul(
        "mm_out", merged, w_out, dims=NN, grid=(T // tl, 1, 1),
        a_spec=row_big, b_spec=whole,
        acc_shape=(tl, D_MODEL), extra=[(x, row_big)],
        outs=[(jax.ShapeDtypeStruct((T, D_MODEL), F32), row_big)], epilogue=epi_residual)
    hm = _rmsnorm("rms_mlp", x2, g_mlp, tm)

    def epi_relu2(acc, extra_refs, out_refs, ij):
        out_refs[0][...] = acc
        r = jnp.maximum(acc, 0.0)
        out_refs[1][...] = (r * r).astype(BF16)

    up, u = _matmul(
        "mm_up", hm, w_up, dims=NN, grid=(T // tl, N_SHARD, 1),
        a_spec=row_big, b_spec=pl.BlockSpec((None, D_MODEL, D_MODEL), lambda i, j, k: (j, 0, 0)),
        acc_shape=(tl, D_MODEL),
        outs=[(jax.ShapeDtypeStruct((T, D_FF), F32), sq), (jax.ShapeDtypeStruct((T, D_FF), BF16), sq)],
        epilogue=epi_relu2)

    def epi_loss(acc, extra_refs, out_refs, ij):
        y = extra_refs[0][...] + acc
        err = y - extra_refs[1][...]
        dyv = err * (1.0 / D_MODEL)
        out_refs[0][...] = dyv
        out_refs[2][...] = dyv.astype(BF16)
        sq = jnp.sum(jnp.sum(err * err, axis=1, keepdims=True), axis=0, keepdims=True)

        @pl.when(ij[0] == 0)
        def _():
            out_refs[1][...] = jnp.zeros_like(out_refs[1])

        out_refs[1][...] += jnp.broadcast_to(sq, out_refs[1].shape)

    kblk = pl.BlockSpec((tl, D_MODEL), lambda i, j, k: (i, k))
    dy, loss_acc, dy_bf = _matmul(
        "mm_down", u, w_down, dims=NN, grid=(T // tl, 1, N_SHARD),
        a_spec=kblk, b_spec=pl.BlockSpec((D_MODEL, D_MODEL), lambda i, j, k: (k, 0)),
        acc_shape=(tl, D_MODEL), extra=[(x2, row_big), (tgt, row_big)],
        outs=[(jax.ShapeDtypeStruct((T, D_MODEL), F32), row_big),
              (jax.ShapeDtypeStruct((8, LANES), F32), pl.BlockSpec((8, LANES), lambda i, j, k: (0, 0))),
              (jax.ShapeDtypeStruct((T, D_MODEL), BF16), row_big)],
        epilogue=epi_loss)
    loss = loss_acc[0, 0] * (0.5 / D_MODEL)

    def epi_dup(acc, extra_refs, out_refs, ij):
        out_refs[0][...] = (acc * (2.0 * jnp.maximum(extra_refs[0][...], 0.0))).astype(BF16)

    (dup,) = _matmul(
        "mm_dup", dy_bf, w_down, dims=NT, grid=(T // tl, N_SHARD, 1),
        a_spec=row_big, b_spec=pl.BlockSpec((D_MODEL, D_MODEL), lambda i, j, k: (j, 0)),
        acc_shape=(tl, D_MODEL), extra=[(up, sq)],
        outs=[(jax.ShapeDtypeStruct((T, D_FF), BF16), sq)], epilogue=epi_dup)

    nkt = T // tl
    t_rows = pl.BlockSpec((tl, D_MODEL), lambda i, j, k: (k, i))
    t_cols = pl.BlockSpec((tl, D_MODEL), lambda i, j, k: (k, j))
    (d_w_down,) = _matmul(
        "mm_dw_down", u, dy_bf, dims=TN, grid=(N_SHARD, 1, nkt),
        a_spec=t_rows, b_spec=t_cols, acc_shape=(D_MODEL, D_MODEL),
        outs=[(jax.ShapeDtypeStruct((D_FF, D_MODEL), F32), pl.BlockSpec((D_MODEL, D_MODEL), lambda i, j, k: (i, 0)))],
        epilogue=_epi_store)
    (d_w_up,) = _matmul(
        "mm_dw_up", hm, dup, dims=TN, grid=(1, N_SHARD, nkt),
        a_spec=t_rows, b_spec=t_cols, acc_shape=(D_MODEL, D_MODEL),
        outs=[(jax.ShapeDtypeStruct((N_SHARD, D_MODEL, D_MODEL), F32),
               pl.BlockSpec((None, D_MODEL, D_MODEL), lambda i, j, k: (j, 0, 0)))],
        epilogue=_epi_store)

    def epi_rms_bwd(acc, extra_refs, out_refs, ij):
        dx, dg = _rmsnorm_bwd_rows(acc, extra_refs[0][...], extra_refs[1][...])
        out_refs[0][...] = dx + extra_refs[2][...]

        @pl.when(ij[0] == 0)
        def _():
            out_refs[1][...] = jnp.zeros_like(out_refs[1])

        out_refs[1][...] += dg

    gain_spec = pl.BlockSpec((1, D_MODEL), lambda i, j, k: (0, 0))
    dx2, d_g_mlp = _matmul(
        "mm_dhm", dup, w_up, dims=NT, grid=(T // tl, 1, N_SHARD),
        a_spec=kblk, b_spec=pl.BlockSpec((None, D_MODEL, D_MODEL), lambda i, j, k: (k, 0, 0)),
        acc_shape=(tl, D_MODEL), extra=[(x2, row_big), (g_mlp, gain_spec), (dy, row_big)],
        outs=[(jax.ShapeDtypeStruct((T, D_MODEL), F32), row_big), (jax.ShapeDtypeStruct((1, D_MODEL), F32), gain_spec)],
        epilogue=epi_rms_bwd)

    (dmerged,) = _matmul(
        "mm_dmerged", dx2, w_out, dims=NT, grid=(T // tl, 1, 1),
        a_spec=row_big, b_spec=whole,
        acc_shape=(tl, D_MODEL), outs=[(jax.ShapeDtypeStruct((T, D_MODEL), F32), row_big)], epilogue=_epi_store)
    (d_w_out,) = _matmul(
        "mm_dw_out", merged, dx2, dims=TN, grid=(1, 1, nkt),
        a_spec=t_rows, b_spec=t_cols, acc_shape=(D_MODEL, D_MODEL),
        outs=[(jax.ShapeDtypeStruct((D_MODEL, D_MODEL), F32), whole)],
        epilogue=_epi_store)
    dmerged = reducer.early_start({"w_mlp_down": d_w_down, "w_mlp_up": d_w_up, "w_out": d_w_out}, dmerged)
    (dgl, do_swa, do_fox, do_mem, dl_swa, do_fox_aug, dl_mem, d_wo_swa, d_wo_fox, d_wo_mem, d_b_gate) = _merge_bwd(
        proj, b_gate, o3, w_o3, dmerged, T, min(256, T))
    do_fox = reducer.early_send(do_fox)

    dqa, dkad, dvad, dbias, dsk = _swa_bwd(sinks, qa, kad, vad, bias, do_swa, lse_swa, dl_swa, T)
    dqf, dqf_aug, dkf, dkf_aug, dvf = _fox_bwd(qf, qf_aug_bwd, kf, kf_aug, vf, do_fox, do_fox_aug, T,
                                               min(FOX_BWD_TQ, T), min(FOX_BWD_TK, T))
    dvf = reducer.early_finish(dvf)
    dqm, dmk, dmv = _mem_bwd(qm, mk, mv, do_mem, lse_mem, dl_mem, T, tm)
    d_w_kv, d_g_mem, d_kn_mem = _mem_prep_bwd(mem, g_mem, memn, kv, w_kv, kn_mem, gm128, dmk, dmv)
    d_rel = _swa_bias_bwd(dbias, bucket)
    dlo, gacc = _prep_bwd(proj, dqa, dkad, dvad, dqf, dkf, dvf, dqm, dqf_aug, dkf_aug, gains, bfor, triu, gm64, gm128,
                          T, tb_prep)

    def dwc_half(name, dpart):
        (res,) = _matmul(
            name, h, dpart, dims=TN, grid=(1, LO_W // D_MODEL, nkt),
            a_spec=t_rows, b_spec=t_cols, acc_shape=(D_MODEL, D_MODEL),
            outs=[(jax.ShapeDtypeStruct((D_MODEL, LO_W), F32), pl.BlockSpec((D_MODEL, D_MODEL), lambda i, j, k: (0, j)))],
            epilogue=_epi_store)
        return res

    d_wc_lo = dwc_half("mm_dwc_lo", dlo)
    d_wc_gl = dwc_half("mm_dwc_gl", dgl)
    dlo = reducer.late_start({"wc_lo": d_wc_lo, "wc_gl": d_wc_gl, "w_mem_kv": d_w_kv, "w_o_swa": d_wo_swa,
                              "w_o_fox": d_wo_fox, "w_o_mem": d_wo_mem}, dlo)
    (dh_lo,) = _matmul(
        "mm_dh_lo", dlo, wc, dims=NT, grid=(T // tl, 1, LO_W // D_MODEL),
        a_spec=kblk, b_spec=pl.BlockSpec((D_MODEL, D_MODEL), lambda i, j, k: (0, k)),
        acc_shape=(tl, D_MODEL), outs=[(jax.ShapeDtypeStruct((T, D_MODEL), F32), row_big)], epilogue=_epi_store)
    dh_lo = reducer.late_send(dh_lo)

    def epi_dx(acc, extra_refs, out_refs, ij):
        dhh = acc + extra_refs[3][...]
        dx, dg = _rmsnorm_bwd_rows(dhh, extra_refs[0][...], extra_refs[1][...])
        out_refs[0][...] = dx + extra_refs[2][...]

        @pl.when(ij[0] == 0)
        def _():
            out_refs[1][...] = jnp.zeros_like(out_refs[1])

        out_refs[1][...] += dg

    grad_x, d_g_mix = _matmul(
        "mm_dh_gl", dgl, wc, dims=NT, grid=(T // tm, 1, GATE_W // D_MODEL),
        a_spec=pl.BlockSpec((tm, D_MODEL), lambda i, j, k: (i, k)),
        b_spec=pl.BlockSpec((D_MODEL, D_MODEL), lambda i, j, k: (0, k + LO_W // D_MODEL)),
        acc_shape=(tm, D_MODEL), extra=[(x, row_full), (g_mix, gain_spec), (dx2, row_full), (dh_lo, row_full)],
        outs=[(jax.ShapeDtypeStruct((T, D_MODEL), F32), row_full), (jax.ShapeDtypeStruct((1, D_MODEL), F32), gain_spec)],
        epilogue=epi_dx)

    fold64 = lambda row: (row[:64] + row[64:]).reshape(1, 64)
    grads = {
        "g_mix": d_g_mix, "b_gate": d_b_gate, "b_forget": gacc[5, :FOX_HEADS].reshape(1, FOX_HEADS),
        "qn_swa": fold64(gacc[0]), "kn_swa": fold64(gacc[1]),
        "sink_swa": -dsk[:, :SWA_GROUP, 0].reshape(1, SWA_HEADS), "rel_bias": d_rel[:, :SWA_HEADS],
        "qn_fox": fold64(gacc[2]), "kn_fox": fold64(gacc[3]),
        "g_mem": d_g_mem, "qn_mem": gacc[4].reshape(1, LANES), "kn_mem": d_kn_mem, "g_mlp": d_g_mlp,
    }
    return loss, grad_x, grads


MESH = pl.DeviceIdType.MESH
ANY = pl.BlockSpec(memory_space=pl.ANY)


def _place():
    x, y, c = lax.axis_index("x"), lax.axis_index("y"), lax.axis_index("c")
    chips = [(1 - x, y), (x, 1 - y), (1 - x, 1 - y)]
    return x, y, c, chips


def _handshake(peers):
    barrier = pltpu.get_barrier_semaphore()
    for peer in peers:
        pl.semaphore_signal(barrier, inc=1, device_id=peer, device_id_type=MESH)
    pl.semaphore_wait(barrier, len(peers))


def _all_gather_shards_async(name, collective_id, slots):
    n = len(slots)
    bufs = [jax.new_ref(s, memory_space=pltpu.MemorySpace.HBM) for s in slots]

    def body(ici_send, ici_recv, d2d_send, d2d_recv):
        x, y, c, chips = _place()
        sibling = (x, y, 1 - c)
        me = 2 * x + y
        _handshake([(px, py, c) for px, py in chips] + [sibling])

        def half(a, who):
            hr = slots[a].shape[1] // 2
            return pl.ds(pl.multiple_of(who * hr, hr), hr)

        def ici(a, j, slot, to):
            return pltpu.make_async_remote_copy(
                src_ref=bufs[a].at[me, half(a, c)], dst_ref=bufs[a].at[slot, half(a, c)],
                send_sem=ici_send.at[3 * a + j], recv_sem=ici_recv.at[3 * a + j], device_id=to, device_id_type=MESH)

        def d2d(a, j, slot, which):
            part = bufs[a].at[slot, half(a, which)]
            return pltpu.make_async_remote_copy(
                src_ref=part, dst_ref=part, send_sem=d2d_send.at[3 * a + j], recv_sem=d2d_recv.at[3 * a + j],
                device_id=sibling, device_id_type=MESH)

        sends = [ici(a, j, me, (*chip, c)) for a in range(n) for j, chip in enumerate(chips)]
        for cp in sends:
            cp.start()
        passed = []
        for a in range(n):
            for j, (px, py) in enumerate(chips):
                ici(a, j, 2 * px + py, (px, py, c)).wait_recv()
                cp = d2d(a, j, 2 * px + py, c)
                cp.start()
                passed.append(cp)
        for a in range(n):
            for j, (px, py) in enumerate(chips):
                d2d(a, j, 2 * px + py, 1 - c).wait_recv()
        for cp in sends + passed:
            cp.wait_send()

    pl.kernel(
        body, mesh=plsc.ScalarSubcoreMesh(axis_name="seq", num_cores=1), name=name,
        scratch_types=[pltpu.SemaphoreType.DMA((3 * n,))] * 4,
        compiler_params=pltpu.CompilerParams(collective_id=collective_id),
    )()
    return [b[...] for b in bufs]


def _sequencer_call(name, collective_id, n_sems, body):
    pl.kernel(
        body, mesh=plsc.ScalarSubcoreMesh(axis_name="seq", num_cores=1), name=name,
        scratch_types=[pltpu.SemaphoreType.DMA((n_sems,))] * 2,
        compiler_params=pltpu.CompilerParams(collective_id=collective_id),
    )()


def _hbm_ref(value):
    return jax.new_ref(value, memory_space=pltpu.MemorySpace.HBM)


def _pair_exchange(name, collective_id, gs):
    n = len(gs)
    src = [_hbm_ref(g) for g in gs]
    stage = [jax.empty_ref(jax.ShapeDtypeStruct((N_SHARD, g.shape[1] // 2, g.shape[2]), g.dtype),
                           memory_space=pltpu.MemorySpace.HBM) for g in gs]

    def body(send_sem, recv_sem):
        x, y, c, _ = _place()
        sibling = (x, y, 1 - c)
        _handshake([sibling])
        copies = []
        for a in range(n):
            hr = gs[a].shape[1] // 2
            theirs = pl.ds(pl.multiple_of((1 - c) * hr, hr), hr)
            copies.append(pltpu.make_async_remote_copy(
                src_ref=src[a].at[:, theirs, :], dst_ref=stage[a], send_sem=send_sem.at[a], recv_sem=recv_sem.at[a],
                device_id=sibling, device_id_type=MESH))
        for cp in copies:
            cp.start()
        for cp in copies:
            cp.wait()

    _sequencer_call(name, collective_id, n, body)
    return [s[...] for s in stage]


def _chip_exchange(name, collective_id, sums):
    n = len(sums)
    src = [_hbm_ref(s) for s in sums]
    got = [jax.empty_ref(jax.ShapeDtypeStruct((3,) + s.shape[1:], s.dtype), memory_space=pltpu.MemorySpace.HBM)
           for s in sums]

    def body(send_sem, recv_sem):
        x, y, c, chips = _place()
        _handshake([(px, py, c) for px, py in chips])
        copies = []
        for a in range(n):
            for j, (px, py) in enumerate(chips):
                copies.append(pltpu.make_async_remote_copy(
                    src_ref=src[a].at[2 * px + py], dst_ref=got[a].at[j],
                    send_sem=send_sem.at[3 * a + j], recv_sem=recv_sem.at[3 * a + j],
                    device_id=(px, py, c), device_id_type=MESH))
        for cp in copies:
            cp.start()
        for cp in copies:
            cp.wait()

    _sequencer_call(name, collective_id, 3 * n, body)
    return [g[...] for g in got]


def _pair_gather(name, collective_id, fulls):
    n = len(fulls)
    full = [_hbm_ref(f) for f in fulls]

    def body(send_sem, recv_sem):
        x, y, c, _ = _place()
        sibling = (x, y, 1 - c)
        _handshake([sibling])
        copies = []
        for a in range(n):
            hr = fulls[a].shape[0] // 2
            mine = full[a].at[pl.ds(pl.multiple_of(c * hr, hr), hr)]
            copies.append(pltpu.make_async_remote_copy(
                src_ref=mine, dst_ref=mine, send_sem=send_sem.at[a], recv_sem=recv_sem.at[a],
                device_id=sibling, device_id_type=MESH))
        for cp in copies:
            cp.start()
        for cp in copies:
            cp.wait()

    _sequencer_call(name, collective_id, n, body)
    return [f[...] for f in full]


ELEMENTWISE_BLOCK_ELEMS = 256 * 1024


def _row_block(rows, cols):
    rb = 8
    while rb * 2 * cols <= ELEMENTWISE_BLOCK_ELEMS and rb * 2 <= rows:
        rb *= 2
    return rb


def _pair_sum(name, place, g, stage):
    _, R, C = g.shape
    hr = R // 2
    rb = _row_block(hr, C)
    nb = hr // rb

    def body(place_ref, g_ref, st_ref, sum_bf, own_f32):
        s = pl.program_id(1)
        tot = g_ref[...] + st_ref[...]
        sum_bf[...] = tot.astype(BF16)

        @pl.when(s == place_ref[0])
        def _():
            own_f32[...] = tot

    return pl.pallas_call(
        body, name=name,
        grid_spec=pltpu.PrefetchScalarGridSpec(
            num_scalar_prefetch=1, grid=(nb, N_SHARD),
            in_specs=[pl.BlockSpec((None, rb, C), lambda i, s, pr: (s, pr[1] * nb + i, 0)),
                      pl.BlockSpec((None, rb, C), lambda i, s, pr: (s, i, 0))],
            out_specs=[pl.BlockSpec((None, rb, C), lambda i, s, pr: (s, i, 0)),
                       pl.BlockSpec((rb, C), lambda i, s, pr: (i, 0))]),
        out_shape=[jax.ShapeDtypeStruct((N_SHARD, hr, C), BF16), jax.ShapeDtypeStruct((hr, C), F32)],
        compiler_params=_cparams("arbitrary", "arbitrary"),
    )(place, g, stage)


def _final_sum(name, place, own, got):
    hr, C = own.shape
    rb = _row_block(hr, C)
    nb = hr // rb

    def body(place_ref, own_ref, got_ref, o_ref):
        o_ref[...] = ((own_ref[...] + got_ref[0].astype(F32)) + got_ref[1].astype(F32)) + got_ref[2].astype(F32)

    return pl.pallas_call(
        body, name=name,
        grid_spec=pltpu.PrefetchScalarGridSpec(
            num_scalar_prefetch=1, grid=(nb,),
            in_specs=[pl.BlockSpec((rb, C), lambda i, pr: (i, 0)), pl.BlockSpec((3, rb, C), lambda i, pr: (0, i, 0))],
            out_specs=pl.BlockSpec((rb, C), lambda i, pr: (pr[1] * nb + i, 0))),
        out_shape=jax.ShapeDtypeStruct((2 * hr, C), F32),
        compiler_params=_cparams("arbitrary"),
    )(place, own, got)


def _adamw_math(w, g, m, v):
    m = ADAM_B1 * m + (1.0 - ADAM_B1) * g
    v = ADAM_B2 * v + (1.0 - ADAM_B2) * (g * g)
    m_hat = m / (1.0 - ADAM_B1 ** ADAM_STEP)
    v_hat = v / (1.0 - ADAM_B2 ** ADAM_STEP)
    delta = -ADAM_LR * (m_hat / (jnp.sqrt(v_hat) + ADAM_EPS) + ADAM_WD * w)
    return delta, m, v


def _adamw(name, w, g, m, v):
    R, Cw = w.shape
    Cg = g.shape[1]
    rb = _row_block(R, Cg)

    def body(w_ref, g_ref, m_ref, v_ref, g_o, d_o, m_o, v_o):
        gv = g_ref[...]
        delta, mn, vn = _adamw_math(w_ref[...], gv, m_ref[...], v_ref[...])
        g_o[...] = gv
        d_o[...] = delta
        m_o[...] = mn
        v_o[...] = vn

    blk = pl.BlockSpec((rb, Cg), lambda i: (i, 0))
    return pl.pallas_call(
        body, name=name, grid=(R // rb,),
        in_specs=[blk] * 4, out_specs=[blk] * 4,
        out_shape=[jax.ShapeDtypeStruct((R, Cw), F32)] * 4,
        compiler_params=_cparams("parallel"),
    )(w, g, m, v)


N_DEV = 8
SMALL_ROWS = 64


def _small_allreduce_adamw(g, w, m, v):
    def body(g_ref, w_ref, m_ref, v_ref, all_ref, gs_o, d_o, m_o, v_o, send_sems, recv_sems, local_sem):
        x, y, c, chips = _place()
        me, sibling = (x, y, c), (x, y, 1 - c)

        def rows(px, py, pc):
            return all_ref.at[pl.ds(pl.multiple_of((4 * px + 2 * py + pc) * SMALL_ROWS, SMALL_ROWS), SMALL_ROWS), :]

        def copy(k, block, to, src=None):
            return pltpu.make_async_remote_copy(
                src_ref=rows(*block) if src is None else src, dst_ref=rows(*block),
                send_sem=send_sems.at[k], recv_sem=recv_sems.at[k], device_id=to, device_id_type=MESH)

        mine = pltpu.make_async_copy(g_ref, rows(*me), local_sem)
        mine.start()
        first = [copy(0, me, sibling, src=g_ref)]
        first += [copy(1 + j, me, (*chip, c), src=g_ref) for j, chip in enumerate(chips)]
        for cp in first:
            cp.start()
        passed = [copy(4 + j, (*chip, c), sibling) for j, chip in enumerate(chips)]
        for j, chip in enumerate(chips):
            copy(1 + j, (*chip, c), me).wait_recv()
            passed[j].start()
        copy(0, sibling, me).wait_recv()
        for j, chip in enumerate(chips):
            copy(4 + j, (*chip, 1 - c), me).wait_recv()
        for cp in first + passed:
            cp.wait_send()
        mine.wait()

        tot = all_ref[0:SMALL_ROWS, :]
        for d in range(1, N_DEV):
            tot = tot + all_ref[d * SMALL_ROWS:(d + 1) * SMALL_ROWS, :]
        delta, mn, vn = _adamw_math(w_ref[...], tot, m_ref[...], v_ref[...])
        gs_o[...] = tot
        d_o[...] = delta
        m_o[...] = mn
        v_o[...] = vn

    vm = pl.BlockSpec(memory_space=pltpu.VMEM)
    shp = jax.ShapeDtypeStruct((SMALL_ROWS, LANES), F32)
    res = pl.pallas_call(
        body, name="small_allreduce_adamw", in_specs=[vm] * 4, out_specs=[vm] * 5,
        out_shape=[jax.ShapeDtypeStruct((N_DEV * SMALL_ROWS, LANES), F32), shp, shp, shp, shp],
        scratch_shapes=[pltpu.SemaphoreType.DMA((7,)), pltpu.SemaphoreType.DMA((7,)), pltpu.SemaphoreType.DMA],
    )(g, w, m, v)
    return res[1:]


SMALL_NAMES = ("g_mix", "b_gate", "b_forget", "qn_swa", "kn_swa", "sink_swa", "rel_bias", "qn_fox", "kn_fox",
               "g_mem", "qn_mem", "kn_mem", "g_mlp")
BIG_NAMES = ("w_in", "w_mem_kv", "w_o_swa", "w_o_fox", "w_o_mem", "w_out", "w_mlp_up", "w_mlp_down")
WEIGHT_NAMES = ("g_mix", "w_in", "b_gate", "b_forget", "qn_swa", "kn_swa", "sink_swa", "rel_bias", "qn_fox", "kn_fox",
                "g_mem", "w_mem_kv", "qn_mem", "kn_mem", "w_o_swa", "w_o_fox", "w_o_mem", "w_out", "g_mlp",
                "w_mlp_up", "w_mlp_down")


def _pack_small(parts, extra=None):
    rows = []
    for n in SMALL_NAMES:
        flat = parts[n].reshape(-1).astype(F32)
        flat = jnp.pad(flat, (0, (-flat.size) % LANES))
        rows.append(flat.reshape(-1, LANES))
    if extra is not None:
        rows.append(jnp.pad(extra.reshape(1, 1), ((0, 0), (0, LANES - 1))))
    packed = jnp.concatenate(rows, axis=0)
    return jnp.pad(packed, ((0, SMALL_ROWS - packed.shape[0]), (0, 0)))


def _unpack_small(packed, shapes):
    out, r = {}, 0
    for n in SMALL_NAMES:
        size = math.prod(shapes[n])
        nr = -(-size // LANES)
        out[n] = packed[r:r + nr].reshape(-1)[:size].reshape(shapes[n])
        r += nr
    return out, packed[r, 0]


W_IN_SEGMENTS = ((C_QA, 0, 512), (C_QF, 768, 512), (C_KF, 1280, 512), (C_VF, 1792, 512), (C_QM, 2312, 512),
                 (C_KA, 512, 128), (C_VA, 640, 128), (C_FL, 2304, FOX_HEADS), (C_GL, 2824, GATE_W))
RELAYOUT_ROWS = 256


def _permute_pieces(src_of_dst):
    blocks = []
    for b in range(len(src_of_dst) // LANES):
        runs, lane = [], 0
        while lane < LANES:
            src = src_of_dst[b * LANES + lane]
            if src is None:
                lane += 1
                continue
            plane, col = src
            end = lane + 1
            while (end < LANES and src_of_dst[b * LANES + end] == (plane, col + end - lane)
                   and (col + end - lane) // LANES == col // LANES):
                end += 1
            runs.append((plane, col // LANES, (lane - col) % LANES, lane, end))
            lane = end
        blocks.append(runs)
    return blocks


def _permuted_block(runs, load, rows):
    lane = _lane((rows, LANES))
    acc = jnp.zeros((rows, LANES), F32)
    for plane, blk, shift, lo, hi in runs:
        x = load(plane, blk).astype(F32)
        if shift:
            x = pltpu.roll(x, shift, 1)
        acc = x if (lo, hi) == (0, LANES) else jnp.where((lane >= lo) & (lane < hi), x, acc)
    return acc


def _w_in_to_segments(g_in):
    src_of_dst = [None] * PROJ_W
    for mine, theirs, width in W_IN_SEGMENTS:
        for k in range(width):
            src_of_dst[mine + k] = ((theirs + k) // IN_SHARD, (theirs + k) % IN_SHARD)
    blocks = _permute_pieces(src_of_dst)
    rb = RELAYOUT_ROWS

    def body(src_ref, out_ref):
        for b, runs in enumerate(blocks):
            blk = _permuted_block(runs, lambda p, c: src_ref[p, :, c * LANES:(c + 1) * LANES], rb)
            out_ref[:, b * LANES:(b + 1) * LANES] = blk.astype(out_ref.dtype)

    return pl.pallas_call(
        body, name="w_in_to_segments", grid=(D_MODEL // rb,),
        in_specs=[pl.BlockSpec((N_SHARD, rb, IN_SHARD_PAD), lambda i: (0, i, 0))],
        out_specs=pl.BlockSpec((rb, PROJ_W), lambda i: (i, 0)),
        out_shape=jax.ShapeDtypeStruct((D_MODEL, PROJ_W), g_in.dtype),
        compiler_params=_cparams("parallel"),
    )(g_in)


def _w_in_from_segments(lo, gl):
    mine_of_theirs = {}
    for mine, theirs, width in W_IN_SEGMENTS:
        for k in range(width):
            mine_of_theirs[theirs + k] = mine + k
    src_of_dst = [None] * (N_SHARD * IN_SHARD_PAD)
    for s in range(N_SHARD):
        for l in range(IN_SHARD):
            j = mine_of_theirs[s * IN_SHARD + l]
            src_of_dst[s * IN_SHARD_PAD + l] = (j // LO_W, j % LO_W)
    blocks = _permute_pieces(src_of_dst)
    per_slot = IN_SHARD_PAD // LANES
    rb = RELAYOUT_ROWS

    def body(lo_ref, gl_ref, out_ref):
        planes = (lo_ref, gl_ref)
        for b, runs in enumerate(blocks):
            blk = _permuted_block(runs, lambda p, c: planes[p][:, c * LANES:(c + 1) * LANES], rb)
            c0 = (b % per_slot) * LANES
            out_ref[b // per_slot, :, c0:c0 + LANES] = blk

    half = pl.BlockSpec((rb, LO_W), lambda i: (i, 0))
    return pl.pallas_call(
        body, name="w_in_from_segments", grid=(D_MODEL // rb,),
        in_specs=[half, half],
        out_specs=pl.BlockSpec((N_SHARD, rb, IN_SHARD_PAD), lambda i: (0, i, 0)),
        out_shape=jax.ShapeDtypeStruct((N_SHARD, D_MODEL, IN_SHARD_PAD), F32),
        compiler_params=_cparams("parallel"),
    )(lo, gl)


def _after(first, then):
    return lax.optimization_barrier((first, then))


class _ReduceGroup:
    def __init__(self, tag, first_collective_id, place):
        self.tag, self.first_id, self.place = tag, first_collective_id, place

    def start(self, local, tie):
        self.names = tuple(local)
        mine, tie = _after([local[n] for n in self.names], tie)
        self.mine = mine
        self.staged = _pair_exchange("pair_exchange_" + self.tag, self.first_id, mine)
        return tie

    def send(self, tie):
        staged, tie = _after(self.staged, tie)
        sums = [_pair_sum("pair_sum_" + n, self.place, g, st) for n, g, st in zip(self.names, self.mine, staged)]
        travel, tie = _after([s[0] for s in sums], tie)
        self.own = [s[1] for s in sums]
        self.got = _chip_exchange("chip_exchange_" + self.tag, self.first_id + 1, travel)
        return tie

    def finish(self, tie):
        got, tie = _after(self.got, tie)
        halves = [_final_sum("final_sum_" + n, self.place, o, r) for n, o, r in zip(self.names, self.own, got)]
        halves, tie = _after(halves, tie)
        summed = _pair_gather("pair_gather_" + self.tag, self.first_id + 2, halves)
        self.summed = dict(zip(self.names, summed))
        return tie


class _GradReducer:
    def __init__(self, place):
        self.early = _ReduceGroup("early", 2, place)
        self.late = _ReduceGroup("late", 5, place)

    @staticmethod
    def _slot_rows(a):
        return a.reshape(N_SHARD, a.shape[0] // N_SHARD, a.shape[1])

    def early_start(self, g, tie):
        return self.early.start({"w_mlp_down": self._slot_rows(g["w_mlp_down"]), "w_mlp_up": g["w_mlp_up"],
                                 "w_out": self._slot_rows(g["w_out"])}, tie)

    def early_send(self, tie):
        return self.early.send(tie)

    def early_finish(self, tie):
        return self.early.finish(tie)

    def late_start(self, g, tie):
        d_in = _w_in_from_segments(g["wc_lo"], g["wc_gl"])
        return self.late.start({"w_in": d_in, "w_mem_kv": self._slot_rows(g["w_mem_kv"]), "w_o_swa": g["w_o_swa"],
                                "w_o_fox": g["w_o_fox"], "w_o_mem": g["w_o_mem"]}, tie)

    def late_send(self, tie):
        return self.late.send(tie)

    def late_finish(self, tie):
        return self.late.finish(tie)

    @property
    def summed(self):
        return {**self.early.summed, **self.late.summed}


def kernel(x, mem, g_mix, w_in, b_gate, b_forget, qn_swa, kn_swa, sink_swa, rel_bias, qn_fox, kn_fox, g_mem, w_mem_kv, qn_mem, kn_mem, w_o_swa, w_o_fox, w_o_mem, w_out, g_mlp, w_mlp_up, w_mlp_down, loss_target, m_g_mix, m_w_in, m_b_gate, m_b_forget, m_qn_swa, m_kn_swa, m_sink_swa, m_rel_bias, m_qn_fox, m_kn_fox, m_g_mem, m_w_mem_kv, m_qn_mem, m_kn_mem, m_w_o_swa, m_w_o_fox, m_w_o_mem, m_w_out, m_g_mlp, m_w_mlp_up, m_w_mlp_down, v_g_mix, v_w_in, v_b_gate, v_b_forget, v_qn_swa, v_kn_swa, v_sink_swa, v_rel_bias, v_qn_fox, v_kn_fox, v_g_mem, v_w_mem_kv, v_qn_mem, v_kn_mem, v_w_o_swa, v_w_o_fox, v_w_o_mem, v_w_out, v_g_mlp, v_w_mlp_up, v_w_mlp_down):
    given = dict(locals())
    W = {n: given[n] for n in WEIGHT_NAMES}
    M = {n: given["m_" + n] for n in WEIGHT_NAMES}
    V = {n: given["v_" + n] for n in WEIGHT_NAMES}
    pad_in = ((0, 0), (0, IN_SHARD_PAD - IN_SHARD))

    shards = [jnp.pad(w_in[0].astype(BF16), pad_in)] + [W[n][0].astype(BF16) for n in BIG_NAMES[1:]]
    slots = [jnp.broadcast_to(s[None], (N_SHARD,) + s.shape) for s in shards]
    (g_in,) = _all_gather_shards_async("all_gather_w_in", 1, slots[:1])
    g_in, late = lax.optimization_barrier((g_in, slots[1:]))
    g_kv, g_oa, g_of, g_om, g_out, g_up, g_down = _all_gather_shards_async("all_gather_weights_async", 8, late)
    wc = _w_in_to_segments(g_in)
    small = {n: (W[n] if n == "rel_bias" else W[n].reshape(1, -1)) for n in SMALL_NAMES}

    place = jnp.stack([2 * lax.axis_index("x") + lax.axis_index("y"), lax.axis_index("c")]).astype(jnp.int32)
    reducer = _GradReducer(place)
    loss, grad_x, grads = _local_step(
        x[0], mem[0], loss_target[0], small, wc, g_kv.reshape(D_MODEL, D_MODEL), (g_oa, g_of, g_om),
        g_out.reshape(D_MODEL, D_MODEL), g_up, g_down.reshape(D_FF, D_MODEL), reducer)

    out = {}

    def adamw_of(names, summed):
        for n in names:
            res = _adamw("adamw_" + n, W[n][0], summed[n], M[n][0], V[n][0])
            out[n] = [r.reshape(W[n].shape) for r in res]

    adamw_of(reducer.early.names, reducer.early.summed)
    shapes = {n: W[n].shape for n in SMALL_NAMES}
    packed = _small_allreduce_adamw(_pack_small(grads, loss), _pack_small(W), _pack_small(M), _pack_small(V))
    done_meanwhile = ([out[n] for n in reducer.early.names], packed)
    (early_out, packed), grad_x = reducer.late_finish((done_meanwhile, grad_x))
    for n, res in zip(reducer.early.names, early_out):
        out[n] = res
    adamw_of(reducer.late.names, reducer.late.summed)
    unpacked = [_unpack_small(p, shapes) for p in packed]
    for n in SMALL_NAMES:
        out[n] = [u[0][n] for u in unpacked]
    loss_total = unpacked[0][1]

    return (loss_total, grad_x.reshape(x.shape),
            *[out[n][0] for n in WEIGHT_NAMES], *[out[n][1] for n in WEIGHT_NAMES],
            *[out[n][2] for n in WEIGHT_NAMES], *[out[n][3] for n in WEIGHT_NAMES])
```

```python
import functools
import math

import jax
import jax.numpy as jnp
from jax import lax
from jax.experimental import pallas as pl
from jax.experimental.pallas import tpu as pltpu
from jax.experimental.pallas import tpu_sc as plsc

F32 = jnp.float32
BF16 = jnp.bfloat16

D_MODEL = 1024
N_MEM = 256
SWA_HEADS = 8
SWA_KV_HEADS = 2
SWA_HEAD_DIM = 64
WINDOW = 128
FOX_HEADS = 8
FOX_HEAD_DIM = 64
MEM_HEADS = 4
MEM_HEAD_DIM = 128
D_FF = 4 * D_MODEL
REL_BUCKETS = 32
REL_MAX_DIST = 128
EPS = 1e-6
NEG = -1e30
GATE_W = 3 * D_MODEL
IN_WIDTH = 5896
N_SHARD = 4
IN_SHARD = IN_WIDTH // N_SHARD
IN_SHARD_PAD = 1536

ADAM_LR = 0.001
ADAM_B1 = 0.9
ADAM_B2 = 0.999
ADAM_EPS = 1e-08
ADAM_WD = 0.01
ADAM_STEP = 10

LANES = 128
V7X_VMEM_BYTES = 64 * 1024 * 1024
VMEM_LIMIT = V7X_VMEM_BYTES * 3 // 4

C_QA, C_QF, C_KF, C_VF, C_QM, C_KA, C_VA, C_FL, C_GL = 0, 512, 1024, 1536, 2048, 2560, 2688, 2816, 3072
LO_W = 3072
PROJ_W = 6144

NN = (((1,), (0,)), ((), ()))
NT = (((1,), (1,)), ((), ()))
TN = (((0,), (0,)), ((), ()))


def _dot(a, b, dims=NN):
    return lax.dot_general(a, b, dims, preferred_element_type=F32)


def _cparams(*sem):
    return pltpu.CompilerParams(dimension_semantics=sem, vmem_limit_bytes=VMEM_LIMIT)


def _split3(a):
    hi = a.astype(BF16)
    r1 = a - hi.astype(F32)
    mid = r1.astype(BF16)
    lo = (r1 - mid.astype(F32)).astype(BF16)
    return hi, mid, lo


def _dot3_right(a, g):
    hi, mid, lo = _split3(a)
    return _dot(hi, g) + _dot(mid, g) + _dot(lo, g)


def _dot3_left(g, a):
    hi, mid, lo = _split3(a)
    return _dot(g, hi) + _dot(g, mid) + _dot(g, lo)


def _group_mean_matrix(d):
    r = jnp.arange(LANES)
    return jnp.where((r[:, None] // d) == (r[None, :] // d), 1.0 / d, 0.0).astype(BF16)


def _lane(shape):
    return lax.broadcasted_iota(jnp.int32, shape, len(shape) - 1)


def _matmul(name, a, b, *, dims, grid, a_spec, b_spec, acc_shape, outs, epilogue, extra=()):
    nk = grid[2]
    n_extra = len(extra)

    def body(a_ref, b_ref, *rest):
        extra_refs = rest[:n_extra]
        out_refs = rest[n_extra:n_extra + len(outs)]
        i, j, k = pl.program_id(0), pl.program_id(1), pl.program_id(2)
        part = _dot(a_ref[...].astype(BF16), b_ref[...].astype(BF16), dims)
        if nk == 1:
            epilogue(part, extra_refs, out_refs, (i, j))
            return
        acc_ref = rest[-1]

        @pl.when(k == 0)
        def _():
            acc_ref[...] = part

        @pl.when((k > 0) & (k < nk - 1))
        def _():
            acc_ref[...] += part

        @pl.when(k == nk - 1)
        def _():
            epilogue(acc_ref[...] + part, extra_refs, out_refs, (i, j))

    res = pl.pallas_call(
        body,
        name=name,
        grid=grid,
        in_specs=[a_spec, b_spec] + [s for _, s in extra],
        out_specs=[s for _, s in outs],
        out_shape=[s for s, _ in outs],
        scratch_shapes=[pltpu.VMEM(acc_shape, F32)] if nk > 1 else [],
        compiler_params=_cparams("arbitrary", "arbitrary", "arbitrary"),
    )(a, b, *[x for x, _ in extra])
    return res


def _epi_store(acc, extra_refs, out_refs, ij):
    out_refs[0][...] = acc.astype(out_refs[0].dtype)


def _rms_rows(x, g):
    r = lax.rsqrt(jnp.mean(x * x, axis=-1, keepdims=True) + EPS)
    return x * r, r


def _rmsnorm_bwd_rows(dh, x, g):
    xhat, r = _rms_rows(x, g)
    dxh = dh * g
    dx = r * (dxh - xhat * jnp.mean(dxh * xhat, axis=-1, keepdims=True))
    return dx, jnp.sum(dh * xhat, axis=0, keepdims=True)


def _rmsnorm(name, x, g, tb):
    T, Dm = x.shape

    def body(x_ref, g_ref, o_ref):
        xhat, _ = _rms_rows(x_ref[...], None)
        o_ref[...] = (xhat * g_ref[...]).astype(o_ref.dtype)

    return pl.pallas_call(
        body, name=name, grid=(T // tb,),
        in_specs=[pl.BlockSpec((tb, Dm), lambda i: (i, 0)), pl.BlockSpec((1, Dm), lambda i: (0, 0))],
        out_specs=pl.BlockSpec((tb, Dm), lambda i: (i, 0)),
        out_shape=jax.ShapeDtypeStruct((T, Dm), BF16),
        compiler_params=_cparams("parallel"),
    )(x, g)


def _head_norm(x, gm, gain):
    ms = _dot3_right(x * x, gm)
    r = lax.rsqrt(ms + EPS)
    return x * r * gain, x * r


def _head_norm_bwd(dy, x, gm, gain):
    ms = _dot3_right(x * x, gm)
    r = lax.rsqrt(ms + EPS)
    xhat = x * r
    dxh = dy * gain
    dx = r * (dxh - xhat * _dot3_right(dxh * xhat, gm))
    return dx, jnp.sum(dy * xhat, axis=0, keepdims=True)


def _log_sigmoid(z):
    return jnp.minimum(z, 0.0) - jnp.log(1.0 + jnp.exp(-jnp.abs(z)))


def _prep_fwd(proj, gains, bfor, tril, gm64, gm128, T, tb):
    nb = T // tb

    def body(qa_ref, qf_ref, kf_ref, vf_ref, qm_ref, ka_ref, va_ref, fl_ref, gains_ref, bfor_ref, tril_ref,
             gm64_ref, gm128_ref,
             qa_o, qf_o, kf_o, vf_o, qm_o, kad_o, vad_o, qaug_o, kaug_o, carry):
        i = pl.program_id(0)
        gm64v = gm64_ref[...]
        gm128v = gm128_ref[...]
        lane = _lane((tb, LANES))

        def norm512(src, dst, row, gm, scale=1.0):
            gain = gains_ref[row:row + 1, :]
            for c in range(4):
                sl = slice(c * LANES, (c + 1) * LANES)
                y, _ = _head_norm(src[:, sl], gm, gain)
                dst[:, sl] = (y * scale).astype(dst.dtype)

        norm512(qa_ref, qa_o, 0, gm64v)
        norm512(qf_ref, qf_o, 2, gm64v, FOX_SCALE)
        norm512(kf_ref, kf_o, 3, gm64v)
        norm512(qm_ref, qm_o, 4, gm128v)
        vf_o[...] = vf_ref[...].astype(vf_o.dtype)

        ka_n, _ = _head_norm(ka_ref[...], gm64v, gains_ref[1:2, :])
        ka_r = pltpu.roll(ka_n, 64, 1)
        va = va_ref[...]
        va_r = pltpu.roll(va, 64, 1)
        lo = lane < 64
        kad_o[0] = jnp.where(lo, ka_n, ka_r).astype(kad_o.dtype)
        kad_o[1] = jnp.where(lo, ka_r, ka_n).astype(kad_o.dtype)
        vad_o[0] = jnp.where(lo, va, va_r).astype(vad_o.dtype)
        vad_o[1] = jnp.where(lo, va_r, va).astype(vad_o.dtype)

        @pl.when(i == 0)
        def _():
            carry[...] = jnp.zeros_like(carry)

        logf = jnp.where(lane < FOX_HEADS, _log_sigmoid(fl_ref[...] + bfor_ref[...]), 0.0)
        c = _dot3_left(tril_ref[...], logf) + carry[0:1, :]
        carry[...] = jnp.broadcast_to(c[tb - 1:tb, :], carry.shape)
        for pair in range(FOX_HEADS // 2):
            qaug = jnp.zeros((tb, LANES), F32)
            kaug = jnp.zeros((tb, LANES), F32)
            for sub in range(2):
                col = jnp.sum(jnp.where(lane == 2 * pair + sub, c, 0.0), axis=1, keepdims=True)
                pieces = [p.astype(F32) for p in _split3(col)]
                base = AUG_STRIDE * sub
                for e in range(3):
                    qaug = jnp.where(lane == base + AUG_C + e, pieces[e], qaug)
                    kaug = jnp.where(lane == base + AUG_NEG_C + e, -pieces[e], kaug)
                qaug = jnp.where((lane >= base + AUG_NEG_C) & (lane < base + AUG_NEG_C + 3), 1.0, qaug)
                ones_k = ((lane >= base + AUG_C) & (lane < base + AUG_C + 3)) | (
                    (lane >= base + AUG_STAT) & (lane < base + AUG_STAT + 3))
                kaug = jnp.where(ones_k, 1.0, kaug)
            sl = slice(pair * LANES, (pair + 1) * LANES)
            qaug_o[:, sl] = qaug.astype(BF16)
            kaug_o[:, sl] = kaug.astype(BF16)

    def seg(width, start):
        return pl.BlockSpec((tb, width), lambda i, s=start // width: (i, s))

    const = lambda shape: pl.BlockSpec(shape, lambda i: tuple(0 for _ in shape))
    rows512 = pl.BlockSpec((tb, 512), lambda i: (i, 0))
    outs = pl.pallas_call(
        body, name="prep_fwd", grid=(nb,),
        in_specs=[seg(512, C_QA), seg(512, C_QF), seg(512, C_KF), seg(512, C_VF), seg(512, C_QM),
                  seg(128, C_KA), seg(128, C_VA), seg(128, C_FL),
                  const((8, LANES)), const((1, LANES)), const((tb, tb)), const((LANES, LANES)), const((LANES, LANES))],
        out_specs=[rows512, rows512, rows512, rows512, rows512,
                   pl.BlockSpec((2, tb, LANES), lambda i: (0, i, 0)), pl.BlockSpec((2, tb, LANES), lambda i: (0, i, 0)),
                   rows512, rows512],
        out_shape=[jax.ShapeDtypeStruct((T, 512), BF16)] * 5
        + [jax.ShapeDtypeStruct((2, T, LANES), BF16)] * 2
        + [jax.ShapeDtypeStruct((T, 512), BF16)] * 2,
        scratch_shapes=[pltpu.VMEM((8, LANES), F32)],
        compiler_params=_cparams("arbitrary"),
    )(proj, proj, proj, proj, proj, proj, proj, proj, gains, bfor, tril, gm64, gm128)
    return outs


def _prep_bwd(proj, dqa, dkad, dvad, dqf, dkf, dvf, dqm, dqf_aug, dkf_aug, gains, bfor, triu, gm64, gm128, T, tb):
    nb = T // tb

    def body(qa_ref, qf_ref, kf_ref, qm_ref, ka_ref, fl_ref,
             dqa_ref, dkad_ref, dvad_ref, dqf_ref, dkf_ref, dvf_ref, dqm_ref, dqfa_ref, dkfa_ref,
             gains_ref, bfor_ref, triu_ref, gm64_ref, gm128_ref,
             dlo_o, gacc_o, carry):
        i = pl.program_id(0)
        gm64v = gm64_ref[...]
        gm128v = gm128_ref[...]
        lane = _lane((tb, LANES))

        @pl.when(i == 0)
        def _():
            carry[...] = jnp.zeros_like(carry)
            gacc_o[...] = jnp.zeros_like(gacc_o)

        def norm512_bwd(dsrc, xsrc, col0, row, gm):
            gain = gains_ref[row:row + 1, :]
            gsum = jnp.zeros((1, LANES), F32)
            for c in range(4):
                sl = slice(c * LANES, (c + 1) * LANES)
                dx, dg = _head_norm_bwd(dsrc[:, sl], xsrc[:, sl], gm, gain)
                dlo_o[:, col0 + c * LANES:col0 + (c + 1) * LANES] = dx.astype(dlo_o.dtype)
                gsum = gsum + dg
            gacc_o[row:row + 1, :] += gsum

        norm512_bwd(dqa_ref, qa_ref, C_QA, 0, gm64v)
        norm512_bwd(dqf_ref, qf_ref, C_QF, 2, gm64v)
        norm512_bwd(dkf_ref, kf_ref, C_KF, 3, gm64v)
        norm512_bwd(dqm_ref, qm_ref, C_QM, 4, gm128v)
        dlo_o[:, C_VF:C_VF + 512] = dvf_ref[...].astype(dlo_o.dtype)

        lo = lane < 64

        def fold(ref):
            f0 = ref[0] + pltpu.roll(ref[0], 64, 1)
            f1 = ref[1] + pltpu.roll(ref[1], 64, 1)
            return jnp.where(lo, f0, f1)

        dka, dg = _head_norm_bwd(fold(dkad_ref), ka_ref[...], gm64v, gains_ref[1:2, :])
        gacc_o[1:2, :] += dg
        dlo_o[:, C_KA:C_KA + LANES] = dka.astype(dlo_o.dtype)
        dlo_o[:, C_VA:C_VA + LANES] = fold(dvad_ref).astype(dlo_o.dtype)

        dc = jnp.zeros((tb, LANES), F32)
        for pair in range(FOX_HEADS // 2):
            sl = slice(pair * LANES, (pair + 1) * LANES)
            rows_sum, cols_sum = dqfa_ref[:, sl], dkfa_ref[:, sl]
            for sub in range(2):
                diff = (jnp.where(lane == AUG_STRIDE * sub + AUG_C, rows_sum, 0.0)
                        - jnp.where(lane == AUG_STRIDE * sub + AUG_NEG_C, cols_sum, 0.0))
                dc = jnp.where(lane == 2 * pair + sub, jnp.sum(diff, axis=1, keepdims=True), dc)
        dlogf = _dot3_left(triu_ref[...], dc) + carry[0:1, :]
        carry[...] = jnp.broadcast_to(dlogf[0:1, :], carry.shape)
        z = fl_ref[...] + bfor_ref[...]
        dfl = jnp.where(lane < FOX_HEADS, dlogf / (1.0 + jnp.exp(z)), 0.0)
        gacc_o[5:6, :] += jnp.sum(dfl, axis=0, keepdims=True)
        dlo_o[:, C_FL:C_FL + LANES] = dfl.astype(dlo_o.dtype)
        dlo_o[:, C_FL + LANES:C_FL + 2 * LANES] = jnp.zeros((tb, LANES), dlo_o.dtype)

    rev = lambda i: nb - 1 - i

    def seg(width, start):
        return pl.BlockSpec((tb, width), lambda i, s=start // width: (rev(i), s))

    const = lambda shape: pl.BlockSpec(shape, lambda i: tuple(0 for _ in shape))
    rows512 = pl.BlockSpec((tb, 512), lambda i: (rev(i), 0))
    dup = pl.BlockSpec((2, tb, LANES), lambda i: (0, rev(i), 0))
    return pl.pallas_call(
        body, name="prep_bwd", grid=(nb,),
        in_specs=[seg(512, C_QA), seg(512, C_QF), seg(512, C_KF), seg(512, C_QM), seg(128, C_KA), seg(128, C_FL),
                  rows512, dup, dup, rows512, rows512, rows512, rows512, rows512, rows512,
                  const((8, LANES)), const((1, LANES)), const((tb, tb)), const((LANES, LANES)), const((LANES, LANES))],
        out_specs=[pl.BlockSpec((tb, LO_W), lambda i: (rev(i), 0)), const((8, LANES))],
        out_shape=[jax.ShapeDtypeStruct((T, LO_W), BF16), jax.ShapeDtypeStruct((8, LANES), F32)],
        scratch_shapes=[pltpu.VMEM((8, LANES), F32)],
        compiler_params=_cparams("arbitrary"),
    )(proj, proj, proj, proj, proj, proj, dqa, dkad, dvad, dqf, dkf, dvf, dqm, dqf_aug, dkf_aug,
      gains, bfor, triu, gm64, gm128)


FOX_SCALE = FOX_HEAD_DIM ** -0.5
AUG_STRIDE = 16
AUG_C = 0
AUG_NEG_C = 3
AUG_STAT = 6
FOX_TQ, FOX_TK = 1024, 1024
FOX_BWD_TQ, FOX_BWD_TK = 1024, 1024


def _fox_head_mask(sub, rows):
    lane = _lane((rows, 2 * LANES))
    main = (lane >= 64 * sub) & (lane < 64 * sub + 64)
    aug = (lane >= LANES + AUG_STRIDE * sub) & (lane < LANES + AUG_STRIDE * (sub + 1))
    return main | aug


def _fox_fwd(q, qaug, k, kaug, v, T, tq, tk):
    nq, nk = T // tq, T // tk
    rep = tk // LANES
    last_of = lambda i: (i * tq + tq - 1) // tk

    def body(q_ref, qa_ref, k_ref, ka_ref, v_ref, o_ref, qab_ref, m_s, acc_s):
        p_, i, j = pl.program_id(0), pl.program_id(1), pl.program_id(2)
        last = last_of(i)

        @pl.when(j == 0)
        def _():
            m_s[...] = jnp.full(m_s.shape, NEG, F32)
            acc_s[...] = jnp.zeros_like(acc_s)

        def step(diagonal):
            q2 = jnp.concatenate([q_ref[...], qa_ref[...]], axis=1)
            k2 = jnp.concatenate([k_ref[...], ka_ref[...]], axis=1)
            v2 = jnp.concatenate([v_ref[...], ka_ref[...]], axis=1)
            if diagonal:
                causal = (lax.broadcasted_iota(jnp.int32, (tq, tk), 1) + j * tk
                          <= lax.broadcasted_iota(jnp.int32, (tq, tk), 0) + i * tq)
            scores = [_dot(jnp.where(_fox_head_mask(sub, tq), q2, jnp.zeros_like(q2)), k2, NT) for sub in range(2)]
            for sub in range(2):
                s = scores[sub]
                if diagonal:
                    s = jnp.where(causal, s, NEG)
                m_prev = m_s[sub]
                m_next = jnp.maximum(m_prev, jnp.max(s, axis=1, keepdims=True))
                p = jnp.exp(s - jnp.tile(m_next, (1, rep)))
                alpha = jnp.exp(m_prev - m_next)
                m_s[sub] = m_next
                acc_s[sub] = acc_s[sub] * jnp.tile(alpha, (1, 2)) + _dot(p.astype(BF16), v2)

        @pl.when(j == last)
        def _():
            step(True)

        @pl.when(j < last)
        def _():
            step(False)

        @pl.when(j == nk - 1)
        def _():
            lane = _lane((tq, LANES))
            outs = []
            qab = qa_ref[...].astype(F32)
            for sub in range(2):
                acc = acc_s[sub]
                base = AUG_STRIDE * sub
                l = jnp.sum(jnp.where(lane == base + AUG_C, acc[:, LANES:], 0.0), axis=1, keepdims=True)
                outs.append(acc[:, :LANES] / l)
                lse = jnp.max(m_s[sub], axis=1, keepdims=True) + jnp.log(l)
                pieces = _split3(-lse)
                for e in range(3):
                    qab = jnp.where(lane == base + AUG_STAT + e, pieces[e].astype(F32), qab)
            o_ref[...] = jnp.where(lane < 64, outs[0], outs[1]).astype(o_ref.dtype)
            qab_ref[...] = qab.astype(BF16)

    qspec = pl.BlockSpec((tq, LANES), lambda p, i, j: (i, p))
    kspec = pl.BlockSpec((tk, LANES), lambda p, i, j: (jnp.minimum(j, last_of(i)), p))
    return pl.pallas_call(
        body, name="fox_fwd", grid=(4, nq, nk),
        in_specs=[qspec, qspec, kspec, kspec, kspec],
        out_specs=[qspec, qspec],
        out_shape=[jax.ShapeDtypeStruct((T, 512), BF16), jax.ShapeDtypeStruct((T, 512), BF16)],
        scratch_shapes=[pltpu.VMEM((2, tq, LANES), F32), pltpu.VMEM((2, tq, 2 * LANES), F32)],
        compiler_params=_cparams("parallel", "parallel", "arbitrary"),
    )(q, qaug, k, kaug, v)


def _fox_bwd(q, qaug, k, kaug, v, do, doaug, T, tq, tk):
    nq, nk = T // tq, T // tk
    first_of = lambda j: (j * tk) // tq

    def body(q_ref, qa_ref, k_ref, ka_ref, v_ref, do_ref, doa_ref,
             dq_ref, dqa_ref, dk_ref, dka_ref, dv_ref, dk_s, dv_s):
        p_, j, i = pl.program_id(0), pl.program_id(1), pl.program_id(2)
        masked = i * tq < (j + 1) * tk - 1

        @pl.when((j == 0) & (i == 0))
        def _():
            dq_ref[...] = jnp.zeros_like(dq_ref)
            dqa_ref[...] = jnp.zeros_like(dqa_ref)

        @pl.when(i == 0)
        def _():
            dk_s[...] = jnp.zeros_like(dk_s)
            dv_s[...] = jnp.zeros_like(dv_s)

        def step(diagonal):
            q2 = jnp.concatenate([q_ref[...], qa_ref[...]], axis=1)
            k2 = jnp.concatenate([k_ref[...], ka_ref[...]], axis=1)
            v2 = jnp.concatenate([v_ref[...], ka_ref[...]], axis=1)
            do2 = jnp.concatenate([do_ref[...], doa_ref[...]], axis=1)
            if diagonal:
                causal = (lax.broadcasted_iota(jnp.int32, (tq, tk), 1) + j * tk
                          <= lax.broadcasted_iota(jnp.int32, (tq, tk), 0) + i * tq)
            qh = [jnp.where(_fox_head_mask(sub, tq), q2, jnp.zeros_like(q2)) for sub in range(2)]
            doh = [jnp.where(_fox_head_mask(sub, tq), do2, jnp.zeros_like(do2)) for sub in range(2)]
            scores = [_dot(qh[sub], k2, NT) for sub in range(2)]
            dps = [_dot(doh[sub], v2, NT) for sub in range(2)]
            dqs = []
            for sub in range(2):
                s = scores[sub]
                if diagonal:
                    s = jnp.where(causal, s, NEG)
                p = jnp.exp(s)
                dsb = (p * dps[sub]).astype(BF16)
                dv_s[...] += _dot(p.astype(BF16), doh[sub][:, :LANES], TN)
                dk_s[...] += _dot(dsb, qh[sub], TN)
                dqs.append(_dot(dsb, k2))
            dq2 = jnp.where(_fox_head_mask(0, tq), dqs[0], dqs[1])
            qrows = pl.ds(pl.multiple_of(i * tq, tq), tq)
            dq_ref[qrows, :] += dq2[:, :LANES] * FOX_SCALE
            dqa_ref[qrows, :] += dq2[:, LANES:]

        @pl.when((i >= first_of(j)) & masked)
        def _():
            step(True)

        @pl.when((i >= first_of(j)) & jnp.logical_not(masked))
        def _():
            step(False)

        @pl.when(i == nq - 1)
        def _():
            dk_ref[...] = dk_s[:, :LANES]
            dka_ref[...] = dk_s[:, LANES:]
            dv_ref[...] = dv_s[...]

    qspec = pl.BlockSpec((tq, LANES), lambda p, j, i: (jnp.maximum(i, first_of(j)), p))
    kspec = pl.BlockSpec((tk, LANES), lambda p, j, i: (j, p))
    resident = pl.BlockSpec((T, LANES), lambda p, j, i: (0, p))
    return pl.pallas_call(
        body, name="fox_bwd", grid=(4, nk, nq),
        in_specs=[qspec, qspec, kspec, kspec, kspec, qspec, qspec],
        out_specs=[resident, resident, kspec, kspec, kspec],
        out_shape=[jax.ShapeDtypeStruct((T, 512), F32)] * 5,
        scratch_shapes=[pltpu.VMEM((tk, 2 * LANES), F32), pltpu.VMEM((tk, LANES), F32)],
        compiler_params=_cparams("arbitrary", "arbitrary", "arbitrary"),
    )(q, qaug, k, kaug, v, do, doaug)


SWA_SUB = 4
SWA_TB = SWA_SUB * WINDOW


def _t5_bucket_matrix():
    t = jnp.arange(WINDOW)[:, None] + WINDOW
    s = jnp.arange(2 * WINDOW)[None, :]
    max_exact = REL_BUCKETS // 2
    d = jnp.maximum(t - s, 0)
    df = jnp.maximum(d, 1).astype(F32)
    large = max_exact + (jnp.log(df / max_exact) / math.log(REL_MAX_DIST / max_exact)
                         * (REL_BUCKETS - max_exact)).astype(jnp.int32)
    large = jnp.minimum(large, REL_BUCKETS - 1)
    return jnp.where(d < max_exact, d, large).astype(jnp.int32)


def _swa_bias(rel_bias, bucket):
    def body(rel_ref, bucket_ref, o_ref):
        b = bucket_ref[...]
        for h in range(SWA_HEADS):
            acc = jnp.zeros(b.shape, F32)
            for r in range(REL_BUCKETS):
                acc = jnp.where(b == r, rel_ref[r, h], acc)
            o_ref[h] = acc

    return pl.pallas_call(
        body, name="swa_bias",
        in_specs=[pl.BlockSpec(memory_space=pltpu.SMEM), pl.BlockSpec(memory_space=pltpu.VMEM)],
        out_specs=pl.BlockSpec(memory_space=pltpu.VMEM),
        out_shape=jax.ShapeDtypeStruct((SWA_HEADS, WINDOW, 2 * WINDOW), F32),
    )(rel_bias, bucket)


def _swa_bias_bwd(dbias, bucket):
    def body(db_ref, bucket_ref, o_ref):
        b = bucket_ref[...]
        lane = _lane((1, LANES))
        for r in range(REL_BUCKETS):
            row = jnp.zeros((1, LANES), F32)
            for h in range(SWA_HEADS):
                part = jnp.sum(jnp.where(b == r, db_ref[h], 0.0), axis=0, keepdims=True)
                tot = jnp.sum(part, axis=1, keepdims=True)
                row = jnp.where(lane == h, tot, row)
            o_ref[r:r + 1, :] = row

    return pl.pallas_call(
        body, name="swa_bias_bwd",
        in_specs=[pl.BlockSpec(memory_space=pltpu.VMEM), pl.BlockSpec(memory_space=pltpu.VMEM)],
        out_specs=pl.BlockSpec(memory_space=pltpu.VMEM),
        out_shape=jax.ShapeDtypeStruct((REL_BUCKETS, LANES), F32),
    )(dbias, bucket)


SWA_GROUP = SWA_HEADS // SWA_KV_HEADS


def _swa_valid(r, i):
    t = (lax.broadcasted_iota(jnp.int32, (SWA_GROUP * WINDOW, 2 * WINDOW), 0) & (WINDOW - 1)) + WINDOW
    s = lax.broadcasted_iota(jnp.int32, (SWA_GROUP * WINDOW, 2 * WINDOW), 1)
    dist = t - s
    band = (dist >= 0) & (dist < WINDOW)
    if r == 0:
        band = band & ((s >= WINDOW) | (i > 0))
    return band


def _swa_stack(blk):
    lane = _lane((WINDOW, LANES))
    parts = []
    for g in range(SWA_GROUP):
        b = blk[:, LANES * (g // 2):LANES * (g // 2 + 1)]
        parts.append(jnp.where((lane >= 64) if g % 2 else (lane < 64), b, jnp.zeros_like(b)))
    return jnp.concatenate(parts, axis=0)


def _swa_unstack(st):
    lane = _lane((WINDOW, LANES))
    W = WINDOW
    return jnp.concatenate([jnp.where(lane < 64, st[2 * b * W:(2 * b + 1) * W], st[(2 * b + 1) * W:(2 * b + 2) * W])
                            for b in range(2)], axis=1)


def _swa_sink_column(sink_ref, kvh):
    row = lax.broadcasted_iota(jnp.int32, (SWA_GROUP * WINDOW, 1), 0)
    col = jnp.full((SWA_GROUP * WINDOW, 1), sink_ref[SWA_GROUP * kvh + SWA_GROUP - 1], F32)
    for g in range(SWA_GROUP - 2, -1, -1):
        col = jnp.where(row < (g + 1) * WINDOW, sink_ref[SWA_GROUP * kvh + g], col)
    return col


def _swa_specs(T):
    W = WINDOW
    qspec = pl.BlockSpec((SWA_TB, 2 * LANES), lambda h, i: (i, h))
    own = pl.BlockSpec((None, SWA_TB, LANES), lambda h, i: (h, i, 0))
    prev = pl.BlockSpec((None, W, LANES), lambda h, i: (h, jnp.maximum(SWA_SUB * i - 1, 0), 0))
    stat = pl.BlockSpec((SWA_GROUP, SWA_TB, LANES), lambda h, i: (h, i, 0))
    bias = pl.BlockSpec((None, SWA_GROUP * W, 2 * W), lambda h, i: (h, 0, 0))
    return qspec, own, prev, stat, bias


def _swa_fwd(sinks, q, kad, vad, bias, T):
    nb = T // SWA_TB
    scale = SWA_HEAD_DIM ** -0.5
    W = WINDOW

    def body(sink_ref, q_ref, k_ref, kp_ref, v_ref, vp_ref, bias_ref, o_ref, lse_ref):
        kvh, i = pl.program_id(0), pl.program_id(1)
        sink = _swa_sink_column(sink_ref, kvh)
        for r in range(SWA_SUB):
            rs = slice(r * W, (r + 1) * W)
            ps = slice((r - 1) * W, r * W)
            k_own, v_own = k_ref[rs, :], v_ref[rs, :]
            k_prev = kp_ref[...] if r == 0 else k_ref[ps, :]
            v_prev = vp_ref[...] if r == 0 else v_ref[ps, :]
            qs = _swa_stack(q_ref[rs, :])
            s = jnp.concatenate([_dot(qs, k_prev, NT), _dot(qs, k_own, NT)], axis=1) * scale + bias_ref[...]
            s = jnp.where(_swa_valid(r, i), s, NEG)
            m = jnp.maximum(jnp.max(s, axis=1, keepdims=True), sink)
            p = jnp.exp(s - m)
            denom = jnp.sum(p, axis=1, keepdims=True) + jnp.exp(sink - m)
            pn = (p / denom).astype(BF16)
            o_ref[rs, :] = _swa_unstack(_dot(pn[:, :W], v_prev) + _dot(pn[:, W:], v_own)).astype(o_ref.dtype)
            lse = m + jnp.log(denom)
            for g in range(SWA_GROUP):
                lse_ref[g, rs, :] = jnp.broadcast_to(lse[g * W:(g + 1) * W], (W, LANES))

    qspec, own, prev, stat, bspec = _swa_specs(T)
    return pl.pallas_call(
        body, name="swa_fwd", grid=(SWA_KV_HEADS, nb),
        in_specs=[pl.BlockSpec(memory_space=pltpu.SMEM), qspec, own, prev, own, prev, bspec],
        out_specs=[qspec, stat],
        out_shape=[jax.ShapeDtypeStruct((T, 512), BF16), jax.ShapeDtypeStruct((SWA_HEADS, T, LANES), F32)],
        compiler_params=_cparams("parallel", "parallel"),
    )(sinks, q, kad, kad, vad, vad, bias.reshape(SWA_KV_HEADS, SWA_GROUP * W, 2 * W))


def _swa_bwd(sinks, q, kad, vad, bias, do, lse, delta, T):
    nb = T // SWA_TB
    scale = SWA_HEAD_DIM ** -0.5
    W = WINDOW

    def body(sink_ref, q_ref, k_ref, kp_ref, v_ref, vp_ref, bias_ref, do_ref, lse_ref, dl_ref,
             dq_ref, dkad_ref, dvad_ref, dbias_ref, dsk_ref):
        kvh, i = pl.program_id(0), pl.program_id(1)
        sink = _swa_sink_column(sink_ref, kvh)

        @pl.when((kvh == 0) & (i == 0))
        def _():
            dkad_ref[...] = jnp.zeros_like(dkad_ref)
            dvad_ref[...] = jnp.zeros_like(dvad_ref)

        @pl.when(i == 0)
        def _():
            dbias_ref[...] = jnp.zeros_like(dbias_ref)
            dsk_ref[...] = jnp.zeros_like(dsk_ref)

        for r in range(SWA_SUB):
            rs = slice(r * W, (r + 1) * W)
            ps = slice((r - 1) * W, r * W)
            k_own, v_own = k_ref[rs, :], v_ref[rs, :]
            k_prev = kp_ref[...] if r == 0 else k_ref[ps, :]
            v_prev = vp_ref[...] if r == 0 else v_ref[ps, :]
            qs = _swa_stack(q_ref[rs, :])
            dos = _swa_stack(do_ref[rs, :])
            lse_b = jnp.concatenate([lse_ref[g, rs, :] for g in range(SWA_GROUP)], axis=0)
            dl_b = jnp.concatenate([dl_ref[g, rs, :] for g in range(SWA_GROUP)], axis=0)
            s = jnp.concatenate([_dot(qs, k_prev, NT), _dot(qs, k_own, NT)], axis=1) * scale + bias_ref[...]
            s = jnp.where(_swa_valid(r, i), s, NEG)
            p = jnp.exp(s - jnp.tile(lse_b, (1, 2)))
            dp = jnp.concatenate([_dot(dos, v_prev, NT), _dot(dos, v_own, NT)], axis=1)
            ds = p * (dp - jnp.tile(dl_b, (1, 2)))
            sink_term = jnp.exp(sink - lse_b) * dl_b
            for g in range(SWA_GROUP):
                dbias_ref[g] += ds[g * W:(g + 1) * W]
                dsk_ref[g:g + 1, :] += jnp.sum(sink_term[g * W:(g + 1) * W], axis=0, keepdims=True)
            dsb = ds.astype(BF16)
            pb = p.astype(BF16)
            dq_ref[rs, :] = _swa_unstack((_dot(dsb[:, :W], k_prev) + _dot(dsb[:, W:], k_own)) * scale)
            own_row = pl.multiple_of(i * SWA_TB + r * W, W)
            dkad_ref[kvh, pl.ds(own_row, W), :] += _dot(dsb[:, W:], qs, TN) * scale
            dvad_ref[kvh, pl.ds(own_row, W), :] += _dot(pb[:, W:], dos, TN)
            dk_prev = _dot(dsb[:, :W], qs, TN) * scale
            dv_prev = _dot(pb[:, :W], dos, TN)
            if r == 0:
                @pl.when(i > 0)
                def _():
                    prev_row = pl.multiple_of(i * SWA_TB - W, W)
                    dkad_ref[kvh, pl.ds(prev_row, W), :] += dk_prev
                    dvad_ref[kvh, pl.ds(prev_row, W), :] += dv_prev
            else:
                prev_row = pl.multiple_of(i * SWA_TB + (r - 1) * W, W)
                dkad_ref[kvh, pl.ds(prev_row, W), :] += dk_prev
                dvad_ref[kvh, pl.ds(prev_row, W), :] += dv_prev

    qspec, own, prev, stat, bspec = _swa_specs(T)
    full = pl.BlockSpec((SWA_KV_HEADS, T, LANES), lambda h, i: (0, 0, 0))
    return pl.pallas_call(
        body, name="swa_bwd", grid=(SWA_KV_HEADS, nb),
        in_specs=[pl.BlockSpec(memory_space=pltpu.SMEM), qspec, own, prev, own, prev, bspec, qspec, stat, stat],
        out_specs=[qspec, full, full, pl.BlockSpec((SWA_GROUP, W, 2 * W), lambda h, i: (h, 0, 0)),
                   pl.BlockSpec((None, 8, LANES), lambda h, i: (h, 0, 0))],
        out_shape=[jax.ShapeDtypeStruct((T, 512), F32), jax.ShapeDtypeStruct((SWA_KV_HEADS, T, LANES), F32),
                   jax.ShapeDtypeStruct((SWA_KV_HEADS, T, LANES), F32), jax.ShapeDtypeStruct((SWA_HEADS, W, 2 * W), F32),
                   jax.ShapeDtypeStruct((SWA_KV_HEADS, 8, LANES), F32)],
        compiler_params=_cparams("arbitrary", "arbitrary"),
    )(sinks, q, kad, kad, vad, vad, bias.reshape(SWA_KV_HEADS, SWA_GROUP * W, 2 * W), do, lse, delta)


def _mem_fwd(q, mk, mv, T, tq):
    scale = MEM_HEAD_DIM ** -0.5

    def body(q_ref, k_ref, v_ref, o_ref, lse_ref):
        s = _dot(q_ref[...], k_ref[...], NT) * scale
        m = jnp.max(s, axis=1, keepdims=True)
        p = jnp.exp(s - m)
        l = jnp.sum(p, axis=1, keepdims=True)
        o_ref[...] = _dot((p / l).astype(BF16), v_ref[...]).astype(o_ref.dtype)
        lse_ref[...] = jnp.broadcast_to(m + jnp.log(l), (tq, LANES))

    qspec = pl.BlockSpec((tq, LANES), lambda h, i: (i, h))
    kspec = pl.BlockSpec((N_MEM, LANES), lambda h, i: (0, h))
    return pl.pallas_call(
        body, name="mem_fwd", grid=(MEM_HEADS, T // tq),
        in_specs=[qspec, kspec, kspec],
        out_specs=[qspec, pl.BlockSpec((None, tq, LANES), lambda h, i: (h, i, 0))],
        out_shape=[jax.ShapeDtypeStruct((T, 512), BF16), jax.ShapeDtypeStruct((MEM_HEADS, T, LANES), F32)],
        compiler_params=_cparams("parallel", "parallel"),
    )(q, mk, mv)


def _mem_bwd(q, mk, mv, do, lse, delta, T, tq):
    scale = MEM_HEAD_DIM ** -0.5
    rep = N_MEM // LANES

    def body(q_ref, k_ref, v_ref, do_ref, lse_ref, dl_ref, dq_ref, dk_ref, dv_ref):
        i = pl.program_id(1)

        @pl.when(i == 0)
        def _():
            dk_ref[...] = jnp.zeros_like(dk_ref)
            dv_ref[...] = jnp.zeros_like(dv_ref)

        qv, dov = q_ref[...], do_ref[...]
        s = _dot(qv, k_ref[...], NT) * scale
        p = jnp.exp(s - jnp.tile(lse_ref[...], (1, rep)))
        dp = _dot(dov, v_ref[...], NT)
        ds = p * (dp - jnp.tile(dl_ref[...], (1, rep)))
        dsb = ds.astype(BF16)
        dq_ref[...] = _dot(dsb, k_ref[...]) * scale
        dk_ref[...] += _dot(dsb, qv, TN) * scale
        dv_ref[...] += _dot(p.astype(BF16), dov, TN)

    qspec = pl.BlockSpec((tq, LANES), lambda h, i: (i, h))
    kspec = pl.BlockSpec((N_MEM, LANES), lambda h, i: (0, h))
    stat = pl.BlockSpec((None, tq, LANES), lambda h, i: (h, i, 0))
    return pl.pallas_call(
        body, name="mem_bwd", grid=(MEM_HEADS, T // tq),
        in_specs=[qspec, kspec, kspec, qspec, stat, stat],
        out_specs=[qspec, kspec, kspec],
        out_shape=[jax.ShapeDtypeStruct((T, 512), F32), jax.ShapeDtypeStruct((N_MEM, 512), F32),
                   jax.ShapeDtypeStruct((N_MEM, 512), F32)],
        compiler_params=_cparams("arbitrary", "arbitrary"),
    )(q, mk, mv, do, lse, delta)


def _mem_prep_fwd(mem, g_mem, w_kv, kn_gain, gm128):
    def body(mem_ref, g_ref, w_ref, kn_ref, gm_ref, memn_o, kv_o, mk_o, mv_o):
        xhat, _ = _rms_rows(mem_ref[...], None)
        memn = (xhat * g_ref[...]).astype(BF16)
        memn_o[...] = memn
        kv = _dot(memn, w_ref[...])
        kv_o[...] = kv
        gm = gm_ref[...]
        for c in range(4):
            sl = slice(c * LANES, (c + 1) * LANES)
            y, _ = _head_norm(kv[:, sl], gm, kn_ref[...])
            mk_o[:, sl] = y.astype(BF16)
        mv_o[...] = kv[:, 512:].astype(BF16)

    vm = pl.BlockSpec(memory_space=pltpu.VMEM)
    return pl.pallas_call(
        body, name="mem_prep_fwd", in_specs=[vm] * 5, out_specs=[vm] * 4,
        out_shape=[jax.ShapeDtypeStruct((N_MEM, D_MODEL), BF16), jax.ShapeDtypeStruct((N_MEM, D_MODEL), F32),
                   jax.ShapeDtypeStruct((N_MEM, 512), BF16), jax.ShapeDtypeStruct((N_MEM, 512), BF16)],
        compiler_params=pltpu.CompilerParams(vmem_limit_bytes=VMEM_LIMIT),
    )(mem, g_mem, w_kv, kn_gain, gm128)


def _mem_prep_bwd(mem, g_mem, memn, kv, w_kv, kn_gain, gm128, dmk, dmv):
    def body(mem_ref, g_ref, memn_ref, kv_ref, w_ref, kn_ref, gm_ref, dmk_ref, dmv_ref, dw_o, dg_o, dkn_o, dkv_s):
        gm = gm_ref[...]
        dkn = jnp.zeros((1, LANES), F32)
        for c in range(4):
            sl = slice(c * LANES, (c + 1) * LANES)
            dx, dg = _head_norm_bwd(dmk_ref[:, sl], kv_ref[:, sl], gm, kn_ref[...])
            dkv_s[:, sl] = dx.astype(BF16)
            dkn = dkn + dg
        dkn_o[...] = dkn
        dkv_s[:, 512:] = dmv_ref[...].astype(BF16)
        dkv = dkv_s[...]
        dw_o[...] = _dot(memn_ref[...], dkv, TN)
        dmemn = _dot(dkv, w_ref[...], NT)
        xhat, _ = _rms_rows(mem_ref[...], None)
        dg_o[...] = jnp.sum(dmemn * xhat, axis=0, keepdims=True)

    vm = pl.BlockSpec(memory_space=pltpu.VMEM)
    return pl.pallas_call(
        body, name="mem_prep_bwd", in_specs=[vm] * 9, out_specs=[vm] * 3,
        out_shape=[jax.ShapeDtypeStruct((D_MODEL, D_MODEL), F32), jax.ShapeDtypeStruct((1, D_MODEL), F32),
                   jax.ShapeDtypeStruct((1, LANES), F32)],
        scratch_shapes=[pltpu.VMEM((N_MEM, D_MODEL), BF16)],
        compiler_params=pltpu.CompilerParams(vmem_limit_bytes=VMEM_LIMIT),
    )(mem, g_mem, memn, kv, w_kv, kn_gain, gm128, dmk, dmv)


SLOT_O = D_MODEL // N_SHARD


def _merge_fwd(proj, b_gate, o3, w3, T, tb):
    def body(gl_ref, bg_ref, oa_ref, of_ref, om_ref, wa_ref, wf_ref, wm_ref, out_ref):
        o_refs = (oa_ref, of_ref, om_ref)
        w_refs = (wa_ref, wf_ref, wm_ref)
        for n in range(N_SHARD):
            acc = jnp.zeros((tb, SLOT_O), F32)
            for b in range(3):
                c0 = b * D_MODEL + n * SLOT_O
                g = jax.nn.sigmoid(gl_ref[:, c0:c0 + SLOT_O] + bg_ref[:, c0:c0 + SLOT_O])
                acc = acc + g * _dot(o_refs[b][...], w_refs[b][n])
            out_ref[:, n * SLOT_O:(n + 1) * SLOT_O] = acc.astype(out_ref.dtype)

    rows = pl.BlockSpec((tb, 512), lambda i: (i, 0))
    wspec = pl.BlockSpec((N_SHARD, 512, SLOT_O), lambda i: (0, 0, 0))
    return pl.pallas_call(
        body, name="merge_fwd", grid=(T // tb,),
        in_specs=[pl.BlockSpec((tb, GATE_W), lambda i: (i, 1)), pl.BlockSpec((1, GATE_W), lambda i: (0, 0)),
                  rows, rows, rows, wspec, wspec, wspec],
        out_specs=pl.BlockSpec((tb, D_MODEL), lambda i: (i, 0)),
        out_shape=jax.ShapeDtypeStruct((T, D_MODEL), BF16),
        compiler_params=_cparams("parallel"),
    )(proj, b_gate, *o3, *w3)


def _merge_bwd(proj, b_gate, o3, w3, dmerged, T, tb):
    heads = (SWA_HEADS, FOX_HEADS, MEM_HEADS)

    def body(gl_ref, bg_ref, oa_ref, of_ref, om_ref, wa_ref, wf_ref, wm_ref, dm_ref,
             dgl_o, doa_o, dof_o, dom_o, dla_o, dlf_o, dlm_o, dwa_o, dwf_o, dwm_o, dbg_o):
        i = pl.program_id(0)
        o_refs = (oa_ref, of_ref, om_ref)
        w_refs = (wa_ref, wf_ref, wm_ref)
        do_refs = (doa_o, dof_o, dom_o)
        dl_refs = (dla_o, dlf_o, dlm_o)
        dw_refs = (dwa_o, dwf_o, dwm_o)

        @pl.when(i == 0)
        def _():
            for r in dw_refs:
                r[...] = jnp.zeros_like(r)
            dbg_o[...] = jnp.zeros_like(dbg_o)

        lane = _lane((tb, LANES))
        for b in range(3):
            ob = o_refs[b][...]
            do = jnp.zeros((tb, 512), F32)
            for n in range(N_SHARD):
                c0 = b * D_MODEL + n * SLOT_O
                g = jax.nn.sigmoid(gl_ref[:, c0:c0 + SLOT_O] + bg_ref[:, c0:c0 + SLOT_O])
                dm = dm_ref[:, n * SLOT_O:(n + 1) * SLOT_O]
                y = _dot(ob, w_refs[b][n])
                dgl = dm * y * g * (1.0 - g)
                dgl_o[:, c0:c0 + SLOT_O] = dgl.astype(dgl_o.dtype)
                dbg_o[:, c0:c0 + SLOT_O] += jnp.sum(dgl, axis=0, keepdims=True)
                dy = (dm * g).astype(BF16)
                do = do + _dot(dy, w_refs[b][n], NT)
                dw_refs[b][n] += _dot(ob, dy, TN)
            do_refs[b][...] = do.astype(BF16)
            prod = do * ob.astype(F32)
            for c in range(4):
                blk = prod[:, c * LANES:(c + 1) * LANES]
                if heads[b] == 8:
                    lo = jnp.sum(jnp.where(lane < 64, blk, 0.0), axis=1, keepdims=True)
                    hi = jnp.sum(jnp.where(lane >= 64, blk, 0.0), axis=1, keepdims=True)
                    if b == 1:
                        aug = jnp.zeros((tb, LANES), F32)
                        for sub, dl in enumerate((lo, hi)):
                            for e, piece in enumerate(_split3(-dl)):
                                aug = jnp.where(lane == AUG_STRIDE * sub + AUG_C + e, piece.astype(F32), aug)
                        dl_refs[b][:, c * LANES:(c + 1) * LANES] = aug.astype(BF16)
                    else:
                        dl_refs[b][2 * c] = jnp.broadcast_to(lo, (tb, LANES))
                        dl_refs[b][2 * c + 1] = jnp.broadcast_to(hi, (tb, LANES))
                else:
                    dl_refs[b][c] = jnp.broadcast_to(jnp.sum(blk, axis=1, keepdims=True), (tb, LANES))

    rows = pl.BlockSpec((tb, 512), lambda i: (i, 0))
    wspec = pl.BlockSpec((N_SHARD, 512, SLOT_O), lambda i: (0, 0, 0))
    stat = lambda h: pl.BlockSpec((h, tb, LANES), lambda i: (0, i, 0))
    return pl.pallas_call(
        body, name="merge_bwd", grid=(T // tb,),
        in_specs=[pl.BlockSpec((tb, GATE_W), lambda i: (i, 1)), pl.BlockSpec((1, GATE_W), lambda i: (0, 0)),
                  rows, rows, rows, wspec, wspec, wspec, pl.BlockSpec((tb, D_MODEL), lambda i: (i, 0))],
        out_specs=[pl.BlockSpec((tb, GATE_W), lambda i: (i, 0)), rows, rows, rows,
                   stat(8), rows, stat(4), wspec, wspec, wspec, pl.BlockSpec((1, GATE_W), lambda i: (0, 0))],
        out_shape=[jax.ShapeDtypeStruct((T, GATE_W), BF16)] + [jax.ShapeDtypeStruct((T, 512), BF16)] * 3
        + [jax.ShapeDtypeStruct((8, T, LANES), F32), jax.ShapeDtypeStruct((T, 512), BF16),
           jax.ShapeDtypeStruct((4, T, LANES), F32)]
        + [jax.ShapeDtypeStruct((N_SHARD, 512, SLOT_O), F32)] * 3 + [jax.ShapeDtypeStruct((1, GATE_W), F32)],
        compiler_params=_cparams("arbitrary"),
    )(proj, b_gate, *o3, *w3, dmerged)


def _local_step(x, mem, tgt, small, g_in, w_kv, w_o3, w_out, w_up, w_down, reducer):
    T = x.shape[0]
    tm = min(512, T)
    tile2 = lambda v: jnp.tile(v.reshape(1, -1), (1, LANES // v.size))
    gains = jnp.concatenate([tile2(small["qn_swa"]), tile2(small["kn_swa"]), tile2(small["qn_fox"]),
                             tile2(small["kn_fox"]), tile2(small["qn_mem"]), jnp.zeros((3, LANES), F32)], axis=0)
    kn_mem = small["kn_mem"].reshape(1, LANES)
    bfor = jnp.pad(small["b_forget"].reshape(1, -1), ((0, 0), (0, LANES - FOX_HEADS)))
    gm64 = _group_mean_matrix(64)
    gm128 = _group_mean_matrix(128)
    tb_prep = min(256, T)
    ones = jnp.ones((tb_prep, tb_prep), F32)
    tril = jnp.tril(ones).astype(BF16)
    triu = jnp.triu(ones).astype(BF16)
    bucket = _t5_bucket_matrix()
    g_mix, g_mlp, g_mem = small["g_mix"], small["g_mlp"], small["g_mem"]
    b_gate = small["b_gate"]
    sinks = small["sink_swa"].reshape(-1)

    tl = min(1024, T)
    sq = pl.BlockSpec((tl, D_MODEL), lambda i, j, k: (i, j))
    h, g_in = reducer.first_inputs(_rmsnorm("rms_mix", x, g_mix, tm), g_in)
    wc = _w_in_to_segments(g_in)
    (proj,) = _matmul(
        "mm_proj", h, wc, dims=NN, grid=(T // tl, PROJ_W // D_MODEL, 1),
        a_spec=pl.BlockSpec((tl, D_MODEL), lambda i, j, k: (i, 0)),
        b_spec=pl.BlockSpec((D_MODEL, D_MODEL), lambda i, j, k: (0, j)),
        acc_shape=(tl, D_MODEL),
        outs=[(jax.ShapeDtypeStruct((T, PROJ_W), F32), sq)],
        epilogue=_epi_store)
    qa, qf, kf, vf, qm, kad, vad, qf_aug, kf_aug = _prep_fwd(proj, gains, bfor, tril, gm64, gm128, T, tb_prep)
    bias = _swa_bias(small["rel_bias"], bucket)
    o_swa, lse_swa = _swa_fwd(sinks, qa, kad, vad, bias, T)
    o_fox, qf_aug_bwd = _fox_fwd(qf, qf_aug, kf, kf_aug, vf, T, min(FOX_TQ, T), min(FOX_TK, T))
    memn, kv, mk, mv = _mem_prep_fwd(mem, g_mem, w_kv, kn_mem, gm128)
    o_mem, lse_mem = _mem_fwd(qm, mk, mv, T, tm)
    o3 = (o_swa, o_fox, o_mem)
    merged = _merge_fwd(proj, b_gate, o3, w_o3, T, min(256, T))

    def epi_residual(acc, extra_refs, out_refs, ij):
        out_refs[0][...] = extra_refs[0][...] + acc

    row_full = pl.BlockSpec((tm, D_MODEL), lambda i, j, k: (i, 0))
    row_big = pl.BlockSpec((tl, D_MODEL), lambda i, j, k: (i, 0))
    whole = pl.BlockSpec((D_MODEL, D_MODEL), lambda i, j, k: (0, 0))
    (x2,) = _matmul(
        "mm_out", merged, w_out, dims=NN, grid=(T // tl, 1, 1),
        a_spec=row_big, b_spec=whole,
        acc_shape=(tl, D_MODEL), extra=[(x, row_big)],
        outs=[(jax.ShapeDtypeStruct((T, D_MODEL), F32), row_big)], epilogue=epi_residual)
    hm = _rmsnorm("rms_mlp", x2, g_mlp, tm)

    def epi_relu2(acc, extra_refs, out_refs, ij):
        out_refs[0][...] = acc
        r = jnp.maximum(acc, 0.0)
        out_refs[1][...] = (r * r).astype(BF16)

    up, u = _matmul(
        "mm_up", hm, w_up, dims=NN, grid=(T // tl, N_SHARD, 1),
        a_spec=row_big, b_spec=pl.BlockSpec((None, D_MODEL, D_MODEL), lambda i, j, k: (j, 0, 0)),
        acc_shape=(tl, D_MODEL),
        outs=[(jax.ShapeDtypeStruct((T, D_FF), F32), sq), (jax.ShapeDtypeStruct((T, D_FF), BF16), sq)],
        epilogue=epi_relu2)

    def epi_loss(acc, extra_refs, out_refs, ij):
        y = extra_refs[0][...] + acc
        err = y - extra_refs[1][...]
        dyv = err * (1.0 / D_MODEL)
        out_refs[0][...] = dyv
        out_refs[2][...] = dyv.astype(BF16)
        sq = jnp.sum(jnp.sum(err * err, axis=1, keepdims=True), axis=0, keepdims=True)

        @pl.when(ij[0] == 0)
        def _():
            out_refs[1][...] = jnp.zeros_like(out_refs[1])

        out_refs[1][...] += jnp.broadcast_to(sq, out_refs[1].shape)

    kblk = pl.BlockSpec((tl, D_MODEL), lambda i, j, k: (i, k))
    dy, loss_acc, dy_bf = _matmul(
        "mm_down", u, w_down, dims=NN, grid=(T // tl, 1, N_SHARD),
        a_spec=kblk, b_spec=pl.BlockSpec((D_MODEL, D_MODEL), lambda i, j, k: (k, 0)),
        acc_shape=(tl, D_MODEL), extra=[(x2, row_big), (tgt, row_big)],
        outs=[(jax.ShapeDtypeStruct((T, D_MODEL), F32), row_big),
              (jax.ShapeDtypeStruct((8, LANES), F32), pl.BlockSpec((8, LANES), lambda i, j, k: (0, 0))),
              (jax.ShapeDtypeStruct((T, D_MODEL), BF16), row_big)],
        epilogue=epi_loss)
    loss = loss_acc[0, 0] * (0.5 / D_MODEL)

    def epi_dup(acc, extra_refs, out_refs, ij):
        out_refs[0][...] = (acc * (2.0 * jnp.maximum(extra_refs[0][...], 0.0))).astype(BF16)

    (dup,) = _matmul(
        "mm_dup", dy_bf, w_down, dims=NT, grid=(T // tl, N_SHARD, 1),
        a_spec=row_big, b_spec=pl.BlockSpec((D_MODEL, D_MODEL), lambda i, j, k: (j, 0)),
        acc_shape=(tl, D_MODEL), extra=[(up, sq)],
        outs=[(jax.ShapeDtypeStruct((T, D_FF), BF16), sq)], epilogue=epi_dup)

    nkt = T // tl
    t_rows = pl.BlockSpec((tl, D_MODEL), lambda i, j, k: (k, i))
    t_cols = pl.BlockSpec((tl, D_MODEL), lambda i, j, k: (k, j))
    (d_w_down,) = _matmul(
        "mm_dw_down", u, dy_bf, dims=TN, grid=(N_SHARD, 1, nkt),
        a_spec=t_rows, b_spec=t_cols, acc_shape=(D_MODEL, D_MODEL),
        outs=[(jax.ShapeDtypeStruct((D_FF, D_MODEL), F32), pl.BlockSpec((D_MODEL, D_MODEL), lambda i, j, k: (i, 0)))],
        epilogue=_epi_store)
    (d_w_up,) = _matmul(
        "mm_dw_up", hm, dup, dims=TN, grid=(1, N_SHARD, nkt),
        a_spec=t_rows, b_spec=t_cols, acc_shape=(D_MODEL, D_MODEL),
        outs=[(jax.ShapeDtypeStruct((N_SHARD, D_MODEL, D_MODEL), F32),
               pl.BlockSpec((None, D_MODEL, D_MODEL), lambda i, j, k: (j, 0, 0)))],
        epilogue=_epi_store)

    def epi_rms_bwd(acc, extra_refs, out_refs, ij):
        dx, dg = _rmsnorm_bwd_rows(acc, extra_refs[0][...], extra_refs[1][...])
        out_refs[0][...] = dx + extra_refs[2][...]

        @pl.when(ij[0] == 0)
        def _():
            out_refs[1][...] = jnp.zeros_like(out_refs[1])

        out_refs[1][...] += dg

    gain_spec = pl.BlockSpec((1, D_MODEL), lambda i, j, k: (0, 0))
    dx2, d_g_mlp = _matmul(
        "mm_dhm", dup, w_up, dims=NT, grid=(T // tl, 1, N_SHARD),
        a_spec=kblk, b_spec=pl.BlockSpec((None, D_MODEL, D_MODEL), lambda i, j, k: (k, 0, 0)),
        acc_shape=(tl, D_MODEL), extra=[(x2, row_big), (g_mlp, gain_spec), (dy, row_big)],
        outs=[(jax.ShapeDtypeStruct((T, D_MODEL), F32), row_big), (jax.ShapeDtypeStruct((1, D_MODEL), F32), gain_spec)],
        epilogue=epi_rms_bwd)

    (dmerged,) = _matmul(
        "mm_dmerged", dx2, w_out, dims=NT, grid=(T // tl, 1, 1),
        a_spec=row_big, b_spec=whole,
        acc_shape=(tl, D_MODEL), outs=[(jax.ShapeDtypeStruct((T, D_MODEL), F32), row_big)], epilogue=_epi_store)
    (d_w_out,) = _matmul(
        "mm_dw_out", merged, dx2, dims=TN, grid=(1, 1, nkt),
        a_spec=t_rows, b_spec=t_cols, acc_shape=(D_MODEL, D_MODEL),
        outs=[(jax.ShapeDtypeStruct((D_MODEL, D_MODEL), F32), whole)],
        epilogue=_epi_store)
    dmerged = reducer.early_start({"w_mlp_down": d_w_down, "w_mlp_up": d_w_up, "w_out": d_w_out}, dmerged)
    (dgl, do_swa, do_fox, do_mem, dl_swa, do_fox_aug, dl_mem, d_wo_swa, d_wo_fox, d_wo_mem, d_b_gate) = _merge_bwd(
        proj, b_gate, o3, w_o3, dmerged, T, min(256, T))
    do_fox = reducer.early_send(do_fox)

    dqa, dkad, dvad, dbias, dsk = _swa_bwd(sinks, qa, kad, vad, bias, do_swa, lse_swa, dl_swa, T)
    dqf, dqf_aug, dkf, dkf_aug, dvf = _fox_bwd(qf, qf_aug_bwd, kf, kf_aug, vf, do_fox, do_fox_aug, T,
                                               min(FOX_BWD_TQ, T), min(FOX_BWD_TK, T))
    dvf = reducer.early_finish(dvf)
    dqm, dmk, dmv = _mem_bwd(qm, mk, mv, do_mem, lse_mem, dl_mem, T, tm)
    d_w_kv, d_g_mem, d_kn_mem = _mem_prep_bwd(mem, g_mem, memn, kv, w_kv, kn_mem, gm128, dmk, dmv)
    d_rel = _swa_bias_bwd(dbias, bucket)
    dlo, gacc = _prep_bwd(proj, dqa, dkad, dvad, dqf, dkf, dvf, dqm, dqf_aug, dkf_aug, gains, bfor, triu, gm64, gm128,
                          T, tb_prep)

    def dwc_half(name, dpart):
        (res,) = _matmul(
            name, h, dpart, dims=TN, grid=(1, LO_W // D_MODEL, nkt),
            a_spec=t_rows, b_spec=t_cols, acc_shape=(D_MODEL, D_MODEL),
            outs=[(jax.ShapeDtypeStruct((D_MODEL, LO_W), F32), pl.BlockSpec((D_MODEL, D_MODEL), lambda i, j, k: (0, j)))],
            epilogue=_epi_store)
        return res

    d_wc_lo = dwc_half("mm_dwc_lo", dlo)
    d_wc_gl = dwc_half("mm_dwc_gl", dgl)
    dlo = reducer.late_start({"wc_lo": d_wc_lo, "wc_gl": d_wc_gl, "w_mem_kv": d_w_kv, "w_o_swa": d_wo_swa,
                              "w_o_fox": d_wo_fox, "w_o_mem": d_wo_mem}, dlo)
    (dh_lo,) = _matmul(
        "mm_dh_lo", dlo, wc, dims=NT, grid=(T // tl, 1, LO_W // D_MODEL),
        a_spec=kblk, b_spec=pl.BlockSpec((D_MODEL, D_MODEL), lambda i, j, k: (0, k)),
        acc_shape=(tl, D_MODEL), outs=[(jax.ShapeDtypeStruct((T, D_MODEL), F32), row_big)], epilogue=_epi_store)
    dh_lo = reducer.late_send(dh_lo)

    def epi_dx(acc, extra_refs, out_refs, ij):
        dhh = acc + extra_refs[3][...]
        dx, dg = _rmsnorm_bwd_rows(dhh, extra_refs[0][...], extra_refs[1][...])
        out_refs[0][...] = dx + extra_refs[2][...]

        @pl.when(ij[0] == 0)
        def _():
            out_refs[1][...] = jnp.zeros_like(out_refs[1])

        out_refs[1][...] += dg

    grad_x, d_g_mix = _matmul(
        "mm_dh_gl", dgl, wc, dims=NT, grid=(T // tm, 1, GATE_W // D_MODEL),
        a_spec=pl.BlockSpec((tm, D_MODEL), lambda i, j, k: (i, k)),
        b_spec=pl.BlockSpec((D_MODEL, D_MODEL), lambda i, j, k: (0, k + LO_W // D_MODEL)),
        acc_shape=(tm, D_MODEL), extra=[(x, row_full), (g_mix, gain_spec), (dx2, row_full), (dh_lo, row_full)],
        outs=[(jax.ShapeDtypeStruct((T, D_MODEL), F32), row_full), (jax.ShapeDtypeStruct((1, D_MODEL), F32), gain_spec)],
        epilogue=epi_dx)

    fold64 = lambda row: (row[:64] + row[64:]).reshape(1, 64)
    grads = {
        "g_mix": d_g_mix, "b_gate": d_b_gate, "b_forget": gacc[5, :FOX_HEADS].reshape(1, FOX_HEADS),
        "qn_swa": fold64(gacc[0]), "kn_swa": fold64(gacc[1]),
        "sink_swa": -dsk[:, :SWA_GROUP, 0].reshape(1, SWA_HEADS), "rel_bias": d_rel[:, :SWA_HEADS],
        "qn_fox": fold64(gacc[2]), "kn_fox": fold64(gacc[3]),
        "g_mem": d_g_mem, "qn_mem": gacc[4].reshape(1, LANES), "kn_mem": d_kn_mem, "g_mlp": d_g_mlp,
    }
    return loss, grad_x, grads


MESH = pl.DeviceIdType.MESH
ANY = pl.BlockSpec(memory_space=pl.ANY)


def _place():
    x, y, c = lax.axis_index("x"), lax.axis_index("y"), lax.axis_index("c")
    chips = [(1 - x, y), (x, 1 - y), (1 - x, 1 - y)]
    return x, y, c, chips


def _handshake(peers):
    barrier = pltpu.get_barrier_semaphore()
    for peer in peers:
        pl.semaphore_signal(barrier, inc=1, device_id=peer, device_id_type=MESH)
    pl.semaphore_wait(barrier, len(peers))


def _all_gather_shards_async(name, collective_id, slots):
    n = len(slots)
    bufs = [jax.new_ref(s, memory_space=pltpu.MemorySpace.HBM) for s in slots]

    def body(ici_send, ici_recv, d2d_send, d2d_recv):
        x, y, c, chips = _place()
        sibling = (x, y, 1 - c)
        me = 2 * x + y
        _handshake([(px, py, c) for px, py in chips] + [sibling])

        def half(a, who):
            hr = slots[a].shape[1] // 2
            return pl.ds(pl.multiple_of(who * hr, hr), hr)

        def ici(a, j, slot, to):
            return pltpu.make_async_remote_copy(
                src_ref=bufs[a].at[me, half(a, c)], dst_ref=bufs[a].at[slot, half(a, c)],
                send_sem=ici_send.at[3 * a + j], recv_sem=ici_recv.at[3 * a + j], device_id=to, device_id_type=MESH)

        def d2d(a, j, slot, which):
            part = bufs[a].at[slot, half(a, which)]
            return pltpu.make_async_remote_copy(
                src_ref=part, dst_ref=part, send_sem=d2d_send.at[3 * a + j], recv_sem=d2d_recv.at[3 * a + j],
                device_id=sibling, device_id_type=MESH)

        sends = [ici(a, j, me, (*chip, c)) for a in range(n) for j, chip in enumerate(chips)]
        for cp in sends:
            cp.start()
        passed = []
        for a in range(n):
            for j, (px, py) in enumerate(chips):
                ici(a, j, 2 * px + py, (px, py, c)).wait_recv()
                cp = d2d(a, j, 2 * px + py, c)
                cp.start()
                passed.append(cp)
        for a in range(n):
            for j, (px, py) in enumerate(chips):
                d2d(a, j, 2 * px + py, 1 - c).wait_recv()
        for cp in sends + passed:
            cp.wait_send()

    pl.kernel(
        body, mesh=plsc.ScalarSubcoreMesh(axis_name="seq", num_cores=1), name=name,
        scratch_types=[pltpu.SemaphoreType.DMA((3 * n,))] * 4,
        compiler_params=pltpu.CompilerParams(collective_id=collective_id),
    )()
    return [b[...] for b in bufs]


def _sequencer_call(name, collective_id, n_sems, body):
    pl.kernel(
        body, mesh=plsc.ScalarSubcoreMesh(axis_name="seq", num_cores=1), name=name,
        scratch_types=[pltpu.SemaphoreType.DMA((n_sems,))] * 2,
        compiler_params=pltpu.CompilerParams(collective_id=collective_id),
    )()


def _hbm_ref(value):
    return jax.new_ref(value, memory_space=pltpu.MemorySpace.HBM)


def _pair_exchange(name, collective_id, gs):
    n = len(gs)
    src = [_hbm_ref(g) for g in gs]
    stage = [jax.empty_ref(jax.ShapeDtypeStruct((N_SHARD, g.shape[1] // 2, g.shape[2]), g.dtype),
                           memory_space=pltpu.MemorySpace.HBM) for g in gs]

    def body(send_sem, recv_sem):
        x, y, c, _ = _place()
        sibling = (x, y, 1 - c)
        _handshake([sibling])
        copies = []
        for a in range(n):
            hr = gs[a].shape[1] // 2
            theirs = pl.ds(pl.multiple_of((1 - c) * hr, hr), hr)
            copies.append(pltpu.make_async_remote_copy(
                src_ref=src[a].at[:, theirs, :], dst_ref=stage[a], send_sem=send_sem.at[a], recv_sem=recv_sem.at[a],
                device_id=sibling, device_id_type=MESH))
        for cp in copies:
            cp.start()
        for cp in copies:
            cp.wait()

    _sequencer_call(name, collective_id, n, body)
    return [s[...] for s in stage]


def _chip_exchange(name, collective_id, sums):
    n = len(sums)
    src = [_hbm_ref(s) for s in sums]
    got = [jax.empty_ref(jax.ShapeDtypeStruct((3,) + s.shape[1:], s.dtype), memory_space=pltpu.MemorySpace.HBM)
           for s in sums]

    def body(send_sem, recv_sem):
        x, y, c, chips = _place()
        _handshake([(px, py, c) for px, py in chips])
        copies = []
        for a in range(n):
            for j, (px, py) in enumerate(chips):
                copies.append(pltpu.make_async_remote_copy(
                    src_ref=src[a].at[2 * px + py], dst_ref=got[a].at[j],
                    send_sem=send_sem.at[3 * a + j], recv_sem=recv_sem.at[3 * a + j],
                    device_id=(px, py, c), device_id_type=MESH))
        for cp in copies:
            cp.start()
        for cp in copies:
            cp.wait()

    _sequencer_call(name, collective_id, 3 * n, body)
    return [g[...] for g in got]


def _pair_gather(name, collective_id, fulls):
    n = len(fulls)
    full = [_hbm_ref(f) for f in fulls]

    def body(send_sem, recv_sem):
        x, y, c, _ = _place()
        sibling = (x, y, 1 - c)
        _handshake([sibling])
        copies = []
        for a in range(n):
            hr = fulls[a].shape[0] // 2
            mine = full[a].at[pl.ds(pl.multiple_of(c * hr, hr), hr)]
            copies.append(pltpu.make_async_remote_copy(
                src_ref=mine, dst_ref=mine, send_sem=send_sem.at[a], recv_sem=recv_sem.at[a],
                device_id=sibling, device_id_type=MESH))
        for cp in copies:
            cp.start()
        for cp in copies:
            cp.wait()

    _sequencer_call(name, collective_id, n, body)
    return [f[...] for f in full]


ELEMENTWISE_BLOCK_ELEMS = 256 * 1024


def _row_block(rows, cols):
    rb = 8
    while rb * 2 * cols <= ELEMENTWISE_BLOCK_ELEMS and rb * 2 <= rows:
        rb *= 2
    return rb


def _pair_sum(name, place, g, stage):
    _, R, C = g.shape
    hr = R // 2
    rb = _row_block(hr, C)
    nb = hr // rb

    def body(place_ref, g_ref, st_ref, sum_bf, own_f32):
        s = pl.program_id(1)
        tot = g_ref[...] + st_ref[...]
        sum_bf[...] = tot.astype(BF16)

        @pl.when(s == place_ref[0])
        def _():
            own_f32[...] = tot

    return pl.pallas_call(
        body, name=name,
        grid_spec=pltpu.PrefetchScalarGridSpec(
            num_scalar_prefetch=1, grid=(nb, N_SHARD),
            in_specs=[pl.BlockSpec((None, rb, C), lambda i, s, pr: (s, pr[1] * nb + i, 0)),
                      pl.BlockSpec((None, rb, C), lambda i, s, pr: (s, i, 0))],
            out_specs=[pl.BlockSpec((None, rb, C), lambda i, s, pr: (s, i, 0)),
                       pl.BlockSpec((rb, C), lambda i, s, pr: (i, 0))]),
        out_shape=[jax.ShapeDtypeStruct((N_SHARD, hr, C), BF16), jax.ShapeDtypeStruct((hr, C), F32)],
        compiler_params=_cparams("arbitrary", "arbitrary"),
    )(place, g, stage)


def _final_sum(name, place, own, got):
    hr, C = own.shape
    rb = _row_block(hr, C)
    nb = hr // rb

    def body(place_ref, own_ref, got_ref, o_ref):
        o_ref[...] = ((own_ref[...] + got_ref[0].astype(F32)) + got_ref[1].astype(F32)) + got_ref[2].astype(F32)

    return pl.pallas_call(
        body, name=name,
        grid_spec=pltpu.PrefetchScalarGridSpec(
            num_scalar_prefetch=1, grid=(nb,),
            in_specs=[pl.BlockSpec((rb, C), lambda i, pr: (i, 0)), pl.BlockSpec((3, rb, C), lambda i, pr: (0, i, 0))],
            out_specs=pl.BlockSpec((rb, C), lambda i, pr: (pr[1] * nb + i, 0))),
        out_shape=jax.ShapeDtypeStruct((2 * hr, C), F32),
        compiler_params=_cparams("arbitrary"),
    )(place, own, got)


def _adamw_math(w, g, m, v):
    m = ADAM_B1 * m + (1.0 - ADAM_B1) * g
    v = ADAM_B2 * v + (1.0 - ADAM_B2) * (g * g)
    m_hat = m / (1.0 - ADAM_B1 ** ADAM_STEP)
    v_hat = v / (1.0 - ADAM_B2 ** ADAM_STEP)
    delta = -ADAM_LR * (m_hat / (jnp.sqrt(v_hat) + ADAM_EPS) + ADAM_WD * w)
    return delta, m, v


def _adamw(name, w, g, m, v):
    R, Cw = w.shape
    Cg = g.shape[1]
    rb = _row_block(R, Cg)

    def body(w_ref, g_ref, m_ref, v_ref, g_o, d_o, m_o, v_o):
        gv = g_ref[...]
        delta, mn, vn = _adamw_math(w_ref[...], gv, m_ref[...], v_ref[...])
        g_o[...] = gv
        d_o[...] = delta
        m_o[...] = mn
        v_o[...] = vn

    blk = pl.BlockSpec((rb, Cg), lambda i: (i, 0))
    return pl.pallas_call(
        body, name=name, grid=(R // rb,),
        in_specs=[blk] * 4, out_specs=[blk] * 4,
        out_shape=[jax.ShapeDtypeStruct((R, Cw), F32)] * 4,
        compiler_params=_cparams("parallel"),
    )(w, g, m, v)


N_DEV = 8
SMALL_ROWS = 64


def _small_allreduce_adamw(g, w, m, v):
    def body(g_ref, w_ref, m_ref, v_ref, all_ref, gs_o, d_o, m_o, v_o, send_sems, recv_sems, local_sem):
        x, y, c, chips = _place()
        me, sibling = (x, y, c), (x, y, 1 - c)

        def rows(px, py, pc):
            return all_ref.at[pl.ds(pl.multiple_of((4 * px + 2 * py + pc) * SMALL_ROWS, SMALL_ROWS), SMALL_ROWS), :]

        def copy(k, block, to, src=None):
            return pltpu.make_async_remote_copy(
                src_ref=rows(*block) if src is None else src, dst_ref=rows(*block),
                send_sem=send_sems.at[k], recv_sem=recv_sems.at[k], device_id=to, device_id_type=MESH)

        mine = pltpu.make_async_copy(g_ref, rows(*me), local_sem)
        mine.start()
        first = [copy(0, me, sibling, src=g_ref)]
        first += [copy(1 + j, me, (*chip, c), src=g_ref) for j, chip in enumerate(chips)]
        for cp in first:
            cp.start()
        passed = [copy(4 + j, (*chip, c), sibling) for j, chip in enumerate(chips)]
        for j, chip in enumerate(chips):
            copy(1 + j, (*chip, c), me).wait_recv()
            passed[j].start()
        copy(0, sibling, me).wait_recv()
        for j, chip in enumerate(chips):
            copy(4 + j, (*chip, 1 - c), me).wait_recv()
        for cp in first + passed:
            cp.wait_send()
        mine.wait()

        tot = all_ref[0:SMALL_ROWS, :]
        for d in range(1, N_DEV):
            tot = tot + all_ref[d * SMALL_ROWS:(d + 1) * SMALL_ROWS, :]
        delta, mn, vn = _adamw_math(w_ref[...], tot, m_ref[...], v_ref[...])
        gs_o[...] = tot
        d_o[...] = delta
        m_o[...] = mn
        v_o[...] = vn

    vm = pl.BlockSpec(memory_space=pltpu.VMEM)
    shp = jax.ShapeDtypeStruct((SMALL_ROWS, LANES), F32)
    res = pl.pallas_call(
        body, name="small_allreduce_adamw", in_specs=[vm] * 4, out_specs=[vm] * 5,
        out_shape=[jax.ShapeDtypeStruct((N_DEV * SMALL_ROWS, LANES), F32), shp, shp, shp, shp],
        scratch_shapes=[pltpu.SemaphoreType.DMA((7,)), pltpu.SemaphoreType.DMA((7,)), pltpu.SemaphoreType.DMA],
    )(g, w, m, v)
    return res[1:]


SMALL_NAMES = ("g_mix", "b_gate", "b_forget", "qn_swa", "kn_swa", "sink_swa", "rel_bias", "qn_fox", "kn_fox",
               "g_mem", "qn_mem", "kn_mem", "g_mlp")
BIG_NAMES = ("w_in", "w_mem_kv", "w_o_swa", "w_o_fox", "w_o_mem", "w_out", "w_mlp_up", "w_mlp_down")
WEIGHT_NAMES = ("g_mix", "w_in", "b_gate", "b_forget", "qn_swa", "kn_swa", "sink_swa", "rel_bias", "qn_fox", "kn_fox",
                "g_mem", "w_mem_kv", "qn_mem", "kn_mem", "w_o_swa", "w_o_fox", "w_o_mem", "w_out", "g_mlp",
                "w_mlp_up", "w_mlp_down")


def _pack_small(parts, extra=None):
    rows = []
    for n in SMALL_NAMES:
        flat = parts[n].reshape(-1).astype(F32)
        flat = jnp.pad(flat, (0, (-flat.size) % LANES))
        rows.append(flat.reshape(-1, LANES))
    if extra is not None:
        rows.append(jnp.pad(extra.reshape(1, 1), ((0, 0), (0, LANES - 1))))
    packed = jnp.concatenate(rows, axis=0)
    return jnp.pad(packed, ((0, SMALL_ROWS - packed.shape[0]), (0, 0)))


def _unpack_small(packed, shapes):
    out, r = {}, 0
    for n in SMALL_NAMES:
        size = math.prod(shapes[n])
        nr = -(-size // LANES)
        out[n] = packed[r:r + nr].reshape(-1)[:size].reshape(shapes[n])
        r += nr
    return out, packed[r, 0]


W_IN_SEGMENTS = ((C_QA, 0, 512), (C_QF, 768, 512), (C_KF, 1280, 512), (C_VF, 1792, 512), (C_QM, 2312, 512),
                 (C_KA, 512, 128), (C_VA, 640, 128), (C_FL, 2304, FOX_HEADS), (C_GL, 2824, GATE_W))
RELAYOUT_ROWS = 256


def _permute_pieces(src_of_dst):
    blocks = []
    for b in range(len(src_of_dst) // LANES):
        runs, lane = [], 0
        while lane < LANES:
            src = src_of_dst[b * LANES + lane]
            if src is None:
                lane += 1
                continue
            plane, col = src
            end = lane + 1
            while (end < LANES and src_of_dst[b * LANES + end] == (plane, col + end - lane)
                   and (col + end - lane) // LANES == col // LANES):
                end += 1
            runs.append((plane, col // LANES, (lane - col) % LANES, lane, end))
            lane = end
        blocks.append(runs)
    return blocks


def _permuted_block(runs, load, rows):
    lane = _lane((rows, LANES))
    acc = jnp.zeros((rows, LANES), F32)
    for plane, blk, shift, lo, hi in runs:
        x = load(plane, blk).astype(F32)
        if shift:
            x = pltpu.roll(x, shift, 1)
        acc = x if (lo, hi) == (0, LANES) else jnp.where((lane >= lo) & (lane < hi), x, acc)
    return acc


def _w_in_to_segments(g_in):
    src_of_dst = [None] * PROJ_W
    for mine, theirs, width in W_IN_SEGMENTS:
        for k in range(width):
            src_of_dst[mine + k] = ((theirs + k) // IN_SHARD, (theirs + k) % IN_SHARD)
    blocks = _permute_pieces(src_of_dst)
    rb = RELAYOUT_ROWS

    def body(src_ref, out_ref):
        for b, runs in enumerate(blocks):
            blk = _permuted_block(runs, lambda p, c: src_ref[p, :, c * LANES:(c + 1) * LANES], rb)
            out_ref[:, b * LANES:(b + 1) * LANES] = blk.astype(out_ref.dtype)

    return pl.pallas_call(
        body, name="w_in_to_segments", grid=(D_MODEL // rb,),
        in_specs=[pl.BlockSpec((N_SHARD, rb, IN_SHARD_PAD), lambda i: (0, i, 0))],
        out_specs=pl.BlockSpec((rb, PROJ_W), lambda i: (i, 0)),
        out_shape=jax.ShapeDtypeStruct((D_MODEL, PROJ_W), g_in.dtype),
        compiler_params=_cparams("parallel"),
    )(g_in)


def _w_in_from_segments(lo, gl):
    mine_of_theirs = {}
    for mine, theirs, width in W_IN_SEGMENTS:
        for k in range(width):
            mine_of_theirs[theirs + k] = mine + k
    src_of_dst = [None] * (N_SHARD * IN_SHARD_PAD)
    for s in range(N_SHARD):
        for l in range(IN_SHARD):
            j = mine_of_theirs[s * IN_SHARD + l]
            src_of_dst[s * IN_SHARD_PAD + l] = (j // LO_W, j % LO_W)
    blocks = _permute_pieces(src_of_dst)
    per_slot = IN_SHARD_PAD // LANES
    rb = RELAYOUT_ROWS

    def body(lo_ref, gl_ref, out_ref):
        planes = (lo_ref, gl_ref)
        for b, runs in enumerate(blocks):
            blk = _permuted_block(runs, lambda p, c: planes[p][:, c * LANES:(c + 1) * LANES], rb)
            c0 = (b % per_slot) * LANES
            out_ref[b // per_slot, :, c0:c0 + LANES] = blk

    half = pl.BlockSpec((rb, LO_W), lambda i: (i, 0))
    return pl.pallas_call(
        body, name="w_in_from_segments", grid=(D_MODEL // rb,),
        in_specs=[half, half],
        out_specs=pl.BlockSpec((N_SHARD, rb, IN_SHARD_PAD), lambda i: (0, i, 0)),
        out_shape=jax.ShapeDtypeStruct((N_SHARD, D_MODEL, IN_SHARD_PAD), F32),
        compiler_params=_cparams("parallel"),
    )(lo, gl)


def _after(first, then):
    return lax.optimization_barrier((first, then))


class _ReduceGroup:
    def __init__(self, tag, first_collective_id, place):
        self.tag, self.first_id, self.place = tag, first_collective_id, place

    def start(self, local, tie):
        self.names = tuple(local)
        mine, tie = _after([local[n] for n in self.names], tie)
        self.mine = mine
        self.staged = _pair_exchange("pair_exchange_" + self.tag, self.first_id, mine)
        return tie

    def send(self, tie):
        staged, tie = _after(self.staged, tie)
        sums = [_pair_sum("pair_sum_" + n, self.place, g, st) for n, g, st in zip(self.names, self.mine, staged)]
        travel, tie = _after([s[0] for s in sums], tie)
        self.own = [s[1] for s in sums]
        self.got = _chip_exchange("chip_exchange_" + self.tag, self.first_id + 1, travel)
        return tie

    def finish(self, tie):
        got, tie = _after(self.got, tie)
        halves = [_final_sum("final_sum_" + n, self.place, o, r) for n, o, r in zip(self.names, self.own, got)]
        halves, tie = _after(halves, tie)
        summed = _pair_gather("pair_gather_" + self.tag, self.first_id + 2, halves)
        self.summed = dict(zip(self.names, summed))
        return tie


class _GradReducer:
    def __init__(self, place, moments_w_in):
        self.early = _ReduceGroup("early", 2, place)
        self.late = _ReduceGroup("late", 5, place)
        self.moments_w_in = moments_w_in

    def first_inputs(self, h, g_in):
        (h, self.moments_w_in), g_in = _after((h, self.moments_w_in), g_in)
        return h, g_in

    @staticmethod
    def _slot_rows(a):
        return a.reshape(N_SHARD, a.shape[0] // N_SHARD, a.shape[1])

    def early_start(self, g, tie):
        return self.early.start({"w_mlp_down": self._slot_rows(g["w_mlp_down"]), "w_mlp_up": g["w_mlp_up"],
                                 "w_out": self._slot_rows(g["w_out"])}, tie)

    def early_send(self, tie):
        return self.early.send(tie)

    def early_finish(self, tie):
        return self.early.finish(tie)

    def late_start(self, g, tie):
        d_in = _w_in_from_segments(g["wc_lo"], g["wc_gl"])
        return self.late.start({"w_in": d_in, "w_mem_kv": self._slot_rows(g["w_mem_kv"]), "w_o_swa": g["w_o_swa"],
                                "w_o_fox": g["w_o_fox"], "w_o_mem": g["w_o_mem"]}, tie)

    def late_send(self, tie):
        return self.late.send(tie)

    def late_finish(self, tie):
        return self.late.finish(tie)

    @property
    def summed(self):
        return {**self.early.summed, **self.late.summed}


def kernel(x, mem, g_mix, w_in, b_gate, b_forget, qn_swa, kn_swa, sink_swa, rel_bias, qn_fox, kn_fox, g_mem, w_mem_kv, qn_mem, kn_mem, w_o_swa, w_o_fox, w_o_mem, w_out, g_mlp, w_mlp_up, w_mlp_down, loss_target, m_g_mix, m_w_in, m_b_gate, m_b_forget, m_qn_swa, m_kn_swa, m_sink_swa, m_rel_bias, m_qn_fox, m_kn_fox, m_g_mem, m_w_mem_kv, m_qn_mem, m_kn_mem, m_w_o_swa, m_w_o_fox, m_w_o_mem, m_w_out, m_g_mlp, m_w_mlp_up, m_w_mlp_down, v_g_mix, v_w_in, v_b_gate, v_b_forget, v_qn_swa, v_kn_swa, v_sink_swa, v_rel_bias, v_qn_fox, v_kn_fox, v_g_mem, v_w_mem_kv, v_qn_mem, v_kn_mem, v_w_o_swa, v_w_o_fox, v_w_o_mem, v_w_out, v_g_mlp, v_w_mlp_up, v_w_mlp_down):
    given = dict(locals())
    W = {n: given[n] for n in WEIGHT_NAMES}
    M = {n: given["m_" + n] for n in WEIGHT_NAMES}
    V = {n: given["v_" + n] for n in WEIGHT_NAMES}
    pad_in = ((0, 0), (0, IN_SHARD_PAD - IN_SHARD))

    shards = [jnp.pad(w_in[0].astype(BF16), pad_in)] + [W[n][0].astype(BF16) for n in BIG_NAMES[1:]]
    slots = [jnp.broadcast_to(s[None], (N_SHARD,) + s.shape) for s in shards]
    (g_in,) = _all_gather_shards_async("all_gather_w_in", 1, slots[:1])
    g_in, late = lax.optimization_barrier((g_in, slots[1:]))
    g_kv, g_oa, g_of, g_om, g_out, g_up, g_down = _all_gather_shards_async("all_gather_weights_async", 8, late)
    small = {n: (W[n] if n == "rel_bias" else W[n].reshape(1, -1)) for n in SMALL_NAMES}

    place = jnp.stack([2 * lax.axis_index("x") + lax.axis_index("y"), lax.axis_index("c")]).astype(jnp.int32)
    reducer = _GradReducer(place, (M["w_in"][0], V["w_in"][0]))
    loss, grad_x, grads = _local_step(
        x[0], mem[0], loss_target[0], small, g_in, g_kv.reshape(D_MODEL, D_MODEL), (g_oa, g_of, g_om),
        g_out.reshape(D_MODEL, D_MODEL), g_up, g_down.reshape(D_FF, D_MODEL), reducer)
    M["w_in"], V["w_in"] = (a[None] for a in reducer.moments_w_in)

    out = {}

    def adamw_of(names, summed):
        for n in names:
            res = _adamw("adamw_" + n, W[n][0], summed[n], M[n][0], V[n][0])
            out[n] = [r.reshape(W[n].shape) for r in res]

    adamw_of(reducer.early.names, reducer.early.summed)
    shapes = {n: W[n].shape for n in SMALL_NAMES}
    packed = _small_allreduce_adamw(_pack_small(grads, loss), _pack_small(W), _pack_small(M), _pack_small(V))
    done_meanwhile = ([out[n] for n in reducer.early.names], packed)
    (early_out, packed), grad_x = reducer.late_finish((done_meanwhile, grad_x))
    for n, res in zip(reducer.early.names, early_out):
        out[n] = res
    adamw_of(reducer.late.names, reducer.late.summed)
    unpacked = [_unpack_small(p, shapes) for p in packed]
    for n in SMALL_NAMES:
        out[n] = [u[0][n] for u in unpacked]
    loss_total = unpacked[0][1]

    return (loss_total, grad_x.reshape(x.shape),
            *[out[n][0] for n in WEIGHT_NAMES], *[out[n][1] for n in WEIGHT_NAMES],
            *[out[n][2] for n in WEIGHT_NAMES], *[out[n][3] for n in WEIGHT_NAMES])
```

```python
import functools
import math

import jax
import jax.numpy as jnp
from jax import lax
from jax.experimental import pallas as pl
from jax.experimental.pallas import tpu as pltpu
from jax.experimental.pallas import tpu_sc as plsc

F32 = jnp.float32
BF16 = jnp.bfloat16

D_MODEL = 1024
N_MEM = 256
SWA_HEADS = 8
SWA_KV_HEADS = 2
SWA_HEAD_DIM = 64
WINDOW = 128
FOX_HEADS = 8
FOX_HEAD_DIM = 64
MEM_HEADS = 4
MEM_HEAD_DIM = 128
D_FF = 4 * D_MODEL
REL_BUCKETS = 32
REL_MAX_DIST = 128
EPS = 1e-6
NEG = -1e30
GATE_W = 3 * D_MODEL
IN_WIDTH = 5896
N_SHARD = 4
IN_SHARD = IN_WIDTH // N_SHARD
IN_SHARD_PAD = 1536

ADAM_LR = 0.001
ADAM_B1 = 0.9
ADAM_B2 = 0.999
ADAM_EPS = 1e-08
ADAM_WD = 0.01
ADAM_STEP = 10

LANES = 128
V7X_VMEM_BYTES = 64 * 1024 * 1024
VMEM_LIMIT = V7X_VMEM_BYTES * 3 // 4

C_QA, C_QF, C_KF, C_VF, C_QM, C_KA, C_VA, C_FL, C_GL = 0, 512, 1024, 1536, 2048, 2560, 2688, 2816, 3072
LO_W = 3072
PROJ_W = 6144

NN = (((1,), (0,)), ((), ()))
NT = (((1,), (1,)), ((), ()))
TN = (((0,), (0,)), ((), ()))


def _dot(a, b, dims=NN):
    return lax.dot_general(a, b, dims, preferred_element_type=F32)


def _cparams(*sem):
    return pltpu.CompilerParams(dimension_semantics=sem, vmem_limit_bytes=VMEM_LIMIT)


def _split3(a):
    hi = a.astype(BF16)
    r1 = a - hi.astype(F32)
    mid = r1.astype(BF16)
    lo = (r1 - mid.astype(F32)).astype(BF16)
    return hi, mid, lo


def _dot3_right(a, g):
    hi, mid, lo = _split3(a)
    return _dot(hi, g) + _dot(mid, g) + _dot(lo, g)


def _dot3_left(g, a):
    hi, mid, lo = _split3(a)
    return _dot(g, hi) + _dot(g, mid) + _dot(g, lo)


def _group_mean_matrix(d):
    r = jnp.arange(LANES)
    return jnp.where((r[:, None] // d) == (r[None, :] // d), 1.0 / d, 0.0).astype(BF16)


def _lane(shape):
    return lax.broadcasted_iota(jnp.int32, shape, len(shape) - 1)


def _matmul(name, a, b, *, dims, grid, a_spec, b_spec, acc_shape, outs, epilogue, extra=()):
    nk = grid[2]
    n_extra = len(extra)

    def body(a_ref, b_ref, *rest):
        extra_refs = rest[:n_extra]
        out_refs = rest[n_extra:n_extra + len(outs)]
        i, j, k = pl.program_id(0), pl.program_id(1), pl.program_id(2)
        part = _dot(a_ref[...].astype(BF16), b_ref[...].astype(BF16), dims)
        if nk == 1:
            epilogue(part, extra_refs, out_refs, (i, j))
            return
        acc_ref = rest[-1]

        @pl.when(k == 0)
        def _():
            acc_ref[...] = part

        @pl.when((k > 0) & (k < nk - 1))
        def _():
            acc_ref[...] += part

        @pl.when(k == nk - 1)
        def _():
            epilogue(acc_ref[...] + part, extra_refs, out_refs, (i, j))

    res = pl.pallas_call(
        body,
        name=name,
        grid=grid,
        in_specs=[a_spec, b_spec] + [s for _, s in extra],
        out_specs=[s for _, s in outs],
        out_shape=[s for s, _ in outs],
        scratch_shapes=[pltpu.VMEM(acc_shape, F32)] if nk > 1 else [],
        compiler_params=_cparams("arbitrary", "arbitrary", "arbitrary"),
    )(a, b, *[x for x, _ in extra])
    return res


def _epi_store(acc, extra_refs, out_refs, ij):
    out_refs[0][...] = acc.astype(out_refs[0].dtype)


def _rms_rows(x, g):
    r = lax.rsqrt(jnp.mean(x * x, axis=-1, keepdims=True) + EPS)
    return x * r, r


def _rmsnorm_bwd_rows(dh, x, g):
    xhat, r = _rms_rows(x, g)
    dxh = dh * g
    dx = r * (dxh - xhat * jnp.mean(dxh * xhat, axis=-1, keepdims=True))
    return dx, jnp.sum(dh * xhat, axis=0, keepdims=True)


def _rmsnorm(name, x, g, tb):
    T, Dm = x.shape

    def body(x_ref, g_ref, o_ref):
        xhat, _ = _rms_rows(x_ref[...], None)
        o_ref[...] = (xhat * g_ref[...]).astype(o_ref.dtype)

    return pl.pallas_call(
        body, name=name, grid=(T // tb,),
        in_specs=[pl.BlockSpec((tb, Dm), lambda i: (i, 0)), pl.BlockSpec((1, Dm), lambda i: (0, 0))],
        out_specs=pl.BlockSpec((tb, Dm), lambda i: (i, 0)),
        out_shape=jax.ShapeDtypeStruct((T, Dm), BF16),
        compiler_params=_cparams("parallel"),
    )(x, g)


def _head_norm(x, gm, gain):
    ms = _dot3_right(x * x, gm)
    r = lax.rsqrt(ms + EPS)
    return x * r * gain, x * r


def _head_norm_bwd(dy, x, gm, gain):
    ms = _dot3_right(x * x, gm)
    r = lax.rsqrt(ms + EPS)
    xhat = x * r
    dxh = dy * gain
    dx = r * (dxh - xhat * _dot3_right(dxh * xhat, gm))
    return dx, jnp.sum(dy * xhat, axis=0, keepdims=True)


def _log_sigmoid(z):
    return jnp.minimum(z, 0.0) - jnp.log(1.0 + jnp.exp(-jnp.abs(z)))


def _prep_fwd(proj, gains, bfor, tril, gm64, gm128, T, tb):
    nb = T // tb

    def body(qa_ref, qf_ref, kf_ref, vf_ref, qm_ref, ka_ref, va_ref, fl_ref, gains_ref, bfor_ref, tril_ref,
             gm64_ref, gm128_ref,
             qa_o, qf_o, kf_o, vf_o, qm_o, kad_o, vad_o, qaug_o, kaug_o, carry):
        i = pl.program_id(0)
        gm64v = gm64_ref[...]
        gm128v = gm128_ref[...]
        lane = _lane((tb, LANES))

        def norm512(src, dst, row, gm, scale=1.0):
            gain = gains_ref[row:row + 1, :]
            for c in range(4):
                sl = slice(c * LANES, (c + 1) * LANES)
                y, _ = _head_norm(src[:, sl], gm, gain)
                dst[:, sl] = (y * scale).astype(dst.dtype)

        norm512(qa_ref, qa_o, 0, gm64v)
        norm512(qf_ref, qf_o, 2, gm64v, FOX_SCALE)
        norm512(kf_ref, kf_o, 3, gm64v)
        norm512(qm_ref, qm_o, 4, gm128v)
        vf_o[...] = vf_ref[...].astype(vf_o.dtype)

        ka_n, _ = _head_norm(ka_ref[...], gm64v, gains_ref[1:2, :])
        ka_r = pltpu.roll(ka_n, 64, 1)
        va = va_ref[...]
        va_r = pltpu.roll(va, 64, 1)
        lo = lane < 64
        kad_o[0] = jnp.where(lo, ka_n, ka_r).astype(kad_o.dtype)
        kad_o[1] = jnp.where(lo, ka_r, ka_n).astype(kad_o.dtype)
        vad_o[0] = jnp.where(lo, va, va_r).astype(vad_o.dtype)
        vad_o[1] = jnp.where(lo, va_r, va).astype(vad_o.dtype)

        @pl.when(i == 0)
        def _():
            carry[...] = jnp.zeros_like(carry)

        logf = jnp.where(lane < FOX_HEADS, _log_sigmoid(fl_ref[...] + bfor_ref[...]), 0.0)
        c = _dot3_left(tril_ref[...], logf) + carry[0:1, :]
        carry[...] = jnp.broadcast_to(c[tb - 1:tb, :], carry.shape)
        for pair in range(FOX_HEADS // 2):
            qaug = jnp.zeros((tb, LANES), F32)
            kaug = jnp.zeros((tb, LANES), F32)
            for sub in range(2):
                col = jnp.sum(jnp.where(lane == 2 * pair + sub, c, 0.0), axis=1, keepdims=True)
                pieces = [p.astype(F32) for p in _split3(col)]
                base = AUG_STRIDE * sub
                for e in range(3):
                    qaug = jnp.where(lane == base + AUG_C + e, pieces[e], qaug)
                    kaug = jnp.where(lane == base + AUG_NEG_C + e, -pieces[e], kaug)
                qaug = jnp.where((lane >= base + AUG_NEG_C) & (lane < base + AUG_NEG_C + 3), 1.0, qaug)
                ones_k = ((lane >= base + AUG_C) & (lane < base + AUG_C + 3)) | (
                    (lane >= base + AUG_STAT) & (lane < base + AUG_STAT + 3))
                kaug = jnp.where(ones_k, 1.0, kaug)
            sl = slice(pair * LANES, (pair + 1) * LANES)
            qaug_o[:, sl] = qaug.astype(BF16)
            kaug_o[:, sl] = kaug.astype(BF16)

    def seg(width, start):
        return pl.BlockSpec((tb, width), lambda i, s=start // width: (i, s))

    const = lambda shape: pl.BlockSpec(shape, lambda i: tuple(0 for _ in shape))
    rows512 = pl.BlockSpec((tb, 512), lambda i: (i, 0))
    outs = pl.pallas_call(
        body, name="prep_fwd", grid=(nb,),
        in_specs=[seg(512, C_QA), seg(512, C_QF), seg(512, C_KF), seg(512, C_VF), seg(512, C_QM),
                  seg(128, C_KA), seg(128, C_VA), seg(128, C_FL),
                  const((8, LANES)), const((1, LANES)), const((tb, tb)), const((LANES, LANES)), const((LANES, LANES))],
        out_specs=[rows512, rows512, rows512, rows512, rows512,
                   pl.BlockSpec((2, tb, LANES), lambda i: (0, i, 0)), pl.BlockSpec((2, tb, LANES), lambda i: (0, i, 0)),
                   rows512, rows512],
        out_shape=[jax.ShapeDtypeStruct((T, 512), BF16)] * 5
        + [jax.ShapeDtypeStruct((2, T, LANES), BF16)] * 2
        + [jax.ShapeDtypeStruct((T, 512), BF16)] * 2,
        scratch_shapes=[pltpu.VMEM((8, LANES), F32)],
        compiler_params=_cparams("arbitrary"),
    )(proj, proj, proj, proj, proj, proj, proj, proj, gains, bfor, tril, gm64, gm128)
    return outs


def _prep_bwd(proj, dqa, dkad, dvad, dqf, dkf, dvf, dqm, dqf_aug, dkf_aug, gains, bfor, triu, gm64, gm128, T, tb):
    nb = T // tb

    def body(qa_ref, qf_ref, kf_ref, qm_ref, ka_ref, fl_ref,
             dqa_ref, dkad_ref, dvad_ref, dqf_ref, dkf_ref, dvf_ref, dqm_ref, dqfa_ref, dkfa_ref,
             gains_ref, bfor_ref, triu_ref, gm64_ref, gm128_ref,
             dlo_o, gacc_o, carry):
        i = pl.program_id(0)
        gm64v = gm64_ref[...]
        gm128v = gm128_ref[...]
        lane = _lane((tb, LANES))

        @pl.when(i == 0)
        def _():
            carry[...] = jnp.zeros_like(carry)
            gacc_o[...] = jnp.zeros_like(gacc_o)

        def norm512_bwd(dsrc, xsrc, col0, row, gm):
            gain = gains_ref[row:row + 1, :]
            gsum = jnp.zeros((1, LANES), F32)
            for c in range(4):
                sl = slice(c * LANES, (c + 1) * LANES)
                dx, dg = _head_norm_bwd(dsrc[:, sl], xsrc[:, sl], gm, gain)
                dlo_o[:, col0 + c * LANES:col0 + (c + 1) * LANES] = dx.astype(dlo_o.dtype)
                gsum = gsum + dg
            gacc_o[row:row + 1, :] += gsum

        norm512_bwd(dqa_ref, qa_ref, C_QA, 0, gm64v)
        norm512_bwd(dqf_ref, qf_ref, C_QF, 2, gm64v)
        norm512_bwd(dkf_ref, kf_ref, C_KF, 3, gm64v)
        norm512_bwd(dqm_ref, qm_ref, C_QM, 4, gm128v)
        dlo_o[:, C_VF:C_VF + 512] = dvf_ref[...].astype(dlo_o.dtype)

        lo = lane < 64

        def fold(ref):
            f0 = ref[0] + pltpu.roll(ref[0], 64, 1)
            f1 = ref[1] + pltpu.roll(ref[1], 64, 1)
            return jnp.where(lo, f0, f1)

        dka, dg = _head_norm_bwd(fold(dkad_ref), ka_ref[...], gm64v, gains_ref[1:2, :])
        gacc_o[1:2, :] += dg
        dlo_o[:, C_KA:C_KA + LANES] = dka.astype(dlo_o.dtype)
        dlo_o[:, C_VA:C_VA + LANES] = fold(dvad_ref).astype(dlo_o.dtype)

        dc = jnp.zeros((tb, LANES), F32)
        for pair in range(FOX_HEADS // 2):
            sl = slice(pair * LANES, (pair + 1) * LANES)
            rows_sum, cols_sum = dqfa_ref[:, sl], dkfa_ref[:, sl]
            for sub in range(2):
                diff = (jnp.where(lane == AUG_STRIDE * sub + AUG_C, rows_sum, 0.0)
                        - jnp.where(lane == AUG_STRIDE * sub + AUG_NEG_C, cols_sum, 0.0))
                dc = jnp.where(lane == 2 * pair + sub, jnp.sum(diff, axis=1, keepdims=True), dc)
        dlogf = _dot3_left(triu_ref[...], dc) + carry[0:1, :]
        carry[...] = jnp.broadcast_to(dlogf[0:1, :], carry.shape)
        z = fl_ref[...] + bfor_ref[...]
        dfl = jnp.where(lane < FOX_HEADS, dlogf / (1.0 + jnp.exp(z)), 0.0)
        gacc_o[5:6, :] += jnp.sum(dfl, axis=0, keepdims=True)
        dlo_o[:, C_FL:C_FL + LANES] = dfl.astype(dlo_o.dtype)
        dlo_o[:, C_FL + LANES:C_FL + 2 * LANES] = jnp.zeros((tb, LANES), dlo_o.dtype)

    rev = lambda i: nb - 1 - i

    def seg(width, start):
        return pl.BlockSpec((tb, width), lambda i, s=start // width: (rev(i), s))

    const = lambda shape: pl.BlockSpec(shape, lambda i: tuple(0 for _ in shape))
    rows512 = pl.BlockSpec((tb, 512), lambda i: (rev(i), 0))
    dup = pl.BlockSpec((2, tb, LANES), lambda i: (0, rev(i), 0))
    return pl.pallas_call(
        body, name="prep_bwd", grid=(nb,),
        in_specs=[seg(512, C_QA), seg(512, C_QF), seg(512, C_KF), seg(512, C_QM), seg(128, C_KA), seg(128, C_FL),
                  rows512, dup, dup, rows512, rows512, rows512, rows512, rows512, rows512,
                  const((8, LANES)), const((1, LANES)), const((tb, tb)), const((LANES, LANES)), const((LANES, LANES))],
        out_specs=[pl.BlockSpec((tb, LO_W), lambda i: (rev(i), 0)), const((8, LANES))],
        out_shape=[jax.ShapeDtypeStruct((T, LO_W), BF16), jax.ShapeDtypeStruct((8, LANES), F32)],
        scratch_shapes=[pltpu.VMEM((8, LANES), F32)],
        compiler_params=_cparams("arbitrary"),
    )(proj, proj, proj, proj, proj, proj, dqa, dkad, dvad, dqf, dkf, dvf, dqm, dqf_aug, dkf_aug,
      gains, bfor, triu, gm64, gm128)


FOX_SCALE = FOX_HEAD_DIM ** -0.5
AUG_STRIDE = 16
AUG_C = 0
AUG_NEG_C = 3
AUG_STAT = 6
FOX_TQ, FOX_TK = 1024, 1024
FOX_BWD_TQ, FOX_BWD_TK = 1024, 1024


def _fox_head_mask(sub, rows):
    lane = _lane((rows, 2 * LANES))
    main = (lane >= 64 * sub) & (lane < 64 * sub + 64)
    aug = (lane >= LANES + AUG_STRIDE * sub) & (lane < LANES + AUG_STRIDE * (sub + 1))
    return main | aug


def _fox_fwd(q, qaug, k, kaug, v, T, tq, tk):
    nq, nk = T // tq, T // tk
    rep = tk // LANES
    last_of = lambda i: (i * tq + tq - 1) // tk

    def body(q_ref, qa_ref, k_ref, ka_ref, v_ref, o_ref, qab_ref, m_s, acc_s):
        p_, i, j = pl.program_id(0), pl.program_id(1), pl.program_id(2)
        last = last_of(i)

        @pl.when(j == 0)
        def _():
            m_s[...] = jnp.full(m_s.shape, NEG, F32)
            acc_s[...] = jnp.zeros_like(acc_s)

        def step(diagonal):
            q2 = jnp.concatenate([q_ref[...], qa_ref[...]], axis=1)
            k2 = jnp.concatenate([k_ref[...], ka_ref[...]], axis=1)
            v2 = jnp.concatenate([v_ref[...], ka_ref[...]], axis=1)
            if diagonal:
                causal = (lax.broadcasted_iota(jnp.int32, (tq, tk), 1) + j * tk
                          <= lax.broadcasted_iota(jnp.int32, (tq, tk), 0) + i * tq)
            scores = [_dot(jnp.where(_fox_head_mask(sub, tq), q2, jnp.zeros_like(q2)), k2, NT) for sub in range(2)]
            for sub in range(2):
                s = scores[sub]
                if diagonal:
                    s = jnp.where(causal, s, NEG)
                m_prev = m_s[sub]
                m_next = jnp.maximum(m_prev, jnp.max(s, axis=1, keepdims=True))
                p = jnp.exp(s - jnp.tile(m_next, (1, rep)))
                alpha = jnp.exp(m_prev - m_next)
                m_s[sub] = m_next
                acc_s[sub] = acc_s[sub] * jnp.tile(alpha, (1, 2)) + _dot(p.astype(BF16), v2)

        @pl.when(j == last)
        def _():
            step(True)

        @pl.when(j < last)
        def _():
            step(False)

        @pl.when(j == nk - 1)
        def _():
            lane = _lane((tq, LANES))
            outs = []
            qab = qa_ref[...].astype(F32)
            for sub in range(2):
                acc = acc_s[sub]
                base = AUG_STRIDE * sub
                l = jnp.sum(jnp.where(lane == base + AUG_C, acc[:, LANES:], 0.0), axis=1, keepdims=True)
                outs.append(acc[:, :LANES] / l)
                lse = jnp.max(m_s[sub], axis=1, keepdims=True) + jnp.log(l)
                pieces = _split3(-lse)
                for e in range(3):
                    qab = jnp.where(lane == base + AUG_STAT + e, pieces[e].astype(F32), qab)
            o_ref[...] = jnp.where(lane < 64, outs[0], outs[1]).astype(o_ref.dtype)
            qab_ref[...] = qab.astype(BF16)

    qspec = pl.BlockSpec((tq, LANES), lambda p, i, j: (i, p))
    kspec = pl.BlockSpec((tk, LANES), lambda p, i, j: (jnp.minimum(j, last_of(i)), p))
    return pl.pallas_call(
        body, name="fox_fwd", grid=(4, nq, nk),
        in_specs=[qspec, qspec, kspec, kspec, kspec],
        out_specs=[qspec, qspec],
        out_shape=[jax.ShapeDtypeStruct((T, 512), BF16), jax.ShapeDtypeStruct((T, 512), BF16)],
        scratch_shapes=[pltpu.VMEM((2, tq, LANES), F32), pltpu.VMEM((2, tq, 2 * LANES), F32)],
        compiler_params=_cparams("parallel", "parallel", "arbitrary"),
    )(q, qaug, k, kaug, v)


def _fox_bwd(q, qaug, k, kaug, v, do, doaug, T, tq, tk):
    nq, nk = T // tq, T // tk
    first_of = lambda j: (j * tk) // tq

    def body(q_ref, qa_ref, k_ref, ka_ref, v_ref, do_ref, doa_ref,
             dq_ref, dqa_ref, dk_ref, dka_ref, dv_ref, dk_s, dv_s):
        p_, j, i = pl.program_id(0), pl.program_id(1), pl.program_id(2)
        masked = i * tq < (j + 1) * tk - 1

        @pl.when((j == 0) & (i == 0))
        def _():
            dq_ref[...] = jnp.zeros_like(dq_ref)
            dqa_ref[...] = jnp.zeros_like(dqa_ref)

        @pl.when(i == 0)
        def _():
            dk_s[...] = jnp.zeros_like(dk_s)
            dv_s[...] = jnp.zeros_like(dv_s)

        def step(diagonal):
            q2 = jnp.concatenate([q_ref[...], qa_ref[...]], axis=1)
            k2 = jnp.concatenate([k_ref[...], ka_ref[...]], axis=1)
            v2 = jnp.concatenate([v_ref[...], ka_ref[...]], axis=1)
            do2 = jnp.concatenate([do_ref[...], doa_ref[...]], axis=1)
            if diagonal:
                causal = (lax.broadcasted_iota(jnp.int32, (tq, tk), 1) + j * tk
                          <= lax.broadcasted_iota(jnp.int32, (tq, tk), 0) + i * tq)
            qh = [jnp.where(_fox_head_mask(sub, tq), q2, jnp.zeros_like(q2)) for sub in range(2)]
            doh = [jnp.where(_fox_head_mask(sub, tq), do2, jnp.zeros_like(do2)) for sub in range(2)]
            scores = [_dot(qh[sub], k2, NT) for sub in range(2)]
            dps = [_dot(doh[sub], v2, NT) for sub in range(2)]
            dqs = []
            for sub in range(2):
                s = scores[sub]
                if diagonal:
                    s = jnp.where(causal, s, NEG)
                p = jnp.exp(s)
                dsb = (p * dps[sub]).astype(BF16)
                dv_s[...] += _dot(p.astype(BF16), doh[sub][:, :LANES], TN)
                dk_s[...] += _dot(dsb, qh[sub], TN)
                dqs.append(_dot(dsb, k2))
            dq2 = jnp.where(_fox_head_mask(0, tq), dqs[0], dqs[1])
            qrows = pl.ds(pl.multiple_of(i * tq, tq), tq)
            dq_ref[qrows, :] += dq2[:, :LANES] * FOX_SCALE
            dqa_ref[qrows, :] += dq2[:, LANES:]

        @pl.when((i >= first_of(j)) & masked)
        def _():
            step(True)

        @pl.when((i >= first_of(j)) & jnp.logical_not(masked))
        def _():
            step(False)

        @pl.when(i == nq - 1)
        def _():
            dk_ref[...] = dk_s[:, :LANES]
            dka_ref[...] = dk_s[:, LANES:]
            dv_ref[...] = dv_s[...]

    qspec = pl.BlockSpec((tq, LANES), lambda p, j, i: (jnp.maximum(i, first_of(j)), p))
    kspec = pl.BlockSpec((tk, LANES), lambda p, j, i: (j, p))
    resident = pl.BlockSpec((T, LANES), lambda p, j, i: (0, p))
    return pl.pallas_call(
        body, name="fox_bwd", grid=(4, nk, nq),
        in_specs=[qspec, qspec, kspec, kspec, kspec, qspec, qspec],
        out_specs=[resident, resident, kspec, kspec, kspec],
        out_shape=[jax.ShapeDtypeStruct((T, 512), F32)] * 5,
        scratch_shapes=[pltpu.VMEM((tk, 2 * LANES), F32), pltpu.VMEM((tk, LANES), F32)],
        compiler_params=_cparams("arbitrary", "arbitrary", "arbitrary"),
    )(q, qaug, k, kaug, v, do, doaug)


SWA_SUB = 4
SWA_TB = SWA_SUB * WINDOW


def _t5_bucket_matrix():
    t = jnp.arange(WINDOW)[:, None] + WINDOW
    s = jnp.arange(2 * WINDOW)[None, :]
    max_exact = REL_BUCKETS // 2
    d = jnp.maximum(t - s, 0)
    df = jnp.maximum(d, 1).astype(F32)
    large = max_exact + (jnp.log(df / max_exact) / math.log(REL_MAX_DIST / max_exact)
                         * (REL_BUCKETS - max_exact)).astype(jnp.int32)
    large = jnp.minimum(large, REL_BUCKETS - 1)
    return jnp.where(d < max_exact, d, large).astype(jnp.int32)


def _swa_bias(rel_bias, bucket):
    def body(rel_ref, bucket_ref, o_ref):
        b = bucket_ref[...]
        for h in range(SWA_HEADS):
            acc = jnp.zeros(b.shape, F32)
            for r in range(REL_BUCKETS):
                acc = jnp.where(b == r, rel_ref[r, h], acc)
            o_ref[h] = acc

    return pl.pallas_call(
        body, name="swa_bias",
        in_specs=[pl.BlockSpec(memory_space=pltpu.SMEM), pl.BlockSpec(memory_space=pltpu.VMEM)],
        out_specs=pl.BlockSpec(memory_space=pltpu.VMEM),
        out_shape=jax.ShapeDtypeStruct((SWA_HEADS, WINDOW, 2 * WINDOW), F32),
    )(rel_bias, bucket)


def _swa_bias_bwd(dbias, bucket):
    def body(db_ref, bucket_ref, o_ref):
        b = bucket_ref[...]
        lane = _lane((1, LANES))
        for r in range(REL_BUCKETS):
            row = jnp.zeros((1, LANES), F32)
            for h in range(SWA_HEADS):
                part = jnp.sum(jnp.where(b == r, db_ref[h], 0.0), axis=0, keepdims=True)
                tot = jnp.sum(part, axis=1, keepdims=True)
                row = jnp.where(lane == h, tot, row)
            o_ref[r:r + 1, :] = row

    return pl.pallas_call(
        body, name="swa_bias_bwd",
        in_specs=[pl.BlockSpec(memory_space=pltpu.VMEM), pl.BlockSpec(memory_space=pltpu.VMEM)],
        out_specs=pl.BlockSpec(memory_space=pltpu.VMEM),
        out_shape=jax.ShapeDtypeStruct((REL_BUCKETS, LANES), F32),
    )(dbias, bucket)


SWA_GROUP = SWA_HEADS // SWA_KV_HEADS


def _swa_valid(r, i):
    t = (lax.broadcasted_iota(jnp.int32, (SWA_GROUP * WINDOW, 2 * WINDOW), 0) & (WINDOW - 1)) + WINDOW
    s = lax.broadcasted_iota(jnp.int32, (SWA_GROUP * WINDOW, 2 * WINDOW), 1)
    dist = t - s
    band = (dist >= 0) & (dist < WINDOW)
    if r == 0:
        band = band & ((s >= WINDOW) | (i > 0))
    return band


def _swa_stack(blk):
    lane = _lane((WINDOW, LANES))
    parts = []
    for g in range(SWA_GROUP):
        b = blk[:, LANES * (g // 2):LANES * (g // 2 + 1)]
        parts.append(jnp.where((lane >= 64) if g % 2 else (lane < 64), b, jnp.zeros_like(b)))
    return jnp.concatenate(parts, axis=0)


def _swa_unstack(st):
    lane = _lane((WINDOW, LANES))
    W = WINDOW
    return jnp.concatenate([jnp.where(lane < 64, st[2 * b * W:(2 * b + 1) * W], st[(2 * b + 1) * W:(2 * b + 2) * W])
                            for b in range(2)], axis=1)


def _swa_sink_column(sink_ref, kvh):
    row = lax.broadcasted_iota(jnp.int32, (SWA_GROUP * WINDOW, 1), 0)
    col = jnp.full((SWA_GROUP * WINDOW, 1), sink_ref[SWA_GROUP * kvh + SWA_GROUP - 1], F32)
    for g in range(SWA_GROUP - 2, -1, -1):
        col = jnp.where(row < (g + 1) * WINDOW, sink_ref[SWA_GROUP * kvh + g], col)
    return col


def _swa_specs(T):
    W = WINDOW
    qspec = pl.BlockSpec((SWA_TB, 2 * LANES), lambda h, i: (i, h))
    own = pl.BlockSpec((None, SWA_TB, LANES), lambda h, i: (h, i, 0))
    prev = pl.BlockSpec((None, W, LANES), lambda h, i: (h, jnp.maximum(SWA_SUB * i - 1, 0), 0))
    stat = pl.BlockSpec((SWA_GROUP, SWA_TB, LANES), lambda h, i: (h, i, 0))
    bias = pl.BlockSpec((None, SWA_GROUP * W, 2 * W), lambda h, i: (h, 0, 0))
    return qspec, own, prev, stat, bias


def _swa_fwd(sinks, q, kad, vad, bias, T):
    nb = T // SWA_TB
    scale = SWA_HEAD_DIM ** -0.5
    W = WINDOW

    def body(sink_ref, q_ref, k_ref, kp_ref, v_ref, vp_ref, bias_ref, o_ref, lse_ref):
        kvh, i = pl.program_id(0), pl.program_id(1)
        sink = _swa_sink_column(sink_ref, kvh)
        for r in range(SWA_SUB):
            rs = slice(r * W, (r + 1) * W)
            ps = slice((r - 1) * W, r * W)
            k_own, v_own = k_ref[rs, :], v_ref[rs, :]
            k_prev = kp_ref[...] if r == 0 else k_ref[ps, :]
            v_prev = vp_ref[...] if r == 0 else v_ref[ps, :]
            qs = _swa_stack(q_ref[rs, :])
            s = jnp.concatenate([_dot(qs, k_prev, NT), _dot(qs, k_own, NT)], axis=1) * scale + bias_ref[...]
            s = jnp.where(_swa_valid(r, i), s, NEG)
            m = jnp.maximum(jnp.max(s, axis=1, keepdims=True), sink)
            p = jnp.exp(s - m)
            denom = jnp.sum(p, axis=1, keepdims=True) + jnp.exp(sink - m)
            pn = (p / denom).astype(BF16)
            o_ref[rs, :] = _swa_unstack(_dot(pn[:, :W], v_prev) + _dot(pn[:, W:], v_own)).astype(o_ref.dtype)
            lse = m + jnp.log(denom)
            for g in range(SWA_GROUP):
                lse_ref[g, rs, :] = jnp.broadcast_to(lse[g * W:(g + 1) * W], (W, LANES))

    qspec, own, prev, stat, bspec = _swa_specs(T)
    return pl.pallas_call(
        body, name="swa_fwd", grid=(SWA_KV_HEADS, nb),
        in_specs=[pl.BlockSpec(memory_space=pltpu.SMEM), qspec, own, prev, own, prev, bspec],
        out_specs=[qspec, stat],
        out_shape=[jax.ShapeDtypeStruct((T, 512), BF16), jax.ShapeDtypeStruct((SWA_HEADS, T, LANES), F32)],
        compiler_params=_cparams("parallel", "parallel"),
    )(sinks, q, kad, kad, vad, vad, bias.reshape(SWA_KV_HEADS, SWA_GROUP * W, 2 * W))


def _swa_bwd(sinks, q, kad, vad, bias, do, lse, delta, T):
    nb = T // SWA_TB
    scale = SWA_HEAD_DIM ** -0.5
    W = WINDOW

    def body(sink_ref, q_ref, k_ref, kp_ref, v_ref, vp_ref, bias_ref, do_ref, lse_ref, dl_ref,
             dq_ref, dkad_ref, dvad_ref, dbias_ref, dsk_ref):
        kvh, i = pl.program_id(0), pl.program_id(1)
        sink = _swa_sink_column(sink_ref, kvh)

        @pl.when((kvh == 0) & (i == 0))
        def _():
            dkad_ref[...] = jnp.zeros_like(dkad_ref)
            dvad_ref[...] = jnp.zeros_like(dvad_ref)

        @pl.when(i == 0)
        def _():
            dbias_ref[...] = jnp.zeros_like(dbias_ref)
            dsk_ref[...] = jnp.zeros_like(dsk_ref)

        for r in range(SWA_SUB):
            rs = slice(r * W, (r + 1) * W)
            ps = slice((r - 1) * W, r * W)
            k_own, v_own = k_ref[rs, :], v_ref[rs, :]
            k_prev = kp_ref[...] if r == 0 else k_ref[ps, :]
            v_prev = vp_ref[...] if r == 0 else v_ref[ps, :]
            qs = _swa_stack(q_ref[rs, :])
            dos = _swa_stack(do_ref[rs, :])
            lse_b = jnp.concatenate([lse_ref[g, rs, :] for g in range(SWA_GROUP)], axis=0)
            dl_b = jnp.concatenate([dl_ref[g, rs, :] for g in range(SWA_GROUP)], axis=0)
            s = jnp.concatenate([_dot(qs, k_prev, NT), _dot(qs, k_own, NT)], axis=1) * scale + bias_ref[...]
            s = jnp.where(_swa_valid(r, i), s, NEG)
            p = jnp.exp(s - jnp.tile(lse_b, (1, 2)))
            dp = jnp.concatenate([_dot(dos, v_prev, NT), _dot(dos, v_own, NT)], axis=1)
            ds = p * (dp - jnp.tile(dl_b, (1, 2)))
            sink_term = jnp.exp(sink - lse_b) * dl_b
            for g in range(SWA_GROUP):
                dbias_ref[g] += ds[g * W:(g + 1) * W]
                dsk_ref[g:g + 1, :] += jnp.sum(sink_term[g * W:(g + 1) * W], axis=0, keepdims=True)
            dsb = ds.astype(BF16)
            pb = p.astype(BF16)
            dq_ref[rs, :] = _swa_unstack((_dot(dsb[:, :W], k_prev) + _dot(dsb[:, W:], k_own)) * scale)
            own_row = pl.multiple_of(i * SWA_TB + r * W, W)
            dkad_ref[kvh, pl.ds(own_row, W), :] += _dot(dsb[:, W:], qs, TN) * scale
            dvad_ref[kvh, pl.ds(own_row, W), :] += _dot(pb[:, W:], dos, TN)
            dk_prev = _dot(dsb[:, :W], qs, TN) * scale
            dv_prev = _dot(pb[:, :W], dos, TN)
            if r == 0:
                @pl.when(i > 0)
                def _():
                    prev_row = pl.multiple_of(i * SWA_TB - W, W)
                    dkad_ref[kvh, pl.ds(prev_row, W), :] += dk_prev
                    dvad_ref[kvh, pl.ds(prev_row, W), :] += dv_prev
            else:
                prev_row = pl.multiple_of(i * SWA_TB + (r - 1) * W, W)
                dkad_ref[kvh, pl.ds(prev_row, W), :] += dk_prev
                dvad_ref[kvh, pl.ds(prev_row, W), :] += dv_prev

    qspec, own, prev, stat, bspec = _swa_specs(T)
    full = pl.BlockSpec((SWA_KV_HEADS, T, LANES), lambda h, i: (0, 0, 0))
    return pl.pallas_call(
        body, name="swa_bwd", grid=(SWA_KV_HEADS, nb),
        in_specs=[pl.BlockSpec(memory_space=pltpu.SMEM), qspec, own, prev, own, prev, bspec, qspec, stat, stat],
        out_specs=[qspec, full, full, pl.BlockSpec((SWA_GROUP, W, 2 * W), lambda h, i: (h, 0, 0)),
                   pl.BlockSpec((None, 8, LANES), lambda h, i: (h, 0, 0))],
        out_shape=[jax.ShapeDtypeStruct((T, 512), F32), jax.ShapeDtypeStruct((SWA_KV_HEADS, T, LANES), F32),
                   jax.ShapeDtypeStruct((SWA_KV_HEADS, T, LANES), F32), jax.ShapeDtypeStruct((SWA_HEADS, W, 2 * W), F32),
                   jax.ShapeDtypeStruct((SWA_KV_HEADS, 8, LANES), F32)],
        compiler_params=_cparams("arbitrary", "arbitrary"),
    )(sinks, q, kad, kad, vad, vad, bias.reshape(SWA_KV_HEADS, SWA_GROUP * W, 2 * W), do, lse, delta)


def _mem_fwd(q, mk, mv, T, tq):
    scale = MEM_HEAD_DIM ** -0.5

    def body(q_ref, k_ref, v_ref, o_ref, lse_ref):
        s = _dot(q_ref[...], k_ref[...], NT) * scale
        m = jnp.max(s, axis=1, keepdims=True)
        p = jnp.exp(s - m)
        l = jnp.sum(p, axis=1, keepdims=True)
        o_ref[...] = _dot((p / l).astype(BF16), v_ref[...]).astype(o_ref.dtype)
        lse_ref[...] = jnp.broadcast_to(m + jnp.log(l), (tq, LANES))

    qspec = pl.BlockSpec((tq, LANES), lambda h, i: (i, h))
    kspec = pl.BlockSpec((N_MEM, LANES), lambda h, i: (0, h))
    return pl.pallas_call(
        body, name="mem_fwd", grid=(MEM_HEADS, T // tq),
        in_specs=[qspec, kspec, kspec],
        out_specs=[qspec, pl.BlockSpec((None, tq, LANES), lambda h, i: (h, i, 0))],
        out_shape=[jax.ShapeDtypeStruct((T, 512), BF16), jax.ShapeDtypeStruct((MEM_HEADS, T, LANES), F32)],
        compiler_params=_cparams("parallel", "parallel"),
    )(q, mk, mv)


def _mem_bwd(q, mk, mv, do, lse, delta, T, tq):
    scale = MEM_HEAD_DIM ** -0.5
    rep = N_MEM // LANES

    def body(q_ref, k_ref, v_ref, do_ref, lse_ref, dl_ref, dq_ref, dk_ref, dv_ref):
        i = pl.program_id(1)

        @pl.when(i == 0)
        def _():
            dk_ref[...] = jnp.zeros_like(dk_ref)
            dv_ref[...] = jnp.zeros_like(dv_ref)

        qv, dov = q_ref[...], do_ref[...]
        s = _dot(qv, k_ref[...], NT) * scale
        p = jnp.exp(s - jnp.tile(lse_ref[...], (1, rep)))
        dp = _dot(dov, v_ref[...], NT)
        ds = p * (dp - jnp.tile(dl_ref[...], (1, rep)))
        dsb = ds.astype(BF16)
        dq_ref[...] = _dot(dsb, k_ref[...]) * scale
        dk_ref[...] += _dot(dsb, qv, TN) * scale
        dv_ref[...] += _dot(p.astype(BF16), dov, TN)

    qspec = pl.BlockSpec((tq, LANES), lambda h, i: (i, h))
    kspec = pl.BlockSpec((N_MEM, LANES), lambda h, i: (0, h))
    stat = pl.BlockSpec((None, tq, LANES), lambda h, i: (h, i, 0))
    return pl.pallas_call(
        body, name="mem_bwd", grid=(MEM_HEADS, T // tq),
        in_specs=[qspec, kspec, kspec, qspec, stat, stat],
        out_specs=[qspec, kspec, kspec],
        out_shape=[jax.ShapeDtypeStruct((T, 512), F32), jax.ShapeDtypeStruct((N_MEM, 512), F32),
                   jax.ShapeDtypeStruct((N_MEM, 512), F32)],
        compiler_params=_cparams("arbitrary", "arbitrary"),
    )(q, mk, mv, do, lse, delta)


def _mem_prep_fwd(mem, g_mem, w_kv, kn_gain, gm128):
    def body(mem_ref, g_ref, w_ref, kn_ref, gm_ref, memn_o, kv_o, mk_o, mv_o):
        xhat, _ = _rms_rows(mem_ref[...], None)
        memn = (xhat * g_ref[...]).astype(BF16)
        memn_o[...] = memn
        kv = _dot(memn, w_ref[...])
        kv_o[...] = kv
        gm = gm_ref[...]
        for c in range(4):
            sl = slice(c * LANES, (c + 1) * LANES)
            y, _ = _head_norm(kv[:, sl], gm, kn_ref[...])
            mk_o[:, sl] = y.astype(BF16)
        mv_o[...] = kv[:, 512:].astype(BF16)

    vm = pl.BlockSpec(memory_space=pltpu.VMEM)
    return pl.pallas_call(
        body, name="mem_prep_fwd", in_specs=[vm] * 5, out_specs=[vm] * 4,
        out_shape=[jax.ShapeDtypeStruct((N_MEM, D_MODEL), BF16), jax.ShapeDtypeStruct((N_MEM, D_MODEL), F32),
                   jax.ShapeDtypeStruct((N_MEM, 512), BF16), jax.ShapeDtypeStruct((N_MEM, 512), BF16)],
        compiler_params=pltpu.CompilerParams(vmem_limit_bytes=VMEM_LIMIT),
    )(mem, g_mem, w_kv, kn_gain, gm128)


def _mem_prep_bwd(mem, g_mem, memn, kv, w_kv, kn_gain, gm128, dmk, dmv):
    def body(mem_ref, g_ref, memn_ref, kv_ref, w_ref, kn_ref, gm_ref, dmk_ref, dmv_ref, dw_o, dg_o, dkn_o, dkv_s):
        gm = gm_ref[...]
        dkn = jnp.zeros((1, LANES), F32)
        for c in range(4):
            sl = slice(c * LANES, (c + 1) * LANES)
            dx, dg = _head_norm_bwd(dmk_ref[:, sl], kv_ref[:, sl], gm, kn_ref[...])
            dkv_s[:, sl] = dx.astype(BF16)
            dkn = dkn + dg
        dkn_o[...] = dkn
        dkv_s[:, 512:] = dmv_ref[...].astype(BF16)
        dkv = dkv_s[...]
        dw_o[...] = _dot(memn_ref[...], dkv, TN)
        dmemn = _dot(dkv, w_ref[...], NT)
        xhat, _ = _rms_rows(mem_ref[...], None)
        dg_o[...] = jnp.sum(dmemn * xhat, axis=0, keepdims=True)

    vm = pl.BlockSpec(memory_space=pltpu.VMEM)
    return pl.pallas_call(
        body, name="mem_prep_bwd", in_specs=[vm] * 9, out_specs=[vm] * 3,
        out_shape=[jax.ShapeDtypeStruct((D_MODEL, D_MODEL), F32), jax.ShapeDtypeStruct((1, D_MODEL), F32),
                   jax.ShapeDtypeStruct((1, LANES), F32)],
        scratch_shapes=[pltpu.VMEM((N_MEM, D_MODEL), BF16)],
        compiler_params=pltpu.CompilerParams(vmem_limit_bytes=VMEM_LIMIT),
    )(mem, g_mem, memn, kv, w_kv, kn_gain, gm128, dmk, dmv)


SLOT_O = D_MODEL // N_SHARD


def _merge_fwd(proj, b_gate, o3, w3, T, tb):
    def body(gl_ref, bg_ref, oa_ref, of_ref, om_ref, wa_ref, wf_ref, wm_ref, out_ref):
        o_refs = (oa_ref, of_ref, om_ref)
        w_refs = (wa_ref, wf_ref, wm_ref)
        for n in range(N_SHARD):
            acc = jnp.zeros((tb, SLOT_O), F32)
            for b in range(3):
                c0 = b * D_MODEL + n * SLOT_O
                g = jax.nn.sigmoid(gl_ref[:, c0:c0 + SLOT_O] + bg_ref[:, c0:c0 + SLOT_O])
                acc = acc + g * _dot(o_refs[b][...], w_refs[b][n])
            out_ref[:, n * SLOT_O:(n + 1) * SLOT_O] = acc.astype(out_ref.dtype)

    rows = pl.BlockSpec((tb, 512), lambda i: (i, 0))
    wspec = pl.BlockSpec((N_SHARD, 512, SLOT_O), lambda i: (0, 0, 0))
    return pl.pallas_call(
        body, name="merge_fwd", grid=(T // tb,),
        in_specs=[pl.BlockSpec((tb, GATE_W), lambda i: (i, 1)), pl.BlockSpec((1, GATE_W), lambda i: (0, 0)),
                  rows, rows, rows, wspec, wspec, wspec],
        out_specs=pl.BlockSpec((tb, D_MODEL), lambda i: (i, 0)),
        out_shape=jax.ShapeDtypeStruct((T, D_MODEL), BF16),
        compiler_params=_cparams("parallel"),
    )(proj, b_gate, *o3, *w3)


def _merge_bwd(proj, b_gate, o3, w3, dmerged, T, tb):
    heads = (SWA_HEADS, FOX_HEADS, MEM_HEADS)

    def body(gl_ref, bg_ref, oa_ref, of_ref, om_ref, wa_ref, wf_ref, wm_ref, dm_ref,
             dgl_o, doa_o, dof_o, dom_o, dla_o, dlf_o, dlm_o, dwa_o, dwf_o, dwm_o, dbg_o):
        i = pl.program_id(0)
        o_refs = (oa_ref, of_ref, om_ref)
        w_refs = (wa_ref, wf_ref, wm_ref)
        do_refs = (doa_o, dof_o, dom_o)
        dl_refs = (dla_o, dlf_o, dlm_o)
        dw_refs = (dwa_o, dwf_o, dwm_o)

        @pl.when(i == 0)
        def _():
            for r in dw_refs:
                r[...] = jnp.zeros_like(r)
            dbg_o[...] = jnp.zeros_like(dbg_o)

        lane = _lane((tb, LANES))
        for b in range(3):
            ob = o_refs[b][...]
            do = jnp.zeros((tb, 512), F32)
            for n in range(N_SHARD):
                c0 = b * D_MODEL + n * SLOT_O
                g = jax.nn.sigmoid(gl_ref[:, c0:c0 + SLOT_O] + bg_ref[:, c0:c0 + SLOT_O])
                dm = dm_ref[:, n * SLOT_O:(n + 1) * SLOT_O]
                y = _dot(ob, w_refs[b][n])
                dgl = dm * y * g * (1.0 - g)
                dgl_o[:, c0:c0 + SLOT_O] = dgl.astype(dgl_o.dtype)
                dbg_o[:, c0:c0 + SLOT_O] += jnp.sum(dgl, axis=0, keepdims=True)
                dy = (dm * g).astype(BF16)
                do = do + _dot(dy, w_refs[b][n], NT)
                dw_refs[b][n] += _dot(ob, dy, TN)
            do_refs[b][...] = do.astype(BF16)
            prod = do * ob.astype(F32)
            for c in range(4):
                blk = prod[:, c * LANES:(c + 1) * LANES]
                if heads[b] == 8:
                    lo = jnp.sum(jnp.where(lane < 64, blk, 0.0), axis=1, keepdims=True)
                    hi = jnp.sum(jnp.where(lane >= 64, blk, 0.0), axis=1, keepdims=True)
                    if b == 1:
                        aug = jnp.zeros((tb, LANES), F32)
                        for sub, dl in enumerate((lo, hi)):
                            for e, piece in enumerate(_split3(-dl)):
                                aug = jnp.where(lane == AUG_STRIDE * sub + AUG_C + e, piece.astype(F32), aug)
                        dl_refs[b][:, c * LANES:(c + 1) * LANES] = aug.astype(BF16)
                    else:
                        dl_refs[b][2 * c] = jnp.broadcast_to(lo, (tb, LANES))
                        dl_refs[b][2 * c + 1] = jnp.broadcast_to(hi, (tb, LANES))
                else:
                    dl_refs[b][c] = jnp.broadcast_to(jnp.sum(blk, axis=1, keepdims=True), (tb, LANES))

    rows = pl.BlockSpec((tb, 512), lambda i: (i, 0))
    wspec = pl.BlockSpec((N_SHARD, 512, SLOT_O), lambda i: (0, 0, 0))
    stat = lambda h: pl.BlockSpec((h, tb, LANES), lambda i: (0, i, 0))
    return pl.pallas_call(
        body, name="merge_bwd", grid=(T // tb,),
        in_specs=[pl.BlockSpec((tb, GATE_W), lambda i: (i, 1)), pl.BlockSpec((1, GATE_W), lambda i: (0, 0)),
                  rows, rows, rows, wspec, wspec, wspec, pl.BlockSpec((tb, D_MODEL), lambda i: (i, 0))],
        out_specs=[pl.BlockSpec((tb, GATE_W), lambda i: (i, 0)), rows, rows, rows,
                   stat(8), rows, stat(4), wspec, wspec, wspec, pl.BlockSpec((1, GATE_W), lambda i: (0, 0))],
        out_shape=[jax.ShapeDtypeStruct((T, GATE_W), BF16)] + [jax.ShapeDtypeStruct((T, 512), BF16)] * 3
        + [jax.ShapeDtypeStruct((8, T, LANES), F32), jax.ShapeDtypeStruct((T, 512), BF16),
           jax.ShapeDtypeStruct((4, T, LANES), F32)]
        + [jax.ShapeDtypeStruct((N_SHARD, 512, SLOT_O), F32)] * 3 + [jax.ShapeDtypeStruct((1, GATE_W), F32)],
        compiler_params=_cparams("arbitrary"),
    )(proj, b_gate, *o3, *w3, dmerged)


def _local_step(x, h, mem, tgt, small, g_in, w_kv, w_o3, w_out, w_up, w_down, reducer):
    T = x.shape[0]
    tm = min(512, T)
    tile2 = lambda v: jnp.tile(v.reshape(1, -1), (1, LANES // v.size))
    gains = jnp.concatenate([tile2(small["qn_swa"]), tile2(small["kn_swa"]), tile2(small["qn_fox"]),
                             tile2(small["kn_fox"]), tile2(small["qn_mem"]), jnp.zeros((3, LANES), F32)], axis=0)
    kn_mem = small["kn_mem"].reshape(1, LANES)
    bfor = jnp.pad(small["b_forget"].reshape(1, -1), ((0, 0), (0, LANES - FOX_HEADS)))
    gm64 = _group_mean_matrix(64)
    gm128 = _group_mean_matrix(128)
    tb_prep = min(256, T)
    ones = jnp.ones((tb_prep, tb_prep), F32)
    tril = jnp.tril(ones).astype(BF16)
    triu = jnp.triu(ones).astype(BF16)
    bucket = _t5_bucket_matrix()
    g_mix, g_mlp, g_mem = small["g_mix"], small["g_mlp"], small["g_mem"]
    b_gate = small["b_gate"]
    sinks = small["sink_swa"].reshape(-1)

    tl = min(1024, T)
    sq = pl.BlockSpec((tl, D_MODEL), lambda i, j, k: (i, j))
    wc = _w_in_to_segments(g_in)
    (proj,) = _matmul(
        "mm_proj", h, wc, dims=NN, grid=(T // tl, PROJ_W // D_MODEL, 1),
        a_spec=pl.BlockSpec((tl, D_MODEL), lambda i, j, k: (i, 0)),
        b_spec=pl.BlockSpec((D_MODEL, D_MODEL), lambda i, j, k: (0, j)),
        acc_shape=(tl, D_MODEL),
        outs=[(jax.ShapeDtypeStruct((T, PROJ_W), F32), sq)],
        epilogue=_epi_store)
    qa, qf, kf, vf, qm, kad, vad, qf_aug, kf_aug = _prep_fwd(proj, gains, bfor, tril, gm64, gm128, T, tb_prep)
    bias = _swa_bias(small["rel_bias"], bucket)
    o_swa, lse_swa = _swa_fwd(sinks, qa, kad, vad, bias, T)
    o_fox, qf_aug_bwd = _fox_fwd(qf, qf_aug, kf, kf_aug, vf, T, min(FOX_TQ, T), min(FOX_TK, T))
    memn, kv, mk, mv = _mem_prep_fwd(mem, g_mem, w_kv, kn_mem, gm128)
    o_mem, lse_mem = _mem_fwd(qm, mk, mv, T, tm)
    o3 = (o_swa, o_fox, o_mem)
    merged = _merge_fwd(proj, b_gate, o3, w_o3, T, min(256, T))

    def epi_residual(acc, extra_refs, out_refs, ij):
        out_refs[0][...] = extra_refs[0][...] + acc

    row_full = pl.BlockSpec((tm, D_MODEL), lambda i, j, k: (i, 0))
    row_big = pl.BlockSpec((tl, D_MODEL), lambda i, j, k: (i, 0))
    whole = pl.BlockSpec((D_MODEL, D_MODEL), lambda i, j, k: (0, 0))
    (x2,) = _matmul(
        "mm_out", merged, w_out, dims=NN, grid=(T // tl, 1, 1),
        a_spec=row_big, b_spec=whole,
        acc_shape=(tl, D_MODEL), extra=[(x, row_big)],
        outs=[(jax.ShapeDtypeStruct((T, D_MODEL), F32), row_big)], epilogue=epi_residual)
    hm = _rmsnorm("rms_mlp", x2, g_mlp, tm)

    def epi_relu2(acc, extra_refs, out_refs, ij):
        out_refs[0][...] = acc
        r = jnp.maximum(acc, 0.0)
        out_refs[1][...] = (r * r).astype(BF16)

    up, u = _matmul(
        "mm_up", hm, w_up, dims=NN, grid=(T // tl, N_SHARD, 1),
        a_spec=row_big, b_spec=pl.BlockSpec((None, D_MODEL, D_MODEL), lambda i, j, k: (j, 0, 0)),
        acc_shape=(tl, D_MODEL),
        outs=[(jax.ShapeDtypeStruct((T, D_FF), F32), sq), (jax.ShapeDtypeStruct((T, D_FF), BF16), sq)],
        epilogue=epi_relu2)

    def epi_loss(acc, extra_refs, out_refs, ij):
        y = extra_refs[0][...] + acc
        err = y - extra_refs[1][...]
        dyv = err * (1.0 / D_MODEL)
        out_refs[0][...] = dyv
        out_refs[2][...] = dyv.astype(BF16)
        sq = jnp.sum(jnp.sum(err * err, axis=1, keepdims=True), axis=0, keepdims=True)

        @pl.when(ij[0] == 0)
        def _():
            out_refs[1][...] = jnp.zeros_like(out_refs[1])

        out_refs[1][...] += jnp.broadcast_to(sq, out_refs[1].shape)

    kblk = pl.BlockSpec((tl, D_MODEL), lambda i, j, k: (i, k))
    dy, loss_acc, dy_bf = _matmul(
        "mm_down", u, w_down, dims=NN, grid=(T // tl, 1, N_SHARD),
        a_spec=kblk, b_spec=pl.BlockSpec((D_MODEL, D_MODEL), lambda i, j, k: (k, 0)),
        acc_shape=(tl, D_MODEL), extra=[(x2, row_big), (tgt, row_big)],
        outs=[(jax.ShapeDtypeStruct((T, D_MODEL), F32), row_big),
              (jax.ShapeDtypeStruct((8, LANES), F32), pl.BlockSpec((8, LANES), lambda i, j, k: (0, 0))),
              (jax.ShapeDtypeStruct((T, D_MODEL), BF16), row_big)],
        epilogue=epi_loss)
    loss = loss_acc[0, 0] * (0.5 / D_MODEL)

    def epi_dup(acc, extra_refs, out_refs, ij):
        out_refs[0][...] = (acc * (2.0 * jnp.maximum(extra_refs[0][...], 0.0))).astype(BF16)

    (dup,) = _matmul(
        "mm_dup", dy_bf, w_down, dims=NT, grid=(T // tl, N_SHARD, 1),
        a_spec=row_big, b_spec=pl.BlockSpec((D_MODEL, D_MODEL), lambda i, j, k: (j, 0)),
        acc_shape=(tl, D_MODEL), extra=[(up, sq)],
        outs=[(jax.ShapeDtypeStruct((T, D_FF), BF16), sq)], epilogue=epi_dup)

    nkt = T // tl
    t_rows = pl.BlockSpec((tl, D_MODEL), lambda i, j, k: (k, i))
    t_cols = pl.BlockSpec((tl, D_MODEL), lambda i, j, k: (k, j))
    (d_w_down,) = _matmul(
        "mm_dw_down", u, dy_bf, dims=TN, grid=(N_SHARD, 1, nkt),
        a_spec=t_rows, b_spec=t_cols, acc_shape=(D_MODEL, D_MODEL),
        outs=[(jax.ShapeDtypeStruct((D_FF, D_MODEL), F32), pl.BlockSpec((D_MODEL, D_MODEL), lambda i, j, k: (i, 0)))],
        epilogue=_epi_store)
    (d_w_up,) = _matmul(
        "mm_dw_up", hm, dup, dims=TN, grid=(1, N_SHARD, nkt),
        a_spec=t_rows, b_spec=t_cols, acc_shape=(D_MODEL, D_MODEL),
        outs=[(jax.ShapeDtypeStruct((N_SHARD, D_MODEL, D_MODEL), F32),
               pl.BlockSpec((None, D_MODEL, D_MODEL), lambda i, j, k: (j, 0, 0)))],
        epilogue=_epi_store)

    def epi_rms_bwd(acc, extra_refs, out_refs, ij):
        dx, dg = _rmsnorm_bwd_rows(acc, extra_refs[0][...], extra_refs[1][...])
        out_refs[0][...] = dx + extra_refs[2][...]

        @pl.when(ij[0] == 0)
        def _():
            out_refs[1][...] = jnp.zeros_like(out_refs[1])

        out_refs[1][...] += dg

    gain_spec = pl.BlockSpec((1, D_MODEL), lambda i, j, k: (0, 0))
    dx2, d_g_mlp = _matmul(
        "mm_dhm", dup, w_up, dims=NT, grid=(T // tl, 1, N_SHARD),
        a_spec=kblk, b_spec=pl.BlockSpec((None, D_MODEL, D_MODEL), lambda i, j, k: (k, 0, 0)),
        acc_shape=(tl, D_MODEL), extra=[(x2, row_big), (g_mlp, gain_spec), (dy, row_big)],
        outs=[(jax.ShapeDtypeStruct((T, D_MODEL), F32), row_big), (jax.ShapeDtypeStruct((1, D_MODEL), F32), gain_spec)],
        epilogue=epi_rms_bwd)

    (dmerged,) = _matmul(
        "mm_dmerged", dx2, w_out, dims=NT, grid=(T // tl, 1, 1),
        a_spec=row_big, b_spec=whole,
        acc_shape=(tl, D_MODEL), outs=[(jax.ShapeDtypeStruct((T, D_MODEL), F32), row_big)], epilogue=_epi_store)
    (d_w_out,) = _matmul(
        "mm_dw_out", merged, dx2, dims=TN, grid=(1, 1, nkt),
        a_spec=t_rows, b_spec=t_cols, acc_shape=(D_MODEL, D_MODEL),
        outs=[(jax.ShapeDtypeStruct((D_MODEL, D_MODEL), F32), whole)],
        epilogue=_epi_store)
    (dgl, do_swa, do_fox, do_mem, dl_swa, do_fox_aug, dl_mem, d_wo_swa, d_wo_fox, d_wo_mem, d_b_gate) = _merge_bwd(
        proj, b_gate, o3, w_o3, dmerged, T, min(256, T))

    dqm, dmk, dmv = _mem_bwd(qm, mk, mv, do_mem, lse_mem, dl_mem, T, tm)
    d_w_kv, d_g_mem, d_kn_mem = _mem_prep_bwd(mem, g_mem, memn, kv, w_kv, kn_mem, gm128, dmk, dmv)
    do_swa = reducer.early_start({"w_mlp_down": d_w_down, "w_mlp_up": d_w_up, "w_out": d_w_out, "w_mem_kv": d_w_kv,
                                  "w_o_swa": d_wo_swa, "w_o_fox": d_wo_fox, "w_o_mem": d_wo_mem}, do_swa)
    dqa, dkad, dvad, dbias, dsk = _swa_bwd(sinks, qa, kad, vad, bias, do_swa, lse_swa, dl_swa, T)
    dqa, do_fox = reducer.early_send((dqa, do_fox))
    dqf, dqf_aug, dkf, dkf_aug, dvf = _fox_bwd(qf, qf_aug_bwd, kf, kf_aug, vf, do_fox, do_fox_aug, T,
                                               min(FOX_BWD_TQ, T), min(FOX_BWD_TK, T))
    dvf = reducer.early_finish(dvf)
    d_rel = _swa_bias_bwd(dbias, bucket)
    dlo, gacc = _prep_bwd(proj, dqa, dkad, dvad, dqf, dkf, dvf, dqm, dqf_aug, dkf_aug, gains, bfor, triu, gm64, gm128,
                          T, tb_prep)

    def dwc_half(name, dpart):
        (res,) = _matmul(
            name, h, dpart, dims=TN, grid=(1, LO_W // D_MODEL, nkt),
            a_spec=t_rows, b_spec=t_cols, acc_shape=(D_MODEL, D_MODEL),
            outs=[(jax.ShapeDtypeStruct((D_MODEL, LO_W), F32), pl.BlockSpec((D_MODEL, D_MODEL), lambda i, j, k: (0, j)))],
            epilogue=_epi_store)
        return res

    d_wc_lo = dwc_half("mm_dwc_lo", dlo)
    d_wc_gl = dwc_half("mm_dwc_gl", dgl)
    dlo = reducer.late_start({"wc_lo": d_wc_lo, "wc_gl": d_wc_gl}, dlo)
    (dh_lo,) = _matmul(
        "mm_dh_lo", dlo, wc, dims=NT, grid=(T // tl, 1, LO_W // D_MODEL),
        a_spec=kblk, b_spec=pl.BlockSpec((D_MODEL, D_MODEL), lambda i, j, k: (0, k)),
        acc_shape=(tl, D_MODEL), outs=[(jax.ShapeDtypeStruct((T, D_MODEL), F32), row_big)], epilogue=_epi_store)
    dh_lo = reducer.late_send(dh_lo)

    def epi_dx(acc, extra_refs, out_refs, ij):
        dhh = acc + extra_refs[3][...]
        dx, dg = _rmsnorm_bwd_rows(dhh, extra_refs[0][...], extra_refs[1][...])
        out_refs[0][...] = dx + extra_refs[2][...]

        @pl.when(ij[0] == 0)
        def _():
            out_refs[1][...] = jnp.zeros_like(out_refs[1])

        out_refs[1][...] += dg

    grad_x, d_g_mix = _matmul(
        "mm_dh_gl", dgl, wc, dims=NT, grid=(T // tm, 1, GATE_W // D_MODEL),
        a_spec=pl.BlockSpec((tm, D_MODEL), lambda i, j, k: (i, k)),
        b_spec=pl.BlockSpec((D_MODEL, D_MODEL), lambda i, j, k: (0, k + LO_W // D_MODEL)),
        acc_shape=(tm, D_MODEL), extra=[(x, row_full), (g_mix, gain_spec), (dx2, row_full), (dh_lo, row_full)],
        outs=[(jax.ShapeDtypeStruct((T, D_MODEL), F32), row_full), (jax.ShapeDtypeStruct((1, D_MODEL), F32), gain_spec)],
        epilogue=epi_dx)

    fold64 = lambda row: (row[:64] + row[64:]).reshape(1, 64)
    grads = {
        "g_mix": d_g_mix, "b_gate": d_b_gate, "b_forget": gacc[5, :FOX_HEADS].reshape(1, FOX_HEADS),
        "qn_swa": fold64(gacc[0]), "kn_swa": fold64(gacc[1]),
        "sink_swa": -dsk[:, :SWA_GROUP, 0].reshape(1, SWA_HEADS), "rel_bias": d_rel[:, :SWA_HEADS],
        "qn_fox": fold64(gacc[2]), "kn_fox": fold64(gacc[3]),
        "g_mem": d_g_mem, "qn_mem": gacc[4].reshape(1, LANES), "kn_mem": d_kn_mem, "g_mlp": d_g_mlp,
    }
    return loss, grad_x, grads


MESH = pl.DeviceIdType.MESH
ANY = pl.BlockSpec(memory_space=pl.ANY)


def _place():
    x, y, c = lax.axis_index("x"), lax.axis_index("y"), lax.axis_index("c")
    chips = [(1 - x, y), (x, 1 - y), (1 - x, 1 - y)]
    return x, y, c, chips


def _handshake(peers):
    barrier = pltpu.get_barrier_semaphore()
    for peer in peers:
        pl.semaphore_signal(barrier, inc=1, device_id=peer, device_id_type=MESH)
    pl.semaphore_wait(barrier, len(peers))


def _all_gather_shards_async(name, collective_id, slots):
    n = len(slots)
    bufs = [jax.new_ref(s, memory_space=pltpu.MemorySpace.HBM) for s in slots]

    def body(ici_send, ici_recv, d2d_send, d2d_recv):
        x, y, c, chips = _place()
        sibling = (x, y, 1 - c)
        me = 2 * x + y
        _handshake([(px, py, c) for px, py in chips] + [sibling])

        def half(a, who):
            hr = slots[a].shape[1] // 2
            return pl.ds(pl.multiple_of(who * hr, hr), hr)

        def ici(a, j, slot, to):
            return pltpu.make_async_remote_copy(
                src_ref=bufs[a].at[me, half(a, c)], dst_ref=bufs[a].at[slot, half(a, c)],
                send_sem=ici_send.at[3 * a + j], recv_sem=ici_recv.at[3 * a + j], device_id=to, device_id_type=MESH)

        def d2d(a, j, slot, which):
            part = bufs[a].at[slot, half(a, which)]
            return pltpu.make_async_remote_copy(
                src_ref=part, dst_ref=part, send_sem=d2d_send.at[3 * a + j], recv_sem=d2d_recv.at[3 * a + j],
                device_id=sibling, device_id_type=MESH)

        sends = [ici(a, j, me, (*chip, c)) for a in range(n) for j, chip in enumerate(chips)]
        for cp in sends:
            cp.start()
        passed = []
        for a in range(n):
            for j, (px, py) in enumerate(chips):
                ici(a, j, 2 * px + py, (px, py, c)).wait_recv()
                cp = d2d(a, j, 2 * px + py, c)
                cp.start()
                passed.append(cp)
        for a in range(n):
            for j, (px, py) in enumerate(chips):
                d2d(a, j, 2 * px + py, 1 - c).wait_recv()
        for cp in sends + passed:
            cp.wait_send()

    pl.kernel(
        body, mesh=plsc.ScalarSubcoreMesh(axis_name="seq", num_cores=1), name=name,
        scratch_types=[pltpu.SemaphoreType.DMA((3 * n,))] * 4,
        compiler_params=pltpu.CompilerParams(collective_id=collective_id),
    )()
    return [b[...] for b in bufs]


def _sequencer_call(name, collective_id, n_sems, body):
    pl.kernel(
        body, mesh=plsc.ScalarSubcoreMesh(axis_name="seq", num_cores=1), name=name,
        scratch_types=[pltpu.SemaphoreType.DMA((n_sems,))] * 2,
        compiler_params=pltpu.CompilerParams(collective_id=collective_id),
    )()


def _hbm_ref(value):
    return jax.new_ref(value, memory_space=pltpu.MemorySpace.HBM)


def _pair_exchange(name, collective_id, gs):
    n = len(gs)
    src = [_hbm_ref(g) for g in gs]
    stage = [jax.empty_ref(jax.ShapeDtypeStruct((N_SHARD, g.shape[1] // 2, g.shape[2]), g.dtype),
                           memory_space=pltpu.MemorySpace.HBM) for g in gs]

    def body(send_sem, recv_sem):
        x, y, c, _ = _place()
        sibling = (x, y, 1 - c)
        _handshake([sibling])
        copies = []
        for a in range(n):
            hr = gs[a].shape[1] // 2
            theirs = pl.ds(pl.multiple_of((1 - c) * hr, hr), hr)
            copies.append(pltpu.make_async_remote_copy(
                src_ref=src[a].at[:, theirs, :], dst_ref=stage[a], send_sem=send_sem.at[a], recv_sem=recv_sem.at[a],
                device_id=sibling, device_id_type=MESH))
        for cp in copies:
            cp.start()
        for cp in copies:
            cp.wait()

    _sequencer_call(name, collective_id, n, body)
    return [s[...] for s in stage]


def _chip_exchange(name, collective_id, sums):
    n = len(sums)
    src = [_hbm_ref(s) for s in sums]
    got = [jax.empty_ref(jax.ShapeDtypeStruct((3,) + s.shape[1:], s.dtype), memory_space=pltpu.MemorySpace.HBM)
           for s in sums]

    def body(send_sem, recv_sem):
        x, y, c, chips = _place()
        _handshake([(px, py, c) for px, py in chips])
        copies = []
        for a in range(n):
            for j, (px, py) in enumerate(chips):
                copies.append(pltpu.make_async_remote_copy(
                    src_ref=src[a].at[2 * px + py], dst_ref=got[a].at[j],
                    send_sem=send_sem.at[3 * a + j], recv_sem=recv_sem.at[3 * a + j],
                    device_id=(px, py, c), device_id_type=MESH))
        for cp in copies:
            cp.start()
        for cp in copies:
            cp.wait()

    _sequencer_call(name, collective_id, 3 * n, body)
    return [g[...] for g in got]


def _pair_gather(name, collective_id, fulls):
    n = len(fulls)
    full = [_hbm_ref(f) for f in fulls]

    def body(send_sem, recv_sem):
        x, y, c, _ = _place()
        sibling = (x, y, 1 - c)
        _handshake([sibling])
        copies = []
        for a in range(n):
            hr = fulls[a].shape[0] // 2
            mine = full[a].at[pl.ds(pl.multiple_of(c * hr, hr), hr)]
            copies.append(pltpu.make_async_remote_copy(
                src_ref=mine, dst_ref=mine, send_sem=send_sem.at[a], recv_sem=recv_sem.at[a],
                device_id=sibling, device_id_type=MESH))
        for cp in copies:
            cp.start()
        for cp in copies:
            cp.wait()

    _sequencer_call(name, collective_id, n, body)
    return [f[...] for f in full]


ELEMENTWISE_BLOCK_ELEMS = 256 * 1024


def _row_block(rows, cols):
    rb = 8
    while rb * 2 * cols <= ELEMENTWISE_BLOCK_ELEMS and rb * 2 <= rows:
        rb *= 2
    return rb


def _pair_sum(name, place, g, stage):
    _, R, C = g.shape
    hr = R // 2
    rb = _row_block(hr, C)
    nb = hr // rb

    def body(place_ref, g_ref, st_ref, sum_bf, own_f32):
        s = pl.program_id(1)
        tot = g_ref[...] + st_ref[...]
        sum_bf[...] = tot.astype(BF16)

        @pl.when(s == place_ref[0])
        def _():
            own_f32[...] = tot

    return pl.pallas_call(
        body, name=name,
        grid_spec=pltpu.PrefetchScalarGridSpec(
            num_scalar_prefetch=1, grid=(nb, N_SHARD),
            in_specs=[pl.BlockSpec((None, rb, C), lambda i, s, pr: (s, pr[1] * nb + i, 0)),
                      pl.BlockSpec((None, rb, C), lambda i, s, pr: (s, i, 0))],
            out_specs=[pl.BlockSpec((None, rb, C), lambda i, s, pr: (s, i, 0)),
                       pl.BlockSpec((rb, C), lambda i, s, pr: (i, 0))]),
        out_shape=[jax.ShapeDtypeStruct((N_SHARD, hr, C), BF16), jax.ShapeDtypeStruct((hr, C), F32)],
        compiler_params=_cparams("arbitrary", "arbitrary"),
    )(place, g, stage)


def _final_sum(name, place, own, got):
    hr, C = own.shape
    rb = _row_block(hr, C)
    nb = hr // rb

    def body(place_ref, own_ref, got_ref, o_ref):
        o_ref[...] = ((own_ref[...] + got_ref[0].astype(F32)) + got_ref[1].astype(F32)) + got_ref[2].astype(F32)

    return pl.pallas_call(
        body, name=name,
        grid_spec=pltpu.PrefetchScalarGridSpec(
            num_scalar_prefetch=1, grid=(nb,),
            in_specs=[pl.BlockSpec((rb, C), lambda i, pr: (i, 0)), pl.BlockSpec((3, rb, C), lambda i, pr: (0, i, 0))],
            out_specs=pl.BlockSpec((rb, C), lambda i, pr: (pr[1] * nb + i, 0))),
        out_shape=jax.ShapeDtypeStruct((2 * hr, C), F32),
        compiler_params=_cparams("arbitrary"),
    )(place, own, got)


def _adamw_math(w, g, m, v):
    m = ADAM_B1 * m + (1.0 - ADAM_B1) * g
    v = ADAM_B2 * v + (1.0 - ADAM_B2) * (g * g)
    m_hat = m / (1.0 - ADAM_B1 ** ADAM_STEP)
    v_hat = v / (1.0 - ADAM_B2 ** ADAM_STEP)
    delta = -ADAM_LR * (m_hat / (jnp.sqrt(v_hat) + ADAM_EPS) + ADAM_WD * w)
    return delta, m, v


def _adamw(name, w, g, m, v):
    R, Cw = w.shape
    Cg = g.shape[1]
    rb = _row_block(R, Cg)

    def body(w_ref, g_ref, m_ref, v_ref, g_o, d_o, m_o, v_o):
        gv = g_ref[...]
        delta, mn, vn = _adamw_math(w_ref[...], gv, m_ref[...], v_ref[...])
        g_o[...] = gv
        d_o[...] = delta
        m_o[...] = mn
        v_o[...] = vn

    blk = pl.BlockSpec((rb, Cg), lambda i: (i, 0))
    return pl.pallas_call(
        body, name=name, grid=(R // rb,),
        in_specs=[blk] * 4, out_specs=[blk] * 4,
        out_shape=[jax.ShapeDtypeStruct((R, Cw), F32)] * 4,
        compiler_params=_cparams("parallel"),
    )(w, g, m, v)


N_DEV = 8
SMALL_ROWS = 64


def _small_allreduce_adamw(g, w, m, v):
    def body(g_ref, w_ref, m_ref, v_ref, all_ref, gs_o, d_o, m_o, v_o, send_sems, recv_sems, local_sem):
        x, y, c, chips = _place()
        me, sibling = (x, y, c), (x, y, 1 - c)

        def rows(px, py, pc):
            return all_ref.at[pl.ds(pl.multiple_of((4 * px + 2 * py + pc) * SMALL_ROWS, SMALL_ROWS), SMALL_ROWS), :]

        def copy(k, block, to, src=None):
            return pltpu.make_async_remote_copy(
                src_ref=rows(*block) if src is None else src, dst_ref=rows(*block),
                send_sem=send_sems.at[k], recv_sem=recv_sems.at[k], device_id=to, device_id_type=MESH)

        mine = pltpu.make_async_copy(g_ref, rows(*me), local_sem)
        mine.start()
        first = [copy(0, me, sibling, src=g_ref)]
        first += [copy(1 + j, me, (*chip, c), src=g_ref) for j, chip in enumerate(chips)]
        for cp in first:
            cp.start()
        passed = [copy(4 + j, (*chip, c), sibling) for j, chip in enumerate(chips)]
        for j, chip in enumerate(chips):
            copy(1 + j, (*chip, c), me).wait_recv()
            passed[j].start()
        copy(0, sibling, me).wait_recv()
        for j, chip in enumerate(chips):
            copy(4 + j, (*chip, 1 - c), me).wait_recv()
        for cp in first + passed:
            cp.wait_send()
        mine.wait()

        tot = all_ref[0:SMALL_ROWS, :]
        for d in range(1, N_DEV):
            tot = tot + all_ref[d * SMALL_ROWS:(d + 1) * SMALL_ROWS, :]
        delta, mn, vn = _adamw_math(w_ref[...], tot, m_ref[...], v_ref[...])
        gs_o[...] = tot
        d_o[...] = delta
        m_o[...] = mn
        v_o[...] = vn

    vm = pl.BlockSpec(memory_space=pltpu.VMEM)
    shp = jax.ShapeDtypeStruct((SMALL_ROWS, LANES), F32)
    res = pl.pallas_call(
        body, name="small_allreduce_adamw", in_specs=[vm] * 4, out_specs=[vm] * 5,
        out_shape=[jax.ShapeDtypeStruct((N_DEV * SMALL_ROWS, LANES), F32), shp, shp, shp, shp],
        scratch_shapes=[pltpu.SemaphoreType.DMA((7,)), pltpu.SemaphoreType.DMA((7,)), pltpu.SemaphoreType.DMA],
    )(g, w, m, v)
    return res[1:]


SMALL_NAMES = ("g_mix", "b_gate", "b_forget", "qn_swa", "kn_swa", "sink_swa", "rel_bias", "qn_fox", "kn_fox",
               "g_mem", "qn_mem", "kn_mem", "g_mlp")
BIG_NAMES = ("w_in", "w_mem_kv", "w_o_swa", "w_o_fox", "w_o_mem", "w_out", "w_mlp_up", "w_mlp_down")
WEIGHT_NAMES = ("g_mix", "w_in", "b_gate", "b_forget", "qn_swa", "kn_swa", "sink_swa", "rel_bias", "qn_fox", "kn_fox",
                "g_mem", "w_mem_kv", "qn_mem", "kn_mem", "w_o_swa", "w_o_fox", "w_o_mem", "w_out", "g_mlp",
                "w_mlp_up", "w_mlp_down")


def _pack_small(parts, extra=None):
    rows = []
    for n in SMALL_NAMES:
        flat = parts[n].reshape(-1).astype(F32)
        flat = jnp.pad(flat, (0, (-flat.size) % LANES))
        rows.append(flat.reshape(-1, LANES))
    if extra is not None:
        rows.append(jnp.pad(extra.reshape(1, 1), ((0, 0), (0, LANES - 1))))
    packed = jnp.concatenate(rows, axis=0)
    return jnp.pad(packed, ((0, SMALL_ROWS - packed.shape[0]), (0, 0)))


def _unpack_small(packed, shapes):
    out, r = {}, 0
    for n in SMALL_NAMES:
        size = math.prod(shapes[n])
        nr = -(-size // LANES)
        out[n] = packed[r:r + nr].reshape(-1)[:size].reshape(shapes[n])
        r += nr
    return out, packed[r, 0]


W_IN_SEGMENTS = ((C_QA, 0, 512), (C_QF, 768, 512), (C_KF, 1280, 512), (C_VF, 1792, 512), (C_QM, 2312, 512),
                 (C_KA, 512, 128), (C_VA, 640, 128), (C_FL, 2304, FOX_HEADS), (C_GL, 2824, GATE_W))
RELAYOUT_ROWS = 256


def _permute_pieces(src_of_dst):
    blocks = []
    for b in range(len(src_of_dst) // LANES):
        runs, lane = [], 0
        while lane < LANES:
            src = src_of_dst[b * LANES + lane]
            if src is None:
                lane += 1
                continue
            plane, col = src
            end = lane + 1
            while (end < LANES and src_of_dst[b * LANES + end] == (plane, col + end - lane)
                   and (col + end - lane) // LANES == col // LANES):
                end += 1
            runs.append((plane, col // LANES, (lane - col) % LANES, lane, end))
            lane = end
        blocks.append(runs)
    return blocks


def _permuted_block(runs, load, rows):
    lane = _lane((rows, LANES))
    acc = jnp.zeros((rows, LANES), F32)
    for plane, blk, shift, lo, hi in runs:
        x = load(plane, blk).astype(F32)
        if shift:
            x = pltpu.roll(x, shift, 1)
        acc = x if (lo, hi) == (0, LANES) else jnp.where((lane >= lo) & (lane < hi), x, acc)
    return acc


def _w_in_to_segments(g_in):
    src_of_dst = [None] * PROJ_W
    for mine, theirs, width in W_IN_SEGMENTS:
        for k in range(width):
            src_of_dst[mine + k] = ((theirs + k) // IN_SHARD, (theirs + k) % IN_SHARD)
    blocks = _permute_pieces(src_of_dst)
    rb = RELAYOUT_ROWS

    def body(src_ref, out_ref):
        for b, runs in enumerate(blocks):
            blk = _permuted_block(runs, lambda p, c: src_ref[p, :, c * LANES:(c + 1) * LANES], rb)
            out_ref[:, b * LANES:(b + 1) * LANES] = blk.astype(out_ref.dtype)

    return pl.pallas_call(
        body, name="w_in_to_segments", grid=(D_MODEL // rb,),
        in_specs=[pl.BlockSpec((N_SHARD, rb, IN_SHARD_PAD), lambda i: (0, i, 0))],
        out_specs=pl.BlockSpec((rb, PROJ_W), lambda i: (i, 0)),
        out_shape=jax.ShapeDtypeStruct((D_MODEL, PROJ_W), g_in.dtype),
        compiler_params=_cparams("parallel"),
    )(g_in)


def _w_in_from_segments(lo, gl):
    mine_of_theirs = {}
    for mine, theirs, width in W_IN_SEGMENTS:
        for k in range(width):
            mine_of_theirs[theirs + k] = mine + k
    src_of_dst = [None] * (N_SHARD * IN_SHARD_PAD)
    for s in range(N_SHARD):
        for l in range(IN_SHARD):
            j = mine_of_theirs[s * IN_SHARD + l]
            src_of_dst[s * IN_SHARD_PAD + l] = (j // LO_W, j % LO_W)
    blocks = _permute_pieces(src_of_dst)
    per_slot = IN_SHARD_PAD // LANES
    rb = RELAYOUT_ROWS

    def body(lo_ref, gl_ref, out_ref):
        planes = (lo_ref, gl_ref)
        for b, runs in enumerate(blocks):
            blk = _permuted_block(runs, lambda p, c: planes[p][:, c * LANES:(c + 1) * LANES], rb)
            c0 = (b % per_slot) * LANES
            out_ref[b // per_slot, :, c0:c0 + LANES] = blk

    half = pl.BlockSpec((rb, LO_W), lambda i: (i, 0))
    return pl.pallas_call(
        body, name="w_in_from_segments", grid=(D_MODEL // rb,),
        in_specs=[half, half],
        out_specs=pl.BlockSpec((N_SHARD, rb, IN_SHARD_PAD), lambda i: (0, i, 0)),
        out_shape=jax.ShapeDtypeStruct((N_SHARD, D_MODEL, IN_SHARD_PAD), F32),
        compiler_params=_cparams("parallel"),
    )(lo, gl)


def _after(first, then):
    return lax.optimization_barrier((first, then))


class _ReduceGroup:
    def __init__(self, tag, first_collective_id, place):
        self.tag, self.first_id, self.place = tag, first_collective_id, place

    def start(self, local, tie):
        self.names = tuple(local)
        mine, tie = _after([local[n] for n in self.names], tie)
        self.mine = mine
        self.staged = _pair_exchange("pair_exchange_" + self.tag, self.first_id, mine)
        return tie

    def send(self, tie):
        staged, tie = _after(self.staged, tie)
        sums = [_pair_sum("pair_sum_" + n, self.place, g, st) for n, g, st in zip(self.names, self.mine, staged)]
        travel, tie = _after([s[0] for s in sums], tie)
        self.own = [s[1] for s in sums]
        self.got = _chip_exchange("chip_exchange_" + self.tag, self.first_id + 1, travel)
        return tie

    def finish(self, tie):
        got, tie = _after(self.got, tie)
        halves = [_final_sum("final_sum_" + n, self.place, o, r) for n, o, r in zip(self.names, self.own, got)]
        halves, tie = _after(halves, tie)
        summed = _pair_gather("pair_gather_" + self.tag, self.first_id + 2, halves)
        self.summed = dict(zip(self.names, summed))
        return tie


class _GradReducer:
    def __init__(self, place):
        self.early = _ReduceGroup("early", 2, place)
        self.late = _ReduceGroup("late", 5, place)

    @staticmethod
    def _slot_rows(a):
        return a.reshape(N_SHARD, a.shape[0] // N_SHARD, a.shape[1])

    def early_start(self, g, tie):
        return self.early.start({"w_mlp_down": self._slot_rows(g["w_mlp_down"]), "w_mlp_up": g["w_mlp_up"],
                                 "w_out": self._slot_rows(g["w_out"]), "w_mem_kv": self._slot_rows(g["w_mem_kv"]),
                                 "w_o_swa": g["w_o_swa"], "w_o_fox": g["w_o_fox"], "w_o_mem": g["w_o_mem"]}, tie)

    def early_send(self, tie):
        return self.early.send(tie)

    def early_finish(self, tie):
        return self.early.finish(tie)

    def late_start(self, g, tie):
        d_in = _w_in_from_segments(g["wc_lo"], g["wc_gl"])
        return self.late.start({"w_in": d_in}, tie)

    def late_send(self, tie):
        return self.late.send(tie)

    def late_finish(self, tie):
        return self.late.finish(tie)

    @property
    def summed(self):
        return {**self.early.summed, **self.late.summed}


def kernel(x, mem, g_mix, w_in, b_gate, b_forget, qn_swa, kn_swa, sink_swa, rel_bias, qn_fox, kn_fox, g_mem, w_mem_kv, qn_mem, kn_mem, w_o_swa, w_o_fox, w_o_mem, w_out, g_mlp, w_mlp_up, w_mlp_down, loss_target, m_g_mix, m_w_in, m_b_gate, m_b_forget, m_qn_swa, m_kn_swa, m_sink_swa, m_rel_bias, m_qn_fox, m_kn_fox, m_g_mem, m_w_mem_kv, m_qn_mem, m_kn_mem, m_w_o_swa, m_w_o_fox, m_w_o_mem, m_w_out, m_g_mlp, m_w_mlp_up, m_w_mlp_down, v_g_mix, v_w_in, v_b_gate, v_b_forget, v_qn_swa, v_kn_swa, v_sink_swa, v_rel_bias, v_qn_fox, v_kn_fox, v_g_mem, v_w_mem_kv, v_qn_mem, v_kn_mem, v_w_o_swa, v_w_o_fox, v_w_o_mem, v_w_out, v_g_mlp, v_w_mlp_up, v_w_mlp_down):
    given = dict(locals())
    W = {n: given[n] for n in WEIGHT_NAMES}
    M = {n: given["m_" + n] for n in WEIGHT_NAMES}
    V = {n: given["v_" + n] for n in WEIGHT_NAMES}
    pad_in = ((0, 0), (0, IN_SHARD_PAD - IN_SHARD))

    shards = [jnp.pad(w_in[0].astype(BF16), pad_in)] + [W[n][0].astype(BF16) for n in BIG_NAMES[1:]]
    slots = [jnp.broadcast_to(s[None], (N_SHARD,) + s.shape) for s in shards]
    (g_in,) = _all_gather_shards_async("all_gather_w_in", 1, slots[:1])
    small = {n: (W[n] if n == "rel_bias" else W[n].reshape(1, -1)) for n in SMALL_NAMES}
    h = _rmsnorm("rms_mix", x[0], small["g_mix"], min(512, x.shape[1]))
    g_in, late, h, (m_in, v_in) = lax.optimization_barrier((g_in, slots[1:], h, (M["w_in"][0], V["w_in"][0])))
    M["w_in"], V["w_in"] = m_in[None], v_in[None]
    g_kv, g_oa, g_of, g_om, g_out, g_up, g_down = _all_gather_shards_async("all_gather_weights_async", 8, late)

    place = jnp.stack([2 * lax.axis_index("x") + lax.axis_index("y"), lax.axis_index("c")]).astype(jnp.int32)
    reducer = _GradReducer(place)
    loss, grad_x, grads = _local_step(
        x[0], h, mem[0], loss_target[0], small, g_in, g_kv.reshape(D_MODEL, D_MODEL), (g_oa, g_of, g_om),
        g_out.reshape(D_MODEL, D_MODEL), g_up, g_down.reshape(D_FF, D_MODEL), reducer)

    out = {}

    def adamw_of(names, summed):
        for n in names:
            res = _adamw("adamw_" + n, W[n][0], summed[n], M[n][0], V[n][0])
            out[n] = [r.reshape(W[n].shape) for r in res]

    adamw_of(reducer.early.names, reducer.early.summed)
    shapes = {n: W[n].shape for n in SMALL_NAMES}
    packed = _small_allreduce_adamw(_pack_small(grads, loss), _pack_small(W), _pack_small(M), _pack_small(V))
    done_meanwhile = ([out[n] for n in reducer.early.names], packed)
    (early_out, packed), grad_x = reducer.late_finish((done_meanwhile, grad_x))
    for n, res in zip(reducer.early.names, early_out):
        out[n] = res
    adamw_of(reducer.late.names, reducer.late.summed)
    unpacked = [_unpack_small(p, shapes) for p in packed]
    for n in SMALL_NAMES:
        out[n] = [u[0][n] for u in unpacked]
    loss_total = unpacked[0][1]

    return (loss_total, grad_x.reshape(x.shape),
            *[out[n][0] for n in WEIGHT_NAMES], *[out[n][1] for n in WEIGHT_NAMES],
            *[out[n][2] for n in WEIGHT_NAMES], *[out[n][3] for n in WEIGHT_NAMES])
```

```python
import functools
import math

import jax
import jax.numpy as jnp
from jax import lax
from jax.experimental import pallas as pl
from jax.experimental.pallas import tpu as pltpu
from jax.experimental.pallas import tpu_sc as plsc

F32 = jnp.float32
BF16 = jnp.bfloat16

D_MODEL = 1024
N_MEM = 256
SWA_HEADS = 8
SWA_KV_HEADS = 2
SWA_HEAD_DIM = 64
WINDOW = 128
FOX_HEADS = 8
FOX_HEAD_DIM = 64
MEM_HEADS = 4
MEM_HEAD_DIM = 128
D_FF = 4 * D_MODEL
REL_BUCKETS = 32
REL_MAX_DIST = 128
EPS = 1e-6
NEG = -1e30
GATE_W = 3 * D_MODEL
IN_WIDTH = 5896
N_SHARD = 4
IN_SHARD = IN_WIDTH // N_SHARD
IN_SHARD_PAD = 1536

ADAM_LR = 0.001
ADAM_B1 = 0.9
ADAM_B2 = 0.999
ADAM_EPS = 1e-08
ADAM_WD = 0.01
ADAM_STEP = 10

LANES = 128
V7X_VMEM_BYTES = 64 * 1024 * 1024
VMEM_LIMIT = V7X_VMEM_BYTES * 7 // 8

C_QA, C_QF, C_KF, C_VF, C_QM, C_KA, C_VA, C_FL, C_GL = 0, 512, 1024, 1536, 2048, 2560, 2688, 2816, 3072
LO_W = 3072
PROJ_W = 6144

NN = (((1,), (0,)), ((), ()))
NT = (((1,), (1,)), ((), ()))
TN = (((0,), (0,)), ((), ()))


def _dot(a, b, dims=NN):
    return lax.dot_general(a, b, dims, preferred_element_type=F32)


def _cparams(*sem):
    return pltpu.CompilerParams(dimension_semantics=sem, vmem_limit_bytes=VMEM_LIMIT)


def _split3(a):
    hi = a.astype(BF16)
    r1 = a - hi.astype(F32)
    mid = r1.astype(BF16)
    lo = (r1 - mid.astype(F32)).astype(BF16)
    return hi, mid, lo


def _group_mean(a, g2):
    hi = a.astype(BF16)
    mid = (a - hi.astype(F32)).astype(BF16)
    return _dot(jnp.concatenate([hi, mid], axis=1), g2)


def _dot3_left(g, a):
    hi, mid, lo = _split3(a)
    return _dot(g, hi) + _dot(g, mid) + _dot(g, lo)


def _group_mean_matrix(d):
    r = jnp.arange(LANES)
    g = jnp.where((r[:, None] // d) == (r[None, :] // d), 1.0 / d, 0.0).astype(BF16)
    return jnp.concatenate([g, g], axis=0)


def _lane(shape):
    return lax.broadcasted_iota(jnp.int32, shape, len(shape) - 1)


def _matmul(name, a, b, *, dims, grid, a_spec, b_spec, acc_shape, outs, epilogue, extra=()):
    nk = grid[2]
    n_extra = len(extra)

    def body(a_ref, b_ref, *rest):
        extra_refs = rest[:n_extra]
        out_refs = rest[n_extra:n_extra + len(outs)]
        i, j, k = pl.program_id(0), pl.program_id(1), pl.program_id(2)
        part = _dot(a_ref[...].astype(BF16), b_ref[...].astype(BF16), dims)
        if nk == 1:
            epilogue(part, extra_refs, out_refs, (i, j))
            return
        acc_ref = rest[-1]

        @pl.when(k == 0)
        def _():
            acc_ref[...] = part

        @pl.when((k > 0) & (k < nk - 1))
        def _():
            acc_ref[...] += part

        @pl.when(k == nk - 1)
        def _():
            epilogue(acc_ref[...] + part, extra_refs, out_refs, (i, j))

    res = pl.pallas_call(
        body,
        name=name,
        grid=grid,
        in_specs=[a_spec, b_spec] + [s for _, s in extra],
        out_specs=[s for _, s in outs],
        out_shape=[s for s, _ in outs],
        scratch_shapes=[pltpu.VMEM(acc_shape, F32)] if nk > 1 else [],
        compiler_params=_cparams("arbitrary", "arbitrary", "arbitrary"),
    )(a, b, *[x for x, _ in extra])
    return res


def _epi_store(acc, extra_refs, out_refs, ij):
    out_refs[0][...] = acc.astype(out_refs[0].dtype)


def _rms_rows(x, g):
    r = lax.rsqrt(jnp.mean(x * x, axis=-1, keepdims=True) + EPS)
    return x * r, r


def _rmsnorm_bwd_rows(dh, x, g):
    xhat, r = _rms_rows(x, g)
    dxh = dh * g
    dx = r * (dxh - xhat * jnp.mean(dxh * xhat, axis=-1, keepdims=True))
    return dx, jnp.sum(dh * xhat, axis=0, keepdims=True)


def _rmsnorm(name, x, g, tb):
    T, Dm = x.shape

    def body(x_ref, g_ref, o_ref):
        xhat, _ = _rms_rows(x_ref[...], None)
        o_ref[...] = (xhat * g_ref[...]).astype(o_ref.dtype)

    return pl.pallas_call(
        body, name=name, grid=(T // tb,),
        in_specs=[pl.BlockSpec((tb, Dm), lambda i: (i, 0)), pl.BlockSpec((1, Dm), lambda i: (0, 0))],
        out_specs=pl.BlockSpec((tb, Dm), lambda i: (i, 0)),
        out_shape=jax.ShapeDtypeStruct((T, Dm), BF16),
        compiler_params=_cparams("parallel"),
    )(x, g)


def _head_norm(x, gm, gain):
    ms = _group_mean(x * x, gm)
    r = lax.rsqrt(ms + EPS)
    return x * r * gain, x * r


def _head_norm_bwd(dy, x, gm, gain):
    ms = _group_mean(x * x, gm)
    r = lax.rsqrt(ms + EPS)
    xhat = x * r
    dxh = dy * gain
    dx = r * (dxh - xhat * _group_mean(dxh * xhat, gm))
    return dx, jnp.sum(dy * xhat, axis=0, keepdims=True)


def _log_sigmoid(z):
    return jnp.minimum(z, 0.0) - jnp.log(1.0 + jnp.exp(-jnp.abs(z)))


def _prep_fwd(proj, gains, bfor, tril, gm64, gm128, T, tb):
    nb = T // tb

    def body(qa_ref, qf_ref, kf_ref, vf_ref, qm_ref, ka_ref, va_ref, fl_ref, gains_ref, bfor_ref, tril_ref,
             gm64_ref, gm128_ref,
             qa_o, qf_o, kf_o, vf_o, qm_o, kad_o, vad_o, qaug_o, kaug_o, carry):
        i = pl.program_id(0)
        gm64v = gm64_ref[...]
        gm128v = gm128_ref[...]
        lane = _lane((tb, LANES))

        def norm512(src, dst, row, gm, scale=1.0):
            gain = gains_ref[row:row + 1, :]
            for c in range(4):
                sl = slice(c * LANES, (c + 1) * LANES)
                y, _ = _head_norm(src[:, sl], gm, gain)
                dst[:, sl] = (y * scale).astype(dst.dtype)

        norm512(qa_ref, qa_o, 0, gm64v)
        norm512(qf_ref, qf_o, 2, gm64v, FOX_SCALE)
        norm512(kf_ref, kf_o, 3, gm64v)
        norm512(qm_ref, qm_o, 4, gm128v)
        vf_o[...] = vf_ref[...].astype(vf_o.dtype)

        ka_n, _ = _head_norm(ka_ref[...], gm64v, gains_ref[1:2, :])
        ka_r = pltpu.roll(ka_n, 64, 1)
        va = va_ref[...]
        va_r = pltpu.roll(va, 64, 1)
        lo = lane < 64
        kad_o[0] = jnp.where(lo, ka_n, ka_r).astype(kad_o.dtype)
        kad_o[1] = jnp.where(lo, ka_r, ka_n).astype(kad_o.dtype)
        vad_o[0] = jnp.where(lo, va, va_r).astype(vad_o.dtype)
        vad_o[1] = jnp.where(lo, va_r, va).astype(vad_o.dtype)

        @pl.when(i == 0)
        def _():
            carry[...] = jnp.zeros_like(carry)

        logf = jnp.where(lane < FOX_HEADS, _log_sigmoid(fl_ref[...] + bfor_ref[...]), 0.0)
        c = _dot3_left(tril_ref[...], logf) + carry[0:1, :]
        carry[...] = jnp.broadcast_to(c[tb - 1:tb, :], carry.shape)
        for pair in range(FOX_HEADS // 2):
            qaug = jnp.zeros((tb, LANES), F32)
            kaug = jnp.zeros((tb, LANES), F32)
            for sub in range(2):
                col = jnp.sum(jnp.where(lane == 2 * pair + sub, c, 0.0), axis=1, keepdims=True)
                pieces = [p.astype(F32) for p in _split3(col)]
                base = AUG_STRIDE * sub
                for e in range(3):
                    qaug = jnp.where(lane == base + AUG_C + e, pieces[e], qaug)
                    kaug = jnp.where(lane == base + AUG_NEG_C + e, -pieces[e], kaug)
                qaug = jnp.where((lane >= base + AUG_NEG_C) & (lane < base + AUG_NEG_C + 3), 1.0, qaug)
                ones_k = ((lane >= base + AUG_C) & (lane < base + AUG_C + 3)) | (
                    (lane >= base + AUG_STAT) & (lane < base + AUG_STAT + 3))
                kaug = jnp.where(ones_k, 1.0, kaug)
            sl = slice(pair * LANES, (pair + 1) * LANES)
            qaug_o[:, sl] = qaug.astype(BF16)
            kaug_o[:, sl] = kaug.astype(BF16)

    def seg(width, start):
        return pl.BlockSpec((tb, width), lambda i, s=start // width: (i, s))

    const = lambda shape: pl.BlockSpec(shape, lambda i: tuple(0 for _ in shape))
    rows512 = pl.BlockSpec((tb, 512), lambda i: (i, 0))
    outs = pl.pallas_call(
        body, name="prep_fwd", grid=(nb,),
        in_specs=[seg(512, C_QA), seg(512, C_QF), seg(512, C_KF), seg(512, C_VF), seg(512, C_QM),
                  seg(128, C_KA), seg(128, C_VA), seg(128, C_FL),
                  const((8, LANES)), const((1, LANES)), const((tb, tb)), const((2 * LANES, LANES)), const((2 * LANES, LANES))],
        out_specs=[rows512, rows512, rows512, rows512, rows512,
                   pl.BlockSpec((2, tb, LANES), lambda i: (0, i, 0)), pl.BlockSpec((2, tb, LANES), lambda i: (0, i, 0)),
                   rows512, rows512],
        out_shape=[jax.ShapeDtypeStruct((T, 512), BF16)] * 5
        + [jax.ShapeDtypeStruct((2, T, LANES), BF16)] * 2
        + [jax.ShapeDtypeStruct((T, 512), BF16)] * 2,
        scratch_shapes=[pltpu.VMEM((8, LANES), F32)],
        compiler_params=_cparams("arbitrary"),
    )(proj, proj, proj, proj, proj, proj, proj, proj, gains, bfor, tril, gm64, gm128)
    return outs


def _prep_bwd(proj, dqa, dkad, dvad, dqf, dkf, dvf, dqm, dqf_aug, dkf_aug, gains, bfor, triu, gm64, gm128, T, tb):
    nb = T // tb

    def body(qa_ref, qf_ref, kf_ref, qm_ref, ka_ref, fl_ref,
             dqa_ref, dkad_ref, dvad_ref, dqf_ref, dkf_ref, dvf_ref, dqm_ref, dqfa_ref, dkfa_ref,
             gains_ref, bfor_ref, triu_ref, gm64_ref, gm128_ref,
             dlo_o, gacc_o, carry):
        i = pl.program_id(0)
        gm64v = gm64_ref[...]
        gm128v = gm128_ref[...]
        lane = _lane((tb, LANES))

        @pl.when(i == 0)
        def _():
            carry[...] = jnp.zeros_like(carry)
            gacc_o[...] = jnp.zeros_like(gacc_o)

        def norm512_bwd(dsrc, xsrc, col0, row, gm):
            gain = gains_ref[row:row + 1, :]
            gsum = jnp.zeros((1, LANES), F32)
            for c in range(4):
                sl = slice(c * LANES, (c + 1) * LANES)
                dx, dg = _head_norm_bwd(dsrc[:, sl], xsrc[:, sl], gm, gain)
                dlo_o[:, col0 + c * LANES:col0 + (c + 1) * LANES] = dx.astype(dlo_o.dtype)
                gsum = gsum + dg
            gacc_o[row:row + 1, :] += gsum

        norm512_bwd(dqa_ref, qa_ref, C_QA, 0, gm64v)
        norm512_bwd(dqf_ref, qf_ref, C_QF, 2, gm64v)
        norm512_bwd(dkf_ref, kf_ref, C_KF, 3, gm64v)
        norm512_bwd(dqm_ref, qm_ref, C_QM, 4, gm128v)
        dlo_o[:, C_VF:C_VF + 512] = dvf_ref[...].astype(dlo_o.dtype)

        lo = lane < 64

        def fold(ref):
            f0 = ref[0] + pltpu.roll(ref[0], 64, 1)
            f1 = ref[1] + pltpu.roll(ref[1], 64, 1)
            return jnp.where(lo, f0, f1)

        dka, dg = _head_norm_bwd(fold(dkad_ref), ka_ref[...], gm64v, gains_ref[1:2, :])
        gacc_o[1:2, :] += dg
        dlo_o[:, C_KA:C_KA + LANES] = dka.astype(dlo_o.dtype)
        dlo_o[:, C_VA:C_VA + LANES] = fold(dvad_ref).astype(dlo_o.dtype)

        dc = jnp.zeros((tb, LANES), F32)
        for pair in range(FOX_HEADS // 2):
            sl = slice(pair * LANES, (pair + 1) * LANES)
            rows_sum, cols_sum = dqfa_ref[:, sl], dkfa_ref[:, sl]
            for sub in range(2):
                diff = (jnp.where(lane == AUG_STRIDE * sub + AUG_C, rows_sum, 0.0)
                        - jnp.where(lane == AUG_STRIDE * sub + AUG_NEG_C, cols_sum, 0.0))
                dc = jnp.where(lane == 2 * pair + sub, jnp.sum(diff, axis=1, keepdims=True), dc)
        dlogf = _dot3_left(triu_ref[...], dc) + carry[0:1, :]
        carry[...] = jnp.broadcast_to(dlogf[0:1, :], carry.shape)
        z = fl_ref[...] + bfor_ref[...]
        dfl = jnp.where(lane < FOX_HEADS, dlogf / (1.0 + jnp.exp(z)), 0.0)
        gacc_o[5:6, :] += jnp.sum(dfl, axis=0, keepdims=True)
        dlo_o[:, C_FL:C_FL + LANES] = dfl.astype(dlo_o.dtype)
        dlo_o[:, C_FL + LANES:C_FL + 2 * LANES] = jnp.zeros((tb, LANES), dlo_o.dtype)

    rev = lambda i: nb - 1 - i

    def seg(width, start):
        return pl.BlockSpec((tb, width), lambda i, s=start // width: (rev(i), s))

    const = lambda shape: pl.BlockSpec(shape, lambda i: tuple(0 for _ in shape))
    rows512 = pl.BlockSpec((tb, 512), lambda i: (rev(i), 0))
    dup = pl.BlockSpec((2, tb, LANES), lambda i: (0, rev(i), 0))
    return pl.pallas_call(
        body, name="prep_bwd", grid=(nb,),
        in_specs=[seg(512, C_QA), seg(512, C_QF), seg(512, C_KF), seg(512, C_QM), seg(128, C_KA), seg(128, C_FL),
                  rows512, dup, dup, rows512, rows512, rows512, rows512, rows512, rows512,
                  const((8, LANES)), const((1, LANES)), const((tb, tb)), const((2 * LANES, LANES)), const((2 * LANES, LANES))],
        out_specs=[pl.BlockSpec((tb, LO_W), lambda i: (rev(i), 0)), const((8, LANES))],
        out_shape=[jax.ShapeDtypeStruct((T, LO_W), BF16), jax.ShapeDtypeStruct((8, LANES), F32)],
        scratch_shapes=[pltpu.VMEM((8, LANES), F32)],
        compiler_params=_cparams("arbitrary"),
    )(proj, proj, proj, proj, proj, proj, dqa, dkad, dvad, dqf, dkf, dvf, dqm, dqf_aug, dkf_aug,
      gains, bfor, triu, gm64, gm128)


FOX_SCALE = FOX_HEAD_DIM ** -0.5
AUG_STRIDE = 16
AUG_C = 0
AUG_NEG_C = 3
AUG_STAT = 6
FOX_TQ, FOX_TK = 1024, 1024
FOX_BWD_TQ, FOX_BWD_TK = 1024, 1024


def _fox_head_mask(sub, rows):
    lane = _lane((rows, 2 * LANES))
    main = (lane >= 64 * sub) & (lane < 64 * sub + 64)
    aug = (lane >= LANES + AUG_STRIDE * sub) & (lane < LANES + AUG_STRIDE * (sub + 1))
    return main | aug


def _fox_fwd(q, qaug, k, kaug, v, T, tq, tk):
    nq, nk = T // tq, T // tk
    rep = tk // LANES
    last_of = lambda i: (i * tq + tq - 1) // tk

    def body(q_ref, qa_ref, k_ref, ka_ref, v_ref, o_ref, qab_ref, m_s, acc_s):
        p_, i, j = pl.program_id(0), pl.program_id(1), pl.program_id(2)
        last = last_of(i)

        @pl.when(j == 0)
        def _():
            m_s[...] = jnp.full(m_s.shape, NEG, F32)
            acc_s[...] = jnp.zeros_like(acc_s)

        def step(diagonal):
            q2 = jnp.concatenate([q_ref[...], qa_ref[...]], axis=1)
            k2 = jnp.concatenate([k_ref[...], ka_ref[...]], axis=1)
            v2 = jnp.concatenate([v_ref[...], ka_ref[...]], axis=1)
            if diagonal:
                causal = (lax.broadcasted_iota(jnp.int32, (tq, tk), 1) + j * tk
                          <= lax.broadcasted_iota(jnp.int32, (tq, tk), 0) + i * tq)
            scores = [_dot(jnp.where(_fox_head_mask(sub, tq), q2, jnp.zeros_like(q2)), k2, NT) for sub in range(2)]
            for sub in range(2):
                s = scores[sub]
                if diagonal:
                    s = jnp.where(causal, s, NEG)
                m_prev = m_s[sub]
                m_next = jnp.maximum(m_prev, jnp.max(s, axis=1, keepdims=True))
                p = jnp.exp(s - jnp.tile(m_next, (1, rep)))
                alpha = jnp.exp(m_prev - m_next)
                m_s[sub] = m_next
                acc_s[sub] = acc_s[sub] * jnp.tile(alpha, (1, 2)) + _dot(p.astype(BF16), v2)

        @pl.when(j == last)
        def _():
            step(True)

        @pl.when(j < last)
        def _():
            step(False)

        @pl.when(j == nk - 1)
        def _():
            lane = _lane((tq, LANES))
            outs = []
            qab = qa_ref[...].astype(F32)
            for sub in range(2):
                acc = acc_s[sub]
                base = AUG_STRIDE * sub
                l = jnp.sum(jnp.where(lane == base + AUG_C, acc[:, LANES:], 0.0), axis=1, keepdims=True)
                outs.append(acc[:, :LANES] / l)
                lse = jnp.max(m_s[sub], axis=1, keepdims=True) + jnp.log(l)
                pieces = _split3(-lse)
                for e in range(3):
                    qab = jnp.where(lane == base + AUG_STAT + e, pieces[e].astype(F32), qab)
            o_ref[...] = jnp.where(lane < 64, outs[0], outs[1]).astype(o_ref.dtype)
            qab_ref[...] = qab.astype(BF16)

    qspec = pl.BlockSpec((tq, LANES), lambda p, i, j: (i, p))
    kspec = pl.BlockSpec((tk, LANES), lambda p, i, j: (jnp.minimum(j, last_of(i)), p))
    return pl.pallas_call(
        body, name="fox_fwd", grid=(4, nq, nk),
        in_specs=[qspec, qspec, kspec, kspec, kspec],
        out_specs=[qspec, qspec],
        out_shape=[jax.ShapeDtypeStruct((T, 512), BF16), jax.ShapeDtypeStruct((T, 512), BF16)],
        scratch_shapes=[pltpu.VMEM((2, tq, LANES), F32), pltpu.VMEM((2, tq, 2 * LANES), F32)],
        compiler_params=_cparams("parallel", "parallel", "arbitrary"),
    )(q, qaug, k, kaug, v)


def _fox_bwd(q, qaug, k, kaug, v, do, doaug, T, tq, tk):
    nq, nk = T // tq, T // tk
    first_of = lambda j: (j * tk) // tq

    def body(q_ref, qa_ref, k_ref, ka_ref, v_ref, do_ref, doa_ref,
             dq_ref, dqa_ref, dk_ref, dka_ref, dv_ref, dk_s, dv_s):
        p_, j, i = pl.program_id(0), pl.program_id(1), pl.program_id(2)
        masked = i * tq < (j + 1) * tk - 1

        @pl.when((j == 0) & (i == 0))
        def _():
            dq_ref[...] = jnp.zeros_like(dq_ref)
            dqa_ref[...] = jnp.zeros_like(dqa_ref)

        @pl.when(i == 0)
        def _():
            dk_s[...] = jnp.zeros_like(dk_s)
            dv_s[...] = jnp.zeros_like(dv_s)

        def step(diagonal):
            q2 = jnp.concatenate([q_ref[...], qa_ref[...]], axis=1)
            k2 = jnp.concatenate([k_ref[...], ka_ref[...]], axis=1)
            v2 = jnp.concatenate([v_ref[...], ka_ref[...]], axis=1)
            do2 = jnp.concatenate([do_ref[...], doa_ref[...]], axis=1)
            if diagonal:
                causal = (lax.broadcasted_iota(jnp.int32, (tq, tk), 1) + j * tk
                          <= lax.broadcasted_iota(jnp.int32, (tq, tk), 0) + i * tq)
            qh = [jnp.where(_fox_head_mask(sub, tq), q2, jnp.zeros_like(q2)) for sub in range(2)]
            doh = [jnp.where(_fox_head_mask(sub, tq), do2, jnp.zeros_like(do2)) for sub in range(2)]
            scores = [_dot(qh[sub], k2, NT) for sub in range(2)]
            dps = [_dot(doh[sub], v2, NT) for sub in range(2)]
            dqs = []
            for sub in range(2):
                s = scores[sub]
                if diagonal:
                    s = jnp.where(causal, s, NEG)
                p = jnp.exp(s)
                dsb = (p * dps[sub]).astype(BF16)
                dv_s[...] += _dot(p.astype(BF16), doh[sub][:, :LANES], TN)
                dk_s[...] += _dot(dsb, qh[sub], TN)
                dqs.append(_dot(dsb, k2))
            dq2 = jnp.where(_fox_head_mask(0, tq), dqs[0], dqs[1])
            qrows = pl.ds(pl.multiple_of(i * tq, tq), tq)
            dq_ref[qrows, :] += dq2[:, :LANES] * FOX_SCALE
            dqa_ref[qrows, :] += dq2[:, LANES:]

        @pl.when((i >= first_of(j)) & masked)
        def _():
            step(True)

        @pl.when((i >= first_of(j)) & jnp.logical_not(masked))
        def _():
            step(False)

        @pl.when(i == nq - 1)
        def _():
            dk_ref[...] = dk_s[:, :LANES]
            dka_ref[...] = dk_s[:, LANES:]
            dv_ref[...] = dv_s[...]

    qspec = pl.BlockSpec((tq, LANES), lambda p, j, i: (jnp.maximum(i, first_of(j)), p))
    kspec = pl.BlockSpec((tk, LANES), lambda p, j, i: (j, p))
    resident = pl.BlockSpec((T, LANES), lambda p, j, i: (0, p))
    return pl.pallas_call(
        body, name="fox_bwd", grid=(4, nk, nq),
        in_specs=[qspec, qspec, kspec, kspec, kspec, qspec, qspec],
        out_specs=[resident, resident, kspec, kspec, kspec],
        out_shape=[jax.ShapeDtypeStruct((T, 512), F32)] * 5,
        scratch_shapes=[pltpu.VMEM((tk, 2 * LANES), F32), pltpu.VMEM((tk, LANES), F32)],
        compiler_params=_cparams("arbitrary", "arbitrary", "arbitrary"),
    )(q, qaug, k, kaug, v, do, doaug)


SWA_SUB = 4
SWA_TB = SWA_SUB * WINDOW


def _t5_bucket_matrix():
    t = jnp.arange(WINDOW)[:, None] + WINDOW
    s = jnp.arange(2 * WINDOW)[None, :]
    max_exact = REL_BUCKETS // 2
    d = jnp.maximum(t - s, 0)
    df = jnp.maximum(d, 1).astype(F32)
    large = max_exact + (jnp.log(df / max_exact) / math.log(REL_MAX_DIST / max_exact)
                         * (REL_BUCKETS - max_exact)).astype(jnp.int32)
    large = jnp.minimum(large, REL_BUCKETS - 1)
    return jnp.where(d < max_exact, d, large).astype(jnp.int32)


def _swa_bias(rel_bias, bucket):
    def body(rel_ref, bucket_ref, o_ref):
        b = bucket_ref[...]
        for h in range(SWA_HEADS):
            acc = jnp.zeros(b.shape, F32)
            for r in range(REL_BUCKETS):
                acc = jnp.where(b == r, rel_ref[r, h], acc)
            o_ref[h] = acc

    return pl.pallas_call(
        body, name="swa_bias",
        in_specs=[pl.BlockSpec(memory_space=pltpu.SMEM), pl.BlockSpec(memory_space=pltpu.VMEM)],
        out_specs=pl.BlockSpec(memory_space=pltpu.VMEM),
        out_shape=jax.ShapeDtypeStruct((SWA_HEADS, WINDOW, 2 * WINDOW), F32),
    )(rel_bias, bucket)


def _swa_bias_bwd(dbias, bucket):
    def body(db_ref, bucket_ref, o_ref):
        b = bucket_ref[...]
        lane = _lane((1, LANES))
        for r in range(REL_BUCKETS):
            row = jnp.zeros((1, LANES), F32)
            for h in range(SWA_HEADS):
                part = jnp.sum(jnp.where(b == r, db_ref[h], 0.0), axis=0, keepdims=True)
                tot = jnp.sum(part, axis=1, keepdims=True)
                row = jnp.where(lane == h, tot, row)
            o_ref[r:r + 1, :] = row

    return pl.pallas_call(
        body, name="swa_bias_bwd",
        in_specs=[pl.BlockSpec(memory_space=pltpu.VMEM), pl.BlockSpec(memory_space=pltpu.VMEM)],
        out_specs=pl.BlockSpec(memory_space=pltpu.VMEM),
        out_shape=jax.ShapeDtypeStruct((REL_BUCKETS, LANES), F32),
    )(dbias, bucket)


SWA_GROUP = SWA_HEADS // SWA_KV_HEADS


def _swa_valid(r, i):
    t = (lax.broadcasted_iota(jnp.int32, (SWA_GROUP * WINDOW, 2 * WINDOW), 0) & (WINDOW - 1)) + WINDOW
    s = lax.broadcasted_iota(jnp.int32, (SWA_GROUP * WINDOW, 2 * WINDOW), 1)
    dist = t - s
    band = (dist >= 0) & (dist < WINDOW)
    if r == 0:
        band = band & ((s >= WINDOW) | (i > 0))
    return band


def _swa_stack(blk):
    lane = _lane((WINDOW, LANES))
    parts = []
    for g in range(SWA_GROUP):
        b = blk[:, LANES * (g // 2):LANES * (g // 2 + 1)]
        parts.append(jnp.where((lane >= 64) if g % 2 else (lane < 64), b, jnp.zeros_like(b)))
    return jnp.concatenate(parts, axis=0)


def _swa_unstack(st):
    lane = _lane((WINDOW, LANES))
    W = WINDOW
    return jnp.concatenate([jnp.where(lane < 64, st[2 * b * W:(2 * b + 1) * W], st[(2 * b + 1) * W:(2 * b + 2) * W])
                            for b in range(2)], axis=1)


def _swa_sink_column(sink_ref, kvh):
    row = lax.broadcasted_iota(jnp.int32, (SWA_GROUP * WINDOW, 1), 0)
    col = jnp.full((SWA_GROUP * WINDOW, 1), sink_ref[SWA_GROUP * kvh + SWA_GROUP - 1], F32)
    for g in range(SWA_GROUP - 2, -1, -1):
        col = jnp.where(row < (g + 1) * WINDOW, sink_ref[SWA_GROUP * kvh + g], col)
    return col


def _swa_specs(T):
    W = WINDOW
    qspec = pl.BlockSpec((SWA_TB, 2 * LANES), lambda h, i: (i, h))
    own = pl.BlockSpec((None, SWA_TB, LANES), lambda h, i: (h, i, 0))
    prev = pl.BlockSpec((None, W, LANES), lambda h, i: (h, jnp.maximum(SWA_SUB * i - 1, 0), 0))
    stat = pl.BlockSpec((SWA_GROUP, SWA_TB, LANES), lambda h, i: (h, i, 0))
    bias = pl.BlockSpec((None, SWA_GROUP * W, 2 * W), lambda h, i: (h, 0, 0))
    return qspec, own, prev, stat, bias


def _swa_fwd(sinks, q, kad, vad, bias, T):
    nb = T // SWA_TB
    scale = SWA_HEAD_DIM ** -0.5
    W = WINDOW

    def body(sink_ref, q_ref, k_ref, kp_ref, v_ref, vp_ref, bias_ref, o_ref, lse_ref):
        kvh, i = pl.program_id(0), pl.program_id(1)
        sink = _swa_sink_column(sink_ref, kvh)
        for r in range(SWA_SUB):
            rs = slice(r * W, (r + 1) * W)
            ps = slice((r - 1) * W, r * W)
            k_own, v_own = k_ref[rs, :], v_ref[rs, :]
            k_prev = kp_ref[...] if r == 0 else k_ref[ps, :]
            v_prev = vp_ref[...] if r == 0 else v_ref[ps, :]
            qs = _swa_stack(q_ref[rs, :])
            s = jnp.concatenate([_dot(qs, k_prev, NT), _dot(qs, k_own, NT)], axis=1) * scale + bias_ref[...]
            s = jnp.where(_swa_valid(r, i), s, NEG)
            m = jnp.maximum(jnp.max(s, axis=1, keepdims=True), sink)
            p = jnp.exp(s - m)
            denom = jnp.sum(p, axis=1, keepdims=True) + jnp.exp(sink - m)
            pn = (p / denom).astype(BF16)
            o_ref[rs, :] = _swa_unstack(_dot(pn[:, :W], v_prev) + _dot(pn[:, W:], v_own)).astype(o_ref.dtype)
            lse = m + jnp.log(denom)
            for g in range(SWA_GROUP):
                lse_ref[g, rs, :] = jnp.broadcast_to(lse[g * W:(g + 1) * W], (W, LANES))

    qspec, own, prev, stat, bspec = _swa_specs(T)
    return pl.pallas_call(
        body, name="swa_fwd", grid=(SWA_KV_HEADS, nb),
        in_specs=[pl.BlockSpec(memory_space=pltpu.SMEM), qspec, own, prev, own, prev, bspec],
        out_specs=[qspec, stat],
        out_shape=[jax.ShapeDtypeStruct((T, 512), BF16), jax.ShapeDtypeStruct((SWA_HEADS, T, LANES), F32)],
        compiler_params=_cparams("parallel", "parallel"),
    )(sinks, q, kad, kad, vad, vad, bias.reshape(SWA_KV_HEADS, SWA_GROUP * W, 2 * W))


def _swa_bwd(sinks, q, kad, vad, bias, do, lse, delta, T):
    nb = T // SWA_TB
    scale = SWA_HEAD_DIM ** -0.5
    W = WINDOW

    def body(sink_ref, q_ref, k_ref, kp_ref, v_ref, vp_ref, bias_ref, do_ref, lse_ref, dl_ref,
             dq_ref, dkad_ref, dvad_ref, dbias_ref, dsk_ref):
        kvh, i = pl.program_id(0), pl.program_id(1)
        sink = _swa_sink_column(sink_ref, kvh)

        @pl.when((kvh == 0) & (i == 0))
        def _():
            dkad_ref[...] = jnp.zeros_like(dkad_ref)
            dvad_ref[...] = jnp.zeros_like(dvad_ref)

        @pl.when(i == 0)
        def _():
            dbias_ref[...] = jnp.zeros_like(dbias_ref)
            dsk_ref[...] = jnp.zeros_like(dsk_ref)

        for r in range(SWA_SUB):
            rs = slice(r * W, (r + 1) * W)
            ps = slice((r - 1) * W, r * W)
            k_own, v_own = k_ref[rs, :], v_ref[rs, :]
            k_prev = kp_ref[...] if r == 0 else k_ref[ps, :]
            v_prev = vp_ref[...] if r == 0 else v_ref[ps, :]
            qs = _swa_stack(q_ref[rs, :])
            dos = _swa_stack(do_ref[rs, :])
            lse_b = jnp.concatenate([lse_ref[g, rs, :] for g in range(SWA_GROUP)], axis=0)
            dl_b = jnp.concatenate([dl_ref[g, rs, :] for g in range(SWA_GROUP)], axis=0)
            s = jnp.concatenate([_dot(qs, k_prev, NT), _dot(qs, k_own, NT)], axis=1) * scale + bias_ref[...]
            s = jnp.where(_swa_valid(r, i), s, NEG)
            p = jnp.exp(s - jnp.tile(lse_b, (1, 2)))
            dp = jnp.concatenate([_dot(dos, v_prev, NT), _dot(dos, v_own, NT)], axis=1)
            ds = p * (dp - jnp.tile(dl_b, (1, 2)))
            sink_term = jnp.exp(sink - lse_b) * dl_b
            for g in range(SWA_GROUP):
                dbias_ref[g] += ds[g * W:(g + 1) * W]
                dsk_ref[g:g + 1, :] += jnp.sum(sink_term[g * W:(g + 1) * W], axis=0, keepdims=True)
            dsb = ds.astype(BF16)
            pb = p.astype(BF16)
            dq_ref[rs, :] = _swa_unstack((_dot(dsb[:, :W], k_prev) + _dot(dsb[:, W:], k_own)) * scale)
            own_row = pl.multiple_of(i * SWA_TB + r * W, W)
            dkad_ref[kvh, pl.ds(own_row, W), :] += _dot(dsb[:, W:], qs, TN) * scale
            dvad_ref[kvh, pl.ds(own_row, W), :] += _dot(pb[:, W:], dos, TN)
            dk_prev = _dot(dsb[:, :W], qs, TN) * scale
            dv_prev = _dot(pb[:, :W], dos, TN)
            if r == 0:
                @pl.when(i > 0)
                def _():
                    prev_row = pl.multiple_of(i * SWA_TB - W, W)
                    dkad_ref[kvh, pl.ds(prev_row, W), :] += dk_prev
                    dvad_ref[kvh, pl.ds(prev_row, W), :] += dv_prev
            else:
                prev_row = pl.multiple_of(i * SWA_TB + (r - 1) * W, W)
                dkad_ref[kvh, pl.ds(prev_row, W), :] += dk_prev
                dvad_ref[kvh, pl.ds(prev_row, W), :] += dv_prev

    qspec, own, prev, stat, bspec = _swa_specs(T)
    full = pl.BlockSpec((SWA_KV_HEADS, T, LANES), lambda h, i: (0, 0, 0))
    return pl.pallas_call(
        body, name="swa_bwd", grid=(SWA_KV_HEADS, nb),
        in_specs=[pl.BlockSpec(memory_space=pltpu.SMEM), qspec, own, prev, own, prev, bspec, qspec, stat, stat],
        out_specs=[qspec, full, full, pl.BlockSpec((SWA_GROUP, W, 2 * W), lambda h, i: (h, 0, 0)),
                   pl.BlockSpec((None, 8, LANES), lambda h, i: (h, 0, 0))],
        out_shape=[jax.ShapeDtypeStruct((T, 512), F32), jax.ShapeDtypeStruct((SWA_KV_HEADS, T, LANES), F32),
                   jax.ShapeDtypeStruct((SWA_KV_HEADS, T, LANES), F32), jax.ShapeDtypeStruct((SWA_HEADS, W, 2 * W), F32),
                   jax.ShapeDtypeStruct((SWA_KV_HEADS, 8, LANES), F32)],
        compiler_params=_cparams("arbitrary", "arbitrary"),
    )(sinks, q, kad, kad, vad, vad, bias.reshape(SWA_KV_HEADS, SWA_GROUP * W, 2 * W), do, lse, delta)


def _mem_fwd(q, mk, mv, T, tq):
    scale = MEM_HEAD_DIM ** -0.5

    def body(q_ref, k_ref, v_ref, o_ref, lse_ref):
        s = _dot(q_ref[...], k_ref[...], NT) * scale
        m = jnp.max(s, axis=1, keepdims=True)
        p = jnp.exp(s - m)
        l = jnp.sum(p, axis=1, keepdims=True)
        o_ref[...] = _dot((p / l).astype(BF16), v_ref[...]).astype(o_ref.dtype)
        lse_ref[...] = jnp.broadcast_to(m + jnp.log(l), (tq, LANES))

    qspec = pl.BlockSpec((tq, LANES), lambda h, i: (i, h))
    kspec = pl.BlockSpec((N_MEM, LANES), lambda h, i: (0, h))
    return pl.pallas_call(
        body, name="mem_fwd", grid=(MEM_HEADS, T // tq),
        in_specs=[qspec, kspec, kspec],
        out_specs=[qspec, pl.BlockSpec((None, tq, LANES), lambda h, i: (h, i, 0))],
        out_shape=[jax.ShapeDtypeStruct((T, 512), BF16), jax.ShapeDtypeStruct((MEM_HEADS, T, LANES), F32)],
        compiler_params=_cparams("parallel", "parallel"),
    )(q, mk, mv)


def _mem_bwd(q, mk, mv, do, lse, delta, T, tq):
    scale = MEM_HEAD_DIM ** -0.5
    rep = N_MEM // LANES

    def body(q_ref, k_ref, v_ref, do_ref, lse_ref, dl_ref, dq_ref, dk_ref, dv_ref):
        i = pl.program_id(1)

        @pl.when(i == 0)
        def _():
            dk_ref[...] = jnp.zeros_like(dk_ref)
            dv_ref[...] = jnp.zeros_like(dv_ref)

        qv, dov = q_ref[...], do_ref[...]
        s = _dot(qv, k_ref[...], NT) * scale
        p = jnp.exp(s - jnp.tile(lse_ref[...], (1, rep)))
        dp = _dot(dov, v_ref[...], NT)
        ds = p * (dp - jnp.tile(dl_ref[...], (1, rep)))
        dsb = ds.astype(BF16)
        dq_ref[...] = _dot(dsb, k_ref[...]) * scale
        dk_ref[...] += _dot(dsb, qv, TN) * scale
        dv_ref[...] += _dot(p.astype(BF16), dov, TN)

    qspec = pl.BlockSpec((tq, LANES), lambda h, i: (i, h))
    kspec = pl.BlockSpec((N_MEM, LANES), lambda h, i: (0, h))
    stat = pl.BlockSpec((None, tq, LANES), lambda h, i: (h, i, 0))
    return pl.pallas_call(
        body, name="mem_bwd", grid=(MEM_HEADS, T // tq),
        in_specs=[qspec, kspec, kspec, qspec, stat, stat],
        out_specs=[qspec, kspec, kspec],
        out_shape=[jax.ShapeDtypeStruct((T, 512), F32), jax.ShapeDtypeStruct((N_MEM, 512), F32),
                   jax.ShapeDtypeStruct((N_MEM, 512), F32)],
        compiler_params=_cparams("arbitrary", "arbitrary"),
    )(q, mk, mv, do, lse, delta)


def _mem_prep_fwd(mem, g_mem, w_kv, kn_gain, gm128):
    def body(mem_ref, g_ref, w_ref, kn_ref, gm_ref, memn_o, kv_o, mk_o, mv_o):
        xhat, _ = _rms_rows(mem_ref[...], None)
        memn = (xhat * g_ref[...]).astype(BF16)
        memn_o[...] = memn
        kv = _dot(memn, w_ref[...])
        kv_o[...] = kv
        gm = gm_ref[...]
        for c in range(4):
            sl = slice(c * LANES, (c + 1) * LANES)
            y, _ = _head_norm(kv[:, sl], gm, kn_ref[...])
            mk_o[:, sl] = y.astype(BF16)
        mv_o[...] = kv[:, 512:].astype(BF16)

    vm = pl.BlockSpec(memory_space=pltpu.VMEM)
    return pl.pallas_call(
        body, name="mem_prep_fwd", in_specs=[vm] * 5, out_specs=[vm] * 4,
        out_shape=[jax.ShapeDtypeStruct((N_MEM, D_MODEL), BF16), jax.ShapeDtypeStruct((N_MEM, D_MODEL), F32),
                   jax.ShapeDtypeStruct((N_MEM, 512), BF16), jax.ShapeDtypeStruct((N_MEM, 512), BF16)],
        compiler_params=pltpu.CompilerParams(vmem_limit_bytes=VMEM_LIMIT),
    )(mem, g_mem, w_kv, kn_gain, gm128)


def _mem_prep_bwd(mem, g_mem, memn, kv, w_kv, kn_gain, gm128, dmk, dmv):
    def body(mem_ref, g_ref, memn_ref, kv_ref, w_ref, kn_ref, gm_ref, dmk_ref, dmv_ref, dw_o, dg_o, dkn_o, dkv_s):
        gm = gm_ref[...]
        dkn = jnp.zeros((1, LANES), F32)
        for c in range(4):
            sl = slice(c * LANES, (c + 1) * LANES)
            dx, dg = _head_norm_bwd(dmk_ref[:, sl], kv_ref[:, sl], gm, kn_ref[...])
            dkv_s[:, sl] = dx.astype(BF16)
            dkn = dkn + dg
        dkn_o[...] = dkn
        dkv_s[:, 512:] = dmv_ref[...].astype(BF16)
        dkv = dkv_s[...]
        dw_o[...] = _dot(memn_ref[...], dkv, TN)
        dmemn = _dot(dkv, w_ref[...], NT)
        xhat, _ = _rms_rows(mem_ref[...], None)
        dg_o[...] = jnp.sum(dmemn * xhat, axis=0, keepdims=True)

    vm = pl.BlockSpec(memory_space=pltpu.VMEM)
    return pl.pallas_call(
        body, name="mem_prep_bwd", in_specs=[vm] * 9, out_specs=[vm] * 3,
        out_shape=[jax.ShapeDtypeStruct((D_MODEL, D_MODEL), F32), jax.ShapeDtypeStruct((1, D_MODEL), F32),
                   jax.ShapeDtypeStruct((1, LANES), F32)],
        scratch_shapes=[pltpu.VMEM((N_MEM, D_MODEL), BF16)],
        compiler_params=pltpu.CompilerParams(vmem_limit_bytes=VMEM_LIMIT),
    )(mem, g_mem, memn, kv, w_kv, kn_gain, gm128, dmk, dmv)


SLOT_O = D_MODEL // N_SHARD


def _merge_fwd(proj, b_gate, o3, w3, T, tb):
    def body(gl_ref, bg_ref, oa_ref, of_ref, om_ref, wa_ref, wf_ref, wm_ref, out_ref):
        o_refs = (oa_ref, of_ref, om_ref)
        w_refs = (wa_ref, wf_ref, wm_ref)
        for n in range(N_SHARD):
            acc = jnp.zeros((tb, SLOT_O), F32)
            for b in range(3):
                c0 = b * D_MODEL + n * SLOT_O
                g = jax.nn.sigmoid(gl_ref[:, c0:c0 + SLOT_O] + bg_ref[:, c0:c0 + SLOT_O])
                acc = acc + g * _dot(o_refs[b][...], w_refs[b][n])
            out_ref[:, n * SLOT_O:(n + 1) * SLOT_O] = acc.astype(out_ref.dtype)

    rows = pl.BlockSpec((tb, 512), lambda i: (i, 0))
    wspec = pl.BlockSpec((N_SHARD, 512, SLOT_O), lambda i: (0, 0, 0))
    return pl.pallas_call(
        body, name="merge_fwd", grid=(T // tb,),
        in_specs=[pl.BlockSpec((tb, GATE_W), lambda i: (i, 1)), pl.BlockSpec((1, GATE_W), lambda i: (0, 0)),
                  rows, rows, rows, wspec, wspec, wspec],
        out_specs=pl.BlockSpec((tb, D_MODEL), lambda i: (i, 0)),
        out_shape=jax.ShapeDtypeStruct((T, D_MODEL), BF16),
        compiler_params=_cparams("parallel"),
    )(proj, b_gate, *o3, *w3)


def _merge_bwd(proj, b_gate, o3, w3, dmerged, T, tb):
    heads = (SWA_HEADS, FOX_HEADS, MEM_HEADS)

    def body(gl_ref, bg_ref, oa_ref, of_ref, om_ref, wa_ref, wf_ref, wm_ref, dm_ref,
             dgl_o, doa_o, dof_o, dom_o, dla_o, dlf_o, dlm_o, dwa_o, dwf_o, dwm_o, dbg_o):
        i = pl.program_id(0)
        o_refs = (oa_ref, of_ref, om_ref)
        w_refs = (wa_ref, wf_ref, wm_ref)
        do_refs = (doa_o, dof_o, dom_o)
        dl_refs = (dla_o, dlf_o, dlm_o)
        dw_refs = (dwa_o, dwf_o, dwm_o)

        @pl.when(i == 0)
        def _():
            for r in dw_refs:
                r[...] = jnp.zeros_like(r)
            dbg_o[...] = jnp.zeros_like(dbg_o)

        lane = _lane((tb, LANES))
        for b in range(3):
            ob = o_refs[b][...]
            do = jnp.zeros((tb, 512), F32)
            for n in range(N_SHARD):
                c0 = b * D_MODEL + n * SLOT_O
                g = jax.nn.sigmoid(gl_ref[:, c0:c0 + SLOT_O] + bg_ref[:, c0:c0 + SLOT_O])
                dm = dm_ref[:, n * SLOT_O:(n + 1) * SLOT_O]
                y = _dot(ob, w_refs[b][n])
                dgl = dm * y * g * (1.0 - g)
                dgl_o[:, c0:c0 + SLOT_O] = dgl.astype(dgl_o.dtype)
                dbg_o[:, c0:c0 + SLOT_O] += jnp.sum(dgl, axis=0, keepdims=True)
                dy = (dm * g).astype(BF16)
                do = do + _dot(dy, w_refs[b][n], NT)
                dw_refs[b][n] += _dot(ob, dy, TN)
            do_refs[b][...] = do.astype(BF16)
            prod = do * ob.astype(F32)
            for c in range(4):
                blk = prod[:, c * LANES:(c + 1) * LANES]
                if heads[b] == 8:
                    lo = jnp.sum(jnp.where(lane < 64, blk, 0.0), axis=1, keepdims=True)
                    hi = jnp.sum(jnp.where(lane >= 64, blk, 0.0), axis=1, keepdims=True)
                    if b == 1:
                        aug = jnp.zeros((tb, LANES), F32)
                        for sub, dl in enumerate((lo, hi)):
                            for e, piece in enumerate(_split3(-dl)):
                                aug = jnp.where(lane == AUG_STRIDE * sub + AUG_C + e, piece.astype(F32), aug)
                        dl_refs[b][:, c * LANES:(c + 1) * LANES] = aug.astype(BF16)
                    else:
                        dl_refs[b][2 * c] = jnp.broadcast_to(lo, (tb, LANES))
                        dl_refs[b][2 * c + 1] = jnp.broadcast_to(hi, (tb, LANES))
                else:
                    dl_refs[b][c] = jnp.broadcast_to(jnp.sum(blk, axis=1, keepdims=True), (tb, LANES))

    rows = pl.BlockSpec((tb, 512), lambda i: (i, 0))
    wspec = pl.BlockSpec((N_SHARD, 512, SLOT_O), lambda i: (0, 0, 0))
    stat = lambda h: pl.BlockSpec((h, tb, LANES), lambda i: (0, i, 0))
    return pl.pallas_call(
        body, name="merge_bwd", grid=(T // tb,),
        in_specs=[pl.BlockSpec((tb, GATE_W), lambda i: (i, 1)), pl.BlockSpec((1, GATE_W), lambda i: (0, 0)),
                  rows, rows, rows, wspec, wspec, wspec, pl.BlockSpec((tb, D_MODEL), lambda i: (i, 0))],
        out_specs=[pl.BlockSpec((tb, GATE_W), lambda i: (i, 0)), rows, rows, rows,
                   stat(8), rows, stat(4), wspec, wspec, wspec, pl.BlockSpec((1, GATE_W), lambda i: (0, 0))],
        out_shape=[jax.ShapeDtypeStruct((T, GATE_W), BF16)] + [jax.ShapeDtypeStruct((T, 512), BF16)] * 3
        + [jax.ShapeDtypeStruct((8, T, LANES), F32), jax.ShapeDtypeStruct((T, 512), BF16),
           jax.ShapeDtypeStruct((4, T, LANES), F32)]
        + [jax.ShapeDtypeStruct((N_SHARD, 512, SLOT_O), F32)] * 3 + [jax.ShapeDtypeStruct((1, GATE_W), F32)],
        compiler_params=_cparams("arbitrary"),
    )(proj, b_gate, *o3, *w3, dmerged)


def _local_step(x, h, mem, tgt, small, g_in, w_kv, w_o3, w_out, w_up, w_down, reducer):
    T = x.shape[0]
    tm = min(512, T)
    tile2 = lambda v: jnp.tile(v.reshape(1, -1), (1, LANES // v.size))
    gains = jnp.concatenate([tile2(small["qn_swa"]), tile2(small["kn_swa"]), tile2(small["qn_fox"]),
                             tile2(small["kn_fox"]), tile2(small["qn_mem"]), jnp.zeros((3, LANES), F32)], axis=0)
    kn_mem = small["kn_mem"].reshape(1, LANES)
    bfor = jnp.pad(small["b_forget"].reshape(1, -1), ((0, 0), (0, LANES - FOX_HEADS)))
    gm64 = _group_mean_matrix(64)
    gm128 = _group_mean_matrix(128)
    tb_prep = min(256, T)
    ones = jnp.ones((tb_prep, tb_prep), F32)
    tril = jnp.tril(ones).astype(BF16)
    triu = jnp.triu(ones).astype(BF16)
    bucket = _t5_bucket_matrix()
    g_mix, g_mlp, g_mem = small["g_mix"], small["g_mlp"], small["g_mem"]
    b_gate = small["b_gate"]
    sinks = small["sink_swa"].reshape(-1)

    tl = min(1024, T)
    sq = pl.BlockSpec((tl, D_MODEL), lambda i, j, k: (i, j))
    wc = _w_in_to_segments(g_in)
    (proj,) = _matmul(
        "mm_proj", h, wc, dims=NN, grid=(T // tl, PROJ_W // D_MODEL, 1),
        a_spec=pl.BlockSpec((tl, D_MODEL), lambda i, j, k: (i, 0)),
        b_spec=pl.BlockSpec((D_MODEL, D_MODEL), lambda i, j, k: (0, j)),
        acc_shape=(tl, D_MODEL),
        outs=[(jax.ShapeDtypeStruct((T, PROJ_W), F32), sq)],
        epilogue=_epi_store)
    qa, qf, kf, vf, qm, kad, vad, qf_aug, kf_aug = _prep_fwd(proj, gains, bfor, tril, gm64, gm128, T, tb_prep)
    bias = _swa_bias(small["rel_bias"], bucket)
    o_swa, lse_swa = _swa_fwd(sinks, qa, kad, vad, bias, T)
    o_fox, qf_aug_bwd = _fox_fwd(qf, qf_aug, kf, kf_aug, vf, T, min(FOX_TQ, T), min(FOX_TK, T))
    memn, kv, mk, mv = _mem_prep_fwd(mem, g_mem, w_kv, kn_mem, gm128)
    o_mem, lse_mem = _mem_fwd(qm, mk, mv, T, tm)
    o3 = (o_swa, o_fox, o_mem)
    merged = _merge_fwd(proj, b_gate, o3, w_o3, T, min(512, T))

    def epi_residual(acc, extra_refs, out_refs, ij):
        out_refs[0][...] = extra_refs[0][...] + acc

    row_full = pl.BlockSpec((tm, D_MODEL), lambda i, j, k: (i, 0))
    row_big = pl.BlockSpec((tl, D_MODEL), lambda i, j, k: (i, 0))
    whole = pl.BlockSpec((D_MODEL, D_MODEL), lambda i, j, k: (0, 0))
    (x2,) = _matmul(
        "mm_out", merged, w_out, dims=NN, grid=(T // tl, 1, 1),
        a_spec=row_big, b_spec=whole,
        acc_shape=(tl, D_MODEL), extra=[(x, row_big)],
        outs=[(jax.ShapeDtypeStruct((T, D_MODEL), F32), row_big)], epilogue=epi_residual)
    hm = _rmsnorm("rms_mlp", x2, g_mlp, tm)

    def epi_relu2(acc, extra_refs, out_refs, ij):
        out_refs[0][...] = acc
        r = jnp.maximum(acc, 0.0)
        out_refs[1][...] = (r * r).astype(BF16)

    up, u = _matmul(
        "mm_up", hm, w_up, dims=NN, grid=(T // tl, N_SHARD, 1),
        a_spec=row_big, b_spec=pl.BlockSpec((None, D_MODEL, D_MODEL), lambda i, j, k: (j, 0, 0)),
        acc_shape=(tl, D_MODEL),
        outs=[(jax.ShapeDtypeStruct((T, D_FF), F32), sq), (jax.ShapeDtypeStruct((T, D_FF), BF16), sq)],
        epilogue=epi_relu2)

    def epi_loss(acc, extra_refs, out_refs, ij):
        y = extra_refs[0][...] + acc
        err = y - extra_refs[1][...]
        dyv = err * (1.0 / D_MODEL)
        out_refs[0][...] = dyv
        out_refs[2][...] = dyv.astype(BF16)
        sq = jnp.sum(jnp.sum(err * err, axis=1, keepdims=True), axis=0, keepdims=True)

        @pl.when(ij[0] == 0)
        def _():
            out_refs[1][...] = jnp.zeros_like(out_refs[1])

        out_refs[1][...] += jnp.broadcast_to(sq, out_refs[1].shape)

    kblk = pl.BlockSpec((tl, D_MODEL), lambda i, j, k: (i, k))
    dy, loss_acc, dy_bf = _matmul(
        "mm_down", u, w_down, dims=NN, grid=(T // tl, 1, N_SHARD),
        a_spec=kblk, b_spec=pl.BlockSpec((D_MODEL, D_MODEL), lambda i, j, k: (k, 0)),
        acc_shape=(tl, D_MODEL), extra=[(x2, row_big), (tgt, row_big)],
        outs=[(jax.ShapeDtypeStruct((T, D_MODEL), F32), row_big),
              (jax.ShapeDtypeStruct((8, LANES), F32), pl.BlockSpec((8, LANES), lambda i, j, k: (0, 0))),
              (jax.ShapeDtypeStruct((T, D_MODEL), BF16), row_big)],
        epilogue=epi_loss)
    loss = loss_acc[0, 0] * (0.5 / D_MODEL)

    def epi_dup(acc, extra_refs, out_refs, ij):
        out_refs[0][...] = (acc * (2.0 * jnp.maximum(extra_refs[0][...], 0.0))).astype(BF16)

    (dup,) = _matmul(
        "mm_dup", dy_bf, w_down, dims=NT, grid=(T // tl, N_SHARD, 1),
        a_spec=row_big, b_spec=pl.BlockSpec((D_MODEL, D_MODEL), lambda i, j, k: (j, 0)),
        acc_shape=(tl, D_MODEL), extra=[(up, sq)],
        outs=[(jax.ShapeDtypeStruct((T, D_FF), BF16), sq)], epilogue=epi_dup)

    nkt = T // tl
    t_rows = pl.BlockSpec((tl, D_MODEL), lambda i, j, k: (k, i))
    t_cols = pl.BlockSpec((tl, D_MODEL), lambda i, j, k: (k, j))
    (d_w_down,) = _matmul(
        "mm_dw_down", u, dy_bf, dims=TN, grid=(N_SHARD, 1, nkt),
        a_spec=t_rows, b_spec=t_cols, acc_shape=(D_MODEL, D_MODEL),
        outs=[(jax.ShapeDtypeStruct((D_FF, D_MODEL), F32), pl.BlockSpec((D_MODEL, D_MODEL), lambda i, j, k: (i, 0)))],
        epilogue=_epi_store)
    (d_w_up,) = _matmul(
        "mm_dw_up", hm, dup, dims=TN, grid=(1, N_SHARD, nkt),
        a_spec=t_rows, b_spec=t_cols, acc_shape=(D_MODEL, D_MODEL),
        outs=[(jax.ShapeDtypeStruct((N_SHARD, D_MODEL, D_MODEL), F32),
               pl.BlockSpec((None, D_MODEL, D_MODEL), lambda i, j, k: (j, 0, 0)))],
        epilogue=_epi_store)

    def epi_rms_bwd(acc, extra_refs, out_refs, ij):
        dx, dg = _rmsnorm_bwd_rows(acc, extra_refs[0][...], extra_refs[1][...])
        out_refs[0][...] = dx + extra_refs[2][...]

        @pl.when(ij[0] == 0)
        def _():
            out_refs[1][...] = jnp.zeros_like(out_refs[1])

        out_refs[1][...] += dg

    gain_spec = pl.BlockSpec((1, D_MODEL), lambda i, j, k: (0, 0))
    dx2, d_g_mlp = _matmul(
        "mm_dhm", dup, w_up, dims=NT, grid=(T // tl, 1, N_SHARD),
        a_spec=kblk, b_spec=pl.BlockSpec((None, D_MODEL, D_MODEL), lambda i, j, k: (k, 0, 0)),
        acc_shape=(tl, D_MODEL), extra=[(x2, row_big), (g_mlp, gain_spec), (dy, row_big)],
        outs=[(jax.ShapeDtypeStruct((T, D_MODEL), F32), row_big), (jax.ShapeDtypeStruct((1, D_MODEL), F32), gain_spec)],
        epilogue=epi_rms_bwd)

    (dmerged,) = _matmul(
        "mm_dmerged", dx2, w_out, dims=NT, grid=(T // tl, 1, 1),
        a_spec=row_big, b_spec=whole,
        acc_shape=(tl, D_MODEL), outs=[(jax.ShapeDtypeStruct((T, D_MODEL), F32), row_big)], epilogue=_epi_store)
    (d_w_out,) = _matmul(
        "mm_dw_out", merged, dx2, dims=TN, grid=(1, 1, nkt),
        a_spec=t_rows, b_spec=t_cols, acc_shape=(D_MODEL, D_MODEL),
        outs=[(jax.ShapeDtypeStruct((D_MODEL, D_MODEL), F32), whole)],
        epilogue=_epi_store)
    (dgl, do_swa, do_fox, do_mem, dl_swa, do_fox_aug, dl_mem, d_wo_swa, d_wo_fox, d_wo_mem, d_b_gate) = _merge_bwd(
        proj, b_gate, o3, w_o3, dmerged, T, min(512, T))

    dqm, dmk, dmv = _mem_bwd(qm, mk, mv, do_mem, lse_mem, dl_mem, T, tm)
    d_w_kv, d_g_mem, d_kn_mem = _mem_prep_bwd(mem, g_mem, memn, kv, w_kv, kn_mem, gm128, dmk, dmv)
    do_swa = reducer.early_start({"w_mlp_down": d_w_down, "w_mlp_up": d_w_up, "w_out": d_w_out, "w_mem_kv": d_w_kv,
                                  "w_o_swa": d_wo_swa, "w_o_fox": d_wo_fox, "w_o_mem": d_wo_mem}, do_swa)
    dqa, dkad, dvad, dbias, dsk = _swa_bwd(sinks, qa, kad, vad, bias, do_swa, lse_swa, dl_swa, T)
    dqa, do_fox = reducer.early_send((dqa, do_fox))
    dqf, dqf_aug, dkf, dkf_aug, dvf = _fox_bwd(qf, qf_aug_bwd, kf, kf_aug, vf, do_fox, do_fox_aug, T,
                                               min(FOX_BWD_TQ, T), min(FOX_BWD_TK, T))
    dvf = reducer.early_finish(dvf)
    d_rel = _swa_bias_bwd(dbias, bucket)
    dlo, gacc = _prep_bwd(proj, dqa, dkad, dvad, dqf, dkf, dvf, dqm, dqf_aug, dkf_aug, gains, bfor, triu, gm64, gm128,
                          T, tb_prep)

    def dwc_half(name, dpart):
        (res,) = _matmul(
            name, h, dpart, dims=TN, grid=(1, LO_W // D_MODEL, nkt),
            a_spec=t_rows, b_spec=t_cols, acc_shape=(D_MODEL, D_MODEL),
            outs=[(jax.ShapeDtypeStruct((D_MODEL, LO_W), F32), pl.BlockSpec((D_MODEL, D_MODEL), lambda i, j, k: (0, j)))],
            epilogue=_epi_store)
        return res

    d_wc_lo = dwc_half("mm_dwc_lo", dlo)
    d_wc_gl = dwc_half("mm_dwc_gl", dgl)
    dlo = reducer.late_start({"wc_lo": d_wc_lo, "wc_gl": d_wc_gl}, dlo)
    (dh_lo,) = _matmul(
        "mm_dh_lo", dlo, wc, dims=NT, grid=(T // tl, 1, LO_W // D_MODEL),
        a_spec=kblk, b_spec=pl.BlockSpec((D_MODEL, D_MODEL), lambda i, j, k: (0, k)),
        acc_shape=(tl, D_MODEL), outs=[(jax.ShapeDtypeStruct((T, D_MODEL), F32), row_big)], epilogue=_epi_store)
    dh_lo = reducer.late_send(dh_lo)

    def epi_dx(acc, extra_refs, out_refs, ij):
        dhh = acc + extra_refs[3][...]
        dx, dg = _rmsnorm_bwd_rows(dhh, extra_refs[0][...], extra_refs[1][...])
        out_refs[0][...] = dx + extra_refs[2][...]

        @pl.when(ij[0] == 0)
        def _():
            out_refs[1][...] = jnp.zeros_like(out_refs[1])

        out_refs[1][...] += dg

    grad_x, d_g_mix = _matmul(
        "mm_dh_gl", dgl, wc, dims=NT, grid=(T // tl, 1, GATE_W // D_MODEL),
        a_spec=kblk, b_spec=pl.BlockSpec((D_MODEL, D_MODEL), lambda i, j, k: (0, k + LO_W // D_MODEL)),
        acc_shape=(tl, D_MODEL), extra=[(x, row_big), (g_mix, gain_spec), (dx2, row_big), (dh_lo, row_big)],
        outs=[(jax.ShapeDtypeStruct((T, D_MODEL), F32), row_big), (jax.ShapeDtypeStruct((1, D_MODEL), F32), gain_spec)],
        epilogue=epi_dx)

    fold64 = lambda row: (row[:64] + row[64:]).reshape(1, 64)
    grads = {
        "g_mix": d_g_mix, "b_gate": d_b_gate, "b_forget": gacc[5, :FOX_HEADS].reshape(1, FOX_HEADS),
        "qn_swa": fold64(gacc[0]), "kn_swa": fold64(gacc[1]),
        "sink_swa": -dsk[:, :SWA_GROUP, 0].reshape(1, SWA_HEADS), "rel_bias": d_rel[:, :SWA_HEADS],
        "qn_fox": fold64(gacc[2]), "kn_fox": fold64(gacc[3]),
        "g_mem": d_g_mem, "qn_mem": gacc[4].reshape(1, LANES), "kn_mem": d_kn_mem, "g_mlp": d_g_mlp,
    }
    return loss, grad_x, grads


MESH = pl.DeviceIdType.MESH
ANY = pl.BlockSpec(memory_space=pl.ANY)


def _place():
    x, y, c = lax.axis_index("x"), lax.axis_index("y"), lax.axis_index("c")
    chips = [(1 - x, y), (x, 1 - y), (1 - x, 1 - y)]
    return x, y, c, chips


def _handshake(peers):
    barrier = pltpu.get_barrier_semaphore()
    for peer in peers:
        pl.semaphore_signal(barrier, inc=1, device_id=peer, device_id_type=MESH)
    pl.semaphore_wait(barrier, len(peers))


def _all_gather_shards_async(name, collective_id, slots):
    n = len(slots)
    bufs = [jax.new_ref(s, memory_space=pltpu.MemorySpace.HBM) for s in slots]

    def body(ici_send, ici_recv, d2d_send, d2d_recv):
        x, y, c, chips = _place()
        sibling = (x, y, 1 - c)
        me = 2 * x + y
        _handshake([(px, py, c) for px, py in chips] + [sibling])

        def half(a, who):
            hr = slots[a].shape[1] // 2
            return pl.ds(pl.multiple_of(who * hr, hr), hr)

        def ici(a, j, slot, to):
            return pltpu.make_async_remote_copy(
                src_ref=bufs[a].at[me, half(a, c)], dst_ref=bufs[a].at[slot, half(a, c)],
                send_sem=ici_send.at[3 * a + j], recv_sem=ici_recv.at[3 * a + j], device_id=to, device_id_type=MESH)

        def d2d(a, j, slot, which):
            part = bufs[a].at[slot, half(a, which)]
            return pltpu.make_async_remote_copy(
                src_ref=part, dst_ref=part, send_sem=d2d_send.at[3 * a + j], recv_sem=d2d_recv.at[3 * a + j],
                device_id=sibling, device_id_type=MESH)

        sends = [ici(a, j, me, (*chip, c)) for a in range(n) for j, chip in enumerate(chips)]
        for cp in sends:
            cp.start()
        passed = []
        for a in range(n):
            for j, (px, py) in enumerate(chips):
                ici(a, j, 2 * px + py, (px, py, c)).wait_recv()
                cp = d2d(a, j, 2 * px + py, c)
                cp.start()
                passed.append(cp)
        for a in range(n):
            for j, (px, py) in enumerate(chips):
                d2d(a, j, 2 * px + py, 1 - c).wait_recv()
        for cp in sends + passed:
            cp.wait_send()

    pl.kernel(
        body, mesh=plsc.ScalarSubcoreMesh(axis_name="seq", num_cores=1), name=name,
        scratch_types=[pltpu.SemaphoreType.DMA((3 * n,))] * 4,
        compiler_params=pltpu.CompilerParams(collective_id=collective_id),
    )()
    return [b[...] for b in bufs]


def _sequencer_call(name, collective_id, n_sems, body):
    pl.kernel(
        body, mesh=plsc.ScalarSubcoreMesh(axis_name="seq", num_cores=1), name=name,
        scratch_types=[pltpu.SemaphoreType.DMA((n_sems,))] * 2,
        compiler_params=pltpu.CompilerParams(collective_id=collective_id),
    )()


def _hbm_ref(value):
    return jax.new_ref(value, memory_space=pltpu.MemorySpace.HBM)


def _pair_exchange(name, collective_id, gs):
    n = len(gs)
    src = [_hbm_ref(g) for g in gs]
    stage = [jax.empty_ref(jax.ShapeDtypeStruct((N_SHARD, g.shape[1] // 2, g.shape[2]), g.dtype),
                           memory_space=pltpu.MemorySpace.HBM) for g in gs]

    def body(send_sem, recv_sem):
        x, y, c, _ = _place()
        sibling = (x, y, 1 - c)
        _handshake([sibling])
        copies = []
        for a in range(n):
            hr = gs[a].shape[1] // 2
            theirs = pl.ds(pl.multiple_of((1 - c) * hr, hr), hr)
            copies.append(pltpu.make_async_remote_copy(
                src_ref=src[a].at[:, theirs, :], dst_ref=stage[a], send_sem=send_sem.at[a], recv_sem=recv_sem.at[a],
                device_id=sibling, device_id_type=MESH))
        for cp in copies:
            cp.start()
        for cp in copies:
            cp.wait()

    _sequencer_call(name, collective_id, n, body)
    return [s[...] for s in stage]


def _chip_exchange(name, collective_id, sums):
    n = len(sums)
    src = [_hbm_ref(s) for s in sums]
    got = [jax.empty_ref(jax.ShapeDtypeStruct((3,) + s.shape[1:], s.dtype), memory_space=pltpu.MemorySpace.HBM)
           for s in sums]

    def body(send_sem, recv_sem):
        x, y, c, chips = _place()
        _handshake([(px, py, c) for px, py in chips])
        copies = []
        for a in range(n):
            for j, (px, py) in enumerate(chips):
                copies.append(pltpu.make_async_remote_copy(
                    src_ref=src[a].at[2 * px + py], dst_ref=got[a].at[j],
                    send_sem=send_sem.at[3 * a + j], recv_sem=recv_sem.at[3 * a + j],
                    device_id=(px, py, c), device_id_type=MESH))
        for cp in copies:
            cp.start()
        for cp in copies:
            cp.wait()

    _sequencer_call(name, collective_id, 3 * n, body)
    return [g[...] for g in got]


def _pair_gather(name, collective_id, fulls):
    n = len(fulls)
    full = [_hbm_ref(f) for f in fulls]

    def body(send_sem, recv_sem):
        x, y, c, _ = _place()
        sibling = (x, y, 1 - c)
        _handshake([sibling])
        copies = []
        for a in range(n):
            hr = fulls[a].shape[0] // 2
            mine = full[a].at[pl.ds(pl.multiple_of(c * hr, hr), hr)]
            copies.append(pltpu.make_async_remote_copy(
                src_ref=mine, dst_ref=mine, send_sem=send_sem.at[a], recv_sem=recv_sem.at[a],
                device_id=sibling, device_id_type=MESH))
        for cp in copies:
            cp.start()
        for cp in copies:
            cp.wait()

    _sequencer_call(name, collective_id, n, body)
    return [f[...] for f in full]


ELEMENTWISE_BLOCK_ELEMS = 256 * 1024


def _row_block(rows, cols):
    rb = 8
    while rb * 2 * cols <= ELEMENTWISE_BLOCK_ELEMS and rb * 2 <= rows:
        rb *= 2
    return rb


def _pair_sum(name, place, g, stage):
    _, R, C = g.shape
    hr = R // 2
    rb = _row_block(hr, C)
    nb = hr // rb

    def body(place_ref, g_ref, st_ref, sum_bf, own_f32):
        s = pl.program_id(1)
        tot = g_ref[...] + st_ref[...]
        sum_bf[...] = tot.astype(BF16)

        @pl.when(s == place_ref[0])
        def _():
            own_f32[...] = tot

    return pl.pallas_call(
        body, name=name,
        grid_spec=pltpu.PrefetchScalarGridSpec(
            num_scalar_prefetch=1, grid=(nb, N_SHARD),
            in_specs=[pl.BlockSpec((None, rb, C), lambda i, s, pr: (s, pr[1] * nb + i, 0)),
                      pl.BlockSpec((None, rb, C), lambda i, s, pr: (s, i, 0))],
            out_specs=[pl.BlockSpec((None, rb, C), lambda i, s, pr: (s, i, 0)),
                       pl.BlockSpec((rb, C), lambda i, s, pr: (i, 0))]),
        out_shape=[jax.ShapeDtypeStruct((N_SHARD, hr, C), BF16), jax.ShapeDtypeStruct((hr, C), F32)],
        compiler_params=_cparams("arbitrary", "arbitrary"),
    )(place, g, stage)


def _final_sum(name, place, own, got):
    hr, C = own.shape
    rb = _row_block(hr, C)
    nb = hr // rb

    def body(place_ref, own_ref, got_ref, o_ref):
        o_ref[...] = ((own_ref[...] + got_ref[0].astype(F32)) + got_ref[1].astype(F32)) + got_ref[2].astype(F32)

    return pl.pallas_call(
        body, name=name,
        grid_spec=pltpu.PrefetchScalarGridSpec(
            num_scalar_prefetch=1, grid=(nb,),
            in_specs=[pl.BlockSpec((rb, C), lambda i, pr: (i, 0)), pl.BlockSpec((3, rb, C), lambda i, pr: (0, i, 0))],
            out_specs=pl.BlockSpec((rb, C), lambda i, pr: (pr[1] * nb + i, 0))),
        out_shape=jax.ShapeDtypeStruct((2 * hr, C), F32),
        compiler_params=_cparams("arbitrary"),
    )(place, own, got)


def _adamw_math(w, g, m, v):
    m = ADAM_B1 * m + (1.0 - ADAM_B1) * g
    v = ADAM_B2 * v + (1.0 - ADAM_B2) * (g * g)
    m_hat = m / (1.0 - ADAM_B1 ** ADAM_STEP)
    v_hat = v / (1.0 - ADAM_B2 ** ADAM_STEP)
    delta = -ADAM_LR * (m_hat / (jnp.sqrt(v_hat) + ADAM_EPS) + ADAM_WD * w)
    return delta, m, v


def _adamw(name, w, g, m, v):
    R, Cw = w.shape
    Cg = g.shape[1]
    rb = _row_block(R, Cg)

    def body(w_ref, g_ref, m_ref, v_ref, g_o, d_o, m_o, v_o):
        gv = g_ref[...]
        delta, mn, vn = _adamw_math(w_ref[...], gv, m_ref[...], v_ref[...])
        g_o[...] = gv
        d_o[...] = delta
        m_o[...] = mn
        v_o[...] = vn

    blk = pl.BlockSpec((rb, Cg), lambda i: (i, 0))
    return pl.pallas_call(
        body, name=name, grid=(R // rb,),
        in_specs=[blk] * 4, out_specs=[blk] * 4,
        out_shape=[jax.ShapeDtypeStruct((R, Cw), F32)] * 4,
        compiler_params=_cparams("parallel"),
    )(w, g, m, v)


N_DEV = 8
SMALL_ROWS = 64


def _small_allreduce_adamw(g, w, m, v):
    def body(g_ref, w_ref, m_ref, v_ref, all_ref, gs_o, d_o, m_o, v_o, send_sems, recv_sems, local_sem):
        x, y, c, chips = _place()
        me, sibling = (x, y, c), (x, y, 1 - c)

        def rows(px, py, pc):
            return all_ref.at[pl.ds(pl.multiple_of((4 * px + 2 * py + pc) * SMALL_ROWS, SMALL_ROWS), SMALL_ROWS), :]

        def copy(k, block, to, src=None):
            return pltpu.make_async_remote_copy(
                src_ref=rows(*block) if src is None else src, dst_ref=rows(*block),
                send_sem=send_sems.at[k], recv_sem=recv_sems.at[k], device_id=to, device_id_type=MESH)

        mine = pltpu.make_async_copy(g_ref, rows(*me), local_sem)
        mine.start()
        first = [copy(0, me, sibling, src=g_ref)]
        first += [copy(1 + j, me, (*chip, c), src=g_ref) for j, chip in enumerate(chips)]
        for cp in first:
            cp.start()
        passed = [copy(4 + j, (*chip, c), sibling) for j, chip in enumerate(chips)]
        for j, chip in enumerate(chips):
            copy(1 + j, (*chip, c), me).wait_recv()
            passed[j].start()
        copy(0, sibling, me).wait_recv()
        for j, chip in enumerate(chips):
            copy(4 + j, (*chip, 1 - c), me).wait_recv()
        for cp in first + passed:
            cp.wait_send()
        mine.wait()

        tot = all_ref[0:SMALL_ROWS, :]
        for d in range(1, N_DEV):
            tot = tot + all_ref[d * SMALL_ROWS:(d + 1) * SMALL_ROWS, :]
        delta, mn, vn = _adamw_math(w_ref[...], tot, m_ref[...], v_ref[...])
        gs_o[...] = tot
        d_o[...] = delta
        m_o[...] = mn
        v_o[...] = vn

    vm = pl.BlockSpec(memory_space=pltpu.VMEM)
    shp = jax.ShapeDtypeStruct((SMALL_ROWS, LANES), F32)
    res = pl.pallas_call(
        body, name="small_allreduce_adamw", in_specs=[vm] * 4, out_specs=[vm] * 5,
        out_shape=[jax.ShapeDtypeStruct((N_DEV * SMALL_ROWS, LANES), F32), shp, shp, shp, shp],
        scratch_shapes=[pltpu.SemaphoreType.DMA((7,)), pltpu.SemaphoreType.DMA((7,)), pltpu.SemaphoreType.DMA],
    )(g, w, m, v)
    return res[1:]


SMALL_NAMES = ("g_mix", "b_gate", "b_forget", "qn_swa", "kn_swa", "sink_swa", "rel_bias", "qn_fox", "kn_fox",
               "g_mem", "qn_mem", "kn_mem", "g_mlp")
BIG_NAMES = ("w_in", "w_mem_kv", "w_o_swa", "w_o_fox", "w_o_mem", "w_out", "w_mlp_up", "w_mlp_down")
WEIGHT_NAMES = ("g_mix", "w_in", "b_gate", "b_forget", "qn_swa", "kn_swa", "sink_swa", "rel_bias", "qn_fox", "kn_fox",
                "g_mem", "w_mem_kv", "qn_mem", "kn_mem", "w_o_swa", "w_o_fox", "w_o_mem", "w_out", "g_mlp",
                "w_mlp_up", "w_mlp_down")


def _pack_small(parts, extra=None):
    rows = []
    for n in SMALL_NAMES:
        flat = parts[n].reshape(-1).astype(F32)
        flat = jnp.pad(flat, (0, (-flat.size) % LANES))
        rows.append(flat.reshape(-1, LANES))
    if extra is not None:
        rows.append(jnp.pad(extra.reshape(1, 1), ((0, 0), (0, LANES - 1))))
    packed = jnp.concatenate(rows, axis=0)
    return jnp.pad(packed, ((0, SMALL_ROWS - packed.shape[0]), (0, 0)))


def _unpack_small(packed, shapes):
    out, r = {}, 0
    for n in SMALL_NAMES:
        size = math.prod(shapes[n])
        nr = -(-size // LANES)
        out[n] = packed[r:r + nr].reshape(-1)[:size].reshape(shapes[n])
        r += nr
    return out, packed[r, 0]


W_IN_SEGMENTS = ((C_QA, 0, 512), (C_QF, 768, 512), (C_KF, 1280, 512), (C_VF, 1792, 512), (C_QM, 2312, 512),
                 (C_KA, 512, 128), (C_VA, 640, 128), (C_FL, 2304, FOX_HEADS), (C_GL, 2824, GATE_W))
RELAYOUT_ROWS = 256


def _permute_pieces(src_of_dst):
    blocks = []
    for b in range(len(src_of_dst) // LANES):
        runs, lane = [], 0
        while lane < LANES:
            src = src_of_dst[b * LANES + lane]
            if src is None:
                lane += 1
                continue
            plane, col = src
            end = lane + 1
            while (end < LANES and src_of_dst[b * LANES + end] == (plane, col + end - lane)
                   and (col + end - lane) // LANES == col // LANES):
                end += 1
            runs.append((plane, col // LANES, (lane - col) % LANES, lane, end))
            lane = end
        blocks.append(runs)
    return blocks


def _permuted_block(runs, load, rows):
    lane = _lane((rows, LANES))
    acc = jnp.zeros((rows, LANES), F32)
    for plane, blk, shift, lo, hi in runs:
        x = load(plane, blk).astype(F32)
        if shift:
            x = pltpu.roll(x, shift, 1)
        acc = x if (lo, hi) == (0, LANES) else jnp.where((lane >= lo) & (lane < hi), x, acc)
    return acc


def _w_in_to_segments(g_in):
    src_of_dst = [None] * PROJ_W
    for mine, theirs, width in W_IN_SEGMENTS:
        for k in range(width):
            src_of_dst[mine + k] = ((theirs + k) // IN_SHARD, (theirs + k) % IN_SHARD)
    blocks = _permute_pieces(src_of_dst)
    rb = RELAYOUT_ROWS

    def body(src_ref, out_ref):
        for b, runs in enumerate(blocks):
            blk = _permuted_block(runs, lambda p, c: src_ref[p, :, c * LANES:(c + 1) * LANES], rb)
            out_ref[:, b * LANES:(b + 1) * LANES] = blk.astype(out_ref.dtype)

    return pl.pallas_call(
        body, name="w_in_to_segments", grid=(D_MODEL // rb,),
        in_specs=[pl.BlockSpec((N_SHARD, rb, IN_SHARD_PAD), lambda i: (0, i, 0))],
        out_specs=pl.BlockSpec((rb, PROJ_W), lambda i: (i, 0)),
        out_shape=jax.ShapeDtypeStruct((D_MODEL, PROJ_W), g_in.dtype),
        compiler_params=_cparams("parallel"),
    )(g_in)


def _w_in_from_segments(lo, gl):
    mine_of_theirs = {}
    for mine, theirs, width in W_IN_SEGMENTS:
        for k in range(width):
            mine_of_theirs[theirs + k] = mine + k
    src_of_dst = [None] * (N_SHARD * IN_SHARD_PAD)
    for s in range(N_SHARD):
        for l in range(IN_SHARD):
            j = mine_of_theirs[s * IN_SHARD + l]
            src_of_dst[s * IN_SHARD_PAD + l] = (j // LO_W, j % LO_W)
    blocks = _permute_pieces(src_of_dst)
    per_slot = IN_SHARD_PAD // LANES
    rb = RELAYOUT_ROWS

    def body(lo_ref, gl_ref, out_ref):
        planes = (lo_ref, gl_ref)
        for b, runs in enumerate(blocks):
            blk = _permuted_block(runs, lambda p, c: planes[p][:, c * LANES:(c + 1) * LANES], rb)
            c0 = (b % per_slot) * LANES
            out_ref[b // per_slot, :, c0:c0 + LANES] = blk

    half = pl.BlockSpec((rb, LO_W), lambda i: (i, 0))
    return pl.pallas_call(
        body, name="w_in_from_segments", grid=(D_MODEL // rb,),
        in_specs=[half, half],
        out_specs=pl.BlockSpec((N_SHARD, rb, IN_SHARD_PAD), lambda i: (0, i, 0)),
        out_shape=jax.ShapeDtypeStruct((N_SHARD, D_MODEL, IN_SHARD_PAD), F32),
        compiler_params=_cparams("parallel"),
    )(lo, gl)


def _after(first, then):
    return lax.optimization_barrier((first, then))


class _ReduceGroup:
    def __init__(self, tag, first_collective_id, place):
        self.tag, self.first_id, self.place = tag, first_collective_id, place

    def start(self, local, tie):
        self.names = tuple(local)
        mine, tie = _after([local[n] for n in self.names], tie)
        self.mine = mine
        self.staged = _pair_exchange("pair_exchange_" + self.tag, self.first_id, mine)
        return tie

    def send(self, tie):
        staged, tie = _after(self.staged, tie)
        sums = [_pair_sum("pair_sum_" + n, self.place, g, st) for n, g, st in zip(self.names, self.mine, staged)]
        travel, tie = _after([s[0] for s in sums], tie)
        self.own = [s[1] for s in sums]
        self.got = _chip_exchange("chip_exchange_" + self.tag, self.first_id + 1, travel)
        return tie

    def finish(self, tie):
        got, tie = _after(self.got, tie)
        halves = [_final_sum("final_sum_" + n, self.place, o, r) for n, o, r in zip(self.names, self.own, got)]
        halves, tie = _after(halves, tie)
        summed = _pair_gather("pair_gather_" + self.tag, self.first_id + 2, halves)
        self.summed = dict(zip(self.names, summed))
        return tie


class _GradReducer:
    def __init__(self, place):
        self.early = _ReduceGroup("early", 2, place)
        self.late = _ReduceGroup("late", 5, place)

    @staticmethod
    def _slot_rows(a):
        return a.reshape(N_SHARD, a.shape[0] // N_SHARD, a.shape[1])

    def early_start(self, g, tie):
        return self.early.start({"w_mlp_down": self._slot_rows(g["w_mlp_down"]), "w_mlp_up": g["w_mlp_up"],
                                 "w_out": self._slot_rows(g["w_out"]), "w_mem_kv": self._slot_rows(g["w_mem_kv"]),
                                 "w_o_swa": g["w_o_swa"], "w_o_fox": g["w_o_fox"], "w_o_mem": g["w_o_mem"]}, tie)

    def early_send(self, tie):
        return self.early.send(tie)

    def early_finish(self, tie):
        return self.early.finish(tie)

    def late_start(self, g, tie):
        d_in = _w_in_from_segments(g["wc_lo"], g["wc_gl"])
        return self.late.start({"w_in": d_in}, tie)

    def late_send(self, tie):
        return self.late.send(tie)

    def late_finish(self, tie):
        return self.late.finish(tie)

    @property
    def summed(self):
        return {**self.early.summed, **self.late.summed}


def kernel(x, mem, g_mix, w_in, b_gate, b_forget, qn_swa, kn_swa, sink_swa, rel_bias, qn_fox, kn_fox, g_mem, w_mem_kv, qn_mem, kn_mem, w_o_swa, w_o_fox, w_o_mem, w_out, g_mlp, w_mlp_up, w_mlp_down, loss_target, m_g_mix, m_w_in, m_b_gate, m_b_forget, m_qn_swa, m_kn_swa, m_sink_swa, m_rel_bias, m_qn_fox, m_kn_fox, m_g_mem, m_w_mem_kv, m_qn_mem, m_kn_mem, m_w_o_swa, m_w_o_fox, m_w_o_mem, m_w_out, m_g_mlp, m_w_mlp_up, m_w_mlp_down, v_g_mix, v_w_in, v_b_gate, v_b_forget, v_qn_swa, v_kn_swa, v_sink_swa, v_rel_bias, v_qn_fox, v_kn_fox, v_g_mem, v_w_mem_kv, v_qn_mem, v_kn_mem, v_w_o_swa, v_w_o_fox, v_w_o_mem, v_w_out, v_g_mlp, v_w_mlp_up, v_w_mlp_down):
    given = dict(locals())
    W = {n: given[n] for n in WEIGHT_NAMES}
    M = {n: given["m_" + n] for n in WEIGHT_NAMES}
    V = {n: given["v_" + n] for n in WEIGHT_NAMES}
    pad_in = ((0, 0), (0, IN_SHARD_PAD - IN_SHARD))

    shards = [jnp.pad(w_in[0].astype(BF16), pad_in)] + [W[n][0].astype(BF16) for n in BIG_NAMES[1:]]
    slots = [jnp.broadcast_to(s[None], (N_SHARD,) + s.shape) for s in shards]
    (g_in,) = _all_gather_shards_async("all_gather_w_in", 1, slots[:1])
    small = {n: (W[n] if n == "rel_bias" else W[n].reshape(1, -1)) for n in SMALL_NAMES}
    h = _rmsnorm("rms_mix", x[0], small["g_mix"], min(512, x.shape[1]))
    g_in, late, h, (m_in, v_in) = lax.optimization_barrier((g_in, slots[1:], h, (M["w_in"][0], V["w_in"][0])))
    M["w_in"], V["w_in"] = m_in[None], v_in[None]
    g_kv, g_oa, g_of, g_om, g_out, g_up, g_down = _all_gather_shards_async("all_gather_weights_async", 8, late)

    place = jnp.stack([2 * lax.axis_index("x") + lax.axis_index("y"), lax.axis_index("c")]).astype(jnp.int32)
    reducer = _GradReducer(place)
    loss, grad_x, grads = _local_step(
        x[0], h, mem[0], loss_target[0], small, g_in, g_kv.reshape(D_MODEL, D_MODEL), (g_oa, g_of, g_om),
        g_out.reshape(D_MODEL, D_MODEL), g_up, g_down.reshape(D_FF, D_MODEL), reducer)

    out = {}

    def adamw_of(names, summed):
        for n in names:
            res = _adamw("adamw_" + n, W[n][0], summed[n], M[n][0], V[n][0])
            out[n] = [r.reshape(W[n].shape) for r in res]

    adamw_of(reducer.early.names, reducer.early.summed)
    shapes = {n: W[n].shape for n in SMALL_NAMES}
    packed = _small_allreduce_adamw(_pack_small(grads, loss), _pack_small(W), _pack_small(M), _pack_small(V))
    done_meanwhile = ([out[n] for n in reducer.early.names], packed)
    (early_out, packed), grad_x = reducer.late_finish((done_meanwhile, grad_x))
    for n, res in zip(reducer.early.names, early_out):
        out[n] = res
    adamw_of(reducer.late.names, reducer.late.summed)
    unpacked = [_unpack_small(p, shapes) for p in packed]
    for n in SMALL_NAMES:
        out[n] = [u[0][n] for u in unpacked]
    loss_total = unpacked[0][1]

    return (loss_total, grad_x.reshape(x.shape),
            *[out[n][0] for n in WEIGHT_NAMES], *[out[n][1] for n in WEIGHT_NAMES],
            *[out[n][2] for n in WEIGHT_NAMES], *[out[n][3] for n in WEIGHT_NAMES])
```

```python
import functools
import math

import jax
import jax.numpy as jnp
from jax import lax
from jax.experimental import pallas as pl
from jax.experimental.pallas import tpu as pltpu
from jax.experimental.pallas import tpu_sc as plsc

F32 = jnp.float32
BF16 = jnp.bfloat16

D_MODEL = 1024
N_MEM = 256
SWA_HEADS = 8
SWA_KV_HEADS = 2
SWA_HEAD_DIM = 64
WINDOW = 128
FOX_HEADS = 8
FOX_HEAD_DIM = 64
MEM_HEADS = 4
MEM_HEAD_DIM = 128
D_FF = 4 * D_MODEL
REL_BUCKETS = 32
REL_MAX_DIST = 128
EPS = 1e-6
NEG = -1e30
GATE_W = 3 * D_MODEL
IN_WIDTH = 5896
N_SHARD = 4
IN_SHARD = IN_WIDTH // N_SHARD
IN_SHARD_PAD = 1536

ADAM_LR = 0.001
ADAM_B1 = 0.9
ADAM_B2 = 0.999
ADAM_EPS = 1e-08
ADAM_WD = 0.01
ADAM_STEP = 10

LANES = 128
V7X_VMEM_BYTES = 64 * 1024 * 1024
MIB = 1024 * 1024
VMEM_SMALL, VMEM_MID, VMEM_BIG, VMEM_MAX = 24 * MIB, 40 * MIB, 48 * MIB, 56 * MIB

C_QA, C_QF, C_KF, C_VF, C_QM, C_KA, C_VA, C_FL, C_GL = 0, 512, 1024, 1536, 2048, 2560, 2688, 2816, 3072
LO_W = 3072
PROJ_W = 6144

NN = (((1,), (0,)), ((), ()))
NT = (((1,), (1,)), ((), ()))
TN = (((0,), (0,)), ((), ()))


def _dot(a, b, dims=NN):
    return lax.dot_general(a, b, dims, preferred_element_type=F32)


def _cparams(*sem, vmem=VMEM_SMALL):
    return pltpu.CompilerParams(dimension_semantics=sem, vmem_limit_bytes=vmem)


def _split3(a):
    hi = a.astype(BF16)
    r1 = a - hi.astype(F32)
    mid = r1.astype(BF16)
    lo = (r1 - mid.astype(F32)).astype(BF16)
    return hi, mid, lo


def _group_mean(a, g2):
    hi = a.astype(BF16)
    mid = (a - hi.astype(F32)).astype(BF16)
    return _dot(jnp.concatenate([hi, mid], axis=1), g2)


def _dot3_left(g, a):
    hi, mid, lo = _split3(a)
    return _dot(g, hi) + _dot(g, mid) + _dot(g, lo)


def _group_mean_matrix(d):
    r = jnp.arange(LANES)
    g = jnp.where((r[:, None] // d) == (r[None, :] // d), 1.0 / d, 0.0).astype(BF16)
    return jnp.concatenate([g, g], axis=0)


def _lane(shape):
    return lax.broadcasted_iota(jnp.int32, shape, len(shape) - 1)


def _matmul(name, a, b, *, dims, grid, a_spec, b_spec, acc_shape, outs, epilogue, extra=(), vmem=VMEM_BIG):
    nk = grid[2]
    n_extra = len(extra)

    def body(a_ref, b_ref, *rest):
        extra_refs = rest[:n_extra]
        out_refs = rest[n_extra:n_extra + len(outs)]
        i, j, k = pl.program_id(0), pl.program_id(1), pl.program_id(2)
        part = _dot(a_ref[...].astype(BF16), b_ref[...].astype(BF16), dims)
        if nk == 1:
            epilogue(part, extra_refs, out_refs, (i, j))
            return
        acc_ref = rest[-1]

        @pl.when(k == 0)
        def _():
            acc_ref[...] = part

        @pl.when((k > 0) & (k < nk - 1))
        def _():
            acc_ref[...] += part

        @pl.when(k == nk - 1)
        def _():
            epilogue(acc_ref[...] + part, extra_refs, out_refs, (i, j))

    res = pl.pallas_call(
        body,
        name=name,
        grid=grid,
        in_specs=[a_spec, b_spec] + [s for _, s in extra],
        out_specs=[s for _, s in outs],
        out_shape=[s for s, _ in outs],
        scratch_shapes=[pltpu.VMEM(acc_shape, F32)] if nk > 1 else [],
        compiler_params=_cparams("arbitrary", "arbitrary", "arbitrary", vmem=vmem),
    )(a, b, *[x for x, _ in extra])
    return res


def _epi_store(acc, extra_refs, out_refs, ij):
    out_refs[0][...] = acc.astype(out_refs[0].dtype)


def _rms_rows(x, g):
    r = lax.rsqrt(jnp.mean(x * x, axis=-1, keepdims=True) + EPS)
    return x * r, r


def _rmsnorm_bwd_rows(dh, x, g):
    xhat, r = _rms_rows(x, g)
    dxh = dh * g
    dx = r * (dxh - xhat * jnp.mean(dxh * xhat, axis=-1, keepdims=True))
    return dx, jnp.sum(dh * xhat, axis=0, keepdims=True)


def _rmsnorm(name, x, g, tb):
    T, Dm = x.shape

    def body(x_ref, g_ref, o_ref):
        xhat, _ = _rms_rows(x_ref[...], None)
        o_ref[...] = (xhat * g_ref[...]).astype(o_ref.dtype)

    return pl.pallas_call(
        body, name=name, grid=(T // tb,),
        in_specs=[pl.BlockSpec((tb, Dm), lambda i: (i, 0)), pl.BlockSpec((1, Dm), lambda i: (0, 0))],
        out_specs=pl.BlockSpec((tb, Dm), lambda i: (i, 0)),
        out_shape=jax.ShapeDtypeStruct((T, Dm), BF16),
        compiler_params=_cparams("parallel"),
    )(x, g)


def _head_norm(x, gm, gain):
    ms = _group_mean(x * x, gm)
    r = lax.rsqrt(ms + EPS)
    return x * r * gain, x * r


def _head_norm_bwd(dy, x, gm, gain):
    ms = _group_mean(x * x, gm)
    r = lax.rsqrt(ms + EPS)
    xhat = x * r
    dxh = dy * gain
    dx = r * (dxh - xhat * _group_mean(dxh * xhat, gm))
    return dx, jnp.sum(dy * xhat, axis=0, keepdims=True)


def _log_sigmoid(z):
    return jnp.minimum(z, 0.0) - jnp.log(1.0 + jnp.exp(-jnp.abs(z)))


def _prep_fwd(proj, gains, bfor, tril, gm64, gm128, T, tb):
    nb = T // tb

    def body(qa_ref, qf_ref, kf_ref, vf_ref, qm_ref, ka_ref, va_ref, fl_ref, gains_ref, bfor_ref, tril_ref,
             gm64_ref, gm128_ref,
             qa_o, qf_o, kf_o, vf_o, qm_o, kad_o, vad_o, qaug_o, kaug_o, carry):
        i = pl.program_id(0)
        gm64v = gm64_ref[...]
        gm128v = gm128_ref[...]
        lane = _lane((tb, LANES))

        def norm512(src, dst, row, gm, scale=1.0):
            gain = gains_ref[row:row + 1, :]
            for c in range(4):
                sl = slice(c * LANES, (c + 1) * LANES)
                y, _ = _head_norm(src[:, sl], gm, gain)
                dst[:, sl] = (y * scale).astype(dst.dtype)

        norm512(qa_ref, qa_o, 0, gm64v)
        norm512(qf_ref, qf_o, 2, gm64v, FOX_SCALE)
        norm512(kf_ref, kf_o, 3, gm64v)
        norm512(qm_ref, qm_o, 4, gm128v)
        vf_o[...] = vf_ref[...].astype(vf_o.dtype)

        ka_n, _ = _head_norm(ka_ref[...], gm64v, gains_ref[1:2, :])
        ka_r = pltpu.roll(ka_n, 64, 1)
        va = va_ref[...]
        va_r = pltpu.roll(va, 64, 1)
        lo = lane < 64
        kad_o[0] = jnp.where(lo, ka_n, ka_r).astype(kad_o.dtype)
        kad_o[1] = jnp.where(lo, ka_r, ka_n).astype(kad_o.dtype)
        vad_o[0] = jnp.where(lo, va, va_r).astype(vad_o.dtype)
        vad_o[1] = jnp.where(lo, va_r, va).astype(vad_o.dtype)

        @pl.when(i == 0)
        def _():
            carry[...] = jnp.zeros_like(carry)

        logf = jnp.where(lane < FOX_HEADS, _log_sigmoid(fl_ref[...] + bfor_ref[...]), 0.0)
        c = _dot3_left(tril_ref[...], logf) + carry[0:1, :]
        carry[...] = jnp.broadcast_to(c[tb - 1:tb, :], carry.shape)
        for pair in range(FOX_HEADS // 2):
            qaug = jnp.zeros((tb, LANES), F32)
            kaug = jnp.zeros((tb, LANES), F32)
            for sub in range(2):
                col = jnp.sum(jnp.where(lane == 2 * pair + sub, c, 0.0), axis=1, keepdims=True)
                pieces = [p.astype(F32) for p in _split3(col)]
                base = AUG_STRIDE * sub
                for e in range(3):
                    qaug = jnp.where(lane == base + AUG_C + e, pieces[e], qaug)
                    kaug = jnp.where(lane == base + AUG_NEG_C + e, -pieces[e], kaug)
                qaug = jnp.where((lane >= base + AUG_NEG_C) & (lane < base + AUG_NEG_C + 3), 1.0, qaug)
                ones_k = ((lane >= base + AUG_C) & (lane < base + AUG_C + 3)) | (
                    (lane >= base + AUG_STAT) & (lane < base + AUG_STAT + 3))
                kaug = jnp.where(ones_k, 1.0, kaug)
            sl = slice(pair * LANES, (pair + 1) * LANES)
            qaug_o[:, sl] = qaug.astype(BF16)
            kaug_o[:, sl] = kaug.astype(BF16)

    def seg(width, start):
        return pl.BlockSpec((tb, width), lambda i, s=start // width: (i, s))

    const = lambda shape: pl.BlockSpec(shape, lambda i: tuple(0 for _ in shape))
    rows512 = pl.BlockSpec((tb, 512), lambda i: (i, 0))
    outs = pl.pallas_call(
        body, name="prep_fwd", grid=(nb,),
        in_specs=[seg(512, C_QA), seg(512, C_QF), seg(512, C_KF), seg(512, C_VF), seg(512, C_QM),
                  seg(128, C_KA), seg(128, C_VA), seg(128, C_FL),
                  const((8, LANES)), const((1, LANES)), const((tb, tb)), const((2 * LANES, LANES)), const((2 * LANES, LANES))],
        out_specs=[rows512, rows512, rows512, rows512, rows512,
                   pl.BlockSpec((2, tb, LANES), lambda i: (0, i, 0)), pl.BlockSpec((2, tb, LANES), lambda i: (0, i, 0)),
                   rows512, rows512],
        out_shape=[jax.ShapeDtypeStruct((T, 512), BF16)] * 5
        + [jax.ShapeDtypeStruct((2, T, LANES), BF16)] * 2
        + [jax.ShapeDtypeStruct((T, 512), BF16)] * 2,
        scratch_shapes=[pltpu.VMEM((8, LANES), F32)],
        compiler_params=_cparams("arbitrary", vmem=VMEM_MID),
    )(proj, proj, proj, proj, proj, proj, proj, proj, gains, bfor, tril, gm64, gm128)
    return outs


def _prep_bwd(proj, dqa, dkad, dvad, dqf, dkf, dvf, dqm, dqf_aug, dkf_aug, gains, bfor, triu, gm64, gm128, T, tb):
    nb = T // tb

    def body(qa_ref, qf_ref, kf_ref, qm_ref, ka_ref, fl_ref,
             dqa_ref, dkad_ref, dvad_ref, dqf_ref, dkf_ref, dvf_ref, dqm_ref, dqfa_ref, dkfa_ref,
             gains_ref, bfor_ref, triu_ref, gm64_ref, gm128_ref,
             dlo_o, gacc_o, carry):
        i = pl.program_id(0)
        gm64v = gm64_ref[...]
        gm128v = gm128_ref[...]
        lane = _lane((tb, LANES))

        @pl.when(i == 0)
        def _():
            carry[...] = jnp.zeros_like(carry)
            gacc_o[...] = jnp.zeros_like(gacc_o)

        def norm512_bwd(dsrc, xsrc, col0, row, gm):
            gain = gains_ref[row:row + 1, :]
            gsum = jnp.zeros((1, LANES), F32)
            for c in range(4):
                sl = slice(c * LANES, (c + 1) * LANES)
                dx, dg = _head_norm_bwd(dsrc[:, sl], xsrc[:, sl], gm, gain)
                dlo_o[:, col0 + c * LANES:col0 + (c + 1) * LANES] = dx.astype(dlo_o.dtype)
                gsum = gsum + dg
            gacc_o[row:row + 1, :] += gsum

        norm512_bwd(dqa_ref, qa_ref, C_QA, 0, gm64v)
        norm512_bwd(dqf_ref, qf_ref, C_QF, 2, gm64v)
        norm512_bwd(dkf_ref, kf_ref, C_KF, 3, gm64v)
        norm512_bwd(dqm_ref, qm_ref, C_QM, 4, gm128v)
        dlo_o[:, C_VF:C_VF + 512] = dvf_ref[...].astype(dlo_o.dtype)

        lo = lane < 64

        def fold(ref):
            f0 = ref[0] + pltpu.roll(ref[0], 64, 1)
            f1 = ref[1] + pltpu.roll(ref[1], 64, 1)
            return jnp.where(lo, f0, f1)

        dka, dg = _head_norm_bwd(fold(dkad_ref), ka_ref[...], gm64v, gains_ref[1:2, :])
        gacc_o[1:2, :] += dg
        dlo_o[:, C_KA:C_KA + LANES] = dka.astype(dlo_o.dtype)
        dlo_o[:, C_VA:C_VA + LANES] = fold(dvad_ref).astype(dlo_o.dtype)

        dc = jnp.zeros((tb, LANES), F32)
        for pair in range(FOX_HEADS // 2):
            sl = slice(pair * LANES, (pair + 1) * LANES)
            rows_sum, cols_sum = dqfa_ref[:, sl], dkfa_ref[:, sl]
            for sub in range(2):
                diff = (jnp.where(lane == AUG_STRIDE * sub + AUG_C, rows_sum, 0.0)
                        - jnp.where(lane == AUG_STRIDE * sub + AUG_NEG_C, cols_sum, 0.0))
                dc = jnp.where(lane == 2 * pair + sub, jnp.sum(diff, axis=1, keepdims=True), dc)
        dlogf = _dot3_left(triu_ref[...], dc) + carry[0:1, :]
        carry[...] = jnp.broadcast_to(dlogf[0:1, :], carry.shape)
        z = fl_ref[...] + bfor_ref[...]
        dfl = jnp.where(lane < FOX_HEADS, dlogf / (1.0 + jnp.exp(z)), 0.0)
        gacc_o[5:6, :] += jnp.sum(dfl, axis=0, keepdims=True)
        dlo_o[:, C_FL:C_FL + LANES] = dfl.astype(dlo_o.dtype)
        dlo_o[:, C_FL + LANES:C_FL + 2 * LANES] = jnp.zeros((tb, LANES), dlo_o.dtype)

    rev = lambda i: nb - 1 - i

    def seg(width, start):
        return pl.BlockSpec((tb, width), lambda i, s=start // width: (rev(i), s))

    const = lambda shape: pl.BlockSpec(shape, lambda i: tuple(0 for _ in shape))
    rows512 = pl.BlockSpec((tb, 512), lambda i: (rev(i), 0))
    dup = pl.BlockSpec((2, tb, LANES), lambda i: (0, rev(i), 0))
    return pl.pallas_call(
        body, name="prep_bwd", grid=(nb,),
        in_specs=[seg(512, C_QA), seg(512, C_QF), seg(512, C_KF), seg(512, C_QM), seg(128, C_KA), seg(128, C_FL),
                  rows512, dup, dup, rows512, rows512, rows512, rows512, rows512, rows512,
                  const((8, LANES)), const((1, LANES)), const((tb, tb)), const((2 * LANES, LANES)), const((2 * LANES, LANES))],
        out_specs=[pl.BlockSpec((tb, LO_W), lambda i: (rev(i), 0)), const((8, LANES))],
        out_shape=[jax.ShapeDtypeStruct((T, LO_W), BF16), jax.ShapeDtypeStruct((8, LANES), F32)],
        scratch_shapes=[pltpu.VMEM((8, LANES), F32)],
        compiler_params=_cparams("arbitrary", vmem=VMEM_MID),
    )(proj, proj, proj, proj, proj, proj, dqa, dkad, dvad, dqf, dkf, dvf, dqm, dqf_aug, dkf_aug,
      gains, bfor, triu, gm64, gm128)


FOX_SCALE = FOX_HEAD_DIM ** -0.5
AUG_STRIDE = 16
AUG_C = 0
AUG_NEG_C = 3
AUG_STAT = 6
FOX_TQ, FOX_TK = 1024, 1024
FOX_BWD_TQ, FOX_BWD_TK = 1024, 1024


def _fox_head_mask(sub, rows):
    lane = _lane((rows, 2 * LANES))
    main = (lane >= 64 * sub) & (lane < 64 * sub + 64)
    aug = (lane >= LANES + AUG_STRIDE * sub) & (lane < LANES + AUG_STRIDE * (sub + 1))
    return main | aug


def _fox_fwd(q, qaug, k, kaug, v, T, tq, tk):
    nq, nk = T // tq, T // tk
    rep = tk // LANES
    last_of = lambda i: (i * tq + tq - 1) // tk

    def body(q_ref, qa_ref, k_ref, ka_ref, v_ref, o_ref, qab_ref, m_s, acc_s):
        p_, i, j = pl.program_id(0), pl.program_id(1), pl.program_id(2)
        last = last_of(i)

        @pl.when(j == 0)
        def _():
            m_s[...] = jnp.full(m_s.shape, NEG, F32)
            acc_s[...] = jnp.zeros_like(acc_s)

        def step(diagonal):
            q2 = jnp.concatenate([q_ref[...], qa_ref[...]], axis=1)
            k2 = jnp.concatenate([k_ref[...], ka_ref[...]], axis=1)
            v2 = jnp.concatenate([v_ref[...], ka_ref[...]], axis=1)
            if diagonal:
                causal = (lax.broadcasted_iota(jnp.int32, (tq, tk), 1) + j * tk
                          <= lax.broadcasted_iota(jnp.int32, (tq, tk), 0) + i * tq)
            scores = [_dot(jnp.where(_fox_head_mask(sub, tq), q2, jnp.zeros_like(q2)), k2, NT) for sub in range(2)]
            for sub in range(2):
                s = scores[sub]
                if diagonal:
                    s = jnp.where(causal, s, NEG)
                m_prev = m_s[sub]
                m_next = jnp.maximum(m_prev, jnp.max(s, axis=1, keepdims=True))
                p = jnp.exp(s - jnp.tile(m_next, (1, rep)))
                alpha = jnp.exp(m_prev - m_next)
                m_s[sub] = m_next
                acc_s[sub] = acc_s[sub] * jnp.tile(alpha, (1, 2)) + _dot(p.astype(BF16), v2)

        @pl.when(j == last)
        def _():
            step(True)

        @pl.when(j < last)
        def _():
            step(False)

        @pl.when(j == nk - 1)
        def _():
            lane = _lane((tq, LANES))
            outs = []
            qab = qa_ref[...].astype(F32)
            for sub in range(2):
                acc = acc_s[sub]
                base = AUG_STRIDE * sub
                l = jnp.sum(jnp.where(lane == base + AUG_C, acc[:, LANES:], 0.0), axis=1, keepdims=True)
                outs.append(acc[:, :LANES] / l)
                lse = jnp.max(m_s[sub], axis=1, keepdims=True) + jnp.log(l)
                pieces = _split3(-lse)
                for e in range(3):
                    qab = jnp.where(lane == base + AUG_STAT + e, pieces[e].astype(F32), qab)
            o_ref[...] = jnp.where(lane < 64, outs[0], outs[1]).astype(o_ref.dtype)
            qab_ref[...] = qab.astype(BF16)

    qspec = pl.BlockSpec((tq, LANES), lambda p, i, j: (i, p))
    kspec = pl.BlockSpec((tk, LANES), lambda p, i, j: (jnp.minimum(j, last_of(i)), p))
    return pl.pallas_call(
        body, name="fox_fwd", grid=(4, nq, nk),
        in_specs=[qspec, qspec, kspec, kspec, kspec],
        out_specs=[qspec, qspec],
        out_shape=[jax.ShapeDtypeStruct((T, 512), BF16), jax.ShapeDtypeStruct((T, 512), BF16)],
        scratch_shapes=[pltpu.VMEM((2, tq, LANES), F32), pltpu.VMEM((2, tq, 2 * LANES), F32)],
        compiler_params=_cparams("parallel", "parallel", "arbitrary", vmem=VMEM_BIG),
    )(q, qaug, k, kaug, v)


def _fox_bwd(q, qaug, k, kaug, v, do, doaug, T, tq, tk):
    nq, nk = T // tq, T // tk
    first_of = lambda j: (j * tk) // tq

    def body(q_ref, qa_ref, k_ref, ka_ref, v_ref, do_ref, doa_ref,
             dq_ref, dqa_ref, dk_ref, dka_ref, dv_ref, dk_s, dv_s):
        p_, j, i = pl.program_id(0), pl.program_id(1), pl.program_id(2)
        masked = i * tq < (j + 1) * tk - 1

        @pl.when((j == 0) & (i == 0))
        def _():
            dq_ref[...] = jnp.zeros_like(dq_ref)
            dqa_ref[...] = jnp.zeros_like(dqa_ref)

        @pl.when(i == 0)
        def _():
            dk_s[...] = jnp.zeros_like(dk_s)
            dv_s[...] = jnp.zeros_like(dv_s)

        def step(diagonal):
            q2 = jnp.concatenate([q_ref[...], qa_ref[...]], axis=1)
            k2 = jnp.concatenate([k_ref[...], ka_ref[...]], axis=1)
            v2 = jnp.concatenate([v_ref[...], ka_ref[...]], axis=1)
            do2 = jnp.concatenate([do_ref[...], doa_ref[...]], axis=1)
            if diagonal:
                causal = (lax.broadcasted_iota(jnp.int32, (tq, tk), 1) + j * tk
                          <= lax.broadcasted_iota(jnp.int32, (tq, tk), 0) + i * tq)
            qh = [jnp.where(_fox_head_mask(sub, tq), q2, jnp.zeros_like(q2)) for sub in range(2)]
            doh = [jnp.where(_fox_head_mask(sub, tq), do2, jnp.zeros_like(do2)) for sub in range(2)]
            scores = [_dot(qh[sub], k2, NT) for sub in range(2)]
            dps = [_dot(doh[sub], v2, NT) for sub in range(2)]
            dqs = []
            for sub in range(2):
                s = scores[sub]
                if diagonal:
                    s = jnp.where(causal, s, NEG)
                p = jnp.exp(s)
                dsb = (p * dps[sub]).astype(BF16)
                dv_s[...] += _dot(p.astype(BF16), doh[sub][:, :LANES], TN)
                dk_s[...] += _dot(dsb, qh[sub], TN)
                dqs.append(_dot(dsb, k2))
            dq2 = jnp.where(_fox_head_mask(0, tq), dqs[0], dqs[1])
            qrows = pl.ds(pl.multiple_of(i * tq, tq), tq)
            dq_ref[qrows, :] += dq2[:, :LANES] * FOX_SCALE
            dqa_ref[qrows, :] += dq2[:, LANES:]

        @pl.when((i >= first_of(j)) & masked)
        def _():
            step(True)

        @pl.when((i >= first_of(j)) & jnp.logical_not(masked))
        def _():
            step(False)

        @pl.when(i == nq - 1)
        def _():
            dk_ref[...] = dk_s[:, :LANES]
            dka_ref[...] = dk_s[:, LANES:]
            dv_ref[...] = dv_s[...]

    qspec = pl.BlockSpec((tq, LANES), lambda p, j, i: (jnp.maximum(i, first_of(j)), p))
    kspec = pl.BlockSpec((tk, LANES), lambda p, j, i: (j, p))
    resident = pl.BlockSpec((T, LANES), lambda p, j, i: (0, p))
    return pl.pallas_call(
        body, name="fox_bwd", grid=(4, nk, nq),
        in_specs=[qspec, qspec, kspec, kspec, kspec, qspec, qspec],
        out_specs=[resident, resident, kspec, kspec, kspec],
        out_shape=[jax.ShapeDtypeStruct((T, 512), F32)] * 5,
        scratch_shapes=[pltpu.VMEM((tk, 2 * LANES), F32), pltpu.VMEM((tk, LANES), F32)],
        compiler_params=_cparams("arbitrary", "arbitrary", "arbitrary", vmem=VMEM_BIG),
    )(q, qaug, k, kaug, v, do, doaug)


SWA_SUB = 4
SWA_TB = SWA_SUB * WINDOW


def _t5_bucket_matrix():
    t = jnp.arange(WINDOW)[:, None] + WINDOW
    s = jnp.arange(2 * WINDOW)[None, :]
    max_exact = REL_BUCKETS // 2
    d = jnp.maximum(t - s, 0)
    df = jnp.maximum(d, 1).astype(F32)
    large = max_exact + (jnp.log(df / max_exact) / math.log(REL_MAX_DIST / max_exact)
                         * (REL_BUCKETS - max_exact)).astype(jnp.int32)
    large = jnp.minimum(large, REL_BUCKETS - 1)
    return jnp.where(d < max_exact, d, large).astype(jnp.int32)


def _swa_bias(rel_bias, bucket):
    def body(rel_ref, bucket_ref, o_ref):
        b = bucket_ref[...]
        for h in range(SWA_HEADS):
            acc = jnp.zeros(b.shape, F32)
            for r in range(REL_BUCKETS):
                acc = jnp.where(b == r, rel_ref[r, h], acc)
            o_ref[h] = acc

    return pl.pallas_call(
        body, name="swa_bias",
        in_specs=[pl.BlockSpec(memory_space=pltpu.SMEM), pl.BlockSpec(memory_space=pltpu.VMEM)],
        out_specs=pl.BlockSpec(memory_space=pltpu.VMEM),
        out_shape=jax.ShapeDtypeStruct((SWA_HEADS, WINDOW, 2 * WINDOW), F32),
    )(rel_bias, bucket)


def _swa_bias_bwd(dbias, bucket):
    def body(db_ref, bucket_ref, o_ref):
        b = bucket_ref[...]
        lane = _lane((1, LANES))
        for r in range(REL_BUCKETS):
            row = jnp.zeros((1, LANES), F32)
            for h in range(SWA_HEADS):
                part = jnp.sum(jnp.where(b == r, db_ref[h], 0.0), axis=0, keepdims=True)
                tot = jnp.sum(part, axis=1, keepdims=True)
                row = jnp.where(lane == h, tot, row)
            o_ref[r:r + 1, :] = row

    return pl.pallas_call(
        body, name="swa_bias_bwd",
        in_specs=[pl.BlockSpec(memory_space=pltpu.VMEM), pl.BlockSpec(memory_space=pltpu.VMEM)],
        out_specs=pl.BlockSpec(memory_space=pltpu.VMEM),
        out_shape=jax.ShapeDtypeStruct((REL_BUCKETS, LANES), F32),
    )(dbias, bucket)


SWA_GROUP = SWA_HEADS // SWA_KV_HEADS


def _swa_valid(r, i):
    t = (lax.broadcasted_iota(jnp.int32, (SWA_GROUP * WINDOW, 2 * WINDOW), 0) & (WINDOW - 1)) + WINDOW
    s = lax.broadcasted_iota(jnp.int32, (SWA_GROUP * WINDOW, 2 * WINDOW), 1)
    dist = t - s
    band = (dist >= 0) & (dist < WINDOW)
    if r == 0:
        band = band & ((s >= WINDOW) | (i > 0))
    return band


def _swa_stack(blk):
    lane = _lane((WINDOW, LANES))
    parts = []
    for g in range(SWA_GROUP):
        b = blk[:, LANES * (g // 2):LANES * (g // 2 + 1)]
        parts.append(jnp.where((lane >= 64) if g % 2 else (lane < 64), b, jnp.zeros_like(b)))
    return jnp.concatenate(parts, axis=0)


def _swa_unstack(st):
    lane = _lane((WINDOW, LANES))
    W = WINDOW
    return jnp.concatenate([jnp.where(lane < 64, st[2 * b * W:(2 * b + 1) * W], st[(2 * b + 1) * W:(2 * b + 2) * W])
                            for b in range(2)], axis=1)


def _swa_sink_column(sink_ref, kvh):
    row = lax.broadcasted_iota(jnp.int32, (SWA_GROUP * WINDOW, 1), 0)
    col = jnp.full((SWA_GROUP * WINDOW, 1), sink_ref[SWA_GROUP * kvh + SWA_GROUP - 1], F32)
    for g in range(SWA_GROUP - 2, -1, -1):
        col = jnp.where(row < (g + 1) * WINDOW, sink_ref[SWA_GROUP * kvh + g], col)
    return col


def _swa_specs(T):
    W = WINDOW
    qspec = pl.BlockSpec((SWA_TB, 2 * LANES), lambda h, i: (i, h))
    own = pl.BlockSpec((None, SWA_TB, LANES), lambda h, i: (h, i, 0))
    prev = pl.BlockSpec((None, W, LANES), lambda h, i: (h, jnp.maximum(SWA_SUB * i - 1, 0), 0))
    stat = pl.BlockSpec((SWA_GROUP, SWA_TB, LANES), lambda h, i: (h, i, 0))
    bias = pl.BlockSpec((None, SWA_GROUP * W, 2 * W), lambda h, i: (h, 0, 0))
    return qspec, own, prev, stat, bias


def _swa_fwd(sinks, q, kad, vad, bias, T):
    nb = T // SWA_TB
    scale = SWA_HEAD_DIM ** -0.5
    W = WINDOW

    def body(sink_ref, q_ref, k_ref, kp_ref, v_ref, vp_ref, bias_ref, o_ref, lse_ref):
        kvh, i = pl.program_id(0), pl.program_id(1)
        sink = _swa_sink_column(sink_ref, kvh)
        for r in range(SWA_SUB):
            rs = slice(r * W, (r + 1) * W)
            ps = slice((r - 1) * W, r * W)
            k_own, v_own = k_ref[rs, :], v_ref[rs, :]
            k_prev = kp_ref[...] if r == 0 else k_ref[ps, :]
            v_prev = vp_ref[...] if r == 0 else v_ref[ps, :]
            qs = _swa_stack(q_ref[rs, :])
            s = jnp.concatenate([_dot(qs, k_prev, NT), _dot(qs, k_own, NT)], axis=1) * scale + bias_ref[...]
            s = jnp.where(_swa_valid(r, i), s, NEG)
            m = jnp.maximum(jnp.max(s, axis=1, keepdims=True), sink)
            p = jnp.exp(s - m)
            denom = jnp.sum(p, axis=1, keepdims=True) + jnp.exp(sink - m)
            pn = (p / denom).astype(BF16)
            o_ref[rs, :] = _swa_unstack(_dot(pn[:, :W], v_prev) + _dot(pn[:, W:], v_own)).astype(o_ref.dtype)
            lse = m + jnp.log(denom)
            for g in range(SWA_GROUP):
                lse_ref[g, rs, :] = jnp.broadcast_to(lse[g * W:(g + 1) * W], (W, LANES))

    qspec, own, prev, stat, bspec = _swa_specs(T)
    return pl.pallas_call(
        body, name="swa_fwd", grid=(SWA_KV_HEADS, nb),
        in_specs=[pl.BlockSpec(memory_space=pltpu.SMEM), qspec, own, prev, own, prev, bspec],
        out_specs=[qspec, stat],
        out_shape=[jax.ShapeDtypeStruct((T, 512), BF16), jax.ShapeDtypeStruct((SWA_HEADS, T, LANES), F32)],
        compiler_params=_cparams("parallel", "parallel", vmem=VMEM_MID),
    )(sinks, q, kad, kad, vad, vad, bias.reshape(SWA_KV_HEADS, SWA_GROUP * W, 2 * W))


def _swa_bwd(sinks, q, kad, vad, bias, do, lse, delta, T):
    nb = T // SWA_TB
    scale = SWA_HEAD_DIM ** -0.5
    W = WINDOW

    def body(sink_ref, q_ref, k_ref, kp_ref, v_ref, vp_ref, bias_ref, do_ref, lse_ref, dl_ref,
             dq_ref, dkad_ref, dvad_ref, dbias_ref, dsk_ref):
        kvh, i = pl.program_id(0), pl.program_id(1)
        sink = _swa_sink_column(sink_ref, kvh)

        @pl.when((kvh == 0) & (i == 0))
        def _():
            dkad_ref[...] = jnp.zeros_like(dkad_ref)
            dvad_ref[...] = jnp.zeros_like(dvad_ref)

        @pl.when(i == 0)
        def _():
            dbias_ref[...] = jnp.zeros_like(dbias_ref)
            dsk_ref[...] = jnp.zeros_like(dsk_ref)

        for r in range(SWA_SUB):
            rs = slice(r * W, (r + 1) * W)
            ps = slice((r - 1) * W, r * W)
            k_own, v_own = k_ref[rs, :], v_ref[rs, :]
            k_prev = kp_ref[...] if r == 0 else k_ref[ps, :]
            v_prev = vp_ref[...] if r == 0 else v_ref[ps, :]
            qs = _swa_stack(q_ref[rs, :])
            dos = _swa_stack(do_ref[rs, :])
            lse_b = jnp.concatenate([lse_ref[g, rs, :] for g in range(SWA_GROUP)], axis=0)
            dl_b = jnp.concatenate([dl_ref[g, rs, :] for g in range(SWA_GROUP)], axis=0)
            s = jnp.concatenate([_dot(qs, k_prev, NT), _dot(qs, k_own, NT)], axis=1) * scale + bias_ref[...]
            s = jnp.where(_swa_valid(r, i), s, NEG)
            p = jnp.exp(s - jnp.tile(lse_b, (1, 2)))
            dp = jnp.concatenate([_dot(dos, v_prev, NT), _dot(dos, v_own, NT)], axis=1)
            ds = p * (dp - jnp.tile(dl_b, (1, 2)))
            sink_term = jnp.exp(sink - lse_b) * dl_b
            for g in range(SWA_GROUP):
                dbias_ref[g] += ds[g * W:(g + 1) * W]
                dsk_ref[g:g + 1, :] += jnp.sum(sink_term[g * W:(g + 1) * W], axis=0, keepdims=True)
            dsb = ds.astype(BF16)
            pb = p.astype(BF16)
            dq_ref[rs, :] = _swa_unstack((_dot(dsb[:, :W], k_prev) + _dot(dsb[:, W:], k_own)) * scale)
            own_row = pl.multiple_of(i * SWA_TB + r * W, W)
            dkad_ref[kvh, pl.ds(own_row, W), :] += _dot(dsb[:, W:], qs, TN) * scale
            dvad_ref[kvh, pl.ds(own_row, W), :] += _dot(pb[:, W:], dos, TN)
            dk_prev = _dot(dsb[:, :W], qs, TN) * scale
            dv_prev = _dot(pb[:, :W], dos, TN)
            if r == 0:
                @pl.when(i > 0)
                def _():
                    prev_row = pl.multiple_of(i * SWA_TB - W, W)
                    dkad_ref[kvh, pl.ds(prev_row, W), :] += dk_prev
                    dvad_ref[kvh, pl.ds(prev_row, W), :] += dv_prev
            else:
                prev_row = pl.multiple_of(i * SWA_TB + (r - 1) * W, W)
                dkad_ref[kvh, pl.ds(prev_row, W), :] += dk_prev
                dvad_ref[kvh, pl.ds(prev_row, W), :] += dv_prev

    qspec, own, prev, stat, bspec = _swa_specs(T)
    full = pl.BlockSpec((SWA_KV_HEADS, T, LANES), lambda h, i: (0, 0, 0))
    return pl.pallas_call(
        body, name="swa_bwd", grid=(SWA_KV_HEADS, nb),
        in_specs=[pl.BlockSpec(memory_space=pltpu.SMEM), qspec, own, prev, own, prev, bspec, qspec, stat, stat],
        out_specs=[qspec, full, full, pl.BlockSpec((SWA_GROUP, W, 2 * W), lambda h, i: (h, 0, 0)),
                   pl.BlockSpec((None, 8, LANES), lambda h, i: (h, 0, 0))],
        out_shape=[jax.ShapeDtypeStruct((T, 512), F32), jax.ShapeDtypeStruct((SWA_KV_HEADS, T, LANES), F32),
                   jax.ShapeDtypeStruct((SWA_KV_HEADS, T, LANES), F32), jax.ShapeDtypeStruct((SWA_HEADS, W, 2 * W), F32),
                   jax.ShapeDtypeStruct((SWA_KV_HEADS, 8, LANES), F32)],
        compiler_params=_cparams("arbitrary", "arbitrary", vmem=VMEM_MID),
    )(sinks, q, kad, kad, vad, vad, bias.reshape(SWA_KV_HEADS, SWA_GROUP * W, 2 * W), do, lse, delta)


def _mem_fwd(q, mk, mv, T, tq):
    scale = MEM_HEAD_DIM ** -0.5

    def body(q_ref, k_ref, v_ref, o_ref, lse_ref):
        s = _dot(q_ref[...], k_ref[...], NT) * scale
        m = jnp.max(s, axis=1, keepdims=True)
        p = jnp.exp(s - m)
        l = jnp.sum(p, axis=1, keepdims=True)
        o_ref[...] = _dot((p / l).astype(BF16), v_ref[...]).astype(o_ref.dtype)
        lse_ref[...] = jnp.broadcast_to(m + jnp.log(l), (tq, LANES))

    qspec = pl.BlockSpec((tq, LANES), lambda h, i: (i, h))
    kspec = pl.BlockSpec((N_MEM, LANES), lambda h, i: (0, h))
    return pl.pallas_call(
        body, name="mem_fwd", grid=(MEM_HEADS, T // tq),
        in_specs=[qspec, kspec, kspec],
        out_specs=[qspec, pl.BlockSpec((None, tq, LANES), lambda h, i: (h, i, 0))],
        out_shape=[jax.ShapeDtypeStruct((T, 512), BF16), jax.ShapeDtypeStruct((MEM_HEADS, T, LANES), F32)],
        compiler_params=_cparams("parallel", "parallel"),
    )(q, mk, mv)


def _mem_bwd(q, mk, mv, do, lse, delta, T, tq):
    scale = MEM_HEAD_DIM ** -0.5
    rep = N_MEM // LANES

    def body(q_ref, k_ref, v_ref, do_ref, lse_ref, dl_ref, dq_ref, dk_ref, dv_ref):
        i = pl.program_id(1)

        @pl.when(i == 0)
        def _():
            dk_ref[...] = jnp.zeros_like(dk_ref)
            dv_ref[...] = jnp.zeros_like(dv_ref)

        qv, dov = q_ref[...], do_ref[...]
        s = _dot(qv, k_ref[...], NT) * scale
        p = jnp.exp(s - jnp.tile(lse_ref[...], (1, rep)))
        dp = _dot(dov, v_ref[...], NT)
        ds = p * (dp - jnp.tile(dl_ref[...], (1, rep)))
        dsb = ds.astype(BF16)
        dq_ref[...] = _dot(dsb, k_ref[...]) * scale
        dk_ref[...] += _dot(dsb, qv, TN) * scale
        dv_ref[...] += _dot(p.astype(BF16), dov, TN)

    qspec = pl.BlockSpec((tq, LANES), lambda h, i: (i, h))
    kspec = pl.BlockSpec((N_MEM, LANES), lambda h, i: (0, h))
    stat = pl.BlockSpec((None, tq, LANES), lambda h, i: (h, i, 0))
    return pl.pallas_call(
        body, name="mem_bwd", grid=(MEM_HEADS, T // tq),
        in_specs=[qspec, kspec, kspec, qspec, stat, stat],
        out_specs=[qspec, kspec, kspec],
        out_shape=[jax.ShapeDtypeStruct((T, 512), F32), jax.ShapeDtypeStruct((N_MEM, 512), F32),
                   jax.ShapeDtypeStruct((N_MEM, 512), F32)],
        compiler_params=_cparams("arbitrary", "arbitrary"),
    )(q, mk, mv, do, lse, delta)


def _mem_prep_fwd(mem, g_mem, w_kv, kn_gain, gm128):
    def body(mem_ref, g_ref, w_ref, kn_ref, gm_ref, memn_o, kv_o, mk_o, mv_o):
        xhat, _ = _rms_rows(mem_ref[...], None)
        memn = (xhat * g_ref[...]).astype(BF16)
        memn_o[...] = memn
        kv = _dot(memn, w_ref[...])
        kv_o[...] = kv
        gm = gm_ref[...]
        for c in range(4):
            sl = slice(c * LANES, (c + 1) * LANES)
            y, _ = _head_norm(kv[:, sl], gm, kn_ref[...])
            mk_o[:, sl] = y.astype(BF16)
        mv_o[...] = kv[:, 512:].astype(BF16)

    vm = pl.BlockSpec(memory_space=pltpu.VMEM)
    return pl.pallas_call(
        body, name="mem_prep_fwd", in_specs=[vm] * 5, out_specs=[vm] * 4,
        out_shape=[jax.ShapeDtypeStruct((N_MEM, D_MODEL), BF16), jax.ShapeDtypeStruct((N_MEM, D_MODEL), F32),
                   jax.ShapeDtypeStruct((N_MEM, 512), BF16), jax.ShapeDtypeStruct((N_MEM, 512), BF16)],
        compiler_params=pltpu.CompilerParams(vmem_limit_bytes=VMEM_MID),
    )(mem, g_mem, w_kv, kn_gain, gm128)


def _mem_prep_bwd(mem, g_mem, memn, kv, w_kv, kn_gain, gm128, dmk, dmv):
    def body(mem_ref, g_ref, memn_ref, kv_ref, w_ref, kn_ref, gm_ref, dmk_ref, dmv_ref, dw_o, dg_o, dkn_o, dkv_s):
        gm = gm_ref[...]
        dkn = jnp.zeros((1, LANES), F32)
        for c in range(4):
            sl = slice(c * LANES, (c + 1) * LANES)
            dx, dg = _head_norm_bwd(dmk_ref[:, sl], kv_ref[:, sl], gm, kn_ref[...])
            dkv_s[:, sl] = dx.astype(BF16)
            dkn = dkn + dg
        dkn_o[...] = dkn
        dkv_s[:, 512:] = dmv_ref[...].astype(BF16)
        dkv = dkv_s[...]
        dw_o[...] = _dot(memn_ref[...], dkv, TN)
        dmemn = _dot(dkv, w_ref[...], NT)
        xhat, _ = _rms_rows(mem_ref[...], None)
        dg_o[...] = jnp.sum(dmemn * xhat, axis=0, keepdims=True)

    vm = pl.BlockSpec(memory_space=pltpu.VMEM)
    return pl.pallas_call(
        body, name="mem_prep_bwd", in_specs=[vm] * 9, out_specs=[vm] * 3,
        out_shape=[jax.ShapeDtypeStruct((D_MODEL, D_MODEL), F32), jax.ShapeDtypeStruct((1, D_MODEL), F32),
                   jax.ShapeDtypeStruct((1, LANES), F32)],
        scratch_shapes=[pltpu.VMEM((N_MEM, D_MODEL), BF16)],
        compiler_params=pltpu.CompilerParams(vmem_limit_bytes=VMEM_MID),
    )(mem, g_mem, memn, kv, w_kv, kn_gain, gm128, dmk, dmv)


SLOT_O = D_MODEL // N_SHARD


def _merge_fwd(proj, b_gate, o3, w3, T, tb):
    def body(gl_ref, bg_ref, oa_ref, of_ref, om_ref, wa_ref, wf_ref, wm_ref, out_ref):
        o_refs = (oa_ref, of_ref, om_ref)
        w_refs = (wa_ref, wf_ref, wm_ref)
        for n in range(N_SHARD):
            acc = jnp.zeros((tb, SLOT_O), F32)
            for b in range(3):
                c0 = b * D_MODEL + n * SLOT_O
                g = jax.nn.sigmoid(gl_ref[:, c0:c0 + SLOT_O] + bg_ref[:, c0:c0 + SLOT_O])
                acc = acc + g * _dot(o_refs[b][...], w_refs[b][n])
            out_ref[:, n * SLOT_O:(n + 1) * SLOT_O] = acc.astype(out_ref.dtype)

    rows = pl.BlockSpec((tb, 512), lambda i: (i, 0))
    wspec = pl.BlockSpec((N_SHARD, 512, SLOT_O), lambda i: (0, 0, 0))
    return pl.pallas_call(
        body, name="merge_fwd", grid=(T // tb,),
        in_specs=[pl.BlockSpec((tb, GATE_W), lambda i: (i, 1)), pl.BlockSpec((1, GATE_W), lambda i: (0, 0)),
                  rows, rows, rows, wspec, wspec, wspec],
        out_specs=pl.BlockSpec((tb, D_MODEL), lambda i: (i, 0)),
        out_shape=jax.ShapeDtypeStruct((T, D_MODEL), BF16),
        compiler_params=_cparams("parallel", vmem=VMEM_BIG),
    )(proj, b_gate, *o3, *w3)


def _merge_bwd(proj, b_gate, o3, w3, dmerged, T, tb):
    heads = (SWA_HEADS, FOX_HEADS, MEM_HEADS)

    def body(gl_ref, bg_ref, oa_ref, of_ref, om_ref, wa_ref, wf_ref, wm_ref, dm_ref,
             dgl_o, doa_o, dof_o, dom_o, dla_o, dlf_o, dlm_o, dwa_o, dwf_o, dwm_o, dbg_o):
        i = pl.program_id(0)
        o_refs = (oa_ref, of_ref, om_ref)
        w_refs = (wa_ref, wf_ref, wm_ref)
        do_refs = (doa_o, dof_o, dom_o)
        dl_refs = (dla_o, dlf_o, dlm_o)
        dw_refs = (dwa_o, dwf_o, dwm_o)

        @pl.when(i == 0)
        def _():
            for r in dw_refs:
                r[...] = jnp.zeros_like(r)
            dbg_o[...] = jnp.zeros_like(dbg_o)

        lane = _lane((tb, LANES))
        for b in range(3):
            ob = o_refs[b][...]
            do = jnp.zeros((tb, 512), F32)
            for n in range(N_SHARD):
                c0 = b * D_MODEL + n * SLOT_O
                g = jax.nn.sigmoid(gl_ref[:, c0:c0 + SLOT_O] + bg_ref[:, c0:c0 + SLOT_O])
                dm = dm_ref[:, n * SLOT_O:(n + 1) * SLOT_O]
                y = _dot(ob, w_refs[b][n])
                dgl = dm * y * g * (1.0 - g)
                dgl_o[:, c0:c0 + SLOT_O] = dgl.astype(dgl_o.dtype)
                dbg_o[:, c0:c0 + SLOT_O] += jnp.sum(dgl, axis=0, keepdims=True)
                dy = (dm * g).astype(BF16)
                do = do + _dot(dy, w_refs[b][n], NT)
                dw_refs[b][n] += _dot(ob, dy, TN)
            do_refs[b][...] = do.astype(BF16)
            prod = do * ob.astype(F32)
            for c in range(4):
                blk = prod[:, c * LANES:(c + 1) * LANES]
                if heads[b] == 8:
                    lo = jnp.sum(jnp.where(lane < 64, blk, 0.0), axis=1, keepdims=True)
                    hi = jnp.sum(jnp.where(lane >= 64, blk, 0.0), axis=1, keepdims=True)
                    if b == 1:
                        aug = jnp.zeros((tb, LANES), F32)
                        for sub, dl in enumerate((lo, hi)):
                            for e, piece in enumerate(_split3(-dl)):
                                aug = jnp.where(lane == AUG_STRIDE * sub + AUG_C + e, piece.astype(F32), aug)
                        dl_refs[b][:, c * LANES:(c + 1) * LANES] = aug.astype(BF16)
                    else:
                        dl_refs[b][2 * c] = jnp.broadcast_to(lo, (tb, LANES))
                        dl_refs[b][2 * c + 1] = jnp.broadcast_to(hi, (tb, LANES))
                else:
                    dl_refs[b][c] = jnp.broadcast_to(jnp.sum(blk, axis=1, keepdims=True), (tb, LANES))

    rows = pl.BlockSpec((tb, 512), lambda i: (i, 0))
    wspec = pl.BlockSpec((N_SHARD, 512, SLOT_O), lambda i: (0, 0, 0))
    stat = lambda h: pl.BlockSpec((h, tb, LANES), lambda i: (0, i, 0))
    return pl.pallas_call(
        body, name="merge_bwd", grid=(T // tb,),
        in_specs=[pl.BlockSpec((tb, GATE_W), lambda i: (i, 1)), pl.BlockSpec((1, GATE_W), lambda i: (0, 0)),
                  rows, rows, rows, wspec, wspec, wspec, pl.BlockSpec((tb, D_MODEL), lambda i: (i, 0))],
        out_specs=[pl.BlockSpec((tb, GATE_W), lambda i: (i, 0)), rows, rows, rows,
                   stat(8), rows, stat(4), wspec, wspec, wspec, pl.BlockSpec((1, GATE_W), lambda i: (0, 0))],
        out_shape=[jax.ShapeDtypeStruct((T, GATE_W), BF16)] + [jax.ShapeDtypeStruct((T, 512), BF16)] * 3
        + [jax.ShapeDtypeStruct((8, T, LANES), F32), jax.ShapeDtypeStruct((T, 512), BF16),
           jax.ShapeDtypeStruct((4, T, LANES), F32)]
        + [jax.ShapeDtypeStruct((N_SHARD, 512, SLOT_O), F32)] * 3 + [jax.ShapeDtypeStruct((1, GATE_W), F32)],
        compiler_params=_cparams("arbitrary", vmem=VMEM_BIG),
    )(proj, b_gate, *o3, *w3, dmerged)


def _local_step(x, h, mem, tgt, small, g_in, w_kv, w_o3, w_out, w_up, w_down, reducer):
    T = x.shape[0]
    tm = min(512, T)
    tile2 = lambda v: jnp.tile(v.reshape(1, -1), (1, LANES // v.size))
    gains = jnp.concatenate([tile2(small["qn_swa"]), tile2(small["kn_swa"]), tile2(small["qn_fox"]),
                             tile2(small["kn_fox"]), tile2(small["qn_mem"]), jnp.zeros((3, LANES), F32)], axis=0)
    kn_mem = small["kn_mem"].reshape(1, LANES)
    bfor = jnp.pad(small["b_forget"].reshape(1, -1), ((0, 0), (0, LANES - FOX_HEADS)))
    gm64 = _group_mean_matrix(64)
    gm128 = _group_mean_matrix(128)
    tb_prep = min(256, T)
    ones = jnp.ones((tb_prep, tb_prep), F32)
    tril = jnp.tril(ones).astype(BF16)
    triu = jnp.triu(ones).astype(BF16)
    bucket = _t5_bucket_matrix()
    g_mix, g_mlp, g_mem = small["g_mix"], small["g_mlp"], small["g_mem"]
    b_gate = small["b_gate"]
    sinks = small["sink_swa"].reshape(-1)

    tl = min(1024, T)
    sq = pl.BlockSpec((tl, D_MODEL), lambda i, j, k: (i, j))
    wc = _w_in_to_segments(g_in)
    (proj,) = _matmul(
        "mm_proj", h, wc, dims=NN, grid=(T // tl, PROJ_W // D_MODEL, 1),
        a_spec=pl.BlockSpec((tl, D_MODEL), lambda i, j, k: (i, 0)),
        b_spec=pl.BlockSpec((D_MODEL, D_MODEL), lambda i, j, k: (0, j)),
        acc_shape=(tl, D_MODEL),
        outs=[(jax.ShapeDtypeStruct((T, PROJ_W), F32), sq)],
        epilogue=_epi_store)
    qa, qf, kf, vf, qm, kad, vad, qf_aug, kf_aug = _prep_fwd(proj, gains, bfor, tril, gm64, gm128, T, tb_prep)
    bias = _swa_bias(small["rel_bias"], bucket)
    o_swa, lse_swa = _swa_fwd(sinks, qa, kad, vad, bias, T)
    o_fox, qf_aug_bwd = _fox_fwd(qf, qf_aug, kf, kf_aug, vf, T, min(FOX_TQ, T), min(FOX_TK, T))
    memn, kv, mk, mv = _mem_prep_fwd(mem, g_mem, w_kv, kn_mem, gm128)
    o_mem, lse_mem = _mem_fwd(qm, mk, mv, T, tm)
    o3 = (o_swa, o_fox, o_mem)
    merged = _merge_fwd(proj, b_gate, o3, w_o3, T, min(512, T))

    def epi_residual(acc, extra_refs, out_refs, ij):
        out_refs[0][...] = extra_refs[0][...] + acc

    row_full = pl.BlockSpec((tm, D_MODEL), lambda i, j, k: (i, 0))
    row_big = pl.BlockSpec((tl, D_MODEL), lambda i, j, k: (i, 0))
    whole = pl.BlockSpec((D_MODEL, D_MODEL), lambda i, j, k: (0, 0))
    (x2,) = _matmul(
        "mm_out", merged, w_out, dims=NN, grid=(T // tl, 1, 1),
        a_spec=row_big, b_spec=whole,
        acc_shape=(tl, D_MODEL), extra=[(x, row_big)],
        outs=[(jax.ShapeDtypeStruct((T, D_MODEL), F32), row_big)], epilogue=epi_residual)
    hm = _rmsnorm("rms_mlp", x2, g_mlp, tm)

    def epi_relu2(acc, extra_refs, out_refs, ij):
        out_refs[0][...] = acc
        r = jnp.maximum(acc, 0.0)
        out_refs[1][...] = (r * r).astype(BF16)

    up, u = _matmul(
        "mm_up", hm, w_up, dims=NN, grid=(T // tl, N_SHARD, 1),
        a_spec=row_big, b_spec=pl.BlockSpec((None, D_MODEL, D_MODEL), lambda i, j, k: (j, 0, 0)),
        acc_shape=(tl, D_MODEL),
        outs=[(jax.ShapeDtypeStruct((T, D_FF), F32), sq), (jax.ShapeDtypeStruct((T, D_FF), BF16), sq)],
        epilogue=epi_relu2)

    def epi_loss(acc, extra_refs, out_refs, ij):
        y = extra_refs[0][...] + acc
        err = y - extra_refs[1][...]
        dyv = err * (1.0 / D_MODEL)
        out_refs[0][...] = dyv
        out_refs[2][...] = dyv.astype(BF16)
        sq = jnp.sum(jnp.sum(err * err, axis=1, keepdims=True), axis=0, keepdims=True)

        @pl.when(ij[0] == 0)
        def _():
            out_refs[1][...] = jnp.zeros_like(out_refs[1])

        out_refs[1][...] += jnp.broadcast_to(sq, out_refs[1].shape)

    kblk = pl.BlockSpec((tl, D_MODEL), lambda i, j, k: (i, k))
    dy, loss_acc, dy_bf = _matmul(
        "mm_down", u, w_down, dims=NN, grid=(T // tl, 1, N_SHARD),
        a_spec=kblk, b_spec=pl.BlockSpec((D_MODEL, D_MODEL), lambda i, j, k: (k, 0)),
        acc_shape=(tl, D_MODEL), extra=[(x2, row_big), (tgt, row_big)],
        outs=[(jax.ShapeDtypeStruct((T, D_MODEL), F32), row_big),
              (jax.ShapeDtypeStruct((8, LANES), F32), pl.BlockSpec((8, LANES), lambda i, j, k: (0, 0))),
              (jax.ShapeDtypeStruct((T, D_MODEL), BF16), row_big)],
        epilogue=epi_loss)
    loss = loss_acc[0, 0] * (0.5 / D_MODEL)

    def epi_dup(acc, extra_refs, out_refs, ij):
        out_refs[0][...] = (acc * (2.0 * jnp.maximum(extra_refs[0][...], 0.0))).astype(BF16)

    (dup,) = _matmul(
        "mm_dup", dy_bf, w_down, dims=NT, grid=(T // tl, N_SHARD, 1),
        a_spec=row_big, b_spec=pl.BlockSpec((D_MODEL, D_MODEL), lambda i, j, k: (j, 0)),
        acc_shape=(tl, D_MODEL), extra=[(up, sq)],
        outs=[(jax.ShapeDtypeStruct((T, D_FF), BF16), sq)], epilogue=epi_dup)

    nkt = T // tl
    t_rows = pl.BlockSpec((tl, D_MODEL), lambda i, j, k: (k, i))
    t_cols = pl.BlockSpec((tl, D_MODEL), lambda i, j, k: (k, j))
    (d_w_down,) = _matmul(
        "mm_dw_down", u, dy_bf, dims=TN, grid=(N_SHARD, 1, nkt),
        a_spec=t_rows, b_spec=t_cols, acc_shape=(D_MODEL, D_MODEL),
        outs=[(jax.ShapeDtypeStruct((D_FF, D_MODEL), F32), pl.BlockSpec((D_MODEL, D_MODEL), lambda i, j, k: (i, 0)))],
        epilogue=_epi_store)
    (d_w_up,) = _matmul(
        "mm_dw_up", hm, dup, dims=TN, grid=(1, N_SHARD, nkt),
        a_spec=t_rows, b_spec=t_cols, acc_shape=(D_MODEL, D_MODEL),
        outs=[(jax.ShapeDtypeStruct((N_SHARD, D_MODEL, D_MODEL), F32),
               pl.BlockSpec((None, D_MODEL, D_MODEL), lambda i, j, k: (j, 0, 0)))],
        epilogue=_epi_store)

    def epi_rms_bwd(acc, extra_refs, out_refs, ij):
        dx, dg = _rmsnorm_bwd_rows(acc, extra_refs[0][...], extra_refs[1][...])
        out_refs[0][...] = dx + extra_refs[2][...]

        @pl.when(ij[0] == 0)
        def _():
            out_refs[1][...] = jnp.zeros_like(out_refs[1])

        out_refs[1][...] += dg

    gain_spec = pl.BlockSpec((1, D_MODEL), lambda i, j, k: (0, 0))
    dx2, d_g_mlp = _matmul(
        "mm_dhm", dup, w_up, dims=NT, grid=(T // tl, 1, N_SHARD),
        a_spec=kblk, b_spec=pl.BlockSpec((None, D_MODEL, D_MODEL), lambda i, j, k: (k, 0, 0)),
        acc_shape=(tl, D_MODEL), extra=[(x2, row_big), (g_mlp, gain_spec), (dy, row_big)],
        outs=[(jax.ShapeDtypeStruct((T, D_MODEL), F32), row_big), (jax.ShapeDtypeStruct((1, D_MODEL), F32), gain_spec)],
        epilogue=epi_rms_bwd)

    (dmerged,) = _matmul(
        "mm_dmerged", dx2, w_out, dims=NT, grid=(T // tl, 1, 1),
        a_spec=row_big, b_spec=whole,
        acc_shape=(tl, D_MODEL), outs=[(jax.ShapeDtypeStruct((T, D_MODEL), F32), row_big)], epilogue=_epi_store)
    (d_w_out,) = _matmul(
        "mm_dw_out", merged, dx2, dims=TN, grid=(1, 1, nkt),
        a_spec=t_rows, b_spec=t_cols, acc_shape=(D_MODEL, D_MODEL),
        outs=[(jax.ShapeDtypeStruct((D_MODEL, D_MODEL), F32), whole)],
        epilogue=_epi_store)
    (dgl, do_swa, do_fox, do_mem, dl_swa, do_fox_aug, dl_mem, d_wo_swa, d_wo_fox, d_wo_mem, d_b_gate) = _merge_bwd(
        proj, b_gate, o3, w_o3, dmerged, T, min(512, T))

    dqm, dmk, dmv = _mem_bwd(qm, mk, mv, do_mem, lse_mem, dl_mem, T, tm)
    d_w_kv, d_g_mem, d_kn_mem = _mem_prep_bwd(mem, g_mem, memn, kv, w_kv, kn_mem, gm128, dmk, dmv)
    do_swa = reducer.early_start({"w_mlp_down": d_w_down, "w_mlp_up": d_w_up, "w_out": d_w_out, "w_mem_kv": d_w_kv,
                                  "w_o_swa": d_wo_swa, "w_o_fox": d_wo_fox, "w_o_mem": d_wo_mem}, do_swa)
    dqa, dkad, dvad, dbias, dsk = _swa_bwd(sinks, qa, kad, vad, bias, do_swa, lse_swa, dl_swa, T)
    dqa, do_fox = reducer.early_send((dqa, do_fox))
    dqf, dqf_aug, dkf, dkf_aug, dvf = _fox_bwd(qf, qf_aug_bwd, kf, kf_aug, vf, do_fox, do_fox_aug, T,
                                               min(FOX_BWD_TQ, T), min(FOX_BWD_TK, T))
    dvf = reducer.early_finish(dvf)
    d_rel = _swa_bias_bwd(dbias, bucket)
    dlo, gacc = _prep_bwd(proj, dqa, dkad, dvad, dqf, dkf, dvf, dqm, dqf_aug, dkf_aug, gains, bfor, triu, gm64, gm128,
                          T, tb_prep)

    def dwc_half(name, dpart):
        (res,) = _matmul(
            name, h, dpart, dims=TN, grid=(1, LO_W // D_MODEL, nkt),
            a_spec=t_rows, b_spec=t_cols, acc_shape=(D_MODEL, D_MODEL),
            outs=[(jax.ShapeDtypeStruct((D_MODEL, LO_W), F32), pl.BlockSpec((D_MODEL, D_MODEL), lambda i, j, k: (0, j)))],
            epilogue=_epi_store)
        return res

    d_wc_lo = dwc_half("mm_dwc_lo", dlo)
    d_wc_gl = dwc_half("mm_dwc_gl", dgl)
    dlo = reducer.late_start({"wc_lo": d_wc_lo, "wc_gl": d_wc_gl}, dlo)
    (dh_lo,) = _matmul(
        "mm_dh_lo", dlo, wc, dims=NT, grid=(T // tl, 1, LO_W // D_MODEL),
        a_spec=kblk, b_spec=pl.BlockSpec((D_MODEL, D_MODEL), lambda i, j, k: (0, k)),
        acc_shape=(tl, D_MODEL), outs=[(jax.ShapeDtypeStruct((T, D_MODEL), F32), row_big)], epilogue=_epi_store)
    dh_lo = reducer.late_send(dh_lo)

    def epi_dx(acc, extra_refs, out_refs, ij):
        dhh = acc + extra_refs[3][...]
        dx, dg = _rmsnorm_bwd_rows(dhh, extra_refs[0][...], extra_refs[1][...])
        out_refs[0][...] = dx + extra_refs[2][...]

        @pl.when(ij[0] == 0)
        def _():
            out_refs[1][...] = jnp.zeros_like(out_refs[1])

        out_refs[1][...] += dg

    grad_x, d_g_mix = _matmul(
        "mm_dh_gl", dgl, wc, dims=NT, grid=(T // tl, 1, GATE_W // D_MODEL),
        a_spec=kblk, b_spec=pl.BlockSpec((D_MODEL, D_MODEL), lambda i, j, k: (0, k + LO_W // D_MODEL)),
        acc_shape=(tl, D_MODEL), extra=[(x, row_big), (g_mix, gain_spec), (dx2, row_big), (dh_lo, row_big)],
        outs=[(jax.ShapeDtypeStruct((T, D_MODEL), F32), row_big), (jax.ShapeDtypeStruct((1, D_MODEL), F32), gain_spec)],
        epilogue=epi_dx, vmem=VMEM_MAX)

    fold64 = lambda row: (row[:64] + row[64:]).reshape(1, 64)
    grads = {
        "g_mix": d_g_mix, "b_gate": d_b_gate, "b_forget": gacc[5, :FOX_HEADS].reshape(1, FOX_HEADS),
        "qn_swa": fold64(gacc[0]), "kn_swa": fold64(gacc[1]),
        "sink_swa": -dsk[:, :SWA_GROUP, 0].reshape(1, SWA_HEADS), "rel_bias": d_rel[:, :SWA_HEADS],
        "qn_fox": fold64(gacc[2]), "kn_fox": fold64(gacc[3]),
        "g_mem": d_g_mem, "qn_mem": gacc[4].reshape(1, LANES), "kn_mem": d_kn_mem, "g_mlp": d_g_mlp,
    }
    return loss, grad_x, grads


MESH = pl.DeviceIdType.MESH
ANY = pl.BlockSpec(memory_space=pl.ANY)


def _place():
    x, y, c = lax.axis_index("x"), lax.axis_index("y"), lax.axis_index("c")
    chips = [(1 - x, y), (x, 1 - y), (1 - x, 1 - y)]
    return x, y, c, chips


def _handshake(peers):
    barrier = pltpu.get_barrier_semaphore()
    for peer in peers:
        pl.semaphore_signal(barrier, inc=1, device_id=peer, device_id_type=MESH)
    pl.semaphore_wait(barrier, len(peers))


def _all_gather_shards_async(name, collective_id, slots):
    n = len(slots)
    bufs = [jax.new_ref(s, memory_space=pltpu.MemorySpace.HBM) for s in slots]

    def body(ici_send, ici_recv, d2d_send, d2d_recv):
        x, y, c, chips = _place()
        sibling = (x, y, 1 - c)
        me = 2 * x + y
        _handshake([(px, py, c) for px, py in chips] + [sibling])

        def half(a, who):
            hr = slots[a].shape[1] // 2
            return pl.ds(pl.multiple_of(who * hr, hr), hr)

        def ici(a, j, slot, to):
            return pltpu.make_async_remote_copy(
                src_ref=bufs[a].at[me, half(a, c)], dst_ref=bufs[a].at[slot, half(a, c)],
                send_sem=ici_send.at[3 * a + j], recv_sem=ici_recv.at[3 * a + j], device_id=to, device_id_type=MESH)

        def d2d(a, j, slot, which):
            part = bufs[a].at[slot, half(a, which)]
            return pltpu.make_async_remote_copy(
                src_ref=part, dst_ref=part, send_sem=d2d_send.at[3 * a + j], recv_sem=d2d_recv.at[3 * a + j],
                device_id=sibling, device_id_type=MESH)

        sends = [ici(a, j, me, (*chip, c)) for a in range(n) for j, chip in enumerate(chips)]
        for cp in sends:
            cp.start()
        passed = []
        for a in range(n):
            for j, (px, py) in enumerate(chips):
                ici(a, j, 2 * px + py, (px, py, c)).wait_recv()
                cp = d2d(a, j, 2 * px + py, c)
                cp.start()
                passed.append(cp)
        for a in range(n):
            for j, (px, py) in enumerate(chips):
                d2d(a, j, 2 * px + py, 1 - c).wait_recv()
        for cp in sends + passed:
            cp.wait_send()

    pl.kernel(
        body, mesh=plsc.ScalarSubcoreMesh(axis_name="seq", num_cores=1), name=name,
        scratch_types=[pltpu.SemaphoreType.DMA((3 * n,))] * 4,
        compiler_params=pltpu.CompilerParams(collective_id=collective_id),
    )()
    return [b[...] for b in bufs]


def _sequencer_call(name, collective_id, n_sems, body):
    pl.kernel(
        body, mesh=plsc.ScalarSubcoreMesh(axis_name="seq", num_cores=1), name=name,
        scratch_types=[pltpu.SemaphoreType.DMA((n_sems,))] * 2,
        compiler_params=pltpu.CompilerParams(collective_id=collective_id),
    )()


def _hbm_ref(value):
    return jax.new_ref(value, memory_space=pltpu.MemorySpace.HBM)


def _pair_exchange(name, collective_id, gs):
    n = len(gs)
    src = [_hbm_ref(g) for g in gs]
    stage = [jax.empty_ref(jax.ShapeDtypeStruct((N_SHARD, g.shape[1] // 2, g.shape[2]), g.dtype),
                           memory_space=pltpu.MemorySpace.HBM) for g in gs]

    def body(send_sem, recv_sem):
        x, y, c, _ = _place()
        sibling = (x, y, 1 - c)
        _handshake([sibling])
        copies = []
        for a in range(n):
            hr = gs[a].shape[1] // 2
            theirs = pl.ds(pl.multiple_of((1 - c) * hr, hr), hr)
            copies.append(pltpu.make_async_remote_copy(
                src_ref=src[a].at[:, theirs, :], dst_ref=stage[a], send_sem=send_sem.at[a], recv_sem=recv_sem.at[a],
                device_id=sibling, device_id_type=MESH))
        for cp in copies:
            cp.start()
        for cp in copies:
            cp.wait()

    _sequencer_call(name, collective_id, n, body)
    return [s[...] for s in stage]


def _chip_exchange(name, collective_id, sums):
    n = len(sums)
    src = [_hbm_ref(s) for s in sums]
    got = [jax.empty_ref(jax.ShapeDtypeStruct((3,) + s.shape[1:], s.dtype), memory_space=pltpu.MemorySpace.HBM)
           for s in sums]

    def body(send_sem, recv_sem):
        x, y, c, chips = _place()
        _handshake([(px, py, c) for px, py in chips])
        copies = []
        for a in range(n):
            for j, (px, py) in enumerate(chips):
                copies.append(pltpu.make_async_remote_copy(
                    src_ref=src[a].at[2 * px + py], dst_ref=got[a].at[j],
                    send_sem=send_sem.at[3 * a + j], recv_sem=recv_sem.at[3 * a + j],
                    device_id=(px, py, c), device_id_type=MESH))
        for cp in copies:
            cp.start()
        for cp in copies:
            cp.wait()

    _sequencer_call(name, collective_id, 3 * n, body)
    return [g[...] for g in got]


def _pair_gather(name, collective_id, fulls):
    n = len(fulls)
    full = [_hbm_ref(f) for f in fulls]

    def body(send_sem, recv_sem):
        x, y, c, _ = _place()
        sibling = (x, y, 1 - c)
        _handshake([sibling])
        copies = []
        for a in range(n):
            hr = fulls[a].shape[0] // 2
            mine = full[a].at[pl.ds(pl.multiple_of(c * hr, hr), hr)]
            copies.append(pltpu.make_async_remote_copy(
                src_ref=mine, dst_ref=mine, send_sem=send_sem.at[a], recv_sem=recv_sem.at[a],
                device_id=sibling, device_id_type=MESH))
        for cp in copies:
            cp.start()
        for cp in copies:
            cp.wait()

    _sequencer_call(name, collective_id, n, body)
    return [f[...] for f in full]


ELEMENTWISE_BLOCK_ELEMS = 256 * 1024


def _row_block(rows, cols):
    rb = 8
    while rb * 2 * cols <= ELEMENTWISE_BLOCK_ELEMS and rb * 2 <= rows:
        rb *= 2
    return rb


def _pair_sum(name, place, g, stage):
    _, R, C = g.shape
    hr = R // 2
    rb = _row_block(hr, C)
    nb = hr // rb

    def body(place_ref, g_ref, st_ref, sum_bf, own_f32):
        s = pl.program_id(1)
        tot = g_ref[...] + st_ref[...]
        sum_bf[...] = tot.astype(BF16)

        @pl.when(s == place_ref[0])
        def _():
            own_f32[...] = tot

    return pl.pallas_call(
        body, name=name,
        grid_spec=pltpu.PrefetchScalarGridSpec(
            num_scalar_prefetch=1, grid=(nb, N_SHARD),
            in_specs=[pl.BlockSpec((None, rb, C), lambda i, s, pr: (s, pr[1] * nb + i, 0)),
                      pl.BlockSpec((None, rb, C), lambda i, s, pr: (s, i, 0))],
            out_specs=[pl.BlockSpec((None, rb, C), lambda i, s, pr: (s, i, 0)),
                       pl.BlockSpec((rb, C), lambda i, s, pr: (i, 0))]),
        out_shape=[jax.ShapeDtypeStruct((N_SHARD, hr, C), BF16), jax.ShapeDtypeStruct((hr, C), F32)],
        compiler_params=_cparams("arbitrary", "arbitrary"),
    )(place, g, stage)


def _final_sum(name, place, own, got):
    hr, C = own.shape
    rb = _row_block(hr, C)
    nb = hr // rb

    def body(place_ref, own_ref, got_ref, o_ref):
        o_ref[...] = ((own_ref[...] + got_ref[0].astype(F32)) + got_ref[1].astype(F32)) + got_ref[2].astype(F32)

    return pl.pallas_call(
        body, name=name,
        grid_spec=pltpu.PrefetchScalarGridSpec(
            num_scalar_prefetch=1, grid=(nb,),
            in_specs=[pl.BlockSpec((rb, C), lambda i, pr: (i, 0)), pl.BlockSpec((3, rb, C), lambda i, pr: (0, i, 0))],
            out_specs=pl.BlockSpec((rb, C), lambda i, pr: (pr[1] * nb + i, 0))),
        out_shape=jax.ShapeDtypeStruct((2 * hr, C), F32),
        compiler_params=_cparams("arbitrary"),
    )(place, own, got)


def _adamw_math(w, g, m, v):
    m = ADAM_B1 * m + (1.0 - ADAM_B1) * g
    v = ADAM_B2 * v + (1.0 - ADAM_B2) * (g * g)
    m_hat = m / (1.0 - ADAM_B1 ** ADAM_STEP)
    v_hat = v / (1.0 - ADAM_B2 ** ADAM_STEP)
    delta = -ADAM_LR * (m_hat / (jnp.sqrt(v_hat) + ADAM_EPS) + ADAM_WD * w)
    return delta, m, v


def _adamw(name, w, g, m, v):
    R, Cw = w.shape
    Cg = g.shape[1]
    rb = _row_block(R, Cg)

    def body(w_ref, g_ref, m_ref, v_ref, g_o, d_o, m_o, v_o):
        gv = g_ref[...]
        delta, mn, vn = _adamw_math(w_ref[...], gv, m_ref[...], v_ref[...])
        g_o[...] = gv
        d_o[...] = delta
        m_o[...] = mn
        v_o[...] = vn

    blk = pl.BlockSpec((rb, Cg), lambda i: (i, 0))
    return pl.pallas_call(
        body, name=name, grid=(R // rb,),
        in_specs=[blk] * 4, out_specs=[blk] * 4,
        out_shape=[jax.ShapeDtypeStruct((R, Cw), F32)] * 4,
        compiler_params=_cparams("parallel"),
    )(w, g, m, v)


N_DEV = 8
SMALL_ROWS = 64


def _small_allreduce_adamw(g, w, m, v):
    def body(g_ref, w_ref, m_ref, v_ref, all_ref, gs_o, d_o, m_o, v_o, send_sems, recv_sems, local_sem):
        x, y, c, chips = _place()
        me, sibling = (x, y, c), (x, y, 1 - c)

        def rows(px, py, pc):
            return all_ref.at[pl.ds(pl.multiple_of((4 * px + 2 * py + pc) * SMALL_ROWS, SMALL_ROWS), SMALL_ROWS), :]

        def copy(k, block, to, src=None):
            return pltpu.make_async_remote_copy(
                src_ref=rows(*block) if src is None else src, dst_ref=rows(*block),
                send_sem=send_sems.at[k], recv_sem=recv_sems.at[k], device_id=to, device_id_type=MESH)

        mine = pltpu.make_async_copy(g_ref, rows(*me), local_sem)
        mine.start()
        first = [copy(0, me, sibling, src=g_ref)]
        first += [copy(1 + j, me, (*chip, c), src=g_ref) for j, chip in enumerate(chips)]
        for cp in first:
            cp.start()
        passed = [copy(4 + j, (*chip, c), sibling) for j, chip in enumerate(chips)]
        for j, chip in enumerate(chips):
            copy(1 + j, (*chip, c), me).wait_recv()
            passed[j].start()
        copy(0, sibling, me).wait_recv()
        for j, chip in enumerate(chips):
            copy(4 + j, (*chip, 1 - c), me).wait_recv()
        for cp in first + passed:
            cp.wait_send()
        mine.wait()

        tot = all_ref[0:SMALL_ROWS, :]
        for d in range(1, N_DEV):
            tot = tot + all_ref[d * SMALL_ROWS:(d + 1) * SMALL_ROWS, :]
        delta, mn, vn = _adamw_math(w_ref[...], tot, m_ref[...], v_ref[...])
        gs_o[...] = tot
        d_o[...] = delta
        m_o[...] = mn
        v_o[...] = vn

    vm = pl.BlockSpec(memory_space=pltpu.VMEM)
    shp = jax.ShapeDtypeStruct((SMALL_ROWS, LANES), F32)
    res = pl.pallas_call(
        body, name="small_allreduce_adamw", in_specs=[vm] * 4, out_specs=[vm] * 5,
        out_shape=[jax.ShapeDtypeStruct((N_DEV * SMALL_ROWS, LANES), F32), shp, shp, shp, shp],
        scratch_shapes=[pltpu.SemaphoreType.DMA((7,)), pltpu.SemaphoreType.DMA((7,)), pltpu.SemaphoreType.DMA],
    )(g, w, m, v)
    return res[1:]


SMALL_NAMES = ("g_mix", "b_gate", "b_forget", "qn_swa", "kn_swa", "sink_swa", "rel_bias", "qn_fox", "kn_fox",
               "g_mem", "qn_mem", "kn_mem", "g_mlp")
BIG_NAMES = ("w_in", "w_mem_kv", "w_o_swa", "w_o_fox", "w_o_mem", "w_out", "w_mlp_up", "w_mlp_down")
WEIGHT_NAMES = ("g_mix", "w_in", "b_gate", "b_forget", "qn_swa", "kn_swa", "sink_swa", "rel_bias", "qn_fox", "kn_fox",
                "g_mem", "w_mem_kv", "qn_mem", "kn_mem", "w_o_swa", "w_o_fox", "w_o_mem", "w_out", "g_mlp",
                "w_mlp_up", "w_mlp_down")


def _pack_small(parts, extra=None):
    rows = []
    for n in SMALL_NAMES:
        flat = parts[n].reshape(-1).astype(F32)
        flat = jnp.pad(flat, (0, (-flat.size) % LANES))
        rows.append(flat.reshape(-1, LANES))
    if extra is not None:
        rows.append(jnp.pad(extra.reshape(1, 1), ((0, 0), (0, LANES - 1))))
    packed = jnp.concatenate(rows, axis=0)
    return jnp.pad(packed, ((0, SMALL_ROWS - packed.shape[0]), (0, 0)))


def _unpack_small(packed, shapes):
    out, r = {}, 0
    for n in SMALL_NAMES:
        size = math.prod(shapes[n])
        nr = -(-size // LANES)
        out[n] = packed[r:r + nr].reshape(-1)[:size].reshape(shapes[n])
        r += nr
    return out, packed[r, 0]


W_IN_SEGMENTS = ((C_QA, 0, 512), (C_QF, 768, 512), (C_KF, 1280, 512), (C_VF, 1792, 512), (C_QM, 2312, 512),
                 (C_KA, 512, 128), (C_VA, 640, 128), (C_FL, 2304, FOX_HEADS), (C_GL, 2824, GATE_W))
RELAYOUT_ROWS = 256


def _permute_pieces(src_of_dst):
    blocks = []
    for b in range(len(src_of_dst) // LANES):
        runs, lane = [], 0
        while lane < LANES:
            src = src_of_dst[b * LANES + lane]
            if src is None:
                lane += 1
                continue
            plane, col = src
            end = lane + 1
            while (end < LANES and src_of_dst[b * LANES + end] == (plane, col + end - lane)
                   and (col + end - lane) // LANES == col // LANES):
                end += 1
            runs.append((plane, col // LANES, (lane - col) % LANES, lane, end))
            lane = end
        blocks.append(runs)
    return blocks


def _permuted_block(runs, load, rows):
    lane = _lane((rows, LANES))
    acc = jnp.zeros((rows, LANES), F32)
    for plane, blk, shift, lo, hi in runs:
        x = load(plane, blk).astype(F32)
        if shift:
            x = pltpu.roll(x, shift, 1)
        acc = x if (lo, hi) == (0, LANES) else jnp.where((lane >= lo) & (lane < hi), x, acc)
    return acc


def _w_in_to_segments(g_in):
    src_of_dst = [None] * PROJ_W
    for mine, theirs, width in W_IN_SEGMENTS:
        for k in range(width):
            src_of_dst[mine + k] = ((theirs + k) // IN_SHARD, (theirs + k) % IN_SHARD)
    blocks = _permute_pieces(src_of_dst)
    rb = RELAYOUT_ROWS

    def body(src_ref, out_ref):
        for b, runs in enumerate(blocks):
            blk = _permuted_block(runs, lambda p, c: src_ref[p, :, c * LANES:(c + 1) * LANES], rb)
            out_ref[:, b * LANES:(b + 1) * LANES] = blk.astype(out_ref.dtype)

    return pl.pallas_call(
        body, name="w_in_to_segments", grid=(D_MODEL // rb,),
        in_specs=[pl.BlockSpec((N_SHARD, rb, IN_SHARD_PAD), lambda i: (0, i, 0))],
        out_specs=pl.BlockSpec((rb, PROJ_W), lambda i: (i, 0)),
        out_shape=jax.ShapeDtypeStruct((D_MODEL, PROJ_W), g_in.dtype),
        compiler_params=_cparams("parallel", vmem=VMEM_MID),
    )(g_in)


def _w_in_from_segments(lo, gl):
    mine_of_theirs = {}
    for mine, theirs, width in W_IN_SEGMENTS:
        for k in range(width):
            mine_of_theirs[theirs + k] = mine + k
    src_of_dst = [None] * (N_SHARD * IN_SHARD_PAD)
    for s in range(N_SHARD):
        for l in range(IN_SHARD):
            j = mine_of_theirs[s * IN_SHARD + l]
            src_of_dst[s * IN_SHARD_PAD + l] = (j // LO_W, j % LO_W)
    blocks = _permute_pieces(src_of_dst)
    per_slot = IN_SHARD_PAD // LANES
    rb = RELAYOUT_ROWS

    def body(lo_ref, gl_ref, out_ref):
        planes = (lo_ref, gl_ref)
        for b, runs in enumerate(blocks):
            blk = _permuted_block(runs, lambda p, c: planes[p][:, c * LANES:(c + 1) * LANES], rb)
            c0 = (b % per_slot) * LANES
            out_ref[b // per_slot, :, c0:c0 + LANES] = blk

    half = pl.BlockSpec((rb, LO_W), lambda i: (i, 0))
    return pl.pallas_call(
        body, name="w_in_from_segments", grid=(D_MODEL // rb,),
        in_specs=[half, half],
        out_specs=pl.BlockSpec((N_SHARD, rb, IN_SHARD_PAD), lambda i: (0, i, 0)),
        out_shape=jax.ShapeDtypeStruct((N_SHARD, D_MODEL, IN_SHARD_PAD), F32),
        compiler_params=_cparams("parallel", vmem=VMEM_MID),
    )(lo, gl)


def _after(first, then):
    return lax.optimization_barrier((first, then))


class _ReduceGroup:
    def __init__(self, tag, first_collective_id, place):
        self.tag, self.first_id, self.place = tag, first_collective_id, place

    def start(self, local, tie):
        self.names = tuple(local)
        mine, tie = _after([local[n] for n in self.names], tie)
        self.mine = mine
        self.staged = _pair_exchange("pair_exchange_" + self.tag, self.first_id, mine)
        return tie

    def send(self, tie):
        staged, tie = _after(self.staged, tie)
        sums = [_pair_sum("pair_sum_" + n, self.place, g, st) for n, g, st in zip(self.names, self.mine, staged)]
        travel, tie = _after([s[0] for s in sums], tie)
        self.own = [s[1] for s in sums]
        self.got = _chip_exchange("chip_exchange_" + self.tag, self.first_id + 1, travel)
        return tie

    def finish(self, tie):
        got, tie = _after(self.got, tie)
        halves = [_final_sum("final_sum_" + n, self.place, o, r) for n, o, r in zip(self.names, self.own, got)]
        halves, tie = _after(halves, tie)
        summed = _pair_gather("pair_gather_" + self.tag, self.first_id + 2, halves)
        self.summed = dict(zip(self.names, summed))
        return tie


class _GradReducer:
    def __init__(self, place):
        self.early = _ReduceGroup("early", 2, place)
        self.late = _ReduceGroup("late", 5, place)

    @staticmethod
    def _slot_rows(a):
        return a.reshape(N_SHARD, a.shape[0] // N_SHARD, a.shape[1])

    def early_start(self, g, tie):
        return self.early.start({"w_mlp_down": self._slot_rows(g["w_mlp_down"]), "w_mlp_up": g["w_mlp_up"],
                                 "w_out": self._slot_rows(g["w_out"]), "w_mem_kv": self._slot_rows(g["w_mem_kv"]),
                                 "w_o_swa": g["w_o_swa"], "w_o_fox": g["w_o_fox"], "w_o_mem": g["w_o_mem"]}, tie)

    def early_send(self, tie):
        return self.early.send(tie)

    def early_finish(self, tie):
        return self.early.finish(tie)

    def late_start(self, g, tie):
        d_in = _w_in_from_segments(g["wc_lo"], g["wc_gl"])
        return self.late.start({"w_in": d_in}, tie)

    def late_send(self, tie):
        return self.late.send(tie)

    def late_finish(self, tie):
        return self.late.finish(tie)

    @property
    def summed(self):
        return {**self.early.summed, **self.late.summed}


def kernel(x, mem, g_mix, w_in, b_gate, b_forget, qn_swa, kn_swa, sink_swa, rel_bias, qn_fox, kn_fox, g_mem, w_mem_kv, qn_mem, kn_mem, w_o_swa, w_o_fox, w_o_mem, w_out, g_mlp, w_mlp_up, w_mlp_down, loss_target, m_g_mix, m_w_in, m_b_gate, m_b_forget, m_qn_swa, m_kn_swa, m_sink_swa, m_rel_bias, m_qn_fox, m_kn_fox, m_g_mem, m_w_mem_kv, m_qn_mem, m_kn_mem, m_w_o_swa, m_w_o_fox, m_w_o_mem, m_w_out, m_g_mlp, m_w_mlp_up, m_w_mlp_down, v_g_mix, v_w_in, v_b_gate, v_b_forget, v_qn_swa, v_kn_swa, v_sink_swa, v_rel_bias, v_qn_fox, v_kn_fox, v_g_mem, v_w_mem_kv, v_qn_mem, v_kn_mem, v_w_o_swa, v_w_o_fox, v_w_o_mem, v_w_out, v_g_mlp, v_w_mlp_up, v_w_mlp_down):
    given = dict(locals())
    W = {n: given[n] for n in WEIGHT_NAMES}
    M = {n: given["m_" + n] for n in WEIGHT_NAMES}
    V = {n: given["v_" + n] for n in WEIGHT_NAMES}
    pad_in = ((0, 0), (0, IN_SHARD_PAD - IN_SHARD))

    shards = [jnp.pad(w_in[0].astype(BF16), pad_in)] + [W[n][0].astype(BF16) for n in BIG_NAMES[1:]]
    slots = [jnp.broadcast_to(s[None], (N_SHARD,) + s.shape) for s in shards]
    (g_in,) = _all_gather_shards_async("all_gather_w_in", 1, slots[:1])
    small = {n: (W[n] if n == "rel_bias" else W[n].reshape(1, -1)) for n in SMALL_NAMES}
    h = _rmsnorm("rms_mix", x[0], small["g_mix"], min(512, x.shape[1]))
    g_in, late, h, (m_in, v_in) = lax.optimization_barrier((g_in, slots[1:], h, (M["w_in"][0], V["w_in"][0])))
    M["w_in"], V["w_in"] = m_in[None], v_in[None]
    g_kv, g_oa, g_of, g_om, g_out, g_up, g_down = _all_gather_shards_async("all_gather_weights_async", 8, late)

    place = jnp.stack([2 * lax.axis_index("x") + lax.axis_index("y"), lax.axis_index("c")]).astype(jnp.int32)
    reducer = _GradReducer(place)
    loss, grad_x, grads = _local_step(
        x[0], h, mem[0], loss_target[0], small, g_in, g_kv.reshape(D_MODEL, D_MODEL), (g_oa, g_of, g_om),
        g_out.reshape(D_MODEL, D_MODEL), g_up, g_down.reshape(D_FF, D_MODEL), reducer)

    out = {}

    def adamw_of(names, summed):
        for n in names:
            res = _adamw("adamw_" + n, W[n][0], summed[n], M[n][0], V[n][0])
            out[n] = [r.reshape(W[n].shape) for r in res]

    adamw_of(reducer.early.names, reducer.early.summed)
    shapes = {n: W[n].shape for n in SMALL_NAMES}
    packed = _small_allreduce_adamw(_pack_small(grads, loss), _pack_small(W), _pack_small(M), _pack_small(V))
    done_meanwhile = ([out[n] for n in reducer.early.names], packed)
    (early_out, packed), grad_x = reducer.late_finish((done_meanwhile, grad_x))
    for n, res in zip(reducer.early.names, early_out):
        out[n] = res
    adamw_of(reducer.late.names, reducer.late.summed)
    unpacked = [_unpack_small(p, shapes) for p in packed]
    for n in SMALL_NAMES:
        out[n] = [u[0][n] for u in unpacked]
    loss_total = unpacked[0][1]

    return (loss_total, grad_x.reshape(x.shape),
            *[out[n][0] for n in WEIGHT_NAMES], *[out[n][1] for n in WEIGHT_NAMES],
            *[out[n][2] for n in WEIGHT_NAMES], *[out[n][3] for n in WEIGHT_NAMES])
```

```python
import functools
import math

import jax
import jax.numpy as jnp
from jax import lax
from jax.experimental import pallas as pl
from jax.experimental.pallas import tpu as pltpu
from jax.experimental.pallas import tpu_sc as plsc

F32 = jnp.float32
BF16 = jnp.bfloat16

D_MODEL = 1024
N_MEM = 256
SWA_HEADS = 8
SWA_KV_HEADS = 2
SWA_HEAD_DIM = 64
WINDOW = 128
FOX_HEADS = 8
FOX_HEAD_DIM = 64
MEM_HEADS = 4
MEM_HEAD_DIM = 128
D_FF = 4 * D_MODEL
REL_BUCKETS = 32
REL_MAX_DIST = 128
EPS = 1e-6
NEG = -1e30
GATE_W = 3 * D_MODEL
IN_WIDTH = 5896
N_SHARD = 4
IN_SHARD = IN_WIDTH // N_SHARD
IN_SHARD_PAD = 1536

ADAM_LR = 0.001
ADAM_B1 = 0.9
ADAM_B2 = 0.999
ADAM_EPS = 1e-08
ADAM_WD = 0.01
ADAM_STEP = 10

LANES = 128
V7X_VMEM_BYTES = 64 * 1024 * 1024
MIB = 1024 * 1024
VMEM_SMALL, VMEM_MID, VMEM_BIG, VMEM_MAX = 24 * MIB, 40 * MIB, 48 * MIB, 56 * MIB

C_QA, C_QF, C_KF, C_VF, C_QM, C_KA, C_VA, C_FL, C_GL = 0, 512, 1024, 1536, 2048, 2560, 2688, 2816, 3072
LO_W = 3072
PROJ_W = 6144

NN = (((1,), (0,)), ((), ()))
NT = (((1,), (1,)), ((), ()))
TN = (((0,), (0,)), ((), ()))


def _dot(a, b, dims=NN):
    return lax.dot_general(a, b, dims, preferred_element_type=F32)


def _cparams(*sem, vmem=VMEM_SMALL):
    return pltpu.CompilerParams(dimension_semantics=sem, vmem_limit_bytes=vmem)


def _split3(a):
    hi = a.astype(BF16)
    r1 = a - hi.astype(F32)
    mid = r1.astype(BF16)
    lo = (r1 - mid.astype(F32)).astype(BF16)
    return hi, mid, lo


def _group_mean(a, g2):
    hi = a.astype(BF16)
    mid = (a - hi.astype(F32)).astype(BF16)
    return _dot(jnp.concatenate([hi, mid], axis=1), g2)


def _dot3_left(g, a):
    hi, mid, lo = _split3(a)
    return _dot(g, hi) + _dot(g, mid) + _dot(g, lo)


def _group_mean_matrix(d):
    r = jnp.arange(LANES)
    g = jnp.where((r[:, None] // d) == (r[None, :] // d), 1.0 / d, 0.0).astype(BF16)
    return jnp.concatenate([g, g], axis=0)


def _lane(shape):
    return lax.broadcasted_iota(jnp.int32, shape, len(shape) - 1)


def _matmul(name, a, b, *, dims, grid, a_spec, b_spec, acc_shape, outs, epilogue, extra=(), vmem=VMEM_BIG):
    nk = grid[2]
    n_extra = len(extra)

    def body(a_ref, b_ref, *rest):
        extra_refs = rest[:n_extra]
        out_refs = rest[n_extra:n_extra + len(outs)]
        i, j, k = pl.program_id(0), pl.program_id(1), pl.program_id(2)
        part = _dot(a_ref[...].astype(BF16), b_ref[...].astype(BF16), dims)
        if nk == 1:
            epilogue(part, extra_refs, out_refs, (i, j))
            return
        acc_ref = rest[-1]

        @pl.when(k == 0)
        def _():
            acc_ref[...] = part

        @pl.when((k > 0) & (k < nk - 1))
        def _():
            acc_ref[...] += part

        @pl.when(k == nk - 1)
        def _():
            epilogue(acc_ref[...] + part, extra_refs, out_refs, (i, j))

    res = pl.pallas_call(
        body,
        name=name,
        grid=grid,
        in_specs=[a_spec, b_spec] + [s for _, s in extra],
        out_specs=[s for _, s in outs],
        out_shape=[s for s, _ in outs],
        scratch_shapes=[pltpu.VMEM(acc_shape, F32)] if nk > 1 else [],
        compiler_params=_cparams("arbitrary", "arbitrary", "arbitrary", vmem=vmem),
    )(a, b, *[x for x, _ in extra])
    return res


def _epi_store(acc, extra_refs, out_refs, ij):
    out_refs[0][...] = acc.astype(out_refs[0].dtype)


def _rms_rows(x, g):
    r = lax.rsqrt(jnp.mean(x * x, axis=-1, keepdims=True) + EPS)
    return x * r, r


def _rmsnorm_bwd_rows(dh, x, g):
    xhat, r = _rms_rows(x, g)
    dxh = dh * g
    dx = r * (dxh - xhat * jnp.mean(dxh * xhat, axis=-1, keepdims=True))
    return dx, jnp.sum(dh * xhat, axis=0, keepdims=True)


def _rmsnorm(name, x, g, tb):
    T, Dm = x.shape

    def body(x_ref, g_ref, o_ref):
        xhat, _ = _rms_rows(x_ref[...], None)
        o_ref[...] = (xhat * g_ref[...]).astype(o_ref.dtype)

    return pl.pallas_call(
        body, name=name, grid=(T // tb,),
        in_specs=[pl.BlockSpec((tb, Dm), lambda i: (i, 0)), pl.BlockSpec((1, Dm), lambda i: (0, 0))],
        out_specs=pl.BlockSpec((tb, Dm), lambda i: (i, 0)),
        out_shape=jax.ShapeDtypeStruct((T, Dm), BF16),
        compiler_params=_cparams("parallel"),
    )(x, g)


def _head_norm(x, gm, gain):
    ms = _group_mean(x * x, gm)
    r = lax.rsqrt(ms + EPS)
    return x * r * gain, x * r


def _head_norm_bwd(dy, x, gm, gain):
    ms = _group_mean(x * x, gm)
    r = lax.rsqrt(ms + EPS)
    xhat = x * r
    dxh = dy * gain
    dx = r * (dxh - xhat * _group_mean(dxh * xhat, gm))
    return dx, jnp.sum(dy * xhat, axis=0, keepdims=True)


def _log_sigmoid(z):
    return jnp.minimum(z, 0.0) - jnp.log(1.0 + jnp.exp(-jnp.abs(z)))


def _prep_fwd(proj, gains, bfor, tril, gm64, gm128, T, tb):
    nb = T // tb

    def body(qa_ref, qf_ref, kf_ref, vf_ref, qm_ref, ka_ref, va_ref, fl_ref, gains_ref, bfor_ref, tril_ref,
             gm64_ref, gm128_ref,
             qa_o, qf_o, kf_o, vf_o, qm_o, kad_o, vad_o, qaug_o, kaug_o, carry):
        i = pl.program_id(0)
        gm64v = gm64_ref[...]
        gm128v = gm128_ref[...]
        lane = _lane((tb, LANES))

        def norm512(src, dst, row, gm, scale=1.0):
            gain = gains_ref[row:row + 1, :]
            for c in range(4):
                sl = slice(c * LANES, (c + 1) * LANES)
                y, _ = _head_norm(src[:, sl], gm, gain)
                dst[:, sl] = (y * scale).astype(dst.dtype)

        norm512(qa_ref, qa_o, 0, gm64v)
        norm512(qf_ref, qf_o, 2, gm64v, FOX_SCALE)
        norm512(kf_ref, kf_o, 3, gm64v)
        norm512(qm_ref, qm_o, 4, gm128v)
        vf_o[...] = vf_ref[...].astype(vf_o.dtype)

        ka_n, _ = _head_norm(ka_ref[...], gm64v, gains_ref[1:2, :])
        ka_r = pltpu.roll(ka_n, 64, 1)
        va = va_ref[...]
        va_r = pltpu.roll(va, 64, 1)
        lo = lane < 64
        kad_o[0] = jnp.where(lo, ka_n, ka_r).astype(kad_o.dtype)
        kad_o[1] = jnp.where(lo, ka_r, ka_n).astype(kad_o.dtype)
        vad_o[0] = jnp.where(lo, va, va_r).astype(vad_o.dtype)
        vad_o[1] = jnp.where(lo, va_r, va).astype(vad_o.dtype)

        @pl.when(i == 0)
        def _():
            carry[...] = jnp.zeros_like(carry)

        logf = jnp.where(lane < FOX_HEADS, _log_sigmoid(fl_ref[...] + bfor_ref[...]), 0.0)
        c = _dot3_left(tril_ref[...], logf) + carry[0:1, :]
        carry[...] = jnp.broadcast_to(c[tb - 1:tb, :], carry.shape)
        for pair in range(FOX_HEADS // 2):
            qaug = jnp.zeros((tb, LANES), F32)
            kaug = jnp.zeros((tb, LANES), F32)
            for sub in range(2):
                col = jnp.sum(jnp.where(lane == 2 * pair + sub, c, 0.0), axis=1, keepdims=True)
                pieces = [p.astype(F32) for p in _split3(col)]
                base = AUG_STRIDE * sub
                for e in range(3):
                    qaug = jnp.where(lane == base + AUG_C + e, pieces[e], qaug)
                    kaug = jnp.where(lane == base + AUG_NEG_C + e, -pieces[e], kaug)
                qaug = jnp.where((lane >= base + AUG_NEG_C) & (lane < base + AUG_NEG_C + 3), 1.0, qaug)
                ones_k = ((lane >= base + AUG_C) & (lane < base + AUG_C + 3)) | (
                    (lane >= base + AUG_STAT) & (lane < base + AUG_STAT + 3))
                kaug = jnp.where(ones_k, 1.0, kaug)
            sl = slice(pair * LANES, (pair + 1) * LANES)
            qaug_o[:, sl] = qaug.astype(BF16)
            kaug_o[:, sl] = kaug.astype(BF16)

    def seg(width, start):
        return pl.BlockSpec((tb, width), lambda i, s=start // width: (i, s))

    const = lambda shape: pl.BlockSpec(shape, lambda i: tuple(0 for _ in shape))
    rows512 = pl.BlockSpec((tb, 512), lambda i: (i, 0))
    outs = pl.pallas_call(
        body, name="prep_fwd", grid=(nb,),
        in_specs=[seg(512, C_QA), seg(512, C_QF), seg(512, C_KF), seg(512, C_VF), seg(512, C_QM),
                  seg(128, C_KA), seg(128, C_VA), seg(128, C_FL),
                  const((8, LANES)), const((1, LANES)), const((tb, tb)), const((2 * LANES, LANES)), const((2 * LANES, LANES))],
        out_specs=[rows512, rows512, rows512, rows512, rows512,
                   pl.BlockSpec((2, tb, LANES), lambda i: (0, i, 0)), pl.BlockSpec((2, tb, LANES), lambda i: (0, i, 0)),
                   rows512, rows512],
        out_shape=[jax.ShapeDtypeStruct((T, 512), BF16)] * 5
        + [jax.ShapeDtypeStruct((2, T, LANES), BF16)] * 2
        + [jax.ShapeDtypeStruct((T, 512), BF16)] * 2,
        scratch_shapes=[pltpu.VMEM((8, LANES), F32)],
        compiler_params=_cparams("arbitrary", vmem=VMEM_MID),
    )(proj, proj, proj, proj, proj, proj, proj, proj, gains, bfor, tril, gm64, gm128)
    return outs


def _prep_bwd(proj, dqa, dkad, dvad, dqf, dkf, dvf, dqm, dqf_aug, dkf_aug, gains, bfor, triu, gm64, gm128, T, tb):
    nb = T // tb

    def body(qa_ref, qf_ref, kf_ref, qm_ref, ka_ref, fl_ref,
             dqa_ref, dkad_ref, dvad_ref, dqf_ref, dkf_ref, dvf_ref, dqm_ref, dqfa_ref, dkfa_ref,
             gains_ref, bfor_ref, triu_ref, gm64_ref, gm128_ref,
             dlo_o, gacc_o, carry):
        i = pl.program_id(0)
        gm64v = gm64_ref[...]
        gm128v = gm128_ref[...]
        lane = _lane((tb, LANES))

        @pl.when(i == 0)
        def _():
            carry[...] = jnp.zeros_like(carry)
            gacc_o[...] = jnp.zeros_like(gacc_o)

        def norm512_bwd(dsrc, xsrc, col0, row, gm):
            gain = gains_ref[row:row + 1, :]
            gsum = jnp.zeros((1, LANES), F32)
            for c in range(4):
                sl = slice(c * LANES, (c + 1) * LANES)
                dx, dg = _head_norm_bwd(dsrc[:, sl], xsrc[:, sl], gm, gain)
                dlo_o[:, col0 + c * LANES:col0 + (c + 1) * LANES] = dx.astype(dlo_o.dtype)
                gsum = gsum + dg
            gacc_o[row:row + 1, :] += gsum

        norm512_bwd(dqa_ref, qa_ref, C_QA, 0, gm64v)
        norm512_bwd(dqf_ref, qf_ref, C_QF, 2, gm64v)
        norm512_bwd(dkf_ref, kf_ref, C_KF, 3, gm64v)
        norm512_bwd(dqm_ref, qm_ref, C_QM, 4, gm128v)
        dlo_o[:, C_VF:C_VF + 512] = dvf_ref[...].astype(dlo_o.dtype)

        lo = lane < 64

        def fold(ref):
            f0 = ref[0] + pltpu.roll(ref[0], 64, 1)
            f1 = ref[1] + pltpu.roll(ref[1], 64, 1)
            return jnp.where(lo, f0, f1)

        dka, dg = _head_norm_bwd(fold(dkad_ref), ka_ref[...], gm64v, gains_ref[1:2, :])
        gacc_o[1:2, :] += dg
        dlo_o[:, C_KA:C_KA + LANES] = dka.astype(dlo_o.dtype)
        dlo_o[:, C_VA:C_VA + LANES] = fold(dvad_ref).astype(dlo_o.dtype)

        dc = jnp.zeros((tb, LANES), F32)
        for pair in range(FOX_HEADS // 2):
            sl = slice(pair * LANES, (pair + 1) * LANES)
            rows_sum, cols_sum = dqfa_ref[:, sl], dkfa_ref[:, sl]
            for sub in range(2):
                diff = (jnp.where(lane == AUG_STRIDE * sub + AUG_C, rows_sum, 0.0)
                        - jnp.where(lane == AUG_STRIDE * sub + AUG_NEG_C, cols_sum, 0.0))
                dc = jnp.where(lane == 2 * pair + sub, jnp.sum(diff, axis=1, keepdims=True), dc)
        dlogf = _dot3_left(triu_ref[...], dc) + carry[0:1, :]
        carry[...] = jnp.broadcast_to(dlogf[0:1, :], carry.shape)
        z = fl_ref[...] + bfor_ref[...]
        dfl = jnp.where(lane < FOX_HEADS, dlogf / (1.0 + jnp.exp(z)), 0.0)
        gacc_o[5:6, :] += jnp.sum(dfl, axis=0, keepdims=True)
        dlo_o[:, C_FL:C_FL + LANES] = dfl.astype(dlo_o.dtype)
        dlo_o[:, C_FL + LANES:C_FL + 2 * LANES] = jnp.zeros((tb, LANES), dlo_o.dtype)

    rev = lambda i: nb - 1 - i

    def seg(width, start):
        return pl.BlockSpec((tb, width), lambda i, s=start // width: (rev(i), s))

    const = lambda shape: pl.BlockSpec(shape, lambda i: tuple(0 for _ in shape))
    rows512 = pl.BlockSpec((tb, 512), lambda i: (rev(i), 0))
    dup = pl.BlockSpec((2, tb, LANES), lambda i: (0, rev(i), 0))
    return pl.pallas_call(
        body, name="prep_bwd", grid=(nb,),
        in_specs=[seg(512, C_QA), seg(512, C_QF), seg(512, C_KF), seg(512, C_QM), seg(128, C_KA), seg(128, C_FL),
                  rows512, dup, dup, rows512, rows512, rows512, rows512, rows512, rows512,
                  const((8, LANES)), const((1, LANES)), const((tb, tb)), const((2 * LANES, LANES)), const((2 * LANES, LANES))],
        out_specs=[pl.BlockSpec((tb, LO_W), lambda i: (rev(i), 0)), const((8, LANES))],
        out_shape=[jax.ShapeDtypeStruct((T, LO_W), BF16), jax.ShapeDtypeStruct((8, LANES), F32)],
        scratch_shapes=[pltpu.VMEM((8, LANES), F32)],
        compiler_params=_cparams("arbitrary", vmem=VMEM_MID),
    )(proj, proj, proj, proj, proj, proj, dqa, dkad, dvad, dqf, dkf, dvf, dqm, dqf_aug, dkf_aug,
      gains, bfor, triu, gm64, gm128)


FOX_SCALE = FOX_HEAD_DIM ** -0.5
AUG_STRIDE = 16
AUG_C = 0
AUG_NEG_C = 3
AUG_STAT = 6
FOX_TQ, FOX_TK = 1024, 1024
FOX_BWD_TQ, FOX_BWD_TK = 1024, 1024


def _fox_head_mask(sub, rows):
    lane = _lane((rows, 2 * LANES))
    main = (lane >= 64 * sub) & (lane < 64 * sub + 64)
    aug = (lane >= LANES + AUG_STRIDE * sub) & (lane < LANES + AUG_STRIDE * (sub + 1))
    return main | aug


def _fox_pieces(diagonal, tq, tk):
    if diagonal and tq == tk and tq >= 2 * LANES:
        return [(0, tq // 2, tk // 2), (tq // 2, tq, tk)]
    return [(0, tq, tk)]


def _fox_fwd(q, qaug, k, kaug, v, T, tq, tk):
    nq, nk = T // tq, T // tk
    rep = tk // LANES
    last_of = lambda i: (i * tq + tq - 1) // tk

    def body(q_ref, qa_ref, k_ref, ka_ref, v_ref, o_ref, qab_ref, m_s, acc_s):
        p_, i, j = pl.program_id(0), pl.program_id(1), pl.program_id(2)
        last = last_of(i)

        @pl.when(j == 0)
        def _():
            m_s[...] = jnp.full(m_s.shape, NEG, F32)
            acc_s[...] = jnp.zeros_like(acc_s)

        def step(diagonal):
            k2 = jnp.concatenate([k_ref[...], ka_ref[...]], axis=1)
            v2 = jnp.concatenate([v_ref[...], ka_ref[...]], axis=1)
            pieces = _fox_pieces(diagonal, tq, tk)
            work = []
            for r0, r1, nc in pieces:
                rows = slice(r0, r1)
                q2 = jnp.concatenate([q_ref[rows, :], qa_ref[rows, :]], axis=1)
                for sub in range(2):
                    qh = jnp.where(_fox_head_mask(sub, r1 - r0), q2, jnp.zeros_like(q2))
                    work.append((rows, r0, r1 - r0, nc, sub, _dot(qh, k2[:nc], NT)))
            for rows, r0, nr, nc, sub, s in work:
                if diagonal:
                    causal = (lax.broadcasted_iota(jnp.int32, (nr, nc), 1) + j * tk
                              <= lax.broadcasted_iota(jnp.int32, (nr, nc), 0) + (r0 + i * tq))
                    s = jnp.where(causal, s, NEG)
                m_prev = m_s[sub, rows, :]
                m_next = jnp.maximum(m_prev, jnp.max(s, axis=1, keepdims=True))
                p = jnp.exp(s - jnp.tile(m_next, (1, nc // LANES)))
                alpha = jnp.exp(m_prev - m_next)
                m_s[sub, rows, :] = m_next
                acc_s[sub, rows, :] = acc_s[sub, rows, :] * jnp.tile(alpha, (1, 2)) + _dot(p.astype(BF16), v2[:nc])

        @pl.when(j == last)
        def _():
            step(True)

        @pl.when(j < last)
        def _():
            step(False)

        @pl.when(j == nk - 1)
        def _():
            lane = _lane((tq, LANES))
            outs = []
            qab = qa_ref[...].astype(F32)
            for sub in range(2):
                acc = acc_s[sub]
                base = AUG_STRIDE * sub
                l = jnp.sum(jnp.where(lane == base + AUG_C, acc[:, LANES:], 0.0), axis=1, keepdims=True)
                outs.append(acc[:, :LANES] / l)
                lse = jnp.max(m_s[sub], axis=1, keepdims=True) + jnp.log(l)
                pieces = _split3(-lse)
                for e in range(3):
                    qab = jnp.where(lane == base + AUG_STAT + e, pieces[e].astype(F32), qab)
            o_ref[...] = jnp.where(lane < 64, outs[0], outs[1]).astype(o_ref.dtype)
            qab_ref[...] = qab.astype(BF16)

    qspec = pl.BlockSpec((tq, LANES), lambda p, i, j: (i, p))
    kspec = pl.BlockSpec((tk, LANES), lambda p, i, j: (jnp.minimum(j, last_of(i)), p))
    return pl.pallas_call(
        body, name="fox_fwd", grid=(4, nq, nk),
        in_specs=[qspec, qspec, kspec, kspec, kspec],
        out_specs=[qspec, qspec],
        out_shape=[jax.ShapeDtypeStruct((T, 512), BF16), jax.ShapeDtypeStruct((T, 512), BF16)],
        scratch_shapes=[pltpu.VMEM((2, tq, LANES), F32), pltpu.VMEM((2, tq, 2 * LANES), F32)],
        compiler_params=_cparams("parallel", "parallel", "arbitrary", vmem=VMEM_BIG),
    )(q, qaug, k, kaug, v)


def _fox_bwd(q, qaug, k, kaug, v, do, doaug, T, tq, tk):
    nq, nk = T // tq, T // tk
    first_of = lambda j: (j * tk) // tq

    def body(q_ref, qa_ref, k_ref, ka_ref, v_ref, do_ref, doa_ref,
             dq_ref, dqa_ref, dk_ref, dka_ref, dv_ref, dk_s, dv_s):
        p_, j, i = pl.program_id(0), pl.program_id(1), pl.program_id(2)
        masked = i * tq < (j + 1) * tk - 1

        @pl.when((j == 0) & (i == 0))
        def _():
            dq_ref[...] = jnp.zeros_like(dq_ref)
            dqa_ref[...] = jnp.zeros_like(dqa_ref)

        @pl.when(i == 0)
        def _():
            dk_s[...] = jnp.zeros_like(dk_s)
            dv_s[...] = jnp.zeros_like(dv_s)

        def step(diagonal):
            k2 = jnp.concatenate([k_ref[...], ka_ref[...]], axis=1)
            v2 = jnp.concatenate([v_ref[...], ka_ref[...]], axis=1)
            work = []
            for r0, r1, nc in _fox_pieces(diagonal, tq, tk):
                rows = slice(r0, r1)
                q2 = jnp.concatenate([q_ref[rows, :], qa_ref[rows, :]], axis=1)
                do2 = jnp.concatenate([do_ref[rows, :], doa_ref[rows, :]], axis=1)
                for sub in range(2):
                    hm = _fox_head_mask(sub, r1 - r0)
                    qh = jnp.where(hm, q2, jnp.zeros_like(q2))
                    doh = jnp.where(hm, do2, jnp.zeros_like(do2))
                    s = _dot(qh, k2[:nc], NT)
                    dp = _dot(doh, v2[:nc], NT)
                    work.append((r0, r1 - r0, nc, sub, qh, doh, s, dp))
            dqs = {}
            for r0, nr, nc, sub, qh, doh, s, dp in work:
                if diagonal:
                    causal = (lax.broadcasted_iota(jnp.int32, (nr, nc), 1) + j * tk
                              <= lax.broadcasted_iota(jnp.int32, (nr, nc), 0) + (r0 + i * tq))
                    s = jnp.where(causal, s, NEG)
                p = jnp.exp(s)
                dsb = (p * dp).astype(BF16)
                dv_s[0:nc, :] += _dot(p.astype(BF16), doh[:, :LANES], TN)
                dk_s[0:nc, :] += _dot(dsb, qh, TN)
                dqs[(r0, sub)] = _dot(dsb, k2[:nc])
            for r0, r1, nc in _fox_pieces(diagonal, tq, tk):
                dq2 = jnp.where(_fox_head_mask(0, r1 - r0), dqs[(r0, 0)], dqs[(r0, 1)])
                qrows = pl.ds(pl.multiple_of(i * tq + r0, r1 - r0), r1 - r0)
                dq_ref[qrows, :] += dq2[:, :LANES] * FOX_SCALE
                dqa_ref[qrows, :] += dq2[:, LANES:]

        @pl.when((i >= first_of(j)) & masked)
        def _():
            step(True)

        @pl.when((i >= first_of(j)) & jnp.logical_not(masked))
        def _():
            step(False)

        @pl.when(i == nq - 1)
        def _():
            dk_ref[...] = dk_s[:, :LANES]
            dka_ref[...] = dk_s[:, LANES:]
            dv_ref[...] = dv_s[...]

    qspec = pl.BlockSpec((tq, LANES), lambda p, j, i: (jnp.maximum(i, first_of(j)), p))
    kspec = pl.BlockSpec((tk, LANES), lambda p, j, i: (j, p))
    resident = pl.BlockSpec((T, LANES), lambda p, j, i: (0, p))
    return pl.pallas_call(
        body, name="fox_bwd", grid=(4, nk, nq),
        in_specs=[qspec, qspec, kspec, kspec, kspec, qspec, qspec],
        out_specs=[resident, resident, kspec, kspec, kspec],
        out_shape=[jax.ShapeDtypeStruct((T, 512), F32)] * 5,
        scratch_shapes=[pltpu.VMEM((tk, 2 * LANES), F32), pltpu.VMEM((tk, LANES), F32)],
        compiler_params=_cparams("arbitrary", "arbitrary", "arbitrary", vmem=VMEM_BIG),
    )(q, qaug, k, kaug, v, do, doaug)


SWA_SUB = 4
SWA_TB = SWA_SUB * WINDOW


def _t5_bucket_matrix():
    t = jnp.arange(WINDOW)[:, None] + WINDOW
    s = jnp.arange(2 * WINDOW)[None, :]
    max_exact = REL_BUCKETS // 2
    d = jnp.maximum(t - s, 0)
    df = jnp.maximum(d, 1).astype(F32)
    large = max_exact + (jnp.log(df / max_exact) / math.log(REL_MAX_DIST / max_exact)
                         * (REL_BUCKETS - max_exact)).astype(jnp.int32)
    large = jnp.minimum(large, REL_BUCKETS - 1)
    return jnp.where(d < max_exact, d, large).astype(jnp.int32)


def _swa_bias(rel_bias, bucket):
    def body(rel_ref, bucket_ref, o_ref):
        b = bucket_ref[...]
        for h in range(SWA_HEADS):
            acc = jnp.zeros(b.shape, F32)
            for r in range(REL_BUCKETS):
                acc = jnp.where(b == r, rel_ref[r, h], acc)
            o_ref[h] = acc

    return pl.pallas_call(
        body, name="swa_bias",
        in_specs=[pl.BlockSpec(memory_space=pltpu.SMEM), pl.BlockSpec(memory_space=pltpu.VMEM)],
        out_specs=pl.BlockSpec(memory_space=pltpu.VMEM),
        out_shape=jax.ShapeDtypeStruct((SWA_HEADS, WINDOW, 2 * WINDOW), F32),
    )(rel_bias, bucket)


def _swa_bias_bwd(dbias, bucket):
    def body(db_ref, bucket_ref, o_ref):
        b = bucket_ref[...]
        lane = _lane((1, LANES))
        for r in range(REL_BUCKETS):
            row = jnp.zeros((1, LANES), F32)
            for h in range(SWA_HEADS):
                part = jnp.sum(jnp.where(b == r, db_ref[h], 0.0), axis=0, keepdims=True)
                tot = jnp.sum(part, axis=1, keepdims=True)
                row = jnp.where(lane == h, tot, row)
            o_ref[r:r + 1, :] = row

    return pl.pallas_call(
        body, name="swa_bias_bwd",
        in_specs=[pl.BlockSpec(memory_space=pltpu.VMEM), pl.BlockSpec(memory_space=pltpu.VMEM)],
        out_specs=pl.BlockSpec(memory_space=pltpu.VMEM),
        out_shape=jax.ShapeDtypeStruct((REL_BUCKETS, LANES), F32),
    )(dbias, bucket)


SWA_GROUP = SWA_HEADS // SWA_KV_HEADS


def _swa_valid(r, i):
    t = (lax.broadcasted_iota(jnp.int32, (SWA_GROUP * WINDOW, 2 * WINDOW), 0) & (WINDOW - 1)) + WINDOW
    s = lax.broadcasted_iota(jnp.int32, (SWA_GROUP * WINDOW, 2 * WINDOW), 1)
    dist = t - s
    band = (dist >= 0) & (dist < WINDOW)
    if r == 0:
        band = band & ((s >= WINDOW) | (i > 0))
    return band


def _swa_stack(blk):
    lane = _lane((WINDOW, LANES))
    parts = []
    for g in range(SWA_GROUP):
        b = blk[:, LANES * (g // 2):LANES * (g // 2 + 1)]
        parts.append(jnp.where((lane >= 64) if g % 2 else (lane < 64), b, jnp.zeros_like(b)))
    return jnp.concatenate(parts, axis=0)


def _swa_unstack(st):
    lane = _lane((WINDOW, LANES))
    W = WINDOW
    return jnp.concatenate([jnp.where(lane < 64, st[2 * b * W:(2 * b + 1) * W], st[(2 * b + 1) * W:(2 * b + 2) * W])
                            for b in range(2)], axis=1)


def _swa_sink_column(sink_ref, kvh):
    row = lax.broadcasted_iota(jnp.int32, (SWA_GROUP * WINDOW, 1), 0)
    col = jnp.full((SWA_GROUP * WINDOW, 1), sink_ref[SWA_GROUP * kvh + SWA_GROUP - 1], F32)
    for g in range(SWA_GROUP - 2, -1, -1):
        col = jnp.where(row < (g + 1) * WINDOW, sink_ref[SWA_GROUP * kvh + g], col)
    return col


def _swa_specs(T):
    W = WINDOW
    qspec = pl.BlockSpec((SWA_TB, 2 * LANES), lambda h, i: (i, h))
    own = pl.BlockSpec((None, SWA_TB, LANES), lambda h, i: (h, i, 0))
    prev = pl.BlockSpec((None, W, LANES), lambda h, i: (h, jnp.maximum(SWA_SUB * i - 1, 0), 0))
    stat = pl.BlockSpec((SWA_GROUP, SWA_TB, LANES), lambda h, i: (h, i, 0))
    bias = pl.BlockSpec((None, SWA_GROUP * W, 2 * W), lambda h, i: (h, 0, 0))
    return qspec, own, prev, stat, bias


def _swa_fwd(sinks, q, kad, vad, bias, T):
    nb = T // SWA_TB
    scale = SWA_HEAD_DIM ** -0.5
    W = WINDOW

    def body(sink_ref, q_ref, k_ref, kp_ref, v_ref, vp_ref, bias_ref, o_ref, lse_ref):
        kvh, i = pl.program_id(0), pl.program_id(1)
        sink = _swa_sink_column(sink_ref, kvh)
        for r in range(SWA_SUB):
            rs = slice(r * W, (r + 1) * W)
            ps = slice((r - 1) * W, r * W)
            k_own, v_own = k_ref[rs, :], v_ref[rs, :]
            k_prev = kp_ref[...] if r == 0 else k_ref[ps, :]
            v_prev = vp_ref[...] if r == 0 else v_ref[ps, :]
            qs = _swa_stack(q_ref[rs, :])
            s = jnp.concatenate([_dot(qs, k_prev, NT), _dot(qs, k_own, NT)], axis=1) * scale + bias_ref[...]
            s = jnp.where(_swa_valid(r, i), s, NEG)
            m = jnp.maximum(jnp.max(s, axis=1, keepdims=True), sink)
            p = jnp.exp(s - m)
            denom = jnp.sum(p, axis=1, keepdims=True) + jnp.exp(sink - m)
            pn = (p / denom).astype(BF16)
            o_ref[rs, :] = _swa_unstack(_dot(pn[:, :W], v_prev) + _dot(pn[:, W:], v_own)).astype(o_ref.dtype)
            lse = m + jnp.log(denom)
            for g in range(SWA_GROUP):
                lse_ref[g, rs, :] = jnp.broadcast_to(lse[g * W:(g + 1) * W], (W, LANES))

    qspec, own, prev, stat, bspec = _swa_specs(T)
    return pl.pallas_call(
        body, name="swa_fwd", grid=(SWA_KV_HEADS, nb),
        in_specs=[pl.BlockSpec(memory_space=pltpu.SMEM), qspec, own, prev, own, prev, bspec],
        out_specs=[qspec, stat],
        out_shape=[jax.ShapeDtypeStruct((T, 512), BF16), jax.ShapeDtypeStruct((SWA_HEADS, T, LANES), F32)],
        compiler_params=_cparams("parallel", "parallel", vmem=VMEM_MID),
    )(sinks, q, kad, kad, vad, vad, bias.reshape(SWA_KV_HEADS, SWA_GROUP * W, 2 * W))


def _swa_bwd(sinks, q, kad, vad, bias, do, lse, delta, T):
    nb = T // SWA_TB
    scale = SWA_HEAD_DIM ** -0.5
    W = WINDOW

    def body(sink_ref, q_ref, k_ref, kp_ref, v_ref, vp_ref, bias_ref, do_ref, lse_ref, dl_ref,
             dq_ref, dkad_ref, dvad_ref, dbias_ref, dsk_ref):
        kvh, i = pl.program_id(0), pl.program_id(1)
        sink = _swa_sink_column(sink_ref, kvh)

        @pl.when((kvh == 0) & (i == 0))
        def _():
            dkad_ref[...] = jnp.zeros_like(dkad_ref)
            dvad_ref[...] = jnp.zeros_like(dvad_ref)

        @pl.when(i == 0)
        def _():
            dbias_ref[...] = jnp.zeros_like(dbias_ref)
            dsk_ref[...] = jnp.zeros_like(dsk_ref)

        for r in range(SWA_SUB):
            rs = slice(r * W, (r + 1) * W)
            ps = slice((r - 1) * W, r * W)
            k_own, v_own = k_ref[rs, :], v_ref[rs, :]
            k_prev = kp_ref[...] if r == 0 else k_ref[ps, :]
            v_prev = vp_ref[...] if r == 0 else v_ref[ps, :]
            qs = _swa_stack(q_ref[rs, :])
            dos = _swa_stack(do_ref[rs, :])
            lse_b = jnp.concatenate([lse_ref[g, rs, :] for g in range(SWA_GROUP)], axis=0)
            dl_b = jnp.concatenate([dl_ref[g, rs, :] for g in range(SWA_GROUP)], axis=0)
            s = jnp.concatenate([_dot(qs, k_prev, NT), _dot(qs, k_own, NT)], axis=1) * scale + bias_ref[...]
            s = jnp.where(_swa_valid(r, i), s, NEG)
            p = jnp.exp(s - jnp.tile(lse_b, (1, 2)))
            dp = jnp.concatenate([_dot(dos, v_prev, NT), _dot(dos, v_own, NT)], axis=1)
            ds = p * (dp - jnp.tile(dl_b, (1, 2)))
            sink_term = jnp.exp(sink - lse_b) * dl_b
            for g in range(SWA_GROUP):
                dbias_ref[g] += ds[g * W:(g + 1) * W]
                dsk_ref[g:g + 1, :] += jnp.sum(sink_term[g * W:(g + 1) * W], axis=0, keepdims=True)
            dsb = ds.astype(BF16)
            pb = p.astype(BF16)
            dq_ref[rs, :] = _swa_unstack((_dot(dsb[:, :W], k_prev) + _dot(dsb[:, W:], k_own)) * scale)
            own_row = pl.multiple_of(i * SWA_TB + r * W, W)
            dkad_ref[kvh, pl.ds(own_row, W), :] += _dot(dsb[:, W:], qs, TN) * scale
            dvad_ref[kvh, pl.ds(own_row, W), :] += _dot(pb[:, W:], dos, TN)
            dk_prev = _dot(dsb[:, :W], qs, TN) * scale
            dv_prev = _dot(pb[:, :W], dos, TN)
            if r == 0:
                @pl.when(i > 0)
                def _():
                    prev_row = pl.multiple_of(i * SWA_TB - W, W)
                    dkad_ref[kvh, pl.ds(prev_row, W), :] += dk_prev
                    dvad_ref[kvh, pl.ds(prev_row, W), :] += dv_prev
            else:
                prev_row = pl.multiple_of(i * SWA_TB + (r - 1) * W, W)
                dkad_ref[kvh, pl.ds(prev_row, W), :] += dk_prev
                dvad_ref[kvh, pl.ds(prev_row, W), :] += dv_prev

    qspec, own, prev, stat, bspec = _swa_specs(T)
    full = pl.BlockSpec((SWA_KV_HEADS, T, LANES), lambda h, i: (0, 0, 0))
    return pl.pallas_call(
        body, name="swa_bwd", grid=(SWA_KV_HEADS, nb),
        in_specs=[pl.BlockSpec(memory_space=pltpu.SMEM), qspec, own, prev, own, prev, bspec, qspec, stat, stat],
        out_specs=[qspec, full, full, pl.BlockSpec((SWA_GROUP, W, 2 * W), lambda h, i: (h, 0, 0)),
                   pl.BlockSpec((None, 8, LANES), lambda h, i: (h, 0, 0))],
        out_shape=[jax.ShapeDtypeStruct((T, 512), F32), jax.ShapeDtypeStruct((SWA_KV_HEADS, T, LANES), F32),
                   jax.ShapeDtypeStruct((SWA_KV_HEADS, T, LANES), F32), jax.ShapeDtypeStruct((SWA_HEADS, W, 2 * W), F32),
                   jax.ShapeDtypeStruct((SWA_KV_HEADS, 8, LANES), F32)],
        compiler_params=_cparams("arbitrary", "arbitrary", vmem=VMEM_MID),
    )(sinks, q, kad, kad, vad, vad, bias.reshape(SWA_KV_HEADS, SWA_GROUP * W, 2 * W), do, lse, delta)


def _mem_fwd(q, mk, mv, T, tq):
    scale = MEM_HEAD_DIM ** -0.5

    def body(q_ref, k_ref, v_ref, o_ref, lse_ref):
        s = _dot(q_ref[...], k_ref[...], NT) * scale
        m = jnp.max(s, axis=1, keepdims=True)
        p = jnp.exp(s - m)
        l = jnp.sum(p, axis=1, keepdims=True)
        o_ref[...] = _dot((p / l).astype(BF16), v_ref[...]).astype(o_ref.dtype)
        lse_ref[...] = jnp.broadcast_to(m + jnp.log(l), (tq, LANES))

    qspec = pl.BlockSpec((tq, LANES), lambda h, i: (i, h))
    kspec = pl.BlockSpec((N_MEM, LANES), lambda h, i: (0, h))
    return pl.pallas_call(
        body, name="mem_fwd", grid=(MEM_HEADS, T // tq),
        in_specs=[qspec, kspec, kspec],
        out_specs=[qspec, pl.BlockSpec((None, tq, LANES), lambda h, i: (h, i, 0))],
        out_shape=[jax.ShapeDtypeStruct((T, 512), BF16), jax.ShapeDtypeStruct((MEM_HEADS, T, LANES), F32)],
        compiler_params=_cparams("parallel", "parallel"),
    )(q, mk, mv)


def _mem_bwd(q, mk, mv, do, lse, delta, T, tq):
    scale = MEM_HEAD_DIM ** -0.5
    rep = N_MEM // LANES

    def body(q_ref, k_ref, v_ref, do_ref, lse_ref, dl_ref, dq_ref, dk_ref, dv_ref):
        i = pl.program_id(1)

        @pl.when(i == 0)
        def _():
            dk_ref[...] = jnp.zeros_like(dk_ref)
            dv_ref[...] = jnp.zeros_like(dv_ref)

        qv, dov = q_ref[...], do_ref[...]
        s = _dot(qv, k_ref[...], NT) * scale
        p = jnp.exp(s - jnp.tile(lse_ref[...], (1, rep)))
        dp = _dot(dov, v_ref[...], NT)
        ds = p * (dp - jnp.tile(dl_ref[...], (1, rep)))
        dsb = ds.astype(BF16)
        dq_ref[...] = _dot(dsb, k_ref[...]) * scale
        dk_ref[...] += _dot(dsb, qv, TN) * scale
        dv_ref[...] += _dot(p.astype(BF16), dov, TN)

    qspec = pl.BlockSpec((tq, LANES), lambda h, i: (i, h))
    kspec = pl.BlockSpec((N_MEM, LANES), lambda h, i: (0, h))
    stat = pl.BlockSpec((None, tq, LANES), lambda h, i: (h, i, 0))
    return pl.pallas_call(
        body, name="mem_bwd", grid=(MEM_HEADS, T // tq),
        in_specs=[qspec, kspec, kspec, qspec, stat, stat],
        out_specs=[qspec, kspec, kspec],
        out_shape=[jax.ShapeDtypeStruct((T, 512), F32), jax.ShapeDtypeStruct((N_MEM, 512), F32),
                   jax.ShapeDtypeStruct((N_MEM, 512), F32)],
        compiler_params=_cparams("arbitrary", "arbitrary"),
    )(q, mk, mv, do, lse, delta)


def _mem_prep_fwd(mem, g_mem, w_kv, kn_gain, gm128):
    def body(mem_ref, g_ref, w_ref, kn_ref, gm_ref, memn_o, kv_o, mk_o, mv_o):
        xhat, _ = _rms_rows(mem_ref[...], None)
        memn = (xhat * g_ref[...]).astype(BF16)
        memn_o[...] = memn
        kv = _dot(memn, w_ref[...])
        kv_o[...] = kv
        gm = gm_ref[...]
        for c in range(4):
            sl = slice(c * LANES, (c + 1) * LANES)
            y, _ = _head_norm(kv[:, sl], gm, kn_ref[...])
            mk_o[:, sl] = y.astype(BF16)
        mv_o[...] = kv[:, 512:].astype(BF16)

    vm = pl.BlockSpec(memory_space=pltpu.VMEM)
    return pl.pallas_call(
        body, name="mem_prep_fwd", in_specs=[vm] * 5, out_specs=[vm] * 4,
        out_shape=[jax.ShapeDtypeStruct((N_MEM, D_MODEL), BF16), jax.ShapeDtypeStruct((N_MEM, D_MODEL), F32),
                   jax.ShapeDtypeStruct((N_MEM, 512), BF16), jax.ShapeDtypeStruct((N_MEM, 512), BF16)],
        compiler_params=pltpu.CompilerParams(vmem_limit_bytes=VMEM_MID),
    )(mem, g_mem, w_kv, kn_gain, gm128)


def _mem_prep_bwd(mem, g_mem, memn, kv, w_kv, kn_gain, gm128, dmk, dmv):
    def body(mem_ref, g_ref, memn_ref, kv_ref, w_ref, kn_ref, gm_ref, dmk_ref, dmv_ref, dw_o, dg_o, dkn_o, dkv_s):
        gm = gm_ref[...]
        dkn = jnp.zeros((1, LANES), F32)
        for c in range(4):
            sl = slice(c * LANES, (c + 1) * LANES)
            dx, dg = _head_norm_bwd(dmk_ref[:, sl], kv_ref[:, sl], gm, kn_ref[...])
            dkv_s[:, sl] = dx.astype(BF16)
            dkn = dkn + dg
        dkn_o[...] = dkn
        dkv_s[:, 512:] = dmv_ref[...].astype(BF16)
        dkv = dkv_s[...]
        dw_o[...] = _dot(memn_ref[...], dkv, TN)
        dmemn = _dot(dkv, w_ref[...], NT)
        xhat, _ = _rms_rows(mem_ref[...], None)
        dg_o[...] = jnp.sum(dmemn * xhat, axis=0, keepdims=True)

    vm = pl.BlockSpec(memory_space=pltpu.VMEM)
    return pl.pallas_call(
        body, name="mem_prep_bwd", in_specs=[vm] * 9, out_specs=[vm] * 3,
        out_shape=[jax.ShapeDtypeStruct((D_MODEL, D_MODEL), F32), jax.ShapeDtypeStruct((1, D_MODEL), F32),
                   jax.ShapeDtypeStruct((1, LANES), F32)],
        scratch_shapes=[pltpu.VMEM((N_MEM, D_MODEL), BF16)],
        compiler_params=pltpu.CompilerParams(vmem_limit_bytes=VMEM_MID),
    )(mem, g_mem, memn, kv, w_kv, kn_gain, gm128, dmk, dmv)


SLOT_O = D_MODEL // N_SHARD


def _merge_fwd(proj, b_gate, o3, w3, T, tb):
    def body(gl_ref, bg_ref, oa_ref, of_ref, om_ref, wa_ref, wf_ref, wm_ref, out_ref):
        o_refs = (oa_ref, of_ref, om_ref)
        w_refs = (wa_ref, wf_ref, wm_ref)
        for n in range(N_SHARD):
            acc = jnp.zeros((tb, SLOT_O), F32)
            for b in range(3):
                c0 = b * D_MODEL + n * SLOT_O
                g = jax.nn.sigmoid(gl_ref[:, c0:c0 + SLOT_O] + bg_ref[:, c0:c0 + SLOT_O])
                acc = acc + g * _dot(o_refs[b][...], w_refs[b][n])
            out_ref[:, n * SLOT_O:(n + 1) * SLOT_O] = acc.astype(out_ref.dtype)

    rows = pl.BlockSpec((tb, 512), lambda i: (i, 0))
    wspec = pl.BlockSpec((N_SHARD, 512, SLOT_O), lambda i: (0, 0, 0))
    return pl.pallas_call(
        body, name="merge_fwd", grid=(T // tb,),
        in_specs=[pl.BlockSpec((tb, GATE_W), lambda i: (i, 1)), pl.BlockSpec((1, GATE_W), lambda i: (0, 0)),
                  rows, rows, rows, wspec, wspec, wspec],
        out_specs=pl.BlockSpec((tb, D_MODEL), lambda i: (i, 0)),
        out_shape=jax.ShapeDtypeStruct((T, D_MODEL), BF16),
        compiler_params=_cparams("parallel", vmem=VMEM_BIG),
    )(proj, b_gate, *o3, *w3)


def _merge_bwd(proj, b_gate, o3, w3, dmerged, T, tb):
    heads = (SWA_HEADS, FOX_HEADS, MEM_HEADS)

    def body(gl_ref, bg_ref, oa_ref, of_ref, om_ref, wa_ref, wf_ref, wm_ref, dm_ref,
             dgl_o, doa_o, dof_o, dom_o, dla_o, dlf_o, dlm_o, dwa_o, dwf_o, dwm_o, dbg_o):
        i = pl.program_id(0)
        o_refs = (oa_ref, of_ref, om_ref)
        w_refs = (wa_ref, wf_ref, wm_ref)
        do_refs = (doa_o, dof_o, dom_o)
        dl_refs = (dla_o, dlf_o, dlm_o)
        dw_refs = (dwa_o, dwf_o, dwm_o)

        @pl.when(i == 0)
        def _():
            for r in dw_refs:
                r[...] = jnp.zeros_like(r)
            dbg_o[...] = jnp.zeros_like(dbg_o)

        lane = _lane((tb, LANES))
        for b in range(3):
            ob = o_refs[b][...]
            do = jnp.zeros((tb, 512), F32)
            for n in range(N_SHARD):
                c0 = b * D_MODEL + n * SLOT_O
                g = jax.nn.sigmoid(gl_ref[:, c0:c0 + SLOT_O] + bg_ref[:, c0:c0 + SLOT_O])
                dm = dm_ref[:, n * SLOT_O:(n + 1) * SLOT_O]
                y = _dot(ob, w_refs[b][n])
                dgl = dm * y * g * (1.0 - g)
                dgl_o[:, c0:c0 + SLOT_O] = dgl.astype(dgl_o.dtype)
                dbg_o[:, c0:c0 + SLOT_O] += jnp.sum(dgl, axis=0, keepdims=True)
                dy = (dm * g).astype(BF16)
                do = do + _dot(dy, w_refs[b][n], NT)
                dw_refs[b][n] += _dot(ob, dy, TN)
            do_refs[b][...] = do.astype(BF16)
            prod = do * ob.astype(F32)
            for c in range(4):
                blk = prod[:, c * LANES:(c + 1) * LANES]
                if heads[b] == 8:
                    lo = jnp.sum(jnp.where(lane < 64, blk, 0.0), axis=1, keepdims=True)
                    hi = jnp.sum(jnp.where(lane >= 64, blk, 0.0), axis=1, keepdims=True)
                    if b == 1:
                        aug = jnp.zeros((tb, LANES), F32)
                        for sub, dl in enumerate((lo, hi)):
                            for e, piece in enumerate(_split3(-dl)):
                                aug = jnp.where(lane == AUG_STRIDE * sub + AUG_C + e, piece.astype(F32), aug)
                        dl_refs[b][:, c * LANES:(c + 1) * LANES] = aug.astype(BF16)
                    else:
                        dl_refs[b][2 * c] = jnp.broadcast_to(lo, (tb, LANES))
                        dl_refs[b][2 * c + 1] = jnp.broadcast_to(hi, (tb, LANES))
                else:
                    dl_refs[b][c] = jnp.broadcast_to(jnp.sum(blk, axis=1, keepdims=True), (tb, LANES))

    rows = pl.BlockSpec((tb, 512), lambda i: (i, 0))
    wspec = pl.BlockSpec((N_SHARD, 512, SLOT_O), lambda i: (0, 0, 0))
    stat = lambda h: pl.BlockSpec((h, tb, LANES), lambda i: (0, i, 0))
    return pl.pallas_call(
        body, name="merge_bwd", grid=(T // tb,),
        in_specs=[pl.BlockSpec((tb, GATE_W), lambda i: (i, 1)), pl.BlockSpec((1, GATE_W), lambda i: (0, 0)),
                  rows, rows, rows, wspec, wspec, wspec, pl.BlockSpec((tb, D_MODEL), lambda i: (i, 0))],
        out_specs=[pl.BlockSpec((tb, GATE_W), lambda i: (i, 0)), rows, rows, rows,
                   stat(8), rows, stat(4), wspec, wspec, wspec, pl.BlockSpec((1, GATE_W), lambda i: (0, 0))],
        out_shape=[jax.ShapeDtypeStruct((T, GATE_W), BF16)] + [jax.ShapeDtypeStruct((T, 512), BF16)] * 3
        + [jax.ShapeDtypeStruct((8, T, LANES), F32), jax.ShapeDtypeStruct((T, 512), BF16),
           jax.ShapeDtypeStruct((4, T, LANES), F32)]
        + [jax.ShapeDtypeStruct((N_SHARD, 512, SLOT_O), F32)] * 3 + [jax.ShapeDtypeStruct((1, GATE_W), F32)],
        compiler_params=_cparams("arbitrary", vmem=VMEM_BIG),
    )(proj, b_gate, *o3, *w3, dmerged)


def _local_step(x, h, mem, tgt, small, g_in, w_kv, w_o3, w_out, w_up, w_down, reducer):
    T = x.shape[0]
    tm = min(512, T)
    tile2 = lambda v: jnp.tile(v.reshape(1, -1), (1, LANES // v.size))
    gains = jnp.concatenate([tile2(small["qn_swa"]), tile2(small["kn_swa"]), tile2(small["qn_fox"]),
                             tile2(small["kn_fox"]), tile2(small["qn_mem"]), jnp.zeros((3, LANES), F32)], axis=0)
    kn_mem = small["kn_mem"].reshape(1, LANES)
    bfor = jnp.pad(small["b_forget"].reshape(1, -1), ((0, 0), (0, LANES - FOX_HEADS)))
    gm64 = _group_mean_matrix(64)
    gm128 = _group_mean_matrix(128)
    tb_prep = min(256, T)
    ones = jnp.ones((tb_prep, tb_prep), F32)
    tril = jnp.tril(ones).astype(BF16)
    triu = jnp.triu(ones).astype(BF16)
    bucket = _t5_bucket_matrix()
    g_mix, g_mlp, g_mem = small["g_mix"], small["g_mlp"], small["g_mem"]
    b_gate = small["b_gate"]
    sinks = small["sink_swa"].reshape(-1)

    tl = min(1024, T)
    sq = pl.BlockSpec((tl, D_MODEL), lambda i, j, k: (i, j))
    wc = _w_in_to_segments(g_in)
    (proj,) = _matmul(
        "mm_proj", h, wc, dims=NN, grid=(T // tl, PROJ_W // D_MODEL, 1),
        a_spec=pl.BlockSpec((tl, D_MODEL), lambda i, j, k: (i, 0)),
        b_spec=pl.BlockSpec((D_MODEL, D_MODEL), lambda i, j, k: (0, j)),
        acc_shape=(tl, D_MODEL),
        outs=[(jax.ShapeDtypeStruct((T, PROJ_W), F32), sq)],
        epilogue=_epi_store)
    qa, qf, kf, vf, qm, kad, vad, qf_aug, kf_aug = _prep_fwd(proj, gains, bfor, tril, gm64, gm128, T, tb_prep)
    bias = _swa_bias(small["rel_bias"], bucket)
    o_swa, lse_swa = _swa_fwd(sinks, qa, kad, vad, bias, T)
    o_fox, qf_aug_bwd = _fox_fwd(qf, qf_aug, kf, kf_aug, vf, T, min(FOX_TQ, T), min(FOX_TK, T))
    memn, kv, mk, mv = _mem_prep_fwd(mem, g_mem, w_kv, kn_mem, gm128)
    o_mem, lse_mem = _mem_fwd(qm, mk, mv, T, tm)
    o3 = (o_swa, o_fox, o_mem)
    merged = _merge_fwd(proj, b_gate, o3, w_o3, T, min(512, T))

    def epi_residual(acc, extra_refs, out_refs, ij):
        out_refs[0][...] = extra_refs[0][...] + acc

    row_full = pl.BlockSpec((tm, D_MODEL), lambda i, j, k: (i, 0))
    row_big = pl.BlockSpec((tl, D_MODEL), lambda i, j, k: (i, 0))
    whole = pl.BlockSpec((D_MODEL, D_MODEL), lambda i, j, k: (0, 0))
    (x2,) = _matmul(
        "mm_out", merged, w_out, dims=NN, grid=(T // tl, 1, 1),
        a_spec=row_big, b_spec=whole,
        acc_shape=(tl, D_MODEL), extra=[(x, row_big)],
        outs=[(jax.ShapeDtypeStruct((T, D_MODEL), F32), row_big)], epilogue=epi_residual)
    hm = _rmsnorm("rms_mlp", x2, g_mlp, tm)

    def epi_relu2(acc, extra_refs, out_refs, ij):
        out_refs[0][...] = acc
        r = jnp.maximum(acc, 0.0)
        out_refs[1][...] = (r * r).astype(BF16)

    up, u = _matmul(
        "mm_up", hm, w_up, dims=NN, grid=(T // tl, N_SHARD, 1),
        a_spec=row_big, b_spec=pl.BlockSpec((None, D_MODEL, D_MODEL), lambda i, j, k: (j, 0, 0)),
        acc_shape=(tl, D_MODEL),
        outs=[(jax.ShapeDtypeStruct((T, D_FF), F32), sq), (jax.ShapeDtypeStruct((T, D_FF), BF16), sq)],
        epilogue=epi_relu2)

    def epi_loss(acc, extra_refs, out_refs, ij):
        y = extra_refs[0][...] + acc
        err = y - extra_refs[1][...]
        dyv = err * (1.0 / D_MODEL)
        out_refs[0][...] = dyv
        out_refs[2][...] = dyv.astype(BF16)
        sq = jnp.sum(jnp.sum(err * err, axis=1, keepdims=True), axis=0, keepdims=True)

        @pl.when(ij[0] == 0)
        def _():
            out_refs[1][...] = jnp.zeros_like(out_refs[1])

        out_refs[1][...] += jnp.broadcast_to(sq, out_refs[1].shape)

    kblk = pl.BlockSpec((tl, D_MODEL), lambda i, j, k: (i, k))
    dy, loss_acc, dy_bf = _matmul(
        "mm_down", u, w_down, dims=NN, grid=(T // tl, 1, N_SHARD),
        a_spec=kblk, b_spec=pl.BlockSpec((D_MODEL, D_MODEL), lambda i, j, k: (k, 0)),
        acc_shape=(tl, D_MODEL), extra=[(x2, row_big), (tgt, row_big)],
        outs=[(jax.ShapeDtypeStruct((T, D_MODEL), F32), row_big),
              (jax.ShapeDtypeStruct((8, LANES), F32), pl.BlockSpec((8, LANES), lambda i, j, k: (0, 0))),
              (jax.ShapeDtypeStruct((T, D_MODEL), BF16), row_big)],
        epilogue=epi_loss)
    loss = loss_acc[0, 0] * (0.5 / D_MODEL)

    def epi_dup(acc, extra_refs, out_refs, ij):
        out_refs[0][...] = (acc * (2.0 * jnp.maximum(extra_refs[0][...], 0.0))).astype(BF16)

    (dup,) = _matmul(
        "mm_dup", dy_bf, w_down, dims=NT, grid=(T // tl, N_SHARD, 1),
        a_spec=row_big, b_spec=pl.BlockSpec((D_MODEL, D_MODEL), lambda i, j, k: (j, 0)),
        acc_shape=(tl, D_MODEL), extra=[(up, sq)],
        outs=[(jax.ShapeDtypeStruct((T, D_FF), BF16), sq)], epilogue=epi_dup)

    nkt = T // tl
    t_rows = pl.BlockSpec((tl, D_MODEL), lambda i, j, k: (k, i))
    t_cols = pl.BlockSpec((tl, D_MODEL), lambda i, j, k: (k, j))
    (d_w_down,) = _matmul(
        "mm_dw_down", u, dy_bf, dims=TN, grid=(N_SHARD, 1, nkt),
        a_spec=t_rows, b_spec=t_cols, acc_shape=(D_MODEL, D_MODEL),
        outs=[(jax.ShapeDtypeStruct((D_FF, D_MODEL), F32), pl.BlockSpec((D_MODEL, D_MODEL), lambda i, j, k: (i, 0)))],
        epilogue=_epi_store)
    (d_w_up,) = _matmul(
        "mm_dw_up", hm, dup, dims=TN, grid=(1, N_SHARD, nkt),
        a_spec=t_rows, b_spec=t_cols, acc_shape=(D_MODEL, D_MODEL),
        outs=[(jax.ShapeDtypeStruct((N_SHARD, D_MODEL, D_MODEL), F32),
               pl.BlockSpec((None, D_MODEL, D_MODEL), lambda i, j, k: (j, 0, 0)))],
        epilogue=_epi_store)

    def epi_rms_bwd(acc, extra_refs, out_refs, ij):
        dx, dg = _rmsnorm_bwd_rows(acc, extra_refs[0][...], extra_refs[1][...])
        out_refs[0][...] = dx + extra_refs[2][...]

        @pl.when(ij[0] == 0)
        def _():
            out_refs[1][...] = jnp.zeros_like(out_refs[1])

        out_refs[1][...] += dg

    gain_spec = pl.BlockSpec((1, D_MODEL), lambda i, j, k: (0, 0))
    dx2, d_g_mlp = _matmul(
        "mm_dhm", dup, w_up, dims=NT, grid=(T // tl, 1, N_SHARD),
        a_spec=kblk, b_spec=pl.BlockSpec((None, D_MODEL, D_MODEL), lambda i, j, k: (k, 0, 0)),
        acc_shape=(tl, D_MODEL), extra=[(x2, row_big), (g_mlp, gain_spec), (dy, row_big)],
        outs=[(jax.ShapeDtypeStruct((T, D_MODEL), F32), row_big), (jax.ShapeDtypeStruct((1, D_MODEL), F32), gain_spec)],
        epilogue=epi_rms_bwd)

    (dmerged,) = _matmul(
        "mm_dmerged", dx2, w_out, dims=NT, grid=(T // tl, 1, 1),
        a_spec=row_big, b_spec=whole,
        acc_shape=(tl, D_MODEL), outs=[(jax.ShapeDtypeStruct((T, D_MODEL), F32), row_big)], epilogue=_epi_store)
    (d_w_out,) = _matmul(
        "mm_dw_out", merged, dx2, dims=TN, grid=(1, 1, nkt),
        a_spec=t_rows, b_spec=t_cols, acc_shape=(D_MODEL, D_MODEL),
        outs=[(jax.ShapeDtypeStruct((D_MODEL, D_MODEL), F32), whole)],
        epilogue=_epi_store)
    (dgl, do_swa, do_fox, do_mem, dl_swa, do_fox_aug, dl_mem, d_wo_swa, d_wo_fox, d_wo_mem, d_b_gate) = _merge_bwd(
        proj, b_gate, o3, w_o3, dmerged, T, min(512, T))

    dqm, dmk, dmv = _mem_bwd(qm, mk, mv, do_mem, lse_mem, dl_mem, T, tm)
    d_w_kv, d_g_mem, d_kn_mem = _mem_prep_bwd(mem, g_mem, memn, kv, w_kv, kn_mem, gm128, dmk, dmv)
    do_swa = reducer.early_start({"w_mlp_down": d_w_down, "w_mlp_up": d_w_up, "w_out": d_w_out, "w_mem_kv": d_w_kv,
                                  "w_o_swa": d_wo_swa, "w_o_fox": d_wo_fox, "w_o_mem": d_wo_mem}, do_swa)
    dqa, dkad, dvad, dbias, dsk = _swa_bwd(sinks, qa, kad, vad, bias, do_swa, lse_swa, dl_swa, T)
    dqa, do_fox = reducer.early_send((dqa, do_fox))
    dqf, dqf_aug, dkf, dkf_aug, dvf = _fox_bwd(qf, qf_aug_bwd, kf, kf_aug, vf, do_fox, do_fox_aug, T,
                                               min(FOX_BWD_TQ, T), min(FOX_BWD_TK, T))
    dvf = reducer.early_finish(dvf)
    d_rel = _swa_bias_bwd(dbias, bucket)
    dlo, gacc = _prep_bwd(proj, dqa, dkad, dvad, dqf, dkf, dvf, dqm, dqf_aug, dkf_aug, gains, bfor, triu, gm64, gm128,
                          T, tb_prep)

    def dwc_half(name, dpart):
        (res,) = _matmul(
            name, h, dpart, dims=TN, grid=(1, LO_W // D_MODEL, nkt),
            a_spec=t_rows, b_spec=t_cols, acc_shape=(D_MODEL, D_MODEL),
            outs=[(jax.ShapeDtypeStruct((D_MODEL, LO_W), F32), pl.BlockSpec((D_MODEL, D_MODEL), lambda i, j, k: (0, j)))],
            epilogue=_epi_store)
        return res

    d_wc_lo = dwc_half("mm_dwc_lo", dlo)
    d_wc_gl = dwc_half("mm_dwc_gl", dgl)
    dlo = reducer.late_start({"wc_lo": d_wc_lo, "wc_gl": d_wc_gl}, dlo)
    (dh_lo,) = _matmul(
        "mm_dh_lo", dlo, wc, dims=NT, grid=(T // tl, 1, LO_W // D_MODEL),
        a_spec=kblk, b_spec=pl.BlockSpec((D_MODEL, D_MODEL), lambda i, j, k: (0, k)),
        acc_shape=(tl, D_MODEL), outs=[(jax.ShapeDtypeStruct((T, D_MODEL), F32), row_big)], epilogue=_epi_store)
    dh_lo = reducer.late_send(dh_lo)

    def epi_dx(acc, extra_refs, out_refs, ij):
        dhh = acc + extra_refs[3][...]
        dx, dg = _rmsnorm_bwd_rows(dhh, extra_refs[0][...], extra_refs[1][...])
        out_refs[0][...] = dx + extra_refs[2][...]

        @pl.when(ij[0] == 0)
        def _():
            out_refs[1][...] = jnp.zeros_like(out_refs[1])

        out_refs[1][...] += dg

    grad_x, d_g_mix = _matmul(
        "mm_dh_gl", dgl, wc, dims=NT, grid=(T // tl, 1, GATE_W // D_MODEL),
        a_spec=kblk, b_spec=pl.BlockSpec((D_MODEL, D_MODEL), lambda i, j, k: (0, k + LO_W // D_MODEL)),
        acc_shape=(tl, D_MODEL), extra=[(x, row_big), (g_mix, gain_spec), (dx2, row_big), (dh_lo, row_big)],
        outs=[(jax.ShapeDtypeStruct((T, D_MODEL), F32), row_big), (jax.ShapeDtypeStruct((1, D_MODEL), F32), gain_spec)],
        epilogue=epi_dx, vmem=VMEM_MAX)

    fold64 = lambda row: (row[:64] + row[64:]).reshape(1, 64)
    grads = {
        "g_mix": d_g_mix, "b_gate": d_b_gate, "b_forget": gacc[5, :FOX_HEADS].reshape(1, FOX_HEADS),
        "qn_swa": fold64(gacc[0]), "kn_swa": fold64(gacc[1]),
        "sink_swa": -dsk[:, :SWA_GROUP, 0].reshape(1, SWA_HEADS), "rel_bias": d_rel[:, :SWA_HEADS],
        "qn_fox": fold64(gacc[2]), "kn_fox": fold64(gacc[3]),
        "g_mem": d_g_mem, "qn_mem": gacc[4].reshape(1, LANES), "kn_mem": d_kn_mem, "g_mlp": d_g_mlp,
    }
    return loss, grad_x, grads


MESH = pl.DeviceIdType.MESH
ANY = pl.BlockSpec(memory_space=pl.ANY)


def _place():
    x, y, c = lax.axis_index("x"), lax.axis_index("y"), lax.axis_index("c")
    chips = [(1 - x, y), (x, 1 - y), (1 - x, 1 - y)]
    return x, y, c, chips


def _handshake(peers):
    barrier = pltpu.get_barrier_semaphore()
    for peer in peers:
        pl.semaphore_signal(barrier, inc=1, device_id=peer, device_id_type=MESH)
    pl.semaphore_wait(barrier, len(peers))


def _all_gather_shards_async(name, collective_id, slots):
    n = len(slots)
    bufs = [jax.new_ref(s, memory_space=pltpu.MemorySpace.HBM) for s in slots]

    def body(ici_send, ici_recv, d2d_send, d2d_recv):
        x, y, c, chips = _place()
        sibling = (x, y, 1 - c)
        me = 2 * x + y
        _handshake([(px, py, c) for px, py in chips] + [sibling])

        def half(a, who):
            hr = slots[a].shape[1] // 2
            return pl.ds(pl.multiple_of(who * hr, hr), hr)

        def ici(a, j, slot, to):
            return pltpu.make_async_remote_copy(
                src_ref=bufs[a].at[me, half(a, c)], dst_ref=bufs[a].at[slot, half(a, c)],
                send_sem=ici_send.at[3 * a + j], recv_sem=ici_recv.at[3 * a + j], device_id=to, device_id_type=MESH)

        def d2d(a, j, slot, which):
            part = bufs[a].at[slot, half(a, which)]
            return pltpu.make_async_remote_copy(
                src_ref=part, dst_ref=part, send_sem=d2d_send.at[3 * a + j], recv_sem=d2d_recv.at[3 * a + j],
                device_id=sibling, device_id_type=MESH)

        sends = [ici(a, j, me, (*chip, c)) for a in range(n) for j, chip in enumerate(chips)]
        for cp in sends:
            cp.start()
        passed = []
        for a in range(n):
            for j, (px, py) in enumerate(chips):
                ici(a, j, 2 * px + py, (px, py, c)).wait_recv()
                cp = d2d(a, j, 2 * px + py, c)
                cp.start()
                passed.append(cp)
        for a in range(n):
            for j, (px, py) in enumerate(chips):
                d2d(a, j, 2 * px + py, 1 - c).wait_recv()
        for cp in sends + passed:
            cp.wait_send()

    pl.kernel(
        body, mesh=plsc.ScalarSubcoreMesh(axis_name="seq", num_cores=1), name=name,
        scratch_types=[pltpu.SemaphoreType.DMA((3 * n,))] * 4,
        compiler_params=pltpu.CompilerParams(collective_id=collective_id),
    )()
    return [b[...] for b in bufs]


def _sequencer_call(name, collective_id, n_sems, body):
    pl.kernel(
        body, mesh=plsc.ScalarSubcoreMesh(axis_name="seq", num_cores=1), name=name,
        scratch_types=[pltpu.SemaphoreType.DMA((n_sems,))] * 2,
        compiler_params=pltpu.CompilerParams(collective_id=collective_id),
    )()


def _hbm_ref(value):
    return jax.new_ref(value, memory_space=pltpu.MemorySpace.HBM)


def _pair_exchange(name, collective_id, gs):
    n = len(gs)
    src = [_hbm_ref(g) for g in gs]
    stage = [jax.empty_ref(jax.ShapeDtypeStruct((N_SHARD, g.shape[1] // 2, g.shape[2]), g.dtype),
                           memory_space=pltpu.MemorySpace.HBM) for g in gs]

    def body(send_sem, recv_sem):
        x, y, c, _ = _place()
        sibling = (x, y, 1 - c)
        _handshake([sibling])
        copies = []
        for a in range(n):
            hr = gs[a].shape[1] // 2
            theirs = pl.ds(pl.multiple_of((1 - c) * hr, hr), hr)
            copies.append(pltpu.make_async_remote_copy(
                src_ref=src[a].at[:, theirs, :], dst_ref=stage[a], send_sem=send_sem.at[a], recv_sem=recv_sem.at[a],
                device_id=sibling, device_id_type=MESH))
        for cp in copies:
            cp.start()
        for cp in copies:
            cp.wait()

    _sequencer_call(name, collective_id, n, body)
    return [s[...] for s in stage]


def _chip_exchange(name, collective_id, sums):
    n = len(sums)
    src = [_hbm_ref(s) for s in sums]
    got = [jax.empty_ref(jax.ShapeDtypeStruct((3,) + s.shape[1:], s.dtype), memory_space=pltpu.MemorySpace.HBM)
           for s in sums]

    def body(send_sem, recv_sem):
        x, y, c, chips = _place()
        _handshake([(px, py, c) for px, py in chips])
        copies = []
        for a in range(n):
            for j, (px, py) in enumerate(chips):
                copies.append(pltpu.make_async_remote_copy(
                    src_ref=src[a].at[2 * px + py], dst_ref=got[a].at[j],
                    send_sem=send_sem.at[3 * a + j], recv_sem=recv_sem.at[3 * a + j],
                    device_id=(px, py, c), device_id_type=MESH))
        for cp in copies:
            cp.start()
        for cp in copies:
            cp.wait()

    _sequencer_call(name, collective_id, 3 * n, body)
    return [g[...] for g in got]


def _pair_gather(name, collective_id, fulls):
    n = len(fulls)
    full = [_hbm_ref(f) for f in fulls]

    def body(send_sem, recv_sem):
        x, y, c, _ = _place()
        sibling = (x, y, 1 - c)
        _handshake([sibling])
        copies = []
        for a in range(n):
            hr = fulls[a].shape[0] // 2
            mine = full[a].at[pl.ds(pl.multiple_of(c * hr, hr), hr)]
            copies.append(pltpu.make_async_remote_copy(
                src_ref=mine, dst_ref=mine, send_sem=send_sem.at[a], recv_sem=recv_sem.at[a],
                device_id=sibling, device_id_type=MESH))
        for cp in copies:
            cp.start()
        for cp in copies:
            cp.wait()

    _sequencer_call(name, collective_id, n, body)
    return [f[...] for f in full]


ELEMENTWISE_BLOCK_ELEMS = 256 * 1024


def _row_block(rows, cols):
    rb = 8
    while rb * 2 * cols <= ELEMENTWISE_BLOCK_ELEMS and rb * 2 <= rows:
        rb *= 2
    return rb


def _pair_sum(name, place, g, stage):
    _, R, C = g.shape
    hr = R // 2
    rb = _row_block(hr, C)
    nb = hr // rb

    def body(place_ref, g_ref, st_ref, sum_bf, own_f32):
        s = pl.program_id(1)
        tot = g_ref[...] + st_ref[...]
        sum_bf[...] = tot.astype(BF16)

        @pl.when(s == place_ref[0])
        def _():
            own_f32[...] = tot

    return pl.pallas_call(
        body, name=name,
        grid_spec=pltpu.PrefetchScalarGridSpec(
            num_scalar_prefetch=1, grid=(nb, N_SHARD),
            in_specs=[pl.BlockSpec((None, rb, C), lambda i, s, pr: (s, pr[1] * nb + i, 0)),
                      pl.BlockSpec((None, rb, C), lambda i, s, pr: (s, i, 0))],
            out_specs=[pl.BlockSpec((None, rb, C), lambda i, s, pr: (s, i, 0)),
                       pl.BlockSpec((rb, C), lambda i, s, pr: (i, 0))]),
        out_shape=[jax.ShapeDtypeStruct((N_SHARD, hr, C), BF16), jax.ShapeDtypeStruct((hr, C), F32)],
        compiler_params=_cparams("arbitrary", "arbitrary"),
    )(place, g, stage)


def _final_sum(name, place, own, got):
    hr, C = own.shape
    rb = _row_block(hr, C)
    nb = hr // rb

    def body(place_ref, own_ref, got_ref, o_ref):
        o_ref[...] = ((own_ref[...] + got_ref[0].astype(F32)) + got_ref[1].astype(F32)) + got_ref[2].astype(F32)

    return pl.pallas_call(
        body, name=name,
        grid_spec=pltpu.PrefetchScalarGridSpec(
            num_scalar_prefetch=1, grid=(nb,),
            in_specs=[pl.BlockSpec((rb, C), lambda i, pr: (i, 0)), pl.BlockSpec((3, rb, C), lambda i, pr: (0, i, 0))],
            out_specs=pl.BlockSpec((rb, C), lambda i, pr: (pr[1] * nb + i, 0))),
        out_shape=jax.ShapeDtypeStruct((2 * hr, C), F32),
        compiler_params=_cparams("arbitrary"),
    )(place, own, got)


def _adamw_math(w, g, m, v):
    m = ADAM_B1 * m + (1.0 - ADAM_B1) * g
    v = ADAM_B2 * v + (1.0 - ADAM_B2) * (g * g)
    m_hat = m / (1.0 - ADAM_B1 ** ADAM_STEP)
    v_hat = v / (1.0 - ADAM_B2 ** ADAM_STEP)
    delta = -ADAM_LR * (m_hat / (jnp.sqrt(v_hat) + ADAM_EPS) + ADAM_WD * w)
    return delta, m, v


def _adamw(name, w, g, m, v):
    R, Cw = w.shape
    Cg = g.shape[1]
    rb = _row_block(R, Cg)

    def body(w_ref, g_ref, m_ref, v_ref, g_o, d_o, m_o, v_o):
        gv = g_ref[...]
        delta, mn, vn = _adamw_math(w_ref[...], gv, m_ref[...], v_ref[...])
        g_o[...] = gv
        d_o[...] = delta
        m_o[...] = mn
        v_o[...] = vn

    blk = pl.BlockSpec((rb, Cg), lambda i: (i, 0))
    return pl.pallas_call(
        body, name=name, grid=(R // rb,),
        in_specs=[blk] * 4, out_specs=[blk] * 4,
        out_shape=[jax.ShapeDtypeStruct((R, Cw), F32)] * 4,
        compiler_params=_cparams("parallel"),
    )(w, g, m, v)


N_DEV = 8
SMALL_ROWS = 64


def _small_allreduce_adamw(g, w, m, v):
    def body(g_ref, w_ref, m_ref, v_ref, all_ref, gs_o, d_o, m_o, v_o, send_sems, recv_sems, local_sem):
        x, y, c, chips = _place()
        me, sibling = (x, y, c), (x, y, 1 - c)

        def rows(px, py, pc):
            return all_ref.at[pl.ds(pl.multiple_of((4 * px + 2 * py + pc) * SMALL_ROWS, SMALL_ROWS), SMALL_ROWS), :]

        def copy(k, block, to, src=None):
            return pltpu.make_async_remote_copy(
                src_ref=rows(*block) if src is None else src, dst_ref=rows(*block),
                send_sem=send_sems.at[k], recv_sem=recv_sems.at[k], device_id=to, device_id_type=MESH)

        mine = pltpu.make_async_copy(g_ref, rows(*me), local_sem)
        mine.start()
        first = [copy(0, me, sibling, src=g_ref)]
        first += [copy(1 + j, me, (*chip, c), src=g_ref) for j, chip in enumerate(chips)]
        for cp in first:
            cp.start()
        passed = [copy(4 + j, (*chip, c), sibling) for j, chip in enumerate(chips)]
        for j, chip in enumerate(chips):
            copy(1 + j, (*chip, c), me).wait_recv()
            passed[j].start()
        copy(0, sibling, me).wait_recv()
        for j, chip in enumerate(chips):
            copy(4 + j, (*chip, 1 - c), me).wait_recv()
        for cp in first + passed:
            cp.wait_send()
        mine.wait()

        tot = all_ref[0:SMALL_ROWS, :]
        for d in range(1, N_DEV):
            tot = tot + all_ref[d * SMALL_ROWS:(d + 1) * SMALL_ROWS, :]
        delta, mn, vn = _adamw_math(w_ref[...], tot, m_ref[...], v_ref[...])
        gs_o[...] = tot
        d_o[...] = delta
        m_o[...] = mn
        v_o[...] = vn

    vm = pl.BlockSpec(memory_space=pltpu.VMEM)
    shp = jax.ShapeDtypeStruct((SMALL_ROWS, LANES), F32)
    res = pl.pallas_call(
        body, name="small_allreduce_adamw", in_specs=[vm] * 4, out_specs=[vm] * 5,
        out_shape=[jax.ShapeDtypeStruct((N_DEV * SMALL_ROWS, LANES), F32), shp, shp, shp, shp],
        scratch_shapes=[pltpu.SemaphoreType.DMA((7,)), pltpu.SemaphoreType.DMA((7,)), pltpu.SemaphoreType.DMA],
    )(g, w, m, v)
    return res[1:]


SMALL_NAMES = ("g_mix", "b_gate", "b_forget", "qn_swa", "kn_swa", "sink_swa", "rel_bias", "qn_fox", "kn_fox",
               "g_mem", "qn_mem", "kn_mem", "g_mlp")
BIG_NAMES = ("w_in", "w_mem_kv", "w_o_swa", "w_o_fox", "w_o_mem", "w_out", "w_mlp_up", "w_mlp_down")
WEIGHT_NAMES = ("g_mix", "w_in", "b_gate", "b_forget", "qn_swa", "kn_swa", "sink_swa", "rel_bias", "qn_fox", "kn_fox",
                "g_mem", "w_mem_kv", "qn_mem", "kn_mem", "w_o_swa", "w_o_fox", "w_o_mem", "w_out", "g_mlp",
                "w_mlp_up", "w_mlp_down")


def _pack_small(parts, extra=None):
    rows = []
    for n in SMALL_NAMES:
        flat = parts[n].reshape(-1).astype(F32)
        flat = jnp.pad(flat, (0, (-flat.size) % LANES))
        rows.append(flat.reshape(-1, LANES))
    if extra is not None:
        rows.append(jnp.pad(extra.reshape(1, 1), ((0, 0), (0, LANES - 1))))
    packed = jnp.concatenate(rows, axis=0)
    return jnp.pad(packed, ((0, SMALL_ROWS - packed.shape[0]), (0, 0)))


def _unpack_small(packed, shapes):
    out, r = {}, 0
    for n in SMALL_NAMES:
        size = math.prod(shapes[n])
        nr = -(-size // LANES)
        out[n] = packed[r:r + nr].reshape(-1)[:size].reshape(shapes[n])
        r += nr
    return out, packed[r, 0]


W_IN_SEGMENTS = ((C_QA, 0, 512), (C_QF, 768, 512), (C_KF, 1280, 512), (C_VF, 1792, 512), (C_QM, 2312, 512),
                 (C_KA, 512, 128), (C_VA, 640, 128), (C_FL, 2304, FOX_HEADS), (C_GL, 2824, GATE_W))
RELAYOUT_ROWS = 256


def _permute_pieces(src_of_dst):
    blocks = []
    for b in range(len(src_of_dst) // LANES):
        runs, lane = [], 0
        while lane < LANES:
            src = src_of_dst[b * LANES + lane]
            if src is None:
                lane += 1
                continue
            plane, col = src
            end = lane + 1
            while (end < LANES and src_of_dst[b * LANES + end] == (plane, col + end - lane)
                   and (col + end - lane) // LANES == col // LANES):
                end += 1
            runs.append((plane, col // LANES, (lane - col) % LANES, lane, end))
            lane = end
        blocks.append(runs)
    return blocks


def _permuted_block(runs, load, rows):
    lane = _lane((rows, LANES))
    acc = jnp.zeros((rows, LANES), F32)
    for plane, blk, shift, lo, hi in runs:
        x = load(plane, blk).astype(F32)
        if shift:
            x = pltpu.roll(x, shift, 1)
        acc = x if (lo, hi) == (0, LANES) else jnp.where((lane >= lo) & (lane < hi), x, acc)
    return acc


def _w_in_to_segments(g_in):
    src_of_dst = [None] * PROJ_W
    for mine, theirs, width in W_IN_SEGMENTS:
        for k in range(width):
            src_of_dst[mine + k] = ((theirs + k) // IN_SHARD, (theirs + k) % IN_SHARD)
    blocks = _permute_pieces(src_of_dst)
    rb = RELAYOUT_ROWS

    def body(src_ref, out_ref):
        for b, runs in enumerate(blocks):
            blk = _permuted_block(runs, lambda p, c: src_ref[p, :, c * LANES:(c + 1) * LANES], rb)
            out_ref[:, b * LANES:(b + 1) * LANES] = blk.astype(out_ref.dtype)

    return pl.pallas_call(
        body, name="w_in_to_segments", grid=(D_MODEL // rb,),
        in_specs=[pl.BlockSpec((N_SHARD, rb, IN_SHARD_PAD), lambda i: (0, i, 0))],
        out_specs=pl.BlockSpec((rb, PROJ_W), lambda i: (i, 0)),
        out_shape=jax.ShapeDtypeStruct((D_MODEL, PROJ_W), g_in.dtype),
        compiler_params=_cparams("parallel", vmem=VMEM_MID),
    )(g_in)


def _w_in_from_segments(lo, gl):
    mine_of_theirs = {}
    for mine, theirs, width in W_IN_SEGMENTS:
        for k in range(width):
            mine_of_theirs[theirs + k] = mine + k
    src_of_dst = [None] * (N_SHARD * IN_SHARD_PAD)
    for s in range(N_SHARD):
        for l in range(IN_SHARD):
            j = mine_of_theirs[s * IN_SHARD + l]
            src_of_dst[s * IN_SHARD_PAD + l] = (j // LO_W, j % LO_W)
    blocks = _permute_pieces(src_of_dst)
    per_slot = IN_SHARD_PAD // LANES
    rb = RELAYOUT_ROWS

    def body(lo_ref, gl_ref, out_ref):
        planes = (lo_ref, gl_ref)
        for b, runs in enumerate(blocks):
            blk = _permuted_block(runs, lambda p, c: planes[p][:, c * LANES:(c + 1) * LANES], rb)
            c0 = (b % per_slot) * LANES
            out_ref[b // per_slot, :, c0:c0 + LANES] = blk

    half = pl.BlockSpec((rb, LO_W), lambda i: (i, 0))
    return pl.pallas_call(
        body, name="w_in_from_segments", grid=(D_MODEL // rb,),
        in_specs=[half, half],
        out_specs=pl.BlockSpec((N_SHARD, rb, IN_SHARD_PAD), lambda i: (0, i, 0)),
        out_shape=jax.ShapeDtypeStruct((N_SHARD, D_MODEL, IN_SHARD_PAD), F32),
        compiler_params=_cparams("parallel", vmem=VMEM_MID),
    )(lo, gl)


def _after(first, then):
    return lax.optimization_barrier((first, then))


class _ReduceGroup:
    def __init__(self, tag, first_collective_id, place):
        self.tag, self.first_id, self.place = tag, first_collective_id, place

    def start(self, local, tie):
        self.names = tuple(local)
        mine, tie = _after([local[n] for n in self.names], tie)
        self.mine = mine
        self.staged = _pair_exchange("pair_exchange_" + self.tag, self.first_id, mine)
        return tie

    def send(self, tie):
        staged, tie = _after(self.staged, tie)
        sums = [_pair_sum("pair_sum_" + n, self.place, g, st) for n, g, st in zip(self.names, self.mine, staged)]
        travel, tie = _after([s[0] for s in sums], tie)
        self.own = [s[1] for s in sums]
        self.got = _chip_exchange("chip_exchange_" + self.tag, self.first_id + 1, travel)
        return tie

    def finish(self, tie):
        got, tie = _after(self.got, tie)
        halves = [_final_sum("final_sum_" + n, self.place, o, r) for n, o, r in zip(self.names, self.own, got)]
        halves, tie = _after(halves, tie)
        summed = _pair_gather("pair_gather_" + self.tag, self.first_id + 2, halves)
        self.summed = dict(zip(self.names, summed))
        return tie


class _GradReducer:
    def __init__(self, place):
        self.early = _ReduceGroup("early", 2, place)
        self.late = _ReduceGroup("late", 5, place)

    @staticmethod
    def _slot_rows(a):
        return a.reshape(N_SHARD, a.shape[0] // N_SHARD, a.shape[1])

    def early_start(self, g, tie):
        return self.early.start({"w_mlp_down": self._slot_rows(g["w_mlp_down"]), "w_mlp_up": g["w_mlp_up"],
                                 "w_out": self._slot_rows(g["w_out"]), "w_mem_kv": self._slot_rows(g["w_mem_kv"]),
                                 "w_o_swa": g["w_o_swa"], "w_o_fox": g["w_o_fox"], "w_o_mem": g["w_o_mem"]}, tie)

    def early_send(self, tie):
        return self.early.send(tie)

    def early_finish(self, tie):
        return self.early.finish(tie)

    def late_start(self, g, tie):
        d_in = _w_in_from_segments(g["wc_lo"], g["wc_gl"])
        return self.late.start({"w_in": d_in}, tie)

    def late_send(self, tie):
        return self.late.send(tie)

    def late_finish(self, tie):
        return self.late.finish(tie)

    @property
    def summed(self):
        return {**self.early.summed, **self.late.summed}


def kernel(x, mem, g_mix, w_in, b_gate, b_forget, qn_swa, kn_swa, sink_swa, rel_bias, qn_fox, kn_fox, g_mem, w_mem_kv, qn_mem, kn_mem, w_o_swa, w_o_fox, w_o_mem, w_out, g_mlp, w_mlp_up, w_mlp_down, loss_target, m_g_mix, m_w_in, m_b_gate, m_b_forget, m_qn_swa, m_kn_swa, m_sink_swa, m_rel_bias, m_qn_fox, m_kn_fox, m_g_mem, m_w_mem_kv, m_qn_mem, m_kn_mem, m_w_o_swa, m_w_o_fox, m_w_o_mem, m_w_out, m_g_mlp, m_w_mlp_up, m_w_mlp_down, v_g_mix, v_w_in, v_b_gate, v_b_forget, v_qn_swa, v_kn_swa, v_sink_swa, v_rel_bias, v_qn_fox, v_kn_fox, v_g_mem, v_w_mem_kv, v_qn_mem, v_kn_mem, v_w_o_swa, v_w_o_fox, v_w_o_mem, v_w_out, v_g_mlp, v_w_mlp_up, v_w_mlp_down):
    given = dict(locals())
    W = {n: given[n] for n in WEIGHT_NAMES}
    M = {n: given["m_" + n] for n in WEIGHT_NAMES}
    V = {n: given["v_" + n] for n in WEIGHT_NAMES}
    pad_in = ((0, 0), (0, IN_SHARD_PAD - IN_SHARD))

    shards = [jnp.pad(w_in[0].astype(BF16), pad_in)] + [W[n][0].astype(BF16) for n in BIG_NAMES[1:]]
    slots = [jnp.broadcast_to(s[None], (N_SHARD,) + s.shape) for s in shards]
    (g_in,) = _all_gather_shards_async("all_gather_w_in", 1, slots[:1])
    small = {n: (W[n] if n == "rel_bias" else W[n].reshape(1, -1)) for n in SMALL_NAMES}
    h = _rmsnorm("rms_mix", x[0], small["g_mix"], min(512, x.shape[1]))
    g_in, late, h, (m_in, v_in) = lax.optimization_barrier((g_in, slots[1:], h, (M["w_in"][0], V["w_in"][0])))
    M["w_in"], V["w_in"] = m_in[None], v_in[None]
    g_kv, g_oa, g_of, g_om, g_out, g_up, g_down = _all_gather_shards_async("all_gather_weights_async", 8, late)

    place = jnp.stack([2 * lax.axis_index("x") + lax.axis_index("y"), lax.axis_index("c")]).astype(jnp.int32)
    reducer = _GradReducer(place)
    loss, grad_x, grads = _local_step(
        x[0], h, mem[0], loss_target[0], small, g_in, g_kv.reshape(D_MODEL, D_MODEL), (g_oa, g_of, g_om),
        g_out.reshape(D_MODEL, D_MODEL), g_up, g_down.reshape(D_FF, D_MODEL), reducer)

    out = {}

    def adamw_of(names, summed):
        for n in names:
            res = _adamw("adamw_" + n, W[n][0], summed[n], M[n][0], V[n][0])
            out[n] = [r.reshape(W[n].shape) for r in res]

    adamw_of(reducer.early.names, reducer.early.summed)
    shapes = {n: W[n].shape for n in SMALL_NAMES}
    packed = _small_allreduce_adamw(_pack_small(grads, loss), _pack_small(W), _pack_small(M), _pack_small(V))
    done_meanwhile = ([out[n] for n in reducer.early.names], packed)
    (early_out, packed), grad_x = reducer.late_finish((done_meanwhile, grad_x))
    for n, res in zip(reducer.early.names, early_out):
        out[n] = res
    adamw_of(reducer.late.names, reducer.late.summed)
    unpacked = [_unpack_small(p, shapes) for p in packed]
    for n in SMALL_NAMES:
        out[n] = [u[0][n] for u in unpacked]
    loss_total = unpacked[0][1]

    return (loss_total, grad_x.reshape(x.shape),
            *[out[n][0] for n in WEIGHT_NAMES], *[out[n][1] for n in WEIGHT_NAMES],
            *[out[n][2] for n in WEIGHT_NAMES], *[out[n][3] for n in WEIGHT_NAMES])
```

```python
import functools
import math

import jax
import jax.numpy as jnp
from jax import lax
from jax.experimental import pallas as pl
from jax.experimental.pallas import tpu as pltpu
from jax.experimental.pallas import tpu_sc as plsc

F32 = jnp.float32
BF16 = jnp.bfloat16

D_MODEL = 1024
N_MEM = 256
SWA_HEADS = 8
SWA_KV_HEADS = 2
SWA_HEAD_DIM = 64
WINDOW = 128
FOX_HEADS = 8
FOX_HEAD_DIM = 64
MEM_HEADS = 4
MEM_HEAD_DIM = 128
D_FF = 4 * D_MODEL
REL_BUCKETS = 32
REL_MAX_DIST = 128
EPS = 1e-6
NEG = -1e30
GATE_W = 3 * D_MODEL
IN_WIDTH = 5896
N_SHARD = 4
IN_SHARD = IN_WIDTH // N_SHARD
IN_SHARD_PAD = 1536

ADAM_LR = 0.001
ADAM_B1 = 0.9
ADAM_B2 = 0.999
ADAM_EPS = 1e-08
ADAM_WD = 0.01
ADAM_STEP = 10

LANES = 128
V7X_VMEM_BYTES = 64 * 1024 * 1024
MIB = 1024 * 1024
VMEM_SMALL, VMEM_MID, VMEM_BIG, VMEM_MAX = 24 * MIB, 40 * MIB, 48 * MIB, 56 * MIB

C_QA, C_QF, C_KF, C_VF, C_QM, C_KA, C_VA, C_FL, C_GL = 0, 512, 1024, 1536, 2048, 2560, 2688, 2816, 3072
LO_W = 3072
PROJ_W = 6144

NN = (((1,), (0,)), ((), ()))
NT = (((1,), (1,)), ((), ()))
TN = (((0,), (0,)), ((), ()))


def _dot(a, b, dims=NN):
    return lax.dot_general(a, b, dims, preferred_element_type=F32)


def _cparams(*sem, vmem=VMEM_SMALL):
    return pltpu.CompilerParams(dimension_semantics=sem, vmem_limit_bytes=vmem)


def _split3(a):
    hi = a.astype(BF16)
    r1 = a - hi.astype(F32)
    mid = r1.astype(BF16)
    lo = (r1 - mid.astype(F32)).astype(BF16)
    return hi, mid, lo


def _group_mean(a, g2):
    hi = a.astype(BF16)
    mid = (a - hi.astype(F32)).astype(BF16)
    return _dot(jnp.concatenate([hi, mid], axis=1), g2)


def _dot3_left(g, a):
    hi, mid, lo = _split3(a)
    return _dot(g, hi) + _dot(g, mid) + _dot(g, lo)


def _group_mean_matrix(d):
    r = jnp.arange(LANES)
    g = jnp.where((r[:, None] // d) == (r[None, :] // d), 1.0 / d, 0.0).astype(BF16)
    return jnp.concatenate([g, g], axis=0)


def _lane(shape):
    return lax.broadcasted_iota(jnp.int32, shape, len(shape) - 1)


def _matmul(name, a, b, *, dims, grid, a_spec, b_spec, acc_shape, outs, epilogue, extra=(), vmem=VMEM_BIG):
    nk = grid[2]
    n_extra = len(extra)

    def body(a_ref, b_ref, *rest):
        extra_refs = rest[:n_extra]
        out_refs = rest[n_extra:n_extra + len(outs)]
        i, j, k = pl.program_id(0), pl.program_id(1), pl.program_id(2)
        part = _dot(a_ref[...].astype(BF16), b_ref[...].astype(BF16), dims)
        if nk == 1:
            epilogue(part, extra_refs, out_refs, (i, j))
            return
        acc_ref = rest[-1]

        @pl.when(k == 0)
        def _():
            acc_ref[...] = part

        @pl.when((k > 0) & (k < nk - 1))
        def _():
            acc_ref[...] += part

        @pl.when(k == nk - 1)
        def _():
            epilogue(acc_ref[...] + part, extra_refs, out_refs, (i, j))

    res = pl.pallas_call(
        body,
        name=name,
        grid=grid,
        in_specs=[a_spec, b_spec] + [s for _, s in extra],
        out_specs=[s for _, s in outs],
        out_shape=[s for s, _ in outs],
        scratch_shapes=[pltpu.VMEM(acc_shape, F32)] if nk > 1 else [],
        compiler_params=_cparams("arbitrary", "arbitrary", "arbitrary", vmem=vmem),
    )(a, b, *[x for x, _ in extra])
    return res


def _epi_store(acc, extra_refs, out_refs, ij):
    out_refs[0][...] = acc.astype(out_refs[0].dtype)


def _rms_rows(x, g):
    r = lax.rsqrt(jnp.mean(x * x, axis=-1, keepdims=True) + EPS)
    return x * r, r


def _rmsnorm_bwd_rows(dh, x, g):
    xhat, r = _rms_rows(x, g)
    dxh = dh * g
    dx = r * (dxh - xhat * jnp.mean(dxh * xhat, axis=-1, keepdims=True))
    return dx, jnp.sum(dh * xhat, axis=0, keepdims=True)


def _rmsnorm(name, x, g, tb):
    T, Dm = x.shape

    def body(x_ref, g_ref, o_ref):
        xhat, _ = _rms_rows(x_ref[...], None)
        o_ref[...] = (xhat * g_ref[...]).astype(o_ref.dtype)

    return pl.pallas_call(
        body, name=name, grid=(T // tb,),
        in_specs=[pl.BlockSpec((tb, Dm), lambda i: (i, 0)), pl.BlockSpec((1, Dm), lambda i: (0, 0))],
        out_specs=pl.BlockSpec((tb, Dm), lambda i: (i, 0)),
        out_shape=jax.ShapeDtypeStruct((T, Dm), BF16),
        compiler_params=_cparams("parallel"),
    )(x, g)


def _head_norm(x, gm, gain):
    ms = _group_mean(x * x, gm)
    r = lax.rsqrt(ms + EPS)
    return x * r * gain, x * r


def _head_norm_bwd(dy, x, gm, gain):
    ms = _group_mean(x * x, gm)
    r = lax.rsqrt(ms + EPS)
    xhat = x * r
    dxh = dy * gain
    dx = r * (dxh - xhat * _group_mean(dxh * xhat, gm))
    return dx, jnp.sum(dy * xhat, axis=0, keepdims=True)


def _log_sigmoid(z):
    return jnp.minimum(z, 0.0) - jnp.log(1.0 + jnp.exp(-jnp.abs(z)))


def _prep_fwd(proj, gains, bfor, tril, gm64, gm128, T, tb):
    nb = T // tb

    def body(qa_ref, qf_ref, kf_ref, vf_ref, qm_ref, ka_ref, va_ref, fl_ref, gains_ref, bfor_ref, tril_ref,
             gm64_ref, gm128_ref,
             qa_o, qf_o, kf_o, vf_o, qm_o, kad_o, vad_o, qaug_o, kaug_o, carry):
        i = pl.program_id(0)
        gm64v = gm64_ref[...]
        gm128v = gm128_ref[...]
        lane = _lane((tb, LANES))

        def norm512(src, dst, row, gm, scale=1.0):
            gain = gains_ref[row:row + 1, :]
            for c in range(4):
                sl = slice(c * LANES, (c + 1) * LANES)
                y, _ = _head_norm(src[:, sl], gm, gain)
                dst[:, sl] = (y * scale).astype(dst.dtype)

        norm512(qa_ref, qa_o, 0, gm64v)
        norm512(qf_ref, qf_o, 2, gm64v, FOX_SCALE)
        norm512(kf_ref, kf_o, 3, gm64v)
        norm512(qm_ref, qm_o, 4, gm128v)
        vf_o[...] = vf_ref[...].astype(vf_o.dtype)

        ka_n, _ = _head_norm(ka_ref[...], gm64v, gains_ref[1:2, :])
        ka_r = pltpu.roll(ka_n, 64, 1)
        va = va_ref[...]
        va_r = pltpu.roll(va, 64, 1)
        lo = lane < 64
        kad_o[0] = jnp.where(lo, ka_n, ka_r).astype(kad_o.dtype)
        kad_o[1] = jnp.where(lo, ka_r, ka_n).astype(kad_o.dtype)
        vad_o[0] = jnp.where(lo, va, va_r).astype(vad_o.dtype)
        vad_o[1] = jnp.where(lo, va_r, va).astype(vad_o.dtype)

        @pl.when(i == 0)
        def _():
            carry[...] = jnp.zeros_like(carry)

        logf = jnp.where(lane < FOX_HEADS, _log_sigmoid(fl_ref[...] + bfor_ref[...]), 0.0)
        c = _dot3_left(tril_ref[...], logf) + carry[0:1, :]
        carry[...] = jnp.broadcast_to(c[tb - 1:tb, :], carry.shape)
        for pair in range(FOX_HEADS // 2):
            qaug = jnp.zeros((tb, LANES), F32)
            kaug = jnp.zeros((tb, LANES), F32)
            for sub in range(2):
                col = jnp.sum(jnp.where(lane == 2 * pair + sub, c, 0.0), axis=1, keepdims=True)
                pieces = [p.astype(F32) for p in _split3(col)]
                base = AUG_STRIDE * sub
                for e in range(3):
                    qaug = jnp.where(lane == base + AUG_C + e, pieces[e], qaug)
                    kaug = jnp.where(lane == base + AUG_NEG_C + e, -pieces[e], kaug)
                qaug = jnp.where((lane >= base + AUG_NEG_C) & (lane < base + AUG_NEG_C + 3), 1.0, qaug)
                ones_k = ((lane >= base + AUG_C) & (lane < base + AUG_C + 3)) | (
                    (lane >= base + AUG_STAT) & (lane < base + AUG_STAT + 3))
                kaug = jnp.where(ones_k, 1.0, kaug)
            sl = slice(pair * LANES, (pair + 1) * LANES)
            qaug_o[:, sl] = qaug.astype(BF16)
            kaug_o[:, sl] = kaug.astype(BF16)

    def seg(width, start):
        return pl.BlockSpec((tb, width), lambda i, s=start // width: (i, s))

    const = lambda shape: pl.BlockSpec(shape, lambda i: tuple(0 for _ in shape))
    rows512 = pl.BlockSpec((tb, 512), lambda i: (i, 0))
    outs = pl.pallas_call(
        body, name="prep_fwd", grid=(nb,),
        in_specs=[seg(512, C_QA), seg(512, C_QF), seg(512, C_KF), seg(512, C_VF), seg(512, C_QM),
                  seg(128, C_KA), seg(128, C_VA), seg(128, C_FL),
                  const((8, LANES)), const((1, LANES)), const((tb, tb)), const((2 * LANES, LANES)), const((2 * LANES, LANES))],
        out_specs=[rows512, rows512, rows512, rows512, rows512,
                   pl.BlockSpec((2, tb, LANES), lambda i: (0, i, 0)), pl.BlockSpec((2, tb, LANES), lambda i: (0, i, 0)),
                   rows512, rows512],
        out_shape=[jax.ShapeDtypeStruct((T, 512), BF16)] * 5
        + [jax.ShapeDtypeStruct((2, T, LANES), BF16)] * 2
        + [jax.ShapeDtypeStruct((T, 512), BF16)] * 2,
        scratch_shapes=[pltpu.VMEM((8, LANES), F32)],
        compiler_params=_cparams("arbitrary", vmem=VMEM_MID),
    )(proj, proj, proj, proj, proj, proj, proj, proj, gains, bfor, tril, gm64, gm128)
    return outs


def _prep_bwd(proj, dqa, dkad, dvad, dqf, dkf, dvf, dqm, dqf_aug, dkf_aug, gains, bfor, triu, gm64, gm128, T, tb):
    nb = T // tb

    def body(qa_ref, qf_ref, kf_ref, qm_ref, ka_ref, fl_ref,
             dqa_ref, dkad_ref, dvad_ref, dqf_ref, dkf_ref, dvf_ref, dqm_ref, dqfa_ref, dkfa_ref,
             gains_ref, bfor_ref, triu_ref, gm64_ref, gm128_ref,
             dlo_o, gacc_o, carry):
        i = pl.program_id(0)
        gm64v = gm64_ref[...]
        gm128v = gm128_ref[...]
        lane = _lane((tb, LANES))

        @pl.when(i == 0)
        def _():
            carry[...] = jnp.zeros_like(carry)
            gacc_o[...] = jnp.zeros_like(gacc_o)

        def norm512_bwd(dsrc, xsrc, col0, row, gm):
            gain = gains_ref[row:row + 1, :]
            gsum = jnp.zeros((1, LANES), F32)
            for c in range(4):
                sl = slice(c * LANES, (c + 1) * LANES)
                dx, dg = _head_norm_bwd(dsrc[:, sl], xsrc[:, sl], gm, gain)
                dlo_o[:, col0 + c * LANES:col0 + (c + 1) * LANES] = dx.astype(dlo_o.dtype)
                gsum = gsum + dg
            gacc_o[row:row + 1, :] += gsum

        norm512_bwd(dqa_ref, qa_ref, C_QA, 0, gm64v)
        norm512_bwd(dqf_ref, qf_ref, C_QF, 2, gm64v)
        norm512_bwd(dkf_ref, kf_ref, C_KF, 3, gm64v)
        norm512_bwd(dqm_ref, qm_ref, C_QM, 4, gm128v)
        dlo_o[:, C_VF:C_VF + 512] = dvf_ref[...].astype(dlo_o.dtype)

        lo = lane < 64

        def fold(ref):
            f0 = ref[0] + pltpu.roll(ref[0], 64, 1)
            f1 = ref[1] + pltpu.roll(ref[1], 64, 1)
            return jnp.where(lo, f0, f1)

        dka, dg = _head_norm_bwd(fold(dkad_ref), ka_ref[...], gm64v, gains_ref[1:2, :])
        gacc_o[1:2, :] += dg
        dlo_o[:, C_KA:C_KA + LANES] = dka.astype(dlo_o.dtype)
        dlo_o[:, C_VA:C_VA + LANES] = fold(dvad_ref).astype(dlo_o.dtype)

        dc = jnp.zeros((tb, LANES), F32)
        for pair in range(FOX_HEADS // 2):
            sl = slice(pair * LANES, (pair + 1) * LANES)
            rows_sum, cols_sum = dqfa_ref[:, sl], dkfa_ref[:, sl]
            for sub in range(2):
                diff = (jnp.where(lane == AUG_STRIDE * sub + AUG_C, rows_sum, 0.0)
                        - jnp.where(lane == AUG_STRIDE * sub + AUG_NEG_C, cols_sum, 0.0))
                dc = jnp.where(lane == 2 * pair + sub, jnp.sum(diff, axis=1, keepdims=True), dc)
        dlogf = _dot3_left(triu_ref[...], dc) + carry[0:1, :]
        carry[...] = jnp.broadcast_to(dlogf[0:1, :], carry.shape)
        z = fl_ref[...] + bfor_ref[...]
        dfl = jnp.where(lane < FOX_HEADS, dlogf / (1.0 + jnp.exp(z)), 0.0)
        gacc_o[5:6, :] += jnp.sum(dfl, axis=0, keepdims=True)
        dlo_o[:, C_FL:C_FL + LANES] = dfl.astype(dlo_o.dtype)
        dlo_o[:, C_FL + LANES:C_FL + 2 * LANES] = jnp.zeros((tb, LANES), dlo_o.dtype)

    rev = lambda i: nb - 1 - i

    def seg(width, start):
        return pl.BlockSpec((tb, width), lambda i, s=start // width: (rev(i), s))

    const = lambda shape: pl.BlockSpec(shape, lambda i: tuple(0 for _ in shape))
    rows512 = pl.BlockSpec((tb, 512), lambda i: (rev(i), 0))
    dup = pl.BlockSpec((2, tb, LANES), lambda i: (0, rev(i), 0))
    return pl.pallas_call(
        body, name="prep_bwd", grid=(nb,),
        in_specs=[seg(512, C_QA), seg(512, C_QF), seg(512, C_KF), seg(512, C_QM), seg(128, C_KA), seg(128, C_FL),
                  rows512, dup, dup, rows512, rows512, rows512, rows512, rows512, rows512,
                  const((8, LANES)), const((1, LANES)), const((tb, tb)), const((2 * LANES, LANES)), const((2 * LANES, LANES))],
        out_specs=[pl.BlockSpec((tb, LO_W), lambda i: (rev(i), 0)), const((8, LANES))],
        out_shape=[jax.ShapeDtypeStruct((T, LO_W), BF16), jax.ShapeDtypeStruct((8, LANES), F32)],
        scratch_shapes=[pltpu.VMEM((8, LANES), F32)],
        compiler_params=_cparams("arbitrary", vmem=VMEM_MID),
    )(proj, proj, proj, proj, proj, proj, dqa, dkad, dvad, dqf, dkf, dvf, dqm, dqf_aug, dkf_aug,
      gains, bfor, triu, gm64, gm128)


FOX_SCALE = FOX_HEAD_DIM ** -0.5
AUG_STRIDE = 16
AUG_C = 0
AUG_NEG_C = 3
AUG_STAT = 6
FOX_TQ, FOX_TK = 1024, 1024
FOX_BWD_TQ, FOX_BWD_TK = 1024, 1024
FOX_DIAGONAL_PARTS = 4


def _fox_head_mask(sub, rows):
    lane = _lane((rows, 2 * LANES))
    main = (lane >= 64 * sub) & (lane < 64 * sub + 64)
    aug = (lane >= LANES + AUG_STRIDE * sub) & (lane < LANES + AUG_STRIDE * (sub + 1))
    return main | aug


def _fox_pieces(diagonal, tq, tk):
    if diagonal and tq == tk and tq >= FOX_DIAGONAL_PARTS * LANES:
        step = tq // FOX_DIAGONAL_PARTS
        return [(n * step, (n + 1) * step, (n + 1) * step) for n in range(FOX_DIAGONAL_PARTS)]
    return [(0, tq, tk)]


def _fox_fwd(q, qaug, k, kaug, v, T, tq, tk):
    nq, nk = T // tq, T // tk
    rep = tk // LANES
    last_of = lambda i: (i * tq + tq - 1) // tk

    def body(q_ref, qa_ref, k_ref, ka_ref, v_ref, o_ref, qab_ref, m_s, acc_s):
        p_, i, j = pl.program_id(0), pl.program_id(1), pl.program_id(2)
        last = last_of(i)

        @pl.when(j == 0)
        def _():
            m_s[...] = jnp.full(m_s.shape, NEG, F32)
            acc_s[...] = jnp.zeros_like(acc_s)

        def step(diagonal):
            k2 = jnp.concatenate([k_ref[...], ka_ref[...]], axis=1)
            v2 = jnp.concatenate([v_ref[...], ka_ref[...]], axis=1)
            pieces = _fox_pieces(diagonal, tq, tk)
            work = []
            for r0, r1, nc in pieces:
                rows = slice(r0, r1)
                q2 = jnp.concatenate([q_ref[rows, :], qa_ref[rows, :]], axis=1)
                for sub in range(2):
                    qh = jnp.where(_fox_head_mask(sub, r1 - r0), q2, jnp.zeros_like(q2))
                    work.append((rows, r0, r1 - r0, nc, sub, _dot(qh, k2[:nc], NT)))
            for rows, r0, nr, nc, sub, s in work:
                if diagonal:
                    causal = (lax.broadcasted_iota(jnp.int32, (nr, nc), 1) + j * tk
                              <= lax.broadcasted_iota(jnp.int32, (nr, nc), 0) + (r0 + i * tq))
                    s = jnp.where(causal, s, NEG)
                m_prev = m_s[sub, rows, :]
                m_next = jnp.maximum(m_prev, jnp.max(s, axis=1, keepdims=True))
                p = jnp.exp(s - jnp.tile(m_next, (1, nc // LANES)))
                alpha = jnp.exp(m_prev - m_next)
                m_s[sub, rows, :] = m_next
                acc_s[sub, rows, :] = acc_s[sub, rows, :] * jnp.tile(alpha, (1, 2)) + _dot(p.astype(BF16), v2[:nc])

        @pl.when(j == last)
        def _():
            step(True)

        @pl.when(j < last)
        def _():
            step(False)

        @pl.when(j == nk - 1)
        def _():
            lane = _lane((tq, LANES))
            outs = []
            qab = qa_ref[...].astype(F32)
            for sub in range(2):
                acc = acc_s[sub]
                base = AUG_STRIDE * sub
                l = jnp.sum(jnp.where(lane == base + AUG_C, acc[:, LANES:], 0.0), axis=1, keepdims=True)
                outs.append(acc[:, :LANES] / l)
                lse = jnp.max(m_s[sub], axis=1, keepdims=True) + jnp.log(l)
                pieces = _split3(-lse)
                for e in range(3):
                    qab = jnp.where(lane == base + AUG_STAT + e, pieces[e].astype(F32), qab)
            o_ref[...] = jnp.where(lane < 64, outs[0], outs[1]).astype(o_ref.dtype)
            qab_ref[...] = qab.astype(BF16)

    qspec = pl.BlockSpec((tq, LANES), lambda p, i, j: (i, p))
    kspec = pl.BlockSpec((tk, LANES), lambda p, i, j: (jnp.minimum(j, last_of(i)), p))
    return pl.pallas_call(
        body, name="fox_fwd", grid=(4, nq, nk),
        in_specs=[qspec, qspec, kspec, kspec, kspec],
        out_specs=[qspec, qspec],
        out_shape=[jax.ShapeDtypeStruct((T, 512), BF16), jax.ShapeDtypeStruct((T, 512), BF16)],
        scratch_shapes=[pltpu.VMEM((2, tq, LANES), F32), pltpu.VMEM((2, tq, 2 * LANES), F32)],
        compiler_params=_cparams("parallel", "parallel", "arbitrary", vmem=VMEM_BIG),
    )(q, qaug, k, kaug, v)


def _fox_bwd(q, qaug, k, kaug, v, do, doaug, T, tq, tk):
    nq, nk = T // tq, T // tk
    first_of = lambda j: (j * tk) // tq

    def body(q_ref, qa_ref, k_ref, ka_ref, v_ref, do_ref, doa_ref,
             dq_ref, dqa_ref, dk_ref, dka_ref, dv_ref, dk_s, dv_s):
        p_, j, i = pl.program_id(0), pl.program_id(1), pl.program_id(2)
        masked = i * tq < (j + 1) * tk - 1

        @pl.when((j == 0) & (i == 0))
        def _():
            dq_ref[...] = jnp.zeros_like(dq_ref)
            dqa_ref[...] = jnp.zeros_like(dqa_ref)

        @pl.when(i == 0)
        def _():
            dk_s[...] = jnp.zeros_like(dk_s)
            dv_s[...] = jnp.zeros_like(dv_s)

        def step(diagonal):
            k2 = jnp.concatenate([k_ref[...], ka_ref[...]], axis=1)
            v2 = jnp.concatenate([v_ref[...], ka_ref[...]], axis=1)
            work = []
            for r0, r1, nc in _fox_pieces(diagonal, tq, tk):
                rows = slice(r0, r1)
                q2 = jnp.concatenate([q_ref[rows, :], qa_ref[rows, :]], axis=1)
                do2 = jnp.concatenate([do_ref[rows, :], doa_ref[rows, :]], axis=1)
                for sub in range(2):
                    hm = _fox_head_mask(sub, r1 - r0)
                    qh = jnp.where(hm, q2, jnp.zeros_like(q2))
                    doh = jnp.where(hm, do2, jnp.zeros_like(do2))
                    s = _dot(qh, k2[:nc], NT)
                    dp = _dot(doh, v2[:nc], NT)
                    work.append((r0, r1 - r0, nc, sub, qh, doh, s, dp))
            dqs = {}
            for r0, nr, nc, sub, qh, doh, s, dp in work:
                if diagonal:
                    causal = (lax.broadcasted_iota(jnp.int32, (nr, nc), 1) + j * tk
                              <= lax.broadcasted_iota(jnp.int32, (nr, nc), 0) + (r0 + i * tq))
                    s = jnp.where(causal, s, NEG)
                p = jnp.exp(s)
                dsb = (p * dp).astype(BF16)
                dv_s[0:nc, :] += _dot(p.astype(BF16), doh[:, :LANES], TN)
                dk_s[0:nc, :] += _dot(dsb, qh, TN)
                dqs[(r0, sub)] = _dot(dsb, k2[:nc])
            for r0, r1, nc in _fox_pieces(diagonal, tq, tk):
                dq2 = jnp.where(_fox_head_mask(0, r1 - r0), dqs[(r0, 0)], dqs[(r0, 1)])
                qrows = pl.ds(pl.multiple_of(i * tq + r0, r1 - r0), r1 - r0)
                dq_ref[qrows, :] += dq2[:, :LANES] * FOX_SCALE
                dqa_ref[qrows, :] += dq2[:, LANES:]

        @pl.when((i >= first_of(j)) & masked)
        def _():
            step(True)

        @pl.when((i >= first_of(j)) & jnp.logical_not(masked))
        def _():
            step(False)

        @pl.when(i == nq - 1)
        def _():
            dk_ref[...] = dk_s[:, :LANES]
            dka_ref[...] = dk_s[:, LANES:]
            dv_ref[...] = dv_s[...]

    qspec = pl.BlockSpec((tq, LANES), lambda p, j, i: (jnp.maximum(i, first_of(j)), p))
    kspec = pl.BlockSpec((tk, LANES), lambda p, j, i: (j, p))
    resident = pl.BlockSpec((T, LANES), lambda p, j, i: (0, p))
    return pl.pallas_call(
        body, name="fox_bwd", grid=(4, nk, nq),
        in_specs=[qspec, qspec, kspec, kspec, kspec, qspec, qspec],
        out_specs=[resident, resident, kspec, kspec, kspec],
        out_shape=[jax.ShapeDtypeStruct((T, 512), F32)] * 5,
        scratch_shapes=[pltpu.VMEM((tk, 2 * LANES), F32), pltpu.VMEM((tk, LANES), F32)],
        compiler_params=_cparams("arbitrary", "arbitrary", "arbitrary", vmem=VMEM_BIG),
    )(q, qaug, k, kaug, v, do, doaug)


SWA_SUB = 4
SWA_TB = SWA_SUB * WINDOW


def _t5_bucket_matrix():
    t = jnp.arange(WINDOW)[:, None] + WINDOW
    s = jnp.arange(2 * WINDOW)[None, :]
    max_exact = REL_BUCKETS // 2
    d = jnp.maximum(t - s, 0)
    df = jnp.maximum(d, 1).astype(F32)
    large = max_exact + (jnp.log(df / max_exact) / math.log(REL_MAX_DIST / max_exact)
                         * (REL_BUCKETS - max_exact)).astype(jnp.int32)
    large = jnp.minimum(large, REL_BUCKETS - 1)
    return jnp.where(d < max_exact, d, large).astype(jnp.int32)


def _swa_bias(rel_bias, bucket):
    def body(rel_ref, bucket_ref, o_ref):
        b = bucket_ref[...]
        for h in range(SWA_HEADS):
            acc = jnp.zeros(b.shape, F32)
            for r in range(REL_BUCKETS):
                acc = jnp.where(b == r, rel_ref[r, h], acc)
            o_ref[h] = acc

    return pl.pallas_call(
        body, name="swa_bias",
        in_specs=[pl.BlockSpec(memory_space=pltpu.SMEM), pl.BlockSpec(memory_space=pltpu.VMEM)],
        out_specs=pl.BlockSpec(memory_space=pltpu.VMEM),
        out_shape=jax.ShapeDtypeStruct((SWA_HEADS, WINDOW, 2 * WINDOW), F32),
    )(rel_bias, bucket)


def _swa_bias_bwd(dbias, bucket):
    def body(db_ref, bucket_ref, o_ref):
        b = bucket_ref[...]
        lane = _lane((1, LANES))
        for r in range(REL_BUCKETS):
            row = jnp.zeros((1, LANES), F32)
            for h in range(SWA_HEADS):
                part = jnp.sum(jnp.where(b == r, db_ref[h], 0.0), axis=0, keepdims=True)
                tot = jnp.sum(part, axis=1, keepdims=True)
                row = jnp.where(lane == h, tot, row)
            o_ref[r:r + 1, :] = row

    return pl.pallas_call(
        body, name="swa_bias_bwd",
        in_specs=[pl.BlockSpec(memory_space=pltpu.VMEM), pl.BlockSpec(memory_space=pltpu.VMEM)],
        out_specs=pl.BlockSpec(memory_space=pltpu.VMEM),
        out_shape=jax.ShapeDtypeStruct((REL_BUCKETS, LANES), F32),
    )(dbias, bucket)


SWA_GROUP = SWA_HEADS // SWA_KV_HEADS


def _swa_valid(r, i):
    t = (lax.broadcasted_iota(jnp.int32, (SWA_GROUP * WINDOW, 2 * WINDOW), 0) & (WINDOW - 1)) + WINDOW
    s = lax.broadcasted_iota(jnp.int32, (SWA_GROUP * WINDOW, 2 * WINDOW), 1)
    dist = t - s
    band = (dist >= 0) & (dist < WINDOW)
    if r == 0:
        band = band & ((s >= WINDOW) | (i > 0))
    return band


def _swa_stack(blk):
    lane = _lane((WINDOW, LANES))
    parts = []
    for g in range(SWA_GROUP):
        b = blk[:, LANES * (g // 2):LANES * (g // 2 + 1)]
        parts.append(jnp.where((lane >= 64) if g % 2 else (lane < 64), b, jnp.zeros_like(b)))
    return jnp.concatenate(parts, axis=0)


def _swa_unstack(st):
    lane = _lane((WINDOW, LANES))
    W = WINDOW
    return jnp.concatenate([jnp.where(lane < 64, st[2 * b * W:(2 * b + 1) * W], st[(2 * b + 1) * W:(2 * b + 2) * W])
                            for b in range(2)], axis=1)


def _swa_sink_column(sink_ref, kvh):
    row = lax.broadcasted_iota(jnp.int32, (SWA_GROUP * WINDOW, 1), 0)
    col = jnp.full((SWA_GROUP * WINDOW, 1), sink_ref[SWA_GROUP * kvh + SWA_GROUP - 1], F32)
    for g in range(SWA_GROUP - 2, -1, -1):
        col = jnp.where(row < (g + 1) * WINDOW, sink_ref[SWA_GROUP * kvh + g], col)
    return col


def _swa_specs(T):
    W = WINDOW
    qspec = pl.BlockSpec((SWA_TB, 2 * LANES), lambda h, i: (i, h))
    own = pl.BlockSpec((None, SWA_TB, LANES), lambda h, i: (h, i, 0))
    prev = pl.BlockSpec((None, W, LANES), lambda h, i: (h, jnp.maximum(SWA_SUB * i - 1, 0), 0))
    stat = pl.BlockSpec((SWA_GROUP, SWA_TB, LANES), lambda h, i: (h, i, 0))
    bias = pl.BlockSpec((None, SWA_GROUP * W, 2 * W), lambda h, i: (h, 0, 0))
    return qspec, own, prev, stat, bias


def _swa_fwd(sinks, q, kad, vad, bias, T):
    nb = T // SWA_TB
    scale = SWA_HEAD_DIM ** -0.5
    W = WINDOW

    def body(sink_ref, q_ref, k_ref, kp_ref, v_ref, vp_ref, bias_ref, o_ref, lse_ref):
        kvh, i = pl.program_id(0), pl.program_id(1)
        sink = _swa_sink_column(sink_ref, kvh)
        for r in range(SWA_SUB):
            rs = slice(r * W, (r + 1) * W)
            ps = slice((r - 1) * W, r * W)
            k_own, v_own = k_ref[rs, :], v_ref[rs, :]
            k_prev = kp_ref[...] if r == 0 else k_ref[ps, :]
            v_prev = vp_ref[...] if r == 0 else v_ref[ps, :]
            qs = _swa_stack(q_ref[rs, :])
            s = jnp.concatenate([_dot(qs, k_prev, NT), _dot(qs, k_own, NT)], axis=1) * scale + bias_ref[...]
            s = jnp.where(_swa_valid(r, i), s, NEG)
            m = jnp.maximum(jnp.max(s, axis=1, keepdims=True), sink)
            p = jnp.exp(s - m)
            denom = jnp.sum(p, axis=1, keepdims=True) + jnp.exp(sink - m)
            pn = (p / denom).astype(BF16)
            o_ref[rs, :] = _swa_unstack(_dot(pn[:, :W], v_prev) + _dot(pn[:, W:], v_own)).astype(o_ref.dtype)
            lse = m + jnp.log(denom)
            for g in range(SWA_GROUP):
                lse_ref[g, rs, :] = jnp.broadcast_to(lse[g * W:(g + 1) * W], (W, LANES))

    qspec, own, prev, stat, bspec = _swa_specs(T)
    return pl.pallas_call(
        body, name="swa_fwd", grid=(SWA_KV_HEADS, nb),
        in_specs=[pl.BlockSpec(memory_space=pltpu.SMEM), qspec, own, prev, own, prev, bspec],
        out_specs=[qspec, stat],
        out_shape=[jax.ShapeDtypeStruct((T, 512), BF16), jax.ShapeDtypeStruct((SWA_HEADS, T, LANES), F32)],
        compiler_params=_cparams("parallel", "parallel", vmem=VMEM_MID),
    )(sinks, q, kad, kad, vad, vad, bias.reshape(SWA_KV_HEADS, SWA_GROUP * W, 2 * W))


def _swa_bwd(sinks, q, kad, vad, bias, do, lse, delta, T):
    nb = T // SWA_TB
    scale = SWA_HEAD_DIM ** -0.5
    W = WINDOW

    def body(sink_ref, q_ref, k_ref, kp_ref, v_ref, vp_ref, bias_ref, do_ref, lse_ref, dl_ref,
             dq_ref, dkad_ref, dvad_ref, dbias_ref, dsk_ref):
        kvh, i = pl.program_id(0), pl.program_id(1)
        sink = _swa_sink_column(sink_ref, kvh)

        @pl.when((kvh == 0) & (i == 0))
        def _():
            dkad_ref[...] = jnp.zeros_like(dkad_ref)
            dvad_ref[...] = jnp.zeros_like(dvad_ref)

        @pl.when(i == 0)
        def _():
            dbias_ref[...] = jnp.zeros_like(dbias_ref)
            dsk_ref[...] = jnp.zeros_like(dsk_ref)

        for r in range(SWA_SUB):
            rs = slice(r * W, (r + 1) * W)
            ps = slice((r - 1) * W, r * W)
            k_own, v_own = k_ref[rs, :], v_ref[rs, :]
            k_prev = kp_ref[...] if r == 0 else k_ref[ps, :]
            v_prev = vp_ref[...] if r == 0 else v_ref[ps, :]
            qs = _swa_stack(q_ref[rs, :])
            dos = _swa_stack(do_ref[rs, :])
            lse_b = jnp.concatenate([lse_ref[g, rs, :] for g in range(SWA_GROUP)], axis=0)
            dl_b = jnp.concatenate([dl_ref[g, rs, :] for g in range(SWA_GROUP)], axis=0)
            s = jnp.concatenate([_dot(qs, k_prev, NT), _dot(qs, k_own, NT)], axis=1) * scale + bias_ref[...]
            s = jnp.where(_swa_valid(r, i), s, NEG)
            p = jnp.exp(s - jnp.tile(lse_b, (1, 2)))
            dp = jnp.concatenate([_dot(dos, v_prev, NT), _dot(dos, v_own, NT)], axis=1)
            ds = p * (dp - jnp.tile(dl_b, (1, 2)))
            sink_term = jnp.exp(sink - lse_b) * dl_b
            for g in range(SWA_GROUP):
                dbias_ref[g] += ds[g * W:(g + 1) * W]
                dsk_ref[g:g + 1, :] += jnp.sum(sink_term[g * W:(g + 1) * W], axis=0, keepdims=True)
            dsb = ds.astype(BF16)
            pb = p.astype(BF16)
            dq_ref[rs, :] = _swa_unstack((_dot(dsb[:, :W], k_prev) + _dot(dsb[:, W:], k_own)) * scale)
            own_row = pl.multiple_of(i * SWA_TB + r * W, W)
            dkad_ref[kvh, pl.ds(own_row, W), :] += _dot(dsb[:, W:], qs, TN) * scale
            dvad_ref[kvh, pl.ds(own_row, W), :] += _dot(pb[:, W:], dos, TN)
            dk_prev = _dot(dsb[:, :W], qs, TN) * scale
            dv_prev = _dot(pb[:, :W], dos, TN)
            if r == 0:
                @pl.when(i > 0)
                def _():
                    prev_row = pl.multiple_of(i * SWA_TB - W, W)
                    dkad_ref[kvh, pl.ds(prev_row, W), :] += dk_prev
                    dvad_ref[kvh, pl.ds(prev_row, W), :] += dv_prev
            else:
                prev_row = pl.multiple_of(i * SWA_TB + (r - 1) * W, W)
                dkad_ref[kvh, pl.ds(prev_row, W), :] += dk_prev
                dvad_ref[kvh, pl.ds(prev_row, W), :] += dv_prev

    qspec, own, prev, stat, bspec = _swa_specs(T)
    full = pl.BlockSpec((SWA_KV_HEADS, T, LANES), lambda h, i: (0, 0, 0))
    return pl.pallas_call(
        body, name="swa_bwd", grid=(SWA_KV_HEADS, nb),
        in_specs=[pl.BlockSpec(memory_space=pltpu.SMEM), qspec, own, prev, own, prev, bspec, qspec, stat, stat],
        out_specs=[qspec, full, full, pl.BlockSpec((SWA_GROUP, W, 2 * W), lambda h, i: (h, 0, 0)),
                   pl.BlockSpec((None, 8, LANES), lambda h, i: (h, 0, 0))],
        out_shape=[jax.ShapeDtypeStruct((T, 512), F32), jax.ShapeDtypeStruct((SWA_KV_HEADS, T, LANES), F32),
                   jax.ShapeDtypeStruct((SWA_KV_HEADS, T, LANES), F32), jax.ShapeDtypeStruct((SWA_HEADS, W, 2 * W), F32),
                   jax.ShapeDtypeStruct((SWA_KV_HEADS, 8, LANES), F32)],
        compiler_params=_cparams("arbitrary", "arbitrary", vmem=VMEM_MID),
    )(sinks, q, kad, kad, vad, vad, bias.reshape(SWA_KV_HEADS, SWA_GROUP * W, 2 * W), do, lse, delta)


def _mem_fwd(q, mk, mv, T, tq):
    scale = MEM_HEAD_DIM ** -0.5

    def body(q_ref, k_ref, v_ref, o_ref, lse_ref):
        s = _dot(q_ref[...], k_ref[...], NT) * scale
        m = jnp.max(s, axis=1, keepdims=True)
        p = jnp.exp(s - m)
        l = jnp.sum(p, axis=1, keepdims=True)
        o_ref[...] = _dot((p / l).astype(BF16), v_ref[...]).astype(o_ref.dtype)
        lse_ref[...] = jnp.broadcast_to(m + jnp.log(l), (tq, LANES))

    qspec = pl.BlockSpec((tq, LANES), lambda h, i: (i, h))
    kspec = pl.BlockSpec((N_MEM, LANES), lambda h, i: (0, h))
    return pl.pallas_call(
        body, name="mem_fwd", grid=(MEM_HEADS, T // tq),
        in_specs=[qspec, kspec, kspec],
        out_specs=[qspec, pl.BlockSpec((None, tq, LANES), lambda h, i: (h, i, 0))],
        out_shape=[jax.ShapeDtypeStruct((T, 512), BF16), jax.ShapeDtypeStruct((MEM_HEADS, T, LANES), F32)],
        compiler_params=_cparams("parallel", "parallel"),
    )(q, mk, mv)


def _mem_bwd(q, mk, mv, do, lse, delta, T, tq):
    scale = MEM_HEAD_DIM ** -0.5
    rep = N_MEM // LANES

    def body(q_ref, k_ref, v_ref, do_ref, lse_ref, dl_ref, dq_ref, dk_ref, dv_ref):
        i = pl.program_id(1)

        @pl.when(i == 0)
        def _():
            dk_ref[...] = jnp.zeros_like(dk_ref)
            dv_ref[...] = jnp.zeros_like(dv_ref)

        qv, dov = q_ref[...], do_ref[...]
        s = _dot(qv, k_ref[...], NT) * scale
        p = jnp.exp(s - jnp.tile(lse_ref[...], (1, rep)))
        dp = _dot(dov, v_ref[...], NT)
        ds = p * (dp - jnp.tile(dl_ref[...], (1, rep)))
        dsb = ds.astype(BF16)
        dq_ref[...] = _dot(dsb, k_ref[...]) * scale
        dk_ref[...] += _dot(dsb, qv, TN) * scale
        dv_ref[...] += _dot(p.astype(BF16), dov, TN)

    qspec = pl.BlockSpec((tq, LANES), lambda h, i: (i, h))
    kspec = pl.BlockSpec((N_MEM, LANES), lambda h, i: (0, h))
    stat = pl.BlockSpec((None, tq, LANES), lambda h, i: (h, i, 0))
    return pl.pallas_call(
        body, name="mem_bwd", grid=(MEM_HEADS, T // tq),
        in_specs=[qspec, kspec, kspec, qspec, stat, stat],
        out_specs=[qspec, kspec, kspec],
        out_shape=[jax.ShapeDtypeStruct((T, 512), F32), jax.ShapeDtypeStruct((N_MEM, 512), F32),
                   jax.ShapeDtypeStruct((N_MEM, 512), F32)],
        compiler_params=_cparams("arbitrary", "arbitrary"),
    )(q, mk, mv, do, lse, delta)


def _mem_prep_fwd(mem, g_mem, w_kv, kn_gain, gm128):
    def body(mem_ref, g_ref, w_ref, kn_ref, gm_ref, memn_o, kv_o, mk_o, mv_o):
        xhat, _ = _rms_rows(mem_ref[...], None)
        memn = (xhat * g_ref[...]).astype(BF16)
        memn_o[...] = memn
        kv = _dot(memn, w_ref[...])
        kv_o[...] = kv
        gm = gm_ref[...]
        for c in range(4):
            sl = slice(c * LANES, (c + 1) * LANES)
            y, _ = _head_norm(kv[:, sl], gm, kn_ref[...])
            mk_o[:, sl] = y.astype(BF16)
        mv_o[...] = kv[:, 512:].astype(BF16)

    vm = pl.BlockSpec(memory_space=pltpu.VMEM)
    return pl.pallas_call(
        body, name="mem_prep_fwd", in_specs=[vm] * 5, out_specs=[vm] * 4,
        out_shape=[jax.ShapeDtypeStruct((N_MEM, D_MODEL), BF16), jax.ShapeDtypeStruct((N_MEM, D_MODEL), F32),
                   jax.ShapeDtypeStruct((N_MEM, 512), BF16), jax.ShapeDtypeStruct((N_MEM, 512), BF16)],
        compiler_params=pltpu.CompilerParams(vmem_limit_bytes=VMEM_MID),
    )(mem, g_mem, w_kv, kn_gain, gm128)


def _mem_prep_bwd(mem, g_mem, memn, kv, w_kv, kn_gain, gm128, dmk, dmv):
    def body(mem_ref, g_ref, memn_ref, kv_ref, w_ref, kn_ref, gm_ref, dmk_ref, dmv_ref, dw_o, dg_o, dkn_o, dkv_s):
        gm = gm_ref[...]
        dkn = jnp.zeros((1, LANES), F32)
        for c in range(4):
            sl = slice(c * LANES, (c + 1) * LANES)
            dx, dg = _head_norm_bwd(dmk_ref[:, sl], kv_ref[:, sl], gm, kn_ref[...])
            dkv_s[:, sl] = dx.astype(BF16)
            dkn = dkn + dg
        dkn_o[...] = dkn
        dkv_s[:, 512:] = dmv_ref[...].astype(BF16)
        dkv = dkv_s[...]
        dw_o[...] = _dot(memn_ref[...], dkv, TN)
        dmemn = _dot(dkv, w_ref[...], NT)
        xhat, _ = _rms_rows(mem_ref[...], None)
        dg_o[...] = jnp.sum(dmemn * xhat, axis=0, keepdims=True)

    vm = pl.BlockSpec(memory_space=pltpu.VMEM)
    return pl.pallas_call(
        body, name="mem_prep_bwd", in_specs=[vm] * 9, out_specs=[vm] * 3,
        out_shape=[jax.ShapeDtypeStruct((D_MODEL, D_MODEL), F32), jax.ShapeDtypeStruct((1, D_MODEL), F32),
                   jax.ShapeDtypeStruct((1, LANES), F32)],
        scratch_shapes=[pltpu.VMEM((N_MEM, D_MODEL), BF16)],
        compiler_params=pltpu.CompilerParams(vmem_limit_bytes=VMEM_MID),
    )(mem, g_mem, memn, kv, w_kv, kn_gain, gm128, dmk, dmv)


SLOT_O = D_MODEL // N_SHARD


def _merge_fwd(proj, b_gate, o3, w3, T, tb):
    def body(gl_ref, bg_ref, oa_ref, of_ref, om_ref, wa_ref, wf_ref, wm_ref, out_ref):
        o_refs = (oa_ref, of_ref, om_ref)
        w_refs = (wa_ref, wf_ref, wm_ref)
        for n in range(N_SHARD):
            acc = jnp.zeros((tb, SLOT_O), F32)
            for b in range(3):
                c0 = b * D_MODEL + n * SLOT_O
                g = jax.nn.sigmoid(gl_ref[:, c0:c0 + SLOT_O] + bg_ref[:, c0:c0 + SLOT_O])
                acc = acc + g * _dot(o_refs[b][...], w_refs[b][n])
            out_ref[:, n * SLOT_O:(n + 1) * SLOT_O] = acc.astype(out_ref.dtype)

    rows = pl.BlockSpec((tb, 512), lambda i: (i, 0))
    wspec = pl.BlockSpec((N_SHARD, 512, SLOT_O), lambda i: (0, 0, 0))
    return pl.pallas_call(
        body, name="merge_fwd", grid=(T // tb,),
        in_specs=[pl.BlockSpec((tb, GATE_W), lambda i: (i, 1)), pl.BlockSpec((1, GATE_W), lambda i: (0, 0)),
                  rows, rows, rows, wspec, wspec, wspec],
        out_specs=pl.BlockSpec((tb, D_MODEL), lambda i: (i, 0)),
        out_shape=jax.ShapeDtypeStruct((T, D_MODEL), BF16),
        compiler_params=_cparams("parallel", vmem=VMEM_BIG),
    )(proj, b_gate, *o3, *w3)


def _merge_bwd(proj, b_gate, o3, w3, dmerged, T, tb):
    heads = (SWA_HEADS, FOX_HEADS, MEM_HEADS)

    def body(gl_ref, bg_ref, oa_ref, of_ref, om_ref, wa_ref, wf_ref, wm_ref, dm_ref,
             dgl_o, doa_o, dof_o, dom_o, dla_o, dlf_o, dlm_o, dwa_o, dwf_o, dwm_o, dbg_o):
        i = pl.program_id(0)
        o_refs = (oa_ref, of_ref, om_ref)
        w_refs = (wa_ref, wf_ref, wm_ref)
        do_refs = (doa_o, dof_o, dom_o)
        dl_refs = (dla_o, dlf_o, dlm_o)
        dw_refs = (dwa_o, dwf_o, dwm_o)

        @pl.when(i == 0)
        def _():
            for r in dw_refs:
                r[...] = jnp.zeros_like(r)
            dbg_o[...] = jnp.zeros_like(dbg_o)

        lane = _lane((tb, LANES))
        for b in range(3):
            ob = o_refs[b][...]
            do = jnp.zeros((tb, 512), F32)
            for n in range(N_SHARD):
                c0 = b * D_MODEL + n * SLOT_O
                g = jax.nn.sigmoid(gl_ref[:, c0:c0 + SLOT_O] + bg_ref[:, c0:c0 + SLOT_O])
                dm = dm_ref[:, n * SLOT_O:(n + 1) * SLOT_O]
                y = _dot(ob, w_refs[b][n])
                dgl = dm * y * g * (1.0 - g)
                dgl_o[:, c0:c0 + SLOT_O] = dgl.astype(dgl_o.dtype)
                dbg_o[:, c0:c0 + SLOT_O] += jnp.sum(dgl, axis=0, keepdims=True)
                dy = (dm * g).astype(BF16)
                do = do + _dot(dy, w_refs[b][n], NT)
                dw_refs[b][n] += _dot(ob, dy, TN)
            do_refs[b][...] = do.astype(BF16)
            prod = do * ob.astype(F32)
            for c in range(4):
                blk = prod[:, c * LANES:(c + 1) * LANES]
                if heads[b] == 8:
                    lo = jnp.sum(jnp.where(lane < 64, blk, 0.0), axis=1, keepdims=True)
                    hi = jnp.sum(jnp.where(lane >= 64, blk, 0.0), axis=1, keepdims=True)
                    if b == 1:
                        aug = jnp.zeros((tb, LANES), F32)
                        for sub, dl in enumerate((lo, hi)):
                            for e, piece in enumerate(_split3(-dl)):
                                aug = jnp.where(lane == AUG_STRIDE * sub + AUG_C + e, piece.astype(F32), aug)
                        dl_refs[b][:, c * LANES:(c + 1) * LANES] = aug.astype(BF16)
                    else:
                        dl_refs[b][2 * c] = jnp.broadcast_to(lo, (tb, LANES))
                        dl_refs[b][2 * c + 1] = jnp.broadcast_to(hi, (tb, LANES))
                else:
                    dl_refs[b][c] = jnp.broadcast_to(jnp.sum(blk, axis=1, keepdims=True), (tb, LANES))

    rows = pl.BlockSpec((tb, 512), lambda i: (i, 0))
    wspec = pl.BlockSpec((N_SHARD, 512, SLOT_O), lambda i: (0, 0, 0))
    stat = lambda h: pl.BlockSpec((h, tb, LANES), lambda i: (0, i, 0))
    return pl.pallas_call(
        body, name="merge_bwd", grid=(T // tb,),
        in_specs=[pl.BlockSpec((tb, GATE_W), lambda i: (i, 1)), pl.BlockSpec((1, GATE_W), lambda i: (0, 0)),
                  rows, rows, rows, wspec, wspec, wspec, pl.BlockSpec((tb, D_MODEL), lambda i: (i, 0))],
        out_specs=[pl.BlockSpec((tb, GATE_W), lambda i: (i, 0)), rows, rows, rows,
                   stat(8), rows, stat(4), wspec, wspec, wspec, pl.BlockSpec((1, GATE_W), lambda i: (0, 0))],
        out_shape=[jax.ShapeDtypeStruct((T, GATE_W), BF16)] + [jax.ShapeDtypeStruct((T, 512), BF16)] * 3
        + [jax.ShapeDtypeStruct((8, T, LANES), F32), jax.ShapeDtypeStruct((T, 512), BF16),
           jax.ShapeDtypeStruct((4, T, LANES), F32)]
        + [jax.ShapeDtypeStruct((N_SHARD, 512, SLOT_O), F32)] * 3 + [jax.ShapeDtypeStruct((1, GATE_W), F32)],
        compiler_params=_cparams("arbitrary", vmem=VMEM_BIG),
    )(proj, b_gate, *o3, *w3, dmerged)


def _local_step(x, h, mem, tgt, small, g_in, w_kv, w_o3, w_out, w_up, w_down, reducer):
    T = x.shape[0]
    tm = min(512, T)
    tile2 = lambda v: jnp.tile(v.reshape(1, -1), (1, LANES // v.size))
    gains = jnp.concatenate([tile2(small["qn_swa"]), tile2(small["kn_swa"]), tile2(small["qn_fox"]),
                             tile2(small["kn_fox"]), tile2(small["qn_mem"]), jnp.zeros((3, LANES), F32)], axis=0)
    kn_mem = small["kn_mem"].reshape(1, LANES)
    bfor = jnp.pad(small["b_forget"].reshape(1, -1), ((0, 0), (0, LANES - FOX_HEADS)))
    gm64 = _group_mean_matrix(64)
    gm128 = _group_mean_matrix(128)
    tb_prep = min(256, T)
    ones = jnp.ones((tb_prep, tb_prep), F32)
    tril = jnp.tril(ones).astype(BF16)
    triu = jnp.triu(ones).astype(BF16)
    bucket = _t5_bucket_matrix()
    g_mix, g_mlp, g_mem = small["g_mix"], small["g_mlp"], small["g_mem"]
    b_gate = small["b_gate"]
    sinks = small["sink_swa"].reshape(-1)

    tl = min(1024, T)
    sq = pl.BlockSpec((tl, D_MODEL), lambda i, j, k: (i, j))
    wc = _w_in_to_segments(g_in)
    (proj,) = _matmul(
        "mm_proj", h, wc, dims=NN, grid=(T // tl, PROJ_W // D_MODEL, 1),
        a_spec=pl.BlockSpec((tl, D_MODEL), lambda i, j, k: (i, 0)),
        b_spec=pl.BlockSpec((D_MODEL, D_MODEL), lambda i, j, k: (0, j)),
        acc_shape=(tl, D_MODEL),
        outs=[(jax.ShapeDtypeStruct((T, PROJ_W), F32), sq)],
        epilogue=_epi_store)
    qa, qf, kf, vf, qm, kad, vad, qf_aug, kf_aug = _prep_fwd(proj, gains, bfor, tril, gm64, gm128, T, tb_prep)
    bias = _swa_bias(small["rel_bias"], bucket)
    o_swa, lse_swa = _swa_fwd(sinks, qa, kad, vad, bias, T)
    o_fox, qf_aug_bwd = _fox_fwd(qf, qf_aug, kf, kf_aug, vf, T, min(FOX_TQ, T), min(FOX_TK, T))
    memn, kv, mk, mv = _mem_prep_fwd(mem, g_mem, w_kv, kn_mem, gm128)
    o_mem, lse_mem = _mem_fwd(qm, mk, mv, T, tm)
    o3 = (o_swa, o_fox, o_mem)
    merged = _merge_fwd(proj, b_gate, o3, w_o3, T, min(512, T))

    def epi_residual(acc, extra_refs, out_refs, ij):
        out_refs[0][...] = extra_refs[0][...] + acc

    row_full = pl.BlockSpec((tm, D_MODEL), lambda i, j, k: (i, 0))
    row_big = pl.BlockSpec((tl, D_MODEL), lambda i, j, k: (i, 0))
    whole = pl.BlockSpec((D_MODEL, D_MODEL), lambda i, j, k: (0, 0))
    (x2,) = _matmul(
        "mm_out", merged, w_out, dims=NN, grid=(T // tl, 1, 1),
        a_spec=row_big, b_spec=whole,
        acc_shape=(tl, D_MODEL), extra=[(x, row_big)],
        outs=[(jax.ShapeDtypeStruct((T, D_MODEL), F32), row_big)], epilogue=epi_residual)
    hm = _rmsnorm("rms_mlp", x2, g_mlp, tm)

    def epi_relu2(acc, extra_refs, out_refs, ij):
        out_refs[0][...] = acc.astype(BF16)
        r = jnp.maximum(acc, 0.0)
        out_refs[1][...] = (r * r).astype(BF16)

    up, u = _matmul(
        "mm_up", hm, w_up, dims=NN, grid=(T // tl, N_SHARD, 1),
        a_spec=row_big, b_spec=pl.BlockSpec((None, D_MODEL, D_MODEL), lambda i, j, k: (j, 0, 0)),
        acc_shape=(tl, D_MODEL),
        outs=[(jax.ShapeDtypeStruct((T, D_FF), BF16), sq), (jax.ShapeDtypeStruct((T, D_FF), BF16), sq)],
        epilogue=epi_relu2)

    def epi_loss(acc, extra_refs, out_refs, ij):
        y = extra_refs[0][...] + acc
        err = y - extra_refs[1][...]
        dyv = err * (1.0 / D_MODEL)
        out_refs[0][...] = dyv
        out_refs[2][...] = dyv.astype(BF16)
        sq = jnp.sum(jnp.sum(err * err, axis=1, keepdims=True), axis=0, keepdims=True)

        @pl.when(ij[0] == 0)
        def _():
            out_refs[1][...] = jnp.zeros_like(out_refs[1])

        out_refs[1][...] += jnp.broadcast_to(sq, out_refs[1].shape)

    kblk = pl.BlockSpec((tl, D_MODEL), lambda i, j, k: (i, k))
    dy, loss_acc, dy_bf = _matmul(
        "mm_down", u, w_down, dims=NN, grid=(T // tl, 1, N_SHARD),
        a_spec=kblk, b_spec=pl.BlockSpec((D_MODEL, D_MODEL), lambda i, j, k: (k, 0)),
        acc_shape=(tl, D_MODEL), extra=[(x2, row_big), (tgt, row_big)],
        outs=[(jax.ShapeDtypeStruct((T, D_MODEL), F32), row_big),
              (jax.ShapeDtypeStruct((8, LANES), F32), pl.BlockSpec((8, LANES), lambda i, j, k: (0, 0))),
              (jax.ShapeDtypeStruct((T, D_MODEL), BF16), row_big)],
        epilogue=epi_loss)
    loss = loss_acc[0, 0] * (0.5 / D_MODEL)

    def epi_dup(acc, extra_refs, out_refs, ij):
        out_refs[0][...] = (acc * (2.0 * jnp.maximum(extra_refs[0][...].astype(F32), 0.0))).astype(BF16)

    (dup,) = _matmul(
        "mm_dup", dy_bf, w_down, dims=NT, grid=(T // tl, N_SHARD, 1),
        a_spec=row_big, b_spec=pl.BlockSpec((D_MODEL, D_MODEL), lambda i, j, k: (j, 0)),
        acc_shape=(tl, D_MODEL), extra=[(up, sq)],
        outs=[(jax.ShapeDtypeStruct((T, D_FF), BF16), sq)], epilogue=epi_dup)

    nkt = T // tl
    t_rows = pl.BlockSpec((tl, D_MODEL), lambda i, j, k: (k, i))
    t_cols = pl.BlockSpec((tl, D_MODEL), lambda i, j, k: (k, j))
    (d_w_down,) = _matmul(
        "mm_dw_down", u, dy_bf, dims=TN, grid=(N_SHARD, 1, nkt),
        a_spec=t_rows, b_spec=t_cols, acc_shape=(D_MODEL, D_MODEL),
        outs=[(jax.ShapeDtypeStruct((D_FF, D_MODEL), F32), pl.BlockSpec((D_MODEL, D_MODEL), lambda i, j, k: (i, 0)))],
        epilogue=_epi_store)
    (d_w_up,) = _matmul(
        "mm_dw_up", hm, dup, dims=TN, grid=(1, N_SHARD, nkt),
        a_spec=t_rows, b_spec=t_cols, acc_shape=(D_MODEL, D_MODEL),
        outs=[(jax.ShapeDtypeStruct((N_SHARD, D_MODEL, D_MODEL), F32),
               pl.BlockSpec((None, D_MODEL, D_MODEL), lambda i, j, k: (j, 0, 0)))],
        epilogue=_epi_store)

    def epi_rms_bwd(acc, extra_refs, out_refs, ij):
        dx, dg = _rmsnorm_bwd_rows(acc, extra_refs[0][...], extra_refs[1][...])
        out_refs[0][...] = dx + extra_refs[2][...]

        @pl.when(ij[0] == 0)
        def _():
            out_refs[1][...] = jnp.zeros_like(out_refs[1])

        out_refs[1][...] += dg

    gain_spec = pl.BlockSpec((1, D_MODEL), lambda i, j, k: (0, 0))
    dx2, d_g_mlp = _matmul(
        "mm_dhm", dup, w_up, dims=NT, grid=(T // tl, 1, N_SHARD),
        a_spec=kblk, b_spec=pl.BlockSpec((None, D_MODEL, D_MODEL), lambda i, j, k: (k, 0, 0)),
        acc_shape=(tl, D_MODEL), extra=[(x2, row_big), (g_mlp, gain_spec), (dy, row_big)],
        outs=[(jax.ShapeDtypeStruct((T, D_MODEL), F32), row_big), (jax.ShapeDtypeStruct((1, D_MODEL), F32), gain_spec)],
        epilogue=epi_rms_bwd)

    (dmerged,) = _matmul(
        "mm_dmerged", dx2, w_out, dims=NT, grid=(T // tl, 1, 1),
        a_spec=row_big, b_spec=whole,
        acc_shape=(tl, D_MODEL), outs=[(jax.ShapeDtypeStruct((T, D_MODEL), F32), row_big)], epilogue=_epi_store)
    (d_w_out,) = _matmul(
        "mm_dw_out", merged, dx2, dims=TN, grid=(1, 1, nkt),
        a_spec=t_rows, b_spec=t_cols, acc_shape=(D_MODEL, D_MODEL),
        outs=[(jax.ShapeDtypeStruct((D_MODEL, D_MODEL), F32), whole)],
        epilogue=_epi_store)
    (dgl, do_swa, do_fox, do_mem, dl_swa, do_fox_aug, dl_mem, d_wo_swa, d_wo_fox, d_wo_mem, d_b_gate) = _merge_bwd(
        proj, b_gate, o3, w_o3, dmerged, T, min(512, T))

    dqm, dmk, dmv = _mem_bwd(qm, mk, mv, do_mem, lse_mem, dl_mem, T, tm)
    d_w_kv, d_g_mem, d_kn_mem = _mem_prep_bwd(mem, g_mem, memn, kv, w_kv, kn_mem, gm128, dmk, dmv)
    do_swa = reducer.early_start({"w_mlp_down": d_w_down, "w_mlp_up": d_w_up, "w_out": d_w_out, "w_mem_kv": d_w_kv,
                                  "w_o_swa": d_wo_swa, "w_o_fox": d_wo_fox, "w_o_mem": d_wo_mem}, do_swa)
    dqa, dkad, dvad, dbias, dsk = _swa_bwd(sinks, qa, kad, vad, bias, do_swa, lse_swa, dl_swa, T)
    dqa, do_fox = reducer.early_send((dqa, do_fox))
    dqf, dqf_aug, dkf, dkf_aug, dvf = _fox_bwd(qf, qf_aug_bwd, kf, kf_aug, vf, do_fox, do_fox_aug, T,
                                               min(FOX_BWD_TQ, T), min(FOX_BWD_TK, T))
    dvf = reducer.early_finish(dvf)
    d_rel = _swa_bias_bwd(dbias, bucket)
    dlo, gacc = _prep_bwd(proj, dqa, dkad, dvad, dqf, dkf, dvf, dqm, dqf_aug, dkf_aug, gains, bfor, triu, gm64, gm128,
                          T, tb_prep)

    def dwc_half(name, dpart):
        (res,) = _matmul(
            name, h, dpart, dims=TN, grid=(1, LO_W // D_MODEL, nkt),
            a_spec=t_rows, b_spec=t_cols, acc_shape=(D_MODEL, D_MODEL),
            outs=[(jax.ShapeDtypeStruct((D_MODEL, LO_W), F32), pl.BlockSpec((D_MODEL, D_MODEL), lambda i, j, k: (0, j)))],
            epilogue=_epi_store)
        return res

    d_wc_lo = dwc_half("mm_dwc_lo", dlo)
    d_wc_gl = dwc_half("mm_dwc_gl", dgl)
    dlo = reducer.late_start({"wc_lo": d_wc_lo, "wc_gl": d_wc_gl}, dlo)
    (dh_lo,) = _matmul(
        "mm_dh_lo", dlo, wc, dims=NT, grid=(T // tl, 1, LO_W // D_MODEL),
        a_spec=kblk, b_spec=pl.BlockSpec((D_MODEL, D_MODEL), lambda i, j, k: (0, k)),
        acc_shape=(tl, D_MODEL), outs=[(jax.ShapeDtypeStruct((T, D_MODEL), F32), row_big)], epilogue=_epi_store)
    dh_lo = reducer.late_send(dh_lo)

    def epi_dx(acc, extra_refs, out_refs, ij):
        dhh = acc + extra_refs[3][...]
        dx, dg = _rmsnorm_bwd_rows(dhh, extra_refs[0][...], extra_refs[1][...])
        out_refs[0][...] = dx + extra_refs[2][...]

        @pl.when(ij[0] == 0)
        def _():
            out_refs[1][...] = jnp.zeros_like(out_refs[1])

        out_refs[1][...] += dg

    grad_x, d_g_mix = _matmul(
        "mm_dh_gl", dgl, wc, dims=NT, grid=(T // tl, 1, GATE_W // D_MODEL),
        a_spec=kblk, b_spec=pl.BlockSpec((D_MODEL, D_MODEL), lambda i, j, k: (0, k + LO_W // D_MODEL)),
        acc_shape=(tl, D_MODEL), extra=[(x, row_big), (g_mix, gain_spec), (dx2, row_big), (dh_lo, row_big)],
        outs=[(jax.ShapeDtypeStruct((T, D_MODEL), F32), row_big), (jax.ShapeDtypeStruct((1, D_MODEL), F32), gain_spec)],
        epilogue=epi_dx, vmem=VMEM_MAX)

    fold64 = lambda row: (row[:64] + row[64:]).reshape(1, 64)
    grads = {
        "g_mix": d_g_mix, "b_gate": d_b_gate, "b_forget": gacc[5, :FOX_HEADS].reshape(1, FOX_HEADS),
        "qn_swa": fold64(gacc[0]), "kn_swa": fold64(gacc[1]),
        "sink_swa": -dsk[:, :SWA_GROUP, 0].reshape(1, SWA_HEADS), "rel_bias": d_rel[:, :SWA_HEADS],
        "qn_fox": fold64(gacc[2]), "kn_fox": fold64(gacc[3]),
        "g_mem": d_g_mem, "qn_mem": gacc[4].reshape(1, LANES), "kn_mem": d_kn_mem, "g_mlp": d_g_mlp,
    }
    return loss, grad_x, grads


MESH = pl.DeviceIdType.MESH
ANY = pl.BlockSpec(memory_space=pl.ANY)


def _place():
    x, y, c = lax.axis_index("x"), lax.axis_index("y"), lax.axis_index("c")
    chips = [(1 - x, y), (x, 1 - y), (1 - x, 1 - y)]
    return x, y, c, chips


def _handshake(peers):
    barrier = pltpu.get_barrier_semaphore()
    for peer in peers:
        pl.semaphore_signal(barrier, inc=1, device_id=peer, device_id_type=MESH)
    pl.semaphore_wait(barrier, len(peers))


def _all_gather_shards_async(name, collective_id, slots):
    n = len(slots)
    bufs = [jax.new_ref(s, memory_space=pltpu.MemorySpace.HBM) for s in slots]

    def body(ici_send, ici_recv, d2d_send, d2d_recv):
        x, y, c, chips = _place()
        sibling = (x, y, 1 - c)
        me = 2 * x + y
        _handshake([(px, py, c) for px, py in chips] + [sibling])

        def half(a, who):
            hr = slots[a].shape[1] // 2
            return pl.ds(pl.multiple_of(who * hr, hr), hr)

        def ici(a, j, slot, to):
            return pltpu.make_async_remote_copy(
                src_ref=bufs[a].at[me, half(a, c)], dst_ref=bufs[a].at[slot, half(a, c)],
                send_sem=ici_send.at[3 * a + j], recv_sem=ici_recv.at[3 * a + j], device_id=to, device_id_type=MESH)

        def d2d(a, j, slot, which):
            part = bufs[a].at[slot, half(a, which)]
            return pltpu.make_async_remote_copy(
                src_ref=part, dst_ref=part, send_sem=d2d_send.at[3 * a + j], recv_sem=d2d_recv.at[3 * a + j],
                device_id=sibling, device_id_type=MESH)

        sends = [ici(a, j, me, (*chip, c)) for a in range(n) for j, chip in enumerate(chips)]
        for cp in sends:
            cp.start()
        passed = []
        for a in range(n):
            for j, (px, py) in enumerate(chips):
                ici(a, j, 2 * px + py, (px, py, c)).wait_recv()
                cp = d2d(a, j, 2 * px + py, c)
                cp.start()
                passed.append(cp)
        for a in range(n):
            for j, (px, py) in enumerate(chips):
                d2d(a, j, 2 * px + py, 1 - c).wait_recv()
        for cp in sends + passed:
            cp.wait_send()

    pl.kernel(
        body, mesh=plsc.ScalarSubcoreMesh(axis_name="seq", num_cores=1), name=name,
        scratch_types=[pltpu.SemaphoreType.DMA((3 * n,))] * 4,
        compiler_params=pltpu.CompilerParams(collective_id=collective_id),
    )()
    return [b[...] for b in bufs]


def _sequencer_call(name, collective_id, n_sems, body):
    pl.kernel(
        body, mesh=plsc.ScalarSubcoreMesh(axis_name="seq", num_cores=1), name=name,
        scratch_types=[pltpu.SemaphoreType.DMA((n_sems,))] * 2,
        compiler_params=pltpu.CompilerParams(collective_id=collective_id),
    )()


def _hbm_ref(value):
    return jax.new_ref(value, memory_space=pltpu.MemorySpace.HBM)


def _pair_exchange(name, collective_id, gs):
    n = len(gs)
    src = [_hbm_ref(g) for g in gs]
    stage = [jax.empty_ref(jax.ShapeDtypeStruct((N_SHARD, g.shape[1] // 2, g.shape[2]), g.dtype),
                           memory_space=pltpu.MemorySpace.HBM) for g in gs]

    def body(send_sem, recv_sem):
        x, y, c, _ = _place()
        sibling = (x, y, 1 - c)
        _handshake([sibling])
        copies = []
        for a in range(n):
            hr = gs[a].shape[1] // 2
            theirs = pl.ds(pl.multiple_of((1 - c) * hr, hr), hr)
            copies.append(pltpu.make_async_remote_copy(
                src_ref=src[a].at[:, theirs, :], dst_ref=stage[a], send_sem=send_sem.at[a], recv_sem=recv_sem.at[a],
                device_id=sibling, device_id_type=MESH))
        for cp in copies:
            cp.start()
        for cp in copies:
            cp.wait()

    _sequencer_call(name, collective_id, n, body)
    return [s[...] for s in stage]


def _chip_exchange(name, collective_id, sums):
    n = len(sums)
    src = [_hbm_ref(s) for s in sums]
    got = [jax.empty_ref(jax.ShapeDtypeStruct((3,) + s.shape[1:], s.dtype), memory_space=pltpu.MemorySpace.HBM)
           for s in sums]

    def body(send_sem, recv_sem):
        x, y, c, chips = _place()
        _handshake([(px, py, c) for px, py in chips])
        copies = []
        for a in range(n):
            for j, (px, py) in enumerate(chips):
                copies.append(pltpu.make_async_remote_copy(
                    src_ref=src[a].at[2 * px + py], dst_ref=got[a].at[j],
                    send_sem=send_sem.at[3 * a + j], recv_sem=recv_sem.at[3 * a + j],
                    device_id=(px, py, c), device_id_type=MESH))
        for cp in copies:
            cp.start()
        for cp in copies:
            cp.wait()

    _sequencer_call(name, collective_id, 3 * n, body)
    return [g[...] for g in got]


def _pair_gather(name, collective_id, fulls):
    n = len(fulls)
    full = [_hbm_ref(f) for f in fulls]

    def body(send_sem, recv_sem):
        x, y, c, _ = _place()
        sibling = (x, y, 1 - c)
        _handshake([sibling])
        copies = []
        for a in range(n):
            hr = fulls[a].shape[0] // 2
            mine = full[a].at[pl.ds(pl.multiple_of(c * hr, hr), hr)]
            copies.append(pltpu.make_async_remote_copy(
                src_ref=mine, dst_ref=mine, send_sem=send_sem.at[a], recv_sem=recv_sem.at[a],
                device_id=sibling, device_id_type=MESH))
        for cp in copies:
            cp.start()
        for cp in copies:
            cp.wait()

    _sequencer_call(name, collective_id, n, body)
    return [f[...] for f in full]


ELEMENTWISE_BLOCK_ELEMS = 256 * 1024


def _row_block(rows, cols):
    rb = 8
    while rb * 2 * cols <= ELEMENTWISE_BLOCK_ELEMS and rb * 2 <= rows:
        rb *= 2
    return rb


def _pair_sum(name, place, g, stage):
    _, R, C = g.shape
    hr = R // 2
    rb = _row_block(hr, C)
    nb = hr // rb

    def body(place_ref, g_ref, st_ref, sum_bf, own_f32):
        s = pl.program_id(1)
        tot = g_ref[...] + st_ref[...]
        sum_bf[...] = tot.astype(BF16)

        @pl.when(s == place_ref[0])
        def _():
            own_f32[...] = tot

    return pl.pallas_call(
        body, name=name,
        grid_spec=pltpu.PrefetchScalarGridSpec(
            num_scalar_prefetch=1, grid=(nb, N_SHARD),
            in_specs=[pl.BlockSpec((None, rb, C), lambda i, s, pr: (s, pr[1] * nb + i, 0)),
                      pl.BlockSpec((None, rb, C), lambda i, s, pr: (s, i, 0))],
            out_specs=[pl.BlockSpec((None, rb, C), lambda i, s, pr: (s, i, 0)),
                       pl.BlockSpec((rb, C), lambda i, s, pr: (i, 0))]),
        out_shape=[jax.ShapeDtypeStruct((N_SHARD, hr, C), BF16), jax.ShapeDtypeStruct((hr, C), F32)],
        compiler_params=_cparams("arbitrary", "arbitrary"),
    )(place, g, stage)


def _final_sum(name, place, own, got):
    hr, C = own.shape
    rb = _row_block(hr, C)
    nb = hr // rb

    def body(place_ref, own_ref, got_ref, o_ref):
        o_ref[...] = ((own_ref[...] + got_ref[0].astype(F32)) + got_ref[1].astype(F32)) + got_ref[2].astype(F32)

    return pl.pallas_call(
        body, name=name,
        grid_spec=pltpu.PrefetchScalarGridSpec(
            num_scalar_prefetch=1, grid=(nb,),
            in_specs=[pl.BlockSpec((rb, C), lambda i, pr: (i, 0)), pl.BlockSpec((3, rb, C), lambda i, pr: (0, i, 0))],
            out_specs=pl.BlockSpec((rb, C), lambda i, pr: (pr[1] * nb + i, 0))),
        out_shape=jax.ShapeDtypeStruct((2 * hr, C), F32),
        compiler_params=_cparams("arbitrary"),
    )(place, own, got)


def _adamw_math(w, g, m, v):
    m = ADAM_B1 * m + (1.0 - ADAM_B1) * g
    v = ADAM_B2 * v + (1.0 - ADAM_B2) * (g * g)
    m_hat = m / (1.0 - ADAM_B1 ** ADAM_STEP)
    v_hat = v / (1.0 - ADAM_B2 ** ADAM_STEP)
    delta = -ADAM_LR * (m_hat / (jnp.sqrt(v_hat) + ADAM_EPS) + ADAM_WD * w)
    return delta, m, v


def _adamw(name, w, g, m, v):
    R, Cw = w.shape
    Cg = g.shape[1]
    rb = _row_block(R, Cg)

    def body(w_ref, g_ref, m_ref, v_ref, g_o, d_o, m_o, v_o):
        gv = g_ref[...]
        delta, mn, vn = _adamw_math(w_ref[...], gv, m_ref[...], v_ref[...])
        g_o[...] = gv
        d_o[...] = delta
        m_o[...] = mn
        v_o[...] = vn

    blk = pl.BlockSpec((rb, Cg), lambda i: (i, 0))
    return pl.pallas_call(
        body, name=name, grid=(R // rb,),
        in_specs=[blk] * 4, out_specs=[blk] * 4,
        out_shape=[jax.ShapeDtypeStruct((R, Cw), F32)] * 4,
        compiler_params=_cparams("parallel"),
    )(w, g, m, v)


N_DEV = 8
SMALL_ROWS = 64


def _small_allreduce_adamw(g, w, m, v):
    def body(g_ref, w_ref, m_ref, v_ref, all_ref, gs_o, d_o, m_o, v_o, send_sems, recv_sems, local_sem):
        x, y, c, chips = _place()
        me, sibling = (x, y, c), (x, y, 1 - c)

        def rows(px, py, pc):
            return all_ref.at[pl.ds(pl.multiple_of((4 * px + 2 * py + pc) * SMALL_ROWS, SMALL_ROWS), SMALL_ROWS), :]

        def copy(k, block, to, src=None):
            return pltpu.make_async_remote_copy(
                src_ref=rows(*block) if src is None else src, dst_ref=rows(*block),
                send_sem=send_sems.at[k], recv_sem=recv_sems.at[k], device_id=to, device_id_type=MESH)

        mine = pltpu.make_async_copy(g_ref, rows(*me), local_sem)
        mine.start()
        first = [copy(0, me, sibling, src=g_ref)]
        first += [copy(1 + j, me, (*chip, c), src=g_ref) for j, chip in enumerate(chips)]
        for cp in first:
            cp.start()
        passed = [copy(4 + j, (*chip, c), sibling) for j, chip in enumerate(chips)]
        for j, chip in enumerate(chips):
            copy(1 + j, (*chip, c), me).wait_recv()
            passed[j].start()
        copy(0, sibling, me).wait_recv()
        for j, chip in enumerate(chips):
            copy(4 + j, (*chip, 1 - c), me).wait_recv()
        for cp in first + passed:
            cp.wait_send()
        mine.wait()

        tot = all_ref[0:SMALL_ROWS, :]
        for d in range(1, N_DEV):
            tot = tot + all_ref[d * SMALL_ROWS:(d + 1) * SMALL_ROWS, :]
        delta, mn, vn = _adamw_math(w_ref[...], tot, m_ref[...], v_ref[...])
        gs_o[...] = tot
        d_o[...] = delta
        m_o[...] = mn
        v_o[...] = vn

    vm = pl.BlockSpec(memory_space=pltpu.VMEM)
    shp = jax.ShapeDtypeStruct((SMALL_ROWS, LANES), F32)
    res = pl.pallas_call(
        body, name="small_allreduce_adamw", in_specs=[vm] * 4, out_specs=[vm] * 5,
        out_shape=[jax.ShapeDtypeStruct((N_DEV * SMALL_ROWS, LANES), F32), shp, shp, shp, shp],
        scratch_shapes=[pltpu.SemaphoreType.DMA((7,)), pltpu.SemaphoreType.DMA((7,)), pltpu.SemaphoreType.DMA],
    )(g, w, m, v)
    return res[1:]


SMALL_NAMES = ("g_mix", "b_gate", "b_forget", "qn_swa", "kn_swa", "sink_swa", "rel_bias", "qn_fox", "kn_fox",
               "g_mem", "qn_mem", "kn_mem", "g_mlp")
BIG_NAMES = ("w_in", "w_mem_kv", "w_o_swa", "w_o_fox", "w_o_mem", "w_out", "w_mlp_up", "w_mlp_down")
WEIGHT_NAMES = ("g_mix", "w_in", "b_gate", "b_forget", "qn_swa", "kn_swa", "sink_swa", "rel_bias", "qn_fox", "kn_fox",
                "g_mem", "w_mem_kv", "qn_mem", "kn_mem", "w_o_swa", "w_o_fox", "w_o_mem", "w_out", "g_mlp",
                "w_mlp_up", "w_mlp_down")


def _pack_small(parts, extra=None):
    rows = []
    for n in SMALL_NAMES:
        flat = parts[n].reshape(-1).astype(F32)
        flat = jnp.pad(flat, (0, (-flat.size) % LANES))
        rows.append(flat.reshape(-1, LANES))
    if extra is not None:
        rows.append(jnp.pad(extra.reshape(1, 1), ((0, 0), (0, LANES - 1))))
    packed = jnp.concatenate(rows, axis=0)
    return jnp.pad(packed, ((0, SMALL_ROWS - packed.shape[0]), (0, 0)))


def _unpack_small(packed, shapes):
    out, r = {}, 0
    for n in SMALL_NAMES:
        size = math.prod(shapes[n])
        nr = -(-size // LANES)
        out[n] = packed[r:r + nr].reshape(-1)[:size].reshape(shapes[n])
        r += nr
    return out, packed[r, 0]


W_IN_SEGMENTS = ((C_QA, 0, 512), (C_QF, 768, 512), (C_KF, 1280, 512), (C_VF, 1792, 512), (C_QM, 2312, 512),
                 (C_KA, 512, 128), (C_VA, 640, 128), (C_FL, 2304, FOX_HEADS), (C_GL, 2824, GATE_W))
RELAYOUT_ROWS = 256


def _permute_pieces(src_of_dst):
    blocks = []
    for b in range(len(src_of_dst) // LANES):
        runs, lane = [], 0
        while lane < LANES:
            src = src_of_dst[b * LANES + lane]
            if src is None:
                lane += 1
                continue
            plane, col = src
            end = lane + 1
            while (end < LANES and src_of_dst[b * LANES + end] == (plane, col + end - lane)
                   and (col + end - lane) // LANES == col // LANES):
                end += 1
            runs.append((plane, col // LANES, (lane - col) % LANES, lane, end))
            lane = end
        blocks.append(runs)
    return blocks


def _permuted_block(runs, load, rows):
    lane = _lane((rows, LANES))
    acc = jnp.zeros((rows, LANES), F32)
    for plane, blk, shift, lo, hi in runs:
        x = load(plane, blk).astype(F32)
        if shift:
            x = pltpu.roll(x, shift, 1)
        acc = x if (lo, hi) == (0, LANES) else jnp.where((lane >= lo) & (lane < hi), x, acc)
    return acc


def _w_in_to_segments(g_in):
    src_of_dst = [None] * PROJ_W
    for mine, theirs, width in W_IN_SEGMENTS:
        for k in range(width):
            src_of_dst[mine + k] = ((theirs + k) // IN_SHARD, (theirs + k) % IN_SHARD)
    blocks = _permute_pieces(src_of_dst)
    rb = RELAYOUT_ROWS

    def body(src_ref, out_ref):
        for b, runs in enumerate(blocks):
            blk = _permuted_block(runs, lambda p, c: src_ref[p, :, c * LANES:(c + 1) * LANES], rb)
            out_ref[:, b * LANES:(b + 1) * LANES] = blk.astype(out_ref.dtype)

    return pl.pallas_call(
        body, name="w_in_to_segments", grid=(D_MODEL // rb,),
        in_specs=[pl.BlockSpec((N_SHARD, rb, IN_SHARD_PAD), lambda i: (0, i, 0))],
        out_specs=pl.BlockSpec((rb, PROJ_W), lambda i: (i, 0)),
        out_shape=jax.ShapeDtypeStruct((D_MODEL, PROJ_W), g_in.dtype),
        compiler_params=_cparams("parallel", vmem=VMEM_MID),
    )(g_in)


def _w_in_from_segments(lo, gl):
    mine_of_theirs = {}
    for mine, theirs, width in W_IN_SEGMENTS:
        for k in range(width):
            mine_of_theirs[theirs + k] = mine + k
    src_of_dst = [None] * (N_SHARD * IN_SHARD_PAD)
    for s in range(N_SHARD):
        for l in range(IN_SHARD):
            j = mine_of_theirs[s * IN_SHARD + l]
            src_of_dst[s * IN_SHARD_PAD + l] = (j // LO_W, j % LO_W)
    blocks = _permute_pieces(src_of_dst)
    per_slot = IN_SHARD_PAD // LANES
    rb = RELAYOUT_ROWS

    def body(lo_ref, gl_ref, out_ref):
        planes = (lo_ref, gl_ref)
        for b, runs in enumerate(blocks):
            blk = _permuted_block(runs, lambda p, c: planes[p][:, c * LANES:(c + 1) * LANES], rb)
            c0 = (b % per_slot) * LANES
            out_ref[b // per_slot, :, c0:c0 + LANES] = blk

    half = pl.BlockSpec((rb, LO_W), lambda i: (i, 0))
    return pl.pallas_call(
        body, name="w_in_from_segments", grid=(D_MODEL // rb,),
        in_specs=[half, half],
        out_specs=pl.BlockSpec((N_SHARD, rb, IN_SHARD_PAD), lambda i: (0, i, 0)),
        out_shape=jax.ShapeDtypeStruct((N_SHARD, D_MODEL, IN_SHARD_PAD), F32),
        compiler_params=_cparams("parallel", vmem=VMEM_MID),
    )(lo, gl)


def _after(first, then):
    return lax.optimization_barrier((first, then))


class _ReduceGroup:
    def __init__(self, tag, first_collective_id, place):
        self.tag, self.first_id, self.place = tag, first_collective_id, place

    def start(self, local, tie):
        self.names = tuple(local)
        mine, tie = _after([local[n] for n in self.names], tie)
        self.mine = mine
        self.staged = _pair_exchange("pair_exchange_" + self.tag, self.first_id, mine)
        return tie

    def send(self, tie):
        staged, tie = _after(self.staged, tie)
        sums = [_pair_sum("pair_sum_" + n, self.place, g, st) for n, g, st in zip(self.names, self.mine, staged)]
        travel, tie = _after([s[0] for s in sums], tie)
        self.own = [s[1] for s in sums]
        self.got = _chip_exchange("chip_exchange_" + self.tag, self.first_id + 1, travel)
        return tie

    def finish(self, tie):
        got, tie = _after(self.got, tie)
        halves = [_final_sum("final_sum_" + n, self.place, o, r) for n, o, r in zip(self.names, self.own, got)]
        halves, tie = _after(halves, tie)
        summed = _pair_gather("pair_gather_" + self.tag, self.first_id + 2, halves)
        self.summed = dict(zip(self.names, summed))
        return tie


class _GradReducer:
    def __init__(self, place):
        self.early = _ReduceGroup("early", 2, place)
        self.late = _ReduceGroup("late", 5, place)

    @staticmethod
    def _slot_rows(a):
        return a.reshape(N_SHARD, a.shape[0] // N_SHARD, a.shape[1])

    def early_start(self, g, tie):
        return self.early.start({"w_mlp_down": self._slot_rows(g["w_mlp_down"]), "w_mlp_up": g["w_mlp_up"],
                                 "w_out": self._slot_rows(g["w_out"]), "w_mem_kv": self._slot_rows(g["w_mem_kv"]),
                                 "w_o_swa": g["w_o_swa"], "w_o_fox": g["w_o_fox"], "w_o_mem": g["w_o_mem"]}, tie)

    def early_send(self, tie):
        return self.early.send(tie)

    def early_finish(self, tie):
        return self.early.finish(tie)

    def late_start(self, g, tie):
        d_in = _w_in_from_segments(g["wc_lo"], g["wc_gl"])
        return self.late.start({"w_in": d_in}, tie)

    def late_send(self, tie):
        return self.late.send(tie)

    def late_finish(self, tie):
        return self.late.finish(tie)

    @property
    def summed(self):
        return {**self.early.summed, **self.late.summed}


def kernel(x, mem, g_mix, w_in, b_gate, b_forget, qn_swa, kn_swa, sink_swa, rel_bias, qn_fox, kn_fox, g_mem, w_mem_kv, qn_mem, kn_mem, w_o_swa, w_o_fox, w_o_mem, w_out, g_mlp, w_mlp_up, w_mlp_down, loss_target, m_g_mix, m_w_in, m_b_gate, m_b_forget, m_qn_swa, m_kn_swa, m_sink_swa, m_rel_bias, m_qn_fox, m_kn_fox, m_g_mem, m_w_mem_kv, m_qn_mem, m_kn_mem, m_w_o_swa, m_w_o_fox, m_w_o_mem, m_w_out, m_g_mlp, m_w_mlp_up, m_w_mlp_down, v_g_mix, v_w_in, v_b_gate, v_b_forget, v_qn_swa, v_kn_swa, v_sink_swa, v_rel_bias, v_qn_fox, v_kn_fox, v_g_mem, v_w_mem_kv, v_qn_mem, v_kn_mem, v_w_o_swa, v_w_o_fox, v_w_o_mem, v_w_out, v_g_mlp, v_w_mlp_up, v_w_mlp_down):
    given = dict(locals())
    W = {n: given[n] for n in WEIGHT_NAMES}
    M = {n: given["m_" + n] for n in WEIGHT_NAMES}
    V = {n: given["v_" + n] for n in WEIGHT_NAMES}
    pad_in = ((0, 0), (0, IN_SHARD_PAD - IN_SHARD))

    shards = [jnp.pad(w_in[0].astype(BF16), pad_in)] + [W[n][0].astype(BF16) for n in BIG_NAMES[1:]]
    slots = [jnp.broadcast_to(s[None], (N_SHARD,) + s.shape) for s in shards]
    (g_in,) = _all_gather_shards_async("all_gather_w_in", 1, slots[:1])
    small = {n: (W[n] if n == "rel_bias" else W[n].reshape(1, -1)) for n in SMALL_NAMES}
    h = _rmsnorm("rms_mix", x[0], small["g_mix"], min(512, x.shape[1]))
    g_in, late, h, (m_in, v_in) = lax.optimization_barrier((g_in, slots[1:], h, (M["w_in"][0], V["w_in"][0])))
    M["w_in"], V["w_in"] = m_in[None], v_in[None]
    g_kv, g_oa, g_of, g_om, g_out, g_up, g_down = _all_gather_shards_async("all_gather_weights_async", 8, late)

    place = jnp.stack([2 * lax.axis_index("x") + lax.axis_index("y"), lax.axis_index("c")]).astype(jnp.int32)
    reducer = _GradReducer(place)
    loss, grad_x, grads = _local_step(
        x[0], h, mem[0], loss_target[0], small, g_in, g_kv.reshape(D_MODEL, D_MODEL), (g_oa, g_of, g_om),
        g_out.reshape(D_MODEL, D_MODEL), g_up, g_down.reshape(D_FF, D_MODEL), reducer)

    out = {}

    def adamw_of(names, summed):
        for n in names:
            res = _adamw("adamw_" + n, W[n][0], summed[n], M[n][0], V[n][0])
            out[n] = [r.reshape(W[n].shape) for r in res]

    adamw_of(reducer.early.names, reducer.early.summed)
    shapes = {n: W[n].shape for n in SMALL_NAMES}
    packed = _small_allreduce_adamw(_pack_small(grads, loss), _pack_small(W), _pack_small(M), _pack_small(V))
    done_meanwhile = ([out[n] for n in reducer.early.names], packed)
    (early_out, packed), grad_x = reducer.late_finish((done_meanwhile, grad_x))
    for n, res in zip(reducer.early.names, early_out):
        out[n] = res
    adamw_of(reducer.late.names, reducer.late.summed)
    unpacked = [_unpack_small(p, shapes) for p in packed]
    for n in SMALL_NAMES:
        out[n] = [u[0][n] for u in unpacked]
    loss_total = unpacked[0][1]

    return (loss_total, grad_x.reshape(x.shape),
            *[out[n][0] for n in WEIGHT_NAMES], *[out[n][1] for n in WEIGHT_NAMES],
            *[out[n][2] for n in WEIGHT_NAMES], *[out[n][3] for n in WEIGHT_NAMES])
```

```python
import functools
import math

import jax
import jax.numpy as jnp
from jax import lax
from jax.experimental import pallas as pl
from jax.experimental.pallas import tpu as pltpu
from jax.experimental.pallas import tpu_sc as plsc

F32 = jnp.float32
BF16 = jnp.bfloat16

D_MODEL = 1024
N_MEM = 256
SWA_HEADS = 8
SWA_KV_HEADS = 2
SWA_HEAD_DIM = 64
WINDOW = 128
FOX_HEADS = 8
FOX_HEAD_DIM = 64
MEM_HEADS = 4
MEM_HEAD_DIM = 128
D_FF = 4 * D_MODEL
REL_BUCKETS = 32
REL_MAX_DIST = 128
EPS = 1e-6
NEG = -1e30
GATE_W = 3 * D_MODEL
IN_WIDTH = 5896
N_SHARD = 4
IN_SHARD = IN_WIDTH // N_SHARD
IN_SHARD_PAD = 1536

ADAM_LR = 0.001
ADAM_B1 = 0.9
ADAM_B2 = 0.999
ADAM_EPS = 1e-08
ADAM_WD = 0.01
ADAM_STEP = 10

LANES = 128
V7X_VMEM_BYTES = 64 * 1024 * 1024
MIB = 1024 * 1024
VMEM_SMALL, VMEM_MID, VMEM_BIG, VMEM_MAX = 24 * MIB, 40 * MIB, 48 * MIB, 56 * MIB

C_QA, C_QF, C_KF, C_VF, C_QM, C_KA, C_VA, C_FL, C_GL = 0, 512, 1024, 1536, 2048, 2560, 2688, 2816, 3072
LO_W = 3072
PROJ_W = 6144

NN = (((1,), (0,)), ((), ()))
NT = (((1,), (1,)), ((), ()))
TN = (((0,), (0,)), ((), ()))


def _dot(a, b, dims=NN):
    return lax.dot_general(a, b, dims, preferred_element_type=F32)


def _cparams(*sem, vmem=VMEM_SMALL):
    return pltpu.CompilerParams(dimension_semantics=sem, vmem_limit_bytes=vmem)


def _split3(a):
    hi = a.astype(BF16)
    r1 = a - hi.astype(F32)
    mid = r1.astype(BF16)
    lo = (r1 - mid.astype(F32)).astype(BF16)
    return hi, mid, lo


def _group_mean(a, g2):
    hi = a.astype(BF16)
    mid = (a - hi.astype(F32)).astype(BF16)
    return _dot(jnp.concatenate([hi, mid], axis=1), g2)


def _dot3_left(g, a):
    hi, mid, lo = _split3(a)
    return _dot(g, hi) + _dot(g, mid) + _dot(g, lo)


def _group_mean_matrix(d):
    r = jnp.arange(LANES)
    g = jnp.where((r[:, None] // d) == (r[None, :] // d), 1.0 / d, 0.0).astype(BF16)
    return jnp.concatenate([g, g], axis=0)


def _lane(shape):
    return lax.broadcasted_iota(jnp.int32, shape, len(shape) - 1)


def _matmul(name, a, b, *, dims, grid, a_spec, b_spec, acc_shape, outs, epilogue, extra=(), vmem=VMEM_BIG):
    nk = grid[2]
    n_extra = len(extra)

    def body(a_ref, b_ref, *rest):
        extra_refs = rest[:n_extra]
        out_refs = rest[n_extra:n_extra + len(outs)]
        i, j, k = pl.program_id(0), pl.program_id(1), pl.program_id(2)
        if nk == 1:
            epilogue(_dot(a_ref[...].astype(BF16), b_ref[...].astype(BF16), dims), extra_refs, out_refs, (i, j))
            return
        acc_ref = rest[-1]

        @pl.when(k == 0)
        def _():
            acc_ref[...] = jnp.zeros_like(acc_ref)

        acc_ref[...] += _dot(a_ref[...].astype(BF16), b_ref[...].astype(BF16), dims)

        @pl.when(k == nk - 1)
        def _():
            epilogue(acc_ref[...], extra_refs, out_refs, (i, j))

    res = pl.pallas_call(
        body,
        name=name,
        grid=grid,
        in_specs=[a_spec, b_spec] + [s for _, s in extra],
        out_specs=[s for _, s in outs],
        out_shape=[s for s, _ in outs],
        scratch_shapes=[pltpu.VMEM(acc_shape, F32)] if nk > 1 else [],
        compiler_params=_cparams("arbitrary", "arbitrary", "arbitrary", vmem=vmem),
    )(a, b, *[x for x, _ in extra])
    return res


def _epi_store(acc, extra_refs, out_refs, ij):
    out_refs[0][...] = acc.astype(out_refs[0].dtype)


def _rms_rows(x, g):
    r = lax.rsqrt(jnp.mean(x * x, axis=-1, keepdims=True) + EPS)
    return x * r, r


def _rmsnorm_bwd_rows(dh, x, g):
    xhat, r = _rms_rows(x, g)
    dxh = dh * g
    dx = r * (dxh - xhat * jnp.mean(dxh * xhat, axis=-1, keepdims=True))
    return dx, jnp.sum(dh * xhat, axis=0, keepdims=True)


def _rmsnorm(name, x, g, tb):
    T, Dm = x.shape

    def body(x_ref, g_ref, o_ref):
        xhat, _ = _rms_rows(x_ref[...], None)
        o_ref[...] = (xhat * g_ref[...]).astype(o_ref.dtype)

    return pl.pallas_call(
        body, name=name, grid=(T // tb,),
        in_specs=[pl.BlockSpec((tb, Dm), lambda i: (i, 0)), pl.BlockSpec((1, Dm), lambda i: (0, 0))],
        out_specs=pl.BlockSpec((tb, Dm), lambda i: (i, 0)),
        out_shape=jax.ShapeDtypeStruct((T, Dm), BF16),
        compiler_params=_cparams("parallel"),
    )(x, g)


def _head_norm(x, gm, gain):
    ms = _group_mean(x * x, gm)
    r = lax.rsqrt(ms + EPS)
    return x * r * gain, x * r


def _head_norm_bwd(dy, x, gm, gain):
    ms = _group_mean(x * x, gm)
    r = lax.rsqrt(ms + EPS)
    xhat = x * r
    dxh = dy * gain
    dx = r * (dxh - xhat * _group_mean(dxh * xhat, gm))
    return dx, jnp.sum(dy * xhat, axis=0, keepdims=True)


def _log_sigmoid(z):
    return jnp.minimum(z, 0.0) - jnp.log(1.0 + jnp.exp(-jnp.abs(z)))


def _prep_fwd(proj, gains, bfor, tril, gm64, gm128, T, tb):
    nb = T // tb

    def body(qa_ref, qf_ref, kf_ref, vf_ref, qm_ref, ka_ref, va_ref, fl_ref, gains_ref, bfor_ref, tril_ref,
             gm64_ref, gm128_ref,
             qa_o, qf_o, kf_o, vf_o, qm_o, kad_o, vad_o, qaug_o, kaug_o, carry):
        i = pl.program_id(0)
        gm64v = gm64_ref[...]
        gm128v = gm128_ref[...]
        lane = _lane((tb, LANES))

        def norm512(src, dst, row, gm, scale=1.0):
            gain = gains_ref[row:row + 1, :]
            for c in range(4):
                sl = slice(c * LANES, (c + 1) * LANES)
                y, _ = _head_norm(src[:, sl], gm, gain)
                dst[:, sl] = (y * scale).astype(dst.dtype)

        norm512(qa_ref, qa_o, 0, gm64v)
        norm512(qf_ref, qf_o, 2, gm64v, FOX_SCALE)
        norm512(kf_ref, kf_o, 3, gm64v)
        norm512(qm_ref, qm_o, 4, gm128v)
        vf_o[...] = vf_ref[...].astype(vf_o.dtype)

        ka_n, _ = _head_norm(ka_ref[...], gm64v, gains_ref[1:2, :])
        ka_r = pltpu.roll(ka_n, 64, 1)
        va = va_ref[...]
        va_r = pltpu.roll(va, 64, 1)
        lo = lane < 64
        kad_o[0] = jnp.where(lo, ka_n, ka_r).astype(kad_o.dtype)
        kad_o[1] = jnp.where(lo, ka_r, ka_n).astype(kad_o.dtype)
        vad_o[0] = jnp.where(lo, va, va_r).astype(vad_o.dtype)
        vad_o[1] = jnp.where(lo, va_r, va).astype(vad_o.dtype)

        @pl.when(i == 0)
        def _():
            carry[...] = jnp.zeros_like(carry)

        logf = jnp.where(lane < FOX_HEADS, _log_sigmoid(fl_ref[...] + bfor_ref[...]), 0.0)
        c = _dot3_left(tril_ref[...], logf) + carry[0:1, :]
        carry[...] = jnp.broadcast_to(c[tb - 1:tb, :], carry.shape)
        for pair in range(FOX_HEADS // 2):
            qaug = jnp.zeros((tb, LANES), F32)
            kaug = jnp.zeros((tb, LANES), F32)
            for sub in range(2):
                col = jnp.sum(jnp.where(lane == 2 * pair + sub, c, 0.0), axis=1, keepdims=True)
                pieces = [p.astype(F32) for p in _split3(col)]
                base = AUG_STRIDE * sub
                for e in range(3):
                    qaug = jnp.where(lane == base + AUG_C + e, pieces[e], qaug)
                    kaug = jnp.where(lane == base + AUG_NEG_C + e, -pieces[e], kaug)
                qaug = jnp.where((lane >= base + AUG_NEG_C) & (lane < base + AUG_NEG_C + 3), 1.0, qaug)
                ones_k = ((lane >= base + AUG_C) & (lane < base + AUG_C + 3)) | (
                    (lane >= base + AUG_STAT) & (lane < base + AUG_STAT + 3))
                kaug = jnp.where(ones_k, 1.0, kaug)
            sl = slice(pair * LANES, (pair + 1) * LANES)
            qaug_o[:, sl] = qaug.astype(BF16)
            kaug_o[:, sl] = kaug.astype(BF16)

    def seg(width, start):
        return pl.BlockSpec((tb, width), lambda i, s=start // width: (i, s))

    const = lambda shape: pl.BlockSpec(shape, lambda i: tuple(0 for _ in shape))
    rows512 = pl.BlockSpec((tb, 512), lambda i: (i, 0))
    outs = pl.pallas_call(
        body, name="prep_fwd", grid=(nb,),
        in_specs=[seg(512, C_QA), seg(512, C_QF), seg(512, C_KF), seg(512, C_VF), seg(512, C_QM),
                  seg(128, C_KA), seg(128, C_VA), seg(128, C_FL),
                  const((8, LANES)), const((1, LANES)), const((tb, tb)), const((2 * LANES, LANES)), const((2 * LANES, LANES))],
        out_specs=[rows512, rows512, rows512, rows512, rows512,
                   pl.BlockSpec((2, tb, LANES), lambda i: (0, i, 0)), pl.BlockSpec((2, tb, LANES), lambda i: (0, i, 0)),
                   rows512, rows512],
        out_shape=[jax.ShapeDtypeStruct((T, 512), BF16)] * 5
        + [jax.ShapeDtypeStruct((2, T, LANES), BF16)] * 2
        + [jax.ShapeDtypeStruct((T, 512), BF16)] * 2,
        scratch_shapes=[pltpu.VMEM((8, LANES), F32)],
        compiler_params=_cparams("arbitrary", vmem=VMEM_MID),
    )(proj, proj, proj, proj, proj, proj, proj, proj, gains, bfor, tril, gm64, gm128)
    return outs


def _prep_bwd(proj, dqa, dkad, dvad, dqf, dkf, dvf, dqm, dqf_aug, dkf_aug, gains, bfor, triu, gm64, gm128, T, tb):
    nb = T // tb

    def body(qa_ref, qf_ref, kf_ref, qm_ref, ka_ref, fl_ref,
             dqa_ref, dkad_ref, dvad_ref, dqf_ref, dkf_ref, dvf_ref, dqm_ref, dqfa_ref, dkfa_ref,
             gains_ref, bfor_ref, triu_ref, gm64_ref, gm128_ref,
             dlo_o, gacc_o, carry):
        i = pl.program_id(0)
        gm64v = gm64_ref[...]
        gm128v = gm128_ref[...]
        lane = _lane((tb, LANES))

        @pl.when(i == 0)
        def _():
            carry[...] = jnp.zeros_like(carry)
            gacc_o[...] = jnp.zeros_like(gacc_o)

        def norm512_bwd(dsrc, xsrc, col0, row, gm):
            gain = gains_ref[row:row + 1, :]
            gsum = jnp.zeros((1, LANES), F32)
            for c in range(4):
                sl = slice(c * LANES, (c + 1) * LANES)
                dx, dg = _head_norm_bwd(dsrc[:, sl], xsrc[:, sl], gm, gain)
                dlo_o[:, col0 + c * LANES:col0 + (c + 1) * LANES] = dx.astype(dlo_o.dtype)
                gsum = gsum + dg
            gacc_o[row:row + 1, :] += gsum

        norm512_bwd(dqa_ref, qa_ref, C_QA, 0, gm64v)
        norm512_bwd(dqf_ref, qf_ref, C_QF, 2, gm64v)
        norm512_bwd(dkf_ref, kf_ref, C_KF, 3, gm64v)
        norm512_bwd(dqm_ref, qm_ref, C_QM, 4, gm128v)
        dlo_o[:, C_VF:C_VF + 512] = dvf_ref[...].astype(dlo_o.dtype)

        lo = lane < 64

        def fold(ref):
            f0 = ref[0] + pltpu.roll(ref[0], 64, 1)
            f1 = ref[1] + pltpu.roll(ref[1], 64, 1)
            return jnp.where(lo, f0, f1)

        dka, dg = _head_norm_bwd(fold(dkad_ref), ka_ref[...], gm64v, gains_ref[1:2, :])
        gacc_o[1:2, :] += dg
        dlo_o[:, C_KA:C_KA + LANES] = dka.astype(dlo_o.dtype)
        dlo_o[:, C_VA:C_VA + LANES] = fold(dvad_ref).astype(dlo_o.dtype)

        dc = jnp.zeros((tb, LANES), F32)
        for pair in range(FOX_HEADS // 2):
            sl = slice(pair * LANES, (pair + 1) * LANES)
            rows_sum, cols_sum = dqfa_ref[:, sl], dkfa_ref[:, sl]
            for sub in range(2):
                diff = (jnp.where(lane == AUG_STRIDE * sub + AUG_C, rows_sum, 0.0)
                        - jnp.where(lane == AUG_STRIDE * sub + AUG_NEG_C, cols_sum, 0.0))
                dc = jnp.where(lane == 2 * pair + sub, jnp.sum(diff, axis=1, keepdims=True), dc)
        dlogf = _dot3_left(triu_ref[...], dc) + carry[0:1, :]
        carry[...] = jnp.broadcast_to(dlogf[0:1, :], carry.shape)
        z = fl_ref[...] + bfor_ref[...]
        dfl = jnp.where(lane < FOX_HEADS, dlogf / (1.0 + jnp.exp(z)), 0.0)
        gacc_o[5:6, :] += jnp.sum(dfl, axis=0, keepdims=True)
        dlo_o[:, C_FL:C_FL + LANES] = dfl.astype(dlo_o.dtype)
        dlo_o[:, C_FL + LANES:C_FL + 2 * LANES] = jnp.zeros((tb, LANES), dlo_o.dtype)

    rev = lambda i: nb - 1 - i

    def seg(width, start):
        return pl.BlockSpec((tb, width), lambda i, s=start // width: (rev(i), s))

    const = lambda shape: pl.BlockSpec(shape, lambda i: tuple(0 for _ in shape))
    rows512 = pl.BlockSpec((tb, 512), lambda i: (rev(i), 0))
    dup = pl.BlockSpec((2, tb, LANES), lambda i: (0, rev(i), 0))
    return pl.pallas_call(
        body, name="prep_bwd", grid=(nb,),
        in_specs=[seg(512, C_QA), seg(512, C_QF), seg(512, C_KF), seg(512, C_QM), seg(128, C_KA), seg(128, C_FL),
                  rows512, dup, dup, rows512, rows512, rows512, rows512, rows512, rows512,
                  const((8, LANES)), const((1, LANES)), const((tb, tb)), const((2 * LANES, LANES)), const((2 * LANES, LANES))],
        out_specs=[pl.BlockSpec((tb, LO_W), lambda i: (rev(i), 0)), const((8, LANES))],
        out_shape=[jax.ShapeDtypeStruct((T, LO_W), BF16), jax.ShapeDtypeStruct((8, LANES), F32)],
        scratch_shapes=[pltpu.VMEM((8, LANES), F32)],
        compiler_params=_cparams("arbitrary", vmem=VMEM_MID),
    )(proj, proj, proj, proj, proj, proj, dqa, dkad, dvad, dqf, dkf, dvf, dqm, dqf_aug, dkf_aug,
      gains, bfor, triu, gm64, gm128)


FOX_SCALE = FOX_HEAD_DIM ** -0.5
AUG_STRIDE = 16
AUG_C = 0
AUG_NEG_C = 3
AUG_STAT = 6
FOX_TQ, FOX_TK = 1024, 1024
FOX_BWD_TQ, FOX_BWD_TK = 1024, 1024
FOX_DIAGONAL_PARTS = 4


def _fox_head_mask(sub, rows):
    lane = _lane((rows, 2 * LANES))
    main = (lane >= 64 * sub) & (lane < 64 * sub + 64)
    aug = (lane >= LANES + AUG_STRIDE * sub) & (lane < LANES + AUG_STRIDE * (sub + 1))
    return main | aug


def _fox_pieces(diagonal, tq, tk):
    if diagonal and tq == tk and tq >= FOX_DIAGONAL_PARTS * LANES:
        step = tq // FOX_DIAGONAL_PARTS
        return [(n * step, (n + 1) * step, (n + 1) * step) for n in range(FOX_DIAGONAL_PARTS)]
    return [(0, tq, tk)]


def _fox_fwd(q, qaug, k, kaug, v, T, tq, tk):
    nq, nk = T // tq, T // tk
    rep = tk // LANES
    last_of = lambda i: (i * tq + tq - 1) // tk

    def body(q_ref, qa_ref, k_ref, ka_ref, v_ref, o_ref, qab_ref, m_s, acc_s):
        p_, i, j = pl.program_id(0), pl.program_id(1), pl.program_id(2)
        last = last_of(i)

        @pl.when(j == 0)
        def _():
            m_s[...] = jnp.full(m_s.shape, NEG, F32)
            acc_s[...] = jnp.zeros_like(acc_s)

        def step(diagonal):
            k2 = jnp.concatenate([k_ref[...], ka_ref[...]], axis=1)
            v2 = jnp.concatenate([v_ref[...], ka_ref[...]], axis=1)
            pieces = _fox_pieces(diagonal, tq, tk)
            work = []
            for r0, r1, nc in pieces:
                rows = slice(r0, r1)
                q2 = jnp.concatenate([q_ref[rows, :], qa_ref[rows, :]], axis=1)
                for sub in range(2):
                    qh = jnp.where(_fox_head_mask(sub, r1 - r0), q2, jnp.zeros_like(q2))
                    work.append((rows, r0, r1 - r0, nc, sub, _dot(qh, k2[:nc], NT)))
            for rows, r0, nr, nc, sub, s in work:
                if diagonal:
                    causal = (lax.broadcasted_iota(jnp.int32, (nr, nc), 1) + j * tk
                              <= lax.broadcasted_iota(jnp.int32, (nr, nc), 0) + (r0 + i * tq))
                    s = jnp.where(causal, s, NEG)
                m_prev = m_s[sub, rows, :]
                m_next = jnp.maximum(m_prev, jnp.max(s, axis=1, keepdims=True))
                p = jnp.exp(s - jnp.tile(m_next, (1, nc // LANES)))
                alpha = jnp.exp(m_prev - m_next)
                m_s[sub, rows, :] = m_next
                acc_s[sub, rows, :] = acc_s[sub, rows, :] * jnp.tile(alpha, (1, 2)) + _dot(p.astype(BF16), v2[:nc])

        @pl.when(j == last)
        def _():
            step(True)

        @pl.when(j < last)
        def _():
            step(False)

        @pl.when(j == nk - 1)
        def _():
            lane = _lane((tq, LANES))
            outs = []
            qab = qa_ref[...].astype(F32)
            for sub in range(2):
                acc = acc_s[sub]
                base = AUG_STRIDE * sub
                l = jnp.sum(jnp.where(lane == base + AUG_C, acc[:, LANES:], 0.0), axis=1, keepdims=True)
                outs.append(acc[:, :LANES] / l)
                lse = jnp.max(m_s[sub], axis=1, keepdims=True) + jnp.log(l)
                pieces = _split3(-lse)
                for e in range(3):
                    qab = jnp.where(lane == base + AUG_STAT + e, pieces[e].astype(F32), qab)
            o_ref[...] = jnp.where(lane < 64, outs[0], outs[1]).astype(o_ref.dtype)
            qab_ref[...] = qab.astype(BF16)

    qspec = pl.BlockSpec((tq, LANES), lambda p, i, j: (i, p))
    kspec = pl.BlockSpec((tk, LANES), lambda p, i, j: (jnp.minimum(j, last_of(i)), p))
    return pl.pallas_call(
        body, name="fox_fwd", grid=(4, nq, nk),
        in_specs=[qspec, qspec, kspec, kspec, kspec],
        out_specs=[qspec, qspec],
        out_shape=[jax.ShapeDtypeStruct((T, 512), BF16), jax.ShapeDtypeStruct((T, 512), BF16)],
        scratch_shapes=[pltpu.VMEM((2, tq, LANES), F32), pltpu.VMEM((2, tq, 2 * LANES), F32)],
        compiler_params=_cparams("parallel", "parallel", "arbitrary", vmem=VMEM_BIG),
    )(q, qaug, k, kaug, v)


def _fox_bwd(q, qaug, k, kaug, v, do, doaug, T, tq, tk):
    nq, nk = T // tq, T // tk
    first_of = lambda j: (j * tk) // tq

    def body(q_ref, qa_ref, k_ref, ka_ref, v_ref, do_ref, doa_ref,
             dq_ref, dqa_ref, dk_ref, dka_ref, dv_ref, dk_s, dv_s):
        p_, j, i = pl.program_id(0), pl.program_id(1), pl.program_id(2)
        masked = i * tq < (j + 1) * tk - 1

        @pl.when((j == 0) & (i == 0))
        def _():
            dq_ref[...] = jnp.zeros_like(dq_ref)
            dqa_ref[...] = jnp.zeros_like(dqa_ref)

        @pl.when(i == 0)
        def _():
            dk_s[...] = jnp.zeros_like(dk_s)
            dv_s[...] = jnp.zeros_like(dv_s)

        def step(diagonal):
            k2 = jnp.concatenate([k_ref[...], ka_ref[...]], axis=1)
            v2 = jnp.concatenate([v_ref[...], ka_ref[...]], axis=1)
            work = []
            for r0, r1, nc in _fox_pieces(diagonal, tq, tk):
                rows = slice(r0, r1)
                q2 = jnp.concatenate([q_ref[rows, :], qa_ref[rows, :]], axis=1)
                do2 = jnp.concatenate([do_ref[rows, :], doa_ref[rows, :]], axis=1)
                for sub in range(2):
                    hm = _fox_head_mask(sub, r1 - r0)
                    qh = jnp.where(hm, q2, jnp.zeros_like(q2))
                    doh = jnp.where(hm, do2, jnp.zeros_like(do2))
                    s = _dot(qh, k2[:nc], NT)
                    dp = _dot(doh, v2[:nc], NT)
                    work.append((r0, r1 - r0, nc, sub, qh, doh, s, dp))
            dqs = {}
            for r0, nr, nc, sub, qh, doh, s, dp in work:
                if diagonal:
                    causal = (lax.broadcasted_iota(jnp.int32, (nr, nc), 1) + j * tk
                              <= lax.broadcasted_iota(jnp.int32, (nr, nc), 0) + (r0 + i * tq))
                    s = jnp.where(causal, s, NEG)
                p = jnp.exp(s)
                dsb = (p * dp).astype(BF16)
                dv_s[0:nc, :] += _dot(p.astype(BF16), doh[:, :LANES], TN)
                dk_s[0:nc, :] += _dot(dsb, qh, TN)
                dqs[(r0, sub)] = _dot(dsb, k2[:nc])
            for r0, r1, nc in _fox_pieces(diagonal, tq, tk):
                dq2 = jnp.where(_fox_head_mask(0, r1 - r0), dqs[(r0, 0)], dqs[(r0, 1)])
                qrows = pl.ds(pl.multiple_of(i * tq + r0, r1 - r0), r1 - r0)
                dq_ref[qrows, :] += dq2[:, :LANES] * FOX_SCALE
                dqa_ref[qrows, :] += dq2[:, LANES:]

        @pl.when((i >= first_of(j)) & masked)
        def _():
            step(True)

        @pl.when((i >= first_of(j)) & jnp.logical_not(masked))
        def _():
            step(False)

        @pl.when(i == nq - 1)
        def _():
            dk_ref[...] = dk_s[:, :LANES]
            dka_ref[...] = dk_s[:, LANES:]
            dv_ref[...] = dv_s[...]

    qspec = pl.BlockSpec((tq, LANES), lambda p, j, i: (jnp.maximum(i, first_of(j)), p))
    kspec = pl.BlockSpec((tk, LANES), lambda p, j, i: (j, p))
    resident = pl.BlockSpec((T, LANES), lambda p, j, i: (0, p))
    return pl.pallas_call(
        body, name="fox_bwd", grid=(4, nk, nq),
        in_specs=[qspec, qspec, kspec, kspec, kspec, qspec, qspec],
        out_specs=[resident, resident, kspec, kspec, kspec],
        out_shape=[jax.ShapeDtypeStruct((T, 512), F32)] * 5,
        scratch_shapes=[pltpu.VMEM((tk, 2 * LANES), F32), pltpu.VMEM((tk, LANES), F32)],
        compiler_params=_cparams("arbitrary", "arbitrary", "arbitrary", vmem=VMEM_BIG),
    )(q, qaug, k, kaug, v, do, doaug)


SWA_SUB = 4
SWA_TB = SWA_SUB * WINDOW


def _t5_bucket_matrix():
    t = jnp.arange(WINDOW)[:, None] + WINDOW
    s = jnp.arange(2 * WINDOW)[None, :]
    max_exact = REL_BUCKETS // 2
    d = jnp.maximum(t - s, 0)
    df = jnp.maximum(d, 1).astype(F32)
    large = max_exact + (jnp.log(df / max_exact) / math.log(REL_MAX_DIST / max_exact)
                         * (REL_BUCKETS - max_exact)).astype(jnp.int32)
    large = jnp.minimum(large, REL_BUCKETS - 1)
    return jnp.where(d < max_exact, d, large).astype(jnp.int32)


def _swa_bias(rel_bias, bucket):
    def body(rel_ref, bucket_ref, o_ref):
        b = bucket_ref[...]
        for h in range(SWA_HEADS):
            acc = jnp.zeros(b.shape, F32)
            for r in range(REL_BUCKETS):
                acc = jnp.where(b == r, rel_ref[r, h], acc)
            o_ref[h] = acc

    return pl.pallas_call(
        body, name="swa_bias",
        in_specs=[pl.BlockSpec(memory_space=pltpu.SMEM), pl.BlockSpec(memory_space=pltpu.VMEM)],
        out_specs=pl.BlockSpec(memory_space=pltpu.VMEM),
        out_shape=jax.ShapeDtypeStruct((SWA_HEADS, WINDOW, 2 * WINDOW), F32),
    )(rel_bias, bucket)


def _swa_bias_bwd(dbias, bucket):
    def body(db_ref, bucket_ref, o_ref):
        b = bucket_ref[...]
        lane = _lane((1, LANES))
        for r in range(REL_BUCKETS):
            row = jnp.zeros((1, LANES), F32)
            for h in range(SWA_HEADS):
                part = jnp.sum(jnp.where(b == r, db_ref[h], 0.0), axis=0, keepdims=True)
                tot = jnp.sum(part, axis=1, keepdims=True)
                row = jnp.where(lane == h, tot, row)
            o_ref[r:r + 1, :] = row

    return pl.pallas_call(
        body, name="swa_bias_bwd",
        in_specs=[pl.BlockSpec(memory_space=pltpu.VMEM), pl.BlockSpec(memory_space=pltpu.VMEM)],
        out_specs=pl.BlockSpec(memory_space=pltpu.VMEM),
        out_shape=jax.ShapeDtypeStruct((REL_BUCKETS, LANES), F32),
    )(dbias, bucket)


SWA_GROUP = SWA_HEADS // SWA_KV_HEADS


def _swa_valid(r, i):
    t = (lax.broadcasted_iota(jnp.int32, (SWA_GROUP * WINDOW, 2 * WINDOW), 0) & (WINDOW - 1)) + WINDOW
    s = lax.broadcasted_iota(jnp.int32, (SWA_GROUP * WINDOW, 2 * WINDOW), 1)
    dist = t - s
    band = (dist >= 0) & (dist < WINDOW)
    if r == 0:
        band = band & ((s >= WINDOW) | (i > 0))
    return band


def _swa_stack(blk):
    lane = _lane((WINDOW, LANES))
    parts = []
    for g in range(SWA_GROUP):
        b = blk[:, LANES * (g // 2):LANES * (g // 2 + 1)]
        parts.append(jnp.where((lane >= 64) if g % 2 else (lane < 64), b, jnp.zeros_like(b)))
    return jnp.concatenate(parts, axis=0)


def _swa_unstack(st):
    lane = _lane((WINDOW, LANES))
    W = WINDOW
    return jnp.concatenate([jnp.where(lane < 64, st[2 * b * W:(2 * b + 1) * W], st[(2 * b + 1) * W:(2 * b + 2) * W])
                            for b in range(2)], axis=1)


def _swa_sink_column(sink_ref, kvh):
    row = lax.broadcasted_iota(jnp.int32, (SWA_GROUP * WINDOW, 1), 0)
    col = jnp.full((SWA_GROUP * WINDOW, 1), sink_ref[SWA_GROUP * kvh + SWA_GROUP - 1], F32)
    for g in range(SWA_GROUP - 2, -1, -1):
        col = jnp.where(row < (g + 1) * WINDOW, sink_ref[SWA_GROUP * kvh + g], col)
    return col


def _swa_specs(T):
    W = WINDOW
    qspec = pl.BlockSpec((SWA_TB, 2 * LANES), lambda h, i: (i, h))
    own = pl.BlockSpec((None, SWA_TB, LANES), lambda h, i: (h, i, 0))
    prev = pl.BlockSpec((None, W, LANES), lambda h, i: (h, jnp.maximum(SWA_SUB * i - 1, 0), 0))
    stat = pl.BlockSpec((SWA_GROUP, SWA_TB, LANES), lambda h, i: (h, i, 0))
    bias = pl.BlockSpec((None, SWA_GROUP * W, 2 * W), lambda h, i: (h, 0, 0))
    return qspec, own, prev, stat, bias


def _swa_fwd(sinks, q, kad, vad, bias, T):
    nb = T // SWA_TB
    scale = SWA_HEAD_DIM ** -0.5
    W = WINDOW

    def body(sink_ref, q_ref, k_ref, kp_ref, v_ref, vp_ref, bias_ref, o_ref, lse_ref):
        kvh, i = pl.program_id(0), pl.program_id(1)
        sink = _swa_sink_column(sink_ref, kvh)
        for r in range(SWA_SUB):
            rs = slice(r * W, (r + 1) * W)
            ps = slice((r - 1) * W, r * W)
            k_own, v_own = k_ref[rs, :], v_ref[rs, :]
            k_prev = kp_ref[...] if r == 0 else k_ref[ps, :]
            v_prev = vp_ref[...] if r == 0 else v_ref[ps, :]
            qs = _swa_stack(q_ref[rs, :])
            s = jnp.concatenate([_dot(qs, k_prev, NT), _dot(qs, k_own, NT)], axis=1) * scale + bias_ref[...]
            s = jnp.where(_swa_valid(r, i), s, NEG)
            m = jnp.maximum(jnp.max(s, axis=1, keepdims=True), sink)
            p = jnp.exp(s - m)
            denom = jnp.sum(p, axis=1, keepdims=True) + jnp.exp(sink - m)
            pn = (p / denom).astype(BF16)
            o_ref[rs, :] = _swa_unstack(_dot(pn[:, :W], v_prev) + _dot(pn[:, W:], v_own)).astype(o_ref.dtype)
            lse = m + jnp.log(denom)
            for g in range(SWA_GROUP):
                lse_ref[g, rs, :] = jnp.broadcast_to(lse[g * W:(g + 1) * W], (W, LANES))

    qspec, own, prev, stat, bspec = _swa_specs(T)
    return pl.pallas_call(
        body, name="swa_fwd", grid=(SWA_KV_HEADS, nb),
        in_specs=[pl.BlockSpec(memory_space=pltpu.SMEM), qspec, own, prev, own, prev, bspec],
        out_specs=[qspec, stat],
        out_shape=[jax.ShapeDtypeStruct((T, 512), BF16), jax.ShapeDtypeStruct((SWA_HEADS, T, LANES), F32)],
        compiler_params=_cparams("parallel", "parallel", vmem=VMEM_MID),
    )(sinks, q, kad, kad, vad, vad, bias.reshape(SWA_KV_HEADS, SWA_GROUP * W, 2 * W))


def _swa_bwd(sinks, q, kad, vad, bias, do, lse, delta, T):
    nb = T // SWA_TB
    scale = SWA_HEAD_DIM ** -0.5
    W = WINDOW

    def body(sink_ref, q_ref, k_ref, kp_ref, v_ref, vp_ref, bias_ref, do_ref, lse_ref, dl_ref,
             dq_ref, dkad_ref, dvad_ref, dbias_ref, dsk_ref):
        kvh, i = pl.program_id(0), pl.program_id(1)
        sink = _swa_sink_column(sink_ref, kvh)

        @pl.when((kvh == 0) & (i == 0))
        def _():
            dkad_ref[...] = jnp.zeros_like(dkad_ref)
            dvad_ref[...] = jnp.zeros_like(dvad_ref)

        @pl.when(i == 0)
        def _():
            dbias_ref[...] = jnp.zeros_like(dbias_ref)
            dsk_ref[...] = jnp.zeros_like(dsk_ref)

        for r in range(SWA_SUB):
            rs = slice(r * W, (r + 1) * W)
            ps = slice((r - 1) * W, r * W)
            k_own, v_own = k_ref[rs, :], v_ref[rs, :]
            k_prev = kp_ref[...] if r == 0 else k_ref[ps, :]
            v_prev = vp_ref[...] if r == 0 else v_ref[ps, :]
            qs = _swa_stack(q_ref[rs, :])
            dos = _swa_stack(do_ref[rs, :])
            lse_b = jnp.concatenate([lse_ref[g, rs, :] for g in range(SWA_GROUP)], axis=0)
            dl_b = jnp.concatenate([dl_ref[g, rs, :] for g in range(SWA_GROUP)], axis=0)
            s = jnp.concatenate([_dot(qs, k_prev, NT), _dot(qs, k_own, NT)], axis=1) * scale + bias_ref[...]
            s = jnp.where(_swa_valid(r, i), s, NEG)
            p = jnp.exp(s - jnp.tile(lse_b, (1, 2)))
            dp = jnp.concatenate([_dot(dos, v_prev, NT), _dot(dos, v_own, NT)], axis=1)
            ds = p * (dp - jnp.tile(dl_b, (1, 2)))
            sink_term = jnp.exp(sink - lse_b) * dl_b
            for g in range(SWA_GROUP):
                dbias_ref[g] += ds[g * W:(g + 1) * W]
                dsk_ref[g:g + 1, :] += jnp.sum(sink_term[g * W:(g + 1) * W], axis=0, keepdims=True)
            dsb = ds.astype(BF16)
            pb = p.astype(BF16)
            dq_ref[rs, :] = _swa_unstack((_dot(dsb[:, :W], k_prev) + _dot(dsb[:, W:], k_own)) * scale)
            own_row = pl.multiple_of(i * SWA_TB + r * W, W)
            dkad_ref[kvh, pl.ds(own_row, W), :] += _dot(dsb[:, W:], qs, TN) * scale
            dvad_ref[kvh, pl.ds(own_row, W), :] += _dot(pb[:, W:], dos, TN)
            dk_prev = _dot(dsb[:, :W], qs, TN) * scale
            dv_prev = _dot(pb[:, :W], dos, TN)
            if r == 0:
                @pl.when(i > 0)
                def _():
                    prev_row = pl.multiple_of(i * SWA_TB - W, W)
                    dkad_ref[kvh, pl.ds(prev_row, W), :] += dk_prev
                    dvad_ref[kvh, pl.ds(prev_row, W), :] += dv_prev
            else:
                prev_row = pl.multiple_of(i * SWA_TB + (r - 1) * W, W)
                dkad_ref[kvh, pl.ds(prev_row, W), :] += dk_prev
                dvad_ref[kvh, pl.ds(prev_row, W), :] += dv_prev

    qspec, own, prev, stat, bspec = _swa_specs(T)
    full = pl.BlockSpec((SWA_KV_HEADS, T, LANES), lambda h, i: (0, 0, 0))
    return pl.pallas_call(
        body, name="swa_bwd", grid=(SWA_KV_HEADS, nb),
        in_specs=[pl.BlockSpec(memory_space=pltpu.SMEM), qspec, own, prev, own, prev, bspec, qspec, stat, stat],
        out_specs=[qspec, full, full, pl.BlockSpec((SWA_GROUP, W, 2 * W), lambda h, i: (h, 0, 0)),
                   pl.BlockSpec((None, 8, LANES), lambda h, i: (h, 0, 0))],
        out_shape=[jax.ShapeDtypeStruct((T, 512), F32), jax.ShapeDtypeStruct((SWA_KV_HEADS, T, LANES), F32),
                   jax.ShapeDtypeStruct((SWA_KV_HEADS, T, LANES), F32), jax.ShapeDtypeStruct((SWA_HEADS, W, 2 * W), F32),
                   jax.ShapeDtypeStruct((SWA_KV_HEADS, 8, LANES), F32)],
        compiler_params=_cparams("arbitrary", "arbitrary", vmem=VMEM_MID),
    )(sinks, q, kad, kad, vad, vad, bias.reshape(SWA_KV_HEADS, SWA_GROUP * W, 2 * W), do, lse, delta)


def _mem_fwd(q, mk, mv, T, tq):
    scale = MEM_HEAD_DIM ** -0.5

    def body(q_ref, k_ref, v_ref, o_ref, lse_ref):
        s = _dot(q_ref[...], k_ref[...], NT) * scale
        m = jnp.max(s, axis=1, keepdims=True)
        p = jnp.exp(s - m)
        l = jnp.sum(p, axis=1, keepdims=True)
        o_ref[...] = _dot((p / l).astype(BF16), v_ref[...]).astype(o_ref.dtype)
        lse_ref[...] = jnp.broadcast_to(m + jnp.log(l), (tq, LANES))

    qspec = pl.BlockSpec((tq, LANES), lambda h, i: (i, h))
    kspec = pl.BlockSpec((N_MEM, LANES), lambda h, i: (0, h))
    return pl.pallas_call(
        body, name="mem_fwd", grid=(MEM_HEADS, T // tq),
        in_specs=[qspec, kspec, kspec],
        out_specs=[qspec, pl.BlockSpec((None, tq, LANES), lambda h, i: (h, i, 0))],
        out_shape=[jax.ShapeDtypeStruct((T, 512), BF16), jax.ShapeDtypeStruct((MEM_HEADS, T, LANES), F32)],
        compiler_params=_cparams("parallel", "parallel"),
    )(q, mk, mv)


def _mem_bwd(q, mk, mv, do, lse, delta, T, tq):
    scale = MEM_HEAD_DIM ** -0.5
    rep = N_MEM // LANES

    def body(q_ref, k_ref, v_ref, do_ref, lse_ref, dl_ref, dq_ref, dk_ref, dv_ref):
        i = pl.program_id(1)

        @pl.when(i == 0)
        def _():
            dk_ref[...] = jnp.zeros_like(dk_ref)
            dv_ref[...] = jnp.zeros_like(dv_ref)

        qv, dov = q_ref[...], do_ref[...]
        s = _dot(qv, k_ref[...], NT) * scale
        p = jnp.exp(s - jnp.tile(lse_ref[...], (1, rep)))
        dp = _dot(dov, v_ref[...], NT)
        ds = p * (dp - jnp.tile(dl_ref[...], (1, rep)))
        dsb = ds.astype(BF16)
        dq_ref[...] = _dot(dsb, k_ref[...]) * scale
        dk_ref[...] += _dot(dsb, qv, TN) * scale
        dv_ref[...] += _dot(p.astype(BF16), dov, TN)

    qspec = pl.BlockSpec((tq, LANES), lambda h, i: (i, h))
    kspec = pl.BlockSpec((N_MEM, LANES), lambda h, i: (0, h))
    stat = pl.BlockSpec((None, tq, LANES), lambda h, i: (h, i, 0))
    return pl.pallas_call(
        body, name="mem_bwd", grid=(MEM_HEADS, T // tq),
        in_specs=[qspec, kspec, kspec, qspec, stat, stat],
        out_specs=[qspec, kspec, kspec],
        out_shape=[jax.ShapeDtypeStruct((T, 512), F32), jax.ShapeDtypeStruct((N_MEM, 512), F32),
                   jax.ShapeDtypeStruct((N_MEM, 512), F32)],
        compiler_params=_cparams("arbitrary", "arbitrary"),
    )(q, mk, mv, do, lse, delta)


def _mem_prep_fwd(mem, g_mem, w_kv, kn_gain, gm128):
    def body(mem_ref, g_ref, w_ref, kn_ref, gm_ref, memn_o, kv_o, mk_o, mv_o):
        xhat, _ = _rms_rows(mem_ref[...], None)
        memn = (xhat * g_ref[...]).astype(BF16)
        memn_o[...] = memn
        kv = _dot(memn, w_ref[...])
        kv_o[...] = kv
        gm = gm_ref[...]
        for c in range(4):
            sl = slice(c * LANES, (c + 1) * LANES)
            y, _ = _head_norm(kv[:, sl], gm, kn_ref[...])
            mk_o[:, sl] = y.astype(BF16)
        mv_o[...] = kv[:, 512:].astype(BF16)

    vm = pl.BlockSpec(memory_space=pltpu.VMEM)
    return pl.pallas_call(
        body, name="mem_prep_fwd", in_specs=[vm] * 5, out_specs=[vm] * 4,
        out_shape=[jax.ShapeDtypeStruct((N_MEM, D_MODEL), BF16), jax.ShapeDtypeStruct((N_MEM, D_MODEL), F32),
                   jax.ShapeDtypeStruct((N_MEM, 512), BF16), jax.ShapeDtypeStruct((N_MEM, 512), BF16)],
        compiler_params=pltpu.CompilerParams(vmem_limit_bytes=VMEM_MID),
    )(mem, g_mem, w_kv, kn_gain, gm128)


def _mem_prep_bwd(mem, g_mem, memn, kv, w_kv, kn_gain, gm128, dmk, dmv):
    def body(mem_ref, g_ref, memn_ref, kv_ref, w_ref, kn_ref, gm_ref, dmk_ref, dmv_ref, dw_o, dg_o, dkn_o, dkv_s):
        gm = gm_ref[...]
        dkn = jnp.zeros((1, LANES), F32)
        for c in range(4):
            sl = slice(c * LANES, (c + 1) * LANES)
            dx, dg = _head_norm_bwd(dmk_ref[:, sl], kv_ref[:, sl], gm, kn_ref[...])
            dkv_s[:, sl] = dx.astype(BF16)
            dkn = dkn + dg
        dkn_o[...] = dkn
        dkv_s[:, 512:] = dmv_ref[...].astype(BF16)
        dkv = dkv_s[...]
        dw_o[...] = _dot(memn_ref[...], dkv, TN)
        dmemn = _dot(dkv, w_ref[...], NT)
        xhat, _ = _rms_rows(mem_ref[...], None)
        dg_o[...] = jnp.sum(dmemn * xhat, axis=0, keepdims=True)

    vm = pl.BlockSpec(memory_space=pltpu.VMEM)
    return pl.pallas_call(
        body, name="mem_prep_bwd", in_specs=[vm] * 9, out_specs=[vm] * 3,
        out_shape=[jax.ShapeDtypeStruct((D_MODEL, D_MODEL), F32), jax.ShapeDtypeStruct((1, D_MODEL), F32),
                   jax.ShapeDtypeStruct((1, LANES), F32)],
        scratch_shapes=[pltpu.VMEM((N_MEM, D_MODEL), BF16)],
        compiler_params=pltpu.CompilerParams(vmem_limit_bytes=VMEM_MID),
    )(mem, g_mem, memn, kv, w_kv, kn_gain, gm128, dmk, dmv)


SLOT_O = D_MODEL // N_SHARD


def _merge_fwd(proj, b_gate, o3, w3, T, tb):
    def body(gl_ref, bg_ref, oa_ref, of_ref, om_ref, wa_ref, wf_ref, wm_ref, out_ref):
        o_refs = (oa_ref, of_ref, om_ref)
        w_refs = (wa_ref, wf_ref, wm_ref)
        for n in range(N_SHARD):
            acc = jnp.zeros((tb, SLOT_O), F32)
            for b in range(3):
                c0 = b * D_MODEL + n * SLOT_O
                g = jax.nn.sigmoid(gl_ref[:, c0:c0 + SLOT_O] + bg_ref[:, c0:c0 + SLOT_O])
                acc = acc + g * _dot(o_refs[b][...], w_refs[b][n])
            out_ref[:, n * SLOT_O:(n + 1) * SLOT_O] = acc.astype(out_ref.dtype)

    rows = pl.BlockSpec((tb, 512), lambda i: (i, 0))
    wspec = pl.BlockSpec((N_SHARD, 512, SLOT_O), lambda i: (0, 0, 0))
    return pl.pallas_call(
        body, name="merge_fwd", grid=(T // tb,),
        in_specs=[pl.BlockSpec((tb, GATE_W), lambda i: (i, 1)), pl.BlockSpec((1, GATE_W), lambda i: (0, 0)),
                  rows, rows, rows, wspec, wspec, wspec],
        out_specs=pl.BlockSpec((tb, D_MODEL), lambda i: (i, 0)),
        out_shape=jax.ShapeDtypeStruct((T, D_MODEL), BF16),
        compiler_params=_cparams("parallel", vmem=VMEM_BIG),
    )(proj, b_gate, *o3, *w3)


def _merge_bwd(proj, b_gate, o3, w3, dmerged, T, tb):
    heads = (SWA_HEADS, FOX_HEADS, MEM_HEADS)

    def body(gl_ref, bg_ref, oa_ref, of_ref, om_ref, wa_ref, wf_ref, wm_ref, dm_ref,
             dgl_o, doa_o, dof_o, dom_o, dla_o, dlf_o, dlm_o, dwa_o, dwf_o, dwm_o, dbg_o):
        i = pl.program_id(0)
        o_refs = (oa_ref, of_ref, om_ref)
        w_refs = (wa_ref, wf_ref, wm_ref)
        do_refs = (doa_o, dof_o, dom_o)
        dl_refs = (dla_o, dlf_o, dlm_o)
        dw_refs = (dwa_o, dwf_o, dwm_o)

        @pl.when(i == 0)
        def _():
            for r in dw_refs:
                r[...] = jnp.zeros_like(r)
            dbg_o[...] = jnp.zeros_like(dbg_o)

        lane = _lane((tb, LANES))
        for b in range(3):
            ob = o_refs[b][...]
            do = jnp.zeros((tb, 512), F32)
            for n in range(N_SHARD):
                c0 = b * D_MODEL + n * SLOT_O
                g = jax.nn.sigmoid(gl_ref[:, c0:c0 + SLOT_O] + bg_ref[:, c0:c0 + SLOT_O])
                dm = dm_ref[:, n * SLOT_O:(n + 1) * SLOT_O]
                y = _dot(ob, w_refs[b][n])
                dgl = dm * y * g * (1.0 - g)
                dgl_o[:, c0:c0 + SLOT_O] = dgl.astype(dgl_o.dtype)
                dbg_o[:, c0:c0 + SLOT_O] += jnp.sum(dgl, axis=0, keepdims=True)
                dy = (dm * g).astype(BF16)
                do = do + _dot(dy, w_refs[b][n], NT)
                dw_refs[b][n] += _dot(ob, dy, TN)
            do_refs[b][...] = do.astype(BF16)
            prod = do * ob.astype(F32)
            for c in range(4):
                blk = prod[:, c * LANES:(c + 1) * LANES]
                if heads[b] == 8:
                    lo = jnp.sum(jnp.where(lane < 64, blk, 0.0), axis=1, keepdims=True)
                    hi = jnp.sum(jnp.where(lane >= 64, blk, 0.0), axis=1, keepdims=True)
                    if b == 1:
                        aug = jnp.zeros((tb, LANES), F32)
                        for sub, dl in enumerate((lo, hi)):
                            for e, piece in enumerate(_split3(-dl)):
                                aug = jnp.where(lane == AUG_STRIDE * sub + AUG_C + e, piece.astype(F32), aug)
                        dl_refs[b][:, c * LANES:(c + 1) * LANES] = aug.astype(BF16)
                    else:
                        dl_refs[b][2 * c] = jnp.broadcast_to(lo, (tb, LANES))
                        dl_refs[b][2 * c + 1] = jnp.broadcast_to(hi, (tb, LANES))
                else:
                    dl_refs[b][c] = jnp.broadcast_to(jnp.sum(blk, axis=1, keepdims=True), (tb, LANES))

    rows = pl.BlockSpec((tb, 512), lambda i: (i, 0))
    wspec = pl.BlockSpec((N_SHARD, 512, SLOT_O), lambda i: (0, 0, 0))
    stat = lambda h: pl.BlockSpec((h, tb, LANES), lambda i: (0, i, 0))
    return pl.pallas_call(
        body, name="merge_bwd", grid=(T // tb,),
        in_specs=[pl.BlockSpec((tb, GATE_W), lambda i: (i, 1)), pl.BlockSpec((1, GATE_W), lambda i: (0, 0)),
                  rows, rows, rows, wspec, wspec, wspec, pl.BlockSpec((tb, D_MODEL), lambda i: (i, 0))],
        out_specs=[pl.BlockSpec((tb, GATE_W), lambda i: (i, 0)), rows, rows, rows,
                   stat(8), rows, stat(4), wspec, wspec, wspec, pl.BlockSpec((1, GATE_W), lambda i: (0, 0))],
        out_shape=[jax.ShapeDtypeStruct((T, GATE_W), BF16)] + [jax.ShapeDtypeStruct((T, 512), BF16)] * 3
        + [jax.ShapeDtypeStruct((8, T, LANES), F32), jax.ShapeDtypeStruct((T, 512), BF16),
           jax.ShapeDtypeStruct((4, T, LANES), F32)]
        + [jax.ShapeDtypeStruct((N_SHARD, 512, SLOT_O), F32)] * 3 + [jax.ShapeDtypeStruct((1, GATE_W), F32)],
        compiler_params=_cparams("arbitrary", vmem=VMEM_BIG),
    )(proj, b_gate, *o3, *w3, dmerged)


def _local_step(x, h, mem, tgt, small, g_in, w_kv, w_o3, w_out, w_up, w_down, reducer):
    T = x.shape[0]
    tm = min(512, T)
    tile2 = lambda v: jnp.tile(v.reshape(1, -1), (1, LANES // v.size))
    gains = jnp.concatenate([tile2(small["qn_swa"]), tile2(small["kn_swa"]), tile2(small["qn_fox"]),
                             tile2(small["kn_fox"]), tile2(small["qn_mem"]), jnp.zeros((3, LANES), F32)], axis=0)
    kn_mem = small["kn_mem"].reshape(1, LANES)
    bfor = jnp.pad(small["b_forget"].reshape(1, -1), ((0, 0), (0, LANES - FOX_HEADS)))
    gm64 = _group_mean_matrix(64)
    gm128 = _group_mean_matrix(128)
    tb_prep = min(256, T)
    ones = jnp.ones((tb_prep, tb_prep), F32)
    tril = jnp.tril(ones).astype(BF16)
    triu = jnp.triu(ones).astype(BF16)
    bucket = _t5_bucket_matrix()
    g_mix, g_mlp, g_mem = small["g_mix"], small["g_mlp"], small["g_mem"]
    b_gate = small["b_gate"]
    sinks = small["sink_swa"].reshape(-1)

    tl = min(1024, T)
    sq = pl.BlockSpec((tl, D_MODEL), lambda i, j, k: (i, j))
    wc = _w_in_to_segments(g_in)
    (proj,) = _matmul(
        "mm_proj", h, wc, dims=NN, grid=(T // tl, PROJ_W // D_MODEL, 1),
        a_spec=pl.BlockSpec((tl, D_MODEL), lambda i, j, k: (i, 0)),
        b_spec=pl.BlockSpec((D_MODEL, D_MODEL), lambda i, j, k: (0, j)),
        acc_shape=(tl, D_MODEL),
        outs=[(jax.ShapeDtypeStruct((T, PROJ_W), F32), sq)],
        epilogue=_epi_store)
    qa, qf, kf, vf, qm, kad, vad, qf_aug, kf_aug = _prep_fwd(proj, gains, bfor, tril, gm64, gm128, T, tb_prep)
    bias = _swa_bias(small["rel_bias"], bucket)
    o_swa, lse_swa = _swa_fwd(sinks, qa, kad, vad, bias, T)
    o_fox, qf_aug_bwd = _fox_fwd(qf, qf_aug, kf, kf_aug, vf, T, min(FOX_TQ, T), min(FOX_TK, T))
    memn, kv, mk, mv = _mem_prep_fwd(mem, g_mem, w_kv, kn_mem, gm128)
    o_mem, lse_mem = _mem_fwd(qm, mk, mv, T, tm)
    o3 = (o_swa, o_fox, o_mem)
    merged = _merge_fwd(proj, b_gate, o3, w_o3, T, min(512, T))

    def epi_residual(acc, extra_refs, out_refs, ij):
        out_refs[0][...] = extra_refs[0][...] + acc

    row_full = pl.BlockSpec((tm, D_MODEL), lambda i, j, k: (i, 0))
    row_big = pl.BlockSpec((tl, D_MODEL), lambda i, j, k: (i, 0))
    whole = pl.BlockSpec((D_MODEL, D_MODEL), lambda i, j, k: (0, 0))
    (x2,) = _matmul(
        "mm_out", merged, w_out, dims=NN, grid=(T // tl, 1, 1),
        a_spec=row_big, b_spec=whole,
        acc_shape=(tl, D_MODEL), extra=[(x, row_big)],
        outs=[(jax.ShapeDtypeStruct((T, D_MODEL), F32), row_big)], epilogue=epi_residual)
    hm = _rmsnorm("rms_mlp", x2, g_mlp, tm)

    def epi_relu2(acc, extra_refs, out_refs, ij):
        out_refs[0][...] = acc.astype(BF16)
        r = jnp.maximum(acc, 0.0)
        out_refs[1][...] = (r * r).astype(BF16)

    up, u = _matmul(
        "mm_up", hm, w_up, dims=NN, grid=(T // tl, N_SHARD, 1),
        a_spec=row_big, b_spec=pl.BlockSpec((None, D_MODEL, D_MODEL), lambda i, j, k: (j, 0, 0)),
        acc_shape=(tl, D_MODEL),
        outs=[(jax.ShapeDtypeStruct((T, D_FF), BF16), sq), (jax.ShapeDtypeStruct((T, D_FF), BF16), sq)],
        epilogue=epi_relu2)

    def epi_loss(acc, extra_refs, out_refs, ij):
        y = extra_refs[0][...] + acc
        err = y - extra_refs[1][...]
        dyv = err * (1.0 / D_MODEL)
        out_refs[0][...] = dyv
        out_refs[2][...] = dyv.astype(BF16)
        sq = jnp.sum(jnp.sum(err * err, axis=1, keepdims=True), axis=0, keepdims=True)

        @pl.when(ij[0] == 0)
        def _():
            out_refs[1][...] = jnp.zeros_like(out_refs[1])

        out_refs[1][...] += jnp.broadcast_to(sq, out_refs[1].shape)

    kblk = pl.BlockSpec((tl, D_MODEL), lambda i, j, k: (i, k))
    dy, loss_acc, dy_bf = _matmul(
        "mm_down", u, w_down, dims=NN, grid=(T // tl, 1, N_SHARD),
        a_spec=kblk, b_spec=pl.BlockSpec((D_MODEL, D_MODEL), lambda i, j, k: (k, 0)),
        acc_shape=(tl, D_MODEL), extra=[(x2, row_big), (tgt, row_big)],
        outs=[(jax.ShapeDtypeStruct((T, D_MODEL), F32), row_big),
              (jax.ShapeDtypeStruct((8, LANES), F32), pl.BlockSpec((8, LANES), lambda i, j, k: (0, 0))),
              (jax.ShapeDtypeStruct((T, D_MODEL), BF16), row_big)],
        epilogue=epi_loss)
    loss = loss_acc[0, 0] * (0.5 / D_MODEL)

    def epi_dup(acc, extra_refs, out_refs, ij):
        out_refs[0][...] = (acc * (2.0 * jnp.maximum(extra_refs[0][...].astype(F32), 0.0))).astype(BF16)

    (dup,) = _matmul(
        "mm_dup", dy_bf, w_down, dims=NT, grid=(T // tl, N_SHARD, 1),
        a_spec=row_big, b_spec=pl.BlockSpec((D_MODEL, D_MODEL), lambda i, j, k: (j, 0)),
        acc_shape=(tl, D_MODEL), extra=[(up, sq)],
        outs=[(jax.ShapeDtypeStruct((T, D_FF), BF16), sq)], epilogue=epi_dup)

    nkt = T // tl
    t_rows = pl.BlockSpec((tl, D_MODEL), lambda i, j, k: (k, i))
    t_cols = pl.BlockSpec((tl, D_MODEL), lambda i, j, k: (k, j))
    (d_w_down,) = _matmul(
        "mm_dw_down", u, dy_bf, dims=TN, grid=(N_SHARD, 1, nkt),
        a_spec=t_rows, b_spec=t_cols, acc_shape=(D_MODEL, D_MODEL),
        outs=[(jax.ShapeDtypeStruct((D_FF, D_MODEL), F32), pl.BlockSpec((D_MODEL, D_MODEL), lambda i, j, k: (i, 0)))],
        epilogue=_epi_store)
    (d_w_up,) = _matmul(
        "mm_dw_up", hm, dup, dims=TN, grid=(1, N_SHARD, nkt),
        a_spec=t_rows, b_spec=t_cols, acc_shape=(D_MODEL, D_MODEL),
        outs=[(jax.ShapeDtypeStruct((N_SHARD, D_MODEL, D_MODEL), F32),
               pl.BlockSpec((None, D_MODEL, D_MODEL), lambda i, j, k: (j, 0, 0)))],
        epilogue=_epi_store)

    def epi_rms_bwd(acc, extra_refs, out_refs, ij):
        dx, dg = _rmsnorm_bwd_rows(acc, extra_refs[0][...], extra_refs[1][...])
        out_refs[0][...] = dx + extra_refs[2][...]

        @pl.when(ij[0] == 0)
        def _():
            out_refs[1][...] = jnp.zeros_like(out_refs[1])

        out_refs[1][...] += dg

    gain_spec = pl.BlockSpec((1, D_MODEL), lambda i, j, k: (0, 0))
    dx2, d_g_mlp = _matmul(
        "mm_dhm", dup, w_up, dims=NT, grid=(T // tl, 1, N_SHARD),
        a_spec=kblk, b_spec=pl.BlockSpec((None, D_MODEL, D_MODEL), lambda i, j, k: (k, 0, 0)),
        acc_shape=(tl, D_MODEL), extra=[(x2, row_big), (g_mlp, gain_spec), (dy, row_big)],
        outs=[(jax.ShapeDtypeStruct((T, D_MODEL), F32), row_big), (jax.ShapeDtypeStruct((1, D_MODEL), F32), gain_spec)],
        epilogue=epi_rms_bwd)

    (dmerged,) = _matmul(
        "mm_dmerged", dx2, w_out, dims=NT, grid=(T // tl, 1, 1),
        a_spec=row_big, b_spec=whole,
        acc_shape=(tl, D_MODEL), outs=[(jax.ShapeDtypeStruct((T, D_MODEL), F32), row_big)], epilogue=_epi_store)
    (d_w_out,) = _matmul(
        "mm_dw_out", merged, dx2, dims=TN, grid=(1, 1, nkt),
        a_spec=t_rows, b_spec=t_cols, acc_shape=(D_MODEL, D_MODEL),
        outs=[(jax.ShapeDtypeStruct((D_MODEL, D_MODEL), F32), whole)],
        epilogue=_epi_store)
    (dgl, do_swa, do_fox, do_mem, dl_swa, do_fox_aug, dl_mem, d_wo_swa, d_wo_fox, d_wo_mem, d_b_gate) = _merge_bwd(
        proj, b_gate, o3, w_o3, dmerged, T, min(512, T))

    dqm, dmk, dmv = _mem_bwd(qm, mk, mv, do_mem, lse_mem, dl_mem, T, tm)
    d_w_kv, d_g_mem, d_kn_mem = _mem_prep_bwd(mem, g_mem, memn, kv, w_kv, kn_mem, gm128, dmk, dmv)
    do_swa = reducer.early_start({"w_mlp_down": d_w_down, "w_mlp_up": d_w_up, "w_out": d_w_out, "w_mem_kv": d_w_kv,
                                  "w_o_swa": d_wo_swa, "w_o_fox": d_wo_fox, "w_o_mem": d_wo_mem}, do_swa)
    dqa, dkad, dvad, dbias, dsk = _swa_bwd(sinks, qa, kad, vad, bias, do_swa, lse_swa, dl_swa, T)
    dqa, do_fox = reducer.early_send((dqa, do_fox))
    dqf, dqf_aug, dkf, dkf_aug, dvf = _fox_bwd(qf, qf_aug_bwd, kf, kf_aug, vf, do_fox, do_fox_aug, T,
                                               min(FOX_BWD_TQ, T), min(FOX_BWD_TK, T))
    dvf = reducer.early_finish(dvf)
    d_rel = _swa_bias_bwd(dbias, bucket)
    dlo, gacc = _prep_bwd(proj, dqa, dkad, dvad, dqf, dkf, dvf, dqm, dqf_aug, dkf_aug, gains, bfor, triu, gm64, gm128,
                          T, tb_prep)

    def dwc_half(name, dpart):
        (res,) = _matmul(
            name, h, dpart, dims=TN, grid=(1, LO_W // D_MODEL, nkt),
            a_spec=t_rows, b_spec=t_cols, acc_shape=(D_MODEL, D_MODEL),
            outs=[(jax.ShapeDtypeStruct((D_MODEL, LO_W), F32), pl.BlockSpec((D_MODEL, D_MODEL), lambda i, j, k: (0, j)))],
            epilogue=_epi_store)
        return res

    d_wc_lo = dwc_half("mm_dwc_lo", dlo)
    d_wc_gl = dwc_half("mm_dwc_gl", dgl)
    dlo = reducer.late_start({"wc_lo": d_wc_lo, "wc_gl": d_wc_gl}, dlo)
    (dh_lo,) = _matmul(
        "mm_dh_lo", dlo, wc, dims=NT, grid=(T // tl, 1, LO_W // D_MODEL),
        a_spec=kblk, b_spec=pl.BlockSpec((D_MODEL, D_MODEL), lambda i, j, k: (0, k)),
        acc_shape=(tl, D_MODEL), outs=[(jax.ShapeDtypeStruct((T, D_MODEL), F32), row_big)], epilogue=_epi_store)
    dh_lo = reducer.late_send(dh_lo)

    def epi_dx(acc, extra_refs, out_refs, ij):
        dhh = acc + extra_refs[3][...]
        dx, dg = _rmsnorm_bwd_rows(dhh, extra_refs[0][...], extra_refs[1][...])
        out_refs[0][...] = dx + extra_refs[2][...]

        @pl.when(ij[0] == 0)
        def _():
            out_refs[1][...] = jnp.zeros_like(out_refs[1])

        out_refs[1][...] += dg

    grad_x, d_g_mix = _matmul(
        "mm_dh_gl", dgl, wc, dims=NT, grid=(T // tl, 1, GATE_W // D_MODEL),
        a_spec=kblk, b_spec=pl.BlockSpec((D_MODEL, D_MODEL), lambda i, j, k: (0, k + LO_W // D_MODEL)),
        acc_shape=(tl, D_MODEL), extra=[(x, row_big), (g_mix, gain_spec), (dx2, row_big), (dh_lo, row_big)],
        outs=[(jax.ShapeDtypeStruct((T, D_MODEL), F32), row_big), (jax.ShapeDtypeStruct((1, D_MODEL), F32), gain_spec)],
        epilogue=epi_dx, vmem=VMEM_MAX)

    fold64 = lambda row: (row[:64] + row[64:]).reshape(1, 64)
    grads = {
        "g_mix": d_g_mix, "b_gate": d_b_gate, "b_forget": gacc[5, :FOX_HEADS].reshape(1, FOX_HEADS),
        "qn_swa": fold64(gacc[0]), "kn_swa": fold64(gacc[1]),
        "sink_swa": -dsk[:, :SWA_GROUP, 0].reshape(1, SWA_HEADS), "rel_bias": d_rel[:, :SWA_HEADS],
        "qn_fox": fold64(gacc[2]), "kn_fox": fold64(gacc[3]),
        "g_mem": d_g_mem, "qn_mem": gacc[4].reshape(1, LANES), "kn_mem": d_kn_mem, "g_mlp": d_g_mlp,
    }
    return loss, grad_x, grads


MESH = pl.DeviceIdType.MESH
ANY = pl.BlockSpec(memory_space=pl.ANY)


def _place():
    x, y, c = lax.axis_index("x"), lax.axis_index("y"), lax.axis_index("c")
    chips = [(1 - x, y), (x, 1 - y), (1 - x, 1 - y)]
    return x, y, c, chips


def _handshake(peers):
    barrier = pltpu.get_barrier_semaphore()
    for peer in peers:
        pl.semaphore_signal(barrier, inc=1, device_id=peer, device_id_type=MESH)
    pl.semaphore_wait(barrier, len(peers))


def _all_gather_shards_async(name, collective_id, slots):
    n = len(slots)
    bufs = [jax.new_ref(s, memory_space=pltpu.MemorySpace.HBM) for s in slots]

    def body(ici_send, ici_recv, d2d_send, d2d_recv):
        x, y, c, chips = _place()
        sibling = (x, y, 1 - c)
        me = 2 * x + y
        _handshake([(px, py, c) for px, py in chips] + [sibling])

        def half(a, who):
            hr = slots[a].shape[1] // 2
            return pl.ds(pl.multiple_of(who * hr, hr), hr)

        def ici(a, j, slot, to):
            return pltpu.make_async_remote_copy(
                src_ref=bufs[a].at[me, half(a, c)], dst_ref=bufs[a].at[slot, half(a, c)],
                send_sem=ici_send.at[3 * a + j], recv_sem=ici_recv.at[3 * a + j], device_id=to, device_id_type=MESH)

        def d2d(a, j, slot, which):
            part = bufs[a].at[slot, half(a, which)]
            return pltpu.make_async_remote_copy(
                src_ref=part, dst_ref=part, send_sem=d2d_send.at[3 * a + j], recv_sem=d2d_recv.at[3 * a + j],
                device_id=sibling, device_id_type=MESH)

        sends = [ici(a, j, me, (*chip, c)) for a in range(n) for j, chip in enumerate(chips)]
        for cp in sends:
            cp.start()
        passed = []
        for a in range(n):
            for j, (px, py) in enumerate(chips):
                ici(a, j, 2 * px + py, (px, py, c)).wait_recv()
                cp = d2d(a, j, 2 * px + py, c)
                cp.start()
                passed.append(cp)
        for a in range(n):
            for j, (px, py) in enumerate(chips):
                d2d(a, j, 2 * px + py, 1 - c).wait_recv()
        for cp in sends + passed:
            cp.wait_send()

    pl.kernel(
        body, mesh=plsc.ScalarSubcoreMesh(axis_name="seq", num_cores=1), name=name,
        scratch_types=[pltpu.SemaphoreType.DMA((3 * n,))] * 4,
        compiler_params=pltpu.CompilerParams(collective_id=collective_id),
    )()
    return [b[...] for b in bufs]


def _sequencer_call(name, collective_id, n_sems, body):
    pl.kernel(
        body, mesh=plsc.ScalarSubcoreMesh(axis_name="seq", num_cores=1), name=name,
        scratch_types=[pltpu.SemaphoreType.DMA((n_sems,))] * 2,
        compiler_params=pltpu.CompilerParams(collective_id=collective_id),
    )()


def _hbm_ref(value):
    return jax.new_ref(value, memory_space=pltpu.MemorySpace.HBM)


def _pair_exchange(name, collective_id, gs):
    n = len(gs)
    src = [_hbm_ref(g) for g in gs]
    stage = [jax.empty_ref(jax.ShapeDtypeStruct((N_SHARD, g.shape[1] // 2, g.shape[2]), g.dtype),
                           memory_space=pltpu.MemorySpace.HBM) for g in gs]

    def body(send_sem, recv_sem):
        x, y, c, _ = _place()
        sibling = (x, y, 1 - c)
        _handshake([sibling])
        copies = []
        for a in range(n):
            hr = gs[a].shape[1] // 2
            theirs = pl.ds(pl.multiple_of((1 - c) * hr, hr), hr)
            copies.append(pltpu.make_async_remote_copy(
                src_ref=src[a].at[:, theirs, :], dst_ref=stage[a], send_sem=send_sem.at[a], recv_sem=recv_sem.at[a],
                device_id=sibling, device_id_type=MESH))
        for cp in copies:
            cp.start()
        for cp in copies:
            cp.wait()

    _sequencer_call(name, collective_id, n, body)
    return [s[...] for s in stage]


def _chip_exchange(name, collective_id, sums):
    n = len(sums)
    src = [_hbm_ref(s) for s in sums]
    got = [jax.empty_ref(jax.ShapeDtypeStruct((3,) + s.shape[1:], s.dtype), memory_space=pltpu.MemorySpace.HBM)
           for s in sums]

    def body(send_sem, recv_sem):
        x, y, c, chips = _place()
        _handshake([(px, py, c) for px, py in chips])
        copies = []
        for a in range(n):
            for j, (px, py) in enumerate(chips):
                copies.append(pltpu.make_async_remote_copy(
                    src_ref=src[a].at[2 * px + py], dst_ref=got[a].at[j],
                    send_sem=send_sem.at[3 * a + j], recv_sem=recv_sem.at[3 * a + j],
                    device_id=(px, py, c), device_id_type=MESH))
        for cp in copies:
            cp.start()
        for cp in copies:
            cp.wait()

    _sequencer_call(name, collective_id, 3 * n, body)
    return [g[...] for g in got]


def _pair_gather(name, collective_id, fulls):
    n = len(fulls)
    full = [_hbm_ref(f) for f in fulls]

    def body(send_sem, recv_sem):
        x, y, c, _ = _place()
        sibling = (x, y, 1 - c)
        _handshake([sibling])
        copies = []
        for a in range(n):
            hr = fulls[a].shape[0] // 2
            mine = full[a].at[pl.ds(pl.multiple_of(c * hr, hr), hr)]
            copies.append(pltpu.make_async_remote_copy(
                src_ref=mine, dst_ref=mine, send_sem=send_sem.at[a], recv_sem=recv_sem.at[a],
                device_id=sibling, device_id_type=MESH))
        for cp in copies:
            cp.start()
        for cp in copies:
            cp.wait()

    _sequencer_call(name, collective_id, n, body)
    return [f[...] for f in full]


ELEMENTWISE_BLOCK_ELEMS = 256 * 1024


def _row_block(rows, cols):
    rb = 8
    while rb * 2 * cols <= ELEMENTWISE_BLOCK_ELEMS and rb * 2 <= rows:
        rb *= 2
    return rb


def _pair_sum(name, place, g, stage):
    _, R, C = g.shape
    hr = R // 2
    rb = _row_block(hr, C)
    nb = hr // rb

    def body(place_ref, g_ref, st_ref, sum_bf, own_f32):
        s = pl.program_id(1)
        tot = g_ref[...] + st_ref[...]
        sum_bf[...] = tot.astype(BF16)

        @pl.when(s == place_ref[0])
        def _():
            own_f32[...] = tot

    return pl.pallas_call(
        body, name=name,
        grid_spec=pltpu.PrefetchScalarGridSpec(
            num_scalar_prefetch=1, grid=(nb, N_SHARD),
            in_specs=[pl.BlockSpec((None, rb, C), lambda i, s, pr: (s, pr[1] * nb + i, 0)),
                      pl.BlockSpec((None, rb, C), lambda i, s, pr: (s, i, 0))],
            out_specs=[pl.BlockSpec((None, rb, C), lambda i, s, pr: (s, i, 0)),
                       pl.BlockSpec((rb, C), lambda i, s, pr: (i, 0))]),
        out_shape=[jax.ShapeDtypeStruct((N_SHARD, hr, C), BF16), jax.ShapeDtypeStruct((hr, C), F32)],
        compiler_params=_cparams("arbitrary", "arbitrary"),
    )(place, g, stage)


def _final_sum(name, place, own, got):
    hr, C = own.shape
    rb = _row_block(hr, C)
    nb = hr // rb

    def body(place_ref, own_ref, got_ref, o_ref):
        o_ref[...] = ((own_ref[...] + got_ref[0].astype(F32)) + got_ref[1].astype(F32)) + got_ref[2].astype(F32)

    return pl.pallas_call(
        body, name=name,
        grid_spec=pltpu.PrefetchScalarGridSpec(
            num_scalar_prefetch=1, grid=(nb,),
            in_specs=[pl.BlockSpec((rb, C), lambda i, pr: (i, 0)), pl.BlockSpec((3, rb, C), lambda i, pr: (0, i, 0))],
            out_specs=pl.BlockSpec((rb, C), lambda i, pr: (pr[1] * nb + i, 0))),
        out_shape=jax.ShapeDtypeStruct((2 * hr, C), F32),
        compiler_params=_cparams("arbitrary"),
    )(place, own, got)


def _adamw_math(w, g, m, v):
    m = ADAM_B1 * m + (1.0 - ADAM_B1) * g
    v = ADAM_B2 * v + (1.0 - ADAM_B2) * (g * g)
    m_hat = m / (1.0 - ADAM_B1 ** ADAM_STEP)
    v_hat = v / (1.0 - ADAM_B2 ** ADAM_STEP)
    delta = -ADAM_LR * (m_hat / (jnp.sqrt(v_hat) + ADAM_EPS) + ADAM_WD * w)
    return delta, m, v


def _adamw(name, w, g, m, v):
    R, Cw = w.shape
    Cg = g.shape[1]
    rb = _row_block(R, Cg)

    def body(w_ref, g_ref, m_ref, v_ref, g_o, d_o, m_o, v_o):
        gv = g_ref[...]
        delta, mn, vn = _adamw_math(w_ref[...], gv, m_ref[...], v_ref[...])
        g_o[...] = gv
        d_o[...] = delta
        m_o[...] = mn
        v_o[...] = vn

    blk = pl.BlockSpec((rb, Cg), lambda i: (i, 0))
    return pl.pallas_call(
        body, name=name, grid=(R // rb,),
        in_specs=[blk] * 4, out_specs=[blk] * 4,
        out_shape=[jax.ShapeDtypeStruct((R, Cw), F32)] * 4,
        compiler_params=_cparams("parallel"),
    )(w, g, m, v)


N_DEV = 8
SMALL_ROWS = 64


def _small_allreduce_adamw(g, w, m, v):
    def body(g_ref, w_ref, m_ref, v_ref, all_ref, gs_o, d_o, m_o, v_o, send_sems, recv_sems, local_sem):
        x, y, c, chips = _place()
        me, sibling = (x, y, c), (x, y, 1 - c)

        def rows(px, py, pc):
            return all_ref.at[pl.ds(pl.multiple_of((4 * px + 2 * py + pc) * SMALL_ROWS, SMALL_ROWS), SMALL_ROWS), :]

        def copy(k, block, to, src=None):
            return pltpu.make_async_remote_copy(
                src_ref=rows(*block) if src is None else src, dst_ref=rows(*block),
                send_sem=send_sems.at[k], recv_sem=recv_sems.at[k], device_id=to, device_id_type=MESH)

        mine = pltpu.make_async_copy(g_ref, rows(*me), local_sem)
        mine.start()
        first = [copy(0, me, sibling, src=g_ref)]
        first += [copy(1 + j, me, (*chip, c), src=g_ref) for j, chip in enumerate(chips)]
        for cp in first:
            cp.start()
        passed = [copy(4 + j, (*chip, c), sibling) for j, chip in enumerate(chips)]
        for j, chip in enumerate(chips):
            copy(1 + j, (*chip, c), me).wait_recv()
            passed[j].start()
        copy(0, sibling, me).wait_recv()
        for j, chip in enumerate(chips):
            copy(4 + j, (*chip, 1 - c), me).wait_recv()
        for cp in first + passed:
            cp.wait_send()
        mine.wait()

        tot = all_ref[0:SMALL_ROWS, :]
        for d in range(1, N_DEV):
            tot = tot + all_ref[d * SMALL_ROWS:(d + 1) * SMALL_ROWS, :]
        delta, mn, vn = _adamw_math(w_ref[...], tot, m_ref[...], v_ref[...])
        gs_o[...] = tot
        d_o[...] = delta
        m_o[...] = mn
        v_o[...] = vn

    vm = pl.BlockSpec(memory_space=pltpu.VMEM)
    shp = jax.ShapeDtypeStruct((SMALL_ROWS, LANES), F32)
    res = pl.pallas_call(
        body, name="small_allreduce_adamw", in_specs=[vm] * 4, out_specs=[vm] * 5,
        out_shape=[jax.ShapeDtypeStruct((N_DEV * SMALL_ROWS, LANES), F32), shp, shp, shp, shp],
        scratch_shapes=[pltpu.SemaphoreType.DMA((7,)), pltpu.SemaphoreType.DMA((7,)), pltpu.SemaphoreType.DMA],
    )(g, w, m, v)
    return res[1:]


SMALL_NAMES = ("g_mix", "b_gate", "b_forget", "qn_swa", "kn_swa", "sink_swa", "rel_bias", "qn_fox", "kn_fox",
               "g_mem", "qn_mem", "kn_mem", "g_mlp")
BIG_NAMES = ("w_in", "w_mem_kv", "w_o_swa", "w_o_fox", "w_o_mem", "w_out", "w_mlp_up", "w_mlp_down")
WEIGHT_NAMES = ("g_mix", "w_in", "b_gate", "b_forget", "qn_swa", "kn_swa", "sink_swa", "rel_bias", "qn_fox", "kn_fox",
                "g_mem", "w_mem_kv", "qn_mem", "kn_mem", "w_o_swa", "w_o_fox", "w_o_mem", "w_out", "g_mlp",
                "w_mlp_up", "w_mlp_down")


def _pack_small(parts, extra=None):
    rows = []
    for n in SMALL_NAMES:
        flat = parts[n].reshape(-1).astype(F32)
        flat = jnp.pad(flat, (0, (-flat.size) % LANES))
        rows.append(flat.reshape(-1, LANES))
    if extra is not None:
        rows.append(jnp.pad(extra.reshape(1, 1), ((0, 0), (0, LANES - 1))))
    packed = jnp.concatenate(rows, axis=0)
    return jnp.pad(packed, ((0, SMALL_ROWS - packed.shape[0]), (0, 0)))


def _unpack_small(packed, shapes):
    out, r = {}, 0
    for n in SMALL_NAMES:
        size = math.prod(shapes[n])
        nr = -(-size // LANES)
        out[n] = packed[r:r + nr].reshape(-1)[:size].reshape(shapes[n])
        r += nr
    return out, packed[r, 0]


W_IN_SEGMENTS = ((C_QA, 0, 512), (C_QF, 768, 512), (C_KF, 1280, 512), (C_VF, 1792, 512), (C_QM, 2312, 512),
                 (C_KA, 512, 128), (C_VA, 640, 128), (C_FL, 2304, FOX_HEADS), (C_GL, 2824, GATE_W))
RELAYOUT_ROWS = 256


def _permute_pieces(src_of_dst):
    blocks = []
    for b in range(len(src_of_dst) // LANES):
        runs, lane = [], 0
        while lane < LANES:
            src = src_of_dst[b * LANES + lane]
            if src is None:
                lane += 1
                continue
            plane, col = src
            end = lane + 1
            while (end < LANES and src_of_dst[b * LANES + end] == (plane, col + end - lane)
                   and (col + end - lane) // LANES == col // LANES):
                end += 1
            runs.append((plane, col // LANES, (lane - col) % LANES, lane, end))
            lane = end
        blocks.append(runs)
    return blocks


def _permuted_block(runs, load, rows):
    lane = _lane((rows, LANES))
    acc = jnp.zeros((rows, LANES), F32)
    for plane, blk, shift, lo, hi in runs:
        x = load(plane, blk).astype(F32)
        if shift:
            x = pltpu.roll(x, shift, 1)
        acc = x if (lo, hi) == (0, LANES) else jnp.where((lane >= lo) & (lane < hi), x, acc)
    return acc


def _w_in_to_segments(g_in):
    src_of_dst = [None] * PROJ_W
    for mine, theirs, width in W_IN_SEGMENTS:
        for k in range(width):
            src_of_dst[mine + k] = ((theirs + k) // IN_SHARD, (theirs + k) % IN_SHARD)
    blocks = _permute_pieces(src_of_dst)
    rb = RELAYOUT_ROWS

    def body(src_ref, out_ref):
        for b, runs in enumerate(blocks):
            blk = _permuted_block(runs, lambda p, c: src_ref[p, :, c * LANES:(c + 1) * LANES], rb)
            out_ref[:, b * LANES:(b + 1) * LANES] = blk.astype(out_ref.dtype)

    return pl.pallas_call(
        body, name="w_in_to_segments", grid=(D_MODEL // rb,),
        in_specs=[pl.BlockSpec((N_SHARD, rb, IN_SHARD_PAD), lambda i: (0, i, 0))],
        out_specs=pl.BlockSpec((rb, PROJ_W), lambda i: (i, 0)),
        out_shape=jax.ShapeDtypeStruct((D_MODEL, PROJ_W), g_in.dtype),
        compiler_params=_cparams("parallel", vmem=VMEM_MID),
    )(g_in)


def _w_in_from_segments(lo, gl):
    mine_of_theirs = {}
    for mine, theirs, width in W_IN_SEGMENTS:
        for k in range(width):
            mine_of_theirs[theirs + k] = mine + k
    src_of_dst = [None] * (N_SHARD * IN_SHARD_PAD)
    for s in range(N_SHARD):
        for l in range(IN_SHARD):
            j = mine_of_theirs[s * IN_SHARD + l]
            src_of_dst[s * IN_SHARD_PAD + l] = (j // LO_W, j % LO_W)
    blocks = _permute_pieces(src_of_dst)
    per_slot = IN_SHARD_PAD // LANES
    rb = RELAYOUT_ROWS

    def body(lo_ref, gl_ref, out_ref):
        planes = (lo_ref, gl_ref)
        for b, runs in enumerate(blocks):
            blk = _permuted_block(runs, lambda p, c: planes[p][:, c * LANES:(c + 1) * LANES], rb)
            c0 = (b % per_slot) * LANES
            out_ref[b // per_slot, :, c0:c0 + LANES] = blk

    half = pl.BlockSpec((rb, LO_W), lambda i: (i, 0))
    return pl.pallas_call(
        body, name="w_in_from_segments", grid=(D_MODEL // rb,),
        in_specs=[half, half],
        out_specs=pl.BlockSpec((N_SHARD, rb, IN_SHARD_PAD), lambda i: (0, i, 0)),
        out_shape=jax.ShapeDtypeStruct((N_SHARD, D_MODEL, IN_SHARD_PAD), F32),
        compiler_params=_cparams("parallel", vmem=VMEM_MID),
    )(lo, gl)


def _after(first, then):
    return lax.optimization_barrier((first, then))


class _ReduceGroup:
    def __init__(self, tag, first_collective_id, place):
        self.tag, self.first_id, self.place = tag, first_collective_id, place

    def start(self, local, tie):
        self.names = tuple(local)
        mine, tie = _after([local[n] for n in self.names], tie)
        self.mine = mine
        self.staged = _pair_exchange("pair_exchange_" + self.tag, self.first_id, mine)
        return tie

    def send(self, tie):
        staged, tie = _after(self.staged, tie)
        sums = [_pair_sum("pair_sum_" + n, self.place, g, st) for n, g, st in zip(self.names, self.mine, staged)]
        travel, tie = _after([s[0] for s in sums], tie)
        self.own = [s[1] for s in sums]
        self.got = _chip_exchange("chip_exchange_" + self.tag, self.first_id + 1, travel)
        return tie

    def finish(self, tie):
        got, tie = _after(self.got, tie)
        halves = [_final_sum("final_sum_" + n, self.place, o, r) for n, o, r in zip(self.names, self.own, got)]
        halves, tie = _after(halves, tie)
        summed = _pair_gather("pair_gather_" + self.tag, self.first_id + 2, halves)
        self.summed = dict(zip(self.names, summed))
        return tie


class _GradReducer:
    def __init__(self, place):
        self.early = _ReduceGroup("early", 2, place)
        self.late = _ReduceGroup("late", 5, place)

    @staticmethod
    def _slot_rows(a):
        return a.reshape(N_SHARD, a.shape[0] // N_SHARD, a.shape[1])

    def early_start(self, g, tie):
        return self.early.start({"w_mlp_down": self._slot_rows(g["w_mlp_down"]), "w_mlp_up": g["w_mlp_up"],
                                 "w_out": self._slot_rows(g["w_out"]), "w_mem_kv": self._slot_rows(g["w_mem_kv"]),
                                 "w_o_swa": g["w_o_swa"], "w_o_fox": g["w_o_fox"], "w_o_mem": g["w_o_mem"]}, tie)

    def early_send(self, tie):
        return self.early.send(tie)

    def early_finish(self, tie):
        return self.early.finish(tie)

    def late_start(self, g, tie):
        d_in = _w_in_from_segments(g["wc_lo"], g["wc_gl"])
        return self.late.start({"w_in": d_in}, tie)

    def late_send(self, tie):
        return self.late.send(tie)

    def late_finish(self, tie):
        return self.late.finish(tie)

    @property
    def summed(self):
        return {**self.early.summed, **self.late.summed}


def kernel(x, mem, g_mix, w_in, b_gate, b_forget, qn_swa, kn_swa, sink_swa, rel_bias, qn_fox, kn_fox, g_mem, w_mem_kv, qn_mem, kn_mem, w_o_swa, w_o_fox, w_o_mem, w_out, g_mlp, w_mlp_up, w_mlp_down, loss_target, m_g_mix, m_w_in, m_b_gate, m_b_forget, m_qn_swa, m_kn_swa, m_sink_swa, m_rel_bias, m_qn_fox, m_kn_fox, m_g_mem, m_w_mem_kv, m_qn_mem, m_kn_mem, m_w_o_swa, m_w_o_fox, m_w_o_mem, m_w_out, m_g_mlp, m_w_mlp_up, m_w_mlp_down, v_g_mix, v_w_in, v_b_gate, v_b_forget, v_qn_swa, v_kn_swa, v_sink_swa, v_rel_bias, v_qn_fox, v_kn_fox, v_g_mem, v_w_mem_kv, v_qn_mem, v_kn_mem, v_w_o_swa, v_w_o_fox, v_w_o_mem, v_w_out, v_g_mlp, v_w_mlp_up, v_w_mlp_down):
    given = dict(locals())
    W = {n: given[n] for n in WEIGHT_NAMES}
    M = {n: given["m_" + n] for n in WEIGHT_NAMES}
    V = {n: given["v_" + n] for n in WEIGHT_NAMES}
    pad_in = ((0, 0), (0, IN_SHARD_PAD - IN_SHARD))

    shards = [jnp.pad(w_in[0].astype(BF16), pad_in)] + [W[n][0].astype(BF16) for n in BIG_NAMES[1:]]
    slots = [jnp.broadcast_to(s[None], (N_SHARD,) + s.shape) for s in shards]
    (g_in,) = _all_gather_shards_async("all_gather_w_in", 1, slots[:1])
    small = {n: (W[n] if n == "rel_bias" else W[n].reshape(1, -1)) for n in SMALL_NAMES}
    h = _rmsnorm("rms_mix", x[0], small["g_mix"], min(512, x.shape[1]))
    g_in, late, h, (m_in, v_in) = lax.optimization_barrier((g_in, slots[1:], h, (M["w_in"][0], V["w_in"][0])))
    M["w_in"], V["w_in"] = m_in[None], v_in[None]
    g_kv, g_oa, g_of, g_om, g_out, g_up, g_down = _all_gather_shards_async("all_gather_weights_async", 8, late)

    place = jnp.stack([2 * lax.axis_index("x") + lax.axis_index("y"), lax.axis_index("c")]).astype(jnp.int32)
    reducer = _GradReducer(place)
    loss, grad_x, grads = _local_step(
        x[0], h, mem[0], loss_target[0], small, g_in, g_kv.reshape(D_MODEL, D_MODEL), (g_oa, g_of, g_om),
        g_out.reshape(D_MODEL, D_MODEL), g_up, g_down.reshape(D_FF, D_MODEL), reducer)

    out = {}

    def adamw_of(names, summed):
        for n in names:
            res = _adamw("adamw_" + n, W[n][0], summed[n], M[n][0], V[n][0])
            out[n] = [r.reshape(W[n].shape) for r in res]

    adamw_of(reducer.early.names, reducer.early.summed)
    shapes = {n: W[n].shape for n in SMALL_NAMES}
    packed = _small_allreduce_adamw(_pack_small(grads, loss), _pack_small(W), _pack_small(M), _pack_small(V))
    done_meanwhile = ([out[n] for n in reducer.early.names], packed)
    (early_out, packed), grad_x = reducer.late_finish((done_meanwhile, grad_x))
    for n, res in zip(reducer.early.names, early_out):
        out[n] = res
    adamw_of(reducer.late.names, reducer.late.summed)
    unpacked = [_unpack_small(p, shapes) for p in packed]
    for n in SMALL_NAMES:
        out[n] = [u[0][n] for u in unpacked]
    loss_total = unpacked[0][1]

    return (loss_total, grad_x.reshape(x.shape),
            *[out[n][0] for n in WEIGHT_NAMES], *[out[n][1] for n in WEIGHT_NAMES],
            *[out[n][2] for n in WEIGHT_NAMES], *[out[n][3] for n in WEIGHT_NAMES])
```

```python
import functools
import math

import jax
import jax.numpy as jnp
from jax import lax
from jax.experimental import pallas as pl
from jax.experimental.pallas import tpu as pltpu
from jax.experimental.pallas import tpu_sc as plsc

F32 = jnp.float32
BF16 = jnp.bfloat16

D_MODEL = 1024
N_MEM = 256
SWA_HEADS = 8
SWA_KV_HEADS = 2
SWA_HEAD_DIM = 64
WINDOW = 128
FOX_HEADS = 8
FOX_HEAD_DIM = 64
MEM_HEADS = 4
MEM_HEAD_DIM = 128
D_FF = 4 * D_MODEL
REL_BUCKETS = 32
REL_MAX_DIST = 128
EPS = 1e-6
NEG = -1e30
GATE_W = 3 * D_MODEL
IN_WIDTH = 5896
N_SHARD = 4
IN_SHARD = IN_WIDTH // N_SHARD
IN_SHARD_PAD = 1536

ADAM_LR = 0.001
ADAM_B1 = 0.9
ADAM_B2 = 0.999
ADAM_EPS = 1e-08
ADAM_WD = 0.01
ADAM_STEP = 10

LANES = 128
V7X_VMEM_BYTES = 64 * 1024 * 1024
MIB = 1024 * 1024
VMEM_SMALL, VMEM_MID, VMEM_BIG, VMEM_MAX = 24 * MIB, 40 * MIB, 48 * MIB, 56 * MIB

C_QA, C_QF, C_KF, C_VF, C_QM, C_KA, C_VA, C_FL, C_GL = 0, 512, 1024, 1536, 2048, 2560, 2688, 2816, 3072
LO_W = 3072
PROJ_W = 6144

NN = (((1,), (0,)), ((), ()))
NT = (((1,), (1,)), ((), ()))
TN = (((0,), (0,)), ((), ()))


def _dot(a, b, dims=NN):
    return lax.dot_general(a, b, dims, preferred_element_type=F32)


def _cparams(*sem, vmem=VMEM_SMALL):
    return pltpu.CompilerParams(dimension_semantics=sem, vmem_limit_bytes=vmem)


def _split3(a):
    hi = a.astype(BF16)
    r1 = a - hi.astype(F32)
    mid = r1.astype(BF16)
    lo = (r1 - mid.astype(F32)).astype(BF16)
    return hi, mid, lo


def _group_mean(a, g2):
    hi = a.astype(BF16)
    mid = (a - hi.astype(F32)).astype(BF16)
    return _dot(jnp.concatenate([hi, mid], axis=1), g2)


def _dot3_left(g, a):
    hi, mid, lo = _split3(a)
    return _dot(g, hi) + _dot(g, mid) + _dot(g, lo)


def _group_mean_matrix(d):
    r = jnp.arange(LANES)
    g = jnp.where((r[:, None] // d) == (r[None, :] // d), 1.0 / d, 0.0).astype(BF16)
    return jnp.concatenate([g, g], axis=0)


def _lane(shape):
    return lax.broadcasted_iota(jnp.int32, shape, len(shape) - 1)


def _matmul(name, a, b, *, dims, grid, a_spec, b_spec, acc_shape, outs, epilogue, extra=(), vmem=VMEM_BIG):
    nk = grid[2]
    n_extra = len(extra)

    def body(a_ref, b_ref, *rest):
        extra_refs = rest[:n_extra]
        out_refs = rest[n_extra:n_extra + len(outs)]
        i, j, k = pl.program_id(0), pl.program_id(1), pl.program_id(2)
        if nk == 1:
            epilogue(_dot(a_ref[...].astype(BF16), b_ref[...].astype(BF16), dims), extra_refs, out_refs, (i, j))
            return
        acc_ref = rest[-1]

        @pl.when(k == 0)
        def _():
            acc_ref[...] = jnp.zeros_like(acc_ref)

        acc_ref[...] += _dot(a_ref[...].astype(BF16), b_ref[...].astype(BF16), dims)

        @pl.when(k == nk - 1)
        def _():
            epilogue(acc_ref[...], extra_refs, out_refs, (i, j))

    res = pl.pallas_call(
        body,
        name=name,
        grid=grid,
        in_specs=[a_spec, b_spec] + [s for _, s in extra],
        out_specs=[s for _, s in outs],
        out_shape=[s for s, _ in outs],
        scratch_shapes=[pltpu.VMEM(acc_shape, F32)] if nk > 1 else [],
        compiler_params=_cparams("arbitrary", "arbitrary", "arbitrary", vmem=vmem),
    )(a, b, *[x for x, _ in extra])
    return res


def _epi_store(acc, extra_refs, out_refs, ij):
    out_refs[0][...] = acc.astype(out_refs[0].dtype)


def _rms_rows(x, g):
    r = lax.rsqrt(jnp.mean(x * x, axis=-1, keepdims=True) + EPS)
    return x * r, r


def _rmsnorm_bwd_rows(dh, x, g):
    xhat, r = _rms_rows(x, g)
    dxh = dh * g
    dx = r * (dxh - xhat * jnp.mean(dxh * xhat, axis=-1, keepdims=True))
    return dx, jnp.sum(dh * xhat, axis=0, keepdims=True)


def _rmsnorm(name, x, g, tb):
    T, Dm = x.shape

    def body(x_ref, g_ref, o_ref):
        xhat, _ = _rms_rows(x_ref[...], None)
        o_ref[...] = (xhat * g_ref[...]).astype(o_ref.dtype)

    return pl.pallas_call(
        body, name=name, grid=(T // tb,),
        in_specs=[pl.BlockSpec((tb, Dm), lambda i: (i, 0)), pl.BlockSpec((1, Dm), lambda i: (0, 0))],
        out_specs=pl.BlockSpec((tb, Dm), lambda i: (i, 0)),
        out_shape=jax.ShapeDtypeStruct((T, Dm), BF16),
        compiler_params=_cparams("parallel"),
    )(x, g)


def _head_norm(x, gm, gain):
    ms = _group_mean(x * x, gm)
    r = lax.rsqrt(ms + EPS)
    return x * r * gain, x * r


def _head_norm_bwd(dy, x, gm, gain):
    ms = _group_mean(x * x, gm)
    r = lax.rsqrt(ms + EPS)
    xhat = x * r
    dxh = dy * gain
    dx = r * (dxh - xhat * _group_mean(dxh * xhat, gm))
    return dx, jnp.sum(dy * xhat, axis=0, keepdims=True)


def _log_sigmoid(z):
    return jnp.minimum(z, 0.0) - jnp.log(1.0 + jnp.exp(-jnp.abs(z)))


def _prep_fwd(proj, gains, bfor, tril, gm64, gm128, T, tb):
    nb = T // tb

    def body(qa_ref, qf_ref, kf_ref, vf_ref, qm_ref, ka_ref, va_ref, fl_ref, gains_ref, bfor_ref, tril_ref,
             gm64_ref, gm128_ref,
             qa_o, qf_o, kf_o, vf_o, qm_o, kad_o, vad_o, qaug_o, kaug_o, carry):
        i = pl.program_id(0)
        gm64v = gm64_ref[...]
        gm128v = gm128_ref[...]
        lane = _lane((tb, LANES))

        def norm512(src, dst, row, gm, scale=1.0):
            gain = gains_ref[row:row + 1, :]
            for c in range(4):
                sl = slice(c * LANES, (c + 1) * LANES)
                y, _ = _head_norm(src[:, sl], gm, gain)
                dst[:, sl] = (y * scale).astype(dst.dtype)

        norm512(qa_ref, qa_o, 0, gm64v)
        norm512(qf_ref, qf_o, 2, gm64v, FOX_SCALE)
        norm512(kf_ref, kf_o, 3, gm64v)
        norm512(qm_ref, qm_o, 4, gm128v)
        vf_o[...] = vf_ref[...].astype(vf_o.dtype)

        ka_n, _ = _head_norm(ka_ref[...], gm64v, gains_ref[1:2, :])
        ka_r = pltpu.roll(ka_n, 64, 1)
        va = va_ref[...]
        va_r = pltpu.roll(va, 64, 1)
        lo = lane < 64
        kad_o[0] = jnp.where(lo, ka_n, ka_r).astype(kad_o.dtype)
        kad_o[1] = jnp.where(lo, ka_r, ka_n).astype(kad_o.dtype)
        vad_o[0] = jnp.where(lo, va, va_r).astype(vad_o.dtype)
        vad_o[1] = jnp.where(lo, va_r, va).astype(vad_o.dtype)

        @pl.when(i == 0)
        def _():
            carry[...] = jnp.zeros_like(carry)

        logf = jnp.where(lane < FOX_HEADS, _log_sigmoid(fl_ref[...] + bfor_ref[...]), 0.0)
        c = _dot3_left(tril_ref[...], logf) + carry[0:1, :]
        carry[...] = jnp.broadcast_to(c[tb - 1:tb, :], carry.shape)
        for pair in range(FOX_HEADS // 2):
            qaug = jnp.zeros((tb, LANES), F32)
            kaug = jnp.zeros((tb, LANES), F32)
            for sub in range(2):
                col = jnp.sum(jnp.where(lane == 2 * pair + sub, c, 0.0), axis=1, keepdims=True)
                pieces = [p.astype(F32) for p in _split3(col)]
                base = AUG_STRIDE * sub
                for e in range(3):
                    qaug = jnp.where(lane == base + AUG_C + e, pieces[e], qaug)
                    kaug = jnp.where(lane == base + AUG_NEG_C + e, -pieces[e], kaug)
                qaug = jnp.where((lane >= base + AUG_NEG_C) & (lane < base + AUG_NEG_C + 3), 1.0, qaug)
                ones_k = ((lane >= base + AUG_C) & (lane < base + AUG_C + 3)) | (
                    (lane >= base + AUG_STAT) & (lane < base + AUG_STAT + 3))
                kaug = jnp.where(ones_k, 1.0, kaug)
            sl = slice(pair * LANES, (pair + 1) * LANES)
            qaug_o[:, sl] = qaug.astype(BF16)
            kaug_o[:, sl] = kaug.astype(BF16)

    def seg(width, start):
        return pl.BlockSpec((tb, width), lambda i, s=start // width: (i, s))

    const = lambda shape: pl.BlockSpec(shape, lambda i: tuple(0 for _ in shape))
    rows512 = pl.BlockSpec((tb, 512), lambda i: (i, 0))
    outs = pl.pallas_call(
        body, name="prep_fwd", grid=(nb,),
        in_specs=[seg(512, C_QA), seg(512, C_QF), seg(512, C_KF), seg(512, C_VF), seg(512, C_QM),
                  seg(128, C_KA), seg(128, C_VA), seg(128, C_FL),
                  const((8, LANES)), const((1, LANES)), const((tb, tb)), const((2 * LANES, LANES)), const((2 * LANES, LANES))],
        out_specs=[rows512, rows512, rows512, rows512, rows512,
                   pl.BlockSpec((2, tb, LANES), lambda i: (0, i, 0)), pl.BlockSpec((2, tb, LANES), lambda i: (0, i, 0)),
                   rows512, rows512],
        out_shape=[jax.ShapeDtypeStruct((T, 512), BF16)] * 5
        + [jax.ShapeDtypeStruct((2, T, LANES), BF16)] * 2
        + [jax.ShapeDtypeStruct((T, 512), BF16)] * 2,
        scratch_shapes=[pltpu.VMEM((8, LANES), F32)],
        compiler_params=_cparams("arbitrary", vmem=VMEM_MID),
    )(proj, proj, proj, proj, proj, proj, proj, proj, gains, bfor, tril, gm64, gm128)
    return outs


def _prep_bwd(proj, dqa, dkad, dvad, dqf, dkf, dvf, dqm, dqf_aug, dkf_aug, gains, bfor, triu, gm64, gm128, T, tb):
    nb = T // tb

    def body(qa_ref, qf_ref, kf_ref, qm_ref, ka_ref, fl_ref,
             dqa_ref, dkad_ref, dvad_ref, dqf_ref, dkf_ref, dvf_ref, dqm_ref, dqfa_ref, dkfa_ref,
             gains_ref, bfor_ref, triu_ref, gm64_ref, gm128_ref,
             dlo_o, gacc_o, carry):
        i = pl.program_id(0)
        gm64v = gm64_ref[...]
        gm128v = gm128_ref[...]
        lane = _lane((tb, LANES))

        @pl.when(i == 0)
        def _():
            carry[...] = jnp.zeros_like(carry)
            gacc_o[...] = jnp.zeros_like(gacc_o)

        def norm512_bwd(dsrc, xsrc, col0, row, gm):
            gain = gains_ref[row:row + 1, :]
            gsum = jnp.zeros((1, LANES), F32)
            for c in range(4):
                sl = slice(c * LANES, (c + 1) * LANES)
                dx, dg = _head_norm_bwd(dsrc[:, sl], xsrc[:, sl], gm, gain)
                dlo_o[:, col0 + c * LANES:col0 + (c + 1) * LANES] = dx.astype(dlo_o.dtype)
                gsum = gsum + dg
            gacc_o[row:row + 1, :] += gsum

        norm512_bwd(dqa_ref, qa_ref, C_QA, 0, gm64v)
        norm512_bwd(dqf_ref, qf_ref, C_QF, 2, gm64v)
        norm512_bwd(dkf_ref, kf_ref, C_KF, 3, gm64v)
        norm512_bwd(dqm_ref, qm_ref, C_QM, 4, gm128v)
        dlo_o[:, C_VF:C_VF + 512] = dvf_ref[...].astype(dlo_o.dtype)

        lo = lane < 64

        def fold(ref):
            f0 = ref[0] + pltpu.roll(ref[0], 64, 1)
            f1 = ref[1] + pltpu.roll(ref[1], 64, 1)
            return jnp.where(lo, f0, f1)

        dka, dg = _head_norm_bwd(fold(dkad_ref), ka_ref[...], gm64v, gains_ref[1:2, :])
        gacc_o[1:2, :] += dg
        dlo_o[:, C_KA:C_KA + LANES] = dka.astype(dlo_o.dtype)
        dlo_o[:, C_VA:C_VA + LANES] = fold(dvad_ref).astype(dlo_o.dtype)

        dc = jnp.zeros((tb, LANES), F32)
        for pair in range(FOX_HEADS // 2):
            sl = slice(pair * LANES, (pair + 1) * LANES)
            rows_sum, cols_sum = dqfa_ref[:, sl], dkfa_ref[:, sl]
            for sub in range(2):
                diff = (jnp.where(lane == AUG_STRIDE * sub + AUG_C, rows_sum, 0.0)
                        - jnp.where(lane == AUG_STRIDE * sub + AUG_NEG_C, cols_sum, 0.0))
                dc = jnp.where(lane == 2 * pair + sub, jnp.sum(diff, axis=1, keepdims=True), dc)
        dlogf = _dot3_left(triu_ref[...], dc) + carry[0:1, :]
        carry[...] = jnp.broadcast_to(dlogf[0:1, :], carry.shape)
        z = fl_ref[...] + bfor_ref[...]
        dfl = jnp.where(lane < FOX_HEADS, dlogf / (1.0 + jnp.exp(z)), 0.0)
        gacc_o[5:6, :] += jnp.sum(dfl, axis=0, keepdims=True)
        dlo_o[:, C_FL:C_FL + LANES] = dfl.astype(dlo_o.dtype)
        dlo_o[:, C_FL + LANES:C_FL + 2 * LANES] = jnp.zeros((tb, LANES), dlo_o.dtype)

    rev = lambda i: nb - 1 - i

    def seg(width, start):
        return pl.BlockSpec((tb, width), lambda i, s=start // width: (rev(i), s))

    const = lambda shape: pl.BlockSpec(shape, lambda i: tuple(0 for _ in shape))
    rows512 = pl.BlockSpec((tb, 512), lambda i: (rev(i), 0))
    dup = pl.BlockSpec((2, tb, LANES), lambda i: (0, rev(i), 0))
    return pl.pallas_call(
        body, name="prep_bwd", grid=(nb,),
        in_specs=[seg(512, C_QA), seg(512, C_QF), seg(512, C_KF), seg(512, C_QM), seg(128, C_KA), seg(128, C_FL),
                  rows512, dup, dup, rows512, rows512, rows512, rows512, rows512, rows512,
                  const((8, LANES)), const((1, LANES)), const((tb, tb)), const((2 * LANES, LANES)), const((2 * LANES, LANES))],
        out_specs=[pl.BlockSpec((tb, LO_W), lambda i: (rev(i), 0)), const((8, LANES))],
        out_shape=[jax.ShapeDtypeStruct((T, LO_W), BF16), jax.ShapeDtypeStruct((8, LANES), F32)],
        scratch_shapes=[pltpu.VMEM((8, LANES), F32)],
        compiler_params=_cparams("arbitrary", vmem=VMEM_MID),
    )(proj, proj, proj, proj, proj, proj, dqa, dkad, dvad, dqf, dkf, dvf, dqm, dqf_aug, dkf_aug,
      gains, bfor, triu, gm64, gm128)


FOX_SCALE = FOX_HEAD_DIM ** -0.5
AUG_STRIDE = 16
AUG_C = 0
AUG_NEG_C = 3
AUG_STAT = 6
FOX_TQ, FOX_TK = 1024, 1024
FOX_BWD_TQ, FOX_BWD_TK = 1024, 1024
FOX_DIAGONAL_PARTS = 4


def _fox_head_mask(sub, rows):
    lane = _lane((rows, 2 * LANES))
    main = (lane >= 64 * sub) & (lane < 64 * sub + 64)
    aug = (lane >= LANES + AUG_STRIDE * sub) & (lane < LANES + AUG_STRIDE * (sub + 1))
    return main | aug


def _fox_pieces(diagonal, tq, tk):
    if diagonal and tq == tk and tq >= FOX_DIAGONAL_PARTS * LANES:
        step = tq // FOX_DIAGONAL_PARTS
        return [(n * step, (n + 1) * step, (n + 1) * step) for n in range(FOX_DIAGONAL_PARTS)]
    return [(0, tq, tk)]


def _fox_fwd(q, qaug, k, kaug, v, T, tq, tk):
    nq, nk = T // tq, T // tk
    rep = tk // LANES
    last_of = lambda i: (i * tq + tq - 1) // tk

    def body(q_ref, qa_ref, k_ref, ka_ref, v_ref, o_ref, qab_ref, m_s, acc_s):
        p_, i, j = pl.program_id(0), pl.program_id(1), pl.program_id(2)
        last = last_of(i)

        @pl.when(j == 0)
        def _():
            m_s[...] = jnp.full(m_s.shape, NEG, F32)
            acc_s[...] = jnp.zeros_like(acc_s)

        def step(diagonal):
            k2 = jnp.concatenate([k_ref[...], ka_ref[...]], axis=1)
            v2 = jnp.concatenate([v_ref[...], ka_ref[...]], axis=1)
            pieces = _fox_pieces(diagonal, tq, tk)
            work = []
            for r0, r1, nc in pieces:
                rows = slice(r0, r1)
                q2 = jnp.concatenate([q_ref[rows, :], qa_ref[rows, :]], axis=1)
                for sub in range(2):
                    qh = jnp.where(_fox_head_mask(sub, r1 - r0), q2, jnp.zeros_like(q2))
                    work.append((rows, r0, r1 - r0, nc, sub, _dot(qh, k2[:nc], NT)))
            for rows, r0, nr, nc, sub, s in work:
                if diagonal:
                    causal = (lax.broadcasted_iota(jnp.int32, (nr, nc), 1) + j * tk
                              <= lax.broadcasted_iota(jnp.int32, (nr, nc), 0) + (r0 + i * tq))
                    s = jnp.where(causal, s, NEG)
                m_prev = m_s[sub, rows, :]
                m_next = jnp.maximum(m_prev, jnp.max(s, axis=1, keepdims=True))
                p = jnp.exp(s - jnp.tile(m_next, (1, nc // LANES)))
                alpha = jnp.exp(m_prev - m_next)
                m_s[sub, rows, :] = m_next
                acc_s[sub, rows, :] = acc_s[sub, rows, :] * jnp.tile(alpha, (1, 2)) + _dot(p.astype(BF16), v2[:nc])

        @pl.when(j == last)
        def _():
            step(True)

        @pl.when(j < last)
        def _():
            step(False)

        @pl.when(j == nk - 1)
        def _():
            lane = _lane((tq, LANES))
            outs = []
            qab = qa_ref[...].astype(F32)
            for sub in range(2):
                acc = acc_s[sub]
                base = AUG_STRIDE * sub
                l = jnp.sum(jnp.where(lane == base + AUG_C, acc[:, LANES:], 0.0), axis=1, keepdims=True)
                outs.append(acc[:, :LANES] / l)
                lse = jnp.max(m_s[sub], axis=1, keepdims=True) + jnp.log(l)
                pieces = _split3(-lse)
                for e in range(3):
                    qab = jnp.where(lane == base + AUG_STAT + e, pieces[e].astype(F32), qab)
            o_ref[...] = jnp.where(lane < 64, outs[0], outs[1]).astype(o_ref.dtype)
            qab_ref[...] = qab.astype(BF16)

    qspec = pl.BlockSpec((tq, LANES), lambda p, i, j: (i, p))
    kspec = pl.BlockSpec((tk, LANES), lambda p, i, j: (jnp.minimum(j, last_of(i)), p))
    return pl.pallas_call(
        body, name="fox_fwd", grid=(4, nq, nk),
        in_specs=[qspec, qspec, kspec, kspec, kspec],
        out_specs=[qspec, qspec],
        out_shape=[jax.ShapeDtypeStruct((T, 512), BF16), jax.ShapeDtypeStruct((T, 512), BF16)],
        scratch_shapes=[pltpu.VMEM((2, tq, LANES), F32), pltpu.VMEM((2, tq, 2 * LANES), F32)],
        compiler_params=_cparams("parallel", "parallel", "arbitrary", vmem=VMEM_BIG),
    )(q, qaug, k, kaug, v)


def _fox_bwd(q, qaug, k, kaug, v, do, doaug, T, tq, tk):
    nq, nk = T // tq, T // tk
    first_of = lambda j: (j * tk) // tq

    def body(q_ref, qa_ref, k_ref, ka_ref, v_ref, do_ref, doa_ref,
             dq_ref, dqa_ref, dk_ref, dka_ref, dv_ref, dk_s, dv_s):
        p_, j, i = pl.program_id(0), pl.program_id(1), pl.program_id(2)
        masked = i * tq < (j + 1) * tk - 1

        @pl.when((j == 0) & (i == 0))
        def _():
            dq_ref[...] = jnp.zeros_like(dq_ref)
            dqa_ref[...] = jnp.zeros_like(dqa_ref)

        @pl.when(i == 0)
        def _():
            dk_s[...] = jnp.zeros_like(dk_s)
            dv_s[...] = jnp.zeros_like(dv_s)

        def step(diagonal):
            k2 = jnp.concatenate([k_ref[...], ka_ref[...]], axis=1)
            v2 = jnp.concatenate([v_ref[...], ka_ref[...]], axis=1)
            work = []
            for r0, r1, nc in _fox_pieces(diagonal, tq, tk):
                rows = slice(r0, r1)
                q2 = jnp.concatenate([q_ref[rows, :], qa_ref[rows, :]], axis=1)
                do2 = jnp.concatenate([do_ref[rows, :], doa_ref[rows, :]], axis=1)
                for sub in range(2):
                    hm = _fox_head_mask(sub, r1 - r0)
                    qh = jnp.where(hm, q2, jnp.zeros_like(q2))
                    doh = jnp.where(hm, do2, jnp.zeros_like(do2))
                    s = _dot(qh, k2[:nc], NT)
                    dp = _dot(doh, v2[:nc], NT)
                    work.append((r0, r1 - r0, nc, sub, qh, doh, s, dp))
            dqs = {}
            for r0, nr, nc, sub, qh, doh, s, dp in work:
                if diagonal:
                    causal = (lax.broadcasted_iota(jnp.int32, (nr, nc), 1) + j * tk
                              <= lax.broadcasted_iota(jnp.int32, (nr, nc), 0) + (r0 + i * tq))
                    s = jnp.where(causal, s, NEG)
                p = jnp.exp(s)
                dsb = (p * dp).astype(BF16)
                dv_s[0:nc, :] += _dot(p.astype(BF16), doh[:, :LANES], TN)
                dk_s[0:nc, :] += _dot(dsb, qh, TN)
                dqs[(r0, sub)] = _dot(dsb, k2[:nc])
            for r0, r1, nc in _fox_pieces(diagonal, tq, tk):
                dq2 = jnp.where(_fox_head_mask(0, r1 - r0), dqs[(r0, 0)], dqs[(r0, 1)])
                qrows = pl.ds(pl.multiple_of(i * tq + r0, r1 - r0), r1 - r0)
                dq_ref[qrows, :] += dq2[:, :LANES] * FOX_SCALE
                dqa_ref[qrows, :] += dq2[:, LANES:]

        @pl.when((i >= first_of(j)) & masked)
        def _():
            step(True)

        @pl.when((i >= first_of(j)) & jnp.logical_not(masked))
        def _():
            step(False)

        @pl.when(i == nq - 1)
        def _():
            dk_ref[...] = dk_s[:, :LANES]
            dka_ref[...] = dk_s[:, LANES:]
            dv_ref[...] = dv_s[...]

    qspec = pl.BlockSpec((tq, LANES), lambda p, j, i: (jnp.maximum(i, first_of(j)), p))
    kspec = pl.BlockSpec((tk, LANES), lambda p, j, i: (j, p))
    resident = pl.BlockSpec((T, LANES), lambda p, j, i: (0, p))
    return pl.pallas_call(
        body, name="fox_bwd", grid=(4, nk, nq),
        in_specs=[qspec, qspec, kspec, kspec, kspec, qspec, qspec],
        out_specs=[resident, resident, kspec, kspec, kspec],
        out_shape=[jax.ShapeDtypeStruct((T, 512), F32)] * 5,
        scratch_shapes=[pltpu.VMEM((tk, 2 * LANES), F32), pltpu.VMEM((tk, LANES), F32)],
        compiler_params=_cparams("arbitrary", "arbitrary", "arbitrary", vmem=VMEM_BIG),
    )(q, qaug, k, kaug, v, do, doaug)


SWA_SUB = 4
SWA_TB = SWA_SUB * WINDOW


def _t5_bucket_matrix():
    t = jnp.arange(WINDOW)[:, None] + WINDOW
    s = jnp.arange(2 * WINDOW)[None, :]
    max_exact = REL_BUCKETS // 2
    d = jnp.maximum(t - s, 0)
    df = jnp.maximum(d, 1).astype(F32)
    large = max_exact + (jnp.log(df / max_exact) / math.log(REL_MAX_DIST / max_exact)
                         * (REL_BUCKETS - max_exact)).astype(jnp.int32)
    large = jnp.minimum(large, REL_BUCKETS - 1)
    return jnp.where(d < max_exact, d, large).astype(jnp.int32)


def _swa_bias(rel_bias, bucket):
    def body(rel_ref, bucket_ref, o_ref):
        b = bucket_ref[...]
        for h in range(SWA_HEADS):
            acc = jnp.zeros(b.shape, F32)
            for r in range(REL_BUCKETS):
                acc = jnp.where(b == r, rel_ref[r, h], acc)
            o_ref[h] = acc

    return pl.pallas_call(
        body, name="swa_bias",
        in_specs=[pl.BlockSpec(memory_space=pltpu.SMEM), pl.BlockSpec(memory_space=pltpu.VMEM)],
        out_specs=pl.BlockSpec(memory_space=pltpu.VMEM),
        out_shape=jax.ShapeDtypeStruct((SWA_HEADS, WINDOW, 2 * WINDOW), F32),
    )(rel_bias, bucket)


def _swa_bias_bwd(dbias, bucket):
    def body(db_ref, bucket_ref, o_ref):
        b = bucket_ref[...]
        lane = _lane((1, LANES))
        for r in range(REL_BUCKETS):
            row = jnp.zeros((1, LANES), F32)
            for h in range(SWA_HEADS):
                part = jnp.sum(jnp.where(b == r, db_ref[h], 0.0), axis=0, keepdims=True)
                tot = jnp.sum(part, axis=1, keepdims=True)
                row = jnp.where(lane == h, tot, row)
            o_ref[r:r + 1, :] = row

    return pl.pallas_call(
        body, name="swa_bias_bwd",
        in_specs=[pl.BlockSpec(memory_space=pltpu.VMEM), pl.BlockSpec(memory_space=pltpu.VMEM)],
        out_specs=pl.BlockSpec(memory_space=pltpu.VMEM),
        out_shape=jax.ShapeDtypeStruct((REL_BUCKETS, LANES), F32),
    )(dbias, bucket)


SWA_GROUP = SWA_HEADS // SWA_KV_HEADS


def _swa_valid(r, i):
    t = (lax.broadcasted_iota(jnp.int32, (SWA_GROUP * WINDOW, 2 * WINDOW), 0) & (WINDOW - 1)) + WINDOW
    s = lax.broadcasted_iota(jnp.int32, (SWA_GROUP * WINDOW, 2 * WINDOW), 1)
    dist = t - s
    band = (dist >= 0) & (dist < WINDOW)
    if r == 0:
        band = band & ((s >= WINDOW) | (i > 0))
    return band


def _swa_stack(blk):
    lane = _lane((WINDOW, LANES))
    parts = []
    for g in range(SWA_GROUP):
        b = blk[:, LANES * (g // 2):LANES * (g // 2 + 1)]
        parts.append(jnp.where((lane >= 64) if g % 2 else (lane < 64), b, jnp.zeros_like(b)))
    return jnp.concatenate(parts, axis=0)


def _swa_unstack(st):
    lane = _lane((WINDOW, LANES))
    W = WINDOW
    return jnp.concatenate([jnp.where(lane < 64, st[2 * b * W:(2 * b + 1) * W], st[(2 * b + 1) * W:(2 * b + 2) * W])
                            for b in range(2)], axis=1)


def _swa_sink_column(sink_ref, kvh):
    row = lax.broadcasted_iota(jnp.int32, (SWA_GROUP * WINDOW, 1), 0)
    col = jnp.full((SWA_GROUP * WINDOW, 1), sink_ref[SWA_GROUP * kvh + SWA_GROUP - 1], F32)
    for g in range(SWA_GROUP - 2, -1, -1):
        col = jnp.where(row < (g + 1) * WINDOW, sink_ref[SWA_GROUP * kvh + g], col)
    return col


def _swa_specs(T):
    W = WINDOW
    qspec = pl.BlockSpec((SWA_TB, 2 * LANES), lambda h, i: (i, h))
    own = pl.BlockSpec((None, SWA_TB, LANES), lambda h, i: (h, i, 0))
    prev = pl.BlockSpec((None, W, LANES), lambda h, i: (h, jnp.maximum(SWA_SUB * i - 1, 0), 0))
    stat = pl.BlockSpec((SWA_GROUP, SWA_TB, LANES), lambda h, i: (h, i, 0))
    bias = pl.BlockSpec((None, SWA_GROUP * W, 2 * W), lambda h, i: (h, 0, 0))
    return qspec, own, prev, stat, bias


def _swa_fwd(sinks, q, kad, vad, bias, T):
    nb = T // SWA_TB
    scale = SWA_HEAD_DIM ** -0.5
    W = WINDOW

    def body(sink_ref, q_ref, k_ref, kp_ref, v_ref, vp_ref, bias_ref, o_ref, lse_ref):
        kvh, i = pl.program_id(0), pl.program_id(1)
        sink = _swa_sink_column(sink_ref, kvh)
        for r in range(SWA_SUB):
            rs = slice(r * W, (r + 1) * W)
            ps = slice((r - 1) * W, r * W)
            k_own, v_own = k_ref[rs, :], v_ref[rs, :]
            k_prev = kp_ref[...] if r == 0 else k_ref[ps, :]
            v_prev = vp_ref[...] if r == 0 else v_ref[ps, :]
            qs = _swa_stack(q_ref[rs, :])
            s = jnp.concatenate([_dot(qs, k_prev, NT), _dot(qs, k_own, NT)], axis=1) * scale + bias_ref[...]
            s = jnp.where(_swa_valid(r, i), s, NEG)
            m = jnp.maximum(jnp.max(s, axis=1, keepdims=True), sink)
            p = jnp.exp(s - m)
            denom = jnp.sum(p, axis=1, keepdims=True) + jnp.exp(sink - m)
            pn = (p / denom).astype(BF16)
            o_ref[rs, :] = _swa_unstack(_dot(pn[:, :W], v_prev) + _dot(pn[:, W:], v_own)).astype(o_ref.dtype)
            lse = m + jnp.log(denom)
            for g in range(SWA_GROUP):
                lse_ref[g, rs, :] = jnp.broadcast_to(lse[g * W:(g + 1) * W], (W, LANES))

    qspec, own, prev, stat, bspec = _swa_specs(T)
    return pl.pallas_call(
        body, name="swa_fwd", grid=(SWA_KV_HEADS, nb),
        in_specs=[pl.BlockSpec(memory_space=pltpu.SMEM), qspec, own, prev, own, prev, bspec],
        out_specs=[qspec, stat],
        out_shape=[jax.ShapeDtypeStruct((T, 512), BF16), jax.ShapeDtypeStruct((SWA_HEADS, T, LANES), F32)],
        compiler_params=_cparams("parallel", "parallel", vmem=VMEM_MID),
    )(sinks, q, kad, kad, vad, vad, bias.reshape(SWA_KV_HEADS, SWA_GROUP * W, 2 * W))


def _swa_bwd(sinks, q, kad, vad, bias, do, lse, delta, T):
    nb = T // SWA_TB
    scale = SWA_HEAD_DIM ** -0.5
    W = WINDOW

    def body(sink_ref, q_ref, k_ref, kp_ref, v_ref, vp_ref, bias_ref, do_ref, lse_ref, dl_ref,
             dq_ref, dkad_ref, dvad_ref, dbias_ref, dsk_ref):
        kvh, i = pl.program_id(0), pl.program_id(1)
        sink = _swa_sink_column(sink_ref, kvh)

        @pl.when((kvh == 0) & (i == 0))
        def _():
            dkad_ref[...] = jnp.zeros_like(dkad_ref)
            dvad_ref[...] = jnp.zeros_like(dvad_ref)

        @pl.when(i == 0)
        def _():
            dbias_ref[...] = jnp.zeros_like(dbias_ref)
            dsk_ref[...] = jnp.zeros_like(dsk_ref)

        for r in range(SWA_SUB):
            rs = slice(r * W, (r + 1) * W)
            ps = slice((r - 1) * W, r * W)
            k_own, v_own = k_ref[rs, :], v_ref[rs, :]
            k_prev = kp_ref[...] if r == 0 else k_ref[ps, :]
            v_prev = vp_ref[...] if r == 0 else v_ref[ps, :]
            qs = _swa_stack(q_ref[rs, :])
            dos = _swa_stack(do_ref[rs, :])
            lse_b = jnp.concatenate([lse_ref[g, rs, :] for g in range(SWA_GROUP)], axis=0)
            dl_b = jnp.concatenate([dl_ref[g, rs, :] for g in range(SWA_GROUP)], axis=0)
            s = jnp.concatenate([_dot(qs, k_prev, NT), _dot(qs, k_own, NT)], axis=1) * scale + bias_ref[...]
            s = jnp.where(_swa_valid(r, i), s, NEG)
            p = jnp.exp(s - jnp.tile(lse_b, (1, 2)))
            dp = jnp.concatenate([_dot(dos, v_prev, NT), _dot(dos, v_own, NT)], axis=1)
            ds = p * (dp - jnp.tile(dl_b, (1, 2)))
            sink_term = jnp.exp(sink - lse_b) * dl_b
            for g in range(SWA_GROUP):
                dbias_ref[g] += ds[g * W:(g + 1) * W]
                dsk_ref[g:g + 1, :] += jnp.sum(sink_term[g * W:(g + 1) * W], axis=0, keepdims=True)
            dsb = ds.astype(BF16)
            pb = p.astype(BF16)
            dq_ref[rs, :] = _swa_unstack((_dot(dsb[:, :W], k_prev) + _dot(dsb[:, W:], k_own)) * scale)
            own_row = pl.multiple_of(i * SWA_TB + r * W, W)
            dkad_ref[kvh, pl.ds(own_row, W), :] += _dot(dsb[:, W:], qs, TN) * scale
            dvad_ref[kvh, pl.ds(own_row, W), :] += _dot(pb[:, W:], dos, TN)
            dk_prev = _dot(dsb[:, :W], qs, TN) * scale
            dv_prev = _dot(pb[:, :W], dos, TN)
            if r == 0:
                @pl.when(i > 0)
                def _():
                    prev_row = pl.multiple_of(i * SWA_TB - W, W)
                    dkad_ref[kvh, pl.ds(prev_row, W), :] += dk_prev
                    dvad_ref[kvh, pl.ds(prev_row, W), :] += dv_prev
            else:
                prev_row = pl.multiple_of(i * SWA_TB + (r - 1) * W, W)
                dkad_ref[kvh, pl.ds(prev_row, W), :] += dk_prev
                dvad_ref[kvh, pl.ds(prev_row, W), :] += dv_prev

    qspec, own, prev, stat, bspec = _swa_specs(T)
    full = pl.BlockSpec((SWA_KV_HEADS, T, LANES), lambda h, i: (0, 0, 0))
    return pl.pallas_call(
        body, name="swa_bwd", grid=(SWA_KV_HEADS, nb),
        in_specs=[pl.BlockSpec(memory_space=pltpu.SMEM), qspec, own, prev, own, prev, bspec, qspec, stat, stat],
        out_specs=[qspec, full, full, pl.BlockSpec((SWA_GROUP, W, 2 * W), lambda h, i: (h, 0, 0)),
                   pl.BlockSpec((None, 8, LANES), lambda h, i: (h, 0, 0))],
        out_shape=[jax.ShapeDtypeStruct((T, 512), F32), jax.ShapeDtypeStruct((SWA_KV_HEADS, T, LANES), F32),
                   jax.ShapeDtypeStruct((SWA_KV_HEADS, T, LANES), F32), jax.ShapeDtypeStruct((SWA_HEADS, W, 2 * W), F32),
                   jax.ShapeDtypeStruct((SWA_KV_HEADS, 8, LANES), F32)],
        compiler_params=_cparams("arbitrary", "arbitrary", vmem=VMEM_MID),
    )(sinks, q, kad, kad, vad, vad, bias.reshape(SWA_KV_HEADS, SWA_GROUP * W, 2 * W), do, lse, delta)


MEM_TQ = 2048


def _mem_fwd(q, mk, mv, T, tq):
    scale = MEM_HEAD_DIM ** -0.5

    def body(q_ref, k_ref, v_ref, o_ref, lse_ref):
        s = _dot(q_ref[...], k_ref[...], NT) * scale
        m = jnp.max(s, axis=1, keepdims=True)
        p = jnp.exp(s - m)
        l = jnp.sum(p, axis=1, keepdims=True)
        o_ref[...] = _dot((p / l).astype(BF16), v_ref[...]).astype(o_ref.dtype)
        lse_ref[...] = jnp.broadcast_to(m + jnp.log(l), (tq, LANES))

    qspec = pl.BlockSpec((tq, LANES), lambda h, i: (i, h))
    kspec = pl.BlockSpec((N_MEM, LANES), lambda h, i: (0, h))
    return pl.pallas_call(
        body, name="mem_fwd", grid=(MEM_HEADS, T // tq),
        in_specs=[qspec, kspec, kspec],
        out_specs=[qspec, pl.BlockSpec((None, tq, LANES), lambda h, i: (h, i, 0))],
        out_shape=[jax.ShapeDtypeStruct((T, 512), BF16), jax.ShapeDtypeStruct((MEM_HEADS, T, LANES), F32)],
        compiler_params=_cparams("parallel", "parallel"),
    )(q, mk, mv)


def _mem_bwd(q, mk, mv, do, lse, delta, T, tq):
    scale = MEM_HEAD_DIM ** -0.5
    rep = N_MEM // LANES

    def body(q_ref, k_ref, v_ref, do_ref, lse_ref, dl_ref, dq_ref, dk_ref, dv_ref):
        i = pl.program_id(1)

        @pl.when(i == 0)
        def _():
            dk_ref[...] = jnp.zeros_like(dk_ref)
            dv_ref[...] = jnp.zeros_like(dv_ref)

        qv, dov = q_ref[...], do_ref[...]
        s = _dot(qv, k_ref[...], NT) * scale
        p = jnp.exp(s - jnp.tile(lse_ref[...], (1, rep)))
        dp = _dot(dov, v_ref[...], NT)
        ds = p * (dp - jnp.tile(dl_ref[...], (1, rep)))
        dsb = ds.astype(BF16)
        dq_ref[...] = _dot(dsb, k_ref[...]) * scale
        dk_ref[...] += _dot(dsb, qv, TN) * scale
        dv_ref[...] += _dot(p.astype(BF16), dov, TN)

    qspec = pl.BlockSpec((tq, LANES), lambda h, i: (i, h))
    kspec = pl.BlockSpec((N_MEM, LANES), lambda h, i: (0, h))
    stat = pl.BlockSpec((None, tq, LANES), lambda h, i: (h, i, 0))
    return pl.pallas_call(
        body, name="mem_bwd", grid=(MEM_HEADS, T // tq),
        in_specs=[qspec, kspec, kspec, qspec, stat, stat],
        out_specs=[qspec, kspec, kspec],
        out_shape=[jax.ShapeDtypeStruct((T, 512), F32), jax.ShapeDtypeStruct((N_MEM, 512), F32),
                   jax.ShapeDtypeStruct((N_MEM, 512), F32)],
        compiler_params=_cparams("arbitrary", "arbitrary"),
    )(q, mk, mv, do, lse, delta)


def _mem_prep_fwd(mem, g_mem, w_kv, kn_gain, gm128):
    def body(mem_ref, g_ref, w_ref, kn_ref, gm_ref, memn_o, kv_o, mk_o, mv_o):
        xhat, _ = _rms_rows(mem_ref[...], None)
        memn = (xhat * g_ref[...]).astype(BF16)
        memn_o[...] = memn
        kv = _dot(memn, w_ref[...])
        kv_o[...] = kv
        gm = gm_ref[...]
        for c in range(4):
            sl = slice(c * LANES, (c + 1) * LANES)
            y, _ = _head_norm(kv[:, sl], gm, kn_ref[...])
            mk_o[:, sl] = y.astype(BF16)
        mv_o[...] = kv[:, 512:].astype(BF16)

    vm = pl.BlockSpec(memory_space=pltpu.VMEM)
    return pl.pallas_call(
        body, name="mem_prep_fwd", in_specs=[vm] * 5, out_specs=[vm] * 4,
        out_shape=[jax.ShapeDtypeStruct((N_MEM, D_MODEL), BF16), jax.ShapeDtypeStruct((N_MEM, D_MODEL), F32),
                   jax.ShapeDtypeStruct((N_MEM, 512), BF16), jax.ShapeDtypeStruct((N_MEM, 512), BF16)],
        compiler_params=pltpu.CompilerParams(vmem_limit_bytes=VMEM_MID),
    )(mem, g_mem, w_kv, kn_gain, gm128)


def _mem_prep_bwd(mem, g_mem, memn, kv, w_kv, kn_gain, gm128, dmk, dmv):
    def body(mem_ref, g_ref, memn_ref, kv_ref, w_ref, kn_ref, gm_ref, dmk_ref, dmv_ref, dw_o, dg_o, dkn_o, dkv_s):
        gm = gm_ref[...]
        dkn = jnp.zeros((1, LANES), F32)
        for c in range(4):
            sl = slice(c * LANES, (c + 1) * LANES)
            dx, dg = _head_norm_bwd(dmk_ref[:, sl], kv_ref[:, sl], gm, kn_ref[...])
            dkv_s[:, sl] = dx.astype(BF16)
            dkn = dkn + dg
        dkn_o[...] = dkn
        dkv_s[:, 512:] = dmv_ref[...].astype(BF16)
        dkv = dkv_s[...]
        dw_o[...] = _dot(memn_ref[...], dkv, TN)
        dmemn = _dot(dkv, w_ref[...], NT)
        xhat, _ = _rms_rows(mem_ref[...], None)
        dg_o[...] = jnp.sum(dmemn * xhat, axis=0, keepdims=True)

    vm = pl.BlockSpec(memory_space=pltpu.VMEM)
    return pl.pallas_call(
        body, name="mem_prep_bwd", in_specs=[vm] * 9, out_specs=[vm] * 3,
        out_shape=[jax.ShapeDtypeStruct((D_MODEL, D_MODEL), F32), jax.ShapeDtypeStruct((1, D_MODEL), F32),
                   jax.ShapeDtypeStruct((1, LANES), F32)],
        scratch_shapes=[pltpu.VMEM((N_MEM, D_MODEL), BF16)],
        compiler_params=pltpu.CompilerParams(vmem_limit_bytes=VMEM_MID),
    )(mem, g_mem, memn, kv, w_kv, kn_gain, gm128, dmk, dmv)


SLOT_O = D_MODEL // N_SHARD


def _merge_fwd(proj, b_gate, o3, w3, T, tb):
    def body(gl_ref, bg_ref, oa_ref, of_ref, om_ref, wa_ref, wf_ref, wm_ref, out_ref):
        o_refs = (oa_ref, of_ref, om_ref)
        w_refs = (wa_ref, wf_ref, wm_ref)
        for n in range(N_SHARD):
            acc = jnp.zeros((tb, SLOT_O), F32)
            for b in range(3):
                c0 = b * D_MODEL + n * SLOT_O
                g = jax.nn.sigmoid(gl_ref[:, c0:c0 + SLOT_O] + bg_ref[:, c0:c0 + SLOT_O])
                acc = acc + g * _dot(o_refs[b][...], w_refs[b][n])
            out_ref[:, n * SLOT_O:(n + 1) * SLOT_O] = acc.astype(out_ref.dtype)

    rows = pl.BlockSpec((tb, 512), lambda i: (i, 0))
    wspec = pl.BlockSpec((N_SHARD, 512, SLOT_O), lambda i: (0, 0, 0))
    return pl.pallas_call(
        body, name="merge_fwd", grid=(T // tb,),
        in_specs=[pl.BlockSpec((tb, GATE_W), lambda i: (i, 1)), pl.BlockSpec((1, GATE_W), lambda i: (0, 0)),
                  rows, rows, rows, wspec, wspec, wspec],
        out_specs=pl.BlockSpec((tb, D_MODEL), lambda i: (i, 0)),
        out_shape=jax.ShapeDtypeStruct((T, D_MODEL), BF16),
        compiler_params=_cparams("parallel", vmem=VMEM_BIG),
    )(proj, b_gate, *o3, *w3)


def _merge_bwd(proj, b_gate, o3, w3, dmerged, T, tb):
    heads = (SWA_HEADS, FOX_HEADS, MEM_HEADS)

    def body(gl_ref, bg_ref, oa_ref, of_ref, om_ref, wa_ref, wf_ref, wm_ref, dm_ref,
             dgl_o, doa_o, dof_o, dom_o, dla_o, dlf_o, dlm_o, dwa_o, dwf_o, dwm_o, dbg_o):
        i = pl.program_id(0)
        o_refs = (oa_ref, of_ref, om_ref)
        w_refs = (wa_ref, wf_ref, wm_ref)
        do_refs = (doa_o, dof_o, dom_o)
        dl_refs = (dla_o, dlf_o, dlm_o)
        dw_refs = (dwa_o, dwf_o, dwm_o)

        @pl.when(i == 0)
        def _():
            for r in dw_refs:
                r[...] = jnp.zeros_like(r)
            dbg_o[...] = jnp.zeros_like(dbg_o)

        lane = _lane((tb, LANES))
        for b in range(3):
            ob = o_refs[b][...]
            do = jnp.zeros((tb, 512), F32)
            for n in range(N_SHARD):
                c0 = b * D_MODEL + n * SLOT_O
                g = jax.nn.sigmoid(gl_ref[:, c0:c0 + SLOT_O] + bg_ref[:, c0:c0 + SLOT_O])
                dm = dm_ref[:, n * SLOT_O:(n + 1) * SLOT_O]
                y = _dot(ob, w_refs[b][n])
                dgl = dm * y * g * (1.0 - g)
                dgl_o[:, c0:c0 + SLOT_O] = dgl.astype(dgl_o.dtype)
                dbg_o[:, c0:c0 + SLOT_O] += jnp.sum(dgl, axis=0, keepdims=True)
                dy = (dm * g).astype(BF16)
                do = do + _dot(dy, w_refs[b][n], NT)
                dw_refs[b][n] += _dot(ob, dy, TN)
            do_refs[b][...] = do.astype(BF16)
            prod = do * ob.astype(F32)
            for c in range(4):
                blk = prod[:, c * LANES:(c + 1) * LANES]
                if heads[b] == 8:
                    lo = jnp.sum(jnp.where(lane < 64, blk, 0.0), axis=1, keepdims=True)
                    hi = jnp.sum(jnp.where(lane >= 64, blk, 0.0), axis=1, keepdims=True)
                    if b == 1:
                        aug = jnp.zeros((tb, LANES), F32)
                        for sub, dl in enumerate((lo, hi)):
                            for e, piece in enumerate(_split3(-dl)):
                                aug = jnp.where(lane == AUG_STRIDE * sub + AUG_C + e, piece.astype(F32), aug)
                        dl_refs[b][:, c * LANES:(c + 1) * LANES] = aug.astype(BF16)
                    else:
                        dl_refs[b][2 * c] = jnp.broadcast_to(lo, (tb, LANES))
                        dl_refs[b][2 * c + 1] = jnp.broadcast_to(hi, (tb, LANES))
                else:
                    dl_refs[b][c] = jnp.broadcast_to(jnp.sum(blk, axis=1, keepdims=True), (tb, LANES))

    rows = pl.BlockSpec((tb, 512), lambda i: (i, 0))
    wspec = pl.BlockSpec((N_SHARD, 512, SLOT_O), lambda i: (0, 0, 0))
    stat = lambda h: pl.BlockSpec((h, tb, LANES), lambda i: (0, i, 0))
    return pl.pallas_call(
        body, name="merge_bwd", grid=(T // tb,),
        in_specs=[pl.BlockSpec((tb, GATE_W), lambda i: (i, 1)), pl.BlockSpec((1, GATE_W), lambda i: (0, 0)),
                  rows, rows, rows, wspec, wspec, wspec, pl.BlockSpec((tb, D_MODEL), lambda i: (i, 0))],
        out_specs=[pl.BlockSpec((tb, GATE_W), lambda i: (i, 0)), rows, rows, rows,
                   stat(8), rows, stat(4), wspec, wspec, wspec, pl.BlockSpec((1, GATE_W), lambda i: (0, 0))],
        out_shape=[jax.ShapeDtypeStruct((T, GATE_W), BF16)] + [jax.ShapeDtypeStruct((T, 512), BF16)] * 3
        + [jax.ShapeDtypeStruct((8, T, LANES), F32), jax.ShapeDtypeStruct((T, 512), BF16),
           jax.ShapeDtypeStruct((4, T, LANES), F32)]
        + [jax.ShapeDtypeStruct((N_SHARD, 512, SLOT_O), F32)] * 3 + [jax.ShapeDtypeStruct((1, GATE_W), F32)],
        compiler_params=_cparams("arbitrary", vmem=VMEM_BIG),
    )(proj, b_gate, *o3, *w3, dmerged)


def _local_step(x, h, mem, tgt, small, g_in, w_kv, w_o3, w_out, w_up, w_down, reducer):
    T = x.shape[0]
    tm = min(512, T)
    tile2 = lambda v: jnp.tile(v.reshape(1, -1), (1, LANES // v.size))
    gains = jnp.concatenate([tile2(small["qn_swa"]), tile2(small["kn_swa"]), tile2(small["qn_fox"]),
                             tile2(small["kn_fox"]), tile2(small["qn_mem"]), jnp.zeros((3, LANES), F32)], axis=0)
    kn_mem = small["kn_mem"].reshape(1, LANES)
    bfor = jnp.pad(small["b_forget"].reshape(1, -1), ((0, 0), (0, LANES - FOX_HEADS)))
    gm64 = _group_mean_matrix(64)
    gm128 = _group_mean_matrix(128)
    tb_prep = min(512, T)
    ones = jnp.ones((tb_prep, tb_prep), F32)
    tril = jnp.tril(ones).astype(BF16)
    triu = jnp.triu(ones).astype(BF16)
    bucket = _t5_bucket_matrix()
    g_mix, g_mlp, g_mem = small["g_mix"], small["g_mlp"], small["g_mem"]
    b_gate = small["b_gate"]
    sinks = small["sink_swa"].reshape(-1)

    tl = min(1024, T)
    sq = pl.BlockSpec((tl, D_MODEL), lambda i, j, k: (i, j))
    wc = _w_in_to_segments(g_in)
    (proj,) = _matmul(
        "mm_proj", h, wc, dims=NN, grid=(T // tl, PROJ_W // D_MODEL, 1),
        a_spec=pl.BlockSpec((tl, D_MODEL), lambda i, j, k: (i, 0)),
        b_spec=pl.BlockSpec((D_MODEL, D_MODEL), lambda i, j, k: (0, j)),
        acc_shape=(tl, D_MODEL),
        outs=[(jax.ShapeDtypeStruct((T, PROJ_W), F32), sq)],
        epilogue=_epi_store)
    qa, qf, kf, vf, qm, kad, vad, qf_aug, kf_aug = _prep_fwd(proj, gains, bfor, tril, gm64, gm128, T, tb_prep)
    bias = _swa_bias(small["rel_bias"], bucket)
    o_swa, lse_swa = _swa_fwd(sinks, qa, kad, vad, bias, T)
    o_fox, qf_aug_bwd = _fox_fwd(qf, qf_aug, kf, kf_aug, vf, T, min(FOX_TQ, T), min(FOX_TK, T))
    memn, kv, mk, mv = _mem_prep_fwd(mem, g_mem, w_kv, kn_mem, gm128)
    o_mem, lse_mem = _mem_fwd(qm, mk, mv, T, min(MEM_TQ, T))
    o3 = (o_swa, o_fox, o_mem)
    merged = _merge_fwd(proj, b_gate, o3, w_o3, T, min(512, T))

    def epi_residual(acc, extra_refs, out_refs, ij):
        out_refs[0][...] = extra_refs[0][...] + acc

    row_full = pl.BlockSpec((tm, D_MODEL), lambda i, j, k: (i, 0))
    row_big = pl.BlockSpec((tl, D_MODEL), lambda i, j, k: (i, 0))
    whole = pl.BlockSpec((D_MODEL, D_MODEL), lambda i, j, k: (0, 0))
    (x2,) = _matmul(
        "mm_out", merged, w_out, dims=NN, grid=(T // tl, 1, 1),
        a_spec=row_big, b_spec=whole,
        acc_shape=(tl, D_MODEL), extra=[(x, row_big)],
        outs=[(jax.ShapeDtypeStruct((T, D_MODEL), F32), row_big)], epilogue=epi_residual)
    hm = _rmsnorm("rms_mlp", x2, g_mlp, tm)

    def epi_relu2(acc, extra_refs, out_refs, ij):
        out_refs[0][...] = acc.astype(BF16)
        r = jnp.maximum(acc, 0.0)
        out_refs[1][...] = (r * r).astype(BF16)

    up, u = _matmul(
        "mm_up", hm, w_up, dims=NN, grid=(T // tl, N_SHARD, 1),
        a_spec=row_big, b_spec=pl.BlockSpec((None, D_MODEL, D_MODEL), lambda i, j, k: (j, 0, 0)),
        acc_shape=(tl, D_MODEL),
        outs=[(jax.ShapeDtypeStruct((T, D_FF), BF16), sq), (jax.ShapeDtypeStruct((T, D_FF), BF16), sq)],
        epilogue=epi_relu2)

    def epi_loss(acc, extra_refs, out_refs, ij):
        y = extra_refs[0][...] + acc
        err = y - extra_refs[1][...]
        dyv = err * (1.0 / D_MODEL)
        out_refs[0][...] = dyv
        out_refs[2][...] = dyv.astype(BF16)
        sq = jnp.sum(jnp.sum(err * err, axis=1, keepdims=True), axis=0, keepdims=True)

        @pl.when(ij[0] == 0)
        def _():
            out_refs[1][...] = jnp.zeros_like(out_refs[1])

        out_refs[1][...] += jnp.broadcast_to(sq, out_refs[1].shape)

    kblk = pl.BlockSpec((tl, D_MODEL), lambda i, j, k: (i, k))
    dy, loss_acc, dy_bf = _matmul(
        "mm_down", u, w_down, dims=NN, grid=(T // tl, 1, N_SHARD),
        a_spec=kblk, b_spec=pl.BlockSpec((D_MODEL, D_MODEL), lambda i, j, k: (k, 0)),
        acc_shape=(tl, D_MODEL), extra=[(x2, row_big), (tgt, row_big)],
        outs=[(jax.ShapeDtypeStruct((T, D_MODEL), F32), row_big),
              (jax.ShapeDtypeStruct((8, LANES), F32), pl.BlockSpec((8, LANES), lambda i, j, k: (0, 0))),
              (jax.ShapeDtypeStruct((T, D_MODEL), BF16), row_big)],
        epilogue=epi_loss)
    loss = loss_acc[0, 0] * (0.5 / D_MODEL)

    def epi_dup(acc, extra_refs, out_refs, ij):
        out_refs[0][...] = (acc * (2.0 * jnp.maximum(extra_refs[0][...].astype(F32), 0.0))).astype(BF16)

    (dup,) = _matmul(
        "mm_dup", dy_bf, w_down, dims=NT, grid=(T // tl, N_SHARD, 1),
        a_spec=row_big, b_spec=pl.BlockSpec((D_MODEL, D_MODEL), lambda i, j, k: (j, 0)),
        acc_shape=(tl, D_MODEL), extra=[(up, sq)],
        outs=[(jax.ShapeDtypeStruct((T, D_FF), BF16), sq)], epilogue=epi_dup)

    nkt = T // tl
    t_rows = pl.BlockSpec((tl, D_MODEL), lambda i, j, k: (k, i))
    t_cols = pl.BlockSpec((tl, D_MODEL), lambda i, j, k: (k, j))
    (d_w_down,) = _matmul(
        "mm_dw_down", u, dy_bf, dims=TN, grid=(N_SHARD, 1, nkt),
        a_spec=t_rows, b_spec=t_cols, acc_shape=(D_MODEL, D_MODEL),
        outs=[(jax.ShapeDtypeStruct((D_FF, D_MODEL), F32), pl.BlockSpec((D_MODEL, D_MODEL), lambda i, j, k: (i, 0)))],
        epilogue=_epi_store)
    (d_w_up,) = _matmul(
        "mm_dw_up", hm, dup, dims=TN, grid=(1, N_SHARD, nkt),
        a_spec=t_rows, b_spec=t_cols, acc_shape=(D_MODEL, D_MODEL),
        outs=[(jax.ShapeDtypeStruct((N_SHARD, D_MODEL, D_MODEL), F32),
               pl.BlockSpec((None, D_MODEL, D_MODEL), lambda i, j, k: (j, 0, 0)))],
        epilogue=_epi_store)

    def epi_rms_bwd(acc, extra_refs, out_refs, ij):
        dx, dg = _rmsnorm_bwd_rows(acc, extra_refs[0][...], extra_refs[1][...])
        out_refs[0][...] = dx + extra_refs[2][...]

        @pl.when(ij[0] == 0)
        def _():
            out_refs[1][...] = jnp.zeros_like(out_refs[1])

        out_refs[1][...] += dg

    gain_spec = pl.BlockSpec((1, D_MODEL), lambda i, j, k: (0, 0))
    dx2, d_g_mlp = _matmul(
        "mm_dhm", dup, w_up, dims=NT, grid=(T // tl, 1, N_SHARD),
        a_spec=kblk, b_spec=pl.BlockSpec((None, D_MODEL, D_MODEL), lambda i, j, k: (k, 0, 0)),
        acc_shape=(tl, D_MODEL), extra=[(x2, row_big), (g_mlp, gain_spec), (dy, row_big)],
        outs=[(jax.ShapeDtypeStruct((T, D_MODEL), F32), row_big), (jax.ShapeDtypeStruct((1, D_MODEL), F32), gain_spec)],
        epilogue=epi_rms_bwd)

    (dmerged,) = _matmul(
        "mm_dmerged", dx2, w_out, dims=NT, grid=(T // tl, 1, 1),
        a_spec=row_big, b_spec=whole,
        acc_shape=(tl, D_MODEL), outs=[(jax.ShapeDtypeStruct((T, D_MODEL), F32), row_big)], epilogue=_epi_store)
    (d_w_out,) = _matmul(
        "mm_dw_out", merged, dx2, dims=TN, grid=(1, 1, nkt),
        a_spec=t_rows, b_spec=t_cols, acc_shape=(D_MODEL, D_MODEL),
        outs=[(jax.ShapeDtypeStruct((D_MODEL, D_MODEL), F32), whole)],
        epilogue=_epi_store)
    (dgl, do_swa, do_fox, do_mem, dl_swa, do_fox_aug, dl_mem, d_wo_swa, d_wo_fox, d_wo_mem, d_b_gate) = _merge_bwd(
        proj, b_gate, o3, w_o3, dmerged, T, min(512, T))

    dqm, dmk, dmv = _mem_bwd(qm, mk, mv, do_mem, lse_mem, dl_mem, T, min(MEM_TQ, T))
    d_w_kv, d_g_mem, d_kn_mem = _mem_prep_bwd(mem, g_mem, memn, kv, w_kv, kn_mem, gm128, dmk, dmv)
    do_swa = reducer.early_start({"w_mlp_down": d_w_down, "w_mlp_up": d_w_up, "w_out": d_w_out, "w_mem_kv": d_w_kv,
                                  "w_o_swa": d_wo_swa, "w_o_fox": d_wo_fox, "w_o_mem": d_wo_mem}, do_swa)
    dqa, dkad, dvad, dbias, dsk = _swa_bwd(sinks, qa, kad, vad, bias, do_swa, lse_swa, dl_swa, T)
    dqa, do_fox = reducer.early_send((dqa, do_fox))
    dqf, dqf_aug, dkf, dkf_aug, dvf = _fox_bwd(qf, qf_aug_bwd, kf, kf_aug, vf, do_fox, do_fox_aug, T,
                                               min(FOX_BWD_TQ, T), min(FOX_BWD_TK, T))
    dvf = reducer.early_finish(dvf)
    d_rel = _swa_bias_bwd(dbias, bucket)
    dlo, gacc = _prep_bwd(proj, dqa, dkad, dvad, dqf, dkf, dvf, dqm, dqf_aug, dkf_aug, gains, bfor, triu, gm64, gm128,
                          T, tb_prep)

    def dwc_half(name, dpart):
        (res,) = _matmul(
            name, h, dpart, dims=TN, grid=(1, LO_W // D_MODEL, nkt),
            a_spec=t_rows, b_spec=t_cols, acc_shape=(D_MODEL, D_MODEL),
            outs=[(jax.ShapeDtypeStruct((D_MODEL, LO_W), F32), pl.BlockSpec((D_MODEL, D_MODEL), lambda i, j, k: (0, j)))],
            epilogue=_epi_store)
        return res

    d_wc_lo = dwc_half("mm_dwc_lo", dlo)
    d_wc_gl = dwc_half("mm_dwc_gl", dgl)
    dlo = reducer.late_start({"wc_lo": d_wc_lo, "wc_gl": d_wc_gl}, dlo)
    (dh_lo,) = _matmul(
        "mm_dh_lo", dlo, wc, dims=NT, grid=(T // tl, 1, LO_W // D_MODEL),
        a_spec=kblk, b_spec=pl.BlockSpec((D_MODEL, D_MODEL), lambda i, j, k: (0, k)),
        acc_shape=(tl, D_MODEL), outs=[(jax.ShapeDtypeStruct((T, D_MODEL), F32), row_big)], epilogue=_epi_store)
    dh_lo = reducer.late_send(dh_lo)

    def epi_dx(acc, extra_refs, out_refs, ij):
        dhh = acc + extra_refs[3][...]
        dx, dg = _rmsnorm_bwd_rows(dhh, extra_refs[0][...], extra_refs[1][...])
        out_refs[0][...] = dx + extra_refs[2][...]

        @pl.when(ij[0] == 0)
        def _():
            out_refs[1][...] = jnp.zeros_like(out_refs[1])

        out_refs[1][...] += dg

    grad_x, d_g_mix = _matmul(
        "mm_dh_gl", dgl, wc, dims=NT, grid=(T // tl, 1, GATE_W // D_MODEL),
        a_spec=kblk, b_spec=pl.BlockSpec((D_MODEL, D_MODEL), lambda i, j, k: (0, k + LO_W // D_MODEL)),
        acc_shape=(tl, D_MODEL), extra=[(x, row_big), (g_mix, gain_spec), (dx2, row_big), (dh_lo, row_big)],
        outs=[(jax.ShapeDtypeStruct((T, D_MODEL), F32), row_big), (jax.ShapeDtypeStruct((1, D_MODEL), F32), gain_spec)],
        epilogue=epi_dx, vmem=VMEM_MAX)

    fold64 = lambda row: (row[:64] + row[64:]).reshape(1, 64)
    grads = {
        "g_mix": d_g_mix, "b_gate": d_b_gate, "b_forget": gacc[5, :FOX_HEADS].reshape(1, FOX_HEADS),
        "qn_swa": fold64(gacc[0]), "kn_swa": fold64(gacc[1]),
        "sink_swa": -dsk[:, :SWA_GROUP, 0].reshape(1, SWA_HEADS), "rel_bias": d_rel[:, :SWA_HEADS],
        "qn_fox": fold64(gacc[2]), "kn_fox": fold64(gacc[3]),
        "g_mem": d_g_mem, "qn_mem": gacc[4].reshape(1, LANES), "kn_mem": d_kn_mem, "g_mlp": d_g_mlp,
    }
    return loss, grad_x, grads


MESH = pl.DeviceIdType.MESH
ANY = pl.BlockSpec(memory_space=pl.ANY)


def _place():
    x, y, c = lax.axis_index("x"), lax.axis_index("y"), lax.axis_index("c")
    chips = [(1 - x, y), (x, 1 - y), (1 - x, 1 - y)]
    return x, y, c, chips


def _handshake(peers):
    barrier = pltpu.get_barrier_semaphore()
    for peer in peers:
        pl.semaphore_signal(barrier, inc=1, device_id=peer, device_id_type=MESH)
    pl.semaphore_wait(barrier, len(peers))


def _all_gather_shards_async(name, collective_id, slots):
    n = len(slots)
    bufs = [jax.new_ref(s, memory_space=pltpu.MemorySpace.HBM) for s in slots]

    def body(ici_send, ici_recv, d2d_send, d2d_recv):
        x, y, c, chips = _place()
        sibling = (x, y, 1 - c)
        me = 2 * x + y
        _handshake([(px, py, c) for px, py in chips] + [sibling])

        def half(a, who):
            hr = slots[a].shape[1] // 2
            return pl.ds(pl.multiple_of(who * hr, hr), hr)

        def ici(a, j, slot, to):
            return pltpu.make_async_remote_copy(
                src_ref=bufs[a].at[me, half(a, c)], dst_ref=bufs[a].at[slot, half(a, c)],
                send_sem=ici_send.at[3 * a + j], recv_sem=ici_recv.at[3 * a + j], device_id=to, device_id_type=MESH)

        def d2d(a, j, slot, which):
            part = bufs[a].at[slot, half(a, which)]
            return pltpu.make_async_remote_copy(
                src_ref=part, dst_ref=part, send_sem=d2d_send.at[3 * a + j], recv_sem=d2d_recv.at[3 * a + j],
                device_id=sibling, device_id_type=MESH)

        sends = [ici(a, j, me, (*chip, c)) for a in range(n) for j, chip in enumerate(chips)]
        for cp in sends:
            cp.start()
        passed = []
        for a in range(n):
            for j, (px, py) in enumerate(chips):
                ici(a, j, 2 * px + py, (px, py, c)).wait_recv()
                cp = d2d(a, j, 2 * px + py, c)
                cp.start()
                passed.append(cp)
        for a in range(n):
            for j, (px, py) in enumerate(chips):
                d2d(a, j, 2 * px + py, 1 - c).wait_recv()
        for cp in sends + passed:
            cp.wait_send()

    pl.kernel(
        body, mesh=plsc.ScalarSubcoreMesh(axis_name="seq", num_cores=1), name=name,
        scratch_types=[pltpu.SemaphoreType.DMA((3 * n,))] * 4,
        compiler_params=pltpu.CompilerParams(collective_id=collective_id),
    )()
    return [b[...] for b in bufs]


def _sequencer_call(name, collective_id, n_sems, body):
    pl.kernel(
        body, mesh=plsc.ScalarSubcoreMesh(axis_name="seq", num_cores=1), name=name,
        scratch_types=[pltpu.SemaphoreType.DMA((n_sems,))] * 2,
        compiler_params=pltpu.CompilerParams(collective_id=collective_id),
    )()


def _hbm_ref(value):
    return jax.new_ref(value, memory_space=pltpu.MemorySpace.HBM)


def _pair_exchange(name, collective_id, gs):
    n = len(gs)
    src = [_hbm_ref(g) for g in gs]
    stage = [jax.empty_ref(jax.ShapeDtypeStruct((N_SHARD, g.shape[1] // 2, g.shape[2]), g.dtype),
                           memory_space=pltpu.MemorySpace.HBM) for g in gs]

    def body(send_sem, recv_sem):
        x, y, c, _ = _place()
        sibling = (x, y, 1 - c)
        _handshake([sibling])
        copies = []
        for a in range(n):
            hr = gs[a].shape[1] // 2
            theirs = pl.ds(pl.multiple_of((1 - c) * hr, hr), hr)
            copies.append(pltpu.make_async_remote_copy(
                src_ref=src[a].at[:, theirs, :], dst_ref=stage[a], send_sem=send_sem.at[a], recv_sem=recv_sem.at[a],
                device_id=sibling, device_id_type=MESH))
        for cp in copies:
            cp.start()
        for cp in copies:
            cp.wait()

    _sequencer_call(name, collective_id, n, body)
    return [s[...] for s in stage]


def _chip_exchange(name, collective_id, sums):
    n = len(sums)
    src = [_hbm_ref(s) for s in sums]
    got = [jax.empty_ref(jax.ShapeDtypeStruct((3,) + s.shape[1:], s.dtype), memory_space=pltpu.MemorySpace.HBM)
           for s in sums]

    def body(send_sem, recv_sem):
        x, y, c, chips = _place()
        _handshake([(px, py, c) for px, py in chips])
        copies = []
        for a in range(n):
            for j, (px, py) in enumerate(chips):
                copies.append(pltpu.make_async_remote_copy(
                    src_ref=src[a].at[2 * px + py], dst_ref=got[a].at[j],
                    send_sem=send_sem.at[3 * a + j], recv_sem=recv_sem.at[3 * a + j],
                    device_id=(px, py, c), device_id_type=MESH))
        for cp in copies:
            cp.start()
        for cp in copies:
            cp.wait()

    _sequencer_call(name, collective_id, 3 * n, body)
    return [g[...] for g in got]


def _pair_gather(name, collective_id, fulls):
    n = len(fulls)
    full = [_hbm_ref(f) for f in fulls]

    def body(send_sem, recv_sem):
        x, y, c, _ = _place()
        sibling = (x, y, 1 - c)
        _handshake([sibling])
        copies = []
        for a in range(n):
            hr = fulls[a].shape[0] // 2
            mine = full[a].at[pl.ds(pl.multiple_of(c * hr, hr), hr)]
            copies.append(pltpu.make_async_remote_copy(
                src_ref=mine, dst_ref=mine, send_sem=send_sem.at[a], recv_sem=recv_sem.at[a],
                device_id=sibling, device_id_type=MESH))
        for cp in copies:
            cp.start()
        for cp in copies:
            cp.wait()

    _sequencer_call(name, collective_id, n, body)
    return [f[...] for f in full]


ELEMENTWISE_BLOCK_ELEMS = 256 * 1024


def _row_block(rows, cols):
    rb = 8
    while rb * 2 * cols <= ELEMENTWISE_BLOCK_ELEMS and rb * 2 <= rows:
        rb *= 2
    return rb


def _pair_sum(name, place, g, stage):
    _, R, C = g.shape
    hr = R // 2
    rb = _row_block(hr, C)
    nb = hr // rb

    def body(place_ref, g_ref, st_ref, sum_bf, own_f32):
        s = pl.program_id(1)
        tot = g_ref[...] + st_ref[...]
        sum_bf[...] = tot.astype(BF16)

        @pl.when(s == place_ref[0])
        def _():
            own_f32[...] = tot

    return pl.pallas_call(
        body, name=name,
        grid_spec=pltpu.PrefetchScalarGridSpec(
            num_scalar_prefetch=1, grid=(nb, N_SHARD),
            in_specs=[pl.BlockSpec((None, rb, C), lambda i, s, pr: (s, pr[1] * nb + i, 0)),
                      pl.BlockSpec((None, rb, C), lambda i, s, pr: (s, i, 0))],
            out_specs=[pl.BlockSpec((None, rb, C), lambda i, s, pr: (s, i, 0)),
                       pl.BlockSpec((rb, C), lambda i, s, pr: (i, 0))]),
        out_shape=[jax.ShapeDtypeStruct((N_SHARD, hr, C), BF16), jax.ShapeDtypeStruct((hr, C), F32)],
        compiler_params=_cparams("arbitrary", "arbitrary"),
    )(place, g, stage)


def _final_sum(name, place, own, got):
    hr, C = own.shape
    rb = _row_block(hr, C)
    nb = hr // rb

    def body(place_ref, own_ref, got_ref, o_ref):
        o_ref[...] = ((own_ref[...] + got_ref[0].astype(F32)) + got_ref[1].astype(F32)) + got_ref[2].astype(F32)

    return pl.pallas_call(
        body, name=name,
        grid_spec=pltpu.PrefetchScalarGridSpec(
            num_scalar_prefetch=1, grid=(nb,),
            in_specs=[pl.BlockSpec((rb, C), lambda i, pr: (i, 0)), pl.BlockSpec((3, rb, C), lambda i, pr: (0, i, 0))],
            out_specs=pl.BlockSpec((rb, C), lambda i, pr: (pr[1] * nb + i, 0))),
        out_shape=jax.ShapeDtypeStruct((2 * hr, C), F32),
        compiler_params=_cparams("arbitrary"),
    )(place, own, got)


def _adamw_math(w, g, m, v):
    m = ADAM_B1 * m + (1.0 - ADAM_B1) * g
    v = ADAM_B2 * v + (1.0 - ADAM_B2) * (g * g)
    m_hat = m / (1.0 - ADAM_B1 ** ADAM_STEP)
    v_hat = v / (1.0 - ADAM_B2 ** ADAM_STEP)
    delta = -ADAM_LR * (m_hat / (jnp.sqrt(v_hat) + ADAM_EPS) + ADAM_WD * w)
    return delta, m, v


def _adamw(name, w, g, m, v):
    R, Cw = w.shape
    Cg = g.shape[1]
    rb = _row_block(R, Cg)

    def body(w_ref, g_ref, m_ref, v_ref, g_o, d_o, m_o, v_o):
        gv = g_ref[...]
        delta, mn, vn = _adamw_math(w_ref[...], gv, m_ref[...], v_ref[...])
        g_o[...] = gv
        d_o[...] = delta
        m_o[...] = mn
        v_o[...] = vn

    blk = pl.BlockSpec((rb, Cg), lambda i: (i, 0))
    return pl.pallas_call(
        body, name=name, grid=(R // rb,),
        in_specs=[blk] * 4, out_specs=[blk] * 4,
        out_shape=[jax.ShapeDtypeStruct((R, Cw), F32)] * 4,
        compiler_params=_cparams("parallel"),
    )(w, g, m, v)


N_DEV = 8
SMALL_ROWS = 64


def _small_allreduce_adamw(g, w, m, v):
    def body(g_ref, w_ref, m_ref, v_ref, all_ref, gs_o, d_o, m_o, v_o, send_sems, recv_sems, local_sem):
        x, y, c, chips = _place()
        me, sibling = (x, y, c), (x, y, 1 - c)

        def rows(px, py, pc):
            return all_ref.at[pl.ds(pl.multiple_of((4 * px + 2 * py + pc) * SMALL_ROWS, SMALL_ROWS), SMALL_ROWS), :]

        def copy(k, block, to, src=None):
            return pltpu.make_async_remote_copy(
                src_ref=rows(*block) if src is None else src, dst_ref=rows(*block),
                send_sem=send_sems.at[k], recv_sem=recv_sems.at[k], device_id=to, device_id_type=MESH)

        mine = pltpu.make_async_copy(g_ref, rows(*me), local_sem)
        mine.start()
        first = [copy(0, me, sibling, src=g_ref)]
        first += [copy(1 + j, me, (*chip, c), src=g_ref) for j, chip in enumerate(chips)]
        for cp in first:
            cp.start()
        passed = [copy(4 + j, (*chip, c), sibling) for j, chip in enumerate(chips)]
        for j, chip in enumerate(chips):
            copy(1 + j, (*chip, c), me).wait_recv()
            passed[j].start()
        copy(0, sibling, me).wait_recv()
        for j, chip in enumerate(chips):
            copy(4 + j, (*chip, 1 - c), me).wait_recv()
        for cp in first + passed:
            cp.wait_send()
        mine.wait()

        tot = all_ref[0:SMALL_ROWS, :]
        for d in range(1, N_DEV):
            tot = tot + all_ref[d * SMALL_ROWS:(d + 1) * SMALL_ROWS, :]
        delta, mn, vn = _adamw_math(w_ref[...], tot, m_ref[...], v_ref[...])
        gs_o[...] = tot
        d_o[...] = delta
        m_o[...] = mn
        v_o[...] = vn

    vm = pl.BlockSpec(memory_space=pltpu.VMEM)
    shp = jax.ShapeDtypeStruct((SMALL_ROWS, LANES), F32)
    res = pl.pallas_call(
        body, name="small_allreduce_adamw", in_specs=[vm] * 4, out_specs=[vm] * 5,
        out_shape=[jax.ShapeDtypeStruct((N_DEV * SMALL_ROWS, LANES), F32), shp, shp, shp, shp],
        scratch_shapes=[pltpu.SemaphoreType.DMA((7,)), pltpu.SemaphoreType.DMA((7,)), pltpu.SemaphoreType.DMA],
    )(g, w, m, v)
    return res[1:]


SMALL_NAMES = ("g_mix", "b_gate", "b_forget", "qn_swa", "kn_swa", "sink_swa", "rel_bias", "qn_fox", "kn_fox",
               "g_mem", "qn_mem", "kn_mem", "g_mlp")
BIG_NAMES = ("w_in", "w_mem_kv", "w_o_swa", "w_o_fox", "w_o_mem", "w_out", "w_mlp_up", "w_mlp_down")
WEIGHT_NAMES = ("g_mix", "w_in", "b_gate", "b_forget", "qn_swa", "kn_swa", "sink_swa", "rel_bias", "qn_fox", "kn_fox",
                "g_mem", "w_mem_kv", "qn_mem", "kn_mem", "w_o_swa", "w_o_fox", "w_o_mem", "w_out", "g_mlp",
                "w_mlp_up", "w_mlp_down")


def _pack_small(parts, extra=None):
    rows = []
    for n in SMALL_NAMES:
        flat = parts[n].reshape(-1).astype(F32)
        flat = jnp.pad(flat, (0, (-flat.size) % LANES))
        rows.append(flat.reshape(-1, LANES))
    if extra is not None:
        rows.append(jnp.pad(extra.reshape(1, 1), ((0, 0), (0, LANES - 1))))
    packed = jnp.concatenate(rows, axis=0)
    return jnp.pad(packed, ((0, SMALL_ROWS - packed.shape[0]), (0, 0)))


def _unpack_small(packed, shapes):
    out, r = {}, 0
    for n in SMALL_NAMES:
        size = math.prod(shapes[n])
        nr = -(-size // LANES)
        out[n] = packed[r:r + nr].reshape(-1)[:size].reshape(shapes[n])
        r += nr
    return out, packed[r, 0]


W_IN_SEGMENTS = ((C_QA, 0, 512), (C_QF, 768, 512), (C_KF, 1280, 512), (C_VF, 1792, 512), (C_QM, 2312, 512),
                 (C_KA, 512, 128), (C_VA, 640, 128), (C_FL, 2304, FOX_HEADS), (C_GL, 2824, GATE_W))
RELAYOUT_ROWS = 256


def _permute_pieces(src_of_dst):
    blocks = []
    for b in range(len(src_of_dst) // LANES):
        runs, lane = [], 0
        while lane < LANES:
            src = src_of_dst[b * LANES + lane]
            if src is None:
                lane += 1
                continue
            plane, col = src
            end = lane + 1
            while (end < LANES and src_of_dst[b * LANES + end] == (plane, col + end - lane)
                   and (col + end - lane) // LANES == col // LANES):
                end += 1
            runs.append((plane, col // LANES, (lane - col) % LANES, lane, end))
            lane = end
        blocks.append(runs)
    return blocks


def _permuted_block(runs, load, rows):
    lane = _lane((rows, LANES))
    acc = jnp.zeros((rows, LANES), F32)
    for plane, blk, shift, lo, hi in runs:
        x = load(plane, blk).astype(F32)
        if shift:
            x = pltpu.roll(x, shift, 1)
        acc = x if (lo, hi) == (0, LANES) else jnp.where((lane >= lo) & (lane < hi), x, acc)
    return acc


def _w_in_to_segments(g_in):
    src_of_dst = [None] * PROJ_W
    for mine, theirs, width in W_IN_SEGMENTS:
        for k in range(width):
            src_of_dst[mine + k] = ((theirs + k) // IN_SHARD, (theirs + k) % IN_SHARD)
    blocks = _permute_pieces(src_of_dst)
    rb = RELAYOUT_ROWS

    def body(src_ref, out_ref):
        for b, runs in enumerate(blocks):
            blk = _permuted_block(runs, lambda p, c: src_ref[p, :, c * LANES:(c + 1) * LANES], rb)
            out_ref[:, b * LANES:(b + 1) * LANES] = blk.astype(out_ref.dtype)

    return pl.pallas_call(
        body, name="w_in_to_segments", grid=(D_MODEL // rb,),
        in_specs=[pl.BlockSpec((N_SHARD, rb, IN_SHARD_PAD), lambda i: (0, i, 0))],
        out_specs=pl.BlockSpec((rb, PROJ_W), lambda i: (i, 0)),
        out_shape=jax.ShapeDtypeStruct((D_MODEL, PROJ_W), g_in.dtype),
        compiler_params=_cparams("parallel", vmem=VMEM_MID),
    )(g_in)


def _w_in_from_segments(lo, gl):
    mine_of_theirs = {}
    for mine, theirs, width in W_IN_SEGMENTS:
        for k in range(width):
            mine_of_theirs[theirs + k] = mine + k
    src_of_dst = [None] * (N_SHARD * IN_SHARD_PAD)
    for s in range(N_SHARD):
        for l in range(IN_SHARD):
            j = mine_of_theirs[s * IN_SHARD + l]
            src_of_dst[s * IN_SHARD_PAD + l] = (j // LO_W, j % LO_W)
    blocks = _permute_pieces(src_of_dst)
    per_slot = IN_SHARD_PAD // LANES
    rb = RELAYOUT_ROWS

    def body(lo_ref, gl_ref, out_ref):
        planes = (lo_ref, gl_ref)
        for b, runs in enumerate(blocks):
            blk = _permuted_block(runs, lambda p, c: planes[p][:, c * LANES:(c + 1) * LANES], rb)
            c0 = (b % per_slot) * LANES
            out_ref[b // per_slot, :, c0:c0 + LANES] = blk

    half = pl.BlockSpec((rb, LO_W), lambda i: (i, 0))
    return pl.pallas_call(
        body, name="w_in_from_segments", grid=(D_MODEL // rb,),
        in_specs=[half, half],
        out_specs=pl.BlockSpec((N_SHARD, rb, IN_SHARD_PAD), lambda i: (0, i, 0)),
        out_shape=jax.ShapeDtypeStruct((N_SHARD, D_MODEL, IN_SHARD_PAD), F32),
        compiler_params=_cparams("parallel", vmem=VMEM_MID),
    )(lo, gl)


def _after(first, then):
    return lax.optimization_barrier((first, then))


class _ReduceGroup:
    def __init__(self, tag, first_collective_id, place):
        self.tag, self.first_id, self.place = tag, first_collective_id, place

    def start(self, local, tie):
        self.names = tuple(local)
        mine, tie = _after([local[n] for n in self.names], tie)
        self.mine = mine
        self.staged = _pair_exchange("pair_exchange_" + self.tag, self.first_id, mine)
        return tie

    def send(self, tie):
        staged, tie = _after(self.staged, tie)
        sums = [_pair_sum("pair_sum_" + n, self.place, g, st) for n, g, st in zip(self.names, self.mine, staged)]
        travel, tie = _after([s[0] for s in sums], tie)
        self.own = [s[1] for s in sums]
        self.got = _chip_exchange("chip_exchange_" + self.tag, self.first_id + 1, travel)
        return tie

    def finish(self, tie):
        got, tie = _after(self.got, tie)
        halves = [_final_sum("final_sum_" + n, self.place, o, r) for n, o, r in zip(self.names, self.own, got)]
        halves, tie = _after(halves, tie)
        summed = _pair_gather("pair_gather_" + self.tag, self.first_id + 2, halves)
        self.summed = dict(zip(self.names, summed))
        return tie


class _GradReducer:
    def __init__(self, place):
        self.early = _ReduceGroup("early", 2, place)
        self.late = _ReduceGroup("late", 5, place)

    @staticmethod
    def _slot_rows(a):
        return a.reshape(N_SHARD, a.shape[0] // N_SHARD, a.shape[1])

    def early_start(self, g, tie):
        return self.early.start({"w_mlp_down": self._slot_rows(g["w_mlp_down"]), "w_mlp_up": g["w_mlp_up"],
                                 "w_out": self._slot_rows(g["w_out"]), "w_mem_kv": self._slot_rows(g["w_mem_kv"]),
                                 "w_o_swa": g["w_o_swa"], "w_o_fox": g["w_o_fox"], "w_o_mem": g["w_o_mem"]}, tie)

    def early_send(self, tie):
        return self.early.send(tie)

    def early_finish(self, tie):
        return self.early.finish(tie)

    def late_start(self, g, tie):
        d_in = _w_in_from_segments(g["wc_lo"], g["wc_gl"])
        return self.late.start({"w_in": d_in}, tie)

    def late_send(self, tie):
        return self.late.send(tie)

    def late_finish(self, tie):
        return self.late.finish(tie)

    @property
    def summed(self):
        return {**self.early.summed, **self.late.summed}


def kernel(x, mem, g_mix, w_in, b_gate, b_forget, qn_swa, kn_swa, sink_swa, rel_bias, qn_fox, kn_fox, g_mem, w_mem_kv, qn_mem, kn_mem, w_o_swa, w_o_fox, w_o_mem, w_out, g_mlp, w_mlp_up, w_mlp_down, loss_target, m_g_mix, m_w_in, m_b_gate, m_b_forget, m_qn_swa, m_kn_swa, m_sink_swa, m_rel_bias, m_qn_fox, m_kn_fox, m_g_mem, m_w_mem_kv, m_qn_mem, m_kn_mem, m_w_o_swa, m_w_o_fox, m_w_o_mem, m_w_out, m_g_mlp, m_w_mlp_up, m_w_mlp_down, v_g_mix, v_w_in, v_b_gate, v_b_forget, v_qn_swa, v_kn_swa, v_sink_swa, v_rel_bias, v_qn_fox, v_kn_fox, v_g_mem, v_w_mem_kv, v_qn_mem, v_kn_mem, v_w_o_swa, v_w_o_fox, v_w_o_mem, v_w_out, v_g_mlp, v_w_mlp_up, v_w_mlp_down):
    given = dict(locals())
    W = {n: given[n] for n in WEIGHT_NAMES}
    M = {n: given["m_" + n] for n in WEIGHT_NAMES}
    V = {n: given["v_" + n] for n in WEIGHT_NAMES}
    pad_in = ((0, 0), (0, IN_SHARD_PAD - IN_SHARD))

    shards = [jnp.pad(w_in[0].astype(BF16), pad_in)] + [W[n][0].astype(BF16) for n in BIG_NAMES[1:]]
    slots = [jnp.broadcast_to(s[None], (N_SHARD,) + s.shape) for s in shards]
    (g_in,) = _all_gather_shards_async("all_gather_w_in", 1, slots[:1])
    small = {n: (W[n] if n == "rel_bias" else W[n].reshape(1, -1)) for n in SMALL_NAMES}
    h = _rmsnorm("rms_mix", x[0], small["g_mix"], min(512, x.shape[1]))
    g_in, late, h, (m_in, v_in) = lax.optimization_barrier((g_in, slots[1:], h, (M["w_in"][0], V["w_in"][0])))
    M["w_in"], V["w_in"] = m_in[None], v_in[None]
    g_kv, g_oa, g_of, g_om, g_out, g_up, g_down = _all_gather_shards_async("all_gather_weights_async", 8, late)

    place = jnp.stack([2 * lax.axis_index("x") + lax.axis_index("y"), lax.axis_index("c")]).astype(jnp.int32)
    reducer = _GradReducer(place)
    loss, grad_x, grads = _local_step(
        x[0], h, mem[0], loss_target[0], small, g_in, g_kv.reshape(D_MODEL, D_MODEL), (g_oa, g_of, g_om),
        g_out.reshape(D_MODEL, D_MODEL), g_up, g_down.reshape(D_FF, D_MODEL), reducer)

    out = {}

    def adamw_of(names, summed):
        for n in names:
            res = _adamw("adamw_" + n, W[n][0], summed[n], M[n][0], V[n][0])
            out[n] = [r.reshape(W[n].shape) for r in res]

    adamw_of(reducer.early.names, reducer.early.summed)
    shapes = {n: W[n].shape for n in SMALL_NAMES}
    packed = _small_allreduce_adamw(_pack_small(grads, loss), _pack_small(W), _pack_small(M), _pack_small(V))
    done_meanwhile = ([out[n] for n in reducer.early.names], packed)
    (early_out, packed), grad_x = reducer.late_finish((done_meanwhile, grad_x))
    for n, res in zip(reducer.early.names, early_out):
        out[n] = res
    adamw_of(reducer.late.names, reducer.late.summed)
    unpacked = [_unpack_small(p, shapes) for p in packed]
    for n in SMALL_NAMES:
        out[n] = [u[0][n] for u in unpacked]
    loss_total = unpacked[0][1]

    return (loss_total, grad_x.reshape(x.shape),
            *[out[n][0] for n in WEIGHT_NAMES], *[out[n][1] for n in WEIGHT_NAMES],
            *[out[n][2] for n in WEIGHT_NAMES], *[out[n][3] for n in WEIGHT_NAMES])
```

```python
import functools
import math

import jax
import jax.numpy as jnp
from jax import lax
from jax.experimental import pallas as pl
from jax.experimental.pallas import tpu as pltpu
from jax.experimental.pallas import tpu_sc as plsc

F32 = jnp.float32
BF16 = jnp.bfloat16

D_MODEL = 1024
N_MEM = 256
SWA_HEADS = 8
SWA_KV_HEADS = 2
SWA_HEAD_DIM = 64
WINDOW = 128
FOX_HEADS = 8
FOX_HEAD_DIM = 64
MEM_HEADS = 4
MEM_HEAD_DIM = 128
D_FF = 4 * D_MODEL
REL_BUCKETS = 32
REL_MAX_DIST = 128
EPS = 1e-6
NEG = -1e30
GATE_W = 3 * D_MODEL
IN_WIDTH = 5896
N_SHARD = 4
IN_SHARD = IN_WIDTH // N_SHARD
IN_SHARD_PAD = 1536

ADAM_LR = 0.001
ADAM_B1 = 0.9
ADAM_B2 = 0.999
ADAM_EPS = 1e-08
ADAM_WD = 0.01
ADAM_STEP = 10

LANES = 128
V7X_VMEM_BYTES = 64 * 1024 * 1024
MIB = 1024 * 1024
VMEM_SMALL, VMEM_MID, VMEM_BIG, VMEM_MAX = 24 * MIB, 40 * MIB, 48 * MIB, 56 * MIB

C_QA, C_QF, C_KF, C_VF, C_QM, C_KA, C_VA, C_FL, C_GL = 0, 512, 1024, 1536, 2048, 2560, 2688, 2816, 3072
LO_W = 3072
PROJ_W = 6144

NN = (((1,), (0,)), ((), ()))
NT = (((1,), (1,)), ((), ()))
TN = (((0,), (0,)), ((), ()))


def _dot(a, b, dims=NN):
    return lax.dot_general(a, b, dims, preferred_element_type=F32)


def _cparams(*sem, vmem=VMEM_SMALL):
    return pltpu.CompilerParams(dimension_semantics=sem, vmem_limit_bytes=vmem)


def _split3(a):
    hi = a.astype(BF16)
    r1 = a - hi.astype(F32)
    mid = r1.astype(BF16)
    lo = (r1 - mid.astype(F32)).astype(BF16)
    return hi, mid, lo


def _group_mean(a, g2):
    hi = a.astype(BF16)
    mid = (a - hi.astype(F32)).astype(BF16)
    return _dot(jnp.concatenate([hi, mid], axis=1), g2)


def _dot3_left(g, a):
    hi, mid, lo = _split3(a)
    return _dot(g, hi) + _dot(g, mid) + _dot(g, lo)


def _group_mean_matrix(d):
    r = jnp.arange(LANES)
    g = jnp.where((r[:, None] // d) == (r[None, :] // d), 1.0 / d, 0.0).astype(BF16)
    return jnp.concatenate([g, g], axis=0)


def _lane(shape):
    return lax.broadcasted_iota(jnp.int32, shape, len(shape) - 1)


def _matmul(name, a, b, *, dims, grid, a_spec, b_spec, acc_shape, outs, epilogue, extra=(), vmem=VMEM_BIG):
    nk = grid[2]
    n_extra = len(extra)

    def body(a_ref, b_ref, *rest):
        extra_refs = rest[:n_extra]
        out_refs = rest[n_extra:n_extra + len(outs)]
        i, j, k = pl.program_id(0), pl.program_id(1), pl.program_id(2)
        if nk == 1:
            epilogue(_dot(a_ref[...].astype(BF16), b_ref[...].astype(BF16), dims), extra_refs, out_refs, (i, j))
            return
        acc_ref = rest[-1]

        @pl.when(k == 0)
        def _():
            acc_ref[...] = jnp.zeros_like(acc_ref)

        acc_ref[...] += _dot(a_ref[...].astype(BF16), b_ref[...].astype(BF16), dims)

        @pl.when(k == nk - 1)
        def _():
            epilogue(acc_ref[...], extra_refs, out_refs, (i, j))

    res = pl.pallas_call(
        body,
        name=name,
        grid=grid,
        in_specs=[a_spec, b_spec] + [s for _, s in extra],
        out_specs=[s for _, s in outs],
        out_shape=[s for s, _ in outs],
        scratch_shapes=[pltpu.VMEM(acc_shape, F32)] if nk > 1 else [],
        compiler_params=_cparams("arbitrary", "arbitrary", "arbitrary", vmem=vmem),
    )(a, b, *[x for x, _ in extra])
    return res


def _epi_store(acc, extra_refs, out_refs, ij):
    out_refs[0][...] = acc.astype(out_refs[0].dtype)


def _rms_rows(x, g):
    r = lax.rsqrt(jnp.mean(x * x, axis=-1, keepdims=True) + EPS)
    return x * r, r


def _rmsnorm_bwd_rows(dh, x, g):
    xhat, r = _rms_rows(x, g)
    dxh = dh * g
    dx = r * (dxh - xhat * jnp.mean(dxh * xhat, axis=-1, keepdims=True))
    return dx, jnp.sum(dh * xhat, axis=0, keepdims=True)


def _rmsnorm(name, x, g, tb):
    T, Dm = x.shape

    def body(x_ref, g_ref, o_ref):
        xhat, _ = _rms_rows(x_ref[...], None)
        o_ref[...] = (xhat * g_ref[...]).astype(o_ref.dtype)

    return pl.pallas_call(
        body, name=name, grid=(T // tb,),
        in_specs=[pl.BlockSpec((tb, Dm), lambda i: (i, 0)), pl.BlockSpec((1, Dm), lambda i: (0, 0))],
        out_specs=pl.BlockSpec((tb, Dm), lambda i: (i, 0)),
        out_shape=jax.ShapeDtypeStruct((T, Dm), BF16),
        compiler_params=_cparams("parallel"),
    )(x, g)


def _head_norm(x, gm, gain):
    ms = _group_mean(x * x, gm)
    r = lax.rsqrt(ms + EPS)
    return x * r * gain, x * r


def _head_norm_bwd(dy, x, gm, gain):
    ms = _group_mean(x * x, gm)
    r = lax.rsqrt(ms + EPS)
    xhat = x * r
    dxh = dy * gain
    dx = r * (dxh - xhat * _group_mean(dxh * xhat, gm))
    return dx, jnp.sum(dy * xhat, axis=0, keepdims=True)


def _log_sigmoid(z):
    return jnp.minimum(z, 0.0) - jnp.log(1.0 + jnp.exp(-jnp.abs(z)))


def _prep_fwd(proj, gains, bfor, tril, gm64, gm128, T, tb):
    nb = T // tb

    def body(qa_ref, qf_ref, kf_ref, vf_ref, qm_ref, ka_ref, va_ref, fl_ref, gains_ref, bfor_ref, tril_ref,
             gm64_ref, gm128_ref,
             qa_o, qf_o, kf_o, vf_o, qm_o, kad_o, vad_o, qaug_o, kaug_o, carry):
        i = pl.program_id(0)
        gm64v = gm64_ref[...]
        gm128v = gm128_ref[...]
        lane = _lane((tb, LANES))

        def norm512(src, dst, row, gm, scale=1.0):
            gain = gains_ref[row:row + 1, :]
            for c in range(4):
                sl = slice(c * LANES, (c + 1) * LANES)
                y, _ = _head_norm(src[:, sl], gm, gain)
                dst[:, sl] = (y * scale).astype(dst.dtype)

        norm512(qa_ref, qa_o, 0, gm64v)
        norm512(qf_ref, qf_o, 2, gm64v, FOX_SCALE)
        norm512(kf_ref, kf_o, 3, gm64v)
        norm512(qm_ref, qm_o, 4, gm128v)
        vf_o[...] = vf_ref[...].astype(vf_o.dtype)

        ka_n, _ = _head_norm(ka_ref[...], gm64v, gains_ref[1:2, :])
        ka_r = pltpu.roll(ka_n, 64, 1)
        va = va_ref[...]
        va_r = pltpu.roll(va, 64, 1)
        lo = lane < 64
        kad_o[0] = jnp.where(lo, ka_n, ka_r).astype(kad_o.dtype)
        kad_o[1] = jnp.where(lo, ka_r, ka_n).astype(kad_o.dtype)
        vad_o[0] = jnp.where(lo, va, va_r).astype(vad_o.dtype)
        vad_o[1] = jnp.where(lo, va_r, va).astype(vad_o.dtype)

        @pl.when(i == 0)
        def _():
            carry[...] = jnp.zeros_like(carry)

        logf = jnp.where(lane < FOX_HEADS, _log_sigmoid(fl_ref[...] + bfor_ref[...]), 0.0)
        c = _dot3_left(tril_ref[...], logf) + carry[0:1, :]
        carry[...] = jnp.broadcast_to(c[tb - 1:tb, :], carry.shape)
        for pair in range(FOX_HEADS // 2):
            qaug = jnp.zeros((tb, LANES), F32)
            kaug = jnp.zeros((tb, LANES), F32)
            for sub in range(2):
                col = jnp.sum(jnp.where(lane == 2 * pair + sub, c, 0.0), axis=1, keepdims=True)
                pieces = [p.astype(F32) for p in _split3(col)]
                base = AUG_STRIDE * sub
                for e in range(3):
                    qaug = jnp.where(lane == base + AUG_C + e, pieces[e], qaug)
                    kaug = jnp.where(lane == base + AUG_NEG_C + e, -pieces[e], kaug)
                qaug = jnp.where((lane >= base + AUG_NEG_C) & (lane < base + AUG_NEG_C + 3), 1.0, qaug)
                ones_k = ((lane >= base + AUG_C) & (lane < base + AUG_C + 3)) | (
                    (lane >= base + AUG_STAT) & (lane < base + AUG_STAT + 3))
                kaug = jnp.where(ones_k, 1.0, kaug)
            sl = slice(pair * LANES, (pair + 1) * LANES)
            qaug_o[:, sl] = qaug.astype(BF16)
            kaug_o[:, sl] = kaug.astype(BF16)

    def seg(width, start):
        return pl.BlockSpec((tb, width), lambda i, s=start // width: (i, s))

    const = lambda shape: pl.BlockSpec(shape, lambda i: tuple(0 for _ in shape))
    rows512 = pl.BlockSpec((tb, 512), lambda i: (i, 0))
    outs = pl.pallas_call(
        body, name="prep_fwd", grid=(nb,),
        in_specs=[seg(512, C_QA), seg(512, C_QF), seg(512, C_KF), seg(512, C_VF), seg(512, C_QM),
                  seg(128, C_KA), seg(128, C_VA), seg(128, C_FL),
                  const((8, LANES)), const((1, LANES)), const((tb, tb)), const((2 * LANES, LANES)), const((2 * LANES, LANES))],
        out_specs=[rows512, rows512, rows512, rows512, rows512,
                   pl.BlockSpec((2, tb, LANES), lambda i: (0, i, 0)), pl.BlockSpec((2, tb, LANES), lambda i: (0, i, 0)),
                   rows512, rows512],
        out_shape=[jax.ShapeDtypeStruct((T, 512), BF16)] * 5
        + [jax.ShapeDtypeStruct((2, T, LANES), BF16)] * 2
        + [jax.ShapeDtypeStruct((T, 512), BF16)] * 2,
        scratch_shapes=[pltpu.VMEM((8, LANES), F32)],
        compiler_params=_cparams("arbitrary", vmem=VMEM_MID),
    )(proj, proj, proj, proj, proj, proj, proj, proj, gains, bfor, tril, gm64, gm128)
    return outs


def _prep_bwd(proj, dqa, dkad, dvad, dqf, dkf, dvf, dqm, dqf_aug, dkf_aug, gains, bfor, triu, gm64, gm128, T, tb):
    nb = T // tb

    def body(qa_ref, qf_ref, kf_ref, qm_ref, ka_ref, fl_ref,
             dqa_ref, dkad_ref, dvad_ref, dqf_ref, dkf_ref, dvf_ref, dqm_ref, dqfa_ref, dkfa_ref,
             gains_ref, bfor_ref, triu_ref, gm64_ref, gm128_ref,
             dlo_o, gacc_o, carry):
        i = pl.program_id(0)
        gm64v = gm64_ref[...]
        gm128v = gm128_ref[...]
        lane = _lane((tb, LANES))

        @pl.when(i == 0)
        def _():
            carry[...] = jnp.zeros_like(carry)
            gacc_o[...] = jnp.zeros_like(gacc_o)

        def norm512_bwd(dsrc, xsrc, col0, row, gm):
            gain = gains_ref[row:row + 1, :]
            gsum = jnp.zeros((1, LANES), F32)
            for c in range(4):
                sl = slice(c * LANES, (c + 1) * LANES)
                dx, dg = _head_norm_bwd(dsrc[:, sl], xsrc[:, sl], gm, gain)
                dlo_o[:, col0 + c * LANES:col0 + (c + 1) * LANES] = dx.astype(dlo_o.dtype)
                gsum = gsum + dg
            gacc_o[row:row + 1, :] += gsum

        norm512_bwd(dqa_ref, qa_ref, C_QA, 0, gm64v)
        norm512_bwd(dqf_ref, qf_ref, C_QF, 2, gm64v)
        norm512_bwd(dkf_ref, kf_ref, C_KF, 3, gm64v)
        norm512_bwd(dqm_ref, qm_ref, C_QM, 4, gm128v)
        dlo_o[:, C_VF:C_VF + 512] = dvf_ref[...].astype(dlo_o.dtype)

        lo = lane < 64

        def fold(ref):
            f0 = ref[0] + pltpu.roll(ref[0], 64, 1)
            f1 = ref[1] + pltpu.roll(ref[1], 64, 1)
            return jnp.where(lo, f0, f1)

        dka, dg = _head_norm_bwd(fold(dkad_ref), ka_ref[...], gm64v, gains_ref[1:2, :])
        gacc_o[1:2, :] += dg
        dlo_o[:, C_KA:C_KA + LANES] = dka.astype(dlo_o.dtype)
        dlo_o[:, C_VA:C_VA + LANES] = fold(dvad_ref).astype(dlo_o.dtype)

        dc = jnp.zeros((tb, LANES), F32)
        for pair in range(FOX_HEADS // 2):
            sl = slice(pair * LANES, (pair + 1) * LANES)
            rows_sum, cols_sum = dqfa_ref[:, sl], dkfa_ref[:, sl]
            for sub in range(2):
                diff = (jnp.where(lane == AUG_STRIDE * sub + AUG_C, rows_sum, 0.0)
                        - jnp.where(lane == AUG_STRIDE * sub + AUG_NEG_C, cols_sum, 0.0))
                dc = jnp.where(lane == 2 * pair + sub, jnp.sum(diff, axis=1, keepdims=True), dc)
        dlogf = _dot3_left(triu_ref[...], dc) + carry[0:1, :]
        carry[...] = jnp.broadcast_to(dlogf[0:1, :], carry.shape)
        z = fl_ref[...] + bfor_ref[...]
        dfl = jnp.where(lane < FOX_HEADS, dlogf / (1.0 + jnp.exp(z)), 0.0)
        gacc_o[5:6, :] += jnp.sum(dfl, axis=0, keepdims=True)
        dlo_o[:, C_FL:C_FL + LANES] = dfl.astype(dlo_o.dtype)
        dlo_o[:, C_FL + LANES:C_FL + 2 * LANES] = jnp.zeros((tb, LANES), dlo_o.dtype)

    rev = lambda i: nb - 1 - i

    def seg(width, start):
        return pl.BlockSpec((tb, width), lambda i, s=start // width: (rev(i), s))

    const = lambda shape: pl.BlockSpec(shape, lambda i: tuple(0 for _ in shape))
    rows512 = pl.BlockSpec((tb, 512), lambda i: (rev(i), 0))
    dup = pl.BlockSpec((2, tb, LANES), lambda i: (0, rev(i), 0))
    return pl.pallas_call(
        body, name="prep_bwd", grid=(nb,),
        in_specs=[seg(512, C_QA), seg(512, C_QF), seg(512, C_KF), seg(512, C_QM), seg(128, C_KA), seg(128, C_FL),
                  rows512, dup, dup, rows512, rows512, rows512, rows512, rows512, rows512,
                  const((8, LANES)), const((1, LANES)), const((tb, tb)), const((2 * LANES, LANES)), const((2 * LANES, LANES))],
        out_specs=[pl.BlockSpec((tb, LO_W), lambda i: (rev(i), 0)), const((8, LANES))],
        out_shape=[jax.ShapeDtypeStruct((T, LO_W), BF16), jax.ShapeDtypeStruct((8, LANES), F32)],
        scratch_shapes=[pltpu.VMEM((8, LANES), F32)],
        compiler_params=_cparams("arbitrary", vmem=VMEM_MID),
    )(proj, proj, proj, proj, proj, proj, dqa, dkad, dvad, dqf, dkf, dvf, dqm, dqf_aug, dkf_aug,
      gains, bfor, triu, gm64, gm128)


FOX_SCALE = FOX_HEAD_DIM ** -0.5
AUG_STRIDE = 16
AUG_C = 0
AUG_NEG_C = 3
AUG_STAT = 6
FOX_TQ, FOX_TK = 1024, 1024
FOX_BWD_TQ, FOX_BWD_TK = 1024, 1024
FOX_DIAGONAL_PARTS = 4


def _fox_head_mask(sub, rows):
    lane = _lane((rows, 2 * LANES))
    main = (lane >= 64 * sub) & (lane < 64 * sub + 64)
    aug = (lane >= LANES + AUG_STRIDE * sub) & (lane < LANES + AUG_STRIDE * (sub + 1))
    return main | aug


def _fox_pieces(diagonal, tq, tk):
    if diagonal and tq == tk and tq >= FOX_DIAGONAL_PARTS * LANES:
        step = tq // FOX_DIAGONAL_PARTS
        return [(n * step, (n + 1) * step, (n + 1) * step) for n in range(FOX_DIAGONAL_PARTS)]
    return [(0, tq, tk)]


def _fox_fwd(q, qaug, k, kaug, v, T, tq, tk):
    nq, nk = T // tq, T // tk
    rep = tk // LANES
    last_of = lambda i: (i * tq + tq - 1) // tk

    def body(q_ref, qa_ref, k_ref, ka_ref, v_ref, o_ref, qab_ref, m_s, acc_s):
        p_, i, j = pl.program_id(0), pl.program_id(1), pl.program_id(2)
        last = last_of(i)

        @pl.when(j == 0)
        def _():
            m_s[...] = jnp.full(m_s.shape, NEG, F32)
            acc_s[...] = jnp.zeros_like(acc_s)

        def step(diagonal):
            k2 = jnp.concatenate([k_ref[...], ka_ref[...]], axis=1)
            v2 = jnp.concatenate([v_ref[...], ka_ref[...]], axis=1)
            pieces = _fox_pieces(diagonal, tq, tk)
            work = []
            for r0, r1, nc in pieces:
                rows = slice(r0, r1)
                q2 = jnp.concatenate([q_ref[rows, :], qa_ref[rows, :]], axis=1)
                for sub in range(2):
                    qh = jnp.where(_fox_head_mask(sub, r1 - r0), q2, jnp.zeros_like(q2))
                    work.append((rows, r0, r1 - r0, nc, sub, _dot(qh, k2[:nc], NT)))
            for rows, r0, nr, nc, sub, s in work:
                if diagonal:
                    causal = (lax.broadcasted_iota(jnp.int32, (nr, nc), 1) + j * tk
                              <= lax.broadcasted_iota(jnp.int32, (nr, nc), 0) + (r0 + i * tq))
                    s = jnp.where(causal, s, NEG)
                m_prev = m_s[sub, rows, :]
                m_next = jnp.maximum(m_prev, jnp.max(s, axis=1, keepdims=True))
                p = jnp.exp(s - jnp.tile(m_next, (1, nc // LANES)))
                alpha = jnp.exp(m_prev - m_next)
                m_s[sub, rows, :] = m_next
                acc_s[sub, rows, :] = acc_s[sub, rows, :] * jnp.tile(alpha, (1, 2)) + _dot(p.astype(BF16), v2[:nc])

        @pl.when(j == last)
        def _():
            step(True)

        @pl.when(j < last)
        def _():
            step(False)

        @pl.when(j == nk - 1)
        def _():
            lane = _lane((tq, LANES))
            outs = []
            qab = qa_ref[...].astype(F32)
            for sub in range(2):
                acc = acc_s[sub]
                base = AUG_STRIDE * sub
                l = jnp.sum(jnp.where(lane == base + AUG_C, acc[:, LANES:], 0.0), axis=1, keepdims=True)
                outs.append(acc[:, :LANES] / l)
                lse = jnp.max(m_s[sub], axis=1, keepdims=True) + jnp.log(l)
                pieces = _split3(-lse)
                for e in range(3):
                    qab = jnp.where(lane == base + AUG_STAT + e, pieces[e].astype(F32), qab)
            o_ref[...] = jnp.where(lane < 64, outs[0], outs[1]).astype(o_ref.dtype)
            qab_ref[...] = qab.astype(BF16)

    qspec = pl.BlockSpec((tq, LANES), lambda p, i, j: (i, p))
    kspec = pl.BlockSpec((tk, LANES), lambda p, i, j: (jnp.minimum(j, last_of(i)), p))
    return pl.pallas_call(
        body, name="fox_fwd", grid=(4, nq, nk),
        in_specs=[qspec, qspec, kspec, kspec, kspec],
        out_specs=[qspec, qspec],
        out_shape=[jax.ShapeDtypeStruct((T, 512), BF16), jax.ShapeDtypeStruct((T, 512), BF16)],
        scratch_shapes=[pltpu.VMEM((2, tq, LANES), F32), pltpu.VMEM((2, tq, 2 * LANES), F32)],
        compiler_params=_cparams("parallel", "parallel", "arbitrary", vmem=VMEM_BIG),
    )(q, qaug, k, kaug, v)


def _fox_bwd(q, qaug, k, kaug, v, do, doaug, T, tq, tk):
    nq, nk = T // tq, T // tk
    first_of = lambda j: (j * tk) // tq

    def body(q_ref, qa_ref, k_ref, ka_ref, v_ref, do_ref, doa_ref,
             dq_ref, dqa_ref, dk_ref, dka_ref, dv_ref, dk_s, dv_s):
        p_, j, i = pl.program_id(0), pl.program_id(1), pl.program_id(2)
        masked = i * tq < (j + 1) * tk - 1

        @pl.when((j == 0) & (i == 0))
        def _():
            dq_ref[...] = jnp.zeros_like(dq_ref)
            dqa_ref[...] = jnp.zeros_like(dqa_ref)

        @pl.when(i == 0)
        def _():
            dk_s[...] = jnp.zeros_like(dk_s)
            dv_s[...] = jnp.zeros_like(dv_s)

        def step(diagonal):
            k2 = jnp.concatenate([k_ref[...], ka_ref[...]], axis=1)
            v2 = jnp.concatenate([v_ref[...], ka_ref[...]], axis=1)
            work = []
            for r0, r1, nc in _fox_pieces(diagonal, tq, tk):
                rows = slice(r0, r1)
                q2 = jnp.concatenate([q_ref[rows, :], qa_ref[rows, :]], axis=1)
                do2 = jnp.concatenate([do_ref[rows, :], doa_ref[rows, :]], axis=1)
                for sub in range(2):
                    hm = _fox_head_mask(sub, r1 - r0)
                    qh = jnp.where(hm, q2, jnp.zeros_like(q2))
                    doh = jnp.where(hm, do2, jnp.zeros_like(do2))
                    s = _dot(qh, k2[:nc], NT)
                    dp = _dot(doh, v2[:nc], NT)
                    work.append((r0, r1 - r0, nc, sub, qh, doh, s, dp))
            dqs = {}
            for r0, nr, nc, sub, qh, doh, s, dp in work:
                if diagonal:
                    causal = (lax.broadcasted_iota(jnp.int32, (nr, nc), 1) + j * tk
                              <= lax.broadcasted_iota(jnp.int32, (nr, nc), 0) + (r0 + i * tq))
                    s = jnp.where(causal, s, NEG)
                p = jnp.exp(s)
                dsb = (p * dp).astype(BF16)
                dv_s[0:nc, :] += _dot(p.astype(BF16), doh[:, :LANES], TN)
                dk_s[0:nc, :] += _dot(dsb, qh, TN)
                dqs[(r0, sub)] = _dot(dsb, k2[:nc])
            for r0, r1, nc in _fox_pieces(diagonal, tq, tk):
                dq2 = jnp.where(_fox_head_mask(0, r1 - r0), dqs[(r0, 0)], dqs[(r0, 1)])
                qrows = pl.ds(pl.multiple_of(i * tq + r0, r1 - r0), r1 - r0)
                dq_ref[qrows, :] += dq2[:, :LANES] * FOX_SCALE
                dqa_ref[qrows, :] += dq2[:, LANES:]

        @pl.when((i >= first_of(j)) & masked)
        def _():
            step(True)

        @pl.when((i >= first_of(j)) & jnp.logical_not(masked))
        def _():
            step(False)

        @pl.when(i == nq - 1)
        def _():
            dk_ref[...] = dk_s[:, :LANES]
            dka_ref[...] = dk_s[:, LANES:]
            dv_ref[...] = dv_s[...]

    qspec = pl.BlockSpec((tq, LANES), lambda p, j, i: (jnp.maximum(i, first_of(j)), p))
    kspec = pl.BlockSpec((tk, LANES), lambda p, j, i: (j, p))
    resident = pl.BlockSpec((T, LANES), lambda p, j, i: (0, p))
    return pl.pallas_call(
        body, name="fox_bwd", grid=(4, nk, nq),
        in_specs=[qspec, qspec, kspec, kspec, kspec, qspec, qspec],
        out_specs=[resident, resident, kspec, kspec, kspec],
        out_shape=[jax.ShapeDtypeStruct((T, 512), F32)] * 5,
        scratch_shapes=[pltpu.VMEM((tk, 2 * LANES), F32), pltpu.VMEM((tk, LANES), F32)],
        compiler_params=_cparams("arbitrary", "arbitrary", "arbitrary", vmem=VMEM_BIG),
    )(q, qaug, k, kaug, v, do, doaug)


SWA_SUB = 16
SWA_TB = SWA_SUB * WINDOW


def _t5_bucket_matrix():
    t = jnp.arange(WINDOW)[:, None] + WINDOW
    s = jnp.arange(2 * WINDOW)[None, :]
    max_exact = REL_BUCKETS // 2
    d = jnp.maximum(t - s, 0)
    df = jnp.maximum(d, 1).astype(F32)
    large = max_exact + (jnp.log(df / max_exact) / math.log(REL_MAX_DIST / max_exact)
                         * (REL_BUCKETS - max_exact)).astype(jnp.int32)
    large = jnp.minimum(large, REL_BUCKETS - 1)
    return jnp.where(d < max_exact, d, large).astype(jnp.int32)


def _swa_bias(rel_bias, bucket):
    def body(rel_ref, bucket_ref, o_ref):
        b = bucket_ref[...]
        for h in range(SWA_HEADS):
            acc = jnp.zeros(b.shape, F32)
            for r in range(REL_BUCKETS):
                acc = jnp.where(b == r, rel_ref[r, h], acc)
            o_ref[h] = acc

    return pl.pallas_call(
        body, name="swa_bias",
        in_specs=[pl.BlockSpec(memory_space=pltpu.SMEM), pl.BlockSpec(memory_space=pltpu.VMEM)],
        out_specs=pl.BlockSpec(memory_space=pltpu.VMEM),
        out_shape=jax.ShapeDtypeStruct((SWA_HEADS, WINDOW, 2 * WINDOW), F32),
    )(rel_bias, bucket)


def _swa_bias_bwd(dbias, bucket):
    def body(db_ref, bucket_ref, o_ref):
        b = bucket_ref[...]
        lane = _lane((1, LANES))
        for r in range(REL_BUCKETS):
            row = jnp.zeros((1, LANES), F32)
            for h in range(SWA_HEADS):
                part = jnp.sum(jnp.where(b == r, db_ref[h], 0.0), axis=0, keepdims=True)
                tot = jnp.sum(part, axis=1, keepdims=True)
                row = jnp.where(lane == h, tot, row)
            o_ref[r:r + 1, :] = row

    return pl.pallas_call(
        body, name="swa_bias_bwd",
        in_specs=[pl.BlockSpec(memory_space=pltpu.VMEM), pl.BlockSpec(memory_space=pltpu.VMEM)],
        out_specs=pl.BlockSpec(memory_space=pltpu.VMEM),
        out_shape=jax.ShapeDtypeStruct((REL_BUCKETS, LANES), F32),
    )(dbias, bucket)


SWA_GROUP = SWA_HEADS // SWA_KV_HEADS


def _swa_valid(r, i):
    t = (lax.broadcasted_iota(jnp.int32, (SWA_GROUP * WINDOW, 2 * WINDOW), 0) & (WINDOW - 1)) + WINDOW
    s = lax.broadcasted_iota(jnp.int32, (SWA_GROUP * WINDOW, 2 * WINDOW), 1)
    dist = t - s
    band = (dist >= 0) & (dist < WINDOW)
    if r == 0:
        band = band & ((s >= WINDOW) | (i > 0))
    return band


def _swa_stack(blk):
    lane = _lane((WINDOW, LANES))
    parts = []
    for g in range(SWA_GROUP):
        b = blk[:, LANES * (g // 2):LANES * (g // 2 + 1)]
        parts.append(jnp.where((lane >= 64) if g % 2 else (lane < 64), b, jnp.zeros_like(b)))
    return jnp.concatenate(parts, axis=0)


def _swa_unstack(st):
    lane = _lane((WINDOW, LANES))
    W = WINDOW
    return jnp.concatenate([jnp.where(lane < 64, st[2 * b * W:(2 * b + 1) * W], st[(2 * b + 1) * W:(2 * b + 2) * W])
                            for b in range(2)], axis=1)


def _swa_sink_column(sink_ref, kvh):
    row = lax.broadcasted_iota(jnp.int32, (SWA_GROUP * WINDOW, 1), 0)
    col = jnp.full((SWA_GROUP * WINDOW, 1), sink_ref[SWA_GROUP * kvh + SWA_GROUP - 1], F32)
    for g in range(SWA_GROUP - 2, -1, -1):
        col = jnp.where(row < (g + 1) * WINDOW, sink_ref[SWA_GROUP * kvh + g], col)
    return col


def _swa_specs(T):
    W = WINDOW
    qspec = pl.BlockSpec((SWA_TB, 2 * LANES), lambda h, i: (i, h))
    own = pl.BlockSpec((None, SWA_TB, LANES), lambda h, i: (h, i, 0))
    prev = pl.BlockSpec((None, W, LANES), lambda h, i: (h, jnp.maximum(SWA_SUB * i - 1, 0), 0))
    stat = pl.BlockSpec((SWA_GROUP, SWA_TB, LANES), lambda h, i: (h, i, 0))
    bias = pl.BlockSpec((None, SWA_GROUP * W, 2 * W), lambda h, i: (h, 0, 0))
    return qspec, own, prev, stat, bias


def _swa_fwd(sinks, q, kad, vad, bias, T):
    nb = T // SWA_TB
    scale = SWA_HEAD_DIM ** -0.5
    W = WINDOW

    def body(sink_ref, q_ref, k_ref, kp_ref, v_ref, vp_ref, bias_ref, o_ref, lse_ref):
        kvh, i = pl.program_id(0), pl.program_id(1)
        sink = _swa_sink_column(sink_ref, kvh)
        for r in range(SWA_SUB):
            rs = slice(r * W, (r + 1) * W)
            ps = slice((r - 1) * W, r * W)
            k_own, v_own = k_ref[rs, :], v_ref[rs, :]
            k_prev = kp_ref[...] if r == 0 else k_ref[ps, :]
            v_prev = vp_ref[...] if r == 0 else v_ref[ps, :]
            qs = _swa_stack(q_ref[rs, :])
            s = jnp.concatenate([_dot(qs, k_prev, NT), _dot(qs, k_own, NT)], axis=1) * scale + bias_ref[...]
            s = jnp.where(_swa_valid(r, i), s, NEG)
            m = jnp.maximum(jnp.max(s, axis=1, keepdims=True), sink)
            p = jnp.exp(s - m)
            denom = jnp.sum(p, axis=1, keepdims=True) + jnp.exp(sink - m)
            pn = (p / denom).astype(BF16)
            o_ref[rs, :] = _swa_unstack(_dot(pn[:, :W], v_prev) + _dot(pn[:, W:], v_own)).astype(o_ref.dtype)
            lse = m + jnp.log(denom)
            for g in range(SWA_GROUP):
                lse_ref[g, rs, :] = jnp.broadcast_to(lse[g * W:(g + 1) * W], (W, LANES))

    qspec, own, prev, stat, bspec = _swa_specs(T)
    return pl.pallas_call(
        body, name="swa_fwd", grid=(SWA_KV_HEADS, nb),
        in_specs=[pl.BlockSpec(memory_space=pltpu.SMEM), qspec, own, prev, own, prev, bspec],
        out_specs=[qspec, stat],
        out_shape=[jax.ShapeDtypeStruct((T, 512), BF16), jax.ShapeDtypeStruct((SWA_HEADS, T, LANES), F32)],
        compiler_params=_cparams("parallel", "parallel", vmem=VMEM_MID),
    )(sinks, q, kad, kad, vad, vad, bias.reshape(SWA_KV_HEADS, SWA_GROUP * W, 2 * W))


def _swa_bwd(sinks, q, kad, vad, bias, do, lse, delta, T):
    nb = T // SWA_TB
    scale = SWA_HEAD_DIM ** -0.5
    W = WINDOW

    def body(sink_ref, q_ref, k_ref, kp_ref, v_ref, vp_ref, bias_ref, do_ref, lse_ref, dl_ref,
             dq_ref, dkad_ref, dvad_ref, dbias_ref, dsk_ref):
        kvh, i = pl.program_id(0), pl.program_id(1)
        sink = _swa_sink_column(sink_ref, kvh)

        @pl.when((kvh == 0) & (i == 0))
        def _():
            dkad_ref[...] = jnp.zeros_like(dkad_ref)
            dvad_ref[...] = jnp.zeros_like(dvad_ref)

        @pl.when(i == 0)
        def _():
            dbias_ref[...] = jnp.zeros_like(dbias_ref)
            dsk_ref[...] = jnp.zeros_like(dsk_ref)

        for r in range(SWA_SUB):
            rs = slice(r * W, (r + 1) * W)
            ps = slice((r - 1) * W, r * W)
            k_own, v_own = k_ref[rs, :], v_ref[rs, :]
            k_prev = kp_ref[...] if r == 0 else k_ref[ps, :]
            v_prev = vp_ref[...] if r == 0 else v_ref[ps, :]
            qs = _swa_stack(q_ref[rs, :])
            dos = _swa_stack(do_ref[rs, :])
            lse_b = jnp.concatenate([lse_ref[g, rs, :] for g in range(SWA_GROUP)], axis=0)
            dl_b = jnp.concatenate([dl_ref[g, rs, :] for g in range(SWA_GROUP)], axis=0)
            s = jnp.concatenate([_dot(qs, k_prev, NT), _dot(qs, k_own, NT)], axis=1) * scale + bias_ref[...]
            s = jnp.where(_swa_valid(r, i), s, NEG)
            p = jnp.exp(s - jnp.tile(lse_b, (1, 2)))
            dp = jnp.concatenate([_dot(dos, v_prev, NT), _dot(dos, v_own, NT)], axis=1)
            ds = p * (dp - jnp.tile(dl_b, (1, 2)))
            sink_term = jnp.exp(sink - lse_b) * dl_b
            for g in range(SWA_GROUP):
                dbias_ref[g] += ds[g * W:(g + 1) * W]
                dsk_ref[g:g + 1, :] += jnp.sum(sink_term[g * W:(g + 1) * W], axis=0, keepdims=True)
            dsb = ds.astype(BF16)
            pb = p.astype(BF16)
            dq_ref[rs, :] = _swa_unstack((_dot(dsb[:, :W], k_prev) + _dot(dsb[:, W:], k_own)) * scale)
            own_row = pl.multiple_of(i * SWA_TB + r * W, W)
            dkad_ref[kvh, pl.ds(own_row, W), :] += _dot(dsb[:, W:], qs, TN) * scale
            dvad_ref[kvh, pl.ds(own_row, W), :] += _dot(pb[:, W:], dos, TN)
            dk_prev = _dot(dsb[:, :W], qs, TN) * scale
            dv_prev = _dot(pb[:, :W], dos, TN)
            if r == 0:
                @pl.when(i > 0)
                def _():
                    prev_row = pl.multiple_of(i * SWA_TB - W, W)
                    dkad_ref[kvh, pl.ds(prev_row, W), :] += dk_prev
                    dvad_ref[kvh, pl.ds(prev_row, W), :] += dv_prev
            else:
                prev_row = pl.multiple_of(i * SWA_TB + (r - 1) * W, W)
                dkad_ref[kvh, pl.ds(prev_row, W), :] += dk_prev
                dvad_ref[kvh, pl.ds(prev_row, W), :] += dv_prev

    qspec, own, prev, stat, bspec = _swa_specs(T)
    full = pl.BlockSpec((SWA_KV_HEADS, T, LANES), lambda h, i: (0, 0, 0))
    return pl.pallas_call(
        body, name="swa_bwd", grid=(SWA_KV_HEADS, nb),
        in_specs=[pl.BlockSpec(memory_space=pltpu.SMEM), qspec, own, prev, own, prev, bspec, qspec, stat, stat],
        out_specs=[qspec, full, full, pl.BlockSpec((SWA_GROUP, W, 2 * W), lambda h, i: (h, 0, 0)),
                   pl.BlockSpec((None, 8, LANES), lambda h, i: (h, 0, 0))],
        out_shape=[jax.ShapeDtypeStruct((T, 512), F32), jax.ShapeDtypeStruct((SWA_KV_HEADS, T, LANES), F32),
                   jax.ShapeDtypeStruct((SWA_KV_HEADS, T, LANES), F32), jax.ShapeDtypeStruct((SWA_HEADS, W, 2 * W), F32),
                   jax.ShapeDtypeStruct((SWA_KV_HEADS, 8, LANES), F32)],
        compiler_params=_cparams("arbitrary", "arbitrary", vmem=VMEM_MID),
    )(sinks, q, kad, kad, vad, vad, bias.reshape(SWA_KV_HEADS, SWA_GROUP * W, 2 * W), do, lse, delta)


MEM_TQ = 4096


def _mem_fwd(q, mk, mv, T, tq):
    scale = MEM_HEAD_DIM ** -0.5

    def body(q_ref, k_ref, v_ref, o_ref, lse_ref):
        s = _dot(q_ref[...], k_ref[...], NT) * scale
        m = jnp.max(s, axis=1, keepdims=True)
        p = jnp.exp(s - m)
        l = jnp.sum(p, axis=1, keepdims=True)
        o_ref[...] = _dot((p / l).astype(BF16), v_ref[...]).astype(o_ref.dtype)
        lse_ref[...] = jnp.broadcast_to(m + jnp.log(l), (tq, LANES))

    qspec = pl.BlockSpec((tq, LANES), lambda h, i: (i, h))
    kspec = pl.BlockSpec((N_MEM, LANES), lambda h, i: (0, h))
    return pl.pallas_call(
        body, name="mem_fwd", grid=(MEM_HEADS, T // tq),
        in_specs=[qspec, kspec, kspec],
        out_specs=[qspec, pl.BlockSpec((None, tq, LANES), lambda h, i: (h, i, 0))],
        out_shape=[jax.ShapeDtypeStruct((T, 512), BF16), jax.ShapeDtypeStruct((MEM_HEADS, T, LANES), F32)],
        compiler_params=_cparams("parallel", "parallel"),
    )(q, mk, mv)


def _mem_bwd(q, mk, mv, do, lse, delta, T, tq):
    scale = MEM_HEAD_DIM ** -0.5
    rep = N_MEM // LANES

    def body(q_ref, k_ref, v_ref, do_ref, lse_ref, dl_ref, dq_ref, dk_ref, dv_ref):
        i = pl.program_id(1)

        @pl.when(i == 0)
        def _():
            dk_ref[...] = jnp.zeros_like(dk_ref)
            dv_ref[...] = jnp.zeros_like(dv_ref)

        qv, dov = q_ref[...], do_ref[...]
        s = _dot(qv, k_ref[...], NT) * scale
        p = jnp.exp(s - jnp.tile(lse_ref[...], (1, rep)))
        dp = _dot(dov, v_ref[...], NT)
        ds = p * (dp - jnp.tile(dl_ref[...], (1, rep)))
        dsb = ds.astype(BF16)
        dq_ref[...] = _dot(dsb, k_ref[...]) * scale
        dk_ref[...] += _dot(dsb, qv, TN) * scale
        dv_ref[...] += _dot(p.astype(BF16), dov, TN)

    qspec = pl.BlockSpec((tq, LANES), lambda h, i: (i, h))
    kspec = pl.BlockSpec((N_MEM, LANES), lambda h, i: (0, h))
    stat = pl.BlockSpec((None, tq, LANES), lambda h, i: (h, i, 0))
    return pl.pallas_call(
        body, name="mem_bwd", grid=(MEM_HEADS, T // tq),
        in_specs=[qspec, kspec, kspec, qspec, stat, stat],
        out_specs=[qspec, kspec, kspec],
        out_shape=[jax.ShapeDtypeStruct((T, 512), F32), jax.ShapeDtypeStruct((N_MEM, 512), F32),
                   jax.ShapeDtypeStruct((N_MEM, 512), F32)],
        compiler_params=_cparams("arbitrary", "arbitrary"),
    )(q, mk, mv, do, lse, delta)


def _mem_prep_fwd(mem, g_mem, w_kv, kn_gain, gm128):
    def body(mem_ref, g_ref, w_ref, kn_ref, gm_ref, memn_o, kv_o, mk_o, mv_o):
        xhat, _ = _rms_rows(mem_ref[...], None)
        memn = (xhat * g_ref[...]).astype(BF16)
        memn_o[...] = memn
        kv = _dot(memn, w_ref[...])
        kv_o[...] = kv
        gm = gm_ref[...]
        for c in range(4):
            sl = slice(c * LANES, (c + 1) * LANES)
            y, _ = _head_norm(kv[:, sl], gm, kn_ref[...])
            mk_o[:, sl] = y.astype(BF16)
        mv_o[...] = kv[:, 512:].astype(BF16)

    vm = pl.BlockSpec(memory_space=pltpu.VMEM)
    return pl.pallas_call(
        body, name="mem_prep_fwd", in_specs=[vm] * 5, out_specs=[vm] * 4,
        out_shape=[jax.ShapeDtypeStruct((N_MEM, D_MODEL), BF16), jax.ShapeDtypeStruct((N_MEM, D_MODEL), F32),
                   jax.ShapeDtypeStruct((N_MEM, 512), BF16), jax.ShapeDtypeStruct((N_MEM, 512), BF16)],
        compiler_params=pltpu.CompilerParams(vmem_limit_bytes=VMEM_MID),
    )(mem, g_mem, w_kv, kn_gain, gm128)


def _mem_prep_bwd(mem, g_mem, memn, kv, w_kv, kn_gain, gm128, dmk, dmv):
    def body(mem_ref, g_ref, memn_ref, kv_ref, w_ref, kn_ref, gm_ref, dmk_ref, dmv_ref, dw_o, dg_o, dkn_o, dkv_s):
        gm = gm_ref[...]
        dkn = jnp.zeros((1, LANES), F32)
        for c in range(4):
            sl = slice(c * LANES, (c + 1) * LANES)
            dx, dg = _head_norm_bwd(dmk_ref[:, sl], kv_ref[:, sl], gm, kn_ref[...])
            dkv_s[:, sl] = dx.astype(BF16)
            dkn = dkn + dg
        dkn_o[...] = dkn
        dkv_s[:, 512:] = dmv_ref[...].astype(BF16)
        dkv = dkv_s[...]
        dw_o[...] = _dot(memn_ref[...], dkv, TN)
        dmemn = _dot(dkv, w_ref[...], NT)
        xhat, _ = _rms_rows(mem_ref[...], None)
        dg_o[...] = jnp.sum(dmemn * xhat, axis=0, keepdims=True)

    vm = pl.BlockSpec(memory_space=pltpu.VMEM)
    return pl.pallas_call(
        body, name="mem_prep_bwd", in_specs=[vm] * 9, out_specs=[vm] * 3,
        out_shape=[jax.ShapeDtypeStruct((D_MODEL, D_MODEL), F32), jax.ShapeDtypeStruct((1, D_MODEL), F32),
                   jax.ShapeDtypeStruct((1, LANES), F32)],
        scratch_shapes=[pltpu.VMEM((N_MEM, D_MODEL), BF16)],
        compiler_params=pltpu.CompilerParams(vmem_limit_bytes=VMEM_MID),
    )(mem, g_mem, memn, kv, w_kv, kn_gain, gm128, dmk, dmv)


SLOT_O = D_MODEL // N_SHARD


def _merge_fwd(proj, b_gate, o3, w3, T, tb):
    def body(gl_ref, bg_ref, oa_ref, of_ref, om_ref, wa_ref, wf_ref, wm_ref, out_ref):
        o_refs = (oa_ref, of_ref, om_ref)
        w_refs = (wa_ref, wf_ref, wm_ref)
        for n in range(N_SHARD):
            acc = jnp.zeros((tb, SLOT_O), F32)
            for b in range(3):
                c0 = b * D_MODEL + n * SLOT_O
                g = jax.nn.sigmoid(gl_ref[:, c0:c0 + SLOT_O] + bg_ref[:, c0:c0 + SLOT_O])
                acc = acc + g * _dot(o_refs[b][...], w_refs[b][n])
            out_ref[:, n * SLOT_O:(n + 1) * SLOT_O] = acc.astype(out_ref.dtype)

    rows = pl.BlockSpec((tb, 512), lambda i: (i, 0))
    wspec = pl.BlockSpec((N_SHARD, 512, SLOT_O), lambda i: (0, 0, 0))
    return pl.pallas_call(
        body, name="merge_fwd", grid=(T // tb,),
        in_specs=[pl.BlockSpec((tb, GATE_W), lambda i: (i, 1)), pl.BlockSpec((1, GATE_W), lambda i: (0, 0)),
                  rows, rows, rows, wspec, wspec, wspec],
        out_specs=pl.BlockSpec((tb, D_MODEL), lambda i: (i, 0)),
        out_shape=jax.ShapeDtypeStruct((T, D_MODEL), BF16),
        compiler_params=_cparams("parallel", vmem=VMEM_BIG),
    )(proj, b_gate, *o3, *w3)


def _merge_bwd(proj, b_gate, o3, w3, dmerged, T, tb):
    heads = (SWA_HEADS, FOX_HEADS, MEM_HEADS)

    def body(gl_ref, bg_ref, oa_ref, of_ref, om_ref, wa_ref, wf_ref, wm_ref, dm_ref,
             dgl_o, doa_o, dof_o, dom_o, dla_o, dlf_o, dlm_o, dwa_o, dwf_o, dwm_o, dbg_o):
        i = pl.program_id(0)
        o_refs = (oa_ref, of_ref, om_ref)
        w_refs = (wa_ref, wf_ref, wm_ref)
        do_refs = (doa_o, dof_o, dom_o)
        dl_refs = (dla_o, dlf_o, dlm_o)
        dw_refs = (dwa_o, dwf_o, dwm_o)

        @pl.when(i == 0)
        def _():
            for r in dw_refs:
                r[...] = jnp.zeros_like(r)
            dbg_o[...] = jnp.zeros_like(dbg_o)

        lane = _lane((tb, LANES))
        for b in range(3):
            ob = o_refs[b][...]
            do = jnp.zeros((tb, 512), F32)
            for n in range(N_SHARD):
                c0 = b * D_MODEL + n * SLOT_O
                g = jax.nn.sigmoid(gl_ref[:, c0:c0 + SLOT_O] + bg_ref[:, c0:c0 + SLOT_O])
                dm = dm_ref[:, n * SLOT_O:(n + 1) * SLOT_O]
                y = _dot(ob, w_refs[b][n])
                dgl = dm * y * g * (1.0 - g)
                dgl_o[:, c0:c0 + SLOT_O] = dgl.astype(dgl_o.dtype)
                dbg_o[:, c0:c0 + SLOT_O] += jnp.sum(dgl, axis=0, keepdims=True)
                dy = (dm * g).astype(BF16)
                do = do + _dot(dy, w_refs[b][n], NT)
                dw_refs[b][n] += _dot(ob, dy, TN)
            do_refs[b][...] = do.astype(BF16)
            prod = do * ob.astype(F32)
            for c in range(4):
                blk = prod[:, c * LANES:(c + 1) * LANES]
                if heads[b] == 8:
                    lo = jnp.sum(jnp.where(lane < 64, blk, 0.0), axis=1, keepdims=True)
                    hi = jnp.sum(jnp.where(lane >= 64, blk, 0.0), axis=1, keepdims=True)
                    if b == 1:
                        aug = jnp.zeros((tb, LANES), F32)
                        for sub, dl in enumerate((lo, hi)):
                            for e, piece in enumerate(_split3(-dl)):
                                aug = jnp.where(lane == AUG_STRIDE * sub + AUG_C + e, piece.astype(F32), aug)
                        dl_refs[b][:, c * LANES:(c + 1) * LANES] = aug.astype(BF16)
                    else:
                        dl_refs[b][2 * c] = jnp.broadcast_to(lo, (tb, LANES))
                        dl_refs[b][2 * c + 1] = jnp.broadcast_to(hi, (tb, LANES))
                else:
                    dl_refs[b][c] = jnp.broadcast_to(jnp.sum(blk, axis=1, keepdims=True), (tb, LANES))

    rows = pl.BlockSpec((tb, 512), lambda i: (i, 0))
    wspec = pl.BlockSpec((N_SHARD, 512, SLOT_O), lambda i: (0, 0, 0))
    stat = lambda h: pl.BlockSpec((h, tb, LANES), lambda i: (0, i, 0))
    return pl.pallas_call(
        body, name="merge_bwd", grid=(T // tb,),
        in_specs=[pl.BlockSpec((tb, GATE_W), lambda i: (i, 1)), pl.BlockSpec((1, GATE_W), lambda i: (0, 0)),
                  rows, rows, rows, wspec, wspec, wspec, pl.BlockSpec((tb, D_MODEL), lambda i: (i, 0))],
        out_specs=[pl.BlockSpec((tb, GATE_W), lambda i: (i, 0)), rows, rows, rows,
                   stat(8), rows, stat(4), wspec, wspec, wspec, pl.BlockSpec((1, GATE_W), lambda i: (0, 0))],
        out_shape=[jax.ShapeDtypeStruct((T, GATE_W), BF16)] + [jax.ShapeDtypeStruct((T, 512), BF16)] * 3
        + [jax.ShapeDtypeStruct((8, T, LANES), F32), jax.ShapeDtypeStruct((T, 512), BF16),
           jax.ShapeDtypeStruct((4, T, LANES), F32)]
        + [jax.ShapeDtypeStruct((N_SHARD, 512, SLOT_O), F32)] * 3 + [jax.ShapeDtypeStruct((1, GATE_W), F32)],
        compiler_params=_cparams("arbitrary", vmem=VMEM_BIG),
    )(proj, b_gate, *o3, *w3, dmerged)


def _local_step(x, h, mem, tgt, small, g_in, w_kv, w_o3, w_out, w_up, w_down, reducer):
    T = x.shape[0]
    tm = min(512, T)
    tile2 = lambda v: jnp.tile(v.reshape(1, -1), (1, LANES // v.size))
    gains = jnp.concatenate([tile2(small["qn_swa"]), tile2(small["kn_swa"]), tile2(small["qn_fox"]),
                             tile2(small["kn_fox"]), tile2(small["qn_mem"]), jnp.zeros((3, LANES), F32)], axis=0)
    kn_mem = small["kn_mem"].reshape(1, LANES)
    bfor = jnp.pad(small["b_forget"].reshape(1, -1), ((0, 0), (0, LANES - FOX_HEADS)))
    gm64 = _group_mean_matrix(64)
    gm128 = _group_mean_matrix(128)
    tb_prep = min(512, T)
    ones = jnp.ones((tb_prep, tb_prep), F32)
    tril = jnp.tril(ones).astype(BF16)
    triu = jnp.triu(ones).astype(BF16)
    bucket = _t5_bucket_matrix()
    g_mix, g_mlp, g_mem = small["g_mix"], small["g_mlp"], small["g_mem"]
    b_gate = small["b_gate"]
    sinks = small["sink_swa"].reshape(-1)

    tl = min(1024, T)
    sq = pl.BlockSpec((tl, D_MODEL), lambda i, j, k: (i, j))
    wc = _w_in_to_segments(g_in)
    (proj,) = _matmul(
        "mm_proj", h, wc, dims=NN, grid=(T // tl, PROJ_W // D_MODEL, 1),
        a_spec=pl.BlockSpec((tl, D_MODEL), lambda i, j, k: (i, 0)),
        b_spec=pl.BlockSpec((D_MODEL, D_MODEL), lambda i, j, k: (0, j)),
        acc_shape=(tl, D_MODEL),
        outs=[(jax.ShapeDtypeStruct((T, PROJ_W), F32), sq)],
        epilogue=_epi_store)
    qa, qf, kf, vf, qm, kad, vad, qf_aug, kf_aug = _prep_fwd(proj, gains, bfor, tril, gm64, gm128, T, tb_prep)
    bias = _swa_bias(small["rel_bias"], bucket)
    o_swa, lse_swa = _swa_fwd(sinks, qa, kad, vad, bias, T)
    o_fox, qf_aug_bwd = _fox_fwd(qf, qf_aug, kf, kf_aug, vf, T, min(FOX_TQ, T), min(FOX_TK, T))
    memn, kv, mk, mv = _mem_prep_fwd(mem, g_mem, w_kv, kn_mem, gm128)
    o_mem, lse_mem = _mem_fwd(qm, mk, mv, T, min(MEM_TQ, T))
    o3 = (o_swa, o_fox, o_mem)
    merged = _merge_fwd(proj, b_gate, o3, w_o3, T, min(512, T))

    def epi_residual(acc, extra_refs, out_refs, ij):
        out_refs[0][...] = extra_refs[0][...] + acc

    row_full = pl.BlockSpec((tm, D_MODEL), lambda i, j, k: (i, 0))
    row_big = pl.BlockSpec((tl, D_MODEL), lambda i, j, k: (i, 0))
    whole = pl.BlockSpec((D_MODEL, D_MODEL), lambda i, j, k: (0, 0))
    (x2,) = _matmul(
        "mm_out", merged, w_out, dims=NN, grid=(T // tl, 1, 1),
        a_spec=row_big, b_spec=whole,
        acc_shape=(tl, D_MODEL), extra=[(x, row_big)],
        outs=[(jax.ShapeDtypeStruct((T, D_MODEL), F32), row_big)], epilogue=epi_residual)
    hm = _rmsnorm("rms_mlp", x2, g_mlp, tm)

    def epi_relu2(acc, extra_refs, out_refs, ij):
        out_refs[0][...] = acc.astype(BF16)
        r = jnp.maximum(acc, 0.0)
        out_refs[1][...] = (r * r).astype(BF16)

    up, u = _matmul(
        "mm_up", hm, w_up, dims=NN, grid=(T // tl, N_SHARD, 1),
        a_spec=row_big, b_spec=pl.BlockSpec((None, D_MODEL, D_MODEL), lambda i, j, k: (j, 0, 0)),
        acc_shape=(tl, D_MODEL),
        outs=[(jax.ShapeDtypeStruct((T, D_FF), BF16), sq), (jax.ShapeDtypeStruct((T, D_FF), BF16), sq)],
        epilogue=epi_relu2)

    def epi_loss(acc, extra_refs, out_refs, ij):
        y = extra_refs[0][...] + acc
        err = y - extra_refs[1][...]
        dyv = err * (1.0 / D_MODEL)
        out_refs[0][...] = dyv
        out_refs[2][...] = dyv.astype(BF16)
        sq = jnp.sum(jnp.sum(err * err, axis=1, keepdims=True), axis=0, keepdims=True)

        @pl.when(ij[0] == 0)
        def _():
            out_refs[1][...] = jnp.zeros_like(out_refs[1])

        out_refs[1][...] += jnp.broadcast_to(sq, out_refs[1].shape)

    kblk = pl.BlockSpec((tl, D_MODEL), lambda i, j, k: (i, k))
    dy, loss_acc, dy_bf = _matmul(
        "mm_down", u, w_down, dims=NN, grid=(T // tl, 1, N_SHARD),
        a_spec=kblk, b_spec=pl.BlockSpec((D_MODEL, D_MODEL), lambda i, j, k: (k, 0)),
        acc_shape=(tl, D_MODEL), extra=[(x2, row_big), (tgt, row_big)],
        outs=[(jax.ShapeDtypeStruct((T, D_MODEL), F32), row_big),
              (jax.ShapeDtypeStruct((8, LANES), F32), pl.BlockSpec((8, LANES), lambda i, j, k: (0, 0))),
              (jax.ShapeDtypeStruct((T, D_MODEL), BF16), row_big)],
        epilogue=epi_loss)
    loss = loss_acc[0, 0] * (0.5 / D_MODEL)

    def epi_dup(acc, extra_refs, out_refs, ij):
        out_refs[0][...] = (acc * (2.0 * jnp.maximum(extra_refs[0][...].astype(F32), 0.0))).astype(BF16)

    (dup,) = _matmul(
        "mm_dup", dy_bf, w_down, dims=NT, grid=(T // tl, N_SHARD, 1),
        a_spec=row_big, b_spec=pl.BlockSpec((D_MODEL, D_MODEL), lambda i, j, k: (j, 0)),
        acc_shape=(tl, D_MODEL), extra=[(up, sq)],
        outs=[(jax.ShapeDtypeStruct((T, D_FF), BF16), sq)], epilogue=epi_dup)

    nkt = T // tl
    t_rows = pl.BlockSpec((tl, D_MODEL), lambda i, j, k: (k, i))
    t_cols = pl.BlockSpec((tl, D_MODEL), lambda i, j, k: (k, j))
    (d_w_down,) = _matmul(
        "mm_dw_down", u, dy_bf, dims=TN, grid=(N_SHARD, 1, nkt),
        a_spec=t_rows, b_spec=t_cols, acc_shape=(D_MODEL, D_MODEL),
        outs=[(jax.ShapeDtypeStruct((D_FF, D_MODEL), F32), pl.BlockSpec((D_MODEL, D_MODEL), lambda i, j, k: (i, 0)))],
        epilogue=_epi_store)
    (d_w_up,) = _matmul(
        "mm_dw_up", hm, dup, dims=TN, grid=(1, N_SHARD, nkt),
        a_spec=t_rows, b_spec=t_cols, acc_shape=(D_MODEL, D_MODEL),
        outs=[(jax.ShapeDtypeStruct((N_SHARD, D_MODEL, D_MODEL), F32),
               pl.BlockSpec((None, D_MODEL, D_MODEL), lambda i, j, k: (j, 0, 0)))],
        epilogue=_epi_store)

    def epi_rms_bwd(acc, extra_refs, out_refs, ij):
        dx, dg = _rmsnorm_bwd_rows(acc, extra_refs[0][...], extra_refs[1][...])
        out_refs[0][...] = dx + extra_refs[2][...]

        @pl.when(ij[0] == 0)
        def _():
            out_refs[1][...] = jnp.zeros_like(out_refs[1])

        out_refs[1][...] += dg

    gain_spec = pl.BlockSpec((1, D_MODEL), lambda i, j, k: (0, 0))
    dx2, d_g_mlp = _matmul(
        "mm_dhm", dup, w_up, dims=NT, grid=(T // tl, 1, N_SHARD),
        a_spec=kblk, b_spec=pl.BlockSpec((None, D_MODEL, D_MODEL), lambda i, j, k: (k, 0, 0)),
        acc_shape=(tl, D_MODEL), extra=[(x2, row_big), (g_mlp, gain_spec), (dy, row_big)],
        outs=[(jax.ShapeDtypeStruct((T, D_MODEL), F32), row_big), (jax.ShapeDtypeStruct((1, D_MODEL), F32), gain_spec)],
        epilogue=epi_rms_bwd)

    (dmerged,) = _matmul(
        "mm_dmerged", dx2, w_out, dims=NT, grid=(T // tl, 1, 1),
        a_spec=row_big, b_spec=whole,
        acc_shape=(tl, D_MODEL), outs=[(jax.ShapeDtypeStruct((T, D_MODEL), F32), row_big)], epilogue=_epi_store)
    (d_w_out,) = _matmul(
        "mm_dw_out", merged, dx2, dims=TN, grid=(1, 1, nkt),
        a_spec=t_rows, b_spec=t_cols, acc_shape=(D_MODEL, D_MODEL),
        outs=[(jax.ShapeDtypeStruct((D_MODEL, D_MODEL), F32), whole)],
        epilogue=_epi_store)
    (dgl, do_swa, do_fox, do_mem, dl_swa, do_fox_aug, dl_mem, d_wo_swa, d_wo_fox, d_wo_mem, d_b_gate) = _merge_bwd(
        proj, b_gate, o3, w_o3, dmerged, T, min(512, T))

    dqm, dmk, dmv = _mem_bwd(qm, mk, mv, do_mem, lse_mem, dl_mem, T, min(MEM_TQ, T))
    d_w_kv, d_g_mem, d_kn_mem = _mem_prep_bwd(mem, g_mem, memn, kv, w_kv, kn_mem, gm128, dmk, dmv)
    do_swa = reducer.early_start({"w_mlp_down": d_w_down, "w_mlp_up": d_w_up, "w_out": d_w_out, "w_mem_kv": d_w_kv,
                                  "w_o_swa": d_wo_swa, "w_o_fox": d_wo_fox, "w_o_mem": d_wo_mem}, do_swa)
    dqa, dkad, dvad, dbias, dsk = _swa_bwd(sinks, qa, kad, vad, bias, do_swa, lse_swa, dl_swa, T)
    dqa, do_fox = reducer.early_send((dqa, do_fox))
    dqf, dqf_aug, dkf, dkf_aug, dvf = _fox_bwd(qf, qf_aug_bwd, kf, kf_aug, vf, do_fox, do_fox_aug, T,
                                               min(FOX_BWD_TQ, T), min(FOX_BWD_TK, T))
    dvf = reducer.early_finish(dvf)
    d_rel = _swa_bias_bwd(dbias, bucket)
    dlo, gacc = _prep_bwd(proj, dqa, dkad, dvad, dqf, dkf, dvf, dqm, dqf_aug, dkf_aug, gains, bfor, triu, gm64, gm128,
                          T, tb_prep)

    def dwc_half(name, dpart):
        (res,) = _matmul(
            name, h, dpart, dims=TN, grid=(1, LO_W // D_MODEL, nkt),
            a_spec=t_rows, b_spec=t_cols, acc_shape=(D_MODEL, D_MODEL),
            outs=[(jax.ShapeDtypeStruct((D_MODEL, LO_W), F32), pl.BlockSpec((D_MODEL, D_MODEL), lambda i, j, k: (0, j)))],
            epilogue=_epi_store)
        return res

    d_wc_lo = dwc_half("mm_dwc_lo", dlo)
    d_wc_gl = dwc_half("mm_dwc_gl", dgl)
    dlo = reducer.late_start({"wc_lo": d_wc_lo, "wc_gl": d_wc_gl}, dlo)
    (dh_lo,) = _matmul(
        "mm_dh_lo", dlo, wc, dims=NT, grid=(T // tl, 1, LO_W // D_MODEL),
        a_spec=kblk, b_spec=pl.BlockSpec((D_MODEL, D_MODEL), lambda i, j, k: (0, k)),
        acc_shape=(tl, D_MODEL), outs=[(jax.ShapeDtypeStruct((T, D_MODEL), F32), row_big)], epilogue=_epi_store)
    dh_lo = reducer.late_send(dh_lo)

    def epi_dx(acc, extra_refs, out_refs, ij):
        dhh = acc + extra_refs[3][...]
        dx, dg = _rmsnorm_bwd_rows(dhh, extra_refs[0][...], extra_refs[1][...])
        out_refs[0][...] = dx + extra_refs[2][...]

        @pl.when(ij[0] == 0)
        def _():
            out_refs[1][...] = jnp.zeros_like(out_refs[1])

        out_refs[1][...] += dg

    grad_x, d_g_mix = _matmul(
        "mm_dh_gl", dgl, wc, dims=NT, grid=(T // tl, 1, GATE_W // D_MODEL),
        a_spec=kblk, b_spec=pl.BlockSpec((D_MODEL, D_MODEL), lambda i, j, k: (0, k + LO_W // D_MODEL)),
        acc_shape=(tl, D_MODEL), extra=[(x, row_big), (g_mix, gain_spec), (dx2, row_big), (dh_lo, row_big)],
        outs=[(jax.ShapeDtypeStruct((T, D_MODEL), F32), row_big), (jax.ShapeDtypeStruct((1, D_MODEL), F32), gain_spec)],
        epilogue=epi_dx, vmem=VMEM_MAX)

    fold64 = lambda row: (row[:64] + row[64:]).reshape(1, 64)
    grads = {
        "g_mix": d_g_mix, "b_gate": d_b_gate, "b_forget": gacc[5, :FOX_HEADS].reshape(1, FOX_HEADS),
        "qn_swa": fold64(gacc[0]), "kn_swa": fold64(gacc[1]),
        "sink_swa": -dsk[:, :SWA_GROUP, 0].reshape(1, SWA_HEADS), "rel_bias": d_rel[:, :SWA_HEADS],
        "qn_fox": fold64(gacc[2]), "kn_fox": fold64(gacc[3]),
        "g_mem": d_g_mem, "qn_mem": gacc[4].reshape(1, LANES), "kn_mem": d_kn_mem, "g_mlp": d_g_mlp,
    }
    return loss, grad_x, grads


MESH = pl.DeviceIdType.MESH
ANY = pl.BlockSpec(memory_space=pl.ANY)


def _place():
    x, y, c = lax.axis_index("x"), lax.axis_index("y"), lax.axis_index("c")
    chips = [(1 - x, y), (x, 1 - y), (1 - x, 1 - y)]
    return x, y, c, chips


def _handshake(peers):
    barrier = pltpu.get_barrier_semaphore()
    for peer in peers:
        pl.semaphore_signal(barrier, inc=1, device_id=peer, device_id_type=MESH)
    pl.semaphore_wait(barrier, len(peers))


def _all_gather_shards_async(name, collective_id, slots):
    n = len(slots)
    bufs = [jax.new_ref(s, memory_space=pltpu.MemorySpace.HBM) for s in slots]

    def body(ici_send, ici_recv, d2d_send, d2d_recv):
        x, y, c, chips = _place()
        sibling = (x, y, 1 - c)
        me = 2 * x + y
        _handshake([(px, py, c) for px, py in chips] + [sibling])

        def half(a, who):
            hr = slots[a].shape[1] // 2
            return pl.ds(pl.multiple_of(who * hr, hr), hr)

        def ici(a, j, slot, to):
            return pltpu.make_async_remote_copy(
                src_ref=bufs[a].at[me, half(a, c)], dst_ref=bufs[a].at[slot, half(a, c)],
                send_sem=ici_send.at[3 * a + j], recv_sem=ici_recv.at[3 * a + j], device_id=to, device_id_type=MESH)

        def d2d(a, j, slot, which):
            part = bufs[a].at[slot, half(a, which)]
            return pltpu.make_async_remote_copy(
                src_ref=part, dst_ref=part, send_sem=d2d_send.at[3 * a + j], recv_sem=d2d_recv.at[3 * a + j],
                device_id=sibling, device_id_type=MESH)

        sends = [ici(a, j, me, (*chip, c)) for a in range(n) for j, chip in enumerate(chips)]
        for cp in sends:
            cp.start()
        passed = []
        for a in range(n):
            for j, (px, py) in enumerate(chips):
                ici(a, j, 2 * px + py, (px, py, c)).wait_recv()
                cp = d2d(a, j, 2 * px + py, c)
                cp.start()
                passed.append(cp)
        for a in range(n):
            for j, (px, py) in enumerate(chips):
                d2d(a, j, 2 * px + py, 1 - c).wait_recv()
        for cp in sends + passed:
            cp.wait_send()

    pl.kernel(
        body, mesh=plsc.ScalarSubcoreMesh(axis_name="seq", num_cores=1), name=name,
        scratch_types=[pltpu.SemaphoreType.DMA((3 * n,))] * 4,
        compiler_params=pltpu.CompilerParams(collective_id=collective_id),
    )()
    return [b[...] for b in bufs]


def _sequencer_call(name, collective_id, n_sems, body):
    pl.kernel(
        body, mesh=plsc.ScalarSubcoreMesh(axis_name="seq", num_cores=1), name=name,
        scratch_types=[pltpu.SemaphoreType.DMA((n_sems,))] * 2,
        compiler_params=pltpu.CompilerParams(collective_id=collective_id),
    )()


def _hbm_ref(value):
    return jax.new_ref(value, memory_space=pltpu.MemorySpace.HBM)


def _pair_exchange(name, collective_id, gs):
    n = len(gs)
    src = [_hbm_ref(g) for g in gs]
    stage = [jax.empty_ref(jax.ShapeDtypeStruct((N_SHARD, g.shape[1] // 2, g.shape[2]), g.dtype),
                           memory_space=pltpu.MemorySpace.HBM) for g in gs]

    def body(send_sem, recv_sem):
        x, y, c, _ = _place()
        sibling = (x, y, 1 - c)
        _handshake([sibling])
        copies = []
        for a in range(n):
            hr = gs[a].shape[1] // 2
            theirs = pl.ds(pl.multiple_of((1 - c) * hr, hr), hr)
            copies.append(pltpu.make_async_remote_copy(
                src_ref=src[a].at[:, theirs, :], dst_ref=stage[a], send_sem=send_sem.at[a], recv_sem=recv_sem.at[a],
                device_id=sibling, device_id_type=MESH))
        for cp in copies:
            cp.start()
        for cp in copies:
            cp.wait()

    _sequencer_call(name, collective_id, n, body)
    return [s[...] for s in stage]


def _chip_exchange(name, collective_id, sums):
    n = len(sums)
    src = [_hbm_ref(s) for s in sums]
    got = [jax.empty_ref(jax.ShapeDtypeStruct((3,) + s.shape[1:], s.dtype), memory_space=pltpu.MemorySpace.HBM)
           for s in sums]

    def body(send_sem, recv_sem):
        x, y, c, chips = _place()
        _handshake([(px, py, c) for px, py in chips])
        copies = []
        for a in range(n):
            for j, (px, py) in enumerate(chips):
                copies.append(pltpu.make_async_remote_copy(
                    src_ref=src[a].at[2 * px + py], dst_ref=got[a].at[j],
                    send_sem=send_sem.at[3 * a + j], recv_sem=recv_sem.at[3 * a + j],
                    device_id=(px, py, c), device_id_type=MESH))
        for cp in copies:
            cp.start()
        for cp in copies:
            cp.wait()

    _sequencer_call(name, collective_id, 3 * n, body)
    return [g[...] for g in got]


def _pair_gather(name, collective_id, fulls):
    n = len(fulls)
    full = [_hbm_ref(f) for f in fulls]

    def body(send_sem, recv_sem):
        x, y, c, _ = _place()
        sibling = (x, y, 1 - c)
        _handshake([sibling])
        copies = []
        for a in range(n):
            hr = fulls[a].shape[0] // 2
            mine = full[a].at[pl.ds(pl.multiple_of(c * hr, hr), hr)]
            copies.append(pltpu.make_async_remote_copy(
                src_ref=mine, dst_ref=mine, send_sem=send_sem.at[a], recv_sem=recv_sem.at[a],
                device_id=sibling, device_id_type=MESH))
        for cp in copies:
            cp.start()
        for cp in copies:
            cp.wait()

    _sequencer_call(name, collective_id, n, body)
    return [f[...] for f in full]


ELEMENTWISE_BLOCK_ELEMS = 256 * 1024


def _row_block(rows, cols):
    rb = 8
    while rb * 2 * cols <= ELEMENTWISE_BLOCK_ELEMS and rb * 2 <= rows:
        rb *= 2
    return rb


def _pair_sum(name, place, g, stage):
    _, R, C = g.shape
    hr = R // 2
    rb = _row_block(hr, C)
    nb = hr // rb

    def body(place_ref, g_ref, st_ref, sum_bf, own_f32):
        s = pl.program_id(1)
        tot = g_ref[...] + st_ref[...]
        sum_bf[...] = tot.astype(BF16)

        @pl.when(s == place_ref[0])
        def _():
            own_f32[...] = tot

    return pl.pallas_call(
        body, name=name,
        grid_spec=pltpu.PrefetchScalarGridSpec(
            num_scalar_prefetch=1, grid=(nb, N_SHARD),
            in_specs=[pl.BlockSpec((None, rb, C), lambda i, s, pr: (s, pr[1] * nb + i, 0)),
                      pl.BlockSpec((None, rb, C), lambda i, s, pr: (s, i, 0))],
            out_specs=[pl.BlockSpec((None, rb, C), lambda i, s, pr: (s, i, 0)),
                       pl.BlockSpec((rb, C), lambda i, s, pr: (i, 0))]),
        out_shape=[jax.ShapeDtypeStruct((N_SHARD, hr, C), BF16), jax.ShapeDtypeStruct((hr, C), F32)],
        compiler_params=_cparams("arbitrary", "arbitrary"),
    )(place, g, stage)


def _final_sum(name, place, own, got):
    hr, C = own.shape
    rb = _row_block(hr, C)
    nb = hr // rb

    def body(place_ref, own_ref, got_ref, o_ref):
        o_ref[...] = ((own_ref[...] + got_ref[0].astype(F32)) + got_ref[1].astype(F32)) + got_ref[2].astype(F32)

    return pl.pallas_call(
        body, name=name,
        grid_spec=pltpu.PrefetchScalarGridSpec(
            num_scalar_prefetch=1, grid=(nb,),
            in_specs=[pl.BlockSpec((rb, C), lambda i, pr: (i, 0)), pl.BlockSpec((3, rb, C), lambda i, pr: (0, i, 0))],
            out_specs=pl.BlockSpec((rb, C), lambda i, pr: (pr[1] * nb + i, 0))),
        out_shape=jax.ShapeDtypeStruct((2 * hr, C), F32),
        compiler_params=_cparams("arbitrary"),
    )(place, own, got)


def _adamw_math(w, g, m, v):
    m = ADAM_B1 * m + (1.0 - ADAM_B1) * g
    v = ADAM_B2 * v + (1.0 - ADAM_B2) * (g * g)
    m_hat = m / (1.0 - ADAM_B1 ** ADAM_STEP)
    v_hat = v / (1.0 - ADAM_B2 ** ADAM_STEP)
    delta = -ADAM_LR * (m_hat / (jnp.sqrt(v_hat) + ADAM_EPS) + ADAM_WD * w)
    return delta, m, v


def _adamw(name, w, g, m, v):
    R, Cw = w.shape
    Cg = g.shape[1]
    rb = _row_block(R, Cg)

    def body(w_ref, g_ref, m_ref, v_ref, g_o, d_o, m_o, v_o):
        gv = g_ref[...]
        delta, mn, vn = _adamw_math(w_ref[...], gv, m_ref[...], v_ref[...])
        g_o[...] = gv
        d_o[...] = delta
        m_o[...] = mn
        v_o[...] = vn

    blk = pl.BlockSpec((rb, Cg), lambda i: (i, 0))
    return pl.pallas_call(
        body, name=name, grid=(R // rb,),
        in_specs=[blk] * 4, out_specs=[blk] * 4,
        out_shape=[jax.ShapeDtypeStruct((R, Cw), F32)] * 4,
        compiler_params=_cparams("parallel"),
    )(w, g, m, v)


N_DEV = 8
SMALL_ROWS = 64


def _small_allreduce_adamw(g, w, m, v):
    def body(g_ref, w_ref, m_ref, v_ref, all_ref, gs_o, d_o, m_o, v_o, send_sems, recv_sems, local_sem):
        x, y, c, chips = _place()
        me, sibling = (x, y, c), (x, y, 1 - c)

        def rows(px, py, pc):
            return all_ref.at[pl.ds(pl.multiple_of((4 * px + 2 * py + pc) * SMALL_ROWS, SMALL_ROWS), SMALL_ROWS), :]

        def copy(k, block, to, src=None):
            return pltpu.make_async_remote_copy(
                src_ref=rows(*block) if src is None else src, dst_ref=rows(*block),
                send_sem=send_sems.at[k], recv_sem=recv_sems.at[k], device_id=to, device_id_type=MESH)

        mine = pltpu.make_async_copy(g_ref, rows(*me), local_sem)
        mine.start()
        first = [copy(0, me, sibling, src=g_ref)]
        first += [copy(1 + j, me, (*chip, c), src=g_ref) for j, chip in enumerate(chips)]
        for cp in first:
            cp.start()
        passed = [copy(4 + j, (*chip, c), sibling) for j, chip in enumerate(chips)]
        for j, chip in enumerate(chips):
            copy(1 + j, (*chip, c), me).wait_recv()
            passed[j].start()
        copy(0, sibling, me).wait_recv()
        for j, chip in enumerate(chips):
            copy(4 + j, (*chip, 1 - c), me).wait_recv()
        for cp in first + passed:
            cp.wait_send()
        mine.wait()

        tot = all_ref[0:SMALL_ROWS, :]
        for d in range(1, N_DEV):
            tot = tot + all_ref[d * SMALL_ROWS:(d + 1) * SMALL_ROWS, :]
        delta, mn, vn = _adamw_math(w_ref[...], tot, m_ref[...], v_ref[...])
        gs_o[...] = tot
        d_o[...] = delta
        m_o[...] = mn
        v_o[...] = vn

    vm = pl.BlockSpec(memory_space=pltpu.VMEM)
    shp = jax.ShapeDtypeStruct((SMALL_ROWS, LANES), F32)
    res = pl.pallas_call(
        body, name="small_allreduce_adamw", in_specs=[vm] * 4, out_specs=[vm] * 5,
        out_shape=[jax.ShapeDtypeStruct((N_DEV * SMALL_ROWS, LANES), F32), shp, shp, shp, shp],
        scratch_shapes=[pltpu.SemaphoreType.DMA((7,)), pltpu.SemaphoreType.DMA((7,)), pltpu.SemaphoreType.DMA],
    )(g, w, m, v)
    return res[1:]


SMALL_NAMES = ("g_mix", "b_gate", "b_forget", "qn_swa", "kn_swa", "sink_swa", "rel_bias", "qn_fox", "kn_fox",
               "g_mem", "qn_mem", "kn_mem", "g_mlp")
BIG_NAMES = ("w_in", "w_mem_kv", "w_o_swa", "w_o_fox", "w_o_mem", "w_out", "w_mlp_up", "w_mlp_down")
WEIGHT_NAMES = ("g_mix", "w_in", "b_gate", "b_forget", "qn_swa", "kn_swa", "sink_swa", "rel_bias", "qn_fox", "kn_fox",
                "g_mem", "w_mem_kv", "qn_mem", "kn_mem", "w_o_swa", "w_o_fox", "w_o_mem", "w_out", "g_mlp",
                "w_mlp_up", "w_mlp_down")


def _pack_small(parts, extra=None):
    rows = []
    for n in SMALL_NAMES:
        flat = parts[n].reshape(-1).astype(F32)
        flat = jnp.pad(flat, (0, (-flat.size) % LANES))
        rows.append(flat.reshape(-1, LANES))
    if extra is not None:
        rows.append(jnp.pad(extra.reshape(1, 1), ((0, 0), (0, LANES - 1))))
    packed = jnp.concatenate(rows, axis=0)
    return jnp.pad(packed, ((0, SMALL_ROWS - packed.shape[0]), (0, 0)))


def _unpack_small(packed, shapes):
    out, r = {}, 0
    for n in SMALL_NAMES:
        size = math.prod(shapes[n])
        nr = -(-size // LANES)
        out[n] = packed[r:r + nr].reshape(-1)[:size].reshape(shapes[n])
        r += nr
    return out, packed[r, 0]


W_IN_SEGMENTS = ((C_QA, 0, 512), (C_QF, 768, 512), (C_KF, 1280, 512), (C_VF, 1792, 512), (C_QM, 2312, 512),
                 (C_KA, 512, 128), (C_VA, 640, 128), (C_FL, 2304, FOX_HEADS), (C_GL, 2824, GATE_W))
RELAYOUT_ROWS = 256


def _permute_pieces(src_of_dst):
    blocks = []
    for b in range(len(src_of_dst) // LANES):
        runs, lane = [], 0
        while lane < LANES:
            src = src_of_dst[b * LANES + lane]
            if src is None:
                lane += 1
                continue
            plane, col = src
            end = lane + 1
            while (end < LANES and src_of_dst[b * LANES + end] == (plane, col + end - lane)
                   and (col + end - lane) // LANES == col // LANES):
                end += 1
            runs.append((plane, col // LANES, (lane - col) % LANES, lane, end))
            lane = end
        blocks.append(runs)
    return blocks


def _permuted_block(runs, load, rows):
    lane = _lane((rows, LANES))
    acc = jnp.zeros((rows, LANES), F32)
    for plane, blk, shift, lo, hi in runs:
        x = load(plane, blk).astype(F32)
        if shift:
            x = pltpu.roll(x, shift, 1)
        acc = x if (lo, hi) == (0, LANES) else jnp.where((lane >= lo) & (lane < hi), x, acc)
    return acc


def _w_in_to_segments(g_in):
    src_of_dst = [None] * PROJ_W
    for mine, theirs, width in W_IN_SEGMENTS:
        for k in range(width):
            src_of_dst[mine + k] = ((theirs + k) // IN_SHARD, (theirs + k) % IN_SHARD)
    blocks = _permute_pieces(src_of_dst)
    rb = RELAYOUT_ROWS

    def body(src_ref, out_ref):
        for b, runs in enumerate(blocks):
            blk = _permuted_block(runs, lambda p, c: src_ref[p, :, c * LANES:(c + 1) * LANES], rb)
            out_ref[:, b * LANES:(b + 1) * LANES] = blk.astype(out_ref.dtype)

    return pl.pallas_call(
        body, name="w_in_to_segments", grid=(D_MODEL // rb,),
        in_specs=[pl.BlockSpec((N_SHARD, rb, IN_SHARD_PAD), lambda i: (0, i, 0))],
        out_specs=pl.BlockSpec((rb, PROJ_W), lambda i: (i, 0)),
        out_shape=jax.ShapeDtypeStruct((D_MODEL, PROJ_W), g_in.dtype),
        compiler_params=_cparams("parallel", vmem=VMEM_MID),
    )(g_in)


def _w_in_from_segments(lo, gl):
    mine_of_theirs = {}
    for mine, theirs, width in W_IN_SEGMENTS:
        for k in range(width):
            mine_of_theirs[theirs + k] = mine + k
    src_of_dst = [None] * (N_SHARD * IN_SHARD_PAD)
    for s in range(N_SHARD):
        for l in range(IN_SHARD):
            j = mine_of_theirs[s * IN_SHARD + l]
            src_of_dst[s * IN_SHARD_PAD + l] = (j // LO_W, j % LO_W)
    blocks = _permute_pieces(src_of_dst)
    per_slot = IN_SHARD_PAD // LANES
    rb = RELAYOUT_ROWS

    def body(lo_ref, gl_ref, out_ref):
        planes = (lo_ref, gl_ref)
        for b, runs in enumerate(blocks):
            blk = _permuted_block(runs, lambda p, c: planes[p][:, c * LANES:(c + 1) * LANES], rb)
            c0 = (b % per_slot) * LANES
            out_ref[b // per_slot, :, c0:c0 + LANES] = blk

    half = pl.BlockSpec((rb, LO_W), lambda i: (i, 0))
    return pl.pallas_call(
        body, name="w_in_from_segments", grid=(D_MODEL // rb,),
        in_specs=[half, half],
        out_specs=pl.BlockSpec((N_SHARD, rb, IN_SHARD_PAD), lambda i: (0, i, 0)),
        out_shape=jax.ShapeDtypeStruct((N_SHARD, D_MODEL, IN_SHARD_PAD), F32),
        compiler_params=_cparams("parallel", vmem=VMEM_MID),
    )(lo, gl)


def _after(first, then):
    return lax.optimization_barrier((first, then))


class _ReduceGroup:
    def __init__(self, tag, first_collective_id, place):
        self.tag, self.first_id, self.place = tag, first_collective_id, place

    def start(self, local, tie):
        self.names = tuple(local)
        mine, tie = _after([local[n] for n in self.names], tie)
        self.mine = mine
        self.staged = _pair_exchange("pair_exchange_" + self.tag, self.first_id, mine)
        return tie

    def send(self, tie):
        staged, tie = _after(self.staged, tie)
        sums = [_pair_sum("pair_sum_" + n, self.place, g, st) for n, g, st in zip(self.names, self.mine, staged)]
        travel, tie = _after([s[0] for s in sums], tie)
        self.own = [s[1] for s in sums]
        self.got = _chip_exchange("chip_exchange_" + self.tag, self.first_id + 1, travel)
        return tie

    def finish(self, tie):
        got, tie = _after(self.got, tie)
        halves = [_final_sum("final_sum_" + n, self.place, o, r) for n, o, r in zip(self.names, self.own, got)]
        halves, tie = _after(halves, tie)
        summed = _pair_gather("pair_gather_" + self.tag, self.first_id + 2, halves)
        self.summed = dict(zip(self.names, summed))
        return tie


class _GradReducer:
    def __init__(self, place):
        self.early = _ReduceGroup("early", 2, place)
        self.late = _ReduceGroup("late", 5, place)

    @staticmethod
    def _slot_rows(a):
        return a.reshape(N_SHARD, a.shape[0] // N_SHARD, a.shape[1])

    def early_start(self, g, tie):
        return self.early.start({"w_mlp_down": self._slot_rows(g["w_mlp_down"]), "w_mlp_up": g["w_mlp_up"],
                                 "w_out": self._slot_rows(g["w_out"]), "w_mem_kv": self._slot_rows(g["w_mem_kv"]),
                                 "w_o_swa": g["w_o_swa"], "w_o_fox": g["w_o_fox"], "w_o_mem": g["w_o_mem"]}, tie)

    def early_send(self, tie):
        return self.early.send(tie)

    def early_finish(self, tie):
        return self.early.finish(tie)

    def late_start(self, g, tie):
        d_in = _w_in_from_segments(g["wc_lo"], g["wc_gl"])
        return self.late.start({"w_in": d_in}, tie)

    def late_send(self, tie):
        return self.late.send(tie)

    def late_finish(self, tie):
        return self.late.finish(tie)

    @property
    def summed(self):
        return {**self.early.summed, **self.late.summed}


def kernel(x, mem, g_mix, w_in, b_gate, b_forget, qn_swa, kn_swa, sink_swa, rel_bias, qn_fox, kn_fox, g_mem, w_mem_kv, qn_mem, kn_mem, w_o_swa, w_o_fox, w_o_mem, w_out, g_mlp, w_mlp_up, w_mlp_down, loss_target, m_g_mix, m_w_in, m_b_gate, m_b_forget, m_qn_swa, m_kn_swa, m_sink_swa, m_rel_bias, m_qn_fox, m_kn_fox, m_g_mem, m_w_mem_kv, m_qn_mem, m_kn_mem, m_w_o_swa, m_w_o_fox, m_w_o_mem, m_w_out, m_g_mlp, m_w_mlp_up, m_w_mlp_down, v_g_mix, v_w_in, v_b_gate, v_b_forget, v_qn_swa, v_kn_swa, v_sink_swa, v_rel_bias, v_qn_fox, v_kn_fox, v_g_mem, v_w_mem_kv, v_qn_mem, v_kn_mem, v_w_o_swa, v_w_o_fox, v_w_o_mem, v_w_out, v_g_mlp, v_w_mlp_up, v_w_mlp_down):
    given = dict(locals())
    W = {n: given[n] for n in WEIGHT_NAMES}
    M = {n: given["m_" + n] for n in WEIGHT_NAMES}
    V = {n: given["v_" + n] for n in WEIGHT_NAMES}
    pad_in = ((0, 0), (0, IN_SHARD_PAD - IN_SHARD))

    shards = [jnp.pad(w_in[0].astype(BF16), pad_in)] + [W[n][0].astype(BF16) for n in BIG_NAMES[1:]]
    slots = [jnp.broadcast_to(s[None], (N_SHARD,) + s.shape) for s in shards]
    (g_in,) = _all_gather_shards_async("all_gather_w_in", 1, slots[:1])
    small = {n: (W[n] if n == "rel_bias" else W[n].reshape(1, -1)) for n in SMALL_NAMES}
    h = _rmsnorm("rms_mix", x[0], small["g_mix"], min(512, x.shape[1]))
    g_in, late, h, (m_in, v_in) = lax.optimization_barrier((g_in, slots[1:], h, (M["w_in"][0], V["w_in"][0])))
    M["w_in"], V["w_in"] = m_in[None], v_in[None]
    g_kv, g_oa, g_of, g_om, g_out, g_up, g_down = _all_gather_shards_async("all_gather_weights_async", 8, late)

    place = jnp.stack([2 * lax.axis_index("x") + lax.axis_index("y"), lax.axis_index("c")]).astype(jnp.int32)
    reducer = _GradReducer(place)
    loss, grad_x, grads = _local_step(
        x[0], h, mem[0], loss_target[0], small, g_in, g_kv.reshape(D_MODEL, D_MODEL), (g_oa, g_of, g_om),
        g_out.reshape(D_MODEL, D_MODEL), g_up, g_down.reshape(D_FF, D_MODEL), reducer)

    out = {}

    def adamw_of(names, summed):
        for n in names:
            res = _adamw("adamw_" + n, W[n][0], summed[n], M[n][0], V[n][0])
            out[n] = [r.reshape(W[n].shape) for r in res]

    adamw_of(reducer.early.names, reducer.early.summed)
    shapes = {n: W[n].shape for n in SMALL_NAMES}
    packed = _small_allreduce_adamw(_pack_small(grads, loss), _pack_small(W), _pack_small(M), _pack_small(V))
    done_meanwhile = ([out[n] for n in reducer.early.names], packed)
    (early_out, packed), grad_x = reducer.late_finish((done_meanwhile, grad_x))
    for n, res in zip(reducer.early.names, early_out):
        out[n] = res
    adamw_of(reducer.late.names, reducer.late.summed)
    unpacked = [_unpack_small(p, shapes) for p in packed]
    for n in SMALL_NAMES:
        out[n] = [u[0][n] for u in unpacked]
    loss_total = unpacked[0][1]

    return (loss_total, grad_x.reshape(x.shape),
            *[out[n][0] for n in WEIGHT_NAMES], *[out[n][1] for n in WEIGHT_NAMES],
            *[out[n][2] for n in WEIGHT_NAMES], *[out[n][3] for n in WEIGHT_NAMES])
```

```python
import functools
import math

import jax
import jax.numpy as jnp
from jax import lax
from jax.experimental import pallas as pl
from jax.experimental.pallas import tpu as pltpu
from jax.experimental.pallas import tpu_sc as plsc

F32 = jnp.float32
BF16 = jnp.bfloat16

D_MODEL = 1024
N_MEM = 256
SWA_HEADS = 8
SWA_KV_HEADS = 2
SWA_HEAD_DIM = 64
WINDOW = 128
FOX_HEADS = 8
FOX_HEAD_DIM = 64
MEM_HEADS = 4
MEM_HEAD_DIM = 128
D_FF = 4 * D_MODEL
REL_BUCKETS = 32
REL_MAX_DIST = 128
EPS = 1e-6
NEG = -1e30
GATE_W = 3 * D_MODEL
IN_WIDTH = 5896
N_SHARD = 4
IN_SHARD = IN_WIDTH // N_SHARD
IN_SHARD_PAD = 1536

ADAM_LR = 0.001
ADAM_B1 = 0.9
ADAM_B2 = 0.999
ADAM_EPS = 1e-08
ADAM_WD = 0.01
ADAM_STEP = 10

LANES = 128
V7X_VMEM_BYTES = 64 * 1024 * 1024
MIB = 1024 * 1024
VMEM_SMALL, VMEM_MID, VMEM_BIG, VMEM_MAX = 48 * MIB, 48 * MIB, 48 * MIB, 56 * MIB

C_QA, C_QF, C_KF, C_VF, C_QM, C_KA, C_VA, C_FL, C_GL = 0, 512, 1024, 1536, 2048, 2560, 2688, 2816, 3072
LO_W = 3072
PROJ_W = 6144

NN = (((1,), (0,)), ((), ()))
NT = (((1,), (1,)), ((), ()))
TN = (((0,), (0,)), ((), ()))


def _dot(a, b, dims=NN):
    return lax.dot_general(a, b, dims, preferred_element_type=F32)


def _cparams(*sem, vmem=VMEM_SMALL):
    return pltpu.CompilerParams(dimension_semantics=sem, vmem_limit_bytes=vmem)


def _split3(a):
    hi = a.astype(BF16)
    r1 = a - hi.astype(F32)
    mid = r1.astype(BF16)
    lo = (r1 - mid.astype(F32)).astype(BF16)
    return hi, mid, lo


def _group_mean(a, g2):
    hi = a.astype(BF16)
    mid = (a - hi.astype(F32)).astype(BF16)
    return _dot(jnp.concatenate([hi, mid], axis=1), g2)


def _dot3_left(g, a):
    hi, mid, lo = _split3(a)
    return _dot(g, hi) + _dot(g, mid) + _dot(g, lo)


def _group_mean_matrix(d):
    r = jnp.arange(LANES)
    g = jnp.where((r[:, None] // d) == (r[None, :] // d), 1.0 / d, 0.0).astype(BF16)
    return jnp.concatenate([g, g], axis=0)


def _lane(shape):
    return lax.broadcasted_iota(jnp.int32, shape, len(shape) - 1)


def _matmul(name, a, b, *, dims, grid, a_spec, b_spec, acc_shape, outs, epilogue, extra=(), vmem=VMEM_BIG):
    nk = grid[2]
    n_extra = len(extra)

    def body(a_ref, b_ref, *rest):
        extra_refs = rest[:n_extra]
        out_refs = rest[n_extra:n_extra + len(outs)]
        i, j, k = pl.program_id(0), pl.program_id(1), pl.program_id(2)
        if nk == 1:
            epilogue(_dot(a_ref[...].astype(BF16), b_ref[...].astype(BF16), dims), extra_refs, out_refs, (i, j))
            return
        acc_ref = rest[-1]

        @pl.when(k == 0)
        def _():
            acc_ref[...] = jnp.zeros_like(acc_ref)

        acc_ref[...] += _dot(a_ref[...].astype(BF16), b_ref[...].astype(BF16), dims)

        @pl.when(k == nk - 1)
        def _():
            epilogue(acc_ref[...], extra_refs, out_refs, (i, j))

    res = pl.pallas_call(
        body,
        name=name,
        grid=grid,
        in_specs=[a_spec, b_spec] + [s for _, s in extra],
        out_specs=[s for _, s in outs],
        out_shape=[s for s, _ in outs],
        scratch_shapes=[pltpu.VMEM(acc_shape, F32)] if nk > 1 else [],
        compiler_params=_cparams("arbitrary", "arbitrary", "arbitrary", vmem=vmem),
    )(a, b, *[x for x, _ in extra])
    return res


def _epi_store(acc, extra_refs, out_refs, ij):
    out_refs[0][...] = acc.astype(out_refs[0].dtype)


def _rms_rows(x, g):
    r = lax.rsqrt(jnp.mean(x * x, axis=-1, keepdims=True) + EPS)
    return x * r, r


def _rmsnorm_bwd_rows(dh, x, g):
    xhat, r = _rms_rows(x, g)
    dxh = dh * g
    dx = r * (dxh - xhat * jnp.mean(dxh * xhat, axis=-1, keepdims=True))
    return dx, jnp.sum(dh * xhat, axis=0, keepdims=True)


def _rmsnorm(name, x, g, tb):
    T, Dm = x.shape

    def body(x_ref, g_ref, o_ref):
        xhat, _ = _rms_rows(x_ref[...], None)
        o_ref[...] = (xhat * g_ref[...]).astype(o_ref.dtype)

    return pl.pallas_call(
        body, name=name, grid=(T // tb,),
        in_specs=[pl.BlockSpec((tb, Dm), lambda i: (i, 0)), pl.BlockSpec((1, Dm), lambda i: (0, 0))],
        out_specs=pl.BlockSpec((tb, Dm), lambda i: (i, 0)),
        out_shape=jax.ShapeDtypeStruct((T, Dm), BF16),
        compiler_params=_cparams("parallel"),
    )(x, g)


def _head_norm(x, gm, gain):
    ms = _group_mean(x * x, gm)
    r = lax.rsqrt(ms + EPS)
    return x * r * gain, x * r


def _head_norm_bwd(dy, x, gm, gain):
    ms = _group_mean(x * x, gm)
    r = lax.rsqrt(ms + EPS)
    xhat = x * r
    dxh = dy * gain
    dx = r * (dxh - xhat * _group_mean(dxh * xhat, gm))
    return dx, jnp.sum(dy * xhat, axis=0, keepdims=True)


def _log_sigmoid(z):
    return jnp.minimum(z, 0.0) - jnp.log(1.0 + jnp.exp(-jnp.abs(z)))


def _prep_fwd(proj, gains, bfor, tril, gm64, gm128, T, tb):
    nb = T // tb

    def body(qa_ref, qf_ref, kf_ref, vf_ref, qm_ref, ka_ref, va_ref, fl_ref, gains_ref, bfor_ref, tril_ref,
             gm64_ref, gm128_ref,
             qa_o, qf_o, kf_o, vf_o, qm_o, kad_o, vad_o, qaug_o, kaug_o, carry):
        i = pl.program_id(0)
        gm64v = gm64_ref[...]
        gm128v = gm128_ref[...]
        lane = _lane((tb, LANES))

        def norm512(src, dst, row, gm, scale=1.0):
            gain = gains_ref[row:row + 1, :]
            for c in range(4):
                sl = slice(c * LANES, (c + 1) * LANES)
                y, _ = _head_norm(src[:, sl], gm, gain)
                dst[:, sl] = (y * scale).astype(dst.dtype)

        norm512(qa_ref, qa_o, 0, gm64v)
        norm512(qf_ref, qf_o, 2, gm64v, FOX_SCALE)
        norm512(kf_ref, kf_o, 3, gm64v)
        norm512(qm_ref, qm_o, 4, gm128v)
        vf_o[...] = vf_ref[...].astype(vf_o.dtype)

        ka_n, _ = _head_norm(ka_ref[...], gm64v, gains_ref[1:2, :])
        ka_r = pltpu.roll(ka_n, 64, 1)
        va = va_ref[...]
        va_r = pltpu.roll(va, 64, 1)
        lo = lane < 64
        kad_o[0] = jnp.where(lo, ka_n, ka_r).astype(kad_o.dtype)
        kad_o[1] = jnp.where(lo, ka_r, ka_n).astype(kad_o.dtype)
        vad_o[0] = jnp.where(lo, va, va_r).astype(vad_o.dtype)
        vad_o[1] = jnp.where(lo, va_r, va).astype(vad_o.dtype)

        @pl.when(i == 0)
        def _():
            carry[...] = jnp.zeros_like(carry)

        logf = jnp.where(lane < FOX_HEADS, _log_sigmoid(fl_ref[...] + bfor_ref[...]), 0.0)
        c = _dot3_left(tril_ref[...], logf) + carry[0:1, :]
        carry[...] = jnp.broadcast_to(c[tb - 1:tb, :], carry.shape)
        for pair in range(FOX_HEADS // 2):
            qaug = jnp.zeros((tb, LANES), F32)
            kaug = jnp.zeros((tb, LANES), F32)
            for sub in range(2):
                col = jnp.sum(jnp.where(lane == 2 * pair + sub, c, 0.0), axis=1, keepdims=True)
                pieces = [p.astype(F32) for p in _split3(col)]
                base = AUG_STRIDE * sub
                for e in range(3):
                    qaug = jnp.where(lane == base + AUG_C + e, pieces[e], qaug)
                    kaug = jnp.where(lane == base + AUG_NEG_C + e, -pieces[e], kaug)
                qaug = jnp.where((lane >= base + AUG_NEG_C) & (lane < base + AUG_NEG_C + 3), 1.0, qaug)
                ones_k = ((lane >= base + AUG_C) & (lane < base + AUG_C + 3)) | (
                    (lane >= base + AUG_STAT) & (lane < base + AUG_STAT + 3))
                kaug = jnp.where(ones_k, 1.0, kaug)
            sl = slice(pair * LANES, (pair + 1) * LANES)
            qaug_o[:, sl] = qaug.astype(BF16)
            kaug_o[:, sl] = kaug.astype(BF16)

    def seg(width, start):
        return pl.BlockSpec((tb, width), lambda i, s=start // width: (i, s))

    const = lambda shape: pl.BlockSpec(shape, lambda i: tuple(0 for _ in shape))
    rows512 = pl.BlockSpec((tb, 512), lambda i: (i, 0))
    outs = pl.pallas_call(
        body, name="prep_fwd", grid=(nb,),
        in_specs=[seg(512, C_QA), seg(512, C_QF), seg(512, C_KF), seg(512, C_VF), seg(512, C_QM),
                  seg(128, C_KA), seg(128, C_VA), seg(128, C_FL),
                  const((8, LANES)), const((1, LANES)), const((tb, tb)), const((2 * LANES, LANES)), const((2 * LANES, LANES))],
        out_specs=[rows512, rows512, rows512, rows512, rows512,
                   pl.BlockSpec((2, tb, LANES), lambda i: (0, i, 0)), pl.BlockSpec((2, tb, LANES), lambda i: (0, i, 0)),
                   rows512, rows512],
        out_shape=[jax.ShapeDtypeStruct((T, 512), BF16)] * 5
        + [jax.ShapeDtypeStruct((2, T, LANES), BF16)] * 2
        + [jax.ShapeDtypeStruct((T, 512), BF16)] * 2,
        scratch_shapes=[pltpu.VMEM((8, LANES), F32)],
        compiler_params=_cparams("arbitrary", vmem=VMEM_MID),
    )(proj, proj, proj, proj, proj, proj, proj, proj, gains, bfor, tril, gm64, gm128)
    return outs


def _prep_bwd(proj, dqa, dkad, dvad, dqf, dkf, dvf, dqm, dqf_aug, dkf_aug, gains, bfor, triu, gm64, gm128, T, tb):
    nb = T // tb

    def body(qa_ref, qf_ref, kf_ref, qm_ref, ka_ref, fl_ref,
             dqa_ref, dkad_ref, dvad_ref, dqf_ref, dkf_ref, dvf_ref, dqm_ref, dqfa_ref, dkfa_ref,
             gains_ref, bfor_ref, triu_ref, gm64_ref, gm128_ref,
             dlo_o, gacc_o, carry):
        i = pl.program_id(0)
        gm64v = gm64_ref[...]
        gm128v = gm128_ref[...]
        lane = _lane((tb, LANES))

        @pl.when(i == 0)
        def _():
            carry[...] = jnp.zeros_like(carry)
            gacc_o[...] = jnp.zeros_like(gacc_o)

        def norm512_bwd(dsrc, xsrc, col0, row, gm):
            gain = gains_ref[row:row + 1, :]
            gsum = jnp.zeros((1, LANES), F32)
            for c in range(4):
                sl = slice(c * LANES, (c + 1) * LANES)
                dx, dg = _head_norm_bwd(dsrc[:, sl], xsrc[:, sl], gm, gain)
                dlo_o[:, col0 + c * LANES:col0 + (c + 1) * LANES] = dx.astype(dlo_o.dtype)
                gsum = gsum + dg
            gacc_o[row:row + 1, :] += gsum

        norm512_bwd(dqa_ref, qa_ref, C_QA, 0, gm64v)
        norm512_bwd(dqf_ref, qf_ref, C_QF, 2, gm64v)
        norm512_bwd(dkf_ref, kf_ref, C_KF, 3, gm64v)
        norm512_bwd(dqm_ref, qm_ref, C_QM, 4, gm128v)
        dlo_o[:, C_VF:C_VF + 512] = dvf_ref[...].astype(dlo_o.dtype)

        lo = lane < 64

        def fold(ref):
            f0 = ref[0] + pltpu.roll(ref[0], 64, 1)
            f1 = ref[1] + pltpu.roll(ref[1], 64, 1)
            return jnp.where(lo, f0, f1)

        dka, dg = _head_norm_bwd(fold(dkad_ref), ka_ref[...], gm64v, gains_ref[1:2, :])
        gacc_o[1:2, :] += dg
        dlo_o[:, C_KA:C_KA + LANES] = dka.astype(dlo_o.dtype)
        dlo_o[:, C_VA:C_VA + LANES] = fold(dvad_ref).astype(dlo_o.dtype)

        dc = jnp.zeros((tb, LANES), F32)
        for pair in range(FOX_HEADS // 2):
            sl = slice(pair * LANES, (pair + 1) * LANES)
            rows_sum, cols_sum = dqfa_ref[:, sl], dkfa_ref[:, sl]
            for sub in range(2):
                diff = (jnp.where(lane == AUG_STRIDE * sub + AUG_C, rows_sum, 0.0)
                        - jnp.where(lane == AUG_STRIDE * sub + AUG_NEG_C, cols_sum, 0.0))
                dc = jnp.where(lane == 2 * pair + sub, jnp.sum(diff, axis=1, keepdims=True), dc)
        dlogf = _dot3_left(triu_ref[...], dc) + carry[0:1, :]
        carry[...] = jnp.broadcast_to(dlogf[0:1, :], carry.shape)
        z = fl_ref[...] + bfor_ref[...]
        dfl = jnp.where(lane < FOX_HEADS, dlogf / (1.0 + jnp.exp(z)), 0.0)
        gacc_o[5:6, :] += jnp.sum(dfl, axis=0, keepdims=True)
        dlo_o[:, C_FL:C_FL + LANES] = dfl.astype(dlo_o.dtype)
        dlo_o[:, C_FL + LANES:C_FL + 2 * LANES] = jnp.zeros((tb, LANES), dlo_o.dtype)

    rev = lambda i: nb - 1 - i

    def seg(width, start):
        return pl.BlockSpec((tb, width), lambda i, s=start // width: (rev(i), s))

    const = lambda shape: pl.BlockSpec(shape, lambda i: tuple(0 for _ in shape))
    rows512 = pl.BlockSpec((tb, 512), lambda i: (rev(i), 0))
    dup = pl.BlockSpec((2, tb, LANES), lambda i: (0, rev(i), 0))
    return pl.pallas_call(
        body, name="prep_bwd", grid=(nb,),
        in_specs=[seg(512, C_QA), seg(512, C_QF), seg(512, C_KF), seg(512, C_QM), seg(128, C_KA), seg(128, C_FL),
                  rows512, dup, dup, rows512, rows512, rows512, rows512, rows512, rows512,
                  const((8, LANES)), const((1, LANES)), const((tb, tb)), const((2 * LANES, LANES)), const((2 * LANES, LANES))],
        out_specs=[pl.BlockSpec((tb, LO_W), lambda i: (rev(i), 0)), const((8, LANES))],
        out_shape=[jax.ShapeDtypeStruct((T, LO_W), BF16), jax.ShapeDtypeStruct((8, LANES), F32)],
        scratch_shapes=[pltpu.VMEM((8, LANES), F32)],
        compiler_params=_cparams("arbitrary", vmem=VMEM_MID),
    )(proj, proj, proj, proj, proj, proj, dqa, dkad, dvad, dqf, dkf, dvf, dqm, dqf_aug, dkf_aug,
      gains, bfor, triu, gm64, gm128)


FOX_SCALE = FOX_HEAD_DIM ** -0.5
AUG_STRIDE = 16
AUG_C = 0
AUG_NEG_C = 3
AUG_STAT = 6
FOX_TQ, FOX_TK = 1024, 1024
FOX_BWD_TQ, FOX_BWD_TK = 1024, 1024
FOX_DIAGONAL_PARTS = 4


def _fox_head_mask(sub, rows):
    lane = _lane((rows, 2 * LANES))
    main = (lane >= 64 * sub) & (lane < 64 * sub + 64)
    aug = (lane >= LANES + AUG_STRIDE * sub) & (lane < LANES + AUG_STRIDE * (sub + 1))
    return main | aug


def _fox_pieces(diagonal, tq, tk):
    if diagonal and tq == tk and tq >= FOX_DIAGONAL_PARTS * LANES:
        step = tq // FOX_DIAGONAL_PARTS
        return [(n * step, (n + 1) * step, (n + 1) * step) for n in range(FOX_DIAGONAL_PARTS)]
    return [(0, tq, tk)]


def _fox_fwd(q, qaug, k, kaug, v, T, tq, tk):
    nq, nk = T // tq, T // tk
    rep = tk // LANES
    last_of = lambda i: (i * tq + tq - 1) // tk

    def body(q_ref, qa_ref, k_ref, ka_ref, v_ref, o_ref, qab_ref, m_s, acc_s):
        p_, i, j = pl.program_id(0), pl.program_id(1), pl.program_id(2)
        last = last_of(i)

        @pl.when(j == 0)
        def _():
            m_s[...] = jnp.full(m_s.shape, NEG, F32)
            acc_s[...] = jnp.zeros_like(acc_s)

        def step(diagonal):
            k2 = jnp.concatenate([k_ref[...], ka_ref[...]], axis=1)
            v2 = jnp.concatenate([v_ref[...], ka_ref[...]], axis=1)
            pieces = _fox_pieces(diagonal, tq, tk)
            work = []
            for r0, r1, nc in pieces:
                rows = slice(r0, r1)
                q2 = jnp.concatenate([q_ref[rows, :], qa_ref[rows, :]], axis=1)
                for sub in range(2):
                    qh = jnp.where(_fox_head_mask(sub, r1 - r0), q2, jnp.zeros_like(q2))
                    work.append((rows, r0, r1 - r0, nc, sub, _dot(qh, k2[:nc], NT)))
            for rows, r0, nr, nc, sub, s in work:
                if diagonal:
                    causal = (lax.broadcasted_iota(jnp.int32, (nr, nc), 1) + j * tk
                              <= lax.broadcasted_iota(jnp.int32, (nr, nc), 0) + (r0 + i * tq))
                    s = jnp.where(causal, s, NEG)
                m_prev = m_s[sub, rows, :]
                m_next = jnp.maximum(m_prev, jnp.max(s, axis=1, keepdims=True))
                p = jnp.exp(s - jnp.tile(m_next, (1, nc // LANES)))
                alpha = jnp.exp(m_prev - m_next)
                m_s[sub, rows, :] = m_next
                acc_s[sub, rows, :] = acc_s[sub, rows, :] * jnp.tile(alpha, (1, 2)) + _dot(p.astype(BF16), v2[:nc])

        @pl.when(j == last)
        def _():
            step(True)

        @pl.when(j < last)
        def _():
            step(False)

        @pl.when(j == nk - 1)
        def _():
            lane = _lane((tq, LANES))
            outs = []
            qab = qa_ref[...].astype(F32)
            for sub in range(2):
                acc = acc_s[sub]
                base = AUG_STRIDE * sub
                l = jnp.sum(jnp.where(lane == base + AUG_C, acc[:, LANES:], 0.0), axis=1, keepdims=True)
                outs.append(acc[:, :LANES] / l)
                lse = jnp.max(m_s[sub], axis=1, keepdims=True) + jnp.log(l)
                pieces = _split3(-lse)
                for e in range(3):
                    qab = jnp.where(lane == base + AUG_STAT + e, pieces[e].astype(F32), qab)
            o_ref[...] = jnp.where(lane < 64, outs[0], outs[1]).astype(o_ref.dtype)
            qab_ref[...] = qab.astype(BF16)

    qspec = pl.BlockSpec((tq, LANES), lambda p, i, j: (i, p))
    kspec = pl.BlockSpec((tk, LANES), lambda p, i, j: (jnp.minimum(j, last_of(i)), p))
    return pl.pallas_call(
        body, name="fox_fwd", grid=(4, nq, nk),
        in_specs=[qspec, qspec, kspec, kspec, kspec],
        out_specs=[qspec, qspec],
        out_shape=[jax.ShapeDtypeStruct((T, 512), BF16), jax.ShapeDtypeStruct((T, 512), BF16)],
        scratch_shapes=[pltpu.VMEM((2, tq, LANES), F32), pltpu.VMEM((2, tq, 2 * LANES), F32)],
        compiler_params=_cparams("parallel", "parallel", "arbitrary", vmem=VMEM_BIG),
    )(q, qaug, k, kaug, v)


def _fox_bwd(q, qaug, k, kaug, v, do, doaug, T, tq, tk):
    nq, nk = T // tq, T // tk
    first_of = lambda j: (j * tk) // tq

    def body(q_ref, qa_ref, k_ref, ka_ref, v_ref, do_ref, doa_ref,
             dq_ref, dqa_ref, dk_ref, dka_ref, dv_ref, dk_s, dv_s):
        p_, j, i = pl.program_id(0), pl.program_id(1), pl.program_id(2)
        masked = i * tq < (j + 1) * tk - 1

        @pl.when((j == 0) & (i == 0))
        def _():
            dq_ref[...] = jnp.zeros_like(dq_ref)
            dqa_ref[...] = jnp.zeros_like(dqa_ref)

        @pl.when(i == 0)
        def _():
            dk_s[...] = jnp.zeros_like(dk_s)
            dv_s[...] = jnp.zeros_like(dv_s)

        def step(diagonal):
            k2 = jnp.concatenate([k_ref[...], ka_ref[...]], axis=1)
            v2 = jnp.concatenate([v_ref[...], ka_ref[...]], axis=1)
            work = []
            for r0, r1, nc in _fox_pieces(diagonal, tq, tk):
                rows = slice(r0, r1)
                q2 = jnp.concatenate([q_ref[rows, :], qa_ref[rows, :]], axis=1)
                do2 = jnp.concatenate([do_ref[rows, :], doa_ref[rows, :]], axis=1)
                for sub in range(2):
                    hm = _fox_head_mask(sub, r1 - r0)
                    qh = jnp.where(hm, q2, jnp.zeros_like(q2))
                    doh = jnp.where(hm, do2, jnp.zeros_like(do2))
                    s = _dot(qh, k2[:nc], NT)
                    dp = _dot(doh, v2[:nc], NT)
                    work.append((r0, r1 - r0, nc, sub, qh, doh, s, dp))
            dqs = {}
            for r0, nr, nc, sub, qh, doh, s, dp in work:
                if diagonal:
                    causal = (lax.broadcasted_iota(jnp.int32, (nr, nc), 1) + j * tk
                              <= lax.broadcasted_iota(jnp.int32, (nr, nc), 0) + (r0 + i * tq))
                    s = jnp.where(causal, s, NEG)
                p = jnp.exp(s)
                dsb = (p * dp).astype(BF16)
                dv_s[0:nc, :] += _dot(p.astype(BF16), doh[:, :LANES], TN)
                dk_s[0:nc, :] += _dot(dsb, qh, TN)
                dqs[(r0, sub)] = _dot(dsb, k2[:nc])
            for r0, r1, nc in _fox_pieces(diagonal, tq, tk):
                dq2 = jnp.where(_fox_head_mask(0, r1 - r0), dqs[(r0, 0)], dqs[(r0, 1)])
                qrows = pl.ds(pl.multiple_of(i * tq + r0, r1 - r0), r1 - r0)
                dq_ref[qrows, :] += dq2[:, :LANES] * FOX_SCALE
                dqa_ref[qrows, :] += dq2[:, LANES:]

        @pl.when((i >= first_of(j)) & masked)
        def _():
            step(True)

        @pl.when((i >= first_of(j)) & jnp.logical_not(masked))
        def _():
            step(False)

        @pl.when(i == nq - 1)
        def _():
            dk_ref[...] = dk_s[:, :LANES]
            dka_ref[...] = dk_s[:, LANES:]
            dv_ref[...] = dv_s[...]

    qspec = pl.BlockSpec((tq, LANES), lambda p, j, i: (jnp.maximum(i, first_of(j)), p))
    kspec = pl.BlockSpec((tk, LANES), lambda p, j, i: (j, p))
    resident = pl.BlockSpec((T, LANES), lambda p, j, i: (0, p))
    return pl.pallas_call(
        body, name="fox_bwd", grid=(4, nk, nq),
        in_specs=[qspec, qspec, kspec, kspec, kspec, qspec, qspec],
        out_specs=[resident, resident, kspec, kspec, kspec],
        out_shape=[jax.ShapeDtypeStruct((T, 512), F32)] * 5,
        scratch_shapes=[pltpu.VMEM((tk, 2 * LANES), F32), pltpu.VMEM((tk, LANES), F32)],
        compiler_params=_cparams("arbitrary", "arbitrary", "arbitrary", vmem=VMEM_BIG),
    )(q, qaug, k, kaug, v, do, doaug)


SWA_SUB = 16
SWA_TB = SWA_SUB * WINDOW


def _t5_bucket_matrix():
    t = jnp.arange(WINDOW)[:, None] + WINDOW
    s = jnp.arange(2 * WINDOW)[None, :]
    max_exact = REL_BUCKETS // 2
    d = jnp.maximum(t - s, 0)
    df = jnp.maximum(d, 1).astype(F32)
    large = max_exact + (jnp.log(df / max_exact) / math.log(REL_MAX_DIST / max_exact)
                         * (REL_BUCKETS - max_exact)).astype(jnp.int32)
    large = jnp.minimum(large, REL_BUCKETS - 1)
    return jnp.where(d < max_exact, d, large).astype(jnp.int32)


def _swa_bias(rel_bias, bucket):
    def body(rel_ref, bucket_ref, o_ref):
        b = bucket_ref[...]
        for h in range(SWA_HEADS):
            acc = jnp.zeros(b.shape, F32)
            for r in range(REL_BUCKETS):
                acc = jnp.where(b == r, rel_ref[r, h], acc)
            o_ref[h] = acc

    return pl.pallas_call(
        body, name="swa_bias",
        in_specs=[pl.BlockSpec(memory_space=pltpu.SMEM), pl.BlockSpec(memory_space=pltpu.VMEM)],
        out_specs=pl.BlockSpec(memory_space=pltpu.VMEM),
        out_shape=jax.ShapeDtypeStruct((SWA_HEADS, WINDOW, 2 * WINDOW), F32),
    )(rel_bias, bucket)


def _swa_bias_bwd(dbias, bucket):
    def body(db_ref, bucket_ref, o_ref):
        b = bucket_ref[...]
        lane = _lane((1, LANES))
        for r in range(REL_BUCKETS):
            row = jnp.zeros((1, LANES), F32)
            for h in range(SWA_HEADS):
                part = jnp.sum(jnp.where(b == r, db_ref[h], 0.0), axis=0, keepdims=True)
                tot = jnp.sum(part, axis=1, keepdims=True)
                row = jnp.where(lane == h, tot, row)
            o_ref[r:r + 1, :] = row

    return pl.pallas_call(
        body, name="swa_bias_bwd",
        in_specs=[pl.BlockSpec(memory_space=pltpu.VMEM), pl.BlockSpec(memory_space=pltpu.VMEM)],
        out_specs=pl.BlockSpec(memory_space=pltpu.VMEM),
        out_shape=jax.ShapeDtypeStruct((REL_BUCKETS, LANES), F32),
    )(dbias, bucket)


SWA_GROUP = SWA_HEADS // SWA_KV_HEADS


def _swa_valid(r, i):
    t = (lax.broadcasted_iota(jnp.int32, (SWA_GROUP * WINDOW, 2 * WINDOW), 0) & (WINDOW - 1)) + WINDOW
    s = lax.broadcasted_iota(jnp.int32, (SWA_GROUP * WINDOW, 2 * WINDOW), 1)
    dist = t - s
    band = (dist >= 0) & (dist < WINDOW)
    if r == 0:
        band = band & ((s >= WINDOW) | (i > 0))
    return band


def _swa_stack(blk):
    lane = _lane((WINDOW, LANES))
    parts = []
    for g in range(SWA_GROUP):
        b = blk[:, LANES * (g // 2):LANES * (g // 2 + 1)]
        parts.append(jnp.where((lane >= 64) if g % 2 else (lane < 64), b, jnp.zeros_like(b)))
    return jnp.concatenate(parts, axis=0)


def _swa_unstack(st):
    lane = _lane((WINDOW, LANES))
    W = WINDOW
    return jnp.concatenate([jnp.where(lane < 64, st[2 * b * W:(2 * b + 1) * W], st[(2 * b + 1) * W:(2 * b + 2) * W])
                            for b in range(2)], axis=1)


def _swa_sink_column(sink_ref, kvh):
    row = lax.broadcasted_iota(jnp.int32, (SWA_GROUP * WINDOW, 1), 0)
    col = jnp.full((SWA_GROUP * WINDOW, 1), sink_ref[SWA_GROUP * kvh + SWA_GROUP - 1], F32)
    for g in range(SWA_GROUP - 2, -1, -1):
        col = jnp.where(row < (g + 1) * WINDOW, sink_ref[SWA_GROUP * kvh + g], col)
    return col


def _swa_specs(T):
    W = WINDOW
    qspec = pl.BlockSpec((SWA_TB, 2 * LANES), lambda h, i: (i, h))
    own = pl.BlockSpec((None, SWA_TB, LANES), lambda h, i: (h, i, 0))
    prev = pl.BlockSpec((None, W, LANES), lambda h, i: (h, jnp.maximum(SWA_SUB * i - 1, 0), 0))
    stat = pl.BlockSpec((SWA_GROUP, SWA_TB, LANES), lambda h, i: (h, i, 0))
    bias = pl.BlockSpec((None, SWA_GROUP * W, 2 * W), lambda h, i: (h, 0, 0))
    return qspec, own, prev, stat, bias


def _swa_fwd(sinks, q, kad, vad, bias, T):
    nb = T // SWA_TB
    scale = SWA_HEAD_DIM ** -0.5
    W = WINDOW

    def body(sink_ref, q_ref, k_ref, kp_ref, v_ref, vp_ref, bias_ref, o_ref, lse_ref):
        kvh, i = pl.program_id(0), pl.program_id(1)
        sink = _swa_sink_column(sink_ref, kvh)
        for r in range(SWA_SUB):
            rs = slice(r * W, (r + 1) * W)
            ps = slice((r - 1) * W, r * W)
            k_own, v_own = k_ref[rs, :], v_ref[rs, :]
            k_prev = kp_ref[...] if r == 0 else k_ref[ps, :]
            v_prev = vp_ref[...] if r == 0 else v_ref[ps, :]
            qs = _swa_stack(q_ref[rs, :])
            s = jnp.concatenate([_dot(qs, k_prev, NT), _dot(qs, k_own, NT)], axis=1) * scale + bias_ref[...]
            s = jnp.where(_swa_valid(r, i), s, NEG)
            m = jnp.maximum(jnp.max(s, axis=1, keepdims=True), sink)
            p = jnp.exp(s - m)
            denom = jnp.sum(p, axis=1, keepdims=True) + jnp.exp(sink - m)
            pn = (p / denom).astype(BF16)
            o_ref[rs, :] = _swa_unstack(_dot(pn[:, :W], v_prev) + _dot(pn[:, W:], v_own)).astype(o_ref.dtype)
            lse = m + jnp.log(denom)
            for g in range(SWA_GROUP):
                lse_ref[g, rs, :] = jnp.broadcast_to(lse[g * W:(g + 1) * W], (W, LANES))

    qspec, own, prev, stat, bspec = _swa_specs(T)
    return pl.pallas_call(
        body, name="swa_fwd", grid=(SWA_KV_HEADS, nb),
        in_specs=[pl.BlockSpec(memory_space=pltpu.SMEM), qspec, own, prev, own, prev, bspec],
        out_specs=[qspec, stat],
        out_shape=[jax.ShapeDtypeStruct((T, 512), BF16), jax.ShapeDtypeStruct((SWA_HEADS, T, LANES), F32)],
        compiler_params=_cparams("parallel", "parallel", vmem=VMEM_MID),
    )(sinks, q, kad, kad, vad, vad, bias.reshape(SWA_KV_HEADS, SWA_GROUP * W, 2 * W))


def _swa_bwd(sinks, q, kad, vad, bias, do, lse, delta, T):
    nb = T // SWA_TB
    scale = SWA_HEAD_DIM ** -0.5
    W = WINDOW

    def body(sink_ref, q_ref, k_ref, kp_ref, v_ref, vp_ref, bias_ref, do_ref, lse_ref, dl_ref,
             dq_ref, dkad_ref, dvad_ref, dbias_ref, dsk_ref):
        kvh, i = pl.program_id(0), pl.program_id(1)
        sink = _swa_sink_column(sink_ref, kvh)

        @pl.when((kvh == 0) & (i == 0))
        def _():
            dkad_ref[...] = jnp.zeros_like(dkad_ref)
            dvad_ref[...] = jnp.zeros_like(dvad_ref)

        @pl.when(i == 0)
        def _():
            dbias_ref[...] = jnp.zeros_like(dbias_ref)
            dsk_ref[...] = jnp.zeros_like(dsk_ref)

        for r in range(SWA_SUB):
            rs = slice(r * W, (r + 1) * W)
            ps = slice((r - 1) * W, r * W)
            k_own, v_own = k_ref[rs, :], v_ref[rs, :]
            k_prev = kp_ref[...] if r == 0 else k_ref[ps, :]
            v_prev = vp_ref[...] if r == 0 else v_ref[ps, :]
            qs = _swa_stack(q_ref[rs, :])
            dos = _swa_stack(do_ref[rs, :])
            lse_b = jnp.concatenate([lse_ref[g, rs, :] for g in range(SWA_GROUP)], axis=0)
            dl_b = jnp.concatenate([dl_ref[g, rs, :] for g in range(SWA_GROUP)], axis=0)
            s = jnp.concatenate([_dot(qs, k_prev, NT), _dot(qs, k_own, NT)], axis=1) * scale + bias_ref[...]
            s = jnp.where(_swa_valid(r, i), s, NEG)
            p = jnp.exp(s - jnp.tile(lse_b, (1, 2)))
            dp = jnp.concatenate([_dot(dos, v_prev, NT), _dot(dos, v_own, NT)], axis=1)
            ds = p * (dp - jnp.tile(dl_b, (1, 2)))
            sink_term = jnp.exp(sink - lse_b) * dl_b
            for g in range(SWA_GROUP):
                dbias_ref[g] += ds[g * W:(g + 1) * W]
                dsk_ref[g:g + 1, :] += jnp.sum(sink_term[g * W:(g + 1) * W], axis=0, keepdims=True)
            dsb = ds.astype(BF16)
            pb = p.astype(BF16)
            dq_ref[rs, :] = _swa_unstack((_dot(dsb[:, :W], k_prev) + _dot(dsb[:, W:], k_own)) * scale)
            own_row = pl.multiple_of(i * SWA_TB + r * W, W)
            dkad_ref[kvh, pl.ds(own_row, W), :] += _dot(dsb[:, W:], qs, TN) * scale
            dvad_ref[kvh, pl.ds(own_row, W), :] += _dot(pb[:, W:], dos, TN)
            dk_prev = _dot(dsb[:, :W], qs, TN) * scale
            dv_prev = _dot(pb[:, :W], dos, TN)
            if r == 0:
                @pl.when(i > 0)
                def _():
                    prev_row = pl.multiple_of(i * SWA_TB - W, W)
                    dkad_ref[kvh, pl.ds(prev_row, W), :] += dk_prev
                    dvad_ref[kvh, pl.ds(prev_row, W), :] += dv_prev
            else:
                prev_row = pl.multiple_of(i * SWA_TB + (r - 1) * W, W)
                dkad_ref[kvh, pl.ds(prev_row, W), :] += dk_prev
                dvad_ref[kvh, pl.ds(prev_row, W), :] += dv_prev

    qspec, own, prev, stat, bspec = _swa_specs(T)
    full = pl.BlockSpec((SWA_KV_HEADS, T, LANES), lambda h, i: (0, 0, 0))
    return pl.pallas_call(
        body, name="swa_bwd", grid=(SWA_KV_HEADS, nb),
        in_specs=[pl.BlockSpec(memory_space=pltpu.SMEM), qspec, own, prev, own, prev, bspec, qspec, stat, stat],
        out_specs=[qspec, full, full, pl.BlockSpec((SWA_GROUP, W, 2 * W), lambda h, i: (h, 0, 0)),
                   pl.BlockSpec((None, 8, LANES), lambda h, i: (h, 0, 0))],
        out_shape=[jax.ShapeDtypeStruct((T, 512), F32), jax.ShapeDtypeStruct((SWA_KV_HEADS, T, LANES), F32),
                   jax.ShapeDtypeStruct((SWA_KV_HEADS, T, LANES), F32), jax.ShapeDtypeStruct((SWA_HEADS, W, 2 * W), F32),
                   jax.ShapeDtypeStruct((SWA_KV_HEADS, 8, LANES), F32)],
        compiler_params=_cparams("arbitrary", "arbitrary", vmem=VMEM_MID),
    )(sinks, q, kad, kad, vad, vad, bias.reshape(SWA_KV_HEADS, SWA_GROUP * W, 2 * W), do, lse, delta)


MEM_TQ = 4096


def _mem_fwd(q, mk, mv, T, tq):
    scale = MEM_HEAD_DIM ** -0.5

    def body(q_ref, k_ref, v_ref, o_ref, lse_ref):
        s = _dot(q_ref[...], k_ref[...], NT) * scale
        m = jnp.max(s, axis=1, keepdims=True)
        p = jnp.exp(s - m)
        l = jnp.sum(p, axis=1, keepdims=True)
        o_ref[...] = _dot((p / l).astype(BF16), v_ref[...]).astype(o_ref.dtype)
        lse_ref[...] = jnp.broadcast_to(m + jnp.log(l), (tq, LANES))

    qspec = pl.BlockSpec((tq, LANES), lambda h, i: (i, h))
    kspec = pl.BlockSpec((N_MEM, LANES), lambda h, i: (0, h))
    return pl.pallas_call(
        body, name="mem_fwd", grid=(MEM_HEADS, T // tq),
        in_specs=[qspec, kspec, kspec],
        out_specs=[qspec, pl.BlockSpec((None, tq, LANES), lambda h, i: (h, i, 0))],
        out_shape=[jax.ShapeDtypeStruct((T, 512), BF16), jax.ShapeDtypeStruct((MEM_HEADS, T, LANES), F32)],
        compiler_params=_cparams("parallel", "parallel"),
    )(q, mk, mv)


def _mem_bwd(q, mk, mv, do, lse, delta, T, tq):
    scale = MEM_HEAD_DIM ** -0.5
    rep = N_MEM // LANES

    def body(q_ref, k_ref, v_ref, do_ref, lse_ref, dl_ref, dq_ref, dk_ref, dv_ref):
        i = pl.program_id(1)

        @pl.when(i == 0)
        def _():
            dk_ref[...] = jnp.zeros_like(dk_ref)
            dv_ref[...] = jnp.zeros_like(dv_ref)

        qv, dov = q_ref[...], do_ref[...]
        s = _dot(qv, k_ref[...], NT) * scale
        p = jnp.exp(s - jnp.tile(lse_ref[...], (1, rep)))
        dp = _dot(dov, v_ref[...], NT)
        ds = p * (dp - jnp.tile(dl_ref[...], (1, rep)))
        dsb = ds.astype(BF16)
        dq_ref[...] = _dot(dsb, k_ref[...]) * scale
        dk_ref[...] += _dot(dsb, qv, TN) * scale
        dv_ref[...] += _dot(p.astype(BF16), dov, TN)

    qspec = pl.BlockSpec((tq, LANES), lambda h, i: (i, h))
    kspec = pl.BlockSpec((N_MEM, LANES), lambda h, i: (0, h))
    stat = pl.BlockSpec((None, tq, LANES), lambda h, i: (h, i, 0))
    return pl.pallas_call(
        body, name="mem_bwd", grid=(MEM_HEADS, T // tq),
        in_specs=[qspec, kspec, kspec, qspec, stat, stat],
        out_specs=[qspec, kspec, kspec],
        out_shape=[jax.ShapeDtypeStruct((T, 512), F32), jax.ShapeDtypeStruct((N_MEM, 512), F32),
                   jax.ShapeDtypeStruct((N_MEM, 512), F32)],
        compiler_params=_cparams("arbitrary", "arbitrary"),
    )(q, mk, mv, do, lse, delta)


def _mem_prep_fwd(mem, g_mem, w_kv, kn_gain, gm128):
    def body(mem_ref, g_ref, w_ref, kn_ref, gm_ref, memn_o, kv_o, mk_o, mv_o):
        xhat, _ = _rms_rows(mem_ref[...], None)
        memn = (xhat * g_ref[...]).astype(BF16)
        memn_o[...] = memn
        kv = _dot(memn, w_ref[...])
        kv_o[...] = kv
        gm = gm_ref[...]
        for c in range(4):
            sl = slice(c * LANES, (c + 1) * LANES)
            y, _ = _head_norm(kv[:, sl], gm, kn_ref[...])
            mk_o[:, sl] = y.astype(BF16)
        mv_o[...] = kv[:, 512:].astype(BF16)

    vm = pl.BlockSpec(memory_space=pltpu.VMEM)
    return pl.pallas_call(
        body, name="mem_prep_fwd", in_specs=[vm] * 5, out_specs=[vm] * 4,
        out_shape=[jax.ShapeDtypeStruct((N_MEM, D_MODEL), BF16), jax.ShapeDtypeStruct((N_MEM, D_MODEL), F32),
                   jax.ShapeDtypeStruct((N_MEM, 512), BF16), jax.ShapeDtypeStruct((N_MEM, 512), BF16)],
        compiler_params=pltpu.CompilerParams(vmem_limit_bytes=VMEM_MID),
    )(mem, g_mem, w_kv, kn_gain, gm128)


def _mem_prep_bwd(mem, g_mem, memn, kv, w_kv, kn_gain, gm128, dmk, dmv):
    def body(mem_ref, g_ref, memn_ref, kv_ref, w_ref, kn_ref, gm_ref, dmk_ref, dmv_ref, dw_o, dg_o, dkn_o, dkv_s):
        gm = gm_ref[...]
        dkn = jnp.zeros((1, LANES), F32)
        for c in range(4):
            sl = slice(c * LANES, (c + 1) * LANES)
            dx, dg = _head_norm_bwd(dmk_ref[:, sl], kv_ref[:, sl], gm, kn_ref[...])
            dkv_s[:, sl] = dx.astype(BF16)
            dkn = dkn + dg
        dkn_o[...] = dkn
        dkv_s[:, 512:] = dmv_ref[...].astype(BF16)
        dkv = dkv_s[...]
        dw_o[...] = _dot(memn_ref[...], dkv, TN)
        dmemn = _dot(dkv, w_ref[...], NT)
        xhat, _ = _rms_rows(mem_ref[...], None)
        dg_o[...] = jnp.sum(dmemn * xhat, axis=0, keepdims=True)

    vm = pl.BlockSpec(memory_space=pltpu.VMEM)
    return pl.pallas_call(
        body, name="mem_prep_bwd", in_specs=[vm] * 9, out_specs=[vm] * 3,
        out_shape=[jax.ShapeDtypeStruct((D_MODEL, D_MODEL), F32), jax.ShapeDtypeStruct((1, D_MODEL), F32),
                   jax.ShapeDtypeStruct((1, LANES), F32)],
        scratch_shapes=[pltpu.VMEM((N_MEM, D_MODEL), BF16)],
        compiler_params=pltpu.CompilerParams(vmem_limit_bytes=VMEM_MID),
    )(mem, g_mem, memn, kv, w_kv, kn_gain, gm128, dmk, dmv)


SLOT_O = D_MODEL // N_SHARD


def _merge_fwd(proj, b_gate, o3, w3, T, tb):
    def body(gl_ref, bg_ref, oa_ref, of_ref, om_ref, wa_ref, wf_ref, wm_ref, out_ref):
        o_refs = (oa_ref, of_ref, om_ref)
        w_refs = (wa_ref, wf_ref, wm_ref)
        for n in range(N_SHARD):
            acc = jnp.zeros((tb, SLOT_O), F32)
            for b in range(3):
                c0 = b * D_MODEL + n * SLOT_O
                g = jax.nn.sigmoid(gl_ref[:, c0:c0 + SLOT_O] + bg_ref[:, c0:c0 + SLOT_O])
                acc = acc + g * _dot(o_refs[b][...], w_refs[b][n])
            out_ref[:, n * SLOT_O:(n + 1) * SLOT_O] = acc.astype(out_ref.dtype)

    rows = pl.BlockSpec((tb, 512), lambda i: (i, 0))
    wspec = pl.BlockSpec((N_SHARD, 512, SLOT_O), lambda i: (0, 0, 0))
    return pl.pallas_call(
        body, name="merge_fwd", grid=(T // tb,),
        in_specs=[pl.BlockSpec((tb, GATE_W), lambda i: (i, 1)), pl.BlockSpec((1, GATE_W), lambda i: (0, 0)),
                  rows, rows, rows, wspec, wspec, wspec],
        out_specs=pl.BlockSpec((tb, D_MODEL), lambda i: (i, 0)),
        out_shape=jax.ShapeDtypeStruct((T, D_MODEL), BF16),
        compiler_params=_cparams("parallel", vmem=VMEM_BIG),
    )(proj, b_gate, *o3, *w3)


def _merge_bwd(proj, b_gate, o3, w3, dmerged, T, tb):
    heads = (SWA_HEADS, FOX_HEADS, MEM_HEADS)

    def body(gl_ref, bg_ref, oa_ref, of_ref, om_ref, wa_ref, wf_ref, wm_ref, dm_ref,
             dgl_o, doa_o, dof_o, dom_o, dla_o, dlf_o, dlm_o, dwa_o, dwf_o, dwm_o, dbg_o):
        i = pl.program_id(0)
        o_refs = (oa_ref, of_ref, om_ref)
        w_refs = (wa_ref, wf_ref, wm_ref)
        do_refs = (doa_o, dof_o, dom_o)
        dl_refs = (dla_o, dlf_o, dlm_o)
        dw_refs = (dwa_o, dwf_o, dwm_o)

        @pl.when(i == 0)
        def _():
            for r in dw_refs:
                r[...] = jnp.zeros_like(r)
            dbg_o[...] = jnp.zeros_like(dbg_o)

        lane = _lane((tb, LANES))
        for b in range(3):
            ob = o_refs[b][...]
            do = jnp.zeros((tb, 512), F32)
            for n in range(N_SHARD):
                c0 = b * D_MODEL + n * SLOT_O
                g = jax.nn.sigmoid(gl_ref[:, c0:c0 + SLOT_O] + bg_ref[:, c0:c0 + SLOT_O])
                dm = dm_ref[:, n * SLOT_O:(n + 1) * SLOT_O]
                y = _dot(ob, w_refs[b][n])
                dgl = dm * y * g * (1.0 - g)
                dgl_o[:, c0:c0 + SLOT_O] = dgl.astype(dgl_o.dtype)
                dbg_o[:, c0:c0 + SLOT_O] += jnp.sum(dgl, axis=0, keepdims=True)
                dy = (dm * g).astype(BF16)
                do = do + _dot(dy, w_refs[b][n], NT)
                dw_refs[b][n] += _dot(ob, dy, TN)
            do_refs[b][...] = do.astype(BF16)
            prod = do * ob.astype(F32)
            for c in range(4):
                blk = prod[:, c * LANES:(c + 1) * LANES]
                if heads[b] == 8:
                    lo = jnp.sum(jnp.where(lane < 64, blk, 0.0), axis=1, keepdims=True)
                    hi = jnp.sum(jnp.where(lane >= 64, blk, 0.0), axis=1, keepdims=True)
                    if b == 1:
                        aug = jnp.zeros((tb, LANES), F32)
                        for sub, dl in enumerate((lo, hi)):
                            for e, piece in enumerate(_split3(-dl)):
                                aug = jnp.where(lane == AUG_STRIDE * sub + AUG_C + e, piece.astype(F32), aug)
                        dl_refs[b][:, c * LANES:(c + 1) * LANES] = aug.astype(BF16)
                    else:
                        dl_refs[b][2 * c] = jnp.broadcast_to(lo, (tb, LANES))
                        dl_refs[b][2 * c + 1] = jnp.broadcast_to(hi, (tb, LANES))
                else:
                    dl_refs[b][c] = jnp.broadcast_to(jnp.sum(blk, axis=1, keepdims=True), (tb, LANES))

    rows = pl.BlockSpec((tb, 512), lambda i: (i, 0))
    wspec = pl.BlockSpec((N_SHARD, 512, SLOT_O), lambda i: (0, 0, 0))
    stat = lambda h: pl.BlockSpec((h, tb, LANES), lambda i: (0, i, 0))
    return pl.pallas_call(
        body, name="merge_bwd", grid=(T // tb,),
        in_specs=[pl.BlockSpec((tb, GATE_W), lambda i: (i, 1)), pl.BlockSpec((1, GATE_W), lambda i: (0, 0)),
                  rows, rows, rows, wspec, wspec, wspec, pl.BlockSpec((tb, D_MODEL), lambda i: (i, 0))],
        out_specs=[pl.BlockSpec((tb, GATE_W), lambda i: (i, 0)), rows, rows, rows,
                   stat(8), rows, stat(4), wspec, wspec, wspec, pl.BlockSpec((1, GATE_W), lambda i: (0, 0))],
        out_shape=[jax.ShapeDtypeStruct((T, GATE_W), BF16)] + [jax.ShapeDtypeStruct((T, 512), BF16)] * 3
        + [jax.ShapeDtypeStruct((8, T, LANES), F32), jax.ShapeDtypeStruct((T, 512), BF16),
           jax.ShapeDtypeStruct((4, T, LANES), F32)]
        + [jax.ShapeDtypeStruct((N_SHARD, 512, SLOT_O), F32)] * 3 + [jax.ShapeDtypeStruct((1, GATE_W), F32)],
        compiler_params=_cparams("arbitrary", vmem=VMEM_BIG),
    )(proj, b_gate, *o3, *w3, dmerged)


def _local_step(x, h, mem, tgt, small, g_in, w_kv, w_o3, w_out, w_up, w_down, reducer):
    T = x.shape[0]
    tm = min(512, T)
    tile2 = lambda v: jnp.tile(v.reshape(1, -1), (1, LANES // v.size))
    gains = jnp.concatenate([tile2(small["qn_swa"]), tile2(small["kn_swa"]), tile2(small["qn_fox"]),
                             tile2(small["kn_fox"]), tile2(small["qn_mem"]), jnp.zeros((3, LANES), F32)], axis=0)
    kn_mem = small["kn_mem"].reshape(1, LANES)
    bfor = jnp.pad(small["b_forget"].reshape(1, -1), ((0, 0), (0, LANES - FOX_HEADS)))
    gm64 = _group_mean_matrix(64)
    gm128 = _group_mean_matrix(128)
    tb_prep = min(512, T)
    ones = jnp.ones((tb_prep, tb_prep), F32)
    tril = jnp.tril(ones).astype(BF16)
    triu = jnp.triu(ones).astype(BF16)
    bucket = _t5_bucket_matrix()
    g_mix, g_mlp, g_mem = small["g_mix"], small["g_mlp"], small["g_mem"]
    b_gate = small["b_gate"]
    sinks = small["sink_swa"].reshape(-1)

    tl = min(1024, T)
    sq = pl.BlockSpec((tl, D_MODEL), lambda i, j, k: (i, j))
    wc = _w_in_to_segments(g_in)
    (proj,) = _matmul(
        "mm_proj", h, wc, dims=NN, grid=(T // tl, PROJ_W // D_MODEL, 1),
        a_spec=pl.BlockSpec((tl, D_MODEL), lambda i, j, k: (i, 0)),
        b_spec=pl.BlockSpec((D_MODEL, D_MODEL), lambda i, j, k: (0, j)),
        acc_shape=(tl, D_MODEL),
        outs=[(jax.ShapeDtypeStruct((T, PROJ_W), F32), sq)],
        epilogue=_epi_store)
    qa, qf, kf, vf, qm, kad, vad, qf_aug, kf_aug = _prep_fwd(proj, gains, bfor, tril, gm64, gm128, T, tb_prep)
    bias = _swa_bias(small["rel_bias"], bucket)
    o_swa, lse_swa = _swa_fwd(sinks, qa, kad, vad, bias, T)
    o_fox, qf_aug_bwd = _fox_fwd(qf, qf_aug, kf, kf_aug, vf, T, min(FOX_TQ, T), min(FOX_TK, T))
    memn, kv, mk, mv = _mem_prep_fwd(mem, g_mem, w_kv, kn_mem, gm128)
    o_mem, lse_mem = _mem_fwd(qm, mk, mv, T, min(MEM_TQ, T))
    o3 = (o_swa, o_fox, o_mem)
    merged = _merge_fwd(proj, b_gate, o3, w_o3, T, min(512, T))

    def epi_residual(acc, extra_refs, out_refs, ij):
        out_refs[0][...] = extra_refs[0][...] + acc

    row_full = pl.BlockSpec((tm, D_MODEL), lambda i, j, k: (i, 0))
    row_big = pl.BlockSpec((tl, D_MODEL), lambda i, j, k: (i, 0))
    whole = pl.BlockSpec((D_MODEL, D_MODEL), lambda i, j, k: (0, 0))
    (x2,) = _matmul(
        "mm_out", merged, w_out, dims=NN, grid=(T // tl, 1, 1),
        a_spec=row_big, b_spec=whole,
        acc_shape=(tl, D_MODEL), extra=[(x, row_big)],
        outs=[(jax.ShapeDtypeStruct((T, D_MODEL), F32), row_big)], epilogue=epi_residual)
    hm = _rmsnorm("rms_mlp", x2, g_mlp, tm)

    def epi_relu2(acc, extra_refs, out_refs, ij):
        out_refs[0][...] = acc.astype(BF16)
        r = jnp.maximum(acc, 0.0)
        out_refs[1][...] = (r * r).astype(BF16)

    up, u = _matmul(
        "mm_up", hm, w_up, dims=NN, grid=(T // tl, N_SHARD, 1),
        a_spec=row_big, b_spec=pl.BlockSpec((None, D_MODEL, D_MODEL), lambda i, j, k: (j, 0, 0)),
        acc_shape=(tl, D_MODEL),
        outs=[(jax.ShapeDtypeStruct((T, D_FF), BF16), sq), (jax.ShapeDtypeStruct((T, D_FF), BF16), sq)],
        epilogue=epi_relu2)

    def epi_loss(acc, extra_refs, out_refs, ij):
        y = extra_refs[0][...] + acc
        err = y - extra_refs[1][...]
        dyv = err * (1.0 / D_MODEL)
        out_refs[0][...] = dyv
        out_refs[2][...] = dyv.astype(BF16)
        sq = jnp.sum(jnp.sum(err * err, axis=1, keepdims=True), axis=0, keepdims=True)

        @pl.when(ij[0] == 0)
        def _():
            out_refs[1][...] = jnp.zeros_like(out_refs[1])

        out_refs[1][...] += jnp.broadcast_to(sq, out_refs[1].shape)

    kblk = pl.BlockSpec((tl, D_MODEL), lambda i, j, k: (i, k))
    dy, loss_acc, dy_bf = _matmul(
        "mm_down", u, w_down, dims=NN, grid=(T // tl, 1, N_SHARD),
        a_spec=kblk, b_spec=pl.BlockSpec((D_MODEL, D_MODEL), lambda i, j, k: (k, 0)),
        acc_shape=(tl, D_MODEL), extra=[(x2, row_big), (tgt, row_big)],
        outs=[(jax.ShapeDtypeStruct((T, D_MODEL), F32), row_big),
              (jax.ShapeDtypeStruct((8, LANES), F32), pl.BlockSpec((8, LANES), lambda i, j, k: (0, 0))),
              (jax.ShapeDtypeStruct((T, D_MODEL), BF16), row_big)],
        epilogue=epi_loss)
    loss = loss_acc[0, 0] * (0.5 / D_MODEL)

    def epi_dup(acc, extra_refs, out_refs, ij):
        out_refs[0][...] = (acc * (2.0 * jnp.maximum(extra_refs[0][...].astype(F32), 0.0))).astype(BF16)

    (dup,) = _matmul(
        "mm_dup", dy_bf, w_down, dims=NT, grid=(T // tl, N_SHARD, 1),
        a_spec=row_big, b_spec=pl.BlockSpec((D_MODEL, D_MODEL), lambda i, j, k: (j, 0)),
        acc_shape=(tl, D_MODEL), extra=[(up, sq)],
        outs=[(jax.ShapeDtypeStruct((T, D_FF), BF16), sq)], epilogue=epi_dup)

    nkt = T // tl
    t_rows = pl.BlockSpec((tl, D_MODEL), lambda i, j, k: (k, i))
    t_cols = pl.BlockSpec((tl, D_MODEL), lambda i, j, k: (k, j))
    (d_w_down,) = _matmul(
        "mm_dw_down", u, dy_bf, dims=TN, grid=(N_SHARD, 1, nkt),
        a_spec=t_rows, b_spec=t_cols, acc_shape=(D_MODEL, D_MODEL),
        outs=[(jax.ShapeDtypeStruct((D_FF, D_MODEL), F32), pl.BlockSpec((D_MODEL, D_MODEL), lambda i, j, k: (i, 0)))],
        epilogue=_epi_store)
    (d_w_up,) = _matmul(
        "mm_dw_up", hm, dup, dims=TN, grid=(1, N_SHARD, nkt),
        a_spec=t_rows, b_spec=t_cols, acc_shape=(D_MODEL, D_MODEL),
        outs=[(jax.ShapeDtypeStruct((N_SHARD, D_MODEL, D_MODEL), F32),
               pl.BlockSpec((None, D_MODEL, D_MODEL), lambda i, j, k: (j, 0, 0)))],
        epilogue=_epi_store)

    def epi_rms_bwd(acc, extra_refs, out_refs, ij):
        dx, dg = _rmsnorm_bwd_rows(acc, extra_refs[0][...], extra_refs[1][...])
        out_refs[0][...] = dx + extra_refs[2][...]

        @pl.when(ij[0] == 0)
        def _():
            out_refs[1][...] = jnp.zeros_like(out_refs[1])

        out_refs[1][...] += dg

    gain_spec = pl.BlockSpec((1, D_MODEL), lambda i, j, k: (0, 0))
    dx2, d_g_mlp = _matmul(
        "mm_dhm", dup, w_up, dims=NT, grid=(T // tl, 1, N_SHARD),
        a_spec=kblk, b_spec=pl.BlockSpec((None, D_MODEL, D_MODEL), lambda i, j, k: (k, 0, 0)),
        acc_shape=(tl, D_MODEL), extra=[(x2, row_big), (g_mlp, gain_spec), (dy, row_big)],
        outs=[(jax.ShapeDtypeStruct((T, D_MODEL), F32), row_big), (jax.ShapeDtypeStruct((1, D_MODEL), F32), gain_spec)],
        epilogue=epi_rms_bwd)

    (dmerged,) = _matmul(
        "mm_dmerged", dx2, w_out, dims=NT, grid=(T // tl, 1, 1),
        a_spec=row_big, b_spec=whole,
        acc_shape=(tl, D_MODEL), outs=[(jax.ShapeDtypeStruct((T, D_MODEL), F32), row_big)], epilogue=_epi_store)
    (d_w_out,) = _matmul(
        "mm_dw_out", merged, dx2, dims=TN, grid=(1, 1, nkt),
        a_spec=t_rows, b_spec=t_cols, acc_shape=(D_MODEL, D_MODEL),
        outs=[(jax.ShapeDtypeStruct((D_MODEL, D_MODEL), F32), whole)],
        epilogue=_epi_store)
    (dgl, do_swa, do_fox, do_mem, dl_swa, do_fox_aug, dl_mem, d_wo_swa, d_wo_fox, d_wo_mem, d_b_gate) = _merge_bwd(
        proj, b_gate, o3, w_o3, dmerged, T, min(512, T))

    dqm, dmk, dmv = _mem_bwd(qm, mk, mv, do_mem, lse_mem, dl_mem, T, min(MEM_TQ, T))
    d_w_kv, d_g_mem, d_kn_mem = _mem_prep_bwd(mem, g_mem, memn, kv, w_kv, kn_mem, gm128, dmk, dmv)
    do_swa = reducer.early_start({"w_mlp_down": d_w_down, "w_mlp_up": d_w_up, "w_out": d_w_out, "w_mem_kv": d_w_kv,
                                  "w_o_swa": d_wo_swa, "w_o_fox": d_wo_fox, "w_o_mem": d_wo_mem}, do_swa)
    dqa, dkad, dvad, dbias, dsk = _swa_bwd(sinks, qa, kad, vad, bias, do_swa, lse_swa, dl_swa, T)
    dqa, do_fox = reducer.early_send((dqa, do_fox))
    dqf, dqf_aug, dkf, dkf_aug, dvf = _fox_bwd(qf, qf_aug_bwd, kf, kf_aug, vf, do_fox, do_fox_aug, T,
                                               min(FOX_BWD_TQ, T), min(FOX_BWD_TK, T))
    dvf = reducer.early_finish(dvf)
    d_rel = _swa_bias_bwd(dbias, bucket)
    dlo, gacc = _prep_bwd(proj, dqa, dkad, dvad, dqf, dkf, dvf, dqm, dqf_aug, dkf_aug, gains, bfor, triu, gm64, gm128,
                          T, tb_prep)

    def dwc_half(name, dpart):
        (res,) = _matmul(
            name, h, dpart, dims=TN, grid=(1, LO_W // D_MODEL, nkt),
            a_spec=t_rows, b_spec=t_cols, acc_shape=(D_MODEL, D_MODEL),
            outs=[(jax.ShapeDtypeStruct((D_MODEL, LO_W), F32), pl.BlockSpec((D_MODEL, D_MODEL), lambda i, j, k: (0, j)))],
            epilogue=_epi_store)
        return res

    d_wc_lo = dwc_half("mm_dwc_lo", dlo)
    d_wc_gl = dwc_half("mm_dwc_gl", dgl)
    dlo = reducer.late_start({"wc_lo": d_wc_lo, "wc_gl": d_wc_gl}, dlo)
    (dh_lo,) = _matmul(
        "mm_dh_lo", dlo, wc, dims=NT, grid=(T // tl, 1, LO_W // D_MODEL),
        a_spec=kblk, b_spec=pl.BlockSpec((D_MODEL, D_MODEL), lambda i, j, k: (0, k)),
        acc_shape=(tl, D_MODEL), outs=[(jax.ShapeDtypeStruct((T, D_MODEL), F32), row_big)], epilogue=_epi_store)
    dh_lo = reducer.late_send(dh_lo)

    def epi_dx(acc, extra_refs, out_refs, ij):
        dhh = acc + extra_refs[3][...]
        dx, dg = _rmsnorm_bwd_rows(dhh, extra_refs[0][...], extra_refs[1][...])
        out_refs[0][...] = dx + extra_refs[2][...]

        @pl.when(ij[0] == 0)
        def _():
            out_refs[1][...] = jnp.zeros_like(out_refs[1])

        out_refs[1][...] += dg

    grad_x, d_g_mix = _matmul(
        "mm_dh_gl", dgl, wc, dims=NT, grid=(T // tl, 1, GATE_W // D_MODEL),
        a_spec=kblk, b_spec=pl.BlockSpec((D_MODEL, D_MODEL), lambda i, j, k: (0, k + LO_W // D_MODEL)),
        acc_shape=(tl, D_MODEL), extra=[(x, row_big), (g_mix, gain_spec), (dx2, row_big), (dh_lo, row_big)],
        outs=[(jax.ShapeDtypeStruct((T, D_MODEL), F32), row_big), (jax.ShapeDtypeStruct((1, D_MODEL), F32), gain_spec)],
        epilogue=epi_dx, vmem=VMEM_MAX)

    fold64 = lambda row: (row[:64] + row[64:]).reshape(1, 64)
    grads = {
        "g_mix": d_g_mix, "b_gate": d_b_gate, "b_forget": gacc[5, :FOX_HEADS].reshape(1, FOX_HEADS),
        "qn_swa": fold64(gacc[0]), "kn_swa": fold64(gacc[1]),
        "sink_swa": -dsk[:, :SWA_GROUP, 0].reshape(1, SWA_HEADS), "rel_bias": d_rel[:, :SWA_HEADS],
        "qn_fox": fold64(gacc[2]), "kn_fox": fold64(gacc[3]),
        "g_mem": d_g_mem, "qn_mem": gacc[4].reshape(1, LANES), "kn_mem": d_kn_mem, "g_mlp": d_g_mlp,
    }
    return loss, grad_x, grads


MESH = pl.DeviceIdType.MESH
ANY = pl.BlockSpec(memory_space=pl.ANY)


def _place():
    x, y, c = lax.axis_index("x"), lax.axis_index("y"), lax.axis_index("c")
    chips = [(1 - x, y), (x, 1 - y), (1 - x, 1 - y)]
    return x, y, c, chips


def _handshake(peers):
    barrier = pltpu.get_barrier_semaphore()
    for peer in peers:
        pl.semaphore_signal(barrier, inc=1, device_id=peer, device_id_type=MESH)
    pl.semaphore_wait(barrier, len(peers))


def _all_gather_shards_async(name, collective_id, slots):
    n = len(slots)
    bufs = [jax.new_ref(s, memory_space=pltpu.MemorySpace.HBM) for s in slots]

    def body(ici_send, ici_recv, d2d_send, d2d_recv):
        x, y, c, chips = _place()
        sibling = (x, y, 1 - c)
        me = 2 * x + y
        _handshake([(px, py, c) for px, py in chips] + [sibling])

        def half(a, who):
            hr = slots[a].shape[1] // 2
            return pl.ds(pl.multiple_of(who * hr, hr), hr)

        def ici(a, j, slot, to):
            return pltpu.make_async_remote_copy(
                src_ref=bufs[a].at[me, half(a, c)], dst_ref=bufs[a].at[slot, half(a, c)],
                send_sem=ici_send.at[3 * a + j], recv_sem=ici_recv.at[3 * a + j], device_id=to, device_id_type=MESH)

        def d2d(a, j, slot, which):
            part = bufs[a].at[slot, half(a, which)]
            return pltpu.make_async_remote_copy(
                src_ref=part, dst_ref=part, send_sem=d2d_send.at[3 * a + j], recv_sem=d2d_recv.at[3 * a + j],
                device_id=sibling, device_id_type=MESH)

        sends = [ici(a, j, me, (*chip, c)) for a in range(n) for j, chip in enumerate(chips)]
        for cp in sends:
            cp.start()
        passed = []
        for a in range(n):
            for j, (px, py) in enumerate(chips):
                ici(a, j, 2 * px + py, (px, py, c)).wait_recv()
                cp = d2d(a, j, 2 * px + py, c)
                cp.start()
                passed.append(cp)
        for a in range(n):
            for j, (px, py) in enumerate(chips):
                d2d(a, j, 2 * px + py, 1 - c).wait_recv()
        for cp in sends + passed:
            cp.wait_send()

    pl.kernel(
        body, mesh=plsc.ScalarSubcoreMesh(axis_name="seq", num_cores=1), name=name,
        scratch_types=[pltpu.SemaphoreType.DMA((3 * n,))] * 4,
        compiler_params=pltpu.CompilerParams(collective_id=collective_id),
    )()
    return [b[...] for b in bufs]


def _sequencer_call(name, collective_id, n_sems, body):
    pl.kernel(
        body, mesh=plsc.ScalarSubcoreMesh(axis_name="seq", num_cores=1), name=name,
        scratch_types=[pltpu.SemaphoreType.DMA((n_sems,))] * 2,
        compiler_params=pltpu.CompilerParams(collective_id=collective_id),
    )()


def _hbm_ref(value):
    return jax.new_ref(value, memory_space=pltpu.MemorySpace.HBM)


def _pair_exchange(name, collective_id, gs):
    n = len(gs)
    src = [_hbm_ref(g) for g in gs]
    stage = [jax.empty_ref(jax.ShapeDtypeStruct((N_SHARD, g.shape[1] // 2, g.shape[2]), g.dtype),
                           memory_space=pltpu.MemorySpace.HBM) for g in gs]

    def body(send_sem, recv_sem):
        x, y, c, _ = _place()
        sibling = (x, y, 1 - c)
        _handshake([sibling])
        copies = []
        for a in range(n):
            hr = gs[a].shape[1] // 2
            theirs = pl.ds(pl.multiple_of((1 - c) * hr, hr), hr)
            copies.append(pltpu.make_async_remote_copy(
                src_ref=src[a].at[:, theirs, :], dst_ref=stage[a], send_sem=send_sem.at[a], recv_sem=recv_sem.at[a],
                device_id=sibling, device_id_type=MESH))
        for cp in copies:
            cp.start()
        for cp in copies:
            cp.wait()

    _sequencer_call(name, collective_id, n, body)
    return [s[...] for s in stage]


def _chip_exchange(name, collective_id, sums):
    n = len(sums)
    src = [_hbm_ref(s) for s in sums]
    got = [jax.empty_ref(jax.ShapeDtypeStruct((3,) + s.shape[1:], s.dtype), memory_space=pltpu.MemorySpace.HBM)
           for s in sums]

    def body(send_sem, recv_sem):
        x, y, c, chips = _place()
        _handshake([(px, py, c) for px, py in chips])
        copies = []
        for a in range(n):
            for j, (px, py) in enumerate(chips):
                copies.append(pltpu.make_async_remote_copy(
                    src_ref=src[a].at[2 * px + py], dst_ref=got[a].at[j],
                    send_sem=send_sem.at[3 * a + j], recv_sem=recv_sem.at[3 * a + j],
                    device_id=(px, py, c), device_id_type=MESH))
        for cp in copies:
            cp.start()
        for cp in copies:
            cp.wait()

    _sequencer_call(name, collective_id, 3 * n, body)
    return [g[...] for g in got]


def _pair_gather(name, collective_id, fulls):
    n = len(fulls)
    full = [_hbm_ref(f) for f in fulls]

    def body(send_sem, recv_sem):
        x, y, c, _ = _place()
        sibling = (x, y, 1 - c)
        _handshake([sibling])
        copies = []
        for a in range(n):
            hr = fulls[a].shape[0] // 2
            mine = full[a].at[pl.ds(pl.multiple_of(c * hr, hr), hr)]
            copies.append(pltpu.make_async_remote_copy(
                src_ref=mine, dst_ref=mine, send_sem=send_sem.at[a], recv_sem=recv_sem.at[a],
                device_id=sibling, device_id_type=MESH))
        for cp in copies:
            cp.start()
        for cp in copies:
            cp.wait()

    _sequencer_call(name, collective_id, n, body)
    return [f[...] for f in full]


ELEMENTWISE_BLOCK_ELEMS = 256 * 1024


def _row_block(rows, cols):
    rb = 8
    while rb * 2 * cols <= ELEMENTWISE_BLOCK_ELEMS and rb * 2 <= rows:
        rb *= 2
    return rb


def _pair_sum(name, place, g, stage):
    _, R, C = g.shape
    hr = R // 2
    rb = _row_block(hr, C)
    nb = hr // rb

    def body(place_ref, g_ref, st_ref, sum_bf, own_f32):
        s = pl.program_id(1)
        tot = g_ref[...] + st_ref[...]
        sum_bf[...] = tot.astype(BF16)

        @pl.when(s == place_ref[0])
        def _():
            own_f32[...] = tot

    return pl.pallas_call(
        body, name=name,
        grid_spec=pltpu.PrefetchScalarGridSpec(
            num_scalar_prefetch=1, grid=(nb, N_SHARD),
            in_specs=[pl.BlockSpec((None, rb, C), lambda i, s, pr: (s, pr[1] * nb + i, 0)),
                      pl.BlockSpec((None, rb, C), lambda i, s, pr: (s, i, 0))],
            out_specs=[pl.BlockSpec((None, rb, C), lambda i, s, pr: (s, i, 0)),
                       pl.BlockSpec((rb, C), lambda i, s, pr: (i, 0))]),
        out_shape=[jax.ShapeDtypeStruct((N_SHARD, hr, C), BF16), jax.ShapeDtypeStruct((hr, C), F32)],
        compiler_params=_cparams("arbitrary", "arbitrary"),
    )(place, g, stage)


def _final_sum(name, place, own, got):
    hr, C = own.shape
    rb = _row_block(hr, C)
    nb = hr // rb

    def body(place_ref, own_ref, got_ref, o_ref):
        o_ref[...] = ((own_ref[...] + got_ref[0].astype(F32)) + got_ref[1].astype(F32)) + got_ref[2].astype(F32)

    return pl.pallas_call(
        body, name=name,
        grid_spec=pltpu.PrefetchScalarGridSpec(
            num_scalar_prefetch=1, grid=(nb,),
            in_specs=[pl.BlockSpec((rb, C), lambda i, pr: (i, 0)), pl.BlockSpec((3, rb, C), lambda i, pr: (0, i, 0))],
            out_specs=pl.BlockSpec((rb, C), lambda i, pr: (pr[1] * nb + i, 0))),
        out_shape=jax.ShapeDtypeStruct((2 * hr, C), F32),
        compiler_params=_cparams("arbitrary"),
    )(place, own, got)


def _adamw_math(w, g, m, v):
    m = ADAM_B1 * m + (1.0 - ADAM_B1) * g
    v = ADAM_B2 * v + (1.0 - ADAM_B2) * (g * g)
    m_hat = m / (1.0 - ADAM_B1 ** ADAM_STEP)
    v_hat = v / (1.0 - ADAM_B2 ** ADAM_STEP)
    delta = -ADAM_LR * (m_hat / (jnp.sqrt(v_hat) + ADAM_EPS) + ADAM_WD * w)
    return delta, m, v


def _adamw(name, w, g, m, v):
    R, Cw = w.shape
    Cg = g.shape[1]
    rb = _row_block(R, Cg)

    def body(w_ref, g_ref, m_ref, v_ref, g_o, d_o, m_o, v_o):
        gv = g_ref[...]
        delta, mn, vn = _adamw_math(w_ref[...], gv, m_ref[...], v_ref[...])
        g_o[...] = gv
        d_o[...] = delta
        m_o[...] = mn
        v_o[...] = vn

    blk = pl.BlockSpec((rb, Cg), lambda i: (i, 0))
    return pl.pallas_call(
        body, name=name, grid=(R // rb,),
        in_specs=[blk] * 4, out_specs=[blk] * 4,
        out_shape=[jax.ShapeDtypeStruct((R, Cw), F32)] * 4,
        compiler_params=_cparams("parallel"),
    )(w, g, m, v)


N_DEV = 8
SMALL_ROWS = 64


def _small_allreduce_adamw(g, w, m, v):
    def body(g_ref, w_ref, m_ref, v_ref, all_ref, gs_o, d_o, m_o, v_o, send_sems, recv_sems, local_sem):
        x, y, c, chips = _place()
        me, sibling = (x, y, c), (x, y, 1 - c)

        def rows(px, py, pc):
            return all_ref.at[pl.ds(pl.multiple_of((4 * px + 2 * py + pc) * SMALL_ROWS, SMALL_ROWS), SMALL_ROWS), :]

        def copy(k, block, to, src=None):
            return pltpu.make_async_remote_copy(
                src_ref=rows(*block) if src is None else src, dst_ref=rows(*block),
                send_sem=send_sems.at[k], recv_sem=recv_sems.at[k], device_id=to, device_id_type=MESH)

        mine = pltpu.make_async_copy(g_ref, rows(*me), local_sem)
        mine.start()
        first = [copy(0, me, sibling, src=g_ref)]
        first += [copy(1 + j, me, (*chip, c), src=g_ref) for j, chip in enumerate(chips)]
        for cp in first:
            cp.start()
        passed = [copy(4 + j, (*chip, c), sibling) for j, chip in enumerate(chips)]
        for j, chip in enumerate(chips):
            copy(1 + j, (*chip, c), me).wait_recv()
            passed[j].start()
        copy(0, sibling, me).wait_recv()
        for j, chip in enumerate(chips):
            copy(4 + j, (*chip, 1 - c), me).wait_recv()
        for cp in first + passed:
            cp.wait_send()
        mine.wait()

        tot = all_ref[0:SMALL_ROWS, :]
        for d in range(1, N_DEV):
            tot = tot + all_ref[d * SMALL_ROWS:(d + 1) * SMALL_ROWS, :]
        delta, mn, vn = _adamw_math(w_ref[...], tot, m_ref[...], v_ref[...])
        gs_o[...] = tot
        d_o[...] = delta
        m_o[...] = mn
        v_o[...] = vn

    vm = pl.BlockSpec(memory_space=pltpu.VMEM)
    shp = jax.ShapeDtypeStruct((SMALL_ROWS, LANES), F32)
    res = pl.pallas_call(
        body, name="small_allreduce_adamw", in_specs=[vm] * 4, out_specs=[vm] * 5,
        out_shape=[jax.ShapeDtypeStruct((N_DEV * SMALL_ROWS, LANES), F32), shp, shp, shp, shp],
        scratch_shapes=[pltpu.SemaphoreType.DMA((7,)), pltpu.SemaphoreType.DMA((7,)), pltpu.SemaphoreType.DMA],
    )(g, w, m, v)
    return res[1:]


SMALL_NAMES = ("g_mix", "b_gate", "b_forget", "qn_swa", "kn_swa", "sink_swa", "rel_bias", "qn_fox", "kn_fox",
               "g_mem", "qn_mem", "kn_mem", "g_mlp")
BIG_NAMES = ("w_in", "w_mem_kv", "w_o_swa", "w_o_fox", "w_o_mem", "w_out", "w_mlp_up", "w_mlp_down")
WEIGHT_NAMES = ("g_mix", "w_in", "b_gate", "b_forget", "qn_swa", "kn_swa", "sink_swa", "rel_bias", "qn_fox", "kn_fox",
                "g_mem", "w_mem_kv", "qn_mem", "kn_mem", "w_o_swa", "w_o_fox", "w_o_mem", "w_out", "g_mlp",
                "w_mlp_up", "w_mlp_down")


def _pack_small(parts, extra=None):
    rows = []
    for n in SMALL_NAMES:
        flat = parts[n].reshape(-1).astype(F32)
        flat = jnp.pad(flat, (0, (-flat.size) % LANES))
        rows.append(flat.reshape(-1, LANES))
    if extra is not None:
        rows.append(jnp.pad(extra.reshape(1, 1), ((0, 0), (0, LANES - 1))))
    packed = jnp.concatenate(rows, axis=0)
    return jnp.pad(packed, ((0, SMALL_ROWS - packed.shape[0]), (0, 0)))


def _unpack_small(packed, shapes):
    out, r = {}, 0
    for n in SMALL_NAMES:
        size = math.prod(shapes[n])
        nr = -(-size // LANES)
        out[n] = packed[r:r + nr].reshape(-1)[:size].reshape(shapes[n])
        r += nr
    return out, packed[r, 0]


W_IN_SEGMENTS = ((C_QA, 0, 512), (C_QF, 768, 512), (C_KF, 1280, 512), (C_VF, 1792, 512), (C_QM, 2312, 512),
                 (C_KA, 512, 128), (C_VA, 640, 128), (C_FL, 2304, FOX_HEADS), (C_GL, 2824, GATE_W))
RELAYOUT_ROWS = 256


def _permute_pieces(src_of_dst):
    blocks = []
    for b in range(len(src_of_dst) // LANES):
        runs, lane = [], 0
        while lane < LANES:
            src = src_of_dst[b * LANES + lane]
            if src is None:
                lane += 1
                continue
            plane, col = src
            end = lane + 1
            while (end < LANES and src_of_dst[b * LANES + end] == (plane, col + end - lane)
                   and (col + end - lane) // LANES == col // LANES):
                end += 1
            runs.append((plane, col // LANES, (lane - col) % LANES, lane, end))
            lane = end
        blocks.append(runs)
    return blocks


def _permuted_block(runs, load, rows):
    lane = _lane((rows, LANES))
    acc = jnp.zeros((rows, LANES), F32)
    for plane, blk, shift, lo, hi in runs:
        x = load(plane, blk).astype(F32)
        if shift:
            x = pltpu.roll(x, shift, 1)
        acc = x if (lo, hi) == (0, LANES) else jnp.where((lane >= lo) & (lane < hi), x, acc)
    return acc


def _w_in_to_segments(g_in):
    src_of_dst = [None] * PROJ_W
    for mine, theirs, width in W_IN_SEGMENTS:
        for k in range(width):
            src_of_dst[mine + k] = ((theirs + k) // IN_SHARD, (theirs + k) % IN_SHARD)
    blocks = _permute_pieces(src_of_dst)
    rb = RELAYOUT_ROWS

    def body(src_ref, out_ref):
        for b, runs in enumerate(blocks):
            blk = _permuted_block(runs, lambda p, c: src_ref[p, :, c * LANES:(c + 1) * LANES], rb)
            out_ref[:, b * LANES:(b + 1) * LANES] = blk.astype(out_ref.dtype)

    return pl.pallas_call(
        body, name="w_in_to_segments", grid=(D_MODEL // rb,),
        in_specs=[pl.BlockSpec((N_SHARD, rb, IN_SHARD_PAD), lambda i: (0, i, 0))],
        out_specs=pl.BlockSpec((rb, PROJ_W), lambda i: (i, 0)),
        out_shape=jax.ShapeDtypeStruct((D_MODEL, PROJ_W), g_in.dtype),
        compiler_params=_cparams("parallel", vmem=VMEM_MID),
    )(g_in)


def _w_in_from_segments(lo, gl):
    mine_of_theirs = {}
    for mine, theirs, width in W_IN_SEGMENTS:
        for k in range(width):
            mine_of_theirs[theirs + k] = mine + k
    src_of_dst = [None] * (N_SHARD * IN_SHARD_PAD)
    for s in range(N_SHARD):
        for l in range(IN_SHARD):
            j = mine_of_theirs[s * IN_SHARD + l]
            src_of_dst[s * IN_SHARD_PAD + l] = (j // LO_W, j % LO_W)
    blocks = _permute_pieces(src_of_dst)
    per_slot = IN_SHARD_PAD // LANES
    rb = RELAYOUT_ROWS

    def body(lo_ref, gl_ref, out_ref):
        planes = (lo_ref, gl_ref)
        for b, runs in enumerate(blocks):
            blk = _permuted_block(runs, lambda p, c: planes[p][:, c * LANES:(c + 1) * LANES], rb)
            c0 = (b % per_slot) * LANES
            out_ref[b // per_slot, :, c0:c0 + LANES] = blk

    half = pl.BlockSpec((rb, LO_W), lambda i: (i, 0))
    return pl.pallas_call(
        body, name="w_in_from_segments", grid=(D_MODEL // rb,),
        in_specs=[half, half],
        out_specs=pl.BlockSpec((N_SHARD, rb, IN_SHARD_PAD), lambda i: (0, i, 0)),
        out_shape=jax.ShapeDtypeStruct((N_SHARD, D_MODEL, IN_SHARD_PAD), F32),
        compiler_params=_cparams("parallel", vmem=VMEM_MID),
    )(lo, gl)


def _after(first, then):
    return lax.optimization_barrier((first, then))


class _ReduceGroup:
    def __init__(self, tag, first_collective_id, place):
        self.tag, self.first_id, self.place = tag, first_collective_id, place

    def start(self, local, tie):
        self.names = tuple(local)
        mine, tie = _after([local[n] for n in self.names], tie)
        self.mine = mine
        self.staged = _pair_exchange("pair_exchange_" + self.tag, self.first_id, mine)
        return tie

    def send(self, tie):
        staged, tie = _after(self.staged, tie)
        sums = [_pair_sum("pair_sum_" + n, self.place, g, st) for n, g, st in zip(self.names, self.mine, staged)]
        travel, tie = _after([s[0] for s in sums], tie)
        self.own = [s[1] for s in sums]
        self.got = _chip_exchange("chip_exchange_" + self.tag, self.first_id + 1, travel)
        return tie

    def finish(self, tie):
        got, tie = _after(self.got, tie)
        halves = [_final_sum("final_sum_" + n, self.place, o, r) for n, o, r in zip(self.names, self.own, got)]
        halves, tie = _after(halves, tie)
        summed = _pair_gather("pair_gather_" + self.tag, self.first_id + 2, halves)
        self.summed = dict(zip(self.names, summed))
        return tie


class _GradReducer:
    def __init__(self, place):
        self.early = _ReduceGroup("early", 2, place)
        self.late = _ReduceGroup("late", 5, place)

    @staticmethod
    def _slot_rows(a):
        return a.reshape(N_SHARD, a.shape[0] // N_SHARD, a.shape[1])

    def early_start(self, g, tie):
        return self.early.start({"w_mlp_down": self._slot_rows(g["w_mlp_down"]), "w_mlp_up": g["w_mlp_up"],
                                 "w_out": self._slot_rows(g["w_out"]), "w_mem_kv": self._slot_rows(g["w_mem_kv"]),
                                 "w_o_swa": g["w_o_swa"], "w_o_fox": g["w_o_fox"], "w_o_mem": g["w_o_mem"]}, tie)

    def early_send(self, tie):
        return self.early.send(tie)

    def early_finish(self, tie):
        return self.early.finish(tie)

    def late_start(self, g, tie):
        d_in = _w_in_from_segments(g["wc_lo"], g["wc_gl"])
        return self.late.start({"w_in": d_in}, tie)

    def late_send(self, tie):
        return self.late.send(tie)

    def late_finish(self, tie):
        return self.late.finish(tie)

    @property
    def summed(self):
        return {**self.early.summed, **self.late.summed}


def kernel(x, mem, g_mix, w_in, b_gate, b_forget, qn_swa, kn_swa, sink_swa, rel_bias, qn_fox, kn_fox, g_mem, w_mem_kv, qn_mem, kn_mem, w_o_swa, w_o_fox, w_o_mem, w_out, g_mlp, w_mlp_up, w_mlp_down, loss_target, m_g_mix, m_w_in, m_b_gate, m_b_forget, m_qn_swa, m_kn_swa, m_sink_swa, m_rel_bias, m_qn_fox, m_kn_fox, m_g_mem, m_w_mem_kv, m_qn_mem, m_kn_mem, m_w_o_swa, m_w_o_fox, m_w_o_mem, m_w_out, m_g_mlp, m_w_mlp_up, m_w_mlp_down, v_g_mix, v_w_in, v_b_gate, v_b_forget, v_qn_swa, v_kn_swa, v_sink_swa, v_rel_bias, v_qn_fox, v_kn_fox, v_g_mem, v_w_mem_kv, v_qn_mem, v_kn_mem, v_w_o_swa, v_w_o_fox, v_w_o_mem, v_w_out, v_g_mlp, v_w_mlp_up, v_w_mlp_down):
    given = dict(locals())
    W = {n: given[n] for n in WEIGHT_NAMES}
    M = {n: given["m_" + n] for n in WEIGHT_NAMES}
    V = {n: given["v_" + n] for n in WEIGHT_NAMES}
    pad_in = ((0, 0), (0, IN_SHARD_PAD - IN_SHARD))

    shards = [jnp.pad(w_in[0].astype(BF16), pad_in)] + [W[n][0].astype(BF16) for n in BIG_NAMES[1:]]
    slots = [jnp.broadcast_to(s[None], (N_SHARD,) + s.shape) for s in shards]
    (g_in,) = _all_gather_shards_async("all_gather_w_in", 1, slots[:1])
    small = {n: (W[n] if n == "rel_bias" else W[n].reshape(1, -1)) for n in SMALL_NAMES}
    h = _rmsnorm("rms_mix", x[0], small["g_mix"], min(512, x.shape[1]))
    g_in, late, h, (m_in, v_in) = lax.optimization_barrier((g_in, slots[1:], h, (M["w_in"][0], V["w_in"][0])))
    M["w_in"], V["w_in"] = m_in[None], v_in[None]
    g_kv, g_oa, g_of, g_om, g_out, g_up, g_down = _all_gather_shards_async("all_gather_weights_async", 8, late)

    place = jnp.stack([2 * lax.axis_index("x") + lax.axis_index("y"), lax.axis_index("c")]).astype(jnp.int32)
    reducer = _GradReducer(place)
    loss, grad_x, grads = _local_step(
        x[0], h, mem[0], loss_target[0], small, g_in, g_kv.reshape(D_MODEL, D_MODEL), (g_oa, g_of, g_om),
        g_out.reshape(D_MODEL, D_MODEL), g_up, g_down.reshape(D_FF, D_MODEL), reducer)

    out = {}

    def adamw_of(names, summed):
        for n in names:
            res = _adamw("adamw_" + n, W[n][0], summed[n], M[n][0], V[n][0])
            out[n] = [r.reshape(W[n].shape) for r in res]

    adamw_of(reducer.early.names, reducer.early.summed)
    shapes = {n: W[n].shape for n in SMALL_NAMES}
    packed = _small_allreduce_adamw(_pack_small(grads, loss), _pack_small(W), _pack_small(M), _pack_small(V))
    done_meanwhile = ([out[n] for n in reducer.early.names], packed)
    (early_out, packed), grad_x = reducer.late_finish((done_meanwhile, grad_x))
    for n, res in zip(reducer.early.names, early_out):
        out[n] = res
    adamw_of(reducer.late.names, reducer.late.summed)
    unpacked = [_unpack_small(p, shapes) for p in packed]
    for n in SMALL_NAMES:
        out[n] = [u[0][n] for u in unpacked]
    loss_total = unpacked[0][1]

    return (loss_total, grad_x.reshape(x.shape),
            *[out[n][0] for n in WEIGHT_NAMES], *[out[n][1] for n in WEIGHT_NAMES],
            *[out[n][2] for n in WEIGHT_NAMES], *[out[n][3] for n in WEIGHT_NAMES])
```

```python
import functools
import math

import jax
import jax.numpy as jnp
from jax import lax
from jax.experimental import pallas as pl
from jax.experimental.pallas import tpu as pltpu
from jax.experimental.pallas import tpu_sc as plsc

F32 = jnp.float32
BF16 = jnp.bfloat16

D_MODEL = 1024
N_MEM = 256
SWA_HEADS = 8
SWA_KV_HEADS = 2
SWA_HEAD_DIM = 64
WINDOW = 128
FOX_HEADS = 8
FOX_HEAD_DIM = 64
MEM_HEADS = 4
MEM_HEAD_DIM = 128
D_FF = 4 * D_MODEL
REL_BUCKETS = 32
REL_MAX_DIST = 128
EPS = 1e-6
NEG = -1e30
GATE_W = 3 * D_MODEL
IN_WIDTH = 5896
N_SHARD = 4
IN_SHARD = IN_WIDTH // N_SHARD
IN_SHARD_PAD = 1536

ADAM_LR = 0.001
ADAM_B1 = 0.9
ADAM_B2 = 0.999
ADAM_EPS = 1e-08
ADAM_WD = 0.01
ADAM_STEP = 10

LANES = 128
V7X_VMEM_BYTES = 64 * 1024 * 1024
MIB = 1024 * 1024
VMEM_SMALL, VMEM_MID, VMEM_BIG, VMEM_MAX = 48 * MIB, 48 * MIB, 48 * MIB, 56 * MIB

C_QA, C_QF, C_KF, C_VF, C_QM, C_KA, C_VA, C_FL, C_GL = 0, 512, 1024, 1536, 2048, 2560, 2688, 2816, 3072
LO_W = 3072
PROJ_W = 6144

NN = (((1,), (0,)), ((), ()))
NT = (((1,), (1,)), ((), ()))
TN = (((0,), (0,)), ((), ()))


def _dot(a, b, dims=NN):
    return lax.dot_general(a, b, dims, preferred_element_type=F32)


def _cparams(*sem, vmem=VMEM_SMALL):
    return pltpu.CompilerParams(dimension_semantics=sem, vmem_limit_bytes=vmem)


def _split3(a):
    hi = a.astype(BF16)
    r1 = a - hi.astype(F32)
    mid = r1.astype(BF16)
    lo = (r1 - mid.astype(F32)).astype(BF16)
    return hi, mid, lo


def _group_mean(a, g2):
    hi = a.astype(BF16)
    mid = (a - hi.astype(F32)).astype(BF16)
    return _dot(jnp.concatenate([hi, mid], axis=1), g2)


def _dot3_left(g, a):
    hi, mid, lo = _split3(a)
    return _dot(g, hi) + _dot(g, mid) + _dot(g, lo)


def _group_mean_matrix(d):
    r = jnp.arange(LANES)
    g = jnp.where((r[:, None] // d) == (r[None, :] // d), 1.0 / d, 0.0).astype(BF16)
    return jnp.concatenate([g, g], axis=0)


def _lane(shape):
    return lax.broadcasted_iota(jnp.int32, shape, len(shape) - 1)


def _matmul(name, a, b, *, dims, grid, a_spec, b_spec, acc_shape, outs, epilogue, extra=(), vmem=VMEM_BIG):
    nk = grid[2]
    n_extra = len(extra)

    def body(a_ref, b_ref, *rest):
        extra_refs = rest[:n_extra]
        out_refs = rest[n_extra:n_extra + len(outs)]
        i, j, k = pl.program_id(0), pl.program_id(1), pl.program_id(2)
        if nk == 1:
            epilogue(_dot(a_ref[...].astype(BF16), b_ref[...].astype(BF16), dims), extra_refs, out_refs, (i, j))
            return
        acc_ref = rest[-1]

        @pl.when(k == 0)
        def _():
            acc_ref[...] = jnp.zeros_like(acc_ref)

        acc_ref[...] += _dot(a_ref[...].astype(BF16), b_ref[...].astype(BF16), dims)

        @pl.when(k == nk - 1)
        def _():
            epilogue(acc_ref[...], extra_refs, out_refs, (i, j))

    res = pl.pallas_call(
        body,
        name=name,
        grid=grid,
        in_specs=[a_spec, b_spec] + [s for _, s in extra],
        out_specs=[s for _, s in outs],
        out_shape=[s for s, _ in outs],
        scratch_shapes=[pltpu.VMEM(acc_shape, F32)] if nk > 1 else [],
        compiler_params=_cparams("arbitrary", "arbitrary", "arbitrary", vmem=vmem),
    )(a, b, *[x for x, _ in extra])
    return res


def _epi_store(acc, extra_refs, out_refs, ij):
    out_refs[0][...] = acc.astype(out_refs[0].dtype)


def _rms_rows(x, g):
    r = lax.rsqrt(jnp.mean(x * x, axis=-1, keepdims=True) + EPS)
    return x * r, r


def _rmsnorm_bwd_rows(dh, x, g):
    xhat, r = _rms_rows(x, g)
    dxh = dh * g
    dx = r * (dxh - xhat * jnp.mean(dxh * xhat, axis=-1, keepdims=True))
    return dx, jnp.sum(dh * xhat, axis=0, keepdims=True)


def _rmsnorm(name, x, g, tb):
    T, Dm = x.shape

    def body(x_ref, g_ref, o_ref):
        xhat, _ = _rms_rows(x_ref[...], None)
        o_ref[...] = (xhat * g_ref[...]).astype(o_ref.dtype)

    return pl.pallas_call(
        body, name=name, grid=(T // tb,),
        in_specs=[pl.BlockSpec((tb, Dm), lambda i: (i, 0)), pl.BlockSpec((1, Dm), lambda i: (0, 0))],
        out_specs=pl.BlockSpec((tb, Dm), lambda i: (i, 0)),
        out_shape=jax.ShapeDtypeStruct((T, Dm), BF16),
        compiler_params=_cparams("parallel"),
    )(x, g)


def _head_norm(x, gm, gain):
    ms = _group_mean(x * x, gm)
    r = lax.rsqrt(ms + EPS)
    return x * r * gain, x * r


def _head_norm_bwd(dy, x, gm, gain):
    ms = _group_mean(x * x, gm)
    r = lax.rsqrt(ms + EPS)
    xhat = x * r
    dxh = dy * gain
    dx = r * (dxh - xhat * _group_mean(dxh * xhat, gm))
    return dx, jnp.sum(dy * xhat, axis=0, keepdims=True)


def _log_sigmoid(z):
    return jnp.minimum(z, 0.0) - jnp.log(1.0 + jnp.exp(-jnp.abs(z)))


def _prep_fwd(proj, gains, bfor, tril, gm64, gm128, T, tb):
    nb = T // tb

    def body(qa_ref, qf_ref, kf_ref, vf_ref, qm_ref, ka_ref, va_ref, fl_ref, gains_ref, bfor_ref, tril_ref,
             gm64_ref, gm128_ref,
             qa_o, qf_o, kf_o, vf_o, qm_o, kad_o, vad_o, qaug_o, kaug_o, carry):
        i = pl.program_id(0)
        gm64v = gm64_ref[...]
        gm128v = gm128_ref[...]
        lane = _lane((tb, LANES))

        def norm512(src, dst, row, gm, scale=1.0):
            gain = gains_ref[row:row + 1, :]
            for c in range(4):
                sl = slice(c * LANES, (c + 1) * LANES)
                y, _ = _head_norm(src[:, sl], gm, gain)
                dst[:, sl] = (y * scale).astype(dst.dtype)

        norm512(qa_ref, qa_o, 0, gm64v)
        norm512(qf_ref, qf_o, 2, gm64v, FOX_SCALE)
        norm512(kf_ref, kf_o, 3, gm64v)
        norm512(qm_ref, qm_o, 4, gm128v)
        vf_o[...] = vf_ref[...].astype(vf_o.dtype)

        ka_n, _ = _head_norm(ka_ref[...], gm64v, gains_ref[1:2, :])
        ka_r = pltpu.roll(ka_n, 64, 1)
        va = va_ref[...]
        va_r = pltpu.roll(va, 64, 1)
        lo = lane < 64
        kad_o[0] = jnp.where(lo, ka_n, ka_r).astype(kad_o.dtype)
        kad_o[1] = jnp.where(lo, ka_r, ka_n).astype(kad_o.dtype)
        vad_o[0] = jnp.where(lo, va, va_r).astype(vad_o.dtype)
        vad_o[1] = jnp.where(lo, va_r, va).astype(vad_o.dtype)

        @pl.when(i == 0)
        def _():
            carry[...] = jnp.zeros_like(carry)

        logf = jnp.where(lane < FOX_HEADS, _log_sigmoid(fl_ref[...] + bfor_ref[...]), 0.0)
        c = _dot3_left(tril_ref[...], logf) + carry[0:1, :]
        carry[...] = jnp.broadcast_to(c[tb - 1:tb, :], carry.shape)
        for pair in range(FOX_HEADS // 2):
            qaug = jnp.zeros((tb, LANES), F32)
            kaug = jnp.zeros((tb, LANES), F32)
            for sub in range(2):
                col = jnp.sum(jnp.where(lane == 2 * pair + sub, c, 0.0), axis=1, keepdims=True)
                pieces = [p.astype(F32) for p in _split3(col)]
                base = AUG_STRIDE * sub
                for e in range(3):
                    qaug = jnp.where(lane == base + AUG_C + e, pieces[e], qaug)
                    kaug = jnp.where(lane == base + AUG_NEG_C + e, -pieces[e], kaug)
                qaug = jnp.where((lane >= base + AUG_NEG_C) & (lane < base + AUG_NEG_C + 3), 1.0, qaug)
                ones_k = ((lane >= base + AUG_C) & (lane < base + AUG_C + 3)) | (
                    (lane >= base + AUG_STAT) & (lane < base + AUG_STAT + 3))
                kaug = jnp.where(ones_k, 1.0, kaug)
            sl = slice(pair * LANES, (pair + 1) * LANES)
            qaug_o[:, sl] = qaug.astype(BF16)
            kaug_o[:, sl] = kaug.astype(BF16)

    def seg(width, start):
        return pl.BlockSpec((tb, width), lambda i, s=start // width: (i, s))

    const = lambda shape: pl.BlockSpec(shape, lambda i: tuple(0 for _ in shape))
    rows512 = pl.BlockSpec((tb, 512), lambda i: (i, 0))
    outs = pl.pallas_call(
        body, name="prep_fwd", grid=(nb,),
        in_specs=[seg(512, C_QA), seg(512, C_QF), seg(512, C_KF), seg(512, C_VF), seg(512, C_QM),
                  seg(128, C_KA), seg(128, C_VA), seg(128, C_FL),
                  const((8, LANES)), const((1, LANES)), const((tb, tb)), const((2 * LANES, LANES)), const((2 * LANES, LANES))],
        out_specs=[rows512, rows512, rows512, rows512, rows512,
                   pl.BlockSpec((2, tb, LANES), lambda i: (0, i, 0)), pl.BlockSpec((2, tb, LANES), lambda i: (0, i, 0)),
                   rows512, rows512],
        out_shape=[jax.ShapeDtypeStruct((T, 512), BF16)] * 5
        + [jax.ShapeDtypeStruct((2, T, LANES), BF16)] * 2
        + [jax.ShapeDtypeStruct((T, 512), BF16)] * 2,
        scratch_shapes=[pltpu.VMEM((8, LANES), F32)],
        compiler_params=_cparams("arbitrary", vmem=VMEM_MID),
    )(proj, proj, proj, proj, proj, proj, proj, proj, gains, bfor, tril, gm64, gm128)
    return outs


def _prep_bwd(proj, dqa, dkad, dvad, dqf, dkf, dvf, dqm, dqf_aug, dkf_aug, gains, bfor, triu, gm64, gm128, T, tb):
    nb = T // tb

    def body(qa_ref, qf_ref, kf_ref, qm_ref, ka_ref, fl_ref,
             dqa_ref, dkad_ref, dvad_ref, dqf_ref, dkf_ref, dvf_ref, dqm_ref, dqfa_ref, dkfa_ref,
             gains_ref, bfor_ref, triu_ref, gm64_ref, gm128_ref,
             dlo_o, gacc_o, carry):
        i = pl.program_id(0)
        gm64v = gm64_ref[...]
        gm128v = gm128_ref[...]
        lane = _lane((tb, LANES))

        @pl.when(i == 0)
        def _():
            carry[...] = jnp.zeros_like(carry)
            gacc_o[...] = jnp.zeros_like(gacc_o)

        def norm512_bwd(dsrc, xsrc, col0, row, gm):
            gain = gains_ref[row:row + 1, :]
            gsum = jnp.zeros((1, LANES), F32)
            for c in range(4):
                sl = slice(c * LANES, (c + 1) * LANES)
                dx, dg = _head_norm_bwd(dsrc[:, sl], xsrc[:, sl], gm, gain)
                dlo_o[:, col0 + c * LANES:col0 + (c + 1) * LANES] = dx.astype(dlo_o.dtype)
                gsum = gsum + dg
            gacc_o[row:row + 1, :] += gsum

        norm512_bwd(dqa_ref, qa_ref, C_QA, 0, gm64v)
        norm512_bwd(dqf_ref, qf_ref, C_QF, 2, gm64v)
        norm512_bwd(dkf_ref, kf_ref, C_KF, 3, gm64v)
        norm512_bwd(dqm_ref, qm_ref, C_QM, 4, gm128v)
        dlo_o[:, C_VF:C_VF + 512] = dvf_ref[...].astype(dlo_o.dtype)

        lo = lane < 64

        def fold(ref):
            f0 = ref[0] + pltpu.roll(ref[0], 64, 1)
            f1 = ref[1] + pltpu.roll(ref[1], 64, 1)
            return jnp.where(lo, f0, f1)

        dka, dg = _head_norm_bwd(fold(dkad_ref), ka_ref[...], gm64v, gains_ref[1:2, :])
        gacc_o[1:2, :] += dg
        dlo_o[:, C_KA:C_KA + LANES] = dka.astype(dlo_o.dtype)
        dlo_o[:, C_VA:C_VA + LANES] = fold(dvad_ref).astype(dlo_o.dtype)

        dc = jnp.zeros((tb, LANES), F32)
        for pair in range(FOX_HEADS // 2):
            sl = slice(pair * LANES, (pair + 1) * LANES)
            rows_sum, cols_sum = dqfa_ref[:, sl], dkfa_ref[:, sl]
            for sub in range(2):
                diff = (jnp.where(lane == AUG_STRIDE * sub + AUG_C, rows_sum, 0.0)
                        - jnp.where(lane == AUG_STRIDE * sub + AUG_NEG_C, cols_sum, 0.0))
                dc = jnp.where(lane == 2 * pair + sub, jnp.sum(diff, axis=1, keepdims=True), dc)
        dlogf = _dot3_left(triu_ref[...], dc) + carry[0:1, :]
        carry[...] = jnp.broadcast_to(dlogf[0:1, :], carry.shape)
        z = fl_ref[...] + bfor_ref[...]
        dfl = jnp.where(lane < FOX_HEADS, dlogf / (1.0 + jnp.exp(z)), 0.0)
        gacc_o[5:6, :] += jnp.sum(dfl, axis=0, keepdims=True)
        dlo_o[:, C_FL:C_FL + LANES] = dfl.astype(dlo_o.dtype)
        dlo_o[:, C_FL + LANES:C_FL + 2 * LANES] = jnp.zeros((tb, LANES), dlo_o.dtype)

    rev = lambda i: nb - 1 - i

    def seg(width, start):
        return pl.BlockSpec((tb, width), lambda i, s=start // width: (rev(i), s))

    const = lambda shape: pl.BlockSpec(shape, lambda i: tuple(0 for _ in shape))
    rows512 = pl.BlockSpec((tb, 512), lambda i: (rev(i), 0))
    dup = pl.BlockSpec((2, tb, LANES), lambda i: (0, rev(i), 0))
    return pl.pallas_call(
        body, name="prep_bwd", grid=(nb,),
        in_specs=[seg(512, C_QA), seg(512, C_QF), seg(512, C_KF), seg(512, C_QM), seg(128, C_KA), seg(128, C_FL),
                  rows512, dup, dup, rows512, rows512, rows512, rows512, rows512, rows512,
                  const((8, LANES)), const((1, LANES)), const((tb, tb)), const((2 * LANES, LANES)), const((2 * LANES, LANES))],
        out_specs=[pl.BlockSpec((tb, LO_W), lambda i: (rev(i), 0)), const((8, LANES))],
        out_shape=[jax.ShapeDtypeStruct((T, LO_W), BF16), jax.ShapeDtypeStruct((8, LANES), F32)],
        scratch_shapes=[pltpu.VMEM((8, LANES), F32)],
        compiler_params=_cparams("arbitrary", vmem=VMEM_MID),
    )(proj, proj, proj, proj, proj, proj, dqa, dkad, dvad, dqf, dkf, dvf, dqm, dqf_aug, dkf_aug,
      gains, bfor, triu, gm64, gm128)


FOX_SCALE = FOX_HEAD_DIM ** -0.5
AUG_STRIDE = 16
AUG_C = 0
AUG_NEG_C = 3
AUG_STAT = 6
FOX_TQ, FOX_TK = 1024, 1024
FOX_BWD_TQ, FOX_BWD_TK = 1024, 1024
FOX_DIAGONAL_PARTS = 4


def _fox_head_mask(sub, rows):
    lane = _lane((rows, 2 * LANES))
    main = (lane >= 64 * sub) & (lane < 64 * sub + 64)
    aug = (lane >= LANES + AUG_STRIDE * sub) & (lane < LANES + AUG_STRIDE * (sub + 1))
    return main | aug


def _fox_pieces(diagonal, tq, tk):
    if diagonal and tq == tk and tq >= FOX_DIAGONAL_PARTS * LANES:
        step = tq // FOX_DIAGONAL_PARTS
        return [(n * step, (n + 1) * step, (n + 1) * step) for n in range(FOX_DIAGONAL_PARTS)]
    return [(0, tq, tk)]


def _fox_fwd(q, qaug, k, kaug, v, T, tq, tk):
    nq, nk = T // tq, T // tk
    rep = tk // LANES
    last_of = lambda i: (i * tq + tq - 1) // tk

    def body(q_ref, qa_ref, k_ref, ka_ref, v_ref, o_ref, qab_ref, m_s, acc_s):
        p_, i, j = pl.program_id(0), pl.program_id(1), pl.program_id(2)
        last = last_of(i)

        @pl.when(j == 0)
        def _():
            m_s[...] = jnp.full(m_s.shape, NEG, F32)
            acc_s[...] = jnp.zeros_like(acc_s)

        def step(diagonal):
            k2 = jnp.concatenate([k_ref[...], ka_ref[...]], axis=1)
            v2 = jnp.concatenate([v_ref[...], ka_ref[...]], axis=1)
            pieces = _fox_pieces(diagonal, tq, tk)
            work = []
            for r0, r1, nc in pieces:
                rows = slice(r0, r1)
                q2 = jnp.concatenate([q_ref[rows, :], qa_ref[rows, :]], axis=1)
                for sub in range(2):
                    qh = jnp.where(_fox_head_mask(sub, r1 - r0), q2, jnp.zeros_like(q2))
                    work.append((rows, r0, r1 - r0, nc, sub, _dot(qh, k2[:nc], NT)))
            for rows, r0, nr, nc, sub, s in work:
                if diagonal:
                    causal = (lax.broadcasted_iota(jnp.int32, (nr, nc), 1) + j * tk
                              <= lax.broadcasted_iota(jnp.int32, (nr, nc), 0) + (r0 + i * tq))
                    s = jnp.where(causal, s, NEG)
                m_prev = m_s[sub, rows, :]
                m_next = jnp.maximum(m_prev, jnp.max(s, axis=1, keepdims=True))
                p = jnp.exp(s - jnp.tile(m_next, (1, nc // LANES)))
                alpha = jnp.exp(m_prev - m_next)
                m_s[sub, rows, :] = m_next
                acc_s[sub, rows, :] = acc_s[sub, rows, :] * jnp.tile(alpha, (1, 2)) + _dot(p.astype(BF16), v2[:nc])

        @pl.when(j == last)
        def _():
            step(True)

        @pl.when(j < last)
        def _():
            step(False)

        @pl.when(j == nk - 1)
        def _():
            lane = _lane((tq, LANES))
            outs = []
            qab = qa_ref[...].astype(F32)
            for sub in range(2):
                acc = acc_s[sub]
                base = AUG_STRIDE * sub
                l = jnp.sum(jnp.where(lane == base + AUG_C, acc[:, LANES:], 0.0), axis=1, keepdims=True)
                outs.append(acc[:, :LANES] / l)
                lse = jnp.max(m_s[sub], axis=1, keepdims=True) + jnp.log(l)
                pieces = _split3(-lse)
                for e in range(3):
                    qab = jnp.where(lane == base + AUG_STAT + e, pieces[e].astype(F32), qab)
            o_ref[...] = jnp.where(lane < 64, outs[0], outs[1]).astype(o_ref.dtype)
            qab_ref[...] = qab.astype(BF16)

    qspec = pl.BlockSpec((tq, LANES), lambda p, i, j: (i, p))
    kspec = pl.BlockSpec((tk, LANES), lambda p, i, j: (jnp.minimum(j, last_of(i)), p))
    return pl.pallas_call(
        body, name="fox_fwd", grid=(4, nq, nk),
        in_specs=[qspec, qspec, kspec, kspec, kspec],
        out_specs=[qspec, qspec],
        out_shape=[jax.ShapeDtypeStruct((T, 512), BF16), jax.ShapeDtypeStruct((T, 512), BF16)],
        scratch_shapes=[pltpu.VMEM((2, tq, LANES), F32), pltpu.VMEM((2, tq, 2 * LANES), F32)],
        compiler_params=_cparams("parallel", "parallel", "arbitrary", vmem=VMEM_BIG),
    )(q, qaug, k, kaug, v)


def _fox_bwd(q, qaug, k, kaug, v, do, doaug, T, tq, tk):
    nq, nk = T // tq, T // tk
    first_of = lambda j: (j * tk) // tq

    def body(q_ref, qa_ref, k_ref, ka_ref, v_ref, do_ref, doa_ref,
             dq_ref, dqa_ref, dk_ref, dka_ref, dv_ref, dk_s, dv_s):
        p_, j, i = pl.program_id(0), pl.program_id(1), pl.program_id(2)
        masked = i * tq < (j + 1) * tk - 1

        @pl.when((j == 0) & (i == 0))
        def _():
            dq_ref[...] = jnp.zeros_like(dq_ref)
            dqa_ref[...] = jnp.zeros_like(dqa_ref)

        @pl.when(i == 0)
        def _():
            dk_s[...] = jnp.zeros_like(dk_s)
            dv_s[...] = jnp.zeros_like(dv_s)

        def step(diagonal):
            k2 = jnp.concatenate([k_ref[...], ka_ref[...]], axis=1)
            v2 = jnp.concatenate([v_ref[...], ka_ref[...]], axis=1)
            work = []
            for r0, r1, nc in _fox_pieces(diagonal, tq, tk):
                rows = slice(r0, r1)
                q2 = jnp.concatenate([q_ref[rows, :], qa_ref[rows, :]], axis=1)
                do2 = jnp.concatenate([do_ref[rows, :], doa_ref[rows, :]], axis=1)
                for sub in range(2):
                    hm = _fox_head_mask(sub, r1 - r0)
                    qh = jnp.where(hm, q2, jnp.zeros_like(q2))
                    doh = jnp.where(hm, do2, jnp.zeros_like(do2))
                    s = _dot(qh, k2[:nc], NT)
                    dp = _dot(doh, v2[:nc], NT)
                    work.append((r0, r1 - r0, nc, sub, qh, doh, s, dp))
            dqs = {}
            for r0, nr, nc, sub, qh, doh, s, dp in work:
                if diagonal:
                    causal = (lax.broadcasted_iota(jnp.int32, (nr, nc), 1) + j * tk
                              <= lax.broadcasted_iota(jnp.int32, (nr, nc), 0) + (r0 + i * tq))
                    s = jnp.where(causal, s, NEG)
                p = jnp.exp(s)
                dsb = (p * dp).astype(BF16)
                dv_s[0:nc, :] += _dot(p.astype(BF16), doh[:, :LANES], TN)
                dk_s[0:nc, :] += _dot(dsb, qh, TN)
                dqs[(r0, sub)] = _dot(dsb, k2[:nc])
            for r0, r1, nc in _fox_pieces(diagonal, tq, tk):
                dq2 = jnp.where(_fox_head_mask(0, r1 - r0), dqs[(r0, 0)], dqs[(r0, 1)])
                qrows = pl.ds(pl.multiple_of(i * tq + r0, r1 - r0), r1 - r0)
                dq_ref[qrows, :] += dq2[:, :LANES] * FOX_SCALE
                dqa_ref[qrows, :] += dq2[:, LANES:]

        @pl.when((i >= first_of(j)) & masked)
        def _():
            step(True)

        @pl.when((i >= first_of(j)) & jnp.logical_not(masked))
        def _():
            step(False)

        @pl.when(i == nq - 1)
        def _():
            dk_ref[...] = dk_s[:, :LANES]
            dka_ref[...] = dk_s[:, LANES:]
            dv_ref[...] = dv_s[...]

    qspec = pl.BlockSpec((tq, LANES), lambda p, j, i: (jnp.maximum(i, first_of(j)), p))
    kspec = pl.BlockSpec((tk, LANES), lambda p, j, i: (j, p))
    resident = pl.BlockSpec((T, LANES), lambda p, j, i: (0, p))
    return pl.pallas_call(
        body, name="fox_bwd", grid=(4, nk, nq),
        in_specs=[qspec, qspec, kspec, kspec, kspec, qspec, qspec],
        out_specs=[resident, resident, kspec, kspec, kspec],
        out_shape=[jax.ShapeDtypeStruct((T, 512), F32)] * 5,
        scratch_shapes=[pltpu.VMEM((tk, 2 * LANES), F32), pltpu.VMEM((tk, LANES), F32)],
        compiler_params=_cparams("arbitrary", "arbitrary", "arbitrary", vmem=VMEM_BIG),
    )(q, qaug, k, kaug, v, do, doaug)


SWA_SUB = 16
SWA_TB = SWA_SUB * WINDOW


def _t5_bucket_matrix():
    t = jnp.arange(WINDOW)[:, None] + WINDOW
    s = jnp.arange(2 * WINDOW)[None, :]
    max_exact = REL_BUCKETS // 2
    d = jnp.maximum(t - s, 0)
    df = jnp.maximum(d, 1).astype(F32)
    large = max_exact + (jnp.log(df / max_exact) / math.log(REL_MAX_DIST / max_exact)
                         * (REL_BUCKETS - max_exact)).astype(jnp.int32)
    large = jnp.minimum(large, REL_BUCKETS - 1)
    return jnp.where(d < max_exact, d, large).astype(jnp.int32)


def _swa_bias(rel_bias, bucket):
    def body(rel_ref, bucket_ref, o_ref):
        b = bucket_ref[...]
        for h in range(SWA_HEADS):
            acc = jnp.zeros(b.shape, F32)
            for r in range(REL_BUCKETS):
                acc = jnp.where(b == r, rel_ref[r, h], acc)
            o_ref[h] = acc

    return pl.pallas_call(
        body, name="swa_bias",
        in_specs=[pl.BlockSpec(memory_space=pltpu.SMEM), pl.BlockSpec(memory_space=pltpu.VMEM)],
        out_specs=pl.BlockSpec(memory_space=pltpu.VMEM),
        out_shape=jax.ShapeDtypeStruct((SWA_HEADS, WINDOW, 2 * WINDOW), F32),
    )(rel_bias, bucket)


def _swa_bias_bwd(dbias, bucket):
    def body(db_ref, bucket_ref, o_ref):
        b = bucket_ref[...]
        lane = _lane((1, LANES))
        for r in range(REL_BUCKETS):
            row = jnp.zeros((1, LANES), F32)
            for h in range(SWA_HEADS):
                part = jnp.sum(jnp.where(b == r, db_ref[h], 0.0), axis=0, keepdims=True)
                tot = jnp.sum(part, axis=1, keepdims=True)
                row = jnp.where(lane == h, tot, row)
            o_ref[r:r + 1, :] = row

    return pl.pallas_call(
        body, name="swa_bias_bwd",
        in_specs=[pl.BlockSpec(memory_space=pltpu.VMEM), pl.BlockSpec(memory_space=pltpu.VMEM)],
        out_specs=pl.BlockSpec(memory_space=pltpu.VMEM),
        out_shape=jax.ShapeDtypeStruct((REL_BUCKETS, LANES), F32),
    )(dbias, bucket)


SWA_GROUP = SWA_HEADS // SWA_KV_HEADS


def _swa_valid(r, i):
    t = (lax.broadcasted_iota(jnp.int32, (SWA_GROUP * WINDOW, 2 * WINDOW), 0) & (WINDOW - 1)) + WINDOW
    s = lax.broadcasted_iota(jnp.int32, (SWA_GROUP * WINDOW, 2 * WINDOW), 1)
    dist = t - s
    band = (dist >= 0) & (dist < WINDOW)
    if r == 0:
        band = band & ((s >= WINDOW) | (i > 0))
    return band


def _swa_stack(blk):
    lane = _lane((WINDOW, LANES))
    parts = []
    for g in range(SWA_GROUP):
        b = blk[:, LANES * (g // 2):LANES * (g // 2 + 1)]
        parts.append(jnp.where((lane >= 64) if g % 2 else (lane < 64), b, jnp.zeros_like(b)))
    return jnp.concatenate(parts, axis=0)


def _swa_unstack(st):
    lane = _lane((WINDOW, LANES))
    W = WINDOW
    return jnp.concatenate([jnp.where(lane < 64, st[2 * b * W:(2 * b + 1) * W], st[(2 * b + 1) * W:(2 * b + 2) * W])
                            for b in range(2)], axis=1)


def _swa_sink_column(sink_ref, kvh):
    row = lax.broadcasted_iota(jnp.int32, (SWA_GROUP * WINDOW, 1), 0)
    col = jnp.full((SWA_GROUP * WINDOW, 1), sink_ref[SWA_GROUP * kvh + SWA_GROUP - 1], F32)
    for g in range(SWA_GROUP - 2, -1, -1):
        col = jnp.where(row < (g + 1) * WINDOW, sink_ref[SWA_GROUP * kvh + g], col)
    return col


def _swa_specs(T):
    W = WINDOW
    qspec = pl.BlockSpec((SWA_TB, 2 * LANES), lambda h, i: (i, h))
    own = pl.BlockSpec((None, SWA_TB, LANES), lambda h, i: (h, i, 0))
    prev = pl.BlockSpec((None, W, LANES), lambda h, i: (h, jnp.maximum(SWA_SUB * i - 1, 0), 0))
    stat = pl.BlockSpec((SWA_GROUP, SWA_TB, LANES), lambda h, i: (h, i, 0))
    bias = pl.BlockSpec((None, SWA_GROUP * W, 2 * W), lambda h, i: (h, 0, 0))
    return qspec, own, prev, stat, bias


def _swa_fwd(sinks, q, kad, vad, bias, T):
    nb = T // SWA_TB
    scale = SWA_HEAD_DIM ** -0.5
    W = WINDOW

    def body(sink_ref, q_ref, k_ref, kp_ref, v_ref, vp_ref, bias_ref, o_ref, lse_ref):
        kvh, i = pl.program_id(0), pl.program_id(1)
        sink = _swa_sink_column(sink_ref, kvh)
        for r in range(SWA_SUB):
            rs = slice(r * W, (r + 1) * W)
            ps = slice((r - 1) * W, r * W)
            k_own, v_own = k_ref[rs, :], v_ref[rs, :]
            k_prev = kp_ref[...] if r == 0 else k_ref[ps, :]
            v_prev = vp_ref[...] if r == 0 else v_ref[ps, :]
            qs = _swa_stack(q_ref[rs, :])
            s = jnp.concatenate([_dot(qs, k_prev, NT), _dot(qs, k_own, NT)], axis=1) * scale + bias_ref[...]
            s = jnp.where(_swa_valid(r, i), s, NEG)
            m = jnp.maximum(jnp.max(s, axis=1, keepdims=True), sink)
            p = jnp.exp(s - m)
            denom = jnp.sum(p, axis=1, keepdims=True) + jnp.exp(sink - m)
            pn = (p / denom).astype(BF16)
            o_ref[rs, :] = _swa_unstack(_dot(pn[:, :W], v_prev) + _dot(pn[:, W:], v_own)).astype(o_ref.dtype)
            lse = m + jnp.log(denom)
            for g in range(SWA_GROUP):
                lse_ref[g, rs, :] = jnp.broadcast_to(lse[g * W:(g + 1) * W], (W, LANES))

    qspec, own, prev, stat, bspec = _swa_specs(T)
    return pl.pallas_call(
        body, name="swa_fwd", grid=(SWA_KV_HEADS, nb),
        in_specs=[pl.BlockSpec(memory_space=pltpu.SMEM), qspec, own, prev, own, prev, bspec],
        out_specs=[qspec, stat],
        out_shape=[jax.ShapeDtypeStruct((T, 512), BF16), jax.ShapeDtypeStruct((SWA_HEADS, T, LANES), F32)],
        compiler_params=_cparams("parallel", "parallel", vmem=VMEM_MID),
    )(sinks, q, kad, kad, vad, vad, bias.reshape(SWA_KV_HEADS, SWA_GROUP * W, 2 * W))


def _swa_bwd(sinks, q, kad, vad, bias, do, lse, delta, T):
    nb = T // SWA_TB
    scale = SWA_HEAD_DIM ** -0.5
    W = WINDOW

    def body(sink_ref, q_ref, k_ref, kp_ref, v_ref, vp_ref, bias_ref, do_ref, lse_ref, dl_ref,
             dq_ref, dkad_ref, dvad_ref, dbias_ref, dsk_ref):
        kvh, i = pl.program_id(0), pl.program_id(1)
        sink = _swa_sink_column(sink_ref, kvh)

        @pl.when((kvh == 0) & (i == 0))
        def _():
            dkad_ref[...] = jnp.zeros_like(dkad_ref)
            dvad_ref[...] = jnp.zeros_like(dvad_ref)

        @pl.when(i == 0)
        def _():
            dbias_ref[...] = jnp.zeros_like(dbias_ref)
            dsk_ref[...] = jnp.zeros_like(dsk_ref)

        for r in range(SWA_SUB):
            rs = slice(r * W, (r + 1) * W)
            ps = slice((r - 1) * W, r * W)
            k_own, v_own = k_ref[rs, :], v_ref[rs, :]
            k_prev = kp_ref[...] if r == 0 else k_ref[ps, :]
            v_prev = vp_ref[...] if r == 0 else v_ref[ps, :]
            qs = _swa_stack(q_ref[rs, :])
            dos = _swa_stack(do_ref[rs, :])
            lse_b = jnp.concatenate([lse_ref[g, rs, :] for g in range(SWA_GROUP)], axis=0)
            dl_b = jnp.concatenate([dl_ref[g, rs, :] for g in range(SWA_GROUP)], axis=0)
            s = jnp.concatenate([_dot(qs, k_prev, NT), _dot(qs, k_own, NT)], axis=1) * scale + bias_ref[...]
            s = jnp.where(_swa_valid(r, i), s, NEG)
            p = jnp.exp(s - jnp.tile(lse_b, (1, 2)))
            dp = jnp.concatenate([_dot(dos, v_prev, NT), _dot(dos, v_own, NT)], axis=1)
            ds = p * (dp - jnp.tile(dl_b, (1, 2)))
            sink_term = jnp.exp(sink - lse_b) * dl_b
            for g in range(SWA_GROUP):
                dbias_ref[g] += ds[g * W:(g + 1) * W]
                dsk_ref[g:g + 1, :] += jnp.sum(sink_term[g * W:(g + 1) * W], axis=0, keepdims=True)
            dsb = ds.astype(BF16)
            pb = p.astype(BF16)
            dq_ref[rs, :] = _swa_unstack((_dot(dsb[:, :W], k_prev) + _dot(dsb[:, W:], k_own)) * scale)
            own_row = pl.multiple_of(i * SWA_TB + r * W, W)
            dkad_ref[kvh, pl.ds(own_row, W), :] += _dot(dsb[:, W:], qs, TN) * scale
            dvad_ref[kvh, pl.ds(own_row, W), :] += _dot(pb[:, W:], dos, TN)
            dk_prev = _dot(dsb[:, :W], qs, TN) * scale
            dv_prev = _dot(pb[:, :W], dos, TN)
            if r == 0:
                @pl.when(i > 0)
                def _():
                    prev_row = pl.multiple_of(i * SWA_TB - W, W)
                    dkad_ref[kvh, pl.ds(prev_row, W), :] += dk_prev
                    dvad_ref[kvh, pl.ds(prev_row, W), :] += dv_prev
            else:
                prev_row = pl.multiple_of(i * SWA_TB + (r - 1) * W, W)
                dkad_ref[kvh, pl.ds(prev_row, W), :] += dk_prev
                dvad_ref[kvh, pl.ds(prev_row, W), :] += dv_prev

    qspec, own, prev, stat, bspec = _swa_specs(T)
    full = pl.BlockSpec((SWA_KV_HEADS, T, LANES), lambda h, i: (0, 0, 0))
    return pl.pallas_call(
        body, name="swa_bwd", grid=(SWA_KV_HEADS, nb),
        in_specs=[pl.BlockSpec(memory_space=pltpu.SMEM), qspec, own, prev, own, prev, bspec, qspec, stat, stat],
        out_specs=[qspec, full, full, pl.BlockSpec((SWA_GROUP, W, 2 * W), lambda h, i: (h, 0, 0)),
                   pl.BlockSpec((None, 8, LANES), lambda h, i: (h, 0, 0))],
        out_shape=[jax.ShapeDtypeStruct((T, 512), F32), jax.ShapeDtypeStruct((SWA_KV_HEADS, T, LANES), F32),
                   jax.ShapeDtypeStruct((SWA_KV_HEADS, T, LANES), F32), jax.ShapeDtypeStruct((SWA_HEADS, W, 2 * W), F32),
                   jax.ShapeDtypeStruct((SWA_KV_HEADS, 8, LANES), F32)],
        compiler_params=_cparams("arbitrary", "arbitrary", vmem=VMEM_MID),
    )(sinks, q, kad, kad, vad, vad, bias.reshape(SWA_KV_HEADS, SWA_GROUP * W, 2 * W), do, lse, delta)


MEM_TQ = 4096


def _mem_fwd(q, mk, mv, T, tq):
    scale = MEM_HEAD_DIM ** -0.5

    def body(q_ref, k_ref, v_ref, o_ref, lse_ref):
        s = _dot(q_ref[...], k_ref[...], NT) * scale
        m = jnp.max(s, axis=1, keepdims=True)
        p = jnp.exp(s - m)
        l = jnp.sum(p, axis=1, keepdims=True)
        o_ref[...] = _dot((p / l).astype(BF16), v_ref[...]).astype(o_ref.dtype)
        lse_ref[...] = jnp.broadcast_to(m + jnp.log(l), (tq, LANES))

    qspec = pl.BlockSpec((tq, LANES), lambda h, i: (i, h))
    kspec = pl.BlockSpec((N_MEM, LANES), lambda h, i: (0, h))
    return pl.pallas_call(
        body, name="mem_fwd", grid=(MEM_HEADS, T // tq),
        in_specs=[qspec, kspec, kspec],
        out_specs=[qspec, pl.BlockSpec((None, tq, LANES), lambda h, i: (h, i, 0))],
        out_shape=[jax.ShapeDtypeStruct((T, 512), BF16), jax.ShapeDtypeStruct((MEM_HEADS, T, LANES), F32)],
        compiler_params=_cparams("parallel", "parallel"),
    )(q, mk, mv)


def _mem_bwd(q, mk, mv, do, lse, delta, T, tq):
    scale = MEM_HEAD_DIM ** -0.5
    rep = N_MEM // LANES

    def body(q_ref, k_ref, v_ref, do_ref, lse_ref, dl_ref, dq_ref, dk_ref, dv_ref):
        i = pl.program_id(1)

        @pl.when(i == 0)
        def _():
            dk_ref[...] = jnp.zeros_like(dk_ref)
            dv_ref[...] = jnp.zeros_like(dv_ref)

        qv, dov = q_ref[...], do_ref[...]
        s = _dot(qv, k_ref[...], NT) * scale
        p = jnp.exp(s - jnp.tile(lse_ref[...], (1, rep)))
        dp = _dot(dov, v_ref[...], NT)
        ds = p * (dp - jnp.tile(dl_ref[...], (1, rep)))
        dsb = ds.astype(BF16)
        dq_ref[...] = _dot(dsb, k_ref[...]) * scale
        dk_ref[...] += _dot(dsb, qv, TN) * scale
        dv_ref[...] += _dot(p.astype(BF16), dov, TN)

    qspec = pl.BlockSpec((tq, LANES), lambda h, i: (i, h))
    kspec = pl.BlockSpec((N_MEM, LANES), lambda h, i: (0, h))
    stat = pl.BlockSpec((None, tq, LANES), lambda h, i: (h, i, 0))
    return pl.pallas_call(
        body, name="mem_bwd", grid=(MEM_HEADS, T // tq),
        in_specs=[qspec, kspec, kspec, qspec, stat, stat],
        out_specs=[qspec, kspec, kspec],
        out_shape=[jax.ShapeDtypeStruct((T, 512), F32), jax.ShapeDtypeStruct((N_MEM, 512), F32),
                   jax.ShapeDtypeStruct((N_MEM, 512), F32)],
        compiler_params=_cparams("arbitrary", "arbitrary"),
    )(q, mk, mv, do, lse, delta)


def _mem_prep_fwd(mem, g_mem, w_kv, kn_gain, gm128):
    def body(mem_ref, g_ref, w_ref, kn_ref, gm_ref, memn_o, kv_o, mk_o, mv_o):
        xhat, _ = _rms_rows(mem_ref[...], None)
        memn = (xhat * g_ref[...]).astype(BF16)
        memn_o[...] = memn
        kv = _dot(memn, w_ref[...])
        kv_o[...] = kv
        gm = gm_ref[...]
        for c in range(4):
            sl = slice(c * LANES, (c + 1) * LANES)
            y, _ = _head_norm(kv[:, sl], gm, kn_ref[...])
            mk_o[:, sl] = y.astype(BF16)
        mv_o[...] = kv[:, 512:].astype(BF16)

    vm = pl.BlockSpec(memory_space=pltpu.VMEM)
    return pl.pallas_call(
        body, name="mem_prep_fwd", in_specs=[vm] * 5, out_specs=[vm] * 4,
        out_shape=[jax.ShapeDtypeStruct((N_MEM, D_MODEL), BF16), jax.ShapeDtypeStruct((N_MEM, D_MODEL), F32),
                   jax.ShapeDtypeStruct((N_MEM, 512), BF16), jax.ShapeDtypeStruct((N_MEM, 512), BF16)],
        compiler_params=pltpu.CompilerParams(vmem_limit_bytes=VMEM_MID),
    )(mem, g_mem, w_kv, kn_gain, gm128)


def _mem_prep_bwd(mem, g_mem, memn, kv, w_kv, kn_gain, gm128, dmk, dmv):
    def body(mem_ref, g_ref, memn_ref, kv_ref, w_ref, kn_ref, gm_ref, dmk_ref, dmv_ref, dw_o, dg_o, dkn_o, dkv_s):
        gm = gm_ref[...]
        dkn = jnp.zeros((1, LANES), F32)
        for c in range(4):
            sl = slice(c * LANES, (c + 1) * LANES)
            dx, dg = _head_norm_bwd(dmk_ref[:, sl], kv_ref[:, sl], gm, kn_ref[...])
            dkv_s[:, sl] = dx.astype(BF16)
            dkn = dkn + dg
        dkn_o[...] = dkn
        dkv_s[:, 512:] = dmv_ref[...].astype(BF16)
        dkv = dkv_s[...]
        dw_o[...] = _dot(memn_ref[...], dkv, TN)
        dmemn = _dot(dkv, w_ref[...], NT)
        xhat, _ = _rms_rows(mem_ref[...], None)
        dg_o[...] = jnp.sum(dmemn * xhat, axis=0, keepdims=True)

    vm = pl.BlockSpec(memory_space=pltpu.VMEM)
    return pl.pallas_call(
        body, name="mem_prep_bwd", in_specs=[vm] * 9, out_specs=[vm] * 3,
        out_shape=[jax.ShapeDtypeStruct((D_MODEL, D_MODEL), F32), jax.ShapeDtypeStruct((1, D_MODEL), F32),
                   jax.ShapeDtypeStruct((1, LANES), F32)],
        scratch_shapes=[pltpu.VMEM((N_MEM, D_MODEL), BF16)],
        compiler_params=pltpu.CompilerParams(vmem_limit_bytes=VMEM_MID),
    )(mem, g_mem, memn, kv, w_kv, kn_gain, gm128, dmk, dmv)


SLOT_O = D_MODEL // N_SHARD


def _merge_fwd(proj, b_gate, o3, w3, T, tb):
    def body(gl_ref, bg_ref, oa_ref, of_ref, om_ref, wa_ref, wf_ref, wm_ref, out_ref):
        o_refs = (oa_ref, of_ref, om_ref)
        w_refs = (wa_ref, wf_ref, wm_ref)
        for n in range(N_SHARD):
            acc = jnp.zeros((tb, SLOT_O), F32)
            for b in range(3):
                c0 = b * D_MODEL + n * SLOT_O
                g = jax.nn.sigmoid(gl_ref[:, c0:c0 + SLOT_O] + bg_ref[:, c0:c0 + SLOT_O])
                acc = acc + g * _dot(o_refs[b][...], w_refs[b][n])
            out_ref[:, n * SLOT_O:(n + 1) * SLOT_O] = acc.astype(out_ref.dtype)

    rows = pl.BlockSpec((tb, 512), lambda i: (i, 0))
    wspec = pl.BlockSpec((N_SHARD, 512, SLOT_O), lambda i: (0, 0, 0))
    return pl.pallas_call(
        body, name="merge_fwd", grid=(T // tb,),
        in_specs=[pl.BlockSpec((tb, GATE_W), lambda i: (i, 1)), pl.BlockSpec((1, GATE_W), lambda i: (0, 0)),
                  rows, rows, rows, wspec, wspec, wspec],
        out_specs=pl.BlockSpec((tb, D_MODEL), lambda i: (i, 0)),
        out_shape=jax.ShapeDtypeStruct((T, D_MODEL), BF16),
        compiler_params=_cparams("parallel", vmem=VMEM_BIG),
    )(proj, b_gate, *o3, *w3)


def _merge_bwd(proj, b_gate, o3, w3, dmerged, T, tb):
    heads = (SWA_HEADS, FOX_HEADS, MEM_HEADS)

    def body(gl_ref, bg_ref, oa_ref, of_ref, om_ref, wa_ref, wf_ref, wm_ref, dm_ref,
             dgl_o, doa_o, dof_o, dom_o, dla_o, dlf_o, dlm_o, dwa_o, dwf_o, dwm_o, dbg_o):
        i = pl.program_id(0)
        o_refs = (oa_ref, of_ref, om_ref)
        w_refs = (wa_ref, wf_ref, wm_ref)
        do_refs = (doa_o, dof_o, dom_o)
        dl_refs = (dla_o, dlf_o, dlm_o)
        dw_refs = (dwa_o, dwf_o, dwm_o)

        @pl.when(i == 0)
        def _():
            for r in dw_refs:
                r[...] = jnp.zeros_like(r)
            dbg_o[...] = jnp.zeros_like(dbg_o)

        lane = _lane((tb, LANES))
        for b in range(3):
            ob = o_refs[b][...]
            do = jnp.zeros((tb, 512), F32)
            for n in range(N_SHARD):
                c0 = b * D_MODEL + n * SLOT_O
                g = jax.nn.sigmoid(gl_ref[:, c0:c0 + SLOT_O] + bg_ref[:, c0:c0 + SLOT_O])
                dm = dm_ref[:, n * SLOT_O:(n + 1) * SLOT_O]
                y = _dot(ob, w_refs[b][n])
                dgl = dm * y * g * (1.0 - g)
                dgl_o[:, c0:c0 + SLOT_O] = dgl.astype(dgl_o.dtype)
                dbg_o[:, c0:c0 + SLOT_O] += jnp.sum(dgl, axis=0, keepdims=True)
                dy = (dm * g).astype(BF16)
                do = do + _dot(dy, w_refs[b][n], NT)
                dw_refs[b][n] += _dot(ob, dy, TN)
            do_refs[b][...] = do.astype(BF16)
            prod = do * ob.astype(F32)
            for c in range(4):
                blk = prod[:, c * LANES:(c + 1) * LANES]
                if heads[b] == 8:
                    lo = jnp.sum(jnp.where(lane < 64, blk, 0.0), axis=1, keepdims=True)
                    hi = jnp.sum(jnp.where(lane >= 64, blk, 0.0), axis=1, keepdims=True)
                    if b == 1:
                        aug = jnp.zeros((tb, LANES), F32)
                        for sub, dl in enumerate((lo, hi)):
                            for e, piece in enumerate(_split3(-dl)):
                                aug = jnp.where(lane == AUG_STRIDE * sub + AUG_C + e, piece.astype(F32), aug)
                        dl_refs[b][:, c * LANES:(c + 1) * LANES] = aug.astype(BF16)
                    else:
                        dl_refs[b][2 * c] = jnp.broadcast_to(lo, (tb, LANES))
                        dl_refs[b][2 * c + 1] = jnp.broadcast_to(hi, (tb, LANES))
                else:
                    dl_refs[b][c] = jnp.broadcast_to(jnp.sum(blk, axis=1, keepdims=True), (tb, LANES))

    rows = pl.BlockSpec((tb, 512), lambda i: (i, 0))
    wspec = pl.BlockSpec((N_SHARD, 512, SLOT_O), lambda i: (0, 0, 0))
    stat = lambda h: pl.BlockSpec((h, tb, LANES), lambda i: (0, i, 0))
    return pl.pallas_call(
        body, name="merge_bwd", grid=(T // tb,),
        in_specs=[pl.BlockSpec((tb, GATE_W), lambda i: (i, 1)), pl.BlockSpec((1, GATE_W), lambda i: (0, 0)),
                  rows, rows, rows, wspec, wspec, wspec, pl.BlockSpec((tb, D_MODEL), lambda i: (i, 0))],
        out_specs=[pl.BlockSpec((tb, GATE_W), lambda i: (i, 0)), rows, rows, rows,
                   stat(8), rows, stat(4), wspec, wspec, wspec, pl.BlockSpec((1, GATE_W), lambda i: (0, 0))],
        out_shape=[jax.ShapeDtypeStruct((T, GATE_W), BF16)] + [jax.ShapeDtypeStruct((T, 512), BF16)] * 3
        + [jax.ShapeDtypeStruct((8, T, LANES), F32), jax.ShapeDtypeStruct((T, 512), BF16),
           jax.ShapeDtypeStruct((4, T, LANES), F32)]
        + [jax.ShapeDtypeStruct((N_SHARD, 512, SLOT_O), F32)] * 3 + [jax.ShapeDtypeStruct((1, GATE_W), F32)],
        compiler_params=_cparams("arbitrary", vmem=VMEM_BIG),
    )(proj, b_gate, *o3, *w3, dmerged)


def _local_step(x, h, mem, tgt, small, g_in, w_kv, w_o3, w_out, w_up, w_down, reducer):
    T = x.shape[0]
    tm = min(512, T)
    tile2 = lambda v: jnp.tile(v.reshape(1, -1), (1, LANES // v.size))
    gains = jnp.concatenate([tile2(small["qn_swa"]), tile2(small["kn_swa"]), tile2(small["qn_fox"]),
                             tile2(small["kn_fox"]), tile2(small["qn_mem"]), jnp.zeros((3, LANES), F32)], axis=0)
    kn_mem = small["kn_mem"].reshape(1, LANES)
    bfor = jnp.pad(small["b_forget"].reshape(1, -1), ((0, 0), (0, LANES - FOX_HEADS)))
    gm64 = _group_mean_matrix(64)
    gm128 = _group_mean_matrix(128)
    tb_prep = min(512, T)
    ones = jnp.ones((tb_prep, tb_prep), F32)
    tril = jnp.tril(ones).astype(BF16)
    triu = jnp.triu(ones).astype(BF16)
    bucket = _t5_bucket_matrix()
    g_mix, g_mlp, g_mem = small["g_mix"], small["g_mlp"], small["g_mem"]
    b_gate = small["b_gate"]
    sinks = small["sink_swa"].reshape(-1)

    tl = min(1024, T)
    sq = pl.BlockSpec((tl, D_MODEL), lambda i, j, k: (i, j))
    wc = _w_in_to_segments(g_in)
    (proj,) = _matmul(
        "mm_proj", h, wc, dims=NN, grid=(T // tl, PROJ_W // D_MODEL, 1),
        a_spec=pl.BlockSpec((tl, D_MODEL), lambda i, j, k: (i, 0)),
        b_spec=pl.BlockSpec((D_MODEL, D_MODEL), lambda i, j, k: (0, j)),
        acc_shape=(tl, D_MODEL),
        outs=[(jax.ShapeDtypeStruct((T, PROJ_W), F32), sq)],
        epilogue=_epi_store)
    qa, qf, kf, vf, qm, kad, vad, qf_aug, kf_aug = _prep_fwd(proj, gains, bfor, tril, gm64, gm128, T, tb_prep)
    bias = _swa_bias(small["rel_bias"], bucket)
    o_swa, lse_swa = _swa_fwd(sinks, qa, kad, vad, bias, T)
    o_fox, qf_aug_bwd = _fox_fwd(qf, qf_aug, kf, kf_aug, vf, T, min(FOX_TQ, T), min(FOX_TK, T))
    memn, kv, mk, mv = _mem_prep_fwd(mem, g_mem, w_kv, kn_mem, gm128)
    o_mem, lse_mem = _mem_fwd(qm, mk, mv, T, min(MEM_TQ, T))
    o3 = (o_swa, o_fox, o_mem)
    merged = _merge_fwd(proj, b_gate, o3, w_o3, T, min(512, T))

    def epi_residual(acc, extra_refs, out_refs, ij):
        out_refs[0][...] = extra_refs[0][...] + acc

    row_full = pl.BlockSpec((tm, D_MODEL), lambda i, j, k: (i, 0))
    row_big = pl.BlockSpec((tl, D_MODEL), lambda i, j, k: (i, 0))
    whole = pl.BlockSpec((D_MODEL, D_MODEL), lambda i, j, k: (0, 0))
    (x2,) = _matmul(
        "mm_out", merged, w_out, dims=NN, grid=(T // tl, 1, 1),
        a_spec=row_big, b_spec=whole,
        acc_shape=(tl, D_MODEL), extra=[(x, row_big)],
        outs=[(jax.ShapeDtypeStruct((T, D_MODEL), F32), row_big)], epilogue=epi_residual)
    hm = _rmsnorm("rms_mlp", x2, g_mlp, tm)

    def epi_relu2(acc, extra_refs, out_refs, ij):
        out_refs[0][...] = acc.astype(BF16)
        r = jnp.maximum(acc, 0.0)
        out_refs[1][...] = (r * r).astype(BF16)

    up, u = _matmul(
        "mm_up", hm, w_up, dims=NN, grid=(T // tl, N_SHARD, 1),
        a_spec=row_big, b_spec=pl.BlockSpec((None, D_MODEL, D_MODEL), lambda i, j, k: (j, 0, 0)),
        acc_shape=(tl, D_MODEL),
        outs=[(jax.ShapeDtypeStruct((T, D_FF), BF16), sq), (jax.ShapeDtypeStruct((T, D_FF), BF16), sq)],
        epilogue=epi_relu2)

    def epi_loss(acc, extra_refs, out_refs, ij):
        y = extra_refs[0][...] + acc
        err = y - extra_refs[1][...]
        dyv = err * (1.0 / D_MODEL)
        out_refs[0][...] = dyv
        out_refs[2][...] = dyv.astype(BF16)
        sq = jnp.sum(jnp.sum(err * err, axis=1, keepdims=True), axis=0, keepdims=True)

        @pl.when(ij[0] == 0)
        def _():
            out_refs[1][...] = jnp.zeros_like(out_refs[1])

        out_refs[1][...] += jnp.broadcast_to(sq, out_refs[1].shape)

    kblk = pl.BlockSpec((tl, D_MODEL), lambda i, j, k: (i, k))
    dy, loss_acc, dy_bf = _matmul(
        "mm_down", u, w_down, dims=NN, grid=(T // tl, 1, N_SHARD),
        a_spec=kblk, b_spec=pl.BlockSpec((D_MODEL, D_MODEL), lambda i, j, k: (k, 0)),
        acc_shape=(tl, D_MODEL), extra=[(x2, row_big), (tgt, row_big)],
        outs=[(jax.ShapeDtypeStruct((T, D_MODEL), F32), row_big),
              (jax.ShapeDtypeStruct((8, LANES), F32), pl.BlockSpec((8, LANES), lambda i, j, k: (0, 0))),
              (jax.ShapeDtypeStruct((T, D_MODEL), BF16), row_big)],
        epilogue=epi_loss)
    loss = loss_acc[0, 0] * (0.5 / D_MODEL)

    def epi_dup(acc, extra_refs, out_refs, ij):
        out_refs[0][...] = (acc * (2.0 * jnp.maximum(extra_refs[0][...].astype(F32), 0.0))).astype(BF16)

    (dup,) = _matmul(
        "mm_dup", dy_bf, w_down, dims=NT, grid=(T // tl, N_SHARD, 1),
        a_spec=row_big, b_spec=pl.BlockSpec((D_MODEL, D_MODEL), lambda i, j, k: (j, 0)),
        acc_shape=(tl, D_MODEL), extra=[(up, sq)],
        outs=[(jax.ShapeDtypeStruct((T, D_FF), BF16), sq)], epilogue=epi_dup)

    nkt = T // tl
    t_rows = pl.BlockSpec((tl, D_MODEL), lambda i, j, k: (k, i))
    t_cols = pl.BlockSpec((tl, D_MODEL), lambda i, j, k: (k, j))
    (d_w_down,) = _matmul(
        "mm_dw_down", u, dy_bf, dims=TN, grid=(N_SHARD, 1, nkt),
        a_spec=t_rows, b_spec=t_cols, acc_shape=(D_MODEL, D_MODEL),
        outs=[(jax.ShapeDtypeStruct((D_FF, D_MODEL), F32), pl.BlockSpec((D_MODEL, D_MODEL), lambda i, j, k: (i, 0)))],
        epilogue=_epi_store)
    (d_w_up,) = _matmul(
        "mm_dw_up", hm, dup, dims=TN, grid=(1, N_SHARD, nkt),
        a_spec=t_rows, b_spec=t_cols, acc_shape=(D_MODEL, D_MODEL),
        outs=[(jax.ShapeDtypeStruct((N_SHARD, D_MODEL, D_MODEL), F32),
               pl.BlockSpec((None, D_MODEL, D_MODEL), lambda i, j, k: (j, 0, 0)))],
        epilogue=_epi_store)

    def epi_rms_bwd(acc, extra_refs, out_refs, ij):
        dx, dg = _rmsnorm_bwd_rows(acc, extra_refs[0][...], extra_refs[1][...])
        out_refs[0][...] = dx + extra_refs[2][...]

        @pl.when(ij[0] == 0)
        def _():
            out_refs[1][...] = jnp.zeros_like(out_refs[1])

        out_refs[1][...] += dg

    gain_spec = pl.BlockSpec((1, D_MODEL), lambda i, j, k: (0, 0))
    dx2, d_g_mlp = _matmul(
        "mm_dhm", dup, w_up, dims=NT, grid=(T // tl, 1, N_SHARD),
        a_spec=kblk, b_spec=pl.BlockSpec((None, D_MODEL, D_MODEL), lambda i, j, k: (k, 0, 0)),
        acc_shape=(tl, D_MODEL), extra=[(x2, row_big), (g_mlp, gain_spec), (dy, row_big)],
        outs=[(jax.ShapeDtypeStruct((T, D_MODEL), F32), row_big), (jax.ShapeDtypeStruct((1, D_MODEL), F32), gain_spec)],
        epilogue=epi_rms_bwd)

    (dmerged,) = _matmul(
        "mm_dmerged", dx2, w_out, dims=NT, grid=(T // tl, 1, 1),
        a_spec=row_big, b_spec=whole,
        acc_shape=(tl, D_MODEL), outs=[(jax.ShapeDtypeStruct((T, D_MODEL), F32), row_big)], epilogue=_epi_store)
    (d_w_out,) = _matmul(
        "mm_dw_out", merged, dx2, dims=TN, grid=(1, 1, nkt),
        a_spec=t_rows, b_spec=t_cols, acc_shape=(D_MODEL, D_MODEL),
        outs=[(jax.ShapeDtypeStruct((D_MODEL, D_MODEL), F32), whole)],
        epilogue=_epi_store)
    (dgl, do_swa, do_fox, do_mem, dl_swa, do_fox_aug, dl_mem, d_wo_swa, d_wo_fox, d_wo_mem, d_b_gate) = _merge_bwd(
        proj, b_gate, o3, w_o3, dmerged, T, min(512, T))

    dqm, dmk, dmv = _mem_bwd(qm, mk, mv, do_mem, lse_mem, dl_mem, T, min(MEM_TQ, T))
    d_w_kv, d_g_mem, d_kn_mem = _mem_prep_bwd(mem, g_mem, memn, kv, w_kv, kn_mem, gm128, dmk, dmv)
    do_swa = reducer.early_start({"w_mlp_down": d_w_down, "w_mlp_up": d_w_up, "w_out": d_w_out, "w_mem_kv": d_w_kv,
                                  "w_o_swa": d_wo_swa, "w_o_fox": d_wo_fox, "w_o_mem": d_wo_mem}, do_swa)
    dqa, dkad, dvad, dbias, dsk = _swa_bwd(sinks, qa, kad, vad, bias, do_swa, lse_swa, dl_swa, T)
    dqa, do_fox = reducer.early_send((dqa, do_fox))
    dqf, dqf_aug, dkf, dkf_aug, dvf = _fox_bwd(qf, qf_aug_bwd, kf, kf_aug, vf, do_fox, do_fox_aug, T,
                                               min(FOX_BWD_TQ, T), min(FOX_BWD_TK, T))
    dvf = reducer.early_finish(dvf)
    d_rel = _swa_bias_bwd(dbias, bucket)
    dlo, gacc = _prep_bwd(proj, dqa, dkad, dvad, dqf, dkf, dvf, dqm, dqf_aug, dkf_aug, gains, bfor, triu, gm64, gm128,
                          T, tb_prep)

    def dwc_half(name, dpart):
        (res,) = _matmul(
            name, h, dpart, dims=TN, grid=(1, LO_W // D_MODEL, nkt),
            a_spec=t_rows, b_spec=t_cols, acc_shape=(D_MODEL, D_MODEL),
            outs=[(jax.ShapeDtypeStruct((D_MODEL, LO_W), F32), pl.BlockSpec((D_MODEL, D_MODEL), lambda i, j, k: (0, j)))],
            epilogue=_epi_store)
        return res

    d_wc_lo = dwc_half("mm_dwc_lo", dlo)
    d_wc_gl = dwc_half("mm_dwc_gl", dgl)
    dlo = reducer.late_start({"wc_lo": d_wc_lo, "wc_gl": d_wc_gl}, dlo)
    (dh_lo,) = _matmul(
        "mm_dh_lo", dlo, wc, dims=NT, grid=(T // tl, 1, LO_W // D_MODEL),
        a_spec=kblk, b_spec=pl.BlockSpec((D_MODEL, D_MODEL), lambda i, j, k: (0, k)),
        acc_shape=(tl, D_MODEL), outs=[(jax.ShapeDtypeStruct((T, D_MODEL), F32), row_big)], epilogue=_epi_store)
    dh_lo = reducer.late_send(dh_lo)

    def epi_dx(acc, extra_refs, out_refs, ij):
        dhh = acc + extra_refs[3][...]
        dx, dg = _rmsnorm_bwd_rows(dhh, extra_refs[0][...], extra_refs[1][...])
        out_refs[0][...] = dx + extra_refs[2][...]

        @pl.when(ij[0] == 0)
        def _():
            out_refs[1][...] = jnp.zeros_like(out_refs[1])

        out_refs[1][...] += dg

    grad_x, d_g_mix = _matmul(
        "mm_dh_gl", dgl, wc, dims=NT, grid=(T // tl, 1, GATE_W // D_MODEL),
        a_spec=kblk, b_spec=pl.BlockSpec((D_MODEL, D_MODEL), lambda i, j, k: (0, k + LO_W // D_MODEL)),
        acc_shape=(tl, D_MODEL), extra=[(x, row_big), (g_mix, gain_spec), (dx2, row_big), (dh_lo, row_big)],
        outs=[(jax.ShapeDtypeStruct((T, D_MODEL), F32), row_big), (jax.ShapeDtypeStruct((1, D_MODEL), F32), gain_spec)],
        epilogue=epi_dx, vmem=VMEM_MAX)

    fold64 = lambda row: (row[:64] + row[64:]).reshape(1, 64)
    grads = {
        "g_mix": d_g_mix, "b_gate": d_b_gate, "b_forget": gacc[5, :FOX_HEADS].reshape(1, FOX_HEADS),
        "qn_swa": fold64(gacc[0]), "kn_swa": fold64(gacc[1]),
        "sink_swa": -dsk[:, :SWA_GROUP, 0].reshape(1, SWA_HEADS), "rel_bias": d_rel[:, :SWA_HEADS],
        "qn_fox": fold64(gacc[2]), "kn_fox": fold64(gacc[3]),
        "g_mem": d_g_mem, "qn_mem": gacc[4].reshape(1, LANES), "kn_mem": d_kn_mem, "g_mlp": d_g_mlp,
    }
    return loss, grad_x, grads


MESH = pl.DeviceIdType.MESH
ANY = pl.BlockSpec(memory_space=pl.ANY)


def _place():
    x, y, c = lax.axis_index("x"), lax.axis_index("y"), lax.axis_index("c")
    chips = [(1 - x, y), (x, 1 - y), (1 - x, 1 - y)]
    return x, y, c, chips


def _handshake(peers):
    barrier = pltpu.get_barrier_semaphore()
    for peer in peers:
        pl.semaphore_signal(barrier, inc=1, device_id=peer, device_id_type=MESH)
    pl.semaphore_wait(barrier, len(peers))


def _all_gather_shards_async(name, collective_id, slots):
    n = len(slots)
    bufs = [jax.new_ref(s, memory_space=pltpu.MemorySpace.HBM) for s in slots]

    def body(ici_send, ici_recv, d2d_send, d2d_recv):
        x, y, c, chips = _place()
        sibling = (x, y, 1 - c)
        me = 2 * x + y
        _handshake([(px, py, c) for px, py in chips] + [sibling])

        def half(a, who):
            hr = slots[a].shape[1] // 2
            return pl.ds(pl.multiple_of(who * hr, hr), hr)

        def ici(a, j, slot, to):
            return pltpu.make_async_remote_copy(
                src_ref=bufs[a].at[me, half(a, c)], dst_ref=bufs[a].at[slot, half(a, c)],
                send_sem=ici_send.at[3 * a + j], recv_sem=ici_recv.at[3 * a + j], device_id=to, device_id_type=MESH)

        def d2d(a, j, slot, which):
            part = bufs[a].at[slot, half(a, which)]
            return pltpu.make_async_remote_copy(
                src_ref=part, dst_ref=part, send_sem=d2d_send.at[3 * a + j], recv_sem=d2d_recv.at[3 * a + j],
                device_id=sibling, device_id_type=MESH)

        sends = [ici(a, j, me, (*chip, c)) for a in range(n) for j, chip in enumerate(chips)]
        for cp in sends:
            cp.start()
        passed = []
        for a in range(n):
            for j, (px, py) in enumerate(chips):
                ici(a, j, 2 * px + py, (px, py, c)).wait_recv()
                cp = d2d(a, j, 2 * px + py, c)
                cp.start()
                passed.append(cp)
        for a in range(n):
            for j, (px, py) in enumerate(chips):
                d2d(a, j, 2 * px + py, 1 - c).wait_recv()
        for cp in sends + passed:
            cp.wait_send()

    pl.kernel(
        body, mesh=plsc.ScalarSubcoreMesh(axis_name="seq", num_cores=1), name=name,
        scratch_types=[pltpu.SemaphoreType.DMA((3 * n,))] * 4,
        compiler_params=pltpu.CompilerParams(collective_id=collective_id),
    )()
    return [b[...] for b in bufs]


def _sequencer_call(name, collective_id, n_sems, body):
    pl.kernel(
        body, mesh=plsc.ScalarSubcoreMesh(axis_name="seq", num_cores=1), name=name,
        scratch_types=[pltpu.SemaphoreType.DMA((n_sems,))] * 2,
        compiler_params=pltpu.CompilerParams(collective_id=collective_id),
    )()


def _hbm_ref(value):
    return jax.new_ref(value, memory_space=pltpu.MemorySpace.HBM)


def _pair_exchange(name, collective_id, gs):
    n = len(gs)
    src = [_hbm_ref(g) for g in gs]
    stage = [jax.empty_ref(jax.ShapeDtypeStruct((N_SHARD, g.shape[1] // 2, g.shape[2]), g.dtype),
                           memory_space=pltpu.MemorySpace.HBM) for g in gs]

    def body(send_sem, recv_sem):
        x, y, c, _ = _place()
        sibling = (x, y, 1 - c)
        _handshake([sibling])
        copies = []
        for a in range(n):
            hr = gs[a].shape[1] // 2
            theirs = pl.ds(pl.multiple_of((1 - c) * hr, hr), hr)
            copies.append(pltpu.make_async_remote_copy(
                src_ref=src[a].at[:, theirs, :], dst_ref=stage[a], send_sem=send_sem.at[a], recv_sem=recv_sem.at[a],
                device_id=sibling, device_id_type=MESH))
        for cp in copies:
            cp.start()
        for cp in copies:
            cp.wait()

    _sequencer_call(name, collective_id, n, body)
    return [s[...] for s in stage]


def _chip_exchange(name, collective_id, sums):
    n = len(sums)
    src = [_hbm_ref(s) for s in sums]
    got = [jax.empty_ref(jax.ShapeDtypeStruct((3,) + s.shape[1:], s.dtype), memory_space=pltpu.MemorySpace.HBM)
           for s in sums]

    def body(send_sem, recv_sem):
        x, y, c, chips = _place()
        _handshake([(px, py, c) for px, py in chips])
        copies = []
        for a in range(n):
            for j, (px, py) in enumerate(chips):
                copies.append(pltpu.make_async_remote_copy(
                    src_ref=src[a].at[2 * px + py], dst_ref=got[a].at[j],
                    send_sem=send_sem.at[3 * a + j], recv_sem=recv_sem.at[3 * a + j],
                    device_id=(px, py, c), device_id_type=MESH))
        for cp in copies:
            cp.start()
        for cp in copies:
            cp.wait()

    _sequencer_call(name, collective_id, 3 * n, body)
    return [g[...] for g in got]


def _pair_gather(name, collective_id, fulls):
    n = len(fulls)
    full = [_hbm_ref(f) for f in fulls]

    def body(send_sem, recv_sem):
        x, y, c, _ = _place()
        sibling = (x, y, 1 - c)
        _handshake([sibling])
        copies = []
        for a in range(n):
            hr = fulls[a].shape[0] // 2
            mine = full[a].at[pl.ds(pl.multiple_of(c * hr, hr), hr)]
            copies.append(pltpu.make_async_remote_copy(
                src_ref=mine, dst_ref=mine, send_sem=send_sem.at[a], recv_sem=recv_sem.at[a],
                device_id=sibling, device_id_type=MESH))
        for cp in copies:
            cp.start()
        for cp in copies:
            cp.wait()

    _sequencer_call(name, collective_id, n, body)
    return [f[...] for f in full]


ELEMENTWISE_BLOCK_ELEMS = 256 * 1024


def _row_block(rows, cols):
    rb = 8
    while rb * 2 * cols <= ELEMENTWISE_BLOCK_ELEMS and rb * 2 <= rows:
        rb *= 2
    return rb


def _pair_sum(name, place, g, stage):
    _, R, C = g.shape
    hr = R // 2
    rb = _row_block(hr, C)
    nb = hr // rb

    def body(place_ref, g_ref, st_ref, sum_bf, own_f32):
        s = pl.program_id(1)
        tot = g_ref[...] + st_ref[...]
        sum_bf[...] = tot.astype(BF16)

        @pl.when(s == place_ref[0])
        def _():
            own_f32[...] = tot

    return pl.pallas_call(
        body, name=name,
        grid_spec=pltpu.PrefetchScalarGridSpec(
            num_scalar_prefetch=1, grid=(nb, N_SHARD),
            in_specs=[pl.BlockSpec((None, rb, C), lambda i, s, pr: (s, pr[1] * nb + i, 0)),
                      pl.BlockSpec((None, rb, C), lambda i, s, pr: (s, i, 0))],
            out_specs=[pl.BlockSpec((None, rb, C), lambda i, s, pr: (s, i, 0)),
                       pl.BlockSpec((rb, C), lambda i, s, pr: (i, 0))]),
        out_shape=[jax.ShapeDtypeStruct((N_SHARD, hr, C), BF16), jax.ShapeDtypeStruct((hr, C), F32)],
        compiler_params=_cparams("arbitrary", "arbitrary"),
    )(place, g, stage)


def _final_sum(name, place, own, got):
    hr, C = own.shape
    rb = _row_block(hr, C)
    nb = hr // rb

    def body(place_ref, own_ref, got_ref, o_ref):
        o_ref[...] = ((own_ref[...] + got_ref[0].astype(F32)) + got_ref[1].astype(F32)) + got_ref[2].astype(F32)

    return pl.pallas_call(
        body, name=name,
        grid_spec=pltpu.PrefetchScalarGridSpec(
            num_scalar_prefetch=1, grid=(nb,),
            in_specs=[pl.BlockSpec((rb, C), lambda i, pr: (i, 0)), pl.BlockSpec((3, rb, C), lambda i, pr: (0, i, 0))],
            out_specs=pl.BlockSpec((rb, C), lambda i, pr: (pr[1] * nb + i, 0))),
        out_shape=jax.ShapeDtypeStruct((2 * hr, C), F32),
        compiler_params=_cparams("arbitrary"),
    )(place, own, got)


def _adamw_math(w, g, m, v):
    m = ADAM_B1 * m + (1.0 - ADAM_B1) * g
    v = ADAM_B2 * v + (1.0 - ADAM_B2) * (g * g)
    m_hat = m / (1.0 - ADAM_B1 ** ADAM_STEP)
    v_hat = v / (1.0 - ADAM_B2 ** ADAM_STEP)
    delta = -ADAM_LR * (m_hat / (jnp.sqrt(v_hat) + ADAM_EPS) + ADAM_WD * w)
    return delta, m, v


def _adamw(name, w, g, m, v):
    R, Cw = w.shape
    Cg = g.shape[1]
    rb = _row_block(R, Cg)

    def body(w_ref, g_ref, m_ref, v_ref, g_o, d_o, m_o, v_o):
        gv = g_ref[...]
        delta, mn, vn = _adamw_math(w_ref[...], gv, m_ref[...], v_ref[...])
        g_o[...] = gv
        d_o[...] = delta
        m_o[...] = mn
        v_o[...] = vn

    blk = pl.BlockSpec((rb, Cg), lambda i: (i, 0))
    return pl.pallas_call(
        body, name=name, grid=(R // rb,),
        in_specs=[blk] * 4, out_specs=[blk] * 4,
        out_shape=[jax.ShapeDtypeStruct((R, Cw), F32)] * 4,
        compiler_params=_cparams("parallel"),
    )(w, g, m, v)


N_DEV = 8
SMALL_ROWS = 64


def _small_allreduce_adamw(g, w, m, v):
    def body(g_ref, w_ref, m_ref, v_ref, all_ref, gs_o, d_o, m_o, v_o, send_sems, recv_sems, local_sem):
        x, y, c, chips = _place()
        me, sibling = (x, y, c), (x, y, 1 - c)

        def rows(px, py, pc):
            return all_ref.at[pl.ds(pl.multiple_of((4 * px + 2 * py + pc) * SMALL_ROWS, SMALL_ROWS), SMALL_ROWS), :]

        def copy(k, block, to, src=None):
            return pltpu.make_async_remote_copy(
                src_ref=rows(*block) if src is None else src, dst_ref=rows(*block),
                send_sem=send_sems.at[k], recv_sem=recv_sems.at[k], device_id=to, device_id_type=MESH)

        mine = pltpu.make_async_copy(g_ref, rows(*me), local_sem)
        mine.start()
        first = [copy(0, me, sibling, src=g_ref)]
        first += [copy(1 + j, me, (*chip, c), src=g_ref) for j, chip in enumerate(chips)]
        for cp in first:
            cp.start()
        passed = [copy(4 + j, (*chip, c), sibling) for j, chip in enumerate(chips)]
        for j, chip in enumerate(chips):
            copy(1 + j, (*chip, c), me).wait_recv()
            passed[j].start()
        copy(0, sibling, me).wait_recv()
        for j, chip in enumerate(chips):
            copy(4 + j, (*chip, 1 - c), me).wait_recv()
        for cp in first + passed:
            cp.wait_send()
        mine.wait()

        tot = all_ref[0:SMALL_ROWS, :]
        for d in range(1, N_DEV):
            tot = tot + all_ref[d * SMALL_ROWS:(d + 1) * SMALL_ROWS, :]
        delta, mn, vn = _adamw_math(w_ref[...], tot, m_ref[...], v_ref[...])
        gs_o[...] = tot
        d_o[...] = delta
        m_o[...] = mn
        v_o[...] = vn

    vm = pl.BlockSpec(memory_space=pltpu.VMEM)
    shp = jax.ShapeDtypeStruct((SMALL_ROWS, LANES), F32)
    res = pl.pallas_call(
        body, name="small_allreduce_adamw", in_specs=[vm] * 4, out_specs=[vm] * 5,
        out_shape=[jax.ShapeDtypeStruct((N_DEV * SMALL_ROWS, LANES), F32), shp, shp, shp, shp],
        scratch_shapes=[pltpu.SemaphoreType.DMA((7,)), pltpu.SemaphoreType.DMA((7,)), pltpu.SemaphoreType.DMA],
    )(g, w, m, v)
    return res[1:]


SMALL_NAMES = ("g_mix", "b_gate", "b_forget", "qn_swa", "kn_swa", "sink_swa", "rel_bias", "qn_fox", "kn_fox",
               "g_mem", "qn_mem", "kn_mem", "g_mlp")
BIG_NAMES = ("w_in", "w_mem_kv", "w_o_swa", "w_o_fox", "w_o_mem", "w_out", "w_mlp_up", "w_mlp_down")
WEIGHT_NAMES = ("g_mix", "w_in", "b_gate", "b_forget", "qn_swa", "kn_swa", "sink_swa", "rel_bias", "qn_fox", "kn_fox",
                "g_mem", "w_mem_kv", "qn_mem", "kn_mem", "w_o_swa", "w_o_fox", "w_o_mem", "w_out", "g_mlp",
                "w_mlp_up", "w_mlp_down")


def _pack_small(parts, extra=None):
    rows = []
    for n in SMALL_NAMES:
        flat = parts[n].reshape(-1).astype(F32)
        flat = jnp.pad(flat, (0, (-flat.size) % LANES))
        rows.append(flat.reshape(-1, LANES))
    if extra is not None:
        rows.append(jnp.pad(extra.reshape(1, 1), ((0, 0), (0, LANES - 1))))
    packed = jnp.concatenate(rows, axis=0)
    return jnp.pad(packed, ((0, SMALL_ROWS - packed.shape[0]), (0, 0)))


def _unpack_small(packed, shapes):
    out, r = {}, 0
    for n in SMALL_NAMES:
        size = math.prod(shapes[n])
        nr = -(-size // LANES)
        out[n] = packed[r:r + nr].reshape(-1)[:size].reshape(shapes[n])
        r += nr
    return out, packed[r, 0]


W_IN_SEGMENTS = ((C_QA, 0, 512), (C_QF, 768, 512), (C_KF, 1280, 512), (C_VF, 1792, 512), (C_QM, 2312, 512),
                 (C_KA, 512, 128), (C_VA, 640, 128), (C_FL, 2304, FOX_HEADS), (C_GL, 2824, GATE_W))
RELAYOUT_ROWS = 256


def _permute_pieces(src_of_dst):
    blocks = []
    for b in range(len(src_of_dst) // LANES):
        runs, lane = [], 0
        while lane < LANES:
            src = src_of_dst[b * LANES + lane]
            if src is None:
                lane += 1
                continue
            plane, col = src
            end = lane + 1
            while (end < LANES and src_of_dst[b * LANES + end] == (plane, col + end - lane)
                   and (col + end - lane) // LANES == col // LANES):
                end += 1
            runs.append((plane, col // LANES, (lane - col) % LANES, lane, end))
            lane = end
        blocks.append(runs)
    return blocks


def _permuted_block(runs, load, rows):
    lane = _lane((rows, LANES))
    acc = jnp.zeros((rows, LANES), F32)
    for plane, blk, shift, lo, hi in runs:
        x = load(plane, blk).astype(F32)
        if shift:
            x = pltpu.roll(x, shift, 1)
        acc = x if (lo, hi) == (0, LANES) else jnp.where((lane >= lo) & (lane < hi), x, acc)
    return acc


def _w_in_to_segments(g_in):
    src_of_dst = [None] * PROJ_W
    for mine, theirs, width in W_IN_SEGMENTS:
        for k in range(width):
            src_of_dst[mine + k] = ((theirs + k) // IN_SHARD, (theirs + k) % IN_SHARD)
    blocks = _permute_pieces(src_of_dst)
    rb = RELAYOUT_ROWS

    def body(src_ref, out_ref):
        for b, runs in enumerate(blocks):
            blk = _permuted_block(runs, lambda p, c: src_ref[p, :, c * LANES:(c + 1) * LANES], rb)
            out_ref[:, b * LANES:(b + 1) * LANES] = blk.astype(out_ref.dtype)

    return pl.pallas_call(
        body, name="w_in_to_segments", grid=(D_MODEL // rb,),
        in_specs=[pl.BlockSpec((N_SHARD, rb, IN_SHARD_PAD), lambda i: (0, i, 0))],
        out_specs=pl.BlockSpec((rb, PROJ_W), lambda i: (i, 0)),
        out_shape=jax.ShapeDtypeStruct((D_MODEL, PROJ_W), g_in.dtype),
        compiler_params=_cparams("parallel", vmem=VMEM_MID),
    )(g_in)


def _w_in_from_segments(lo, gl):
    mine_of_theirs = {}
    for mine, theirs, width in W_IN_SEGMENTS:
        for k in range(width):
            mine_of_theirs[theirs + k] = mine + k
    src_of_dst = [None] * (N_SHARD * IN_SHARD_PAD)
    for s in range(N_SHARD):
        for l in range(IN_SHARD):
            j = mine_of_theirs[s * IN_SHARD + l]
            src_of_dst[s * IN_SHARD_PAD + l] = (j // LO_W, j % LO_W)
    blocks = _permute_pieces(src_of_dst)
    per_slot = IN_SHARD_PAD // LANES
    rb = RELAYOUT_ROWS

    def body(lo_ref, gl_ref, out_ref):
        planes = (lo_ref, gl_ref)
        for b, runs in enumerate(blocks):
            blk = _permuted_block(runs, lambda p, c: planes[p][:, c * LANES:(c + 1) * LANES], rb)
            c0 = (b % per_slot) * LANES
            out_ref[b // per_slot, :, c0:c0 + LANES] = blk

    half = pl.BlockSpec((rb, LO_W), lambda i: (i, 0))
    return pl.pallas_call(
        body, name="w_in_from_segments", grid=(D_MODEL // rb,),
        in_specs=[half, half],
        out_specs=pl.BlockSpec((N_SHARD, rb, IN_SHARD_PAD), lambda i: (0, i, 0)),
        out_shape=jax.ShapeDtypeStruct((N_SHARD, D_MODEL, IN_SHARD_PAD), F32),
        compiler_params=_cparams("parallel", vmem=VMEM_MID),
    )(lo, gl)


def _after(first, then):
    return lax.optimization_barrier((first, then))


class _ReduceGroup:
    def __init__(self, tag, first_collective_id, place):
        self.tag, self.first_id, self.place = tag, first_collective_id, place

    def start(self, local, tie):
        self.names = tuple(local)
        mine, tie = _after([local[n] for n in self.names], tie)
        self.mine = mine
        self.staged = _pair_exchange("pair_exchange_" + self.tag, self.first_id, mine)
        return tie

    def send(self, tie):
        staged, tie = _after(self.staged, tie)
        sums = [_pair_sum("pair_sum_" + n, self.place, g, st) for n, g, st in zip(self.names, self.mine, staged)]
        travel, tie = _after([s[0] for s in sums], tie)
        self.own = [s[1] for s in sums]
        self.got = _chip_exchange("chip_exchange_" + self.tag, self.first_id + 1, travel)
        return tie

    def finish(self, tie):
        got, tie = _after(self.got, tie)
        halves = [_final_sum("final_sum_" + n, self.place, o, r) for n, o, r in zip(self.names, self.own, got)]
        halves, tie = _after(halves, tie)
        summed = _pair_gather("pair_gather_" + self.tag, self.first_id + 2, halves)
        self.summed = dict(zip(self.names, summed))
        return tie


class _GradReducer:
    def __init__(self, place):
        self.early = _ReduceGroup("early", 2, place)
        self.late = _ReduceGroup("late", 5, place)

    @staticmethod
    def _slot_rows(a):
        return a.reshape(N_SHARD, a.shape[0] // N_SHARD, a.shape[1])

    def early_start(self, g, tie):
        return self.early.start({"w_mlp_down": self._slot_rows(g["w_mlp_down"]), "w_mlp_up": g["w_mlp_up"],
                                 "w_out": self._slot_rows(g["w_out"]), "w_mem_kv": self._slot_rows(g["w_mem_kv"]),
                                 "w_o_swa": g["w_o_swa"], "w_o_fox": g["w_o_fox"], "w_o_mem": g["w_o_mem"]}, tie)

    def early_send(self, tie):
        return self.early.send(tie)

    def early_finish(self, tie):
        return self.early.finish(tie)

    def late_start(self, g, tie):
        d_in = _w_in_from_segments(g["wc_lo"], g["wc_gl"])
        return self.late.start({"w_in": d_in}, tie)

    def late_send(self, tie):
        return self.late.send(tie)

    def late_finish(self, tie):
        return self.late.finish(tie)

    @property
    def summed(self):
        return {**self.early.summed, **self.late.summed}


def kernel(x, mem, g_mix, w_in, b_gate, b_forget, qn_swa, kn_swa, sink_swa, rel_bias, qn_fox, kn_fox, g_mem, w_mem_kv, qn_mem, kn_mem, w_o_swa, w_o_fox, w_o_mem, w_out, g_mlp, w_mlp_up, w_mlp_down, loss_target, m_g_mix, m_w_in, m_b_gate, m_b_forget, m_qn_swa, m_kn_swa, m_sink_swa, m_rel_bias, m_qn_fox, m_kn_fox, m_g_mem, m_w_mem_kv, m_qn_mem, m_kn_mem, m_w_o_swa, m_w_o_fox, m_w_o_mem, m_w_out, m_g_mlp, m_w_mlp_up, m_w_mlp_down, v_g_mix, v_w_in, v_b_gate, v_b_forget, v_qn_swa, v_kn_swa, v_sink_swa, v_rel_bias, v_qn_fox, v_kn_fox, v_g_mem, v_w_mem_kv, v_qn_mem, v_kn_mem, v_w_o_swa, v_w_o_fox, v_w_o_mem, v_w_out, v_g_mlp, v_w_mlp_up, v_w_mlp_down):
    given = dict(locals())
    W = {n: given[n] for n in WEIGHT_NAMES}
    M = {n: given["m_" + n] for n in WEIGHT_NAMES}
    V = {n: given["v_" + n] for n in WEIGHT_NAMES}
    pad_in = ((0, 0), (0, IN_SHARD_PAD - IN_SHARD))

    shards = [jnp.pad(w_in[0].astype(BF16), pad_in)] + [W[n][0].astype(BF16) for n in BIG_NAMES[1:]]
    slots = [jnp.broadcast_to(s[None], (N_SHARD,) + s.shape) for s in shards]
    (g_in,) = _all_gather_shards_async("all_gather_w_in", 1, slots[:1])
    small = {n: (W[n] if n == "rel_bias" else W[n].reshape(1, -1)) for n in SMALL_NAMES}
    h = _rmsnorm("rms_mix", x[0], small["g_mix"], min(512, x.shape[1]))
    g_in, late, h, (m_in, v_in) = lax.optimization_barrier((g_in, slots[1:], h, (M["w_in"][0], V["w_in"][0])))
    M["w_in"], V["w_in"] = m_in[None], v_in[None]
    g_kv, g_oa, g_of, g_om, g_out, g_up, g_down = _all_gather_shards_async("all_gather_weights_async", 8, late)

    place = jnp.stack([2 * lax.axis_index("x") + lax.axis_index("y"), lax.axis_index("c")]).astype(jnp.int32)
    reducer = _GradReducer(place)
    loss, grad_x, grads = _local_step(
        x[0], h, mem[0], loss_target[0], small, g_in, g_kv.reshape(D_MODEL, D_MODEL), (g_oa, g_of, g_om),
        g_out.reshape(D_MODEL, D_MODEL), g_up, g_down.reshape(D_FF, D_MODEL), reducer)

    out = {}

    def adamw_of(names, summed):
        for n in names:
            res = _adamw("adamw_" + n, W[n][0], summed[n], M[n][0], V[n][0])
            out[n] = [r.reshape(W[n].shape) for r in res]

    before = ("w_mlp_down", "w_mlp_up")
    behind = tuple(n for n in reducer.early.names if n not in before)
    adamw_of(before, reducer.early.summed)
    tied, grad_x = reducer.late_finish(([out[n] for n in before], grad_x))
    for n, res in zip(before, tied):
        out[n] = res
    adamw_of(reducer.late.names, reducer.late.summed)
    (out["w_in"], rest_summed, small_grads) = lax.optimization_barrier(
        (out["w_in"], {n: reducer.early.summed[n] for n in behind}, _pack_small(grads, loss)))
    adamw_of(behind, rest_summed)
    shapes = {n: W[n].shape for n in SMALL_NAMES}
    packed = _small_allreduce_adamw(small_grads, _pack_small(W), _pack_small(M), _pack_small(V))
    unpacked = [_unpack_small(p, shapes) for p in packed]
    for n in SMALL_NAMES:
        out[n] = [u[0][n] for u in unpacked]
    loss_total = unpacked[0][1]

    return (loss_total, grad_x.reshape(x.shape),
            *[out[n][0] for n in WEIGHT_NAMES], *[out[n][1] for n in WEIGHT_NAMES],
            *[out[n][2] for n in WEIGHT_NAMES], *[out[n][3] for n in WEIGHT_NAMES])
```

```python
import math

import jax
import jax.numpy as jnp
from jax import lax
from jax.experimental import pallas as pl
from jax.experimental.pallas import tpu as pltpu
from jax.experimental.pallas import tpu_sc as plsc

F32 = jnp.float32
BF16 = jnp.bfloat16

D_MODEL = 1024
N_MEM = 256
SWA_HEADS = 8
SWA_KV_HEADS = 2
SWA_HEAD_DIM = 64
WINDOW = 128
FOX_HEADS = 8
FOX_HEAD_DIM = 64
MEM_HEADS = 4
MEM_HEAD_DIM = 128
D_FF = 4 * D_MODEL
REL_BUCKETS = 32
REL_MAX_DIST = 128
EPS = 1e-6
NEG = -1e30
GATE_W = 3 * D_MODEL
IN_WIDTH = 5896
N_SHARD = 4
IN_SHARD = IN_WIDTH // N_SHARD
IN_SHARD_PAD = 1536

ADAM_LR = 0.001
ADAM_B1 = 0.9
ADAM_B2 = 0.999
ADAM_EPS = 1e-08
ADAM_WD = 0.01
ADAM_STEP = 10

LANES = 128
V7X_VMEM_BYTES = 64 * 1024 * 1024
VMEM_SMALL = VMEM_MID = VMEM_BIG = V7X_VMEM_BYTES * 3 // 4
VMEM_MAX = V7X_VMEM_BYTES * 7 // 8

C_QA, C_QF, C_KF, C_VF, C_QM, C_KA, C_VA, C_FL, C_GL = 0, 512, 1024, 1536, 2048, 2560, 2688, 2816, 3072
LO_W = 3072
PROJ_W = 6144
PROJ_TN = 2048

NN = (((1,), (0,)), ((), ()))
NT = (((1,), (1,)), ((), ()))
TN = (((0,), (0,)), ((), ()))


def _dot(a, b, dims=NN):
    return lax.dot_general(a, b, dims, preferred_element_type=F32)


def _cparams(*sem, vmem=VMEM_SMALL):
    return pltpu.CompilerParams(dimension_semantics=sem, vmem_limit_bytes=vmem)


def _split3(a):
    hi = a.astype(BF16)
    r1 = a - hi.astype(F32)
    mid = r1.astype(BF16)
    lo = (r1 - mid.astype(F32)).astype(BF16)
    return hi, mid, lo


def _group_mean(a, g2):
    hi = a.astype(BF16)
    mid = (a - hi.astype(F32)).astype(BF16)
    return _dot(jnp.concatenate([hi, mid], axis=1), g2)


def _dot3_left(g, a):
    hi, mid, lo = _split3(a)
    return _dot(g, hi) + _dot(g, mid) + _dot(g, lo)


def _group_mean_matrix(d):
    r = jnp.arange(LANES)
    g = jnp.where((r[:, None] // d) == (r[None, :] // d), 1.0 / d, 0.0).astype(BF16)
    return jnp.concatenate([g, g], axis=0)


def _lane(shape):
    return lax.broadcasted_iota(jnp.int32, shape, len(shape) - 1)


def _matmul(name, a, b, *, dims, grid, a_spec, b_spec, acc_shape, outs, epilogue, extra=(), vmem=VMEM_BIG):
    nk = grid[2]
    n_extra = len(extra)

    def body(a_ref, b_ref, *rest):
        extra_refs = rest[:n_extra]
        out_refs = rest[n_extra:n_extra + len(outs)]
        i, j, k = pl.program_id(0), pl.program_id(1), pl.program_id(2)
        if nk == 1:
            epilogue(_dot(a_ref[...].astype(BF16), b_ref[...].astype(BF16), dims), extra_refs, out_refs, (i, j))
            return
        acc_ref = rest[-1]

        @pl.when(k == 0)
        def _():
            acc_ref[...] = jnp.zeros_like(acc_ref)

        acc_ref[...] += _dot(a_ref[...].astype(BF16), b_ref[...].astype(BF16), dims)

        @pl.when(k == nk - 1)
        def _():
            epilogue(acc_ref[...], extra_refs, out_refs, (i, j))

    res = pl.pallas_call(
        body,
        name=name,
        grid=grid,
        in_specs=[a_spec, b_spec] + [s for _, s in extra],
        out_specs=[s for _, s in outs],
        out_shape=[s for s, _ in outs],
        scratch_shapes=[pltpu.VMEM(acc_shape, F32)] if nk > 1 else [],
        compiler_params=_cparams("arbitrary", "arbitrary", "arbitrary", vmem=vmem),
    )(a, b, *[x for x, _ in extra])
    return res


def _epi_store(acc, extra_refs, out_refs, ij):
    out_refs[0][...] = acc.astype(out_refs[0].dtype)


def _rms_rows(x, g):
    r = lax.rsqrt(jnp.mean(x * x, axis=-1, keepdims=True) + EPS)
    return x * r, r


def _rmsnorm_bwd_rows(dh, x, g):
    xhat, r = _rms_rows(x, g)
    dxh = dh * g
    dx = r * (dxh - xhat * jnp.mean(dxh * xhat, axis=-1, keepdims=True))
    return dx, jnp.sum(dh * xhat, axis=0, keepdims=True)


def _rmsnorm(name, x, g, tb):
    T, Dm = x.shape

    def body(x_ref, g_ref, o_ref):
        xhat, _ = _rms_rows(x_ref[...], None)
        o_ref[...] = (xhat * g_ref[...]).astype(o_ref.dtype)

    return pl.pallas_call(
        body, name=name, grid=(T // tb,),
        in_specs=[pl.BlockSpec((tb, Dm), lambda i: (i, 0)), pl.BlockSpec((1, Dm), lambda i: (0, 0))],
        out_specs=pl.BlockSpec((tb, Dm), lambda i: (i, 0)),
        out_shape=jax.ShapeDtypeStruct((T, Dm), BF16),
        compiler_params=_cparams("parallel"),
    )(x, g)


def _head_norm(x, gm, gain):
    ms = _group_mean(x * x, gm)
    r = lax.rsqrt(ms + EPS)
    return x * r * gain, x * r


def _head_norm_bwd(dy, x, gm, gain):
    ms = _group_mean(x * x, gm)
    r = lax.rsqrt(ms + EPS)
    xhat = x * r
    dxh = dy * gain
    dx = r * (dxh - xhat * _group_mean(dxh * xhat, gm))
    return dx, jnp.sum(dy * xhat, axis=0, keepdims=True)


def _log_sigmoid(z):
    return jnp.minimum(z, 0.0) - jnp.log(1.0 + jnp.exp(-jnp.abs(z)))


def _prep_fwd(proj, gains, bfor, tril, gm64, gm128, T, tb):
    nb = T // tb

    def body(qa_ref, qf_ref, kf_ref, vf_ref, qm_ref, ka_ref, va_ref, fl_ref, gains_ref, bfor_ref, tril_ref,
             gm64_ref, gm128_ref,
             qa_o, qf_o, kf_o, vf_o, qm_o, kad_o, vad_o, qaug_o, kaug_o, carry):
        i = pl.program_id(0)
        gm64v = gm64_ref[...]
        gm128v = gm128_ref[...]
        lane = _lane((tb, LANES))

        def norm512(src, dst, row, gm, scale=1.0):
            gain = gains_ref[row:row + 1, :]
            for c in range(4):
                sl = slice(c * LANES, (c + 1) * LANES)
                y, _ = _head_norm(src[:, sl], gm, gain)
                dst[:, sl] = (y * scale).astype(dst.dtype)

        norm512(qa_ref, qa_o, 0, gm64v)
        norm512(qf_ref, qf_o, 2, gm64v, FOX_SCALE)
        norm512(kf_ref, kf_o, 3, gm64v)
        norm512(qm_ref, qm_o, 4, gm128v)
        vf_o[...] = vf_ref[...].astype(vf_o.dtype)

        ka_n, _ = _head_norm(ka_ref[...], gm64v, gains_ref[1:2, :])
        ka_r = pltpu.roll(ka_n, 64, 1)
        va = va_ref[...]
        va_r = pltpu.roll(va, 64, 1)
        lo = lane < 64
        kad_o[0] = jnp.where(lo, ka_n, ka_r).astype(kad_o.dtype)
        kad_o[1] = jnp.where(lo, ka_r, ka_n).astype(kad_o.dtype)
        vad_o[0] = jnp.where(lo, va, va_r).astype(vad_o.dtype)
        vad_o[1] = jnp.where(lo, va_r, va).astype(vad_o.dtype)

        @pl.when(i == 0)
        def _():
            carry[...] = jnp.zeros_like(carry)

        logf = jnp.where(lane < FOX_HEADS, _log_sigmoid(fl_ref[...] + bfor_ref[...]), 0.0)
        c = _dot3_left(tril_ref[...], logf) + carry[0:1, :]
        carry[...] = jnp.broadcast_to(c[tb - 1:tb, :], carry.shape)
        for pair in range(FOX_HEADS // 2):
            qaug = jnp.zeros((tb, LANES), F32)
            kaug = jnp.zeros((tb, LANES), F32)
            for sub in range(2):
                col = jnp.sum(jnp.where(lane == 2 * pair + sub, c, 0.0), axis=1, keepdims=True)
                pieces = [p.astype(F32) for p in _split3(col)]
                base = AUG_STRIDE * sub
                for e in range(3):
                    qaug = jnp.where(lane == base + AUG_C + e, pieces[e], qaug)
                    kaug = jnp.where(lane == base + AUG_NEG_C + e, -pieces[e], kaug)
                qaug = jnp.where((lane >= base + AUG_NEG_C) & (lane < base + AUG_NEG_C + 3), 1.0, qaug)
                ones_k = ((lane >= base + AUG_C) & (lane < base + AUG_C + 3)) | (
                    (lane >= base + AUG_STAT) & (lane < base + AUG_STAT + 3))
                kaug = jnp.where(ones_k, 1.0, kaug)
            sl = slice(pair * LANES, (pair + 1) * LANES)
            qaug_o[:, sl] = qaug.astype(BF16)
            kaug_o[:, sl] = kaug.astype(BF16)

    def seg(width, start):
        return pl.BlockSpec((tb, width), lambda i, s=start // width: (i, s))

    const = lambda shape: pl.BlockSpec(shape, lambda i: tuple(0 for _ in shape))
    rows512 = pl.BlockSpec((tb, 512), lambda i: (i, 0))
    outs = pl.pallas_call(
        body, name="prep_fwd", grid=(nb,),
        in_specs=[seg(512, C_QA), seg(512, C_QF), seg(512, C_KF), seg(512, C_VF), seg(512, C_QM),
                  seg(128, C_KA), seg(128, C_VA), seg(128, C_FL),
                  const((8, LANES)), const((1, LANES)), const((tb, tb)), const((2 * LANES, LANES)), const((2 * LANES, LANES))],
        out_specs=[rows512, rows512, rows512, rows512, rows512,
                   pl.BlockSpec((2, tb, LANES), lambda i: (0, i, 0)), pl.BlockSpec((2, tb, LANES), lambda i: (0, i, 0)),
                   rows512, rows512],
        out_shape=[jax.ShapeDtypeStruct((T, 512), BF16)] * 5
        + [jax.ShapeDtypeStruct((2, T, LANES), BF16)] * 2
        + [jax.ShapeDtypeStruct((T, 512), BF16)] * 2,
        scratch_shapes=[pltpu.VMEM((8, LANES), F32)],
        compiler_params=_cparams("arbitrary", vmem=VMEM_MID),
    )(proj, proj, proj, proj, proj, proj, proj, proj, gains, bfor, tril, gm64, gm128)
    return outs


def _prep_bwd(proj, dqa, dkad, dvad, dqf, dkf, dvf, dqm, dqf_aug, dkf_aug, gains, bfor, triu, gm64, gm128, T, tb):
    nb = T // tb

    def body(qa_ref, qf_ref, kf_ref, qm_ref, ka_ref, fl_ref,
             dqa_ref, dkad_ref, dvad_ref, dqf_ref, dkf_ref, dvf_ref, dqm_ref, dqfa_ref, dkfa_ref,
             gains_ref, bfor_ref, triu_ref, gm64_ref, gm128_ref,
             dlo_o, gacc_o, carry):
        i = pl.program_id(0)
        gm64v = gm64_ref[...]
        gm128v = gm128_ref[...]
        lane = _lane((tb, LANES))

        @pl.when(i == 0)
        def _():
            carry[...] = jnp.zeros_like(carry)
            gacc_o[...] = jnp.zeros_like(gacc_o)

        def norm512_bwd(dsrc, xsrc, col0, row, gm):
            gain = gains_ref[row:row + 1, :]
            gsum = jnp.zeros((1, LANES), F32)
            for c in range(4):
                sl = slice(c * LANES, (c + 1) * LANES)
                dx, dg = _head_norm_bwd(dsrc[:, sl], xsrc[:, sl], gm, gain)
                dlo_o[:, col0 + c * LANES:col0 + (c + 1) * LANES] = dx.astype(dlo_o.dtype)
                gsum = gsum + dg
            gacc_o[row:row + 1, :] += gsum

        norm512_bwd(dqa_ref, qa_ref, C_QA, 0, gm64v)
        norm512_bwd(dqf_ref, qf_ref, C_QF, 2, gm64v)
        norm512_bwd(dkf_ref, kf_ref, C_KF, 3, gm64v)
        norm512_bwd(dqm_ref, qm_ref, C_QM, 4, gm128v)
        dlo_o[:, C_VF:C_VF + 512] = dvf_ref[...].astype(dlo_o.dtype)

        lo = lane < 64

        def fold(ref):
            f0 = ref[0] + pltpu.roll(ref[0], 64, 1)
            f1 = ref[1] + pltpu.roll(ref[1], 64, 1)
            return jnp.where(lo, f0, f1)

        dka, dg = _head_norm_bwd(fold(dkad_ref), ka_ref[...], gm64v, gains_ref[1:2, :])
        gacc_o[1:2, :] += dg
        dlo_o[:, C_KA:C_KA + LANES] = dka.astype(dlo_o.dtype)
        dlo_o[:, C_VA:C_VA + LANES] = fold(dvad_ref).astype(dlo_o.dtype)

        dc = jnp.zeros((tb, LANES), F32)
        for pair in range(FOX_HEADS // 2):
            sl = slice(pair * LANES, (pair + 1) * LANES)
            rows_sum, cols_sum = dqfa_ref[:, sl], dkfa_ref[:, sl]
            for sub in range(2):
                diff = (jnp.where(lane == AUG_STRIDE * sub + AUG_C, rows_sum, 0.0)
                        - jnp.where(lane == AUG_STRIDE * sub + AUG_NEG_C, cols_sum, 0.0))
                dc = jnp.where(lane == 2 * pair + sub, jnp.sum(diff, axis=1, keepdims=True), dc)
        dlogf = _dot3_left(triu_ref[...], dc) + carry[0:1, :]
        carry[...] = jnp.broadcast_to(dlogf[0:1, :], carry.shape)
        z = fl_ref[...] + bfor_ref[...]
        dfl = jnp.where(lane < FOX_HEADS, dlogf / (1.0 + jnp.exp(z)), 0.0)
        gacc_o[5:6, :] += jnp.sum(dfl, axis=0, keepdims=True)
        dlo_o[:, C_FL:C_FL + LANES] = dfl.astype(dlo_o.dtype)
        dlo_o[:, C_FL + LANES:C_FL + 2 * LANES] = jnp.zeros((tb, LANES), dlo_o.dtype)

    rev = lambda i: nb - 1 - i

    def seg(width, start):
        return pl.BlockSpec((tb, width), lambda i, s=start // width: (rev(i), s))

    const = lambda shape: pl.BlockSpec(shape, lambda i: tuple(0 for _ in shape))
    rows512 = pl.BlockSpec((tb, 512), lambda i: (rev(i), 0))
    dup = pl.BlockSpec((2, tb, LANES), lambda i: (0, rev(i), 0))
    return pl.pallas_call(
        body, name="prep_bwd", grid=(nb,),
        in_specs=[seg(512, C_QA), seg(512, C_QF), seg(512, C_KF), seg(512, C_QM), seg(128, C_KA), seg(128, C_FL),
                  rows512, dup, dup, rows512, rows512, rows512, rows512, rows512, rows512,
                  const((8, LANES)), const((1, LANES)), const((tb, tb)), const((2 * LANES, LANES)), const((2 * LANES, LANES))],
        out_specs=[pl.BlockSpec((tb, LO_W), lambda i: (rev(i), 0)), const((8, LANES))],
        out_shape=[jax.ShapeDtypeStruct((T, LO_W), BF16), jax.ShapeDtypeStruct((8, LANES), F32)],
        scratch_shapes=[pltpu.VMEM((8, LANES), F32)],
        compiler_params=_cparams("arbitrary", vmem=VMEM_MID),
    )(proj, proj, proj, proj, proj, proj, dqa, dkad, dvad, dqf, dkf, dvf, dqm, dqf_aug, dkf_aug,
      gains, bfor, triu, gm64, gm128)


FOX_SCALE = FOX_HEAD_DIM ** -0.5
AUG_STRIDE = 16
AUG_C = 0
AUG_NEG_C = 3
AUG_STAT = 6
FOX_TQ, FOX_TK = 1024, 1024
FOX_BWD_TQ, FOX_BWD_TK = 1024, 1024
FOX_DIAGONAL_PARTS = 4


def _fox_head_mask(sub, rows):
    lane = _lane((rows, 2 * LANES))
    main = (lane >= 64 * sub) & (lane < 64 * sub + 64)
    aug = (lane >= LANES + AUG_STRIDE * sub) & (lane < LANES + AUG_STRIDE * (sub + 1))
    return main | aug


def _fox_pieces(diagonal, tq, tk):
    if diagonal and tq == tk and tq >= FOX_DIAGONAL_PARTS * LANES:
        step = tq // FOX_DIAGONAL_PARTS
        return [(n * step, (n + 1) * step, (n + 1) * step) for n in range(FOX_DIAGONAL_PARTS)]
    return [(0, tq, tk)]


def _fox_fwd(q, qaug, k, kaug, v, T, tq, tk):
    nq, nk = T // tq, T // tk
    rep = tk // LANES
    last_of = lambda i: (i * tq + tq - 1) // tk

    def body(q_ref, qa_ref, k_ref, ka_ref, v_ref, o_ref, qab_ref, m_s, acc_s):
        p_, i, j = pl.program_id(0), pl.program_id(1), pl.program_id(2)
        last = last_of(i)

        @pl.when(j == 0)
        def _():
            m_s[...] = jnp.full(m_s.shape, NEG, F32)
            acc_s[...] = jnp.zeros_like(acc_s)

        def step(diagonal):
            k2 = jnp.concatenate([k_ref[...], ka_ref[...]], axis=1)
            v2 = jnp.concatenate([v_ref[...], ka_ref[...]], axis=1)
            pieces = _fox_pieces(diagonal, tq, tk)
            work = []
            for r0, r1, nc in pieces:
                rows = slice(r0, r1)
                q2 = jnp.concatenate([q_ref[rows, :], qa_ref[rows, :]], axis=1)
                for sub in range(2):
                    qh = jnp.where(_fox_head_mask(sub, r1 - r0), q2, jnp.zeros_like(q2))
                    work.append((rows, r0, r1 - r0, nc, sub, _dot(qh, k2[:nc], NT)))
            for rows, r0, nr, nc, sub, s in work:
                if diagonal:
                    causal = (lax.broadcasted_iota(jnp.int32, (nr, nc), 1) + j * tk
                              <= lax.broadcasted_iota(jnp.int32, (nr, nc), 0) + (r0 + i * tq))
                    s = jnp.where(causal, s, NEG)
                m_prev = m_s[sub, rows, :]
                m_next = jnp.maximum(m_prev, jnp.max(s, axis=1, keepdims=True))
                p = jnp.exp(s - jnp.tile(m_next, (1, nc // LANES)))
                alpha = jnp.exp(m_prev - m_next)
                m_s[sub, rows, :] = m_next
                acc_s[sub, rows, :] = acc_s[sub, rows, :] * jnp.tile(alpha, (1, 2)) + _dot(p.astype(BF16), v2[:nc])

        @pl.when(j == last)
        def _():
            step(True)

        @pl.when(j < last)
        def _():
            step(False)

        @pl.when(j == nk - 1)
        def _():
            lane = _lane((tq, LANES))
            outs = []
            qab = qa_ref[...].astype(F32)
            for sub in range(2):
                acc = acc_s[sub]
                base = AUG_STRIDE * sub
                l = jnp.sum(jnp.where(lane == base + AUG_C, acc[:, LANES:], 0.0), axis=1, keepdims=True)
                outs.append(acc[:, :LANES] / l)
                lse = jnp.max(m_s[sub], axis=1, keepdims=True) + jnp.log(l)
                pieces = _split3(-lse)
                for e in range(3):
                    qab = jnp.where(lane == base + AUG_STAT + e, pieces[e].astype(F32), qab)
            o_ref[...] = jnp.where(lane < 64, outs[0], outs[1]).astype(o_ref.dtype)
            qab_ref[...] = qab.astype(BF16)

    qspec = pl.BlockSpec((tq, LANES), lambda p, i, j: (i, p))
    kspec = pl.BlockSpec((tk, LANES), lambda p, i, j: (jnp.minimum(j, last_of(i)), p))
    return pl.pallas_call(
        body, name="fox_fwd", grid=(4, nq, nk),
        in_specs=[qspec, qspec, kspec, kspec, kspec],
        out_specs=[qspec, qspec],
        out_shape=[jax.ShapeDtypeStruct((T, 512), BF16), jax.ShapeDtypeStruct((T, 512), BF16)],
        scratch_shapes=[pltpu.VMEM((2, tq, LANES), F32), pltpu.VMEM((2, tq, 2 * LANES), F32)],
        compiler_params=_cparams("parallel", "parallel", "arbitrary", vmem=VMEM_BIG),
    )(q, qaug, k, kaug, v)


def _fox_bwd(q, qaug, k, kaug, v, do, doaug, T, tq, tk):
    nq, nk = T // tq, T // tk
    first_of = lambda j: (j * tk) // tq

    def body(q_ref, qa_ref, k_ref, ka_ref, v_ref, do_ref, doa_ref,
             dq_ref, dqa_ref, dk_ref, dka_ref, dv_ref, dk_s, dv_s):
        p_, j, i = pl.program_id(0), pl.program_id(1), pl.program_id(2)
        masked = i * tq < (j + 1) * tk - 1

        @pl.when((j == 0) & (i == 0))
        def _():
            dq_ref[...] = jnp.zeros_like(dq_ref)
            dqa_ref[...] = jnp.zeros_like(dqa_ref)

        @pl.when(i == 0)
        def _():
            dk_s[...] = jnp.zeros_like(dk_s)
            dv_s[...] = jnp.zeros_like(dv_s)

        def step(diagonal):
            k2 = jnp.concatenate([k_ref[...], ka_ref[...]], axis=1)
            v2 = jnp.concatenate([v_ref[...], ka_ref[...]], axis=1)
            work = []
            for r0, r1, nc in _fox_pieces(diagonal, tq, tk):
                rows = slice(r0, r1)
                q2 = jnp.concatenate([q_ref[rows, :], qa_ref[rows, :]], axis=1)
                do2 = jnp.concatenate([do_ref[rows, :], doa_ref[rows, :]], axis=1)
                for sub in range(2):
                    hm = _fox_head_mask(sub, r1 - r0)
                    qh = jnp.where(hm, q2, jnp.zeros_like(q2))
                    doh = jnp.where(hm, do2, jnp.zeros_like(do2))
                    s = _dot(qh, k2[:nc], NT)
                    dp = _dot(doh, v2[:nc], NT)
                    work.append((r0, r1 - r0, nc, sub, qh, doh, s, dp))
            dqs = {}
            for r0, nr, nc, sub, qh, doh, s, dp in work:
                if diagonal:
                    causal = (lax.broadcasted_iota(jnp.int32, (nr, nc), 1) + j * tk
                              <= lax.broadcasted_iota(jnp.int32, (nr, nc), 0) + (r0 + i * tq))
                    s = jnp.where(causal, s, NEG)
                p = jnp.exp(s)
                dsb = (p * dp).astype(BF16)
                dv_s[0:nc, :] += _dot(p.astype(BF16), doh[:, :LANES], TN)
                dk_s[0:nc, :] += _dot(dsb, qh, TN)
                dqs[(r0, sub)] = _dot(dsb, k2[:nc])
            for r0, r1, nc in _fox_pieces(diagonal, tq, tk):
                dq2 = jnp.where(_fox_head_mask(0, r1 - r0), dqs[(r0, 0)], dqs[(r0, 1)])
                qrows = pl.ds(pl.multiple_of(i * tq + r0, r1 - r0), r1 - r0)
                dq_ref[qrows, :] += dq2[:, :LANES] * FOX_SCALE
                dqa_ref[qrows, :] += dq2[:, LANES:]

        @pl.when((i >= first_of(j)) & masked)
        def _():
            step(True)

        @pl.when((i >= first_of(j)) & jnp.logical_not(masked))
        def _():
            step(False)

        @pl.when(i == nq - 1)
        def _():
            dk_ref[...] = dk_s[:, :LANES]
            dka_ref[...] = dk_s[:, LANES:]
            dv_ref[...] = dv_s[...]

    qspec = pl.BlockSpec((tq, LANES), lambda p, j, i: (jnp.maximum(i, first_of(j)), p))
    kspec = pl.BlockSpec((tk, LANES), lambda p, j, i: (j, p))
    resident = pl.BlockSpec((T, LANES), lambda p, j, i: (0, p))
    return pl.pallas_call(
        body, name="fox_bwd", grid=(4, nk, nq),
        in_specs=[qspec, qspec, kspec, kspec, kspec, qspec, qspec],
        out_specs=[resident, resident, kspec, kspec, kspec],
        out_shape=[jax.ShapeDtypeStruct((T, 512), F32)] * 5,
        scratch_shapes=[pltpu.VMEM((tk, 2 * LANES), F32), pltpu.VMEM((tk, LANES), F32)],
        compiler_params=_cparams("arbitrary", "arbitrary", "arbitrary", vmem=VMEM_BIG),
    )(q, qaug, k, kaug, v, do, doaug)


SWA_SUB = 16
SWA_TB = SWA_SUB * WINDOW


def _t5_bucket_matrix():
    t = jnp.arange(WINDOW)[:, None] + WINDOW
    s = jnp.arange(2 * WINDOW)[None, :]
    max_exact = REL_BUCKETS // 2
    d = jnp.maximum(t - s, 0)
    df = jnp.maximum(d, 1).astype(F32)
    large = max_exact + (jnp.log(df / max_exact) / math.log(REL_MAX_DIST / max_exact)
                         * (REL_BUCKETS - max_exact)).astype(jnp.int32)
    large = jnp.minimum(large, REL_BUCKETS - 1)
    return jnp.where(d < max_exact, d, large).astype(jnp.int32)


def _swa_bias(rel_bias, bucket):
    def body(rel_ref, bucket_ref, o_ref):
        b = bucket_ref[...]
        for h in range(SWA_HEADS):
            acc = jnp.zeros(b.shape, F32)
            for r in range(REL_BUCKETS):
                acc = jnp.where(b == r, rel_ref[r, h], acc)
            o_ref[h] = acc

    return pl.pallas_call(
        body, name="swa_bias",
        in_specs=[pl.BlockSpec(memory_space=pltpu.SMEM), pl.BlockSpec(memory_space=pltpu.VMEM)],
        out_specs=pl.BlockSpec(memory_space=pltpu.VMEM),
        out_shape=jax.ShapeDtypeStruct((SWA_HEADS, WINDOW, 2 * WINDOW), F32),
    )(rel_bias, bucket)


def _swa_bias_bwd(dbias, bucket):
    def body(db_ref, bucket_ref, o_ref):
        b = bucket_ref[...]
        lane = _lane((1, LANES))
        for r in range(REL_BUCKETS):
            row = jnp.zeros((1, LANES), F32)
            for h in range(SWA_HEADS):
                part = jnp.sum(jnp.where(b == r, db_ref[h], 0.0), axis=0, keepdims=True)
                tot = jnp.sum(part, axis=1, keepdims=True)
                row = jnp.where(lane == h, tot, row)
            o_ref[r:r + 1, :] = row

    return pl.pallas_call(
        body, name="swa_bias_bwd",
        in_specs=[pl.BlockSpec(memory_space=pltpu.VMEM), pl.BlockSpec(memory_space=pltpu.VMEM)],
        out_specs=pl.BlockSpec(memory_space=pltpu.VMEM),
        out_shape=jax.ShapeDtypeStruct((REL_BUCKETS, LANES), F32),
    )(dbias, bucket)


SWA_GROUP = SWA_HEADS // SWA_KV_HEADS


def _swa_valid(r, i):
    t = (lax.broadcasted_iota(jnp.int32, (SWA_GROUP * WINDOW, 2 * WINDOW), 0) & (WINDOW - 1)) + WINDOW
    s = lax.broadcasted_iota(jnp.int32, (SWA_GROUP * WINDOW, 2 * WINDOW), 1)
    dist = t - s
    band = (dist >= 0) & (dist < WINDOW)
    if r == 0:
        band = band & ((s >= WINDOW) | (i > 0))
    return band


def _swa_stack(blk):
    lane = _lane((WINDOW, LANES))
    parts = []
    for g in range(SWA_GROUP):
        b = blk[:, LANES * (g // 2):LANES * (g // 2 + 1)]
        parts.append(jnp.where((lane >= 64) if g % 2 else (lane < 64), b, jnp.zeros_like(b)))
    return jnp.concatenate(parts, axis=0)


def _swa_unstack(st):
    lane = _lane((WINDOW, LANES))
    W = WINDOW
    return jnp.concatenate([jnp.where(lane < 64, st[2 * b * W:(2 * b + 1) * W], st[(2 * b + 1) * W:(2 * b + 2) * W])
                            for b in range(2)], axis=1)


def _swa_sink_column(sink_ref, kvh):
    row = lax.broadcasted_iota(jnp.int32, (SWA_GROUP * WINDOW, 1), 0)
    col = jnp.full((SWA_GROUP * WINDOW, 1), sink_ref[SWA_GROUP * kvh + SWA_GROUP - 1], F32)
    for g in range(SWA_GROUP - 2, -1, -1):
        col = jnp.where(row < (g + 1) * WINDOW, sink_ref[SWA_GROUP * kvh + g], col)
    return col


def _swa_specs(T):
    W = WINDOW
    qspec = pl.BlockSpec((SWA_TB, 2 * LANES), lambda h, i: (i, h))
    own = pl.BlockSpec((None, SWA_TB, LANES), lambda h, i: (h, i, 0))
    prev = pl.BlockSpec((None, W, LANES), lambda h, i: (h, jnp.maximum(SWA_SUB * i - 1, 0), 0))
    stat = pl.BlockSpec((SWA_GROUP, SWA_TB, LANES), lambda h, i: (h, i, 0))
    bias = pl.BlockSpec((None, SWA_GROUP * W, 2 * W), lambda h, i: (h, 0, 0))
    return qspec, own, prev, stat, bias


def _swa_fwd(sinks, q, kad, vad, bias, T):
    nb = T // SWA_TB
    scale = SWA_HEAD_DIM ** -0.5
    W = WINDOW

    def body(sink_ref, q_ref, k_ref, kp_ref, v_ref, vp_ref, bias_ref, o_ref, lse_ref):
        kvh, i = pl.program_id(0), pl.program_id(1)
        sink = _swa_sink_column(sink_ref, kvh)
        for r in range(SWA_SUB):
            rs = slice(r * W, (r + 1) * W)
            ps = slice((r - 1) * W, r * W)
            k_own, v_own = k_ref[rs, :], v_ref[rs, :]
            k_prev = kp_ref[...] if r == 0 else k_ref[ps, :]
            v_prev = vp_ref[...] if r == 0 else v_ref[ps, :]
            qs = _swa_stack(q_ref[rs, :])
            s = jnp.concatenate([_dot(qs, k_prev, NT), _dot(qs, k_own, NT)], axis=1) * scale + bias_ref[...]
            s = jnp.where(_swa_valid(r, i), s, NEG)
            m = jnp.maximum(jnp.max(s, axis=1, keepdims=True), sink)
            p = jnp.exp(s - m)
            denom = jnp.sum(p, axis=1, keepdims=True) + jnp.exp(sink - m)
            pn = (p / denom).astype(BF16)
            o_ref[rs, :] = _swa_unstack(_dot(pn[:, :W], v_prev) + _dot(pn[:, W:], v_own)).astype(o_ref.dtype)
            lse = m + jnp.log(denom)
            for g in range(SWA_GROUP):
                lse_ref[g, rs, :] = jnp.broadcast_to(lse[g * W:(g + 1) * W], (W, LANES))

    qspec, own, prev, stat, bspec = _swa_specs(T)
    return pl.pallas_call(
        body, name="swa_fwd", grid=(SWA_KV_HEADS, nb),
        in_specs=[pl.BlockSpec(memory_space=pltpu.SMEM), qspec, own, prev, own, prev, bspec],
        out_specs=[qspec, stat],
        out_shape=[jax.ShapeDtypeStruct((T, 512), BF16), jax.ShapeDtypeStruct((SWA_HEADS, T, LANES), F32)],
        compiler_params=_cparams("parallel", "parallel", vmem=VMEM_MID),
    )(sinks, q, kad, kad, vad, vad, bias.reshape(SWA_KV_HEADS, SWA_GROUP * W, 2 * W))


def _swa_bwd(sinks, q, kad, vad, bias, do, lse, delta, T):
    nb = T // SWA_TB
    scale = SWA_HEAD_DIM ** -0.5
    W = WINDOW

    def body(sink_ref, q_ref, k_ref, kp_ref, v_ref, vp_ref, bias_ref, do_ref, lse_ref, dl_ref,
             dq_ref, dkad_ref, dvad_ref, dbias_ref, dsk_ref):
        kvh, i = pl.program_id(0), pl.program_id(1)
        sink = _swa_sink_column(sink_ref, kvh)

        @pl.when((kvh == 0) & (i == 0))
        def _():
            dkad_ref[...] = jnp.zeros_like(dkad_ref)
            dvad_ref[...] = jnp.zeros_like(dvad_ref)

        @pl.when(i == 0)
        def _():
            dbias_ref[...] = jnp.zeros_like(dbias_ref)
            dsk_ref[...] = jnp.zeros_like(dsk_ref)

        for r in range(SWA_SUB):
            rs = slice(r * W, (r + 1) * W)
            ps = slice((r - 1) * W, r * W)
            k_own, v_own = k_ref[rs, :], v_ref[rs, :]
            k_prev = kp_ref[...] if r == 0 else k_ref[ps, :]
            v_prev = vp_ref[...] if r == 0 else v_ref[ps, :]
            qs = _swa_stack(q_ref[rs, :])
            dos = _swa_stack(do_ref[rs, :])
            lse_b = jnp.concatenate([lse_ref[g, rs, :] for g in range(SWA_GROUP)], axis=0)
            dl_b = jnp.concatenate([dl_ref[g, rs, :] for g in range(SWA_GROUP)], axis=0)
            s = jnp.concatenate([_dot(qs, k_prev, NT), _dot(qs, k_own, NT)], axis=1) * scale + bias_ref[...]
            s = jnp.where(_swa_valid(r, i), s, NEG)
            p = jnp.exp(s - jnp.tile(lse_b, (1, 2)))
            dp = jnp.concatenate([_dot(dos, v_prev, NT), _dot(dos, v_own, NT)], axis=1)
            ds = p * (dp - jnp.tile(dl_b, (1, 2)))
            sink_term = jnp.exp(sink - lse_b) * dl_b
            for g in range(SWA_GROUP):
                dbias_ref[g] += ds[g * W:(g + 1) * W]
                dsk_ref[g:g + 1, :] += jnp.sum(sink_term[g * W:(g + 1) * W], axis=0, keepdims=True)
            dsb = ds.astype(BF16)
            pb = p.astype(BF16)
            dq_ref[rs, :] = _swa_unstack((_dot(dsb[:, :W], k_prev) + _dot(dsb[:, W:], k_own)) * scale)
            own_row = pl.multiple_of(i * SWA_TB + r * W, W)
            dkad_ref[kvh, pl.ds(own_row, W), :] += _dot(dsb[:, W:], qs, TN) * scale
            dvad_ref[kvh, pl.ds(own_row, W), :] += _dot(pb[:, W:], dos, TN)
            dk_prev = _dot(dsb[:, :W], qs, TN) * scale
            dv_prev = _dot(pb[:, :W], dos, TN)
            if r == 0:
                @pl.when(i > 0)
                def _():
                    prev_row = pl.multiple_of(i * SWA_TB - W, W)
                    dkad_ref[kvh, pl.ds(prev_row, W), :] += dk_prev
                    dvad_ref[kvh, pl.ds(prev_row, W), :] += dv_prev
            else:
                prev_row = pl.multiple_of(i * SWA_TB + (r - 1) * W, W)
                dkad_ref[kvh, pl.ds(prev_row, W), :] += dk_prev
                dvad_ref[kvh, pl.ds(prev_row, W), :] += dv_prev

    qspec, own, prev, stat, bspec = _swa_specs(T)
    full = pl.BlockSpec((SWA_KV_HEADS, T, LANES), lambda h, i: (0, 0, 0))
    return pl.pallas_call(
        body, name="swa_bwd", grid=(SWA_KV_HEADS, nb),
        in_specs=[pl.BlockSpec(memory_space=pltpu.SMEM), qspec, own, prev, own, prev, bspec, qspec, stat, stat],
        out_specs=[qspec, full, full, pl.BlockSpec((SWA_GROUP, W, 2 * W), lambda h, i: (h, 0, 0)),
                   pl.BlockSpec((None, 8, LANES), lambda h, i: (h, 0, 0))],
        out_shape=[jax.ShapeDtypeStruct((T, 512), F32), jax.ShapeDtypeStruct((SWA_KV_HEADS, T, LANES), F32),
                   jax.ShapeDtypeStruct((SWA_KV_HEADS, T, LANES), F32), jax.ShapeDtypeStruct((SWA_HEADS, W, 2 * W), F32),
                   jax.ShapeDtypeStruct((SWA_KV_HEADS, 8, LANES), F32)],
        compiler_params=_cparams("arbitrary", "arbitrary", vmem=VMEM_MID),
    )(sinks, q, kad, kad, vad, vad, bias.reshape(SWA_KV_HEADS, SWA_GROUP * W, 2 * W), do, lse, delta)


MEM_TQ = 4096


def _mem_fwd(q, mk, mv, T, tq):
    scale = MEM_HEAD_DIM ** -0.5

    def body(q_ref, k_ref, v_ref, o_ref, lse_ref):
        s = _dot(q_ref[...], k_ref[...], NT) * scale
        m = jnp.max(s, axis=1, keepdims=True)
        p = jnp.exp(s - m)
        l = jnp.sum(p, axis=1, keepdims=True)
        o_ref[...] = _dot((p / l).astype(BF16), v_ref[...]).astype(o_ref.dtype)
        lse_ref[...] = jnp.broadcast_to(m + jnp.log(l), (tq, LANES))

    qspec = pl.BlockSpec((tq, LANES), lambda h, i: (i, h))
    kspec = pl.BlockSpec((N_MEM, LANES), lambda h, i: (0, h))
    return pl.pallas_call(
        body, name="mem_fwd", grid=(MEM_HEADS, T // tq),
        in_specs=[qspec, kspec, kspec],
        out_specs=[qspec, pl.BlockSpec((None, tq, LANES), lambda h, i: (h, i, 0))],
        out_shape=[jax.ShapeDtypeStruct((T, 512), BF16), jax.ShapeDtypeStruct((MEM_HEADS, T, LANES), F32)],
        compiler_params=_cparams("parallel", "parallel"),
    )(q, mk, mv)


def _mem_bwd(q, mk, mv, do, lse, delta, T, tq):
    scale = MEM_HEAD_DIM ** -0.5
    rep = N_MEM // LANES

    def body(q_ref, k_ref, v_ref, do_ref, lse_ref, dl_ref, dq_ref, dk_ref, dv_ref):
        i = pl.program_id(1)

        @pl.when(i == 0)
        def _():
            dk_ref[...] = jnp.zeros_like(dk_ref)
            dv_ref[...] = jnp.zeros_like(dv_ref)

        qv, dov = q_ref[...], do_ref[...]
        s = _dot(qv, k_ref[...], NT) * scale
        p = jnp.exp(s - jnp.tile(lse_ref[...], (1, rep)))
        dp = _dot(dov, v_ref[...], NT)
        ds = p * (dp - jnp.tile(dl_ref[...], (1, rep)))
        dsb = ds.astype(BF16)
        dq_ref[...] = _dot(dsb, k_ref[...]) * scale
        dk_ref[...] += _dot(dsb, qv, TN) * scale
        dv_ref[...] += _dot(p.astype(BF16), dov, TN)

    qspec = pl.BlockSpec((tq, LANES), lambda h, i: (i, h))
    kspec = pl.BlockSpec((N_MEM, LANES), lambda h, i: (0, h))
    stat = pl.BlockSpec((None, tq, LANES), lambda h, i: (h, i, 0))
    return pl.pallas_call(
        body, name="mem_bwd", grid=(MEM_HEADS, T // tq),
        in_specs=[qspec, kspec, kspec, qspec, stat, stat],
        out_specs=[qspec, kspec, kspec],
        out_shape=[jax.ShapeDtypeStruct((T, 512), F32), jax.ShapeDtypeStruct((N_MEM, 512), F32),
                   jax.ShapeDtypeStruct((N_MEM, 512), F32)],
        compiler_params=_cparams("arbitrary", "arbitrary"),
    )(q, mk, mv, do, lse, delta)


def _mem_prep_fwd(mem, g_mem, w_kv, kn_gain, gm128):
    def body(mem_ref, g_ref, w_ref, kn_ref, gm_ref, memn_o, kv_o, mk_o, mv_o):
        xhat, _ = _rms_rows(mem_ref[...], None)
        memn = (xhat * g_ref[...]).astype(BF16)
        memn_o[...] = memn
        kv = _dot(memn, w_ref[...])
        kv_o[...] = kv
        gm = gm_ref[...]
        for c in range(4):
            sl = slice(c * LANES, (c + 1) * LANES)
            y, _ = _head_norm(kv[:, sl], gm, kn_ref[...])
            mk_o[:, sl] = y.astype(BF16)
        mv_o[...] = kv[:, 512:].astype(BF16)

    vm = pl.BlockSpec(memory_space=pltpu.VMEM)
    return pl.pallas_call(
        body, name="mem_prep_fwd", in_specs=[vm] * 5, out_specs=[vm] * 4,
        out_shape=[jax.ShapeDtypeStruct((N_MEM, D_MODEL), BF16), jax.ShapeDtypeStruct((N_MEM, D_MODEL), F32),
                   jax.ShapeDtypeStruct((N_MEM, 512), BF16), jax.ShapeDtypeStruct((N_MEM, 512), BF16)],
        compiler_params=pltpu.CompilerParams(vmem_limit_bytes=VMEM_MID),
    )(mem, g_mem, w_kv, kn_gain, gm128)


def _mem_prep_bwd(mem, g_mem, memn, kv, w_kv, kn_gain, gm128, dmk, dmv):
    def body(mem_ref, g_ref, memn_ref, kv_ref, w_ref, kn_ref, gm_ref, dmk_ref, dmv_ref, dw_o, dg_o, dkn_o, dkv_s):
        gm = gm_ref[...]
        dkn = jnp.zeros((1, LANES), F32)
        for c in range(4):
            sl = slice(c * LANES, (c + 1) * LANES)
            dx, dg = _head_norm_bwd(dmk_ref[:, sl], kv_ref[:, sl], gm, kn_ref[...])
            dkv_s[:, sl] = dx.astype(BF16)
            dkn = dkn + dg
        dkn_o[...] = dkn
        dkv_s[:, 512:] = dmv_ref[...].astype(BF16)
        dkv = dkv_s[...]
        dw_o[...] = _dot(memn_ref[...], dkv, TN)
        dmemn = _dot(dkv, w_ref[...], NT)
        xhat, _ = _rms_rows(mem_ref[...], None)
        dg_o[...] = jnp.sum(dmemn * xhat, axis=0, keepdims=True)

    vm = pl.BlockSpec(memory_space=pltpu.VMEM)
    return pl.pallas_call(
        body, name="mem_prep_bwd", in_specs=[vm] * 9, out_specs=[vm] * 3,
        out_shape=[jax.ShapeDtypeStruct((D_MODEL, D_MODEL), F32), jax.ShapeDtypeStruct((1, D_MODEL), F32),
                   jax.ShapeDtypeStruct((1, LANES), F32)],
        scratch_shapes=[pltpu.VMEM((N_MEM, D_MODEL), BF16)],
        compiler_params=pltpu.CompilerParams(vmem_limit_bytes=VMEM_MID),
    )(mem, g_mem, memn, kv, w_kv, kn_gain, gm128, dmk, dmv)


SLOT_O = D_MODEL // N_SHARD


def _merge_fwd(proj, b_gate, o3, w3, T, tb):
    def body(gl_ref, bg_ref, oa_ref, of_ref, om_ref, wa_ref, wf_ref, wm_ref, out_ref):
        o_refs = (oa_ref, of_ref, om_ref)
        w_refs = (wa_ref, wf_ref, wm_ref)
        for n in range(N_SHARD):
            acc = jnp.zeros((tb, SLOT_O), F32)
            for b in range(3):
                c0 = b * D_MODEL + n * SLOT_O
                g = jax.nn.sigmoid(gl_ref[:, c0:c0 + SLOT_O] + bg_ref[:, c0:c0 + SLOT_O])
                acc = acc + g * _dot(o_refs[b][...], w_refs[b][n])
            out_ref[:, n * SLOT_O:(n + 1) * SLOT_O] = acc.astype(out_ref.dtype)

    rows = pl.BlockSpec((tb, 512), lambda i: (i, 0))
    wspec = pl.BlockSpec((N_SHARD, 512, SLOT_O), lambda i: (0, 0, 0))
    return pl.pallas_call(
        body, name="merge_fwd", grid=(T // tb,),
        in_specs=[pl.BlockSpec((tb, GATE_W), lambda i: (i, 1)), pl.BlockSpec((1, GATE_W), lambda i: (0, 0)),
                  rows, rows, rows, wspec, wspec, wspec],
        out_specs=pl.BlockSpec((tb, D_MODEL), lambda i: (i, 0)),
        out_shape=jax.ShapeDtypeStruct((T, D_MODEL), BF16),
        compiler_params=_cparams("parallel", vmem=VMEM_BIG),
    )(proj, b_gate, *o3, *w3)


def _merge_bwd(proj, b_gate, o3, w3, dmerged, T, tb):
    heads = (SWA_HEADS, FOX_HEADS, MEM_HEADS)

    def body(gl_ref, bg_ref, oa_ref, of_ref, om_ref, wa_ref, wf_ref, wm_ref, dm_ref,
             dgl_o, doa_o, dof_o, dom_o, dla_o, dlf_o, dlm_o, dwa_o, dwf_o, dwm_o, dbg_o):
        i = pl.program_id(0)
        o_refs = (oa_ref, of_ref, om_ref)
        w_refs = (wa_ref, wf_ref, wm_ref)
        do_refs = (doa_o, dof_o, dom_o)
        dl_refs = (dla_o, dlf_o, dlm_o)
        dw_refs = (dwa_o, dwf_o, dwm_o)

        @pl.when(i == 0)
        def _():
            for r in dw_refs:
                r[...] = jnp.zeros_like(r)
            dbg_o[...] = jnp.zeros_like(dbg_o)

        lane = _lane((tb, LANES))
        for b in range(3):
            ob = o_refs[b][...]
            do = jnp.zeros((tb, 512), F32)
            for n in range(N_SHARD):
                c0 = b * D_MODEL + n * SLOT_O
                g = jax.nn.sigmoid(gl_ref[:, c0:c0 + SLOT_O] + bg_ref[:, c0:c0 + SLOT_O])
                dm = dm_ref[:, n * SLOT_O:(n + 1) * SLOT_O]
                y = _dot(ob, w_refs[b][n])
                dgl = dm * y * g * (1.0 - g)
                dgl_o[:, c0:c0 + SLOT_O] = dgl.astype(dgl_o.dtype)
                dbg_o[:, c0:c0 + SLOT_O] += jnp.sum(dgl, axis=0, keepdims=True)
                dy = (dm * g).astype(BF16)
                do = do + _dot(dy, w_refs[b][n], NT)
                dw_refs[b][n] += _dot(ob, dy, TN)
            do_refs[b][...] = do.astype(BF16)
            prod = do * ob.astype(F32)
            for c in range(4):
                blk = prod[:, c * LANES:(c + 1) * LANES]
                if heads[b] == 8:
                    lo = jnp.sum(jnp.where(lane < 64, blk, 0.0), axis=1, keepdims=True)
                    hi = jnp.sum(jnp.where(lane >= 64, blk, 0.0), axis=1, keepdims=True)
                    if b == 1:
                        aug = jnp.zeros((tb, LANES), F32)
                        for sub, dl in enumerate((lo, hi)):
                            for e, piece in enumerate(_split3(-dl)):
                                aug = jnp.where(lane == AUG_STRIDE * sub + AUG_C + e, piece.astype(F32), aug)
                        dl_refs[b][:, c * LANES:(c + 1) * LANES] = aug.astype(BF16)
                    else:
                        dl_refs[b][2 * c] = jnp.broadcast_to(lo, (tb, LANES))
                        dl_refs[b][2 * c + 1] = jnp.broadcast_to(hi, (tb, LANES))
                else:
                    dl_refs[b][c] = jnp.broadcast_to(jnp.sum(blk, axis=1, keepdims=True), (tb, LANES))

    rows = pl.BlockSpec((tb, 512), lambda i: (i, 0))
    wspec = pl.BlockSpec((N_SHARD, 512, SLOT_O), lambda i: (0, 0, 0))
    stat = lambda h: pl.BlockSpec((h, tb, LANES), lambda i: (0, i, 0))
    return pl.pallas_call(
        body, name="merge_bwd", grid=(T // tb,),
        in_specs=[pl.BlockSpec((tb, GATE_W), lambda i: (i, 1)), pl.BlockSpec((1, GATE_W), lambda i: (0, 0)),
                  rows, rows, rows, wspec, wspec, wspec, pl.BlockSpec((tb, D_MODEL), lambda i: (i, 0))],
        out_specs=[pl.BlockSpec((tb, GATE_W), lambda i: (i, 0)), rows, rows, rows,
                   stat(8), rows, stat(4), wspec, wspec, wspec, pl.BlockSpec((1, GATE_W), lambda i: (0, 0))],
        out_shape=[jax.ShapeDtypeStruct((T, GATE_W), BF16)] + [jax.ShapeDtypeStruct((T, 512), BF16)] * 3
        + [jax.ShapeDtypeStruct((8, T, LANES), F32), jax.ShapeDtypeStruct((T, 512), BF16),
           jax.ShapeDtypeStruct((4, T, LANES), F32)]
        + [jax.ShapeDtypeStruct((N_SHARD, 512, SLOT_O), F32)] * 3 + [jax.ShapeDtypeStruct((1, GATE_W), F32)],
        compiler_params=_cparams("arbitrary", vmem=VMEM_BIG),
    )(proj, b_gate, *o3, *w3, dmerged)


def _local_step(x, h, mem, tgt, small, g_in, w_kv, w_o3, w_out, w_up, w_down, reducer):
    T = x.shape[0]
    tm = min(512, T)
    tile2 = lambda v: jnp.tile(v.reshape(1, -1), (1, LANES // v.size))
    gains = jnp.concatenate([tile2(small["qn_swa"]), tile2(small["kn_swa"]), tile2(small["qn_fox"]),
                             tile2(small["kn_fox"]), tile2(small["qn_mem"]), jnp.zeros((3, LANES), F32)], axis=0)
    kn_mem = small["kn_mem"].reshape(1, LANES)
    bfor = jnp.pad(small["b_forget"].reshape(1, -1), ((0, 0), (0, LANES - FOX_HEADS)))
    gm64 = _group_mean_matrix(64)
    gm128 = _group_mean_matrix(128)
    tb_prep = min(512, T)
    ones = jnp.ones((tb_prep, tb_prep), F32)
    tril = jnp.tril(ones).astype(BF16)
    triu = jnp.triu(ones).astype(BF16)
    bucket = _t5_bucket_matrix()
    g_mix, g_mlp, g_mem = small["g_mix"], small["g_mlp"], small["g_mem"]
    b_gate = small["b_gate"]
    sinks = small["sink_swa"].reshape(-1)

    tl = min(1024, T)
    sq = pl.BlockSpec((tl, D_MODEL), lambda i, j, k: (i, j))
    wc = _w_in_to_segments(g_in)
    (proj,) = _matmul(
        "mm_proj", h, wc, dims=NN, grid=(T // tl, PROJ_W // PROJ_TN, 1),
        a_spec=pl.BlockSpec((tl, D_MODEL), lambda i, j, k: (i, 0)),
        b_spec=pl.BlockSpec((D_MODEL, PROJ_TN), lambda i, j, k: (0, j)),
        acc_shape=(tl, PROJ_TN),
        outs=[(jax.ShapeDtypeStruct((T, PROJ_W), F32), pl.BlockSpec((tl, PROJ_TN), lambda i, j, k: (i, j)))],
        epilogue=_epi_store)
    qa, qf, kf, vf, qm, kad, vad, qf_aug, kf_aug = _prep_fwd(proj, gains, bfor, tril, gm64, gm128, T, tb_prep)
    bias = _swa_bias(small["rel_bias"], bucket)
    o_swa, lse_swa = _swa_fwd(sinks, qa, kad, vad, bias, T)
    o_fox, qf_aug_bwd = _fox_fwd(qf, qf_aug, kf, kf_aug, vf, T, min(FOX_TQ, T), min(FOX_TK, T))
    memn, kv, mk, mv = _mem_prep_fwd(mem, g_mem, w_kv, kn_mem, gm128)
    o_mem, lse_mem = _mem_fwd(qm, mk, mv, T, min(MEM_TQ, T))
    o3 = (o_swa, o_fox, o_mem)
    merged = _merge_fwd(proj, b_gate, o3, w_o3, T, min(512, T))

    def epi_residual(acc, extra_refs, out_refs, ij):
        out_refs[0][...] = extra_refs[0][...] + acc

    row_full = pl.BlockSpec((tm, D_MODEL), lambda i, j, k: (i, 0))
    row_big = pl.BlockSpec((tl, D_MODEL), lambda i, j, k: (i, 0))
    whole = pl.BlockSpec((D_MODEL, D_MODEL), lambda i, j, k: (0, 0))
    (x2,) = _matmul(
        "mm_out", merged, w_out, dims=NN, grid=(T // tl, 1, 1),
        a_spec=row_big, b_spec=whole,
        acc_shape=(tl, D_MODEL), extra=[(x, row_big)],
        outs=[(jax.ShapeDtypeStruct((T, D_MODEL), F32), row_big)], epilogue=epi_residual)
    hm = _rmsnorm("rms_mlp", x2, g_mlp, tm)

    def epi_relu2(acc, extra_refs, out_refs, ij):
        out_refs[0][...] = acc.astype(BF16)
        r = jnp.maximum(acc, 0.0)
        out_refs[1][...] = (r * r).astype(BF16)

    up, u = _matmul(
        "mm_up", hm, w_up, dims=NN, grid=(T // tl, N_SHARD, 1),
        a_spec=row_big, b_spec=pl.BlockSpec((None, D_MODEL, D_MODEL), lambda i, j, k: (j, 0, 0)),
        acc_shape=(tl, D_MODEL),
        outs=[(jax.ShapeDtypeStruct((T, D_FF), BF16), sq), (jax.ShapeDtypeStruct((T, D_FF), BF16), sq)],
        epilogue=epi_relu2)

    def epi_loss(acc, extra_refs, out_refs, ij):
        y = extra_refs[0][...] + acc
        err = y - extra_refs[1][...]
        dyv = err * (1.0 / D_MODEL)
        out_refs[0][...] = dyv
        out_refs[2][...] = dyv.astype(BF16)
        sq = jnp.sum(jnp.sum(err * err, axis=1, keepdims=True), axis=0, keepdims=True)

        @pl.when(ij[0] == 0)
        def _():
            out_refs[1][...] = jnp.zeros_like(out_refs[1])

        out_refs[1][...] += jnp.broadcast_to(sq, out_refs[1].shape)

    kblk = pl.BlockSpec((tl, D_MODEL), lambda i, j, k: (i, k))
    dy, loss_acc, dy_bf = _matmul(
        "mm_down", u, w_down, dims=NN, grid=(T // tl, 1, N_SHARD),
        a_spec=kblk, b_spec=pl.BlockSpec((D_MODEL, D_MODEL), lambda i, j, k: (k, 0)),
        acc_shape=(tl, D_MODEL), extra=[(x2, row_big), (tgt, row_big)],
        outs=[(jax.ShapeDtypeStruct((T, D_MODEL), F32), row_big),
              (jax.ShapeDtypeStruct((8, LANES), F32), pl.BlockSpec((8, LANES), lambda i, j, k: (0, 0))),
              (jax.ShapeDtypeStruct((T, D_MODEL), BF16), row_big)],
        epilogue=epi_loss)
    loss = loss_acc[0, 0] * (0.5 / D_MODEL)

    def epi_dup(acc, extra_refs, out_refs, ij):
        out_refs[0][...] = (acc * (2.0 * jnp.maximum(extra_refs[0][...].astype(F32), 0.0))).astype(BF16)

    (dup,) = _matmul(
        "mm_dup", dy_bf, w_down, dims=NT, grid=(T // tl, N_SHARD, 1),
        a_spec=row_big, b_spec=pl.BlockSpec((D_MODEL, D_MODEL), lambda i, j, k: (j, 0)),
        acc_shape=(tl, D_MODEL), extra=[(up, sq)],
        outs=[(jax.ShapeDtypeStruct((T, D_FF), BF16), sq)], epilogue=epi_dup)

    nkt = T // tl
    t_rows = pl.BlockSpec((tl, D_MODEL), lambda i, j, k: (k, i))
    t_cols = pl.BlockSpec((tl, D_MODEL), lambda i, j, k: (k, j))
    (d_w_down,) = _matmul(
        "mm_dw_down", u, dy_bf, dims=TN, grid=(N_SHARD, 1, nkt),
        a_spec=t_rows, b_spec=t_cols, acc_shape=(D_MODEL, D_MODEL),
        outs=[(jax.ShapeDtypeStruct((D_FF, D_MODEL), F32), pl.BlockSpec((D_MODEL, D_MODEL), lambda i, j, k: (i, 0)))],
        epilogue=_epi_store)
    (d_w_up,) = _matmul(
        "mm_dw_up", hm, dup, dims=TN, grid=(1, N_SHARD, nkt),
        a_spec=t_rows, b_spec=t_cols, acc_shape=(D_MODEL, D_MODEL),
        outs=[(jax.ShapeDtypeStruct((N_SHARD, D_MODEL, D_MODEL), F32),
               pl.BlockSpec((None, D_MODEL, D_MODEL), lambda i, j, k: (j, 0, 0)))],
        epilogue=_epi_store)

    def epi_rms_bwd(acc, extra_refs, out_refs, ij):
        dx, dg = _rmsnorm_bwd_rows(acc, extra_refs[0][...], extra_refs[1][...])
        out_refs[0][...] = dx + extra_refs[2][...]

        @pl.when(ij[0] == 0)
        def _():
            out_refs[1][...] = jnp.zeros_like(out_refs[1])

        out_refs[1][...] += dg

    gain_spec = pl.BlockSpec((1, D_MODEL), lambda i, j, k: (0, 0))
    dx2, d_g_mlp = _matmul(
        "mm_dhm", dup, w_up, dims=NT, grid=(T // tl, 1, N_SHARD),
        a_spec=kblk, b_spec=pl.BlockSpec((None, D_MODEL, D_MODEL), lambda i, j, k: (k, 0, 0)),
        acc_shape=(tl, D_MODEL), extra=[(x2, row_big), (g_mlp, gain_spec), (dy, row_big)],
        outs=[(jax.ShapeDtypeStruct((T, D_MODEL), F32), row_big), (jax.ShapeDtypeStruct((1, D_MODEL), F32), gain_spec)],
        epilogue=epi_rms_bwd)

    (dmerged,) = _matmul(
        "mm_dmerged", dx2, w_out, dims=NT, grid=(T // tl, 1, 1),
        a_spec=row_big, b_spec=whole,
        acc_shape=(tl, D_MODEL), outs=[(jax.ShapeDtypeStruct((T, D_MODEL), F32), row_big)], epilogue=_epi_store)
    (d_w_out,) = _matmul(
        "mm_dw_out", merged, dx2, dims=TN, grid=(1, 1, nkt),
        a_spec=t_rows, b_spec=t_cols, acc_shape=(D_MODEL, D_MODEL),
        outs=[(jax.ShapeDtypeStruct((D_MODEL, D_MODEL), F32), whole)],
        epilogue=_epi_store)
    (dgl, do_swa, do_fox, do_mem, dl_swa, do_fox_aug, dl_mem, d_wo_swa, d_wo_fox, d_wo_mem, d_b_gate) = _merge_bwd(
        proj, b_gate, o3, w_o3, dmerged, T, min(512, T))

    dqm, dmk, dmv = _mem_bwd(qm, mk, mv, do_mem, lse_mem, dl_mem, T, min(MEM_TQ, T))
    d_w_kv, d_g_mem, d_kn_mem = _mem_prep_bwd(mem, g_mem, memn, kv, w_kv, kn_mem, gm128, dmk, dmv)
    do_swa = reducer.early_start({"w_mlp_down": d_w_down, "w_mlp_up": d_w_up, "w_out": d_w_out, "w_mem_kv": d_w_kv,
                                  "w_o_swa": d_wo_swa, "w_o_fox": d_wo_fox, "w_o_mem": d_wo_mem}, do_swa)
    dqa, dkad, dvad, dbias, dsk = _swa_bwd(sinks, qa, kad, vad, bias, do_swa, lse_swa, dl_swa, T)
    dqa, do_fox = reducer.early_send((dqa, do_fox))
    dqf, dqf_aug, dkf, dkf_aug, dvf = _fox_bwd(qf, qf_aug_bwd, kf, kf_aug, vf, do_fox, do_fox_aug, T,
                                               min(FOX_BWD_TQ, T), min(FOX_BWD_TK, T))
    dvf = reducer.early_finish(dvf)
    d_rel = _swa_bias_bwd(dbias, bucket)
    dlo, gacc = _prep_bwd(proj, dqa, dkad, dvad, dqf, dkf, dvf, dqm, dqf_aug, dkf_aug, gains, bfor, triu, gm64, gm128,
                          T, tb_prep)

    def dwc_half(name, dpart):
        (res,) = _matmul(
            name, h, dpart, dims=TN, grid=(1, LO_W // D_MODEL, nkt),
            a_spec=t_rows, b_spec=t_cols, acc_shape=(D_MODEL, D_MODEL),
            outs=[(jax.ShapeDtypeStruct((D_MODEL, LO_W), F32), pl.BlockSpec((D_MODEL, D_MODEL), lambda i, j, k: (0, j)))],
            epilogue=_epi_store)
        return res

    d_wc_lo = dwc_half("mm_dwc_lo", dlo)
    d_wc_gl = dwc_half("mm_dwc_gl", dgl)
    dlo = reducer.late_start({"wc_lo": d_wc_lo, "wc_gl": d_wc_gl}, dlo)
    (dh_lo,) = _matmul(
        "mm_dh_lo", dlo, wc, dims=NT, grid=(T // tl, 1, LO_W // D_MODEL),
        a_spec=kblk, b_spec=pl.BlockSpec((D_MODEL, D_MODEL), lambda i, j, k: (0, k)),
        acc_shape=(tl, D_MODEL), outs=[(jax.ShapeDtypeStruct((T, D_MODEL), F32), row_big)], epilogue=_epi_store)
    dh_lo = reducer.late_send(dh_lo)

    def epi_dx(acc, extra_refs, out_refs, ij):
        dhh = acc + extra_refs[3][...]
        dx, dg = _rmsnorm_bwd_rows(dhh, extra_refs[0][...], extra_refs[1][...])
        out_refs[0][...] = dx + extra_refs[2][...]

        @pl.when(ij[0] == 0)
        def _():
            out_refs[1][...] = jnp.zeros_like(out_refs[1])

        out_refs[1][...] += dg

    grad_x, d_g_mix = _matmul(
        "mm_dh_gl", dgl, wc, dims=NT, grid=(T // tl, 1, GATE_W // D_MODEL),
        a_spec=kblk, b_spec=pl.BlockSpec((D_MODEL, D_MODEL), lambda i, j, k: (0, k + LO_W // D_MODEL)),
        acc_shape=(tl, D_MODEL), extra=[(x, row_big), (g_mix, gain_spec), (dx2, row_big), (dh_lo, row_big)],
        outs=[(jax.ShapeDtypeStruct((T, D_MODEL), F32), row_big), (jax.ShapeDtypeStruct((1, D_MODEL), F32), gain_spec)],
        epilogue=epi_dx, vmem=VMEM_MAX)

    fold64 = lambda row: (row[:64] + row[64:]).reshape(1, 64)
    grads = {
        "g_mix": d_g_mix, "b_gate": d_b_gate, "b_forget": gacc[5, :FOX_HEADS].reshape(1, FOX_HEADS),
        "qn_swa": fold64(gacc[0]), "kn_swa": fold64(gacc[1]),
        "sink_swa": -dsk[:, :SWA_GROUP, 0].reshape(1, SWA_HEADS), "rel_bias": d_rel[:, :SWA_HEADS],
        "qn_fox": fold64(gacc[2]), "kn_fox": fold64(gacc[3]),
        "g_mem": d_g_mem, "qn_mem": gacc[4].reshape(1, LANES), "kn_mem": d_kn_mem, "g_mlp": d_g_mlp,
    }
    return loss, grad_x, grads


MESH = pl.DeviceIdType.MESH


def _place():
    x, y, c = lax.axis_index("x"), lax.axis_index("y"), lax.axis_index("c")
    chips = [(1 - x, y), (x, 1 - y), (1 - x, 1 - y)]
    return x, y, c, chips


def _handshake(peers):
    barrier = pltpu.get_barrier_semaphore()
    for peer in peers:
        pl.semaphore_signal(barrier, inc=1, device_id=peer, device_id_type=MESH)
    pl.semaphore_wait(barrier, len(peers))


def _all_gather_shards_async(name, collective_id, slots):
    n = len(slots)
    bufs = [jax.new_ref(s, memory_space=pltpu.MemorySpace.HBM) for s in slots]

    def body(ici_send, ici_recv, d2d_send, d2d_recv):
        x, y, c, chips = _place()
        sibling = (x, y, 1 - c)
        me = 2 * x + y
        _handshake([(px, py, c) for px, py in chips] + [sibling])

        def half(a, who):
            hr = slots[a].shape[1] // 2
            return pl.ds(pl.multiple_of(who * hr, hr), hr)

        def ici(a, j, slot, to):
            return pltpu.make_async_remote_copy(
                src_ref=bufs[a].at[me, half(a, c)], dst_ref=bufs[a].at[slot, half(a, c)],
                send_sem=ici_send.at[3 * a + j], recv_sem=ici_recv.at[3 * a + j], device_id=to, device_id_type=MESH)

        def d2d(a, j, slot, which):
            part = bufs[a].at[slot, half(a, which)]
            return pltpu.make_async_remote_copy(
                src_ref=part, dst_ref=part, send_sem=d2d_send.at[3 * a + j], recv_sem=d2d_recv.at[3 * a + j],
                device_id=sibling, device_id_type=MESH)

        sends = [ici(a, j, me, (*chip, c)) for a in range(n) for j, chip in enumerate(chips)]
        for cp in sends:
            cp.start()
        passed = []
        for a in range(n):
            for j, (px, py) in enumerate(chips):
                ici(a, j, 2 * px + py, (px, py, c)).wait_recv()
                cp = d2d(a, j, 2 * px + py, c)
                cp.start()
                passed.append(cp)
        for a in range(n):
            for j, (px, py) in enumerate(chips):
                d2d(a, j, 2 * px + py, 1 - c).wait_recv()
        for cp in sends + passed:
            cp.wait_send()

    pl.kernel(
        body, mesh=plsc.ScalarSubcoreMesh(axis_name="seq", num_cores=1), name=name,
        scratch_types=[pltpu.SemaphoreType.DMA((3 * n,))] * 4,
        compiler_params=pltpu.CompilerParams(collective_id=collective_id),
    )()
    return [b[...] for b in bufs]


def _sequencer_call(name, collective_id, n_sems, body):
    pl.kernel(
        body, mesh=plsc.ScalarSubcoreMesh(axis_name="seq", num_cores=1), name=name,
        scratch_types=[pltpu.SemaphoreType.DMA((n_sems,))] * 2,
        compiler_params=pltpu.CompilerParams(collective_id=collective_id),
    )()


def _hbm_ref(value):
    return jax.new_ref(value, memory_space=pltpu.MemorySpace.HBM)


def _pair_exchange(name, collective_id, gs):
    n = len(gs)
    src = [_hbm_ref(g) for g in gs]
    stage = [jax.empty_ref(jax.ShapeDtypeStruct((N_SHARD, g.shape[1] // 2, g.shape[2]), g.dtype),
                           memory_space=pltpu.MemorySpace.HBM) for g in gs]

    def body(send_sem, recv_sem):
        x, y, c, _ = _place()
        sibling = (x, y, 1 - c)
        _handshake([sibling])
        copies = []
        for a in range(n):
            hr = gs[a].shape[1] // 2
            theirs = pl.ds(pl.multiple_of((1 - c) * hr, hr), hr)
            copies.append(pltpu.make_async_remote_copy(
                src_ref=src[a].at[:, theirs, :], dst_ref=stage[a], send_sem=send_sem.at[a], recv_sem=recv_sem.at[a],
                device_id=sibling, device_id_type=MESH))
        for cp in copies:
            cp.start()
        for cp in copies:
            cp.wait()

    _sequencer_call(name, collective_id, n, body)
    return [s[...] for s in stage]


def _chip_exchange(name, collective_id, sums):
    n = len(sums)
    src = [_hbm_ref(s) for s in sums]
    got = [jax.empty_ref(jax.ShapeDtypeStruct((3,) + s.shape[1:], s.dtype), memory_space=pltpu.MemorySpace.HBM)
           for s in sums]

    def body(send_sem, recv_sem):
        x, y, c, chips = _place()
        _handshake([(px, py, c) for px, py in chips])
        copies = []
        for a in range(n):
            for j, (px, py) in enumerate(chips):
                copies.append(pltpu.make_async_remote_copy(
                    src_ref=src[a].at[2 * px + py], dst_ref=got[a].at[j],
                    send_sem=send_sem.at[3 * a + j], recv_sem=recv_sem.at[3 * a + j],
                    device_id=(px, py, c), device_id_type=MESH))
        for cp in copies:
            cp.start()
        for cp in copies:
            cp.wait()

    _sequencer_call(name, collective_id, 3 * n, body)
    return [g[...] for g in got]


def _pair_gather(name, collective_id, fulls):
    n = len(fulls)
    full = [_hbm_ref(f) for f in fulls]

    def body(send_sem, recv_sem):
        x, y, c, _ = _place()
        sibling = (x, y, 1 - c)
        _handshake([sibling])
        copies = []
        for a in range(n):
            hr = fulls[a].shape[0] // 2
            mine = full[a].at[pl.ds(pl.multiple_of(c * hr, hr), hr)]
            copies.append(pltpu.make_async_remote_copy(
                src_ref=mine, dst_ref=mine, send_sem=send_sem.at[a], recv_sem=recv_sem.at[a],
                device_id=sibling, device_id_type=MESH))
        for cp in copies:
            cp.start()
        for cp in copies:
            cp.wait()

    _sequencer_call(name, collective_id, n, body)
    return [f[...] for f in full]


ELEMENTWISE_BLOCK_ELEMS = 256 * 1024


def _row_block(rows, cols):
    rb = 8
    while rb * 2 * cols <= ELEMENTWISE_BLOCK_ELEMS and rb * 2 <= rows:
        rb *= 2
    return rb


def _pair_sum(name, place, g, stage):
    _, R, C = g.shape
    hr = R // 2
    rb = _row_block(hr, C)
    nb = hr // rb

    def body(place_ref, g_ref, st_ref, sum_bf, own_f32):
        s = pl.program_id(1)
        tot = g_ref[...] + st_ref[...]
        sum_bf[...] = tot.astype(BF16)

        @pl.when(s == place_ref[0])
        def _():
            own_f32[...] = tot

    return pl.pallas_call(
        body, name=name,
        grid_spec=pltpu.PrefetchScalarGridSpec(
            num_scalar_prefetch=1, grid=(nb, N_SHARD),
            in_specs=[pl.BlockSpec((None, rb, C), lambda i, s, pr: (s, pr[1] * nb + i, 0)),
                      pl.BlockSpec((None, rb, C), lambda i, s, pr: (s, i, 0))],
            out_specs=[pl.BlockSpec((None, rb, C), lambda i, s, pr: (s, i, 0)),
                       pl.BlockSpec((rb, C), lambda i, s, pr: (i, 0))]),
        out_shape=[jax.ShapeDtypeStruct((N_SHARD, hr, C), BF16), jax.ShapeDtypeStruct((hr, C), F32)],
        compiler_params=_cparams("arbitrary", "arbitrary"),
    )(place, g, stage)


def _final_sum(name, place, own, got):
    hr, C = own.shape
    rb = _row_block(hr, C)
    nb = hr // rb

    def body(place_ref, own_ref, got_ref, o_ref):
        o_ref[...] = ((own_ref[...] + got_ref[0].astype(F32)) + got_ref[1].astype(F32)) + got_ref[2].astype(F32)

    return pl.pallas_call(
        body, name=name,
        grid_spec=pltpu.PrefetchScalarGridSpec(
            num_scalar_prefetch=1, grid=(nb,),
            in_specs=[pl.BlockSpec((rb, C), lambda i, pr: (i, 0)), pl.BlockSpec((3, rb, C), lambda i, pr: (0, i, 0))],
            out_specs=pl.BlockSpec((rb, C), lambda i, pr: (pr[1] * nb + i, 0))),
        out_shape=jax.ShapeDtypeStruct((2 * hr, C), F32),
        compiler_params=_cparams("arbitrary"),
    )(place, own, got)


def _adamw_math(w, g, m, v):
    m = ADAM_B1 * m + (1.0 - ADAM_B1) * g
    v = ADAM_B2 * v + (1.0 - ADAM_B2) * (g * g)
    m_hat = m / (1.0 - ADAM_B1 ** ADAM_STEP)
    v_hat = v / (1.0 - ADAM_B2 ** ADAM_STEP)
    delta = -ADAM_LR * (m_hat / (jnp.sqrt(v_hat) + ADAM_EPS) + ADAM_WD * w)
    return delta, m, v


def _adamw(name, w, g, m, v):
    R, Cw = w.shape
    Cg = g.shape[1]
    rb = _row_block(R, Cg)

    def body(w_ref, g_ref, m_ref, v_ref, g_o, d_o, m_o, v_o):
        gv = g_ref[...]
        delta, mn, vn = _adamw_math(w_ref[...], gv, m_ref[...], v_ref[...])
        g_o[...] = gv
        d_o[...] = delta
        m_o[...] = mn
        v_o[...] = vn

    blk = pl.BlockSpec((rb, Cg), lambda i: (i, 0))
    return pl.pallas_call(
        body, name=name, grid=(R // rb,),
        in_specs=[blk] * 4, out_specs=[blk] * 4,
        out_shape=[jax.ShapeDtypeStruct((R, Cw), F32)] * 4,
        compiler_params=_cparams("parallel"),
    )(w, g, m, v)


N_DEV = 8
SMALL_ROWS = 64


def _small_allreduce_adamw(g, w, m, v):
    def body(g_ref, w_ref, m_ref, v_ref, all_ref, gs_o, d_o, m_o, v_o, send_sems, recv_sems, local_sem):
        x, y, c, chips = _place()
        me, sibling = (x, y, c), (x, y, 1 - c)

        def rows(px, py, pc):
            return all_ref.at[pl.ds(pl.multiple_of((4 * px + 2 * py + pc) * SMALL_ROWS, SMALL_ROWS), SMALL_ROWS), :]

        def copy(k, block, to, src=None):
            return pltpu.make_async_remote_copy(
                src_ref=rows(*block) if src is None else src, dst_ref=rows(*block),
                send_sem=send_sems.at[k], recv_sem=recv_sems.at[k], device_id=to, device_id_type=MESH)

        mine = pltpu.make_async_copy(g_ref, rows(*me), local_sem)
        mine.start()
        first = [copy(0, me, sibling, src=g_ref)]
        first += [copy(1 + j, me, (*chip, c), src=g_ref) for j, chip in enumerate(chips)]
        for cp in first:
            cp.start()
        passed = [copy(4 + j, (*chip, c), sibling) for j, chip in enumerate(chips)]
        for j, chip in enumerate(chips):
            copy(1 + j, (*chip, c), me).wait_recv()
            passed[j].start()
        copy(0, sibling, me).wait_recv()
        for j, chip in enumerate(chips):
            copy(4 + j, (*chip, 1 - c), me).wait_recv()
        for cp in first + passed:
            cp.wait_send()
        mine.wait()

        tot = all_ref[0:SMALL_ROWS, :]
        for d in range(1, N_DEV):
            tot = tot + all_ref[d * SMALL_ROWS:(d + 1) * SMALL_ROWS, :]
        delta, mn, vn = _adamw_math(w_ref[...], tot, m_ref[...], v_ref[...])
        gs_o[...] = tot
        d_o[...] = delta
        m_o[...] = mn
        v_o[...] = vn

    vm = pl.BlockSpec(memory_space=pltpu.VMEM)
    shp = jax.ShapeDtypeStruct((SMALL_ROWS, LANES), F32)
    res = pl.pallas_call(
        body, name="small_allreduce_adamw", in_specs=[vm] * 4, out_specs=[vm] * 5,
        out_shape=[jax.ShapeDtypeStruct((N_DEV * SMALL_ROWS, LANES), F32), shp, shp, shp, shp],
        scratch_shapes=[pltpu.SemaphoreType.DMA((7,)), pltpu.SemaphoreType.DMA((7,)), pltpu.SemaphoreType.DMA],
    )(g, w, m, v)
    return res[1:]


SMALL_NAMES = ("g_mix", "b_gate", "b_forget", "qn_swa", "kn_swa", "sink_swa", "rel_bias", "qn_fox", "kn_fox",
               "g_mem", "qn_mem", "kn_mem", "g_mlp")
BIG_NAMES = ("w_in", "w_mem_kv", "w_o_swa", "w_o_fox", "w_o_mem", "w_out", "w_mlp_up", "w_mlp_down")
WEIGHT_NAMES = ("g_mix", "w_in", "b_gate", "b_forget", "qn_swa", "kn_swa", "sink_swa", "rel_bias", "qn_fox", "kn_fox",
                "g_mem", "w_mem_kv", "qn_mem", "kn_mem", "w_o_swa", "w_o_fox", "w_o_mem", "w_out", "g_mlp",
                "w_mlp_up", "w_mlp_down")


def _pack_small(parts, extra=None):
    rows = []
    for n in SMALL_NAMES:
        flat = parts[n].reshape(-1).astype(F32)
        flat = jnp.pad(flat, (0, (-flat.size) % LANES))
        rows.append(flat.reshape(-1, LANES))
    if extra is not None:
        rows.append(jnp.pad(extra.reshape(1, 1), ((0, 0), (0, LANES - 1))))
    packed = jnp.concatenate(rows, axis=0)
    return jnp.pad(packed, ((0, SMALL_ROWS - packed.shape[0]), (0, 0)))


def _unpack_small(packed, shapes):
    out, r = {}, 0
    for n in SMALL_NAMES:
        size = math.prod(shapes[n])
        nr = -(-size // LANES)
        out[n] = packed[r:r + nr].reshape(-1)[:size].reshape(shapes[n])
        r += nr
    return out, packed[r, 0]


W_IN_SEGMENTS = ((C_QA, 0, 512), (C_QF, 768, 512), (C_KF, 1280, 512), (C_VF, 1792, 512), (C_QM, 2312, 512),
                 (C_KA, 512, 128), (C_VA, 640, 128), (C_FL, 2304, FOX_HEADS), (C_GL, 2824, GATE_W))
RELAYOUT_ROWS = 256


def _permute_pieces(src_of_dst):
    blocks = []
    for b in range(len(src_of_dst) // LANES):
        runs, lane = [], 0
        while lane < LANES:
            src = src_of_dst[b * LANES + lane]
            if src is None:
                lane += 1
                continue
            plane, col = src
            end = lane + 1
            while (end < LANES and src_of_dst[b * LANES + end] == (plane, col + end - lane)
                   and (col + end - lane) // LANES == col // LANES):
                end += 1
            runs.append((plane, col // LANES, (lane - col) % LANES, lane, end))
            lane = end
        blocks.append(runs)
    return blocks


def _permuted_block(runs, load, rows):
    lane = _lane((rows, LANES))
    acc = jnp.zeros((rows, LANES), F32)
    for plane, blk, shift, lo, hi in runs:
        x = load(plane, blk).astype(F32)
        if shift:
            x = pltpu.roll(x, shift, 1)
        acc = x if (lo, hi) == (0, LANES) else jnp.where((lane >= lo) & (lane < hi), x, acc)
    return acc


def _w_in_to_segments(g_in):
    src_of_dst = [None] * PROJ_W
    for mine, theirs, width in W_IN_SEGMENTS:
        for k in range(width):
            src_of_dst[mine + k] = ((theirs + k) // IN_SHARD, (theirs + k) % IN_SHARD)
    blocks = _permute_pieces(src_of_dst)
    rb = RELAYOUT_ROWS

    def body(src_ref, out_ref):
        for b, runs in enumerate(blocks):
            blk = _permuted_block(runs, lambda p, c: src_ref[p, :, c * LANES:(c + 1) * LANES], rb)
            out_ref[:, b * LANES:(b + 1) * LANES] = blk.astype(out_ref.dtype)

    return pl.pallas_call(
        body, name="w_in_to_segments", grid=(D_MODEL // rb,),
        in_specs=[pl.BlockSpec((N_SHARD, rb, IN_SHARD_PAD), lambda i: (0, i, 0))],
        out_specs=pl.BlockSpec((rb, PROJ_W), lambda i: (i, 0)),
        out_shape=jax.ShapeDtypeStruct((D_MODEL, PROJ_W), g_in.dtype),
        compiler_params=_cparams("parallel", vmem=VMEM_MID),
    )(g_in)


def _w_in_from_segments(lo, gl):
    mine_of_theirs = {}
    for mine, theirs, width in W_IN_SEGMENTS:
        for k in range(width):
            mine_of_theirs[theirs + k] = mine + k
    src_of_dst = [None] * (N_SHARD * IN_SHARD_PAD)
    for s in range(N_SHARD):
        for l in range(IN_SHARD):
            j = mine_of_theirs[s * IN_SHARD + l]
            src_of_dst[s * IN_SHARD_PAD + l] = (j // LO_W, j % LO_W)
    blocks = _permute_pieces(src_of_dst)
    per_slot = IN_SHARD_PAD // LANES
    rb = RELAYOUT_ROWS

    def body(lo_ref, gl_ref, out_ref):
        planes = (lo_ref, gl_ref)
        for b, runs in enumerate(blocks):
            blk = _permuted_block(runs, lambda p, c: planes[p][:, c * LANES:(c + 1) * LANES], rb)
            c0 = (b % per_slot) * LANES
            out_ref[b // per_slot, :, c0:c0 + LANES] = blk

    half = pl.BlockSpec((rb, LO_W), lambda i: (i, 0))
    return pl.pallas_call(
        body, name="w_in_from_segments", grid=(D_MODEL // rb,),
        in_specs=[half, half],
        out_specs=pl.BlockSpec((N_SHARD, rb, IN_SHARD_PAD), lambda i: (0, i, 0)),
        out_shape=jax.ShapeDtypeStruct((N_SHARD, D_MODEL, IN_SHARD_PAD), F32),
        compiler_params=_cparams("parallel", vmem=VMEM_MID),
    )(lo, gl)


def _after(first, then):
    return lax.optimization_barrier((first, then))


class _ReduceGroup:
    def __init__(self, tag, first_collective_id, place):
        self.tag, self.first_id, self.place = tag, first_collective_id, place

    def start(self, local, tie):
        self.names = tuple(local)
        mine, tie = _after([local[n] for n in self.names], tie)
        self.mine = mine
        self.staged = _pair_exchange("pair_exchange_" + self.tag, self.first_id, mine)
        return tie

    def send(self, tie):
        staged, tie = _after(self.staged, tie)
        sums = [_pair_sum("pair_sum_" + n, self.place, g, st) for n, g, st in zip(self.names, self.mine, staged)]
        travel, tie = _after([s[0] for s in sums], tie)
        self.own = [s[1] for s in sums]
        self.got = _chip_exchange("chip_exchange_" + self.tag, self.first_id + 1, travel)
        return tie

    def finish(self, tie):
        got, tie = _after(self.got, tie)
        halves = [_final_sum("final_sum_" + n, self.place, o, r) for n, o, r in zip(self.names, self.own, got)]
        halves, tie = _after(halves, tie)
        summed = _pair_gather("pair_gather_" + self.tag, self.first_id + 2, halves)
        self.summed = dict(zip(self.names, summed))
        return tie


class _GradReducer:
    def __init__(self, place):
        self.early = _ReduceGroup("early", 2, place)
        self.late = _ReduceGroup("late", 5, place)

    @staticmethod
    def _slot_rows(a):
        return a.reshape(N_SHARD, a.shape[0] // N_SHARD, a.shape[1])

    def early_start(self, g, tie):
        return self.early.start({"w_mlp_down": self._slot_rows(g["w_mlp_down"]), "w_mlp_up": g["w_mlp_up"],
                                 "w_out": self._slot_rows(g["w_out"]), "w_mem_kv": self._slot_rows(g["w_mem_kv"]),
                                 "w_o_swa": g["w_o_swa"], "w_o_fox": g["w_o_fox"], "w_o_mem": g["w_o_mem"]}, tie)

    def early_send(self, tie):
        return self.early.send(tie)

    def early_finish(self, tie):
        return self.early.finish(tie)

    def late_start(self, g, tie):
        d_in = _w_in_from_segments(g["wc_lo"], g["wc_gl"])
        return self.late.start({"w_in": d_in}, tie)

    def late_send(self, tie):
        return self.late.send(tie)

    def late_finish(self, tie):
        return self.late.finish(tie)

    @property
    def summed(self):
        return {**self.early.summed, **self.late.summed}


def kernel(x, mem, g_mix, w_in, b_gate, b_forget, qn_swa, kn_swa, sink_swa, rel_bias, qn_fox, kn_fox, g_mem, w_mem_kv, qn_mem, kn_mem, w_o_swa, w_o_fox, w_o_mem, w_out, g_mlp, w_mlp_up, w_mlp_down, loss_target, m_g_mix, m_w_in, m_b_gate, m_b_forget, m_qn_swa, m_kn_swa, m_sink_swa, m_rel_bias, m_qn_fox, m_kn_fox, m_g_mem, m_w_mem_kv, m_qn_mem, m_kn_mem, m_w_o_swa, m_w_o_fox, m_w_o_mem, m_w_out, m_g_mlp, m_w_mlp_up, m_w_mlp_down, v_g_mix, v_w_in, v_b_gate, v_b_forget, v_qn_swa, v_kn_swa, v_sink_swa, v_rel_bias, v_qn_fox, v_kn_fox, v_g_mem, v_w_mem_kv, v_qn_mem, v_kn_mem, v_w_o_swa, v_w_o_fox, v_w_o_mem, v_w_out, v_g_mlp, v_w_mlp_up, v_w_mlp_down):
    given = dict(locals())
    W = {n: given[n] for n in WEIGHT_NAMES}
    M = {n: given["m_" + n] for n in WEIGHT_NAMES}
    V = {n: given["v_" + n] for n in WEIGHT_NAMES}
    pad_in = ((0, 0), (0, IN_SHARD_PAD - IN_SHARD))

    shards = [jnp.pad(w_in[0].astype(BF16), pad_in)] + [W[n][0].astype(BF16) for n in BIG_NAMES[1:]]
    slots = [jnp.broadcast_to(s[None], (N_SHARD,) + s.shape) for s in shards]
    (g_in,) = _all_gather_shards_async("all_gather_w_in", 1, slots[:1])
    small = {n: (W[n] if n == "rel_bias" else W[n].reshape(1, -1)) for n in SMALL_NAMES}
    h = _rmsnorm("rms_mix", x[0], small["g_mix"], min(512, x.shape[1]))
    g_in, late, h, (m_in, v_in) = lax.optimization_barrier((g_in, slots[1:], h, (M["w_in"][0], V["w_in"][0])))
    M["w_in"], V["w_in"] = m_in[None], v_in[None]
    g_kv, g_oa, g_of, g_om, g_out, g_up, g_down = _all_gather_shards_async("all_gather_weights_async", 8, late)

    place = jnp.stack([2 * lax.axis_index("x") + lax.axis_index("y"), lax.axis_index("c")]).astype(jnp.int32)
    reducer = _GradReducer(place)
    loss, grad_x, grads = _local_step(
        x[0], h, mem[0], loss_target[0], small, g_in, g_kv.reshape(D_MODEL, D_MODEL), (g_oa, g_of, g_om),
        g_out.reshape(D_MODEL, D_MODEL), g_up, g_down.reshape(D_FF, D_MODEL), reducer)

    out = {}

    def adamw_of(names, summed):
        for n in names:
            res = _adamw("adamw_" + n, W[n][0], summed[n], M[n][0], V[n][0])
            out[n] = [r.reshape(W[n].shape) for r in res]

    adamw_of(reducer.early.names, reducer.early.summed)
    shapes = {n: W[n].shape for n in SMALL_NAMES}
    packed = _small_allreduce_adamw(_pack_small(grads, loss), _pack_small(W), _pack_small(M), _pack_small(V))
    done_meanwhile = ([out[n] for n in reducer.early.names], packed)
    (early_out, packed), grad_x = reducer.late_finish((done_meanwhile, grad_x))
    for n, res in zip(reducer.early.names, early_out):
        out[n] = res
    adamw_of(reducer.late.names, reducer.late.summed)
    unpacked = [_unpack_small(p, shapes) for p in packed]
    for n in SMALL_NAMES:
        out[n] = [u[0][n] for u in unpacked]
    loss_total = unpacked[0][1]

    return (loss_total, grad_x.reshape(x.shape),
            *[out[n][0] for n in WEIGHT_NAMES], *[out[n][1] for n in WEIGHT_NAMES],
            *[out[n][2] for n in WEIGHT_NAMES], *[out[n][3] for n in WEIGHT_NAMES])
```

```python
import math

import jax
import jax.numpy as jnp
from jax import lax
from jax.experimental import pallas as pl
from jax.experimental.pallas import tpu as pltpu
from jax.experimental.pallas import tpu_sc as plsc

F32 = jnp.float32
BF16 = jnp.bfloat16

D_MODEL = 1024
N_MEM = 256
SWA_HEADS = 8
SWA_KV_HEADS = 2
SWA_HEAD_DIM = 64
WINDOW = 128
FOX_HEADS = 8
FOX_HEAD_DIM = 64
MEM_HEADS = 4
MEM_HEAD_DIM = 128
D_FF = 4 * D_MODEL
REL_BUCKETS = 32
REL_MAX_DIST = 128
EPS = 1e-6
NEG = -1e30
GATE_W = 3 * D_MODEL
IN_WIDTH = 5896
N_SHARD = 4
IN_SHARD = IN_WIDTH // N_SHARD
IN_SHARD_PAD = 1536

ADAM_LR = 0.001
ADAM_B1 = 0.9
ADAM_B2 = 0.999
ADAM_EPS = 1e-08
ADAM_WD = 0.01
ADAM_STEP = 10

LANES = 128
V7X_VMEM_BYTES = 64 * 1024 * 1024
VMEM_SMALL = VMEM_MID = VMEM_BIG = V7X_VMEM_BYTES * 3 // 4
VMEM_MAX = V7X_VMEM_BYTES * 7 // 8

C_QA, C_QF, C_KF, C_VF, C_QM, C_KA, C_VA, C_FL, C_GL = 0, 512, 1024, 1536, 2048, 2560, 2688, 2816, 3072
LO_W = 3072
PROJ_W = 6144
PROJ_TN = 2048

NN = (((1,), (0,)), ((), ()))
NT = (((1,), (1,)), ((), ()))
TN = (((0,), (0,)), ((), ()))


def _dot(a, b, dims=NN):
    return lax.dot_general(a, b, dims, preferred_element_type=F32)


def _cparams(*sem, vmem=VMEM_SMALL):
    return pltpu.CompilerParams(dimension_semantics=sem, vmem_limit_bytes=vmem)


def _split3(a):
    hi = a.astype(BF16)
    r1 = a - hi.astype(F32)
    mid = r1.astype(BF16)
    lo = (r1 - mid.astype(F32)).astype(BF16)
    return hi, mid, lo


def _group_mean(a, g2):
    hi = a.astype(BF16)
    mid = (a - hi.astype(F32)).astype(BF16)
    return _dot(jnp.concatenate([hi, mid], axis=1), g2)


def _dot3_left(g, a):
    hi, mid, lo = _split3(a)
    return _dot(g, hi) + _dot(g, mid) + _dot(g, lo)


def _group_mean_matrix(d):
    r = jnp.arange(LANES)
    g = jnp.where((r[:, None] // d) == (r[None, :] // d), 1.0 / d, 0.0).astype(BF16)
    return jnp.concatenate([g, g], axis=0)


def _lane(shape):
    return lax.broadcasted_iota(jnp.int32, shape, len(shape) - 1)


def _matmul(name, a, b, *, dims, grid, a_spec, b_spec, acc_shape, outs, epilogue, extra=(), vmem=VMEM_BIG):
    nk = grid[2]
    n_extra = len(extra)

    def body(a_ref, b_ref, *rest):
        extra_refs = rest[:n_extra]
        out_refs = rest[n_extra:n_extra + len(outs)]
        i, j, k = pl.program_id(0), pl.program_id(1), pl.program_id(2)
        if nk == 1:
            epilogue(_dot(a_ref[...].astype(BF16), b_ref[...].astype(BF16), dims), extra_refs, out_refs, (i, j))
            return
        acc_ref = rest[-1]

        @pl.when(k == 0)
        def _():
            acc_ref[...] = jnp.zeros_like(acc_ref)

        acc_ref[...] += _dot(a_ref[...].astype(BF16), b_ref[...].astype(BF16), dims)

        @pl.when(k == nk - 1)
        def _():
            epilogue(acc_ref[...], extra_refs, out_refs, (i, j))

    res = pl.pallas_call(
        body,
        name=name,
        grid=grid,
        in_specs=[a_spec, b_spec] + [s for _, s in extra],
        out_specs=[s for _, s in outs],
        out_shape=[s for s, _ in outs],
        scratch_shapes=[pltpu.VMEM(acc_shape, F32)] if nk > 1 else [],
        compiler_params=_cparams("arbitrary", "arbitrary", "arbitrary", vmem=vmem),
    )(a, b, *[x for x, _ in extra])
    return res


def _epi_store(acc, extra_refs, out_refs, ij):
    out_refs[0][...] = acc.astype(out_refs[0].dtype)


def _rms_rows(x, g):
    r = lax.rsqrt(jnp.mean(x * x, axis=-1, keepdims=True) + EPS)
    return x * r, r


def _rmsnorm_bwd_rows(dh, x, g):
    xhat, r = _rms_rows(x, g)
    dxh = dh * g
    dx = r * (dxh - xhat * jnp.mean(dxh * xhat, axis=-1, keepdims=True))
    return dx, jnp.sum(dh * xhat, axis=0, keepdims=True)


def _rmsnorm(name, x, g, tb):
    T, Dm = x.shape

    def body(x_ref, g_ref, o_ref):
        xhat, _ = _rms_rows(x_ref[...], None)
        o_ref[...] = (xhat * g_ref[...]).astype(o_ref.dtype)

    return pl.pallas_call(
        body, name=name, grid=(T // tb,),
        in_specs=[pl.BlockSpec((tb, Dm), lambda i: (i, 0)), pl.BlockSpec((1, Dm), lambda i: (0, 0))],
        out_specs=pl.BlockSpec((tb, Dm), lambda i: (i, 0)),
        out_shape=jax.ShapeDtypeStruct((T, Dm), BF16),
        compiler_params=_cparams("parallel"),
    )(x, g)


def _head_norm(x, gm, gain):
    ms = _group_mean(x * x, gm)
    r = lax.rsqrt(ms + EPS)
    return x * r * gain, x * r


def _head_norm_bwd(dy, x, gm, gain):
    ms = _group_mean(x * x, gm)
    r = lax.rsqrt(ms + EPS)
    xhat = x * r
    dxh = dy * gain
    dx = r * (dxh - xhat * _group_mean(dxh * xhat, gm))
    return dx, jnp.sum(dy * xhat, axis=0, keepdims=True)


def _log_sigmoid(z):
    return jnp.minimum(z, 0.0) - jnp.log(1.0 + jnp.exp(-jnp.abs(z)))


def _prep_fwd(proj, gains, bfor, tril, gm64, gm128, T, tb):
    nb = T // tb

    def body(qa_ref, qf_ref, kf_ref, vf_ref, qm_ref, ka_ref, va_ref, fl_ref, gains_ref, bfor_ref, tril_ref,
             gm64_ref, gm128_ref,
             qa_o, qf_o, kf_o, vf_o, qm_o, kad_o, vad_o, qaug_o, kaug_o, carry):
        i = pl.program_id(0)
        gm64v = gm64_ref[...]
        gm128v = gm128_ref[...]
        lane = _lane((tb, LANES))

        def norm512(src, dst, row, gm, scale=1.0):
            gain = gains_ref[row:row + 1, :]
            for c in range(4):
                sl = slice(c * LANES, (c + 1) * LANES)
                y, _ = _head_norm(src[:, sl], gm, gain)
                dst[:, sl] = (y * scale).astype(dst.dtype)

        norm512(qa_ref, qa_o, 0, gm64v)
        norm512(qf_ref, qf_o, 2, gm64v, FOX_SCALE)
        norm512(kf_ref, kf_o, 3, gm64v)
        norm512(qm_ref, qm_o, 4, gm128v)
        vf_o[...] = vf_ref[...].astype(vf_o.dtype)

        ka_n, _ = _head_norm(ka_ref[...], gm64v, gains_ref[1:2, :])
        ka_r = pltpu.roll(ka_n, 64, 1)
        va = va_ref[...]
        va_r = pltpu.roll(va, 64, 1)
        lo = lane < 64
        kad_o[0] = jnp.where(lo, ka_n, ka_r).astype(kad_o.dtype)
        kad_o[1] = jnp.where(lo, ka_r, ka_n).astype(kad_o.dtype)
        vad_o[0] = jnp.where(lo, va, va_r).astype(vad_o.dtype)
        vad_o[1] = jnp.where(lo, va_r, va).astype(vad_o.dtype)

        @pl.when(i == 0)
        def _():
            carry[...] = jnp.zeros_like(carry)

        logf = jnp.where(lane < FOX_HEADS, _log_sigmoid(fl_ref[...] + bfor_ref[...]), 0.0)
        c = _dot3_left(tril_ref[...], logf) + carry[0:1, :]
        carry[...] = jnp.broadcast_to(c[tb - 1:tb, :], carry.shape)
        for pair in range(FOX_HEADS // 2):
            qaug = jnp.zeros((tb, LANES), F32)
            kaug = jnp.zeros((tb, LANES), F32)
            for sub in range(2):
                col = jnp.sum(jnp.where(lane == 2 * pair + sub, c, 0.0), axis=1, keepdims=True)
                pieces = [p.astype(F32) for p in _split3(col)]
                base = AUG_STRIDE * sub
                for e in range(3):
                    qaug = jnp.where(lane == base + AUG_C + e, pieces[e], qaug)
                    kaug = jnp.where(lane == base + AUG_NEG_C + e, -pieces[e], kaug)
                qaug = jnp.where((lane >= base + AUG_NEG_C) & (lane < base + AUG_NEG_C + 3), 1.0, qaug)
                ones_k = ((lane >= base + AUG_C) & (lane < base + AUG_C + 3)) | (
                    (lane >= base + AUG_STAT) & (lane < base + AUG_STAT + 3))
                kaug = jnp.where(ones_k, 1.0, kaug)
            sl = slice(pair * LANES, (pair + 1) * LANES)
            qaug_o[:, sl] = qaug.astype(BF16)
            kaug_o[:, sl] = kaug.astype(BF16)

    def seg(width, start):
        return pl.BlockSpec((tb, width), lambda i, s=start // width: (i, s))

    const = lambda shape: pl.BlockSpec(shape, lambda i: tuple(0 for _ in shape))
    rows512 = pl.BlockSpec((tb, 512), lambda i: (i, 0))
    outs = pl.pallas_call(
        body, name="prep_fwd", grid=(nb,),
        in_specs=[seg(512, C_QA), seg(512, C_QF), seg(512, C_KF), seg(512, C_VF), seg(512, C_QM),
                  seg(128, C_KA), seg(128, C_VA), seg(128, C_FL),
                  const((8, LANES)), const((1, LANES)), const((tb, tb)), const((2 * LANES, LANES)), const((2 * LANES, LANES))],
        out_specs=[rows512, rows512, rows512, rows512, rows512,
                   pl.BlockSpec((2, tb, LANES), lambda i: (0, i, 0)), pl.BlockSpec((2, tb, LANES), lambda i: (0, i, 0)),
                   rows512, rows512],
        out_shape=[jax.ShapeDtypeStruct((T, 512), BF16)] * 5
        + [jax.ShapeDtypeStruct((2, T, LANES), BF16)] * 2
        + [jax.ShapeDtypeStruct((T, 512), BF16)] * 2,
        scratch_shapes=[pltpu.VMEM((8, LANES), F32)],
        compiler_params=_cparams("arbitrary", vmem=VMEM_MID),
    )(proj, proj, proj, proj, proj, proj, proj, proj, gains, bfor, tril, gm64, gm128)
    return outs


def _prep_bwd(proj, dqa, dkad, dvad, dqf, dkf, dvf, dqm, dqf_aug, dkf_aug, gains, bfor, triu, gm64, gm128, T, tb):
    nb = T // tb

    def body(qa_ref, qf_ref, kf_ref, qm_ref, ka_ref, fl_ref,
             dqa_ref, dkad_ref, dvad_ref, dqf_ref, dkf_ref, dvf_ref, dqm_ref, dqfa_ref, dkfa_ref,
             gains_ref, bfor_ref, triu_ref, gm64_ref, gm128_ref,
             dlo_o, gacc_o, carry):
        i = pl.program_id(0)
        gm64v = gm64_ref[...]
        gm128v = gm128_ref[...]
        lane = _lane((tb, LANES))

        @pl.when(i == 0)
        def _():
            carry[...] = jnp.zeros_like(carry)
            gacc_o[...] = jnp.zeros_like(gacc_o)

        def norm512_bwd(dsrc, xsrc, col0, row, gm):
            gain = gains_ref[row:row + 1, :]
            gsum = jnp.zeros((1, LANES), F32)
            for c in range(4):
                sl = slice(c * LANES, (c + 1) * LANES)
                dx, dg = _head_norm_bwd(dsrc[:, sl], xsrc[:, sl], gm, gain)
                dlo_o[:, col0 + c * LANES:col0 + (c + 1) * LANES] = dx.astype(dlo_o.dtype)
                gsum = gsum + dg
            gacc_o[row:row + 1, :] += gsum

        norm512_bwd(dqa_ref, qa_ref, C_QA, 0, gm64v)
        norm512_bwd(dqf_ref, qf_ref, C_QF, 2, gm64v)
        norm512_bwd(dkf_ref, kf_ref, C_KF, 3, gm64v)
        norm512_bwd(dqm_ref, qm_ref, C_QM, 4, gm128v)
        dlo_o[:, C_VF:C_VF + 512] = dvf_ref[...].astype(dlo_o.dtype)

        lo = lane < 64

        def fold(ref):
            f0 = ref[0] + pltpu.roll(ref[0], 64, 1)
            f1 = ref[1] + pltpu.roll(ref[1], 64, 1)
            return jnp.where(lo, f0, f1)

        dka, dg = _head_norm_bwd(fold(dkad_ref), ka_ref[...], gm64v, gains_ref[1:2, :])
        gacc_o[1:2, :] += dg
        dlo_o[:, C_KA:C_KA + LANES] = dka.astype(dlo_o.dtype)
        dlo_o[:, C_VA:C_VA + LANES] = fold(dvad_ref).astype(dlo_o.dtype)

        dc = jnp.zeros((tb, LANES), F32)
        for pair in range(FOX_HEADS // 2):
            sl = slice(pair * LANES, (pair + 1) * LANES)
            rows_sum, cols_sum = dqfa_ref[:, sl], dkfa_ref[:, sl]
            for sub in range(2):
                diff = (jnp.where(lane == AUG_STRIDE * sub + AUG_C, rows_sum, 0.0)
                        - jnp.where(lane == AUG_STRIDE * sub + AUG_NEG_C, cols_sum, 0.0))
                dc = jnp.where(lane == 2 * pair + sub, jnp.sum(diff, axis=1, keepdims=True), dc)
        dlogf = _dot3_left(triu_ref[...], dc) + carry[0:1, :]
        carry[...] = jnp.broadcast_to(dlogf[0:1, :], carry.shape)
        z = fl_ref[...] + bfor_ref[...]
        dfl = jnp.where(lane < FOX_HEADS, dlogf / (1.0 + jnp.exp(z)), 0.0)
        gacc_o[5:6, :] += jnp.sum(dfl, axis=0, keepdims=True)
        dlo_o[:, C_FL:C_FL + LANES] = dfl.astype(dlo_o.dtype)
        dlo_o[:, C_FL + LANES:C_FL + 2 * LANES] = jnp.zeros((tb, LANES), dlo_o.dtype)

    rev = lambda i: nb - 1 - i

    def seg(width, start):
        return pl.BlockSpec((tb, width), lambda i, s=start // width: (rev(i), s))

    const = lambda shape: pl.BlockSpec(shape, lambda i: tuple(0 for _ in shape))
    rows512 = pl.BlockSpec((tb, 512), lambda i: (rev(i), 0))
    dup = pl.BlockSpec((2, tb, LANES), lambda i: (0, rev(i), 0))
    return pl.pallas_call(
        body, name="prep_bwd", grid=(nb,),
        in_specs=[seg(512, C_QA), seg(512, C_QF), seg(512, C_KF), seg(512, C_QM), seg(128, C_KA), seg(128, C_FL),
                  rows512, dup, dup, rows512, rows512, rows512, rows512, rows512, rows512,
                  const((8, LANES)), const((1, LANES)), const((tb, tb)), const((2 * LANES, LANES)), const((2 * LANES, LANES))],
        out_specs=[pl.BlockSpec((tb, LO_W), lambda i: (rev(i), 0)), const((8, LANES))],
        out_shape=[jax.ShapeDtypeStruct((T, LO_W), BF16), jax.ShapeDtypeStruct((8, LANES), F32)],
        scratch_shapes=[pltpu.VMEM((8, LANES), F32)],
        compiler_params=_cparams("arbitrary", vmem=VMEM_MID),
    )(proj, proj, proj, proj, proj, proj, dqa, dkad, dvad, dqf, dkf, dvf, dqm, dqf_aug, dkf_aug,
      gains, bfor, triu, gm64, gm128)


FOX_SCALE = FOX_HEAD_DIM ** -0.5
AUG_STRIDE = 16
AUG_C = 0
AUG_NEG_C = 3
AUG_STAT = 6
FOX_TQ, FOX_TK = 1024, 1024
FOX_BWD_TQ, FOX_BWD_TK = 1024, 1024
FOX_DIAGONAL_PARTS = 4


def _fox_head_mask(sub, rows):
    lane = _lane((rows, 2 * LANES))
    main = (lane >= 64 * sub) & (lane < 64 * sub + 64)
    aug = (lane >= LANES + AUG_STRIDE * sub) & (lane < LANES + AUG_STRIDE * (sub + 1))
    return main | aug


def _fox_pieces(diagonal, tq, tk):
    if diagonal and tq == tk and tq >= FOX_DIAGONAL_PARTS * LANES:
        step = tq // FOX_DIAGONAL_PARTS
        return [(n * step, (n + 1) * step, (n + 1) * step) for n in range(FOX_DIAGONAL_PARTS)]
    return [(0, tq, tk)]


def _fox_fwd(q, qaug, k, kaug, v, T, tq, tk):
    nq, nk = T // tq, T // tk
    rep = tk // LANES
    last_of = lambda i: (i * tq + tq - 1) // tk

    def body(q_ref, qa_ref, k_ref, ka_ref, v_ref, o_ref, qab_ref, m_s, acc_s):
        p_, i, j = pl.program_id(0), pl.program_id(1), pl.program_id(2)
        last = last_of(i)

        @pl.when(j == 0)
        def _():
            m_s[...] = jnp.full(m_s.shape, NEG, F32)
            acc_s[...] = jnp.zeros_like(acc_s)

        def step(diagonal):
            k2 = jnp.concatenate([k_ref[...], ka_ref[...]], axis=1)
            v2 = jnp.concatenate([v_ref[...], ka_ref[...]], axis=1)
            pieces = _fox_pieces(diagonal, tq, tk)
            work = []
            for r0, r1, nc in pieces:
                rows = slice(r0, r1)
                q2 = jnp.concatenate([q_ref[rows, :], qa_ref[rows, :]], axis=1)
                for sub in range(2):
                    qh = jnp.where(_fox_head_mask(sub, r1 - r0), q2, jnp.zeros_like(q2))
                    work.append((rows, r0, r1 - r0, nc, sub, _dot(qh, k2[:nc], NT)))
            for rows, r0, nr, nc, sub, s in work:
                if diagonal:
                    causal = (lax.broadcasted_iota(jnp.int32, (nr, nc), 1) + j * tk
                              <= lax.broadcasted_iota(jnp.int32, (nr, nc), 0) + (r0 + i * tq))
                    s = jnp.where(causal, s, NEG)
                m_prev = m_s[sub, rows, :]
                m_next = jnp.maximum(m_prev, jnp.max(s, axis=1, keepdims=True))
                p = jnp.exp(s - jnp.tile(m_next, (1, nc // LANES)))
                alpha = jnp.exp(m_prev - m_next)
                m_s[sub, rows, :] = m_next
                acc_s[sub, rows, :] = acc_s[sub, rows, :] * jnp.tile(alpha, (1, 2)) + _dot(p.astype(BF16), v2[:nc])

        @pl.when(j == last)
        def _():
            step(True)

        @pl.when(j < last)
        def _():
            step(False)

        @pl.when(j == nk - 1)
        def _():
            lane = _lane((tq, LANES))
            outs = []
            qab = qa_ref[...].astype(F32)
            for sub in range(2):
                acc = acc_s[sub]
                base = AUG_STRIDE * sub
                l = jnp.sum(jnp.where(lane == base + AUG_C, acc[:, LANES:], 0.0), axis=1, keepdims=True)
                outs.append(acc[:, :LANES] / l)
                lse = jnp.max(m_s[sub], axis=1, keepdims=True) + jnp.log(l)
                pieces = _split3(-lse)
                for e in range(3):
                    qab = jnp.where(lane == base + AUG_STAT + e, pieces[e].astype(F32), qab)
            o_ref[...] = jnp.where(lane < 64, outs[0], outs[1]).astype(o_ref.dtype)
            qab_ref[...] = qab.astype(BF16)

    qspec = pl.BlockSpec((tq, LANES), lambda p, i, j: (i, p))
    kspec = pl.BlockSpec((tk, LANES), lambda p, i, j: (jnp.minimum(j, last_of(i)), p))
    return pl.pallas_call(
        body, name="fox_fwd", grid=(4, nq, nk),
        in_specs=[qspec, qspec, kspec, kspec, kspec],
        out_specs=[qspec, qspec],
        out_shape=[jax.ShapeDtypeStruct((T, 512), BF16), jax.ShapeDtypeStruct((T, 512), BF16)],
        scratch_shapes=[pltpu.VMEM((2, tq, LANES), F32), pltpu.VMEM((2, tq, 2 * LANES), F32)],
        compiler_params=_cparams("parallel", "parallel", "arbitrary", vmem=VMEM_BIG),
    )(q, qaug, k, kaug, v)


def _fox_bwd(q, qaug, k, kaug, v, do, doaug, T, tq, tk):
    nq, nk = T // tq, T // tk
    first_of = lambda j: (j * tk) // tq

    def body(q_ref, qa_ref, k_ref, ka_ref, v_ref, do_ref, doa_ref,
             dq_ref, dqa_ref, dk_ref, dka_ref, dv_ref, dk_s, dv_s):
        p_, j, i = pl.program_id(0), pl.program_id(1), pl.program_id(2)
        masked = i * tq < (j + 1) * tk - 1

        @pl.when((j == 0) & (i == 0))
        def _():
            dq_ref[...] = jnp.zeros_like(dq_ref)
            dqa_ref[...] = jnp.zeros_like(dqa_ref)

        @pl.when(i == 0)
        def _():
            dk_s[...] = jnp.zeros_like(dk_s)
            dv_s[...] = jnp.zeros_like(dv_s)

        def step(diagonal):
            k2 = jnp.concatenate([k_ref[...], ka_ref[...]], axis=1)
            v2 = jnp.concatenate([v_ref[...], ka_ref[...]], axis=1)
            work = []
            for r0, r1, nc in _fox_pieces(diagonal, tq, tk):
                rows = slice(r0, r1)
                q2 = jnp.concatenate([q_ref[rows, :], qa_ref[rows, :]], axis=1)
                do2 = jnp.concatenate([do_ref[rows, :], doa_ref[rows, :]], axis=1)
                for sub in range(2):
                    hm = _fox_head_mask(sub, r1 - r0)
                    qh = jnp.where(hm, q2, jnp.zeros_like(q2))
                    doh = jnp.where(hm, do2, jnp.zeros_like(do2))
                    s = _dot(qh, k2[:nc], NT)
                    dp = _dot(doh, v2[:nc], NT)
                    work.append((r0, r1 - r0, nc, sub, qh, doh, s, dp))
            dqs = {}
            for r0, nr, nc, sub, qh, doh, s, dp in work:
                if diagonal:
                    causal = (lax.broadcasted_iota(jnp.int32, (nr, nc), 1) + j * tk
                              <= lax.broadcasted_iota(jnp.int32, (nr, nc), 0) + (r0 + i * tq))
                    s = jnp.where(causal, s, NEG)
                p = jnp.exp(s)
                dsb = (p * dp).astype(BF16)
                dv_s[0:nc, :] += _dot(p.astype(BF16), doh[:, :LANES], TN)
                dk_s[0:nc, :] += _dot(dsb, qh, TN)
                dqs[(r0, sub)] = _dot(dsb, k2[:nc])
            for r0, r1, nc in _fox_pieces(diagonal, tq, tk):
                dq2 = jnp.where(_fox_head_mask(0, r1 - r0), dqs[(r0, 0)], dqs[(r0, 1)])
                qrows = pl.ds(pl.multiple_of(i * tq + r0, r1 - r0), r1 - r0)
                dq_ref[qrows, :] += dq2[:, :LANES] * FOX_SCALE
                dqa_ref[qrows, :] += dq2[:, LANES:]

        @pl.when((i >= first_of(j)) & masked)
        def _():
            step(True)

        @pl.when((i >= first_of(j)) & jnp.logical_not(masked))
        def _():
            step(False)

        @pl.when(i == nq - 1)
        def _():
            dk_ref[...] = dk_s[:, :LANES]
            dka_ref[...] = dk_s[:, LANES:]
            dv_ref[...] = dv_s[...]

    qspec = pl.BlockSpec((tq, LANES), lambda p, j, i: (jnp.maximum(i, first_of(j)), p))
    kspec = pl.BlockSpec((tk, LANES), lambda p, j, i: (j, p))
    resident = pl.BlockSpec((T, LANES), lambda p, j, i: (0, p))
    return pl.pallas_call(
        body, name="fox_bwd", grid=(4, nk, nq),
        in_specs=[qspec, qspec, kspec, kspec, kspec, qspec, qspec],
        out_specs=[resident, resident, kspec, kspec, kspec],
        out_shape=[jax.ShapeDtypeStruct((T, 512), F32)] * 5,
        scratch_shapes=[pltpu.VMEM((tk, 2 * LANES), F32), pltpu.VMEM((tk, LANES), F32)],
        compiler_params=_cparams("arbitrary", "arbitrary", "arbitrary", vmem=VMEM_BIG),
    )(q, qaug, k, kaug, v, do, doaug)


SWA_SUB = 16
SWA_TB = SWA_SUB * WINDOW


def _t5_bucket_matrix():
    t = jnp.arange(WINDOW)[:, None] + WINDOW
    s = jnp.arange(2 * WINDOW)[None, :]
    max_exact = REL_BUCKETS // 2
    d = jnp.maximum(t - s, 0)
    df = jnp.maximum(d, 1).astype(F32)
    large = max_exact + (jnp.log(df / max_exact) / math.log(REL_MAX_DIST / max_exact)
                         * (REL_BUCKETS - max_exact)).astype(jnp.int32)
    large = jnp.minimum(large, REL_BUCKETS - 1)
    return jnp.where(d < max_exact, d, large).astype(jnp.int32)


def _swa_bias(rel_bias, bucket):
    def body(rel_ref, bucket_ref, o_ref):
        b = bucket_ref[...]
        for h in range(SWA_HEADS):
            acc = jnp.zeros(b.shape, F32)
            for r in range(REL_BUCKETS):
                acc = jnp.where(b == r, rel_ref[r, h], acc)
            o_ref[h] = acc

    return pl.pallas_call(
        body, name="swa_bias",
        in_specs=[pl.BlockSpec(memory_space=pltpu.SMEM), pl.BlockSpec(memory_space=pltpu.VMEM)],
        out_specs=pl.BlockSpec(memory_space=pltpu.VMEM),
        out_shape=jax.ShapeDtypeStruct((SWA_HEADS, WINDOW, 2 * WINDOW), F32),
    )(rel_bias, bucket)


def _swa_bias_bwd(dbias, bucket):
    def body(db_ref, bucket_ref, o_ref):
        b = bucket_ref[...]
        lane = _lane((1, LANES))
        for r in range(REL_BUCKETS):
            row = jnp.zeros((1, LANES), F32)
            for h in range(SWA_HEADS):
                part = jnp.sum(jnp.where(b == r, db_ref[h], 0.0), axis=0, keepdims=True)
                tot = jnp.sum(part, axis=1, keepdims=True)
                row = jnp.where(lane == h, tot, row)
            o_ref[r:r + 1, :] = row

    return pl.pallas_call(
        body, name="swa_bias_bwd",
        in_specs=[pl.BlockSpec(memory_space=pltpu.VMEM), pl.BlockSpec(memory_space=pltpu.VMEM)],
        out_specs=pl.BlockSpec(memory_space=pltpu.VMEM),
        out_shape=jax.ShapeDtypeStruct((REL_BUCKETS, LANES), F32),
    )(dbias, bucket)


SWA_GROUP = SWA_HEADS // SWA_KV_HEADS


def _swa_valid(r, i):
    t = (lax.broadcasted_iota(jnp.int32, (SWA_GROUP * WINDOW, 2 * WINDOW), 0) & (WINDOW - 1)) + WINDOW
    s = lax.broadcasted_iota(jnp.int32, (SWA_GROUP * WINDOW, 2 * WINDOW), 1)
    dist = t - s
    band = (dist >= 0) & (dist < WINDOW)
    if r == 0:
        band = band & ((s >= WINDOW) | (i > 0))
    return band


def _swa_stack(blk):
    lane = _lane((WINDOW, LANES))
    parts = []
    for g in range(SWA_GROUP):
        b = blk[:, LANES * (g // 2):LANES * (g // 2 + 1)]
        parts.append(jnp.where((lane >= 64) if g % 2 else (lane < 64), b, jnp.zeros_like(b)))
    return jnp.concatenate(parts, axis=0)


def _swa_unstack(st):
    lane = _lane((WINDOW, LANES))
    W = WINDOW
    return jnp.concatenate([jnp.where(lane < 64, st[2 * b * W:(2 * b + 1) * W], st[(2 * b + 1) * W:(2 * b + 2) * W])
                            for b in range(2)], axis=1)


def _swa_sink_column(sink_ref, kvh):
    row = lax.broadcasted_iota(jnp.int32, (SWA_GROUP * WINDOW, 1), 0)
    col = jnp.full((SWA_GROUP * WINDOW, 1), sink_ref[SWA_GROUP * kvh + SWA_GROUP - 1], F32)
    for g in range(SWA_GROUP - 2, -1, -1):
        col = jnp.where(row < (g + 1) * WINDOW, sink_ref[SWA_GROUP * kvh + g], col)
    return col


def _swa_specs(T):
    W = WINDOW
    qspec = pl.BlockSpec((SWA_TB, 2 * LANES), lambda h, i: (i, h))
    own = pl.BlockSpec((None, SWA_TB, LANES), lambda h, i: (h, i, 0))
    prev = pl.BlockSpec((None, W, LANES), lambda h, i: (h, jnp.maximum(SWA_SUB * i - 1, 0), 0))
    stat = pl.BlockSpec((SWA_GROUP, SWA_TB, LANES), lambda h, i: (h, i, 0))
    bias = pl.BlockSpec((None, SWA_GROUP * W, 2 * W), lambda h, i: (h, 0, 0))
    return qspec, own, prev, stat, bias


def _swa_fwd(sinks, q, kad, vad, bias, T):
    nb = T // SWA_TB
    scale = SWA_HEAD_DIM ** -0.5
    W = WINDOW

    def body(sink_ref, q_ref, k_ref, kp_ref, v_ref, vp_ref, bias_ref, o_ref, lse_ref):
        kvh, i = pl.program_id(0), pl.program_id(1)
        sink = _swa_sink_column(sink_ref, kvh)
        for r in range(SWA_SUB):
            rs = slice(r * W, (r + 1) * W)
            ps = slice((r - 1) * W, r * W)
            k_own, v_own = k_ref[rs, :], v_ref[rs, :]
            k_prev = kp_ref[...] if r == 0 else k_ref[ps, :]
            v_prev = vp_ref[...] if r == 0 else v_ref[ps, :]
            qs = _swa_stack(q_ref[rs, :])
            s = jnp.concatenate([_dot(qs, k_prev, NT), _dot(qs, k_own, NT)], axis=1) * scale + bias_ref[...]
            s = jnp.where(_swa_valid(r, i), s, NEG)
            m = jnp.maximum(jnp.max(s, axis=1, keepdims=True), sink)
            p = jnp.exp(s - m)
            denom = jnp.sum(p, axis=1, keepdims=True) + jnp.exp(sink - m)
            pn = (p / denom).astype(BF16)
            o_ref[rs, :] = _swa_unstack(_dot(pn[:, :W], v_prev) + _dot(pn[:, W:], v_own)).astype(o_ref.dtype)
            lse = m + jnp.log(denom)
            for g in range(SWA_GROUP):
                lse_ref[g, rs, :] = jnp.broadcast_to(lse[g * W:(g + 1) * W], (W, LANES))

    qspec, own, prev, stat, bspec = _swa_specs(T)
    return pl.pallas_call(
        body, name="swa_fwd", grid=(SWA_KV_HEADS, nb),
        in_specs=[pl.BlockSpec(memory_space=pltpu.SMEM), qspec, own, prev, own, prev, bspec],
        out_specs=[qspec, stat],
        out_shape=[jax.ShapeDtypeStruct((T, 512), BF16), jax.ShapeDtypeStruct((SWA_HEADS, T, LANES), F32)],
        compiler_params=_cparams("parallel", "parallel", vmem=VMEM_MID),
    )(sinks, q, kad, kad, vad, vad, bias.reshape(SWA_KV_HEADS, SWA_GROUP * W, 2 * W))


def _swa_bwd(sinks, q, kad, vad, bias, do, lse, delta, T):
    nb = T // SWA_TB
    scale = SWA_HEAD_DIM ** -0.5
    W = WINDOW

    def body(sink_ref, q_ref, k_ref, kp_ref, v_ref, vp_ref, bias_ref, do_ref, lse_ref, dl_ref,
             dq_ref, dkad_ref, dvad_ref, dbias_ref, dsk_ref):
        kvh, i = pl.program_id(0), pl.program_id(1)
        sink = _swa_sink_column(sink_ref, kvh)

        @pl.when((kvh == 0) & (i == 0))
        def _():
            dkad_ref[...] = jnp.zeros_like(dkad_ref)
            dvad_ref[...] = jnp.zeros_like(dvad_ref)

        @pl.when(i == 0)
        def _():
            dbias_ref[...] = jnp.zeros_like(dbias_ref)
            dsk_ref[...] = jnp.zeros_like(dsk_ref)

        for r in range(SWA_SUB):
            rs = slice(r * W, (r + 1) * W)
            ps = slice((r - 1) * W, r * W)
            k_own, v_own = k_ref[rs, :], v_ref[rs, :]
            k_prev = kp_ref[...] if r == 0 else k_ref[ps, :]
            v_prev = vp_ref[...] if r == 0 else v_ref[ps, :]
            qs = _swa_stack(q_ref[rs, :])
            dos = _swa_stack(do_ref[rs, :])
            lse_b = jnp.concatenate([lse_ref[g, rs, :] for g in range(SWA_GROUP)], axis=0)
            dl_b = jnp.concatenate([dl_ref[g, rs, :] for g in range(SWA_GROUP)], axis=0)
            s = jnp.concatenate([_dot(qs, k_prev, NT), _dot(qs, k_own, NT)], axis=1) * scale + bias_ref[...]
            s = jnp.where(_swa_valid(r, i), s, NEG)
            p = jnp.exp(s - jnp.tile(lse_b, (1, 2)))
            dp = jnp.concatenate([_dot(dos, v_prev, NT), _dot(dos, v_own, NT)], axis=1)
            ds = p * (dp - jnp.tile(dl_b, (1, 2)))
            sink_term = jnp.exp(sink - lse_b) * dl_b
            for g in range(SWA_GROUP):
                dbias_ref[g] += ds[g * W:(g + 1) * W]
                dsk_ref[g:g + 1, :] += jnp.sum(sink_term[g * W:(g + 1) * W], axis=0, keepdims=True)
            dsb = ds.astype(BF16)
            pb = p.astype(BF16)
            dq_ref[rs, :] = _swa_unstack((_dot(dsb[:, :W], k_prev) + _dot(dsb[:, W:], k_own)) * scale)
            own_row = pl.multiple_of(i * SWA_TB + r * W, W)
            dkad_ref[kvh, pl.ds(own_row, W), :] += _dot(dsb[:, W:], qs, TN) * scale
            dvad_ref[kvh, pl.ds(own_row, W), :] += _dot(pb[:, W:], dos, TN)
            dk_prev = _dot(dsb[:, :W], qs, TN) * scale
            dv_prev = _dot(pb[:, :W], dos, TN)
            if r == 0:
                @pl.when(i > 0)
                def _():
                    prev_row = pl.multiple_of(i * SWA_TB - W, W)
                    dkad_ref[kvh, pl.ds(prev_row, W), :] += dk_prev
                    dvad_ref[kvh, pl.ds(prev_row, W), :] += dv_prev
            else:
                prev_row = pl.multiple_of(i * SWA_TB + (r - 1) * W, W)
                dkad_ref[kvh, pl.ds(prev_row, W), :] += dk_prev
                dvad_ref[kvh, pl.ds(prev_row, W), :] += dv_prev

    qspec, own, prev, stat, bspec = _swa_specs(T)
    full = pl.BlockSpec((SWA_KV_HEADS, T, LANES), lambda h, i: (0, 0, 0))
    return pl.pallas_call(
        body, name="swa_bwd", grid=(SWA_KV_HEADS, nb),
        in_specs=[pl.BlockSpec(memory_space=pltpu.SMEM), qspec, own, prev, own, prev, bspec, qspec, stat, stat],
        out_specs=[qspec, full, full, pl.BlockSpec((SWA_GROUP, W, 2 * W), lambda h, i: (h, 0, 0)),
                   pl.BlockSpec((None, 8, LANES), lambda h, i: (h, 0, 0))],
        out_shape=[jax.ShapeDtypeStruct((T, 512), F32), jax.ShapeDtypeStruct((SWA_KV_HEADS, T, LANES), F32),
                   jax.ShapeDtypeStruct((SWA_KV_HEADS, T, LANES), F32), jax.ShapeDtypeStruct((SWA_HEADS, W, 2 * W), F32),
                   jax.ShapeDtypeStruct((SWA_KV_HEADS, 8, LANES), F32)],
        compiler_params=_cparams("arbitrary", "arbitrary", vmem=VMEM_MID),
    )(sinks, q, kad, kad, vad, vad, bias.reshape(SWA_KV_HEADS, SWA_GROUP * W, 2 * W), do, lse, delta)


MEM_TQ = 4096


def _mem_fwd(q, mk, mv, T, tq):
    scale = MEM_HEAD_DIM ** -0.5

    def body(q_ref, k_ref, v_ref, o_ref, lse_ref):
        s = _dot(q_ref[...], k_ref[...], NT) * scale
        m = jnp.max(s, axis=1, keepdims=True)
        p = jnp.exp(s - m)
        l = jnp.sum(p, axis=1, keepdims=True)
        o_ref[...] = _dot((p / l).astype(BF16), v_ref[...]).astype(o_ref.dtype)
        lse_ref[...] = jnp.broadcast_to(m + jnp.log(l), (tq, LANES))

    qspec = pl.BlockSpec((tq, LANES), lambda h, i: (i, h))
    kspec = pl.BlockSpec((N_MEM, LANES), lambda h, i: (0, h))
    return pl.pallas_call(
        body, name="mem_fwd", grid=(MEM_HEADS, T // tq),
        in_specs=[qspec, kspec, kspec],
        out_specs=[qspec, pl.BlockSpec((None, tq, LANES), lambda h, i: (h, i, 0))],
        out_shape=[jax.ShapeDtypeStruct((T, 512), BF16), jax.ShapeDtypeStruct((MEM_HEADS, T, LANES), F32)],
        compiler_params=_cparams("parallel", "parallel"),
    )(q, mk, mv)


def _mem_bwd(q, mk, mv, do, lse, delta, T, tq):
    scale = MEM_HEAD_DIM ** -0.5
    rep = N_MEM // LANES

    def body(q_ref, k_ref, v_ref, do_ref, lse_ref, dl_ref, dq_ref, dk_ref, dv_ref):
        i = pl.program_id(1)

        @pl.when(i == 0)
        def _():
            dk_ref[...] = jnp.zeros_like(dk_ref)
            dv_ref[...] = jnp.zeros_like(dv_ref)

        qv, dov = q_ref[...], do_ref[...]
        s = _dot(qv, k_ref[...], NT) * scale
        p = jnp.exp(s - jnp.tile(lse_ref[...], (1, rep)))
        dp = _dot(dov, v_ref[...], NT)
        ds = p * (dp - jnp.tile(dl_ref[...], (1, rep)))
        dsb = ds.astype(BF16)
        dq_ref[...] = _dot(dsb, k_ref[...]) * scale
        dk_ref[...] += _dot(dsb, qv, TN) * scale
        dv_ref[...] += _dot(p.astype(BF16), dov, TN)

    qspec = pl.BlockSpec((tq, LANES), lambda h, i: (i, h))
    kspec = pl.BlockSpec((N_MEM, LANES), lambda h, i: (0, h))
    stat = pl.BlockSpec((None, tq, LANES), lambda h, i: (h, i, 0))
    return pl.pallas_call(
        body, name="mem_bwd", grid=(MEM_HEADS, T // tq),
        in_specs=[qspec, kspec, kspec, qspec, stat, stat],
        out_specs=[qspec, kspec, kspec],
        out_shape=[jax.ShapeDtypeStruct((T, 512), F32), jax.ShapeDtypeStruct((N_MEM, 512), F32),
                   jax.ShapeDtypeStruct((N_MEM, 512), F32)],
        compiler_params=_cparams("arbitrary", "arbitrary"),
    )(q, mk, mv, do, lse, delta)


def _mem_prep_fwd(mem, g_mem, w_kv, kn_gain, gm128):
    def body(mem_ref, g_ref, w_ref, kn_ref, gm_ref, memn_o, kv_o, mk_o, mv_o):
        xhat, _ = _rms_rows(mem_ref[...], None)
        memn = (xhat * g_ref[...]).astype(BF16)
        memn_o[...] = memn
        kv = _dot(memn, w_ref[...])
        kv_o[...] = kv
        gm = gm_ref[...]
        for c in range(4):
            sl = slice(c * LANES, (c + 1) * LANES)
            y, _ = _head_norm(kv[:, sl], gm, kn_ref[...])
            mk_o[:, sl] = y.astype(BF16)
        mv_o[...] = kv[:, 512:].astype(BF16)

    vm = pl.BlockSpec(memory_space=pltpu.VMEM)
    return pl.pallas_call(
        body, name="mem_prep_fwd", in_specs=[vm] * 5, out_specs=[vm] * 4,
        out_shape=[jax.ShapeDtypeStruct((N_MEM, D_MODEL), BF16), jax.ShapeDtypeStruct((N_MEM, D_MODEL), F32),
                   jax.ShapeDtypeStruct((N_MEM, 512), BF16), jax.ShapeDtypeStruct((N_MEM, 512), BF16)],
        compiler_params=pltpu.CompilerParams(vmem_limit_bytes=VMEM_MID),
    )(mem, g_mem, w_kv, kn_gain, gm128)


def _mem_prep_bwd(mem, g_mem, memn, kv, w_kv, kn_gain, gm128, dmk, dmv):
    def body(mem_ref, g_ref, memn_ref, kv_ref, w_ref, kn_ref, gm_ref, dmk_ref, dmv_ref, dw_o, dg_o, dkn_o, dkv_s):
        gm = gm_ref[...]
        dkn = jnp.zeros((1, LANES), F32)
        for c in range(4):
            sl = slice(c * LANES, (c + 1) * LANES)
            dx, dg = _head_norm_bwd(dmk_ref[:, sl], kv_ref[:, sl], gm, kn_ref[...])
            dkv_s[:, sl] = dx.astype(BF16)
            dkn = dkn + dg
        dkn_o[...] = dkn
        dkv_s[:, 512:] = dmv_ref[...].astype(BF16)
        dkv = dkv_s[...]
        dw_o[...] = _dot(memn_ref[...], dkv, TN)
        dmemn = _dot(dkv, w_ref[...], NT)
        xhat, _ = _rms_rows(mem_ref[...], None)
        dg_o[...] = jnp.sum(dmemn * xhat, axis=0, keepdims=True)

    vm = pl.BlockSpec(memory_space=pltpu.VMEM)
    return pl.pallas_call(
        body, name="mem_prep_bwd", in_specs=[vm] * 9, out_specs=[vm] * 3,
        out_shape=[jax.ShapeDtypeStruct((D_MODEL, D_MODEL), F32), jax.ShapeDtypeStruct((1, D_MODEL), F32),
                   jax.ShapeDtypeStruct((1, LANES), F32)],
        scratch_shapes=[pltpu.VMEM((N_MEM, D_MODEL), BF16)],
        compiler_params=pltpu.CompilerParams(vmem_limit_bytes=VMEM_MID),
    )(mem, g_mem, memn, kv, w_kv, kn_gain, gm128, dmk, dmv)


SLOT_O = D_MODEL // N_SHARD


def _merge_fwd(proj, b_gate, o3, w3, T, tb):
    def body(gl_ref, bg_ref, oa_ref, of_ref, om_ref, wa_ref, wf_ref, wm_ref, out_ref):
        o_refs = (oa_ref, of_ref, om_ref)
        w_refs = (wa_ref, wf_ref, wm_ref)
        for n in range(N_SHARD):
            acc = jnp.zeros((tb, SLOT_O), F32)
            for b in range(3):
                c0 = b * D_MODEL + n * SLOT_O
                g = jax.nn.sigmoid(gl_ref[:, c0:c0 + SLOT_O] + bg_ref[:, c0:c0 + SLOT_O])
                acc = acc + g * _dot(o_refs[b][...], w_refs[b][n])
            out_ref[:, n * SLOT_O:(n + 1) * SLOT_O] = acc.astype(out_ref.dtype)

    rows = pl.BlockSpec((tb, 512), lambda i: (i, 0))
    wspec = pl.BlockSpec((N_SHARD, 512, SLOT_O), lambda i: (0, 0, 0))
    return pl.pallas_call(
        body, name="merge_fwd", grid=(T // tb,),
        in_specs=[pl.BlockSpec((tb, GATE_W), lambda i: (i, 1)), pl.BlockSpec((1, GATE_W), lambda i: (0, 0)),
                  rows, rows, rows, wspec, wspec, wspec],
        out_specs=pl.BlockSpec((tb, D_MODEL), lambda i: (i, 0)),
        out_shape=jax.ShapeDtypeStruct((T, D_MODEL), BF16),
        compiler_params=_cparams("parallel", vmem=VMEM_BIG),
    )(proj, b_gate, *o3, *w3)


def _merge_bwd(proj, b_gate, o3, w3, dmerged, T, tb):
    heads = (SWA_HEADS, FOX_HEADS, MEM_HEADS)

    def body(gl_ref, bg_ref, oa_ref, of_ref, om_ref, wa_ref, wf_ref, wm_ref, dm_ref,
             dgl_o, doa_o, dof_o, dom_o, dla_o, dlf_o, dlm_o, dwa_o, dwf_o, dwm_o, dbg_o):
        i = pl.program_id(0)
        o_refs = (oa_ref, of_ref, om_ref)
        w_refs = (wa_ref, wf_ref, wm_ref)
        do_refs = (doa_o, dof_o, dom_o)
        dl_refs = (dla_o, dlf_o, dlm_o)
        dw_refs = (dwa_o, dwf_o, dwm_o)

        @pl.when(i == 0)
        def _():
            for r in dw_refs:
                r[...] = jnp.zeros_like(r)
            dbg_o[...] = jnp.zeros_like(dbg_o)

        lane = _lane((tb, LANES))
        for b in range(3):
            ob = o_refs[b][...]
            do = jnp.zeros((tb, 512), F32)
            for n in range(N_SHARD):
                c0 = b * D_MODEL + n * SLOT_O
                g = jax.nn.sigmoid(gl_ref[:, c0:c0 + SLOT_O] + bg_ref[:, c0:c0 + SLOT_O])
                dm = dm_ref[:, n * SLOT_O:(n + 1) * SLOT_O]
                y = _dot(ob, w_refs[b][n])
                dgl = dm * y * g * (1.0 - g)
                dgl_o[:, c0:c0 + SLOT_O] = dgl.astype(dgl_o.dtype)
                dbg_o[:, c0:c0 + SLOT_O] += jnp.sum(dgl, axis=0, keepdims=True)
                dy = (dm * g).astype(BF16)
                do = do + _dot(dy, w_refs[b][n], NT)
                dw_refs[b][n] += _dot(ob, dy, TN)
            do_refs[b][...] = do.astype(BF16)
            prod = do * ob.astype(F32)
            for c in range(4):
                blk = prod[:, c * LANES:(c + 1) * LANES]
                if heads[b] == 8:
                    lo = jnp.sum(jnp.where(lane < 64, blk, 0.0), axis=1, keepdims=True)
                    hi = jnp.sum(jnp.where(lane >= 64, blk, 0.0), axis=1, keepdims=True)
                    if b == 1:
                        aug = jnp.zeros((tb, LANES), F32)
                        for sub, dl in enumerate((lo, hi)):
                            for e, piece in enumerate(_split3(-dl)):
                                aug = jnp.where(lane == AUG_STRIDE * sub + AUG_C + e, piece.astype(F32), aug)
                        dl_refs[b][:, c * LANES:(c + 1) * LANES] = aug.astype(BF16)
                    else:
                        dl_refs[b][2 * c] = jnp.broadcast_to(lo, (tb, LANES))
                        dl_refs[b][2 * c + 1] = jnp.broadcast_to(hi, (tb, LANES))
                else:
                    dl_refs[b][c] = jnp.broadcast_to(jnp.sum(blk, axis=1, keepdims=True), (tb, LANES))

    rows = pl.BlockSpec((tb, 512), lambda i: (i, 0))
    wspec = pl.BlockSpec((N_SHARD, 512, SLOT_O), lambda i: (0, 0, 0))
    stat = lambda h: pl.BlockSpec((h, tb, LANES), lambda i: (0, i, 0))
    return pl.pallas_call(
        body, name="merge_bwd", grid=(T // tb,),
        in_specs=[pl.BlockSpec((tb, GATE_W), lambda i: (i, 1)), pl.BlockSpec((1, GATE_W), lambda i: (0, 0)),
                  rows, rows, rows, wspec, wspec, wspec, pl.BlockSpec((tb, D_MODEL), lambda i: (i, 0))],
        out_specs=[pl.BlockSpec((tb, GATE_W), lambda i: (i, 0)), rows, rows, rows,
                   stat(8), rows, stat(4), wspec, wspec, wspec, pl.BlockSpec((1, GATE_W), lambda i: (0, 0))],
        out_shape=[jax.ShapeDtypeStruct((T, GATE_W), BF16)] + [jax.ShapeDtypeStruct((T, 512), BF16)] * 3
        + [jax.ShapeDtypeStruct((8, T, LANES), F32), jax.ShapeDtypeStruct((T, 512), BF16),
           jax.ShapeDtypeStruct((4, T, LANES), F32)]
        + [jax.ShapeDtypeStruct((N_SHARD, 512, SLOT_O), F32)] * 3 + [jax.ShapeDtypeStruct((1, GATE_W), F32)],
        compiler_params=_cparams("arbitrary", vmem=VMEM_BIG),
    )(proj, b_gate, *o3, *w3, dmerged)


def _local_step(x, h, mem, tgt, small, g_in, w_kv, w_o3, w_out, w_up, w_down, reducer):
    T = x.shape[0]
    tm = min(512, T)
    tile2 = lambda v: jnp.tile(v.reshape(1, -1), (1, LANES // v.size))
    gains = jnp.concatenate([tile2(small["qn_swa"]), tile2(small["kn_swa"]), tile2(small["qn_fox"]),
                             tile2(small["kn_fox"]), tile2(small["qn_mem"]), jnp.zeros((3, LANES), F32)], axis=0)
    kn_mem = small["kn_mem"].reshape(1, LANES)
    bfor = jnp.pad(small["b_forget"].reshape(1, -1), ((0, 0), (0, LANES - FOX_HEADS)))
    gm64 = _group_mean_matrix(64)
    gm128 = _group_mean_matrix(128)
    tb_prep = min(512, T)
    ones = jnp.ones((tb_prep, tb_prep), F32)
    tril = jnp.tril(ones).astype(BF16)
    triu = jnp.triu(ones).astype(BF16)
    bucket = _t5_bucket_matrix()
    g_mix, g_mlp, g_mem = small["g_mix"], small["g_mlp"], small["g_mem"]
    b_gate = small["b_gate"]
    sinks = small["sink_swa"].reshape(-1)

    tl = min(1024, T)
    sq = pl.BlockSpec((tl, D_MODEL), lambda i, j, k: (i, j))
    wc = _w_in_to_segments(g_in)
    (proj,) = _matmul(
        "mm_proj", h, wc, dims=NN, grid=(T // tl, PROJ_W // PROJ_TN, 1),
        a_spec=pl.BlockSpec((tl, D_MODEL), lambda i, j, k: (i, 0)),
        b_spec=pl.BlockSpec((D_MODEL, PROJ_TN), lambda i, j, k: (0, j)),
        acc_shape=(tl, PROJ_TN),
        outs=[(jax.ShapeDtypeStruct((T, PROJ_W), F32), pl.BlockSpec((tl, PROJ_TN), lambda i, j, k: (i, j)))],
        epilogue=_epi_store)
    qa, qf, kf, vf, qm, kad, vad, qf_aug, kf_aug = _prep_fwd(proj, gains, bfor, tril, gm64, gm128, T, tb_prep)
    bias = _swa_bias(small["rel_bias"], bucket)
    o_swa, lse_swa = _swa_fwd(sinks, qa, kad, vad, bias, T)
    o_fox, qf_aug_bwd = _fox_fwd(qf, qf_aug, kf, kf_aug, vf, T, min(FOX_TQ, T), min(FOX_TK, T))
    memn, kv, mk, mv = _mem_prep_fwd(mem, g_mem, w_kv, kn_mem, gm128)
    o_mem, lse_mem = _mem_fwd(qm, mk, mv, T, min(MEM_TQ, T))
    o3 = (o_swa, o_fox, o_mem)
    merged = _merge_fwd(proj, b_gate, o3, w_o3, T, min(512, T))

    def epi_residual(acc, extra_refs, out_refs, ij):
        out_refs[0][...] = extra_refs[0][...] + acc

    row_full = pl.BlockSpec((tm, D_MODEL), lambda i, j, k: (i, 0))
    row_big = pl.BlockSpec((tl, D_MODEL), lambda i, j, k: (i, 0))
    whole = pl.BlockSpec((D_MODEL, D_MODEL), lambda i, j, k: (0, 0))
    (x2,) = _matmul(
        "mm_out", merged, w_out, dims=NN, grid=(T // tl, 1, 1),
        a_spec=row_big, b_spec=whole,
        acc_shape=(tl, D_MODEL), extra=[(x, row_big)],
        outs=[(jax.ShapeDtypeStruct((T, D_MODEL), F32), row_big)], epilogue=epi_residual)
    hm = _rmsnorm("rms_mlp", x2, g_mlp, tm)

    def epi_relu2(acc, extra_refs, out_refs, ij):
        out_refs[0][...] = acc.astype(BF16)
        r = jnp.maximum(acc, 0.0)
        out_refs[1][...] = (r * r).astype(BF16)

    up, u = _matmul(
        "mm_up", hm, w_up, dims=NN, grid=(T // tl, N_SHARD, 1),
        a_spec=row_big, b_spec=pl.BlockSpec((None, D_MODEL, D_MODEL), lambda i, j, k: (j, 0, 0)),
        acc_shape=(tl, D_MODEL),
        outs=[(jax.ShapeDtypeStruct((T, D_FF), BF16), sq), (jax.ShapeDtypeStruct((T, D_FF), BF16), sq)],
        epilogue=epi_relu2)

    def epi_loss(acc, extra_refs, out_refs, ij):
        y = extra_refs[0][...] + acc
        err = y - extra_refs[1][...]
        dyv = err * (1.0 / D_MODEL)
        out_refs[0][...] = dyv
        out_refs[2][...] = dyv.astype(BF16)
        sq = jnp.sum(jnp.sum(err * err, axis=1, keepdims=True), axis=0, keepdims=True)

        @pl.when(ij[0] == 0)
        def _():
            out_refs[1][...] = jnp.zeros_like(out_refs[1])

        out_refs[1][...] += jnp.broadcast_to(sq, out_refs[1].shape)

    kblk = pl.BlockSpec((tl, D_MODEL), lambda i, j, k: (i, k))
    dy, loss_acc, dy_bf = _matmul(
        "mm_down", u, w_down, dims=NN, grid=(T // tl, 1, N_SHARD),
        a_spec=kblk, b_spec=pl.BlockSpec((D_MODEL, D_MODEL), lambda i, j, k: (k, 0)),
        acc_shape=(tl, D_MODEL), extra=[(x2, row_big), (tgt, row_big)],
        outs=[(jax.ShapeDtypeStruct((T, D_MODEL), F32), row_big),
              (jax.ShapeDtypeStruct((8, LANES), F32), pl.BlockSpec((8, LANES), lambda i, j, k: (0, 0))),
              (jax.ShapeDtypeStruct((T, D_MODEL), BF16), row_big)],
        epilogue=epi_loss)
    loss = loss_acc[0, 0] * (0.5 / D_MODEL)

    def epi_dup(acc, extra_refs, out_refs, ij):
        out_refs[0][...] = (acc * (2.0 * jnp.maximum(extra_refs[0][...].astype(F32), 0.0))).astype(BF16)

    wide = pl.BlockSpec((tl, PROJ_TN), lambda i, j, k: (i, j))
    (dup,) = _matmul(
        "mm_dup", dy_bf, w_down, dims=NT, grid=(T // tl, D_FF // PROJ_TN, 1),
        a_spec=row_big, b_spec=pl.BlockSpec((PROJ_TN, D_MODEL), lambda i, j, k: (j, 0)),
        acc_shape=(tl, PROJ_TN), extra=[(up, wide)],
        outs=[(jax.ShapeDtypeStruct((T, D_FF), BF16), wide)], epilogue=epi_dup)

    nkt = T // tl
    t_rows = pl.BlockSpec((tl, D_MODEL), lambda i, j, k: (k, i))
    t_cols = pl.BlockSpec((tl, D_MODEL), lambda i, j, k: (k, j))
    (d_w_down,) = _matmul(
        "mm_dw_down", u, dy_bf, dims=TN, grid=(N_SHARD, 1, nkt),
        a_spec=t_rows, b_spec=t_cols, acc_shape=(D_MODEL, D_MODEL),
        outs=[(jax.ShapeDtypeStruct((D_FF, D_MODEL), F32), pl.BlockSpec((D_MODEL, D_MODEL), lambda i, j, k: (i, 0)))],
        epilogue=_epi_store)
    (d_w_up,) = _matmul(
        "mm_dw_up", hm, dup, dims=TN, grid=(1, N_SHARD, nkt),
        a_spec=t_rows, b_spec=t_cols, acc_shape=(D_MODEL, D_MODEL),
        outs=[(jax.ShapeDtypeStruct((N_SHARD, D_MODEL, D_MODEL), F32),
               pl.BlockSpec((None, D_MODEL, D_MODEL), lambda i, j, k: (j, 0, 0)))],
        epilogue=_epi_store)

    def epi_rms_bwd(acc, extra_refs, out_refs, ij):
        dx, dg = _rmsnorm_bwd_rows(acc, extra_refs[0][...], extra_refs[1][...])
        out_refs[0][...] = dx + extra_refs[2][...]

        @pl.when(ij[0] == 0)
        def _():
            out_refs[1][...] = jnp.zeros_like(out_refs[1])

        out_refs[1][...] += dg

    gain_spec = pl.BlockSpec((1, D_MODEL), lambda i, j, k: (0, 0))
    dx2, d_g_mlp = _matmul(
        "mm_dhm", dup, w_up, dims=NT, grid=(T // tl, 1, N_SHARD),
        a_spec=kblk, b_spec=pl.BlockSpec((None, D_MODEL, D_MODEL), lambda i, j, k: (k, 0, 0)),
        acc_shape=(tl, D_MODEL), extra=[(x2, row_big), (g_mlp, gain_spec), (dy, row_big)],
        outs=[(jax.ShapeDtypeStruct((T, D_MODEL), F32), row_big), (jax.ShapeDtypeStruct((1, D_MODEL), F32), gain_spec)],
        epilogue=epi_rms_bwd)

    (dmerged,) = _matmul(
        "mm_dmerged", dx2, w_out, dims=NT, grid=(T // tl, 1, 1),
        a_spec=row_big, b_spec=whole,
        acc_shape=(tl, D_MODEL), outs=[(jax.ShapeDtypeStruct((T, D_MODEL), F32), row_big)], epilogue=_epi_store)
    (d_w_out,) = _matmul(
        "mm_dw_out", merged, dx2, dims=TN, grid=(1, 1, nkt),
        a_spec=t_rows, b_spec=t_cols, acc_shape=(D_MODEL, D_MODEL),
        outs=[(jax.ShapeDtypeStruct((D_MODEL, D_MODEL), F32), whole)],
        epilogue=_epi_store)
    (dgl, do_swa, do_fox, do_mem, dl_swa, do_fox_aug, dl_mem, d_wo_swa, d_wo_fox, d_wo_mem, d_b_gate) = _merge_bwd(
        proj, b_gate, o3, w_o3, dmerged, T, min(512, T))

    dqm, dmk, dmv = _mem_bwd(qm, mk, mv, do_mem, lse_mem, dl_mem, T, min(MEM_TQ, T))
    d_w_kv, d_g_mem, d_kn_mem = _mem_prep_bwd(mem, g_mem, memn, kv, w_kv, kn_mem, gm128, dmk, dmv)
    do_swa = reducer.early_start({"w_mlp_down": d_w_down, "w_mlp_up": d_w_up, "w_out": d_w_out, "w_mem_kv": d_w_kv,
                                  "w_o_swa": d_wo_swa, "w_o_fox": d_wo_fox, "w_o_mem": d_wo_mem}, do_swa)
    dqa, dkad, dvad, dbias, dsk = _swa_bwd(sinks, qa, kad, vad, bias, do_swa, lse_swa, dl_swa, T)
    dqa, do_fox = reducer.early_send((dqa, do_fox))
    dqf, dqf_aug, dkf, dkf_aug, dvf = _fox_bwd(qf, qf_aug_bwd, kf, kf_aug, vf, do_fox, do_fox_aug, T,
                                               min(FOX_BWD_TQ, T), min(FOX_BWD_TK, T))
    dvf = reducer.early_finish(dvf)
    d_rel = _swa_bias_bwd(dbias, bucket)
    dlo, gacc = _prep_bwd(proj, dqa, dkad, dvad, dqf, dkf, dvf, dqm, dqf_aug, dkf_aug, gains, bfor, triu, gm64, gm128,
                          T, tb_prep)

    def dwc_half(name, dpart):
        (res,) = _matmul(
            name, h, dpart, dims=TN, grid=(1, LO_W // D_MODEL, nkt),
            a_spec=t_rows, b_spec=t_cols, acc_shape=(D_MODEL, D_MODEL),
            outs=[(jax.ShapeDtypeStruct((D_MODEL, LO_W), F32), pl.BlockSpec((D_MODEL, D_MODEL), lambda i, j, k: (0, j)))],
            epilogue=_epi_store)
        return res

    d_wc_lo = dwc_half("mm_dwc_lo", dlo)
    d_wc_gl = dwc_half("mm_dwc_gl", dgl)
    dlo = reducer.late_start({"wc_lo": d_wc_lo, "wc_gl": d_wc_gl}, dlo)
    (dh_lo,) = _matmul(
        "mm_dh_lo", dlo, wc, dims=NT, grid=(T // tl, 1, LO_W // D_MODEL),
        a_spec=kblk, b_spec=pl.BlockSpec((D_MODEL, D_MODEL), lambda i, j, k: (0, k)),
        acc_shape=(tl, D_MODEL), outs=[(jax.ShapeDtypeStruct((T, D_MODEL), F32), row_big)], epilogue=_epi_store)
    dh_lo = reducer.late_send(dh_lo)

    def epi_dx(acc, extra_refs, out_refs, ij):
        dhh = acc + extra_refs[3][...]
        dx, dg = _rmsnorm_bwd_rows(dhh, extra_refs[0][...], extra_refs[1][...])
        out_refs[0][...] = dx + extra_refs[2][...]

        @pl.when(ij[0] == 0)
        def _():
            out_refs[1][...] = jnp.zeros_like(out_refs[1])

        out_refs[1][...] += dg

    grad_x, d_g_mix = _matmul(
        "mm_dh_gl", dgl, wc, dims=NT, grid=(T // tl, 1, GATE_W // D_MODEL),
        a_spec=kblk, b_spec=pl.BlockSpec((D_MODEL, D_MODEL), lambda i, j, k: (0, k + LO_W // D_MODEL)),
        acc_shape=(tl, D_MODEL), extra=[(x, row_big), (g_mix, gain_spec), (dx2, row_big), (dh_lo, row_big)],
        outs=[(jax.ShapeDtypeStruct((T, D_MODEL), F32), row_big), (jax.ShapeDtypeStruct((1, D_MODEL), F32), gain_spec)],
        epilogue=epi_dx, vmem=VMEM_MAX)

    fold64 = lambda row: (row[:64] + row[64:]).reshape(1, 64)
    grads = {
        "g_mix": d_g_mix, "b_gate": d_b_gate, "b_forget": gacc[5, :FOX_HEADS].reshape(1, FOX_HEADS),
        "qn_swa": fold64(gacc[0]), "kn_swa": fold64(gacc[1]),
        "sink_swa": -dsk[:, :SWA_GROUP, 0].reshape(1, SWA_HEADS), "rel_bias": d_rel[:, :SWA_HEADS],
        "qn_fox": fold64(gacc[2]), "kn_fox": fold64(gacc[3]),
        "g_mem": d_g_mem, "qn_mem": gacc[4].reshape(1, LANES), "kn_mem": d_kn_mem, "g_mlp": d_g_mlp,
    }
    return loss, grad_x, grads


MESH = pl.DeviceIdType.MESH


def _place():
    x, y, c = lax.axis_index("x"), lax.axis_index("y"), lax.axis_index("c")
    chips = [(1 - x, y), (x, 1 - y), (1 - x, 1 - y)]
    return x, y, c, chips


def _handshake(peers):
    barrier = pltpu.get_barrier_semaphore()
    for peer in peers:
        pl.semaphore_signal(barrier, inc=1, device_id=peer, device_id_type=MESH)
    pl.semaphore_wait(barrier, len(peers))


def _all_gather_shards_async(name, collective_id, slots):
    n = len(slots)
    bufs = [jax.new_ref(s, memory_space=pltpu.MemorySpace.HBM) for s in slots]

    def body(ici_send, ici_recv, d2d_send, d2d_recv):
        x, y, c, chips = _place()
        sibling = (x, y, 1 - c)
        me = 2 * x + y
        _handshake([(px, py, c) for px, py in chips] + [sibling])

        def half(a, who):
            hr = slots[a].shape[1] // 2
            return pl.ds(pl.multiple_of(who * hr, hr), hr)

        def ici(a, j, slot, to):
            return pltpu.make_async_remote_copy(
                src_ref=bufs[a].at[me, half(a, c)], dst_ref=bufs[a].at[slot, half(a, c)],
                send_sem=ici_send.at[3 * a + j], recv_sem=ici_recv.at[3 * a + j], device_id=to, device_id_type=MESH)

        def d2d(a, j, slot, which):
            part = bufs[a].at[slot, half(a, which)]
            return pltpu.make_async_remote_copy(
                src_ref=part, dst_ref=part, send_sem=d2d_send.at[3 * a + j], recv_sem=d2d_recv.at[3 * a + j],
                device_id=sibling, device_id_type=MESH)

        sends = [ici(a, j, me, (*chip, c)) for a in range(n) for j, chip in enumerate(chips)]
        for cp in sends:
            cp.start()
        passed = []
        for a in range(n):
            for j, (px, py) in enumerate(chips):
                ici(a, j, 2 * px + py, (px, py, c)).wait_recv()
                cp = d2d(a, j, 2 * px + py, c)
                cp.start()
                passed.append(cp)
        for a in range(n):
            for j, (px, py) in enumerate(chips):
                d2d(a, j, 2 * px + py, 1 - c).wait_recv()
        for cp in sends + passed:
            cp.wait_send()

    pl.kernel(
        body, mesh=plsc.ScalarSubcoreMesh(axis_name="seq", num_cores=1), name=name,
        scratch_types=[pltpu.SemaphoreType.DMA((3 * n,))] * 4,
        compiler_params=pltpu.CompilerParams(collective_id=collective_id),
    )()
    return [b[...] for b in bufs]


def _sequencer_call(name, collective_id, n_sems, body):
    pl.kernel(
        body, mesh=plsc.ScalarSubcoreMesh(axis_name="seq", num_cores=1), name=name,
        scratch_types=[pltpu.SemaphoreType.DMA((n_sems,))] * 2,
        compiler_params=pltpu.CompilerParams(collective_id=collective_id),
    )()


def _hbm_ref(value):
    return jax.new_ref(value, memory_space=pltpu.MemorySpace.HBM)


def _pair_exchange(name, collective_id, gs):
    n = len(gs)
    src = [_hbm_ref(g) for g in gs]
    stage = [jax.empty_ref(jax.ShapeDtypeStruct((N_SHARD, g.shape[1] // 2, g.shape[2]), g.dtype),
                           memory_space=pltpu.MemorySpace.HBM) for g in gs]

    def body(send_sem, recv_sem):
        x, y, c, _ = _place()
        sibling = (x, y, 1 - c)
        _handshake([sibling])
        copies = []
        for a in range(n):
            hr = gs[a].shape[1] // 2
            theirs = pl.ds(pl.multiple_of((1 - c) * hr, hr), hr)
            copies.append(pltpu.make_async_remote_copy(
                src_ref=src[a].at[:, theirs, :], dst_ref=stage[a], send_sem=send_sem.at[a], recv_sem=recv_sem.at[a],
                device_id=sibling, device_id_type=MESH))
        for cp in copies:
            cp.start()
        for cp in copies:
            cp.wait()

    _sequencer_call(name, collective_id, n, body)
    return [s[...] for s in stage]


def _chip_exchange(name, collective_id, sums):
    n = len(sums)
    src = [_hbm_ref(s) for s in sums]
    got = [jax.empty_ref(jax.ShapeDtypeStruct((3,) + s.shape[1:], s.dtype), memory_space=pltpu.MemorySpace.HBM)
           for s in sums]

    def body(send_sem, recv_sem):
        x, y, c, chips = _place()
        _handshake([(px, py, c) for px, py in chips])
        copies = []
        for a in range(n):
            for j, (px, py) in enumerate(chips):
                copies.append(pltpu.make_async_remote_copy(
                    src_ref=src[a].at[2 * px + py], dst_ref=got[a].at[j],
                    send_sem=send_sem.at[3 * a + j], recv_sem=recv_sem.at[3 * a + j],
                    device_id=(px, py, c), device_id_type=MESH))
        for cp in copies:
            cp.start()
        for cp in copies:
            cp.wait()

    _sequencer_call(name, collective_id, 3 * n, body)
    return [g[...] for g in got]


def _pair_gather(name, collective_id, fulls):
    n = len(fulls)
    full = [_hbm_ref(f) for f in fulls]

    def body(send_sem, recv_sem):
        x, y, c, _ = _place()
        sibling = (x, y, 1 - c)
        _handshake([sibling])
        copies = []
        for a in range(n):
            hr = fulls[a].shape[0] // 2
            mine = full[a].at[pl.ds(pl.multiple_of(c * hr, hr), hr)]
            copies.append(pltpu.make_async_remote_copy(
                src_ref=mine, dst_ref=mine, send_sem=send_sem.at[a], recv_sem=recv_sem.at[a],
                device_id=sibling, device_id_type=MESH))
        for cp in copies:
            cp.start()
        for cp in copies:
            cp.wait()

    _sequencer_call(name, collective_id, n, body)
    return [f[...] for f in full]


ELEMENTWISE_BLOCK_ELEMS = 256 * 1024


def _row_block(rows, cols):
    rb = 8
    while rb * 2 * cols <= ELEMENTWISE_BLOCK_ELEMS and rb * 2 <= rows:
        rb *= 2
    return rb


def _pair_sum(name, place, g, stage):
    _, R, C = g.shape
    hr = R // 2
    rb = _row_block(hr, C)
    nb = hr // rb

    def body(place_ref, g_ref, st_ref, sum_bf, own_f32):
        s = pl.program_id(1)
        tot = g_ref[...] + st_ref[...]
        sum_bf[...] = tot.astype(BF16)

        @pl.when(s == place_ref[0])
        def _():
            own_f32[...] = tot

    return pl.pallas_call(
        body, name=name,
        grid_spec=pltpu.PrefetchScalarGridSpec(
            num_scalar_prefetch=1, grid=(nb, N_SHARD),
            in_specs=[pl.BlockSpec((None, rb, C), lambda i, s, pr: (s, pr[1] * nb + i, 0)),
                      pl.BlockSpec((None, rb, C), lambda i, s, pr: (s, i, 0))],
            out_specs=[pl.BlockSpec((None, rb, C), lambda i, s, pr: (s, i, 0)),
                       pl.BlockSpec((rb, C), lambda i, s, pr: (i, 0))]),
        out_shape=[jax.ShapeDtypeStruct((N_SHARD, hr, C), BF16), jax.ShapeDtypeStruct((hr, C), F32)],
        compiler_params=_cparams("arbitrary", "arbitrary"),
    )(place, g, stage)


def _final_sum(name, place, own, got):
    hr, C = own.shape
    rb = _row_block(hr, C)
    nb = hr // rb

    def body(place_ref, own_ref, got_ref, o_ref):
        o_ref[...] = ((own_ref[...] + got_ref[0].astype(F32)) + got_ref[1].astype(F32)) + got_ref[2].astype(F32)

    return pl.pallas_call(
        body, name=name,
        grid_spec=pltpu.PrefetchScalarGridSpec(
            num_scalar_prefetch=1, grid=(nb,),
            in_specs=[pl.BlockSpec((rb, C), lambda i, pr: (i, 0)), pl.BlockSpec((3, rb, C), lambda i, pr: (0, i, 0))],
            out_specs=pl.BlockSpec((rb, C), lambda i, pr: (pr[1] * nb + i, 0))),
        out_shape=jax.ShapeDtypeStruct((2 * hr, C), F32),
        compiler_params=_cparams("arbitrary"),
    )(place, own, got)


def _adamw_math(w, g, m, v):
    m = ADAM_B1 * m + (1.0 - ADAM_B1) * g
    v = ADAM_B2 * v + (1.0 - ADAM_B2) * (g * g)
    m_hat = m / (1.0 - ADAM_B1 ** ADAM_STEP)
    v_hat = v / (1.0 - ADAM_B2 ** ADAM_STEP)
    delta = -ADAM_LR * (m_hat / (jnp.sqrt(v_hat) + ADAM_EPS) + ADAM_WD * w)
    return delta, m, v


def _adamw(name, w, g, m, v):
    R, Cw = w.shape
    Cg = g.shape[1]
    rb = _row_block(R, Cg)

    def body(w_ref, g_ref, m_ref, v_ref, g_o, d_o, m_o, v_o):
        gv = g_ref[...]
        delta, mn, vn = _adamw_math(w_ref[...], gv, m_ref[...], v_ref[...])
        g_o[...] = gv
        d_o[...] = delta
        m_o[...] = mn
        v_o[...] = vn

    blk = pl.BlockSpec((rb, Cg), lambda i: (i, 0))
    return pl.pallas_call(
        body, name=name, grid=(R // rb,),
        in_specs=[blk] * 4, out_specs=[blk] * 4,
        out_shape=[jax.ShapeDtypeStruct((R, Cw), F32)] * 4,
        compiler_params=_cparams("parallel"),
    )(w, g, m, v)


N_DEV = 8
SMALL_ROWS = 64


def _small_allreduce_adamw(g, w, m, v):
    def body(g_ref, w_ref, m_ref, v_ref, all_ref, gs_o, d_o, m_o, v_o, send_sems, recv_sems, local_sem):
        x, y, c, chips = _place()
        me, sibling = (x, y, c), (x, y, 1 - c)

        def rows(px, py, pc):
            return all_ref.at[pl.ds(pl.multiple_of((4 * px + 2 * py + pc) * SMALL_ROWS, SMALL_ROWS), SMALL_ROWS), :]

        def copy(k, block, to, src=None):
            return pltpu.make_async_remote_copy(
                src_ref=rows(*block) if src is None else src, dst_ref=rows(*block),
                send_sem=send_sems.at[k], recv_sem=recv_sems.at[k], device_id=to, device_id_type=MESH)

        mine = pltpu.make_async_copy(g_ref, rows(*me), local_sem)
        mine.start()
        first = [copy(0, me, sibling, src=g_ref)]
        first += [copy(1 + j, me, (*chip, c), src=g_ref) for j, chip in enumerate(chips)]
        for cp in first:
            cp.start()
        passed = [copy(4 + j, (*chip, c), sibling) for j, chip in enumerate(chips)]
        for j, chip in enumerate(chips):
            copy(1 + j, (*chip, c), me).wait_recv()
            passed[j].start()
        copy(0, sibling, me).wait_recv()
        for j, chip in enumerate(chips):
            copy(4 + j, (*chip, 1 - c), me).wait_recv()
        for cp in first + passed:
            cp.wait_send()
        mine.wait()

        tot = all_ref[0:SMALL_ROWS, :]
        for d in range(1, N_DEV):
            tot = tot + all_ref[d * SMALL_ROWS:(d + 1) * SMALL_ROWS, :]
        delta, mn, vn = _adamw_math(w_ref[...], tot, m_ref[...], v_ref[...])
        gs_o[...] = tot
        d_o[...] = delta
        m_o[...] = mn
        v_o[...] = vn

    vm = pl.BlockSpec(memory_space=pltpu.VMEM)
    shp = jax.ShapeDtypeStruct((SMALL_ROWS, LANES), F32)
    res = pl.pallas_call(
        body, name="small_allreduce_adamw", in_specs=[vm] * 4, out_specs=[vm] * 5,
        out_shape=[jax.ShapeDtypeStruct((N_DEV * SMALL_ROWS, LANES), F32), shp, shp, shp, shp],
        scratch_shapes=[pltpu.SemaphoreType.DMA((7,)), pltpu.SemaphoreType.DMA((7,)), pltpu.SemaphoreType.DMA],
    )(g, w, m, v)
    return res[1:]


SMALL_NAMES = ("g_mix", "b_gate", "b_forget", "qn_swa", "kn_swa", "sink_swa", "rel_bias", "qn_fox", "kn_fox",
               "g_mem", "qn_mem", "kn_mem", "g_mlp")
BIG_NAMES = ("w_in", "w_mem_kv", "w_o_swa", "w_o_fox", "w_o_mem", "w_out", "w_mlp_up", "w_mlp_down")
WEIGHT_NAMES = ("g_mix", "w_in", "b_gate", "b_forget", "qn_swa", "kn_swa", "sink_swa", "rel_bias", "qn_fox", "kn_fox",
                "g_mem", "w_mem_kv", "qn_mem", "kn_mem", "w_o_swa", "w_o_fox", "w_o_mem", "w_out", "g_mlp",
                "w_mlp_up", "w_mlp_down")


def _pack_small(parts, extra=None):
    rows = []
    for n in SMALL_NAMES:
        flat = parts[n].reshape(-1).astype(F32)
        flat = jnp.pad(flat, (0, (-flat.size) % LANES))
        rows.append(flat.reshape(-1, LANES))
    if extra is not None:
        rows.append(jnp.pad(extra.reshape(1, 1), ((0, 0), (0, LANES - 1))))
    packed = jnp.concatenate(rows, axis=0)
    return jnp.pad(packed, ((0, SMALL_ROWS - packed.shape[0]), (0, 0)))


def _unpack_small(packed, shapes):
    out, r = {}, 0
    for n in SMALL_NAMES:
        size = math.prod(shapes[n])
        nr = -(-size // LANES)
        out[n] = packed[r:r + nr].reshape(-1)[:size].reshape(shapes[n])
        r += nr
    return out, packed[r, 0]


W_IN_SEGMENTS = ((C_QA, 0, 512), (C_QF, 768, 512), (C_KF, 1280, 512), (C_VF, 1792, 512), (C_QM, 2312, 512),
                 (C_KA, 512, 128), (C_VA, 640, 128), (C_FL, 2304, FOX_HEADS), (C_GL, 2824, GATE_W))
RELAYOUT_ROWS = 256


def _permute_pieces(src_of_dst):
    blocks = []
    for b in range(len(src_of_dst) // LANES):
        runs, lane = [], 0
        while lane < LANES:
            src = src_of_dst[b * LANES + lane]
            if src is None:
                lane += 1
                continue
            plane, col = src
            end = lane + 1
            while (end < LANES and src_of_dst[b * LANES + end] == (plane, col + end - lane)
                   and (col + end - lane) // LANES == col // LANES):
                end += 1
            runs.append((plane, col // LANES, (lane - col) % LANES, lane, end))
            lane = end
        blocks.append(runs)
    return blocks


def _permuted_block(runs, load, rows):
    lane = _lane((rows, LANES))
    acc = jnp.zeros((rows, LANES), F32)
    for plane, blk, shift, lo, hi in runs:
        x = load(plane, blk).astype(F32)
        if shift:
            x = pltpu.roll(x, shift, 1)
        acc = x if (lo, hi) == (0, LANES) else jnp.where((lane >= lo) & (lane < hi), x, acc)
    return acc


def _w_in_to_segments(g_in):
    src_of_dst = [None] * PROJ_W
    for mine, theirs, width in W_IN_SEGMENTS:
        for k in range(width):
            src_of_dst[mine + k] = ((theirs + k) // IN_SHARD, (theirs + k) % IN_SHARD)
    blocks = _permute_pieces(src_of_dst)
    rb = RELAYOUT_ROWS

    def body(src_ref, out_ref):
        for b, runs in enumerate(blocks):
            blk = _permuted_block(runs, lambda p, c: src_ref[p, :, c * LANES:(c + 1) * LANES], rb)
            out_ref[:, b * LANES:(b + 1) * LANES] = blk.astype(out_ref.dtype)

    return pl.pallas_call(
        body, name="w_in_to_segments", grid=(D_MODEL // rb,),
        in_specs=[pl.BlockSpec((N_SHARD, rb, IN_SHARD_PAD), lambda i: (0, i, 0))],
        out_specs=pl.BlockSpec((rb, PROJ_W), lambda i: (i, 0)),
        out_shape=jax.ShapeDtypeStruct((D_MODEL, PROJ_W), g_in.dtype),
        compiler_params=_cparams("parallel", vmem=VMEM_MID),
    )(g_in)


def _w_in_from_segments(lo, gl):
    mine_of_theirs = {}
    for mine, theirs, width in W_IN_SEGMENTS:
        for k in range(width):
            mine_of_theirs[theirs + k] = mine + k
    src_of_dst = [None] * (N_SHARD * IN_SHARD_PAD)
    for s in range(N_SHARD):
        for l in range(IN_SHARD):
            j = mine_of_theirs[s * IN_SHARD + l]
            src_of_dst[s * IN_SHARD_PAD + l] = (j // LO_W, j % LO_W)
    blocks = _permute_pieces(src_of_dst)
    per_slot = IN_SHARD_PAD // LANES
    rb = RELAYOUT_ROWS

    def body(lo_ref, gl_ref, out_ref):
        planes = (lo_ref, gl_ref)
        for b, runs in enumerate(blocks):
            blk = _permuted_block(runs, lambda p, c: planes[p][:, c * LANES:(c + 1) * LANES], rb)
            c0 = (b % per_slot) * LANES
            out_ref[b // per_slot, :, c0:c0 + LANES] = blk

    half = pl.BlockSpec((rb, LO_W), lambda i: (i, 0))
    return pl.pallas_call(
        body, name="w_in_from_segments", grid=(D_MODEL // rb,),
        in_specs=[half, half],
        out_specs=pl.BlockSpec((N_SHARD, rb, IN_SHARD_PAD), lambda i: (0, i, 0)),
        out_shape=jax.ShapeDtypeStruct((N_SHARD, D_MODEL, IN_SHARD_PAD), F32),
        compiler_params=_cparams("parallel", vmem=VMEM_MID),
    )(lo, gl)


def _after(first, then):
    return lax.optimization_barrier((first, then))


class _ReduceGroup:
    def __init__(self, tag, first_collective_id, place):
        self.tag, self.first_id, self.place = tag, first_collective_id, place

    def start(self, local, tie):
        self.names = tuple(local)
        mine, tie = _after([local[n] for n in self.names], tie)
        self.mine = mine
        self.staged = _pair_exchange("pair_exchange_" + self.tag, self.first_id, mine)
        return tie

    def send(self, tie):
        staged, tie = _after(self.staged, tie)
        sums = [_pair_sum("pair_sum_" + n, self.place, g, st) for n, g, st in zip(self.names, self.mine, staged)]
        travel, tie = _after([s[0] for s in sums], tie)
        self.own = [s[1] for s in sums]
        self.got = _chip_exchange("chip_exchange_" + self.tag, self.first_id + 1, travel)
        return tie

    def finish(self, tie):
        got, tie = _after(self.got, tie)
        halves = [_final_sum("final_sum_" + n, self.place, o, r) for n, o, r in zip(self.names, self.own, got)]
        halves, tie = _after(halves, tie)
        summed = _pair_gather("pair_gather_" + self.tag, self.first_id + 2, halves)
        self.summed = dict(zip(self.names, summed))
        return tie


class _GradReducer:
    def __init__(self, place):
        self.early = _ReduceGroup("early", 2, place)
        self.late = _ReduceGroup("late", 5, place)

    @staticmethod
    def _slot_rows(a):
        return a.reshape(N_SHARD, a.shape[0] // N_SHARD, a.shape[1])

    def early_start(self, g, tie):
        return self.early.start({"w_mlp_down": self._slot_rows(g["w_mlp_down"]), "w_mlp_up": g["w_mlp_up"],
                                 "w_out": self._slot_rows(g["w_out"]), "w_mem_kv": self._slot_rows(g["w_mem_kv"]),
                                 "w_o_swa": g["w_o_swa"], "w_o_fox": g["w_o_fox"], "w_o_mem": g["w_o_mem"]}, tie)

    def early_send(self, tie):
        return self.early.send(tie)

    def early_finish(self, tie):
        return self.early.finish(tie)

    def late_start(self, g, tie):
        d_in = _w_in_from_segments(g["wc_lo"], g["wc_gl"])
        return self.late.start({"w_in": d_in}, tie)

    def late_send(self, tie):
        return self.late.send(tie)

    def late_finish(self, tie):
        return self.late.finish(tie)

    @property
    def summed(self):
        return {**self.early.summed, **self.late.summed}


def kernel(x, mem, g_mix, w_in, b_gate, b_forget, qn_swa, kn_swa, sink_swa, rel_bias, qn_fox, kn_fox, g_mem, w_mem_kv, qn_mem, kn_mem, w_o_swa, w_o_fox, w_o_mem, w_out, g_mlp, w_mlp_up, w_mlp_down, loss_target, m_g_mix, m_w_in, m_b_gate, m_b_forget, m_qn_swa, m_kn_swa, m_sink_swa, m_rel_bias, m_qn_fox, m_kn_fox, m_g_mem, m_w_mem_kv, m_qn_mem, m_kn_mem, m_w_o_swa, m_w_o_fox, m_w_o_mem, m_w_out, m_g_mlp, m_w_mlp_up, m_w_mlp_down, v_g_mix, v_w_in, v_b_gate, v_b_forget, v_qn_swa, v_kn_swa, v_sink_swa, v_rel_bias, v_qn_fox, v_kn_fox, v_g_mem, v_w_mem_kv, v_qn_mem, v_kn_mem, v_w_o_swa, v_w_o_fox, v_w_o_mem, v_w_out, v_g_mlp, v_w_mlp_up, v_w_mlp_down):
    given = dict(locals())
    W = {n: given[n] for n in WEIGHT_NAMES}
    M = {n: given["m_" + n] for n in WEIGHT_NAMES}
    V = {n: given["v_" + n] for n in WEIGHT_NAMES}
    pad_in = ((0, 0), (0, IN_SHARD_PAD - IN_SHARD))

    shards = [jnp.pad(w_in[0].astype(BF16), pad_in)] + [W[n][0].astype(BF16) for n in BIG_NAMES[1:]]
    slots = [jnp.broadcast_to(s[None], (N_SHARD,) + s.shape) for s in shards]
    (g_in,) = _all_gather_shards_async("all_gather_w_in", 1, slots[:1])
    small = {n: (W[n] if n == "rel_bias" else W[n].reshape(1, -1)) for n in SMALL_NAMES}
    h = _rmsnorm("rms_mix", x[0], small["g_mix"], min(512, x.shape[1]))
    g_in, late, h, (m_in, v_in) = lax.optimization_barrier((g_in, slots[1:], h, (M["w_in"][0], V["w_in"][0])))
    M["w_in"], V["w_in"] = m_in[None], v_in[None]
    g_kv, g_oa, g_of, g_om, g_out, g_up, g_down = _all_gather_shards_async("all_gather_weights_async", 8, late)

    place = jnp.stack([2 * lax.axis_index("x") + lax.axis_index("y"), lax.axis_index("c")]).astype(jnp.int32)
    reducer = _GradReducer(place)
    loss, grad_x, grads = _local_step(
        x[0], h, mem[0], loss_target[0], small, g_in, g_kv.reshape(D_MODEL, D_MODEL), (g_oa, g_of, g_om),
        g_out.reshape(D_MODEL, D_MODEL), g_up, g_down.reshape(D_FF, D_MODEL), reducer)

    out = {}

    def adamw_of(names, summed):
        for n in names:
            res = _adamw("adamw_" + n, W[n][0], summed[n], M[n][0], V[n][0])
            out[n] = [r.reshape(W[n].shape) for r in res]

    adamw_of(reducer.early.names, reducer.early.summed)
    shapes = {n: W[n].shape for n in SMALL_NAMES}
    packed = _small_allreduce_adamw(_pack_small(grads, loss), _pack_small(W), _pack_small(M), _pack_small(V))
    done_meanwhile = ([out[n] for n in reducer.early.names], packed)
    (early_out, packed), grad_x = reducer.late_finish((done_meanwhile, grad_x))
    for n, res in zip(reducer.early.names, early_out):
        out[n] = res
    adamw_of(reducer.late.names, reducer.late.summed)
    unpacked = [_unpack_small(p, shapes) for p in packed]
    for n in SMALL_NAMES:
        out[n] = [u[0][n] for u in unpacked]
    loss_total = unpacked[0][1]

    return (loss_total, grad_x.reshape(x.shape),
            *[out[n][0] for n in WEIGHT_NAMES], *[out[n][1] for n in WEIGHT_NAMES],
            *[out[n][2] for n in WEIGHT_NAMES], *[out[n][3] for n in WEIGHT_NAMES])
```

```python
import math

import jax
import jax.numpy as jnp
from jax import lax
from jax.experimental import pallas as pl
from jax.experimental.pallas import tpu as pltpu
from jax.experimental.pallas import tpu_sc as plsc

F32 = jnp.float32
BF16 = jnp.bfloat16

D_MODEL = 1024
N_MEM = 256
SWA_HEADS = 8
SWA_KV_HEADS = 2
SWA_HEAD_DIM = 64
WINDOW = 128
FOX_HEADS = 8
FOX_HEAD_DIM = 64
MEM_HEADS = 4
MEM_HEAD_DIM = 128
D_FF = 4 * D_MODEL
REL_BUCKETS = 32
REL_MAX_DIST = 128
EPS = 1e-6
NEG = -1e30
GATE_W = 3 * D_MODEL
IN_WIDTH = 5896
N_SHARD = 4
IN_SHARD = IN_WIDTH // N_SHARD
IN_SHARD_PAD = 1536

ADAM_LR = 0.001
ADAM_B1 = 0.9
ADAM_B2 = 0.999
ADAM_EPS = 1e-08
ADAM_WD = 0.01
ADAM_STEP = 10

LANES = 128
V7X_VMEM_BYTES = 64 * 1024 * 1024
VMEM_SMALL = VMEM_MID = VMEM_BIG = V7X_VMEM_BYTES * 3 // 4
VMEM_MAX = V7X_VMEM_BYTES * 7 // 8

C_QA, C_QF, C_KF, C_VF, C_QM, C_KA, C_VA, C_FL, C_GL = 0, 512, 1024, 1536, 2048, 2560, 2688, 2816, 3072
LO_W = 3072
PROJ_W = 6144
PROJ_TN = 2048

NN = (((1,), (0,)), ((), ()))
NT = (((1,), (1,)), ((), ()))
TN = (((0,), (0,)), ((), ()))


def _dot(a, b, dims=NN):
    return lax.dot_general(a, b, dims, preferred_element_type=F32)


def _cparams(*sem, vmem=VMEM_SMALL):
    return pltpu.CompilerParams(dimension_semantics=sem, vmem_limit_bytes=vmem)


def _split3(a):
    hi = a.astype(BF16)
    r1 = a - hi.astype(F32)
    mid = r1.astype(BF16)
    lo = (r1 - mid.astype(F32)).astype(BF16)
    return hi, mid, lo


def _group_mean(a, g2):
    hi = a.astype(BF16)
    mid = (a - hi.astype(F32)).astype(BF16)
    return _dot(jnp.concatenate([hi, mid], axis=1), g2)


def _dot3_left(g, a):
    hi, mid, lo = _split3(a)
    return _dot(g, hi) + _dot(g, mid) + _dot(g, lo)


def _group_mean_matrix(d):
    r = jnp.arange(LANES)
    g = jnp.where((r[:, None] // d) == (r[None, :] // d), 1.0 / d, 0.0).astype(BF16)
    return jnp.concatenate([g, g], axis=0)


def _lane(shape):
    return lax.broadcasted_iota(jnp.int32, shape, len(shape) - 1)


def _matmul(name, a, b, *, dims, grid, a_spec, b_spec, acc_shape, outs, epilogue, extra=(), vmem=VMEM_BIG):
    nk = grid[2]
    n_extra = len(extra)

    def body(a_ref, b_ref, *rest):
        extra_refs = rest[:n_extra]
        out_refs = rest[n_extra:n_extra + len(outs)]
        i, j, k = pl.program_id(0), pl.program_id(1), pl.program_id(2)
        if nk == 1:
            epilogue(_dot(a_ref[...].astype(BF16), b_ref[...].astype(BF16), dims), extra_refs, out_refs, (i, j))
            return
        acc_ref = rest[-1]

        @pl.when(k == 0)
        def _():
            acc_ref[...] = jnp.zeros_like(acc_ref)

        acc_ref[...] += _dot(a_ref[...].astype(BF16), b_ref[...].astype(BF16), dims)

        @pl.when(k == nk - 1)
        def _():
            epilogue(acc_ref[...], extra_refs, out_refs, (i, j))

    res = pl.pallas_call(
        body,
        name=name,
        grid=grid,
        in_specs=[a_spec, b_spec] + [s for _, s in extra],
        out_specs=[s for _, s in outs],
        out_shape=[s for s, _ in outs],
        scratch_shapes=[pltpu.VMEM(acc_shape, F32)] if nk > 1 else [],
        compiler_params=_cparams("arbitrary", "arbitrary", "arbitrary", vmem=vmem),
    )(a, b, *[x for x, _ in extra])
    return res


def _epi_store(acc, extra_refs, out_refs, ij):
    out_refs[0][...] = acc.astype(out_refs[0].dtype)


def _rms_rows(x, g):
    r = lax.rsqrt(jnp.mean(x * x, axis=-1, keepdims=True) + EPS)
    return x * r, r


def _rmsnorm_bwd_rows(dh, x, g):
    xhat, r = _rms_rows(x, g)
    dxh = dh * g
    dx = r * (dxh - xhat * jnp.mean(dxh * xhat, axis=-1, keepdims=True))
    return dx, jnp.sum(dh * xhat, axis=0, keepdims=True)


def _rmsnorm(name, x, g, tb):
    T, Dm = x.shape

    def body(x_ref, g_ref, o_ref):
        xhat, _ = _rms_rows(x_ref[...], None)
        o_ref[...] = (xhat * g_ref[...]).astype(o_ref.dtype)

    return pl.pallas_call(
        body, name=name, grid=(T // tb,),
        in_specs=[pl.BlockSpec((tb, Dm), lambda i: (i, 0)), pl.BlockSpec((1, Dm), lambda i: (0, 0))],
        out_specs=pl.BlockSpec((tb, Dm), lambda i: (i, 0)),
        out_shape=jax.ShapeDtypeStruct((T, Dm), BF16),
        compiler_params=_cparams("parallel"),
    )(x, g)


def _head_norm(x, gm, gain):
    ms = _group_mean(x * x, gm)
    r = lax.rsqrt(ms + EPS)
    return x * r * gain, x * r


def _head_norm_bwd(dy, x, gm, gain):
    ms = _group_mean(x * x, gm)
    r = lax.rsqrt(ms + EPS)
    xhat = x * r
    dxh = dy * gain
    dx = r * (dxh - xhat * _group_mean(dxh * xhat, gm))
    return dx, jnp.sum(dy * xhat, axis=0, keepdims=True)


def _log_sigmoid(z):
    return jnp.minimum(z, 0.0) - jnp.log(1.0 + jnp.exp(-jnp.abs(z)))


def _prep_fwd(proj, gains, bfor, tril, gm64, gm128, T, tb):
    nb = T // tb

    def body(qa_ref, qf_ref, kf_ref, vf_ref, qm_ref, ka_ref, va_ref, fl_ref, gains_ref, bfor_ref, tril_ref,
             gm64_ref, gm128_ref,
             qa_o, qf_o, kf_o, vf_o, qm_o, kad_o, vad_o, qaug_o, kaug_o, carry):
        i = pl.program_id(0)
        gm64v = gm64_ref[...]
        gm128v = gm128_ref[...]
        lane = _lane((tb, LANES))

        def norm512(src, dst, row, gm, scale=1.0):
            gain = gains_ref[row:row + 1, :]
            for c in range(4):
                sl = slice(c * LANES, (c + 1) * LANES)
                y, _ = _head_norm(src[:, sl], gm, gain)
                dst[:, sl] = (y * scale).astype(dst.dtype)

        norm512(qa_ref, qa_o, 0, gm64v)
        norm512(qf_ref, qf_o, 2, gm64v, FOX_SCALE)
        norm512(kf_ref, kf_o, 3, gm64v)
        norm512(qm_ref, qm_o, 4, gm128v)
        vf_o[...] = vf_ref[...].astype(vf_o.dtype)

        ka_n, _ = _head_norm(ka_ref[...], gm64v, gains_ref[1:2, :])
        ka_r = pltpu.roll(ka_n, 64, 1)
        va = va_ref[...]
        va_r = pltpu.roll(va, 64, 1)
        lo = lane < 64
        kad_o[0] = jnp.where(lo, ka_n, ka_r).astype(kad_o.dtype)
        kad_o[1] = jnp.where(lo, ka_r, ka_n).astype(kad_o.dtype)
        vad_o[0] = jnp.where(lo, va, va_r).astype(vad_o.dtype)
        vad_o[1] = jnp.where(lo, va_r, va).astype(vad_o.dtype)

        @pl.when(i == 0)
        def _():
            carry[...] = jnp.zeros_like(carry)

        logf = jnp.where(lane < FOX_HEADS, _log_sigmoid(fl_ref[...] + bfor_ref[...]), 0.0)
        c = _dot3_left(tril_ref[...], logf) + carry[0:1, :]
        carry[...] = jnp.broadcast_to(c[tb - 1:tb, :], carry.shape)
        for pair in range(FOX_HEADS // 2):
            qaug = jnp.zeros((tb, LANES), F32)
            kaug = jnp.zeros((tb, LANES), F32)
            for sub in range(2):
                col = jnp.sum(jnp.where(lane == 2 * pair + sub, c, 0.0), axis=1, keepdims=True)
                pieces = [p.astype(F32) for p in _split3(col)]
                base = AUG_STRIDE * sub
                for e in range(3):
                    qaug = jnp.where(lane == base + AUG_C + e, pieces[e], qaug)
                    kaug = jnp.where(lane == base + AUG_NEG_C + e, -pieces[e], kaug)
                qaug = jnp.where((lane >= base + AUG_NEG_C) & (lane < base + AUG_NEG_C + 3), 1.0, qaug)
                ones_k = ((lane >= base + AUG_C) & (lane < base + AUG_C + 3)) | (
                    (lane >= base + AUG_STAT) & (lane < base + AUG_STAT + 3))
                kaug = jnp.where(ones_k, 1.0, kaug)
            sl = slice(pair * LANES, (pair + 1) * LANES)
            qaug_o[:, sl] = qaug.astype(BF16)
            kaug_o[:, sl] = kaug.astype(BF16)

    def seg(width, start):
        return pl.BlockSpec((tb, width), lambda i, s=start // width: (i, s))

    const = lambda shape: pl.BlockSpec(shape, lambda i: tuple(0 for _ in shape))
    rows512 = pl.BlockSpec((tb, 512), lambda i: (i, 0))
    outs = pl.pallas_call(
        body, name="prep_fwd", grid=(nb,),
        in_specs=[seg(512, C_QA), seg(512, C_QF), seg(512, C_KF), seg(512, C_VF), seg(512, C_QM),
                  seg(128, C_KA), seg(128, C_VA), seg(128, C_FL),
                  const((8, LANES)), const((1, LANES)), const((tb, tb)), const((2 * LANES, LANES)), const((2 * LANES, LANES))],
        out_specs=[rows512, rows512, rows512, rows512, rows512,
                   pl.BlockSpec((2, tb, LANES), lambda i: (0, i, 0)), pl.BlockSpec((2, tb, LANES), lambda i: (0, i, 0)),
                   rows512, rows512],
        out_shape=[jax.ShapeDtypeStruct((T, 512), BF16)] * 5
        + [jax.ShapeDtypeStruct((2, T, LANES), BF16)] * 2
        + [jax.ShapeDtypeStruct((T, 512), BF16)] * 2,
        scratch_shapes=[pltpu.VMEM((8, LANES), F32)],
        compiler_params=_cparams("arbitrary", vmem=VMEM_MID),
    )(proj, proj, proj, proj, proj, proj, proj, proj, gains, bfor, tril, gm64, gm128)
    return outs


def _prep_bwd(proj, dqa, dkad, dvad, dqf, dkf, dvf, dqm, dqf_aug, dkf_aug, gains, bfor, triu, gm64, gm128, T, tb):
    nb = T // tb

    def body(qa_ref, qf_ref, kf_ref, qm_ref, ka_ref, fl_ref,
             dqa_ref, dkad_ref, dvad_ref, dqf_ref, dkf_ref, dvf_ref, dqm_ref, dqfa_ref, dkfa_ref,
             gains_ref, bfor_ref, triu_ref, gm64_ref, gm128_ref,
             dlo_o, gacc_o, carry):
        i = pl.program_id(0)
        gm64v = gm64_ref[...]
        gm128v = gm128_ref[...]
        lane = _lane((tb, LANES))

        @pl.when(i == 0)
        def _():
            carry[...] = jnp.zeros_like(carry)
            gacc_o[...] = jnp.zeros_like(gacc_o)

        def norm512_bwd(dsrc, xsrc, col0, row, gm):
            gain = gains_ref[row:row + 1, :]
            gsum = jnp.zeros((1, LANES), F32)
            for c in range(4):
                sl = slice(c * LANES, (c + 1) * LANES)
                dx, dg = _head_norm_bwd(dsrc[:, sl], xsrc[:, sl], gm, gain)
                dlo_o[:, col0 + c * LANES:col0 + (c + 1) * LANES] = dx.astype(dlo_o.dtype)
                gsum = gsum + dg
            gacc_o[row:row + 1, :] += gsum

        norm512_bwd(dqa_ref, qa_ref, C_QA, 0, gm64v)
        norm512_bwd(dqf_ref, qf_ref, C_QF, 2, gm64v)
        norm512_bwd(dkf_ref, kf_ref, C_KF, 3, gm64v)
        norm512_bwd(dqm_ref, qm_ref, C_QM, 4, gm128v)
        dlo_o[:, C_VF:C_VF + 512] = dvf_ref[...].astype(dlo_o.dtype)

        lo = lane < 64

        def fold(ref):
            f0 = ref[0] + pltpu.roll(ref[0], 64, 1)
            f1 = ref[1] + pltpu.roll(ref[1], 64, 1)
            return jnp.where(lo, f0, f1)

        dka, dg = _head_norm_bwd(fold(dkad_ref), ka_ref[...], gm64v, gains_ref[1:2, :])
        gacc_o[1:2, :] += dg
        dlo_o[:, C_KA:C_KA + LANES] = dka.astype(dlo_o.dtype)
        dlo_o[:, C_VA:C_VA + LANES] = fold(dvad_ref).astype(dlo_o.dtype)

        dc = jnp.zeros((tb, LANES), F32)
        for pair in range(FOX_HEADS // 2):
            sl = slice(pair * LANES, (pair + 1) * LANES)
            rows_sum, cols_sum = dqfa_ref[:, sl], dkfa_ref[:, sl]
            for sub in range(2):
                diff = (jnp.where(lane == AUG_STRIDE * sub + AUG_C, rows_sum, 0.0)
                        - jnp.where(lane == AUG_STRIDE * sub + AUG_NEG_C, cols_sum, 0.0))
                dc = jnp.where(lane == 2 * pair + sub, jnp.sum(diff, axis=1, keepdims=True), dc)
        dlogf = _dot3_left(triu_ref[...], dc) + carry[0:1, :]
        carry[...] = jnp.broadcast_to(dlogf[0:1, :], carry.shape)
        z = fl_ref[...] + bfor_ref[...]
        dfl = jnp.where(lane < FOX_HEADS, dlogf / (1.0 + jnp.exp(z)), 0.0)
        gacc_o[5:6, :] += jnp.sum(dfl, axis=0, keepdims=True)
        dlo_o[:, C_FL:C_FL + LANES] = dfl.astype(dlo_o.dtype)
        dlo_o[:, C_FL + LANES:C_FL + 2 * LANES] = jnp.zeros((tb, LANES), dlo_o.dtype)

    rev = lambda i: nb - 1 - i

    def seg(width, start):
        return pl.BlockSpec((tb, width), lambda i, s=start // width: (rev(i), s))

    const = lambda shape: pl.BlockSpec(shape, lambda i: tuple(0 for _ in shape))
    rows512 = pl.BlockSpec((tb, 512), lambda i: (rev(i), 0))
    dup = pl.BlockSpec((2, tb, LANES), lambda i: (0, rev(i), 0))
    return pl.pallas_call(
        body, name="prep_bwd", grid=(nb,),
        in_specs=[seg(512, C_QA), seg(512, C_QF), seg(512, C_KF), seg(512, C_QM), seg(128, C_KA), seg(128, C_FL),
                  rows512, dup, dup, rows512, rows512, rows512, rows512, rows512, rows512,
                  const((8, LANES)), const((1, LANES)), const((tb, tb)), const((2 * LANES, LANES)), const((2 * LANES, LANES))],
        out_specs=[pl.BlockSpec((tb, LO_W), lambda i: (rev(i), 0)), const((8, LANES))],
        out_shape=[jax.ShapeDtypeStruct((T, LO_W), BF16), jax.ShapeDtypeStruct((8, LANES), F32)],
        scratch_shapes=[pltpu.VMEM((8, LANES), F32)],
        compiler_params=_cparams("arbitrary", vmem=VMEM_MID),
    )(proj, proj, proj, proj, proj, proj, dqa, dkad, dvad, dqf, dkf, dvf, dqm, dqf_aug, dkf_aug,
      gains, bfor, triu, gm64, gm128)


FOX_SCALE = FOX_HEAD_DIM ** -0.5
AUG_STRIDE = 16
AUG_C = 0
AUG_NEG_C = 3
AUG_STAT = 6
FOX_TQ, FOX_TK = 1024, 1024
FOX_BWD_TQ, FOX_BWD_TK = 1024, 1024
FOX_DIAGONAL_PARTS = 4


def _fox_head_mask(sub, rows):
    lane = _lane((rows, 2 * LANES))
    main = (lane >= 64 * sub) & (lane < 64 * sub + 64)
    aug = (lane >= LANES + AUG_STRIDE * sub) & (lane < LANES + AUG_STRIDE * (sub + 1))
    return main | aug


def _fox_pieces(diagonal, tq, tk):
    if diagonal and tq == tk and tq >= FOX_DIAGONAL_PARTS * LANES:
        step = tq // FOX_DIAGONAL_PARTS
        return [(n * step, (n + 1) * step, (n + 1) * step) for n in range(FOX_DIAGONAL_PARTS)]
    return [(0, tq, tk)]


def _fox_fwd(q, qaug, k, kaug, v, T, tq, tk):
    nq, nk = T // tq, T // tk
    rep = tk // LANES
    last_of = lambda i: (i * tq + tq - 1) // tk

    def body(q_ref, qa_ref, k_ref, ka_ref, v_ref, o_ref, qab_ref, m_s, acc_s):
        p_, i, j = pl.program_id(0), pl.program_id(1), pl.program_id(2)
        last = last_of(i)

        @pl.when(j == 0)
        def _():
            m_s[...] = jnp.full(m_s.shape, NEG, F32)
            acc_s[...] = jnp.zeros_like(acc_s)

        def step(diagonal):
            k2 = jnp.concatenate([k_ref[...], ka_ref[...]], axis=1)
            v2 = jnp.concatenate([v_ref[...], ka_ref[...]], axis=1)
            pieces = _fox_pieces(diagonal, tq, tk)
            work = []
            for r0, r1, nc in pieces:
                rows = slice(r0, r1)
                q2 = jnp.concatenate([q_ref[rows, :], qa_ref[rows, :]], axis=1)
                for sub in range(2):
                    qh = jnp.where(_fox_head_mask(sub, r1 - r0), q2, jnp.zeros_like(q2))
                    work.append((rows, r0, r1 - r0, nc, sub, _dot(qh, k2[:nc], NT)))
            for rows, r0, nr, nc, sub, s in work:
                if diagonal:
                    causal = (lax.broadcasted_iota(jnp.int32, (nr, nc), 1) + j * tk
                              <= lax.broadcasted_iota(jnp.int32, (nr, nc), 0) + (r0 + i * tq))
                    s = jnp.where(causal, s, NEG)
                m_prev = m_s[sub, rows, :]
                m_next = jnp.maximum(m_prev, jnp.max(s, axis=1, keepdims=True))
                p = jnp.exp(s - jnp.tile(m_next, (1, nc // LANES)))
                alpha = jnp.exp(m_prev - m_next)
                m_s[sub, rows, :] = m_next
                acc_s[sub, rows, :] = acc_s[sub, rows, :] * jnp.tile(alpha, (1, 2)) + _dot(p.astype(BF16), v2[:nc])

        @pl.when(j == last)
        def _():
            step(True)

        @pl.when(j < last)
        def _():
            step(False)

        @pl.when(j == nk - 1)
        def _():
            lane = _lane((tq, LANES))
            outs = []
            qab = qa_ref[...].astype(F32)
            for sub in range(2):
                acc = acc_s[sub]
                base = AUG_STRIDE * sub
                l = jnp.sum(jnp.where(lane == base + AUG_C, acc[:, LANES:], 0.0), axis=1, keepdims=True)
                outs.append(acc[:, :LANES] / l)
                lse = jnp.max(m_s[sub], axis=1, keepdims=True) + jnp.log(l)
                pieces = _split3(-lse)
                for e in range(3):
                    qab = jnp.where(lane == base + AUG_STAT + e, pieces[e].astype(F32), qab)
            o_ref[...] = jnp.where(lane < 64, outs[0], outs[1]).astype(o_ref.dtype)
            qab_ref[...] = qab.astype(BF16)

    qspec = pl.BlockSpec((tq, LANES), lambda p, i, j: (i, p))
    kspec = pl.BlockSpec((tk, LANES), lambda p, i, j: (jnp.minimum(j, last_of(i)), p))
    return pl.pallas_call(
        body, name="fox_fwd", grid=(4, nq, nk),
        in_specs=[qspec, qspec, kspec, kspec, kspec],
        out_specs=[qspec, qspec],
        out_shape=[jax.ShapeDtypeStruct((T, 512), BF16), jax.ShapeDtypeStruct((T, 512), BF16)],
        scratch_shapes=[pltpu.VMEM((2, tq, LANES), F32), pltpu.VMEM((2, tq, 2 * LANES), F32)],
        compiler_params=_cparams("parallel", "parallel", "arbitrary", vmem=VMEM_BIG),
    )(q, qaug, k, kaug, v)


def _fox_bwd(q, qaug, k, kaug, v, do, doaug, T, tq, tk):
    nq, nk = T // tq, T // tk
    first_of = lambda j: (j * tk) // tq

    def body(q_ref, qa_ref, k_ref, ka_ref, v_ref, do_ref, doa_ref,
             dq_ref, dqa_ref, dk_ref, dka_ref, dv_ref, dk_s, dv_s):
        p_, j, i = pl.program_id(0), pl.program_id(1), pl.program_id(2)
        masked = i * tq < (j + 1) * tk - 1

        @pl.when((j == 0) & (i == 0))
        def _():
            dq_ref[...] = jnp.zeros_like(dq_ref)
            dqa_ref[...] = jnp.zeros_like(dqa_ref)

        @pl.when(i == 0)
        def _():
            dk_s[...] = jnp.zeros_like(dk_s)
            dv_s[...] = jnp.zeros_like(dv_s)

        def step(diagonal):
            k2 = jnp.concatenate([k_ref[...], ka_ref[...]], axis=1)
            v2 = jnp.concatenate([v_ref[...], ka_ref[...]], axis=1)
            work = []
            for r0, r1, nc in _fox_pieces(diagonal, tq, tk):
                rows = slice(r0, r1)
                q2 = jnp.concatenate([q_ref[rows, :], qa_ref[rows, :]], axis=1)
                do2 = jnp.concatenate([do_ref[rows, :], doa_ref[rows, :]], axis=1)
                for sub in range(2):
                    hm = _fox_head_mask(sub, r1 - r0)
                    qh = jnp.where(hm, q2, jnp.zeros_like(q2))
                    doh = jnp.where(hm, do2, jnp.zeros_like(do2))
                    s = _dot(qh, k2[:nc], NT)
                    dp = _dot(doh, v2[:nc], NT)
                    work.append((r0, r1 - r0, nc, sub, qh, doh, s, dp))
            dqs = {}
            for r0, nr, nc, sub, qh, doh, s, dp in work:
                if diagonal:
                    causal = (lax.broadcasted_iota(jnp.int32, (nr, nc), 1) + j * tk
                              <= lax.broadcasted_iota(jnp.int32, (nr, nc), 0) + (r0 + i * tq))
                    s = jnp.where(causal, s, NEG)
                p = jnp.exp(s)
                dsb = (p * dp).astype(BF16)
                dv_s[0:nc, :] += _dot(p.astype(BF16), doh[:, :LANES], TN)
                dk_s[0:nc, :] += _dot(dsb, qh, TN)
                dqs[(r0, sub)] = _dot(dsb, k2[:nc])
            for r0, r1, nc in _fox_pieces(diagonal, tq, tk):
                dq2 = jnp.where(_fox_head_mask(0, r1 - r0), dqs[(r0, 0)], dqs[(r0, 1)])
                qrows = pl.ds(pl.multiple_of(i * tq + r0, r1 - r0), r1 - r0)
                dq_ref[qrows, :] += dq2[:, :LANES] * FOX_SCALE
                dqa_ref[qrows, :] += dq2[:, LANES:]

        @pl.when((i >= first_of(j)) & masked)
        def _():
            step(True)

        @pl.when((i >= first_of(j)) & jnp.logical_not(masked))
        def _():
            step(False)

        @pl.when(i == nq - 1)
        def _():
            dk_ref[...] = dk_s[:, :LANES]
            dka_ref[...] = dk_s[:, LANES:]
            dv_ref[...] = dv_s[...]

    qspec = pl.BlockSpec((tq, LANES), lambda p, j, i: (jnp.maximum(i, first_of(j)), p))
    kspec = pl.BlockSpec((tk, LANES), lambda p, j, i: (j, p))
    resident = pl.BlockSpec((T, LANES), lambda p, j, i: (0, p))
    return pl.pallas_call(
        body, name="fox_bwd", grid=(4, nk, nq),
        in_specs=[qspec, qspec, kspec, kspec, kspec, qspec, qspec],
        out_specs=[resident, resident, kspec, kspec, kspec],
        out_shape=[jax.ShapeDtypeStruct((T, 512), F32)] * 5,
        scratch_shapes=[pltpu.VMEM((tk, 2 * LANES), F32), pltpu.VMEM((tk, LANES), F32)],
        compiler_params=_cparams("arbitrary", "arbitrary", "arbitrary", vmem=VMEM_BIG),
    )(q, qaug, k, kaug, v, do, doaug)


SWA_SUB = 16
SWA_TB = SWA_SUB * WINDOW


def _t5_bucket_matrix():
    t = jnp.arange(WINDOW)[:, None] + WINDOW
    s = jnp.arange(2 * WINDOW)[None, :]
    max_exact = REL_BUCKETS // 2
    d = jnp.maximum(t - s, 0)
    df = jnp.maximum(d, 1).astype(F32)
    large = max_exact + (jnp.log(df / max_exact) / math.log(REL_MAX_DIST / max_exact)
                         * (REL_BUCKETS - max_exact)).astype(jnp.int32)
    large = jnp.minimum(large, REL_BUCKETS - 1)
    return jnp.where(d < max_exact, d, large).astype(jnp.int32)


def _swa_bias(rel_bias, bucket):
    def body(rel_ref, bucket_ref, o_ref):
        b = bucket_ref[...]
        for h in range(SWA_HEADS):
            acc = jnp.zeros(b.shape, F32)
            for r in range(REL_BUCKETS):
                acc = jnp.where(b == r, rel_ref[r, h], acc)
            o_ref[h] = acc

    return pl.pallas_call(
        body, name="swa_bias",
        in_specs=[pl.BlockSpec(memory_space=pltpu.SMEM), pl.BlockSpec(memory_space=pltpu.VMEM)],
        out_specs=pl.BlockSpec(memory_space=pltpu.VMEM),
        out_shape=jax.ShapeDtypeStruct((SWA_HEADS, WINDOW, 2 * WINDOW), F32),
    )(rel_bias, bucket)


def _swa_bias_bwd(dbias, bucket):
    def body(db_ref, bucket_ref, o_ref):
        b = bucket_ref[...]
        lane = _lane((1, LANES))
        for r in range(REL_BUCKETS):
            row = jnp.zeros((1, LANES), F32)
            for h in range(SWA_HEADS):
                part = jnp.sum(jnp.where(b == r, db_ref[h], 0.0), axis=0, keepdims=True)
                tot = jnp.sum(part, axis=1, keepdims=True)
                row = jnp.where(lane == h, tot, row)
            o_ref[r:r + 1, :] = row

    return pl.pallas_call(
        body, name="swa_bias_bwd",
        in_specs=[pl.BlockSpec(memory_space=pltpu.VMEM), pl.BlockSpec(memory_space=pltpu.VMEM)],
        out_specs=pl.BlockSpec(memory_space=pltpu.VMEM),
        out_shape=jax.ShapeDtypeStruct((REL_BUCKETS, LANES), F32),
    )(dbias, bucket)


SWA_GROUP = SWA_HEADS // SWA_KV_HEADS


def _swa_valid(r, i):
    t = (lax.broadcasted_iota(jnp.int32, (SWA_GROUP * WINDOW, 2 * WINDOW), 0) & (WINDOW - 1)) + WINDOW
    s = lax.broadcasted_iota(jnp.int32, (SWA_GROUP * WINDOW, 2 * WINDOW), 1)
    dist = t - s
    band = (dist >= 0) & (dist < WINDOW)
    if r == 0:
        band = band & ((s >= WINDOW) | (i > 0))
    return band


def _swa_stack(blk):
    lane = _lane((WINDOW, LANES))
    parts = []
    for g in range(SWA_GROUP):
        b = blk[:, LANES * (g // 2):LANES * (g // 2 + 1)]
        parts.append(jnp.where((lane >= 64) if g % 2 else (lane < 64), b, jnp.zeros_like(b)))
    return jnp.concatenate(parts, axis=0)


def _swa_unstack(st):
    lane = _lane((WINDOW, LANES))
    W = WINDOW
    return jnp.concatenate([jnp.where(lane < 64, st[2 * b * W:(2 * b + 1) * W], st[(2 * b + 1) * W:(2 * b + 2) * W])
                            for b in range(2)], axis=1)


def _swa_sink_column(sink_ref, kvh):
    row = lax.broadcasted_iota(jnp.int32, (SWA_GROUP * WINDOW, 1), 0)
    col = jnp.full((SWA_GROUP * WINDOW, 1), sink_ref[SWA_GROUP * kvh + SWA_GROUP - 1], F32)
    for g in range(SWA_GROUP - 2, -1, -1):
        col = jnp.where(row < (g + 1) * WINDOW, sink_ref[SWA_GROUP * kvh + g], col)
    return col


def _swa_specs(T):
    W = WINDOW
    qspec = pl.BlockSpec((SWA_TB, 2 * LANES), lambda h, i: (i, h))
    own = pl.BlockSpec((None, SWA_TB, LANES), lambda h, i: (h, i, 0))
    prev = pl.BlockSpec((None, W, LANES), lambda h, i: (h, jnp.maximum(SWA_SUB * i - 1, 0), 0))
    stat = pl.BlockSpec((SWA_GROUP, SWA_TB, LANES), lambda h, i: (h, i, 0))
    bias = pl.BlockSpec((None, SWA_GROUP * W, 2 * W), lambda h, i: (h, 0, 0))
    return qspec, own, prev, stat, bias


def _swa_fwd(sinks, q, kad, vad, bias, T):
    nb = T // SWA_TB
    scale = SWA_HEAD_DIM ** -0.5
    W = WINDOW

    def body(sink_ref, q_ref, k_ref, kp_ref, v_ref, vp_ref, bias_ref, o_ref, lse_ref):
        kvh, i = pl.program_id(0), pl.program_id(1)
        sink = _swa_sink_column(sink_ref, kvh)
        for r in range(SWA_SUB):
            rs = slice(r * W, (r + 1) * W)
            ps = slice((r - 1) * W, r * W)
            k_own, v_own = k_ref[rs, :], v_ref[rs, :]
            k_prev = kp_ref[...] if r == 0 else k_ref[ps, :]
            v_prev = vp_ref[...] if r == 0 else v_ref[ps, :]
            qs = _swa_stack(q_ref[rs, :])
            s = jnp.concatenate([_dot(qs, k_prev, NT), _dot(qs, k_own, NT)], axis=1) * scale + bias_ref[...]
            s = jnp.where(_swa_valid(r, i), s, NEG)
            m = jnp.maximum(jnp.max(s, axis=1, keepdims=True), sink)
            p = jnp.exp(s - m)
            denom = jnp.sum(p, axis=1, keepdims=True) + jnp.exp(sink - m)
            pn = (p / denom).astype(BF16)
            o_ref[rs, :] = _swa_unstack(_dot(pn[:, :W], v_prev) + _dot(pn[:, W:], v_own)).astype(o_ref.dtype)
            lse = m + jnp.log(denom)
            for g in range(SWA_GROUP):
                lse_ref[g, rs, :] = jnp.broadcast_to(lse[g * W:(g + 1) * W], (W, LANES))

    qspec, own, prev, stat, bspec = _swa_specs(T)
    return pl.pallas_call(
        body, name="swa_fwd", grid=(SWA_KV_HEADS, nb),
        in_specs=[pl.BlockSpec(memory_space=pltpu.SMEM), qspec, own, prev, own, prev, bspec],
        out_specs=[qspec, stat],
        out_shape=[jax.ShapeDtypeStruct((T, 512), BF16), jax.ShapeDtypeStruct((SWA_HEADS, T, LANES), F32)],
        compiler_params=_cparams("parallel", "parallel", vmem=VMEM_MID),
    )(sinks, q, kad, kad, vad, vad, bias.reshape(SWA_KV_HEADS, SWA_GROUP * W, 2 * W))


def _swa_bwd(sinks, q, kad, vad, bias, do, lse, delta, T):
    nb = T // SWA_TB
    scale = SWA_HEAD_DIM ** -0.5
    W = WINDOW

    def body(sink_ref, q_ref, k_ref, kp_ref, v_ref, vp_ref, bias_ref, do_ref, lse_ref, dl_ref,
             dq_ref, dkad_ref, dvad_ref, dbias_ref, dsk_ref):
        kvh, i = pl.program_id(0), pl.program_id(1)
        sink = _swa_sink_column(sink_ref, kvh)

        @pl.when((kvh == 0) & (i == 0))
        def _():
            dkad_ref[...] = jnp.zeros_like(dkad_ref)
            dvad_ref[...] = jnp.zeros_like(dvad_ref)

        @pl.when(i == 0)
        def _():
            dbias_ref[...] = jnp.zeros_like(dbias_ref)
            dsk_ref[...] = jnp.zeros_like(dsk_ref)

        for r in range(SWA_SUB):
            rs = slice(r * W, (r + 1) * W)
            ps = slice((r - 1) * W, r * W)
            k_own, v_own = k_ref[rs, :], v_ref[rs, :]
            k_prev = kp_ref[...] if r == 0 else k_ref[ps, :]
            v_prev = vp_ref[...] if r == 0 else v_ref[ps, :]
            qs = _swa_stack(q_ref[rs, :])
            dos = _swa_stack(do_ref[rs, :])
            lse_b = jnp.concatenate([lse_ref[g, rs, :] for g in range(SWA_GROUP)], axis=0)
            dl_b = jnp.concatenate([dl_ref[g, rs, :] for g in range(SWA_GROUP)], axis=0)
            s = jnp.concatenate([_dot(qs, k_prev, NT), _dot(qs, k_own, NT)], axis=1) * scale + bias_ref[...]
            s = jnp.where(_swa_valid(r, i), s, NEG)
            p = jnp.exp(s - jnp.tile(lse_b, (1, 2)))
            dp = jnp.concatenate([_dot(dos, v_prev, NT), _dot(dos, v_own, NT)], axis=1)
            ds = p * (dp - jnp.tile(dl_b, (1, 2)))
            sink_term = jnp.exp(sink - lse_b) * dl_b
            for g in range(SWA_GROUP):
                dbias_ref[g] += ds[g * W:(g + 1) * W]
                dsk_ref[g:g + 1, :] += jnp.sum(sink_term[g * W:(g + 1) * W], axis=0, keepdims=True)
            dsb = ds.astype(BF16)
            pb = p.astype(BF16)
            dq_ref[rs, :] = _swa_unstack((_dot(dsb[:, :W], k_prev) + _dot(dsb[:, W:], k_own)) * scale)
            own_row = pl.multiple_of(i * SWA_TB + r * W, W)
            dkad_ref[kvh, pl.ds(own_row, W), :] += _dot(dsb[:, W:], qs, TN) * scale
            dvad_ref[kvh, pl.ds(own_row, W), :] += _dot(pb[:, W:], dos, TN)
            dk_prev = _dot(dsb[:, :W], qs, TN) * scale
            dv_prev = _dot(pb[:, :W], dos, TN)
            if r == 0:
                @pl.when(i > 0)
                def _():
                    prev_row = pl.multiple_of(i * SWA_TB - W, W)
                    dkad_ref[kvh, pl.ds(prev_row, W), :] += dk_prev
                    dvad_ref[kvh, pl.ds(prev_row, W), :] += dv_prev
            else:
                prev_row = pl.multiple_of(i * SWA_TB + (r - 1) * W, W)
                dkad_ref[kvh, pl.ds(prev_row, W), :] += dk_prev
                dvad_ref[kvh, pl.ds(prev_row, W), :] += dv_prev

    qspec, own, prev, stat, bspec = _swa_specs(T)
    full = pl.BlockSpec((SWA_KV_HEADS, T, LANES), lambda h, i: (0, 0, 0))
    return pl.pallas_call(
        body, name="swa_bwd", grid=(SWA_KV_HEADS, nb),
        in_specs=[pl.BlockSpec(memory_space=pltpu.SMEM), qspec, own, prev, own, prev, bspec, qspec, stat, stat],
        out_specs=[qspec, full, full, pl.BlockSpec((SWA_GROUP, W, 2 * W), lambda h, i: (h, 0, 0)),
                   pl.BlockSpec((None, 8, LANES), lambda h, i: (h, 0, 0))],
        out_shape=[jax.ShapeDtypeStruct((T, 512), F32), jax.ShapeDtypeStruct((SWA_KV_HEADS, T, LANES), F32),
                   jax.ShapeDtypeStruct((SWA_KV_HEADS, T, LANES), F32), jax.ShapeDtypeStruct((SWA_HEADS, W, 2 * W), F32),
                   jax.ShapeDtypeStruct((SWA_KV_HEADS, 8, LANES), F32)],
        compiler_params=_cparams("arbitrary", "arbitrary", vmem=VMEM_MID),
    )(sinks, q, kad, kad, vad, vad, bias.reshape(SWA_KV_HEADS, SWA_GROUP * W, 2 * W), do, lse, delta)


MEM_TQ = 4096


def _mem_fwd(q, mk, mv, T, tq):
    scale = MEM_HEAD_DIM ** -0.5

    def body(q_ref, k_ref, v_ref, o_ref, lse_ref):
        s = _dot(q_ref[...], k_ref[...], NT) * scale
        m = jnp.max(s, axis=1, keepdims=True)
        p = jnp.exp(s - m)
        l = jnp.sum(p, axis=1, keepdims=True)
        o_ref[...] = _dot((p / l).astype(BF16), v_ref[...]).astype(o_ref.dtype)
        lse_ref[...] = jnp.broadcast_to(m + jnp.log(l), (tq, LANES))

    qspec = pl.BlockSpec((tq, LANES), lambda h, i: (i, h))
    kspec = pl.BlockSpec((N_MEM, LANES), lambda h, i: (0, h))
    return pl.pallas_call(
        body, name="mem_fwd", grid=(MEM_HEADS, T // tq),
        in_specs=[qspec, kspec, kspec],
        out_specs=[qspec, pl.BlockSpec((None, tq, LANES), lambda h, i: (h, i, 0))],
        out_shape=[jax.ShapeDtypeStruct((T, 512), BF16), jax.ShapeDtypeStruct((MEM_HEADS, T, LANES), F32)],
        compiler_params=_cparams("parallel", "parallel"),
    )(q, mk, mv)


def _mem_bwd(q, mk, mv, do, lse, delta, T, tq):
    scale = MEM_HEAD_DIM ** -0.5
    rep = N_MEM // LANES

    def body(q_ref, k_ref, v_ref, do_ref, lse_ref, dl_ref, dq_ref, dk_ref, dv_ref):
        i = pl.program_id(1)

        @pl.when(i == 0)
        def _():
            dk_ref[...] = jnp.zeros_like(dk_ref)
            dv_ref[...] = jnp.zeros_like(dv_ref)

        qv, dov = q_ref[...], do_ref[...]
        s = _dot(qv, k_ref[...], NT) * scale
        p = jnp.exp(s - jnp.tile(lse_ref[...], (1, rep)))
        dp = _dot(dov, v_ref[...], NT)
        ds = p * (dp - jnp.tile(dl_ref[...], (1, rep)))
        dsb = ds.astype(BF16)
        dq_ref[...] = _dot(dsb, k_ref[...]) * scale
        dk_ref[...] += _dot(dsb, qv, TN) * scale
        dv_ref[...] += _dot(p.astype(BF16), dov, TN)

    qspec = pl.BlockSpec((tq, LANES), lambda h, i: (i, h))
    kspec = pl.BlockSpec((N_MEM, LANES), lambda h, i: (0, h))
    stat = pl.BlockSpec((None, tq, LANES), lambda h, i: (h, i, 0))
    return pl.pallas_call(
        body, name="mem_bwd", grid=(MEM_HEADS, T // tq),
        in_specs=[qspec, kspec, kspec, qspec, stat, stat],
        out_specs=[qspec, kspec, kspec],
        out_shape=[jax.ShapeDtypeStruct((T, 512), F32), jax.ShapeDtypeStruct((N_MEM, 512), F32),
                   jax.ShapeDtypeStruct((N_MEM, 512), F32)],
        compiler_params=_cparams("arbitrary", "arbitrary"),
    )(q, mk, mv, do, lse, delta)


def _mem_prep_fwd(mem, g_mem, w_kv, kn_gain, gm128):
    def body(mem_ref, g_ref, w_ref, kn_ref, gm_ref, memn_o, kv_o, mk_o, mv_o):
        xhat, _ = _rms_rows(mem_ref[...], None)
        memn = (xhat * g_ref[...]).astype(BF16)
        memn_o[...] = memn
        kv = _dot(memn, w_ref[...])
        kv_o[...] = kv
        gm = gm_ref[...]
        for c in range(4):
            sl = slice(c * LANES, (c + 1) * LANES)
            y, _ = _head_norm(kv[:, sl], gm, kn_ref[...])
            mk_o[:, sl] = y.astype(BF16)
        mv_o[...] = kv[:, 512:].astype(BF16)

    vm = pl.BlockSpec(memory_space=pltpu.VMEM)
    return pl.pallas_call(
        body, name="mem_prep_fwd", in_specs=[vm] * 5, out_specs=[vm] * 4,
        out_shape=[jax.ShapeDtypeStruct((N_MEM, D_MODEL), BF16), jax.ShapeDtypeStruct((N_MEM, D_MODEL), F32),
                   jax.ShapeDtypeStruct((N_MEM, 512), BF16), jax.ShapeDtypeStruct((N_MEM, 512), BF16)],
        compiler_params=pltpu.CompilerParams(vmem_limit_bytes=VMEM_MID),
    )(mem, g_mem, w_kv, kn_gain, gm128)


def _mem_prep_bwd(mem, g_mem, memn, kv, w_kv, kn_gain, gm128, dmk, dmv):
    def body(mem_ref, g_ref, memn_ref, kv_ref, w_ref, kn_ref, gm_ref, dmk_ref, dmv_ref, dw_o, dg_o, dkn_o, dkv_s):
        gm = gm_ref[...]
        dkn = jnp.zeros((1, LANES), F32)
        for c in range(4):
            sl = slice(c * LANES, (c + 1) * LANES)
            dx, dg = _head_norm_bwd(dmk_ref[:, sl], kv_ref[:, sl], gm, kn_ref[...])
            dkv_s[:, sl] = dx.astype(BF16)
            dkn = dkn + dg
        dkn_o[...] = dkn
        dkv_s[:, 512:] = dmv_ref[...].astype(BF16)
        dkv = dkv_s[...]
        dw_o[...] = _dot(memn_ref[...], dkv, TN)
        dmemn = _dot(dkv, w_ref[...], NT)
        xhat, _ = _rms_rows(mem_ref[...], None)
        dg_o[...] = jnp.sum(dmemn * xhat, axis=0, keepdims=True)

    vm = pl.BlockSpec(memory_space=pltpu.VMEM)
    return pl.pallas_call(
        body, name="mem_prep_bwd", in_specs=[vm] * 9, out_specs=[vm] * 3,
        out_shape=[jax.ShapeDtypeStruct((D_MODEL, D_MODEL), F32), jax.ShapeDtypeStruct((1, D_MODEL), F32),
                   jax.ShapeDtypeStruct((1, LANES), F32)],
        scratch_shapes=[pltpu.VMEM((N_MEM, D_MODEL), BF16)],
        compiler_params=pltpu.CompilerParams(vmem_limit_bytes=VMEM_MID),
    )(mem, g_mem, memn, kv, w_kv, kn_gain, gm128, dmk, dmv)


SLOT_O = D_MODEL // N_SHARD


def _merge_fwd(proj, b_gate, o3, w3, T, tb):
    def body(gl_ref, bg_ref, oa_ref, of_ref, om_ref, wa_ref, wf_ref, wm_ref, out_ref):
        o_refs = (oa_ref, of_ref, om_ref)
        w_refs = (wa_ref, wf_ref, wm_ref)
        for n in range(N_SHARD):
            acc = jnp.zeros((tb, SLOT_O), F32)
            for b in range(3):
                c0 = b * D_MODEL + n * SLOT_O
                g = jax.nn.sigmoid(gl_ref[:, c0:c0 + SLOT_O] + bg_ref[:, c0:c0 + SLOT_O])
                acc = acc + g * _dot(o_refs[b][...], w_refs[b][n])
            out_ref[:, n * SLOT_O:(n + 1) * SLOT_O] = acc.astype(out_ref.dtype)

    rows = pl.BlockSpec((tb, 512), lambda i: (i, 0))
    wspec = pl.BlockSpec((N_SHARD, 512, SLOT_O), lambda i: (0, 0, 0))
    return pl.pallas_call(
        body, name="merge_fwd", grid=(T // tb,),
        in_specs=[pl.BlockSpec((tb, GATE_W), lambda i: (i, 1)), pl.BlockSpec((1, GATE_W), lambda i: (0, 0)),
                  rows, rows, rows, wspec, wspec, wspec],
        out_specs=pl.BlockSpec((tb, D_MODEL), lambda i: (i, 0)),
        out_shape=jax.ShapeDtypeStruct((T, D_MODEL), BF16),
        compiler_params=_cparams("parallel", vmem=VMEM_BIG),
    )(proj, b_gate, *o3, *w3)


def _merge_bwd(proj, b_gate, o3, w3, dmerged, T, tb):
    heads = (SWA_HEADS, FOX_HEADS, MEM_HEADS)

    def body(gl_ref, bg_ref, oa_ref, of_ref, om_ref, wa_ref, wf_ref, wm_ref, dm_ref,
             dgl_o, doa_o, dof_o, dom_o, dla_o, dlf_o, dlm_o, dwa_o, dwf_o, dwm_o, dbg_o):
        i = pl.program_id(0)
        o_refs = (oa_ref, of_ref, om_ref)
        w_refs = (wa_ref, wf_ref, wm_ref)
        do_refs = (doa_o, dof_o, dom_o)
        dl_refs = (dla_o, dlf_o, dlm_o)
        dw_refs = (dwa_o, dwf_o, dwm_o)

        @pl.when(i == 0)
        def _():
            for r in dw_refs:
                r[...] = jnp.zeros_like(r)
            dbg_o[...] = jnp.zeros_like(dbg_o)

        lane = _lane((tb, LANES))
        for b in range(3):
            ob = o_refs[b][...]
            do = jnp.zeros((tb, 512), F32)
            for n in range(N_SHARD):
                c0 = b * D_MODEL + n * SLOT_O
                g = jax.nn.sigmoid(gl_ref[:, c0:c0 + SLOT_O] + bg_ref[:, c0:c0 + SLOT_O])
                dm = dm_ref[:, n * SLOT_O:(n + 1) * SLOT_O]
                y = _dot(ob, w_refs[b][n])
                dgl = dm * y * g * (1.0 - g)
                dgl_o[:, c0:c0 + SLOT_O] = dgl.astype(dgl_o.dtype)
                dbg_o[:, c0:c0 + SLOT_O] += jnp.sum(dgl, axis=0, keepdims=True)
                dy = (dm * g).astype(BF16)
                do = do + _dot(dy, w_refs[b][n], NT)
                dw_refs[b][n] += _dot(ob, dy, TN)
            do_refs[b][...] = do.astype(BF16)
            prod = do * ob.astype(F32)
            for c in range(4):
                blk = prod[:, c * LANES:(c + 1) * LANES]
                if heads[b] == 8:
                    lo = jnp.sum(jnp.where(lane < 64, blk, 0.0), axis=1, keepdims=True)
                    hi = jnp.sum(jnp.where(lane >= 64, blk, 0.0), axis=1, keepdims=True)
                    if b == 1:
                        aug = jnp.zeros((tb, LANES), F32)
                        for sub, dl in enumerate((lo, hi)):
                            for e, piece in enumerate(_split3(-dl)):
                                aug = jnp.where(lane == AUG_STRIDE * sub + AUG_C + e, piece.astype(F32), aug)
                        dl_refs[b][:, c * LANES:(c + 1) * LANES] = aug.astype(BF16)
                    else:
                        dl_refs[b][2 * c] = jnp.broadcast_to(lo, (tb, LANES))
                        dl_refs[b][2 * c + 1] = jnp.broadcast_to(hi, (tb, LANES))
                else:
                    dl_refs[b][c] = jnp.broadcast_to(jnp.sum(blk, axis=1, keepdims=True), (tb, LANES))

    rows = pl.BlockSpec((tb, 512), lambda i: (i, 0))
    wspec = pl.BlockSpec((N_SHARD, 512, SLOT_O), lambda i: (0, 0, 0))
    stat = lambda h: pl.BlockSpec((h, tb, LANES), lambda i: (0, i, 0))
    return pl.pallas_call(
        body, name="merge_bwd", grid=(T // tb,),
        in_specs=[pl.BlockSpec((tb, GATE_W), lambda i: (i, 1)), pl.BlockSpec((1, GATE_W), lambda i: (0, 0)),
                  rows, rows, rows, wspec, wspec, wspec, pl.BlockSpec((tb, D_MODEL), lambda i: (i, 0))],
        out_specs=[pl.BlockSpec((tb, GATE_W), lambda i: (i, 0)), rows, rows, rows,
                   stat(8), rows, stat(4), wspec, wspec, wspec, pl.BlockSpec((1, GATE_W), lambda i: (0, 0))],
        out_shape=[jax.ShapeDtypeStruct((T, GATE_W), BF16)] + [jax.ShapeDtypeStruct((T, 512), BF16)] * 3
        + [jax.ShapeDtypeStruct((8, T, LANES), F32), jax.ShapeDtypeStruct((T, 512), BF16),
           jax.ShapeDtypeStruct((4, T, LANES), F32)]
        + [jax.ShapeDtypeStruct((N_SHARD, 512, SLOT_O), F32)] * 3 + [jax.ShapeDtypeStruct((1, GATE_W), F32)],
        compiler_params=_cparams("arbitrary", vmem=VMEM_BIG),
    )(proj, b_gate, *o3, *w3, dmerged)


def _local_step(x, h, mem, tgt, small, g_in, w_kv, w_o3, w_out, w_up, w_down, reducer):
    T = x.shape[0]
    tm = min(512, T)
    tile2 = lambda v: jnp.tile(v.reshape(1, -1), (1, LANES // v.size))
    gains = jnp.concatenate([tile2(small["qn_swa"]), tile2(small["kn_swa"]), tile2(small["qn_fox"]),
                             tile2(small["kn_fox"]), tile2(small["qn_mem"]), jnp.zeros((3, LANES), F32)], axis=0)
    kn_mem = small["kn_mem"].reshape(1, LANES)
    bfor = jnp.pad(small["b_forget"].reshape(1, -1), ((0, 0), (0, LANES - FOX_HEADS)))
    gm64 = _group_mean_matrix(64)
    gm128 = _group_mean_matrix(128)
    tb_prep = min(512, T)
    ones = jnp.ones((tb_prep, tb_prep), F32)
    tril = jnp.tril(ones).astype(BF16)
    triu = jnp.triu(ones).astype(BF16)
    bucket = _t5_bucket_matrix()
    g_mix, g_mlp, g_mem = small["g_mix"], small["g_mlp"], small["g_mem"]
    b_gate = small["b_gate"]
    sinks = small["sink_swa"].reshape(-1)

    tl = min(1024, T)
    sq = pl.BlockSpec((tl, D_MODEL), lambda i, j, k: (i, j))
    wc = _w_in_to_segments(g_in)
    (proj,) = _matmul(
        "mm_proj", h, wc, dims=NN, grid=(T // tl, PROJ_W // PROJ_TN, 1),
        a_spec=pl.BlockSpec((tl, D_MODEL), lambda i, j, k: (i, 0)),
        b_spec=pl.BlockSpec((D_MODEL, PROJ_TN), lambda i, j, k: (0, j)),
        acc_shape=(tl, PROJ_TN),
        outs=[(jax.ShapeDtypeStruct((T, PROJ_W), F32), pl.BlockSpec((tl, PROJ_TN), lambda i, j, k: (i, j)))],
        epilogue=_epi_store)
    qa, qf, kf, vf, qm, kad, vad, qf_aug, kf_aug = _prep_fwd(proj, gains, bfor, tril, gm64, gm128, T, tb_prep)
    bias = _swa_bias(small["rel_bias"], bucket)
    o_swa, lse_swa = _swa_fwd(sinks, qa, kad, vad, bias, T)
    o_fox, qf_aug_bwd = _fox_fwd(qf, qf_aug, kf, kf_aug, vf, T, min(FOX_TQ, T), min(FOX_TK, T))
    memn, kv, mk, mv = _mem_prep_fwd(mem, g_mem, w_kv, kn_mem, gm128)
    o_mem, lse_mem = _mem_fwd(qm, mk, mv, T, min(MEM_TQ, T))
    o3 = (o_swa, o_fox, o_mem)
    merged = _merge_fwd(proj, b_gate, o3, w_o3, T, min(512, T))

    def epi_residual(acc, extra_refs, out_refs, ij):
        out_refs[0][...] = extra_refs[0][...] + acc

    row_full = pl.BlockSpec((tm, D_MODEL), lambda i, j, k: (i, 0))
    row_big = pl.BlockSpec((tl, D_MODEL), lambda i, j, k: (i, 0))
    whole = pl.BlockSpec((D_MODEL, D_MODEL), lambda i, j, k: (0, 0))
    (x2,) = _matmul(
        "mm_out", merged, w_out, dims=NN, grid=(T // tl, 1, 1),
        a_spec=row_big, b_spec=whole,
        acc_shape=(tl, D_MODEL), extra=[(x, row_big)],
        outs=[(jax.ShapeDtypeStruct((T, D_MODEL), F32), row_big)], epilogue=epi_residual)
    hm = _rmsnorm("rms_mlp", x2, g_mlp, tm)

    def epi_relu2(acc, extra_refs, out_refs, ij):
        out_refs[0][...] = acc.astype(BF16)
        r = jnp.maximum(acc, 0.0)
        out_refs[1][...] = (r * r).astype(BF16)

    up, u = _matmul(
        "mm_up", hm, w_up, dims=NN, grid=(T // tl, N_SHARD, 1),
        a_spec=row_big, b_spec=pl.BlockSpec((None, D_MODEL, D_MODEL), lambda i, j, k: (j, 0, 0)),
        acc_shape=(tl, D_MODEL),
        outs=[(jax.ShapeDtypeStruct((T, D_FF), BF16), sq), (jax.ShapeDtypeStruct((T, D_FF), BF16), sq)],
        epilogue=epi_relu2)

    def epi_loss(acc, extra_refs, out_refs, ij):
        y = extra_refs[0][...] + acc
        err = y - extra_refs[1][...]
        dyv = err * (1.0 / D_MODEL)
        out_refs[0][...] = dyv
        out_refs[2][...] = dyv.astype(BF16)
        sq = jnp.sum(jnp.sum(err * err, axis=1, keepdims=True), axis=0, keepdims=True)

        @pl.when(ij[0] == 0)
        def _():
            out_refs[1][...] = jnp.zeros_like(out_refs[1])

        out_refs[1][...] += jnp.broadcast_to(sq, out_refs[1].shape)

    kblk = pl.BlockSpec((tl, D_MODEL), lambda i, j, k: (i, k))
    dy, loss_acc, dy_bf = _matmul(
        "mm_down", u, w_down, dims=NN, grid=(T // tl, 1, N_SHARD),
        a_spec=kblk, b_spec=pl.BlockSpec((D_MODEL, D_MODEL), lambda i, j, k: (k, 0)),
        acc_shape=(tl, D_MODEL), extra=[(x2, row_big), (tgt, row_big)],
        outs=[(jax.ShapeDtypeStruct((T, D_MODEL), F32), row_big),
              (jax.ShapeDtypeStruct((8, LANES), F32), pl.BlockSpec((8, LANES), lambda i, j, k: (0, 0))),
              (jax.ShapeDtypeStruct((T, D_MODEL), BF16), row_big)],
        epilogue=epi_loss)
    loss = loss_acc[0, 0] * (0.5 / D_MODEL)

    def epi_dup(acc, extra_refs, out_refs, ij):
        out_refs[0][...] = (acc * (2.0 * jnp.maximum(extra_refs[0][...].astype(F32), 0.0))).astype(BF16)

    (dup,) = _matmul(
        "mm_dup", dy_bf, w_down, dims=NT, grid=(T // tl, N_SHARD, 1),
        a_spec=row_big, b_spec=pl.BlockSpec((D_MODEL, D_MODEL), lambda i, j, k: (j, 0)),
        acc_shape=(tl, D_MODEL), extra=[(up, sq)],
        outs=[(jax.ShapeDtypeStruct((T, D_FF), BF16), sq)], epilogue=epi_dup)

    nkt = T // tl
    t_rows = pl.BlockSpec((tl, D_MODEL), lambda i, j, k: (k, i))
    t_cols = pl.BlockSpec((tl, D_MODEL), lambda i, j, k: (k, j))
    (d_w_down,) = _matmul(
        "mm_dw_down", u, dy_bf, dims=TN, grid=(N_SHARD, 1, nkt),
        a_spec=t_rows, b_spec=t_cols, acc_shape=(D_MODEL, D_MODEL),
        outs=[(jax.ShapeDtypeStruct((D_FF, D_MODEL), F32), pl.BlockSpec((D_MODEL, D_MODEL), lambda i, j, k: (i, 0)))],
        epilogue=_epi_store)
    (d_w_up,) = _matmul(
        "mm_dw_up", hm, dup, dims=TN, grid=(1, N_SHARD, nkt),
        a_spec=t_rows, b_spec=t_cols, acc_shape=(D_MODEL, D_MODEL),
        outs=[(jax.ShapeDtypeStruct((N_SHARD, D_MODEL, D_MODEL), F32),
               pl.BlockSpec((None, D_MODEL, D_MODEL), lambda i, j, k: (j, 0, 0)))],
        epilogue=_epi_store)

    def epi_rms_bwd(acc, extra_refs, out_refs, ij):
        dx, dg = _rmsnorm_bwd_rows(acc, extra_refs[0][...], extra_refs[1][...])
        out_refs[0][...] = dx + extra_refs[2][...]

        @pl.when(ij[0] == 0)
        def _():
            out_refs[1][...] = jnp.zeros_like(out_refs[1])

        out_refs[1][...] += dg

    gain_spec = pl.BlockSpec((1, D_MODEL), lambda i, j, k: (0, 0))
    dx2, d_g_mlp = _matmul(
        "mm_dhm", dup, w_up, dims=NT, grid=(T // tl, 1, N_SHARD),
        a_spec=kblk, b_spec=pl.BlockSpec((None, D_MODEL, D_MODEL), lambda i, j, k: (k, 0, 0)),
        acc_shape=(tl, D_MODEL), extra=[(x2, row_big), (g_mlp, gain_spec), (dy, row_big)],
        outs=[(jax.ShapeDtypeStruct((T, D_MODEL), F32), row_big), (jax.ShapeDtypeStruct((1, D_MODEL), F32), gain_spec)],
        epilogue=epi_rms_bwd)

    (dmerged,) = _matmul(
        "mm_dmerged", dx2, w_out, dims=NT, grid=(T // tl, 1, 1),
        a_spec=row_big, b_spec=whole,
        acc_shape=(tl, D_MODEL), outs=[(jax.ShapeDtypeStruct((T, D_MODEL), F32), row_big)], epilogue=_epi_store)
    (d_w_out,) = _matmul(
        "mm_dw_out", merged, dx2, dims=TN, grid=(1, 1, nkt),
        a_spec=t_rows, b_spec=t_cols, acc_shape=(D_MODEL, D_MODEL),
        outs=[(jax.ShapeDtypeStruct((D_MODEL, D_MODEL), F32), whole)],
        epilogue=_epi_store)
    (dgl, do_swa, do_fox, do_mem, dl_swa, do_fox_aug, dl_mem, d_wo_swa, d_wo_fox, d_wo_mem, d_b_gate) = _merge_bwd(
        proj, b_gate, o3, w_o3, dmerged, T, min(512, T))

    dqm, dmk, dmv = _mem_bwd(qm, mk, mv, do_mem, lse_mem, dl_mem, T, min(MEM_TQ, T))
    d_w_kv, d_g_mem, d_kn_mem = _mem_prep_bwd(mem, g_mem, memn, kv, w_kv, kn_mem, gm128, dmk, dmv)
    do_swa = reducer.early_start({"w_mlp_down": d_w_down, "w_mlp_up": d_w_up, "w_out": d_w_out, "w_mem_kv": d_w_kv,
                                  "w_o_swa": d_wo_swa, "w_o_fox": d_wo_fox, "w_o_mem": d_wo_mem}, do_swa)
    dqa, dkad, dvad, dbias, dsk = _swa_bwd(sinks, qa, kad, vad, bias, do_swa, lse_swa, dl_swa, T)
    dqa, do_fox = reducer.early_send((dqa, do_fox))
    dqf, dqf_aug, dkf, dkf_aug, dvf = _fox_bwd(qf, qf_aug_bwd, kf, kf_aug, vf, do_fox, do_fox_aug, T,
                                               min(FOX_BWD_TQ, T), min(FOX_BWD_TK, T))
    dvf = reducer.early_finish(dvf)
    d_rel = _swa_bias_bwd(dbias, bucket)
    dlo, gacc = _prep_bwd(proj, dqa, dkad, dvad, dqf, dkf, dvf, dqm, dqf_aug, dkf_aug, gains, bfor, triu, gm64, gm128,
                          T, tb_prep)

    def dwc_half(name, dpart):
        (res,) = _matmul(
            name, h, dpart, dims=TN, grid=(1, LO_W // D_MODEL, nkt),
            a_spec=t_rows, b_spec=t_cols, acc_shape=(D_MODEL, D_MODEL),
            outs=[(jax.ShapeDtypeStruct((D_MODEL, LO_W), F32), pl.BlockSpec((D_MODEL, D_MODEL), lambda i, j, k: (0, j)))],
            epilogue=_epi_store)
        return res

    d_wc_lo = dwc_half("mm_dwc_lo", dlo)
    d_wc_gl = dwc_half("mm_dwc_gl", dgl)
    dlo = reducer.late_start({"wc_lo": d_wc_lo, "wc_gl": d_wc_gl}, dlo)
    (dh_lo,) = _matmul(
        "mm_dh_lo", dlo, wc, dims=NT, grid=(T // tl, 1, LO_W // D_MODEL),
        a_spec=kblk, b_spec=pl.BlockSpec((D_MODEL, D_MODEL), lambda i, j, k: (0, k)),
        acc_shape=(tl, D_MODEL), outs=[(jax.ShapeDtypeStruct((T, D_MODEL), F32), row_big)], epilogue=_epi_store)
    dh_lo = reducer.late_send(dh_lo)

    def epi_dx(acc, extra_refs, out_refs, ij):
        dhh = acc + extra_refs[3][...]
        dx, dg = _rmsnorm_bwd_rows(dhh, extra_refs[0][...], extra_refs[1][...])
        out_refs[0][...] = dx + extra_refs[2][...]

        @pl.when(ij[0] == 0)
        def _():
            out_refs[1][...] = jnp.zeros_like(out_refs[1])

        out_refs[1][...] += dg

    grad_x, d_g_mix = _matmul(
        "mm_dh_gl", dgl, wc, dims=NT, grid=(T // tl, 1, GATE_W // D_MODEL),
        a_spec=kblk, b_spec=pl.BlockSpec((D_MODEL, D_MODEL), lambda i, j, k: (0, k + LO_W // D_MODEL)),
        acc_shape=(tl, D_MODEL), extra=[(x, row_big), (g_mix, gain_spec), (dx2, row_big), (dh_lo, row_big)],
        outs=[(jax.ShapeDtypeStruct((T, D_MODEL), F32), row_big), (jax.ShapeDtypeStruct((1, D_MODEL), F32), gain_spec)],
        epilogue=epi_dx, vmem=VMEM_MAX)

    fold64 = lambda row: (row[:64] + row[64:]).reshape(1, 64)
    grads = {
        "g_mix": d_g_mix, "b_gate": d_b_gate, "b_forget": gacc[5, :FOX_HEADS].reshape(1, FOX_HEADS),
        "qn_swa": fold64(gacc[0]), "kn_swa": fold64(gacc[1]),
        "sink_swa": -dsk[:, :SWA_GROUP, 0].reshape(1, SWA_HEADS), "rel_bias": d_rel[:, :SWA_HEADS],
        "qn_fox": fold64(gacc[2]), "kn_fox": fold64(gacc[3]),
        "g_mem": d_g_mem, "qn_mem": gacc[4].reshape(1, LANES), "kn_mem": d_kn_mem, "g_mlp": d_g_mlp,
    }
    return loss, grad_x, grads


MESH = pl.DeviceIdType.MESH


def _place():
    x, y, c = lax.axis_index("x"), lax.axis_index("y"), lax.axis_index("c")
    chips = [(1 - x, y), (x, 1 - y), (1 - x, 1 - y)]
    return x, y, c, chips


def _handshake(peers):
    barrier = pltpu.get_barrier_semaphore()
    for peer in peers:
        pl.semaphore_signal(barrier, inc=1, device_id=peer, device_id_type=MESH)
    pl.semaphore_wait(barrier, len(peers))


def _all_gather_shards_async(name, collective_id, slots):
    n = len(slots)
    bufs = [jax.new_ref(s, memory_space=pltpu.MemorySpace.HBM) for s in slots]

    def body(ici_send, ici_recv, d2d_send, d2d_recv):
        x, y, c, chips = _place()
        sibling = (x, y, 1 - c)
        me = 2 * x + y
        _handshake([(px, py, c) for px, py in chips] + [sibling])

        def half(a, who):
            hr = slots[a].shape[1] // 2
            return pl.ds(pl.multiple_of(who * hr, hr), hr)

        def ici(a, j, slot, to):
            return pltpu.make_async_remote_copy(
                src_ref=bufs[a].at[me, half(a, c)], dst_ref=bufs[a].at[slot, half(a, c)],
                send_sem=ici_send.at[3 * a + j], recv_sem=ici_recv.at[3 * a + j], device_id=to, device_id_type=MESH)

        def d2d(a, j, slot, which):
            part = bufs[a].at[slot, half(a, which)]
            return pltpu.make_async_remote_copy(
                src_ref=part, dst_ref=part, send_sem=d2d_send.at[3 * a + j], recv_sem=d2d_recv.at[3 * a + j],
                device_id=sibling, device_id_type=MESH)

        sends = [ici(a, j, me, (*chip, c)) for a in range(n) for j, chip in enumerate(chips)]
        for cp in sends:
            cp.start()
        passed = []
        for a in range(n):
            for j, (px, py) in enumerate(chips):
                ici(a, j, 2 * px + py, (px, py, c)).wait_recv()
                cp = d2d(a, j, 2 * px + py, c)
                cp.start()
                passed.append(cp)
        for a in range(n):
            for j, (px, py) in enumerate(chips):
                d2d(a, j, 2 * px + py, 1 - c).wait_recv()
        for cp in sends + passed:
            cp.wait_send()

    pl.kernel(
        body, mesh=plsc.ScalarSubcoreMesh(axis_name="seq", num_cores=1), name=name,
        scratch_types=[pltpu.SemaphoreType.DMA((3 * n,))] * 4,
        compiler_params=pltpu.CompilerParams(collective_id=collective_id),
    )()
    return [b[...] for b in bufs]


def _sequencer_call(name, collective_id, n_sems, body):
    pl.kernel(
        body, mesh=plsc.ScalarSubcoreMesh(axis_name="seq", num_cores=1), name=name,
        scratch_types=[pltpu.SemaphoreType.DMA((n_sems,))] * 2,
        compiler_params=pltpu.CompilerParams(collective_id=collective_id),
    )()


def _hbm_ref(value):
    return jax.new_ref(value, memory_space=pltpu.MemorySpace.HBM)


def _pair_exchange(name, collective_id, gs):
    n = len(gs)
    src = [_hbm_ref(g) for g in gs]
    stage = [jax.empty_ref(jax.ShapeDtypeStruct((N_SHARD, g.shape[1] // 2, g.shape[2]), g.dtype),
                           memory_space=pltpu.MemorySpace.HBM) for g in gs]

    def body(send_sem, recv_sem):
        x, y, c, _ = _place()
        sibling = (x, y, 1 - c)
        _handshake([sibling])
        copies = []
        for a in range(n):
            hr = gs[a].shape[1] // 2
            theirs = pl.ds(pl.multiple_of((1 - c) * hr, hr), hr)
            copies.append(pltpu.make_async_remote_copy(
                src_ref=src[a].at[:, theirs, :], dst_ref=stage[a], send_sem=send_sem.at[a], recv_sem=recv_sem.at[a],
                device_id=sibling, device_id_type=MESH))
        for cp in copies:
            cp.start()
        for cp in copies:
            cp.wait()

    _sequencer_call(name, collective_id, n, body)
    return [s[...] for s in stage]


def _chip_exchange(name, collective_id, sums):
    n = len(sums)
    src = [_hbm_ref(s) for s in sums]
    got = [jax.empty_ref(jax.ShapeDtypeStruct((3,) + s.shape[1:], s.dtype), memory_space=pltpu.MemorySpace.HBM)
           for s in sums]

    def body(send_sem, recv_sem):
        x, y, c, chips = _place()
        _handshake([(px, py, c) for px, py in chips])
        copies = []
        for a in range(n):
            for j, (px, py) in enumerate(chips):
                copies.append(pltpu.make_async_remote_copy(
                    src_ref=src[a].at[2 * px + py], dst_ref=got[a].at[j],
                    send_sem=send_sem.at[3 * a + j], recv_sem=recv_sem.at[3 * a + j],
                    device_id=(px, py, c), device_id_type=MESH))
        for cp in copies:
            cp.start()
        for cp in copies:
            cp.wait()

    _sequencer_call(name, collective_id, 3 * n, body)
    return [g[...] for g in got]


def _pair_gather(name, collective_id, fulls):
    n = len(fulls)
    full = [_hbm_ref(f) for f in fulls]

    def body(send_sem, recv_sem):
        x, y, c, _ = _place()
        sibling = (x, y, 1 - c)
        _handshake([sibling])
        copies = []
        for a in range(n):
            hr = fulls[a].shape[0] // 2
            mine = full[a].at[pl.ds(pl.multiple_of(c * hr, hr), hr)]
            copies.append(pltpu.make_async_remote_copy(
                src_ref=mine, dst_ref=mine, send_sem=send_sem.at[a], recv_sem=recv_sem.at[a],
                device_id=sibling, device_id_type=MESH))
        for cp in copies:
            cp.start()
        for cp in copies:
            cp.wait()

    _sequencer_call(name, collective_id, n, body)
    return [f[...] for f in full]


ELEMENTWISE_BLOCK_ELEMS = 512 * 1024


def _row_block(rows, cols):
    rb = 8
    while rb * 2 * cols <= ELEMENTWISE_BLOCK_ELEMS and rb * 2 <= rows:
        rb *= 2
    return rb


def _pair_sum(name, place, g, stage):
    _, R, C = g.shape
    hr = R // 2
    rb = _row_block(hr, C)
    nb = hr // rb

    def body(place_ref, g_ref, st_ref, sum_bf, own_f32):
        s = pl.program_id(1)
        tot = g_ref[...] + st_ref[...]
        sum_bf[...] = tot.astype(BF16)

        @pl.when(s == place_ref[0])
        def _():
            own_f32[...] = tot

    return pl.pallas_call(
        body, name=name,
        grid_spec=pltpu.PrefetchScalarGridSpec(
            num_scalar_prefetch=1, grid=(nb, N_SHARD),
            in_specs=[pl.BlockSpec((None, rb, C), lambda i, s, pr: (s, pr[1] * nb + i, 0)),
                      pl.BlockSpec((None, rb, C), lambda i, s, pr: (s, i, 0))],
            out_specs=[pl.BlockSpec((None, rb, C), lambda i, s, pr: (s, i, 0)),
                       pl.BlockSpec((rb, C), lambda i, s, pr: (i, 0))]),
        out_shape=[jax.ShapeDtypeStruct((N_SHARD, hr, C), BF16), jax.ShapeDtypeStruct((hr, C), F32)],
        compiler_params=_cparams("arbitrary", "arbitrary"),
    )(place, g, stage)


def _final_sum(name, place, own, got):
    hr, C = own.shape
    rb = _row_block(hr, C)
    nb = hr // rb

    def body(place_ref, own_ref, got_ref, o_ref):
        o_ref[...] = ((own_ref[...] + got_ref[0].astype(F32)) + got_ref[1].astype(F32)) + got_ref[2].astype(F32)

    return pl.pallas_call(
        body, name=name,
        grid_spec=pltpu.PrefetchScalarGridSpec(
            num_scalar_prefetch=1, grid=(nb,),
            in_specs=[pl.BlockSpec((rb, C), lambda i, pr: (i, 0)), pl.BlockSpec((3, rb, C), lambda i, pr: (0, i, 0))],
            out_specs=pl.BlockSpec((rb, C), lambda i, pr: (pr[1] * nb + i, 0))),
        out_shape=jax.ShapeDtypeStruct((2 * hr, C), F32),
        compiler_params=_cparams("arbitrary"),
    )(place, own, got)


def _adamw_math(w, g, m, v):
    m = ADAM_B1 * m + (1.0 - ADAM_B1) * g
    v = ADAM_B2 * v + (1.0 - ADAM_B2) * (g * g)
    m_hat = m / (1.0 - ADAM_B1 ** ADAM_STEP)
    v_hat = v / (1.0 - ADAM_B2 ** ADAM_STEP)
    delta = -ADAM_LR * (m_hat / (jnp.sqrt(v_hat) + ADAM_EPS) + ADAM_WD * w)
    return delta, m, v


def _adamw(name, w, g, m, v):
    R, Cw = w.shape
    Cg = g.shape[1]
    rb = _row_block(R, Cg)

    def body(w_ref, g_ref, m_ref, v_ref, g_o, d_o, m_o, v_o):
        gv = g_ref[...]
        delta, mn, vn = _adamw_math(w_ref[...], gv, m_ref[...], v_ref[...])
        g_o[...] = gv
        d_o[...] = delta
        m_o[...] = mn
        v_o[...] = vn

    blk = pl.BlockSpec((rb, Cg), lambda i: (i, 0))
    return pl.pallas_call(
        body, name=name, grid=(R // rb,),
        in_specs=[blk] * 4, out_specs=[blk] * 4,
        out_shape=[jax.ShapeDtypeStruct((R, Cw), F32)] * 4,
        compiler_params=_cparams("parallel"),
    )(w, g, m, v)


N_DEV = 8
SMALL_ROWS = 64


def _small_allreduce_adamw(g, w, m, v):
    def body(g_ref, w_ref, m_ref, v_ref, all_ref, gs_o, d_o, m_o, v_o, send_sems, recv_sems, local_sem):
        x, y, c, chips = _place()
        me, sibling = (x, y, c), (x, y, 1 - c)

        def rows(px, py, pc):
            return all_ref.at[pl.ds(pl.multiple_of((4 * px + 2 * py + pc) * SMALL_ROWS, SMALL_ROWS), SMALL_ROWS), :]

        def copy(k, block, to, src=None):
            return pltpu.make_async_remote_copy(
                src_ref=rows(*block) if src is None else src, dst_ref=rows(*block),
                send_sem=send_sems.at[k], recv_sem=recv_sems.at[k], device_id=to, device_id_type=MESH)

        mine = pltpu.make_async_copy(g_ref, rows(*me), local_sem)
        mine.start()
        first = [copy(0, me, sibling, src=g_ref)]
        first += [copy(1 + j, me, (*chip, c), src=g_ref) for j, chip in enumerate(chips)]
        for cp in first:
            cp.start()
        passed = [copy(4 + j, (*chip, c), sibling) for j, chip in enumerate(chips)]
        for j, chip in enumerate(chips):
            copy(1 + j, (*chip, c), me).wait_recv()
            passed[j].start()
        copy(0, sibling, me).wait_recv()
        for j, chip in enumerate(chips):
            copy(4 + j, (*chip, 1 - c), me).wait_recv()
        for cp in first + passed:
            cp.wait_send()
        mine.wait()

        tot = all_ref[0:SMALL_ROWS, :]
        for d in range(1, N_DEV):
            tot = tot + all_ref[d * SMALL_ROWS:(d + 1) * SMALL_ROWS, :]
        delta, mn, vn = _adamw_math(w_ref[...], tot, m_ref[...], v_ref[...])
        gs_o[...] = tot
        d_o[...] = delta
        m_o[...] = mn
        v_o[...] = vn

    vm = pl.BlockSpec(memory_space=pltpu.VMEM)
    shp = jax.ShapeDtypeStruct((SMALL_ROWS, LANES), F32)
    res = pl.pallas_call(
        body, name="small_allreduce_adamw", in_specs=[vm] * 4, out_specs=[vm] * 5,
        out_shape=[jax.ShapeDtypeStruct((N_DEV * SMALL_ROWS, LANES), F32), shp, shp, shp, shp],
        scratch_shapes=[pltpu.SemaphoreType.DMA((7,)), pltpu.SemaphoreType.DMA((7,)), pltpu.SemaphoreType.DMA],
    )(g, w, m, v)
    return res[1:]


SMALL_NAMES = ("g_mix", "b_gate", "b_forget", "qn_swa", "kn_swa", "sink_swa", "rel_bias", "qn_fox", "kn_fox",
               "g_mem", "qn_mem", "kn_mem", "g_mlp")
BIG_NAMES = ("w_in", "w_mem_kv", "w_o_swa", "w_o_fox", "w_o_mem", "w_out", "w_mlp_up", "w_mlp_down")
WEIGHT_NAMES = ("g_mix", "w_in", "b_gate", "b_forget", "qn_swa", "kn_swa", "sink_swa", "rel_bias", "qn_fox", "kn_fox",
                "g_mem", "w_mem_kv", "qn_mem", "kn_mem", "w_o_swa", "w_o_fox", "w_o_mem", "w_out", "g_mlp",
                "w_mlp_up", "w_mlp_down")


def _pack_small(parts, extra=None):
    rows = []
    for n in SMALL_NAMES:
        flat = parts[n].reshape(-1).astype(F32)
        flat = jnp.pad(flat, (0, (-flat.size) % LANES))
        rows.append(flat.reshape(-1, LANES))
    if extra is not None:
        rows.append(jnp.pad(extra.reshape(1, 1), ((0, 0), (0, LANES - 1))))
    packed = jnp.concatenate(rows, axis=0)
    return jnp.pad(packed, ((0, SMALL_ROWS - packed.shape[0]), (0, 0)))


def _unpack_small(packed, shapes):
    out, r = {}, 0
    for n in SMALL_NAMES:
        size = math.prod(shapes[n])
        nr = -(-size // LANES)
        out[n] = packed[r:r + nr].reshape(-1)[:size].reshape(shapes[n])
        r += nr
    return out, packed[r, 0]


W_IN_SEGMENTS = ((C_QA, 0, 512), (C_QF, 768, 512), (C_KF, 1280, 512), (C_VF, 1792, 512), (C_QM, 2312, 512),
                 (C_KA, 512, 128), (C_VA, 640, 128), (C_FL, 2304, FOX_HEADS), (C_GL, 2824, GATE_W))
RELAYOUT_ROWS = 256


def _permute_pieces(src_of_dst):
    blocks = []
    for b in range(len(src_of_dst) // LANES):
        runs, lane = [], 0
        while lane < LANES:
            src = src_of_dst[b * LANES + lane]
            if src is None:
                lane += 1
                continue
            plane, col = src
            end = lane + 1
            while (end < LANES and src_of_dst[b * LANES + end] == (plane, col + end - lane)
                   and (col + end - lane) // LANES == col // LANES):
                end += 1
            runs.append((plane, col // LANES, (lane - col) % LANES, lane, end))
            lane = end
        blocks.append(runs)
    return blocks


def _permuted_block(runs, load, rows):
    lane = _lane((rows, LANES))
    acc = jnp.zeros((rows, LANES), F32)
    for plane, blk, shift, lo, hi in runs:
        x = load(plane, blk).astype(F32)
        if shift:
            x = pltpu.roll(x, shift, 1)
        acc = x if (lo, hi) == (0, LANES) else jnp.where((lane >= lo) & (lane < hi), x, acc)
    return acc


def _w_in_to_segments(g_in):
    src_of_dst = [None] * PROJ_W
    for mine, theirs, width in W_IN_SEGMENTS:
        for k in range(width):
            src_of_dst[mine + k] = ((theirs + k) // IN_SHARD, (theirs + k) % IN_SHARD)
    blocks = _permute_pieces(src_of_dst)
    rb = RELAYOUT_ROWS

    def body(src_ref, out_ref):
        for b, runs in enumerate(blocks):
            blk = _permuted_block(runs, lambda p, c: src_ref[p, :, c * LANES:(c + 1) * LANES], rb)
            out_ref[:, b * LANES:(b + 1) * LANES] = blk.astype(out_ref.dtype)

    return pl.pallas_call(
        body, name="w_in_to_segments", grid=(D_MODEL // rb,),
        in_specs=[pl.BlockSpec((N_SHARD, rb, IN_SHARD_PAD), lambda i: (0, i, 0))],
        out_specs=pl.BlockSpec((rb, PROJ_W), lambda i: (i, 0)),
        out_shape=jax.ShapeDtypeStruct((D_MODEL, PROJ_W), g_in.dtype),
        compiler_params=_cparams("parallel", vmem=VMEM_MID),
    )(g_in)


def _w_in_from_segments(lo, gl):
    mine_of_theirs = {}
    for mine, theirs, width in W_IN_SEGMENTS:
        for k in range(width):
            mine_of_theirs[theirs + k] = mine + k
    src_of_dst = [None] * (N_SHARD * IN_SHARD_PAD)
    for s in range(N_SHARD):
        for l in range(IN_SHARD):
            j = mine_of_theirs[s * IN_SHARD + l]
            src_of_dst[s * IN_SHARD_PAD + l] = (j // LO_W, j % LO_W)
    blocks = _permute_pieces(src_of_dst)
    per_slot = IN_SHARD_PAD // LANES
    rb = RELAYOUT_ROWS

    def body(lo_ref, gl_ref, out_ref):
        planes = (lo_ref, gl_ref)
        for b, runs in enumerate(blocks):
            blk = _permuted_block(runs, lambda p, c: planes[p][:, c * LANES:(c + 1) * LANES], rb)
            c0 = (b % per_slot) * LANES
            out_ref[b // per_slot, :, c0:c0 + LANES] = blk

    half = pl.BlockSpec((rb, LO_W), lambda i: (i, 0))
    return pl.pallas_call(
        body, name="w_in_from_segments", grid=(D_MODEL // rb,),
        in_specs=[half, half],
        out_specs=pl.BlockSpec((N_SHARD, rb, IN_SHARD_PAD), lambda i: (0, i, 0)),
        out_shape=jax.ShapeDtypeStruct((N_SHARD, D_MODEL, IN_SHARD_PAD), F32),
        compiler_params=_cparams("parallel", vmem=VMEM_MID),
    )(lo, gl)


def _after(first, then):
    return lax.optimization_barrier((first, then))


class _ReduceGroup:
    def __init__(self, tag, first_collective_id, place):
        self.tag, self.first_id, self.place = tag, first_collective_id, place

    def start(self, local, tie):
        self.names = tuple(local)
        mine, tie = _after([local[n] for n in self.names], tie)
        self.mine = mine
        self.staged = _pair_exchange("pair_exchange_" + self.tag, self.first_id, mine)
        return tie

    def send(self, tie):
        staged, tie = _after(self.staged, tie)
        sums = [_pair_sum("pair_sum_" + n, self.place, g, st) for n, g, st in zip(self.names, self.mine, staged)]
        travel, tie = _after([s[0] for s in sums], tie)
        self.own = [s[1] for s in sums]
        self.got = _chip_exchange("chip_exchange_" + self.tag, self.first_id + 1, travel)
        return tie

    def finish(self, tie):
        got, tie = _after(self.got, tie)
        halves = [_final_sum("final_sum_" + n, self.place, o, r) for n, o, r in zip(self.names, self.own, got)]
        halves, tie = _after(halves, tie)
        summed = _pair_gather("pair_gather_" + self.tag, self.first_id + 2, halves)
        self.summed = dict(zip(self.names, summed))
        return tie


class _GradReducer:
    def __init__(self, place):
        self.early = _ReduceGroup("early", 2, place)
        self.late = _ReduceGroup("late", 5, place)

    @staticmethod
    def _slot_rows(a):
        return a.reshape(N_SHARD, a.shape[0] // N_SHARD, a.shape[1])

    def early_start(self, g, tie):
        return self.early.start({"w_mlp_down": self._slot_rows(g["w_mlp_down"]), "w_mlp_up": g["w_mlp_up"],
                                 "w_out": self._slot_rows(g["w_out"]), "w_mem_kv": self._slot_rows(g["w_mem_kv"]),
                                 "w_o_swa": g["w_o_swa"], "w_o_fox": g["w_o_fox"], "w_o_mem": g["w_o_mem"]}, tie)

    def early_send(self, tie):
        return self.early.send(tie)

    def early_finish(self, tie):
        return self.early.finish(tie)

    def late_start(self, g, tie):
        d_in = _w_in_from_segments(g["wc_lo"], g["wc_gl"])
        return self.late.start({"w_in": d_in}, tie)

    def late_send(self, tie):
        return self.late.send(tie)

    def late_finish(self, tie):
        return self.late.finish(tie)

    @property
    def summed(self):
        return {**self.early.summed, **self.late.summed}


def kernel(x, mem, g_mix, w_in, b_gate, b_forget, qn_swa, kn_swa, sink_swa, rel_bias, qn_fox, kn_fox, g_mem, w_mem_kv, qn_mem, kn_mem, w_o_swa, w_o_fox, w_o_mem, w_out, g_mlp, w_mlp_up, w_mlp_down, loss_target, m_g_mix, m_w_in, m_b_gate, m_b_forget, m_qn_swa, m_kn_swa, m_sink_swa, m_rel_bias, m_qn_fox, m_kn_fox, m_g_mem, m_w_mem_kv, m_qn_mem, m_kn_mem, m_w_o_swa, m_w_o_fox, m_w_o_mem, m_w_out, m_g_mlp, m_w_mlp_up, m_w_mlp_down, v_g_mix, v_w_in, v_b_gate, v_b_forget, v_qn_swa, v_kn_swa, v_sink_swa, v_rel_bias, v_qn_fox, v_kn_fox, v_g_mem, v_w_mem_kv, v_qn_mem, v_kn_mem, v_w_o_swa, v_w_o_fox, v_w_o_mem, v_w_out, v_g_mlp, v_w_mlp_up, v_w_mlp_down):
    given = dict(locals())
    W = {n: given[n] for n in WEIGHT_NAMES}
    M = {n: given["m_" + n] for n in WEIGHT_NAMES}
    V = {n: given["v_" + n] for n in WEIGHT_NAMES}
    pad_in = ((0, 0), (0, IN_SHARD_PAD - IN_SHARD))

    shards = [jnp.pad(w_in[0].astype(BF16), pad_in)] + [W[n][0].astype(BF16) for n in BIG_NAMES[1:]]
    slots = [jnp.broadcast_to(s[None], (N_SHARD,) + s.shape) for s in shards]
    (g_in,) = _all_gather_shards_async("all_gather_w_in", 1, slots[:1])
    small = {n: (W[n] if n == "rel_bias" else W[n].reshape(1, -1)) for n in SMALL_NAMES}
    h = _rmsnorm("rms_mix", x[0], small["g_mix"], min(512, x.shape[1]))
    g_in, late, h, (m_in, v_in) = lax.optimization_barrier((g_in, slots[1:], h, (M["w_in"][0], V["w_in"][0])))
    M["w_in"], V["w_in"] = m_in[None], v_in[None]
    g_kv, g_oa, g_of, g_om, g_out, g_up, g_down = _all_gather_shards_async("all_gather_weights_async", 8, late)

    place = jnp.stack([2 * lax.axis_index("x") + lax.axis_index("y"), lax.axis_index("c")]).astype(jnp.int32)
    reducer = _GradReducer(place)
    loss, grad_x, grads = _local_step(
        x[0], h, mem[0], loss_target[0], small, g_in, g_kv.reshape(D_MODEL, D_MODEL), (g_oa, g_of, g_om),
        g_out.reshape(D_MODEL, D_MODEL), g_up, g_down.reshape(D_FF, D_MODEL), reducer)

    out = {}

    def adamw_of(names, summed):
        for n in names:
            res = _adamw("adamw_" + n, W[n][0], summed[n], M[n][0], V[n][0])
            out[n] = [r.reshape(W[n].shape) for r in res]

    adamw_of(reducer.early.names, reducer.early.summed)
    shapes = {n: W[n].shape for n in SMALL_NAMES}
    packed = _small_allreduce_adamw(_pack_small(grads, loss), _pack_small(W), _pack_small(M), _pack_small(V))
    done_meanwhile = ([out[n] for n in reducer.early.names], packed)
    (early_out, packed), grad_x = reducer.late_finish((done_meanwhile, grad_x))
    for n, res in zip(reducer.early.names, early_out):
        out[n] = res
    adamw_of(reducer.late.names, reducer.late.summed)
    unpacked = [_unpack_small(p, shapes) for p in packed]
    for n in SMALL_NAMES:
        out[n] = [u[0][n] for u in unpacked]
    loss_total = unpacked[0][1]

    return (loss_total, grad_x.reshape(x.shape),
            *[out[n][0] for n in WEIGHT_NAMES], *[out[n][1] for n in WEIGHT_NAMES],
            *[out[n][2] for n in WEIGHT_NAMES], *[out[n][3] for n in WEIGHT_NAMES])
```

```python
import math

import jax
import jax.numpy as jnp
from jax import lax
from jax.experimental import pallas as pl
from jax.experimental.pallas import tpu as pltpu
from jax.experimental.pallas import tpu_sc as plsc

F32 = jnp.float32
BF16 = jnp.bfloat16

D_MODEL = 1024
N_MEM = 256
SWA_HEADS = 8
SWA_KV_HEADS = 2
SWA_HEAD_DIM = 64
WINDOW = 128
FOX_HEADS = 8
FOX_HEAD_DIM = 64
MEM_HEADS = 4
MEM_HEAD_DIM = 128
D_FF = 4 * D_MODEL
REL_BUCKETS = 32
REL_MAX_DIST = 128
EPS = 1e-6
NEG = -1e30
GATE_W = 3 * D_MODEL
IN_WIDTH = 5896
N_SHARD = 4
IN_SHARD = IN_WIDTH // N_SHARD
IN_SHARD_PAD = 1536

ADAM_LR = 0.001
ADAM_B1 = 0.9
ADAM_B2 = 0.999
ADAM_EPS = 1e-08
ADAM_WD = 0.01
ADAM_STEP = 10

LANES = 128
V7X_VMEM_BYTES = 64 * 1024 * 1024
VMEM_SMALL = VMEM_MID = VMEM_BIG = V7X_VMEM_BYTES * 3 // 4
VMEM_MAX = V7X_VMEM_BYTES * 7 // 8

C_QA, C_QF, C_KF, C_VF, C_QM, C_KA, C_VA, C_FL, C_GL = 0, 512, 1024, 1536, 2048, 2560, 2688, 2816, 3072
LO_W = 3072
PROJ_W = 6144
PROJ_TN = 2048

NN = (((1,), (0,)), ((), ()))
NT = (((1,), (1,)), ((), ()))
TN = (((0,), (0,)), ((), ()))


def _dot(a, b, dims=NN):
    return lax.dot_general(a, b, dims, preferred_element_type=F32)


def _cparams(*sem, vmem=VMEM_SMALL):
    return pltpu.CompilerParams(dimension_semantics=sem, vmem_limit_bytes=vmem)


def _split3(a):
    hi = a.astype(BF16)
    r1 = a - hi.astype(F32)
    mid = r1.astype(BF16)
    lo = (r1 - mid.astype(F32)).astype(BF16)
    return hi, mid, lo


def _group_mean(a, g2):
    hi = a.astype(BF16)
    mid = (a - hi.astype(F32)).astype(BF16)
    return _dot(jnp.concatenate([hi, mid], axis=1), g2)


def _dot3_left(g, a):
    hi, mid, lo = _split3(a)
    return _dot(g, hi) + _dot(g, mid) + _dot(g, lo)


def _group_mean_matrix(d):
    r = jnp.arange(LANES)
    g = jnp.where((r[:, None] // d) == (r[None, :] // d), 1.0 / d, 0.0).astype(BF16)
    return jnp.concatenate([g, g], axis=0)


def _lane(shape):
    return lax.broadcasted_iota(jnp.int32, shape, len(shape) - 1)


def _matmul(name, a, b, *, dims, grid, a_spec, b_spec, acc_shape, outs, epilogue, extra=(), vmem=VMEM_BIG):
    nk = grid[2]
    n_extra = len(extra)

    def body(a_ref, b_ref, *rest):
        extra_refs = rest[:n_extra]
        out_refs = rest[n_extra:n_extra + len(outs)]
        i, j, k = pl.program_id(0), pl.program_id(1), pl.program_id(2)
        if nk == 1:
            epilogue(_dot(a_ref[...].astype(BF16), b_ref[...].astype(BF16), dims), extra_refs, out_refs, (i, j))
            return
        acc_ref = rest[-1]

        @pl.when(k == 0)
        def _():
            acc_ref[...] = jnp.zeros_like(acc_ref)

        acc_ref[...] += _dot(a_ref[...].astype(BF16), b_ref[...].astype(BF16), dims)

        @pl.when(k == nk - 1)
        def _():
            epilogue(acc_ref[...], extra_refs, out_refs, (i, j))

    res = pl.pallas_call(
        body,
        name=name,
        grid=grid,
        in_specs=[a_spec, b_spec] + [s for _, s in extra],
        out_specs=[s for _, s in outs],
        out_shape=[s for s, _ in outs],
        scratch_shapes=[pltpu.VMEM(acc_shape, F32)] if nk > 1 else [],
        compiler_params=_cparams("arbitrary", "arbitrary", "arbitrary", vmem=vmem),
    )(a, b, *[x for x, _ in extra])
    return res


def _epi_store(acc, extra_refs, out_refs, ij):
    out_refs[0][...] = acc.astype(out_refs[0].dtype)


def _rms_rows(x, g):
    r = lax.rsqrt(jnp.mean(x * x, axis=-1, keepdims=True) + EPS)
    return x * r, r


def _rmsnorm_bwd_rows(dh, x, g):
    xhat, r = _rms_rows(x, g)
    dxh = dh * g
    dx = r * (dxh - xhat * jnp.mean(dxh * xhat, axis=-1, keepdims=True))
    return dx, jnp.sum(dh * xhat, axis=0, keepdims=True)


def _rmsnorm(name, x, g, tb):
    T, Dm = x.shape

    def body(x_ref, g_ref, o_ref):
        xhat, _ = _rms_rows(x_ref[...], None)
        o_ref[...] = (xhat * g_ref[...]).astype(o_ref.dtype)

    return pl.pallas_call(
        body, name=name, grid=(T // tb,),
        in_specs=[pl.BlockSpec((tb, Dm), lambda i: (i, 0)), pl.BlockSpec((1, Dm), lambda i: (0, 0))],
        out_specs=pl.BlockSpec((tb, Dm), lambda i: (i, 0)),
        out_shape=jax.ShapeDtypeStruct((T, Dm), BF16),
        compiler_params=_cparams("parallel"),
    )(x, g)


def _head_norm(x, gm, gain):
    ms = _group_mean(x * x, gm)
    r = lax.rsqrt(ms + EPS)
    return x * r * gain, x * r


def _head_norm_bwd(dy, x, gm, gain):
    ms = _group_mean(x * x, gm)
    r = lax.rsqrt(ms + EPS)
    xhat = x * r
    dxh = dy * gain
    dx = r * (dxh - xhat * _group_mean(dxh * xhat, gm))
    return dx, jnp.sum(dy * xhat, axis=0, keepdims=True)


def _log_sigmoid(z):
    return jnp.minimum(z, 0.0) - jnp.log(1.0 + jnp.exp(-jnp.abs(z)))


def _prep_fwd(proj, gains, bfor, tril, gm64, gm128, T, tb):
    nb = T // tb

    def body(qa_ref, qf_ref, kf_ref, vf_ref, qm_ref, ka_ref, va_ref, fl_ref, gains_ref, bfor_ref, tril_ref,
             gm64_ref, gm128_ref,
             qa_o, qf_o, kf_o, vf_o, qm_o, kad_o, vad_o, qaug_o, kaug_o, carry):
        i = pl.program_id(0)
        gm64v = gm64_ref[...]
        gm128v = gm128_ref[...]
        lane = _lane((tb, LANES))

        def norm512(src, dst, row, gm, scale=1.0):
            gain = gains_ref[row:row + 1, :]
            for c in range(4):
                sl = slice(c * LANES, (c + 1) * LANES)
                y, _ = _head_norm(src[:, sl], gm, gain)
                dst[:, sl] = (y * scale).astype(dst.dtype)

        norm512(qa_ref, qa_o, 0, gm64v)
        norm512(qf_ref, qf_o, 2, gm64v, FOX_SCALE)
        norm512(kf_ref, kf_o, 3, gm64v)
        norm512(qm_ref, qm_o, 4, gm128v)
        vf_o[...] = vf_ref[...].astype(vf_o.dtype)

        ka_n, _ = _head_norm(ka_ref[...], gm64v, gains_ref[1:2, :])
        ka_r = pltpu.roll(ka_n, 64, 1)
        va = va_ref[...]
        va_r = pltpu.roll(va, 64, 1)
        lo = lane < 64
        kad_o[0] = jnp.where(lo, ka_n, ka_r).astype(kad_o.dtype)
        kad_o[1] = jnp.where(lo, ka_r, ka_n).astype(kad_o.dtype)
        vad_o[0] = jnp.where(lo, va, va_r).astype(vad_o.dtype)
        vad_o[1] = jnp.where(lo, va_r, va).astype(vad_o.dtype)

        @pl.when(i == 0)
        def _():
            carry[...] = jnp.zeros_like(carry)

        logf = jnp.where(lane < FOX_HEADS, _log_sigmoid(fl_ref[...] + bfor_ref[...]), 0.0)
        c = _dot3_left(tril_ref[...], logf) + carry[0:1, :]
        carry[...] = jnp.broadcast_to(c[tb - 1:tb, :], carry.shape)
        for pair in range(FOX_HEADS // 2):
            qaug = jnp.zeros((tb, LANES), F32)
            kaug = jnp.zeros((tb, LANES), F32)
            for sub in range(2):
                col = jnp.sum(jnp.where(lane == 2 * pair + sub, c, 0.0), axis=1, keepdims=True)
                pieces = [p.astype(F32) for p in _split3(col)]
                base = AUG_STRIDE * sub
                for e in range(3):
                    qaug = jnp.where(lane == base + AUG_C + e, pieces[e], qaug)
                    kaug = jnp.where(lane == base + AUG_NEG_C + e, -pieces[e], kaug)
                qaug = jnp.where((lane >= base + AUG_NEG_C) & (lane < base + AUG_NEG_C + 3), 1.0, qaug)
                ones_k = ((lane >= base + AUG_C) & (lane < base + AUG_C + 3)) | (
                    (lane >= base + AUG_STAT) & (lane < base + AUG_STAT + 3))
                kaug = jnp.where(ones_k, 1.0, kaug)
            sl = slice(pair * LANES, (pair + 1) * LANES)
            qaug_o[:, sl] = qaug.astype(BF16)
            kaug_o[:, sl] = kaug.astype(BF16)

    def seg(width, start):
        return pl.BlockSpec((tb, width), lambda i, s=start // width: (i, s))

    const = lambda shape: pl.BlockSpec(shape, lambda i: tuple(0 for _ in shape))
    rows512 = pl.BlockSpec((tb, 512), lambda i: (i, 0))
    outs = pl.pallas_call(
        body, name="prep_fwd", grid=(nb,),
        in_specs=[seg(512, C_QA), seg(512, C_QF), seg(512, C_KF), seg(512, C_VF), seg(512, C_QM),
                  seg(128, C_KA), seg(128, C_VA), seg(128, C_FL),
                  const((8, LANES)), const((1, LANES)), const((tb, tb)), const((2 * LANES, LANES)), const((2 * LANES, LANES))],
        out_specs=[rows512, rows512, rows512, rows512, rows512,
                   pl.BlockSpec((2, tb, LANES), lambda i: (0, i, 0)), pl.BlockSpec((2, tb, LANES), lambda i: (0, i, 0)),
                   rows512, rows512],
        out_shape=[jax.ShapeDtypeStruct((T, 512), BF16)] * 5
        + [jax.ShapeDtypeStruct((2, T, LANES), BF16)] * 2
        + [jax.ShapeDtypeStruct((T, 512), BF16)] * 2,
        scratch_shapes=[pltpu.VMEM((8, LANES), F32)],
        compiler_params=_cparams("arbitrary", vmem=VMEM_MID),
    )(proj, proj, proj, proj, proj, proj, proj, proj, gains, bfor, tril, gm64, gm128)
    return outs


def _prep_bwd(proj, dqa, dkad, dvad, dqf, dkf, dvf, dqm, dqf_aug, dkf_aug, gains, bfor, triu, gm64, gm128, T, tb):
    nb = T // tb

    def body(qa_ref, qf_ref, kf_ref, qm_ref, ka_ref, fl_ref,
             dqa_ref, dkad_ref, dvad_ref, dqf_ref, dkf_ref, dvf_ref, dqm_ref, dqfa_ref, dkfa_ref,
             gains_ref, bfor_ref, triu_ref, gm64_ref, gm128_ref,
             dlo_o, gacc_o, carry):
        i = pl.program_id(0)
        gm64v = gm64_ref[...]
        gm128v = gm128_ref[...]
        lane = _lane((tb, LANES))

        @pl.when(i == 0)
        def _():
            carry[...] = jnp.zeros_like(carry)
            gacc_o[...] = jnp.zeros_like(gacc_o)

        def norm512_bwd(dsrc, xsrc, col0, row, gm):
            gain = gains_ref[row:row + 1, :]
            gsum = jnp.zeros((1, LANES), F32)
            for c in range(4):
                sl = slice(c * LANES, (c + 1) * LANES)
                dx, dg = _head_norm_bwd(dsrc[:, sl], xsrc[:, sl], gm, gain)
                dlo_o[:, col0 + c * LANES:col0 + (c + 1) * LANES] = dx.astype(dlo_o.dtype)
                gsum = gsum + dg
            gacc_o[row:row + 1, :] += gsum

        norm512_bwd(dqa_ref, qa_ref, C_QA, 0, gm64v)
        norm512_bwd(dqf_ref, qf_ref, C_QF, 2, gm64v)
        norm512_bwd(dkf_ref, kf_ref, C_KF, 3, gm64v)
        norm512_bwd(dqm_ref, qm_ref, C_QM, 4, gm128v)
        dlo_o[:, C_VF:C_VF + 512] = dvf_ref[...].astype(dlo_o.dtype)

        lo = lane < 64

        def fold(ref):
            f0 = ref[0] + pltpu.roll(ref[0], 64, 1)
            f1 = ref[1] + pltpu.roll(ref[1], 64, 1)
            return jnp.where(lo, f0, f1)

        dka, dg = _head_norm_bwd(fold(dkad_ref), ka_ref[...], gm64v, gains_ref[1:2, :])
        gacc_o[1:2, :] += dg
        dlo_o[:, C_KA:C_KA + LANES] = dka.astype(dlo_o.dtype)
        dlo_o[:, C_VA:C_VA + LANES] = fold(dvad_ref).astype(dlo_o.dtype)

        dc = jnp.zeros((tb, LANES), F32)
        for pair in range(FOX_HEADS // 2):
            sl = slice(pair * LANES, (pair + 1) * LANES)
            rows_sum, cols_sum = dqfa_ref[:, sl], dkfa_ref[:, sl]
            for sub in range(2):
                diff = (jnp.where(lane == AUG_STRIDE * sub + AUG_C, rows_sum, 0.0)
                        - jnp.where(lane == AUG_STRIDE * sub + AUG_NEG_C, cols_sum, 0.0))
                dc = jnp.where(lane == 2 * pair + sub, jnp.sum(diff, axis=1, keepdims=True), dc)
        dlogf = _dot3_left(triu_ref[...], dc) + carry[0:1, :]
        carry[...] = jnp.broadcast_to(dlogf[0:1, :], carry.shape)
        z = fl_ref[...] + bfor_ref[...]
        dfl = jnp.where(lane < FOX_HEADS, dlogf / (1.0 + jnp.exp(z)), 0.0)
        gacc_o[5:6, :] += jnp.sum(dfl, axis=0, keepdims=True)
        dlo_o[:, C_FL:C_FL + LANES] = dfl.astype(dlo_o.dtype)
        dlo_o[:, C_FL + LANES:C_FL + 2 * LANES] = jnp.zeros((tb, LANES), dlo_o.dtype)

    rev = lambda i: nb - 1 - i

    def seg(width, start):
        return pl.BlockSpec((tb, width), lambda i, s=start // width: (rev(i), s))

    const = lambda shape: pl.BlockSpec(shape, lambda i: tuple(0 for _ in shape))
    rows512 = pl.BlockSpec((tb, 512), lambda i: (rev(i), 0))
    dup = pl.BlockSpec((2, tb, LANES), lambda i: (0, rev(i), 0))
    return pl.pallas_call(
        body, name="prep_bwd", grid=(nb,),
        in_specs=[seg(512, C_QA), seg(512, C_QF), seg(512, C_KF), seg(512, C_QM), seg(128, C_KA), seg(128, C_FL),
                  rows512, dup, dup, rows512, rows512, rows512, rows512, rows512, rows512,
                  const((8, LANES)), const((1, LANES)), const((tb, tb)), const((2 * LANES, LANES)), const((2 * LANES, LANES))],
        out_specs=[pl.BlockSpec((tb, LO_W), lambda i: (rev(i), 0)), const((8, LANES))],
        out_shape=[jax.ShapeDtypeStruct((T, LO_W), BF16), jax.ShapeDtypeStruct((8, LANES), F32)],
        scratch_shapes=[pltpu.VMEM((8, LANES), F32)],
        compiler_params=_cparams("arbitrary", vmem=VMEM_MID),
    )(proj, proj, proj, proj, proj, proj, dqa, dkad, dvad, dqf, dkf, dvf, dqm, dqf_aug, dkf_aug,
      gains, bfor, triu, gm64, gm128)


FOX_SCALE = FOX_HEAD_DIM ** -0.5
AUG_STRIDE = 16
AUG_C = 0
AUG_NEG_C = 3
AUG_STAT = 6
FOX_TQ, FOX_TK = 1024, 1024
FOX_BWD_TQ, FOX_BWD_TK = 1024, 1024
FOX_DIAGONAL_PARTS = 4


def _fox_head_mask(sub, rows):
    lane = _lane((rows, 2 * LANES))
    main = (lane >= 64 * sub) & (lane < 64 * sub + 64)
    aug = (lane >= LANES + AUG_STRIDE * sub) & (lane < LANES + AUG_STRIDE * (sub + 1))
    return main | aug


def _fox_pieces(diagonal, tq, tk):
    if diagonal and tq == tk and tq >= FOX_DIAGONAL_PARTS * LANES:
        step = tq // FOX_DIAGONAL_PARTS
        return [(n * step, (n + 1) * step, (n + 1) * step) for n in range(FOX_DIAGONAL_PARTS)]
    return [(0, tq, tk)]


def _fox_fwd(q, qaug, k, kaug, v, T, tq, tk):
    nq, nk = T // tq, T // tk
    rep = tk // LANES
    last_of = lambda i: (i * tq + tq - 1) // tk

    def body(q_ref, qa_ref, k_ref, ka_ref, v_ref, o_ref, qab_ref, m_s, acc_s):
        p_, i, j = pl.program_id(0), pl.program_id(1), pl.program_id(2)
        last = last_of(i)

        @pl.when(j == 0)
        def _():
            m_s[...] = jnp.full(m_s.shape, NEG, F32)
            acc_s[...] = jnp.zeros_like(acc_s)

        def step(diagonal):
            k2 = jnp.concatenate([k_ref[...], ka_ref[...]], axis=1)
            v2 = jnp.concatenate([v_ref[...], ka_ref[...]], axis=1)
            pieces = _fox_pieces(diagonal, tq, tk)
            work = []
            for r0, r1, nc in pieces:
                rows = slice(r0, r1)
                q2 = jnp.concatenate([q_ref[rows, :], qa_ref[rows, :]], axis=1)
                for sub in range(2):
                    qh = jnp.where(_fox_head_mask(sub, r1 - r0), q2, jnp.zeros_like(q2))
                    work.append((rows, r0, r1 - r0, nc, sub, _dot(qh, k2[:nc], NT)))
            for rows, r0, nr, nc, sub, s in work:
                if diagonal:
                    causal = (lax.broadcasted_iota(jnp.int32, (nr, nc), 1) + j * tk
                              <= lax.broadcasted_iota(jnp.int32, (nr, nc), 0) + (r0 + i * tq))
                    s = jnp.where(causal, s, NEG)
                m_prev = m_s[sub, rows, :]
                m_next = jnp.maximum(m_prev, jnp.max(s, axis=1, keepdims=True))
                p = jnp.exp(s - jnp.tile(m_next, (1, nc // LANES)))
                alpha = jnp.exp(m_prev - m_next)
                m_s[sub, rows, :] = m_next
                acc_s[sub, rows, :] = acc_s[sub, rows, :] * jnp.tile(alpha, (1, 2)) + _dot(p.astype(BF16), v2[:nc])

        @pl.when(j == last)
        def _():
            step(True)

        @pl.when(j < last)
        def _():
            step(False)

        @pl.when(j == nk - 1)
        def _():
            lane = _lane((tq, LANES))
            outs = []
            qab = qa_ref[...].astype(F32)
            for sub in range(2):
                acc = acc_s[sub]
                base = AUG_STRIDE * sub
                l = jnp.sum(jnp.where(lane == base + AUG_C, acc[:, LANES:], 0.0), axis=1, keepdims=True)
                outs.append(acc[:, :LANES] / l)
                lse = jnp.max(m_s[sub], axis=1, keepdims=True) + jnp.log(l)
                pieces = _split3(-lse)
                for e in range(3):
                    qab = jnp.where(lane == base + AUG_STAT + e, pieces[e].astype(F32), qab)
            o_ref[...] = jnp.where(lane < 64, outs[0], outs[1]).astype(o_ref.dtype)
            qab_ref[...] = qab.astype(BF16)

    qspec = pl.BlockSpec((tq, LANES), lambda p, i, j: (i, p))
    kspec = pl.BlockSpec((tk, LANES), lambda p, i, j: (jnp.minimum(j, last_of(i)), p))
    return pl.pallas_call(
        body, name="fox_fwd", grid=(4, nq, nk),
        in_specs=[qspec, qspec, kspec, kspec, kspec],
        out_specs=[qspec, qspec],
        out_shape=[jax.ShapeDtypeStruct((T, 512), BF16), jax.ShapeDtypeStruct((T, 512), BF16)],
        scratch_shapes=[pltpu.VMEM((2, tq, LANES), F32), pltpu.VMEM((2, tq, 2 * LANES), F32)],
        compiler_params=_cparams("parallel", "parallel", "arbitrary", vmem=VMEM_BIG),
    )(q, qaug, k, kaug, v)


def _fox_bwd(q, qaug, k, kaug, v, do, doaug, T, tq, tk):
    nq, nk = T // tq, T // tk
    first_of = lambda j: (j * tk) // tq

    def body(q_ref, qa_ref, k_ref, ka_ref, v_ref, do_ref, doa_ref,
             dq_ref, dqa_ref, dk_ref, dka_ref, dv_ref, dk_s, dv_s):
        p_, j, i = pl.program_id(0), pl.program_id(1), pl.program_id(2)
        masked = i * tq < (j + 1) * tk - 1

        @pl.when((j == 0) & (i == 0))
        def _():
            dq_ref[...] = jnp.zeros_like(dq_ref)
            dqa_ref[...] = jnp.zeros_like(dqa_ref)

        @pl.when(i == 0)
        def _():
            dk_s[...] = jnp.zeros_like(dk_s)
            dv_s[...] = jnp.zeros_like(dv_s)

        def step(diagonal):
            k2 = jnp.concatenate([k_ref[...], ka_ref[...]], axis=1)
            v2 = jnp.concatenate([v_ref[...], ka_ref[...]], axis=1)
            work = []
            for r0, r1, nc in _fox_pieces(diagonal, tq, tk):
                rows = slice(r0, r1)
                q2 = jnp.concatenate([q_ref[rows, :], qa_ref[rows, :]], axis=1)
                do2 = jnp.concatenate([do_ref[rows, :], doa_ref[rows, :]], axis=1)
                for sub in range(2):
                    hm = _fox_head_mask(sub, r1 - r0)
                    qh = jnp.where(hm, q2, jnp.zeros_like(q2))
                    doh = jnp.where(hm, do2, jnp.zeros_like(do2))
                    s = _dot(qh, k2[:nc], NT)
                    dp = _dot(doh, v2[:nc], NT)
                    work.append((r0, r1 - r0, nc, sub, qh, doh, s, dp))
            dqs = {}
            for r0, nr, nc, sub, qh, doh, s, dp in work:
                if diagonal:
                    causal = (lax.broadcasted_iota(jnp.int32, (nr, nc), 1) + j * tk
                              <= lax.broadcasted_iota(jnp.int32, (nr, nc), 0) + (r0 + i * tq))
                    s = jnp.where(causal, s, NEG)
                p = jnp.exp(s)
                dsb = (p * dp).astype(BF16)
                dv_s[0:nc, :] += _dot(p.astype(BF16), doh[:, :LANES], TN)
                dk_s[0:nc, :] += _dot(dsb, qh, TN)
                dqs[(r0, sub)] = _dot(dsb, k2[:nc])
            for r0, r1, nc in _fox_pieces(diagonal, tq, tk):
                dq2 = jnp.where(_fox_head_mask(0, r1 - r0), dqs[(r0, 0)], dqs[(r0, 1)])
                qrows = pl.ds(pl.multiple_of(i * tq + r0, r1 - r0), r1 - r0)
                dq_ref[qrows, :] += dq2[:, :LANES] * FOX_SCALE
                dqa_ref[qrows, :] += dq2[:, LANES:]

        @pl.when((i >= first_of(j)) & masked)
        def _():
            step(True)

        @pl.when((i >= first_of(j)) & jnp.logical_not(masked))
        def _():
            step(False)

        @pl.when(i == nq - 1)
        def _():
            dk_ref[...] = dk_s[:, :LANES]
            dka_ref[...] = dk_s[:, LANES:]
            dv_ref[...] = dv_s[...]

    qspec = pl.BlockSpec((tq, LANES), lambda p, j, i: (jnp.maximum(i, first_of(j)), p))
    kspec = pl.BlockSpec((tk, LANES), lambda p, j, i: (j, p))
    resident = pl.BlockSpec((T, LANES), lambda p, j, i: (0, p))
    return pl.pallas_call(
        body, name="fox_bwd", grid=(4, nk, nq),
        in_specs=[qspec, qspec, kspec, kspec, kspec, qspec, qspec],
        out_specs=[resident, resident, kspec, kspec, kspec],
        out_shape=[jax.ShapeDtypeStruct((T, 512), F32)] * 5,
        scratch_shapes=[pltpu.VMEM((tk, 2 * LANES), F32), pltpu.VMEM((tk, LANES), F32)],
        compiler_params=_cparams("arbitrary", "arbitrary", "arbitrary", vmem=VMEM_BIG),
    )(q, qaug, k, kaug, v, do, doaug)


SWA_SUB = 16
SWA_TB = SWA_SUB * WINDOW


def _t5_bucket_matrix():
    t = jnp.arange(WINDOW)[:, None] + WINDOW
    s = jnp.arange(2 * WINDOW)[None, :]
    max_exact = REL_BUCKETS // 2
    d = jnp.maximum(t - s, 0)
    df = jnp.maximum(d, 1).astype(F32)
    large = max_exact + (jnp.log(df / max_exact) / math.log(REL_MAX_DIST / max_exact)
                         * (REL_BUCKETS - max_exact)).astype(jnp.int32)
    large = jnp.minimum(large, REL_BUCKETS - 1)
    return jnp.where(d < max_exact, d, large).astype(jnp.int32)


def _swa_bias(rel_bias, bucket):
    def body(rel_ref, bucket_ref, o_ref):
        b = bucket_ref[...]
        for h in range(SWA_HEADS):
            acc = jnp.zeros(b.shape, F32)
            for r in range(REL_BUCKETS):
                acc = jnp.where(b == r, rel_ref[r, h], acc)
            o_ref[h] = acc

    return pl.pallas_call(
        body, name="swa_bias",
        in_specs=[pl.BlockSpec(memory_space=pltpu.SMEM), pl.BlockSpec(memory_space=pltpu.VMEM)],
        out_specs=pl.BlockSpec(memory_space=pltpu.VMEM),
        out_shape=jax.ShapeDtypeStruct((SWA_HEADS, WINDOW, 2 * WINDOW), F32),
    )(rel_bias, bucket)


def _swa_bias_bwd(dbias, bucket):
    def body(db_ref, bucket_ref, o_ref):
        b = bucket_ref[...]
        lane = _lane((1, LANES))
        for r in range(REL_BUCKETS):
            row = jnp.zeros((1, LANES), F32)
            for h in range(SWA_HEADS):
                part = jnp.sum(jnp.where(b == r, db_ref[h], 0.0), axis=0, keepdims=True)
                tot = jnp.sum(part, axis=1, keepdims=True)
                row = jnp.where(lane == h, tot, row)
            o_ref[r:r + 1, :] = row

    return pl.pallas_call(
        body, name="swa_bias_bwd",
        in_specs=[pl.BlockSpec(memory_space=pltpu.VMEM), pl.BlockSpec(memory_space=pltpu.VMEM)],
        out_specs=pl.BlockSpec(memory_space=pltpu.VMEM),
        out_shape=jax.ShapeDtypeStruct((REL_BUCKETS, LANES), F32),
    )(dbias, bucket)


SWA_GROUP = SWA_HEADS // SWA_KV_HEADS


def _swa_valid(r, i):
    t = (lax.broadcasted_iota(jnp.int32, (SWA_GROUP * WINDOW, 2 * WINDOW), 0) & (WINDOW - 1)) + WINDOW
    s = lax.broadcasted_iota(jnp.int32, (SWA_GROUP * WINDOW, 2 * WINDOW), 1)
    dist = t - s
    band = (dist >= 0) & (dist < WINDOW)
    if r == 0:
        band = band & ((s >= WINDOW) | (i > 0))
    return band


def _swa_stack(blk):
    lane = _lane((WINDOW, LANES))
    parts = []
    for g in range(SWA_GROUP):
        b = blk[:, LANES * (g // 2):LANES * (g // 2 + 1)]
        parts.append(jnp.where((lane >= 64) if g % 2 else (lane < 64), b, jnp.zeros_like(b)))
    return jnp.concatenate(parts, axis=0)


def _swa_unstack(st):
    lane = _lane((WINDOW, LANES))
    W = WINDOW
    return jnp.concatenate([jnp.where(lane < 64, st[2 * b * W:(2 * b + 1) * W], st[(2 * b + 1) * W:(2 * b + 2) * W])
                            for b in range(2)], axis=1)


def _swa_sink_column(sink_ref, kvh):
    row = lax.broadcasted_iota(jnp.int32, (SWA_GROUP * WINDOW, 1), 0)
    col = jnp.full((SWA_GROUP * WINDOW, 1), sink_ref[SWA_GROUP * kvh + SWA_GROUP - 1], F32)
    for g in range(SWA_GROUP - 2, -1, -1):
        col = jnp.where(row < (g + 1) * WINDOW, sink_ref[SWA_GROUP * kvh + g], col)
    return col


def _swa_specs(T):
    W = WINDOW
    qspec = pl.BlockSpec((SWA_TB, 2 * LANES), lambda h, i: (i, h))
    own = pl.BlockSpec((None, SWA_TB, LANES), lambda h, i: (h, i, 0))
    prev = pl.BlockSpec((None, W, LANES), lambda h, i: (h, jnp.maximum(SWA_SUB * i - 1, 0), 0))
    stat = pl.BlockSpec((SWA_GROUP, SWA_TB, LANES), lambda h, i: (h, i, 0))
    bias = pl.BlockSpec((None, SWA_GROUP * W, 2 * W), lambda h, i: (h, 0, 0))
    return qspec, own, prev, stat, bias


def _swa_fwd(sinks, q, kad, vad, bias, T):
    nb = T // SWA_TB
    scale = SWA_HEAD_DIM ** -0.5
    W = WINDOW

    def body(sink_ref, q_ref, k_ref, kp_ref, v_ref, vp_ref, bias_ref, o_ref, lse_ref):
        kvh, i = pl.program_id(0), pl.program_id(1)
        sink = _swa_sink_column(sink_ref, kvh)
        for r in range(SWA_SUB):
            rs = slice(r * W, (r + 1) * W)
            ps = slice((r - 1) * W, r * W)
            k_own, v_own = k_ref[rs, :], v_ref[rs, :]
            k_prev = kp_ref[...] if r == 0 else k_ref[ps, :]
            v_prev = vp_ref[...] if r == 0 else v_ref[ps, :]
            qs = _swa_stack(q_ref[rs, :])
            s = jnp.concatenate([_dot(qs, k_prev, NT), _dot(qs, k_own, NT)], axis=1) * scale + bias_ref[...]
            s = jnp.where(_swa_valid(r, i), s, NEG)
            m = jnp.maximum(jnp.max(s, axis=1, keepdims=True), sink)
            p = jnp.exp(s - m)
            denom = jnp.sum(p, axis=1, keepdims=True) + jnp.exp(sink - m)
            pn = (p / denom).astype(BF16)
            o_ref[rs, :] = _swa_unstack(_dot(pn[:, :W], v_prev) + _dot(pn[:, W:], v_own)).astype(o_ref.dtype)
            lse = m + jnp.log(denom)
            for g in range(SWA_GROUP):
                lse_ref[g, rs, :] = jnp.broadcast_to(lse[g * W:(g + 1) * W], (W, LANES))

    qspec, own, prev, stat, bspec = _swa_specs(T)
    return pl.pallas_call(
        body, name="swa_fwd", grid=(SWA_KV_HEADS, nb),
        in_specs=[pl.BlockSpec(memory_space=pltpu.SMEM), qspec, own, prev, own, prev, bspec],
        out_specs=[qspec, stat],
        out_shape=[jax.ShapeDtypeStruct((T, 512), BF16), jax.ShapeDtypeStruct((SWA_HEADS, T, LANES), F32)],
        compiler_params=_cparams("parallel", "parallel", vmem=VMEM_MID),
    )(sinks, q, kad, kad, vad, vad, bias.reshape(SWA_KV_HEADS, SWA_GROUP * W, 2 * W))


def _swa_bwd(sinks, q, kad, vad, bias, do, lse, delta, T):
    nb = T // SWA_TB
    scale = SWA_HEAD_DIM ** -0.5
    W = WINDOW

    def body(sink_ref, q_ref, k_ref, kp_ref, v_ref, vp_ref, bias_ref, do_ref, lse_ref, dl_ref,
             dq_ref, dkad_ref, dvad_ref, dbias_ref, dsk_ref):
        kvh, i = pl.program_id(0), pl.program_id(1)
        sink = _swa_sink_column(sink_ref, kvh)

        @pl.when((kvh == 0) & (i == 0))
        def _():
            dkad_ref[...] = jnp.zeros_like(dkad_ref)
            dvad_ref[...] = jnp.zeros_like(dvad_ref)

        @pl.when(i == 0)
        def _():
            dbias_ref[...] = jnp.zeros_like(dbias_ref)
            dsk_ref[...] = jnp.zeros_like(dsk_ref)

        for r in range(SWA_SUB):
            rs = slice(r * W, (r + 1) * W)
            ps = slice((r - 1) * W, r * W)
            k_own, v_own = k_ref[rs, :], v_ref[rs, :]
            k_prev = kp_ref[...] if r == 0 else k_ref[ps, :]
            v_prev = vp_ref[...] if r == 0 else v_ref[ps, :]
            qs = _swa_stack(q_ref[rs, :])
            dos = _swa_stack(do_ref[rs, :])
            lse_b = jnp.concatenate([lse_ref[g, rs, :] for g in range(SWA_GROUP)], axis=0)
            dl_b = jnp.concatenate([dl_ref[g, rs, :] for g in range(SWA_GROUP)], axis=0)
            s = jnp.concatenate([_dot(qs, k_prev, NT), _dot(qs, k_own, NT)], axis=1) * scale + bias_ref[...]
            s = jnp.where(_swa_valid(r, i), s, NEG)
            p = jnp.exp(s - jnp.tile(lse_b, (1, 2)))
            dp = jnp.concatenate([_dot(dos, v_prev, NT), _dot(dos, v_own, NT)], axis=1)
            ds = p * (dp - jnp.tile(dl_b, (1, 2)))
            sink_term = jnp.exp(sink - lse_b) * dl_b
            for g in range(SWA_GROUP):
                dbias_ref[g] += ds[g * W:(g + 1) * W]
                dsk_ref[g:g + 1, :] += jnp.sum(sink_term[g * W:(g + 1) * W], axis=0, keepdims=True)
            dsb = ds.astype(BF16)
            pb = p.astype(BF16)
            dq_ref[rs, :] = _swa_unstack((_dot(dsb[:, :W], k_prev) + _dot(dsb[:, W:], k_own)) * scale)
            own_row = pl.multiple_of(i * SWA_TB + r * W, W)
            dkad_ref[kvh, pl.ds(own_row, W), :] += _dot(dsb[:, W:], qs, TN) * scale
            dvad_ref[kvh, pl.ds(own_row, W), :] += _dot(pb[:, W:], dos, TN)
            dk_prev = _dot(dsb[:, :W], qs, TN) * scale
            dv_prev = _dot(pb[:, :W], dos, TN)
            if r == 0:
                @pl.when(i > 0)
                def _():
                    prev_row = pl.multiple_of(i * SWA_TB - W, W)
                    dkad_ref[kvh, pl.ds(prev_row, W), :] += dk_prev
                    dvad_ref[kvh, pl.ds(prev_row, W), :] += dv_prev
            else:
                prev_row = pl.multiple_of(i * SWA_TB + (r - 1) * W, W)
                dkad_ref[kvh, pl.ds(prev_row, W), :] += dk_prev
                dvad_ref[kvh, pl.ds(prev_row, W), :] += dv_prev

    qspec, own, prev, stat, bspec = _swa_specs(T)
    full = pl.BlockSpec((SWA_KV_HEADS, T, LANES), lambda h, i: (0, 0, 0))
    return pl.pallas_call(
        body, name="swa_bwd", grid=(SWA_KV_HEADS, nb),
        in_specs=[pl.BlockSpec(memory_space=pltpu.SMEM), qspec, own, prev, own, prev, bspec, qspec, stat, stat],
        out_specs=[qspec, full, full, pl.BlockSpec((SWA_GROUP, W, 2 * W), lambda h, i: (h, 0, 0)),
                   pl.BlockSpec((None, 8, LANES), lambda h, i: (h, 0, 0))],
        out_shape=[jax.ShapeDtypeStruct((T, 512), F32), jax.ShapeDtypeStruct((SWA_KV_HEADS, T, LANES), F32),
                   jax.ShapeDtypeStruct((SWA_KV_HEADS, T, LANES), F32), jax.ShapeDtypeStruct((SWA_HEADS, W, 2 * W), F32),
                   jax.ShapeDtypeStruct((SWA_KV_HEADS, 8, LANES), F32)],
        compiler_params=_cparams("arbitrary", "arbitrary", vmem=VMEM_MID),
    )(sinks, q, kad, kad, vad, vad, bias.reshape(SWA_KV_HEADS, SWA_GROUP * W, 2 * W), do, lse, delta)


MEM_TQ = 4096


def _mem_fwd(q, mk, mv, T, tq):
    scale = MEM_HEAD_DIM ** -0.5

    def body(q_ref, k_ref, v_ref, o_ref, lse_ref):
        s = _dot(q_ref[...], k_ref[...], NT) * scale
        m = jnp.max(s, axis=1, keepdims=True)
        p = jnp.exp(s - m)
        l = jnp.sum(p, axis=1, keepdims=True)
        o_ref[...] = _dot((p / l).astype(BF16), v_ref[...]).astype(o_ref.dtype)
        lse_ref[...] = jnp.broadcast_to(m + jnp.log(l), (tq, LANES))

    qspec = pl.BlockSpec((tq, LANES), lambda h, i: (i, h))
    kspec = pl.BlockSpec((N_MEM, LANES), lambda h, i: (0, h))
    return pl.pallas_call(
        body, name="mem_fwd", grid=(MEM_HEADS, T // tq),
        in_specs=[qspec, kspec, kspec],
        out_specs=[qspec, pl.BlockSpec((None, tq, LANES), lambda h, i: (h, i, 0))],
        out_shape=[jax.ShapeDtypeStruct((T, 512), BF16), jax.ShapeDtypeStruct((MEM_HEADS, T, LANES), F32)],
        compiler_params=_cparams("parallel", "parallel"),
    )(q, mk, mv)


def _mem_bwd(q, mk, mv, do, lse, delta, T, tq):
    scale = MEM_HEAD_DIM ** -0.5
    rep = N_MEM // LANES

    def body(q_ref, k_ref, v_ref, do_ref, lse_ref, dl_ref, dq_ref, dk_ref, dv_ref):
        i = pl.program_id(1)

        @pl.when(i == 0)
        def _():
            dk_ref[...] = jnp.zeros_like(dk_ref)
            dv_ref[...] = jnp.zeros_like(dv_ref)

        qv, dov = q_ref[...], do_ref[...]
        s = _dot(qv, k_ref[...], NT) * scale
        p = jnp.exp(s - jnp.tile(lse_ref[...], (1, rep)))
        dp = _dot(dov, v_ref[...], NT)
        ds = p * (dp - jnp.tile(dl_ref[...], (1, rep)))
        dsb = ds.astype(BF16)
        dq_ref[...] = _dot(dsb, k_ref[...]) * scale
        dk_ref[...] += _dot(dsb, qv, TN) * scale
        dv_ref[...] += _dot(p.astype(BF16), dov, TN)

    qspec = pl.BlockSpec((tq, LANES), lambda h, i: (i, h))
    kspec = pl.BlockSpec((N_MEM, LANES), lambda h, i: (0, h))
    stat = pl.BlockSpec((None, tq, LANES), lambda h, i: (h, i, 0))
    return pl.pallas_call(
        body, name="mem_bwd", grid=(MEM_HEADS, T // tq),
        in_specs=[qspec, kspec, kspec, qspec, stat, stat],
        out_specs=[qspec, kspec, kspec],
        out_shape=[jax.ShapeDtypeStruct((T, 512), F32), jax.ShapeDtypeStruct((N_MEM, 512), F32),
                   jax.ShapeDtypeStruct((N_MEM, 512), F32)],
        compiler_params=_cparams("arbitrary", "arbitrary"),
    )(q, mk, mv, do, lse, delta)


def _mem_prep_fwd(mem, g_mem, w_kv, kn_gain, gm128):
    def body(mem_ref, g_ref, w_ref, kn_ref, gm_ref, memn_o, kv_o, mk_o, mv_o):
        xhat, _ = _rms_rows(mem_ref[...], None)
        memn = (xhat * g_ref[...]).astype(BF16)
        memn_o[...] = memn
        kv = _dot(memn, w_ref[...])
        kv_o[...] = kv
        gm = gm_ref[...]
        for c in range(4):
            sl = slice(c * LANES, (c + 1) * LANES)
            y, _ = _head_norm(kv[:, sl], gm, kn_ref[...])
            mk_o[:, sl] = y.astype(BF16)
        mv_o[...] = kv[:, 512:].astype(BF16)

    vm = pl.BlockSpec(memory_space=pltpu.VMEM)
    return pl.pallas_call(
        body, name="mem_prep_fwd", in_specs=[vm] * 5, out_specs=[vm] * 4,
        out_shape=[jax.ShapeDtypeStruct((N_MEM, D_MODEL), BF16), jax.ShapeDtypeStruct((N_MEM, D_MODEL), F32),
                   jax.ShapeDtypeStruct((N_MEM, 512), BF16), jax.ShapeDtypeStruct((N_MEM, 512), BF16)],
        compiler_params=pltpu.CompilerParams(vmem_limit_bytes=VMEM_MID),
    )(mem, g_mem, w_kv, kn_gain, gm128)


def _mem_prep_bwd(mem, g_mem, memn, kv, w_kv, kn_gain, gm128, dmk, dmv):
    def body(mem_ref, g_ref, memn_ref, kv_ref, w_ref, kn_ref, gm_ref, dmk_ref, dmv_ref, dw_o, dg_o, dkn_o, dkv_s):
        gm = gm_ref[...]
        dkn = jnp.zeros((1, LANES), F32)
        for c in range(4):
            sl = slice(c * LANES, (c + 1) * LANES)
            dx, dg = _head_norm_bwd(dmk_ref[:, sl], kv_ref[:, sl], gm, kn_ref[...])
            dkv_s[:, sl] = dx.astype(BF16)
            dkn = dkn + dg
        dkn_o[...] = dkn
        dkv_s[:, 512:] = dmv_ref[...].astype(BF16)
        dkv = dkv_s[...]
        dw_o[...] = _dot(memn_ref[...], dkv, TN)
        dmemn = _dot(dkv, w_ref[...], NT)
        xhat, _ = _rms_rows(mem_ref[...], None)
        dg_o[...] = jnp.sum(dmemn * xhat, axis=0, keepdims=True)

    vm = pl.BlockSpec(memory_space=pltpu.VMEM)
    return pl.pallas_call(
        body, name="mem_prep_bwd", in_specs=[vm] * 9, out_specs=[vm] * 3,
        out_shape=[jax.ShapeDtypeStruct((D_MODEL, D_MODEL), F32), jax.ShapeDtypeStruct((1, D_MODEL), F32),
                   jax.ShapeDtypeStruct((1, LANES), F32)],
        scratch_shapes=[pltpu.VMEM((N_MEM, D_MODEL), BF16)],
        compiler_params=pltpu.CompilerParams(vmem_limit_bytes=VMEM_MID),
    )(mem, g_mem, memn, kv, w_kv, kn_gain, gm128, dmk, dmv)


SLOT_O = D_MODEL // N_SHARD


def _merge_fwd(proj, b_gate, o3, w3, T, tb):
    def body(gl_ref, bg_ref, oa_ref, of_ref, om_ref, wa_ref, wf_ref, wm_ref, out_ref):
        o_refs = (oa_ref, of_ref, om_ref)
        w_refs = (wa_ref, wf_ref, wm_ref)
        for n in range(N_SHARD):
            acc = jnp.zeros((tb, SLOT_O), F32)
            for b in range(3):
                c0 = b * D_MODEL + n * SLOT_O
                g = jax.nn.sigmoid(gl_ref[:, c0:c0 + SLOT_O] + bg_ref[:, c0:c0 + SLOT_O])
                acc = acc + g * _dot(o_refs[b][...], w_refs[b][n])
            out_ref[:, n * SLOT_O:(n + 1) * SLOT_O] = acc.astype(out_ref.dtype)

    rows = pl.BlockSpec((tb, 512), lambda i: (i, 0))
    wspec = pl.BlockSpec((N_SHARD, 512, SLOT_O), lambda i: (0, 0, 0))
    return pl.pallas_call(
        body, name="merge_fwd", grid=(T // tb,),
        in_specs=[pl.BlockSpec((tb, GATE_W), lambda i: (i, 1)), pl.BlockSpec((1, GATE_W), lambda i: (0, 0)),
                  rows, rows, rows, wspec, wspec, wspec],
        out_specs=pl.BlockSpec((tb, D_MODEL), lambda i: (i, 0)),
        out_shape=jax.ShapeDtypeStruct((T, D_MODEL), BF16),
        compiler_params=_cparams("parallel", vmem=VMEM_BIG),
    )(proj, b_gate, *o3, *w3)


def _merge_bwd(proj, b_gate, o3, w3, dmerged, T, tb):
    heads = (SWA_HEADS, FOX_HEADS, MEM_HEADS)

    def body(gl_ref, bg_ref, oa_ref, of_ref, om_ref, wa_ref, wf_ref, wm_ref, dm_ref,
             dgl_o, doa_o, dof_o, dom_o, dla_o, dlf_o, dlm_o, dwa_o, dwf_o, dwm_o, dbg_o):
        i = pl.program_id(0)
        o_refs = (oa_ref, of_ref, om_ref)
        w_refs = (wa_ref, wf_ref, wm_ref)
        do_refs = (doa_o, dof_o, dom_o)
        dl_refs = (dla_o, dlf_o, dlm_o)
        dw_refs = (dwa_o, dwf_o, dwm_o)

        @pl.when(i == 0)
        def _():
            for r in dw_refs:
                r[...] = jnp.zeros_like(r)
            dbg_o[...] = jnp.zeros_like(dbg_o)

        lane = _lane((tb, LANES))
        for b in range(3):
            ob = o_refs[b][...]
            do = jnp.zeros((tb, 512), F32)
            for n in range(N_SHARD):
                c0 = b * D_MODEL + n * SLOT_O
                g = jax.nn.sigmoid(gl_ref[:, c0:c0 + SLOT_O] + bg_ref[:, c0:c0 + SLOT_O])
                dm = dm_ref[:, n * SLOT_O:(n + 1) * SLOT_O]
                y = _dot(ob, w_refs[b][n])
                dgl = dm * y * g * (1.0 - g)
                dgl_o[:, c0:c0 + SLOT_O] = dgl.astype(dgl_o.dtype)
                dbg_o[:, c0:c0 + SLOT_O] += jnp.sum(dgl, axis=0, keepdims=True)
                dy = (dm * g).astype(BF16)
                do = do + _dot(dy, w_refs[b][n], NT)
                dw_refs[b][n] += _dot(ob, dy, TN)
            do_refs[b][...] = do.astype(BF16)
            prod = do * ob.astype(F32)
            for c in range(4):
                blk = prod[:, c * LANES:(c + 1) * LANES]
                if heads[b] == 8:
                    lo = jnp.sum(jnp.where(lane < 64, blk, 0.0), axis=1, keepdims=True)
                    hi = jnp.sum(jnp.where(lane >= 64, blk, 0.0), axis=1, keepdims=True)
                    if b == 1:
                        aug = jnp.zeros((tb, LANES), F32)
                        for sub, dl in enumerate((lo, hi)):
                            for e, piece in enumerate(_split3(-dl)):
                                aug = jnp.where(lane == AUG_STRIDE * sub + AUG_C + e, piece.astype(F32), aug)
                        dl_refs[b][:, c * LANES:(c + 1) * LANES] = aug.astype(BF16)
                    else:
                        dl_refs[b][2 * c] = jnp.broadcast_to(lo, (tb, LANES))
                        dl_refs[b][2 * c + 1] = jnp.broadcast_to(hi, (tb, LANES))
                else:
                    dl_refs[b][c] = jnp.broadcast_to(jnp.sum(blk, axis=1, keepdims=True), (tb, LANES))

    rows = pl.BlockSpec((tb, 512), lambda i: (i, 0))
    wspec = pl.BlockSpec((N_SHARD, 512, SLOT_O), lambda i: (0, 0, 0))
    stat = lambda h: pl.BlockSpec((h, tb, LANES), lambda i: (0, i, 0))
    return pl.pallas_call(
        body, name="merge_bwd", grid=(T // tb,),
        in_specs=[pl.BlockSpec((tb, GATE_W), lambda i: (i, 1)), pl.BlockSpec((1, GATE_W), lambda i: (0, 0)),
                  rows, rows, rows, wspec, wspec, wspec, pl.BlockSpec((tb, D_MODEL), lambda i: (i, 0))],
        out_specs=[pl.BlockSpec((tb, GATE_W), lambda i: (i, 0)), rows, rows, rows,
                   stat(8), rows, stat(4), wspec, wspec, wspec, pl.BlockSpec((1, GATE_W), lambda i: (0, 0))],
        out_shape=[jax.ShapeDtypeStruct((T, GATE_W), BF16)] + [jax.ShapeDtypeStruct((T, 512), BF16)] * 3
        + [jax.ShapeDtypeStruct((8, T, LANES), F32), jax.ShapeDtypeStruct((T, 512), BF16),
           jax.ShapeDtypeStruct((4, T, LANES), F32)]
        + [jax.ShapeDtypeStruct((N_SHARD, 512, SLOT_O), F32)] * 3 + [jax.ShapeDtypeStruct((1, GATE_W), F32)],
        compiler_params=_cparams("arbitrary", vmem=VMEM_BIG),
    )(proj, b_gate, *o3, *w3, dmerged)


def _local_step(x, h, mem, tgt, small, g_in, w_kv, w_o3, w_out, w_up, w_down, reducer):
    T = x.shape[0]
    tm = min(512, T)
    tile2 = lambda v: jnp.tile(v.reshape(1, -1), (1, LANES // v.size))
    gains = jnp.concatenate([tile2(small["qn_swa"]), tile2(small["kn_swa"]), tile2(small["qn_fox"]),
                             tile2(small["kn_fox"]), tile2(small["qn_mem"]), jnp.zeros((3, LANES), F32)], axis=0)
    kn_mem = small["kn_mem"].reshape(1, LANES)
    bfor = jnp.pad(small["b_forget"].reshape(1, -1), ((0, 0), (0, LANES - FOX_HEADS)))
    gm64 = _group_mean_matrix(64)
    gm128 = _group_mean_matrix(128)
    tb_prep = min(512, T)
    ones = jnp.ones((tb_prep, tb_prep), F32)
    tril = jnp.tril(ones).astype(BF16)
    triu = jnp.triu(ones).astype(BF16)
    bucket = _t5_bucket_matrix()
    g_mix, g_mlp, g_mem = small["g_mix"], small["g_mlp"], small["g_mem"]
    b_gate = small["b_gate"]
    sinks = small["sink_swa"].reshape(-1)

    tl = min(1024, T)
    sq = pl.BlockSpec((tl, D_MODEL), lambda i, j, k: (i, j))
    wc = _w_in_to_segments(g_in)
    (proj,) = _matmul(
        "mm_proj", h, wc, dims=NN, grid=(T // tl, PROJ_W // PROJ_TN, 1),
        a_spec=pl.BlockSpec((tl, D_MODEL), lambda i, j, k: (i, 0)),
        b_spec=pl.BlockSpec((D_MODEL, PROJ_TN), lambda i, j, k: (0, j)),
        acc_shape=(tl, PROJ_TN),
        outs=[(jax.ShapeDtypeStruct((T, PROJ_W), F32), pl.BlockSpec((tl, PROJ_TN), lambda i, j, k: (i, j)))],
        epilogue=_epi_store)
    qa, qf, kf, vf, qm, kad, vad, qf_aug, kf_aug = _prep_fwd(proj, gains, bfor, tril, gm64, gm128, T, tb_prep)
    bias = _swa_bias(small["rel_bias"], bucket)
    o_swa, lse_swa = _swa_fwd(sinks, qa, kad, vad, bias, T)
    o_fox, qf_aug_bwd = _fox_fwd(qf, qf_aug, kf, kf_aug, vf, T, min(FOX_TQ, T), min(FOX_TK, T))
    memn, kv, mk, mv = _mem_prep_fwd(mem, g_mem, w_kv, kn_mem, gm128)
    o_mem, lse_mem = _mem_fwd(qm, mk, mv, T, min(MEM_TQ, T))
    o3 = (o_swa, o_fox, o_mem)
    merged = _merge_fwd(proj, b_gate, o3, w_o3, T, min(512, T))

    def epi_residual(acc, extra_refs, out_refs, ij):
        out_refs[0][...] = extra_refs[0][...] + acc

    row_full = pl.BlockSpec((tm, D_MODEL), lambda i, j, k: (i, 0))
    row_big = pl.BlockSpec((tl, D_MODEL), lambda i, j, k: (i, 0))
    whole = pl.BlockSpec((D_MODEL, D_MODEL), lambda i, j, k: (0, 0))
    (x2,) = _matmul(
        "mm_out", merged, w_out, dims=NN, grid=(T // tl, 1, 1),
        a_spec=row_big, b_spec=whole,
        acc_shape=(tl, D_MODEL), extra=[(x, row_big)],
        outs=[(jax.ShapeDtypeStruct((T, D_MODEL), F32), row_big)], epilogue=epi_residual)
    hm = _rmsnorm("rms_mlp", x2, g_mlp, tl)

    def epi_relu2(acc, extra_refs, out_refs, ij):
        out_refs[0][...] = acc.astype(BF16)
        r = jnp.maximum(acc, 0.0)
        out_refs[1][...] = (r * r).astype(BF16)

    up, u = _matmul(
        "mm_up", hm, w_up, dims=NN, grid=(T // tl, N_SHARD, 1),
        a_spec=row_big, b_spec=pl.BlockSpec((None, D_MODEL, D_MODEL), lambda i, j, k: (j, 0, 0)),
        acc_shape=(tl, D_MODEL),
        outs=[(jax.ShapeDtypeStruct((T, D_FF), BF16), sq), (jax.ShapeDtypeStruct((T, D_FF), BF16), sq)],
        epilogue=epi_relu2)

    def epi_loss(acc, extra_refs, out_refs, ij):
        y = extra_refs[0][...] + acc
        err = y - extra_refs[1][...]
        dyv = err * (1.0 / D_MODEL)
        out_refs[0][...] = dyv
        out_refs[2][...] = dyv.astype(BF16)
        sq = jnp.sum(jnp.sum(err * err, axis=1, keepdims=True), axis=0, keepdims=True)

        @pl.when(ij[0] == 0)
        def _():
            out_refs[1][...] = jnp.zeros_like(out_refs[1])

        out_refs[1][...] += jnp.broadcast_to(sq, out_refs[1].shape)

    kblk = pl.BlockSpec((tl, D_MODEL), lambda i, j, k: (i, k))
    dy, loss_acc, dy_bf = _matmul(
        "mm_down", u, w_down, dims=NN, grid=(T // tl, 1, N_SHARD),
        a_spec=kblk, b_spec=pl.BlockSpec((D_MODEL, D_MODEL), lambda i, j, k: (k, 0)),
        acc_shape=(tl, D_MODEL), extra=[(x2, row_big), (tgt, row_big)],
        outs=[(jax.ShapeDtypeStruct((T, D_MODEL), F32), row_big),
              (jax.ShapeDtypeStruct((8, LANES), F32), pl.BlockSpec((8, LANES), lambda i, j, k: (0, 0))),
              (jax.ShapeDtypeStruct((T, D_MODEL), BF16), row_big)],
        epilogue=epi_loss)
    loss = loss_acc[0, 0] * (0.5 / D_MODEL)

    def epi_dup(acc, extra_refs, out_refs, ij):
        out_refs[0][...] = (acc * (2.0 * jnp.maximum(extra_refs[0][...].astype(F32), 0.0))).astype(BF16)

    (dup,) = _matmul(
        "mm_dup", dy_bf, w_down, dims=NT, grid=(T // tl, N_SHARD, 1),
        a_spec=row_big, b_spec=pl.BlockSpec((D_MODEL, D_MODEL), lambda i, j, k: (j, 0)),
        acc_shape=(tl, D_MODEL), extra=[(up, sq)],
        outs=[(jax.ShapeDtypeStruct((T, D_FF), BF16), sq)], epilogue=epi_dup)

    nkt = T // tl
    t_rows = pl.BlockSpec((tl, D_MODEL), lambda i, j, k: (k, i))
    t_cols = pl.BlockSpec((tl, D_MODEL), lambda i, j, k: (k, j))
    (d_w_down,) = _matmul(
        "mm_dw_down", u, dy_bf, dims=TN, grid=(N_SHARD, 1, nkt),
        a_spec=t_rows, b_spec=t_cols, acc_shape=(D_MODEL, D_MODEL),
        outs=[(jax.ShapeDtypeStruct((D_FF, D_MODEL), F32), pl.BlockSpec((D_MODEL, D_MODEL), lambda i, j, k: (i, 0)))],
        epilogue=_epi_store)
    (d_w_up,) = _matmul(
        "mm_dw_up", hm, dup, dims=TN, grid=(1, N_SHARD, nkt),
        a_spec=t_rows, b_spec=t_cols, acc_shape=(D_MODEL, D_MODEL),
        outs=[(jax.ShapeDtypeStruct((N_SHARD, D_MODEL, D_MODEL), F32),
               pl.BlockSpec((None, D_MODEL, D_MODEL), lambda i, j, k: (j, 0, 0)))],
        epilogue=_epi_store)

    def epi_rms_bwd(acc, extra_refs, out_refs, ij):
        dx, dg = _rmsnorm_bwd_rows(acc, extra_refs[0][...], extra_refs[1][...])
        out_refs[0][...] = dx + extra_refs[2][...]

        @pl.when(ij[0] == 0)
        def _():
            out_refs[1][...] = jnp.zeros_like(out_refs[1])

        out_refs[1][...] += dg

    gain_spec = pl.BlockSpec((1, D_MODEL), lambda i, j, k: (0, 0))
    dx2, d_g_mlp = _matmul(
        "mm_dhm", dup, w_up, dims=NT, grid=(T // tl, 1, N_SHARD),
        a_spec=kblk, b_spec=pl.BlockSpec((None, D_MODEL, D_MODEL), lambda i, j, k: (k, 0, 0)),
        acc_shape=(tl, D_MODEL), extra=[(x2, row_big), (g_mlp, gain_spec), (dy, row_big)],
        outs=[(jax.ShapeDtypeStruct((T, D_MODEL), F32), row_big), (jax.ShapeDtypeStruct((1, D_MODEL), F32), gain_spec)],
        epilogue=epi_rms_bwd)

    (dmerged,) = _matmul(
        "mm_dmerged", dx2, w_out, dims=NT, grid=(T // tl, 1, 1),
        a_spec=row_big, b_spec=whole,
        acc_shape=(tl, D_MODEL), outs=[(jax.ShapeDtypeStruct((T, D_MODEL), F32), row_big)], epilogue=_epi_store)
    (d_w_out,) = _matmul(
        "mm_dw_out", merged, dx2, dims=TN, grid=(1, 1, nkt),
        a_spec=t_rows, b_spec=t_cols, acc_shape=(D_MODEL, D_MODEL),
        outs=[(jax.ShapeDtypeStruct((D_MODEL, D_MODEL), F32), whole)],
        epilogue=_epi_store)
    (dgl, do_swa, do_fox, do_mem, dl_swa, do_fox_aug, dl_mem, d_wo_swa, d_wo_fox, d_wo_mem, d_b_gate) = _merge_bwd(
        proj, b_gate, o3, w_o3, dmerged, T, min(512, T))

    dqm, dmk, dmv = _mem_bwd(qm, mk, mv, do_mem, lse_mem, dl_mem, T, min(MEM_TQ, T))
    d_w_kv, d_g_mem, d_kn_mem = _mem_prep_bwd(mem, g_mem, memn, kv, w_kv, kn_mem, gm128, dmk, dmv)
    do_swa = reducer.early_start({"w_mlp_down": d_w_down, "w_mlp_up": d_w_up, "w_out": d_w_out, "w_mem_kv": d_w_kv,
                                  "w_o_swa": d_wo_swa, "w_o_fox": d_wo_fox, "w_o_mem": d_wo_mem}, do_swa)
    dqa, dkad, dvad, dbias, dsk = _swa_bwd(sinks, qa, kad, vad, bias, do_swa, lse_swa, dl_swa, T)
    dqa, do_fox = reducer.early_send((dqa, do_fox))
    dqf, dqf_aug, dkf, dkf_aug, dvf = _fox_bwd(qf, qf_aug_bwd, kf, kf_aug, vf, do_fox, do_fox_aug, T,
                                               min(FOX_BWD_TQ, T), min(FOX_BWD_TK, T))
    dvf = reducer.early_finish(dvf)
    d_rel = _swa_bias_bwd(dbias, bucket)
    dlo, gacc = _prep_bwd(proj, dqa, dkad, dvad, dqf, dkf, dvf, dqm, dqf_aug, dkf_aug, gains, bfor, triu, gm64, gm128,
                          T, tb_prep)

    def dwc_half(name, dpart):
        (res,) = _matmul(
            name, h, dpart, dims=TN, grid=(1, LO_W // D_MODEL, nkt),
            a_spec=t_rows, b_spec=t_cols, acc_shape=(D_MODEL, D_MODEL),
            outs=[(jax.ShapeDtypeStruct((D_MODEL, LO_W), F32), pl.BlockSpec((D_MODEL, D_MODEL), lambda i, j, k: (0, j)))],
            epilogue=_epi_store)
        return res

    d_wc_lo = dwc_half("mm_dwc_lo", dlo)
    d_wc_gl = dwc_half("mm_dwc_gl", dgl)
    dlo = reducer.late_start({"wc_lo": d_wc_lo, "wc_gl": d_wc_gl}, dlo)
    (dh_lo,) = _matmul(
        "mm_dh_lo", dlo, wc, dims=NT, grid=(T // tl, 1, LO_W // D_MODEL),
        a_spec=kblk, b_spec=pl.BlockSpec((D_MODEL, D_MODEL), lambda i, j, k: (0, k)),
        acc_shape=(tl, D_MODEL), outs=[(jax.ShapeDtypeStruct((T, D_MODEL), F32), row_big)], epilogue=_epi_store)
    dh_lo = reducer.late_send(dh_lo)

    def epi_dx(acc, extra_refs, out_refs, ij):
        dhh = acc + extra_refs[3][...]
        dx, dg = _rmsnorm_bwd_rows(dhh, extra_refs[0][...], extra_refs[1][...])
        out_refs[0][...] = dx + extra_refs[2][...]

        @pl.when(ij[0] == 0)
        def _():
            out_refs[1][...] = jnp.zeros_like(out_refs[1])

        out_refs[1][...] += dg

    grad_x, d_g_mix = _matmul(
        "mm_dh_gl", dgl, wc, dims=NT, grid=(T // tl, 1, GATE_W // D_MODEL),
        a_spec=kblk, b_spec=pl.BlockSpec((D_MODEL, D_MODEL), lambda i, j, k: (0, k + LO_W // D_MODEL)),
        acc_shape=(tl, D_MODEL), extra=[(x, row_big), (g_mix, gain_spec), (dx2, row_big), (dh_lo, row_big)],
        outs=[(jax.ShapeDtypeStruct((T, D_MODEL), F32), row_big), (jax.ShapeDtypeStruct((1, D_MODEL), F32), gain_spec)],
        epilogue=epi_dx, vmem=VMEM_MAX)

    fold64 = lambda row: (row[:64] + row[64:]).reshape(1, 64)
    grads = {
        "g_mix": d_g_mix, "b_gate": d_b_gate, "b_forget": gacc[5, :FOX_HEADS].reshape(1, FOX_HEADS),
        "qn_swa": fold64(gacc[0]), "kn_swa": fold64(gacc[1]),
        "sink_swa": -dsk[:, :SWA_GROUP, 0].reshape(1, SWA_HEADS), "rel_bias": d_rel[:, :SWA_HEADS],
        "qn_fox": fold64(gacc[2]), "kn_fox": fold64(gacc[3]),
        "g_mem": d_g_mem, "qn_mem": gacc[4].reshape(1, LANES), "kn_mem": d_kn_mem, "g_mlp": d_g_mlp,
    }
    return loss, grad_x, grads


MESH = pl.DeviceIdType.MESH


def _place():
    x, y, c = lax.axis_index("x"), lax.axis_index("y"), lax.axis_index("c")
    chips = [(1 - x, y), (x, 1 - y), (1 - x, 1 - y)]
    return x, y, c, chips


def _handshake(peers):
    barrier = pltpu.get_barrier_semaphore()
    for peer in peers:
        pl.semaphore_signal(barrier, inc=1, device_id=peer, device_id_type=MESH)
    pl.semaphore_wait(barrier, len(peers))


def _all_gather_shards_async(name, collective_id, slots):
    n = len(slots)
    bufs = [jax.new_ref(s, memory_space=pltpu.MemorySpace.HBM) for s in slots]

    def body(ici_send, ici_recv, d2d_send, d2d_recv):
        x, y, c, chips = _place()
        sibling = (x, y, 1 - c)
        me = 2 * x + y
        _handshake([(px, py, c) for px, py in chips] + [sibling])

        def half(a, who):
            hr = slots[a].shape[1] // 2
            return pl.ds(pl.multiple_of(who * hr, hr), hr)

        def ici(a, j, slot, to):
            return pltpu.make_async_remote_copy(
                src_ref=bufs[a].at[me, half(a, c)], dst_ref=bufs[a].at[slot, half(a, c)],
                send_sem=ici_send.at[3 * a + j], recv_sem=ici_recv.at[3 * a + j], device_id=to, device_id_type=MESH)

        def d2d(a, j, slot, which):
            part = bufs[a].at[slot, half(a, which)]
            return pltpu.make_async_remote_copy(
                src_ref=part, dst_ref=part, send_sem=d2d_send.at[3 * a + j], recv_sem=d2d_recv.at[3 * a + j],
                device_id=sibling, device_id_type=MESH)

        sends = [ici(a, j, me, (*chip, c)) for a in range(n) for j, chip in enumerate(chips)]
        for cp in sends:
            cp.start()
        passed = []
        for a in range(n):
            for j, (px, py) in enumerate(chips):
                ici(a, j, 2 * px + py, (px, py, c)).wait_recv()
                cp = d2d(a, j, 2 * px + py, c)
                cp.start()
                passed.append(cp)
        for a in range(n):
            for j, (px, py) in enumerate(chips):
                d2d(a, j, 2 * px + py, 1 - c).wait_recv()
        for cp in sends + passed:
            cp.wait_send()

    pl.kernel(
        body, mesh=plsc.ScalarSubcoreMesh(axis_name="seq", num_cores=1), name=name,
        scratch_types=[pltpu.SemaphoreType.DMA((3 * n,))] * 4,
        compiler_params=pltpu.CompilerParams(collective_id=collective_id),
    )()
    return [b[...] for b in bufs]


def _sequencer_call(name, collective_id, n_sems, body):
    pl.kernel(
        body, mesh=plsc.ScalarSubcoreMesh(axis_name="seq", num_cores=1), name=name,
        scratch_types=[pltpu.SemaphoreType.DMA((n_sems,))] * 2,
        compiler_params=pltpu.CompilerParams(collective_id=collective_id),
    )()


def _hbm_ref(value):
    return jax.new_ref(value, memory_space=pltpu.MemorySpace.HBM)


def _pair_exchange(name, collective_id, gs):
    n = len(gs)
    src = [_hbm_ref(g) for g in gs]
    stage = [jax.empty_ref(jax.ShapeDtypeStruct((N_SHARD, g.shape[1] // 2, g.shape[2]), g.dtype),
                           memory_space=pltpu.MemorySpace.HBM) for g in gs]

    def body(send_sem, recv_sem):
        x, y, c, _ = _place()
        sibling = (x, y, 1 - c)
        _handshake([sibling])
        copies = []
        for a in range(n):
            hr = gs[a].shape[1] // 2
            theirs = pl.ds(pl.multiple_of((1 - c) * hr, hr), hr)
            copies.append(pltpu.make_async_remote_copy(
                src_ref=src[a].at[:, theirs, :], dst_ref=stage[a], send_sem=send_sem.at[a], recv_sem=recv_sem.at[a],
                device_id=sibling, device_id_type=MESH))
        for cp in copies:
            cp.start()
        for cp in copies:
            cp.wait()

    _sequencer_call(name, collective_id, n, body)
    return [s[...] for s in stage]


def _chip_exchange(name, collective_id, sums):
    n = len(sums)
    src = [_hbm_ref(s) for s in sums]
    got = [jax.empty_ref(jax.ShapeDtypeStruct((3,) + s.shape[1:], s.dtype), memory_space=pltpu.MemorySpace.HBM)
           for s in sums]

    def body(send_sem, recv_sem):
        x, y, c, chips = _place()
        _handshake([(px, py, c) for px, py in chips])
        copies = []
        for a in range(n):
            for j, (px, py) in enumerate(chips):
                copies.append(pltpu.make_async_remote_copy(
                    src_ref=src[a].at[2 * px + py], dst_ref=got[a].at[j],
                    send_sem=send_sem.at[3 * a + j], recv_sem=recv_sem.at[3 * a + j],
                    device_id=(px, py, c), device_id_type=MESH))
        for cp in copies:
            cp.start()
        for cp in copies:
            cp.wait()

    _sequencer_call(name, collective_id, 3 * n, body)
    return [g[...] for g in got]


def _pair_gather(name, collective_id, fulls):
    n = len(fulls)
    full = [_hbm_ref(f) for f in fulls]

    def body(send_sem, recv_sem):
        x, y, c, _ = _place()
        sibling = (x, y, 1 - c)
        _handshake([sibling])
        copies = []
        for a in range(n):
            hr = fulls[a].shape[0] // 2
            mine = full[a].at[pl.ds(pl.multiple_of(c * hr, hr), hr)]
            copies.append(pltpu.make_async_remote_copy(
                src_ref=mine, dst_ref=mine, send_sem=send_sem.at[a], recv_sem=recv_sem.at[a],
                device_id=sibling, device_id_type=MESH))
        for cp in copies:
            cp.start()
        for cp in copies:
            cp.wait()

    _sequencer_call(name, collective_id, n, body)
    return [f[...] for f in full]


ELEMENTWISE_BLOCK_ELEMS = 512 * 1024


def _row_block(rows, cols):
    rb = 8
    while rb * 2 * cols <= ELEMENTWISE_BLOCK_ELEMS and rb * 2 <= rows:
        rb *= 2
    return rb


def _pair_sum(name, place, g, stage):
    _, R, C = g.shape
    hr = R // 2
    rb = _row_block(hr, C)
    nb = hr // rb

    def body(place_ref, g_ref, st_ref, sum_bf, own_f32):
        s = pl.program_id(1)
        tot = g_ref[...] + st_ref[...]
        sum_bf[...] = tot.astype(BF16)

        @pl.when(s == place_ref[0])
        def _():
            own_f32[...] = tot

    return pl.pallas_call(
        body, name=name,
        grid_spec=pltpu.PrefetchScalarGridSpec(
            num_scalar_prefetch=1, grid=(nb, N_SHARD),
            in_specs=[pl.BlockSpec((None, rb, C), lambda i, s, pr: (s, pr[1] * nb + i, 0)),
                      pl.BlockSpec((None, rb, C), lambda i, s, pr: (s, i, 0))],
            out_specs=[pl.BlockSpec((None, rb, C), lambda i, s, pr: (s, i, 0)),
                       pl.BlockSpec((rb, C), lambda i, s, pr: (i, 0))]),
        out_shape=[jax.ShapeDtypeStruct((N_SHARD, hr, C), BF16), jax.ShapeDtypeStruct((hr, C), F32)],
        compiler_params=_cparams("arbitrary", "arbitrary"),
    )(place, g, stage)


def _final_sum(name, place, own, got):
    hr, C = own.shape
    rb = _row_block(hr, C)
    nb = hr // rb

    def body(place_ref, own_ref, got_ref, o_ref):
        o_ref[...] = ((own_ref[...] + got_ref[0].astype(F32)) + got_ref[1].astype(F32)) + got_ref[2].astype(F32)

    return pl.pallas_call(
        body, name=name,
        grid_spec=pltpu.PrefetchScalarGridSpec(
            num_scalar_prefetch=1, grid=(nb,),
            in_specs=[pl.BlockSpec((rb, C), lambda i, pr: (i, 0)), pl.BlockSpec((3, rb, C), lambda i, pr: (0, i, 0))],
            out_specs=pl.BlockSpec((rb, C), lambda i, pr: (pr[1] * nb + i, 0))),
        out_shape=jax.ShapeDtypeStruct((2 * hr, C), F32),
        compiler_params=_cparams("arbitrary"),
    )(place, own, got)


def _adamw_math(w, g, m, v):
    m = ADAM_B1 * m + (1.0 - ADAM_B1) * g
    v = ADAM_B2 * v + (1.0 - ADAM_B2) * (g * g)
    m_hat = m / (1.0 - ADAM_B1 ** ADAM_STEP)
    v_hat = v / (1.0 - ADAM_B2 ** ADAM_STEP)
    delta = -ADAM_LR * (m_hat / (jnp.sqrt(v_hat) + ADAM_EPS) + ADAM_WD * w)
    return delta, m, v


def _adamw(name, w, g, m, v):
    R, Cw = w.shape
    Cg = g.shape[1]
    rb = _row_block(R, Cg)

    def body(w_ref, g_ref, m_ref, v_ref, g_o, d_o, m_o, v_o):
        gv = g_ref[...]
        delta, mn, vn = _adamw_math(w_ref[...], gv, m_ref[...], v_ref[...])
        g_o[...] = gv
        d_o[...] = delta
        m_o[...] = mn
        v_o[...] = vn

    blk = pl.BlockSpec((rb, Cg), lambda i: (i, 0))
    return pl.pallas_call(
        body, name=name, grid=(R // rb,),
        in_specs=[blk] * 4, out_specs=[blk] * 4,
        out_shape=[jax.ShapeDtypeStruct((R, Cw), F32)] * 4,
        compiler_params=_cparams("parallel"),
    )(w, g, m, v)


N_DEV = 8
SMALL_ROWS = 64


def _small_allreduce_adamw(g, w, m, v):
    def body(g_ref, w_ref, m_ref, v_ref, all_ref, gs_o, d_o, m_o, v_o, send_sems, recv_sems, local_sem):
        x, y, c, chips = _place()
        me, sibling = (x, y, c), (x, y, 1 - c)

        def rows(px, py, pc):
            return all_ref.at[pl.ds(pl.multiple_of((4 * px + 2 * py + pc) * SMALL_ROWS, SMALL_ROWS), SMALL_ROWS), :]

        def copy(k, block, to, src=None):
            return pltpu.make_async_remote_copy(
                src_ref=rows(*block) if src is None else src, dst_ref=rows(*block),
                send_sem=send_sems.at[k], recv_sem=recv_sems.at[k], device_id=to, device_id_type=MESH)

        mine = pltpu.make_async_copy(g_ref, rows(*me), local_sem)
        mine.start()
        first = [copy(0, me, sibling, src=g_ref)]
        first += [copy(1 + j, me, (*chip, c), src=g_ref) for j, chip in enumerate(chips)]
        for cp in first:
            cp.start()
        passed = [copy(4 + j, (*chip, c), sibling) for j, chip in enumerate(chips)]
        for j, chip in enumerate(chips):
            copy(1 + j, (*chip, c), me).wait_recv()
            passed[j].start()
        copy(0, sibling, me).wait_recv()
        for j, chip in enumerate(chips):
            copy(4 + j, (*chip, 1 - c), me).wait_recv()
        for cp in first + passed:
            cp.wait_send()
        mine.wait()

        tot = all_ref[0:SMALL_ROWS, :]
        for d in range(1, N_DEV):
            tot = tot + all_ref[d * SMALL_ROWS:(d + 1) * SMALL_ROWS, :]
        delta, mn, vn = _adamw_math(w_ref[...], tot, m_ref[...], v_ref[...])
        gs_o[...] = tot
        d_o[...] = delta
        m_o[...] = mn
        v_o[...] = vn

    vm = pl.BlockSpec(memory_space=pltpu.VMEM)
    shp = jax.ShapeDtypeStruct((SMALL_ROWS, LANES), F32)
    res = pl.pallas_call(
        body, name="small_allreduce_adamw", in_specs=[vm] * 4, out_specs=[vm] * 5,
        out_shape=[jax.ShapeDtypeStruct((N_DEV * SMALL_ROWS, LANES), F32), shp, shp, shp, shp],
        scratch_shapes=[pltpu.SemaphoreType.DMA((7,)), pltpu.SemaphoreType.DMA((7,)), pltpu.SemaphoreType.DMA],
    )(g, w, m, v)
    return res[1:]


SMALL_NAMES = ("g_mix", "b_gate", "b_forget", "qn_swa", "kn_swa", "sink_swa", "rel_bias", "qn_fox", "kn_fox",
               "g_mem", "qn_mem", "kn_mem", "g_mlp")
BIG_NAMES = ("w_in", "w_mem_kv", "w_o_swa", "w_o_fox", "w_o_mem", "w_out", "w_mlp_up", "w_mlp_down")
WEIGHT_NAMES = ("g_mix", "w_in", "b_gate", "b_forget", "qn_swa", "kn_swa", "sink_swa", "rel_bias", "qn_fox", "kn_fox",
                "g_mem", "w_mem_kv", "qn_mem", "kn_mem", "w_o_swa", "w_o_fox", "w_o_mem", "w_out", "g_mlp",
                "w_mlp_up", "w_mlp_down")


def _pack_small(parts, extra=None):
    rows = []
    for n in SMALL_NAMES:
        flat = parts[n].reshape(-1).astype(F32)
        flat = jnp.pad(flat, (0, (-flat.size) % LANES))
        rows.append(flat.reshape(-1, LANES))
    if extra is not None:
        rows.append(jnp.pad(extra.reshape(1, 1), ((0, 0), (0, LANES - 1))))
    packed = jnp.concatenate(rows, axis=0)
    return jnp.pad(packed, ((0, SMALL_ROWS - packed.shape[0]), (0, 0)))


def _unpack_small(packed, shapes):
    out, r = {}, 0
    for n in SMALL_NAMES:
        size = math.prod(shapes[n])
        nr = -(-size // LANES)
        out[n] = packed[r:r + nr].reshape(-1)[:size].reshape(shapes[n])
        r += nr
    return out, packed[r, 0]


W_IN_SEGMENTS = ((C_QA, 0, 512), (C_QF, 768, 512), (C_KF, 1280, 512), (C_VF, 1792, 512), (C_QM, 2312, 512),
                 (C_KA, 512, 128), (C_VA, 640, 128), (C_FL, 2304, FOX_HEADS), (C_GL, 2824, GATE_W))
RELAYOUT_ROWS = 256


def _permute_pieces(src_of_dst):
    blocks = []
    for b in range(len(src_of_dst) // LANES):
        runs, lane = [], 0
        while lane < LANES:
            src = src_of_dst[b * LANES + lane]
            if src is None:
                lane += 1
                continue
            plane, col = src
            end = lane + 1
            while (end < LANES and src_of_dst[b * LANES + end] == (plane, col + end - lane)
                   and (col + end - lane) // LANES == col // LANES):
                end += 1
            runs.append((plane, col // LANES, (lane - col) % LANES, lane, end))
            lane = end
        blocks.append(runs)
    return blocks


def _permuted_block(runs, load, rows):
    lane = _lane((rows, LANES))
    acc = jnp.zeros((rows, LANES), F32)
    for plane, blk, shift, lo, hi in runs:
        x = load(plane, blk).astype(F32)
        if shift:
            x = pltpu.roll(x, shift, 1)
        acc = x if (lo, hi) == (0, LANES) else jnp.where((lane >= lo) & (lane < hi), x, acc)
    return acc


def _w_in_to_segments(g_in):
    src_of_dst = [None] * PROJ_W
    for mine, theirs, width in W_IN_SEGMENTS:
        for k in range(width):
            src_of_dst[mine + k] = ((theirs + k) // IN_SHARD, (theirs + k) % IN_SHARD)
    blocks = _permute_pieces(src_of_dst)
    rb = RELAYOUT_ROWS

    def body(src_ref, out_ref):
        for b, runs in enumerate(blocks):
            blk = _permuted_block(runs, lambda p, c: src_ref[p, :, c * LANES:(c + 1) * LANES], rb)
            out_ref[:, b * LANES:(b + 1) * LANES] = blk.astype(out_ref.dtype)

    return pl.pallas_call(
        body, name="w_in_to_segments", grid=(D_MODEL // rb,),
        in_specs=[pl.BlockSpec((N_SHARD, rb, IN_SHARD_PAD), lambda i: (0, i, 0))],
        out_specs=pl.BlockSpec((rb, PROJ_W), lambda i: (i, 0)),
        out_shape=jax.ShapeDtypeStruct((D_MODEL, PROJ_W), g_in.dtype),
        compiler_params=_cparams("parallel", vmem=VMEM_MID),
    )(g_in)


def _w_in_from_segments(lo, gl):
    mine_of_theirs = {}
    for mine, theirs, width in W_IN_SEGMENTS:
        for k in range(width):
            mine_of_theirs[theirs + k] = mine + k
    src_of_dst = [None] * (N_SHARD * IN_SHARD_PAD)
    for s in range(N_SHARD):
        for l in range(IN_SHARD):
            j = mine_of_theirs[s * IN_SHARD + l]
            src_of_dst[s * IN_SHARD_PAD + l] = (j // LO_W, j % LO_W)
    blocks = _permute_pieces(src_of_dst)
    per_slot = IN_SHARD_PAD // LANES
    rb = RELAYOUT_ROWS

    def body(lo_ref, gl_ref, out_ref):
        planes = (lo_ref, gl_ref)
        for b, runs in enumerate(blocks):
            blk = _permuted_block(runs, lambda p, c: planes[p][:, c * LANES:(c + 1) * LANES], rb)
            c0 = (b % per_slot) * LANES
            out_ref[b // per_slot, :, c0:c0 + LANES] = blk

    half = pl.BlockSpec((rb, LO_W), lambda i: (i, 0))
    return pl.pallas_call(
        body, name="w_in_from_segments", grid=(D_MODEL // rb,),
        in_specs=[half, half],
        out_specs=pl.BlockSpec((N_SHARD, rb, IN_SHARD_PAD), lambda i: (0, i, 0)),
        out_shape=jax.ShapeDtypeStruct((N_SHARD, D_MODEL, IN_SHARD_PAD), F32),
        compiler_params=_cparams("parallel", vmem=VMEM_MID),
    )(lo, gl)


def _after(first, then):
    return lax.optimization_barrier((first, then))


class _ReduceGroup:
    def __init__(self, tag, first_collective_id, place):
        self.tag, self.first_id, self.place = tag, first_collective_id, place

    def start(self, local, tie):
        self.names = tuple(local)
        mine, tie = _after([local[n] for n in self.names], tie)
        self.mine = mine
        self.staged = _pair_exchange("pair_exchange_" + self.tag, self.first_id, mine)
        return tie

    def send(self, tie):
        staged, tie = _after(self.staged, tie)
        sums = [_pair_sum("pair_sum_" + n, self.place, g, st) for n, g, st in zip(self.names, self.mine, staged)]
        travel, tie = _after([s[0] for s in sums], tie)
        self.own = [s[1] for s in sums]
        self.got = _chip_exchange("chip_exchange_" + self.tag, self.first_id + 1, travel)
        return tie

    def finish(self, tie):
        got, tie = _after(self.got, tie)
        halves = [_final_sum("final_sum_" + n, self.place, o, r) for n, o, r in zip(self.names, self.own, got)]
        halves, tie = _after(halves, tie)
        summed = _pair_gather("pair_gather_" + self.tag, self.first_id + 2, halves)
        self.summed = dict(zip(self.names, summed))
        return tie


class _GradReducer:
    def __init__(self, place):
        self.early = _ReduceGroup("early", 2, place)
        self.late = _ReduceGroup("late", 5, place)

    @staticmethod
    def _slot_rows(a):
        return a.reshape(N_SHARD, a.shape[0] // N_SHARD, a.shape[1])

    def early_start(self, g, tie):
        return self.early.start({"w_mlp_down": self._slot_rows(g["w_mlp_down"]), "w_mlp_up": g["w_mlp_up"],
                                 "w_out": self._slot_rows(g["w_out"]), "w_mem_kv": self._slot_rows(g["w_mem_kv"]),
                                 "w_o_swa": g["w_o_swa"], "w_o_fox": g["w_o_fox"], "w_o_mem": g["w_o_mem"]}, tie)

    def early_send(self, tie):
        return self.early.send(tie)

    def early_finish(self, tie):
        return self.early.finish(tie)

    def late_start(self, g, tie):
        d_in = _w_in_from_segments(g["wc_lo"], g["wc_gl"])
        return self.late.start({"w_in": d_in}, tie)

    def late_send(self, tie):
        return self.late.send(tie)

    def late_finish(self, tie):
        return self.late.finish(tie)

    @property
    def summed(self):
        return {**self.early.summed, **self.late.summed}


def kernel(x, mem, g_mix, w_in, b_gate, b_forget, qn_swa, kn_swa, sink_swa, rel_bias, qn_fox, kn_fox, g_mem, w_mem_kv, qn_mem, kn_mem, w_o_swa, w_o_fox, w_o_mem, w_out, g_mlp, w_mlp_up, w_mlp_down, loss_target, m_g_mix, m_w_in, m_b_gate, m_b_forget, m_qn_swa, m_kn_swa, m_sink_swa, m_rel_bias, m_qn_fox, m_kn_fox, m_g_mem, m_w_mem_kv, m_qn_mem, m_kn_mem, m_w_o_swa, m_w_o_fox, m_w_o_mem, m_w_out, m_g_mlp, m_w_mlp_up, m_w_mlp_down, v_g_mix, v_w_in, v_b_gate, v_b_forget, v_qn_swa, v_kn_swa, v_sink_swa, v_rel_bias, v_qn_fox, v_kn_fox, v_g_mem, v_w_mem_kv, v_qn_mem, v_kn_mem, v_w_o_swa, v_w_o_fox, v_w_o_mem, v_w_out, v_g_mlp, v_w_mlp_up, v_w_mlp_down):
    given = dict(locals())
    W = {n: given[n] for n in WEIGHT_NAMES}
    M = {n: given["m_" + n] for n in WEIGHT_NAMES}
    V = {n: given["v_" + n] for n in WEIGHT_NAMES}
    pad_in = ((0, 0), (0, IN_SHARD_PAD - IN_SHARD))

    shards = [jnp.pad(w_in[0].astype(BF16), pad_in)] + [W[n][0].astype(BF16) for n in BIG_NAMES[1:]]
    slots = [jnp.broadcast_to(s[None], (N_SHARD,) + s.shape) for s in shards]
    (g_in,) = _all_gather_shards_async("all_gather_w_in", 1, slots[:1])
    small = {n: (W[n] if n == "rel_bias" else W[n].reshape(1, -1)) for n in SMALL_NAMES}
    h = _rmsnorm("rms_mix", x[0], small["g_mix"], min(1024, x.shape[1]))
    g_in, late, h, (m_in, v_in) = lax.optimization_barrier((g_in, slots[1:], h, (M["w_in"][0], V["w_in"][0])))
    M["w_in"], V["w_in"] = m_in[None], v_in[None]
    g_kv, g_oa, g_of, g_om, g_out, g_up, g_down = _all_gather_shards_async("all_gather_weights_async", 8, late)

    place = jnp.stack([2 * lax.axis_index("x") + lax.axis_index("y"), lax.axis_index("c")]).astype(jnp.int32)
    reducer = _GradReducer(place)
    loss, grad_x, grads = _local_step(
        x[0], h, mem[0], loss_target[0], small, g_in, g_kv.reshape(D_MODEL, D_MODEL), (g_oa, g_of, g_om),
        g_out.reshape(D_MODEL, D_MODEL), g_up, g_down.reshape(D_FF, D_MODEL), reducer)

    out = {}

    def adamw_of(names, summed):
        for n in names:
            res = _adamw("adamw_" + n, W[n][0], summed[n], M[n][0], V[n][0])
            out[n] = [r.reshape(W[n].shape) for r in res]

    adamw_of(reducer.early.names, reducer.early.summed)
    shapes = {n: W[n].shape for n in SMALL_NAMES}
    packed = _small_allreduce_adamw(_pack_small(grads, loss), _pack_small(W), _pack_small(M), _pack_small(V))
    done_meanwhile = ([out[n] for n in reducer.early.names], packed)
    (early_out, packed), grad_x = reducer.late_finish((done_meanwhile, grad_x))
    for n, res in zip(reducer.early.names, early_out):
        out[n] = res
    adamw_of(reducer.late.names, reducer.late.summed)
    unpacked = [_unpack_small(p, shapes) for p in packed]
    for n in SMALL_NAMES:
        out[n] = [u[0][n] for u in unpacked]
    loss_total = unpacked[0][1]

    return (loss_total, grad_x.reshape(x.shape),
            *[out[n][0] for n in WEIGHT_NAMES], *[out[n][1] for n in WEIGHT_NAMES],
            *[out[n][2] for n in WEIGHT_NAMES], *[out[n][3] for n in WEIGHT_NAMES])
```

```python
import math

import jax
import jax.numpy as jnp
from jax import lax
from jax.experimental import pallas as pl
from jax.experimental.pallas import tpu as pltpu
from jax.experimental.pallas import tpu_sc as plsc

F32 = jnp.float32
BF16 = jnp.bfloat16

D_MODEL = 1024
N_MEM = 256
SWA_HEADS = 8
SWA_KV_HEADS = 2
SWA_HEAD_DIM = 64
WINDOW = 128
FOX_HEADS = 8
FOX_HEAD_DIM = 64
MEM_HEADS = 4
MEM_HEAD_DIM = 128
D_FF = 4 * D_MODEL
REL_BUCKETS = 32
REL_MAX_DIST = 128
EPS = 1e-6
NEG = -1e30
GATE_W = 3 * D_MODEL
IN_WIDTH = 5896
N_SHARD = 4
IN_SHARD = IN_WIDTH // N_SHARD
IN_SHARD_PAD = 1536

ADAM_LR = 0.001
ADAM_B1 = 0.9
ADAM_B2 = 0.999
ADAM_EPS = 1e-08
ADAM_WD = 0.01
ADAM_STEP = 10

LANES = 128
V7X_VMEM_BYTES = 64 * 1024 * 1024
VMEM_SMALL = VMEM_MID = VMEM_BIG = V7X_VMEM_BYTES * 3 // 4
VMEM_MAX = V7X_VMEM_BYTES * 7 // 8

C_QA, C_QF, C_KF, C_VF, C_QM, C_KA, C_VA, C_FL, C_GL = 0, 512, 1024, 1536, 2048, 2560, 2688, 2816, 3072
LO_W = 3072
PROJ_W = 6144
PROJ_TN = 2048

NN = (((1,), (0,)), ((), ()))
NT = (((1,), (1,)), ((), ()))
TN = (((0,), (0,)), ((), ()))


def _dot(a, b, dims=NN):
    return lax.dot_general(a, b, dims, preferred_element_type=F32)


def _cparams(*sem, vmem=VMEM_SMALL):
    return pltpu.CompilerParams(dimension_semantics=sem, vmem_limit_bytes=vmem)


def _split3(a):
    hi = a.astype(BF16)
    r1 = a - hi.astype(F32)
    mid = r1.astype(BF16)
    lo = (r1 - mid.astype(F32)).astype(BF16)
    return hi, mid, lo


def _group_mean(a, g2):
    hi = a.astype(BF16)
    mid = (a - hi.astype(F32)).astype(BF16)
    return _dot(jnp.concatenate([hi, mid], axis=1), g2)


def _dot3_left(g, a):
    hi, mid, lo = _split3(a)
    return _dot(g, hi) + _dot(g, mid) + _dot(g, lo)


def _group_mean_matrix(d):
    r = jnp.arange(LANES)
    g = jnp.where((r[:, None] // d) == (r[None, :] // d), 1.0 / d, 0.0).astype(BF16)
    return jnp.concatenate([g, g], axis=0)


def _lane(shape):
    return lax.broadcasted_iota(jnp.int32, shape, len(shape) - 1)


def _matmul(name, a, b, *, dims, grid, a_spec, b_spec, acc_shape, outs, epilogue, extra=(), vmem=VMEM_BIG):
    nk = grid[2]
    n_extra = len(extra)

    def body(a_ref, b_ref, *rest):
        extra_refs = rest[:n_extra]
        out_refs = rest[n_extra:n_extra + len(outs)]
        i, j, k = pl.program_id(0), pl.program_id(1), pl.program_id(2)
        if nk == 1:
            epilogue(_dot(a_ref[...].astype(BF16), b_ref[...].astype(BF16), dims), extra_refs, out_refs, (i, j))
            return
        acc_ref = rest[-1]

        @pl.when(k == 0)
        def _():
            acc_ref[...] = jnp.zeros_like(acc_ref)

        acc_ref[...] += _dot(a_ref[...].astype(BF16), b_ref[...].astype(BF16), dims)

        @pl.when(k == nk - 1)
        def _():
            epilogue(acc_ref[...], extra_refs, out_refs, (i, j))

    res = pl.pallas_call(
        body,
        name=name,
        grid=grid,
        in_specs=[a_spec, b_spec] + [s for _, s in extra],
        out_specs=[s for _, s in outs],
        out_shape=[s for s, _ in outs],
        scratch_shapes=[pltpu.VMEM(acc_shape, F32)] if nk > 1 else [],
        compiler_params=_cparams("arbitrary", "arbitrary", "arbitrary", vmem=vmem),
    )(a, b, *[x for x, _ in extra])
    return res


def _epi_store(acc, extra_refs, out_refs, ij):
    out_refs[0][...] = acc.astype(out_refs[0].dtype)


def _rms_rows(x, g):
    r = lax.rsqrt(jnp.mean(x * x, axis=-1, keepdims=True) + EPS)
    return x * r, r


def _rmsnorm_bwd_rows(dh, x, g):
    xhat, r = _rms_rows(x, g)
    dxh = dh * g
    dx = r * (dxh - xhat * jnp.mean(dxh * xhat, axis=-1, keepdims=True))
    return dx, jnp.sum(dh * xhat, axis=0, keepdims=True)


def _rmsnorm(name, x, g, tb):
    T, Dm = x.shape

    def body(x_ref, g_ref, o_ref):
        xhat, _ = _rms_rows(x_ref[...], None)
        o_ref[...] = (xhat * g_ref[...]).astype(o_ref.dtype)

    return pl.pallas_call(
        body, name=name, grid=(T // tb,),
        in_specs=[pl.BlockSpec((tb, Dm), lambda i: (i, 0)), pl.BlockSpec((1, Dm), lambda i: (0, 0))],
        out_specs=pl.BlockSpec((tb, Dm), lambda i: (i, 0)),
        out_shape=jax.ShapeDtypeStruct((T, Dm), BF16),
        compiler_params=_cparams("parallel"),
    )(x, g)


def _head_norm(x, gm, gain):
    ms = _group_mean(x * x, gm)
    r = lax.rsqrt(ms + EPS)
    return x * r * gain, x * r


def _head_norm_bwd(dy, x, gm, gain):
    ms = _group_mean(x * x, gm)
    r = lax.rsqrt(ms + EPS)
    xhat = x * r
    dxh = dy * gain
    dx = r * (dxh - xhat * _group_mean(dxh * xhat, gm))
    return dx, jnp.sum(dy * xhat, axis=0, keepdims=True)


def _log_sigmoid(z):
    return jnp.minimum(z, 0.0) - jnp.log(1.0 + jnp.exp(-jnp.abs(z)))


def _prep_fwd(proj, gains, bfor, tril, gm64, gm128, T, tb):
    nb = T // tb

    def body(qa_ref, qf_ref, kf_ref, vf_ref, qm_ref, ka_ref, va_ref, fl_ref, gains_ref, bfor_ref, tril_ref,
             gm64_ref, gm128_ref,
             qa_o, qf_o, kf_o, vf_o, qm_o, kad_o, vad_o, qaug_o, kaug_o, carry):
        i = pl.program_id(0)
        gm64v = gm64_ref[...]
        gm128v = gm128_ref[...]
        lane = _lane((tb, LANES))

        def norm512(src, dst, row, gm, scale=1.0):
            gain = gains_ref[row:row + 1, :]
            for c in range(4):
                sl = slice(c * LANES, (c + 1) * LANES)
                y, _ = _head_norm(src[:, sl], gm, gain)
                dst[:, sl] = (y * scale).astype(dst.dtype)

        norm512(qa_ref, qa_o, 0, gm64v)
        norm512(qf_ref, qf_o, 2, gm64v, FOX_SCALE)
        norm512(kf_ref, kf_o, 3, gm64v)
        norm512(qm_ref, qm_o, 4, gm128v)
        vf_o[...] = vf_ref[...].astype(vf_o.dtype)

        ka_n, _ = _head_norm(ka_ref[...], gm64v, gains_ref[1:2, :])
        ka_r = pltpu.roll(ka_n, 64, 1)
        va = va_ref[...]
        va_r = pltpu.roll(va, 64, 1)
        lo = lane < 64
        kad_o[0] = jnp.where(lo, ka_n, ka_r).astype(kad_o.dtype)
        kad_o[1] = jnp.where(lo, ka_r, ka_n).astype(kad_o.dtype)
        vad_o[0] = jnp.where(lo, va, va_r).astype(vad_o.dtype)
        vad_o[1] = jnp.where(lo, va_r, va).astype(vad_o.dtype)

        @pl.when(i == 0)
        def _():
            carry[...] = jnp.zeros_like(carry)

        logf = jnp.where(lane < FOX_HEADS, _log_sigmoid(fl_ref[...] + bfor_ref[...]), 0.0)
        c = _dot3_left(tril_ref[...], logf) + carry[0:1, :]
        carry[...] = jnp.broadcast_to(c[tb - 1:tb, :], carry.shape)
        for pair in range(FOX_HEADS // 2):
            qaug = jnp.zeros((tb, LANES), F32)
            kaug = jnp.zeros((tb, LANES), F32)
            for sub in range(2):
                col = jnp.sum(jnp.where(lane == 2 * pair + sub, c, 0.0), axis=1, keepdims=True)
                pieces = [p.astype(F32) for p in _split3(col)]
                base = AUG_STRIDE * sub
                for e in range(3):
                    qaug = jnp.where(lane == base + AUG_C + e, pieces[e], qaug)
                    kaug = jnp.where(lane == base + AUG_NEG_C + e, -pieces[e], kaug)
                qaug = jnp.where((lane >= base + AUG_NEG_C) & (lane < base + AUG_NEG_C + 3), 1.0, qaug)
                ones_k = ((lane >= base + AUG_C) & (lane < base + AUG_C + 3)) | (
                    (lane >= base + AUG_STAT) & (lane < base + AUG_STAT + 3))
                kaug = jnp.where(ones_k, 1.0, kaug)
            sl = slice(pair * LANES, (pair + 1) * LANES)
            qaug_o[:, sl] = qaug.astype(BF16)
            kaug_o[:, sl] = kaug.astype(BF16)

    def seg(width, start):
        return pl.BlockSpec((tb, width), lambda i, s=start // width: (i, s))

    const = lambda shape: pl.BlockSpec(shape, lambda i: tuple(0 for _ in shape))
    rows512 = pl.BlockSpec((tb, 512), lambda i: (i, 0))
    outs = pl.pallas_call(
        body, name="prep_fwd", grid=(nb,),
        in_specs=[seg(512, C_QA), seg(512, C_QF), seg(512, C_KF), seg(512, C_VF), seg(512, C_QM),
                  seg(128, C_KA), seg(128, C_VA), seg(128, C_FL),
                  const((8, LANES)), const((1, LANES)), const((tb, tb)), const((2 * LANES, LANES)), const((2 * LANES, LANES))],
        out_specs=[rows512, rows512, rows512, rows512, rows512,
                   pl.BlockSpec((2, tb, LANES), lambda i: (0, i, 0)), pl.BlockSpec((2, tb, LANES), lambda i: (0, i, 0)),
                   rows512, rows512],
        out_shape=[jax.ShapeDtypeStruct((T, 512), BF16)] * 5
        + [jax.ShapeDtypeStruct((2, T, LANES), BF16)] * 2
        + [jax.ShapeDtypeStruct((T, 512), BF16)] * 2,
        scratch_shapes=[pltpu.VMEM((8, LANES), F32)],
        compiler_params=_cparams("arbitrary", vmem=VMEM_MID),
    )(proj, proj, proj, proj, proj, proj, proj, proj, gains, bfor, tril, gm64, gm128)
    return outs


def _prep_bwd(proj, dqa, dkad, dvad, dqf, dkf, dvf, dqm, dqf_aug, dkf_aug, gains, bfor, triu, gm64, gm128, T, tb):
    nb = T // tb

    def body(qa_ref, qf_ref, kf_ref, qm_ref, ka_ref, fl_ref,
             dqa_ref, dkad_ref, dvad_ref, dqf_ref, dkf_ref, dvf_ref, dqm_ref, dqfa_ref, dkfa_ref,
             gains_ref, bfor_ref, triu_ref, gm64_ref, gm128_ref,
             dlo_o, gacc_o, carry):
        i = pl.program_id(0)
        gm64v = gm64_ref[...]
        gm128v = gm128_ref[...]
        lane = _lane((tb, LANES))

        @pl.when(i == 0)
        def _():
            carry[...] = jnp.zeros_like(carry)
            gacc_o[...] = jnp.zeros_like(gacc_o)

        def norm512_bwd(dsrc, xsrc, col0, row, gm):
            gain = gains_ref[row:row + 1, :]
            gsum = jnp.zeros((1, LANES), F32)
            for c in range(4):
                sl = slice(c * LANES, (c + 1) * LANES)
                dx, dg = _head_norm_bwd(dsrc[:, sl], xsrc[:, sl], gm, gain)
                dlo_o[:, col0 + c * LANES:col0 + (c + 1) * LANES] = dx.astype(dlo_o.dtype)
                gsum = gsum + dg
            gacc_o[row:row + 1, :] += gsum

        norm512_bwd(dqa_ref, qa_ref, C_QA, 0, gm64v)
        norm512_bwd(dqf_ref, qf_ref, C_QF, 2, gm64v)
        norm512_bwd(dkf_ref, kf_ref, C_KF, 3, gm64v)
        norm512_bwd(dqm_ref, qm_ref, C_QM, 4, gm128v)
        dlo_o[:, C_VF:C_VF + 512] = dvf_ref[...].astype(dlo_o.dtype)

        lo = lane < 64

        def fold(ref):
            f0 = ref[0] + pltpu.roll(ref[0], 64, 1)
            f1 = ref[1] + pltpu.roll(ref[1], 64, 1)
            return jnp.where(lo, f0, f1)

        dka, dg = _head_norm_bwd(fold(dkad_ref), ka_ref[...], gm64v, gains_ref[1:2, :])
        gacc_o[1:2, :] += dg
        dlo_o[:, C_KA:C_KA + LANES] = dka.astype(dlo_o.dtype)
        dlo_o[:, C_VA:C_VA + LANES] = fold(dvad_ref).astype(dlo_o.dtype)

        dc = jnp.zeros((tb, LANES), F32)
        for pair in range(FOX_HEADS // 2):
            sl = slice(pair * LANES, (pair + 1) * LANES)
            rows_sum, cols_sum = dqfa_ref[:, sl], dkfa_ref[:, sl]
            for sub in range(2):
                diff = (jnp.where(lane == AUG_STRIDE * sub + AUG_C, rows_sum, 0.0)
                        - jnp.where(lane == AUG_STRIDE * sub + AUG_NEG_C, cols_sum, 0.0))
                dc = jnp.where(lane == 2 * pair + sub, jnp.sum(diff, axis=1, keepdims=True), dc)
        dlogf = _dot3_left(triu_ref[...], dc) + carry[0:1, :]
        carry[...] = jnp.broadcast_to(dlogf[0:1, :], carry.shape)
        z = fl_ref[...] + bfor_ref[...]
        dfl = jnp.where(lane < FOX_HEADS, dlogf / (1.0 + jnp.exp(z)), 0.0)
        gacc_o[5:6, :] += jnp.sum(dfl, axis=0, keepdims=True)
        dlo_o[:, C_FL:C_FL + LANES] = dfl.astype(dlo_o.dtype)
        dlo_o[:, C_FL + LANES:C_FL + 2 * LANES] = jnp.zeros((tb, LANES), dlo_o.dtype)

    rev = lambda i: nb - 1 - i

    def seg(width, start):
        return pl.BlockSpec((tb, width), lambda i, s=start // width: (rev(i), s))

    const = lambda shape: pl.BlockSpec(shape, lambda i: tuple(0 for _ in shape))
    rows512 = pl.BlockSpec((tb, 512), lambda i: (rev(i), 0))
    dup = pl.BlockSpec((2, tb, LANES), lambda i: (0, rev(i), 0))
    return pl.pallas_call(
        body, name="prep_bwd", grid=(nb,),
        in_specs=[seg(512, C_QA), seg(512, C_QF), seg(512, C_KF), seg(512, C_QM), seg(128, C_KA), seg(128, C_FL),
                  rows512, dup, dup, rows512, rows512, rows512, rows512, rows512, rows512,
                  const((8, LANES)), const((1, LANES)), const((tb, tb)), const((2 * LANES, LANES)), const((2 * LANES, LANES))],
        out_specs=[pl.BlockSpec((tb, LO_W), lambda i: (rev(i), 0)), const((8, LANES))],
        out_shape=[jax.ShapeDtypeStruct((T, LO_W), BF16), jax.ShapeDtypeStruct((8, LANES), F32)],
        scratch_shapes=[pltpu.VMEM((8, LANES), F32)],
        compiler_params=_cparams("arbitrary", vmem=VMEM_MID),
    )(proj, proj, proj, proj, proj, proj, dqa, dkad, dvad, dqf, dkf, dvf, dqm, dqf_aug, dkf_aug,
      gains, bfor, triu, gm64, gm128)


FOX_SCALE = FOX_HEAD_DIM ** -0.5
AUG_STRIDE = 16
AUG_C = 0
AUG_NEG_C = 3
AUG_STAT = 6
FOX_TQ, FOX_TK = 1024, 1024
FOX_BWD_TQ, FOX_BWD_TK = 1024, 1024
FOX_DIAGONAL_PARTS = 4


def _fox_head_mask(sub, rows):
    lane = _lane((rows, 2 * LANES))
    main = (lane >= 64 * sub) & (lane < 64 * sub + 64)
    aug = (lane >= LANES + AUG_STRIDE * sub) & (lane < LANES + AUG_STRIDE * (sub + 1))
    return main | aug


def _fox_pieces(diagonal, tq, tk):
    if diagonal and tq == tk and tq >= FOX_DIAGONAL_PARTS * LANES:
        step = tq // FOX_DIAGONAL_PARTS
        return [(n * step, (n + 1) * step, (n + 1) * step) for n in range(FOX_DIAGONAL_PARTS)]
    return [(0, tq, tk)]


def _fox_fwd(q, qaug, k, kaug, v, T, tq, tk):
    nq, nk = T // tq, T // tk
    rep = tk // LANES
    last_of = lambda i: (i * tq + tq - 1) // tk

    def body(q_ref, qa_ref, k_ref, ka_ref, v_ref, o_ref, qab_ref, m_s, acc_s):
        p_, i, j = pl.program_id(0), pl.program_id(1), pl.program_id(2)
        last = last_of(i)

        @pl.when(j == 0)
        def _():
            m_s[...] = jnp.full(m_s.shape, NEG, F32)
            acc_s[...] = jnp.zeros_like(acc_s)

        def step(diagonal):
            k2 = jnp.concatenate([k_ref[...], ka_ref[...]], axis=1)
            v2 = jnp.concatenate([v_ref[...], ka_ref[...]], axis=1)
            pieces = _fox_pieces(diagonal, tq, tk)
            work = []
            for r0, r1, nc in pieces:
                rows = slice(r0, r1)
                q2 = jnp.concatenate([q_ref[rows, :], qa_ref[rows, :]], axis=1)
                for sub in range(2):
                    qh = jnp.where(_fox_head_mask(sub, r1 - r0), q2, jnp.zeros_like(q2))
                    work.append((rows, r0, r1 - r0, nc, sub, _dot(qh, k2[:nc], NT)))
            for rows, r0, nr, nc, sub, s in work:
                if diagonal:
                    causal = (lax.broadcasted_iota(jnp.int32, (nr, nc), 1) + j * tk
                              <= lax.broadcasted_iota(jnp.int32, (nr, nc), 0) + (r0 + i * tq))
                    s = jnp.where(causal, s, NEG)
                m_prev = m_s[sub, rows, :]
                m_next = jnp.maximum(m_prev, jnp.max(s, axis=1, keepdims=True))
                p = jnp.exp(s - jnp.tile(m_next, (1, nc // LANES)))
                alpha = jnp.exp(m_prev - m_next)
                m_s[sub, rows, :] = m_next
                acc_s[sub, rows, :] = acc_s[sub, rows, :] * jnp.tile(alpha, (1, 2)) + _dot(p.astype(BF16), v2[:nc])

        @pl.when(j == last)
        def _():
            step(True)

        @pl.when(j < last)
        def _():
            step(False)

        @pl.when(j == nk - 1)
        def _():
            lane = _lane((tq, LANES))
            outs = []
            qab = qa_ref[...].astype(F32)
            for sub in range(2):
                acc = acc_s[sub]
                base = AUG_STRIDE * sub
                l = jnp.sum(jnp.where(lane == base + AUG_C, acc[:, LANES:], 0.0), axis=1, keepdims=True)
                outs.append(acc[:, :LANES] / l)
                lse = jnp.max(m_s[sub], axis=1, keepdims=True) + jnp.log(l)
                pieces = _split3(-lse)
                for e in range(3):
                    qab = jnp.where(lane == base + AUG_STAT + e, pieces[e].astype(F32), qab)
            o_ref[...] = jnp.where(lane < 64, outs[0], outs[1]).astype(o_ref.dtype)
            qab_ref[...] = qab.astype(BF16)

    qspec = pl.BlockSpec((tq, LANES), lambda p, i, j: (i, p))
    kspec = pl.BlockSpec((tk, LANES), lambda p, i, j: (jnp.minimum(j, last_of(i)), p))
    return pl.pallas_call(
        body, name="fox_fwd", grid=(4, nq, nk),
        in_specs=[qspec, qspec, kspec, kspec, kspec],
        out_specs=[qspec, qspec],
        out_shape=[jax.ShapeDtypeStruct((T, 512), BF16), jax.ShapeDtypeStruct((T, 512), BF16)],
        scratch_shapes=[pltpu.VMEM((2, tq, LANES), F32), pltpu.VMEM((2, tq, 2 * LANES), F32)],
        compiler_params=_cparams("parallel", "parallel", "arbitrary", vmem=VMEM_BIG),
    )(q, qaug, k, kaug, v)


def _fox_bwd(q, qaug, k, kaug, v, do, doaug, T, tq, tk):
    nq, nk = T // tq, T // tk
    first_of = lambda j: (j * tk) // tq

    def body(q_ref, qa_ref, k_ref, ka_ref, v_ref, do_ref, doa_ref,
             dq_ref, dqa_ref, dk_ref, dka_ref, dv_ref, dk_s, dv_s):
        p_, j, i = pl.program_id(0), pl.program_id(1), pl.program_id(2)
        masked = i * tq < (j + 1) * tk - 1

        @pl.when((j == 0) & (i == 0))
        def _():
            dq_ref[...] = jnp.zeros_like(dq_ref)
            dqa_ref[...] = jnp.zeros_like(dqa_ref)

        @pl.when(i == 0)
        def _():
            dk_s[...] = jnp.zeros_like(dk_s)
            dv_s[...] = jnp.zeros_like(dv_s)

        def step(diagonal):
            k2 = jnp.concatenate([k_ref[...], ka_ref[...]], axis=1)
            v2 = jnp.concatenate([v_ref[...], ka_ref[...]], axis=1)
            work = []
            for r0, r1, nc in _fox_pieces(diagonal, tq, tk):
                rows = slice(r0, r1)
                q2 = jnp.concatenate([q_ref[rows, :], qa_ref[rows, :]], axis=1)
                do2 = jnp.concatenate([do_ref[rows, :], doa_ref[rows, :]], axis=1)
                for sub in range(2):
                    hm = _fox_head_mask(sub, r1 - r0)
                    qh = jnp.where(hm, q2, jnp.zeros_like(q2))
                    doh = jnp.where(hm, do2, jnp.zeros_like(do2))
                    s = _dot(qh, k2[:nc], NT)
                    dp = _dot(doh, v2[:nc], NT)
                    work.append((r0, r1 - r0, nc, sub, qh, doh, s, dp))
            dqs = {}
            for r0, nr, nc, sub, qh, doh, s, dp in work:
                if diagonal:
                    causal = (lax.broadcasted_iota(jnp.int32, (nr, nc), 1) + j * tk
                              <= lax.broadcasted_iota(jnp.int32, (nr, nc), 0) + (r0 + i * tq))
                    s = jnp.where(causal, s, NEG)
                p = jnp.exp(s)
                dsb = (p * dp).astype(BF16)
                dv_s[0:nc, :] += _dot(p.astype(BF16), doh[:, :LANES], TN)
                dk_s[0:nc, :] += _dot(dsb, qh, TN)
                dqs[(r0, sub)] = _dot(dsb, k2[:nc])
            for r0, r1, nc in _fox_pieces(diagonal, tq, tk):
                dq2 = jnp.where(_fox_head_mask(0, r1 - r0), dqs[(r0, 0)], dqs[(r0, 1)])
                qrows = pl.ds(pl.multiple_of(i * tq + r0, r1 - r0), r1 - r0)
                dq_ref[qrows, :] += dq2[:, :LANES] * FOX_SCALE
                dqa_ref[qrows, :] += dq2[:, LANES:]

        @pl.when((i >= first_of(j)) & masked)
        def _():
            step(True)

        @pl.when((i >= first_of(j)) & jnp.logical_not(masked))
        def _():
            step(False)

        @pl.when(i == nq - 1)
        def _():
            dk_ref[...] = dk_s[:, :LANES]
            dka_ref[...] = dk_s[:, LANES:]
            dv_ref[...] = dv_s[...]

    qspec = pl.BlockSpec((tq, LANES), lambda p, j, i: (jnp.maximum(i, first_of(j)), p))
    kspec = pl.BlockSpec((tk, LANES), lambda p, j, i: (j, p))
    resident = pl.BlockSpec((T, LANES), lambda p, j, i: (0, p))
    return pl.pallas_call(
        body, name="fox_bwd", grid=(4, nk, nq),
        in_specs=[qspec, qspec, kspec, kspec, kspec, qspec, qspec],
        out_specs=[resident, resident, kspec, kspec, kspec],
        out_shape=[jax.ShapeDtypeStruct((T, 512), F32)] * 5,
        scratch_shapes=[pltpu.VMEM((tk, 2 * LANES), F32), pltpu.VMEM((tk, LANES), F32)],
        compiler_params=_cparams("arbitrary", "arbitrary", "arbitrary", vmem=VMEM_BIG),
    )(q, qaug, k, kaug, v, do, doaug)


SWA_SUB = 16
SWA_TB = SWA_SUB * WINDOW


def _t5_bucket_matrix():
    t = jnp.arange(WINDOW)[:, None] + WINDOW
    s = jnp.arange(2 * WINDOW)[None, :]
    max_exact = REL_BUCKETS // 2
    d = jnp.maximum(t - s, 0)
    df = jnp.maximum(d, 1).astype(F32)
    large = max_exact + (jnp.log(df / max_exact) / math.log(REL_MAX_DIST / max_exact)
                         * (REL_BUCKETS - max_exact)).astype(jnp.int32)
    large = jnp.minimum(large, REL_BUCKETS - 1)
    return jnp.where(d < max_exact, d, large).astype(jnp.int32)


def _swa_bias(rel_bias, bucket):
    def body(rel_ref, bucket_ref, o_ref):
        b = bucket_ref[...]
        for h in range(SWA_HEADS):
            acc = jnp.zeros(b.shape, F32)
            for r in range(REL_BUCKETS):
                acc = jnp.where(b == r, rel_ref[r, h], acc)
            o_ref[h] = acc

    return pl.pallas_call(
        body, name="swa_bias",
        in_specs=[pl.BlockSpec(memory_space=pltpu.SMEM), pl.BlockSpec(memory_space=pltpu.VMEM)],
        out_specs=pl.BlockSpec(memory_space=pltpu.VMEM),
        out_shape=jax.ShapeDtypeStruct((SWA_HEADS, WINDOW, 2 * WINDOW), F32),
    )(rel_bias, bucket)


def _swa_bias_bwd(dbias, bucket):
    def body(db_ref, bucket_ref, o_ref):
        b = bucket_ref[...]
        lane = _lane((1, LANES))
        for r in range(REL_BUCKETS):
            row = jnp.zeros((1, LANES), F32)
            for h in range(SWA_HEADS):
                part = jnp.sum(jnp.where(b == r, db_ref[h], 0.0), axis=0, keepdims=True)
                tot = jnp.sum(part, axis=1, keepdims=True)
                row = jnp.where(lane == h, tot, row)
            o_ref[r:r + 1, :] = row

    return pl.pallas_call(
        body, name="swa_bias_bwd",
        in_specs=[pl.BlockSpec(memory_space=pltpu.VMEM), pl.BlockSpec(memory_space=pltpu.VMEM)],
        out_specs=pl.BlockSpec(memory_space=pltpu.VMEM),
        out_shape=jax.ShapeDtypeStruct((REL_BUCKETS, LANES), F32),
    )(dbias, bucket)


SWA_GROUP = SWA_HEADS // SWA_KV_HEADS


def _swa_valid(r, i):
    t = (lax.broadcasted_iota(jnp.int32, (SWA_GROUP * WINDOW, 2 * WINDOW), 0) & (WINDOW - 1)) + WINDOW
    s = lax.broadcasted_iota(jnp.int32, (SWA_GROUP * WINDOW, 2 * WINDOW), 1)
    dist = t - s
    band = (dist >= 0) & (dist < WINDOW)
    if r == 0:
        band = band & ((s >= WINDOW) | (i > 0))
    return band


def _swa_stack(blk):
    lane = _lane((WINDOW, LANES))
    parts = []
    for g in range(SWA_GROUP):
        b = blk[:, LANES * (g // 2):LANES * (g // 2 + 1)]
        parts.append(jnp.where((lane >= 64) if g % 2 else (lane < 64), b, jnp.zeros_like(b)))
    return jnp.concatenate(parts, axis=0)


def _swa_unstack(st):
    lane = _lane((WINDOW, LANES))
    W = WINDOW
    return jnp.concatenate([jnp.where(lane < 64, st[2 * b * W:(2 * b + 1) * W], st[(2 * b + 1) * W:(2 * b + 2) * W])
                            for b in range(2)], axis=1)


def _swa_sink_column(sink_ref, kvh):
    row = lax.broadcasted_iota(jnp.int32, (SWA_GROUP * WINDOW, 1), 0)
    col = jnp.full((SWA_GROUP * WINDOW, 1), sink_ref[SWA_GROUP * kvh + SWA_GROUP - 1], F32)
    for g in range(SWA_GROUP - 2, -1, -1):
        col = jnp.where(row < (g + 1) * WINDOW, sink_ref[SWA_GROUP * kvh + g], col)
    return col


def _swa_specs(T):
    W = WINDOW
    qspec = pl.BlockSpec((SWA_TB, 2 * LANES), lambda h, i: (i, h))
    own = pl.BlockSpec((None, SWA_TB, LANES), lambda h, i: (h, i, 0))
    prev = pl.BlockSpec((None, W, LANES), lambda h, i: (h, jnp.maximum(SWA_SUB * i - 1, 0), 0))
    stat = pl.BlockSpec((SWA_GROUP, SWA_TB, LANES), lambda h, i: (h, i, 0))
    bias = pl.BlockSpec((None, SWA_GROUP * W, 2 * W), lambda h, i: (h, 0, 0))
    return qspec, own, prev, stat, bias


def _swa_fwd(sinks, q, kad, vad, bias, T):
    nb = T // SWA_TB
    scale = SWA_HEAD_DIM ** -0.5
    W = WINDOW

    def body(sink_ref, q_ref, k_ref, kp_ref, v_ref, vp_ref, bias_ref, o_ref, lse_ref):
        kvh, i = pl.program_id(0), pl.program_id(1)
        sink = _swa_sink_column(sink_ref, kvh)
        for r in range(SWA_SUB):
            rs = slice(r * W, (r + 1) * W)
            ps = slice((r - 1) * W, r * W)
            k_own, v_own = k_ref[rs, :], v_ref[rs, :]
            k_prev = kp_ref[...] if r == 0 else k_ref[ps, :]
            v_prev = vp_ref[...] if r == 0 else v_ref[ps, :]
            qs = _swa_stack(q_ref[rs, :])
            s = jnp.concatenate([_dot(qs, k_prev, NT), _dot(qs, k_own, NT)], axis=1) * scale + bias_ref[...]
            s = jnp.where(_swa_valid(r, i), s, NEG)
            m = jnp.maximum(jnp.max(s, axis=1, keepdims=True), sink)
            p = jnp.exp(s - m)
            denom = jnp.sum(p, axis=1, keepdims=True) + jnp.exp(sink - m)
            pn = (p / denom).astype(BF16)
            o_ref[rs, :] = _swa_unstack(_dot(pn[:, :W], v_prev) + _dot(pn[:, W:], v_own)).astype(o_ref.dtype)
            lse = m + jnp.log(denom)
            for g in range(SWA_GROUP):
                lse_ref[g, rs, :] = jnp.broadcast_to(lse[g * W:(g + 1) * W], (W, LANES))

    qspec, own, prev, stat, bspec = _swa_specs(T)
    return pl.pallas_call(
        body, name="swa_fwd", grid=(SWA_KV_HEADS, nb),
        in_specs=[pl.BlockSpec(memory_space=pltpu.SMEM), qspec, own, prev, own, prev, bspec],
        out_specs=[qspec, stat],
        out_shape=[jax.ShapeDtypeStruct((T, 512), BF16), jax.ShapeDtypeStruct((SWA_HEADS, T, LANES), F32)],
        compiler_params=_cparams("parallel", "parallel", vmem=VMEM_MID),
    )(sinks, q, kad, kad, vad, vad, bias.reshape(SWA_KV_HEADS, SWA_GROUP * W, 2 * W))


def _swa_bwd(sinks, q, kad, vad, bias, do, lse, delta, T):
    nb = T // SWA_TB
    scale = SWA_HEAD_DIM ** -0.5
    W = WINDOW

    def body(sink_ref, q_ref, k_ref, kp_ref, v_ref, vp_ref, bias_ref, do_ref, lse_ref, dl_ref,
             dq_ref, dkad_ref, dvad_ref, dbias_ref, dsk_ref):
        kvh, i = pl.program_id(0), pl.program_id(1)
        sink = _swa_sink_column(sink_ref, kvh)

        @pl.when((kvh == 0) & (i == 0))
        def _():
            dkad_ref[...] = jnp.zeros_like(dkad_ref)
            dvad_ref[...] = jnp.zeros_like(dvad_ref)

        @pl.when(i == 0)
        def _():
            dbias_ref[...] = jnp.zeros_like(dbias_ref)
            dsk_ref[...] = jnp.zeros_like(dsk_ref)

        for r in range(SWA_SUB):
            rs = slice(r * W, (r + 1) * W)
            ps = slice((r - 1) * W, r * W)
            k_own, v_own = k_ref[rs, :], v_ref[rs, :]
            k_prev = kp_ref[...] if r == 0 else k_ref[ps, :]
            v_prev = vp_ref[...] if r == 0 else v_ref[ps, :]
            qs = _swa_stack(q_ref[rs, :])
            dos = _swa_stack(do_ref[rs, :])
            lse_b = jnp.concatenate([lse_ref[g, rs, :] for g in range(SWA_GROUP)], axis=0)
            dl_b = jnp.concatenate([dl_ref[g, rs, :] for g in range(SWA_GROUP)], axis=0)
            s = jnp.concatenate([_dot(qs, k_prev, NT), _dot(qs, k_own, NT)], axis=1) * scale + bias_ref[...]
            s = jnp.where(_swa_valid(r, i), s, NEG)
            p = jnp.exp(s - jnp.tile(lse_b, (1, 2)))
            dp = jnp.concatenate([_dot(dos, v_prev, NT), _dot(dos, v_own, NT)], axis=1)
            ds = p * (dp - jnp.tile(dl_b, (1, 2)))
            sink_term = jnp.exp(sink - lse_b) * dl_b
            for g in range(SWA_GROUP):
                dbias_ref[g] += ds[g * W:(g + 1) * W]
                dsk_ref[g:g + 1, :] += jnp.sum(sink_term[g * W:(g + 1) * W], axis=0, keepdims=True)
            dsb = ds.astype(BF16)
            pb = p.astype(BF16)
            dq_ref[rs, :] = _swa_unstack((_dot(dsb[:, :W], k_prev) + _dot(dsb[:, W:], k_own)) * scale)
            own_row = pl.multiple_of(i * SWA_TB + r * W, W)
            dkad_ref[kvh, pl.ds(own_row, W), :] += _dot(dsb[:, W:], qs, TN) * scale
            dvad_ref[kvh, pl.ds(own_row, W), :] += _dot(pb[:, W:], dos, TN)
            dk_prev = _dot(dsb[:, :W], qs, TN) * scale
            dv_prev = _dot(pb[:, :W], dos, TN)
            if r == 0:
                @pl.when(i > 0)
                def _():
                    prev_row = pl.multiple_of(i * SWA_TB - W, W)
                    dkad_ref[kvh, pl.ds(prev_row, W), :] += dk_prev
                    dvad_ref[kvh, pl.ds(prev_row, W), :] += dv_prev
            else:
                prev_row = pl.multiple_of(i * SWA_TB + (r - 1) * W, W)
                dkad_ref[kvh, pl.ds(prev_row, W), :] += dk_prev
                dvad_ref[kvh, pl.ds(prev_row, W), :] += dv_prev

    qspec, own, prev, stat, bspec = _swa_specs(T)
    full = pl.BlockSpec((SWA_KV_HEADS, T, LANES), lambda h, i: (0, 0, 0))
    return pl.pallas_call(
        body, name="swa_bwd", grid=(SWA_KV_HEADS, nb),
        in_specs=[pl.BlockSpec(memory_space=pltpu.SMEM), qspec, own, prev, own, prev, bspec, qspec, stat, stat],
        out_specs=[qspec, full, full, pl.BlockSpec((SWA_GROUP, W, 2 * W), lambda h, i: (h, 0, 0)),
                   pl.BlockSpec((None, 8, LANES), lambda h, i: (h, 0, 0))],
        out_shape=[jax.ShapeDtypeStruct((T, 512), F32), jax.ShapeDtypeStruct((SWA_KV_HEADS, T, LANES), F32),
                   jax.ShapeDtypeStruct((SWA_KV_HEADS, T, LANES), F32), jax.ShapeDtypeStruct((SWA_HEADS, W, 2 * W), F32),
                   jax.ShapeDtypeStruct((SWA_KV_HEADS, 8, LANES), F32)],
        compiler_params=_cparams("arbitrary", "arbitrary", vmem=VMEM_MID),
    )(sinks, q, kad, kad, vad, vad, bias.reshape(SWA_KV_HEADS, SWA_GROUP * W, 2 * W), do, lse, delta)


MEM_TQ = 4096


def _mem_fwd(q, mk, mv, T, tq):
    scale = MEM_HEAD_DIM ** -0.5

    def body(q_ref, k_ref, v_ref, o_ref, lse_ref):
        s = _dot(q_ref[...], k_ref[...], NT) * scale
        m = jnp.max(s, axis=1, keepdims=True)
        p = jnp.exp(s - m)
        l = jnp.sum(p, axis=1, keepdims=True)
        o_ref[...] = _dot((p / l).astype(BF16), v_ref[...]).astype(o_ref.dtype)
        lse_ref[...] = jnp.broadcast_to(m + jnp.log(l), (tq, LANES))

    qspec = pl.BlockSpec((tq, LANES), lambda h, i: (i, h))
    kspec = pl.BlockSpec((N_MEM, LANES), lambda h, i: (0, h))
    return pl.pallas_call(
        body, name="mem_fwd", grid=(MEM_HEADS, T // tq),
        in_specs=[qspec, kspec, kspec],
        out_specs=[qspec, pl.BlockSpec((None, tq, LANES), lambda h, i: (h, i, 0))],
        out_shape=[jax.ShapeDtypeStruct((T, 512), BF16), jax.ShapeDtypeStruct((MEM_HEADS, T, LANES), F32)],
        compiler_params=_cparams("parallel", "parallel"),
    )(q, mk, mv)


def _mem_bwd(q, mk, mv, do, lse, delta, T, tq):
    scale = MEM_HEAD_DIM ** -0.5
    rep = N_MEM // LANES

    def body(q_ref, k_ref, v_ref, do_ref, lse_ref, dl_ref, dq_ref, dk_ref, dv_ref):
        i = pl.program_id(1)

        @pl.when(i == 0)
        def _():
            dk_ref[...] = jnp.zeros_like(dk_ref)
            dv_ref[...] = jnp.zeros_like(dv_ref)

        qv, dov = q_ref[...], do_ref[...]
        s = _dot(qv, k_ref[...], NT) * scale
        p = jnp.exp(s - jnp.tile(lse_ref[...], (1, rep)))
        dp = _dot(dov, v_ref[...], NT)
        ds = p * (dp - jnp.tile(dl_ref[...], (1, rep)))
        dsb = ds.astype(BF16)
        dq_ref[...] = _dot(dsb, k_ref[...]) * scale
        dk_ref[...] += _dot(dsb, qv, TN) * scale
        dv_ref[...] += _dot(p.astype(BF16), dov, TN)

    qspec = pl.BlockSpec((tq, LANES), lambda h, i: (i, h))
    kspec = pl.BlockSpec((N_MEM, LANES), lambda h, i: (0, h))
    stat = pl.BlockSpec((None, tq, LANES), lambda h, i: (h, i, 0))
    return pl.pallas_call(
        body, name="mem_bwd", grid=(MEM_HEADS, T // tq),
        in_specs=[qspec, kspec, kspec, qspec, stat, stat],
        out_specs=[qspec, kspec, kspec],
        out_shape=[jax.ShapeDtypeStruct((T, 512), F32), jax.ShapeDtypeStruct((N_MEM, 512), F32),
                   jax.ShapeDtypeStruct((N_MEM, 512), F32)],
        compiler_params=_cparams("arbitrary", "arbitrary"),
    )(q, mk, mv, do, lse, delta)


def _mem_prep_fwd(mem, g_mem, w_kv, kn_gain, gm128):
    def body(mem_ref, g_ref, w_ref, kn_ref, gm_ref, memn_o, kv_o, mk_o, mv_o):
        xhat, _ = _rms_rows(mem_ref[...], None)
        memn = (xhat * g_ref[...]).astype(BF16)
        memn_o[...] = memn
        kv = _dot(memn, w_ref[...])
        kv_o[...] = kv
        gm = gm_ref[...]
        for c in range(4):
            sl = slice(c * LANES, (c + 1) * LANES)
            y, _ = _head_norm(kv[:, sl], gm, kn_ref[...])
            mk_o[:, sl] = y.astype(BF16)
        mv_o[...] = kv[:, 512:].astype(BF16)

    vm = pl.BlockSpec(memory_space=pltpu.VMEM)
    return pl.pallas_call(
        body, name="mem_prep_fwd", in_specs=[vm] * 5, out_specs=[vm] * 4,
        out_shape=[jax.ShapeDtypeStruct((N_MEM, D_MODEL), BF16), jax.ShapeDtypeStruct((N_MEM, D_MODEL), F32),
                   jax.ShapeDtypeStruct((N_MEM, 512), BF16), jax.ShapeDtypeStruct((N_MEM, 512), BF16)],
        compiler_params=pltpu.CompilerParams(vmem_limit_bytes=VMEM_MID),
    )(mem, g_mem, w_kv, kn_gain, gm128)


def _mem_prep_bwd(mem, g_mem, memn, kv, w_kv, kn_gain, gm128, dmk, dmv):
    def body(mem_ref, g_ref, memn_ref, kv_ref, w_ref, kn_ref, gm_ref, dmk_ref, dmv_ref, dw_o, dg_o, dkn_o, dkv_s):
        gm = gm_ref[...]
        dkn = jnp.zeros((1, LANES), F32)
        for c in range(4):
            sl = slice(c * LANES, (c + 1) * LANES)
            dx, dg = _head_norm_bwd(dmk_ref[:, sl], kv_ref[:, sl], gm, kn_ref[...])
            dkv_s[:, sl] = dx.astype(BF16)
            dkn = dkn + dg
        dkn_o[...] = dkn
        dkv_s[:, 512:] = dmv_ref[...].astype(BF16)
        dkv = dkv_s[...]
        dw_o[...] = _dot(memn_ref[...], dkv, TN)
        dmemn = _dot(dkv, w_ref[...], NT)
        xhat, _ = _rms_rows(mem_ref[...], None)
        dg_o[...] = jnp.sum(dmemn * xhat, axis=0, keepdims=True)

    vm = pl.BlockSpec(memory_space=pltpu.VMEM)
    return pl.pallas_call(
        body, name="mem_prep_bwd", in_specs=[vm] * 9, out_specs=[vm] * 3,
        out_shape=[jax.ShapeDtypeStruct((D_MODEL, D_MODEL), F32), jax.ShapeDtypeStruct((1, D_MODEL), F32),
                   jax.ShapeDtypeStruct((1, LANES), F32)],
        scratch_shapes=[pltpu.VMEM((N_MEM, D_MODEL), BF16)],
        compiler_params=pltpu.CompilerParams(vmem_limit_bytes=VMEM_MID),
    )(mem, g_mem, memn, kv, w_kv, kn_gain, gm128, dmk, dmv)


SLOT_O = D_MODEL // N_SHARD


def _merge_fwd(proj, b_gate, o3, w3, T, tb):
    def body(gl_ref, bg_ref, oa_ref, of_ref, om_ref, wa_ref, wf_ref, wm_ref, out_ref):
        o_refs = (oa_ref, of_ref, om_ref)
        w_refs = (wa_ref, wf_ref, wm_ref)
        for n in range(N_SHARD):
            acc = jnp.zeros((tb, SLOT_O), F32)
            for b in range(3):
                c0 = b * D_MODEL + n * SLOT_O
                g = jax.nn.sigmoid(gl_ref[:, c0:c0 + SLOT_O] + bg_ref[:, c0:c0 + SLOT_O])
                acc = acc + g * _dot(o_refs[b][...], w_refs[b][n])
            out_ref[:, n * SLOT_O:(n + 1) * SLOT_O] = acc.astype(out_ref.dtype)

    rows = pl.BlockSpec((tb, 512), lambda i: (i, 0))
    wspec = pl.BlockSpec((N_SHARD, 512, SLOT_O), lambda i: (0, 0, 0))
    return pl.pallas_call(
        body, name="merge_fwd", grid=(T // tb,),
        in_specs=[pl.BlockSpec((tb, GATE_W), lambda i: (i, 1)), pl.BlockSpec((1, GATE_W), lambda i: (0, 0)),
                  rows, rows, rows, wspec, wspec, wspec],
        out_specs=pl.BlockSpec((tb, D_MODEL), lambda i: (i, 0)),
        out_shape=jax.ShapeDtypeStruct((T, D_MODEL), BF16),
        compiler_params=_cparams("parallel", vmem=VMEM_BIG),
    )(proj, b_gate, *o3, *w3)


def _merge_bwd(proj, b_gate, o3, w3, dmerged, T, tb):
    heads = (SWA_HEADS, FOX_HEADS, MEM_HEADS)

    def body(gl_ref, bg_ref, oa_ref, of_ref, om_ref, wa_ref, wf_ref, wm_ref, dm_ref,
             dgl_o, doa_o, dof_o, dom_o, dla_o, dlf_o, dlm_o, dwa_o, dwf_o, dwm_o, dbg_o):
        i = pl.program_id(0)
        o_refs = (oa_ref, of_ref, om_ref)
        w_refs = (wa_ref, wf_ref, wm_ref)
        do_refs = (doa_o, dof_o, dom_o)
        dl_refs = (dla_o, dlf_o, dlm_o)
        dw_refs = (dwa_o, dwf_o, dwm_o)

        @pl.when(i == 0)
        def _():
            for r in dw_refs:
                r[...] = jnp.zeros_like(r)
            dbg_o[...] = jnp.zeros_like(dbg_o)

        lane = _lane((tb, LANES))
        for b in range(3):
            ob = o_refs[b][...]
            do = jnp.zeros((tb, 512), F32)
            for n in range(N_SHARD):
                c0 = b * D_MODEL + n * SLOT_O
                g = jax.nn.sigmoid(gl_ref[:, c0:c0 + SLOT_O] + bg_ref[:, c0:c0 + SLOT_O])
                dm = dm_ref[:, n * SLOT_O:(n + 1) * SLOT_O]
                y = _dot(ob, w_refs[b][n])
                dgl = dm * y * g * (1.0 - g)
                dgl_o[:, c0:c0 + SLOT_O] = dgl.astype(dgl_o.dtype)
                dbg_o[:, c0:c0 + SLOT_O] += jnp.sum(dgl, axis=0, keepdims=True)
                dy = (dm * g).astype(BF16)
                do = do + _dot(dy, w_refs[b][n], NT)
                dw_refs[b][n] += _dot(ob, dy, TN)
            do_refs[b][...] = do.astype(BF16)
            prod = do * ob.astype(F32)
            for c in range(4):
                blk = prod[:, c * LANES:(c + 1) * LANES]
                if heads[b] == 8:
                    lo = jnp.sum(jnp.where(lane < 64, blk, 0.0), axis=1, keepdims=True)
                    hi = jnp.sum(jnp.where(lane >= 64, blk, 0.0), axis=1, keepdims=True)
                    if b == 1:
                        aug = jnp.zeros((tb, LANES), F32)
                        for sub, dl in enumerate((lo, hi)):
                            for e, piece in enumerate(_split3(-dl)):
                                aug = jnp.where(lane == AUG_STRIDE * sub + AUG_C + e, piece.astype(F32), aug)
                        dl_refs[b][:, c * LANES:(c + 1) * LANES] = aug.astype(BF16)
                    else:
                        dl_refs[b][2 * c] = jnp.broadcast_to(lo, (tb, LANES))
                        dl_refs[b][2 * c + 1] = jnp.broadcast_to(hi, (tb, LANES))
                else:
                    dl_refs[b][c] = jnp.broadcast_to(jnp.sum(blk, axis=1, keepdims=True), (tb, LANES))

    rows = pl.BlockSpec((tb, 512), lambda i: (i, 0))
    wspec = pl.BlockSpec((N_SHARD, 512, SLOT_O), lambda i: (0, 0, 0))
    stat = lambda h: pl.BlockSpec((h, tb, LANES), lambda i: (0, i, 0))
    return pl.pallas_call(
        body, name="merge_bwd", grid=(T // tb,),
        in_specs=[pl.BlockSpec((tb, GATE_W), lambda i: (i, 1)), pl.BlockSpec((1, GATE_W), lambda i: (0, 0)),
                  rows, rows, rows, wspec, wspec, wspec, pl.BlockSpec((tb, D_MODEL), lambda i: (i, 0))],
        out_specs=[pl.BlockSpec((tb, GATE_W), lambda i: (i, 0)), rows, rows, rows,
                   stat(8), rows, stat(4), wspec, wspec, wspec, pl.BlockSpec((1, GATE_W), lambda i: (0, 0))],
        out_shape=[jax.ShapeDtypeStruct((T, GATE_W), BF16)] + [jax.ShapeDtypeStruct((T, 512), BF16)] * 3
        + [jax.ShapeDtypeStruct((8, T, LANES), F32), jax.ShapeDtypeStruct((T, 512), BF16),
           jax.ShapeDtypeStruct((4, T, LANES), F32)]
        + [jax.ShapeDtypeStruct((N_SHARD, 512, SLOT_O), F32)] * 3 + [jax.ShapeDtypeStruct((1, GATE_W), F32)],
        compiler_params=_cparams("arbitrary", vmem=VMEM_BIG),
    )(proj, b_gate, *o3, *w3, dmerged)


def _local_step(x, h, mem, tgt, small, g_in, w_kv, w_o3, w_out, w_up, w_down, reducer):
    T = x.shape[0]
    tm = min(512, T)
    tile2 = lambda v: jnp.tile(v.reshape(1, -1), (1, LANES // v.size))
    gains = jnp.concatenate([tile2(small["qn_swa"]), tile2(small["kn_swa"]), tile2(small["qn_fox"]),
                             tile2(small["kn_fox"]), tile2(small["qn_mem"]), jnp.zeros((3, LANES), F32)], axis=0)
    kn_mem = small["kn_mem"].reshape(1, LANES)
    bfor = jnp.pad(small["b_forget"].reshape(1, -1), ((0, 0), (0, LANES - FOX_HEADS)))
    gm64 = _group_mean_matrix(64)
    gm128 = _group_mean_matrix(128)
    tb_prep = min(512, T)
    ones = jnp.ones((tb_prep, tb_prep), F32)
    tril = jnp.tril(ones).astype(BF16)
    triu = jnp.triu(ones).astype(BF16)
    bucket = _t5_bucket_matrix()
    g_mix, g_mlp, g_mem = small["g_mix"], small["g_mlp"], small["g_mem"]
    b_gate = small["b_gate"]
    sinks = small["sink_swa"].reshape(-1)

    tl = min(1024, T)
    sq = pl.BlockSpec((tl, D_MODEL), lambda i, j, k: (i, j))
    wc = _w_in_to_segments(g_in)
    (proj,) = _matmul(
        "mm_proj", h, wc, dims=NN, grid=(T // tl, PROJ_W // PROJ_TN, 1),
        a_spec=pl.BlockSpec((tl, D_MODEL), lambda i, j, k: (i, 0)),
        b_spec=pl.BlockSpec((D_MODEL, PROJ_TN), lambda i, j, k: (0, j)),
        acc_shape=(tl, PROJ_TN),
        outs=[(jax.ShapeDtypeStruct((T, PROJ_W), F32), pl.BlockSpec((tl, PROJ_TN), lambda i, j, k: (i, j)))],
        epilogue=_epi_store)
    qa, qf, kf, vf, qm, kad, vad, qf_aug, kf_aug = _prep_fwd(proj, gains, bfor, tril, gm64, gm128, T, tb_prep)
    bias = _swa_bias(small["rel_bias"], bucket)
    o_swa, lse_swa = _swa_fwd(sinks, qa, kad, vad, bias, T)
    o_fox, qf_aug_bwd = _fox_fwd(qf, qf_aug, kf, kf_aug, vf, T, min(FOX_TQ, T), min(FOX_TK, T))
    memn, kv, mk, mv = _mem_prep_fwd(mem, g_mem, w_kv, kn_mem, gm128)
    o_mem, lse_mem = _mem_fwd(qm, mk, mv, T, min(MEM_TQ, T))
    o3 = (o_swa, o_fox, o_mem)
    merged = _merge_fwd(proj, b_gate, o3, w_o3, T, min(512, T))

    def epi_residual(acc, extra_refs, out_refs, ij):
        out_refs[0][...] = extra_refs[0][...] + acc

    row_full = pl.BlockSpec((tm, D_MODEL), lambda i, j, k: (i, 0))
    row_big = pl.BlockSpec((tl, D_MODEL), lambda i, j, k: (i, 0))
    whole = pl.BlockSpec((D_MODEL, D_MODEL), lambda i, j, k: (0, 0))
    (x2,) = _matmul(
        "mm_out", merged, w_out, dims=NN, grid=(T // tl, 1, 1),
        a_spec=row_big, b_spec=whole,
        acc_shape=(tl, D_MODEL), extra=[(x, row_big)],
        outs=[(jax.ShapeDtypeStruct((T, D_MODEL), F32), row_big)], epilogue=epi_residual)
    hm = _rmsnorm("rms_mlp", x2, g_mlp, tl)

    def epi_relu2(acc, extra_refs, out_refs, ij):
        out_refs[0][...] = acc.astype(BF16)
        r = jnp.maximum(acc, 0.0)
        out_refs[1][...] = (r * r).astype(BF16)

    up, u = _matmul(
        "mm_up", hm, w_up, dims=NN, grid=(T // tl, N_SHARD, 1),
        a_spec=row_big, b_spec=pl.BlockSpec((None, D_MODEL, D_MODEL), lambda i, j, k: (j, 0, 0)),
        acc_shape=(tl, D_MODEL),
        outs=[(jax.ShapeDtypeStruct((T, D_FF), BF16), sq), (jax.ShapeDtypeStruct((T, D_FF), BF16), sq)],
        epilogue=epi_relu2)

    def epi_loss(acc, extra_refs, out_refs, ij):
        y = extra_refs[0][...] + acc
        err = y - extra_refs[1][...]
        dyv = err * (1.0 / D_MODEL)
        out_refs[0][...] = dyv
        out_refs[2][...] = dyv.astype(BF16)
        sq = jnp.sum(jnp.sum(err * err, axis=1, keepdims=True), axis=0, keepdims=True)

        @pl.when(ij[0] == 0)
        def _():
            out_refs[1][...] = jnp.zeros_like(out_refs[1])

        out_refs[1][...] += jnp.broadcast_to(sq, out_refs[1].shape)

    kblk = pl.BlockSpec((tl, D_MODEL), lambda i, j, k: (i, k))
    dy, loss_acc, dy_bf = _matmul(
        "mm_down", u, w_down, dims=NN, grid=(T // tl, 1, N_SHARD),
        a_spec=kblk, b_spec=pl.BlockSpec((D_MODEL, D_MODEL), lambda i, j, k: (k, 0)),
        acc_shape=(tl, D_MODEL), extra=[(x2, row_big), (tgt, row_big)],
        outs=[(jax.ShapeDtypeStruct((T, D_MODEL), F32), row_big),
              (jax.ShapeDtypeStruct((8, LANES), F32), pl.BlockSpec((8, LANES), lambda i, j, k: (0, 0))),
              (jax.ShapeDtypeStruct((T, D_MODEL), BF16), row_big)],
        epilogue=epi_loss)
    loss = loss_acc[0, 0] * (0.5 / D_MODEL)

    def epi_dup(acc, extra_refs, out_refs, ij):
        out_refs[0][...] = (acc * (2.0 * jnp.maximum(extra_refs[0][...].astype(F32), 0.0))).astype(BF16)

    (dup,) = _matmul(
        "mm_dup", dy_bf, w_down, dims=NT, grid=(T // tl, N_SHARD, 1),
        a_spec=row_big, b_spec=pl.BlockSpec((D_MODEL, D_MODEL), lambda i, j, k: (j, 0)),
        acc_shape=(tl, D_MODEL), extra=[(up, sq)],
        outs=[(jax.ShapeDtypeStruct((T, D_FF), BF16), sq)], epilogue=epi_dup)

    nkt = T // tl
    t_rows = pl.BlockSpec((tl, D_MODEL), lambda i, j, k: (k, i))
    t_cols = pl.BlockSpec((tl, D_MODEL), lambda i, j, k: (k, j))
    (d_w_down,) = _matmul(
        "mm_dw_down", u, dy_bf, dims=TN, grid=(N_SHARD, 1, nkt),
        a_spec=t_rows, b_spec=t_cols, acc_shape=(D_MODEL, D_MODEL),
        outs=[(jax.ShapeDtypeStruct((D_FF, D_MODEL), F32), pl.BlockSpec((D_MODEL, D_MODEL), lambda i, j, k: (i, 0)))],
        epilogue=_epi_store)
    (d_w_up,) = _matmul(
        "mm_dw_up", hm, dup, dims=TN, grid=(1, N_SHARD, nkt),
        a_spec=t_rows, b_spec=t_cols, acc_shape=(D_MODEL, D_MODEL),
        outs=[(jax.ShapeDtypeStruct((N_SHARD, D_MODEL, D_MODEL), F32),
               pl.BlockSpec((None, D_MODEL, D_MODEL), lambda i, j, k: (j, 0, 0)))],
        epilogue=_epi_store)

    def epi_rms_bwd(acc, extra_refs, out_refs, ij):
        dx, dg = _rmsnorm_bwd_rows(acc, extra_refs[0][...], extra_refs[1][...])
        out_refs[0][...] = dx + extra_refs[2][...]

        @pl.when(ij[0] == 0)
        def _():
            out_refs[1][...] = jnp.zeros_like(out_refs[1])

        out_refs[1][...] += dg

    gain_spec = pl.BlockSpec((1, D_MODEL), lambda i, j, k: (0, 0))
    dx2, d_g_mlp = _matmul(
        "mm_dhm", dup, w_up, dims=NT, grid=(T // tl, 1, N_SHARD),
        a_spec=kblk, b_spec=pl.BlockSpec((None, D_MODEL, D_MODEL), lambda i, j, k: (k, 0, 0)),
        acc_shape=(tl, D_MODEL), extra=[(x2, row_big), (g_mlp, gain_spec), (dy, row_big)],
        outs=[(jax.ShapeDtypeStruct((T, D_MODEL), F32), row_big), (jax.ShapeDtypeStruct((1, D_MODEL), F32), gain_spec)],
        epilogue=epi_rms_bwd)

    (dmerged,) = _matmul(
        "mm_dmerged", dx2, w_out, dims=NT, grid=(T // tl, 1, 1),
        a_spec=row_big, b_spec=whole,
        acc_shape=(tl, D_MODEL), outs=[(jax.ShapeDtypeStruct((T, D_MODEL), F32), row_big)], epilogue=_epi_store)
    (d_w_out,) = _matmul(
        "mm_dw_out", merged, dx2, dims=TN, grid=(1, 1, nkt),
        a_spec=t_rows, b_spec=t_cols, acc_shape=(D_MODEL, D_MODEL),
        outs=[(jax.ShapeDtypeStruct((D_MODEL, D_MODEL), F32), whole)],
        epilogue=_epi_store)
    (dgl, do_swa, do_fox, do_mem, dl_swa, do_fox_aug, dl_mem, d_wo_swa, d_wo_fox, d_wo_mem, d_b_gate) = _merge_bwd(
        proj, b_gate, o3, w_o3, dmerged, T, min(512, T))

    dqm, dmk, dmv = _mem_bwd(qm, mk, mv, do_mem, lse_mem, dl_mem, T, min(MEM_TQ, T))
    d_w_kv, d_g_mem, d_kn_mem = _mem_prep_bwd(mem, g_mem, memn, kv, w_kv, kn_mem, gm128, dmk, dmv)
    do_swa = reducer.early_start({"w_mlp_down": d_w_down, "w_mlp_up": d_w_up, "w_out": d_w_out, "w_mem_kv": d_w_kv,
                                  "w_o_swa": d_wo_swa, "w_o_fox": d_wo_fox, "w_o_mem": d_wo_mem}, do_swa)
    dqa, dkad, dvad, dbias, dsk = _swa_bwd(sinks, qa, kad, vad, bias, do_swa, lse_swa, dl_swa, T)
    dqa, do_fox = reducer.early_send((dqa, do_fox))
    dqf, dqf_aug, dkf, dkf_aug, dvf = _fox_bwd(qf, qf_aug_bwd, kf, kf_aug, vf, do_fox, do_fox_aug, T,
                                               min(FOX_BWD_TQ, T), min(FOX_BWD_TK, T))
    dvf = reducer.early_finish(dvf)
    d_rel = _swa_bias_bwd(dbias, bucket)
    dlo, gacc = _prep_bwd(proj, dqa, dkad, dvad, dqf, dkf, dvf, dqm, dqf_aug, dkf_aug, gains, bfor, triu, gm64, gm128,
                          T, tb_prep)

    def dwc_half(name, dpart):
        (res,) = _matmul(
            name, h, dpart, dims=TN, grid=(1, LO_W // D_MODEL, nkt),
            a_spec=t_rows, b_spec=t_cols, acc_shape=(D_MODEL, D_MODEL),
            outs=[(jax.ShapeDtypeStruct((D_MODEL, LO_W), F32), pl.BlockSpec((D_MODEL, D_MODEL), lambda i, j, k: (0, j)))],
            epilogue=_epi_store)
        return res

    d_wc_lo = dwc_half("mm_dwc_lo", dlo)
    d_wc_gl = dwc_half("mm_dwc_gl", dgl)
    dlo = reducer.late_start({"wc_lo": d_wc_lo, "wc_gl": d_wc_gl}, dlo)
    (dh_lo,) = _matmul(
        "mm_dh_lo", dlo, wc, dims=NT, grid=(T // tl, 1, LO_W // D_MODEL),
        a_spec=kblk, b_spec=pl.BlockSpec((D_MODEL, D_MODEL), lambda i, j, k: (0, k)),
        acc_shape=(tl, D_MODEL), outs=[(jax.ShapeDtypeStruct((T, D_MODEL), F32), row_big)], epilogue=_epi_store)
    dh_lo = reducer.late_send(dh_lo)

    def epi_dx(acc, extra_refs, out_refs, ij):
        dhh = acc + extra_refs[3][...]
        dx, dg = _rmsnorm_bwd_rows(dhh, extra_refs[0][...], extra_refs[1][...])
        out_refs[0][...] = dx + extra_refs[2][...]

        @pl.when(ij[0] == 0)
        def _():
            out_refs[1][...] = jnp.zeros_like(out_refs[1])

        out_refs[1][...] += dg

    grad_x, d_g_mix = _matmul(
        "mm_dh_gl", dgl, wc, dims=NT, grid=(T // tl, 1, GATE_W // D_MODEL),
        a_spec=kblk, b_spec=pl.BlockSpec((D_MODEL, D_MODEL), lambda i, j, k: (0, k + LO_W // D_MODEL)),
        acc_shape=(tl, D_MODEL), extra=[(x, row_big), (g_mix, gain_spec), (dx2, row_big), (dh_lo, row_big)],
        outs=[(jax.ShapeDtypeStruct((T, D_MODEL), F32), row_big), (jax.ShapeDtypeStruct((1, D_MODEL), F32), gain_spec)],
        epilogue=epi_dx, vmem=VMEM_MAX)

    fold64 = lambda row: (row[:64] + row[64:]).reshape(1, 64)
    grads = {
        "g_mix": d_g_mix, "b_gate": d_b_gate, "b_forget": gacc[5, :FOX_HEADS].reshape(1, FOX_HEADS),
        "qn_swa": fold64(gacc[0]), "kn_swa": fold64(gacc[1]),
        "sink_swa": -dsk[:, :SWA_GROUP, 0].reshape(1, SWA_HEADS), "rel_bias": d_rel[:, :SWA_HEADS],
        "qn_fox": fold64(gacc[2]), "kn_fox": fold64(gacc[3]),
        "g_mem": d_g_mem, "qn_mem": gacc[4].reshape(1, LANES), "kn_mem": d_kn_mem, "g_mlp": d_g_mlp,
    }
    return loss, grad_x, grads


MESH = pl.DeviceIdType.MESH


def _place():
    x, y, c = lax.axis_index("x"), lax.axis_index("y"), lax.axis_index("c")
    chips = [(1 - x, y), (x, 1 - y), (1 - x, 1 - y)]
    return x, y, c, chips


def _handshake(peers):
    barrier = pltpu.get_barrier_semaphore()
    for peer in peers:
        pl.semaphore_signal(barrier, inc=1, device_id=peer, device_id_type=MESH)
    pl.semaphore_wait(barrier, len(peers))


def _all_gather_shards_async(name, collective_id, slots):
    n = len(slots)
    bufs = [jax.new_ref(s, memory_space=pltpu.MemorySpace.HBM) for s in slots]

    def body(ici_send, ici_recv, d2d_send, d2d_recv):
        x, y, c, chips = _place()
        sibling = (x, y, 1 - c)
        me = 2 * x + y
        _handshake([(px, py, c) for px, py in chips] + [sibling])

        def half(a, who):
            hr = slots[a].shape[1] // 2
            return pl.ds(pl.multiple_of(who * hr, hr), hr)

        def ici(a, j, slot, to):
            return pltpu.make_async_remote_copy(
                src_ref=bufs[a].at[me, half(a, c)], dst_ref=bufs[a].at[slot, half(a, c)],
                send_sem=ici_send.at[3 * a + j], recv_sem=ici_recv.at[3 * a + j], device_id=to, device_id_type=MESH)

        def d2d(a, j, slot, which):
            part = bufs[a].at[slot, half(a, which)]
            return pltpu.make_async_remote_copy(
                src_ref=part, dst_ref=part, send_sem=d2d_send.at[3 * a + j], recv_sem=d2d_recv.at[3 * a + j],
                device_id=sibling, device_id_type=MESH)

        sends = [ici(a, j, me, (*chip, c)) for a in range(n) for j, chip in enumerate(chips)]
        for cp in sends:
            cp.start()
        passed = []
        for a in range(n):
            for j, (px, py) in enumerate(chips):
                ici(a, j, 2 * px + py, (px, py, c)).wait_recv()
                cp = d2d(a, j, 2 * px + py, c)
                cp.start()
                passed.append(cp)
        for a in range(n):
            for j, (px, py) in enumerate(chips):
                d2d(a, j, 2 * px + py, 1 - c).wait_recv()
        for cp in sends + passed:
            cp.wait_send()

    pl.kernel(
        body, mesh=plsc.ScalarSubcoreMesh(axis_name="seq", num_cores=1), name=name,
        scratch_types=[pltpu.SemaphoreType.DMA((3 * n,))] * 4,
        compiler_params=pltpu.CompilerParams(collective_id=collective_id),
    )()
    return [b[...] for b in bufs]


def _sequencer_call(name, collective_id, n_sems, body):
    pl.kernel(
        body, mesh=plsc.ScalarSubcoreMesh(axis_name="seq", num_cores=1), name=name,
        scratch_types=[pltpu.SemaphoreType.DMA((n_sems,))] * 2,
        compiler_params=pltpu.CompilerParams(collective_id=collective_id),
    )()


def _hbm_ref(value):
    return jax.new_ref(value, memory_space=pltpu.MemorySpace.HBM)


def _all_gather_neighbours(name, collective_id, slots):
    buf = jax.new_ref(slots, memory_space=pltpu.MemorySpace.HBM)
    hr = slots.shape[1] // 2
    qr = hr // 2

    def body(ici_send, ici_recv, fwd_send, fwd_recv, d2d_send, d2d_recv):
        x, y, c, _ = _place()
        sibling = (x, y, 1 - c)
        nbr = [(1 - x, y), (x, 1 - y)]
        diag = (1 - x, 1 - y)
        slot_of = lambda chip: 2 * chip[0] + chip[1]
        me = 2 * x + y
        _handshake([(*nbr[0], c), (*nbr[1], c), sibling])

        def rows(core, quarter=None):
            start = core * hr if quarter is None else core * hr + quarter * qr
            size = hr if quarter is None else qr
            return pl.ds(pl.multiple_of(start, size), size)

        def copy(part, send_sem, recv_sem, to):
            return pltpu.make_async_remote_copy(src_ref=part, dst_ref=part, send_sem=send_sem, recv_sem=recv_sem,
                                                device_id=to, device_id_type=MESH)

        sends = [copy(buf.at[me, rows(c)], ici_send.at[j], ici_recv.at[j], (*nbr[j], c)) for j in range(2)]
        for cp in sends:
            cp.start()
        moving = []
        for j in range(2):
            theirs = slot_of(nbr[j])
            copy(buf.at[theirs, rows(c)], ici_send.at[j], ici_recv.at[j], (*nbr[j], c)).wait_recv()
            on = copy(buf.at[theirs, rows(c, j)], fwd_send.at[j], fwd_recv.at[j], (*nbr[1 - j], c))
            down = copy(buf.at[theirs, rows(c)], d2d_send.at[j], d2d_recv.at[j], sibling)
            on.start()
            down.start()
            moving += [on, down]
        for j in range(2):
            part = buf.at[slot_of(diag), rows(c, j)]
            copy(part, fwd_send.at[j], fwd_recv.at[j], (*nbr[1 - j], c)).wait_recv()
            down = copy(part, d2d_send.at[2 + j], d2d_recv.at[2 + j], sibling)
            down.start()
            moving.append(down)
        for j in range(2):
            copy(buf.at[slot_of(nbr[j]), rows(1 - c)], d2d_send.at[j], d2d_recv.at[j], sibling).wait_recv()
            copy(buf.at[slot_of(diag), rows(1 - c, j)], d2d_send.at[2 + j], d2d_recv.at[2 + j], sibling).wait_recv()
        for cp in sends + moving:
            cp.wait_send()

    pl.kernel(
        body, mesh=plsc.ScalarSubcoreMesh(axis_name="seq", num_cores=1), name=name,
        scratch_types=[pltpu.SemaphoreType.DMA((2,))] * 4 + [pltpu.SemaphoreType.DMA((4,))] * 2,
        compiler_params=pltpu.CompilerParams(collective_id=collective_id),
    )()
    return buf[...]


def _pair_exchange(name, collective_id, gs):
    n = len(gs)
    src = [_hbm_ref(g) for g in gs]
    stage = [jax.empty_ref(jax.ShapeDtypeStruct((N_SHARD, g.shape[1] // 2, g.shape[2]), g.dtype),
                           memory_space=pltpu.MemorySpace.HBM) for g in gs]

    def body(send_sem, recv_sem):
        x, y, c, _ = _place()
        sibling = (x, y, 1 - c)
        _handshake([sibling])
        copies = []
        for a in range(n):
            hr = gs[a].shape[1] // 2
            theirs = pl.ds(pl.multiple_of((1 - c) * hr, hr), hr)
            copies.append(pltpu.make_async_remote_copy(
                src_ref=src[a].at[:, theirs, :], dst_ref=stage[a], send_sem=send_sem.at[a], recv_sem=recv_sem.at[a],
                device_id=sibling, device_id_type=MESH))
        for cp in copies:
            cp.start()
        for cp in copies:
            cp.wait()

    _sequencer_call(name, collective_id, n, body)
    return [s[...] for s in stage]


def _chip_exchange(name, collective_id, sums):
    n = len(sums)
    src = [_hbm_ref(s) for s in sums]
    got = [jax.empty_ref(jax.ShapeDtypeStruct((3,) + s.shape[1:], s.dtype), memory_space=pltpu.MemorySpace.HBM)
           for s in sums]

    def body(send_sem, recv_sem):
        x, y, c, chips = _place()
        _handshake([(px, py, c) for px, py in chips])
        copies = []
        for a in range(n):
            for j, (px, py) in enumerate(chips):
                copies.append(pltpu.make_async_remote_copy(
                    src_ref=src[a].at[2 * px + py], dst_ref=got[a].at[j],
                    send_sem=send_sem.at[3 * a + j], recv_sem=recv_sem.at[3 * a + j],
                    device_id=(px, py, c), device_id_type=MESH))
        for cp in copies:
            cp.start()
        for cp in copies:
            cp.wait()

    _sequencer_call(name, collective_id, 3 * n, body)
    return [g[...] for g in got]


def _pair_gather(name, collective_id, fulls):
    n = len(fulls)
    full = [_hbm_ref(f) for f in fulls]

    def body(send_sem, recv_sem):
        x, y, c, _ = _place()
        sibling = (x, y, 1 - c)
        _handshake([sibling])
        copies = []
        for a in range(n):
            hr = fulls[a].shape[0] // 2
            mine = full[a].at[pl.ds(pl.multiple_of(c * hr, hr), hr)]
            copies.append(pltpu.make_async_remote_copy(
                src_ref=mine, dst_ref=mine, send_sem=send_sem.at[a], recv_sem=recv_sem.at[a],
                device_id=sibling, device_id_type=MESH))
        for cp in copies:
            cp.start()
        for cp in copies:
            cp.wait()

    _sequencer_call(name, collective_id, n, body)
    return [f[...] for f in full]


ELEMENTWISE_BLOCK_ELEMS = 512 * 1024


def _row_block(rows, cols):
    rb = 8
    while rb * 2 * cols <= ELEMENTWISE_BLOCK_ELEMS and rb * 2 <= rows:
        rb *= 2
    return rb


def _pair_sum(name, place, g, stage):
    _, R, C = g.shape
    hr = R // 2
    rb = _row_block(hr, C)
    nb = hr // rb

    def body(place_ref, g_ref, st_ref, sum_bf, own_f32):
        s = pl.program_id(1)
        tot = g_ref[...] + st_ref[...]
        sum_bf[...] = tot.astype(BF16)

        @pl.when(s == place_ref[0])
        def _():
            own_f32[...] = tot

    return pl.pallas_call(
        body, name=name,
        grid_spec=pltpu.PrefetchScalarGridSpec(
            num_scalar_prefetch=1, grid=(nb, N_SHARD),
            in_specs=[pl.BlockSpec((None, rb, C), lambda i, s, pr: (s, pr[1] * nb + i, 0)),
                      pl.BlockSpec((None, rb, C), lambda i, s, pr: (s, i, 0))],
            out_specs=[pl.BlockSpec((None, rb, C), lambda i, s, pr: (s, i, 0)),
                       pl.BlockSpec((rb, C), lambda i, s, pr: (i, 0))]),
        out_shape=[jax.ShapeDtypeStruct((N_SHARD, hr, C), BF16), jax.ShapeDtypeStruct((hr, C), F32)],
        compiler_params=_cparams("arbitrary", "arbitrary"),
    )(place, g, stage)


def _final_sum(name, place, own, got):
    hr, C = own.shape
    rb = _row_block(hr, C)
    nb = hr // rb

    def body(place_ref, own_ref, got_ref, o_ref):
        o_ref[...] = ((own_ref[...] + got_ref[0].astype(F32)) + got_ref[1].astype(F32)) + got_ref[2].astype(F32)

    return pl.pallas_call(
        body, name=name,
        grid_spec=pltpu.PrefetchScalarGridSpec(
            num_scalar_prefetch=1, grid=(nb,),
            in_specs=[pl.BlockSpec((rb, C), lambda i, pr: (i, 0)), pl.BlockSpec((3, rb, C), lambda i, pr: (0, i, 0))],
            out_specs=pl.BlockSpec((rb, C), lambda i, pr: (pr[1] * nb + i, 0))),
        out_shape=jax.ShapeDtypeStruct((2 * hr, C), F32),
        compiler_params=_cparams("arbitrary"),
    )(place, own, got)


def _adamw_math(w, g, m, v):
    m = ADAM_B1 * m + (1.0 - ADAM_B1) * g
    v = ADAM_B2 * v + (1.0 - ADAM_B2) * (g * g)
    m_hat = m / (1.0 - ADAM_B1 ** ADAM_STEP)
    v_hat = v / (1.0 - ADAM_B2 ** ADAM_STEP)
    delta = -ADAM_LR * (m_hat / (jnp.sqrt(v_hat) + ADAM_EPS) + ADAM_WD * w)
    return delta, m, v


def _adamw(name, w, g, m, v):
    R, Cw = w.shape
    Cg = g.shape[1]
    rb = _row_block(R, Cg)

    def body(w_ref, g_ref, m_ref, v_ref, g_o, d_o, m_o, v_o):
        gv = g_ref[...]
        delta, mn, vn = _adamw_math(w_ref[...], gv, m_ref[...], v_ref[...])
        g_o[...] = gv
        d_o[...] = delta
        m_o[...] = mn
        v_o[...] = vn

    blk = pl.BlockSpec((rb, Cg), lambda i: (i, 0))
    return pl.pallas_call(
        body, name=name, grid=(R // rb,),
        in_specs=[blk] * 4, out_specs=[blk] * 4,
        out_shape=[jax.ShapeDtypeStruct((R, Cw), F32)] * 4,
        compiler_params=_cparams("parallel"),
    )(w, g, m, v)


N_DEV = 8
SMALL_ROWS = 64


def _small_allreduce_adamw(g, w, m, v):
    def body(g_ref, w_ref, m_ref, v_ref, all_ref, gs_o, d_o, m_o, v_o, send_sems, recv_sems, local_sem):
        x, y, c, chips = _place()
        me, sibling = (x, y, c), (x, y, 1 - c)

        def rows(px, py, pc):
            return all_ref.at[pl.ds(pl.multiple_of((4 * px + 2 * py + pc) * SMALL_ROWS, SMALL_ROWS), SMALL_ROWS), :]

        def copy(k, block, to, src=None):
            return pltpu.make_async_remote_copy(
                src_ref=rows(*block) if src is None else src, dst_ref=rows(*block),
                send_sem=send_sems.at[k], recv_sem=recv_sems.at[k], device_id=to, device_id_type=MESH)

        mine = pltpu.make_async_copy(g_ref, rows(*me), local_sem)
        mine.start()
        first = [copy(0, me, sibling, src=g_ref)]
        first += [copy(1 + j, me, (*chip, c), src=g_ref) for j, chip in enumerate(chips)]
        for cp in first:
            cp.start()
        passed = [copy(4 + j, (*chip, c), sibling) for j, chip in enumerate(chips)]
        for j, chip in enumerate(chips):
            copy(1 + j, (*chip, c), me).wait_recv()
            passed[j].start()
        copy(0, sibling, me).wait_recv()
        for j, chip in enumerate(chips):
            copy(4 + j, (*chip, 1 - c), me).wait_recv()
        for cp in first + passed:
            cp.wait_send()
        mine.wait()

        tot = all_ref[0:SMALL_ROWS, :]
        for d in range(1, N_DEV):
            tot = tot + all_ref[d * SMALL_ROWS:(d + 1) * SMALL_ROWS, :]
        delta, mn, vn = _adamw_math(w_ref[...], tot, m_ref[...], v_ref[...])
        gs_o[...] = tot
        d_o[...] = delta
        m_o[...] = mn
        v_o[...] = vn

    vm = pl.BlockSpec(memory_space=pltpu.VMEM)
    shp = jax.ShapeDtypeStruct((SMALL_ROWS, LANES), F32)
    res = pl.pallas_call(
        body, name="small_allreduce_adamw", in_specs=[vm] * 4, out_specs=[vm] * 5,
        out_shape=[jax.ShapeDtypeStruct((N_DEV * SMALL_ROWS, LANES), F32), shp, shp, shp, shp],
        scratch_shapes=[pltpu.SemaphoreType.DMA((7,)), pltpu.SemaphoreType.DMA((7,)), pltpu.SemaphoreType.DMA],
    )(g, w, m, v)
    return res[1:]


SMALL_NAMES = ("g_mix", "b_gate", "b_forget", "qn_swa", "kn_swa", "sink_swa", "rel_bias", "qn_fox", "kn_fox",
               "g_mem", "qn_mem", "kn_mem", "g_mlp")
BIG_NAMES = ("w_in", "w_mem_kv", "w_o_swa", "w_o_fox", "w_o_mem", "w_out", "w_mlp_up", "w_mlp_down")
WEIGHT_NAMES = ("g_mix", "w_in", "b_gate", "b_forget", "qn_swa", "kn_swa", "sink_swa", "rel_bias", "qn_fox", "kn_fox",
                "g_mem", "w_mem_kv", "qn_mem", "kn_mem", "w_o_swa", "w_o_fox", "w_o_mem", "w_out", "g_mlp",
                "w_mlp_up", "w_mlp_down")


def _pack_small(parts, extra=None):
    rows = []
    for n in SMALL_NAMES:
        flat = parts[n].reshape(-1).astype(F32)
        flat = jnp.pad(flat, (0, (-flat.size) % LANES))
        rows.append(flat.reshape(-1, LANES))
    if extra is not None:
        rows.append(jnp.pad(extra.reshape(1, 1), ((0, 0), (0, LANES - 1))))
    packed = jnp.concatenate(rows, axis=0)
    return jnp.pad(packed, ((0, SMALL_ROWS - packed.shape[0]), (0, 0)))


def _unpack_small(packed, shapes):
    out, r = {}, 0
    for n in SMALL_NAMES:
        size = math.prod(shapes[n])
        nr = -(-size // LANES)
        out[n] = packed[r:r + nr].reshape(-1)[:size].reshape(shapes[n])
        r += nr
    return out, packed[r, 0]


W_IN_SEGMENTS = ((C_QA, 0, 512), (C_QF, 768, 512), (C_KF, 1280, 512), (C_VF, 1792, 512), (C_QM, 2312, 512),
                 (C_KA, 512, 128), (C_VA, 640, 128), (C_FL, 2304, FOX_HEADS), (C_GL, 2824, GATE_W))
RELAYOUT_ROWS = 256


def _permute_pieces(src_of_dst):
    blocks = []
    for b in range(len(src_of_dst) // LANES):
        runs, lane = [], 0
        while lane < LANES:
            src = src_of_dst[b * LANES + lane]
            if src is None:
                lane += 1
                continue
            plane, col = src
            end = lane + 1
            while (end < LANES and src_of_dst[b * LANES + end] == (plane, col + end - lane)
                   and (col + end - lane) // LANES == col // LANES):
                end += 1
            runs.append((plane, col // LANES, (lane - col) % LANES, lane, end))
            lane = end
        blocks.append(runs)
    return blocks


def _permuted_block(runs, load, rows):
    lane = _lane((rows, LANES))
    acc = jnp.zeros((rows, LANES), F32)
    for plane, blk, shift, lo, hi in runs:
        x = load(plane, blk).astype(F32)
        if shift:
            x = pltpu.roll(x, shift, 1)
        acc = x if (lo, hi) == (0, LANES) else jnp.where((lane >= lo) & (lane < hi), x, acc)
    return acc


def _w_in_to_segments(g_in):
    src_of_dst = [None] * PROJ_W
    for mine, theirs, width in W_IN_SEGMENTS:
        for k in range(width):
            src_of_dst[mine + k] = ((theirs + k) // IN_SHARD, (theirs + k) % IN_SHARD)
    blocks = _permute_pieces(src_of_dst)
    rb = RELAYOUT_ROWS

    def body(src_ref, out_ref):
        for b, runs in enumerate(blocks):
            blk = _permuted_block(runs, lambda p, c: src_ref[p, :, c * LANES:(c + 1) * LANES], rb)
            out_ref[:, b * LANES:(b + 1) * LANES] = blk.astype(out_ref.dtype)

    return pl.pallas_call(
        body, name="w_in_to_segments", grid=(D_MODEL // rb,),
        in_specs=[pl.BlockSpec((N_SHARD, rb, IN_SHARD_PAD), lambda i: (0, i, 0))],
        out_specs=pl.BlockSpec((rb, PROJ_W), lambda i: (i, 0)),
        out_shape=jax.ShapeDtypeStruct((D_MODEL, PROJ_W), g_in.dtype),
        compiler_params=_cparams("parallel", vmem=VMEM_MID),
    )(g_in)


def _w_in_from_segments(lo, gl):
    mine_of_theirs = {}
    for mine, theirs, width in W_IN_SEGMENTS:
        for k in range(width):
            mine_of_theirs[theirs + k] = mine + k
    src_of_dst = [None] * (N_SHARD * IN_SHARD_PAD)
    for s in range(N_SHARD):
        for l in range(IN_SHARD):
            j = mine_of_theirs[s * IN_SHARD + l]
            src_of_dst[s * IN_SHARD_PAD + l] = (j // LO_W, j % LO_W)
    blocks = _permute_pieces(src_of_dst)
    per_slot = IN_SHARD_PAD // LANES
    rb = RELAYOUT_ROWS

    def body(lo_ref, gl_ref, out_ref):
        planes = (lo_ref, gl_ref)
        for b, runs in enumerate(blocks):
            blk = _permuted_block(runs, lambda p, c: planes[p][:, c * LANES:(c + 1) * LANES], rb)
            c0 = (b % per_slot) * LANES
            out_ref[b // per_slot, :, c0:c0 + LANES] = blk

    half = pl.BlockSpec((rb, LO_W), lambda i: (i, 0))
    return pl.pallas_call(
        body, name="w_in_from_segments", grid=(D_MODEL // rb,),
        in_specs=[half, half],
        out_specs=pl.BlockSpec((N_SHARD, rb, IN_SHARD_PAD), lambda i: (0, i, 0)),
        out_shape=jax.ShapeDtypeStruct((N_SHARD, D_MODEL, IN_SHARD_PAD), F32),
        compiler_params=_cparams("parallel", vmem=VMEM_MID),
    )(lo, gl)


def _after(first, then):
    return lax.optimization_barrier((first, then))


class _ReduceGroup:
    def __init__(self, tag, first_collective_id, place):
        self.tag, self.first_id, self.place = tag, first_collective_id, place

    def start(self, local, tie):
        self.names = tuple(local)
        mine, tie = _after([local[n] for n in self.names], tie)
        self.mine = mine
        self.staged = _pair_exchange("pair_exchange_" + self.tag, self.first_id, mine)
        return tie

    def send(self, tie):
        staged, tie = _after(self.staged, tie)
        sums = [_pair_sum("pair_sum_" + n, self.place, g, st) for n, g, st in zip(self.names, self.mine, staged)]
        travel, tie = _after([s[0] for s in sums], tie)
        self.own = [s[1] for s in sums]
        self.got = _chip_exchange("chip_exchange_" + self.tag, self.first_id + 1, travel)
        return tie

    def finish(self, tie):
        got, tie = _after(self.got, tie)
        halves = [_final_sum("final_sum_" + n, self.place, o, r) for n, o, r in zip(self.names, self.own, got)]
        halves, tie = _after(halves, tie)
        summed = _pair_gather("pair_gather_" + self.tag, self.first_id + 2, halves)
        self.summed = dict(zip(self.names, summed))
        return tie


class _GradReducer:
    def __init__(self, place):
        self.early = _ReduceGroup("early", 2, place)
        self.late = _ReduceGroup("late", 5, place)

    @staticmethod
    def _slot_rows(a):
        return a.reshape(N_SHARD, a.shape[0] // N_SHARD, a.shape[1])

    def early_start(self, g, tie):
        return self.early.start({"w_mlp_down": self._slot_rows(g["w_mlp_down"]), "w_mlp_up": g["w_mlp_up"],
                                 "w_out": self._slot_rows(g["w_out"]), "w_mem_kv": self._slot_rows(g["w_mem_kv"]),
                                 "w_o_swa": g["w_o_swa"], "w_o_fox": g["w_o_fox"], "w_o_mem": g["w_o_mem"]}, tie)

    def early_send(self, tie):
        return self.early.send(tie)

    def early_finish(self, tie):
        return self.early.finish(tie)

    def late_start(self, g, tie):
        d_in = _w_in_from_segments(g["wc_lo"], g["wc_gl"])
        return self.late.start({"w_in": d_in}, tie)

    def late_send(self, tie):
        return self.late.send(tie)

    def late_finish(self, tie):
        return self.late.finish(tie)

    @property
    def summed(self):
        return {**self.early.summed, **self.late.summed}


def kernel(x, mem, g_mix, w_in, b_gate, b_forget, qn_swa, kn_swa, sink_swa, rel_bias, qn_fox, kn_fox, g_mem, w_mem_kv, qn_mem, kn_mem, w_o_swa, w_o_fox, w_o_mem, w_out, g_mlp, w_mlp_up, w_mlp_down, loss_target, m_g_mix, m_w_in, m_b_gate, m_b_forget, m_qn_swa, m_kn_swa, m_sink_swa, m_rel_bias, m_qn_fox, m_kn_fox, m_g_mem, m_w_mem_kv, m_qn_mem, m_kn_mem, m_w_o_swa, m_w_o_fox, m_w_o_mem, m_w_out, m_g_mlp, m_w_mlp_up, m_w_mlp_down, v_g_mix, v_w_in, v_b_gate, v_b_forget, v_qn_swa, v_kn_swa, v_sink_swa, v_rel_bias, v_qn_fox, v_kn_fox, v_g_mem, v_w_mem_kv, v_qn_mem, v_kn_mem, v_w_o_swa, v_w_o_fox, v_w_o_mem, v_w_out, v_g_mlp, v_w_mlp_up, v_w_mlp_down):
    given = dict(locals())
    W = {n: given[n] for n in WEIGHT_NAMES}
    M = {n: given["m_" + n] for n in WEIGHT_NAMES}
    V = {n: given["v_" + n] for n in WEIGHT_NAMES}
    pad_in = ((0, 0), (0, IN_SHARD_PAD - IN_SHARD))

    shards = [jnp.pad(w_in[0].astype(BF16), pad_in)] + [W[n][0].astype(BF16) for n in BIG_NAMES[1:]]
    slots = [jnp.broadcast_to(s[None], (N_SHARD,) + s.shape) for s in shards]
    g_in = _all_gather_neighbours("all_gather_w_in", 1, slots[0])
    small = {n: (W[n] if n == "rel_bias" else W[n].reshape(1, -1)) for n in SMALL_NAMES}
    h = _rmsnorm("rms_mix", x[0], small["g_mix"], min(1024, x.shape[1]))
    g_in, late, h, (m_in, v_in) = lax.optimization_barrier((g_in, slots[1:], h, (M["w_in"][0], V["w_in"][0])))
    M["w_in"], V["w_in"] = m_in[None], v_in[None]
    g_kv, g_oa, g_of, g_om, g_out, g_up, g_down = _all_gather_shards_async("all_gather_weights_async", 8, late)

    place = jnp.stack([2 * lax.axis_index("x") + lax.axis_index("y"), lax.axis_index("c")]).astype(jnp.int32)
    reducer = _GradReducer(place)
    loss, grad_x, grads = _local_step(
        x[0], h, mem[0], loss_target[0], small, g_in, g_kv.reshape(D_MODEL, D_MODEL), (g_oa, g_of, g_om),
        g_out.reshape(D_MODEL, D_MODEL), g_up, g_down.reshape(D_FF, D_MODEL), reducer)

    out = {}

    def adamw_of(names, summed):
        for n in names:
            res = _adamw("adamw_" + n, W[n][0], summed[n], M[n][0], V[n][0])
            out[n] = [r.reshape(W[n].shape) for r in res]

    adamw_of(reducer.early.names, reducer.early.summed)
    shapes = {n: W[n].shape for n in SMALL_NAMES}
    packed = _small_allreduce_adamw(_pack_small(grads, loss), _pack_small(W), _pack_small(M), _pack_small(V))
    done_meanwhile = ([out[n] for n in reducer.early.names], packed)
    (early_out, packed), grad_x = reducer.late_finish((done_meanwhile, grad_x))
    for n, res in zip(reducer.early.names, early_out):
        out[n] = res
    adamw_of(reducer.late.names, reducer.late.summed)
    unpacked = [_unpack_small(p, shapes) for p in packed]
    for n in SMALL_NAMES:
        out[n] = [u[0][n] for u in unpacked]
    loss_total = unpacked[0][1]

    return (loss_total, grad_x.reshape(x.shape),
            *[out[n][0] for n in WEIGHT_NAMES], *[out[n][1] for n in WEIGHT_NAMES],
            *[out[n][2] for n in WEIGHT_NAMES], *[out[n][3] for n in WEIGHT_NAMES])
```

```python
import math

import jax
import jax.numpy as jnp
from jax import lax
from jax.experimental import pallas as pl
from jax.experimental.pallas import tpu as pltpu
from jax.experimental.pallas import tpu_sc as plsc

F32 = jnp.float32
BF16 = jnp.bfloat16

D_MODEL = 1024
N_MEM = 256
SWA_HEADS = 8
SWA_KV_HEADS = 2
SWA_HEAD_DIM = 64
WINDOW = 128
FOX_HEADS = 8
FOX_HEAD_DIM = 64
MEM_HEADS = 4
MEM_HEAD_DIM = 128
D_FF = 4 * D_MODEL
REL_BUCKETS = 32
REL_MAX_DIST = 128
EPS = 1e-6
NEG = -1e30
GATE_W = 3 * D_MODEL
IN_WIDTH = 5896
N_SHARD = 4
IN_SHARD = IN_WIDTH // N_SHARD
IN_SHARD_PAD = 1536

ADAM_LR = 0.001
ADAM_B1 = 0.9
ADAM_B2 = 0.999
ADAM_EPS = 1e-08
ADAM_WD = 0.01
ADAM_STEP = 10

LANES = 128
V7X_VMEM_BYTES = 64 * 1024 * 1024
VMEM_SMALL = VMEM_MID = VMEM_BIG = V7X_VMEM_BYTES * 3 // 4
VMEM_MAX = V7X_VMEM_BYTES * 7 // 8

C_QA, C_QF, C_KF, C_VF, C_QM, C_KA, C_VA, C_FL, C_GL = 0, 512, 1024, 1536, 2048, 2560, 2688, 2816, 3072
LO_W = 3072
PROJ_W = 6144
PROJ_TN = 2048

NN = (((1,), (0,)), ((), ()))
NT = (((1,), (1,)), ((), ()))
TN = (((0,), (0,)), ((), ()))


def _dot(a, b, dims=NN):
    return lax.dot_general(a, b, dims, preferred_element_type=F32)


def _cparams(*sem, vmem=VMEM_SMALL):
    return pltpu.CompilerParams(dimension_semantics=sem, vmem_limit_bytes=vmem)


def _split3(a):
    hi = a.astype(BF16)
    r1 = a - hi.astype(F32)
    mid = r1.astype(BF16)
    lo = (r1 - mid.astype(F32)).astype(BF16)
    return hi, mid, lo


def _group_mean(a, g2):
    hi = a.astype(BF16)
    mid = (a - hi.astype(F32)).astype(BF16)
    return _dot(jnp.concatenate([hi, mid], axis=1), g2)


def _dot3_left(g, a):
    hi, mid, lo = _split3(a)
    return _dot(g, hi) + _dot(g, mid) + _dot(g, lo)


def _group_mean_matrix(d):
    r = jnp.arange(LANES)
    g = jnp.where((r[:, None] // d) == (r[None, :] // d), 1.0 / d, 0.0).astype(BF16)
    return jnp.concatenate([g, g], axis=0)


def _lane(shape):
    return lax.broadcasted_iota(jnp.int32, shape, len(shape) - 1)


def _matmul(name, a, b, *, dims, grid, a_spec, b_spec, acc_shape, outs, epilogue, extra=(), vmem=VMEM_BIG):
    nk = grid[2]
    n_extra = len(extra)

    def body(a_ref, b_ref, *rest):
        extra_refs = rest[:n_extra]
        out_refs = rest[n_extra:n_extra + len(outs)]
        i, j, k = pl.program_id(0), pl.program_id(1), pl.program_id(2)
        if nk == 1:
            epilogue(_dot(a_ref[...].astype(BF16), b_ref[...].astype(BF16), dims), extra_refs, out_refs, (i, j))
            return
        acc_ref = rest[-1]

        @pl.when((i == 0) & (j == 0) & (k == 0))
        def _():
            acc_ref[...] = jnp.zeros_like(acc_ref)

        acc_ref[...] += _dot(a_ref[...].astype(BF16), b_ref[...].astype(BF16), dims)

        @pl.when(k == nk - 1)
        def _():
            epilogue(acc_ref[...], extra_refs, out_refs, (i, j))
            acc_ref[...] = jnp.zeros_like(acc_ref)

    res = pl.pallas_call(
        body,
        name=name,
        grid=grid,
        in_specs=[a_spec, b_spec] + [s for _, s in extra],
        out_specs=[s for _, s in outs],
        out_shape=[s for s, _ in outs],
        scratch_shapes=[pltpu.VMEM(acc_shape, F32)] if nk > 1 else [],
        compiler_params=_cparams("arbitrary", "arbitrary", "arbitrary", vmem=vmem),
    )(a, b, *[x for x, _ in extra])
    return res


def _epi_store(acc, extra_refs, out_refs, ij):
    out_refs[0][...] = acc.astype(out_refs[0].dtype)


def _rms_rows(x, g):
    r = lax.rsqrt(jnp.mean(x * x, axis=-1, keepdims=True) + EPS)
    return x * r, r


def _rmsnorm_bwd_rows(dh, x, g):
    xhat, r = _rms_rows(x, g)
    dxh = dh * g
    dx = r * (dxh - xhat * jnp.mean(dxh * xhat, axis=-1, keepdims=True))
    return dx, jnp.sum(dh * xhat, axis=0, keepdims=True)


def _rmsnorm(name, x, g, tb):
    T, Dm = x.shape

    def body(x_ref, g_ref, o_ref):
        xhat, _ = _rms_rows(x_ref[...], None)
        o_ref[...] = (xhat * g_ref[...]).astype(o_ref.dtype)

    return pl.pallas_call(
        body, name=name, grid=(T // tb,),
        in_specs=[pl.BlockSpec((tb, Dm), lambda i: (i, 0)), pl.BlockSpec((1, Dm), lambda i: (0, 0))],
        out_specs=pl.BlockSpec((tb, Dm), lambda i: (i, 0)),
        out_shape=jax.ShapeDtypeStruct((T, Dm), BF16),
        compiler_params=_cparams("parallel"),
    )(x, g)


def _head_norm(x, gm, gain):
    ms = _group_mean(x * x, gm)
    r = lax.rsqrt(ms + EPS)
    return x * r * gain, x * r


def _head_norm_bwd(dy, x, gm, gain):
    ms = _group_mean(x * x, gm)
    r = lax.rsqrt(ms + EPS)
    xhat = x * r
    dxh = dy * gain
    dx = r * (dxh - xhat * _group_mean(dxh * xhat, gm))
    return dx, jnp.sum(dy * xhat, axis=0, keepdims=True)


def _log_sigmoid(z):
    return jnp.minimum(z, 0.0) - jnp.log(1.0 + jnp.exp(-jnp.abs(z)))


def _prep_fwd(proj, gains, bfor, tril, gm64, gm128, T, tb):
    nb = T // tb

    def body(qa_ref, qf_ref, kf_ref, vf_ref, qm_ref, ka_ref, va_ref, fl_ref, gains_ref, bfor_ref, tril_ref,
             gm64_ref, gm128_ref,
             qa_o, qf_o, kf_o, vf_o, qm_o, kad_o, vad_o, qaug_o, kaug_o, carry):
        i = pl.program_id(0)
        gm64v = gm64_ref[...]
        gm128v = gm128_ref[...]
        lane = _lane((tb, LANES))

        def norm512(src, dst, row, gm, scale=1.0):
            gain = gains_ref[row:row + 1, :]
            for c in range(4):
                sl = slice(c * LANES, (c + 1) * LANES)
                y, _ = _head_norm(src[:, sl], gm, gain)
                dst[:, sl] = (y * scale).astype(dst.dtype)

        norm512(qa_ref, qa_o, 0, gm64v)
        norm512(qf_ref, qf_o, 2, gm64v, FOX_SCALE)
        norm512(kf_ref, kf_o, 3, gm64v)
        norm512(qm_ref, qm_o, 4, gm128v)
        vf_o[...] = vf_ref[...].astype(vf_o.dtype)

        ka_n, _ = _head_norm(ka_ref[...], gm64v, gains_ref[1:2, :])
        ka_r = pltpu.roll(ka_n, 64, 1)
        va = va_ref[...]
        va_r = pltpu.roll(va, 64, 1)
        lo = lane < 64
        kad_o[0] = jnp.where(lo, ka_n, ka_r).astype(kad_o.dtype)
        kad_o[1] = jnp.where(lo, ka_r, ka_n).astype(kad_o.dtype)
        vad_o[0] = jnp.where(lo, va, va_r).astype(vad_o.dtype)
        vad_o[1] = jnp.where(lo, va_r, va).astype(vad_o.dtype)

        @pl.when(i == 0)
        def _():
            carry[...] = jnp.zeros_like(carry)

        logf = jnp.where(lane < FOX_HEADS, _log_sigmoid(fl_ref[...] + bfor_ref[...]), 0.0)
        c = _dot3_left(tril_ref[...], logf) + carry[0:1, :]
        carry[...] = jnp.broadcast_to(c[tb - 1:tb, :], carry.shape)
        for pair in range(FOX_HEADS // 2):
            qaug = jnp.zeros((tb, LANES), F32)
            kaug = jnp.zeros((tb, LANES), F32)
            for sub in range(2):
                col = jnp.sum(jnp.where(lane == 2 * pair + sub, c, 0.0), axis=1, keepdims=True)
                pieces = [p.astype(F32) for p in _split3(col)]
                base = AUG_STRIDE * sub
                for e in range(3):
                    qaug = jnp.where(lane == base + AUG_C + e, pieces[e], qaug)
                    kaug = jnp.where(lane == base + AUG_NEG_C + e, -pieces[e], kaug)
                qaug = jnp.where((lane >= base + AUG_NEG_C) & (lane < base + AUG_NEG_C + 3), 1.0, qaug)
                ones_k = ((lane >= base + AUG_C) & (lane < base + AUG_C + 3)) | (
                    (lane >= base + AUG_STAT) & (lane < base + AUG_STAT + 3))
                kaug = jnp.where(ones_k, 1.0, kaug)
            sl = slice(pair * LANES, (pair + 1) * LANES)
            qaug_o[:, sl] = qaug.astype(BF16)
            kaug_o[:, sl] = kaug.astype(BF16)

    def seg(width, start):
        return pl.BlockSpec((tb, width), lambda i, s=start // width: (i, s))

    const = lambda shape: pl.BlockSpec(shape, lambda i: tuple(0 for _ in shape))
    rows512 = pl.BlockSpec((tb, 512), lambda i: (i, 0))
    outs = pl.pallas_call(
        body, name="prep_fwd", grid=(nb,),
        in_specs=[seg(512, C_QA), seg(512, C_QF), seg(512, C_KF), seg(512, C_VF), seg(512, C_QM),
                  seg(128, C_KA), seg(128, C_VA), seg(128, C_FL),
                  const((8, LANES)), const((1, LANES)), const((tb, tb)), const((2 * LANES, LANES)), const((2 * LANES, LANES))],
        out_specs=[rows512, rows512, rows512, rows512, rows512,
                   pl.BlockSpec((2, tb, LANES), lambda i: (0, i, 0)), pl.BlockSpec((2, tb, LANES), lambda i: (0, i, 0)),
                   rows512, rows512],
        out_shape=[jax.ShapeDtypeStruct((T, 512), BF16)] * 5
        + [jax.ShapeDtypeStruct((2, T, LANES), BF16)] * 2
        + [jax.ShapeDtypeStruct((T, 512), BF16)] * 2,
        scratch_shapes=[pltpu.VMEM((8, LANES), F32)],
        compiler_params=_cparams("arbitrary", vmem=VMEM_MID),
    )(proj, proj, proj, proj, proj, proj, proj, proj, gains, bfor, tril, gm64, gm128)
    return outs


def _prep_bwd(proj, dqa, dkad, dvad, dqf, dkf, dvf, dqm, dqf_aug, dkf_aug, gains, bfor, triu, gm64, gm128, T, tb):
    nb = T // tb

    def body(qa_ref, qf_ref, kf_ref, qm_ref, ka_ref, fl_ref,
             dqa_ref, dkad_ref, dvad_ref, dqf_ref, dkf_ref, dvf_ref, dqm_ref, dqfa_ref, dkfa_ref,
             gains_ref, bfor_ref, triu_ref, gm64_ref, gm128_ref,
             dlo_o, gacc_o, carry):
        i = pl.program_id(0)
        gm64v = gm64_ref[...]
        gm128v = gm128_ref[...]
        lane = _lane((tb, LANES))

        @pl.when(i == 0)
        def _():
            carry[...] = jnp.zeros_like(carry)
            gacc_o[...] = jnp.zeros_like(gacc_o)

        def norm512_bwd(dsrc, xsrc, col0, row, gm):
            gain = gains_ref[row:row + 1, :]
            gsum = jnp.zeros((1, LANES), F32)
            for c in range(4):
                sl = slice(c * LANES, (c + 1) * LANES)
                dx, dg = _head_norm_bwd(dsrc[:, sl], xsrc[:, sl], gm, gain)
                dlo_o[:, col0 + c * LANES:col0 + (c + 1) * LANES] = dx.astype(dlo_o.dtype)
                gsum = gsum + dg
            gacc_o[row:row + 1, :] += gsum

        norm512_bwd(dqa_ref, qa_ref, C_QA, 0, gm64v)
        norm512_bwd(dqf_ref, qf_ref, C_QF, 2, gm64v)
        norm512_bwd(dkf_ref, kf_ref, C_KF, 3, gm64v)
        norm512_bwd(dqm_ref, qm_ref, C_QM, 4, gm128v)
        dlo_o[:, C_VF:C_VF + 512] = dvf_ref[...].astype(dlo_o.dtype)

        lo = lane < 64

        def fold(ref):
            f0 = ref[0] + pltpu.roll(ref[0], 64, 1)
            f1 = ref[1] + pltpu.roll(ref[1], 64, 1)
            return jnp.where(lo, f0, f1)

        dka, dg = _head_norm_bwd(fold(dkad_ref), ka_ref[...], gm64v, gains_ref[1:2, :])
        gacc_o[1:2, :] += dg
        dlo_o[:, C_KA:C_KA + LANES] = dka.astype(dlo_o.dtype)
        dlo_o[:, C_VA:C_VA + LANES] = fold(dvad_ref).astype(dlo_o.dtype)

        dc = jnp.zeros((tb, LANES), F32)
        for pair in range(FOX_HEADS // 2):
            sl = slice(pair * LANES, (pair + 1) * LANES)
            rows_sum, cols_sum = dqfa_ref[:, sl], dkfa_ref[:, sl]
            for sub in range(2):
                diff = (jnp.where(lane == AUG_STRIDE * sub + AUG_C, rows_sum, 0.0)
                        - jnp.where(lane == AUG_STRIDE * sub + AUG_NEG_C, cols_sum, 0.0))
                dc = jnp.where(lane == 2 * pair + sub, jnp.sum(diff, axis=1, keepdims=True), dc)
        dlogf = _dot3_left(triu_ref[...], dc) + carry[0:1, :]
        carry[...] = jnp.broadcast_to(dlogf[0:1, :], carry.shape)
        z = fl_ref[...] + bfor_ref[...]
        dfl = jnp.where(lane < FOX_HEADS, dlogf / (1.0 + jnp.exp(z)), 0.0)
        gacc_o[5:6, :] += jnp.sum(dfl, axis=0, keepdims=True)
        dlo_o[:, C_FL:C_FL + LANES] = dfl.astype(dlo_o.dtype)
        dlo_o[:, C_FL + LANES:C_FL + 2 * LANES] = jnp.zeros((tb, LANES), dlo_o.dtype)

    rev = lambda i: nb - 1 - i

    def seg(width, start):
        return pl.BlockSpec((tb, width), lambda i, s=start // width: (rev(i), s))

    const = lambda shape: pl.BlockSpec(shape, lambda i: tuple(0 for _ in shape))
    rows512 = pl.BlockSpec((tb, 512), lambda i: (rev(i), 0))
    dup = pl.BlockSpec((2, tb, LANES), lambda i: (0, rev(i), 0))
    return pl.pallas_call(
        body, name="prep_bwd", grid=(nb,),
        in_specs=[seg(512, C_QA), seg(512, C_QF), seg(512, C_KF), seg(512, C_QM), seg(128, C_KA), seg(128, C_FL),
                  rows512, dup, dup, rows512, rows512, rows512, rows512, rows512, rows512,
                  const((8, LANES)), const((1, LANES)), const((tb, tb)), const((2 * LANES, LANES)), const((2 * LANES, LANES))],
        out_specs=[pl.BlockSpec((tb, LO_W), lambda i: (rev(i), 0)), const((8, LANES))],
        out_shape=[jax.ShapeDtypeStruct((T, LO_W), BF16), jax.ShapeDtypeStruct((8, LANES), F32)],
        scratch_shapes=[pltpu.VMEM((8, LANES), F32)],
        compiler_params=_cparams("arbitrary", vmem=VMEM_MID),
    )(proj, proj, proj, proj, proj, proj, dqa, dkad, dvad, dqf, dkf, dvf, dqm, dqf_aug, dkf_aug,
      gains, bfor, triu, gm64, gm128)


FOX_SCALE = FOX_HEAD_DIM ** -0.5
AUG_STRIDE = 16
AUG_C = 0
AUG_NEG_C = 3
AUG_STAT = 6
FOX_TQ, FOX_TK = 1024, 1024
FOX_BWD_TQ, FOX_BWD_TK = 1024, 1024
FOX_DIAGONAL_PARTS = 4


def _fox_head_mask(sub, rows):
    lane = _lane((rows, 2 * LANES))
    main = (lane >= 64 * sub) & (lane < 64 * sub + 64)
    aug = (lane >= LANES + AUG_STRIDE * sub) & (lane < LANES + AUG_STRIDE * (sub + 1))
    return main | aug


def _fox_pieces(diagonal, tq, tk):
    if diagonal and tq == tk and tq >= FOX_DIAGONAL_PARTS * LANES:
        step = tq // FOX_DIAGONAL_PARTS
        return [(n * step, (n + 1) * step, (n + 1) * step) for n in range(FOX_DIAGONAL_PARTS)]
    return [(0, tq, tk)]


def _fox_fwd(q, qaug, k, kaug, v, T, tq, tk):
    nq, nk = T // tq, T // tk
    rep = tk // LANES
    last_of = lambda i: (i * tq + tq - 1) // tk

    def body(q_ref, qa_ref, k_ref, ka_ref, v_ref, o_ref, qab_ref, m_s, acc_s):
        p_, i, j = pl.program_id(0), pl.program_id(1), pl.program_id(2)
        last = last_of(i)

        @pl.when(j == 0)
        def _():
            m_s[...] = jnp.full(m_s.shape, NEG, F32)
            acc_s[...] = jnp.zeros_like(acc_s)

        def step(diagonal):
            k2 = jnp.concatenate([k_ref[...], ka_ref[...]], axis=1)
            v2 = jnp.concatenate([v_ref[...], ka_ref[...]], axis=1)
            pieces = _fox_pieces(diagonal, tq, tk)
            work = []
            for r0, r1, nc in pieces:
                rows = slice(r0, r1)
                q2 = jnp.concatenate([q_ref[rows, :], qa_ref[rows, :]], axis=1)
                for sub in range(2):
                    qh = jnp.where(_fox_head_mask(sub, r1 - r0), q2, jnp.zeros_like(q2))
                    work.append((rows, r0, r1 - r0, nc, sub, _dot(qh, k2[:nc], NT)))
            for rows, r0, nr, nc, sub, s in work:
                if diagonal:
                    causal = (lax.broadcasted_iota(jnp.int32, (nr, nc), 1) + j * tk
                              <= lax.broadcasted_iota(jnp.int32, (nr, nc), 0) + (r0 + i * tq))
                    s = jnp.where(causal, s, NEG)
                m_prev = m_s[sub, rows, :]
                m_next = jnp.maximum(m_prev, jnp.max(s, axis=1, keepdims=True))
                p = jnp.exp(s - jnp.tile(m_next, (1, nc // LANES)))
                alpha = jnp.exp(m_prev - m_next)
                m_s[sub, rows, :] = m_next
                acc_s[sub, rows, :] = acc_s[sub, rows, :] * jnp.tile(alpha, (1, 2)) + _dot(p.astype(BF16), v2[:nc])

        @pl.when(j == last)
        def _():
            step(True)

        @pl.when(j < last)
        def _():
            step(False)

        @pl.when(j == nk - 1)
        def _():
            lane = _lane((tq, LANES))
            outs = []
            qab = qa_ref[...].astype(F32)
            for sub in range(2):
                acc = acc_s[sub]
                base = AUG_STRIDE * sub
                l = jnp.sum(jnp.where(lane == base + AUG_C, acc[:, LANES:], 0.0), axis=1, keepdims=True)
                outs.append(acc[:, :LANES] / l)
                lse = jnp.max(m_s[sub], axis=1, keepdims=True) + jnp.log(l)
                pieces = _split3(-lse)
                for e in range(3):
                    qab = jnp.where(lane == base + AUG_STAT + e, pieces[e].astype(F32), qab)
            o_ref[...] = jnp.where(lane < 64, outs[0], outs[1]).astype(o_ref.dtype)
            qab_ref[...] = qab.astype(BF16)

    qspec = pl.BlockSpec((tq, LANES), lambda p, i, j: (i, p))
    kspec = pl.BlockSpec((tk, LANES), lambda p, i, j: (jnp.minimum(j, last_of(i)), p))
    return pl.pallas_call(
        body, name="fox_fwd", grid=(4, nq, nk),
        in_specs=[qspec, qspec, kspec, kspec, kspec],
        out_specs=[qspec, qspec],
        out_shape=[jax.ShapeDtypeStruct((T, 512), BF16), jax.ShapeDtypeStruct((T, 512), BF16)],
        scratch_shapes=[pltpu.VMEM((2, tq, LANES), F32), pltpu.VMEM((2, tq, 2 * LANES), F32)],
        compiler_params=_cparams("parallel", "parallel", "arbitrary", vmem=VMEM_BIG),
    )(q, qaug, k, kaug, v)


def _fox_bwd(q, qaug, k, kaug, v, do, doaug, T, tq, tk):
    nq, nk = T // tq, T // tk
    first_of = lambda j: (j * tk) // tq

    def body(q_ref, qa_ref, k_ref, ka_ref, v_ref, do_ref, doa_ref,
             dq_ref, dqa_ref, dk_ref, dka_ref, dv_ref, dk_s, dv_s):
        p_, j, i = pl.program_id(0), pl.program_id(1), pl.program_id(2)
        masked = i * tq < (j + 1) * tk - 1

        @pl.when((j == 0) & (i == 0))
        def _():
            dq_ref[...] = jnp.zeros_like(dq_ref)
            dqa_ref[...] = jnp.zeros_like(dqa_ref)

        @pl.when(i == 0)
        def _():
            dk_s[...] = jnp.zeros_like(dk_s)
            dv_s[...] = jnp.zeros_like(dv_s)

        def step(diagonal):
            k2 = jnp.concatenate([k_ref[...], ka_ref[...]], axis=1)
            v2 = jnp.concatenate([v_ref[...], ka_ref[...]], axis=1)
            work = []
            for r0, r1, nc in _fox_pieces(diagonal, tq, tk):
                rows = slice(r0, r1)
                q2 = jnp.concatenate([q_ref[rows, :], qa_ref[rows, :]], axis=1)
                do2 = jnp.concatenate([do_ref[rows, :], doa_ref[rows, :]], axis=1)
                for sub in range(2):
                    hm = _fox_head_mask(sub, r1 - r0)
                    qh = jnp.where(hm, q2, jnp.zeros_like(q2))
                    doh = jnp.where(hm, do2, jnp.zeros_like(do2))
                    s = _dot(qh, k2[:nc], NT)
                    dp = _dot(doh, v2[:nc], NT)
                    work.append((r0, r1 - r0, nc, sub, qh, doh, s, dp))
            dqs = {}
            for r0, nr, nc, sub, qh, doh, s, dp in work:
                if diagonal:
                    causal = (lax.broadcasted_iota(jnp.int32, (nr, nc), 1) + j * tk
                              <= lax.broadcasted_iota(jnp.int32, (nr, nc), 0) + (r0 + i * tq))
                    s = jnp.where(causal, s, NEG)
                p = jnp.exp(s)
                dsb = (p * dp).astype(BF16)
                dv_s[0:nc, :] += _dot(p.astype(BF16), doh[:, :LANES], TN)
                dk_s[0:nc, :] += _dot(dsb, qh, TN)
                dqs[(r0, sub)] = _dot(dsb, k2[:nc])
            for r0, r1, nc in _fox_pieces(diagonal, tq, tk):
                dq2 = jnp.where(_fox_head_mask(0, r1 - r0), dqs[(r0, 0)], dqs[(r0, 1)])
                qrows = pl.ds(pl.multiple_of(i * tq + r0, r1 - r0), r1 - r0)
                dq_ref[qrows, :] += dq2[:, :LANES] * FOX_SCALE
                dqa_ref[qrows, :] += dq2[:, LANES:]

        @pl.when((i >= first_of(j)) & masked)
        def _():
            step(True)

        @pl.when((i >= first_of(j)) & jnp.logical_not(masked))
        def _():
            step(False)

        @pl.when(i == nq - 1)
        def _():
            dk_ref[...] = dk_s[:, :LANES]
            dka_ref[...] = dk_s[:, LANES:]
            dv_ref[...] = dv_s[...]

    qspec = pl.BlockSpec((tq, LANES), lambda p, j, i: (jnp.maximum(i, first_of(j)), p))
    kspec = pl.BlockSpec((tk, LANES), lambda p, j, i: (j, p))
    resident = pl.BlockSpec((T, LANES), lambda p, j, i: (0, p))
    return pl.pallas_call(
        body, name="fox_bwd", grid=(4, nk, nq),
        in_specs=[qspec, qspec, kspec, kspec, kspec, qspec, qspec],
        out_specs=[resident, resident, kspec, kspec, kspec],
        out_shape=[jax.ShapeDtypeStruct((T, 512), F32)] * 5,
        scratch_shapes=[pltpu.VMEM((tk, 2 * LANES), F32), pltpu.VMEM((tk, LANES), F32)],
        compiler_params=_cparams("arbitrary", "arbitrary", "arbitrary", vmem=VMEM_BIG),
    )(q, qaug, k, kaug, v, do, doaug)


SWA_SUB = 16
SWA_TB = SWA_SUB * WINDOW


def _t5_bucket_matrix():
    t = jnp.arange(WINDOW)[:, None] + WINDOW
    s = jnp.arange(2 * WINDOW)[None, :]
    max_exact = REL_BUCKETS // 2
    d = jnp.maximum(t - s, 0)
    df = jnp.maximum(d, 1).astype(F32)
    large = max_exact + (jnp.log(df / max_exact) / math.log(REL_MAX_DIST / max_exact)
                         * (REL_BUCKETS - max_exact)).astype(jnp.int32)
    large = jnp.minimum(large, REL_BUCKETS - 1)
    return jnp.where(d < max_exact, d, large).astype(jnp.int32)


def _swa_bias(rel_bias, bucket):
    def body(rel_ref, bucket_ref, o_ref):
        b = bucket_ref[...]
        for h in range(SWA_HEADS):
            acc = jnp.zeros(b.shape, F32)
            for r in range(REL_BUCKETS):
                acc = jnp.where(b == r, rel_ref[r, h], acc)
            o_ref[h] = acc

    return pl.pallas_call(
        body, name="swa_bias",
        in_specs=[pl.BlockSpec(memory_space=pltpu.SMEM), pl.BlockSpec(memory_space=pltpu.VMEM)],
        out_specs=pl.BlockSpec(memory_space=pltpu.VMEM),
        out_shape=jax.ShapeDtypeStruct((SWA_HEADS, WINDOW, 2 * WINDOW), F32),
    )(rel_bias, bucket)


def _swa_bias_bwd(dbias, bucket):
    def body(db_ref, bucket_ref, o_ref):
        b = bucket_ref[...]
        lane = _lane((1, LANES))
        for r in range(REL_BUCKETS):
            row = jnp.zeros((1, LANES), F32)
            for h in range(SWA_HEADS):
                part = jnp.sum(jnp.where(b == r, db_ref[h], 0.0), axis=0, keepdims=True)
                tot = jnp.sum(part, axis=1, keepdims=True)
                row = jnp.where(lane == h, tot, row)
            o_ref[r:r + 1, :] = row

    return pl.pallas_call(
        body, name="swa_bias_bwd",
        in_specs=[pl.BlockSpec(memory_space=pltpu.VMEM), pl.BlockSpec(memory_space=pltpu.VMEM)],
        out_specs=pl.BlockSpec(memory_space=pltpu.VMEM),
        out_shape=jax.ShapeDtypeStruct((REL_BUCKETS, LANES), F32),
    )(dbias, bucket)


SWA_GROUP = SWA_HEADS // SWA_KV_HEADS


def _swa_valid(r, i):
    t = (lax.broadcasted_iota(jnp.int32, (SWA_GROUP * WINDOW, 2 * WINDOW), 0) & (WINDOW - 1)) + WINDOW
    s = lax.broadcasted_iota(jnp.int32, (SWA_GROUP * WINDOW, 2 * WINDOW), 1)
    dist = t - s
    band = (dist >= 0) & (dist < WINDOW)
    if r == 0:
        band = band & ((s >= WINDOW) | (i > 0))
    return band


def _swa_stack(blk):
    lane = _lane((WINDOW, LANES))
    parts = []
    for g in range(SWA_GROUP):
        b = blk[:, LANES * (g // 2):LANES * (g // 2 + 1)]
        parts.append(jnp.where((lane >= 64) if g % 2 else (lane < 64), b, jnp.zeros_like(b)))
    return jnp.concatenate(parts, axis=0)


def _swa_unstack(st):
    lane = _lane((WINDOW, LANES))
    W = WINDOW
    return jnp.concatenate([jnp.where(lane < 64, st[2 * b * W:(2 * b + 1) * W], st[(2 * b + 1) * W:(2 * b + 2) * W])
                            for b in range(2)], axis=1)


def _swa_sink_column(sink_ref, kvh):
    row = lax.broadcasted_iota(jnp.int32, (SWA_GROUP * WINDOW, 1), 0)
    col = jnp.full((SWA_GROUP * WINDOW, 1), sink_ref[SWA_GROUP * kvh + SWA_GROUP - 1], F32)
    for g in range(SWA_GROUP - 2, -1, -1):
        col = jnp.where(row < (g + 1) * WINDOW, sink_ref[SWA_GROUP * kvh + g], col)
    return col


def _swa_specs(T):
    W = WINDOW
    qspec = pl.BlockSpec((SWA_TB, 2 * LANES), lambda h, i: (i, h))
    own = pl.BlockSpec((None, SWA_TB, LANES), lambda h, i: (h, i, 0))
    prev = pl.BlockSpec((None, W, LANES), lambda h, i: (h, jnp.maximum(SWA_SUB * i - 1, 0), 0))
    stat = pl.BlockSpec((SWA_GROUP, SWA_TB, LANES), lambda h, i: (h, i, 0))
    bias = pl.BlockSpec((None, SWA_GROUP * W, 2 * W), lambda h, i: (h, 0, 0))
    return qspec, own, prev, stat, bias


def _swa_fwd(sinks, q, kad, vad, bias, T):
    nb = T // SWA_TB
    scale = SWA_HEAD_DIM ** -0.5
    W = WINDOW

    def body(sink_ref, q_ref, k_ref, kp_ref, v_ref, vp_ref, bias_ref, o_ref, lse_ref):
        kvh, i = pl.program_id(0), pl.program_id(1)
        sink = _swa_sink_column(sink_ref, kvh)
        for r in range(SWA_SUB):
            rs = slice(r * W, (r + 1) * W)
            ps = slice((r - 1) * W, r * W)
            k_own, v_own = k_ref[rs, :], v_ref[rs, :]
            k_prev = kp_ref[...] if r == 0 else k_ref[ps, :]
            v_prev = vp_ref[...] if r == 0 else v_ref[ps, :]
            qs = _swa_stack(q_ref[rs, :])
            s = jnp.concatenate([_dot(qs, k_prev, NT), _dot(qs, k_own, NT)], axis=1) * scale + bias_ref[...]
            s = jnp.where(_swa_valid(r, i), s, NEG)
            m = jnp.maximum(jnp.max(s, axis=1, keepdims=True), sink)
            p = jnp.exp(s - m)
            denom = jnp.sum(p, axis=1, keepdims=True) + jnp.exp(sink - m)
            pn = (p / denom).astype(BF16)
            o_ref[rs, :] = _swa_unstack(_dot(pn[:, :W], v_prev) + _dot(pn[:, W:], v_own)).astype(o_ref.dtype)
            lse = m + jnp.log(denom)
            for g in range(SWA_GROUP):
                lse_ref[g, rs, :] = jnp.broadcast_to(lse[g * W:(g + 1) * W], (W, LANES))

    qspec, own, prev, stat, bspec = _swa_specs(T)
    return pl.pallas_call(
        body, name="swa_fwd", grid=(SWA_KV_HEADS, nb),
        in_specs=[pl.BlockSpec(memory_space=pltpu.SMEM), qspec, own, prev, own, prev, bspec],
        out_specs=[qspec, stat],
        out_shape=[jax.ShapeDtypeStruct((T, 512), BF16), jax.ShapeDtypeStruct((SWA_HEADS, T, LANES), F32)],
        compiler_params=_cparams("parallel", "parallel", vmem=VMEM_MID),
    )(sinks, q, kad, kad, vad, vad, bias.reshape(SWA_KV_HEADS, SWA_GROUP * W, 2 * W))


def _swa_bwd(sinks, q, kad, vad, bias, do, lse, delta, T):
    nb = T // SWA_TB
    scale = SWA_HEAD_DIM ** -0.5
    W = WINDOW

    def body(sink_ref, q_ref, k_ref, kp_ref, v_ref, vp_ref, bias_ref, do_ref, lse_ref, dl_ref,
             dq_ref, dkad_ref, dvad_ref, dbias_ref, dsk_ref):
        kvh, i = pl.program_id(0), pl.program_id(1)
        sink = _swa_sink_column(sink_ref, kvh)

        @pl.when((kvh == 0) & (i == 0))
        def _():
            dkad_ref[...] = jnp.zeros_like(dkad_ref)
            dvad_ref[...] = jnp.zeros_like(dvad_ref)

        @pl.when(i == 0)
        def _():
            dbias_ref[...] = jnp.zeros_like(dbias_ref)
            dsk_ref[...] = jnp.zeros_like(dsk_ref)

        for r in range(SWA_SUB):
            rs = slice(r * W, (r + 1) * W)
            ps = slice((r - 1) * W, r * W)
            k_own, v_own = k_ref[rs, :], v_ref[rs, :]
            k_prev = kp_ref[...] if r == 0 else k_ref[ps, :]
            v_prev = vp_ref[...] if r == 0 else v_ref[ps, :]
            qs = _swa_stack(q_ref[rs, :])
            dos = _swa_stack(do_ref[rs, :])
            lse_b = jnp.concatenate([lse_ref[g, rs, :] for g in range(SWA_GROUP)], axis=0)
            dl_b = jnp.concatenate([dl_ref[g, rs, :] for g in range(SWA_GROUP)], axis=0)
            s = jnp.concatenate([_dot(qs, k_prev, NT), _dot(qs, k_own, NT)], axis=1) * scale + bias_ref[...]
            s = jnp.where(_swa_valid(r, i), s, NEG)
            p = jnp.exp(s - jnp.tile(lse_b, (1, 2)))
            dp = jnp.concatenate([_dot(dos, v_prev, NT), _dot(dos, v_own, NT)], axis=1)
            ds = p * (dp - jnp.tile(dl_b, (1, 2)))
            sink_term = jnp.exp(sink - lse_b) * dl_b
            for g in range(SWA_GROUP):
                dbias_ref[g] += ds[g * W:(g + 1) * W]
                dsk_ref[g:g + 1, :] += jnp.sum(sink_term[g * W:(g + 1) * W], axis=0, keepdims=True)
            dsb = ds.astype(BF16)
            pb = p.astype(BF16)
            dq_ref[rs, :] = _swa_unstack((_dot(dsb[:, :W], k_prev) + _dot(dsb[:, W:], k_own)) * scale)
            own_row = pl.multiple_of(i * SWA_TB + r * W, W)
            dkad_ref[kvh, pl.ds(own_row, W), :] += _dot(dsb[:, W:], qs, TN) * scale
            dvad_ref[kvh, pl.ds(own_row, W), :] += _dot(pb[:, W:], dos, TN)
            dk_prev = _dot(dsb[:, :W], qs, TN) * scale
            dv_prev = _dot(pb[:, :W], dos, TN)
            if r == 0:
                @pl.when(i > 0)
                def _():
                    prev_row = pl.multiple_of(i * SWA_TB - W, W)
                    dkad_ref[kvh, pl.ds(prev_row, W), :] += dk_prev
                    dvad_ref[kvh, pl.ds(prev_row, W), :] += dv_prev
            else:
                prev_row = pl.multiple_of(i * SWA_TB + (r - 1) * W, W)
                dkad_ref[kvh, pl.ds(prev_row, W), :] += dk_prev
                dvad_ref[kvh, pl.ds(prev_row, W), :] += dv_prev

    qspec, own, prev, stat, bspec = _swa_specs(T)
    full = pl.BlockSpec((SWA_KV_HEADS, T, LANES), lambda h, i: (0, 0, 0))
    return pl.pallas_call(
        body, name="swa_bwd", grid=(SWA_KV_HEADS, nb),
        in_specs=[pl.BlockSpec(memory_space=pltpu.SMEM), qspec, own, prev, own, prev, bspec, qspec, stat, stat],
        out_specs=[qspec, full, full, pl.BlockSpec((SWA_GROUP, W, 2 * W), lambda h, i: (h, 0, 0)),
                   pl.BlockSpec((None, 8, LANES), lambda h, i: (h, 0, 0))],
        out_shape=[jax.ShapeDtypeStruct((T, 512), F32), jax.ShapeDtypeStruct((SWA_KV_HEADS, T, LANES), F32),
                   jax.ShapeDtypeStruct((SWA_KV_HEADS, T, LANES), F32), jax.ShapeDtypeStruct((SWA_HEADS, W, 2 * W), F32),
                   jax.ShapeDtypeStruct((SWA_KV_HEADS, 8, LANES), F32)],
        compiler_params=_cparams("arbitrary", "arbitrary", vmem=VMEM_MID),
    )(sinks, q, kad, kad, vad, vad, bias.reshape(SWA_KV_HEADS, SWA_GROUP * W, 2 * W), do, lse, delta)


MEM_TQ = 4096


def _mem_fwd(q, mk, mv, T, tq):
    scale = MEM_HEAD_DIM ** -0.5

    def body(q_ref, k_ref, v_ref, o_ref, lse_ref):
        s = _dot(q_ref[...], k_ref[...], NT) * scale
        m = jnp.max(s, axis=1, keepdims=True)
        p = jnp.exp(s - m)
        l = jnp.sum(p, axis=1, keepdims=True)
        o_ref[...] = _dot((p / l).astype(BF16), v_ref[...]).astype(o_ref.dtype)
        lse_ref[...] = jnp.broadcast_to(m + jnp.log(l), (tq, LANES))

    qspec = pl.BlockSpec((tq, LANES), lambda h, i: (i, h))
    kspec = pl.BlockSpec((N_MEM, LANES), lambda h, i: (0, h))
    return pl.pallas_call(
        body, name="mem_fwd", grid=(MEM_HEADS, T // tq),
        in_specs=[qspec, kspec, kspec],
        out_specs=[qspec, pl.BlockSpec((None, tq, LANES), lambda h, i: (h, i, 0))],
        out_shape=[jax.ShapeDtypeStruct((T, 512), BF16), jax.ShapeDtypeStruct((MEM_HEADS, T, LANES), F32)],
        compiler_params=_cparams("parallel", "parallel"),
    )(q, mk, mv)


def _mem_bwd(q, mk, mv, do, lse, delta, T, tq):
    scale = MEM_HEAD_DIM ** -0.5
    rep = N_MEM // LANES

    def body(q_ref, k_ref, v_ref, do_ref, lse_ref, dl_ref, dq_ref, dk_ref, dv_ref):
        i = pl.program_id(1)

        @pl.when(i == 0)
        def _():
            dk_ref[...] = jnp.zeros_like(dk_ref)
            dv_ref[...] = jnp.zeros_like(dv_ref)

        qv, dov = q_ref[...], do_ref[...]
        s = _dot(qv, k_ref[...], NT) * scale
        p = jnp.exp(s - jnp.tile(lse_ref[...], (1, rep)))
        dp = _dot(dov, v_ref[...], NT)
        ds = p * (dp - jnp.tile(dl_ref[...], (1, rep)))
        dsb = ds.astype(BF16)
        dq_ref[...] = _dot(dsb, k_ref[...]) * scale
        dk_ref[...] += _dot(dsb, qv, TN) * scale
        dv_ref[...] += _dot(p.astype(BF16), dov, TN)

    qspec = pl.BlockSpec((tq, LANES), lambda h, i: (i, h))
    kspec = pl.BlockSpec((N_MEM, LANES), lambda h, i: (0, h))
    stat = pl.BlockSpec((None, tq, LANES), lambda h, i: (h, i, 0))
    return pl.pallas_call(
        body, name="mem_bwd", grid=(MEM_HEADS, T // tq),
        in_specs=[qspec, kspec, kspec, qspec, stat, stat],
        out_specs=[qspec, kspec, kspec],
        out_shape=[jax.ShapeDtypeStruct((T, 512), F32), jax.ShapeDtypeStruct((N_MEM, 512), F32),
                   jax.ShapeDtypeStruct((N_MEM, 512), F32)],
        compiler_params=_cparams("arbitrary", "arbitrary"),
    )(q, mk, mv, do, lse, delta)


def _mem_prep_fwd(mem, g_mem, w_kv, kn_gain, gm128):
    def body(mem_ref, g_ref, w_ref, kn_ref, gm_ref, memn_o, kv_o, mk_o, mv_o):
        xhat, _ = _rms_rows(mem_ref[...], None)
        memn = (xhat * g_ref[...]).astype(BF16)
        memn_o[...] = memn
        kv = _dot(memn, w_ref[...])
        kv_o[...] = kv
        gm = gm_ref[...]
        for c in range(4):
            sl = slice(c * LANES, (c + 1) * LANES)
            y, _ = _head_norm(kv[:, sl], gm, kn_ref[...])
            mk_o[:, sl] = y.astype(BF16)
        mv_o[...] = kv[:, 512:].astype(BF16)

    vm = pl.BlockSpec(memory_space=pltpu.VMEM)
    return pl.pallas_call(
        body, name="mem_prep_fwd", in_specs=[vm] * 5, out_specs=[vm] * 4,
        out_shape=[jax.ShapeDtypeStruct((N_MEM, D_MODEL), BF16), jax.ShapeDtypeStruct((N_MEM, D_MODEL), F32),
                   jax.ShapeDtypeStruct((N_MEM, 512), BF16), jax.ShapeDtypeStruct((N_MEM, 512), BF16)],
        compiler_params=pltpu.CompilerParams(vmem_limit_bytes=VMEM_MID),
    )(mem, g_mem, w_kv, kn_gain, gm128)


def _mem_prep_bwd(mem, g_mem, memn, kv, w_kv, kn_gain, gm128, dmk, dmv):
    def body(mem_ref, g_ref, memn_ref, kv_ref, w_ref, kn_ref, gm_ref, dmk_ref, dmv_ref, dw_o, dg_o, dkn_o, dkv_s):
        gm = gm_ref[...]
        dkn = jnp.zeros((1, LANES), F32)
        for c in range(4):
            sl = slice(c * LANES, (c + 1) * LANES)
            dx, dg = _head_norm_bwd(dmk_ref[:, sl], kv_ref[:, sl], gm, kn_ref[...])
            dkv_s[:, sl] = dx.astype(BF16)
            dkn = dkn + dg
        dkn_o[...] = dkn
        dkv_s[:, 512:] = dmv_ref[...].astype(BF16)
        dkv = dkv_s[...]
        dw_o[...] = _dot(memn_ref[...], dkv, TN)
        dmemn = _dot(dkv, w_ref[...], NT)
        xhat, _ = _rms_rows(mem_ref[...], None)
        dg_o[...] = jnp.sum(dmemn * xhat, axis=0, keepdims=True)

    vm = pl.BlockSpec(memory_space=pltpu.VMEM)
    return pl.pallas_call(
        body, name="mem_prep_bwd", in_specs=[vm] * 9, out_specs=[vm] * 3,
        out_shape=[jax.ShapeDtypeStruct((D_MODEL, D_MODEL), F32), jax.ShapeDtypeStruct((1, D_MODEL), F32),
                   jax.ShapeDtypeStruct((1, LANES), F32)],
        scratch_shapes=[pltpu.VMEM((N_MEM, D_MODEL), BF16)],
        compiler_params=pltpu.CompilerParams(vmem_limit_bytes=VMEM_MID),
    )(mem, g_mem, memn, kv, w_kv, kn_gain, gm128, dmk, dmv)


SLOT_O = D_MODEL // N_SHARD


def _merge_fwd(proj, b_gate, o3, w3, T, tb):
    def body(gl_ref, bg_ref, oa_ref, of_ref, om_ref, wa_ref, wf_ref, wm_ref, out_ref):
        o_refs = (oa_ref, of_ref, om_ref)
        w_refs = (wa_ref, wf_ref, wm_ref)
        for n in range(N_SHARD):
            acc = jnp.zeros((tb, SLOT_O), F32)
            for b in range(3):
                c0 = b * D_MODEL + n * SLOT_O
                g = jax.nn.sigmoid(gl_ref[:, c0:c0 + SLOT_O] + bg_ref[:, c0:c0 + SLOT_O])
                acc = acc + g * _dot(o_refs[b][...], w_refs[b][n])
            out_ref[:, n * SLOT_O:(n + 1) * SLOT_O] = acc.astype(out_ref.dtype)

    rows = pl.BlockSpec((tb, 512), lambda i: (i, 0))
    wspec = pl.BlockSpec((N_SHARD, 512, SLOT_O), lambda i: (0, 0, 0))
    return pl.pallas_call(
        body, name="merge_fwd", grid=(T // tb,),
        in_specs=[pl.BlockSpec((tb, GATE_W), lambda i: (i, 1)), pl.BlockSpec((1, GATE_W), lambda i: (0, 0)),
                  rows, rows, rows, wspec, wspec, wspec],
        out_specs=pl.BlockSpec((tb, D_MODEL), lambda i: (i, 0)),
        out_shape=jax.ShapeDtypeStruct((T, D_MODEL), BF16),
        compiler_params=_cparams("parallel", vmem=VMEM_BIG),
    )(proj, b_gate, *o3, *w3)


def _merge_bwd(proj, b_gate, o3, w3, dmerged, T, tb):
    heads = (SWA_HEADS, FOX_HEADS, MEM_HEADS)

    def body(gl_ref, bg_ref, oa_ref, of_ref, om_ref, wa_ref, wf_ref, wm_ref, dm_ref,
             dgl_o, doa_o, dof_o, dom_o, dla_o, dlf_o, dlm_o, dwa_o, dwf_o, dwm_o, dbg_o):
        i = pl.program_id(0)
        o_refs = (oa_ref, of_ref, om_ref)
        w_refs = (wa_ref, wf_ref, wm_ref)
        do_refs = (doa_o, dof_o, dom_o)
        dl_refs = (dla_o, dlf_o, dlm_o)
        dw_refs = (dwa_o, dwf_o, dwm_o)

        @pl.when(i == 0)
        def _():
            for r in dw_refs:
                r[...] = jnp.zeros_like(r)
            dbg_o[...] = jnp.zeros_like(dbg_o)

        lane = _lane((tb, LANES))
        for b in range(3):
            ob = o_refs[b][...]
            do = jnp.zeros((tb, 512), F32)
            for n in range(N_SHARD):
                c0 = b * D_MODEL + n * SLOT_O
                g = jax.nn.sigmoid(gl_ref[:, c0:c0 + SLOT_O] + bg_ref[:, c0:c0 + SLOT_O])
                dm = dm_ref[:, n * SLOT_O:(n + 1) * SLOT_O]
                y = _dot(ob, w_refs[b][n])
                dgl = dm * y * g * (1.0 - g)
                dgl_o[:, c0:c0 + SLOT_O] = dgl.astype(dgl_o.dtype)
                dbg_o[:, c0:c0 + SLOT_O] += jnp.sum(dgl, axis=0, keepdims=True)
                dy = (dm * g).astype(BF16)
                do = do + _dot(dy, w_refs[b][n], NT)
                dw_refs[b][n] += _dot(ob, dy, TN)
            do_refs[b][...] = do.astype(BF16)
            prod = do * ob.astype(F32)
            for c in range(4):
                blk = prod[:, c * LANES:(c + 1) * LANES]
                if heads[b] == 8:
                    lo = jnp.sum(jnp.where(lane < 64, blk, 0.0), axis=1, keepdims=True)
                    hi = jnp.sum(jnp.where(lane >= 64, blk, 0.0), axis=1, keepdims=True)
                    if b == 1:
                        aug = jnp.zeros((tb, LANES), F32)
                        for sub, dl in enumerate((lo, hi)):
                            for e, piece in enumerate(_split3(-dl)):
                                aug = jnp.where(lane == AUG_STRIDE * sub + AUG_C + e, piece.astype(F32), aug)
                        dl_refs[b][:, c * LANES:(c + 1) * LANES] = aug.astype(BF16)
                    else:
                        dl_refs[b][2 * c] = jnp.broadcast_to(lo, (tb, LANES))
                        dl_refs[b][2 * c + 1] = jnp.broadcast_to(hi, (tb, LANES))
                else:
                    dl_refs[b][c] = jnp.broadcast_to(jnp.sum(blk, axis=1, keepdims=True), (tb, LANES))

    rows = pl.BlockSpec((tb, 512), lambda i: (i, 0))
    wspec = pl.BlockSpec((N_SHARD, 512, SLOT_O), lambda i: (0, 0, 0))
    stat = lambda h: pl.BlockSpec((h, tb, LANES), lambda i: (0, i, 0))
    return pl.pallas_call(
        body, name="merge_bwd", grid=(T // tb,),
        in_specs=[pl.BlockSpec((tb, GATE_W), lambda i: (i, 1)), pl.BlockSpec((1, GATE_W), lambda i: (0, 0)),
                  rows, rows, rows, wspec, wspec, wspec, pl.BlockSpec((tb, D_MODEL), lambda i: (i, 0))],
        out_specs=[pl.BlockSpec((tb, GATE_W), lambda i: (i, 0)), rows, rows, rows,
                   stat(8), rows, stat(4), wspec, wspec, wspec, pl.BlockSpec((1, GATE_W), lambda i: (0, 0))],
        out_shape=[jax.ShapeDtypeStruct((T, GATE_W), BF16)] + [jax.ShapeDtypeStruct((T, 512), BF16)] * 3
        + [jax.ShapeDtypeStruct((8, T, LANES), F32), jax.ShapeDtypeStruct((T, 512), BF16),
           jax.ShapeDtypeStruct((4, T, LANES), F32)]
        + [jax.ShapeDtypeStruct((N_SHARD, 512, SLOT_O), F32)] * 3 + [jax.ShapeDtypeStruct((1, GATE_W), F32)],
        compiler_params=_cparams("arbitrary", vmem=VMEM_BIG),
    )(proj, b_gate, *o3, *w3, dmerged)


def _local_step(x, h, mem, tgt, small, g_in, w_kv, w_o3, w_out, w_up, w_down, reducer):
    T = x.shape[0]
    tm = min(512, T)
    tile2 = lambda v: jnp.tile(v.reshape(1, -1), (1, LANES // v.size))
    gains = jnp.concatenate([tile2(small["qn_swa"]), tile2(small["kn_swa"]), tile2(small["qn_fox"]),
                             tile2(small["kn_fox"]), tile2(small["qn_mem"]), jnp.zeros((3, LANES), F32)], axis=0)
    kn_mem = small["kn_mem"].reshape(1, LANES)
    bfor = jnp.pad(small["b_forget"].reshape(1, -1), ((0, 0), (0, LANES - FOX_HEADS)))
    gm64 = _group_mean_matrix(64)
    gm128 = _group_mean_matrix(128)
    tb_prep = min(512, T)
    ones = jnp.ones((tb_prep, tb_prep), F32)
    tril = jnp.tril(ones).astype(BF16)
    triu = jnp.triu(ones).astype(BF16)
    bucket = _t5_bucket_matrix()
    g_mix, g_mlp, g_mem = small["g_mix"], small["g_mlp"], small["g_mem"]
    b_gate = small["b_gate"]
    sinks = small["sink_swa"].reshape(-1)

    tl = min(1024, T)
    sq = pl.BlockSpec((tl, D_MODEL), lambda i, j, k: (i, j))
    wc = _w_in_to_segments(g_in)
    (proj,) = _matmul(
        "mm_proj", h, wc, dims=NN, grid=(T // tl, PROJ_W // PROJ_TN, 1),
        a_spec=pl.BlockSpec((tl, D_MODEL), lambda i, j, k: (i, 0)),
        b_spec=pl.BlockSpec((D_MODEL, PROJ_TN), lambda i, j, k: (0, j)),
        acc_shape=(tl, PROJ_TN),
        outs=[(jax.ShapeDtypeStruct((T, PROJ_W), F32), pl.BlockSpec((tl, PROJ_TN), lambda i, j, k: (i, j)))],
        epilogue=_epi_store)
    qa, qf, kf, vf, qm, kad, vad, qf_aug, kf_aug = _prep_fwd(proj, gains, bfor, tril, gm64, gm128, T, tb_prep)
    bias = _swa_bias(small["rel_bias"], bucket)
    o_swa, lse_swa = _swa_fwd(sinks, qa, kad, vad, bias, T)
    o_fox, qf_aug_bwd = _fox_fwd(qf, qf_aug, kf, kf_aug, vf, T, min(FOX_TQ, T), min(FOX_TK, T))
    memn, kv, mk, mv = _mem_prep_fwd(mem, g_mem, w_kv, kn_mem, gm128)
    o_mem, lse_mem = _mem_fwd(qm, mk, mv, T, min(MEM_TQ, T))
    o3 = (o_swa, o_fox, o_mem)
    merged = _merge_fwd(proj, b_gate, o3, w_o3, T, min(512, T))

    def epi_residual(acc, extra_refs, out_refs, ij):
        out_refs[0][...] = extra_refs[0][...] + acc

    row_full = pl.BlockSpec((tm, D_MODEL), lambda i, j, k: (i, 0))
    row_big = pl.BlockSpec((tl, D_MODEL), lambda i, j, k: (i, 0))
    whole = pl.BlockSpec((D_MODEL, D_MODEL), lambda i, j, k: (0, 0))
    (x2,) = _matmul(
        "mm_out", merged, w_out, dims=NN, grid=(T // tl, 1, 1),
        a_spec=row_big, b_spec=whole,
        acc_shape=(tl, D_MODEL), extra=[(x, row_big)],
        outs=[(jax.ShapeDtypeStruct((T, D_MODEL), F32), row_big)], epilogue=epi_residual)
    hm = _rmsnorm("rms_mlp", x2, g_mlp, tl)

    def epi_relu2(acc, extra_refs, out_refs, ij):
        out_refs[0][...] = acc.astype(BF16)
        r = jnp.maximum(acc, 0.0)
        out_refs[1][...] = (r * r).astype(BF16)

    up, u = _matmul(
        "mm_up", hm, w_up, dims=NN, grid=(T // tl, N_SHARD, 1),
        a_spec=row_big, b_spec=pl.BlockSpec((None, D_MODEL, D_MODEL), lambda i, j, k: (j, 0, 0)),
        acc_shape=(tl, D_MODEL),
        outs=[(jax.ShapeDtypeStruct((T, D_FF), BF16), sq), (jax.ShapeDtypeStruct((T, D_FF), BF16), sq)],
        epilogue=epi_relu2)

    def epi_loss(acc, extra_refs, out_refs, ij):
        y = extra_refs[0][...] + acc
        err = y - extra_refs[1][...]
        dyv = err * (1.0 / D_MODEL)
        out_refs[0][...] = dyv
        out_refs[2][...] = dyv.astype(BF16)
        sq = jnp.sum(jnp.sum(err * err, axis=1, keepdims=True), axis=0, keepdims=True)

        @pl.when(ij[0] == 0)
        def _():
            out_refs[1][...] = jnp.zeros_like(out_refs[1])

        out_refs[1][...] += jnp.broadcast_to(sq, out_refs[1].shape)

    kblk = pl.BlockSpec((tl, D_MODEL), lambda i, j, k: (i, k))
    dy, loss_acc, dy_bf = _matmul(
        "mm_down", u, w_down, dims=NN, grid=(T // tl, 1, N_SHARD),
        a_spec=kblk, b_spec=pl.BlockSpec((D_MODEL, D_MODEL), lambda i, j, k: (k, 0)),
        acc_shape=(tl, D_MODEL), extra=[(x2, row_big), (tgt, row_big)],
        outs=[(jax.ShapeDtypeStruct((T, D_MODEL), F32), row_big),
              (jax.ShapeDtypeStruct((8, LANES), F32), pl.BlockSpec((8, LANES), lambda i, j, k: (0, 0))),
              (jax.ShapeDtypeStruct((T, D_MODEL), BF16), row_big)],
        epilogue=epi_loss)
    loss = loss_acc[0, 0] * (0.5 / D_MODEL)

    def epi_dup(acc, extra_refs, out_refs, ij):
        out_refs[0][...] = (acc * (2.0 * jnp.maximum(extra_refs[0][...].astype(F32), 0.0))).astype(BF16)

    (dup,) = _matmul(
        "mm_dup", dy_bf, w_down, dims=NT, grid=(T // tl, N_SHARD, 1),
        a_spec=row_big, b_spec=pl.BlockSpec((D_MODEL, D_MODEL), lambda i, j, k: (j, 0)),
        acc_shape=(tl, D_MODEL), extra=[(up, sq)],
        outs=[(jax.ShapeDtypeStruct((T, D_FF), BF16), sq)], epilogue=epi_dup)

    nkt = T // tl
    t_rows = pl.BlockSpec((tl, D_MODEL), lambda i, j, k: (k, i))
    t_cols = pl.BlockSpec((tl, D_MODEL), lambda i, j, k: (k, j))
    (d_w_down,) = _matmul(
        "mm_dw_down", u, dy_bf, dims=TN, grid=(N_SHARD, 1, nkt),
        a_spec=t_rows, b_spec=t_cols, acc_shape=(D_MODEL, D_MODEL),
        outs=[(jax.ShapeDtypeStruct((D_FF, D_MODEL), F32), pl.BlockSpec((D_MODEL, D_MODEL), lambda i, j, k: (i, 0)))],
        epilogue=_epi_store)
    (d_w_up,) = _matmul(
        "mm_dw_up", hm, dup, dims=TN, grid=(1, N_SHARD, nkt),
        a_spec=t_rows, b_spec=t_cols, acc_shape=(D_MODEL, D_MODEL),
        outs=[(jax.ShapeDtypeStruct((N_SHARD, D_MODEL, D_MODEL), F32),
               pl.BlockSpec((None, D_MODEL, D_MODEL), lambda i, j, k: (j, 0, 0)))],
        epilogue=_epi_store)

    def epi_rms_bwd(acc, extra_refs, out_refs, ij):
        dx, dg = _rmsnorm_bwd_rows(acc, extra_refs[0][...], extra_refs[1][...])
        out_refs[0][...] = dx + extra_refs[2][...]

        @pl.when(ij[0] == 0)
        def _():
            out_refs[1][...] = jnp.zeros_like(out_refs[1])

        out_refs[1][...] += dg

    gain_spec = pl.BlockSpec((1, D_MODEL), lambda i, j, k: (0, 0))
    dx2, d_g_mlp = _matmul(
        "mm_dhm", dup, w_up, dims=NT, grid=(T // tl, 1, N_SHARD),
        a_spec=kblk, b_spec=pl.BlockSpec((None, D_MODEL, D_MODEL), lambda i, j, k: (k, 0, 0)),
        acc_shape=(tl, D_MODEL), extra=[(x2, row_big), (g_mlp, gain_spec), (dy, row_big)],
        outs=[(jax.ShapeDtypeStruct((T, D_MODEL), F32), row_big), (jax.ShapeDtypeStruct((1, D_MODEL), F32), gain_spec)],
        epilogue=epi_rms_bwd)

    (dmerged,) = _matmul(
        "mm_dmerged", dx2, w_out, dims=NT, grid=(T // tl, 1, 1),
        a_spec=row_big, b_spec=whole,
        acc_shape=(tl, D_MODEL), outs=[(jax.ShapeDtypeStruct((T, D_MODEL), F32), row_big)], epilogue=_epi_store)
    (d_w_out,) = _matmul(
        "mm_dw_out", merged, dx2, dims=TN, grid=(1, 1, nkt),
        a_spec=t_rows, b_spec=t_cols, acc_shape=(D_MODEL, D_MODEL),
        outs=[(jax.ShapeDtypeStruct((D_MODEL, D_MODEL), F32), whole)],
        epilogue=_epi_store)
    (dgl, do_swa, do_fox, do_mem, dl_swa, do_fox_aug, dl_mem, d_wo_swa, d_wo_fox, d_wo_mem, d_b_gate) = _merge_bwd(
        proj, b_gate, o3, w_o3, dmerged, T, min(512, T))

    dqm, dmk, dmv = _mem_bwd(qm, mk, mv, do_mem, lse_mem, dl_mem, T, min(MEM_TQ, T))
    d_w_kv, d_g_mem, d_kn_mem = _mem_prep_bwd(mem, g_mem, memn, kv, w_kv, kn_mem, gm128, dmk, dmv)
    do_swa = reducer.early_start({"w_mlp_down": d_w_down, "w_mlp_up": d_w_up, "w_out": d_w_out, "w_mem_kv": d_w_kv,
                                  "w_o_swa": d_wo_swa, "w_o_fox": d_wo_fox, "w_o_mem": d_wo_mem}, do_swa)
    dqa, dkad, dvad, dbias, dsk = _swa_bwd(sinks, qa, kad, vad, bias, do_swa, lse_swa, dl_swa, T)
    dqa, do_fox = reducer.early_send((dqa, do_fox))
    dqf, dqf_aug, dkf, dkf_aug, dvf = _fox_bwd(qf, qf_aug_bwd, kf, kf_aug, vf, do_fox, do_fox_aug, T,
                                               min(FOX_BWD_TQ, T), min(FOX_BWD_TK, T))
    dvf = reducer.early_finish(dvf)
    d_rel = _swa_bias_bwd(dbias, bucket)
    dlo, gacc = _prep_bwd(proj, dqa, dkad, dvad, dqf, dkf, dvf, dqm, dqf_aug, dkf_aug, gains, bfor, triu, gm64, gm128,
                          T, tb_prep)

    def dwc_half(name, dpart):
        (res,) = _matmul(
            name, h, dpart, dims=TN, grid=(1, LO_W // D_MODEL, nkt),
            a_spec=t_rows, b_spec=t_cols, acc_shape=(D_MODEL, D_MODEL),
            outs=[(jax.ShapeDtypeStruct((D_MODEL, LO_W), F32), pl.BlockSpec((D_MODEL, D_MODEL), lambda i, j, k: (0, j)))],
            epilogue=_epi_store)
        return res

    d_wc_lo = dwc_half("mm_dwc_lo", dlo)
    d_wc_gl = dwc_half("mm_dwc_gl", dgl)
    dlo = reducer.late_start({"wc_lo": d_wc_lo, "wc_gl": d_wc_gl}, dlo)
    (dh_lo,) = _matmul(
        "mm_dh_lo", dlo, wc, dims=NT, grid=(T // tl, 1, LO_W // D_MODEL),
        a_spec=kblk, b_spec=pl.BlockSpec((D_MODEL, D_MODEL), lambda i, j, k: (0, k)),
        acc_shape=(tl, D_MODEL), outs=[(jax.ShapeDtypeStruct((T, D_MODEL), F32), row_big)], epilogue=_epi_store)
    dh_lo = reducer.late_send(dh_lo)

    def epi_dx(acc, extra_refs, out_refs, ij):
        dhh = acc + extra_refs[3][...]
        dx, dg = _rmsnorm_bwd_rows(dhh, extra_refs[0][...], extra_refs[1][...])
        out_refs[0][...] = dx + extra_refs[2][...]

        @pl.when(ij[0] == 0)
        def _():
            out_refs[1][...] = jnp.zeros_like(out_refs[1])

        out_refs[1][...] += dg

    grad_x, d_g_mix = _matmul(
        "mm_dh_gl", dgl, wc, dims=NT, grid=(T // tl, 1, GATE_W // D_MODEL),
        a_spec=kblk, b_spec=pl.BlockSpec((D_MODEL, D_MODEL), lambda i, j, k: (0, k + LO_W // D_MODEL)),
        acc_shape=(tl, D_MODEL), extra=[(x, row_big), (g_mix, gain_spec), (dx2, row_big), (dh_lo, row_big)],
        outs=[(jax.ShapeDtypeStruct((T, D_MODEL), F32), row_big), (jax.ShapeDtypeStruct((1, D_MODEL), F32), gain_spec)],
        epilogue=epi_dx, vmem=VMEM_MAX)

    fold64 = lambda row: (row[:64] + row[64:]).reshape(1, 64)
    grads = {
        "g_mix": d_g_mix, "b_gate": d_b_gate, "b_forget": gacc[5, :FOX_HEADS].reshape(1, FOX_HEADS),
        "qn_swa": fold64(gacc[0]), "kn_swa": fold64(gacc[1]),
        "sink_swa": -dsk[:, :SWA_GROUP, 0].reshape(1, SWA_HEADS), "rel_bias": d_rel[:, :SWA_HEADS],
        "qn_fox": fold64(gacc[2]), "kn_fox": fold64(gacc[3]),
        "g_mem": d_g_mem, "qn_mem": gacc[4].reshape(1, LANES), "kn_mem": d_kn_mem, "g_mlp": d_g_mlp,
    }
    return loss, grad_x, grads


MESH = pl.DeviceIdType.MESH


def _place():
    x, y, c = lax.axis_index("x"), lax.axis_index("y"), lax.axis_index("c")
    chips = [(1 - x, y), (x, 1 - y), (1 - x, 1 - y)]
    return x, y, c, chips


def _handshake(peers):
    barrier = pltpu.get_barrier_semaphore()
    for peer in peers:
        pl.semaphore_signal(barrier, inc=1, device_id=peer, device_id_type=MESH)
    pl.semaphore_wait(barrier, len(peers))


def _all_gather_shards_async(name, collective_id, slots):
    n = len(slots)
    bufs = [jax.new_ref(s, memory_space=pltpu.MemorySpace.HBM) for s in slots]

    def body(ici_send, ici_recv, d2d_send, d2d_recv):
        x, y, c, chips = _place()
        sibling = (x, y, 1 - c)
        me = 2 * x + y
        _handshake([(px, py, c) for px, py in chips] + [sibling])

        def half(a, who):
            hr = slots[a].shape[1] // 2
            return pl.ds(pl.multiple_of(who * hr, hr), hr)

        def ici(a, j, slot, to):
            return pltpu.make_async_remote_copy(
                src_ref=bufs[a].at[me, half(a, c)], dst_ref=bufs[a].at[slot, half(a, c)],
                send_sem=ici_send.at[3 * a + j], recv_sem=ici_recv.at[3 * a + j], device_id=to, device_id_type=MESH)

        def d2d(a, j, slot, which):
            part = bufs[a].at[slot, half(a, which)]
            return pltpu.make_async_remote_copy(
                src_ref=part, dst_ref=part, send_sem=d2d_send.at[3 * a + j], recv_sem=d2d_recv.at[3 * a + j],
                device_id=sibling, device_id_type=MESH)

        sends = [ici(a, j, me, (*chip, c)) for a in range(n) for j, chip in enumerate(chips)]
        for cp in sends:
            cp.start()
        passed = []
        for a in range(n):
            for j, (px, py) in enumerate(chips):
                ici(a, j, 2 * px + py, (px, py, c)).wait_recv()
                cp = d2d(a, j, 2 * px + py, c)
                cp.start()
                passed.append(cp)
        for a in range(n):
            for j, (px, py) in enumerate(chips):
                d2d(a, j, 2 * px + py, 1 - c).wait_recv()
        for cp in sends + passed:
            cp.wait_send()

    pl.kernel(
        body, mesh=plsc.ScalarSubcoreMesh(axis_name="seq", num_cores=1), name=name,
        scratch_types=[pltpu.SemaphoreType.DMA((3 * n,))] * 4,
        compiler_params=pltpu.CompilerParams(collective_id=collective_id),
    )()
    return [b[...] for b in bufs]


def _sequencer_call(name, collective_id, n_sems, body):
    pl.kernel(
        body, mesh=plsc.ScalarSubcoreMesh(axis_name="seq", num_cores=1), name=name,
        scratch_types=[pltpu.SemaphoreType.DMA((n_sems,))] * 2,
        compiler_params=pltpu.CompilerParams(collective_id=collective_id),
    )()


def _hbm_ref(value):
    return jax.new_ref(value, memory_space=pltpu.MemorySpace.HBM)


def _all_gather_neighbours(name, collective_id, slots):
    buf = jax.new_ref(slots, memory_space=pltpu.MemorySpace.HBM)
    hr = slots.shape[1] // 2
    qr = hr // 2

    def body(ici_send, ici_recv, fwd_send, fwd_recv, d2d_send, d2d_recv):
        x, y, c, _ = _place()
        sibling = (x, y, 1 - c)
        nbr = [(1 - x, y), (x, 1 - y)]
        diag = (1 - x, 1 - y)
        slot_of = lambda chip: 2 * chip[0] + chip[1]
        me = 2 * x + y
        _handshake([(*nbr[0], c), (*nbr[1], c), sibling])

        def rows(core, quarter=None):
            start = core * hr if quarter is None else core * hr + quarter * qr
            size = hr if quarter is None else qr
            return pl.ds(pl.multiple_of(start, size), size)

        def copy(part, send_sem, recv_sem, to):
            return pltpu.make_async_remote_copy(src_ref=part, dst_ref=part, send_sem=send_sem, recv_sem=recv_sem,
                                                device_id=to, device_id_type=MESH)

        sends = [copy(buf.at[me, rows(c)], ici_send.at[j], ici_recv.at[j], (*nbr[j], c)) for j in range(2)]
        for cp in sends:
            cp.start()
        moving = []
        for j in range(2):
            theirs = slot_of(nbr[j])
            copy(buf.at[theirs, rows(c)], ici_send.at[j], ici_recv.at[j], (*nbr[j], c)).wait_recv()
            on = copy(buf.at[theirs, rows(c, j)], fwd_send.at[j], fwd_recv.at[j], (*nbr[1 - j], c))
            down = copy(buf.at[theirs, rows(c)], d2d_send.at[j], d2d_recv.at[j], sibling)
            on.start()
            down.start()
            moving += [on, down]
        for j in range(2):
            part = buf.at[slot_of(diag), rows(c, j)]
            copy(part, fwd_send.at[j], fwd_recv.at[j], (*nbr[1 - j], c)).wait_recv()
            down = copy(part, d2d_send.at[2 + j], d2d_recv.at[2 + j], sibling)
            down.start()
            moving.append(down)
        for j in range(2):
            copy(buf.at[slot_of(nbr[j]), rows(1 - c)], d2d_send.at[j], d2d_recv.at[j], sibling).wait_recv()
            copy(buf.at[slot_of(diag), rows(1 - c, j)], d2d_send.at[2 + j], d2d_recv.at[2 + j], sibling).wait_recv()
        for cp in sends + moving:
            cp.wait_send()

    pl.kernel(
        body, mesh=plsc.ScalarSubcoreMesh(axis_name="seq", num_cores=1), name=name,
        scratch_types=[pltpu.SemaphoreType.DMA((2,))] * 4 + [pltpu.SemaphoreType.DMA((4,))] * 2,
        compiler_params=pltpu.CompilerParams(collective_id=collective_id),
    )()
    return buf[...]


def _pair_exchange(name, collective_id, gs):
    n = len(gs)
    src = [_hbm_ref(g) for g in gs]
    stage = [jax.empty_ref(jax.ShapeDtypeStruct((N_SHARD, g.shape[1] // 2, g.shape[2]), g.dtype),
                           memory_space=pltpu.MemorySpace.HBM) for g in gs]

    def body(send_sem, recv_sem):
        x, y, c, _ = _place()
        sibling = (x, y, 1 - c)
        _handshake([sibling])
        copies = []
        for a in range(n):
            hr = gs[a].shape[1] // 2
            theirs = pl.ds(pl.multiple_of((1 - c) * hr, hr), hr)
            copies.append(pltpu.make_async_remote_copy(
                src_ref=src[a].at[:, theirs, :], dst_ref=stage[a], send_sem=send_sem.at[a], recv_sem=recv_sem.at[a],
                device_id=sibling, device_id_type=MESH))
        for cp in copies:
            cp.start()
        for cp in copies:
            cp.wait()

    _sequencer_call(name, collective_id, n, body)
    return [s[...] for s in stage]


def _chip_exchange(name, collective_id, sums):
    n = len(sums)
    src = [_hbm_ref(s) for s in sums]
    got = [jax.empty_ref(jax.ShapeDtypeStruct((3,) + s.shape[1:], s.dtype), memory_space=pltpu.MemorySpace.HBM)
           for s in sums]

    def body(send_sem, recv_sem):
        x, y, c, chips = _place()
        _handshake([(px, py, c) for px, py in chips])
        copies = []
        for a in range(n):
            for j, (px, py) in enumerate(chips):
                copies.append(pltpu.make_async_remote_copy(
                    src_ref=src[a].at[2 * px + py], dst_ref=got[a].at[j],
                    send_sem=send_sem.at[3 * a + j], recv_sem=recv_sem.at[3 * a + j],
                    device_id=(px, py, c), device_id_type=MESH))
        for cp in copies:
            cp.start()
        for cp in copies:
            cp.wait()

    _sequencer_call(name, collective_id, 3 * n, body)
    return [g[...] for g in got]


def _pair_gather(name, collective_id, fulls):
    n = len(fulls)
    full = [_hbm_ref(f) for f in fulls]

    def body(send_sem, recv_sem):
        x, y, c, _ = _place()
        sibling = (x, y, 1 - c)
        _handshake([sibling])
        copies = []
        for a in range(n):
            hr = fulls[a].shape[0] // 2
            mine = full[a].at[pl.ds(pl.multiple_of(c * hr, hr), hr)]
            copies.append(pltpu.make_async_remote_copy(
                src_ref=mine, dst_ref=mine, send_sem=send_sem.at[a], recv_sem=recv_sem.at[a],
                device_id=sibling, device_id_type=MESH))
        for cp in copies:
            cp.start()
        for cp in copies:
            cp.wait()

    _sequencer_call(name, collective_id, n, body)
    return [f[...] for f in full]


ELEMENTWISE_BLOCK_ELEMS = 512 * 1024


def _row_block(rows, cols):
    rb = 8
    while rb * 2 * cols <= ELEMENTWISE_BLOCK_ELEMS and rb * 2 <= rows:
        rb *= 2
    return rb


def _pair_sum(name, place, g, stage):
    _, R, C = g.shape
    hr = R // 2
    rb = _row_block(hr, C)
    nb = hr // rb

    def body(place_ref, g_ref, st_ref, sum_bf, own_f32):
        s = pl.program_id(1)
        tot = g_ref[...] + st_ref[...]
        sum_bf[...] = tot.astype(BF16)

        @pl.when(s == place_ref[0])
        def _():
            own_f32[...] = tot

    return pl.pallas_call(
        body, name=name,
        grid_spec=pltpu.PrefetchScalarGridSpec(
            num_scalar_prefetch=1, grid=(nb, N_SHARD),
            in_specs=[pl.BlockSpec((None, rb, C), lambda i, s, pr: (s, pr[1] * nb + i, 0)),
                      pl.BlockSpec((None, rb, C), lambda i, s, pr: (s, i, 0))],
            out_specs=[pl.BlockSpec((None, rb, C), lambda i, s, pr: (s, i, 0)),
                       pl.BlockSpec((rb, C), lambda i, s, pr: (i, 0))]),
        out_shape=[jax.ShapeDtypeStruct((N_SHARD, hr, C), BF16), jax.ShapeDtypeStruct((hr, C), F32)],
        compiler_params=_cparams("arbitrary", "arbitrary"),
    )(place, g, stage)


def _final_sum(name, place, own, got):
    hr, C = own.shape
    rb = _row_block(hr, C)
    nb = hr // rb

    def body(place_ref, own_ref, got_ref, o_ref):
        o_ref[...] = ((own_ref[...] + got_ref[0].astype(F32)) + got_ref[1].astype(F32)) + got_ref[2].astype(F32)

    return pl.pallas_call(
        body, name=name,
        grid_spec=pltpu.PrefetchScalarGridSpec(
            num_scalar_prefetch=1, grid=(nb,),
            in_specs=[pl.BlockSpec((rb, C), lambda i, pr: (i, 0)), pl.BlockSpec((3, rb, C), lambda i, pr: (0, i, 0))],
            out_specs=pl.BlockSpec((rb, C), lambda i, pr: (pr[1] * nb + i, 0))),
        out_shape=jax.ShapeDtypeStruct((2 * hr, C), F32),
        compiler_params=_cparams("arbitrary"),
    )(place, own, got)


def _adamw_math(w, g, m, v):
    m = ADAM_B1 * m + (1.0 - ADAM_B1) * g
    v = ADAM_B2 * v + (1.0 - ADAM_B2) * (g * g)
    m_hat = m / (1.0 - ADAM_B1 ** ADAM_STEP)
    v_hat = v / (1.0 - ADAM_B2 ** ADAM_STEP)
    delta = -ADAM_LR * (m_hat / (jnp.sqrt(v_hat) + ADAM_EPS) + ADAM_WD * w)
    return delta, m, v


def _adamw(name, w, g, m, v):
    R, Cw = w.shape
    Cg = g.shape[1]
    rb = _row_block(R, Cg)

    def body(w_ref, g_ref, m_ref, v_ref, g_o, d_o, m_o, v_o):
        gv = g_ref[...]
        delta, mn, vn = _adamw_math(w_ref[...], gv, m_ref[...], v_ref[...])
        g_o[...] = gv
        d_o[...] = delta
        m_o[...] = mn
        v_o[...] = vn

    blk = pl.BlockSpec((rb, Cg), lambda i: (i, 0))
    return pl.pallas_call(
        body, name=name, grid=(R // rb,),
        in_specs=[blk] * 4, out_specs=[blk] * 4,
        out_shape=[jax.ShapeDtypeStruct((R, Cw), F32)] * 4,
        compiler_params=_cparams("parallel"),
    )(w, g, m, v)


N_DEV = 8
SMALL_ROWS = 64


def _small_allreduce_adamw(g, w, m, v):
    def body(g_ref, w_ref, m_ref, v_ref, all_ref, gs_o, d_o, m_o, v_o, send_sems, recv_sems, local_sem):
        x, y, c, chips = _place()
        me, sibling = (x, y, c), (x, y, 1 - c)

        def rows(px, py, pc):
            return all_ref.at[pl.ds(pl.multiple_of((4 * px + 2 * py + pc) * SMALL_ROWS, SMALL_ROWS), SMALL_ROWS), :]

        def copy(k, block, to, src=None):
            return pltpu.make_async_remote_copy(
                src_ref=rows(*block) if src is None else src, dst_ref=rows(*block),
                send_sem=send_sems.at[k], recv_sem=recv_sems.at[k], device_id=to, device_id_type=MESH)

        mine = pltpu.make_async_copy(g_ref, rows(*me), local_sem)
        mine.start()
        first = [copy(0, me, sibling, src=g_ref)]
        first += [copy(1 + j, me, (*chip, c), src=g_ref) for j, chip in enumerate(chips)]
        for cp in first:
            cp.start()
        passed = [copy(4 + j, (*chip, c), sibling) for j, chip in enumerate(chips)]
        for j, chip in enumerate(chips):
            copy(1 + j, (*chip, c), me).wait_recv()
            passed[j].start()
        copy(0, sibling, me).wait_recv()
        for j, chip in enumerate(chips):
            copy(4 + j, (*chip, 1 - c), me).wait_recv()
        for cp in first + passed:
            cp.wait_send()
        mine.wait()

        tot = all_ref[0:SMALL_ROWS, :]
        for d in range(1, N_DEV):
            tot = tot + all_ref[d * SMALL_ROWS:(d + 1) * SMALL_ROWS, :]
        delta, mn, vn = _adamw_math(w_ref[...], tot, m_ref[...], v_ref[...])
        gs_o[...] = tot
        d_o[...] = delta
        m_o[...] = mn
        v_o[...] = vn

    vm = pl.BlockSpec(memory_space=pltpu.VMEM)
    shp = jax.ShapeDtypeStruct((SMALL_ROWS, LANES), F32)
    res = pl.pallas_call(
        body, name="small_allreduce_adamw", in_specs=[vm] * 4, out_specs=[vm] * 5,
        out_shape=[jax.ShapeDtypeStruct((N_DEV * SMALL_ROWS, LANES), F32), shp, shp, shp, shp],
        scratch_shapes=[pltpu.SemaphoreType.DMA((7,)), pltpu.SemaphoreType.DMA((7,)), pltpu.SemaphoreType.DMA],
    )(g, w, m, v)
    return res[1:]


SMALL_NAMES = ("g_mix", "b_gate", "b_forget", "qn_swa", "kn_swa", "sink_swa", "rel_bias", "qn_fox", "kn_fox",
               "g_mem", "qn_mem", "kn_mem", "g_mlp")
BIG_NAMES = ("w_in", "w_mem_kv", "w_o_swa", "w_o_fox", "w_o_mem", "w_out", "w_mlp_up", "w_mlp_down")
WEIGHT_NAMES = ("g_mix", "w_in", "b_gate", "b_forget", "qn_swa", "kn_swa", "sink_swa", "rel_bias", "qn_fox", "kn_fox",
                "g_mem", "w_mem_kv", "qn_mem", "kn_mem", "w_o_swa", "w_o_fox", "w_o_mem", "w_out", "g_mlp",
                "w_mlp_up", "w_mlp_down")


def _pack_small(parts, extra=None):
    rows = []
    for n in SMALL_NAMES:
        flat = parts[n].reshape(-1).astype(F32)
        flat = jnp.pad(flat, (0, (-flat.size) % LANES))
        rows.append(flat.reshape(-1, LANES))
    if extra is not None:
        rows.append(jnp.pad(extra.reshape(1, 1), ((0, 0), (0, LANES - 1))))
    packed = jnp.concatenate(rows, axis=0)
    return jnp.pad(packed, ((0, SMALL_ROWS - packed.shape[0]), (0, 0)))


def _unpack_small(packed, shapes):
    out, r = {}, 0
    for n in SMALL_NAMES:
        size = math.prod(shapes[n])
        nr = -(-size // LANES)
        out[n] = packed[r:r + nr].reshape(-1)[:size].reshape(shapes[n])
        r += nr
    return out, packed[r, 0]


W_IN_SEGMENTS = ((C_QA, 0, 512), (C_QF, 768, 512), (C_KF, 1280, 512), (C_VF, 1792, 512), (C_QM, 2312, 512),
                 (C_KA, 512, 128), (C_VA, 640, 128), (C_FL, 2304, FOX_HEADS), (C_GL, 2824, GATE_W))
RELAYOUT_ROWS = 256


def _permute_pieces(src_of_dst):
    blocks = []
    for b in range(len(src_of_dst) // LANES):
        runs, lane = [], 0
        while lane < LANES:
            src = src_of_dst[b * LANES + lane]
            if src is None:
                lane += 1
                continue
            plane, col = src
            end = lane + 1
            while (end < LANES and src_of_dst[b * LANES + end] == (plane, col + end - lane)
                   and (col + end - lane) // LANES == col // LANES):
                end += 1
            runs.append((plane, col // LANES, (lane - col) % LANES, lane, end))
            lane = end
        blocks.append(runs)
    return blocks


def _permuted_block(runs, load, rows):
    lane = _lane((rows, LANES))
    acc = jnp.zeros((rows, LANES), F32)
    for plane, blk, shift, lo, hi in runs:
        x = load(plane, blk).astype(F32)
        if shift:
            x = pltpu.roll(x, shift, 1)
        acc = x if (lo, hi) == (0, LANES) else jnp.where((lane >= lo) & (lane < hi), x, acc)
    return acc


def _w_in_to_segments(g_in):
    src_of_dst = [None] * PROJ_W
    for mine, theirs, width in W_IN_SEGMENTS:
        for k in range(width):
            src_of_dst[mine + k] = ((theirs + k) // IN_SHARD, (theirs + k) % IN_SHARD)
    blocks = _permute_pieces(src_of_dst)
    rb = RELAYOUT_ROWS

    def body(src_ref, out_ref):
        for b, runs in enumerate(blocks):
            blk = _permuted_block(runs, lambda p, c: src_ref[p, :, c * LANES:(c + 1) * LANES], rb)
            out_ref[:, b * LANES:(b + 1) * LANES] = blk.astype(out_ref.dtype)

    return pl.pallas_call(
        body, name="w_in_to_segments", grid=(D_MODEL // rb,),
        in_specs=[pl.BlockSpec((N_SHARD, rb, IN_SHARD_PAD), lambda i: (0, i, 0))],
        out_specs=pl.BlockSpec((rb, PROJ_W), lambda i: (i, 0)),
        out_shape=jax.ShapeDtypeStruct((D_MODEL, PROJ_W), g_in.dtype),
        compiler_params=_cparams("parallel", vmem=VMEM_MID),
    )(g_in)


def _w_in_from_segments(lo, gl):
    mine_of_theirs = {}
    for mine, theirs, width in W_IN_SEGMENTS:
        for k in range(width):
            mine_of_theirs[theirs + k] = mine + k
    src_of_dst = [None] * (N_SHARD * IN_SHARD_PAD)
    for s in range(N_SHARD):
        for l in range(IN_SHARD):
            j = mine_of_theirs[s * IN_SHARD + l]
            src_of_dst[s * IN_SHARD_PAD + l] = (j // LO_W, j % LO_W)
    blocks = _permute_pieces(src_of_dst)
    per_slot = IN_SHARD_PAD // LANES
    rb = RELAYOUT_ROWS

    def body(lo_ref, gl_ref, out_ref):
        planes = (lo_ref, gl_ref)
        for b, runs in enumerate(blocks):
            blk = _permuted_block(runs, lambda p, c: planes[p][:, c * LANES:(c + 1) * LANES], rb)
            c0 = (b % per_slot) * LANES
            out_ref[b // per_slot, :, c0:c0 + LANES] = blk

    half = pl.BlockSpec((rb, LO_W), lambda i: (i, 0))
    return pl.pallas_call(
        body, name="w_in_from_segments", grid=(D_MODEL // rb,),
        in_specs=[half, half],
        out_specs=pl.BlockSpec((N_SHARD, rb, IN_SHARD_PAD), lambda i: (0, i, 0)),
        out_shape=jax.ShapeDtypeStruct((N_SHARD, D_MODEL, IN_SHARD_PAD), F32),
        compiler_params=_cparams("parallel", vmem=VMEM_MID),
    )(lo, gl)


def _after(first, then):
    return lax.optimization_barrier((first, then))


class _ReduceGroup:
    def __init__(self, tag, first_collective_id, place):
        self.tag, self.first_id, self.place = tag, first_collective_id, place

    def start(self, local, tie):
        self.names = tuple(local)
        mine, tie = _after([local[n] for n in self.names], tie)
        self.mine = mine
        self.staged = _pair_exchange("pair_exchange_" + self.tag, self.first_id, mine)
        return tie

    def send(self, tie):
        staged, tie = _after(self.staged, tie)
        sums = [_pair_sum("pair_sum_" + n, self.place, g, st) for n, g, st in zip(self.names, self.mine, staged)]
        travel, tie = _after([s[0] for s in sums], tie)
        self.own = [s[1] for s in sums]
        self.got = _chip_exchange("chip_exchange_" + self.tag, self.first_id + 1, travel)
        return tie

    def finish(self, tie):
        got, tie = _after(self.got, tie)
        halves = [_final_sum("final_sum_" + n, self.place, o, r) for n, o, r in zip(self.names, self.own, got)]
        halves, tie = _after(halves, tie)
        summed = _pair_gather("pair_gather_" + self.tag, self.first_id + 2, halves)
        self.summed = dict(zip(self.names, summed))
        return tie


class _GradReducer:
    def __init__(self, place):
        self.early = _ReduceGroup("early", 2, place)
        self.late = _ReduceGroup("late", 5, place)

    @staticmethod
    def _slot_rows(a):
        return a.reshape(N_SHARD, a.shape[0] // N_SHARD, a.shape[1])

    def early_start(self, g, tie):
        return self.early.start({"w_mlp_down": self._slot_rows(g["w_mlp_down"]), "w_mlp_up": g["w_mlp_up"],
                                 "w_out": self._slot_rows(g["w_out"]), "w_mem_kv": self._slot_rows(g["w_mem_kv"]),
                                 "w_o_swa": g["w_o_swa"], "w_o_fox": g["w_o_fox"], "w_o_mem": g["w_o_mem"]}, tie)

    def early_send(self, tie):
        return self.early.send(tie)

    def early_finish(self, tie):
        return self.early.finish(tie)

    def late_start(self, g, tie):
        d_in = _w_in_from_segments(g["wc_lo"], g["wc_gl"])
        return self.late.start({"w_in": d_in}, tie)

    def late_send(self, tie):
        return self.late.send(tie)

    def late_finish(self, tie):
        return self.late.finish(tie)

    @property
    def summed(self):
        return {**self.early.summed, **self.late.summed}


def kernel(x, mem, g_mix, w_in, b_gate, b_forget, qn_swa, kn_swa, sink_swa, rel_bias, qn_fox, kn_fox, g_mem, w_mem_kv, qn_mem, kn_mem, w_o_swa, w_o_fox, w_o_mem, w_out, g_mlp, w_mlp_up, w_mlp_down, loss_target, m_g_mix, m_w_in, m_b_gate, m_b_forget, m_qn_swa, m_kn_swa, m_sink_swa, m_rel_bias, m_qn_fox, m_kn_fox, m_g_mem, m_w_mem_kv, m_qn_mem, m_kn_mem, m_w_o_swa, m_w_o_fox, m_w_o_mem, m_w_out, m_g_mlp, m_w_mlp_up, m_w_mlp_down, v_g_mix, v_w_in, v_b_gate, v_b_forget, v_qn_swa, v_kn_swa, v_sink_swa, v_rel_bias, v_qn_fox, v_kn_fox, v_g_mem, v_w_mem_kv, v_qn_mem, v_kn_mem, v_w_o_swa, v_w_o_fox, v_w_o_mem, v_w_out, v_g_mlp, v_w_mlp_up, v_w_mlp_down):
    given = dict(locals())
    W = {n: given[n] for n in WEIGHT_NAMES}
    M = {n: given["m_" + n] for n in WEIGHT_NAMES}
    V = {n: given["v_" + n] for n in WEIGHT_NAMES}
    pad_in = ((0, 0), (0, IN_SHARD_PAD - IN_SHARD))

    shards = [jnp.pad(w_in[0].astype(BF16), pad_in)] + [W[n][0].astype(BF16) for n in BIG_NAMES[1:]]
    slots = [jnp.broadcast_to(s[None], (N_SHARD,) + s.shape) for s in shards]
    g_in = _all_gather_neighbours("all_gather_w_in", 1, slots[0])
    small = {n: (W[n] if n == "rel_bias" else W[n].reshape(1, -1)) for n in SMALL_NAMES}
    h = _rmsnorm("rms_mix", x[0], small["g_mix"], min(1024, x.shape[1]))
    g_in, late, h, (m_in, v_in) = lax.optimization_barrier((g_in, slots[1:], h, (M["w_in"][0], V["w_in"][0])))
    M["w_in"], V["w_in"] = m_in[None], v_in[None]
    g_kv, g_oa, g_of, g_om, g_out, g_up, g_down = _all_gather_shards_async("all_gather_weights_async", 8, late)

    place = jnp.stack([2 * lax.axis_index("x") + lax.axis_index("y"), lax.axis_index("c")]).astype(jnp.int32)
    reducer = _GradReducer(place)
    loss, grad_x, grads = _local_step(
        x[0], h, mem[0], loss_target[0], small, g_in, g_kv.reshape(D_MODEL, D_MODEL), (g_oa, g_of, g_om),
        g_out.reshape(D_MODEL, D_MODEL), g_up, g_down.reshape(D_FF, D_MODEL), reducer)

    out = {}

    def adamw_of(names, summed):
        for n in names:
            res = _adamw("adamw_" + n, W[n][0], summed[n], M[n][0], V[n][0])
            out[n] = [r.reshape(W[n].shape) for r in res]

    adamw_of(reducer.early.names, reducer.early.summed)
    shapes = {n: W[n].shape for n in SMALL_NAMES}
    packed = _small_allreduce_adamw(_pack_small(grads, loss), _pack_small(W), _pack_small(M), _pack_small(V))
    done_meanwhile = ([out[n] for n in reducer.early.names], packed)
    (early_out, packed), grad_x = reducer.late_finish((done_meanwhile, grad_x))
    for n, res in zip(reducer.early.names, early_out):
        out[n] = res
    adamw_of(reducer.late.names, reducer.late.summed)
    unpacked = [_unpack_small(p, shapes) for p in packed]
    for n in SMALL_NAMES:
        out[n] = [u[0][n] for u in unpacked]
    loss_total = unpacked[0][1]

    return (loss_total, grad_x.reshape(x.shape),
            *[out[n][0] for n in WEIGHT_NAMES], *[out[n][1] for n in WEIGHT_NAMES],
            *[out[n][2] for n in WEIGHT_NAMES], *[out[n][3] for n in WEIGHT_NAMES])
```
